```python
import jax, jax.numpy as jnp
from jax import lax
import numpy as np

D_MODEL = 1024
BATCH = 8
SEQ = 2048
DEPTH = 2

N_A_LAYERS = DEPTH // 2
N_B_LAYERS = DEPTH - N_A_LAYERS
POOL_WINDOWS = (2, 4, 8, 16)
N_POOL_GROUPS = len(POOL_WINDOWS)
POOL_GROUP = D_MODEL // N_POOL_GROUPS
HEAD_DIM = 64
N_HEADS = D_MODEL // HEAD_DIM
N_KV_HEADS = 4
GQA_GROUP = N_HEADS // N_KV_HEADS
WINDOW = 128
BLOCK = 128
D_FF = -(-8 * D_MODEL // (3 * 256)) * 256
PLE_DIM = 256
EPS = 1e-6
NEG_INF = -1e30

kernel_name = "yoco_pool_swa_sink_hybrid"


def rmsnorm(x, g):
    xf = x.astype(jnp.float32)
    y = xf * lax.rsqrt(jnp.mean(xf * xf, axis=-1, keepdims=True) + EPS)
    return (y * g.astype(jnp.float32)).astype(x.dtype)


def causal_multiscale_pool(h):
    B, S, C = h.shape
    hf = h.astype(jnp.float32)
    cp = jnp.concatenate([jnp.zeros((B, 1, C), jnp.float32), jnp.cumsum(hf, axis=1)], axis=1)
    pos1 = jnp.arange(1, S + 1, dtype=jnp.int32)
    outs = []
    for gi, w in enumerate(POOL_WINDOWS):
        sl = cp[:, :, gi * POOL_GROUP:(gi + 1) * POOL_GROUP]
        hi = sl[:, 1:]
        lo = jnp.concatenate([jnp.zeros((B, w - 1, POOL_GROUP), jnp.float32),
                              sl[:, :S - w + 1]], axis=1)
        cnt = jnp.minimum(pos1, w).astype(jnp.float32)[None, :, None]
        outs.append((hi - lo) / cnt)
    pooled = jnp.concatenate(outs, axis=-1)
    return (pooled - hf).astype(h.dtype)


def pool_mixer(h, w_pool, scale):
    B, S, _ = h.shape
    d = causal_multiscale_pool(h).reshape(B, S, N_POOL_GROUPS, POOL_GROUP)
    y = jnp.einsum('bsgc,gcd->bsgd', d, w_pool).reshape(B, S, D_MODEL)
    return y * scale


def alibi_slopes():
    h = jnp.arange(1, N_HEADS + 1, dtype=jnp.float32)
    return jnp.exp2(-8.0 * h / N_HEADS).reshape(N_KV_HEADS, GQA_GROUP)


def swa_sink_attention(q, k, v, sinks):
    B, S = q.shape[:2]
    nb = S // BLOCK
    qb = q.reshape(B, nb, BLOCK, N_KV_HEADS, GQA_GROUP, HEAD_DIM)
    kb = k.reshape(B, nb, BLOCK, N_KV_HEADS, HEAD_DIM)
    vb = v.reshape(B, nb, BLOCK, N_KV_HEADS, HEAD_DIM)
    pad = ((0, 0), (1, 0), (0, 0), (0, 0), (0, 0))
    kwin = jnp.concatenate([jnp.pad(kb, pad)[:, :-1], kb], axis=2)
    vwin = jnp.concatenate([jnp.pad(vb, pad)[:, :-1], vb], axis=2)
    scores = jnp.einsum('bnqkgd,bnskd->bnkgqs', qb, kwin,
                        preferred_element_type=jnp.float32) * (HEAD_DIM ** -0.5)
    qi = jnp.arange(BLOCK)[:, None]
    si = jnp.arange(2 * BLOCK)[None, :]
    rel = BLOCK + qi - si
    kpos = (jnp.arange(nb)[:, None, None] - 1) * BLOCK + si[None]
    valid = (rel >= 0)[None] & (rel < WINDOW)[None] & (kpos >= 0)
    bias = -alibi_slopes()[:, :, None, None] * rel.astype(jnp.float32)
    scores = jnp.where(valid[None, :, None, None], scores + bias[None, None], NEG_INF)
    sink = jnp.broadcast_to(sinks.astype(jnp.float32).reshape(N_KV_HEADS, GQA_GROUP)[None, None, :, :, None, None],
                            scores.shape[:-1] + (1,))
    probs = jax.nn.softmax(jnp.concatenate([scores, sink], axis=-1), axis=-1)[..., :-1]
    out = jnp.einsum('bnkgqs,bnskd->bnqkgd', probs.astype(v.dtype), vwin)
    return out.reshape(B, S, N_HEADS * HEAD_DIM)


def swiglu(h, w_gu, w_down):
    gu = h @ w_gu
    g, u = gu[..., :D_FF], gu[..., D_FF:]
    return (jax.nn.silu(g) * u) @ w_down


def _fwd_setup_inputs(seed: int = 0) -> dict:
    key = jax.random.key(seed)
    ks = jax.random.split(key, 24)
    f32 = jnp.float32
    nrm = lambda k, s, fan: jax.random.normal(k, s, f32) * (fan ** -0.5)
    gain = lambda k, s: 1.0 + 0.1 * jax.random.normal(k, s, f32)
    KV = N_KV_HEADS * HEAD_DIM
    QD = N_HEADS * HEAD_DIM
    return {
        "x": jax.random.normal(ks[0], (BATCH, SEQ, D_MODEL), f32),
        "p": jax.random.normal(ks[1], (DEPTH, BATCH, SEQ, PLE_DIM), f32),
        "pre_mix_g": gain(ks[2], (DEPTH, D_MODEL)),
        "post_mix_g": gain(ks[3], (DEPTH, D_MODEL)),
        "pre_ffn_g": gain(ks[4], (DEPTH, D_MODEL)),
        "post_ffn_g": gain(ks[5], (DEPTH, D_MODEL)),
        "pool_w": nrm(ks[6], (N_A_LAYERS, N_POOL_GROUPS, POOL_GROUP, POOL_GROUP), POOL_GROUP),
        "pool_scale": gain(ks[7], (N_A_LAYERS, D_MODEL)),
        "kv_g": gain(ks[8], (D_MODEL,)),
        "w_kv": nrm(ks[9], (D_MODEL, 2 * KV), D_MODEL),
        "w_q": nrm(ks[10], (N_B_LAYERS, D_MODEL, QD), D_MODEL),
        "sinks": 0.5 * jax.random.normal(ks[11], (N_B_LAYERS, N_HEADS), f32),
        "w_o": nrm(ks[12], (N_B_LAYERS, QD, D_MODEL), QD),
        "w_gu": nrm(ks[13], (DEPTH, D_MODEL, 2 * D_FF), D_MODEL),
        "w_down": nrm(ks[14], (DEPTH, D_FF, D_MODEL), D_FF),
        "ple_g": gain(ks[15], (DEPTH, D_MODEL)),
        "w_ple_gate": nrm(ks[16], (DEPTH, D_MODEL, D_MODEL), D_MODEL),
        "w_ple_proj": nrm(ks[17], (DEPTH, PLE_DIM, D_MODEL), PLE_DIM),
        "ple_post_g": gain(ks[18], (DEPTH, D_MODEL)),
    }


def _fwd_reference(x, p, pre_mix_g, post_mix_g, pre_ffn_g, post_ffn_g, pool_w, pool_scale,
              kv_g, w_kv, w_q, sinks, w_o, w_gu, w_down, ple_g, w_ple_gate, w_ple_proj,
              ple_post_g):
    B, S, _ = x.shape
    KV = N_KV_HEADS * HEAD_DIM
    k = v = None
    for i in range(DEPTH):
        h = rmsnorm(x, pre_mix_g[i])
        if i < N_A_LAYERS:
            y = pool_mixer(h, pool_w[i], pool_scale[i])
        else:
            b = i - N_A_LAYERS
            q = (h @ w_q[b]).reshape(B, S, N_HEADS, HEAD_DIM)
            y = swa_sink_attention(q, k, v, sinks[b]) @ w_o[b]
        x = x + rmsnorm(y, post_mix_g[i])
        h = rmsnorm(x, pre_ffn_g[i])
        x = x + rmsnorm(swiglu(h, w_gu[i], w_down[i]), post_ffn_g[i])
        gate = jax.nn.sigmoid(rmsnorm(x, ple_g[i]) @ w_ple_gate[i])
        e = (p[i].astype(x.dtype) @ w_ple_proj[i]) * gate
        x = x + rmsnorm(e, ple_post_g[i])
        if i == N_A_LAYERS - 1:
            kv = rmsnorm(x, kv_g) @ w_kv
            k = kv[..., :KV].reshape(B, S, N_KV_HEADS, HEAD_DIM)
            v = kv[..., KV:].reshape(B, S, N_KV_HEADS, HEAD_DIM)
    return x


import jax as _jax
import jax.numpy as _jnp

TWIN_FORMAT = 'train_step'
FWD_PARAMS = ['x', 'p', 'pre_mix_g', 'post_mix_g', 'pre_ffn_g', 'post_ffn_g', 'pool_w', 'pool_scale', 'kv_g', 'w_kv', 'w_q', 'sinks', 'w_o', 'w_gu', 'w_down', 'ple_g', 'w_ple_gate', 'w_ple_proj', 'ple_post_g']
TWIN_WEIGHTS = ['pre_mix_g', 'post_mix_g', 'pre_ffn_g', 'post_ffn_g', 'pool_w', 'pool_scale', 'kv_g', 'w_kv', 'w_q', 'sinks', 'w_o', 'w_gu', 'w_down', 'ple_g', 'w_ple_gate', 'w_ple_proj', 'ple_post_g']
TWIN_DIFF_INPUT = 'x'
TWIN_INPUTS = ['x', 'p', 'pre_mix_g', 'post_mix_g', 'pre_ffn_g', 'post_ffn_g', 'pool_w', 'pool_scale', 'kv_g', 'w_kv', 'w_q', 'sinks', 'w_o', 'w_gu', 'w_down', 'ple_g', 'w_ple_gate', 'w_ple_proj', 'ple_post_g', 'loss_target', 'm_pre_mix_g', 'm_post_mix_g', 'm_pre_ffn_g', 'm_post_ffn_g', 'm_pool_w', 'm_pool_scale', 'm_kv_g', 'm_w_kv', 'm_w_q', 'm_sinks', 'm_w_o', 'm_w_gu', 'm_w_down', 'm_ple_g', 'm_w_ple_gate', 'm_w_ple_proj', 'm_ple_post_g', 'v_pre_mix_g', 'v_post_mix_g', 'v_pre_ffn_g', 'v_post_ffn_g', 'v_pool_w', 'v_pool_scale', 'v_kv_g', 'v_w_kv', 'v_w_q', 'v_sinks', 'v_w_o', 'v_w_gu', 'v_w_down', 'v_ple_g', 'v_w_ple_gate', 'v_w_ple_proj', 'v_ple_post_g']
TWIN_OUTPUTS = ['loss', 'grad_x', 'grad_pre_mix_g', 'grad_post_mix_g', 'grad_pre_ffn_g', 'grad_post_ffn_g', 'grad_pool_w', 'grad_pool_scale', 'grad_kv_g', 'grad_w_kv', 'grad_w_q', 'grad_sinks', 'grad_w_o', 'grad_w_gu', 'grad_w_down', 'grad_ple_g', 'grad_w_ple_gate', 'grad_w_ple_proj', 'grad_ple_post_g', 'delta_pre_mix_g', 'delta_post_mix_g', 'delta_pre_ffn_g', 'delta_post_ffn_g', 'delta_pool_w', 'delta_pool_scale', 'delta_kv_g', 'delta_w_kv', 'delta_w_q', 'delta_sinks', 'delta_w_o', 'delta_w_gu', 'delta_w_down', 'delta_ple_g', 'delta_w_ple_gate', 'delta_w_ple_proj', 'delta_ple_post_g', 'new_m_pre_mix_g', 'new_m_post_mix_g', 'new_m_pre_ffn_g', 'new_m_post_ffn_g', 'new_m_pool_w', 'new_m_pool_scale', 'new_m_kv_g', 'new_m_w_kv', 'new_m_w_q', 'new_m_sinks', 'new_m_w_o', 'new_m_w_gu', 'new_m_w_down', 'new_m_ple_g', 'new_m_w_ple_gate', 'new_m_w_ple_proj', 'new_m_ple_post_g', 'new_v_pre_mix_g', 'new_v_post_mix_g', 'new_v_pre_ffn_g', 'new_v_post_ffn_g', 'new_v_pool_w', 'new_v_pool_scale', 'new_v_kv_g', 'new_v_w_kv', 'new_v_w_q', 'new_v_sinks', 'new_v_w_o', 'new_v_w_gu', 'new_v_w_down', 'new_v_ple_g', 'new_v_w_ple_gate', 'new_v_w_ple_proj', 'new_v_ple_post_g']
TWIN_LEAF_KINDS = {'loss': 'loss', 'grad_x': 'grad_x', 'grad_pre_mix_g': 'grad_w', 'grad_post_mix_g': 'grad_w', 'grad_pre_ffn_g': 'grad_w', 'grad_post_ffn_g': 'grad_w', 'grad_pool_w': 'grad_w', 'grad_pool_scale': 'grad_w', 'grad_kv_g': 'grad_w', 'grad_w_kv': 'grad_w', 'grad_w_q': 'grad_w', 'grad_sinks': 'grad_w', 'grad_w_o': 'grad_w', 'grad_w_gu': 'grad_w', 'grad_w_down': 'grad_w', 'grad_ple_g': 'grad_w', 'grad_w_ple_gate': 'grad_w', 'grad_w_ple_proj': 'grad_w', 'grad_ple_post_g': 'grad_w', 'delta_pre_mix_g': 'delta_w', 'delta_post_mix_g': 'delta_w', 'delta_pre_ffn_g': 'delta_w', 'delta_post_ffn_g': 'delta_w', 'delta_pool_w': 'delta_w', 'delta_pool_scale': 'delta_w', 'delta_kv_g': 'delta_w', 'delta_w_kv': 'delta_w', 'delta_w_q': 'delta_w', 'delta_sinks': 'delta_w', 'delta_w_o': 'delta_w', 'delta_w_gu': 'delta_w', 'delta_w_down': 'delta_w', 'delta_ple_g': 'delta_w', 'delta_w_ple_gate': 'delta_w', 'delta_w_ple_proj': 'delta_w', 'delta_ple_post_g': 'delta_w', 'new_m_pre_mix_g': 'new_m', 'new_m_post_mix_g': 'new_m', 'new_m_pre_ffn_g': 'new_m', 'new_m_post_ffn_g': 'new_m', 'new_m_pool_w': 'new_m', 'new_m_pool_scale': 'new_m', 'new_m_kv_g': 'new_m', 'new_m_w_kv': 'new_m', 'new_m_w_q': 'new_m', 'new_m_sinks': 'new_m', 'new_m_w_o': 'new_m', 'new_m_w_gu': 'new_m', 'new_m_w_down': 'new_m', 'new_m_ple_g': 'new_m', 'new_m_w_ple_gate': 'new_m', 'new_m_w_ple_proj': 'new_m', 'new_m_ple_post_g': 'new_m', 'new_v_pre_mix_g': 'new_v', 'new_v_post_mix_g': 'new_v', 'new_v_pre_ffn_g': 'new_v', 'new_v_post_ffn_g': 'new_v', 'new_v_pool_w': 'new_v', 'new_v_pool_scale': 'new_v', 'new_v_kv_g': 'new_v', 'new_v_w_kv': 'new_v', 'new_v_w_q': 'new_v', 'new_v_sinks': 'new_v', 'new_v_w_o': 'new_v', 'new_v_w_gu': 'new_v', 'new_v_w_down': 'new_v', 'new_v_ple_g': 'new_v', 'new_v_w_ple_gate': 'new_v', 'new_v_w_ple_proj': 'new_v', 'new_v_ple_post_g': 'new_v'}


def _forward(args):
    return _fwd_reference(*[args[k] for k in FWD_PARAMS])


def _output_shape():
    out = _jax.eval_shape(lambda: _forward(_fwd_setup_inputs(0)))
    return out.shape, out.dtype

N_MICROBATCH = 1
ADAM_LR = 0.001
ADAM_B1 = 0.9
ADAM_B2 = 0.999
ADAM_EPS = 1e-08
ADAM_WD = 0.01
ADAM_STEP = 10
PER_EXAMPLE_BATCH_AXIS = {'x': 0, 'p': 1, 'loss_target': 0}
SHARED_INPUTS = []
_WEIGHT_DTYPES = {'pre_mix_g': _jnp.float32, 'post_mix_g': _jnp.float32, 'pre_ffn_g': _jnp.float32, 'post_ffn_g': _jnp.float32, 'pool_w': _jnp.float32, 'pool_scale': _jnp.float32, 'kv_g': _jnp.float32, 'w_kv': _jnp.float32, 'w_q': _jnp.float32, 'sinks': _jnp.float32, 'w_o': _jnp.float32, 'w_gu': _jnp.float32, 'w_down': _jnp.float32, 'ple_g': _jnp.float32, 'w_ple_gate': _jnp.float32, 'w_ple_proj': _jnp.float32, 'ple_post_g': _jnp.float32}
MOMENT_SCALE = {'pre_mix_g': 7.815090e-01, 'post_mix_g': 1.648683e+01, 'pre_ffn_g': 7.069640e-01, 'post_ffn_g': 1.602142e+01, 'pool_w': 1.015797e+00, 'pool_scale': 3.176886e+00, 'kv_g': 7.046289e-01, 'w_kv': 9.470450e-01, 'w_q': 3.384750e-01, 'sinks': 8.584790e-01, 'w_o': 6.120134e-01, 'w_gu': 2.970103e-01, 'w_down': 5.344881e-01, 'ple_g': 1.701682e-01, 'w_ple_gate': 1.707466e-01, 'w_ple_proj': 4.834549e-01, 'ple_post_g': 1.610280e+01}


def _to_microbatches(a, axis):
    t = _jnp.moveaxis(a, axis, 0)
    t = t.reshape((N_MICROBATCH, t.shape[0] // N_MICROBATCH) + t.shape[1:])
    return _jnp.moveaxis(t, 1, axis + 1)


def setup_inputs(seed: int = 0) -> dict:
    inp = _fwd_setup_inputs(seed)
    key = _jax.random.fold_in(_jax.random.key(seed), 7919)
    shape, _ = _output_shape()
    out = dict(inp)
    out["loss_target"] = _jax.random.normal(_jax.random.fold_in(key, 0), shape, _jnp.float32)
    for i, name in enumerate(TWIN_WEIGHTS):
        w = inp[name].astype(_jnp.float32)
        if MOMENT_SCALE is None:
            s = _jnp.sqrt(_jnp.mean(_jnp.square(w)) + 1e-30)
        else:
            s = MOMENT_SCALE[name]
        km, kv = _jax.random.split(_jax.random.fold_in(key, i + 1))
        out[name] = w
        out["m_" + name] = s * _jax.random.normal(km, w.shape, _jnp.float32)
        out["v_" + name] = (s * s) * _jax.random.uniform(kv, w.shape, _jnp.float32, 0.5, 1.5)
    if N_MICROBATCH > 1:
        for name, axis in PER_EXAMPLE_BATCH_AXIS.items():
            out[name] = _to_microbatches(out[name], axis)
    return {'x': out['x'], 'p': out['p'], 'pre_mix_g': out['pre_mix_g'], 'post_mix_g': out['post_mix_g'], 'pre_ffn_g': out['pre_ffn_g'], 'post_ffn_g': out['post_ffn_g'], 'pool_w': out['pool_w'], 'pool_scale': out['pool_scale'], 'kv_g': out['kv_g'], 'w_kv': out['w_kv'], 'w_q': out['w_q'], 'sinks': out['sinks'], 'w_o': out['w_o'], 'w_gu': out['w_gu'], 'w_down': out['w_down'], 'ple_g': out['ple_g'], 'w_ple_gate': out['w_ple_gate'], 'w_ple_proj': out['w_ple_proj'], 'ple_post_g': out['ple_post_g'], 'loss_target': out['loss_target'], 'm_pre_mix_g': out['m_pre_mix_g'], 'm_post_mix_g': out['m_post_mix_g'], 'm_pre_ffn_g': out['m_pre_ffn_g'], 'm_post_ffn_g': out['m_post_ffn_g'], 'm_pool_w': out['m_pool_w'], 'm_pool_scale': out['m_pool_scale'], 'm_kv_g': out['m_kv_g'], 'm_w_kv': out['m_w_kv'], 'm_w_q': out['m_w_q'], 'm_sinks': out['m_sinks'], 'm_w_o': out['m_w_o'], 'm_w_gu': out['m_w_gu'], 'm_w_down': out['m_w_down'], 'm_ple_g': out['m_ple_g'], 'm_w_ple_gate': out['m_w_ple_gate'], 'm_w_ple_proj': out['m_w_ple_proj'], 'm_ple_post_g': out['m_ple_post_g'], 'v_pre_mix_g': out['v_pre_mix_g'], 'v_post_mix_g': out['v_post_mix_g'], 'v_pre_ffn_g': out['v_pre_ffn_g'], 'v_post_ffn_g': out['v_post_ffn_g'], 'v_pool_w': out['v_pool_w'], 'v_pool_scale': out['v_pool_scale'], 'v_kv_g': out['v_kv_g'], 'v_w_kv': out['v_w_kv'], 'v_w_q': out['v_w_q'], 'v_sinks': out['v_sinks'], 'v_w_o': out['v_w_o'], 'v_w_gu': out['v_w_gu'], 'v_w_down': out['v_w_down'], 'v_ple_g': out['v_ple_g'], 'v_w_ple_gate': out['v_w_ple_gate'], 'v_w_ple_proj': out['v_w_ple_proj'], 'v_ple_post_g': out['v_ple_post_g']}


def _loss(weights, diff, rest, loss_target):
    with _jax.named_scope("forward"):
        args = {**rest, TWIN_DIFF_INPUT: diff, **{k: w.astype(_WEIGHT_DTYPES[k]) for k, w in weights.items()}}
        y = _forward(args)
    with _jax.named_scope("loss_head"):
        err = _jnp.square(y.astype(_jnp.float32) - loss_target)
        return 0.5 * _jnp.sum(_jnp.mean(err, axis=-1)) if err.ndim else 0.5 * err


def _adamw(w, g, m, v):
    m = ADAM_B1 * m + (1.0 - ADAM_B1) * g
    v = ADAM_B2 * v + (1.0 - ADAM_B2) * _jnp.square(g)
    m_hat = m / (1.0 - ADAM_B1 ** ADAM_STEP)
    v_hat = v / (1.0 - ADAM_B2 ** ADAM_STEP)
    delta = -ADAM_LR * (m_hat / (_jnp.sqrt(v_hat) + ADAM_EPS) + ADAM_WD * w)
    return delta, m, v


def reference(x, p, pre_mix_g, post_mix_g, pre_ffn_g, post_ffn_g, pool_w, pool_scale, kv_g, w_kv, w_q, sinks, w_o, w_gu, w_down, ple_g, w_ple_gate, w_ple_proj, ple_post_g, loss_target, m_pre_mix_g, m_post_mix_g, m_pre_ffn_g, m_post_ffn_g, m_pool_w, m_pool_scale, m_kv_g, m_w_kv, m_w_q, m_sinks, m_w_o, m_w_gu, m_w_down, m_ple_g, m_w_ple_gate, m_w_ple_proj, m_ple_post_g, v_pre_mix_g, v_post_mix_g, v_pre_ffn_g, v_post_ffn_g, v_pool_w, v_pool_scale, v_kv_g, v_w_kv, v_w_q, v_sinks, v_w_o, v_w_gu, v_w_down, v_ple_g, v_w_ple_gate, v_w_ple_proj, v_ple_post_g):
    given = dict(x=x, p=p, pre_mix_g=pre_mix_g, post_mix_g=post_mix_g, pre_ffn_g=pre_ffn_g, post_ffn_g=post_ffn_g, pool_w=pool_w, pool_scale=pool_scale, kv_g=kv_g, w_kv=w_kv, w_q=w_q, sinks=sinks, w_o=w_o, w_gu=w_gu, w_down=w_down, ple_g=ple_g, w_ple_gate=w_ple_gate, w_ple_proj=w_ple_proj, ple_post_g=ple_post_g, loss_target=loss_target, m_pre_mix_g=m_pre_mix_g, m_post_mix_g=m_post_mix_g, m_pre_ffn_g=m_pre_ffn_g, m_post_ffn_g=m_post_ffn_g, m_pool_w=m_pool_w, m_pool_scale=m_pool_scale, m_kv_g=m_kv_g, m_w_kv=m_w_kv, m_w_q=m_w_q, m_sinks=m_sinks, m_w_o=m_w_o, m_w_gu=m_w_gu, m_w_down=m_w_down, m_ple_g=m_ple_g, m_w_ple_gate=m_w_ple_gate, m_w_ple_proj=m_w_ple_proj, m_ple_post_g=m_ple_post_g, v_pre_mix_g=v_pre_mix_g, v_post_mix_g=v_post_mix_g, v_pre_ffn_g=v_pre_ffn_g, v_post_ffn_g=v_post_ffn_g, v_pool_w=v_pool_w, v_pool_scale=v_pool_scale, v_kv_g=v_kv_g, v_w_kv=v_w_kv, v_w_q=v_w_q, v_sinks=v_sinks, v_w_o=v_w_o, v_w_gu=v_w_gu, v_w_down=v_w_down, v_ple_g=v_ple_g, v_w_ple_gate=v_w_ple_gate, v_w_ple_proj=v_w_ple_proj, v_ple_post_g=v_ple_post_g)
    weights = {n: given[n] for n in TWIN_WEIGHTS}
    shared = {n: given[n] for n in SHARED_INPUTS}
    per_example = {n: given[n] for n in ['x', 'p']}
    grad_fn = _jax.value_and_grad(_loss, argnums=(0, 1))

    def one_microbatch(ex, loss_target):
        ex = dict(ex)
        diff = ex.pop(TWIN_DIFF_INPUT)
        return grad_fn(weights, diff, {**shared, **ex}, loss_target)

    if N_MICROBATCH == 1:
        loss, (grad_w, grad_x) = one_microbatch(per_example, given["loss_target"])
    else:
        def body(carry, xs):
            loss_sum, grad_sum = carry
            l_k, (gw_k, gx_k) = one_microbatch(xs[0], xs[1])
            with _jax.named_scope("update"):
                return (loss_sum + l_k, _jax.tree.map(_jnp.add, grad_sum, gw_k)), gx_k

        init = (_jnp.zeros((), _jnp.float32), _jax.tree.map(_jnp.zeros_like, weights))
        (loss, grad_w), grad_x = _jax.lax.scan(body, init, (per_example, given["loss_target"]))
    with _jax.named_scope("update"):
        delta_w, new_m, new_v = {}, {}, {}
        for n in TWIN_WEIGHTS:
            delta_w[n], new_m[n], new_v[n] = _adamw(weights[n], grad_w[n], given["m_" + n], given["v_" + n])
    return (loss, grad_x, *[grad_w[n] for n in TWIN_WEIGHTS], *[delta_w[n] for n in TWIN_WEIGHTS],
            *[new_m[n] for n in TWIN_WEIGHTS], *[new_v[n] for n in TWIN_WEIGHTS])
```

```python
import functools

import jax
import jax.numpy as jnp
from jax import lax
from jax.experimental import pallas as pl
from jax.experimental.pallas import tpu as pltpu

F32 = jnp.float32
BF16 = jnp.bfloat16

N_DEV = 8
D_MODEL = 1024
N_POOL_GROUPS = 4
POOL_GROUP = 256
POOL_HALO = 16
HEAD_DIM = 64
N_HEADS = 16
N_KV_HEADS = 4
GQA_GROUP = 4
KV_DIM = N_KV_HEADS * HEAD_DIM
ATT_BLOCK = 128
D_FF = 2816
FF_CHUNKS = 4
FF_BLOCK = D_FF // FF_CHUNKS
WD_ROWS = D_FF // N_DEV
PLE_DIM = 256
EPS = 1e-6
NEG_INF = -1e30
ATT_SCALE = HEAD_DIM ** -0.5

ADAM_LR = 0.001
ADAM_B1 = 0.9
ADAM_B2 = 0.999
ADAM_EPS = 1e-08
ADAM_WD = 0.01
ADAM_STEP = 10

ROW_TILE = 256
VMEM_BIG = 56 * 1024 * 1024
VMEM_MID = 40 * 1024 * 1024

SV_ROWS = 16
SV_PRE_MIX, SV_POST_MIX, SV_PRE_FFN, SV_POST_FFN, SV_PLE, SV_PLE_POST = 0, 2, 4, 6, 8, 10
SV_KV, SV_POOL_SCALE, SV_SINKS, SV_LOSS = 12, 13, 14, 15

MESH = pl.DeviceIdType.MESH
ANY = pl.BlockSpec(memory_space=pl.ANY)


def _dot(a, b):
    return jnp.dot(a, b, preferred_element_type=F32)


def _dot_nt(a, b):
    return lax.dot_general(a, b, (((1,), (1,)), ((), ())), preferred_element_type=F32)


def _dot_tn(a, b):
    return lax.dot_general(a, b, (((0,), (0,)), ((), ())), preferred_element_type=F32)


def _rstd(x):
    return lax.rsqrt(jnp.mean(x * x, axis=-1, keepdims=True) + EPS)


def _rms(x, g):
    return x * _rstd(x) * g


def _rms_bwd(x, g, dy):
    r = _rstd(x)
    n = x * r
    dn = dy * g
    dx = r * (dn - n * jnp.mean(dn * n, axis=-1, keepdims=True))
    dg = jnp.sum(dy * n, axis=0, keepdims=True)
    return dx, dg


def _sigmoid(x):
    return 1.0 / (1.0 + jnp.exp(-x))


def _acc(ref, val, first):
    @pl.when(first)
    def _():
        ref[...] = val

    @pl.when(jnp.logical_not(first))
    def _():
        ref[...] += val


def _pool_counts(row0, rows):
    t = row0 + lax.broadcasted_iota(jnp.int32, (rows, D_MODEL), 0) + 1
    grp = lax.broadcasted_iota(jnp.int32, (rows, D_MODEL), 1) // POOL_GROUP
    win = jnp.left_shift(2, grp)
    return jnp.minimum(t, win).astype(F32)


def _window_sums(ext, shift_of):
    outs = []
    s = ext
    for gi in range(N_POOL_GROUPS):
        s = s + pltpu.roll(s, shift_of(1 << gi), axis=0)
        outs.append(s[:, :POOL_GROUP])
        s = s[:, POOL_GROUP:]
    return jnp.concatenate(outs, axis=1)


def _cparams(n_axes, vmem):
    return pltpu.CompilerParams(dimension_semantics=("arbitrary",) * n_axes, vmem_limit_bytes=vmem)


def _row_spec(cols, tm=ROW_TILE):
    return pl.BlockSpec((tm, cols), lambda i: (i, 0))


def _full_spec(shape):
    zeros = (0,) * len(shape)
    return pl.BlockSpec(shape, lambda *_: zeros)


def _vec_spec():
    return _full_spec((1, D_MODEL))


def _fwd_pool_mixer(x, g_pre, wp, scale, g_post, g_ffn):
    T = x.shape[0]
    tm = ROW_TILE
    nt = T // tm

    def body(x_ref, gpre_ref, wp_ref, sc_ref, gpost_ref, gffn_ref, x1_ref, h2_ref, yraw_ref, d_ref, carry):
        i = pl.program_id(0)

        @pl.when(i == 0)
        def _():
            carry[...] = jnp.zeros_like(carry)

        xv = x_ref[...]
        h = _rms(xv, gpre_ref[...])
        ext = jnp.concatenate([carry[...], h], axis=0)
        carry[...] = h[tm - POOL_HALO:, :]
        sums = _window_sums(ext, lambda k: k)[POOL_HALO:, :]
        d = sums / _pool_counts(i * tm, tm) - h
        db = d.astype(BF16)
        d_ref[...] = db
        yraw = jnp.concatenate(
            [_dot(db[:, g * POOL_GROUP:(g + 1) * POOL_GROUP], wp_ref[g]) for g in range(N_POOL_GROUPS)], axis=1)
        yraw_ref[...] = yraw
        x1 = xv + _rms(yraw * sc_ref[...], gpost_ref[...])
        x1_ref[...] = x1
        h2_ref[...] = _rms(x1, gffn_ref[...]).astype(BF16)

    return pl.pallas_call(
        body, name="fwd_pool_mixer", grid=(nt,),
        in_specs=[_row_spec(D_MODEL), _vec_spec(), _full_spec((N_POOL_GROUPS, POOL_GROUP, POOL_GROUP)), _vec_spec(),
                  _vec_spec(), _vec_spec()],
        out_specs=[_row_spec(D_MODEL)] * 4,
        out_shape=[jax.ShapeDtypeStruct((T, D_MODEL), F32), jax.ShapeDtypeStruct((T, D_MODEL), BF16),
                   jax.ShapeDtypeStruct((T, D_MODEL), F32), jax.ShapeDtypeStruct((T, D_MODEL), BF16)],
        scratch_shapes=[pltpu.VMEM((POOL_HALO, D_MODEL), F32)],
        compiler_params=_cparams(1, VMEM_MID),
    )(x, g_pre, wp, scale, g_post, g_ffn)


def _fwd_ffn(layer, h2, x1, wgu, wd, g_post, g_ple):
    T = h2.shape[0]
    tm = ROW_TILE
    nt = T // tm
    last = FF_CHUNKS - 1

    def body(h2_ref, x1_ref, wgu_ref, wd_ref, gpost_ref, gple_ref, gs_ref, us_ref, f_ref, x2_ref, h3_ref, acc):
        k = pl.program_id(0)
        i = pl.program_id(1)
        rows = pl.ds(pl.multiple_of(i * tm, tm), tm)
        h = h2_ref[...]
        g = _dot(h, wgu_ref[0])
        u = _dot(h, wgu_ref[1])
        gs_ref[...] = g.astype(BF16)
        us_ref[...] = u.astype(BF16)
        a = (g * _sigmoid(g) * u).astype(BF16)
        part = _dot(a, wd_ref[...])

        @pl.when(k == 0)
        def _():
            acc[rows, :] = part

        @pl.when(jnp.logical_and(k > 0, k < last))
        def _():
            acc[rows, :] += part

        @pl.when(k == last)
        def _():
            f = acc[rows, :] + part
            f_ref[...] = f
            x2 = x1_ref[...] + _rms(f, gpost_ref[...])
            x2_ref[...] = x2
            h3_ref[...] = _rms(x2, gple_ref[...]).astype(BF16)

    def late(k, i):
        return (jnp.where(k == last, i, 0), 0)

    return pl.pallas_call(
        body, name=f"fwd_ffn{layer}", grid=(FF_CHUNKS, nt),
        in_specs=[pl.BlockSpec((tm, D_MODEL), lambda k, i: (i, 0)),
                  pl.BlockSpec((tm, D_MODEL), late),
                  pl.BlockSpec((None, None, 2, D_MODEL, FF_BLOCK), lambda k, i: (layer, k, 0, 0, 0)),
                  pl.BlockSpec((None, FF_BLOCK, D_MODEL), lambda k, i: (layer, k, 0)),
                  pl.BlockSpec((1, D_MODEL), lambda k, i: (0, 0)),
                  pl.BlockSpec((1, D_MODEL), lambda k, i: (0, 0))],
        out_specs=[pl.BlockSpec((None, tm, FF_BLOCK), lambda k, i: (k, i, 0)),
                   pl.BlockSpec((None, tm, FF_BLOCK), lambda k, i: (k, i, 0)),
                   pl.BlockSpec((tm, D_MODEL), late),
                   pl.BlockSpec((tm, D_MODEL), late),
                   pl.BlockSpec((tm, D_MODEL), late)],
        out_shape=[jax.ShapeDtypeStruct((FF_CHUNKS, T, FF_BLOCK), BF16),
                   jax.ShapeDtypeStruct((FF_CHUNKS, T, FF_BLOCK), BF16),
                   jax.ShapeDtypeStruct((T, D_MODEL), F32),
                   jax.ShapeDtypeStruct((T, D_MODEL), F32),
                   jax.ShapeDtypeStruct((T, D_MODEL), BF16)],
        scratch_shapes=[pltpu.VMEM((T, D_MODEL), F32)],
        compiler_params=_cparams(2, VMEM_MID),
    )(h2, x1, wgu, wd, g_post, g_ple)


def _fwd_ple(layer, x2, h3, p, wgate, wproj, g_post, target=None):
    T = x2.shape[0]
    tm = ROW_TILE
    nt = T // tm
    with_loss = target is not None

    def body(*refs):
        if with_loss:
            x2_ref, h3_ref, p_ref, wg_ref, wp_ref, gpost_ref, tgt_ref, out_ref, z_ref, pe_ref, loss_ref = refs
        else:
            x2_ref, h3_ref, p_ref, wg_ref, wp_ref, gpost_ref, out_ref, z_ref, pe_ref = refs
        z = _dot(h3_ref[...], wg_ref[...])
        pe = _dot(p_ref[...].astype(BF16), wp_ref[...])
        z_ref[...] = z
        pe_ref[...] = pe
        x3 = x2_ref[...] + _rms(pe * _sigmoid(z), gpost_ref[...])
        if with_loss:
            err = x3 - tgt_ref[...]
            out_ref[...] = err * (1.0 / D_MODEL)
            part = 0.5 * jnp.sum(jnp.mean(err * err, axis=-1, keepdims=True), axis=0, keepdims=True)
            _acc(loss_ref, part, pl.program_id(0) == 0)
        else:
            out_ref[...] = x3

    in_specs = [_row_spec(D_MODEL), _row_spec(D_MODEL), _row_spec(PLE_DIM),
                pl.BlockSpec((None, D_MODEL, D_MODEL), lambda i: (layer, 0, 0)),
                pl.BlockSpec((None, PLE_DIM, D_MODEL), lambda i: (layer, 0, 0)), _vec_spec()]
    out_specs = [_row_spec(D_MODEL)] * 3
    out_shape = [jax.ShapeDtypeStruct((T, D_MODEL), F32)] * 3
    args = [x2, h3, p, wgate, wproj, g_post]
    if with_loss:
        in_specs.append(_row_spec(D_MODEL))
        out_specs.append(_full_spec((1, 1)))
        out_shape.append(jax.ShapeDtypeStruct((1, 1), F32))
        args.append(target)
    return pl.pallas_call(
        body, name=f"fwd_ple{layer}", grid=(nt,), in_specs=in_specs, out_specs=out_specs, out_shape=out_shape,
        compiler_params=_cparams(1, VMEM_MID),
    )(*args)


def _fwd_qkv(x3, g_kv, g_mix, wkv, wq):
    T = x3.shape[0]
    nt = T // ROW_TILE

    def body(x_ref, gkv_ref, gmix_ref, wkv_ref, wq_ref, hk_ref, h1_ref, q_ref, kv_ref):
        xv = x_ref[...]
        r = _rstd(xv)
        hk = (xv * r * gkv_ref[...]).astype(BF16)
        h1 = (xv * r * gmix_ref[...]).astype(BF16)
        hk_ref[...] = hk
        h1_ref[...] = h1
        kv_ref[...] = _dot(hk, wkv_ref[...]).astype(BF16)
        q_ref[...] = _dot(h1, wq_ref[...]).astype(BF16)

    return pl.pallas_call(
        body, name="fwd_qkv", grid=(nt,),
        in_specs=[_row_spec(D_MODEL), _vec_spec(), _vec_spec(), _full_spec((D_MODEL, 2 * KV_DIM)),
                  _full_spec((D_MODEL, D_MODEL))],
        out_specs=[_row_spec(D_MODEL), _row_spec(D_MODEL), _row_spec(D_MODEL), _row_spec(2 * KV_DIM)],
        out_shape=[jax.ShapeDtypeStruct((T, D_MODEL), BF16)] * 3 + [jax.ShapeDtypeStruct((T, 2 * KV_DIM), BF16)],
        compiler_params=_cparams(1, VMEM_MID),
    )(x3, g_kv, g_mix, wkv, wq)


def _alibi_slope(h):
    return 2.0 ** (-8.0 * (h + 1) / N_HEADS)


def _att_mask(n):
    qi = lax.broadcasted_iota(jnp.int32, (ATT_BLOCK, 2 * ATT_BLOCK), 0)
    si = lax.broadcasted_iota(jnp.int32, (ATT_BLOCK, 2 * ATT_BLOCK), 1)
    rel = ATT_BLOCK + qi - si
    valid = (rel >= 0) & (rel < ATT_BLOCK) & ((si >= ATT_BLOCK) | (n > 0))
    return rel.astype(F32), valid


def _att_probs(qh, kk, relf, valid, slope, sink):
    s = _dot_nt(qh, kk) * ATT_SCALE
    s = jnp.where(valid, s - slope * relf, NEG_INF)
    m = jnp.maximum(jnp.max(s, axis=-1, keepdims=True), sink)
    e = jnp.exp(s - m)
    es = jnp.exp(sink - m)
    inv = 1.0 / (jnp.sum(e, axis=-1, keepdims=True) + es)
    return e * inv, es * inv


def _fwd_attention(q, kpad, vpad, sinks):
    T = q.shape[0]
    nb = T // ATT_BLOCK

    def body(q_ref, k_ref, v_ref, sink_ref, o_ref):
        n = pl.program_id(0)
        start = pl.multiple_of(n * ATT_BLOCK, ATT_BLOCK)
        kw = k_ref[pl.ds(start, 2 * ATT_BLOCK), :]
        vw = v_ref[pl.ds(start, 2 * ATT_BLOCK), :]
        relf, valid = _att_mask(n)
        outs = []
        for h in range(N_HEADS):
            kh = h // GQA_GROUP
            qh = q_ref[:, h * HEAD_DIM:(h + 1) * HEAD_DIM]
            kk = kw[:, kh * HEAD_DIM:(kh + 1) * HEAD_DIM]
            vv = vw[:, kh * HEAD_DIM:(kh + 1) * HEAD_DIM]
            pr, _ = _att_probs(qh, kk, relf, valid, _alibi_slope(h), sink_ref[0, h])
            outs.append(_dot(pr.astype(BF16), vv))
        o_ref[...] = jnp.concatenate(outs, axis=1).astype(BF16)

    return pl.pallas_call(
        body, name="fwd_attention", grid=(nb,),
        in_specs=[_row_spec(D_MODEL, ATT_BLOCK), _full_spec((T + ATT_BLOCK, KV_DIM)), _full_spec((T + ATT_BLOCK, KV_DIM)),
                  pl.BlockSpec(memory_space=pltpu.SMEM)],
        out_specs=_row_spec(D_MODEL, ATT_BLOCK),
        out_shape=jax.ShapeDtypeStruct((T, D_MODEL), BF16),
        compiler_params=_cparams(1, VMEM_MID),
    )(q, kpad, vpad, sinks)


def _fwd_attn_out(attn, x, wo, g_post, g_ffn):
    T = x.shape[0]
    nt = T // ROW_TILE

    def body(a_ref, x_ref, wo_ref, gpost_ref, gffn_ref, y_ref, x1_ref, h2_ref):
        y = _dot(a_ref[...], wo_ref[...])
        y_ref[...] = y
        x1 = x_ref[...] + _rms(y, gpost_ref[...])
        x1_ref[...] = x1
        h2_ref[...] = _rms(x1, gffn_ref[...]).astype(BF16)

    return pl.pallas_call(
        body, name="fwd_attn_out", grid=(nt,),
        in_specs=[_row_spec(D_MODEL), _row_spec(D_MODEL), _full_spec((D_MODEL, D_MODEL)), _vec_spec(), _vec_spec()],
        out_specs=[_row_spec(D_MODEL)] * 3,
        out_shape=[jax.ShapeDtypeStruct((T, D_MODEL), F32), jax.ShapeDtypeStruct((T, D_MODEL), F32),
                   jax.ShapeDtypeStruct((T, D_MODEL), BF16)],
        compiler_params=_cparams(1, VMEM_MID),
    )(attn, x, wo, g_post, g_ffn)


def _bwd_ple(layer, dx3, x2, z, pe, h3, p, f, wgate, g_ple_post, g_ple, g_post_ffn):
    T = x2.shape[0]
    tm = ROW_TILE
    nt = T // tm

    def body(dx3_ref, x2_ref, z_ref, pe_ref, h3_ref, p_ref, f_ref, wg_ref, gpp_ref, gp_ref, gpf_ref,
             dx2_ref, df_ref, dwg_ref, dwp_ref, dgpp_ref, dgp_ref, dgpf_ref, acc_g, acc_p):
        i = pl.program_id(0)
        first = i == 0
        dx3v = dx3_ref[...]
        gate = _sigmoid(z_ref[...])
        pev = pe_ref[...]
        de, dgpp = _rms_bwd(pev * gate, gpp_ref[...], dx3v)
        dpe = (de * gate).astype(BF16)
        dz = (de * pev * gate * (1.0 - gate)).astype(BF16)
        _acc(acc_p, _dot_tn(p_ref[...].astype(BF16), dpe), first)
        _acc(acc_g, _dot_tn(h3_ref[...], dz), first)
        dh3 = _dot_nt(dz, wg_ref[...])
        dxn, dgp = _rms_bwd(x2_ref[...], gp_ref[...], dh3)
        dx2 = dx3v + dxn
        dx2_ref[...] = dx2
        df, dgpf = _rms_bwd(f_ref[...], gpf_ref[...], dx2)
        df_ref[...] = df.astype(BF16)
        _acc(dgpp_ref, dgpp, first)
        _acc(dgp_ref, dgp, first)
        _acc(dgpf_ref, dgpf, first)

        @pl.when(i == nt - 1)
        def _():
            dwg_ref[...] = acc_g[...].astype(BF16)
            dwp_ref[...] = acc_p[...].astype(BF16)

    return pl.pallas_call(
        body, name=f"bwd_ple{layer}", grid=(nt,),
        in_specs=[_row_spec(D_MODEL)] * 5 + [_row_spec(PLE_DIM), _row_spec(D_MODEL),
                  pl.BlockSpec((None, D_MODEL, D_MODEL), lambda i: (layer, 0, 0)), _vec_spec(), _vec_spec(), _vec_spec()],
        out_specs=[_row_spec(D_MODEL), _row_spec(D_MODEL), _full_spec((D_MODEL, D_MODEL)), _full_spec((PLE_DIM, D_MODEL)),
                   _vec_spec(), _vec_spec(), _vec_spec()],
        out_shape=[jax.ShapeDtypeStruct((T, D_MODEL), F32), jax.ShapeDtypeStruct((T, D_MODEL), BF16),
                   jax.ShapeDtypeStruct((D_MODEL, D_MODEL), BF16), jax.ShapeDtypeStruct((PLE_DIM, D_MODEL), BF16)]
                  + [jax.ShapeDtypeStruct((1, D_MODEL), F32)] * 3,
        scratch_shapes=[pltpu.VMEM((D_MODEL, D_MODEL), F32), pltpu.VMEM((PLE_DIM, D_MODEL), F32)],
        compiler_params=_cparams(1, VMEM_MID),
    )(dx3, x2, z, pe, h3, p, f, wgate, g_ple_post, g_ple, g_post_ffn)


def _bwd_ffn(layer, df, h2, gs, us, wgu, wd):
    T = h2.shape[0]
    tm = ROW_TILE
    nt = T // tm
    last = FF_CHUNKS - 1

    def body(df_ref, h2_ref, gs_ref, us_ref, wgu_ref, wd_ref, dh_ref, dgu_ref, dwd_ref, acc_h, acc_g, acc_u, acc_d):
        k = pl.program_id(0)
        i = pl.program_id(1)
        first = i == 0
        rows = pl.ds(pl.multiple_of(i * tm, tm), tm)
        g = gs_ref[...].astype(F32)
        u = us_ref[...].astype(F32)
        sg = _sigmoid(g)
        silu = g * sg
        a = (silu * u).astype(BF16)
        dfv = df_ref[...]
        h = h2_ref[...]
        da = _dot_nt(dfv, wd_ref[...])
        dg = (da * u * (sg * (1.0 + g * (1.0 - sg)))).astype(BF16)
        du = (da * silu).astype(BF16)
        _acc(acc_d, _dot_tn(a, dfv), first)
        _acc(acc_g, _dot_tn(h, dg), first)
        _acc(acc_u, _dot_tn(h, du), first)
        dh = _dot_nt(dg, wgu_ref[0]) + _dot_nt(du, wgu_ref[1])

        @pl.when(k == 0)
        def _():
            acc_h[rows, :] = dh

        @pl.when(jnp.logical_and(k > 0, k < last))
        def _():
            acc_h[rows, :] += dh

        @pl.when(k == last)
        def _():
            dh_ref[...] = acc_h[rows, :] + dh

        @pl.when(i == nt - 1)
        def _():
            dgu_ref[0] = acc_g[...].astype(BF16)
            dgu_ref[1] = acc_u[...].astype(BF16)
            dwd_ref[...] = acc_d[...].astype(BF16)

    return pl.pallas_call(
        body, name=f"bwd_ffn{layer}", grid=(FF_CHUNKS, nt),
        in_specs=[pl.BlockSpec((tm, D_MODEL), lambda k, i: (i, 0)),
                  pl.BlockSpec((tm, D_MODEL), lambda k, i: (i, 0)),
                  pl.BlockSpec((None, tm, FF_BLOCK), lambda k, i: (k, i, 0)),
                  pl.BlockSpec((None, tm, FF_BLOCK), lambda k, i: (k, i, 0)),
                  pl.BlockSpec((None, None, 2, D_MODEL, FF_BLOCK), lambda k, i: (layer, k, 0, 0, 0)),
                  pl.BlockSpec((None, FF_BLOCK, D_MODEL), lambda k, i: (layer, k, 0))],
        out_specs=[pl.BlockSpec((tm, D_MODEL), lambda k, i: (jnp.where(k == last, i, 0), 0)),
                   pl.BlockSpec((None, 2, D_MODEL, FF_BLOCK), lambda k, i: (k, 0, 0, 0)),
                   pl.BlockSpec((FF_BLOCK, D_MODEL), lambda k, i: (k, 0))],
        out_shape=[jax.ShapeDtypeStruct((T, D_MODEL), F32),
                   jax.ShapeDtypeStruct((FF_CHUNKS, 2, D_MODEL, FF_BLOCK), BF16),
                   jax.ShapeDtypeStruct((D_FF, D_MODEL), BF16)],
        scratch_shapes=[pltpu.VMEM((T, D_MODEL), F32), pltpu.VMEM((D_MODEL, FF_BLOCK), F32),
                        pltpu.VMEM((D_MODEL, FF_BLOCK), F32), pltpu.VMEM((FF_BLOCK, D_MODEL), F32)],
        compiler_params=_cparams(2, VMEM_BIG),
    )(df, h2, gs, us, wgu, wd)


def _bwd_attn_out(dx2, dh2, x1, y, attn, wo, g_ffn, g_post):
    T = x1.shape[0]
    nt = T // ROW_TILE

    def body(dx2_ref, dh2_ref, x1_ref, y_ref, a_ref, wo_ref, gffn_ref, gpost_ref,
             dx1_ref, da_ref, dwo_ref, dgf_ref, dgp_ref, acc):
        i = pl.program_id(0)
        first = i == 0
        dxn, dgf = _rms_bwd(x1_ref[...], gffn_ref[...], dh2_ref[...])
        dx1 = dx2_ref[...] + dxn
        dx1_ref[...] = dx1
        dy, dgp = _rms_bwd(y_ref[...], gpost_ref[...], dx1)
        dyb = dy.astype(BF16)
        da_ref[...] = _dot_nt(dyb, wo_ref[...]).astype(BF16)
        _acc(acc, _dot_tn(a_ref[...], dyb), first)
        _acc(dgf_ref, dgf, first)
        _acc(dgp_ref, dgp, first)

        @pl.when(i == nt - 1)
        def _():
            dwo_ref[...] = acc[...].astype(BF16)

    return pl.pallas_call(
        body, name="bwd_attn_out", grid=(nt,),
        in_specs=[_row_spec(D_MODEL)] * 5 + [_full_spec((D_MODEL, D_MODEL)), _vec_spec(), _vec_spec()],
        out_specs=[_row_spec(D_MODEL), _row_spec(D_MODEL), _full_spec((D_MODEL, D_MODEL)), _vec_spec(), _vec_spec()],
        out_shape=[jax.ShapeDtypeStruct((T, D_MODEL), F32), jax.ShapeDtypeStruct((T, D_MODEL), BF16),
                   jax.ShapeDtypeStruct((D_MODEL, D_MODEL), BF16)] + [jax.ShapeDtypeStruct((1, D_MODEL), F32)] * 2,
        scratch_shapes=[pltpu.VMEM((D_MODEL, D_MODEL), F32)],
        compiler_params=_cparams(1, VMEM_MID),
    )(dx2, dh2, x1, y, attn, wo, g_ffn, g_post)


def _bwd_attention(q, dattn, kpad, vpad, sinks):
    T = q.shape[0]
    nb = T // ATT_BLOCK

    def body(q_ref, do_ref, k_ref, v_ref, sink_ref, dq_ref, dk_ref, dv_ref, ds_ref):
        n = pl.program_id(0)

        @pl.when(n == 0)
        def _():
            dk_ref[...] = jnp.zeros_like(dk_ref)
            dv_ref[...] = jnp.zeros_like(dv_ref)
            ds_ref[...] = jnp.zeros_like(ds_ref)

        start = pl.multiple_of(n * ATT_BLOCK, ATT_BLOCK)
        win = pl.ds(start, 2 * ATT_BLOCK)
        kw = k_ref[win, :]
        vw = v_ref[win, :]
        relf, valid = _att_mask(n)
        lane = lax.broadcasted_iota(jnp.int32, (1, ATT_BLOCK), 1)
        dsink = jnp.zeros((1, ATT_BLOCK), F32)
        dqs, dks, dvs = [], [], []
        for kh in range(N_KV_HEADS):
            kk = kw[:, kh * HEAD_DIM:(kh + 1) * HEAD_DIM]
            vv = vw[:, kh * HEAD_DIM:(kh + 1) * HEAD_DIM]
            dk_h = jnp.zeros((2 * ATT_BLOCK, HEAD_DIM), F32)
            dv_h = jnp.zeros((2 * ATT_BLOCK, HEAD_DIM), F32)
            for gq in range(GQA_GROUP):
                h = kh * GQA_GROUP + gq
                qh = q_ref[:, h * HEAD_DIM:(h + 1) * HEAD_DIM]
                do = do_ref[:, h * HEAD_DIM:(h + 1) * HEAD_DIM]
                pr, ps = _att_probs(qh, kk, relf, valid, _alibi_slope(h), sink_ref[0, h])
                dp = _dot_nt(do, vv)
                delta = jnp.sum(pr * dp, axis=-1, keepdims=True)
                dsb = (pr * (dp - delta) * ATT_SCALE).astype(BF16)
                dsink = dsink + jnp.where(lane == h, -jnp.sum(ps * delta, axis=0, keepdims=True), 0.0)
                dqs.append(_dot(dsb, kk))
                dk_h = dk_h + _dot_tn(dsb, qh)
                dv_h = dv_h + _dot_tn(pr.astype(BF16), do)
            dks.append(dk_h)
            dvs.append(dv_h)
        dq_ref[...] = jnp.concatenate(dqs, axis=1).astype(BF16)
        dk_ref[win, :] += jnp.concatenate(dks, axis=1)
        dv_ref[win, :] += jnp.concatenate(dvs, axis=1)
        ds_ref[...] += dsink

    return pl.pallas_call(
        body, name="bwd_attention", grid=(nb,),
        in_specs=[_row_spec(D_MODEL, ATT_BLOCK), _row_spec(D_MODEL, ATT_BLOCK), _full_spec((T + ATT_BLOCK, KV_DIM)),
                  _full_spec((T + ATT_BLOCK, KV_DIM)), pl.BlockSpec(memory_space=pltpu.SMEM)],
        out_specs=[_row_spec(D_MODEL, ATT_BLOCK), _full_spec((T + ATT_BLOCK, KV_DIM)), _full_spec((T + ATT_BLOCK, KV_DIM)),
                   _full_spec((1, ATT_BLOCK))],
        out_shape=[jax.ShapeDtypeStruct((T, D_MODEL), BF16), jax.ShapeDtypeStruct((T + ATT_BLOCK, KV_DIM), F32),
                   jax.ShapeDtypeStruct((T + ATT_BLOCK, KV_DIM), F32), jax.ShapeDtypeStruct((1, ATT_BLOCK), F32)],
        compiler_params=_cparams(1, VMEM_MID),
    )(q, dattn, kpad, vpad, sinks)


def _bwd_qkv(dxres, dq, dkv, x3, h1, hk, wq, wkv, g_mix, g_kv):
    T = x3.shape[0]
    nt = T // ROW_TILE

    def body(dxr_ref, dq_ref, dkv_ref, x_ref, h1_ref, hk_ref, wq_ref, wkv_ref, gmix_ref, gkv_ref,
             dx_ref, dwq_ref, dwkv_ref, dgm_ref, dgk_ref, acc_q, acc_kv):
        i = pl.program_id(0)
        first = i == 0
        dqv = dq_ref[...]
        dkvv = dkv_ref[...]
        xv = x_ref[...]
        d1, dgm = _rms_bwd(xv, gmix_ref[...], _dot_nt(dqv, wq_ref[...]))
        d2, dgk = _rms_bwd(xv, gkv_ref[...], _dot_nt(dkvv, wkv_ref[...]))
        dx_ref[...] = dxr_ref[...] + d1 + d2
        _acc(acc_q, _dot_tn(h1_ref[...], dqv), first)
        _acc(acc_kv, _dot_tn(hk_ref[...], dkvv), first)
        _acc(dgm_ref, dgm, first)
        _acc(dgk_ref, dgk, first)

        @pl.when(i == nt - 1)
        def _():
            dwq_ref[...] = acc_q[...].astype(BF16)
            dwkv_ref[...] = acc_kv[...].astype(BF16)

    return pl.pallas_call(
        body, name="bwd_qkv", grid=(nt,),
        in_specs=[_row_spec(D_MODEL), _row_spec(D_MODEL), _row_spec(2 * KV_DIM), _row_spec(D_MODEL), _row_spec(D_MODEL),
                  _row_spec(D_MODEL), _full_spec((D_MODEL, D_MODEL)), _full_spec((D_MODEL, 2 * KV_DIM)), _vec_spec(),
                  _vec_spec()],
        out_specs=[_row_spec(D_MODEL), _full_spec((D_MODEL, D_MODEL)), _full_spec((D_MODEL, 2 * KV_DIM)), _vec_spec(),
                   _vec_spec()],
        out_shape=[jax.ShapeDtypeStruct((T, D_MODEL), F32), jax.ShapeDtypeStruct((D_MODEL, D_MODEL), BF16),
                   jax.ShapeDtypeStruct((D_MODEL, 2 * KV_DIM), BF16)] + [jax.ShapeDtypeStruct((1, D_MODEL), F32)] * 2,
        scratch_shapes=[pltpu.VMEM((D_MODEL, D_MODEL), F32), pltpu.VMEM((D_MODEL, 2 * KV_DIM), F32)],
        compiler_params=_cparams(1, VMEM_MID),
    )(dxres, dq, dkv, x3, h1, hk, wq, wkv, g_mix, g_kv)


def _bwd_pool_mixer(dx2, dh2, x1, x, yraw, d, wp, scale, g_ffn, g_post, g_pre):
    T = x.shape[0]
    tm = ROW_TILE
    nt = T // tm

    def body(dx2_ref, dh2_ref, x1_ref, x_ref, yraw_ref, d_ref, wp_ref, sc_ref, gffn_ref, gpost_ref, gpre_ref,
             dx_ref, dwp_ref, dsc_ref, dgf_ref, dgp_ref, dgm_ref, carry, acc):
        i = pl.program_id(0)
        first = i == 0
        tile = nt - 1 - i

        @pl.when(first)
        def _():
            carry[...] = jnp.zeros_like(carry)

        dxn, dgf = _rms_bwd(x1_ref[...], gffn_ref[...], dh2_ref[...])
        dx1 = dx2_ref[...] + dxn
        yraw = yraw_ref[...]
        sc = sc_ref[...]
        dy, dgp = _rms_bwd(yraw * sc, gpost_ref[...], dx1)
        dsc = jnp.sum(dy * yraw, axis=0, keepdims=True)
        dyb = (dy * sc).astype(BF16)
        dv = d_ref[...]
        dds = []
        for g in range(N_POOL_GROUPS):
            cols = slice(g * POOL_GROUP, (g + 1) * POOL_GROUP)
            dds.append(_dot_nt(dyb[:, cols], wp_ref[g]))
            _acc(acc.at[g], _dot_tn(dv[:, cols], dyb[:, cols]), first)
        dd = jnp.concatenate(dds, axis=1)
        e = dd / _pool_counts(tile * tm, tm)
        ext = jnp.concatenate([e, carry[...]], axis=0)
        carry[...] = e[:POOL_HALO, :]
        sums = _window_sums(ext, lambda k: tm + POOL_HALO - k)[:tm, :]
        dxm, dgm = _rms_bwd(x_ref[...], gpre_ref[...], sums - dd)
        dx_ref[...] = dx1 + dxm
        _acc(dsc_ref, dsc, first)
        _acc(dgf_ref, dgf, first)
        _acc(dgp_ref, dgp, first)
        _acc(dgm_ref, dgm, first)

        @pl.when(i == nt - 1)
        def _():
            dwp_ref[...] = acc[...].astype(BF16)

    rev = pl.BlockSpec((tm, D_MODEL), lambda i: (nt - 1 - i, 0))
    return pl.pallas_call(
        body, name="bwd_pool_mixer", grid=(nt,),
        in_specs=[rev] * 6 + [_full_spec((N_POOL_GROUPS, POOL_GROUP, POOL_GROUP))] + [_vec_spec()] * 4,
        out_specs=[rev, _full_spec((N_POOL_GROUPS, POOL_GROUP, POOL_GROUP))] + [_vec_spec()] * 4,
        out_shape=[jax.ShapeDtypeStruct((T, D_MODEL), F32),
                   jax.ShapeDtypeStruct((N_POOL_GROUPS, POOL_GROUP, POOL_GROUP), BF16)]
                  + [jax.ShapeDtypeStruct((1, D_MODEL), F32)] * 4,
        scratch_shapes=[pltpu.VMEM((POOL_HALO, D_MODEL), F32), pltpu.VMEM((N_POOL_GROUPS, POOL_GROUP, POOL_GROUP), F32)],
        compiler_params=_cparams(1, VMEM_MID),
    )(dx2, dh2, x1, x, yraw, d, wp, scale, g_ffn, g_post, g_pre)


def _my_place():
    return lax.axis_index("x"), lax.axis_index("y"), lax.axis_index("c")


def _dev_index(px, py, pc):
    return 4 * px + 2 * py + pc


def _peer_by_relation(r):
    x, y, c = _my_place()
    return (x ^ ((r >> 2) & 1), y ^ ((r >> 1) & 1), c ^ (r & 1))


def _slot_pool(ref, j):
    return ref.at[:, pl.ds(pl.multiple_of(j * 32, 32), 32), :]


def _slot_scale(ref, j):
    return ref.at[:, pl.ds(pl.multiple_of(j * 128, 128), 128)]


def _slot_rows128(ref, j):
    return ref.at[pl.ds(pl.multiple_of(j * 128, 128), 128), :]


def _slot_gu(ref, j):
    return ref.at[:, j % FF_CHUNKS, j // FF_CHUNKS]


def _slot_wd(ref, j):
    return ref.at[:, pl.ds(pl.multiple_of(j * WD_ROWS, 16), WD_ROWS), :]


def _slot_rows128_l(ref, j):
    return ref.at[:, pl.ds(pl.multiple_of(j * 128, 128), 128), :]


def _slot_cols128_l(ref, j):
    return ref.at[:, :, pl.ds(pl.multiple_of(j * 128, 128), 128)]


_GATHER_SLOTS = (_slot_pool, _slot_scale, _slot_rows128, _slot_rows128, _slot_rows128, _slot_gu, _slot_wd,
                 _slot_rows128_l, _slot_cols128_l)


def _all_gather_weights(shards):
    n_t = len(shards)
    out_shape = [
        jax.ShapeDtypeStruct((N_POOL_GROUPS, POOL_GROUP, POOL_GROUP), BF16),
        jax.ShapeDtypeStruct((1, D_MODEL), F32),
        jax.ShapeDtypeStruct((D_MODEL, 2 * KV_DIM), BF16),
        jax.ShapeDtypeStruct((D_MODEL, D_MODEL), BF16),
        jax.ShapeDtypeStruct((D_MODEL, D_MODEL), BF16),
        jax.ShapeDtypeStruct((2, FF_CHUNKS, 2, D_MODEL, FF_BLOCK), BF16),
        jax.ShapeDtypeStruct((2, D_FF, D_MODEL), BF16),
        jax.ShapeDtypeStruct((2, D_MODEL, D_MODEL), BF16),
        jax.ShapeDtypeStruct((2, PLE_DIM, D_MODEL), BF16),
    ]

    def body(*refs):
        srcs = refs[:n_t]
        outs = refs[n_t:2 * n_t]
        send_sems, recv_sems, local_sems = refs[2 * n_t:]
        x, y, c = _my_place()
        me, sibling = (x, y, c), (x, y, 1 - c)
        chips = [(1 - x, y), (x, 1 - y), (1 - x, 1 - y)]

        def slot(t, dev):
            return _GATHER_SLOTS[t](outs[t], _dev_index(*dev))

        def copy(t, k, block, to, src=None):
            return pltpu.make_async_remote_copy(
                src_ref=slot(t, block) if src is None else src, dst_ref=slot(t, block),
                send_sem=send_sems.at[t, k], recv_sem=recv_sems.at[t, k], device_id=to, device_id_type=MESH)

        mine = [pltpu.make_async_copy(srcs[t], slot(t, me), local_sems.at[t]) for t in range(n_t)]
        for cp in mine:
            cp.start()
        first = []
        for t in range(n_t):
            first.append(copy(t, 0, me, sibling, src=srcs[t]))
            first += [copy(t, 1 + j, me, (*chip, c), src=srcs[t]) for j, chip in enumerate(chips)]
        for cp in first:
            cp.start()
        passed = []
        for j, chip in enumerate(chips):
            for t in range(n_t):
                copy(t, 1 + j, (*chip, c), me).wait_recv()
                fwd = copy(t, 4 + j, (*chip, c), sibling)
                fwd.start()
                passed.append(fwd)
        for t in range(n_t):
            copy(t, 0, sibling, me).wait_recv()
            for j, chip in enumerate(chips):
                copy(t, 4 + j, (*chip, 1 - c), me).wait_recv()
        for cp in first + passed:
            cp.wait_send()
        for cp in mine:
            cp.wait()

    return pl.pallas_call(
        body, name="all_gather_weights", out_shape=out_shape,
        in_specs=[ANY] * n_t, out_specs=[ANY] * n_t,
        scratch_shapes=[pltpu.SemaphoreType.DMA((n_t, 7)), pltpu.SemaphoreType.DMA((n_t, 7)),
                        pltpu.SemaphoreType.DMA((n_t,))],
    )(*shards)


def _block_pool(ref, j):
    return ref.at[:, pl.ds(pl.multiple_of(j * 32, 32), 32), :]


def _block_rows128(ref, j):
    return ref.at[pl.ds(pl.multiple_of(j * 128, 128), 128), :]


def _block_gu(ref, j):
    return ref.at[j % FF_CHUNKS, j // FF_CHUNKS]


def _block_wd(ref, j):
    return ref.at[pl.ds(pl.multiple_of(j * WD_ROWS, 16), WD_ROWS), :]


def _block_cols128(ref, j):
    return ref.at[:, pl.ds(pl.multiple_of(j * 128, 128), 128)]


_SCATTER_PIECES = (
    (_block_pool, 0, None), (_block_rows128, 1, None), (_block_rows128, 2, None), (_block_rows128, 3, None),
    (_block_gu, 4, 0), (_block_gu, 4, 1), (_block_wd, 5, 0), (_block_wd, 5, 1),
    (_block_rows128, 6, 0), (_block_rows128, 6, 1), (_block_cols128, 7, 0), (_block_cols128, 7, 1),
)


def _scatter_grads(grads):
    n_p = len(grads)
    out_shape = [
        jax.ShapeDtypeStruct((N_DEV, N_POOL_GROUPS, 32, POOL_GROUP), BF16),
        jax.ShapeDtypeStruct((N_DEV, 128, 2 * KV_DIM), BF16),
        jax.ShapeDtypeStruct((N_DEV, 128, D_MODEL), BF16),
        jax.ShapeDtypeStruct((N_DEV, 128, D_MODEL), BF16),
        jax.ShapeDtypeStruct((N_DEV, 2, D_MODEL, FF_BLOCK), BF16),
        jax.ShapeDtypeStruct((N_DEV, 2, WD_ROWS, D_MODEL), BF16),
        jax.ShapeDtypeStruct((N_DEV, 2, 128, D_MODEL), BF16),
        jax.ShapeDtypeStruct((N_DEV, 2, PLE_DIM, 128), BF16),
    ]
    n_o = len(out_shape)

    def body(*refs):
        srcs = refs[:n_p]
        outs = refs[n_p:n_p + n_o]
        send_sems, recv_sems, local_sems = refs[n_p + n_o:]
        x, y, c = _my_place()
        me = _dev_index(x, y, c)

        def landing(t, sender):
            _, o, layer = _SCATTER_PIECES[t]
            return outs[o].at[sender] if layer is None else outs[o].at[sender, layer]

        def copy(t, r):
            peer = _peer_by_relation(r)
            return pltpu.make_async_remote_copy(
                src_ref=_SCATTER_PIECES[t][0](srcs[t], _dev_index(*peer)), dst_ref=landing(t, me),
                send_sem=send_sems.at[t, r - 1], recv_sem=recv_sems.at[t, r - 1], device_id=peer, device_id_type=MESH)

        mine = [pltpu.make_async_copy(_SCATTER_PIECES[t][0](srcs[t], me), landing(t, me), local_sems.at[t])
                for t in range(n_p)]
        for cp in mine:
            cp.start()
        sends = [copy(t, r) for t in range(n_p) for r in range(1, N_DEV)]
        for cp in sends:
            cp.start()
        for cp in sends:
            cp.wait()
        for cp in mine:
            cp.wait()

    return pl.pallas_call(
        body, name="scatter_grads", out_shape=out_shape,
        in_specs=[ANY] * n_p, out_specs=[ANY] * n_o,
        scratch_shapes=[pltpu.SemaphoreType.DMA((n_p, N_DEV - 1)), pltpu.SemaphoreType.DMA((n_p, N_DEV - 1)),
                        pltpu.SemaphoreType.DMA((n_p,))],
    )(*grads)


def _adamw_math(w, g, m, v):
    m = ADAM_B1 * m + (1.0 - ADAM_B1) * g
    v = ADAM_B2 * v + (1.0 - ADAM_B2) * (g * g)
    m_hat = m / (1.0 - ADAM_B1 ** ADAM_STEP)
    v_hat = v / (1.0 - ADAM_B2 ** ADAM_STEP)
    delta = -ADAM_LR * (m_hat / (jnp.sqrt(v_hat) + ADAM_EPS) + ADAM_WD * w)
    return delta, m, v


def _adamw(name, w, m, v, landing, tr):
    R, C = w.shape

    def body(w_ref, m_ref, v_ref, l_ref, g_ref, d_ref, nm_ref, nv_ref):
        g = l_ref[0].astype(F32)
        for s in range(1, N_DEV):
            g = g + l_ref[s].astype(F32)
        g_ref[...] = g
        d_ref[...], nm_ref[...], nv_ref[...] = _adamw_math(w_ref[...], g, m_ref[...], v_ref[...])

    spec = pl.BlockSpec((tr, C), lambda i: (i, 0))
    return pl.pallas_call(
        body, name=f"adamw_{name}", grid=(R // tr,),
        in_specs=[spec, spec, spec, pl.BlockSpec((N_DEV, tr, C), lambda i: (0, i, 0))],
        out_specs=[spec] * 4, out_shape=[jax.ShapeDtypeStruct((R, C), F32)] * 4,
        compiler_params=_cparams(1, VMEM_MID),
    )(w, m, v, landing)


def _small_all_reduce_adamw(part, w, m, v):
    def body(part_ref, w_ref, m_ref, v_ref, g_ref, d_ref, nm_ref, nv_ref, buf, send_sems, recv_sems):
        x, y, c = _my_place()
        me = _dev_index(x, y, c)
        buf[me] = part_ref[...]
        copies = [pltpu.make_async_remote_copy(
            src_ref=part_ref, dst_ref=buf.at[me], send_sem=send_sems.at[r - 1], recv_sem=recv_sems.at[r - 1],
            device_id=_peer_by_relation(r), device_id_type=MESH) for r in range(1, N_DEV)]
        for cp in copies:
            cp.start()
        for cp in copies:
            cp.wait()
        g = buf[0]
        for s in range(1, N_DEV):
            g = g + buf[s]
        g_ref[...] = g
        d_ref[...], nm_ref[...], nv_ref[...] = _adamw_math(w_ref[...], g, m_ref[...], v_ref[...])

    vm = pl.BlockSpec(memory_space=pltpu.VMEM)
    return pl.pallas_call(
        body, name="small_all_reduce_adamw", out_shape=[jax.ShapeDtypeStruct((SV_ROWS, D_MODEL), F32)] * 4,
        in_specs=[vm] * 4, out_specs=[vm] * 4,
        scratch_shapes=[pltpu.VMEM((N_DEV, SV_ROWS, D_MODEL), F32), pltpu.SemaphoreType.DMA((N_DEV - 1,)),
                        pltpu.SemaphoreType.DMA((N_DEV - 1,))],
    )(part, w, m, v)


def _local_step(x, p, tgt, gains, sinks, W):
    wp, scale, wkv, wq, wo, wgu, wd, wgate, wproj = W
    row = lambda a, i: a[i:i + 1]
    g_pre_mix, g_post_mix = gains["pre_mix_g"], gains["post_mix_g"]
    g_pre_ffn, g_post_ffn = gains["pre_ffn_g"], gains["post_ffn_g"]
    g_ple, g_ple_post, g_kv = gains["ple_g"], gains["ple_post_g"], gains["kv_g"]

    x1_0, h2_0, yraw, dpool = _fwd_pool_mixer(x, row(g_pre_mix, 0), wp, scale, row(g_post_mix, 0), row(g_pre_ffn, 0))
    gs0, us0, f0, x2_0, h3_0 = _fwd_ffn(0, h2_0, x1_0, wgu, wd, row(g_post_ffn, 0), row(g_ple, 0))
    x3_0, z0, pe0 = _fwd_ple(0, x2_0, h3_0, p[0], wgate, wproj, row(g_ple_post, 0))
    hk, h1, q, kv = _fwd_qkv(x3_0, g_kv, row(g_pre_mix, 1), wkv, wq)
    front = ((ATT_BLOCK, 0), (0, 0))
    kpad = jnp.pad(kv[:, :KV_DIM], front)
    vpad = jnp.pad(kv[:, KV_DIM:], front)
    attn = _fwd_attention(q, kpad, vpad, sinks)
    y1, x1_1, h2_1 = _fwd_attn_out(attn, x3_0, wo, row(g_post_mix, 1), row(g_pre_ffn, 1))
    gs1, us1, f1, x2_1, h3_1 = _fwd_ffn(1, h2_1, x1_1, wgu, wd, row(g_post_ffn, 1), row(g_ple, 1))
    dx3_1, z1, pe1, loss = _fwd_ple(1, x2_1, h3_1, p[1], wgate, wproj, row(g_ple_post, 1), target=tgt)

    dx2_1, df1, dwgate1, dwproj1, dg_ple_post1, dg_ple1, dg_post_ffn1 = _bwd_ple(
        1, dx3_1, x2_1, z1, pe1, h3_1, p[1], f1, wgate, row(g_ple_post, 1), row(g_ple, 1), row(g_post_ffn, 1))
    dh2_1, dgu1, dwd1 = _bwd_ffn(1, df1, h2_1, gs1, us1, wgu, wd)
    dx1_1, dattn, dwo, dg_pre_ffn1, dg_post_mix1 = _bwd_attn_out(
        dx2_1, dh2_1, x1_1, y1, attn, wo, row(g_pre_ffn, 1), row(g_post_mix, 1))
    dq, dkpad, dvpad, dsinks = _bwd_attention(q, dattn, kpad, vpad, sinks)
    dkv = jnp.concatenate([dkpad[ATT_BLOCK:], dvpad[ATT_BLOCK:]], axis=1).astype(BF16)
    dx3_0, dwq, dwkv, dg_pre_mix1, dg_kv = _bwd_qkv(dx1_1, dq, dkv, x3_0, h1, hk, wq, wkv, row(g_pre_mix, 1), g_kv)
    dx2_0, df0, dwgate0, dwproj0, dg_ple_post0, dg_ple0, dg_post_ffn0 = _bwd_ple(
        0, dx3_0, x2_0, z0, pe0, h3_0, p[0], f0, wgate, row(g_ple_post, 0), row(g_ple, 0), row(g_post_ffn, 0))
    dh2_0, dgu0, dwd0 = _bwd_ffn(0, df0, h2_0, gs0, us0, wgu, wd)
    grad_x, dwp, dscale, dg_pre_ffn0, dg_post_mix0, dg_pre_mix0 = _bwd_pool_mixer(
        dx2_0, dh2_0, x1_0, x, yraw, dpool, wp, scale, row(g_pre_ffn, 0), row(g_post_mix, 0), row(g_pre_mix, 0))

    big = (dwp, dwkv, dwq, dwo, dgu0, dgu1, dwd0, dwd1, dwgate0, dwgate1, dwproj0, dwproj1)
    lanes = lambda a: jnp.pad(a, ((0, 0), (0, D_MODEL - a.shape[1])))
    small = jnp.concatenate([
        dg_pre_mix0, dg_pre_mix1, dg_post_mix0, dg_post_mix1, dg_pre_ffn0, dg_pre_ffn1, dg_post_ffn0, dg_post_ffn1,
        dg_ple0, dg_ple1, dg_ple_post0, dg_ple_post1, dg_kv, dscale, lanes(dsinks[:, :N_HEADS]), lanes(loss)], axis=0)
    return grad_x, big, small


def kernel(x, p, pre_mix_g, post_mix_g, pre_ffn_g, post_ffn_g, pool_w, pool_scale, kv_g, w_kv, w_q, sinks, w_o, w_gu, w_down, ple_g, w_ple_gate, w_ple_proj, ple_post_g, loss_target, m_pre_mix_g, m_post_mix_g, m_pre_ffn_g, m_post_ffn_g, m_pool_w, m_pool_scale, m_kv_g, m_w_kv, m_w_q, m_sinks, m_w_o, m_w_gu, m_w_down, m_ple_g, m_w_ple_gate, m_w_ple_proj, m_ple_post_g, v_pre_mix_g, v_post_mix_g, v_pre_ffn_g, v_post_ffn_g, v_pool_w, v_pool_scale, v_kv_g, v_w_kv, v_w_q, v_sinks, v_w_o, v_w_gu, v_w_down, v_ple_g, v_w_ple_gate, v_w_ple_proj, v_ple_post_g):
    me = _dev_index(*_my_place())

    shards = (pool_w[0].astype(BF16), pool_scale, w_kv.astype(BF16), w_q[0].astype(BF16), w_o[0].astype(BF16),
              w_gu.astype(BF16), w_down.astype(BF16), w_ple_gate.astype(BF16), w_ple_proj.astype(BF16))
    W = _all_gather_weights(shards)

    gains = dict(pre_mix_g=pre_mix_g, post_mix_g=post_mix_g, pre_ffn_g=pre_ffn_g, post_ffn_g=post_ffn_g,
                 ple_g=ple_g, ple_post_g=ple_post_g, kv_g=kv_g[None, :])
    grad_x, big, small = _local_step(x[0], p[:, 0], loss_target[0], gains, sinks, W)

    l_pool, l_kv, l_q, l_o, l_gu, l_wd, l_gate, l_proj = _scatter_grads(big)

    def update(name, w, m, v, landing, tr):
        R = w.size // w.shape[-1]
        outs = _adamw(name, w.reshape(R, -1), m.reshape(R, -1), v.reshape(R, -1), landing.reshape(N_DEV, R, -1), tr)
        return [o.reshape(w.shape) for o in outs]

    upd = {
        "pool_w": update("pool_w", pool_w, m_pool_w, v_pool_w, l_pool, 128),
        "w_kv": update("w_kv", w_kv, m_w_kv, v_w_kv, l_kv, 128),
        "w_q": update("w_q", w_q, m_w_q, v_w_q, l_q, 128),
        "w_o": update("w_o", w_o, m_w_o, v_w_o, l_o, 128),
        "w_gu": update("w_gu", w_gu, m_w_gu, v_w_gu, l_gu, 256),
        "w_down": update("w_down", w_down, m_w_down, v_w_down, l_wd, WD_ROWS),
        "w_ple_gate": update("w_ple_gate", w_ple_gate, m_w_ple_gate, v_w_ple_gate, l_gate, 256),
        "w_ple_proj": update("w_ple_proj", w_ple_proj, m_w_ple_proj, v_w_ple_proj, l_proj, 512),
    }

    lane0 = me * 128

    def slab(pre_mix, post_mix, pre_ffn, post_ffn, ple, ple_post, kv, scale_shard, snk):
        scale_row = lax.dynamic_update_slice(jnp.zeros((1, D_MODEL), F32), scale_shard, (0, lane0))
        snk_row = jnp.pad(snk, ((0, 0), (0, D_MODEL - N_HEADS)))
        return jnp.concatenate([pre_mix, post_mix, pre_ffn, post_ffn, ple, ple_post, kv[None, :], scale_row, snk_row,
                                jnp.zeros((1, D_MODEL), F32)], axis=0)

    sw = slab(pre_mix_g, post_mix_g, pre_ffn_g, post_ffn_g, ple_g, ple_post_g, kv_g, pool_scale, sinks)
    sm = slab(m_pre_mix_g, m_post_mix_g, m_pre_ffn_g, m_post_ffn_g, m_ple_g, m_ple_post_g, m_kv_g, m_pool_scale, m_sinks)
    sv = slab(v_pre_mix_g, v_post_mix_g, v_pre_ffn_g, v_post_ffn_g, v_ple_g, v_ple_post_g, v_kv_g, v_pool_scale, v_sinks)
    sg, sd, snm, snv = _small_all_reduce_adamw(small, sw, sm, sv)
    loss = sg[SV_LOSS, 0]

    def unslab(s):
        return {
            "pre_mix_g": s[SV_PRE_MIX:SV_PRE_MIX + 2], "post_mix_g": s[SV_POST_MIX:SV_POST_MIX + 2],
            "pre_ffn_g": s[SV_PRE_FFN:SV_PRE_FFN + 2], "post_ffn_g": s[SV_POST_FFN:SV_POST_FFN + 2],
            "ple_g": s[SV_PLE:SV_PLE + 2], "ple_post_g": s[SV_PLE_POST:SV_PLE_POST + 2], "kv_g": s[SV_KV],
            "pool_scale": lax.dynamic_slice(s, (SV_POOL_SCALE, lane0), (1, 128)),
            "sinks": s[SV_SINKS:SV_SINKS + 1, :N_HEADS],
        }

    names = ["pre_mix_g", "post_mix_g", "pre_ffn_g", "post_ffn_g", "pool_w", "pool_scale", "kv_g", "w_kv", "w_q",
             "sinks", "w_o", "w_gu", "w_down", "ple_g", "w_ple_gate", "w_ple_proj", "ple_post_g"]
    outs = [loss, grad_x[None]]
    for kind, slab_out in enumerate((sg, sd, snm, snv)):
        small_out = unslab(slab_out)
        outs += [upd[n][kind] if n in upd else small_out[n] for n in names]
    return tuple(outs)
```

```python
import functools

import jax
import jax.numpy as jnp
from jax import lax
from jax.experimental import pallas as pl
from jax.experimental.pallas import tpu as pltpu

F32 = jnp.float32
BF16 = jnp.bfloat16

N_DEV = 8
D_MODEL = 1024
N_POOL_GROUPS = 4
POOL_GROUP = 256
POOL_HALO = 16
HEAD_DIM = 64
N_HEADS = 16
N_KV_HEADS = 4
GQA_GROUP = 4
KV_DIM = N_KV_HEADS * HEAD_DIM
ATT_BLOCK = 128
D_FF = 2816
FF_CHUNKS = 4
FF_BLOCK = D_FF // FF_CHUNKS
WD_ROWS = D_FF // N_DEV
PLE_DIM = 256
EPS = 1e-6
NEG_INF = -1e30
ATT_SCALE = HEAD_DIM ** -0.5

ADAM_LR = 0.001
ADAM_B1 = 0.9
ADAM_B2 = 0.999
ADAM_EPS = 1e-08
ADAM_WD = 0.01
ADAM_STEP = 10

ROW_TILE = 256
VMEM_BIG = 56 * 1024 * 1024
VMEM_MID = 40 * 1024 * 1024

SV_ROWS = 16
SV_PRE_MIX, SV_POST_MIX, SV_PRE_FFN, SV_POST_FFN, SV_PLE, SV_PLE_POST = 0, 2, 4, 6, 8, 10
SV_KV, SV_POOL_SCALE, SV_SINKS, SV_LOSS = 12, 13, 14, 15

MESH = pl.DeviceIdType.MESH
ANY = pl.BlockSpec(memory_space=pl.ANY)


def _dot(a, b):
    return jnp.dot(a, b, preferred_element_type=F32)


def _dot_nt(a, b):
    return lax.dot_general(a, b, (((1,), (1,)), ((), ())), preferred_element_type=F32)


def _dot_tn(a, b):
    return lax.dot_general(a, b, (((0,), (0,)), ((), ())), preferred_element_type=F32)


def _rstd(x):
    return lax.rsqrt(jnp.mean(x * x, axis=-1, keepdims=True) + EPS)


def _rms(x, g):
    return x * _rstd(x) * g


def _rms_bwd(x, g, dy):
    r = _rstd(x)
    n = x * r
    dn = dy * g
    dx = r * (dn - n * jnp.mean(dn * n, axis=-1, keepdims=True))
    dg = jnp.sum(dy * n, axis=0, keepdims=True)
    return dx, dg


def _sigmoid(x):
    return 1.0 / (1.0 + jnp.exp(-x))


def _acc(ref, val, first):
    @pl.when(first)
    def _():
        ref[...] = val

    @pl.when(jnp.logical_not(first))
    def _():
        ref[...] += val


def _pool_counts(row0, rows):
    t = row0 + lax.broadcasted_iota(jnp.int32, (rows, D_MODEL), 0) + 1
    grp = lax.broadcasted_iota(jnp.int32, (rows, D_MODEL), 1) // POOL_GROUP
    win = jnp.left_shift(2, grp)
    return jnp.minimum(t, win).astype(F32)


def _window_sums(ext, shift_of):
    outs = []
    s = ext
    for gi in range(N_POOL_GROUPS):
        s = s + pltpu.roll(s, shift_of(1 << gi), axis=0)
        outs.append(s[:, :POOL_GROUP])
        s = s[:, POOL_GROUP:]
    return jnp.concatenate(outs, axis=1)


def _cparams(n_axes, vmem):
    return pltpu.CompilerParams(dimension_semantics=("arbitrary",) * n_axes, vmem_limit_bytes=vmem)


def _row_spec(cols, tm=ROW_TILE):
    return pl.BlockSpec((tm, cols), lambda i: (i, 0))


def _full_spec(shape):
    zeros = (0,) * len(shape)
    return pl.BlockSpec(shape, lambda *_: zeros)


def _vec_spec():
    return _full_spec((1, D_MODEL))


def _launch(body, *, name, grid, in_specs, out_specs, out_shape, args, scratch_shapes=(), vmem=VMEM_MID, job=None):
    n_in, n_out, n_scr = len(args), len(out_shape), len(scratch_shapes)
    j_args, j_out, j_scr = ([], [], []) if job is None else (job.args, job.out_shape, job.scratch)

    def run(*refs):
        groups, at = [], 0
        for n in (n_in, len(j_args), n_out, len(j_out), n_scr, len(j_scr)):
            groups.append(refs[at:at + n])
            at += n
        ins, j_ins, outs, j_outs, scr, j_sems = groups
        if job is None:
            body(*ins, *outs, *scr)
        elif not grid:
            job.start(j_ins, j_outs, j_sems)
            body(*ins, *outs, *scr)
            job.finish(j_ins, j_outs, j_sems)
        else:
            ids = [pl.program_id(a) for a in range(len(grid))]
            first = functools.reduce(jnp.logical_and, [i == 0 for i in ids])
            last = functools.reduce(jnp.logical_and, [i == g - 1 for i, g in zip(ids, grid)])
            pl.when(first)(lambda: job.start(j_ins, j_outs, j_sems))
            body(*ins, *outs, *scr)
            pl.when(last)(lambda: job.finish(j_ins, j_outs, j_sems))

    res = pl.pallas_call(
        run, name=name, grid=grid,
        in_specs=list(in_specs) + [ANY] * len(j_args), out_specs=list(out_specs) + [ANY] * len(j_out),
        out_shape=list(out_shape) + list(j_out), scratch_shapes=list(scratch_shapes) + list(j_scr),
        compiler_params=_cparams(len(grid), vmem),
    )(*args, *j_args)
    return res[:n_out], res[n_out:]


def _fwd_pool_mixer(x, g_pre, wp, scale, g_post, g_ffn, job=None):
    T = x.shape[0]
    tm = ROW_TILE
    nt = T // tm

    def body(x_ref, gpre_ref, wp_ref, sc_ref, gpost_ref, gffn_ref, x1_ref, h2_ref, yraw_ref, d_ref, carry):
        i = pl.program_id(0)

        @pl.when(i == 0)
        def _():
            carry[...] = jnp.zeros_like(carry)

        xv = x_ref[...]
        h = _rms(xv, gpre_ref[...])
        ext = jnp.concatenate([carry[...], h], axis=0)
        carry[...] = h[tm - POOL_HALO:, :]
        sums = _window_sums(ext, lambda k: k)[POOL_HALO:, :]
        d = sums / _pool_counts(i * tm, tm) - h
        db = d.astype(BF16)
        d_ref[...] = db
        yraw = jnp.concatenate(
            [_dot(db[:, g * POOL_GROUP:(g + 1) * POOL_GROUP], wp_ref[g]) for g in range(N_POOL_GROUPS)], axis=1)
        yraw_ref[...] = yraw
        x1 = xv + _rms(yraw * sc_ref[...], gpost_ref[...])
        x1_ref[...] = x1
        h2_ref[...] = _rms(x1, gffn_ref[...]).astype(BF16)

    return _launch(
        body, name="fwd_pool_mixer", grid=(nt,),
        in_specs=[_row_spec(D_MODEL), _vec_spec(), _full_spec((N_POOL_GROUPS, POOL_GROUP, POOL_GROUP)), _vec_spec(),
                  _vec_spec(), _vec_spec()],
        out_specs=[_row_spec(D_MODEL)] * 4,
        out_shape=[jax.ShapeDtypeStruct((T, D_MODEL), F32), jax.ShapeDtypeStruct((T, D_MODEL), BF16),
                   jax.ShapeDtypeStruct((T, D_MODEL), F32), jax.ShapeDtypeStruct((T, D_MODEL), BF16)],
        scratch_shapes=[pltpu.VMEM((POOL_HALO, D_MODEL), F32)],
        args=(x, g_pre, wp, scale, g_post, g_ffn), job=job)


def _fwd_ffn(layer, h2, x1, wgu, wd, g_post, g_ple, job=None):
    T = h2.shape[0]
    tm = ROW_TILE
    nt = T // tm
    last = FF_CHUNKS - 1

    def body(h2_ref, x1_ref, wgu_ref, wd_ref, gpost_ref, gple_ref, gs_ref, us_ref, f_ref, x2_ref, h3_ref, acc):
        k = pl.program_id(0)
        i = pl.program_id(1)
        rows = pl.ds(pl.multiple_of(i * tm, tm), tm)
        h = h2_ref[...]
        g = _dot(h, wgu_ref[0])
        u = _dot(h, wgu_ref[1])
        gs_ref[...] = g.astype(BF16)
        us_ref[...] = u.astype(BF16)
        a = (g * _sigmoid(g) * u).astype(BF16)
        part = _dot(a, wd_ref[...])

        @pl.when(k == 0)
        def _():
            acc[rows, :] = part

        @pl.when(jnp.logical_and(k > 0, k < last))
        def _():
            acc[rows, :] += part

        @pl.when(k == last)
        def _():
            f = acc[rows, :] + part
            f_ref[...] = f
            x2 = x1_ref[...] + _rms(f, gpost_ref[...])
            x2_ref[...] = x2
            h3_ref[...] = _rms(x2, gple_ref[...]).astype(BF16)

    def late(k, i):
        return (jnp.where(k == last, i, 0), 0)

    return _launch(
        body, name=f"fwd_ffn{layer}", grid=(FF_CHUNKS, nt),
        in_specs=[pl.BlockSpec((tm, D_MODEL), lambda k, i: (i, 0)),
                  pl.BlockSpec((tm, D_MODEL), late),
                  pl.BlockSpec((None, 2, D_MODEL, FF_BLOCK), lambda k, i: (k, 0, 0, 0)),
                  pl.BlockSpec((FF_BLOCK, D_MODEL), lambda k, i: (k, 0)),
                  pl.BlockSpec((1, D_MODEL), lambda k, i: (0, 0)),
                  pl.BlockSpec((1, D_MODEL), lambda k, i: (0, 0))],
        out_specs=[pl.BlockSpec((None, tm, FF_BLOCK), lambda k, i: (k, i, 0)),
                   pl.BlockSpec((None, tm, FF_BLOCK), lambda k, i: (k, i, 0)),
                   pl.BlockSpec((tm, D_MODEL), late),
                   pl.BlockSpec((tm, D_MODEL), late),
                   pl.BlockSpec((tm, D_MODEL), late)],
        out_shape=[jax.ShapeDtypeStruct((FF_CHUNKS, T, FF_BLOCK), BF16),
                   jax.ShapeDtypeStruct((FF_CHUNKS, T, FF_BLOCK), BF16),
                   jax.ShapeDtypeStruct((T, D_MODEL), F32),
                   jax.ShapeDtypeStruct((T, D_MODEL), F32),
                   jax.ShapeDtypeStruct((T, D_MODEL), BF16)],
        scratch_shapes=[pltpu.VMEM((T, D_MODEL), F32)],
        args=(h2, x1, wgu, wd, g_post, g_ple), job=job)


def _fwd_ple(layer, x2, h3, p, wgate, wproj, g_post, target=None, job=None):
    T = x2.shape[0]
    tm = ROW_TILE
    nt = T // tm
    with_loss = target is not None

    def body(*refs):
        if with_loss:
            x2_ref, h3_ref, p_ref, wg_ref, wp_ref, gpost_ref, tgt_ref, out_ref, z_ref, pe_ref, loss_ref = refs
        else:
            x2_ref, h3_ref, p_ref, wg_ref, wp_ref, gpost_ref, out_ref, z_ref, pe_ref = refs
        z = _dot(h3_ref[...], wg_ref[...])
        pe = _dot(p_ref[...].astype(BF16), wp_ref[...])
        z_ref[...] = z
        pe_ref[...] = pe
        x3 = x2_ref[...] + _rms(pe * _sigmoid(z), gpost_ref[...])
        if with_loss:
            err = x3 - tgt_ref[...]
            out_ref[...] = err * (1.0 / D_MODEL)
            part = 0.5 * jnp.sum(jnp.mean(err * err, axis=-1, keepdims=True), axis=0, keepdims=True)
            _acc(loss_ref, part, pl.program_id(0) == 0)
        else:
            out_ref[...] = x3

    in_specs = [_row_spec(D_MODEL), _row_spec(D_MODEL), _row_spec(PLE_DIM), _full_spec((D_MODEL, D_MODEL)),
                _full_spec((PLE_DIM, D_MODEL)), _vec_spec()]
    out_specs = [_row_spec(D_MODEL)] * 3
    out_shape = [jax.ShapeDtypeStruct((T, D_MODEL), F32)] * 3
    args = [x2, h3, p, wgate, wproj, g_post]
    if with_loss:
        in_specs.append(_row_spec(D_MODEL))
        out_specs.append(_full_spec((1, 1)))
        out_shape.append(jax.ShapeDtypeStruct((1, 1), F32))
        args.append(target)
    return _launch(body, name=f"fwd_ple{layer}", grid=(nt,), in_specs=in_specs, out_specs=out_specs,
                   out_shape=out_shape, args=args, job=job)


def _fwd_qkv(x3, g_kv, g_mix, wkv, wq, job=None):
    T = x3.shape[0]
    nt = T // ROW_TILE

    def body(x_ref, gkv_ref, gmix_ref, wkv_ref, wq_ref, hk_ref, h1_ref, q_ref, kv_ref):
        xv = x_ref[...]
        r = _rstd(xv)
        hk = (xv * r * gkv_ref[...]).astype(BF16)
        h1 = (xv * r * gmix_ref[...]).astype(BF16)
        hk_ref[...] = hk
        h1_ref[...] = h1
        kv_ref[...] = _dot(hk, wkv_ref[...]).astype(BF16)
        q_ref[...] = _dot(h1, wq_ref[...]).astype(BF16)

    return _launch(
        body, name="fwd_qkv", grid=(nt,),
        in_specs=[_row_spec(D_MODEL), _vec_spec(), _vec_spec(), _full_spec((D_MODEL, 2 * KV_DIM)),
                  _full_spec((D_MODEL, D_MODEL))],
        out_specs=[_row_spec(D_MODEL), _row_spec(D_MODEL), _row_spec(D_MODEL), _row_spec(2 * KV_DIM)],
        out_shape=[jax.ShapeDtypeStruct((T, D_MODEL), BF16)] * 3 + [jax.ShapeDtypeStruct((T, 2 * KV_DIM), BF16)],
        args=(x3, g_kv, g_mix, wkv, wq), job=job)


def _alibi_slope(h):
    return 2.0 ** (-8.0 * (h + 1) / N_HEADS)


def _att_mask(n):
    qi = lax.broadcasted_iota(jnp.int32, (ATT_BLOCK, 2 * ATT_BLOCK), 0)
    si = lax.broadcasted_iota(jnp.int32, (ATT_BLOCK, 2 * ATT_BLOCK), 1)
    rel = ATT_BLOCK + qi - si
    valid = (rel >= 0) & (rel < ATT_BLOCK) & ((si >= ATT_BLOCK) | (n > 0))
    return rel.astype(F32), valid


def _att_probs(qh, kk, relf, valid, slope, sink):
    s = _dot_nt(qh, kk) * ATT_SCALE
    s = jnp.where(valid, s - slope * relf, NEG_INF)
    m = jnp.maximum(jnp.max(s, axis=-1, keepdims=True), sink)
    e = jnp.exp(s - m)
    es = jnp.exp(sink - m)
    inv = 1.0 / (jnp.sum(e, axis=-1, keepdims=True) + es)
    return e * inv, es * inv


def _fwd_attention(q, kpad, vpad, sinks, job=None):
    T = q.shape[0]
    nb = T // ATT_BLOCK

    def body(q_ref, k_ref, v_ref, sink_ref, o_ref):
        n = pl.program_id(0)
        start = pl.multiple_of(n * ATT_BLOCK, ATT_BLOCK)
        kw = k_ref[pl.ds(start, 2 * ATT_BLOCK), :]
        vw = v_ref[pl.ds(start, 2 * ATT_BLOCK), :]
        relf, valid = _att_mask(n)
        outs = []
        for h in range(N_HEADS):
            kh = h // GQA_GROUP
            qh = q_ref[:, h * HEAD_DIM:(h + 1) * HEAD_DIM]
            kk = kw[:, kh * HEAD_DIM:(kh + 1) * HEAD_DIM]
            vv = vw[:, kh * HEAD_DIM:(kh + 1) * HEAD_DIM]
            pr, _ = _att_probs(qh, kk, relf, valid, _alibi_slope(h), sink_ref[0, h])
            outs.append(_dot(pr.astype(BF16), vv))
        o_ref[...] = jnp.concatenate(outs, axis=1).astype(BF16)

    return _launch(
        body, name="fwd_attention", grid=(nb,),
        in_specs=[_row_spec(D_MODEL, ATT_BLOCK), _full_spec((T + ATT_BLOCK, KV_DIM)), _full_spec((T + ATT_BLOCK, KV_DIM)),
                  pl.BlockSpec(memory_space=pltpu.SMEM)],
        out_specs=[_row_spec(D_MODEL, ATT_BLOCK)],
        out_shape=[jax.ShapeDtypeStruct((T, D_MODEL), BF16)],
        args=(q, kpad, vpad, sinks), job=job)


def _fwd_attn_out(attn, x, wo, g_post, g_ffn, job=None):
    T = x.shape[0]
    nt = T // ROW_TILE

    def body(a_ref, x_ref, wo_ref, gpost_ref, gffn_ref, y_ref, x1_ref, h2_ref):
        y = _dot(a_ref[...], wo_ref[...])
        y_ref[...] = y
        x1 = x_ref[...] + _rms(y, gpost_ref[...])
        x1_ref[...] = x1
        h2_ref[...] = _rms(x1, gffn_ref[...]).astype(BF16)

    return _launch(
        body, name="fwd_attn_out", grid=(nt,),
        in_specs=[_row_spec(D_MODEL), _row_spec(D_MODEL), _full_spec((D_MODEL, D_MODEL)), _vec_spec(), _vec_spec()],
        out_specs=[_row_spec(D_MODEL)] * 3,
        out_shape=[jax.ShapeDtypeStruct((T, D_MODEL), F32), jax.ShapeDtypeStruct((T, D_MODEL), F32),
                   jax.ShapeDtypeStruct((T, D_MODEL), BF16)],
        args=(attn, x, wo, g_post, g_ffn), job=job)


def _bwd_ple(layer, dx3, x2, z, pe, h3, p, f, wgate, g_ple_post, g_ple, g_post_ffn, job=None):
    T = x2.shape[0]
    tm = ROW_TILE
    nt = T // tm

    def body(dx3_ref, x2_ref, z_ref, pe_ref, h3_ref, p_ref, f_ref, wg_ref, gpp_ref, gp_ref, gpf_ref,
             dx2_ref, df_ref, dwg_ref, dwp_ref, dgpp_ref, dgp_ref, dgpf_ref, acc_g, acc_p):
        i = pl.program_id(0)
        first = i == 0
        dx3v = dx3_ref[...]
        gate = _sigmoid(z_ref[...])
        pev = pe_ref[...]
        de, dgpp = _rms_bwd(pev * gate, gpp_ref[...], dx3v)
        dpe = (de * gate).astype(BF16)
        dz = (de * pev * gate * (1.0 - gate)).astype(BF16)
        _acc(acc_p, _dot_tn(p_ref[...].astype(BF16), dpe), first)
        _acc(acc_g, _dot_tn(h3_ref[...], dz), first)
        dh3 = _dot_nt(dz, wg_ref[...])
        dxn, dgp = _rms_bwd(x2_ref[...], gp_ref[...], dh3)
        dx2 = dx3v + dxn
        dx2_ref[...] = dx2
        df, dgpf = _rms_bwd(f_ref[...], gpf_ref[...], dx2)
        df_ref[...] = df.astype(BF16)
        _acc(dgpp_ref, dgpp, first)
        _acc(dgp_ref, dgp, first)
        _acc(dgpf_ref, dgpf, first)

        @pl.when(i == nt - 1)
        def _():
            dwg_ref[...] = acc_g[...].astype(BF16)
            dwp_ref[...] = acc_p[...].astype(BF16)

    return _launch(
        body, name=f"bwd_ple{layer}", grid=(nt,),
        in_specs=[_row_spec(D_MODEL)] * 5 + [_row_spec(PLE_DIM), _row_spec(D_MODEL), _full_spec((D_MODEL, D_MODEL)),
                  _vec_spec(), _vec_spec(), _vec_spec()],
        out_specs=[_row_spec(D_MODEL), _row_spec(D_MODEL), _full_spec((D_MODEL, D_MODEL)), _full_spec((PLE_DIM, D_MODEL)),
                   _vec_spec(), _vec_spec(), _vec_spec()],
        out_shape=[jax.ShapeDtypeStruct((T, D_MODEL), F32), jax.ShapeDtypeStruct((T, D_MODEL), BF16),
                   jax.ShapeDtypeStruct((D_MODEL, D_MODEL), BF16), jax.ShapeDtypeStruct((PLE_DIM, D_MODEL), BF16)]
                  + [jax.ShapeDtypeStruct((1, D_MODEL), F32)] * 3,
        scratch_shapes=[pltpu.VMEM((D_MODEL, D_MODEL), F32), pltpu.VMEM((PLE_DIM, D_MODEL), F32)],
        args=(dx3, x2, z, pe, h3, p, f, wgate, g_ple_post, g_ple, g_post_ffn), job=job)


def _bwd_ffn(layer, df, h2, gs, us, wgu, wd, job=None):
    T = h2.shape[0]
    tm = ROW_TILE
    nt = T // tm
    last = FF_CHUNKS - 1

    def body(df_ref, h2_ref, gs_ref, us_ref, wgu_ref, wd_ref, dh_ref, dgu_ref, dwd_ref, acc_h, acc_g, acc_u, acc_d):
        k = pl.program_id(0)
        i = pl.program_id(1)
        first = i == 0
        rows = pl.ds(pl.multiple_of(i * tm, tm), tm)
        g = gs_ref[...].astype(F32)
        u = us_ref[...].astype(F32)
        sg = _sigmoid(g)
        silu = g * sg
        a = (silu * u).astype(BF16)
        dfv = df_ref[...]
        h = h2_ref[...]
        da = _dot_nt(dfv, wd_ref[...])
        dg = (da * u * (sg * (1.0 + g * (1.0 - sg)))).astype(BF16)
        du = (da * silu).astype(BF16)
        _acc(acc_d, _dot_tn(a, dfv), first)
        _acc(acc_g, _dot_tn(h, dg), first)
        _acc(acc_u, _dot_tn(h, du), first)
        dh = _dot_nt(dg, wgu_ref[0]) + _dot_nt(du, wgu_ref[1])

        @pl.when(k == 0)
        def _():
            acc_h[rows, :] = dh

        @pl.when(jnp.logical_and(k > 0, k < last))
        def _():
            acc_h[rows, :] += dh

        @pl.when(k == last)
        def _():
            dh_ref[...] = acc_h[rows, :] + dh

        @pl.when(i == nt - 1)
        def _():
            dgu_ref[0] = acc_g[...].astype(BF16)
            dgu_ref[1] = acc_u[...].astype(BF16)
            dwd_ref[...] = acc_d[...].astype(BF16)

    return _launch(
        body, name=f"bwd_ffn{layer}", grid=(FF_CHUNKS, nt),
        in_specs=[pl.BlockSpec((tm, D_MODEL), lambda k, i: (i, 0)),
                  pl.BlockSpec((tm, D_MODEL), lambda k, i: (i, 0)),
                  pl.BlockSpec((None, tm, FF_BLOCK), lambda k, i: (k, i, 0)),
                  pl.BlockSpec((None, tm, FF_BLOCK), lambda k, i: (k, i, 0)),
                  pl.BlockSpec((None, 2, D_MODEL, FF_BLOCK), lambda k, i: (k, 0, 0, 0)),
                  pl.BlockSpec((FF_BLOCK, D_MODEL), lambda k, i: (k, 0))],
        out_specs=[pl.BlockSpec((tm, D_MODEL), lambda k, i: (jnp.where(k == last, i, 0), 0)),
                   pl.BlockSpec((None, 2, D_MODEL, FF_BLOCK), lambda k, i: (k, 0, 0, 0)),
                   pl.BlockSpec((FF_BLOCK, D_MODEL), lambda k, i: (k, 0))],
        out_shape=[jax.ShapeDtypeStruct((T, D_MODEL), F32),
                   jax.ShapeDtypeStruct((FF_CHUNKS, 2, D_MODEL, FF_BLOCK), BF16),
                   jax.ShapeDtypeStruct((D_FF, D_MODEL), BF16)],
        scratch_shapes=[pltpu.VMEM((T, D_MODEL), F32), pltpu.VMEM((D_MODEL, FF_BLOCK), F32),
                        pltpu.VMEM((D_MODEL, FF_BLOCK), F32), pltpu.VMEM((FF_BLOCK, D_MODEL), F32)],
        vmem=VMEM_BIG, args=(df, h2, gs, us, wgu, wd), job=job)


def _bwd_attn_out(dx2, dh2, x1, y, attn, wo, g_ffn, g_post, job=None):
    T = x1.shape[0]
    nt = T // ROW_TILE

    def body(dx2_ref, dh2_ref, x1_ref, y_ref, a_ref, wo_ref, gffn_ref, gpost_ref,
             dx1_ref, da_ref, dwo_ref, dgf_ref, dgp_ref, acc):
        i = pl.program_id(0)
        first = i == 0
        dxn, dgf = _rms_bwd(x1_ref[...], gffn_ref[...], dh2_ref[...])
        dx1 = dx2_ref[...] + dxn
        dx1_ref[...] = dx1
        dy, dgp = _rms_bwd(y_ref[...], gpost_ref[...], dx1)
        dyb = dy.astype(BF16)
        da_ref[...] = _dot_nt(dyb, wo_ref[...]).astype(BF16)
        _acc(acc, _dot_tn(a_ref[...], dyb), first)
        _acc(dgf_ref, dgf, first)
        _acc(dgp_ref, dgp, first)

        @pl.when(i == nt - 1)
        def _():
            dwo_ref[...] = acc[...].astype(BF16)

    return _launch(
        body, name="bwd_attn_out", grid=(nt,),
        in_specs=[_row_spec(D_MODEL)] * 5 + [_full_spec((D_MODEL, D_MODEL)), _vec_spec(), _vec_spec()],
        out_specs=[_row_spec(D_MODEL), _row_spec(D_MODEL), _full_spec((D_MODEL, D_MODEL)), _vec_spec(), _vec_spec()],
        out_shape=[jax.ShapeDtypeStruct((T, D_MODEL), F32), jax.ShapeDtypeStruct((T, D_MODEL), BF16),
                   jax.ShapeDtypeStruct((D_MODEL, D_MODEL), BF16)] + [jax.ShapeDtypeStruct((1, D_MODEL), F32)] * 2,
        scratch_shapes=[pltpu.VMEM((D_MODEL, D_MODEL), F32)],
        args=(dx2, dh2, x1, y, attn, wo, g_ffn, g_post), job=job)


def _bwd_attention(q, dattn, kpad, vpad, sinks, job=None):
    T = q.shape[0]
    nb = T // ATT_BLOCK

    def body(q_ref, do_ref, k_ref, v_ref, sink_ref, dq_ref, dk_ref, dv_ref, ds_ref):
        n = pl.program_id(0)

        @pl.when(n == 0)
        def _():
            dk_ref[...] = jnp.zeros_like(dk_ref)
            dv_ref[...] = jnp.zeros_like(dv_ref)
            ds_ref[...] = jnp.zeros_like(ds_ref)

        start = pl.multiple_of(n * ATT_BLOCK, ATT_BLOCK)
        win = pl.ds(start, 2 * ATT_BLOCK)
        kw = k_ref[win, :]
        vw = v_ref[win, :]
        relf, valid = _att_mask(n)
        lane = lax.broadcasted_iota(jnp.int32, (1, ATT_BLOCK), 1)
        dsink = jnp.zeros((1, ATT_BLOCK), F32)
        dqs, dks, dvs = [], [], []
        for kh in range(N_KV_HEADS):
            kk = kw[:, kh * HEAD_DIM:(kh + 1) * HEAD_DIM]
            vv = vw[:, kh * HEAD_DIM:(kh + 1) * HEAD_DIM]
            dk_h = jnp.zeros((2 * ATT_BLOCK, HEAD_DIM), F32)
            dv_h = jnp.zeros((2 * ATT_BLOCK, HEAD_DIM), F32)
            for gq in range(GQA_GROUP):
                h = kh * GQA_GROUP + gq
                qh = q_ref[:, h * HEAD_DIM:(h + 1) * HEAD_DIM]
                do = do_ref[:, h * HEAD_DIM:(h + 1) * HEAD_DIM]
                pr, ps = _att_probs(qh, kk, relf, valid, _alibi_slope(h), sink_ref[0, h])
                dp = _dot_nt(do, vv)
                delta = jnp.sum(pr * dp, axis=-1, keepdims=True)
                dsb = (pr * (dp - delta) * ATT_SCALE).astype(BF16)
                dsink = dsink + jnp.where(lane == h, -jnp.sum(ps * delta, axis=0, keepdims=True), 0.0)
                dqs.append(_dot(dsb, kk))
                dk_h = dk_h + _dot_tn(dsb, qh)
                dv_h = dv_h + _dot_tn(pr.astype(BF16), do)
            dks.append(dk_h)
            dvs.append(dv_h)
        dq_ref[...] = jnp.concatenate(dqs, axis=1).astype(BF16)
        dk_ref[win, :] += jnp.concatenate(dks, axis=1)
        dv_ref[win, :] += jnp.concatenate(dvs, axis=1)
        ds_ref[...] += dsink

    return _launch(
        body, name="bwd_attention", grid=(nb,),
        in_specs=[_row_spec(D_MODEL, ATT_BLOCK), _row_spec(D_MODEL, ATT_BLOCK), _full_spec((T + ATT_BLOCK, KV_DIM)),
                  _full_spec((T + ATT_BLOCK, KV_DIM)), pl.BlockSpec(memory_space=pltpu.SMEM)],
        out_specs=[_row_spec(D_MODEL, ATT_BLOCK), _full_spec((T + ATT_BLOCK, KV_DIM)), _full_spec((T + ATT_BLOCK, KV_DIM)),
                   _full_spec((1, ATT_BLOCK))],
        out_shape=[jax.ShapeDtypeStruct((T, D_MODEL), BF16), jax.ShapeDtypeStruct((T + ATT_BLOCK, KV_DIM), F32),
                   jax.ShapeDtypeStruct((T + ATT_BLOCK, KV_DIM), F32), jax.ShapeDtypeStruct((1, ATT_BLOCK), F32)],
        args=(q, dattn, kpad, vpad, sinks), job=job)


def _bwd_qkv(dxres, dq, dkv, x3, h1, hk, wq, wkv, g_mix, g_kv, job=None):
    T = x3.shape[0]
    nt = T // ROW_TILE

    def body(dxr_ref, dq_ref, dkv_ref, x_ref, h1_ref, hk_ref, wq_ref, wkv_ref, gmix_ref, gkv_ref,
             dx_ref, dwq_ref, dwkv_ref, dgm_ref, dgk_ref, acc_q, acc_kv):
        i = pl.program_id(0)
        first = i == 0
        dqv = dq_ref[...]
        dkvv = dkv_ref[...]
        xv = x_ref[...]
        d1, dgm = _rms_bwd(xv, gmix_ref[...], _dot_nt(dqv, wq_ref[...]))
        d2, dgk = _rms_bwd(xv, gkv_ref[...], _dot_nt(dkvv, wkv_ref[...]))
        dx_ref[...] = dxr_ref[...] + d1 + d2
        _acc(acc_q, _dot_tn(h1_ref[...], dqv), first)
        _acc(acc_kv, _dot_tn(hk_ref[...], dkvv), first)
        _acc(dgm_ref, dgm, first)
        _acc(dgk_ref, dgk, first)

        @pl.when(i == nt - 1)
        def _():
            dwq_ref[...] = acc_q[...].astype(BF16)
            dwkv_ref[...] = acc_kv[...].astype(BF16)

    return _launch(
        body, name="bwd_qkv", grid=(nt,),
        in_specs=[_row_spec(D_MODEL), _row_spec(D_MODEL), _row_spec(2 * KV_DIM), _row_spec(D_MODEL), _row_spec(D_MODEL),
                  _row_spec(D_MODEL), _full_spec((D_MODEL, D_MODEL)), _full_spec((D_MODEL, 2 * KV_DIM)), _vec_spec(),
                  _vec_spec()],
        out_specs=[_row_spec(D_MODEL), _full_spec((D_MODEL, D_MODEL)), _full_spec((D_MODEL, 2 * KV_DIM)), _vec_spec(),
                   _vec_spec()],
        out_shape=[jax.ShapeDtypeStruct((T, D_MODEL), F32), jax.ShapeDtypeStruct((D_MODEL, D_MODEL), BF16),
                   jax.ShapeDtypeStruct((D_MODEL, 2 * KV_DIM), BF16)] + [jax.ShapeDtypeStruct((1, D_MODEL), F32)] * 2,
        scratch_shapes=[pltpu.VMEM((D_MODEL, D_MODEL), F32), pltpu.VMEM((D_MODEL, 2 * KV_DIM), F32)],
        args=(dxres, dq, dkv, x3, h1, hk, wq, wkv, g_mix, g_kv), job=job)


def _bwd_pool_mixer(dx2, dh2, x1, x, yraw, d, wp, scale, g_ffn, g_post, g_pre, job=None):
    T = x.shape[0]
    tm = ROW_TILE
    nt = T // tm

    def body(dx2_ref, dh2_ref, x1_ref, x_ref, yraw_ref, d_ref, wp_ref, sc_ref, gffn_ref, gpost_ref, gpre_ref,
             dx_ref, dwp_ref, dsc_ref, dgf_ref, dgp_ref, dgm_ref, carry, acc):
        i = pl.program_id(0)
        first = i == 0
        tile = nt - 1 - i

        @pl.when(first)
        def _():
            carry[...] = jnp.zeros_like(carry)

        dxn, dgf = _rms_bwd(x1_ref[...], gffn_ref[...], dh2_ref[...])
        dx1 = dx2_ref[...] + dxn
        yraw = yraw_ref[...]
        sc = sc_ref[...]
        dy, dgp = _rms_bwd(yraw * sc, gpost_ref[...], dx1)
        dsc = jnp.sum(dy * yraw, axis=0, keepdims=True)
        dyb = (dy * sc).astype(BF16)
        dv = d_ref[...]
        dds = []
        for g in range(N_POOL_GROUPS):
            cols = slice(g * POOL_GROUP, (g + 1) * POOL_GROUP)
            dds.append(_dot_nt(dyb[:, cols], wp_ref[g]))
            _acc(acc.at[g], _dot_tn(dv[:, cols], dyb[:, cols]), first)
        dd = jnp.concatenate(dds, axis=1)
        e = dd / _pool_counts(tile * tm, tm)
        ext = jnp.concatenate([e, carry[...]], axis=0)
        carry[...] = e[:POOL_HALO, :]
        sums = _window_sums(ext, lambda k: tm + POOL_HALO - k)[:tm, :]
        dxm, dgm = _rms_bwd(x_ref[...], gpre_ref[...], sums - dd)
        dx_ref[...] = dx1 + dxm
        _acc(dsc_ref, dsc, first)
        _acc(dgf_ref, dgf, first)
        _acc(dgp_ref, dgp, first)
        _acc(dgm_ref, dgm, first)

        @pl.when(i == nt - 1)
        def _():
            dwp_ref[...] = acc[...].astype(BF16)

    rev = pl.BlockSpec((tm, D_MODEL), lambda i: (nt - 1 - i, 0))
    return _launch(
        body, name="bwd_pool_mixer", grid=(nt,),
        in_specs=[rev] * 6 + [_full_spec((N_POOL_GROUPS, POOL_GROUP, POOL_GROUP))] + [_vec_spec()] * 4,
        out_specs=[rev, _full_spec((N_POOL_GROUPS, POOL_GROUP, POOL_GROUP))] + [_vec_spec()] * 4,
        out_shape=[jax.ShapeDtypeStruct((T, D_MODEL), F32),
                   jax.ShapeDtypeStruct((N_POOL_GROUPS, POOL_GROUP, POOL_GROUP), BF16)]
                  + [jax.ShapeDtypeStruct((1, D_MODEL), F32)] * 4,
        scratch_shapes=[pltpu.VMEM((POOL_HALO, D_MODEL), F32), pltpu.VMEM((N_POOL_GROUPS, POOL_GROUP, POOL_GROUP), F32)],
        args=(dx2, dh2, x1, x, yraw, d, wp, scale, g_ffn, g_post, g_pre), job=job)


def _my_place():
    return lax.axis_index("x"), lax.axis_index("y"), lax.axis_index("c")


def _dev_index(px, py, pc):
    return 4 * px + 2 * py + pc


def _peer_by_relation(r):
    x, y, c = _my_place()
    return (x ^ ((r >> 2) & 1), y ^ ((r >> 1) & 1), c ^ (r & 1))


def _slot_pool(ref, j):
    return ref.at[:, pl.ds(pl.multiple_of(j * 32, 32), 32), :]


def _slot_scale(ref, j):
    return ref.at[:, pl.ds(pl.multiple_of(j * 128, 128), 128)]


def _slot_rows128(ref, j):
    return ref.at[pl.ds(pl.multiple_of(j * 128, 128), 128), :]


def _slot_gu(ref, j):
    return ref.at[j % FF_CHUNKS, j // FF_CHUNKS]


def _slot_wd(ref, j):
    return ref.at[pl.ds(pl.multiple_of(j * WD_ROWS, 16), WD_ROWS), :]


def _slot_cols128(ref, j):
    return ref.at[:, pl.ds(pl.multiple_of(j * 128, 128), 128)]


_GATHERED = {
    "pool": ((N_POOL_GROUPS, POOL_GROUP, POOL_GROUP), BF16, _slot_pool),
    "scale": ((1, D_MODEL), F32, _slot_scale),
    "kv": ((D_MODEL, 2 * KV_DIM), BF16, _slot_rows128),
    "q": ((D_MODEL, D_MODEL), BF16, _slot_rows128),
    "o": ((D_MODEL, D_MODEL), BF16, _slot_rows128),
    "gu": ((FF_CHUNKS, 2, D_MODEL, FF_BLOCK), BF16, _slot_gu),
    "wd": ((D_FF, D_MODEL), BF16, _slot_wd),
    "gate": ((D_MODEL, D_MODEL), BF16, _slot_rows128),
    "proj": ((PLE_DIM, D_MODEL), BF16, _slot_cols128),
}


def _no_compute():
    pass


class _AllGather:
    def __init__(self, names, shards):
        self.kinds = [_GATHERED[n.rstrip("01")] for n in names]
        self.args = [shards[n] for n in names]
        self.out_shape = [jax.ShapeDtypeStruct(shape, dtype) for shape, dtype, _ in self.kinds]
        n = len(names)
        self.scratch = [pltpu.SemaphoreType.DMA((n, 7)), pltpu.SemaphoreType.DMA((n, 7)), pltpu.SemaphoreType.DMA((n,))]

    def _copies(self, srcs, outs, sems):
        send_sems, recv_sems, local_sems = sems
        x, y, c = _my_place()
        me, sibling = (x, y, c), (x, y, 1 - c)
        chips = [(1 - x, y), (x, 1 - y), (1 - x, 1 - y)]
        n = len(srcs)

        def slot(t, dev):
            return self.kinds[t][2](outs[t], _dev_index(*dev))

        def copy(t, k, block, to, src=None):
            return pltpu.make_async_remote_copy(
                src_ref=slot(t, block) if src is None else src, dst_ref=slot(t, block),
                send_sem=send_sems.at[t, k], recv_sem=recv_sems.at[t, k], device_id=to, device_id_type=MESH)

        mine = [pltpu.make_async_copy(srcs[t], slot(t, me), local_sems.at[t]) for t in range(n)]
        first = []
        for t in range(n):
            first.append(copy(t, 0, me, sibling, src=srcs[t]))
            first += [copy(t, 1 + j, me, (*chip, c), src=srcs[t]) for j, chip in enumerate(chips)]
        return me, sibling, chips, copy, mine, first

    def start(self, srcs, outs, sems):
        _, _, _, _, mine, first = self._copies(srcs, outs, sems)
        for cp in mine + first:
            cp.start()

    def finish(self, srcs, outs, sems):
        me, sibling, chips, copy, mine, first = self._copies(srcs, outs, sems)
        c = me[2]
        n = len(srcs)
        passed = []
        for j, chip in enumerate(chips):
            for t in range(n):
                copy(t, 1 + j, (*chip, c), me).wait_recv()
                fwd = copy(t, 4 + j, (*chip, c), sibling)
                fwd.start()
                passed.append(fwd)
        for t in range(n):
            copy(t, 0, sibling, me).wait_recv()
            for j, chip in enumerate(chips):
                copy(t, 4 + j, (*chip, 1 - c), me).wait_recv()
        for cp in first + passed:
            cp.wait_send()
        for cp in mine:
            cp.wait()


def _all_gather_only(name, names, shards):
    return _launch(_no_compute, name=name, grid=(), in_specs=[], out_specs=[], out_shape=[], args=(),
                   job=_AllGather(names, shards))[1]


def _block_pool(ref, j):
    return ref.at[:, pl.ds(pl.multiple_of(j * 32, 32), 32), :]


def _block_rows128(ref, j):
    return ref.at[pl.ds(pl.multiple_of(j * 128, 128), 128), :]


def _block_gu(ref, j):
    return ref.at[j % FF_CHUNKS, j // FF_CHUNKS]


def _block_wd(ref, j):
    return ref.at[pl.ds(pl.multiple_of(j * WD_ROWS, 16), WD_ROWS), :]


def _block_cols128(ref, j):
    return ref.at[:, pl.ds(pl.multiple_of(j * 128, 128), 128)]


_SCATTER_PIECES = (
    (_block_pool, 0, None), (_block_rows128, 1, None), (_block_rows128, 2, None), (_block_rows128, 3, None),
    (_block_gu, 4, 0), (_block_gu, 4, 1), (_block_wd, 5, 0), (_block_wd, 5, 1),
    (_block_rows128, 6, 0), (_block_rows128, 6, 1), (_block_cols128, 7, 0), (_block_cols128, 7, 1),
)


def _scatter_grads(grads):
    n_p = len(grads)
    out_shape = [
        jax.ShapeDtypeStruct((N_DEV, N_POOL_GROUPS, 32, POOL_GROUP), BF16),
        jax.ShapeDtypeStruct((N_DEV, 128, 2 * KV_DIM), BF16),
        jax.ShapeDtypeStruct((N_DEV, 128, D_MODEL), BF16),
        jax.ShapeDtypeStruct((N_DEV, 128, D_MODEL), BF16),
        jax.ShapeDtypeStruct((N_DEV, 2, D_MODEL, FF_BLOCK), BF16),
        jax.ShapeDtypeStruct((N_DEV, 2, WD_ROWS, D_MODEL), BF16),
        jax.ShapeDtypeStruct((N_DEV, 2, 128, D_MODEL), BF16),
        jax.ShapeDtypeStruct((N_DEV, 2, PLE_DIM, 128), BF16),
    ]
    n_o = len(out_shape)

    def body(*refs):
        srcs = refs[:n_p]
        outs = refs[n_p:n_p + n_o]
        send_sems, recv_sems, local_sems = refs[n_p + n_o:]
        x, y, c = _my_place()
        me = _dev_index(x, y, c)

        def landing(t, sender):
            _, o, layer = _SCATTER_PIECES[t]
            return outs[o].at[sender] if layer is None else outs[o].at[sender, layer]

        def copy(t, r):
            peer = _peer_by_relation(r)
            return pltpu.make_async_remote_copy(
                src_ref=_SCATTER_PIECES[t][0](srcs[t], _dev_index(*peer)), dst_ref=landing(t, me),
                send_sem=send_sems.at[t, r - 1], recv_sem=recv_sems.at[t, r - 1], device_id=peer, device_id_type=MESH)

        mine = [pltpu.make_async_copy(_SCATTER_PIECES[t][0](srcs[t], me), landing(t, me), local_sems.at[t])
                for t in range(n_p)]
        for cp in mine:
            cp.start()
        sends = [copy(t, r) for t in range(n_p) for r in range(1, N_DEV)]
        for cp in sends:
            cp.start()
        for cp in sends:
            cp.wait()
        for cp in mine:
            cp.wait()

    return pl.pallas_call(
        body, name="scatter_grads", out_shape=out_shape,
        in_specs=[ANY] * n_p, out_specs=[ANY] * n_o,
        scratch_shapes=[pltpu.SemaphoreType.DMA((n_p, N_DEV - 1)), pltpu.SemaphoreType.DMA((n_p, N_DEV - 1)),
                        pltpu.SemaphoreType.DMA((n_p,))],
    )(*grads)


def _adamw_math(w, g, m, v):
    m = ADAM_B1 * m + (1.0 - ADAM_B1) * g
    v = ADAM_B2 * v + (1.0 - ADAM_B2) * (g * g)
    m_hat = m / (1.0 - ADAM_B1 ** ADAM_STEP)
    v_hat = v / (1.0 - ADAM_B2 ** ADAM_STEP)
    delta = -ADAM_LR * (m_hat / (jnp.sqrt(v_hat) + ADAM_EPS) + ADAM_WD * w)
    return delta, m, v


def _adamw(name, w, m, v, landing, tr):
    R, C = w.shape

    def body(w_ref, m_ref, v_ref, l_ref, g_ref, d_ref, nm_ref, nv_ref):
        g = l_ref[0].astype(F32)
        for s in range(1, N_DEV):
            g = g + l_ref[s].astype(F32)
        g_ref[...] = g
        d_ref[...], nm_ref[...], nv_ref[...] = _adamw_math(w_ref[...], g, m_ref[...], v_ref[...])

    spec = pl.BlockSpec((tr, C), lambda i: (i, 0))
    return pl.pallas_call(
        body, name=f"adamw_{name}", grid=(R // tr,),
        in_specs=[spec, spec, spec, pl.BlockSpec((N_DEV, tr, C), lambda i: (0, i, 0))],
        out_specs=[spec] * 4, out_shape=[jax.ShapeDtypeStruct((R, C), F32)] * 4,
        compiler_params=_cparams(1, VMEM_MID),
    )(w, m, v, landing)


def _small_all_reduce_adamw(part, w, m, v):
    def body(part_ref, w_ref, m_ref, v_ref, g_ref, d_ref, nm_ref, nv_ref, buf, send_sems, recv_sems):
        x, y, c = _my_place()
        me = _dev_index(x, y, c)
        buf[me] = part_ref[...]
        copies = [pltpu.make_async_remote_copy(
            src_ref=part_ref, dst_ref=buf.at[me], send_sem=send_sems.at[r - 1], recv_sem=recv_sems.at[r - 1],
            device_id=_peer_by_relation(r), device_id_type=MESH) for r in range(1, N_DEV)]
        for cp in copies:
            cp.start()
        for cp in copies:
            cp.wait()
        g = buf[0]
        for s in range(1, N_DEV):
            g = g + buf[s]
        g_ref[...] = g
        d_ref[...], nm_ref[...], nv_ref[...] = _adamw_math(w_ref[...], g, m_ref[...], v_ref[...])

    vm = pl.BlockSpec(memory_space=pltpu.VMEM)
    return pl.pallas_call(
        body, name="small_all_reduce_adamw", out_shape=[jax.ShapeDtypeStruct((SV_ROWS, D_MODEL), F32)] * 4,
        in_specs=[vm] * 4, out_specs=[vm] * 4,
        scratch_shapes=[pltpu.VMEM((N_DEV, SV_ROWS, D_MODEL), F32), pltpu.SemaphoreType.DMA((N_DEV - 1,)),
                        pltpu.SemaphoreType.DMA((N_DEV - 1,))],
    )(part, w, m, v)


def _local_step(x, p, tgt, gains, sinks, shards):
    row = lambda a, i: a[i:i + 1]
    gather = lambda *names: _AllGather(names, shards)
    g_pre_mix, g_post_mix = gains["pre_mix_g"], gains["post_mix_g"]
    g_pre_ffn, g_post_ffn = gains["pre_ffn_g"], gains["post_ffn_g"]
    g_ple, g_ple_post, g_kv = gains["ple_g"], gains["ple_post_g"], gains["kv_g"]

    wp, scale, wgu0 = _all_gather_only("gather_first", ("pool", "scale", "gu0"), shards)
    (x1_0, h2_0, yraw, dpool), (wd0,) = _fwd_pool_mixer(
        x, row(g_pre_mix, 0), wp, scale, row(g_post_mix, 0), row(g_pre_ffn, 0), job=gather("wd0"))
    (gs0, us0, f0, x2_0, h3_0), (wgate0, wproj0, wkv, wq, wo) = _fwd_ffn(
        0, h2_0, x1_0, wgu0, wd0, row(g_post_ffn, 0), row(g_ple, 0), job=gather("gate0", "proj0", "kv", "q", "o"))
    (x3_0, z0, pe0), _ = _fwd_ple(0, x2_0, h3_0, p[0], wgate0, wproj0, row(g_ple_post, 0))
    (hk, h1, q, kv), _ = _fwd_qkv(x3_0, g_kv, row(g_pre_mix, 1), wkv, wq)
    front = ((ATT_BLOCK, 0), (0, 0))
    kpad = jnp.pad(kv[:, :KV_DIM], front)
    vpad = jnp.pad(kv[:, KV_DIM:], front)
    (attn,), (wgu1,) = _fwd_attention(q, kpad, vpad, sinks, job=gather("gu1"))
    (y1, x1_1, h2_1), (wd1,) = _fwd_attn_out(attn, x3_0, wo, row(g_post_mix, 1), row(g_pre_ffn, 1), job=gather("wd1"))
    (gs1, us1, f1, x2_1, h3_1), (wgate1, wproj1) = _fwd_ffn(
        1, h2_1, x1_1, wgu1, wd1, row(g_post_ffn, 1), row(g_ple, 1), job=gather("gate1", "proj1"))
    (dx3_1, z1, pe1, loss), _ = _fwd_ple(1, x2_1, h3_1, p[1], wgate1, wproj1, row(g_ple_post, 1), target=tgt)

    (dx2_1, df1, dwgate1, dwproj1, dg_ple_post1, dg_ple1, dg_post_ffn1), _ = _bwd_ple(
        1, dx3_1, x2_1, z1, pe1, h3_1, p[1], f1, wgate1, row(g_ple_post, 1), row(g_ple, 1), row(g_post_ffn, 1))
    (dh2_1, dgu1, dwd1), _ = _bwd_ffn(1, df1, h2_1, gs1, us1, wgu1, wd1)
    (dx1_1, dattn, dwo, dg_pre_ffn1, dg_post_mix1), _ = _bwd_attn_out(
        dx2_1, dh2_1, x1_1, y1, attn, wo, row(g_pre_ffn, 1), row(g_post_mix, 1))
    (dq, dkpad, dvpad, dsinks), _ = _bwd_attention(q, dattn, kpad, vpad, sinks)
    dkv = jnp.concatenate([dkpad[ATT_BLOCK:], dvpad[ATT_BLOCK:]], axis=1).astype(BF16)
    (dx3_0, dwq, dwkv, dg_pre_mix1, dg_kv), _ = _bwd_qkv(
        dx1_1, dq, dkv, x3_0, h1, hk, wq, wkv, row(g_pre_mix, 1), g_kv)
    (dx2_0, df0, dwgate0, dwproj0, dg_ple_post0, dg_ple0, dg_post_ffn0), _ = _bwd_ple(
        0, dx3_0, x2_0, z0, pe0, h3_0, p[0], f0, wgate0, row(g_ple_post, 0), row(g_ple, 0), row(g_post_ffn, 0))
    (dh2_0, dgu0, dwd0), _ = _bwd_ffn(0, df0, h2_0, gs0, us0, wgu0, wd0)
    (grad_x, dwp, dscale, dg_pre_ffn0, dg_post_mix0, dg_pre_mix0), _ = _bwd_pool_mixer(
        dx2_0, dh2_0, x1_0, x, yraw, dpool, wp, scale, row(g_pre_ffn, 0), row(g_post_mix, 0), row(g_pre_mix, 0))

    big = (dwp, dwkv, dwq, dwo, dgu0, dgu1, dwd0, dwd1, dwgate0, dwgate1, dwproj0, dwproj1)
    lanes = lambda a: jnp.pad(a, ((0, 0), (0, D_MODEL - a.shape[1])))
    small = jnp.concatenate([
        dg_pre_mix0, dg_pre_mix1, dg_post_mix0, dg_post_mix1, dg_pre_ffn0, dg_pre_ffn1, dg_post_ffn0, dg_post_ffn1,
        dg_ple0, dg_ple1, dg_ple_post0, dg_ple_post1, dg_kv, dscale, lanes(dsinks[:, :N_HEADS]), lanes(loss)], axis=0)
    return grad_x, big, small


def kernel(x, p, pre_mix_g, post_mix_g, pre_ffn_g, post_ffn_g, pool_w, pool_scale, kv_g, w_kv, w_q, sinks, w_o, w_gu, w_down, ple_g, w_ple_gate, w_ple_proj, ple_post_g, loss_target, m_pre_mix_g, m_post_mix_g, m_pre_ffn_g, m_post_ffn_g, m_pool_w, m_pool_scale, m_kv_g, m_w_kv, m_w_q, m_sinks, m_w_o, m_w_gu, m_w_down, m_ple_g, m_w_ple_gate, m_w_ple_proj, m_ple_post_g, v_pre_mix_g, v_post_mix_g, v_pre_ffn_g, v_post_ffn_g, v_pool_w, v_pool_scale, v_kv_g, v_w_kv, v_w_q, v_sinks, v_w_o, v_w_gu, v_w_down, v_ple_g, v_w_ple_gate, v_w_ple_proj, v_ple_post_g):
    me = _dev_index(*_my_place())

    shards = {"pool": pool_w[0].astype(BF16), "scale": pool_scale, "kv": w_kv.astype(BF16),
              "q": w_q[0].astype(BF16), "o": w_o[0].astype(BF16)}
    for layer in range(2):
        shards[f"gu{layer}"] = w_gu[layer].astype(BF16)
        shards[f"wd{layer}"] = w_down[layer].astype(BF16)
        shards[f"gate{layer}"] = w_ple_gate[layer].astype(BF16)
        shards[f"proj{layer}"] = w_ple_proj[layer].astype(BF16)
    gains = dict(pre_mix_g=pre_mix_g, post_mix_g=post_mix_g, pre_ffn_g=pre_ffn_g, post_ffn_g=post_ffn_g,
                 ple_g=ple_g, ple_post_g=ple_post_g, kv_g=kv_g[None, :])
    grad_x, big, small = _local_step(x[0], p[:, 0], loss_target[0], gains, sinks, shards)

    l_pool, l_kv, l_q, l_o, l_gu, l_wd, l_gate, l_proj = _scatter_grads(big)

    def update(name, w, m, v, landing, tr):
        R = w.size // w.shape[-1]
        outs = _adamw(name, w.reshape(R, -1), m.reshape(R, -1), v.reshape(R, -1), landing.reshape(N_DEV, R, -1), tr)
        return [o.reshape(w.shape) for o in outs]

    upd = {
        "pool_w": update("pool_w", pool_w, m_pool_w, v_pool_w, l_pool, 128),
        "w_kv": update("w_kv", w_kv, m_w_kv, v_w_kv, l_kv, 128),
        "w_q": update("w_q", w_q, m_w_q, v_w_q, l_q, 128),
        "w_o": update("w_o", w_o, m_w_o, v_w_o, l_o, 128),
        "w_gu": update("w_gu", w_gu, m_w_gu, v_w_gu, l_gu, 256),
        "w_down": update("w_down", w_down, m_w_down, v_w_down, l_wd, WD_ROWS),
        "w_ple_gate": update("w_ple_gate", w_ple_gate, m_w_ple_gate, v_w_ple_gate, l_gate, 256),
        "w_ple_proj": update("w_ple_proj", w_ple_proj, m_w_ple_proj, v_w_ple_proj, l_proj, 512),
    }

    lane0 = me * 128

    def slab(pre_mix, post_mix, pre_ffn, post_ffn, ple, ple_post, kv, scale_shard, snk):
        scale_row = lax.dynamic_update_slice(jnp.zeros((1, D_MODEL), F32), scale_shard, (0, lane0))
        snk_row = jnp.pad(snk, ((0, 0), (0, D_MODEL - N_HEADS)))
        return jnp.concatenate([pre_mix, post_mix, pre_ffn, post_ffn, ple, ple_post, kv[None, :], scale_row, snk_row,
                                jnp.zeros((1, D_MODEL), F32)], axis=0)

    sw = slab(pre_mix_g, post_mix_g, pre_ffn_g, post_ffn_g, ple_g, ple_post_g, kv_g, pool_scale, sinks)
    sm = slab(m_pre_mix_g, m_post_mix_g, m_pre_ffn_g, m_post_ffn_g, m_ple_g, m_ple_post_g, m_kv_g, m_pool_scale, m_sinks)
    sv = slab(v_pre_mix_g, v_post_mix_g, v_pre_ffn_g, v_post_ffn_g, v_ple_g, v_ple_post_g, v_kv_g, v_pool_scale, v_sinks)
    sg, sd, snm, snv = _small_all_reduce_adamw(small, sw, sm, sv)
    loss = sg[SV_LOSS, 0]

    def unslab(s):
        return {
            "pre_mix_g": s[SV_PRE_MIX:SV_PRE_MIX + 2], "post_mix_g": s[SV_POST_MIX:SV_POST_MIX + 2],
            "pre_ffn_g": s[SV_PRE_FFN:SV_PRE_FFN + 2], "post_ffn_g": s[SV_POST_FFN:SV_POST_FFN + 2],
            "ple_g": s[SV_PLE:SV_PLE + 2], "ple_post_g": s[SV_PLE_POST:SV_PLE_POST + 2], "kv_g": s[SV_KV],
            "pool_scale": lax.dynamic_slice(s, (SV_POOL_SCALE, lane0), (1, 128)),
            "sinks": s[SV_SINKS:SV_SINKS + 1, :N_HEADS],
        }

    names = ["pre_mix_g", "post_mix_g", "pre_ffn_g", "post_ffn_g", "pool_w", "pool_scale", "kv_g", "w_kv", "w_q",
             "sinks", "w_o", "w_gu", "w_down", "ple_g", "w_ple_gate", "w_ple_proj", "ple_post_g"]
    outs = [loss, grad_x[None]]
    for kind, slab_out in enumerate((sg, sd, snm, snv)):
        small_out = unslab(slab_out)
        outs += [upd[n][kind] if n in upd else small_out[n] for n in names]
    return tuple(outs)
```

```python
import functools

import jax
import jax.numpy as jnp
from jax import lax
from jax.experimental import pallas as pl
from jax.experimental.pallas import tpu as pltpu

F32 = jnp.float32
BF16 = jnp.bfloat16

N_DEV = 8
D_MODEL = 1024
N_POOL_GROUPS = 4
POOL_GROUP = 256
POOL_HALO = 16
HEAD_DIM = 64
N_HEADS = 16
N_KV_HEADS = 4
GQA_GROUP = 4
KV_DIM = N_KV_HEADS * HEAD_DIM
ATT_BLOCK = 128
D_FF = 2816
FF_CHUNKS = 4
FF_BLOCK = D_FF // FF_CHUNKS
WD_ROWS = D_FF // N_DEV
PLE_DIM = 256
EPS = 1e-6
NEG_INF = -1e30
ATT_SCALE = HEAD_DIM ** -0.5

ADAM_LR = 0.001
ADAM_B1 = 0.9
ADAM_B2 = 0.999
ADAM_EPS = 1e-08
ADAM_WD = 0.01
ADAM_STEP = 10

ROW_TILE = 256
VMEM_BIG = 56 * 1024 * 1024
VMEM_MID = 40 * 1024 * 1024

SV_ROWS = 16
SV_PRE_MIX, SV_POST_MIX, SV_PRE_FFN, SV_POST_FFN, SV_PLE, SV_PLE_POST = 0, 2, 4, 6, 8, 10
SV_KV, SV_POOL_SCALE, SV_SINKS, SV_LOSS = 12, 13, 14, 15

MESH = pl.DeviceIdType.MESH
ANY = pl.BlockSpec(memory_space=pl.ANY)


def _dot(a, b):
    return jnp.dot(a, b, preferred_element_type=F32)


def _dot_nt(a, b):
    return lax.dot_general(a, b, (((1,), (1,)), ((), ())), preferred_element_type=F32)


def _dot_tn(a, b):
    return lax.dot_general(a, b, (((0,), (0,)), ((), ())), preferred_element_type=F32)


def _rstd(x):
    return lax.rsqrt(jnp.mean(x * x, axis=-1, keepdims=True) + EPS)


def _rms(x, g):
    return x * _rstd(x) * g


def _rms_bwd(x, g, dy):
    r = _rstd(x)
    n = x * r
    dn = dy * g
    dx = r * (dn - n * jnp.mean(dn * n, axis=-1, keepdims=True))
    dg = jnp.sum(dy * n, axis=0, keepdims=True)
    return dx, dg


def _sigmoid(x):
    return 1.0 / (1.0 + jnp.exp(-x))


def _acc(ref, val, first):
    @pl.when(first)
    def _():
        ref[...] = val

    @pl.when(jnp.logical_not(first))
    def _():
        ref[...] += val


def _pool_counts(row0, rows):
    t = row0 + lax.broadcasted_iota(jnp.int32, (rows, D_MODEL), 0) + 1
    grp = lax.broadcasted_iota(jnp.int32, (rows, D_MODEL), 1) // POOL_GROUP
    win = jnp.left_shift(2, grp)
    return jnp.minimum(t, win).astype(F32)


def _window_sums(ext, shift_of):
    outs = []
    s = ext
    for gi in range(N_POOL_GROUPS):
        s = s + pltpu.roll(s, shift_of(1 << gi), axis=0)
        outs.append(s[:, :POOL_GROUP])
        s = s[:, POOL_GROUP:]
    return jnp.concatenate(outs, axis=1)


def _cparams(n_axes, vmem):
    return pltpu.CompilerParams(dimension_semantics=("arbitrary",) * n_axes, vmem_limit_bytes=vmem)


def _row_spec(cols, tm=ROW_TILE):
    return pl.BlockSpec((tm, cols), lambda i: (i, 0))


def _full_spec(shape):
    zeros = (0,) * len(shape)
    return pl.BlockSpec(shape, lambda *_: zeros)


def _vec_spec():
    return _full_spec((1, D_MODEL))


def _launch(body, *, name, grid, in_specs, out_specs, out_shape, args, scratch_shapes=(), vmem=VMEM_MID, job=None):
    n_in, n_out, n_scr = len(args), len(out_shape), len(scratch_shapes)
    j_args, j_out, j_scr = ([], [], []) if job is None else (job.args, job.out_shape, job.scratch)

    def run(*refs):
        groups, at = [], 0
        for n in (n_in, len(j_args), n_out, len(j_out), n_scr, len(j_scr)):
            groups.append(refs[at:at + n])
            at += n
        ins, j_ins, outs, j_outs, scr, j_sems = groups
        if job is None:
            body(*ins, *outs, *scr)
        elif not grid:
            job.start(j_ins, j_outs, j_sems)
            body(*ins, *outs, *scr)
            job.finish(j_ins, j_outs, j_sems)
        else:
            ids = [pl.program_id(a) for a in range(len(grid))]
            first = functools.reduce(jnp.logical_and, [i == 0 for i in ids])
            last = functools.reduce(jnp.logical_and, [i == g - 1 for i, g in zip(ids, grid)])
            pl.when(first)(lambda: job.start(j_ins, j_outs, j_sems))
            body(*ins, *outs, *scr)
            pl.when(last)(lambda: job.finish(j_ins, j_outs, j_sems))

    res = pl.pallas_call(
        run, name=name, grid=grid,
        in_specs=list(in_specs) + [ANY] * len(j_args), out_specs=list(out_specs) + [ANY] * len(j_out),
        out_shape=list(out_shape) + list(j_out), scratch_shapes=list(scratch_shapes) + list(j_scr),
        compiler_params=_cparams(len(grid), vmem),
    )(*args, *j_args)
    return res[:n_out], res[n_out:]


def _fwd_pool_mixer(x, g_pre, wp, scale, g_post, g_ffn, job=None):
    T = x.shape[0]
    tm = ROW_TILE
    nt = T // tm

    def body(x_ref, gpre_ref, wp_ref, sc_ref, gpost_ref, gffn_ref, x1_ref, h2_ref, yraw_ref, d_ref, carry):
        i = pl.program_id(0)

        @pl.when(i == 0)
        def _():
            carry[...] = jnp.zeros_like(carry)

        xv = x_ref[...]
        h = _rms(xv, gpre_ref[...])
        ext = jnp.concatenate([carry[...], h], axis=0)
        carry[...] = h[tm - POOL_HALO:, :]
        sums = _window_sums(ext, lambda k: k)[POOL_HALO:, :]
        d = sums / _pool_counts(i * tm, tm) - h
        db = d.astype(BF16)
        d_ref[...] = db
        yraw = jnp.concatenate(
            [_dot(db[:, g * POOL_GROUP:(g + 1) * POOL_GROUP], wp_ref[g]) for g in range(N_POOL_GROUPS)], axis=1)
        yraw_ref[...] = yraw
        x1 = xv + _rms(yraw * sc_ref[...], gpost_ref[...])
        x1_ref[...] = x1
        h2_ref[...] = _rms(x1, gffn_ref[...]).astype(BF16)

    return _launch(
        body, name="fwd_pool_mixer", grid=(nt,),
        in_specs=[_row_spec(D_MODEL), _vec_spec(), _full_spec((N_POOL_GROUPS, POOL_GROUP, POOL_GROUP)), _vec_spec(),
                  _vec_spec(), _vec_spec()],
        out_specs=[_row_spec(D_MODEL)] * 4,
        out_shape=[jax.ShapeDtypeStruct((T, D_MODEL), F32), jax.ShapeDtypeStruct((T, D_MODEL), BF16),
                   jax.ShapeDtypeStruct((T, D_MODEL), F32), jax.ShapeDtypeStruct((T, D_MODEL), BF16)],
        scratch_shapes=[pltpu.VMEM((POOL_HALO, D_MODEL), F32)],
        args=(x, g_pre, wp, scale, g_post, g_ffn), job=job)


def _fwd_ffn(layer, h2, x1, wgu, wd, g_post, g_ple, job=None):
    T = h2.shape[0]
    tm = ROW_TILE
    nt = T // tm
    last = FF_CHUNKS - 1

    def body(h2_ref, x1_ref, wgu_ref, wd_ref, gpost_ref, gple_ref, gs_ref, us_ref, f_ref, x2_ref, h3_ref, acc):
        k = pl.program_id(0)
        i = pl.program_id(1)
        rows = pl.ds(pl.multiple_of(i * tm, tm), tm)
        h = h2_ref[...]
        g = _dot(h, wgu_ref[0])
        u = _dot(h, wgu_ref[1])
        gs_ref[...] = g.astype(BF16)
        us_ref[...] = u.astype(BF16)
        a = (g * _sigmoid(g) * u).astype(BF16)
        part = _dot(a, wd_ref[...])

        @pl.when(k == 0)
        def _():
            acc[rows, :] = part

        @pl.when(jnp.logical_and(k > 0, k < last))
        def _():
            acc[rows, :] += part

        @pl.when(k == last)
        def _():
            f = acc[rows, :] + part
            f_ref[...] = f
            x2 = x1_ref[...] + _rms(f, gpost_ref[...])
            x2_ref[...] = x2
            h3_ref[...] = _rms(x2, gple_ref[...]).astype(BF16)

    def late(k, i):
        return (jnp.where(k == last, i, 0), 0)

    return _launch(
        body, name=f"fwd_ffn{layer}", grid=(FF_CHUNKS, nt),
        in_specs=[pl.BlockSpec((tm, D_MODEL), lambda k, i: (i, 0)),
                  pl.BlockSpec((tm, D_MODEL), late),
                  pl.BlockSpec((None, 2, D_MODEL, FF_BLOCK), lambda k, i: (k, 0, 0, 0)),
                  pl.BlockSpec((FF_BLOCK, D_MODEL), lambda k, i: (k, 0)),
                  pl.BlockSpec((1, D_MODEL), lambda k, i: (0, 0)),
                  pl.BlockSpec((1, D_MODEL), lambda k, i: (0, 0))],
        out_specs=[pl.BlockSpec((None, tm, FF_BLOCK), lambda k, i: (k, i, 0)),
                   pl.BlockSpec((None, tm, FF_BLOCK), lambda k, i: (k, i, 0)),
                   pl.BlockSpec((tm, D_MODEL), late),
                   pl.BlockSpec((tm, D_MODEL), late),
                   pl.BlockSpec((tm, D_MODEL), late)],
        out_shape=[jax.ShapeDtypeStruct((FF_CHUNKS, T, FF_BLOCK), BF16),
                   jax.ShapeDtypeStruct((FF_CHUNKS, T, FF_BLOCK), BF16),
                   jax.ShapeDtypeStruct((T, D_MODEL), F32),
                   jax.ShapeDtypeStruct((T, D_MODEL), F32),
                   jax.ShapeDtypeStruct((T, D_MODEL), BF16)],
        scratch_shapes=[pltpu.VMEM((T, D_MODEL), F32)],
        args=(h2, x1, wgu, wd, g_post, g_ple), job=job)


def _fwd_ple(layer, x2, h3, p, wgate, wproj, g_post, target=None, job=None):
    T = x2.shape[0]
    tm = ROW_TILE
    nt = T // tm
    with_loss = target is not None

    def body(*refs):
        if with_loss:
            x2_ref, h3_ref, p_ref, wg_ref, wp_ref, gpost_ref, tgt_ref, out_ref, z_ref, pe_ref, loss_ref = refs
        else:
            x2_ref, h3_ref, p_ref, wg_ref, wp_ref, gpost_ref, out_ref, z_ref, pe_ref = refs
        z = _dot(h3_ref[...], wg_ref[...])
        pe = _dot(p_ref[...].astype(BF16), wp_ref[...])
        z_ref[...] = z
        pe_ref[...] = pe
        x3 = x2_ref[...] + _rms(pe * _sigmoid(z), gpost_ref[...])
        if with_loss:
            err = x3 - tgt_ref[...]
            out_ref[...] = err * (1.0 / D_MODEL)
            part = 0.5 * jnp.sum(jnp.mean(err * err, axis=-1, keepdims=True), axis=0, keepdims=True)
            _acc(loss_ref, part, pl.program_id(0) == 0)
        else:
            out_ref[...] = x3

    in_specs = [_row_spec(D_MODEL), _row_spec(D_MODEL), _row_spec(PLE_DIM), _full_spec((D_MODEL, D_MODEL)),
                _full_spec((PLE_DIM, D_MODEL)), _vec_spec()]
    out_specs = [_row_spec(D_MODEL)] * 3
    out_shape = [jax.ShapeDtypeStruct((T, D_MODEL), F32)] * 3
    args = [x2, h3, p, wgate, wproj, g_post]
    if with_loss:
        in_specs.append(_row_spec(D_MODEL))
        out_specs.append(_full_spec((1, 1)))
        out_shape.append(jax.ShapeDtypeStruct((1, 1), F32))
        args.append(target)
    return _launch(body, name=f"fwd_ple{layer}", grid=(nt,), in_specs=in_specs, out_specs=out_specs,
                   out_shape=out_shape, args=args, job=job)


def _fwd_qkv(x3, g_kv, g_mix, wkv, wq, job=None):
    T = x3.shape[0]
    nt = T // ROW_TILE

    def body(x_ref, gkv_ref, gmix_ref, wkv_ref, wq_ref, hk_ref, h1_ref, q_ref, kv_ref):
        xv = x_ref[...]
        r = _rstd(xv)
        hk = (xv * r * gkv_ref[...]).astype(BF16)
        h1 = (xv * r * gmix_ref[...]).astype(BF16)
        hk_ref[...] = hk
        h1_ref[...] = h1
        kv_ref[...] = _dot(hk, wkv_ref[...]).astype(BF16)
        q_ref[...] = _dot(h1, wq_ref[...]).astype(BF16)

    return _launch(
        body, name="fwd_qkv", grid=(nt,),
        in_specs=[_row_spec(D_MODEL), _vec_spec(), _vec_spec(), _full_spec((D_MODEL, 2 * KV_DIM)),
                  _full_spec((D_MODEL, D_MODEL))],
        out_specs=[_row_spec(D_MODEL), _row_spec(D_MODEL), _row_spec(D_MODEL), _row_spec(2 * KV_DIM)],
        out_shape=[jax.ShapeDtypeStruct((T, D_MODEL), BF16)] * 3 + [jax.ShapeDtypeStruct((T, 2 * KV_DIM), BF16)],
        args=(x3, g_kv, g_mix, wkv, wq), job=job)


def _alibi_slope(h):
    return 2.0 ** (-8.0 * (h + 1) / N_HEADS)


def _att_mask(n):
    qi = lax.broadcasted_iota(jnp.int32, (ATT_BLOCK, 2 * ATT_BLOCK), 0)
    si = lax.broadcasted_iota(jnp.int32, (ATT_BLOCK, 2 * ATT_BLOCK), 1)
    rel = ATT_BLOCK + qi - si
    valid = (rel >= 0) & (rel < ATT_BLOCK) & ((si >= ATT_BLOCK) | (n > 0))
    return rel.astype(F32), valid


def _att_probs(qh, kk, relf, valid, slope, sink):
    s = _dot_nt(qh, kk) * ATT_SCALE
    s = jnp.where(valid, s - slope * relf, NEG_INF)
    m = jnp.maximum(jnp.max(s, axis=-1, keepdims=True), sink)
    e = jnp.exp(s - m)
    es = jnp.exp(sink - m)
    inv = 1.0 / (jnp.sum(e, axis=-1, keepdims=True) + es)
    return e * inv, es * inv


def _fwd_attention(q, kpad, vpad, sinks, job=None):
    T = q.shape[0]
    nb = T // ATT_BLOCK

    def body(q_ref, k_ref, v_ref, sink_ref, o_ref):
        n = pl.program_id(0)
        start = pl.multiple_of(n * ATT_BLOCK, ATT_BLOCK)
        kw = k_ref[pl.ds(start, 2 * ATT_BLOCK), :]
        vw = v_ref[pl.ds(start, 2 * ATT_BLOCK), :]
        relf, valid = _att_mask(n)
        outs = []
        for h in range(N_HEADS):
            kh = h // GQA_GROUP
            qh = q_ref[:, h * HEAD_DIM:(h + 1) * HEAD_DIM]
            kk = kw[:, kh * HEAD_DIM:(kh + 1) * HEAD_DIM]
            vv = vw[:, kh * HEAD_DIM:(kh + 1) * HEAD_DIM]
            pr, _ = _att_probs(qh, kk, relf, valid, _alibi_slope(h), sink_ref[0, h])
            outs.append(_dot(pr.astype(BF16), vv))
        o_ref[...] = jnp.concatenate(outs, axis=1).astype(BF16)

    return _launch(
        body, name="fwd_attention", grid=(nb,),
        in_specs=[_row_spec(D_MODEL, ATT_BLOCK), _full_spec((T + ATT_BLOCK, KV_DIM)), _full_spec((T + ATT_BLOCK, KV_DIM)),
                  pl.BlockSpec(memory_space=pltpu.SMEM)],
        out_specs=[_row_spec(D_MODEL, ATT_BLOCK)],
        out_shape=[jax.ShapeDtypeStruct((T, D_MODEL), BF16)],
        args=(q, kpad, vpad, sinks), job=job)


def _fwd_attn_out(attn, x, wo, g_post, g_ffn, job=None):
    T = x.shape[0]
    nt = T // ROW_TILE

    def body(a_ref, x_ref, wo_ref, gpost_ref, gffn_ref, y_ref, x1_ref, h2_ref):
        y = _dot(a_ref[...], wo_ref[...])
        y_ref[...] = y
        x1 = x_ref[...] + _rms(y, gpost_ref[...])
        x1_ref[...] = x1
        h2_ref[...] = _rms(x1, gffn_ref[...]).astype(BF16)

    return _launch(
        body, name="fwd_attn_out", grid=(nt,),
        in_specs=[_row_spec(D_MODEL), _row_spec(D_MODEL), _full_spec((D_MODEL, D_MODEL)), _vec_spec(), _vec_spec()],
        out_specs=[_row_spec(D_MODEL)] * 3,
        out_shape=[jax.ShapeDtypeStruct((T, D_MODEL), F32), jax.ShapeDtypeStruct((T, D_MODEL), F32),
                   jax.ShapeDtypeStruct((T, D_MODEL), BF16)],
        args=(attn, x, wo, g_post, g_ffn), job=job)


def _bwd_ple(layer, dx3, x2, z, pe, h3, p, f, wgate, g_ple_post, g_ple, g_post_ffn, job=None):
    T = x2.shape[0]
    tm = ROW_TILE
    nt = T // tm

    def body(dx3_ref, x2_ref, z_ref, pe_ref, h3_ref, p_ref, f_ref, wg_ref, gpp_ref, gp_ref, gpf_ref,
             dx2_ref, df_ref, dwg_ref, dwp_ref, dgpp_ref, dgp_ref, dgpf_ref, acc_g, acc_p):
        i = pl.program_id(0)
        first = i == 0
        dx3v = dx3_ref[...]
        gate = _sigmoid(z_ref[...])
        pev = pe_ref[...]
        de, dgpp = _rms_bwd(pev * gate, gpp_ref[...], dx3v)
        dpe = (de * gate).astype(BF16)
        dz = (de * pev * gate * (1.0 - gate)).astype(BF16)
        _acc(acc_p, _dot_tn(p_ref[...].astype(BF16), dpe), first)
        _acc(acc_g, _dot_tn(h3_ref[...], dz), first)
        dh3 = _dot_nt(dz, wg_ref[...])
        dxn, dgp = _rms_bwd(x2_ref[...], gp_ref[...], dh3)
        dx2 = dx3v + dxn
        dx2_ref[...] = dx2
        df, dgpf = _rms_bwd(f_ref[...], gpf_ref[...], dx2)
        df_ref[...] = df.astype(BF16)
        _acc(dgpp_ref, dgpp, first)
        _acc(dgp_ref, dgp, first)
        _acc(dgpf_ref, dgpf, first)

        @pl.when(i == nt - 1)
        def _():
            dwg_ref[...] = acc_g[...].astype(BF16)
            dwp_ref[...] = acc_p[...].astype(BF16)

    return _launch(
        body, name=f"bwd_ple{layer}", grid=(nt,),
        in_specs=[_row_spec(D_MODEL)] * 5 + [_row_spec(PLE_DIM), _row_spec(D_MODEL), _full_spec((D_MODEL, D_MODEL)),
                  _vec_spec(), _vec_spec(), _vec_spec()],
        out_specs=[_row_spec(D_MODEL), _row_spec(D_MODEL), _full_spec((D_MODEL, D_MODEL)), _full_spec((PLE_DIM, D_MODEL)),
                   _vec_spec(), _vec_spec(), _vec_spec()],
        out_shape=[jax.ShapeDtypeStruct((T, D_MODEL), F32), jax.ShapeDtypeStruct((T, D_MODEL), BF16),
                   jax.ShapeDtypeStruct((D_MODEL, D_MODEL), BF16), jax.ShapeDtypeStruct((PLE_DIM, D_MODEL), BF16)]
                  + [jax.ShapeDtypeStruct((1, D_MODEL), F32)] * 3,
        scratch_shapes=[pltpu.VMEM((D_MODEL, D_MODEL), F32), pltpu.VMEM((PLE_DIM, D_MODEL), F32)],
        args=(dx3, x2, z, pe, h3, p, f, wgate, g_ple_post, g_ple, g_post_ffn), job=job)


def _bwd_ffn(layer, df, h2, gs, us, wgu, wd, job=None):
    T = h2.shape[0]
    tm = ROW_TILE
    nt = T // tm
    last = FF_CHUNKS - 1

    def body(df_ref, h2_ref, gs_ref, us_ref, wgu_ref, wd_ref, dh_ref, dgu_ref, dwd_ref, acc_h, acc_g, acc_u, acc_d):
        k = pl.program_id(0)
        i = pl.program_id(1)
        first = i == 0
        rows = pl.ds(pl.multiple_of(i * tm, tm), tm)
        g = gs_ref[...].astype(F32)
        u = us_ref[...].astype(F32)
        sg = _sigmoid(g)
        silu = g * sg
        a = (silu * u).astype(BF16)
        dfv = df_ref[...]
        h = h2_ref[...]
        da = _dot_nt(dfv, wd_ref[...])
        dg = (da * u * (sg * (1.0 + g * (1.0 - sg)))).astype(BF16)
        du = (da * silu).astype(BF16)
        _acc(acc_d, _dot_tn(a, dfv), first)
        _acc(acc_g, _dot_tn(h, dg), first)
        _acc(acc_u, _dot_tn(h, du), first)
        dh = _dot_nt(dg, wgu_ref[0]) + _dot_nt(du, wgu_ref[1])

        @pl.when(k == 0)
        def _():
            acc_h[rows, :] = dh

        @pl.when(jnp.logical_and(k > 0, k < last))
        def _():
            acc_h[rows, :] += dh

        @pl.when(k == last)
        def _():
            dh_ref[...] = acc_h[rows, :] + dh

        @pl.when(i == nt - 1)
        def _():
            dgu_ref[0] = acc_g[...].astype(BF16)
            dgu_ref[1] = acc_u[...].astype(BF16)
            dwd_ref[...] = acc_d[...].astype(BF16)

    return _launch(
        body, name=f"bwd_ffn{layer}", grid=(FF_CHUNKS, nt),
        in_specs=[pl.BlockSpec((tm, D_MODEL), lambda k, i: (i, 0)),
                  pl.BlockSpec((tm, D_MODEL), lambda k, i: (i, 0)),
                  pl.BlockSpec((None, tm, FF_BLOCK), lambda k, i: (k, i, 0)),
                  pl.BlockSpec((None, tm, FF_BLOCK), lambda k, i: (k, i, 0)),
                  pl.BlockSpec((None, 2, D_MODEL, FF_BLOCK), lambda k, i: (k, 0, 0, 0)),
                  pl.BlockSpec((FF_BLOCK, D_MODEL), lambda k, i: (k, 0))],
        out_specs=[pl.BlockSpec((tm, D_MODEL), lambda k, i: (jnp.where(k == last, i, 0), 0)),
                   pl.BlockSpec((None, 2, D_MODEL, FF_BLOCK), lambda k, i: (k, 0, 0, 0)),
                   pl.BlockSpec((FF_BLOCK, D_MODEL), lambda k, i: (k, 0))],
        out_shape=[jax.ShapeDtypeStruct((T, D_MODEL), F32),
                   jax.ShapeDtypeStruct((FF_CHUNKS, 2, D_MODEL, FF_BLOCK), BF16),
                   jax.ShapeDtypeStruct((D_FF, D_MODEL), BF16)],
        scratch_shapes=[pltpu.VMEM((T, D_MODEL), F32), pltpu.VMEM((D_MODEL, FF_BLOCK), F32),
                        pltpu.VMEM((D_MODEL, FF_BLOCK), F32), pltpu.VMEM((FF_BLOCK, D_MODEL), F32)],
        vmem=VMEM_BIG, args=(df, h2, gs, us, wgu, wd), job=job)


def _bwd_attn_out(dx2, dh2, x1, y, attn, wo, g_ffn, g_post, job=None):
    T = x1.shape[0]
    nt = T // ROW_TILE

    def body(dx2_ref, dh2_ref, x1_ref, y_ref, a_ref, wo_ref, gffn_ref, gpost_ref,
             dx1_ref, da_ref, dwo_ref, dgf_ref, dgp_ref, acc):
        i = pl.program_id(0)
        first = i == 0
        dxn, dgf = _rms_bwd(x1_ref[...], gffn_ref[...], dh2_ref[...])
        dx1 = dx2_ref[...] + dxn
        dx1_ref[...] = dx1
        dy, dgp = _rms_bwd(y_ref[...], gpost_ref[...], dx1)
        dyb = dy.astype(BF16)
        da_ref[...] = _dot_nt(dyb, wo_ref[...]).astype(BF16)
        _acc(acc, _dot_tn(a_ref[...], dyb), first)
        _acc(dgf_ref, dgf, first)
        _acc(dgp_ref, dgp, first)

        @pl.when(i == nt - 1)
        def _():
            dwo_ref[...] = acc[...].astype(BF16)

    return _launch(
        body, name="bwd_attn_out", grid=(nt,),
        in_specs=[_row_spec(D_MODEL)] * 5 + [_full_spec((D_MODEL, D_MODEL)), _vec_spec(), _vec_spec()],
        out_specs=[_row_spec(D_MODEL), _row_spec(D_MODEL), _full_spec((D_MODEL, D_MODEL)), _vec_spec(), _vec_spec()],
        out_shape=[jax.ShapeDtypeStruct((T, D_MODEL), F32), jax.ShapeDtypeStruct((T, D_MODEL), BF16),
                   jax.ShapeDtypeStruct((D_MODEL, D_MODEL), BF16)] + [jax.ShapeDtypeStruct((1, D_MODEL), F32)] * 2,
        scratch_shapes=[pltpu.VMEM((D_MODEL, D_MODEL), F32)],
        args=(dx2, dh2, x1, y, attn, wo, g_ffn, g_post), job=job)


def _bwd_attention(q, dattn, kpad, vpad, sinks, job=None):
    T = q.shape[0]
    nb = T // ATT_BLOCK

    def body(q_ref, do_ref, k_ref, v_ref, sink_ref, dq_ref, dk_ref, dv_ref, ds_ref):
        n = pl.program_id(0)

        @pl.when(n == 0)
        def _():
            dk_ref[...] = jnp.zeros_like(dk_ref)
            dv_ref[...] = jnp.zeros_like(dv_ref)
            ds_ref[...] = jnp.zeros_like(ds_ref)

        start = pl.multiple_of(n * ATT_BLOCK, ATT_BLOCK)
        win = pl.ds(start, 2 * ATT_BLOCK)
        kw = k_ref[win, :]
        vw = v_ref[win, :]
        relf, valid = _att_mask(n)
        lane = lax.broadcasted_iota(jnp.int32, (1, ATT_BLOCK), 1)
        dsink = jnp.zeros((1, ATT_BLOCK), F32)
        dqs, dks, dvs = [], [], []
        for kh in range(N_KV_HEADS):
            kk = kw[:, kh * HEAD_DIM:(kh + 1) * HEAD_DIM]
            vv = vw[:, kh * HEAD_DIM:(kh + 1) * HEAD_DIM]
            dk_h = jnp.zeros((2 * ATT_BLOCK, HEAD_DIM), F32)
            dv_h = jnp.zeros((2 * ATT_BLOCK, HEAD_DIM), F32)
            for gq in range(GQA_GROUP):
                h = kh * GQA_GROUP + gq
                qh = q_ref[:, h * HEAD_DIM:(h + 1) * HEAD_DIM]
                do = do_ref[:, h * HEAD_DIM:(h + 1) * HEAD_DIM]
                pr, ps = _att_probs(qh, kk, relf, valid, _alibi_slope(h), sink_ref[0, h])
                dp = _dot_nt(do, vv)
                delta = jnp.sum(pr * dp, axis=-1, keepdims=True)
                dsb = (pr * (dp - delta) * ATT_SCALE).astype(BF16)
                dsink = dsink + jnp.where(lane == h, -jnp.sum(ps * delta, axis=0, keepdims=True), 0.0)
                dqs.append(_dot(dsb, kk))
                dk_h = dk_h + _dot_tn(dsb, qh)
                dv_h = dv_h + _dot_tn(pr.astype(BF16), do)
            dks.append(dk_h)
            dvs.append(dv_h)
        dq_ref[...] = jnp.concatenate(dqs, axis=1).astype(BF16)
        dk_ref[win, :] += jnp.concatenate(dks, axis=1)
        dv_ref[win, :] += jnp.concatenate(dvs, axis=1)
        ds_ref[...] += dsink

    return _launch(
        body, name="bwd_attention", grid=(nb,),
        in_specs=[_row_spec(D_MODEL, ATT_BLOCK), _row_spec(D_MODEL, ATT_BLOCK), _full_spec((T + ATT_BLOCK, KV_DIM)),
                  _full_spec((T + ATT_BLOCK, KV_DIM)), pl.BlockSpec(memory_space=pltpu.SMEM)],
        out_specs=[_row_spec(D_MODEL, ATT_BLOCK), _full_spec((T + ATT_BLOCK, KV_DIM)), _full_spec((T + ATT_BLOCK, KV_DIM)),
                   _full_spec((1, ATT_BLOCK))],
        out_shape=[jax.ShapeDtypeStruct((T, D_MODEL), BF16), jax.ShapeDtypeStruct((T + ATT_BLOCK, KV_DIM), F32),
                   jax.ShapeDtypeStruct((T + ATT_BLOCK, KV_DIM), F32), jax.ShapeDtypeStruct((1, ATT_BLOCK), F32)],
        args=(q, dattn, kpad, vpad, sinks), job=job)


def _bwd_qkv(dxres, dq, dkv, x3, h1, hk, wq, wkv, g_mix, g_kv, job=None):
    T = x3.shape[0]
    nt = T // ROW_TILE

    def body(dxr_ref, dq_ref, dkv_ref, x_ref, h1_ref, hk_ref, wq_ref, wkv_ref, gmix_ref, gkv_ref,
             dx_ref, dwq_ref, dwkv_ref, dgm_ref, dgk_ref, acc_q, acc_kv):
        i = pl.program_id(0)
        first = i == 0
        dqv = dq_ref[...]
        dkvv = dkv_ref[...]
        xv = x_ref[...]
        d1, dgm = _rms_bwd(xv, gmix_ref[...], _dot_nt(dqv, wq_ref[...]))
        d2, dgk = _rms_bwd(xv, gkv_ref[...], _dot_nt(dkvv, wkv_ref[...]))
        dx_ref[...] = dxr_ref[...] + d1 + d2
        _acc(acc_q, _dot_tn(h1_ref[...], dqv), first)
        _acc(acc_kv, _dot_tn(hk_ref[...], dkvv), first)
        _acc(dgm_ref, dgm, first)
        _acc(dgk_ref, dgk, first)

        @pl.when(i == nt - 1)
        def _():
            dwq_ref[...] = acc_q[...].astype(BF16)
            dwkv_ref[...] = acc_kv[...].astype(BF16)

    return _launch(
        body, name="bwd_qkv", grid=(nt,),
        in_specs=[_row_spec(D_MODEL), _row_spec(D_MODEL), _row_spec(2 * KV_DIM), _row_spec(D_MODEL), _row_spec(D_MODEL),
                  _row_spec(D_MODEL), _full_spec((D_MODEL, D_MODEL)), _full_spec((D_MODEL, 2 * KV_DIM)), _vec_spec(),
                  _vec_spec()],
        out_specs=[_row_spec(D_MODEL), _full_spec((D_MODEL, D_MODEL)), _full_spec((D_MODEL, 2 * KV_DIM)), _vec_spec(),
                   _vec_spec()],
        out_shape=[jax.ShapeDtypeStruct((T, D_MODEL), F32), jax.ShapeDtypeStruct((D_MODEL, D_MODEL), BF16),
                   jax.ShapeDtypeStruct((D_MODEL, 2 * KV_DIM), BF16)] + [jax.ShapeDtypeStruct((1, D_MODEL), F32)] * 2,
        scratch_shapes=[pltpu.VMEM((D_MODEL, D_MODEL), F32), pltpu.VMEM((D_MODEL, 2 * KV_DIM), F32)],
        args=(dxres, dq, dkv, x3, h1, hk, wq, wkv, g_mix, g_kv), job=job)


def _bwd_pool_mixer(dx2, dh2, x1, x, yraw, d, wp, scale, g_ffn, g_post, g_pre, job=None):
    T = x.shape[0]
    tm = ROW_TILE
    nt = T // tm

    def body(dx2_ref, dh2_ref, x1_ref, x_ref, yraw_ref, d_ref, wp_ref, sc_ref, gffn_ref, gpost_ref, gpre_ref,
             dx_ref, dwp_ref, dsc_ref, dgf_ref, dgp_ref, dgm_ref, carry, acc):
        i = pl.program_id(0)
        first = i == 0
        tile = nt - 1 - i

        @pl.when(first)
        def _():
            carry[...] = jnp.zeros_like(carry)

        dxn, dgf = _rms_bwd(x1_ref[...], gffn_ref[...], dh2_ref[...])
        dx1 = dx2_ref[...] + dxn
        yraw = yraw_ref[...]
        sc = sc_ref[...]
        dy, dgp = _rms_bwd(yraw * sc, gpost_ref[...], dx1)
        dsc = jnp.sum(dy * yraw, axis=0, keepdims=True)
        dyb = (dy * sc).astype(BF16)
        dv = d_ref[...]
        dds = []
        for g in range(N_POOL_GROUPS):
            cols = slice(g * POOL_GROUP, (g + 1) * POOL_GROUP)
            dds.append(_dot_nt(dyb[:, cols], wp_ref[g]))
            _acc(acc.at[g], _dot_tn(dv[:, cols], dyb[:, cols]), first)
        dd = jnp.concatenate(dds, axis=1)
        e = dd / _pool_counts(tile * tm, tm)
        ext = jnp.concatenate([e, carry[...]], axis=0)
        carry[...] = e[:POOL_HALO, :]
        sums = _window_sums(ext, lambda k: tm + POOL_HALO - k)[:tm, :]
        dxm, dgm = _rms_bwd(x_ref[...], gpre_ref[...], sums - dd)
        dx_ref[...] = dx1 + dxm
        _acc(dsc_ref, dsc, first)
        _acc(dgf_ref, dgf, first)
        _acc(dgp_ref, dgp, first)
        _acc(dgm_ref, dgm, first)

        @pl.when(i == nt - 1)
        def _():
            dwp_ref[...] = acc[...].astype(BF16)

    rev = pl.BlockSpec((tm, D_MODEL), lambda i: (nt - 1 - i, 0))
    return _launch(
        body, name="bwd_pool_mixer", grid=(nt,),
        in_specs=[rev] * 6 + [_full_spec((N_POOL_GROUPS, POOL_GROUP, POOL_GROUP))] + [_vec_spec()] * 4,
        out_specs=[rev, _full_spec((N_POOL_GROUPS, POOL_GROUP, POOL_GROUP))] + [_vec_spec()] * 4,
        out_shape=[jax.ShapeDtypeStruct((T, D_MODEL), F32),
                   jax.ShapeDtypeStruct((N_POOL_GROUPS, POOL_GROUP, POOL_GROUP), BF16)]
                  + [jax.ShapeDtypeStruct((1, D_MODEL), F32)] * 4,
        scratch_shapes=[pltpu.VMEM((POOL_HALO, D_MODEL), F32), pltpu.VMEM((N_POOL_GROUPS, POOL_GROUP, POOL_GROUP), F32)],
        args=(dx2, dh2, x1, x, yraw, d, wp, scale, g_ffn, g_post, g_pre), job=job)


def _my_place():
    return lax.axis_index("x"), lax.axis_index("y"), lax.axis_index("c")


def _dev_index(px, py, pc):
    return 4 * px + 2 * py + pc


def _peer_by_relation(r):
    x, y, c = _my_place()
    return (x ^ ((r >> 2) & 1), y ^ ((r >> 1) & 1), c ^ (r & 1))


def _slot_pool(ref, j):
    return ref.at[:, pl.ds(pl.multiple_of(j * 32, 32), 32), :]


def _slot_scale(ref, j):
    return ref.at[:, pl.ds(pl.multiple_of(j * 128, 128), 128)]


def _slot_rows128(ref, j):
    return ref.at[pl.ds(pl.multiple_of(j * 128, 128), 128), :]


def _slot_gu(ref, j):
    return ref.at[j % FF_CHUNKS, j // FF_CHUNKS]


def _slot_wd(ref, j):
    return ref.at[pl.ds(pl.multiple_of(j * WD_ROWS, 16), WD_ROWS), :]


def _slot_cols128(ref, j):
    return ref.at[:, pl.ds(pl.multiple_of(j * 128, 128), 128)]


_GATHERED = {
    "pool": ((N_POOL_GROUPS, POOL_GROUP, POOL_GROUP), BF16, _slot_pool),
    "scale": ((1, D_MODEL), F32, _slot_scale),
    "kv": ((D_MODEL, 2 * KV_DIM), BF16, _slot_rows128),
    "q": ((D_MODEL, D_MODEL), BF16, _slot_rows128),
    "o": ((D_MODEL, D_MODEL), BF16, _slot_rows128),
    "gu": ((FF_CHUNKS, 2, D_MODEL, FF_BLOCK), BF16, _slot_gu),
    "wd": ((D_FF, D_MODEL), BF16, _slot_wd),
    "gate": ((D_MODEL, D_MODEL), BF16, _slot_rows128),
    "proj": ((PLE_DIM, D_MODEL), BF16, _slot_cols128),
}


def _no_compute():
    pass


class _AllGather:
    def __init__(self, names, shards):
        self.kinds = [_GATHERED[n.rstrip("01")] for n in names]
        self.args = [shards[n] for n in names]
        self.out_shape = [jax.ShapeDtypeStruct(shape, dtype) for shape, dtype, _ in self.kinds]
        n = len(names)
        self.scratch = [pltpu.SemaphoreType.DMA((n, 7)), pltpu.SemaphoreType.DMA((n, 7)), pltpu.SemaphoreType.DMA((n,))]

    def _copies(self, srcs, outs, sems):
        send_sems, recv_sems, local_sems = sems
        x, y, c = _my_place()
        me, sibling = (x, y, c), (x, y, 1 - c)
        chips = [(1 - x, y), (x, 1 - y), (1 - x, 1 - y)]
        n = len(srcs)

        def slot(t, dev):
            return self.kinds[t][2](outs[t], _dev_index(*dev))

        def copy(t, k, block, to, src=None):
            return pltpu.make_async_remote_copy(
                src_ref=slot(t, block) if src is None else src, dst_ref=slot(t, block),
                send_sem=send_sems.at[t, k], recv_sem=recv_sems.at[t, k], device_id=to, device_id_type=MESH)

        mine = [pltpu.make_async_copy(srcs[t], slot(t, me), local_sems.at[t]) for t in range(n)]
        first = []
        for t in range(n):
            first.append(copy(t, 0, me, sibling, src=srcs[t]))
            first += [copy(t, 1 + j, me, (*chip, c), src=srcs[t]) for j, chip in enumerate(chips)]
        return me, sibling, chips, copy, mine, first

    def start(self, srcs, outs, sems):
        _, _, _, _, mine, first = self._copies(srcs, outs, sems)
        for cp in mine + first:
            cp.start()

    def finish(self, srcs, outs, sems):
        me, sibling, chips, copy, mine, first = self._copies(srcs, outs, sems)
        c = me[2]
        n = len(srcs)
        passed = []
        for j, chip in enumerate(chips):
            for t in range(n):
                copy(t, 1 + j, (*chip, c), me).wait_recv()
                fwd = copy(t, 4 + j, (*chip, c), sibling)
                fwd.start()
                passed.append(fwd)
        for t in range(n):
            copy(t, 0, sibling, me).wait_recv()
            for j, chip in enumerate(chips):
                copy(t, 4 + j, (*chip, 1 - c), me).wait_recv()
        for cp in first + passed:
            cp.wait_send()
        for cp in mine:
            cp.wait()


def _all_gather_only(name, names, shards):
    return _launch(_no_compute, name=name, grid=(), in_specs=[], out_specs=[], out_shape=[], args=(),
                   job=_AllGather(names, shards))[1]


def _block_pool(ref, j):
    return ref.at[:, pl.ds(pl.multiple_of(j * 32, 32), 32), :]


def _block_rows128(ref, j):
    return ref.at[pl.ds(pl.multiple_of(j * 128, 128), 128), :]


def _block_gu(ref, j):
    return ref.at[j % FF_CHUNKS, j // FF_CHUNKS]


def _block_wd(ref, j):
    return ref.at[pl.ds(pl.multiple_of(j * WD_ROWS, 16), WD_ROWS), :]


def _block_cols128(ref, j):
    return ref.at[:, pl.ds(pl.multiple_of(j * 128, 128), 128)]


_SCATTERED = {
    "pool": ((N_POOL_GROUPS, 32, POOL_GROUP), _block_pool),
    "kv": ((128, 2 * KV_DIM), _block_rows128),
    "q": ((128, D_MODEL), _block_rows128),
    "o": ((128, D_MODEL), _block_rows128),
    "gu": ((D_MODEL, FF_BLOCK), _block_gu),
    "wd": ((WD_ROWS, D_MODEL), _block_wd),
    "gate": ((128, D_MODEL), _block_rows128),
    "proj": ((PLE_DIM, 128), _block_cols128),
}


class _Scatter:
    def __init__(self, grads):
        names = list(grads)
        self.kinds = [_SCATTERED[n.rstrip("01")] for n in names]
        self.args = [grads[n] for n in names]
        self.out_shape = [jax.ShapeDtypeStruct((N_DEV, *block), BF16) for block, _ in self.kinds]
        n = len(names)
        self.scratch = [pltpu.SemaphoreType.DMA((n, N_DEV - 1)), pltpu.SemaphoreType.DMA((n, N_DEV - 1)),
                        pltpu.SemaphoreType.DMA((n,))]

    def _copies(self, srcs, outs, sems):
        send_sems, recv_sems, local_sems = sems
        me = _dev_index(*_my_place())
        mine, sends = [], []
        for t, (_, block) in enumerate(self.kinds):
            mine.append(pltpu.make_async_copy(block(srcs[t], me), outs[t].at[me], local_sems.at[t]))
            for r in range(1, N_DEV):
                peer = _peer_by_relation(r)
                sends.append(pltpu.make_async_remote_copy(
                    src_ref=block(srcs[t], _dev_index(*peer)), dst_ref=outs[t].at[me],
                    send_sem=send_sems.at[t, r - 1], recv_sem=recv_sems.at[t, r - 1], device_id=peer,
                    device_id_type=MESH))
        return mine + sends

    def start(self, srcs, outs, sems):
        for cp in self._copies(srcs, outs, sems):
            cp.start()

    def finish(self, srcs, outs, sems):
        for cp in self._copies(srcs, outs, sems):
            cp.wait()


def _scatter_only(name, grads):
    return _launch(_no_compute, name=name, grid=(), in_specs=[], out_specs=[], out_shape=[], args=(),
                   job=_Scatter(grads))[1]


def _adamw_math(w, g, m, v):
    m = ADAM_B1 * m + (1.0 - ADAM_B1) * g
    v = ADAM_B2 * v + (1.0 - ADAM_B2) * (g * g)
    m_hat = m / (1.0 - ADAM_B1 ** ADAM_STEP)
    v_hat = v / (1.0 - ADAM_B2 ** ADAM_STEP)
    delta = -ADAM_LR * (m_hat / (jnp.sqrt(v_hat) + ADAM_EPS) + ADAM_WD * w)
    return delta, m, v


def _adamw(name, w, m, v, landings, tr):
    n_layers = len(landings)
    C = w.shape[1]
    R = w.shape[0] // n_layers
    nr = R // tr

    def body(w_ref, m_ref, v_ref, *rest):
        l_refs, (g_ref, d_ref, nm_ref, nv_ref) = rest[:n_layers], rest[n_layers:]
        for layer, l_ref in enumerate(l_refs):
            @pl.when(pl.program_id(0) == layer)
            def _(l_ref=l_ref):
                g = l_ref[0].astype(F32)
                for s in range(1, N_DEV):
                    g = g + l_ref[s].astype(F32)
                g_ref[...] = g
                d_ref[...], nm_ref[...], nv_ref[...] = _adamw_math(w_ref[...], g, m_ref[...], v_ref[...])

    spec = pl.BlockSpec((tr, C), lambda l, i: (l * nr + i, 0))
    l_specs = [pl.BlockSpec((N_DEV, tr, C), lambda l, i, layer=layer: (0, jnp.where(l == layer, i, 0), 0))
               for layer in range(n_layers)]
    return pl.pallas_call(
        body, name=f"adamw_{name}", grid=(n_layers, nr),
        in_specs=[spec, spec, spec] + l_specs,
        out_specs=[spec] * 4, out_shape=[jax.ShapeDtypeStruct(w.shape, F32)] * 4,
        compiler_params=_cparams(2, VMEM_MID),
    )(w, m, v, *landings)


def _small_all_reduce_adamw(part, w, m, v):
    def body(part_ref, w_ref, m_ref, v_ref, g_ref, d_ref, nm_ref, nv_ref, buf, send_sems, recv_sems):
        x, y, c = _my_place()
        me = _dev_index(x, y, c)
        buf[me] = part_ref[...]
        copies = [pltpu.make_async_remote_copy(
            src_ref=part_ref, dst_ref=buf.at[me], send_sem=send_sems.at[r - 1], recv_sem=recv_sems.at[r - 1],
            device_id=_peer_by_relation(r), device_id_type=MESH) for r in range(1, N_DEV)]
        for cp in copies:
            cp.start()
        for cp in copies:
            cp.wait()
        g = buf[0]
        for s in range(1, N_DEV):
            g = g + buf[s]
        g_ref[...] = g
        d_ref[...], nm_ref[...], nv_ref[...] = _adamw_math(w_ref[...], g, m_ref[...], v_ref[...])

    vm = pl.BlockSpec(memory_space=pltpu.VMEM)
    return pl.pallas_call(
        body, name="small_all_reduce_adamw", out_shape=[jax.ShapeDtypeStruct((SV_ROWS, D_MODEL), F32)] * 4,
        in_specs=[vm] * 4, out_specs=[vm] * 4,
        scratch_shapes=[pltpu.VMEM((N_DEV, SV_ROWS, D_MODEL), F32), pltpu.SemaphoreType.DMA((N_DEV - 1,)),
                        pltpu.SemaphoreType.DMA((N_DEV - 1,))],
    )(part, w, m, v)


def _local_step(x, p, tgt, gains, sinks, shards):
    row = lambda a, i: a[i:i + 1]
    gather = lambda *names: _AllGather(names, shards)
    g_pre_mix, g_post_mix = gains["pre_mix_g"], gains["post_mix_g"]
    g_pre_ffn, g_post_ffn = gains["pre_ffn_g"], gains["post_ffn_g"]
    g_ple, g_ple_post, g_kv = gains["ple_g"], gains["ple_post_g"], gains["kv_g"]

    wp, scale, wgu0 = _all_gather_only("gather_first", ("pool", "scale", "gu0"), shards)
    (x1_0, h2_0, yraw, dpool), (wd0,) = _fwd_pool_mixer(
        x, row(g_pre_mix, 0), wp, scale, row(g_post_mix, 0), row(g_pre_ffn, 0), job=gather("wd0"))
    (gs0, us0, f0, x2_0, h3_0), (wgate0, wproj0, wkv, wq, wo) = _fwd_ffn(
        0, h2_0, x1_0, wgu0, wd0, row(g_post_ffn, 0), row(g_ple, 0), job=gather("gate0", "proj0", "kv", "q", "o"))
    (x3_0, z0, pe0), _ = _fwd_ple(0, x2_0, h3_0, p[0], wgate0, wproj0, row(g_ple_post, 0))
    (hk, h1, q, kv), _ = _fwd_qkv(x3_0, g_kv, row(g_pre_mix, 1), wkv, wq)
    front = ((ATT_BLOCK, 0), (0, 0))
    kpad = jnp.pad(kv[:, :KV_DIM], front)
    vpad = jnp.pad(kv[:, KV_DIM:], front)
    (attn,), (wgu1,) = _fwd_attention(q, kpad, vpad, sinks, job=gather("gu1"))
    (y1, x1_1, h2_1), (wd1,) = _fwd_attn_out(attn, x3_0, wo, row(g_post_mix, 1), row(g_pre_ffn, 1), job=gather("wd1"))
    (gs1, us1, f1, x2_1, h3_1), (wgate1, wproj1) = _fwd_ffn(
        1, h2_1, x1_1, wgu1, wd1, row(g_post_ffn, 1), row(g_ple, 1), job=gather("gate1", "proj1"))
    (dx3_1, z1, pe1, loss), _ = _fwd_ple(1, x2_1, h3_1, p[1], wgate1, wproj1, row(g_ple_post, 1), target=tgt)

    landed = {}

    def scatter(**grads):
        return _Scatter(grads), list(grads)

    def keep(job_and_names, arrays):
        landed.update(zip(job_and_names[1], arrays))

    (dx2_1, df1, dwgate1, dwproj1, dg_ple_post1, dg_ple1, dg_post_ffn1), _ = _bwd_ple(
        1, dx3_1, x2_1, z1, pe1, h3_1, p[1], f1, wgate1, row(g_ple_post, 1), row(g_ple, 1), row(g_post_ffn, 1))
    sc = scatter(gate1=dwgate1, proj1=dwproj1)
    (dh2_1, dgu1, dwd1), out = _bwd_ffn(1, df1, h2_1, gs1, us1, wgu1, wd1, job=sc[0])
    keep(sc, out)
    (dx1_1, dattn, dwo, dg_pre_ffn1, dg_post_mix1), _ = _bwd_attn_out(
        dx2_1, dh2_1, x1_1, y1, attn, wo, row(g_pre_ffn, 1), row(g_post_mix, 1))
    sc = scatter(gu1=dgu1)
    (dq, dkpad, dvpad, dsinks), out = _bwd_attention(q, dattn, kpad, vpad, sinks, job=sc[0])
    keep(sc, out)
    dkv = jnp.concatenate([dkpad[ATT_BLOCK:], dvpad[ATT_BLOCK:]], axis=1).astype(BF16)
    sc = scatter(o=dwo)
    (dx3_0, dwq, dwkv, dg_pre_mix1, dg_kv), out = _bwd_qkv(
        dx1_1, dq, dkv, x3_0, h1, hk, wq, wkv, row(g_pre_mix, 1), g_kv, job=sc[0])
    keep(sc, out)
    sc = scatter(q=dwq, kv=dwkv)
    (dx2_0, df0, dwgate0, dwproj0, dg_ple_post0, dg_ple0, dg_post_ffn0), out = _bwd_ple(
        0, dx3_0, x2_0, z0, pe0, h3_0, p[0], f0, wgate0, row(g_ple_post, 0), row(g_ple, 0), row(g_post_ffn, 0),
        job=sc[0])
    keep(sc, out)
    sc = scatter(wd1=dwd1, gate0=dwgate0, proj0=dwproj0)
    (dh2_0, dgu0, dwd0), out = _bwd_ffn(0, df0, h2_0, gs0, us0, wgu0, wd0, job=sc[0])
    keep(sc, out)
    sc = scatter(gu0=dgu0, wd0=dwd0)
    (grad_x, dwp, dscale, dg_pre_ffn0, dg_post_mix0, dg_pre_mix0), out = _bwd_pool_mixer(
        dx2_0, dh2_0, x1_0, x, yraw, dpool, wp, scale, row(g_pre_ffn, 0), row(g_post_mix, 0), row(g_pre_mix, 0),
        job=sc[0])
    keep(sc, out)
    landed["pool"], = _scatter_only("scatter_last", {"pool": dwp})

    lanes = lambda a: jnp.pad(a, ((0, 0), (0, D_MODEL - a.shape[1])))
    small = jnp.concatenate([
        dg_pre_mix0, dg_pre_mix1, dg_post_mix0, dg_post_mix1, dg_pre_ffn0, dg_pre_ffn1, dg_post_ffn0, dg_post_ffn1,
        dg_ple0, dg_ple1, dg_ple_post0, dg_ple_post1, dg_kv, dscale, lanes(dsinks[:, :N_HEADS]), lanes(loss)], axis=0)
    return grad_x, landed, small


def kernel(x, p, pre_mix_g, post_mix_g, pre_ffn_g, post_ffn_g, pool_w, pool_scale, kv_g, w_kv, w_q, sinks, w_o, w_gu, w_down, ple_g, w_ple_gate, w_ple_proj, ple_post_g, loss_target, m_pre_mix_g, m_post_mix_g, m_pre_ffn_g, m_post_ffn_g, m_pool_w, m_pool_scale, m_kv_g, m_w_kv, m_w_q, m_sinks, m_w_o, m_w_gu, m_w_down, m_ple_g, m_w_ple_gate, m_w_ple_proj, m_ple_post_g, v_pre_mix_g, v_post_mix_g, v_pre_ffn_g, v_post_ffn_g, v_pool_w, v_pool_scale, v_kv_g, v_w_kv, v_w_q, v_sinks, v_w_o, v_w_gu, v_w_down, v_ple_g, v_w_ple_gate, v_w_ple_proj, v_ple_post_g):
    me = _dev_index(*_my_place())

    shards = {"pool": pool_w[0].astype(BF16), "scale": pool_scale, "kv": w_kv.astype(BF16),
              "q": w_q[0].astype(BF16), "o": w_o[0].astype(BF16)}
    for layer in range(2):
        shards[f"gu{layer}"] = w_gu[layer].astype(BF16)
        shards[f"wd{layer}"] = w_down[layer].astype(BF16)
        shards[f"gate{layer}"] = w_ple_gate[layer].astype(BF16)
        shards[f"proj{layer}"] = w_ple_proj[layer].astype(BF16)
    gains = dict(pre_mix_g=pre_mix_g, post_mix_g=post_mix_g, pre_ffn_g=pre_ffn_g, post_ffn_g=post_ffn_g,
                 ple_g=ple_g, ple_post_g=ple_post_g, kv_g=kv_g[None, :])
    grad_x, landed, small = _local_step(x[0], p[:, 0], loss_target[0], gains, sinks, shards)

    def update(name, w, m, v, landings, tr):
        R = w.size // w.shape[-1]
        flat = [a.reshape(N_DEV, -1, a.shape[-1]) for a in landings]
        outs = _adamw(name, w.reshape(R, -1), m.reshape(R, -1), v.reshape(R, -1), flat, tr)
        return [o.reshape(w.shape) for o in outs]

    both = lambda name: [landed[name + "0"], landed[name + "1"]]
    upd = {
        "pool_w": update("pool_w", pool_w, m_pool_w, v_pool_w, [landed["pool"]], 128),
        "w_kv": update("w_kv", w_kv, m_w_kv, v_w_kv, [landed["kv"]], 128),
        "w_q": update("w_q", w_q, m_w_q, v_w_q, [landed["q"]], 128),
        "w_o": update("w_o", w_o, m_w_o, v_w_o, [landed["o"]], 128),
        "w_gu": update("w_gu", w_gu, m_w_gu, v_w_gu, both("gu"), 256),
        "w_down": update("w_down", w_down, m_w_down, v_w_down, both("wd"), WD_ROWS),
        "w_ple_gate": update("w_ple_gate", w_ple_gate, m_w_ple_gate, v_w_ple_gate, both("gate"), 128),
        "w_ple_proj": update("w_ple_proj", w_ple_proj, m_w_ple_proj, v_w_ple_proj, both("proj"), 256),
    }

    lane0 = me * 128

    def slab(pre_mix, post_mix, pre_ffn, post_ffn, ple, ple_post, kv, scale_shard, snk):
        scale_row = lax.dynamic_update_slice(jnp.zeros((1, D_MODEL), F32), scale_shard, (0, lane0))
        snk_row = jnp.pad(snk, ((0, 0), (0, D_MODEL - N_HEADS)))
        return jnp.concatenate([pre_mix, post_mix, pre_ffn, post_ffn, ple, ple_post, kv[None, :], scale_row, snk_row,
                                jnp.zeros((1, D_MODEL), F32)], axis=0)

    sw = slab(pre_mix_g, post_mix_g, pre_ffn_g, post_ffn_g, ple_g, ple_post_g, kv_g, pool_scale, sinks)
    sm = slab(m_pre_mix_g, m_post_mix_g, m_pre_ffn_g, m_post_ffn_g, m_ple_g, m_ple_post_g, m_kv_g, m_pool_scale, m_sinks)
    sv = slab(v_pre_mix_g, v_post_mix_g, v_pre_ffn_g, v_post_ffn_g, v_ple_g, v_ple_post_g, v_kv_g, v_pool_scale, v_sinks)
    sg, sd, snm, snv = _small_all_reduce_adamw(small, sw, sm, sv)
    loss = sg[SV_LOSS, 0]

    def unslab(s):
        return {
            "pre_mix_g": s[SV_PRE_MIX:SV_PRE_MIX + 2], "post_mix_g": s[SV_POST_MIX:SV_POST_MIX + 2],
            "pre_ffn_g": s[SV_PRE_FFN:SV_PRE_FFN + 2], "post_ffn_g": s[SV_POST_FFN:SV_POST_FFN + 2],
            "ple_g": s[SV_PLE:SV_PLE + 2], "ple_post_g": s[SV_PLE_POST:SV_PLE_POST + 2], "kv_g": s[SV_KV],
            "pool_scale": lax.dynamic_slice(s, (SV_POOL_SCALE, lane0), (1, 128)),
            "sinks": s[SV_SINKS:SV_SINKS + 1, :N_HEADS],
        }

    names = ["pre_mix_g", "post_mix_g", "pre_ffn_g", "post_ffn_g", "pool_w", "pool_scale", "kv_g", "w_kv", "w_q",
             "sinks", "w_o", "w_gu", "w_down", "ple_g", "w_ple_gate", "w_ple_proj", "ple_post_g"]
    outs = [loss, grad_x[None]]
    for kind, slab_out in enumerate((sg, sd, snm, snv)):
        small_out = unslab(slab_out)
        outs += [upd[n][kind] if n in upd else small_out[n] for n in names]
    return tuple(outs)
```

```python
import functools

import jax
import jax.numpy as jnp
from jax import lax
from jax.experimental import pallas as pl
from jax.experimental.pallas import tpu as pltpu

F32 = jnp.float32
BF16 = jnp.bfloat16

N_DEV = 8
D_MODEL = 1024
N_POOL_GROUPS = 4
POOL_GROUP = 256
POOL_HALO = 16
HEAD_DIM = 64
N_HEADS = 16
N_KV_HEADS = 4
GQA_GROUP = 4
KV_DIM = N_KV_HEADS * HEAD_DIM
ATT_BLOCK = 128
D_FF = 2816
FF_CHUNKS = 4
FF_BLOCK = D_FF // FF_CHUNKS
WD_ROWS = D_FF // N_DEV
FF_QUARTERS = 4
FF_QUARTER = D_MODEL // FF_QUARTERS
N_CHIPS = 4
PLE_DIM = 256
EPS = 1e-6
NEG_INF = -1e30
ATT_SCALE = HEAD_DIM ** -0.5

ADAM_LR = 0.001
ADAM_B1 = 0.9
ADAM_B2 = 0.999
ADAM_EPS = 1e-08
ADAM_WD = 0.01
ADAM_STEP = 10

ROW_TILE = 256
VMEM_BIG = 56 * 1024 * 1024
VMEM_MID = 48 * 1024 * 1024
HBM_PIN_ELEMS = 64 * 1024

SV_ROWS = 16
SV_PRE_MIX, SV_POST_MIX, SV_PRE_FFN, SV_POST_FFN, SV_PLE, SV_PLE_POST = 0, 2, 4, 6, 8, 10
SV_KV, SV_POOL_SCALE, SV_SINKS, SV_LOSS = 12, 13, 14, 15

MESH = pl.DeviceIdType.MESH
ANY = pl.BlockSpec(memory_space=pl.ANY)


def _dot(a, b):
    return jnp.dot(a, b, preferred_element_type=F32)


def _dot_nt(a, b):
    return lax.dot_general(a, b, (((1,), (1,)), ((), ())), preferred_element_type=F32)


def _dot_tn(a, b):
    return lax.dot_general(a, b, (((0,), (0,)), ((), ())), preferred_element_type=F32)


def _rstd(x):
    return lax.rsqrt(jnp.mean(x * x, axis=-1, keepdims=True) + EPS)


def _rms(x, g):
    return x * _rstd(x) * g


def _rms_bwd(x, g, dy):
    r = _rstd(x)
    n = x * r
    dn = dy * g
    dx = r * (dn - n * jnp.mean(dn * n, axis=-1, keepdims=True))
    dg = jnp.sum(dy * n, axis=0, keepdims=True)
    return dx, dg


def _sigmoid(x):
    return 1.0 / (1.0 + jnp.exp(-x))


def _acc(ref, val, first):
    @pl.when(first)
    def _():
        ref[...] = val

    @pl.when(jnp.logical_not(first))
    def _():
        ref[...] += val


def _pool_counts(row0, rows):
    t = row0 + lax.broadcasted_iota(jnp.int32, (rows, D_MODEL), 0) + 1
    grp = lax.broadcasted_iota(jnp.int32, (rows, D_MODEL), 1) // POOL_GROUP
    win = jnp.left_shift(2, grp)
    return jnp.minimum(t, win).astype(F32)


def _window_sums(ext, shift_of):
    outs = []
    s = ext
    for gi in range(N_POOL_GROUPS):
        s = s + pltpu.roll(s, shift_of(1 << gi), axis=0)
        outs.append(s[:, :POOL_GROUP])
        s = s[:, POOL_GROUP:]
    return jnp.concatenate(outs, axis=1)


def _cparams(n_axes, vmem):
    return pltpu.CompilerParams(dimension_semantics=("arbitrary",) * n_axes, vmem_limit_bytes=vmem)


def _row_spec(cols, tm=ROW_TILE):
    return pl.BlockSpec((tm, cols), lambda i: (i, 0))


def _full_spec(shape):
    zeros = (0,) * len(shape)
    return pl.BlockSpec(shape, lambda *_: zeros)


def _vec_spec():
    return _full_spec((1, D_MODEL))


class _Gain:
    def __init__(self, stacked, layer):
        self.stacked, self.layer = stacked, layer

    def spec(self):
        layer = self.layer
        return pl.BlockSpec((None, 1, D_MODEL), lambda *_: (layer, 0, 0))


def _in_hbm(a):
    return pltpu.with_memory_space_constraint(a, pltpu.HBM) if a.size >= HBM_PIN_ELEMS else a


def _launch(body, *, name, grid, in_specs, out_specs, out_shape, args, scratch_shapes=(), vmem=VMEM_MID, job=None):
    in_specs = [a.spec() if isinstance(a, _Gain) else s for s, a in zip(in_specs, args)]
    args = [a.stacked if isinstance(a, _Gain) else _in_hbm(a) for a in args]
    n_in, n_out, n_scr = len(args), len(out_shape), len(scratch_shapes)
    j_args, j_out, j_scr = ([], [], []) if job is None else ([_in_hbm(a) for a in job.args], job.out_shape, job.scratch)

    def run(*refs):
        groups, at = [], 0
        for n in (n_in, len(j_args), n_out, len(j_out), n_scr, len(j_scr)):
            groups.append(refs[at:at + n])
            at += n
        ins, j_ins, outs, j_outs, scr, j_sems = groups
        if job is None:
            body(*ins, *outs, *scr)
        elif not grid:
            job.start(j_ins, j_outs, j_sems)
            body(*ins, *outs, *scr)
            job.finish(j_ins, j_outs, j_sems)
        else:
            ids = [pl.program_id(a) for a in range(len(grid))]
            first = functools.reduce(jnp.logical_and, [i == 0 for i in ids])
            last = functools.reduce(jnp.logical_and, [i == g - 1 for i, g in zip(ids, grid)])
            pl.when(first)(lambda: job.start(j_ins, j_outs, j_sems))
            body(*ins, *outs, *scr)
            pl.when(last)(lambda: job.finish(j_ins, j_outs, j_sems))

    res = pl.pallas_call(
        run, name=name, grid=grid,
        in_specs=list(in_specs) + [ANY] * len(j_args), out_specs=list(out_specs) + [ANY] * len(j_out),
        out_shape=list(out_shape) + list(j_out), scratch_shapes=list(scratch_shapes) + list(j_scr),
        compiler_params=_cparams(len(grid), vmem),
    )(*args, *j_args)
    return res[:n_out], res[n_out:]


def _fwd_pool_mixer(x, g_pre, wp, scale, g_post, g_ffn, job=None):
    T = x.shape[0]
    tm = ROW_TILE
    nt = T // tm

    def body(x_ref, gpre_ref, wp_ref, sc_ref, gpost_ref, gffn_ref, x1_ref, h2_ref, yraw_ref, d_ref, carry):
        i = pl.program_id(0)

        @pl.when(i == 0)
        def _():
            carry[...] = jnp.zeros_like(carry)

        xv = x_ref[...]
        h = _rms(xv, gpre_ref[...])
        ext = jnp.concatenate([carry[...], h], axis=0)
        carry[...] = h[tm - POOL_HALO:, :]
        sums = _window_sums(ext, lambda k: k)[POOL_HALO:, :]
        d = sums / _pool_counts(i * tm, tm) - h
        db = d.astype(BF16)
        d_ref[...] = db
        yraw = jnp.concatenate(
            [_dot(db[:, g * POOL_GROUP:(g + 1) * POOL_GROUP], wp_ref[g]) for g in range(N_POOL_GROUPS)], axis=1)
        yraw_ref[...] = yraw
        x1 = xv + _rms(yraw * sc_ref[...], gpost_ref[...])
        x1_ref[...] = x1
        h2_ref[...] = _rms(x1, gffn_ref[...]).astype(BF16)

    return _launch(
        body, name="fwd_pool_mixer", grid=(nt,),
        in_specs=[_row_spec(D_MODEL), _vec_spec(), _full_spec((N_POOL_GROUPS, POOL_GROUP, POOL_GROUP)), _vec_spec(),
                  _vec_spec(), _vec_spec()],
        out_specs=[_row_spec(D_MODEL)] * 4,
        out_shape=[jax.ShapeDtypeStruct((T, D_MODEL), F32), jax.ShapeDtypeStruct((T, D_MODEL), BF16),
                   jax.ShapeDtypeStruct((T, D_MODEL), F32), jax.ShapeDtypeStruct((T, D_MODEL), BF16)],
        scratch_shapes=[pltpu.VMEM((POOL_HALO, D_MODEL), F32)],
        args=(x, g_pre, wp, scale, g_post, g_ffn), job=job)


def _fwd_ffn(layer, h2, x1, wgu, wd, g_post, g_ple, job=None):
    T = h2.shape[0]
    tm = ROW_TILE
    nt = T // tm
    last = FF_CHUNKS - 1

    def body(h2_ref, x1_ref, wgu_ref, wd_ref, gpost_ref, gple_ref, gs_ref, us_ref, f_ref, x2_ref, h3_ref, acc):
        k = pl.program_id(0)
        i = pl.program_id(1)
        rows = pl.ds(pl.multiple_of(i * tm, tm), tm)
        h = h2_ref[...]
        g = _dot(h, wgu_ref[0])
        u = _dot(h, wgu_ref[1])
        gs_ref[...] = g.astype(BF16)
        us_ref[...] = u.astype(BF16)
        a = (g * _sigmoid(g) * u).astype(BF16)
        part = _dot(a, wd_ref[...])

        @pl.when(k == 0)
        def _():
            acc[rows, :] = part

        @pl.when(jnp.logical_and(k > 0, k < last))
        def _():
            acc[rows, :] += part

        @pl.when(k == last)
        def _():
            f = acc[rows, :] + part
            f_ref[...] = f
            x2 = x1_ref[...] + _rms(f, gpost_ref[...])
            x2_ref[...] = x2
            h3_ref[...] = _rms(x2, gple_ref[...]).astype(BF16)

    def late(k, i):
        return (jnp.where(k == last, i, 0), 0)

    return _launch(
        body, name=f"fwd_ffn{layer}", grid=(FF_CHUNKS, nt),
        in_specs=[pl.BlockSpec((tm, D_MODEL), lambda k, i: (i, 0)),
                  pl.BlockSpec((tm, D_MODEL), late),
                  pl.BlockSpec((None, 2, D_MODEL, FF_BLOCK), lambda k, i: (k, 0, 0, 0)),
                  pl.BlockSpec((FF_BLOCK, D_MODEL), lambda k, i: (k, 0)),
                  pl.BlockSpec((1, D_MODEL), lambda k, i: (0, 0)),
                  pl.BlockSpec((1, D_MODEL), lambda k, i: (0, 0))],
        out_specs=[pl.BlockSpec((None, tm, FF_BLOCK), lambda k, i: (k, i, 0)),
                   pl.BlockSpec((None, tm, FF_BLOCK), lambda k, i: (k, i, 0)),
                   pl.BlockSpec((tm, D_MODEL), late),
                   pl.BlockSpec((tm, D_MODEL), late),
                   pl.BlockSpec((tm, D_MODEL), late)],
        out_shape=[jax.ShapeDtypeStruct((FF_CHUNKS, T, FF_BLOCK), BF16),
                   jax.ShapeDtypeStruct((FF_CHUNKS, T, FF_BLOCK), BF16),
                   jax.ShapeDtypeStruct((T, D_MODEL), F32),
                   jax.ShapeDtypeStruct((T, D_MODEL), F32),
                   jax.ShapeDtypeStruct((T, D_MODEL), BF16)],
        scratch_shapes=[pltpu.VMEM((T, D_MODEL), F32)],
        args=(h2, x1, wgu, wd, g_post, g_ple), job=job)


def _fwd_ple(layer, x2, h3, p, wgate, wproj, g_post, target=None, job=None):
    T = x2.shape[0]
    tm = ROW_TILE
    nt = T // tm
    with_loss = target is not None

    def body(*refs):
        if with_loss:
            x2_ref, h3_ref, p_ref, wg_ref, wp_ref, gpost_ref, tgt_ref, out_ref, z_ref, pe_ref, loss_ref = refs
        else:
            x2_ref, h3_ref, p_ref, wg_ref, wp_ref, gpost_ref, out_ref, z_ref, pe_ref = refs
        z = _dot(h3_ref[...], wg_ref[...])
        pe = _dot(p_ref[...].astype(BF16), wp_ref[...])
        z_ref[...] = z
        pe_ref[...] = pe
        x3 = x2_ref[...] + _rms(pe * _sigmoid(z), gpost_ref[...])
        if with_loss:
            err = x3 - tgt_ref[...]
            out_ref[...] = err * (1.0 / D_MODEL)
            part = 0.5 * jnp.sum(jnp.mean(err * err, axis=-1, keepdims=True), axis=0, keepdims=True)
            _acc(loss_ref, part, pl.program_id(0) == 0)
        else:
            out_ref[...] = x3

    in_specs = [_row_spec(D_MODEL), _row_spec(D_MODEL), _row_spec(PLE_DIM), _full_spec((D_MODEL, D_MODEL)),
                _full_spec((PLE_DIM, D_MODEL)), _vec_spec()]
    out_specs = [_row_spec(D_MODEL)] * 3
    out_shape = [jax.ShapeDtypeStruct((T, D_MODEL), F32)] * 3
    args = [x2, h3, p, wgate, wproj, g_post]
    if with_loss:
        in_specs.append(_row_spec(D_MODEL))
        out_specs.append(_full_spec((1, 1)))
        out_shape.append(jax.ShapeDtypeStruct((1, 1), F32))
        args.append(target)
    return _launch(body, name=f"fwd_ple{layer}", grid=(nt,), in_specs=in_specs, out_specs=out_specs,
                   out_shape=out_shape, args=args, job=job)


def _fwd_qkv(x3, g_kv, g_mix, wkv, wq, job=None):
    T = x3.shape[0]
    nt = T // ROW_TILE

    def body(x_ref, gkv_ref, gmix_ref, wkv_ref, wq_ref, hk_ref, h1_ref, q_ref, kv_ref):
        xv = x_ref[...]
        r = _rstd(xv)
        hk = (xv * r * gkv_ref[...]).astype(BF16)
        h1 = (xv * r * gmix_ref[...]).astype(BF16)
        hk_ref[...] = hk
        h1_ref[...] = h1
        kv_ref[...] = _dot(hk, wkv_ref[...]).astype(BF16)
        q_ref[...] = _dot(h1, wq_ref[...]).astype(BF16)

    return _launch(
        body, name="fwd_qkv", grid=(nt,),
        in_specs=[_row_spec(D_MODEL), _vec_spec(), _vec_spec(), _full_spec((D_MODEL, 2 * KV_DIM)),
                  _full_spec((D_MODEL, D_MODEL))],
        out_specs=[_row_spec(D_MODEL), _row_spec(D_MODEL), _row_spec(D_MODEL), _row_spec(2 * KV_DIM)],
        out_shape=[jax.ShapeDtypeStruct((T, D_MODEL), BF16)] * 3 + [jax.ShapeDtypeStruct((T, 2 * KV_DIM), BF16)],
        args=(x3, g_kv, g_mix, wkv, wq), job=job)


def _alibi_slope(h):
    return 2.0 ** (-8.0 * (h + 1) / N_HEADS)


def _att_mask(n):
    qi = lax.broadcasted_iota(jnp.int32, (ATT_BLOCK, 2 * ATT_BLOCK), 0)
    si = lax.broadcasted_iota(jnp.int32, (ATT_BLOCK, 2 * ATT_BLOCK), 1)
    rel = ATT_BLOCK + qi - si
    valid = (rel >= 0) & (rel < ATT_BLOCK) & ((si >= ATT_BLOCK) | (n > 0))
    return rel.astype(F32), valid


def _att_probs(qh, kk, relf, valid, slope, sink):
    s = _dot_nt(qh, kk) * ATT_SCALE
    s = jnp.where(valid, s - slope * relf, NEG_INF)
    m = jnp.maximum(jnp.max(s, axis=-1, keepdims=True), sink)
    e = jnp.exp(s - m)
    es = jnp.exp(sink - m)
    inv = 1.0 / (jnp.sum(e, axis=-1, keepdims=True) + es)
    return e * inv, es * inv


def _fwd_attention(q, kpad, vpad, sinks, job=None):
    T = q.shape[0]
    nb = T // ATT_BLOCK

    def body(q_ref, k_ref, v_ref, sink_ref, o_ref):
        n = pl.program_id(0)
        start = pl.multiple_of(n * ATT_BLOCK, ATT_BLOCK)
        kw = k_ref[pl.ds(start, 2 * ATT_BLOCK), :]
        vw = v_ref[pl.ds(start, 2 * ATT_BLOCK), :]
        relf, valid = _att_mask(n)
        outs = []
        for h in range(N_HEADS):
            kh = h // GQA_GROUP
            qh = q_ref[:, h * HEAD_DIM:(h + 1) * HEAD_DIM]
            kk = kw[:, kh * HEAD_DIM:(kh + 1) * HEAD_DIM]
            vv = vw[:, kh * HEAD_DIM:(kh + 1) * HEAD_DIM]
            pr, _ = _att_probs(qh, kk, relf, valid, _alibi_slope(h), sink_ref[0, h])
            outs.append(_dot(pr.astype(BF16), vv))
        o_ref[...] = jnp.concatenate(outs, axis=1).astype(BF16)

    return _launch(
        body, name="fwd_attention", grid=(nb,),
        in_specs=[_row_spec(D_MODEL, ATT_BLOCK), _full_spec((T + ATT_BLOCK, KV_DIM)), _full_spec((T + ATT_BLOCK, KV_DIM)),
                  pl.BlockSpec(memory_space=pltpu.SMEM)],
        out_specs=[_row_spec(D_MODEL, ATT_BLOCK)],
        out_shape=[jax.ShapeDtypeStruct((T, D_MODEL), BF16)],
        args=(q, kpad, vpad, sinks), job=job)


def _fwd_attn_out(attn, x, wo, g_post, g_ffn, job=None):
    T = x.shape[0]
    nt = T // ROW_TILE

    def body(a_ref, x_ref, wo_ref, gpost_ref, gffn_ref, y_ref, x1_ref, h2_ref):
        y = _dot(a_ref[...], wo_ref[...])
        y_ref[...] = y
        x1 = x_ref[...] + _rms(y, gpost_ref[...])
        x1_ref[...] = x1
        h2_ref[...] = _rms(x1, gffn_ref[...]).astype(BF16)

    return _launch(
        body, name="fwd_attn_out", grid=(nt,),
        in_specs=[_row_spec(D_MODEL), _row_spec(D_MODEL), _full_spec((D_MODEL, D_MODEL)), _vec_spec(), _vec_spec()],
        out_specs=[_row_spec(D_MODEL)] * 3,
        out_shape=[jax.ShapeDtypeStruct((T, D_MODEL), F32), jax.ShapeDtypeStruct((T, D_MODEL), F32),
                   jax.ShapeDtypeStruct((T, D_MODEL), BF16)],
        args=(attn, x, wo, g_post, g_ffn), job=job)


def _bwd_ple(layer, dx3, x2, z, pe, h3, p, f, wgate, g_ple_post, g_ple, g_post_ffn, job=None):
    T = x2.shape[0]
    tm = ROW_TILE
    nt = T // tm

    def body(dx3_ref, x2_ref, z_ref, pe_ref, h3_ref, p_ref, f_ref, wg_ref, gpp_ref, gp_ref, gpf_ref,
             dx2_ref, df_ref, dwg_ref, dwp_ref, dgpp_ref, dgp_ref, dgpf_ref, acc_g, acc_p):
        i = pl.program_id(0)
        first = i == 0
        dx3v = dx3_ref[...]
        gate = _sigmoid(z_ref[...])
        pev = pe_ref[...]
        de, dgpp = _rms_bwd(pev * gate, gpp_ref[...], dx3v)
        dpe = (de * gate).astype(BF16)
        dz = (de * pev * gate * (1.0 - gate)).astype(BF16)
        _acc(acc_p, _dot_tn(p_ref[...].astype(BF16), dpe), first)
        _acc(acc_g, _dot_tn(h3_ref[...], dz), first)
        dh3 = _dot_nt(dz, wg_ref[...])
        dxn, dgp = _rms_bwd(x2_ref[...], gp_ref[...], dh3)
        dx2 = dx3v + dxn
        dx2_ref[...] = dx2
        df, dgpf = _rms_bwd(f_ref[...], gpf_ref[...], dx2)
        df_ref[...] = df.astype(BF16)
        _acc(dgpp_ref, dgpp, first)
        _acc(dgp_ref, dgp, first)
        _acc(dgpf_ref, dgpf, first)

        @pl.when(i == nt - 1)
        def _():
            dwg_ref[...] = acc_g[...].astype(BF16)
            dwp_ref[...] = acc_p[...].astype(BF16)

    return _launch(
        body, name=f"bwd_ple{layer}", grid=(nt,),
        in_specs=[_row_spec(D_MODEL)] * 5 + [_row_spec(PLE_DIM), _row_spec(D_MODEL), _full_spec((D_MODEL, D_MODEL)),
                  _vec_spec(), _vec_spec(), _vec_spec()],
        out_specs=[_row_spec(D_MODEL), _row_spec(D_MODEL), _full_spec((D_MODEL, D_MODEL)), _full_spec((PLE_DIM, D_MODEL)),
                   _vec_spec(), _vec_spec(), _vec_spec()],
        out_shape=[jax.ShapeDtypeStruct((T, D_MODEL), F32), jax.ShapeDtypeStruct((T, D_MODEL), BF16),
                   jax.ShapeDtypeStruct((D_MODEL, D_MODEL), BF16), jax.ShapeDtypeStruct((PLE_DIM, D_MODEL), BF16)]
                  + [jax.ShapeDtypeStruct((1, D_MODEL), F32)] * 3,
        scratch_shapes=[pltpu.VMEM((D_MODEL, D_MODEL), F32), pltpu.VMEM((PLE_DIM, D_MODEL), F32)],
        args=(dx3, x2, z, pe, h3, p, f, wgate, g_ple_post, g_ple, g_post_ffn), job=job)


def _bwd_ffn_act(layer, df, gs, us, wgu, wd, job=None):
    T = df.shape[0]
    tm = ROW_TILE
    nt = T // tm
    last = FF_CHUNKS - 1

    def body(df_ref, gs_ref, us_ref, wgu_ref, wd_ref, dh_ref, dg_ref, du_ref, a_ref, acc_h):
        k = pl.program_id(0)
        i = pl.program_id(1)
        rows = pl.ds(pl.multiple_of(i * tm, tm), tm)
        g = gs_ref[...].astype(F32)
        u = us_ref[...].astype(F32)
        sg = _sigmoid(g)
        silu = g * sg
        a_ref[...] = (silu * u).astype(BF16)
        da = _dot_nt(df_ref[...], wd_ref[...])
        dg = (da * u * (sg * (1.0 + g * (1.0 - sg)))).astype(BF16)
        du = (da * silu).astype(BF16)
        dg_ref[...] = dg
        du_ref[...] = du
        dh = _dot_nt(dg, wgu_ref[0]) + _dot_nt(du, wgu_ref[1])

        @pl.when(k == 0)
        def _():
            acc_h[rows, :] = dh

        @pl.when(jnp.logical_and(k > 0, k < last))
        def _():
            acc_h[rows, :] += dh

        @pl.when(k == last)
        def _():
            dh_ref[...] = acc_h[rows, :] + dh

    chunk_rows = pl.BlockSpec((None, tm, FF_BLOCK), lambda k, i: (k, i, 0))
    saved = jax.ShapeDtypeStruct((FF_CHUNKS, T, FF_BLOCK), BF16)
    return _launch(
        body, name=f"bwd_ffn_act{layer}", grid=(FF_CHUNKS, nt),
        in_specs=[pl.BlockSpec((tm, D_MODEL), lambda k, i: (i, 0)), chunk_rows, chunk_rows,
                  pl.BlockSpec((None, 2, D_MODEL, FF_BLOCK), lambda k, i: (k, 0, 0, 0)),
                  pl.BlockSpec((FF_BLOCK, D_MODEL), lambda k, i: (k, 0))],
        out_specs=[pl.BlockSpec((tm, D_MODEL), lambda k, i: (jnp.where(k == last, i, 0), 0)),
                   chunk_rows, chunk_rows, chunk_rows],
        out_shape=[jax.ShapeDtypeStruct((T, D_MODEL), F32), saved, saved, saved],
        scratch_shapes=[pltpu.VMEM((T, D_MODEL), F32)],
        args=(df, gs, us, wgu, wd), job=job)


def _bwd_ffn_dw(layer, q, h2, df, dg, du, a, job=None):
    T = h2.shape[0]

    def body(h_ref, df_ref, dg_ref, du_ref, a_ref, dgu_ref, dwd_ref):
        h = h_ref[...]
        dgu_ref[0] = _dot_tn(h, dg_ref[...]).astype(BF16)
        dgu_ref[1] = _dot_tn(h, du_ref[...]).astype(BF16)
        dwd_ref[...] = _dot_tn(a_ref[...], df_ref[...]).astype(BF16)

    cols = pl.BlockSpec((T, FF_QUARTER), lambda k: (0, q))
    chunk = pl.BlockSpec((None, T, FF_BLOCK), lambda k: (k, 0, 0))
    return _launch(
        body, name=f"bwd_ffn_dw{layer}_{q}", grid=(FF_CHUNKS,),
        in_specs=[cols, cols, chunk, chunk, chunk],
        out_specs=[pl.BlockSpec((None, 2, FF_QUARTER, FF_BLOCK), lambda k: (k, 0, 0, 0)),
                   pl.BlockSpec((FF_BLOCK, FF_QUARTER), lambda k: (k, 0))],
        out_shape=[jax.ShapeDtypeStruct((FF_CHUNKS, 2, FF_QUARTER, FF_BLOCK), BF16),
                   jax.ShapeDtypeStruct((D_FF, FF_QUARTER), BF16)],
        args=(h2, df, dg, du, a), job=job)


def _bwd_attn_out(dx2, dh2, x1, y, attn, wo, g_ffn, g_post, job=None):
    T = x1.shape[0]
    nt = T // ROW_TILE

    def body(dx2_ref, dh2_ref, x1_ref, y_ref, a_ref, wo_ref, gffn_ref, gpost_ref,
             dx1_ref, da_ref, dwo_ref, dgf_ref, dgp_ref, acc):
        i = pl.program_id(0)
        first = i == 0
        dxn, dgf = _rms_bwd(x1_ref[...], gffn_ref[...], dh2_ref[...])
        dx1 = dx2_ref[...] + dxn
        dx1_ref[...] = dx1
        dy, dgp = _rms_bwd(y_ref[...], gpost_ref[...], dx1)
        dyb = dy.astype(BF16)
        da_ref[...] = _dot_nt(dyb, wo_ref[...]).astype(BF16)
        _acc(acc, _dot_tn(a_ref[...], dyb), first)
        _acc(dgf_ref, dgf, first)
        _acc(dgp_ref, dgp, first)

        @pl.when(i == nt - 1)
        def _():
            dwo_ref[...] = acc[...].astype(BF16)

    return _launch(
        body, name="bwd_attn_out", grid=(nt,),
        in_specs=[_row_spec(D_MODEL)] * 5 + [_full_spec((D_MODEL, D_MODEL)), _vec_spec(), _vec_spec()],
        out_specs=[_row_spec(D_MODEL), _row_spec(D_MODEL), _full_spec((D_MODEL, D_MODEL)), _vec_spec(), _vec_spec()],
        out_shape=[jax.ShapeDtypeStruct((T, D_MODEL), F32), jax.ShapeDtypeStruct((T, D_MODEL), BF16),
                   jax.ShapeDtypeStruct((D_MODEL, D_MODEL), BF16)] + [jax.ShapeDtypeStruct((1, D_MODEL), F32)] * 2,
        scratch_shapes=[pltpu.VMEM((D_MODEL, D_MODEL), F32)],
        args=(dx2, dh2, x1, y, attn, wo, g_ffn, g_post), job=job)


def _bwd_attention(q, dattn, kpad, vpad, sinks, job=None):
    T = q.shape[0]
    nb = T // ATT_BLOCK

    def body(q_ref, do_ref, k_ref, v_ref, sink_ref, dq_ref, dk_ref, dv_ref, ds_ref):
        n = pl.program_id(0)

        @pl.when(n == 0)
        def _():
            dk_ref[...] = jnp.zeros_like(dk_ref)
            dv_ref[...] = jnp.zeros_like(dv_ref)
            ds_ref[...] = jnp.zeros_like(ds_ref)

        start = pl.multiple_of(n * ATT_BLOCK, ATT_BLOCK)
        win = pl.ds(start, 2 * ATT_BLOCK)
        kw = k_ref[win, :]
        vw = v_ref[win, :]
        relf, valid = _att_mask(n)
        lane = lax.broadcasted_iota(jnp.int32, (1, ATT_BLOCK), 1)
        dsink = jnp.zeros((1, ATT_BLOCK), F32)
        dqs, dks, dvs = [], [], []
        for kh in range(N_KV_HEADS):
            kk = kw[:, kh * HEAD_DIM:(kh + 1) * HEAD_DIM]
            vv = vw[:, kh * HEAD_DIM:(kh + 1) * HEAD_DIM]
            dk_h = jnp.zeros((2 * ATT_BLOCK, HEAD_DIM), F32)
            dv_h = jnp.zeros((2 * ATT_BLOCK, HEAD_DIM), F32)
            for gq in range(GQA_GROUP):
                h = kh * GQA_GROUP + gq
                qh = q_ref[:, h * HEAD_DIM:(h + 1) * HEAD_DIM]
                do = do_ref[:, h * HEAD_DIM:(h + 1) * HEAD_DIM]
                pr, ps = _att_probs(qh, kk, relf, valid, _alibi_slope(h), sink_ref[0, h])
                dp = _dot_nt(do, vv)
                delta = jnp.sum(pr * dp, axis=-1, keepdims=True)
                dsb = (pr * (dp - delta) * ATT_SCALE).astype(BF16)
                dsink = dsink + jnp.where(lane == h, -jnp.sum(ps * delta, axis=0, keepdims=True), 0.0)
                dqs.append(_dot(dsb, kk))
                dk_h = dk_h + _dot_tn(dsb, qh)
                dv_h = dv_h + _dot_tn(pr.astype(BF16), do)
            dks.append(dk_h)
            dvs.append(dv_h)
        dq_ref[...] = jnp.concatenate(dqs, axis=1).astype(BF16)
        dk_ref[win, :] += jnp.concatenate(dks, axis=1)
        dv_ref[win, :] += jnp.concatenate(dvs, axis=1)
        ds_ref[...] += dsink

    return _launch(
        body, name="bwd_attention", grid=(nb,),
        in_specs=[_row_spec(D_MODEL, ATT_BLOCK), _row_spec(D_MODEL, ATT_BLOCK), _full_spec((T + ATT_BLOCK, KV_DIM)),
                  _full_spec((T + ATT_BLOCK, KV_DIM)), pl.BlockSpec(memory_space=pltpu.SMEM)],
        out_specs=[_row_spec(D_MODEL, ATT_BLOCK), _full_spec((T + ATT_BLOCK, KV_DIM)), _full_spec((T + ATT_BLOCK, KV_DIM)),
                   _full_spec((1, ATT_BLOCK))],
        out_shape=[jax.ShapeDtypeStruct((T, D_MODEL), BF16), jax.ShapeDtypeStruct((T + ATT_BLOCK, KV_DIM), F32),
                   jax.ShapeDtypeStruct((T + ATT_BLOCK, KV_DIM), F32), jax.ShapeDtypeStruct((1, ATT_BLOCK), F32)],
        args=(q, dattn, kpad, vpad, sinks), job=job)


def _bwd_qkv(dxres, dq, dkv, x3, h1, hk, wq, wkv, g_mix, g_kv, job=None):
    T = x3.shape[0]
    nt = T // ROW_TILE

    def body(dxr_ref, dq_ref, dkv_ref, x_ref, h1_ref, hk_ref, wq_ref, wkv_ref, gmix_ref, gkv_ref,
             dx_ref, dwq_ref, dwkv_ref, dgm_ref, dgk_ref, acc_q, acc_kv):
        i = pl.program_id(0)
        first = i == 0
        dqv = dq_ref[...]
        dkvv = dkv_ref[...]
        xv = x_ref[...]
        d1, dgm = _rms_bwd(xv, gmix_ref[...], _dot_nt(dqv, wq_ref[...]))
        d2, dgk = _rms_bwd(xv, gkv_ref[...], _dot_nt(dkvv, wkv_ref[...]))
        dx_ref[...] = dxr_ref[...] + d1 + d2
        _acc(acc_q, _dot_tn(h1_ref[...], dqv), first)
        _acc(acc_kv, _dot_tn(hk_ref[...], dkvv), first)
        _acc(dgm_ref, dgm, first)
        _acc(dgk_ref, dgk, first)

        @pl.when(i == nt - 1)
        def _():
            dwq_ref[...] = acc_q[...].astype(BF16)
            dwkv_ref[...] = acc_kv[...].astype(BF16)

    return _launch(
        body, name="bwd_qkv", grid=(nt,),
        in_specs=[_row_spec(D_MODEL), _row_spec(D_MODEL), _row_spec(2 * KV_DIM), _row_spec(D_MODEL), _row_spec(D_MODEL),
                  _row_spec(D_MODEL), _full_spec((D_MODEL, D_MODEL)), _full_spec((D_MODEL, 2 * KV_DIM)), _vec_spec(),
                  _vec_spec()],
        out_specs=[_row_spec(D_MODEL), _full_spec((D_MODEL, D_MODEL)), _full_spec((D_MODEL, 2 * KV_DIM)), _vec_spec(),
                   _vec_spec()],
        out_shape=[jax.ShapeDtypeStruct((T, D_MODEL), F32), jax.ShapeDtypeStruct((D_MODEL, D_MODEL), BF16),
                   jax.ShapeDtypeStruct((D_MODEL, 2 * KV_DIM), BF16)] + [jax.ShapeDtypeStruct((1, D_MODEL), F32)] * 2,
        scratch_shapes=[pltpu.VMEM((D_MODEL, D_MODEL), F32), pltpu.VMEM((D_MODEL, 2 * KV_DIM), F32)],
        args=(dxres, dq, dkv, x3, h1, hk, wq, wkv, g_mix, g_kv), job=job)


def _bwd_pool_mixer(dx2, dh2, x1, x, yraw, d, wp, scale, g_ffn, g_post, g_pre, job=None):
    T = x.shape[0]
    tm = ROW_TILE
    nt = T // tm

    def body(dx2_ref, dh2_ref, x1_ref, x_ref, yraw_ref, d_ref, wp_ref, sc_ref, gffn_ref, gpost_ref, gpre_ref,
             dx_ref, dwp_ref, dsc_ref, dgf_ref, dgp_ref, dgm_ref, carry, acc):
        i = pl.program_id(0)
        first = i == 0
        tile = nt - 1 - i

        @pl.when(first)
        def _():
            carry[...] = jnp.zeros_like(carry)

        dxn, dgf = _rms_bwd(x1_ref[...], gffn_ref[...], dh2_ref[...])
        dx1 = dx2_ref[...] + dxn
        yraw = yraw_ref[...]
        sc = sc_ref[...]
        dy, dgp = _rms_bwd(yraw * sc, gpost_ref[...], dx1)
        dsc = jnp.sum(dy * yraw, axis=0, keepdims=True)
        dyb = (dy * sc).astype(BF16)
        dv = d_ref[...]
        dds = []
        for g in range(N_POOL_GROUPS):
            cols = slice(g * POOL_GROUP, (g + 1) * POOL_GROUP)
            dds.append(_dot_nt(dyb[:, cols], wp_ref[g]))
            _acc(acc.at[g], _dot_tn(dv[:, cols], dyb[:, cols]), first)
        dd = jnp.concatenate(dds, axis=1)
        e = dd / _pool_counts(tile * tm, tm)
        ext = jnp.concatenate([e, carry[...]], axis=0)
        carry[...] = e[:POOL_HALO, :]
        sums = _window_sums(ext, lambda k: tm + POOL_HALO - k)[:tm, :]
        dxm, dgm = _rms_bwd(x_ref[...], gpre_ref[...], sums - dd)
        dx_ref[...] = dx1 + dxm
        _acc(dsc_ref, dsc, first)
        _acc(dgf_ref, dgf, first)
        _acc(dgp_ref, dgp, first)
        _acc(dgm_ref, dgm, first)

        @pl.when(i == nt - 1)
        def _():
            dwp_ref[...] = acc[...].astype(BF16)

    rev = pl.BlockSpec((tm, D_MODEL), lambda i: (nt - 1 - i, 0))
    return _launch(
        body, name="bwd_pool_mixer", grid=(nt,),
        in_specs=[rev] * 6 + [_full_spec((N_POOL_GROUPS, POOL_GROUP, POOL_GROUP))] + [_vec_spec()] * 4,
        out_specs=[rev, _full_spec((N_POOL_GROUPS, POOL_GROUP, POOL_GROUP))] + [_vec_spec()] * 4,
        out_shape=[jax.ShapeDtypeStruct((T, D_MODEL), F32),
                   jax.ShapeDtypeStruct((N_POOL_GROUPS, POOL_GROUP, POOL_GROUP), BF16)]
                  + [jax.ShapeDtypeStruct((1, D_MODEL), F32)] * 4,
        scratch_shapes=[pltpu.VMEM((POOL_HALO, D_MODEL), F32), pltpu.VMEM((N_POOL_GROUPS, POOL_GROUP, POOL_GROUP), F32)],
        args=(dx2, dh2, x1, x, yraw, d, wp, scale, g_ffn, g_post, g_pre), job=job)


def _my_place():
    return lax.axis_index("x"), lax.axis_index("y"), lax.axis_index("c")


def _dev_index(px, py, pc):
    return 4 * px + 2 * py + pc


def _peer_by_relation(r):
    x, y, c = _my_place()
    return (x ^ ((r >> 2) & 1), y ^ ((r >> 1) & 1), c ^ (r & 1))


def _slot_pool(ref, j):
    return ref.at[:, pl.ds(pl.multiple_of(j * 32, 32), 32), :]


def _slot_scale(ref, j):
    return ref.at[:, pl.ds(pl.multiple_of(j * 128, 128), 128)]


def _slot_rows128(ref, j):
    return ref.at[pl.ds(pl.multiple_of(j * 128, 128), 128), :]


def _slot_gu(ref, j):
    return ref.at[j % FF_CHUNKS, j // FF_CHUNKS]


def _slot_wd(ref, j):
    return ref.at[pl.ds(pl.multiple_of(j * WD_ROWS, 16), WD_ROWS), :]


def _slot_cols128(ref, j):
    return ref.at[:, pl.ds(pl.multiple_of(j * 128, 128), 128)]


_GATHERED = {
    "pool": ((N_POOL_GROUPS, POOL_GROUP, POOL_GROUP), BF16, _slot_pool),
    "scale": ((1, D_MODEL), F32, _slot_scale),
    "kv": ((D_MODEL, 2 * KV_DIM), BF16, _slot_rows128),
    "q": ((D_MODEL, D_MODEL), BF16, _slot_rows128),
    "o": ((D_MODEL, D_MODEL), BF16, _slot_rows128),
    "gu": ((FF_CHUNKS, 2, D_MODEL, FF_BLOCK), BF16, _slot_gu),
    "wd": ((D_FF, D_MODEL), BF16, _slot_wd),
    "gate": ((D_MODEL, D_MODEL), BF16, _slot_rows128),
    "proj": ((PLE_DIM, D_MODEL), BF16, _slot_cols128),
}


def _no_compute():
    pass


class _AllGather:
    def __init__(self, names, shards):
        self.kinds = [_GATHERED[n.rstrip("01")] for n in names]
        self.args = [shards[n] for n in names]
        self.out_shape = [jax.ShapeDtypeStruct(shape, dtype) for shape, dtype, _ in self.kinds]
        n = len(names)
        self.scratch = [pltpu.SemaphoreType.DMA((n, 7)), pltpu.SemaphoreType.DMA((n, 7)), pltpu.SemaphoreType.DMA((n,))]

    def _copies(self, srcs, outs, sems):
        send_sems, recv_sems, local_sems = sems
        x, y, c = _my_place()
        me, sibling = (x, y, c), (x, y, 1 - c)
        chips = [(1 - x, y), (x, 1 - y), (1 - x, 1 - y)]
        n = len(srcs)

        def slot(t, dev):
            return self.kinds[t][2](outs[t], _dev_index(*dev))

        def copy(t, k, block, to, src=None):
            return pltpu.make_async_remote_copy(
                src_ref=slot(t, block) if src is None else src, dst_ref=slot(t, block),
                send_sem=send_sems.at[t, k], recv_sem=recv_sems.at[t, k], device_id=to, device_id_type=MESH)

        mine = [pltpu.make_async_copy(srcs[t], slot(t, me), local_sems.at[t]) for t in range(n)]
        first = []
        for t in range(n):
            first.append(copy(t, 0, me, sibling, src=srcs[t]))
            first += [copy(t, 1 + j, me, (*chip, c), src=srcs[t]) for j, chip in enumerate(chips)]
        return me, sibling, chips, copy, mine, first

    def start(self, srcs, outs, sems):
        _, _, _, _, mine, first = self._copies(srcs, outs, sems)
        for cp in mine + first:
            cp.start()

    def finish(self, srcs, outs, sems):
        me, sibling, chips, copy, mine, first = self._copies(srcs, outs, sems)
        c = me[2]
        n = len(srcs)
        passed = []
        for j, chip in enumerate(chips):
            for t in range(n):
                copy(t, 1 + j, (*chip, c), me).wait_recv()
                fwd = copy(t, 4 + j, (*chip, c), sibling)
                fwd.start()
                passed.append(fwd)
        for t in range(n):
            copy(t, 0, sibling, me).wait_recv()
            for j, chip in enumerate(chips):
                copy(t, 4 + j, (*chip, 1 - c), me).wait_recv()
        for cp in first + passed:
            cp.wait_send()
        for cp in mine:
            cp.wait()


def _all_gather_only(name, names, shards):
    return _launch(_no_compute, name=name, grid=(), in_specs=[], out_specs=[], out_shape=[], args=(),
                   job=_AllGather(names, shards))[1]


def _block_pool(ref, j):
    return ref.at[:, pl.ds(pl.multiple_of(j * 32, 32), 32), :]


def _block_rows128(ref, j):
    return ref.at[pl.ds(pl.multiple_of(j * 128, 128), 128), :]


def _block_gu(ref, j):
    return ref.at[j % FF_CHUNKS, j // FF_CHUNKS]


def _block_wd(ref, j):
    return ref.at[pl.ds(pl.multiple_of(j * WD_ROWS, 16), WD_ROWS), :]


def _block_cols128(ref, j):
    return ref.at[:, pl.ds(pl.multiple_of(j * 128, 128), 128)]


_SCATTERED = {
    "pool": ((N_POOL_GROUPS, 32, POOL_GROUP), _block_pool),
    "kv": ((128, 2 * KV_DIM), _block_rows128),
    "q": ((128, D_MODEL), _block_rows128),
    "o": ((128, D_MODEL), _block_rows128),
    "gu": ((FF_QUARTER, FF_BLOCK), _block_gu),
    "wd": ((WD_ROWS, FF_QUARTER), _block_wd),
    "gate": ((128, D_MODEL), _block_rows128),
    "proj": ((PLE_DIM, 128), _block_cols128),
}


class _SiblingSwap:
    def __init__(self, pieces):
        self.kinds = [_SCATTERED[kind] for kind, _ in pieces]
        self.args = [g for _, g in pieces]
        self.out_shape = [jax.ShapeDtypeStruct((N_CHIPS, *block), BF16) for block, _ in self.kinds]
        n = len(pieces)
        self.scratch = [pltpu.SemaphoreType.DMA((n, N_CHIPS)), pltpu.SemaphoreType.DMA((n, N_CHIPS))]

    def _copies(self, srcs, outs, sems):
        send_sems, recv_sems = sems
        x, y, c = _my_place()
        return [pltpu.make_async_remote_copy(
            src_ref=block(srcs[t], 2 * ch + 1 - c), dst_ref=outs[t].at[ch], send_sem=send_sems.at[t, ch],
            recv_sem=recv_sems.at[t, ch], device_id=(x, y, 1 - c), device_id_type=MESH)
            for t, (_, block) in enumerate(self.kinds) for ch in range(N_CHIPS)]

    def start(self, srcs, outs, sems):
        for cp in self._copies(srcs, outs, sems):
            cp.start()

    def finish(self, srcs, outs, sems):
        for cp in self._copies(srcs, outs, sems):
            cp.wait()


class _ChipScatter:
    def __init__(self, pieces):
        self.kinds = [_SCATTERED[kind] for kind, _, _ in pieces]
        self.n = n = len(pieces)
        self.args = [g for _, g, _ in pieces] + [s for _, _, s in pieces]
        self.out_shape = [jax.ShapeDtypeStruct((N_CHIPS, *block), BF16) for block, _ in self.kinds]
        self.scratch = []
        for block, _ in self.kinds:
            self.scratch += [pltpu.VMEM((N_CHIPS, *block), BF16)] * 3
        self.scratch += [pltpu.SemaphoreType.DMA((n, N_CHIPS + 1)), pltpu.SemaphoreType.DMA((n, N_CHIPS - 1)),
                         pltpu.SemaphoreType.DMA((n, N_CHIPS - 1)), pltpu.SemaphoreType.DMA((n,))]

    def _sends(self, outs, scr):
        n = self.n
        send_sems, recv_sems, local_sems = scr[3 * n + 1:]
        x, y, c = _my_place()
        chip = 2 * x + y
        copies = []
        for t in range(n):
            total = scr[3 * t + 2]
            copies.append(pltpu.make_async_copy(total.at[chip], outs[t].at[chip], local_sems.at[t]))
            for r in range(1, N_CHIPS):
                to = chip ^ r
                copies.append(pltpu.make_async_remote_copy(
                    src_ref=total.at[to], dst_ref=outs[t].at[chip], send_sem=send_sems.at[t, r - 1],
                    recv_sem=recv_sems.at[t, r - 1], device_id=(to // 2, to % 2, c), device_id_type=MESH))
        return copies

    def start(self, ins, outs, scr):
        n = self.n
        load_sems = scr[3 * n]
        c = lax.axis_index("c")
        loads = []
        for t, (_, block) in enumerate(self.kinds):
            mine, theirs = scr[3 * t], scr[3 * t + 1]
            loads += [pltpu.make_async_copy(block(ins[t], 2 * ch + c), mine.at[ch], load_sems.at[t, ch])
                      for ch in range(N_CHIPS)]
            loads.append(pltpu.make_async_copy(ins[n + t], theirs, load_sems.at[t, N_CHIPS]))
        for cp in loads:
            cp.start()
        for cp in loads:
            cp.wait()
        for t in range(n):
            mine, theirs, total = scr[3 * t:3 * t + 3]
            for ch in range(N_CHIPS):
                total[ch] = (mine[ch].astype(F32) + theirs[ch].astype(F32)).astype(BF16)
        for cp in self._sends(outs, scr):
            cp.start()

    def finish(self, ins, outs, scr):
        for cp in self._sends(outs, scr):
            cp.wait()


class _Jobs:
    def __init__(self, *jobs):
        self.jobs = jobs
        self.args = [a for j in jobs for a in j.args]
        self.out_shape = [o for j in jobs for o in j.out_shape]
        self.scratch = [s for j in jobs for s in j.scratch]

    def _split(self, refs, attr):
        at = 0
        for j in self.jobs:
            n = len(getattr(j, attr))
            yield refs[at:at + n]
            at += n

    def _each(self, ins, outs, scr):
        return zip(self.jobs, self._split(ins, "args"), self._split(outs, "out_shape"), self._split(scr, "scratch"))

    def start(self, ins, outs, scr):
        for j, i, o, s in self._each(ins, outs, scr):
            j.start(i, o, s)

    def finish(self, ins, outs, scr):
        for j, i, o, s in self._each(ins, outs, scr):
            j.finish(i, o, s)

    def split_outputs(self, outs):
        return list(self._split(outs, "out_shape"))


def _adamw_math(w, g, m, v):
    m = ADAM_B1 * m + (1.0 - ADAM_B1) * g
    v = ADAM_B2 * v + (1.0 - ADAM_B2) * (g * g)
    m_hat = m / (1.0 - ADAM_B1 ** ADAM_STEP)
    v_hat = v / (1.0 - ADAM_B2 ** ADAM_STEP)
    delta = -ADAM_LR * (m_hat / (jnp.sqrt(v_hat) + ADAM_EPS) + ADAM_WD * w)
    return delta, m, v


def _adamw(name, w, m, v, landings, n_col_blocks=1, job=None):
    _, r, c = landings[0].shape
    grid = (w.shape[0] // r, n_col_blocks)

    def body(w_ref, m_ref, v_ref, *rest):
        l_refs, (g_ref, d_ref, nm_ref, nv_ref) = rest[:len(landings)], rest[len(landings):]
        step = pl.program_id(0) * n_col_blocks + pl.program_id(1)
        for idx, l_ref in enumerate(l_refs):
            @pl.when(step == idx)
            def _(l_ref=l_ref):
                g = l_ref[0].astype(F32)
                for s in range(1, N_CHIPS):
                    g = g + l_ref[s].astype(F32)
                g_ref[...] = g
                d_ref[...], nm_ref[...], nv_ref[...] = _adamw_math(w_ref[...], g, m_ref[...], v_ref[...])

    spec = pl.BlockSpec((r, c), lambda a, b: (a, b))
    return _launch(
        body, name=f"adamw_{name}", grid=grid,
        in_specs=[spec, spec, spec] + [_full_spec((N_CHIPS, r, c))] * len(landings),
        out_specs=[spec] * 4, out_shape=[jax.ShapeDtypeStruct(w.shape, F32)] * 4,
        args=(w, m, v, *landings), vmem=VMEM_BIG, job=job)


def _small_all_reduce_adamw(part, w, m, v):
    def body(part_ref, w_ref, m_ref, v_ref, g_ref, d_ref, nm_ref, nv_ref, buf, send_sems, recv_sems):
        x, y, c = _my_place()
        me = _dev_index(x, y, c)
        buf[me] = part_ref[...]
        copies = [pltpu.make_async_remote_copy(
            src_ref=part_ref, dst_ref=buf.at[me], send_sem=send_sems.at[r - 1], recv_sem=recv_sems.at[r - 1],
            device_id=_peer_by_relation(r), device_id_type=MESH) for r in range(1, N_DEV)]
        for cp in copies:
            cp.start()
        for cp in copies:
            cp.wait()
        g = buf[0]
        for s in range(1, N_DEV):
            g = g + buf[s]
        g_ref[...] = g
        d_ref[...], nm_ref[...], nv_ref[...] = _adamw_math(w_ref[...], g, m_ref[...], v_ref[...])

    vm = pl.BlockSpec(memory_space=pltpu.VMEM)
    return pl.pallas_call(
        body, name="small_all_reduce_adamw", out_shape=[jax.ShapeDtypeStruct((SV_ROWS, D_MODEL), F32)] * 4,
        in_specs=[vm] * 4, out_specs=[vm] * 4,
        scratch_shapes=[pltpu.VMEM((N_DEV, SV_ROWS, D_MODEL), F32), pltpu.SemaphoreType.DMA((N_DEV - 1,)),
                        pltpu.SemaphoreType.DMA((N_DEV - 1,))],
    )(part, w, m, v)


def _local_step(x, p, tgt, gains, sinks, shards, weights):
    row = _Gain
    gather = lambda *names: _AllGather(names, shards)
    g_pre_mix, g_post_mix = gains["pre_mix_g"], gains["post_mix_g"]
    g_pre_ffn, g_post_ffn = gains["pre_ffn_g"], gains["post_ffn_g"]
    g_ple, g_ple_post, g_kv = gains["ple_g"], gains["ple_post_g"], _Gain(gains["kv_g"], 0)

    wp, scale, wgu0 = _all_gather_only("gather_first", ("pool", "scale", "gu0"), shards)
    (x1_0, h2_0, yraw, dpool), (wd0,) = _fwd_pool_mixer(
        x, row(g_pre_mix, 0), wp, scale, row(g_post_mix, 0), row(g_pre_ffn, 0), job=gather("wd0"))
    (gs0, us0, f0, x2_0, h3_0), (wgate0, wproj0, wkv, wq, wo) = _fwd_ffn(
        0, h2_0, x1_0, wgu0, wd0, row(g_post_ffn, 0), row(g_ple, 0), job=gather("gate0", "proj0", "kv", "q", "o"))
    (x3_0, z0, pe0), _ = _fwd_ple(0, x2_0, h3_0, p[0], wgate0, wproj0, row(g_ple_post, 0))
    (hk, h1, q, kv), _ = _fwd_qkv(x3_0, g_kv, row(g_pre_mix, 1), wkv, wq)
    front = ((ATT_BLOCK, 0), (0, 0))
    kpad = jnp.pad(kv[:, :KV_DIM], front)
    vpad = jnp.pad(kv[:, KV_DIM:], front)
    (attn,), (wgu1,) = _fwd_attention(q, kpad, vpad, sinks, job=gather("gu1"))
    (y1, x1_1, h2_1), (wd1,) = _fwd_attn_out(attn, x3_0, wo, row(g_post_mix, 1), row(g_pre_ffn, 1), job=gather("wd1"))
    (gs1, us1, f1, x2_1, h3_1), (wgate1, wproj1) = _fwd_ffn(
        1, h2_1, x1_1, wgu1, wd1, row(g_post_ffn, 1), row(g_ple, 1), job=gather("gate1", "proj1"))
    (dx3_1, z1, pe1, loss), _ = _fwd_ple(1, x2_1, h3_1, p[1], wgate1, wproj1, row(g_ple_post, 1), target=tgt)

    produced, swapped, landed = {}, {}, {}

    def kind_of(name):
        return name.rstrip("0123_")

    def carry(swap=(), spread=()):
        jobs = []
        if swap:
            jobs.append(_SiblingSwap([(kind_of(n), produced[n]) for n in swap]))
        if spread:
            jobs.append(_ChipScatter([(kind_of(n), produced[n], swapped[n]) for n in spread]))
        return _Jobs(*jobs)

    def carried(jobs, outs, swap=(), spread=()):
        parts = jobs.split_outputs(outs)
        if swap:
            swapped.update(zip(swap, parts[0]))
        if spread:
            landed.update(zip(spread, parts[-1]))

    def hosted(call, *args, swap=(), spread=()):
        jobs = carry(swap, spread)
        outs, job_outs = call(*args, job=jobs)
        carried(jobs, job_outs, swap, spread)
        return outs

    def ffn_weight_grads(layer, h2, df, dg, du, a, hosts):
        for qtr in range(FF_QUARTERS):
            dgu, dwd = hosted(_bwd_ffn_dw, layer, qtr, h2, df, dg, du, a, **hosts[qtr])
            produced[f"gu{layer}_{qtr}"], produced[f"wd{layer}_{qtr}"] = dgu, dwd

    ffn_q = lambda layer, qtr: (f"gu{layer}_{qtr}", f"wd{layer}_{qtr}")

    dx2_1, df1, produced["gate1"], produced["proj1"], dg_ple_post1, dg_ple1, dg_post_ffn1 = hosted(
        _bwd_ple, 1, dx3_1, x2_1, z1, pe1, h3_1, p[1], f1, wgate1, row(g_ple_post, 1), row(g_ple, 1),
        row(g_post_ffn, 1))
    dh2_1, dg1, du1, a1 = hosted(_bwd_ffn_act, 1, df1, gs1, us1, wgu1, wd1, swap=("gate1", "proj1"))
    ffn_weight_grads(1, h2_1, df1, dg1, du1, a1, [
        dict(spread=("gate1", "proj1")), dict(swap=ffn_q(1, 0)), dict(swap=ffn_q(1, 1)), dict(swap=ffn_q(1, 2))])
    dx1_1, dattn, produced["o"], dg_pre_ffn1, dg_post_mix1 = hosted(
        _bwd_attn_out, dx2_1, dh2_1, x1_1, y1, attn, wo, row(g_pre_ffn, 1), row(g_post_mix, 1),
        swap=ffn_q(1, 3), spread=ffn_q(1, 0))
    dq, dkpad, dvpad, dsinks = hosted(_bwd_attention, q, dattn, kpad, vpad, sinks,
                                      spread=ffn_q(1, 1) + ffn_q(1, 2) + ffn_q(1, 3))
    dkv = jnp.concatenate([dkpad[ATT_BLOCK:], dvpad[ATT_BLOCK:]], axis=1).astype(BF16)
    dx3_0, produced["q"], produced["kv"], dg_pre_mix1, dg_kv = hosted(
        _bwd_qkv, dx1_1, dq, dkv, x3_0, h1, hk, wq, wkv, row(g_pre_mix, 1), g_kv, swap=("o",))
    dx2_0, df0, produced["gate0"], produced["proj0"], dg_ple_post0, dg_ple0, dg_post_ffn0 = hosted(
        _bwd_ple, 0, dx3_0, x2_0, z0, pe0, h3_0, p[0], f0, wgate0, row(g_ple_post, 0), row(g_ple, 0),
        row(g_post_ffn, 0), swap=("q", "kv"), spread=("o",))
    dh2_0, dg0, du0, a0 = hosted(_bwd_ffn_act, 0, df0, gs0, us0, wgu0, wd0,
                                 swap=("gate0", "proj0"), spread=("q", "kv"))
    ffn_weight_grads(0, h2_0, df0, dg0, du0, a0, [
        dict(spread=("gate0", "proj0")), dict(swap=ffn_q(0, 0)), dict(swap=ffn_q(0, 1), spread=ffn_q(0, 0)),
        dict(swap=ffn_q(0, 2), spread=ffn_q(0, 1))])
    grad_x, produced["pool"], dscale, dg_pre_ffn0, dg_post_mix0, dg_pre_mix0 = hosted(
        _bwd_pool_mixer, dx2_0, dh2_0, x1_0, x, yraw, dpool, wp, scale, row(g_pre_ffn, 0), row(g_post_mix, 0),
        row(g_pre_mix, 0), swap=ffn_q(0, 3), spread=ffn_q(0, 2))

    def update(name, n_col_blocks=1, pieces=None, swap=(), spread=()):
        w, m, v = weights[name]
        rows = w.size // w.shape[-1]
        flat = [landed[n].reshape(N_CHIPS, -1, landed[n].shape[-1]) for n in (pieces or [kind_short[name]])]
        outs = hosted(_adamw, name, w.reshape(rows, -1), m.reshape(rows, -1), v.reshape(rows, -1), flat,
                      n_col_blocks, swap=swap, spread=spread)
        return [o.reshape(w.shape) for o in outs]

    kind_short = {"w_q": "q", "w_kv": "kv", "w_o": "o", "pool_w": "pool"}
    upd = {}
    upd["w_ple_gate"] = update("w_ple_gate", pieces=("gate0", "gate1"), swap=("pool",), spread=ffn_q(0, 3))
    upd["w_ple_proj"] = update("w_ple_proj", pieces=("proj0", "proj1"), spread=("pool",))
    for name in ("w_q", "w_kv", "w_o", "pool_w"):
        upd[name] = update(name)
    upd["w_gu"] = update("w_gu", pieces=[f"gu{layer}_{qtr}" for layer in range(2) for qtr in range(FF_QUARTERS)])
    upd["w_down"] = update("w_down", FF_QUARTERS,
                           pieces=[f"wd{layer}_{qtr}" for layer in range(2) for qtr in range(FF_QUARTERS)])

    lanes = lambda a: jnp.pad(a, ((0, 0), (0, D_MODEL - a.shape[1])))
    small = jnp.concatenate([
        dg_pre_mix0, dg_pre_mix1, dg_post_mix0, dg_post_mix1, dg_pre_ffn0, dg_pre_ffn1, dg_post_ffn0, dg_post_ffn1,
        dg_ple0, dg_ple1, dg_ple_post0, dg_ple_post1, dg_kv, dscale, lanes(dsinks[:, :N_HEADS]), lanes(loss)], axis=0)
    return grad_x, upd, small


def kernel(x, p, pre_mix_g, post_mix_g, pre_ffn_g, post_ffn_g, pool_w, pool_scale, kv_g, w_kv, w_q, sinks, w_o, w_gu, w_down, ple_g, w_ple_gate, w_ple_proj, ple_post_g, loss_target, m_pre_mix_g, m_post_mix_g, m_pre_ffn_g, m_post_ffn_g, m_pool_w, m_pool_scale, m_kv_g, m_w_kv, m_w_q, m_sinks, m_w_o, m_w_gu, m_w_down, m_ple_g, m_w_ple_gate, m_w_ple_proj, m_ple_post_g, v_pre_mix_g, v_post_mix_g, v_pre_ffn_g, v_post_ffn_g, v_pool_w, v_pool_scale, v_kv_g, v_w_kv, v_w_q, v_sinks, v_w_o, v_w_gu, v_w_down, v_ple_g, v_w_ple_gate, v_w_ple_proj, v_ple_post_g):
    me = _dev_index(*_my_place())

    shards = {"pool": pool_w[0].astype(BF16), "scale": pool_scale, "kv": w_kv.astype(BF16),
              "q": w_q[0].astype(BF16), "o": w_o[0].astype(BF16)}
    for layer in range(2):
        shards[f"gu{layer}"] = w_gu[layer].astype(BF16)
        shards[f"wd{layer}"] = w_down[layer].astype(BF16)
        shards[f"gate{layer}"] = w_ple_gate[layer].astype(BF16)
        shards[f"proj{layer}"] = w_ple_proj[layer].astype(BF16)
    stacked = lambda g: g.reshape(-1, 1, D_MODEL)
    gains = dict(pre_mix_g=stacked(pre_mix_g), post_mix_g=stacked(post_mix_g), pre_ffn_g=stacked(pre_ffn_g),
                 post_ffn_g=stacked(post_ffn_g), ple_g=stacked(ple_g), ple_post_g=stacked(ple_post_g),
                 kv_g=stacked(kv_g))
    weights = {"pool_w": (pool_w, m_pool_w, v_pool_w), "w_kv": (w_kv, m_w_kv, v_w_kv), "w_q": (w_q, m_w_q, v_w_q),
               "w_o": (w_o, m_w_o, v_w_o), "w_gu": (w_gu, m_w_gu, v_w_gu), "w_down": (w_down, m_w_down, v_w_down),
               "w_ple_gate": (w_ple_gate, m_w_ple_gate, v_w_ple_gate),
               "w_ple_proj": (w_ple_proj, m_w_ple_proj, v_w_ple_proj)}
    grad_x, upd, small = _local_step(x[0], p[:, 0], loss_target[0], gains, sinks, shards, weights)

    lane0 = me * 128

    def slab(pre_mix, post_mix, pre_ffn, post_ffn, ple, ple_post, kv, scale_shard, snk):
        scale_row = lax.dynamic_update_slice(jnp.zeros((1, D_MODEL), F32), scale_shard, (0, lane0))
        snk_row = jnp.pad(snk, ((0, 0), (0, D_MODEL - N_HEADS)))
        return jnp.concatenate([pre_mix, post_mix, pre_ffn, post_ffn, ple, ple_post, kv[None, :], scale_row, snk_row,
                                jnp.zeros((1, D_MODEL), F32)], axis=0)

    sw = slab(pre_mix_g, post_mix_g, pre_ffn_g, post_ffn_g, ple_g, ple_post_g, kv_g, pool_scale, sinks)
    sm = slab(m_pre_mix_g, m_post_mix_g, m_pre_ffn_g, m_post_ffn_g, m_ple_g, m_ple_post_g, m_kv_g, m_pool_scale, m_sinks)
    sv = slab(v_pre_mix_g, v_post_mix_g, v_pre_ffn_g, v_post_ffn_g, v_ple_g, v_ple_post_g, v_kv_g, v_pool_scale, v_sinks)
    sg, sd, snm, snv = _small_all_reduce_adamw(small, sw, sm, sv)
    loss = sg[SV_LOSS, 0]

    def unslab(s):
        return {
            "pre_mix_g": s[SV_PRE_MIX:SV_PRE_MIX + 2], "post_mix_g": s[SV_POST_MIX:SV_POST_MIX + 2],
            "pre_ffn_g": s[SV_PRE_FFN:SV_PRE_FFN + 2], "post_ffn_g": s[SV_POST_FFN:SV_POST_FFN + 2],
            "ple_g": s[SV_PLE:SV_PLE + 2], "ple_post_g": s[SV_PLE_POST:SV_PLE_POST + 2], "kv_g": s[SV_KV],
            "pool_scale": lax.dynamic_slice(s, (SV_POOL_SCALE, lane0), (1, 128)),
            "sinks": s[SV_SINKS:SV_SINKS + 1, :N_HEADS],
        }

    names = ["pre_mix_g", "post_mix_g", "pre_ffn_g", "post_ffn_g", "pool_w", "pool_scale", "kv_g", "w_kv", "w_q",
             "sinks", "w_o", "w_gu", "w_down", "ple_g", "w_ple_gate", "w_ple_proj", "ple_post_g"]
    outs = [loss, grad_x[None]]
    for kind, slab_out in enumerate((sg, sd, snm, snv)):
        small_out = unslab(slab_out)
        outs += [upd[n][kind] if n in upd else small_out[n] for n in names]
    return tuple(outs)
```

```python
import functools

import jax
import jax.numpy as jnp
from jax import lax
from jax.experimental import pallas as pl
from jax.experimental.pallas import tpu as pltpu

F32 = jnp.float32
BF16 = jnp.bfloat16

N_DEV = 8
D_MODEL = 1024
N_POOL_GROUPS = 4
POOL_GROUP = 256
POOL_HALO = 16
HEAD_DIM = 64
N_HEADS = 16
N_KV_HEADS = 4
GQA_GROUP = 4
KV_DIM = N_KV_HEADS * HEAD_DIM
ATT_BLOCK = 128
D_FF = 2816
FF_CHUNKS = 4
FF_BLOCK = D_FF // FF_CHUNKS
WD_ROWS = D_FF // N_DEV
FF_PARTS = 2
FF_PART = D_MODEL // FF_PARTS
N_CHIPS = 4
PLE_DIM = 256
EPS = 1e-6
NEG_INF = -1e30
ATT_SCALE = HEAD_DIM ** -0.5

ADAM_LR = 0.001
ADAM_B1 = 0.9
ADAM_B2 = 0.999
ADAM_EPS = 1e-08
ADAM_WD = 0.01
ADAM_STEP = 10

ROW_TILE = 256
FFN_ROW_TILE = 512
VMEM_BIG = 56 * 1024 * 1024
VMEM_MID = 48 * 1024 * 1024
HBM_PIN_ELEMS = 64 * 1024

SV_ROWS = 16
SV_PRE_MIX, SV_POST_MIX, SV_PRE_FFN, SV_POST_FFN, SV_PLE, SV_PLE_POST = 0, 2, 4, 6, 8, 10
SV_KV, SV_POOL_SCALE, SV_SINKS, SV_LOSS = 12, 13, 14, 15

MESH = pl.DeviceIdType.MESH
ANY = pl.BlockSpec(memory_space=pl.ANY)


def _dot(a, b):
    return jnp.dot(a, b, preferred_element_type=F32)


def _dot_nt(a, b):
    return lax.dot_general(a, b, (((1,), (1,)), ((), ())), preferred_element_type=F32)


def _dot_tn(a, b):
    return lax.dot_general(a, b, (((0,), (0,)), ((), ())), preferred_element_type=F32)


def _rstd(x):
    return lax.rsqrt(jnp.mean(x * x, axis=-1, keepdims=True) + EPS)


def _rms(x, g):
    return x * _rstd(x) * g


def _rms_bwd(x, g, dy):
    r = _rstd(x)
    n = x * r
    dn = dy * g
    dx = r * (dn - n * jnp.mean(dn * n, axis=-1, keepdims=True))
    dg = jnp.sum(dy * n, axis=0, keepdims=True)
    return dx, dg


def _sigmoid(x):
    return 1.0 / (1.0 + jnp.exp(-x))


def _acc(ref, val, first):
    @pl.when(first)
    def _():
        ref[...] = val

    @pl.when(jnp.logical_not(first))
    def _():
        ref[...] += val


def _pool_counts(row0, rows):
    t = row0 + lax.broadcasted_iota(jnp.int32, (rows, D_MODEL), 0) + 1
    grp = lax.broadcasted_iota(jnp.int32, (rows, D_MODEL), 1) // POOL_GROUP
    win = jnp.left_shift(2, grp)
    return jnp.minimum(t, win).astype(F32)


def _window_sums(ext, shift_of):
    outs = []
    s = ext
    for gi in range(N_POOL_GROUPS):
        s = s + pltpu.roll(s, shift_of(1 << gi), axis=0)
        outs.append(s[:, :POOL_GROUP])
        s = s[:, POOL_GROUP:]
    return jnp.concatenate(outs, axis=1)


def _cparams(n_axes, vmem):
    return pltpu.CompilerParams(dimension_semantics=("arbitrary",) * n_axes, vmem_limit_bytes=vmem)


def _row_spec(cols, tm=ROW_TILE):
    return pl.BlockSpec((tm, cols), lambda i: (i, 0))


def _full_spec(shape):
    zeros = (0,) * len(shape)
    return pl.BlockSpec(shape, lambda *_: zeros)


def _vec_spec():
    return _full_spec((1, D_MODEL))


class _Gain:
    def __init__(self, stacked, layer):
        self.stacked, self.layer = stacked, layer

    def spec(self):
        layer = self.layer
        return pl.BlockSpec((None, 1, D_MODEL), lambda *_: (layer, 0, 0))


def _in_hbm(a):
    return pltpu.with_memory_space_constraint(a, pltpu.HBM) if a.size >= HBM_PIN_ELEMS else a


def _launch(body, *, name, grid, in_specs, out_specs, out_shape, args, scratch_shapes=(), vmem=VMEM_MID, job=None):
    in_specs = [a.spec() if isinstance(a, _Gain) else s for s, a in zip(in_specs, args)]
    args = [a.stacked if isinstance(a, _Gain) else _in_hbm(a) for a in args]
    n_in, n_out, n_scr = len(args), len(out_shape), len(scratch_shapes)
    j_args, j_out, j_scr = ([], [], []) if job is None else ([_in_hbm(a) for a in job.args], job.out_shape, job.scratch)

    def run(*refs):
        groups, at = [], 0
        for n in (n_in, len(j_args), n_out, len(j_out), n_scr, len(j_scr)):
            groups.append(refs[at:at + n])
            at += n
        ins, j_ins, outs, j_outs, scr, j_sems = groups
        if job is None:
            body(*ins, *outs, *scr)
        elif not grid:
            job.start(j_ins, j_outs, j_sems)
            body(*ins, *outs, *scr)
            job.finish(j_ins, j_outs, j_sems)
        else:
            ids = [pl.program_id(a) for a in range(len(grid))]
            first = functools.reduce(jnp.logical_and, [i == 0 for i in ids])
            last = functools.reduce(jnp.logical_and, [i == g - 1 for i, g in zip(ids, grid)])
            pl.when(first)(lambda: job.start(j_ins, j_outs, j_sems))
            body(*ins, *outs, *scr)
            pl.when(last)(lambda: job.finish(j_ins, j_outs, j_sems))

    res = pl.pallas_call(
        run, name=name, grid=grid,
        in_specs=list(in_specs) + [ANY] * len(j_args), out_specs=list(out_specs) + [ANY] * len(j_out),
        out_shape=list(out_shape) + list(j_out), scratch_shapes=list(scratch_shapes) + list(j_scr),
        compiler_params=_cparams(len(grid), vmem),
    )(*args, *j_args)
    return res[:n_out], res[n_out:]


def _fwd_pool_mixer(x, g_pre, wp, scale, g_post, g_ffn, job=None):
    T = x.shape[0]
    tm = ROW_TILE
    nt = T // tm

    def body(x_ref, gpre_ref, wp_ref, sc_ref, gpost_ref, gffn_ref, x1_ref, h2_ref, yraw_ref, d_ref, carry):
        i = pl.program_id(0)

        @pl.when(i == 0)
        def _():
            carry[...] = jnp.zeros_like(carry)

        xv = x_ref[...]
        h = _rms(xv, gpre_ref[...])
        ext = jnp.concatenate([carry[...], h], axis=0)
        carry[...] = h[tm - POOL_HALO:, :]
        sums = _window_sums(ext, lambda k: k)[POOL_HALO:, :]
        d = sums / _pool_counts(i * tm, tm) - h
        db = d.astype(BF16)
        d_ref[...] = db
        yraw = jnp.concatenate(
            [_dot(db[:, g * POOL_GROUP:(g + 1) * POOL_GROUP], wp_ref[g]) for g in range(N_POOL_GROUPS)], axis=1)
        yraw_ref[...] = yraw
        x1 = xv + _rms(yraw * sc_ref[...], gpost_ref[...])
        x1_ref[...] = x1
        h2_ref[...] = _rms(x1, gffn_ref[...]).astype(BF16)

    return _launch(
        body, name="fwd_pool_mixer", grid=(nt,),
        in_specs=[_row_spec(D_MODEL), _vec_spec(), _full_spec((N_POOL_GROUPS, POOL_GROUP, POOL_GROUP)), _vec_spec(),
                  _vec_spec(), _vec_spec()],
        out_specs=[_row_spec(D_MODEL)] * 4,
        out_shape=[jax.ShapeDtypeStruct((T, D_MODEL), F32), jax.ShapeDtypeStruct((T, D_MODEL), BF16),
                   jax.ShapeDtypeStruct((T, D_MODEL), F32), jax.ShapeDtypeStruct((T, D_MODEL), BF16)],
        scratch_shapes=[pltpu.VMEM((POOL_HALO, D_MODEL), F32)],
        args=(x, g_pre, wp, scale, g_post, g_ffn), job=job)


def _fwd_ffn(layer, h2, x1, wgu, wd, g_post, g_ple, job=None):
    T = h2.shape[0]
    tm = min(FFN_ROW_TILE, T)
    nt = T // tm
    last = FF_CHUNKS - 1

    def body(h2_ref, x1_ref, wgu_ref, wd_ref, gpost_ref, gple_ref, gs_ref, us_ref, f_ref, x2_ref, h3_ref, acc):
        k = pl.program_id(0)
        i = pl.program_id(1)
        rows = pl.ds(pl.multiple_of(i * tm, tm), tm)
        h = h2_ref[...]
        g = _dot(h, wgu_ref[0])
        u = _dot(h, wgu_ref[1])
        gs_ref[...] = g.astype(BF16)
        us_ref[...] = u.astype(BF16)
        a = (g * _sigmoid(g) * u).astype(BF16)
        part = _dot(a, wd_ref[...])

        @pl.when(k == 0)
        def _():
            acc[rows, :] = part

        @pl.when(jnp.logical_and(k > 0, k < last))
        def _():
            acc[rows, :] += part

        @pl.when(k == last)
        def _():
            f = acc[rows, :] + part
            f_ref[...] = f
            x2 = x1_ref[...] + _rms(f, gpost_ref[...])
            x2_ref[...] = x2
            h3_ref[...] = _rms(x2, gple_ref[...]).astype(BF16)

    def late(k, i):
        return (jnp.where(k == last, i, 0), 0)

    return _launch(
        body, name=f"fwd_ffn{layer}", grid=(FF_CHUNKS, nt),
        in_specs=[pl.BlockSpec((tm, D_MODEL), lambda k, i: (i, 0)),
                  pl.BlockSpec((tm, D_MODEL), late),
                  pl.BlockSpec((None, 2, D_MODEL, FF_BLOCK), lambda k, i: (k, 0, 0, 0)),
                  pl.BlockSpec((FF_BLOCK, D_MODEL), lambda k, i: (k, 0)),
                  pl.BlockSpec((1, D_MODEL), lambda k, i: (0, 0)),
                  pl.BlockSpec((1, D_MODEL), lambda k, i: (0, 0))],
        out_specs=[pl.BlockSpec((None, tm, FF_BLOCK), lambda k, i: (k, i, 0)),
                   pl.BlockSpec((None, tm, FF_BLOCK), lambda k, i: (k, i, 0)),
                   pl.BlockSpec((tm, D_MODEL), late),
                   pl.BlockSpec((tm, D_MODEL), late),
                   pl.BlockSpec((tm, D_MODEL), late)],
        out_shape=[jax.ShapeDtypeStruct((FF_CHUNKS, T, FF_BLOCK), BF16),
                   jax.ShapeDtypeStruct((FF_CHUNKS, T, FF_BLOCK), BF16),
                   jax.ShapeDtypeStruct((T, D_MODEL), F32),
                   jax.ShapeDtypeStruct((T, D_MODEL), F32),
                   jax.ShapeDtypeStruct((T, D_MODEL), BF16)],
        scratch_shapes=[pltpu.VMEM((T, D_MODEL), F32)],
        args=(h2, x1, wgu, wd, g_post, g_ple), vmem=VMEM_BIG, job=job)


def _fwd_ple(layer, x2, h3, p, wgate, wproj, g_post, target=None, job=None):
    T = x2.shape[0]
    tm = ROW_TILE
    nt = T // tm
    with_loss = target is not None

    def body(*refs):
        if with_loss:
            x2_ref, h3_ref, p_ref, wg_ref, wp_ref, gpost_ref, tgt_ref, out_ref, z_ref, pe_ref, loss_ref = refs
        else:
            x2_ref, h3_ref, p_ref, wg_ref, wp_ref, gpost_ref, out_ref, z_ref, pe_ref = refs
        z = _dot(h3_ref[...], wg_ref[...])
        pe = _dot(p_ref[...].astype(BF16), wp_ref[...])
        z_ref[...] = z
        pe_ref[...] = pe
        x3 = x2_ref[...] + _rms(pe * _sigmoid(z), gpost_ref[...])
        if with_loss:
            err = x3 - tgt_ref[...]
            out_ref[...] = err * (1.0 / D_MODEL)
            part = 0.5 * jnp.sum(jnp.mean(err * err, axis=-1, keepdims=True), axis=0, keepdims=True)
            _acc(loss_ref, part, pl.program_id(0) == 0)
        else:
            out_ref[...] = x3

    in_specs = [_row_spec(D_MODEL), _row_spec(D_MODEL), _row_spec(PLE_DIM), _full_spec((D_MODEL, D_MODEL)),
                _full_spec((PLE_DIM, D_MODEL)), _vec_spec()]
    out_specs = [_row_spec(D_MODEL)] * 3
    out_shape = [jax.ShapeDtypeStruct((T, D_MODEL), F32)] * 3
    args = [x2, h3, p, wgate, wproj, g_post]
    if with_loss:
        in_specs.append(_row_spec(D_MODEL))
        out_specs.append(_full_spec((1, 1)))
        out_shape.append(jax.ShapeDtypeStruct((1, 1), F32))
        args.append(target)
    return _launch(body, name=f"fwd_ple{layer}", grid=(nt,), in_specs=in_specs, out_specs=out_specs,
                   out_shape=out_shape, args=args, job=job)


def _fwd_qkv(x3, g_kv, g_mix, wkv, wq, job=None):
    T = x3.shape[0]
    nt = T // ROW_TILE

    def body(x_ref, gkv_ref, gmix_ref, wkv_ref, wq_ref, hk_ref, h1_ref, q_ref, kv_ref):
        xv = x_ref[...]
        r = _rstd(xv)
        hk = (xv * r * gkv_ref[...]).astype(BF16)
        h1 = (xv * r * gmix_ref[...]).astype(BF16)
        hk_ref[...] = hk
        h1_ref[...] = h1
        kv_ref[...] = _dot(hk, wkv_ref[...]).astype(BF16)
        q_ref[...] = _dot(h1, wq_ref[...]).astype(BF16)

    return _launch(
        body, name="fwd_qkv", grid=(nt,),
        in_specs=[_row_spec(D_MODEL), _vec_spec(), _vec_spec(), _full_spec((D_MODEL, 2 * KV_DIM)),
                  _full_spec((D_MODEL, D_MODEL))],
        out_specs=[_row_spec(D_MODEL), _row_spec(D_MODEL), _row_spec(D_MODEL), _row_spec(2 * KV_DIM)],
        out_shape=[jax.ShapeDtypeStruct((T, D_MODEL), BF16)] * 3 + [jax.ShapeDtypeStruct((T, 2 * KV_DIM), BF16)],
        args=(x3, g_kv, g_mix, wkv, wq), job=job)


def _alibi_slope(h):
    return 2.0 ** (-8.0 * (h + 1) / N_HEADS)


def _att_mask(n):
    qi = lax.broadcasted_iota(jnp.int32, (ATT_BLOCK, 2 * ATT_BLOCK), 0)
    si = lax.broadcasted_iota(jnp.int32, (ATT_BLOCK, 2 * ATT_BLOCK), 1)
    rel = ATT_BLOCK + qi - si
    valid = (rel >= 0) & (rel < ATT_BLOCK) & ((si >= ATT_BLOCK) | (n > 0))
    return rel.astype(F32), valid


def _att_probs(qh, kk, relf, valid, slope, sink):
    s = _dot_nt(qh, kk) * ATT_SCALE
    s = jnp.where(valid, s - slope * relf, NEG_INF)
    m = jnp.maximum(jnp.max(s, axis=-1, keepdims=True), sink)
    e = jnp.exp(s - m)
    es = jnp.exp(sink - m)
    inv = 1.0 / (jnp.sum(e, axis=-1, keepdims=True) + es)
    return e * inv, es * inv


def _fwd_attention(q, kpad, vpad, sinks, job=None):
    T = q.shape[0]
    nb = T // ATT_BLOCK

    def body(q_ref, k_ref, v_ref, sink_ref, o_ref):
        n = pl.program_id(0)
        start = pl.multiple_of(n * ATT_BLOCK, ATT_BLOCK)
        kw = k_ref[pl.ds(start, 2 * ATT_BLOCK), :]
        vw = v_ref[pl.ds(start, 2 * ATT_BLOCK), :]
        relf, valid = _att_mask(n)
        outs = []
        for h in range(N_HEADS):
            kh = h // GQA_GROUP
            qh = q_ref[:, h * HEAD_DIM:(h + 1) * HEAD_DIM]
            kk = kw[:, kh * HEAD_DIM:(kh + 1) * HEAD_DIM]
            vv = vw[:, kh * HEAD_DIM:(kh + 1) * HEAD_DIM]
            pr, _ = _att_probs(qh, kk, relf, valid, _alibi_slope(h), sink_ref[0, h])
            outs.append(_dot(pr.astype(BF16), vv))
        o_ref[...] = jnp.concatenate(outs, axis=1).astype(BF16)

    return _launch(
        body, name="fwd_attention", grid=(nb,),
        in_specs=[_row_spec(D_MODEL, ATT_BLOCK), _full_spec((T + ATT_BLOCK, KV_DIM)), _full_spec((T + ATT_BLOCK, KV_DIM)),
                  pl.BlockSpec(memory_space=pltpu.SMEM)],
        out_specs=[_row_spec(D_MODEL, ATT_BLOCK)],
        out_shape=[jax.ShapeDtypeStruct((T, D_MODEL), BF16)],
        args=(q, kpad, vpad, sinks), job=job)


def _fwd_attn_out(attn, x, wo, g_post, g_ffn, job=None):
    T = x.shape[0]
    nt = T // ROW_TILE

    def body(a_ref, x_ref, wo_ref, gpost_ref, gffn_ref, y_ref, x1_ref, h2_ref):
        y = _dot(a_ref[...], wo_ref[...])
        y_ref[...] = y
        x1 = x_ref[...] + _rms(y, gpost_ref[...])
        x1_ref[...] = x1
        h2_ref[...] = _rms(x1, gffn_ref[...]).astype(BF16)

    return _launch(
        body, name="fwd_attn_out", grid=(nt,),
        in_specs=[_row_spec(D_MODEL), _row_spec(D_MODEL), _full_spec((D_MODEL, D_MODEL)), _vec_spec(), _vec_spec()],
        out_specs=[_row_spec(D_MODEL)] * 3,
        out_shape=[jax.ShapeDtypeStruct((T, D_MODEL), F32), jax.ShapeDtypeStruct((T, D_MODEL), F32),
                   jax.ShapeDtypeStruct((T, D_MODEL), BF16)],
        args=(attn, x, wo, g_post, g_ffn), job=job)


def _bwd_ple(layer, dx3, x2, z, pe, h3, p, f, wgate, g_ple_post, g_ple, g_post_ffn, job=None):
    T = x2.shape[0]
    tm = ROW_TILE
    nt = T // tm

    def body(dx3_ref, x2_ref, z_ref, pe_ref, h3_ref, p_ref, f_ref, wg_ref, gpp_ref, gp_ref, gpf_ref,
             dx2_ref, df_ref, dwg_ref, dwp_ref, dgpp_ref, dgp_ref, dgpf_ref, acc_g, acc_p):
        i = pl.program_id(0)
        first = i == 0
        dx3v = dx3_ref[...]
        gate = _sigmoid(z_ref[...])
        pev = pe_ref[...]
        de, dgpp = _rms_bwd(pev * gate, gpp_ref[...], dx3v)
        dpe = (de * gate).astype(BF16)
        dz = (de * pev * gate * (1.0 - gate)).astype(BF16)
        _acc(acc_p, _dot_tn(p_ref[...].astype(BF16), dpe), first)
        _acc(acc_g, _dot_tn(h3_ref[...], dz), first)
        dh3 = _dot_nt(dz, wg_ref[...])
        dxn, dgp = _rms_bwd(x2_ref[...], gp_ref[...], dh3)
        dx2 = dx3v + dxn
        dx2_ref[...] = dx2
        df, dgpf = _rms_bwd(f_ref[...], gpf_ref[...], dx2)
        df_ref[...] = df.astype(BF16)
        _acc(dgpp_ref, dgpp, first)
        _acc(dgp_ref, dgp, first)
        _acc(dgpf_ref, dgpf, first)

        @pl.when(i == nt - 1)
        def _():
            dwg_ref[...] = acc_g[...].astype(BF16)
            dwp_ref[...] = acc_p[...].astype(BF16)

    return _launch(
        body, name=f"bwd_ple{layer}", grid=(nt,),
        in_specs=[_row_spec(D_MODEL)] * 5 + [_row_spec(PLE_DIM), _row_spec(D_MODEL), _full_spec((D_MODEL, D_MODEL)),
                  _vec_spec(), _vec_spec(), _vec_spec()],
        out_specs=[_row_spec(D_MODEL), _row_spec(D_MODEL), _full_spec((D_MODEL, D_MODEL)), _full_spec((PLE_DIM, D_MODEL)),
                   _vec_spec(), _vec_spec(), _vec_spec()],
        out_shape=[jax.ShapeDtypeStruct((T, D_MODEL), F32), jax.ShapeDtypeStruct((T, D_MODEL), BF16),
                   jax.ShapeDtypeStruct((D_MODEL, D_MODEL), BF16), jax.ShapeDtypeStruct((PLE_DIM, D_MODEL), BF16)]
                  + [jax.ShapeDtypeStruct((1, D_MODEL), F32)] * 3,
        scratch_shapes=[pltpu.VMEM((D_MODEL, D_MODEL), F32), pltpu.VMEM((PLE_DIM, D_MODEL), F32)],
        args=(dx3, x2, z, pe, h3, p, f, wgate, g_ple_post, g_ple, g_post_ffn), job=job)


def _bwd_ffn_act(layer, df, gs, us, wgu, wd, job=None):
    T = df.shape[0]
    tm = min(FFN_ROW_TILE, T)
    nt = T // tm
    last = FF_CHUNKS - 1

    def body(df_ref, gs_ref, us_ref, wgu_ref, wd_ref, dh_ref, dg_ref, du_ref, a_ref, acc_h):
        k = pl.program_id(0)
        i = pl.program_id(1)
        rows = pl.ds(pl.multiple_of(i * tm, tm), tm)
        g = gs_ref[...].astype(F32)
        u = us_ref[...].astype(F32)
        sg = _sigmoid(g)
        silu = g * sg
        a_ref[...] = (silu * u).astype(BF16)
        da = _dot_nt(df_ref[...], wd_ref[...])
        dg = (da * u * (sg * (1.0 + g * (1.0 - sg)))).astype(BF16)
        du = (da * silu).astype(BF16)
        dg_ref[...] = dg
        du_ref[...] = du
        dh = _dot_nt(dg, wgu_ref[0]) + _dot_nt(du, wgu_ref[1])

        @pl.when(k == 0)
        def _():
            acc_h[rows, :] = dh

        @pl.when(jnp.logical_and(k > 0, k < last))
        def _():
            acc_h[rows, :] += dh

        @pl.when(k == last)
        def _():
            dh_ref[...] = acc_h[rows, :] + dh

    chunk_rows = pl.BlockSpec((None, tm, FF_BLOCK), lambda k, i: (k, i, 0))
    saved = jax.ShapeDtypeStruct((FF_CHUNKS, T, FF_BLOCK), BF16)
    return _launch(
        body, name=f"bwd_ffn_act{layer}", grid=(FF_CHUNKS, nt),
        in_specs=[pl.BlockSpec((tm, D_MODEL), lambda k, i: (i, 0)), chunk_rows, chunk_rows,
                  pl.BlockSpec((None, 2, D_MODEL, FF_BLOCK), lambda k, i: (k, 0, 0, 0)),
                  pl.BlockSpec((FF_BLOCK, D_MODEL), lambda k, i: (k, 0))],
        out_specs=[pl.BlockSpec((tm, D_MODEL), lambda k, i: (jnp.where(k == last, i, 0), 0)),
                   chunk_rows, chunk_rows, chunk_rows],
        out_shape=[jax.ShapeDtypeStruct((T, D_MODEL), F32), saved, saved, saved],
        scratch_shapes=[pltpu.VMEM((T, D_MODEL), F32)],
        args=(df, gs, us, wgu, wd), vmem=VMEM_BIG, job=job)


def _bwd_ffn_dw(layer, q, h2, df, dg, du, a, job=None):
    T = h2.shape[0]

    def body(h_ref, df_ref, dg_ref, du_ref, a_ref, dgu_ref, dwd_ref):
        h = h_ref[...]
        dgu_ref[0] = _dot_tn(h, dg_ref[...]).astype(BF16)
        dgu_ref[1] = _dot_tn(h, du_ref[...]).astype(BF16)
        dwd_ref[...] = _dot_tn(a_ref[...], df_ref[...]).astype(BF16)

    cols = pl.BlockSpec((T, FF_PART), lambda k: (0, q))
    chunk = pl.BlockSpec((None, T, FF_BLOCK), lambda k: (k, 0, 0))
    return _launch(
        body, name=f"bwd_ffn_dw{layer}_{q}", grid=(FF_CHUNKS,),
        in_specs=[cols, cols, chunk, chunk, chunk],
        out_specs=[pl.BlockSpec((None, 2, FF_PART, FF_BLOCK), lambda k: (k, 0, 0, 0)),
                   pl.BlockSpec((FF_BLOCK, FF_PART), lambda k: (k, 0))],
        out_shape=[jax.ShapeDtypeStruct((FF_CHUNKS, 2, FF_PART, FF_BLOCK), BF16),
                   jax.ShapeDtypeStruct((D_FF, FF_PART), BF16)],
        args=(h2, df, dg, du, a), vmem=VMEM_BIG, job=job)


def _bwd_attn_out(dx2, dh2, x1, y, attn, wo, g_ffn, g_post, job=None):
    T = x1.shape[0]
    nt = T // ROW_TILE

    def body(dx2_ref, dh2_ref, x1_ref, y_ref, a_ref, wo_ref, gffn_ref, gpost_ref,
             dx1_ref, da_ref, dwo_ref, dgf_ref, dgp_ref, acc):
        i = pl.program_id(0)
        first = i == 0
        dxn, dgf = _rms_bwd(x1_ref[...], gffn_ref[...], dh2_ref[...])
        dx1 = dx2_ref[...] + dxn
        dx1_ref[...] = dx1
        dy, dgp = _rms_bwd(y_ref[...], gpost_ref[...], dx1)
        dyb = dy.astype(BF16)
        da_ref[...] = _dot_nt(dyb, wo_ref[...]).astype(BF16)
        _acc(acc, _dot_tn(a_ref[...], dyb), first)
        _acc(dgf_ref, dgf, first)
        _acc(dgp_ref, dgp, first)

        @pl.when(i == nt - 1)
        def _():
            dwo_ref[...] = acc[...].astype(BF16)

    return _launch(
        body, name="bwd_attn_out", grid=(nt,),
        in_specs=[_row_spec(D_MODEL)] * 5 + [_full_spec((D_MODEL, D_MODEL)), _vec_spec(), _vec_spec()],
        out_specs=[_row_spec(D_MODEL), _row_spec(D_MODEL), _full_spec((D_MODEL, D_MODEL)), _vec_spec(), _vec_spec()],
        out_shape=[jax.ShapeDtypeStruct((T, D_MODEL), F32), jax.ShapeDtypeStruct((T, D_MODEL), BF16),
                   jax.ShapeDtypeStruct((D_MODEL, D_MODEL), BF16)] + [jax.ShapeDtypeStruct((1, D_MODEL), F32)] * 2,
        scratch_shapes=[pltpu.VMEM((D_MODEL, D_MODEL), F32)],
        args=(dx2, dh2, x1, y, attn, wo, g_ffn, g_post), job=job)


def _bwd_attention(q, dattn, kpad, vpad, sinks, job=None):
    T = q.shape[0]
    nb = T // ATT_BLOCK

    def body(q_ref, do_ref, k_ref, v_ref, sink_ref, dq_ref, dk_ref, dv_ref, ds_ref):
        n = pl.program_id(0)

        @pl.when(n == 0)
        def _():
            dk_ref[...] = jnp.zeros_like(dk_ref)
            dv_ref[...] = jnp.zeros_like(dv_ref)
            ds_ref[...] = jnp.zeros_like(ds_ref)

        start = pl.multiple_of(n * ATT_BLOCK, ATT_BLOCK)
        win = pl.ds(start, 2 * ATT_BLOCK)
        kw = k_ref[win, :]
        vw = v_ref[win, :]
        relf, valid = _att_mask(n)
        lane = lax.broadcasted_iota(jnp.int32, (1, ATT_BLOCK), 1)
        dsink = jnp.zeros((1, ATT_BLOCK), F32)
        dqs, dks, dvs = [], [], []
        for kh in range(N_KV_HEADS):
            kk = kw[:, kh * HEAD_DIM:(kh + 1) * HEAD_DIM]
            vv = vw[:, kh * HEAD_DIM:(kh + 1) * HEAD_DIM]
            dk_h = jnp.zeros((2 * ATT_BLOCK, HEAD_DIM), F32)
            dv_h = jnp.zeros((2 * ATT_BLOCK, HEAD_DIM), F32)
            for gq in range(GQA_GROUP):
                h = kh * GQA_GROUP + gq
                qh = q_ref[:, h * HEAD_DIM:(h + 1) * HEAD_DIM]
                do = do_ref[:, h * HEAD_DIM:(h + 1) * HEAD_DIM]
                pr, ps = _att_probs(qh, kk, relf, valid, _alibi_slope(h), sink_ref[0, h])
                dp = _dot_nt(do, vv)
                delta = jnp.sum(pr * dp, axis=-1, keepdims=True)
                dsb = (pr * (dp - delta) * ATT_SCALE).astype(BF16)
                dsink = dsink + jnp.where(lane == h, -jnp.sum(ps * delta, axis=0, keepdims=True), 0.0)
                dqs.append(_dot(dsb, kk))
                dk_h = dk_h + _dot_tn(dsb, qh)
                dv_h = dv_h + _dot_tn(pr.astype(BF16), do)
            dks.append(dk_h)
            dvs.append(dv_h)
        dq_ref[...] = jnp.concatenate(dqs, axis=1).astype(BF16)
        dk_ref[win, :] += jnp.concatenate(dks, axis=1)
        dv_ref[win, :] += jnp.concatenate(dvs, axis=1)
        ds_ref[...] += dsink

    return _launch(
        body, name="bwd_attention", grid=(nb,),
        in_specs=[_row_spec(D_MODEL, ATT_BLOCK), _row_spec(D_MODEL, ATT_BLOCK), _full_spec((T + ATT_BLOCK, KV_DIM)),
                  _full_spec((T + ATT_BLOCK, KV_DIM)), pl.BlockSpec(memory_space=pltpu.SMEM)],
        out_specs=[_row_spec(D_MODEL, ATT_BLOCK), _full_spec((T + ATT_BLOCK, KV_DIM)), _full_spec((T + ATT_BLOCK, KV_DIM)),
                   _full_spec((1, ATT_BLOCK))],
        out_shape=[jax.ShapeDtypeStruct((T, D_MODEL), BF16), jax.ShapeDtypeStruct((T + ATT_BLOCK, KV_DIM), F32),
                   jax.ShapeDtypeStruct((T + ATT_BLOCK, KV_DIM), F32), jax.ShapeDtypeStruct((1, ATT_BLOCK), F32)],
        args=(q, dattn, kpad, vpad, sinks), vmem=VMEM_BIG, job=job)


def _bwd_qkv(dxres, dq, dkv, x3, h1, hk, wq, wkv, g_mix, g_kv, job=None):
    T = x3.shape[0]
    nt = T // ROW_TILE

    def body(dxr_ref, dq_ref, dkv_ref, x_ref, h1_ref, hk_ref, wq_ref, wkv_ref, gmix_ref, gkv_ref,
             dx_ref, dwq_ref, dwkv_ref, dgm_ref, dgk_ref, acc_q, acc_kv):
        i = pl.program_id(0)
        first = i == 0
        dqv = dq_ref[...]
        dkvv = dkv_ref[...]
        xv = x_ref[...]
        d1, dgm = _rms_bwd(xv, gmix_ref[...], _dot_nt(dqv, wq_ref[...]))
        d2, dgk = _rms_bwd(xv, gkv_ref[...], _dot_nt(dkvv, wkv_ref[...]))
        dx_ref[...] = dxr_ref[...] + d1 + d2
        _acc(acc_q, _dot_tn(h1_ref[...], dqv), first)
        _acc(acc_kv, _dot_tn(hk_ref[...], dkvv), first)
        _acc(dgm_ref, dgm, first)
        _acc(dgk_ref, dgk, first)

        @pl.when(i == nt - 1)
        def _():
            dwq_ref[...] = acc_q[...].astype(BF16)
            dwkv_ref[...] = acc_kv[...].astype(BF16)

    return _launch(
        body, name="bwd_qkv", grid=(nt,),
        in_specs=[_row_spec(D_MODEL), _row_spec(D_MODEL), _row_spec(2 * KV_DIM), _row_spec(D_MODEL), _row_spec(D_MODEL),
                  _row_spec(D_MODEL), _full_spec((D_MODEL, D_MODEL)), _full_spec((D_MODEL, 2 * KV_DIM)), _vec_spec(),
                  _vec_spec()],
        out_specs=[_row_spec(D_MODEL), _full_spec((D_MODEL, D_MODEL)), _full_spec((D_MODEL, 2 * KV_DIM)), _vec_spec(),
                   _vec_spec()],
        out_shape=[jax.ShapeDtypeStruct((T, D_MODEL), F32), jax.ShapeDtypeStruct((D_MODEL, D_MODEL), BF16),
                   jax.ShapeDtypeStruct((D_MODEL, 2 * KV_DIM), BF16)] + [jax.ShapeDtypeStruct((1, D_MODEL), F32)] * 2,
        scratch_shapes=[pltpu.VMEM((D_MODEL, D_MODEL), F32), pltpu.VMEM((D_MODEL, 2 * KV_DIM), F32)],
        args=(dxres, dq, dkv, x3, h1, hk, wq, wkv, g_mix, g_kv), job=job)


def _bwd_pool_mixer(dx2, dh2, x1, x, yraw, d, wp, scale, g_ffn, g_post, g_pre, job=None):
    T = x.shape[0]
    tm = ROW_TILE
    nt = T // tm

    def body(dx2_ref, dh2_ref, x1_ref, x_ref, yraw_ref, d_ref, wp_ref, sc_ref, gffn_ref, gpost_ref, gpre_ref,
             dx_ref, dwp_ref, dsc_ref, dgf_ref, dgp_ref, dgm_ref, carry, acc):
        i = pl.program_id(0)
        first = i == 0
        tile = nt - 1 - i

        @pl.when(first)
        def _():
            carry[...] = jnp.zeros_like(carry)

        dxn, dgf = _rms_bwd(x1_ref[...], gffn_ref[...], dh2_ref[...])
        dx1 = dx2_ref[...] + dxn
        yraw = yraw_ref[...]
        sc = sc_ref[...]
        dy, dgp = _rms_bwd(yraw * sc, gpost_ref[...], dx1)
        dsc = jnp.sum(dy * yraw, axis=0, keepdims=True)
        dyb = (dy * sc).astype(BF16)
        dv = d_ref[...]
        dds = []
        for g in range(N_POOL_GROUPS):
            cols = slice(g * POOL_GROUP, (g + 1) * POOL_GROUP)
            dds.append(_dot_nt(dyb[:, cols], wp_ref[g]))
            _acc(acc.at[g], _dot_tn(dv[:, cols], dyb[:, cols]), first)
        dd = jnp.concatenate(dds, axis=1)
        e = dd / _pool_counts(tile * tm, tm)
        ext = jnp.concatenate([e, carry[...]], axis=0)
        carry[...] = e[:POOL_HALO, :]
        sums = _window_sums(ext, lambda k: tm + POOL_HALO - k)[:tm, :]
        dxm, dgm = _rms_bwd(x_ref[...], gpre_ref[...], sums - dd)
        dx_ref[...] = dx1 + dxm
        _acc(dsc_ref, dsc, first)
        _acc(dgf_ref, dgf, first)
        _acc(dgp_ref, dgp, first)
        _acc(dgm_ref, dgm, first)

        @pl.when(i == nt - 1)
        def _():
            dwp_ref[...] = acc[...].astype(BF16)

    rev = pl.BlockSpec((tm, D_MODEL), lambda i: (nt - 1 - i, 0))
    return _launch(
        body, name="bwd_pool_mixer", grid=(nt,),
        in_specs=[rev] * 6 + [_full_spec((N_POOL_GROUPS, POOL_GROUP, POOL_GROUP))] + [_vec_spec()] * 4,
        out_specs=[rev, _full_spec((N_POOL_GROUPS, POOL_GROUP, POOL_GROUP))] + [_vec_spec()] * 4,
        out_shape=[jax.ShapeDtypeStruct((T, D_MODEL), F32),
                   jax.ShapeDtypeStruct((N_POOL_GROUPS, POOL_GROUP, POOL_GROUP), BF16)]
                  + [jax.ShapeDtypeStruct((1, D_MODEL), F32)] * 4,
        scratch_shapes=[pltpu.VMEM((POOL_HALO, D_MODEL), F32), pltpu.VMEM((N_POOL_GROUPS, POOL_GROUP, POOL_GROUP), F32)],
        args=(dx2, dh2, x1, x, yraw, d, wp, scale, g_ffn, g_post, g_pre), job=job)


def _my_place():
    return lax.axis_index("x"), lax.axis_index("y"), lax.axis_index("c")


def _dev_index(px, py, pc):
    return 4 * px + 2 * py + pc


def _peer_by_relation(r):
    x, y, c = _my_place()
    return (x ^ ((r >> 2) & 1), y ^ ((r >> 1) & 1), c ^ (r & 1))


def _slot_pool(ref, j):
    return ref.at[:, pl.ds(pl.multiple_of(j * 32, 32), 32), :]


def _slot_scale(ref, j):
    return ref.at[:, pl.ds(pl.multiple_of(j * 128, 128), 128)]


def _slot_rows128(ref, j):
    return ref.at[pl.ds(pl.multiple_of(j * 128, 128), 128), :]


def _slot_gu(ref, j):
    return ref.at[j % FF_CHUNKS, j // FF_CHUNKS]


def _slot_wd(ref, j):
    return ref.at[pl.ds(pl.multiple_of(j * WD_ROWS, 16), WD_ROWS), :]


def _slot_cols128(ref, j):
    return ref.at[:, pl.ds(pl.multiple_of(j * 128, 128), 128)]


_GATHERED = {
    "pool": ((N_POOL_GROUPS, POOL_GROUP, POOL_GROUP), BF16, _slot_pool),
    "scale": ((1, D_MODEL), F32, _slot_scale),
    "kv": ((D_MODEL, 2 * KV_DIM), BF16, _slot_rows128),
    "q": ((D_MODEL, D_MODEL), BF16, _slot_rows128),
    "o": ((D_MODEL, D_MODEL), BF16, _slot_rows128),
    "gu": ((FF_CHUNKS, 2, D_MODEL, FF_BLOCK), BF16, _slot_gu),
    "wd": ((D_FF, D_MODEL), BF16, _slot_wd),
    "gate": ((D_MODEL, D_MODEL), BF16, _slot_rows128),
    "proj": ((PLE_DIM, D_MODEL), BF16, _slot_cols128),
}


def _no_compute():
    pass


class _AllGather:
    def __init__(self, names, shards):
        self.kinds = [_GATHERED[n.rstrip("01")] for n in names]
        self.args = [shards[n] for n in names]
        self.out_shape = [jax.ShapeDtypeStruct(shape, dtype) for shape, dtype, _ in self.kinds]
        n = len(names)
        self.scratch = [pltpu.SemaphoreType.DMA((n, 7)), pltpu.SemaphoreType.DMA((n, 7)), pltpu.SemaphoreType.DMA((n,))]

    def _copies(self, srcs, outs, sems):
        send_sems, recv_sems, local_sems = sems
        x, y, c = _my_place()
        me, sibling = (x, y, c), (x, y, 1 - c)
        chips = [(1 - x, y), (x, 1 - y), (1 - x, 1 - y)]
        n = len(srcs)

        def slot(t, dev):
            return self.kinds[t][2](outs[t], _dev_index(*dev))

        def copy(t, k, block, to, src=None):
            return pltpu.make_async_remote_copy(
                src_ref=slot(t, block) if src is None else src, dst_ref=slot(t, block),
                send_sem=send_sems.at[t, k], recv_sem=recv_sems.at[t, k], device_id=to, device_id_type=MESH)

        mine = [pltpu.make_async_copy(srcs[t], slot(t, me), local_sems.at[t]) for t in range(n)]
        first = []
        for t in range(n):
            first.append(copy(t, 0, me, sibling, src=srcs[t]))
            first += [copy(t, 1 + j, me, (*chip, c), src=srcs[t]) for j, chip in enumerate(chips)]
        return me, sibling, chips, copy, mine, first

    def start(self, srcs, outs, sems):
        _, _, _, _, mine, first = self._copies(srcs, outs, sems)
        for cp in mine + first:
            cp.start()

    def finish(self, srcs, outs, sems):
        me, sibling, chips, copy, mine, first = self._copies(srcs, outs, sems)
        c = me[2]
        n = len(srcs)
        passed = []
        for j, chip in enumerate(chips):
            for t in range(n):
                copy(t, 1 + j, (*chip, c), me).wait_recv()
                fwd = copy(t, 4 + j, (*chip, c), sibling)
                fwd.start()
                passed.append(fwd)
        for t in range(n):
            copy(t, 0, sibling, me).wait_recv()
            for j, chip in enumerate(chips):
                copy(t, 4 + j, (*chip, 1 - c), me).wait_recv()
        for cp in first + passed:
            cp.wait_send()
        for cp in mine:
            cp.wait()


def _all_gather_only(name, names, shards):
    return _launch(_no_compute, name=name, grid=(), in_specs=[], out_specs=[], out_shape=[], args=(),
                   job=_AllGather(names, shards))[1]


def _block_pool(ref, j):
    return ref.at[:, pl.ds(pl.multiple_of(j * 32, 32), 32), :]


def _block_rows128(ref, j):
    return ref.at[pl.ds(pl.multiple_of(j * 128, 128), 128), :]


def _block_gu(ref, j):
    return ref.at[j % FF_CHUNKS, j // FF_CHUNKS]


def _block_wd(ref, j):
    return ref.at[pl.ds(pl.multiple_of(j * WD_ROWS, 16), WD_ROWS), :]


def _block_cols128(ref, j):
    return ref.at[:, pl.ds(pl.multiple_of(j * 128, 128), 128)]


_SCATTERED = {
    "pool": ((N_POOL_GROUPS, 32, POOL_GROUP), _block_pool),
    "kv": ((128, 2 * KV_DIM), _block_rows128),
    "q": ((128, D_MODEL), _block_rows128),
    "o": ((128, D_MODEL), _block_rows128),
    "gu": ((FF_PART, FF_BLOCK), _block_gu),
    "wd": ((WD_ROWS, FF_PART), _block_wd),
    "gate": ((128, D_MODEL), _block_rows128),
    "proj": ((PLE_DIM, 128), _block_cols128),
}


class _SiblingSwap:
    def __init__(self, pieces):
        self.kinds = [_SCATTERED[kind] for kind, _ in pieces]
        self.args = [g for _, g in pieces]
        self.out_shape = [jax.ShapeDtypeStruct((N_CHIPS, *block), BF16) for block, _ in self.kinds]
        n = len(pieces)
        self.scratch = [pltpu.SemaphoreType.DMA((n, N_CHIPS)), pltpu.SemaphoreType.DMA((n, N_CHIPS))]

    def _copies(self, srcs, outs, sems):
        send_sems, recv_sems = sems
        x, y, c = _my_place()
        return [pltpu.make_async_remote_copy(
            src_ref=block(srcs[t], 2 * ch + 1 - c), dst_ref=outs[t].at[ch], send_sem=send_sems.at[t, ch],
            recv_sem=recv_sems.at[t, ch], device_id=(x, y, 1 - c), device_id_type=MESH)
            for t, (_, block) in enumerate(self.kinds) for ch in range(N_CHIPS)]

    def start(self, srcs, outs, sems):
        for cp in self._copies(srcs, outs, sems):
            cp.start()

    def finish(self, srcs, outs, sems):
        for cp in self._copies(srcs, outs, sems):
            cp.wait()


class _ChipScatter:
    def __init__(self, pieces):
        self.kinds = [_SCATTERED[kind] for kind, _, _ in pieces]
        self.n = n = len(pieces)
        self.args = [g for _, g, _ in pieces] + [s for _, _, s in pieces]
        self.out_shape = [jax.ShapeDtypeStruct((N_CHIPS, *block), BF16) for block, _ in self.kinds]
        self.scratch = []
        for block, _ in self.kinds:
            self.scratch += [pltpu.VMEM((N_CHIPS, *block), BF16)] * 3
        self.scratch += [pltpu.SemaphoreType.DMA((n, N_CHIPS + 1)), pltpu.SemaphoreType.DMA((n, N_CHIPS - 1)),
                         pltpu.SemaphoreType.DMA((n, N_CHIPS - 1)), pltpu.SemaphoreType.DMA((n,))]

    def _sends(self, outs, scr):
        n = self.n
        send_sems, recv_sems, local_sems = scr[3 * n + 1:]
        x, y, c = _my_place()
        chip = 2 * x + y
        copies = []
        for t in range(n):
            total = scr[3 * t + 2]
            copies.append(pltpu.make_async_copy(total.at[chip], outs[t].at[chip], local_sems.at[t]))
            for r in range(1, N_CHIPS):
                to = chip ^ r
                copies.append(pltpu.make_async_remote_copy(
                    src_ref=total.at[to], dst_ref=outs[t].at[chip], send_sem=send_sems.at[t, r - 1],
                    recv_sem=recv_sems.at[t, r - 1], device_id=(to // 2, to % 2, c), device_id_type=MESH))
        return copies

    def start(self, ins, outs, scr):
        n = self.n
        load_sems = scr[3 * n]
        c = lax.axis_index("c")
        loads = []
        for t, (_, block) in enumerate(self.kinds):
            mine, theirs = scr[3 * t], scr[3 * t + 1]
            loads += [pltpu.make_async_copy(block(ins[t], 2 * ch + c), mine.at[ch], load_sems.at[t, ch])
                      for ch in range(N_CHIPS)]
            loads.append(pltpu.make_async_copy(ins[n + t], theirs, load_sems.at[t, N_CHIPS]))
        for cp in loads:
            cp.start()
        for cp in loads:
            cp.wait()
        for t in range(n):
            mine, theirs, total = scr[3 * t:3 * t + 3]
            for ch in range(N_CHIPS):
                total[ch] = (mine[ch].astype(F32) + theirs[ch].astype(F32)).astype(BF16)
        for cp in self._sends(outs, scr):
            cp.start()

    def finish(self, ins, outs, scr):
        for cp in self._sends(outs, scr):
            cp.wait()


class _Jobs:
    def __init__(self, *jobs):
        self.jobs = jobs
        self.args = [a for j in jobs for a in j.args]
        self.out_shape = [o for j in jobs for o in j.out_shape]
        self.scratch = [s for j in jobs for s in j.scratch]

    def _split(self, refs, attr):
        at = 0
        for j in self.jobs:
            n = len(getattr(j, attr))
            yield refs[at:at + n]
            at += n

    def _each(self, ins, outs, scr):
        return zip(self.jobs, self._split(ins, "args"), self._split(outs, "out_shape"), self._split(scr, "scratch"))

    def start(self, ins, outs, scr):
        for j, i, o, s in self._each(ins, outs, scr):
            j.start(i, o, s)

    def finish(self, ins, outs, scr):
        for j, i, o, s in self._each(ins, outs, scr):
            j.finish(i, o, s)

    def split_outputs(self, outs):
        return list(self._split(outs, "out_shape"))


def _adamw_math(w, g, m, v):
    m = ADAM_B1 * m + (1.0 - ADAM_B1) * g
    v = ADAM_B2 * v + (1.0 - ADAM_B2) * (g * g)
    m_hat = m / (1.0 - ADAM_B1 ** ADAM_STEP)
    v_hat = v / (1.0 - ADAM_B2 ** ADAM_STEP)
    delta = -ADAM_LR * (m_hat / (jnp.sqrt(v_hat) + ADAM_EPS) + ADAM_WD * w)
    return delta, m, v


def _adamw(name, w, m, v, landings, n_col_blocks=1, job=None):
    _, r, c = landings[0].shape
    grid = (w.shape[0] // r, n_col_blocks)

    def body(w_ref, m_ref, v_ref, *rest):
        l_refs, (g_ref, d_ref, nm_ref, nv_ref) = rest[:len(landings)], rest[len(landings):]
        step = pl.program_id(0) * n_col_blocks + pl.program_id(1)
        for idx, l_ref in enumerate(l_refs):
            @pl.when(step == idx)
            def _(l_ref=l_ref):
                g = l_ref[0].astype(F32)
                for s in range(1, N_CHIPS):
                    g = g + l_ref[s].astype(F32)
                g_ref[...] = g
                d_ref[...], nm_ref[...], nv_ref[...] = _adamw_math(w_ref[...], g, m_ref[...], v_ref[...])

    spec = pl.BlockSpec((r, c), lambda a, b: (a, b))
    return _launch(
        body, name=f"adamw_{name}", grid=grid,
        in_specs=[spec, spec, spec] + [_full_spec((N_CHIPS, r, c))] * len(landings),
        out_specs=[spec] * 4, out_shape=[jax.ShapeDtypeStruct(w.shape, F32)] * 4,
        args=(w, m, v, *landings), vmem=VMEM_BIG, job=job)


def _small_all_reduce_adamw(part, w, m, v):
    def body(part_ref, w_ref, m_ref, v_ref, g_ref, d_ref, nm_ref, nv_ref, buf, send_sems, recv_sems):
        x, y, c = _my_place()
        me = _dev_index(x, y, c)
        buf[me] = part_ref[...]
        copies = [pltpu.make_async_remote_copy(
            src_ref=part_ref, dst_ref=buf.at[me], send_sem=send_sems.at[r - 1], recv_sem=recv_sems.at[r - 1],
            device_id=_peer_by_relation(r), device_id_type=MESH) for r in range(1, N_DEV)]
        for cp in copies:
            cp.start()
        for cp in copies:
            cp.wait()
        g = buf[0]
        for s in range(1, N_DEV):
            g = g + buf[s]
        g_ref[...] = g
        d_ref[...], nm_ref[...], nv_ref[...] = _adamw_math(w_ref[...], g, m_ref[...], v_ref[...])

    vm = pl.BlockSpec(memory_space=pltpu.VMEM)
    return pl.pallas_call(
        body, name="small_all_reduce_adamw", out_shape=[jax.ShapeDtypeStruct((SV_ROWS, D_MODEL), F32)] * 4,
        in_specs=[vm] * 4, out_specs=[vm] * 4,
        scratch_shapes=[pltpu.VMEM((N_DEV, SV_ROWS, D_MODEL), F32), pltpu.SemaphoreType.DMA((N_DEV - 1,)),
                        pltpu.SemaphoreType.DMA((N_DEV - 1,))],
    )(part, w, m, v)


def _local_step(x, p, tgt, gains, sinks, shards, weights):
    row = _Gain
    gather = lambda *names: _AllGather(names, shards)
    g_pre_mix, g_post_mix = gains["pre_mix_g"], gains["post_mix_g"]
    g_pre_ffn, g_post_ffn = gains["pre_ffn_g"], gains["post_ffn_g"]
    g_ple, g_ple_post, g_kv = gains["ple_g"], gains["ple_post_g"], _Gain(gains["kv_g"], 0)

    wp, scale, wgu0 = _all_gather_only("gather_first", ("pool", "scale", "gu0"), shards)
    (x1_0, h2_0, yraw, dpool), (wd0,) = _fwd_pool_mixer(
        x, row(g_pre_mix, 0), wp, scale, row(g_post_mix, 0), row(g_pre_ffn, 0), job=gather("wd0"))
    (gs0, us0, f0, x2_0, h3_0), (wgate0, wproj0, wgu1) = _fwd_ffn(
        0, h2_0, x1_0, wgu0, wd0, row(g_post_ffn, 0), row(g_ple, 0), job=gather("gate0", "proj0", "gu1"))
    (x3_0, z0, pe0), (wkv, wq) = _fwd_ple(0, x2_0, h3_0, p[0], wgate0, wproj0, row(g_ple_post, 0),
                                          job=gather("kv", "q"))
    (hk, h1, q, kv), (wo,) = _fwd_qkv(x3_0, g_kv, row(g_pre_mix, 1), wkv, wq, job=gather("o"))
    front = ((ATT_BLOCK, 0), (0, 0))
    kpad = jnp.pad(kv[:, :KV_DIM], front)
    vpad = jnp.pad(kv[:, KV_DIM:], front)
    (attn,), (wd1,) = _fwd_attention(q, kpad, vpad, sinks, job=gather("wd1"))
    (y1, x1_1, h2_1), _ = _fwd_attn_out(attn, x3_0, wo, row(g_post_mix, 1), row(g_pre_ffn, 1))
    (gs1, us1, f1, x2_1, h3_1), (wgate1, wproj1) = _fwd_ffn(
        1, h2_1, x1_1, wgu1, wd1, row(g_post_ffn, 1), row(g_ple, 1), job=gather("gate1", "proj1"))
    (dx3_1, z1, pe1, loss), _ = _fwd_ple(1, x2_1, h3_1, p[1], wgate1, wproj1, row(g_ple_post, 1), target=tgt)

    produced, swapped, landed = {}, {}, {}

    def kind_of(name):
        return name.rstrip("0123_")

    def carry(swap=(), spread=()):
        jobs = []
        if swap:
            jobs.append(_SiblingSwap([(kind_of(n), produced[n]) for n in swap]))
        if spread:
            jobs.append(_ChipScatter([(kind_of(n), produced[n], swapped[n]) for n in spread]))
        return _Jobs(*jobs)

    def carried(jobs, outs, swap=(), spread=()):
        parts = jobs.split_outputs(outs)
        if swap:
            swapped.update(zip(swap, parts[0]))
        if spread:
            landed.update(zip(spread, parts[-1]))

    def hosted(call, *args, swap=(), spread=()):
        jobs = carry(swap, spread)
        outs, job_outs = call(*args, job=jobs)
        carried(jobs, job_outs, swap, spread)
        return outs

    def ffn_weight_grads(layer, h2, df, dg, du, a, hosts):
        for qtr in range(FF_PARTS):
            dgu, dwd = hosted(_bwd_ffn_dw, layer, qtr, h2, df, dg, du, a, **hosts[qtr])
            produced[f"gu{layer}_{qtr}"], produced[f"wd{layer}_{qtr}"] = dgu, dwd

    ffn_q = lambda layer, qtr: (f"gu{layer}_{qtr}", f"wd{layer}_{qtr}")

    dx2_1, df1, produced["gate1"], produced["proj1"], dg_ple_post1, dg_ple1, dg_post_ffn1 = hosted(
        _bwd_ple, 1, dx3_1, x2_1, z1, pe1, h3_1, p[1], f1, wgate1, row(g_ple_post, 1), row(g_ple, 1),
        row(g_post_ffn, 1))
    dh2_1, dg1, du1, a1 = hosted(_bwd_ffn_act, 1, df1, gs1, us1, wgu1, wd1, swap=("gate1", "proj1"))
    ffn_weight_grads(1, h2_1, df1, dg1, du1, a1, [dict(spread=("gate1", "proj1")), dict(swap=ffn_q(1, 0))])
    dx1_1, dattn, produced["o"], dg_pre_ffn1, dg_post_mix1 = hosted(
        _bwd_attn_out, dx2_1, dh2_1, x1_1, y1, attn, wo, row(g_pre_ffn, 1), row(g_post_mix, 1), swap=ffn_q(1, 1))
    dq, dkpad, dvpad, dsinks = hosted(_bwd_attention, q, dattn, kpad, vpad, sinks,
                                      spread=ffn_q(1, 0) + ffn_q(1, 1))
    dkv = jnp.concatenate([dkpad[ATT_BLOCK:], dvpad[ATT_BLOCK:]], axis=1).astype(BF16)
    dx3_0, produced["q"], produced["kv"], dg_pre_mix1, dg_kv = hosted(
        _bwd_qkv, dx1_1, dq, dkv, x3_0, h1, hk, wq, wkv, row(g_pre_mix, 1), g_kv, swap=("o",))
    dx2_0, df0, produced["gate0"], produced["proj0"], dg_ple_post0, dg_ple0, dg_post_ffn0 = hosted(
        _bwd_ple, 0, dx3_0, x2_0, z0, pe0, h3_0, p[0], f0, wgate0, row(g_ple_post, 0), row(g_ple, 0),
        row(g_post_ffn, 0), swap=("q", "kv"), spread=("o",))
    dh2_0, dg0, du0, a0 = hosted(_bwd_ffn_act, 0, df0, gs0, us0, wgu0, wd0,
                                 swap=("gate0", "proj0"), spread=("q", "kv"))
    ffn_weight_grads(0, h2_0, df0, dg0, du0, a0, [dict(spread=("gate0", "proj0")), dict(swap=ffn_q(0, 0))])
    grad_x, produced["pool"], dscale, dg_pre_ffn0, dg_post_mix0, dg_pre_mix0 = hosted(
        _bwd_pool_mixer, dx2_0, dh2_0, x1_0, x, yraw, dpool, wp, scale, row(g_pre_ffn, 0), row(g_post_mix, 0),
        row(g_pre_mix, 0), swap=ffn_q(0, 1), spread=ffn_q(0, 0))

    def update(name, n_col_blocks=1, pieces=None, swap=(), spread=()):
        w, m, v = weights[name]
        rows = w.size // w.shape[-1]
        flat = [landed[n].reshape(N_CHIPS, -1, landed[n].shape[-1]) for n in (pieces or [kind_short[name]])]
        outs = hosted(_adamw, name, w.reshape(rows, -1), m.reshape(rows, -1), v.reshape(rows, -1), flat,
                      n_col_blocks, swap=swap, spread=spread)
        return [o.reshape(w.shape) for o in outs]

    kind_short = {"w_q": "q", "w_kv": "kv", "w_o": "o", "pool_w": "pool"}
    upd = {}
    upd["w_ple_gate"] = update("w_ple_gate", pieces=("gate0", "gate1"), swap=("pool",), spread=ffn_q(0, 1))
    upd["w_ple_proj"] = update("w_ple_proj", pieces=("proj0", "proj1"), spread=("pool",))
    for name in ("w_q", "w_kv", "w_o", "pool_w"):
        upd[name] = update(name)
    upd["w_gu"] = update("w_gu", pieces=[f"gu{layer}_{qtr}" for layer in range(2) for qtr in range(FF_PARTS)])
    upd["w_down"] = update("w_down", FF_PARTS,
                           pieces=[f"wd{layer}_{qtr}" for layer in range(2) for qtr in range(FF_PARTS)])

    lanes = lambda a: jnp.pad(a, ((0, 0), (0, D_MODEL - a.shape[1])))
    small = jnp.concatenate([
        dg_pre_mix0, dg_pre_mix1, dg_post_mix0, dg_post_mix1, dg_pre_ffn0, dg_pre_ffn1, dg_post_ffn0, dg_post_ffn1,
        dg_ple0, dg_ple1, dg_ple_post0, dg_ple_post1, dg_kv, dscale, lanes(dsinks[:, :N_HEADS]), lanes(loss)], axis=0)
    return grad_x, upd, small


def kernel(x, p, pre_mix_g, post_mix_g, pre_ffn_g, post_ffn_g, pool_w, pool_scale, kv_g, w_kv, w_q, sinks, w_o, w_gu, w_down, ple_g, w_ple_gate, w_ple_proj, ple_post_g, loss_target, m_pre_mix_g, m_post_mix_g, m_pre_ffn_g, m_post_ffn_g, m_pool_w, m_pool_scale, m_kv_g, m_w_kv, m_w_q, m_sinks, m_w_o, m_w_gu, m_w_down, m_ple_g, m_w_ple_gate, m_w_ple_proj, m_ple_post_g, v_pre_mix_g, v_post_mix_g, v_pre_ffn_g, v_post_ffn_g, v_pool_w, v_pool_scale, v_kv_g, v_w_kv, v_w_q, v_sinks, v_w_o, v_w_gu, v_w_down, v_ple_g, v_w_ple_gate, v_w_ple_proj, v_ple_post_g):
    me = _dev_index(*_my_place())

    shards = {"pool": pool_w[0].astype(BF16), "scale": pool_scale, "kv": w_kv.astype(BF16),
              "q": w_q[0].astype(BF16), "o": w_o[0].astype(BF16)}
    for layer in range(2):
        shards[f"gu{layer}"] = w_gu[layer].astype(BF16)
        shards[f"wd{layer}"] = w_down[layer].astype(BF16)
        shards[f"gate{layer}"] = w_ple_gate[layer].astype(BF16)
        shards[f"proj{layer}"] = w_ple_proj[layer].astype(BF16)
    stacked = lambda g: g.reshape(-1, 1, D_MODEL)
    gains = dict(pre_mix_g=stacked(pre_mix_g), post_mix_g=stacked(post_mix_g), pre_ffn_g=stacked(pre_ffn_g),
                 post_ffn_g=stacked(post_ffn_g), ple_g=stacked(ple_g), ple_post_g=stacked(ple_post_g),
                 kv_g=stacked(kv_g))
    weights = {"pool_w": (pool_w, m_pool_w, v_pool_w), "w_kv": (w_kv, m_w_kv, v_w_kv), "w_q": (w_q, m_w_q, v_w_q),
               "w_o": (w_o, m_w_o, v_w_o), "w_gu": (w_gu, m_w_gu, v_w_gu), "w_down": (w_down, m_w_down, v_w_down),
               "w_ple_gate": (w_ple_gate, m_w_ple_gate, v_w_ple_gate),
               "w_ple_proj": (w_ple_proj, m_w_ple_proj, v_w_ple_proj)}
    grad_x, upd, small = _local_step(x[0], p[:, 0], loss_target[0], gains, sinks, shards, weights)

    lane0 = me * 128

    def slab(pre_mix, post_mix, pre_ffn, post_ffn, ple, ple_post, kv, scale_shard, snk):
        scale_row = lax.dynamic_update_slice(jnp.zeros((1, D_MODEL), F32), scale_shard, (0, lane0))
        snk_row = jnp.pad(snk, ((0, 0), (0, D_MODEL - N_HEADS)))
        return jnp.concatenate([pre_mix, post_mix, pre_ffn, post_ffn, ple, ple_post, kv[None, :], scale_row, snk_row,
                                jnp.zeros((1, D_MODEL), F32)], axis=0)

    sw = slab(pre_mix_g, post_mix_g, pre_ffn_g, post_ffn_g, ple_g, ple_post_g, kv_g, pool_scale, sinks)
    sm = slab(m_pre_mix_g, m_post_mix_g, m_pre_ffn_g, m_post_ffn_g, m_ple_g, m_ple_post_g, m_kv_g, m_pool_scale, m_sinks)
    sv = slab(v_pre_mix_g, v_post_mix_g, v_pre_ffn_g, v_post_ffn_g, v_ple_g, v_ple_post_g, v_kv_g, v_pool_scale, v_sinks)
    sg, sd, snm, snv = _small_all_reduce_adamw(small, sw, sm, sv)
    loss = sg[SV_LOSS, 0]

    def unslab(s):
        return {
            "pre_mix_g": s[SV_PRE_MIX:SV_PRE_MIX + 2], "post_mix_g": s[SV_POST_MIX:SV_POST_MIX + 2],
            "pre_ffn_g": s[SV_PRE_FFN:SV_PRE_FFN + 2], "post_ffn_g": s[SV_POST_FFN:SV_POST_FFN + 2],
            "ple_g": s[SV_PLE:SV_PLE + 2], "ple_post_g": s[SV_PLE_POST:SV_PLE_POST + 2], "kv_g": s[SV_KV],
            "pool_scale": lax.dynamic_slice(s, (SV_POOL_SCALE, lane0), (1, 128)),
            "sinks": s[SV_SINKS:SV_SINKS + 1, :N_HEADS],
        }

    names = ["pre_mix_g", "post_mix_g", "pre_ffn_g", "post_ffn_g", "pool_w", "pool_scale", "kv_g", "w_kv", "w_q",
             "sinks", "w_o", "w_gu", "w_down", "ple_g", "w_ple_gate", "w_ple_proj", "ple_post_g"]
    outs = [loss, grad_x[None]]
    for kind, slab_out in enumerate((sg, sd, snm, snv)):
        small_out = unslab(slab_out)
        outs += [upd[n][kind] if n in upd else small_out[n] for n in names]
    return tuple(outs)
```

```python
import functools

import jax
import jax.numpy as jnp
from jax import lax
from jax.experimental import pallas as pl
from jax.experimental.pallas import tpu as pltpu

F32 = jnp.float32
BF16 = jnp.bfloat16

N_DEV = 8
D_MODEL = 1024
N_POOL_GROUPS = 4
POOL_GROUP = 256
POOL_HALO = 16
HEAD_DIM = 64
N_HEADS = 16
N_KV_HEADS = 4
GQA_GROUP = 4
KV_DIM = N_KV_HEADS * HEAD_DIM
ATT_BLOCK = 128
D_FF = 2816
FF_CHUNKS = 4
FF_BLOCK = D_FF // FF_CHUNKS
WD_ROWS = D_FF // N_DEV
FF_PARTS = 2
FF_PART = D_MODEL // FF_PARTS
N_CHIPS = 4
PLE_DIM = 256
EPS = 1e-6
NEG_INF = -1e30
ATT_SCALE = HEAD_DIM ** -0.5

ADAM_LR = 0.001
ADAM_B1 = 0.9
ADAM_B2 = 0.999
ADAM_EPS = 1e-08
ADAM_WD = 0.01
ADAM_STEP = 10

ROW_TILE = 512
FFN_ROW_TILE = 512
FFN_SUB_TILES = 2
VMEM_BIG = 56 * 1024 * 1024
VMEM_MID = 48 * 1024 * 1024
HBM_PIN_ELEMS = 64 * 1024

SV_ROWS = 16
SV_PRE_MIX, SV_POST_MIX, SV_PRE_FFN, SV_POST_FFN, SV_PLE, SV_PLE_POST = 0, 2, 4, 6, 8, 10
SV_KV, SV_POOL_SCALE, SV_SINKS, SV_LOSS = 12, 13, 14, 15

MESH = pl.DeviceIdType.MESH
ANY = pl.BlockSpec(memory_space=pl.ANY)


def _dot(a, b):
    return jnp.dot(a, b, preferred_element_type=F32)


def _dot_nt(a, b):
    return lax.dot_general(a, b, (((1,), (1,)), ((), ())), preferred_element_type=F32)


def _dot_tn(a, b):
    return lax.dot_general(a, b, (((0,), (0,)), ((), ())), preferred_element_type=F32)


def _rstd(x):
    return lax.rsqrt(jnp.mean(x * x, axis=-1, keepdims=True) + EPS)


def _rms(x, g):
    return x * _rstd(x) * g


def _rms_bwd(x, g, dy):
    r = _rstd(x)
    n = x * r
    dn = dy * g
    dx = r * (dn - n * jnp.mean(dn * n, axis=-1, keepdims=True))
    dg = jnp.sum(dy * n, axis=0, keepdims=True)
    return dx, dg


def _sigmoid(x):
    return 1.0 / (1.0 + jnp.exp(-x))


def _acc(ref, val, first):
    @pl.when(first)
    def _():
        ref[...] = val

    @pl.when(jnp.logical_not(first))
    def _():
        ref[...] += val


def _pool_counts(row0, rows):
    t = row0 + lax.broadcasted_iota(jnp.int32, (rows, D_MODEL), 0) + 1
    grp = lax.broadcasted_iota(jnp.int32, (rows, D_MODEL), 1) // POOL_GROUP
    win = jnp.left_shift(2, grp)
    return jnp.minimum(t, win).astype(F32)


def _window_sums(ext, shift_of):
    outs = []
    s = ext
    for gi in range(N_POOL_GROUPS):
        s = s + pltpu.roll(s, shift_of(1 << gi), axis=0)
        outs.append(s[:, :POOL_GROUP])
        s = s[:, POOL_GROUP:]
    return jnp.concatenate(outs, axis=1)


def _cparams(n_axes, vmem):
    return pltpu.CompilerParams(dimension_semantics=("arbitrary",) * n_axes, vmem_limit_bytes=vmem)


def _row_spec(cols, tm=ROW_TILE):
    return pl.BlockSpec((tm, cols), lambda i: (i, 0))


def _full_spec(shape):
    zeros = (0,) * len(shape)
    return pl.BlockSpec(shape, lambda *_: zeros)


def _vec_spec():
    return _full_spec((1, D_MODEL))


class _Gain:
    def __init__(self, stacked, layer):
        self.stacked, self.layer = stacked, layer

    def spec(self):
        layer = self.layer
        return pl.BlockSpec((None, 1, D_MODEL), lambda *_: (layer, 0, 0))


def _in_hbm(a):
    return pltpu.with_memory_space_constraint(a, pltpu.HBM) if a.size >= HBM_PIN_ELEMS else a


def _launch(body, *, name, grid, in_specs, out_specs, out_shape, args, scratch_shapes=(), vmem=VMEM_MID, job=None):
    in_specs = [a.spec() if isinstance(a, _Gain) else s for s, a in zip(in_specs, args)]
    args = [a.stacked if isinstance(a, _Gain) else _in_hbm(a) for a in args]
    n_in, n_out, n_scr = len(args), len(out_shape), len(scratch_shapes)
    j_args, j_out, j_scr = ([], [], []) if job is None else ([_in_hbm(a) for a in job.args], job.out_shape, job.scratch)

    def run(*refs):
        groups, at = [], 0
        for n in (n_in, len(j_args), n_out, len(j_out), n_scr, len(j_scr)):
            groups.append(refs[at:at + n])
            at += n
        ins, j_ins, outs, j_outs, scr, j_sems = groups
        if job is None:
            body(*ins, *outs, *scr)
        elif not grid:
            job.start(j_ins, j_outs, j_sems)
            body(*ins, *outs, *scr)
            job.finish(j_ins, j_outs, j_sems)
        else:
            ids = [pl.program_id(a) for a in range(len(grid))]
            first = functools.reduce(jnp.logical_and, [i == 0 for i in ids])
            last = functools.reduce(jnp.logical_and, [i == g - 1 for i, g in zip(ids, grid)])
            pl.when(first)(lambda: job.start(j_ins, j_outs, j_sems))
            body(*ins, *outs, *scr)
            pl.when(last)(lambda: job.finish(j_ins, j_outs, j_sems))

    res = pl.pallas_call(
        run, name=name, grid=grid,
        in_specs=list(in_specs) + [ANY] * len(j_args), out_specs=list(out_specs) + [ANY] * len(j_out),
        out_shape=list(out_shape) + list(j_out), scratch_shapes=list(scratch_shapes) + list(j_scr),
        compiler_params=_cparams(len(grid), vmem),
    )(*args, *j_args)
    return res[:n_out], res[n_out:]


def _fwd_pool_mixer(x, g_pre, wp, scale, g_post, g_ffn, job=None):
    T = x.shape[0]
    tm = ROW_TILE
    nt = T // tm

    def body(x_ref, gpre_ref, wp_ref, sc_ref, gpost_ref, gffn_ref, x1_ref, h2_ref, yraw_ref, d_ref, carry):
        i = pl.program_id(0)

        @pl.when(i == 0)
        def _():
            carry[...] = jnp.zeros_like(carry)

        xv = x_ref[...]
        h = _rms(xv, gpre_ref[...])
        ext = jnp.concatenate([carry[...], h], axis=0)
        carry[...] = h[tm - POOL_HALO:, :]
        sums = _window_sums(ext, lambda k: k)[POOL_HALO:, :]
        d = sums / _pool_counts(i * tm, tm) - h
        db = d.astype(BF16)
        d_ref[...] = db
        yraw = jnp.concatenate(
            [_dot(db[:, g * POOL_GROUP:(g + 1) * POOL_GROUP], wp_ref[g]) for g in range(N_POOL_GROUPS)], axis=1)
        yraw_ref[...] = yraw
        x1 = xv + _rms(yraw * sc_ref[...], gpost_ref[...])
        x1_ref[...] = x1
        h2_ref[...] = _rms(x1, gffn_ref[...]).astype(BF16)

    return _launch(
        body, name="fwd_pool_mixer", grid=(nt,),
        in_specs=[_row_spec(D_MODEL), _vec_spec(), _full_spec((N_POOL_GROUPS, POOL_GROUP, POOL_GROUP)), _vec_spec(),
                  _vec_spec(), _vec_spec()],
        out_specs=[_row_spec(D_MODEL)] * 4,
        out_shape=[jax.ShapeDtypeStruct((T, D_MODEL), F32), jax.ShapeDtypeStruct((T, D_MODEL), BF16),
                   jax.ShapeDtypeStruct((T, D_MODEL), F32), jax.ShapeDtypeStruct((T, D_MODEL), BF16)],
        scratch_shapes=[pltpu.VMEM((POOL_HALO, D_MODEL), F32)],
        args=(x, g_pre, wp, scale, g_post, g_ffn), job=job)


def _fwd_ffn(layer, h2, x1, wgu, wd, g_post, g_ple, job=None):
    T = h2.shape[0]
    tm = min(FFN_ROW_TILE, T)
    nt = T // tm
    sub = tm // FFN_SUB_TILES
    last = FF_CHUNKS - 1

    def body(h2_ref, x1_ref, wgu_ref, wd_ref, gpost_ref, gple_ref, gs_ref, us_ref, f_ref, x2_ref, h3_ref, acc):
        k = pl.program_id(0)
        i = pl.program_id(1)
        rows = pl.ds(pl.multiple_of(i * tm, tm), tm)
        parts = []
        for s in range(FFN_SUB_TILES):
            r = pl.ds(s * sub, sub)
            h = h2_ref[r, :]
            g = _dot(h, wgu_ref[0])
            u = _dot(h, wgu_ref[1])
            gs_ref[r, :] = g.astype(BF16)
            us_ref[r, :] = u.astype(BF16)
            a = (g * _sigmoid(g) * u).astype(BF16)
            parts.append(_dot(a, wd_ref[...]))
        part = jnp.concatenate(parts, axis=0)

        @pl.when(k == 0)
        def _():
            acc[rows, :] = part

        @pl.when(jnp.logical_and(k > 0, k < last))
        def _():
            acc[rows, :] += part

        @pl.when(k == last)
        def _():
            f = acc[rows, :] + part
            f_ref[...] = f
            x2 = x1_ref[...] + _rms(f, gpost_ref[...])
            x2_ref[...] = x2
            h3_ref[...] = _rms(x2, gple_ref[...]).astype(BF16)

    def late(k, i):
        return (jnp.where(k == last, i, 0), 0)

    return _launch(
        body, name=f"fwd_ffn{layer}", grid=(FF_CHUNKS, nt),
        in_specs=[pl.BlockSpec((tm, D_MODEL), lambda k, i: (i, 0)),
                  pl.BlockSpec((tm, D_MODEL), late),
                  pl.BlockSpec((None, 2, D_MODEL, FF_BLOCK), lambda k, i: (k, 0, 0, 0)),
                  pl.BlockSpec((FF_BLOCK, D_MODEL), lambda k, i: (k, 0)),
                  pl.BlockSpec((1, D_MODEL), lambda k, i: (0, 0)),
                  pl.BlockSpec((1, D_MODEL), lambda k, i: (0, 0))],
        out_specs=[pl.BlockSpec((None, tm, FF_BLOCK), lambda k, i: (k, i, 0)),
                   pl.BlockSpec((None, tm, FF_BLOCK), lambda k, i: (k, i, 0)),
                   pl.BlockSpec((tm, D_MODEL), late),
                   pl.BlockSpec((tm, D_MODEL), late),
                   pl.BlockSpec((tm, D_MODEL), late)],
        out_shape=[jax.ShapeDtypeStruct((FF_CHUNKS, T, FF_BLOCK), BF16),
                   jax.ShapeDtypeStruct((FF_CHUNKS, T, FF_BLOCK), BF16),
                   jax.ShapeDtypeStruct((T, D_MODEL), F32),
                   jax.ShapeDtypeStruct((T, D_MODEL), F32),
                   jax.ShapeDtypeStruct((T, D_MODEL), BF16)],
        scratch_shapes=[pltpu.VMEM((T, D_MODEL), F32)],
        args=(h2, x1, wgu, wd, g_post, g_ple), vmem=VMEM_BIG, job=job)


def _fwd_ple(layer, x2, h3, p, wgate, wproj, g_post, target=None, job=None):
    T = x2.shape[0]
    tm = ROW_TILE
    nt = T // tm
    with_loss = target is not None

    def body(*refs):
        if with_loss:
            x2_ref, h3_ref, p_ref, wg_ref, wp_ref, gpost_ref, tgt_ref, out_ref, z_ref, pe_ref, loss_ref = refs
        else:
            x2_ref, h3_ref, p_ref, wg_ref, wp_ref, gpost_ref, out_ref, z_ref, pe_ref = refs
        z = _dot(h3_ref[...], wg_ref[...])
        pe = _dot(p_ref[...].astype(BF16), wp_ref[...])
        z_ref[...] = z
        pe_ref[...] = pe
        x3 = x2_ref[...] + _rms(pe * _sigmoid(z), gpost_ref[...])
        if with_loss:
            err = x3 - tgt_ref[...]
            out_ref[...] = err * (1.0 / D_MODEL)
            part = 0.5 * jnp.sum(jnp.mean(err * err, axis=-1, keepdims=True), axis=0, keepdims=True)
            _acc(loss_ref, part, pl.program_id(0) == 0)
        else:
            out_ref[...] = x3

    in_specs = [_row_spec(D_MODEL), _row_spec(D_MODEL), _row_spec(PLE_DIM), _full_spec((D_MODEL, D_MODEL)),
                _full_spec((PLE_DIM, D_MODEL)), _vec_spec()]
    out_specs = [_row_spec(D_MODEL)] * 3
    out_shape = [jax.ShapeDtypeStruct((T, D_MODEL), F32)] * 3
    args = [x2, h3, p, wgate, wproj, g_post]
    if with_loss:
        in_specs.append(_row_spec(D_MODEL))
        out_specs.append(_full_spec((1, 1)))
        out_shape.append(jax.ShapeDtypeStruct((1, 1), F32))
        args.append(target)
    return _launch(body, name=f"fwd_ple{layer}", grid=(nt,), in_specs=in_specs, out_specs=out_specs,
                   out_shape=out_shape, args=args, job=job)


def _fwd_qkv(x3, g_kv, g_mix, wkv, wq, job=None):
    T = x3.shape[0]
    nt = T // ROW_TILE

    def body(x_ref, gkv_ref, gmix_ref, wkv_ref, wq_ref, hk_ref, h1_ref, q_ref, kv_ref):
        xv = x_ref[...]
        r = _rstd(xv)
        hk = (xv * r * gkv_ref[...]).astype(BF16)
        h1 = (xv * r * gmix_ref[...]).astype(BF16)
        hk_ref[...] = hk
        h1_ref[...] = h1
        kv_ref[...] = _dot(hk, wkv_ref[...]).astype(BF16)
        q_ref[...] = _dot(h1, wq_ref[...]).astype(BF16)

    return _launch(
        body, name="fwd_qkv", grid=(nt,),
        in_specs=[_row_spec(D_MODEL), _vec_spec(), _vec_spec(), _full_spec((D_MODEL, 2 * KV_DIM)),
                  _full_spec((D_MODEL, D_MODEL))],
        out_specs=[_row_spec(D_MODEL), _row_spec(D_MODEL), _row_spec(D_MODEL), _row_spec(2 * KV_DIM)],
        out_shape=[jax.ShapeDtypeStruct((T, D_MODEL), BF16)] * 3 + [jax.ShapeDtypeStruct((T, 2 * KV_DIM), BF16)],
        args=(x3, g_kv, g_mix, wkv, wq), job=job)


def _alibi_slope(h):
    return 2.0 ** (-8.0 * (h + 1) / N_HEADS)


def _att_mask(n):
    qi = lax.broadcasted_iota(jnp.int32, (ATT_BLOCK, 2 * ATT_BLOCK), 0)
    si = lax.broadcasted_iota(jnp.int32, (ATT_BLOCK, 2 * ATT_BLOCK), 1)
    rel = ATT_BLOCK + qi - si
    valid = (rel >= 0) & (rel < ATT_BLOCK) & ((si >= ATT_BLOCK) | (n > 0))
    return rel.astype(F32), valid


def _att_probs(qh, kk, relf, valid, slope, sink):
    s = _dot_nt(qh, kk) * ATT_SCALE
    s = jnp.where(valid, s - slope * relf, NEG_INF)
    m = jnp.maximum(jnp.max(s, axis=-1, keepdims=True), sink)
    e = jnp.exp(s - m)
    es = jnp.exp(sink - m)
    inv = 1.0 / (jnp.sum(e, axis=-1, keepdims=True) + es)
    return e * inv, es * inv


def _fwd_attention(q, kpad, vpad, sinks, job=None):
    T = q.shape[0]
    nb = T // ATT_BLOCK

    def body(q_ref, k_ref, v_ref, sink_ref, o_ref):
        n = pl.program_id(0)
        start = pl.multiple_of(n * ATT_BLOCK, ATT_BLOCK)
        kw = k_ref[pl.ds(start, 2 * ATT_BLOCK), :]
        vw = v_ref[pl.ds(start, 2 * ATT_BLOCK), :]
        relf, valid = _att_mask(n)
        outs = []
        for h in range(N_HEADS):
            kh = h // GQA_GROUP
            qh = q_ref[:, h * HEAD_DIM:(h + 1) * HEAD_DIM]
            kk = kw[:, kh * HEAD_DIM:(kh + 1) * HEAD_DIM]
            vv = vw[:, kh * HEAD_DIM:(kh + 1) * HEAD_DIM]
            pr, _ = _att_probs(qh, kk, relf, valid, _alibi_slope(h), sink_ref[0, h])
            outs.append(_dot(pr.astype(BF16), vv))
        o_ref[...] = jnp.concatenate(outs, axis=1).astype(BF16)

    return _launch(
        body, name="fwd_attention", grid=(nb,),
        in_specs=[_row_spec(D_MODEL, ATT_BLOCK), _full_spec((T + ATT_BLOCK, KV_DIM)), _full_spec((T + ATT_BLOCK, KV_DIM)),
                  pl.BlockSpec(memory_space=pltpu.SMEM)],
        out_specs=[_row_spec(D_MODEL, ATT_BLOCK)],
        out_shape=[jax.ShapeDtypeStruct((T, D_MODEL), BF16)],
        args=(q, kpad, vpad, sinks), job=job)


def _fwd_attn_out(attn, x, wo, g_post, g_ffn, job=None):
    T = x.shape[0]
    nt = T // ROW_TILE

    def body(a_ref, x_ref, wo_ref, gpost_ref, gffn_ref, y_ref, x1_ref, h2_ref):
        y = _dot(a_ref[...], wo_ref[...])
        y_ref[...] = y
        x1 = x_ref[...] + _rms(y, gpost_ref[...])
        x1_ref[...] = x1
        h2_ref[...] = _rms(x1, gffn_ref[...]).astype(BF16)

    return _launch(
        body, name="fwd_attn_out", grid=(nt,),
        in_specs=[_row_spec(D_MODEL), _row_spec(D_MODEL), _full_spec((D_MODEL, D_MODEL)), _vec_spec(), _vec_spec()],
        out_specs=[_row_spec(D_MODEL)] * 3,
        out_shape=[jax.ShapeDtypeStruct((T, D_MODEL), F32), jax.ShapeDtypeStruct((T, D_MODEL), F32),
                   jax.ShapeDtypeStruct((T, D_MODEL), BF16)],
        args=(attn, x, wo, g_post, g_ffn), job=job)


def _bwd_ple(layer, dx3, x2, z, pe, h3, p, f, wgate, g_ple_post, g_ple, g_post_ffn, job=None):
    T = x2.shape[0]
    tm = ROW_TILE
    nt = T // tm

    def body(dx3_ref, x2_ref, z_ref, pe_ref, h3_ref, p_ref, f_ref, wg_ref, gpp_ref, gp_ref, gpf_ref,
             dx2_ref, df_ref, dwg_ref, dwp_ref, dgpp_ref, dgp_ref, dgpf_ref, acc_g, acc_p):
        i = pl.program_id(0)
        first = i == 0
        dx3v = dx3_ref[...]
        gate = _sigmoid(z_ref[...])
        pev = pe_ref[...]
        de, dgpp = _rms_bwd(pev * gate, gpp_ref[...], dx3v)
        dpe = (de * gate).astype(BF16)
        dz = (de * pev * gate * (1.0 - gate)).astype(BF16)
        _acc(acc_p, _dot_tn(p_ref[...].astype(BF16), dpe), first)
        _acc(acc_g, _dot_tn(h3_ref[...], dz), first)
        dh3 = _dot_nt(dz, wg_ref[...])
        dxn, dgp = _rms_bwd(x2_ref[...], gp_ref[...], dh3)
        dx2 = dx3v + dxn
        dx2_ref[...] = dx2
        df, dgpf = _rms_bwd(f_ref[...], gpf_ref[...], dx2)
        df_ref[...] = df.astype(BF16)
        _acc(dgpp_ref, dgpp, first)
        _acc(dgp_ref, dgp, first)
        _acc(dgpf_ref, dgpf, first)

        @pl.when(i == nt - 1)
        def _():
            dwg_ref[...] = acc_g[...].astype(BF16)
            dwp_ref[...] = acc_p[...].astype(BF16)

    return _launch(
        body, name=f"bwd_ple{layer}", grid=(nt,),
        in_specs=[_row_spec(D_MODEL)] * 5 + [_row_spec(PLE_DIM), _row_spec(D_MODEL), _full_spec((D_MODEL, D_MODEL)),
                  _vec_spec(), _vec_spec(), _vec_spec()],
        out_specs=[_row_spec(D_MODEL), _row_spec(D_MODEL), _full_spec((D_MODEL, D_MODEL)), _full_spec((PLE_DIM, D_MODEL)),
                   _vec_spec(), _vec_spec(), _vec_spec()],
        out_shape=[jax.ShapeDtypeStruct((T, D_MODEL), F32), jax.ShapeDtypeStruct((T, D_MODEL), BF16),
                   jax.ShapeDtypeStruct((D_MODEL, D_MODEL), BF16), jax.ShapeDtypeStruct((PLE_DIM, D_MODEL), BF16)]
                  + [jax.ShapeDtypeStruct((1, D_MODEL), F32)] * 3,
        scratch_shapes=[pltpu.VMEM((D_MODEL, D_MODEL), F32), pltpu.VMEM((PLE_DIM, D_MODEL), F32)],
        args=(dx3, x2, z, pe, h3, p, f, wgate, g_ple_post, g_ple, g_post_ffn), job=job)


def _bwd_ffn_act(layer, df, gs, us, wgu, wd, job=None):
    T = df.shape[0]
    tm = min(FFN_ROW_TILE, T)
    nt = T // tm
    sub = tm // FFN_SUB_TILES
    last = FF_CHUNKS - 1

    def body(df_ref, gs_ref, us_ref, wgu_ref, wd_ref, dh_ref, dg_ref, du_ref, a_ref, acc_h):
        k = pl.program_id(0)
        i = pl.program_id(1)
        rows = pl.ds(pl.multiple_of(i * tm, tm), tm)
        dhs = []
        for s in range(FFN_SUB_TILES):
            r = pl.ds(s * sub, sub)
            g = gs_ref[r, :].astype(F32)
            u = us_ref[r, :].astype(F32)
            sg = _sigmoid(g)
            silu = g * sg
            a_ref[r, :] = (silu * u).astype(BF16)
            da = _dot_nt(df_ref[r, :], wd_ref[...])
            dg = (da * u * (sg * (1.0 + g * (1.0 - sg)))).astype(BF16)
            du = (da * silu).astype(BF16)
            dg_ref[r, :] = dg
            du_ref[r, :] = du
            dhs.append(_dot_nt(dg, wgu_ref[0]) + _dot_nt(du, wgu_ref[1]))
        dh = jnp.concatenate(dhs, axis=0)

        @pl.when(k == 0)
        def _():
            acc_h[rows, :] = dh

        @pl.when(jnp.logical_and(k > 0, k < last))
        def _():
            acc_h[rows, :] += dh

        @pl.when(k == last)
        def _():
            dh_ref[...] = acc_h[rows, :] + dh

    chunk_rows = pl.BlockSpec((None, tm, FF_BLOCK), lambda k, i: (k, i, 0))
    saved = jax.ShapeDtypeStruct((FF_CHUNKS, T, FF_BLOCK), BF16)
    return _launch(
        body, name=f"bwd_ffn_act{layer}", grid=(FF_CHUNKS, nt),
        in_specs=[pl.BlockSpec((tm, D_MODEL), lambda k, i: (i, 0)), chunk_rows, chunk_rows,
                  pl.BlockSpec((None, 2, D_MODEL, FF_BLOCK), lambda k, i: (k, 0, 0, 0)),
                  pl.BlockSpec((FF_BLOCK, D_MODEL), lambda k, i: (k, 0))],
        out_specs=[pl.BlockSpec((tm, D_MODEL), lambda k, i: (jnp.where(k == last, i, 0), 0)),
                   chunk_rows, chunk_rows, chunk_rows],
        out_shape=[jax.ShapeDtypeStruct((T, D_MODEL), F32), saved, saved, saved],
        scratch_shapes=[pltpu.VMEM((T, D_MODEL), F32)],
        args=(df, gs, us, wgu, wd), vmem=VMEM_BIG, job=job)


def _bwd_ffn_dw(layer, q, h2, df, dg, du, a, job=None):
    T = h2.shape[0]

    def body(h_ref, df_ref, dg_ref, du_ref, a_ref, dgu_ref, dwd_ref):
        h = h_ref[...]
        dgu_ref[0] = _dot_tn(h, dg_ref[...]).astype(BF16)
        dgu_ref[1] = _dot_tn(h, du_ref[...]).astype(BF16)
        dwd_ref[...] = _dot_tn(a_ref[...], df_ref[...]).astype(BF16)

    cols = pl.BlockSpec((T, FF_PART), lambda k: (0, q))
    chunk = pl.BlockSpec((None, T, FF_BLOCK), lambda k: (k, 0, 0))
    return _launch(
        body, name=f"bwd_ffn_dw{layer}_{q}", grid=(FF_CHUNKS,),
        in_specs=[cols, cols, chunk, chunk, chunk],
        out_specs=[pl.BlockSpec((None, 2, FF_PART, FF_BLOCK), lambda k: (k, 0, 0, 0)),
                   pl.BlockSpec((FF_BLOCK, FF_PART), lambda k: (k, 0))],
        out_shape=[jax.ShapeDtypeStruct((FF_CHUNKS, 2, FF_PART, FF_BLOCK), BF16),
                   jax.ShapeDtypeStruct((D_FF, FF_PART), BF16)],
        args=(h2, df, dg, du, a), vmem=VMEM_BIG, job=job)


def _bwd_attn_out(dx2, dh2, x1, y, attn, wo, g_ffn, g_post, job=None):
    T = x1.shape[0]
    nt = T // ROW_TILE

    def body(dx2_ref, dh2_ref, x1_ref, y_ref, a_ref, wo_ref, gffn_ref, gpost_ref,
             dx1_ref, da_ref, dwo_ref, dgf_ref, dgp_ref, acc):
        i = pl.program_id(0)
        first = i == 0
        dxn, dgf = _rms_bwd(x1_ref[...], gffn_ref[...], dh2_ref[...])
        dx1 = dx2_ref[...] + dxn
        dx1_ref[...] = dx1
        dy, dgp = _rms_bwd(y_ref[...], gpost_ref[...], dx1)
        dyb = dy.astype(BF16)
        da_ref[...] = _dot_nt(dyb, wo_ref[...]).astype(BF16)
        _acc(acc, _dot_tn(a_ref[...], dyb), first)
        _acc(dgf_ref, dgf, first)
        _acc(dgp_ref, dgp, first)

        @pl.when(i == nt - 1)
        def _():
            dwo_ref[...] = acc[...].astype(BF16)

    return _launch(
        body, name="bwd_attn_out", grid=(nt,),
        in_specs=[_row_spec(D_MODEL)] * 5 + [_full_spec((D_MODEL, D_MODEL)), _vec_spec(), _vec_spec()],
        out_specs=[_row_spec(D_MODEL), _row_spec(D_MODEL), _full_spec((D_MODEL, D_MODEL)), _vec_spec(), _vec_spec()],
        out_shape=[jax.ShapeDtypeStruct((T, D_MODEL), F32), jax.ShapeDtypeStruct((T, D_MODEL), BF16),
                   jax.ShapeDtypeStruct((D_MODEL, D_MODEL), BF16)] + [jax.ShapeDtypeStruct((1, D_MODEL), F32)] * 2,
        scratch_shapes=[pltpu.VMEM((D_MODEL, D_MODEL), F32)],
        args=(dx2, dh2, x1, y, attn, wo, g_ffn, g_post), job=job)


def _bwd_attention(q, dattn, kpad, vpad, sinks, job=None):
    T = q.shape[0]
    nb = T // ATT_BLOCK

    def body(q_ref, do_ref, k_ref, v_ref, sink_ref, dq_ref, dk_ref, dv_ref, ds_ref):
        n = pl.program_id(0)

        @pl.when(n == 0)
        def _():
            dk_ref[...] = jnp.zeros_like(dk_ref)
            dv_ref[...] = jnp.zeros_like(dv_ref)
            ds_ref[...] = jnp.zeros_like(ds_ref)

        start = pl.multiple_of(n * ATT_BLOCK, ATT_BLOCK)
        win = pl.ds(start, 2 * ATT_BLOCK)
        kw = k_ref[win, :]
        vw = v_ref[win, :]
        relf, valid = _att_mask(n)
        lane = lax.broadcasted_iota(jnp.int32, (1, ATT_BLOCK), 1)
        dsink = jnp.zeros((1, ATT_BLOCK), F32)
        dqs, dks, dvs = [], [], []
        for kh in range(N_KV_HEADS):
            kk = kw[:, kh * HEAD_DIM:(kh + 1) * HEAD_DIM]
            vv = vw[:, kh * HEAD_DIM:(kh + 1) * HEAD_DIM]
            dk_h = jnp.zeros((2 * ATT_BLOCK, HEAD_DIM), F32)
            dv_h = jnp.zeros((2 * ATT_BLOCK, HEAD_DIM), F32)
            for gq in range(GQA_GROUP):
                h = kh * GQA_GROUP + gq
                qh = q_ref[:, h * HEAD_DIM:(h + 1) * HEAD_DIM]
                do = do_ref[:, h * HEAD_DIM:(h + 1) * HEAD_DIM]
                pr, ps = _att_probs(qh, kk, relf, valid, _alibi_slope(h), sink_ref[0, h])
                dp = _dot_nt(do, vv)
                delta = jnp.sum(pr * dp, axis=-1, keepdims=True)
                dsb = (pr * (dp - delta) * ATT_SCALE).astype(BF16)
                dsink = dsink + jnp.where(lane == h, -jnp.sum(ps * delta, axis=0, keepdims=True), 0.0)
                dqs.append(_dot(dsb, kk))
                dk_h = dk_h + _dot_tn(dsb, qh)
                dv_h = dv_h + _dot_tn(pr.astype(BF16), do)
            dks.append(dk_h)
            dvs.append(dv_h)
        dq_ref[...] = jnp.concatenate(dqs, axis=1).astype(BF16)
        dk_ref[win, :] += jnp.concatenate(dks, axis=1)
        dv_ref[win, :] += jnp.concatenate(dvs, axis=1)
        ds_ref[...] += dsink

    return _launch(
        body, name="bwd_attention", grid=(nb,),
        in_specs=[_row_spec(D_MODEL, ATT_BLOCK), _row_spec(D_MODEL, ATT_BLOCK), _full_spec((T + ATT_BLOCK, KV_DIM)),
                  _full_spec((T + ATT_BLOCK, KV_DIM)), pl.BlockSpec(memory_space=pltpu.SMEM)],
        out_specs=[_row_spec(D_MODEL, ATT_BLOCK), _full_spec((T + ATT_BLOCK, KV_DIM)), _full_spec((T + ATT_BLOCK, KV_DIM)),
                   _full_spec((1, ATT_BLOCK))],
        out_shape=[jax.ShapeDtypeStruct((T, D_MODEL), BF16), jax.ShapeDtypeStruct((T + ATT_BLOCK, KV_DIM), F32),
                   jax.ShapeDtypeStruct((T + ATT_BLOCK, KV_DIM), F32), jax.ShapeDtypeStruct((1, ATT_BLOCK), F32)],
        args=(q, dattn, kpad, vpad, sinks), vmem=VMEM_BIG, job=job)


def _bwd_qkv(dxres, dq, dkv, x3, h1, hk, wq, wkv, g_mix, g_kv, job=None):
    T = x3.shape[0]
    nt = T // ROW_TILE

    def body(dxr_ref, dq_ref, dkv_ref, x_ref, h1_ref, hk_ref, wq_ref, wkv_ref, gmix_ref, gkv_ref,
             dx_ref, dwq_ref, dwkv_ref, dgm_ref, dgk_ref, acc_q, acc_kv):
        i = pl.program_id(0)
        first = i == 0
        dqv = dq_ref[...]
        dkvv = dkv_ref[...]
        xv = x_ref[...]
        d1, dgm = _rms_bwd(xv, gmix_ref[...], _dot_nt(dqv, wq_ref[...]))
        d2, dgk = _rms_bwd(xv, gkv_ref[...], _dot_nt(dkvv, wkv_ref[...]))
        dx_ref[...] = dxr_ref[...] + d1 + d2
        _acc(acc_q, _dot_tn(h1_ref[...], dqv), first)
        _acc(acc_kv, _dot_tn(hk_ref[...], dkvv), first)
        _acc(dgm_ref, dgm, first)
        _acc(dgk_ref, dgk, first)

        @pl.when(i == nt - 1)
        def _():
            dwq_ref[...] = acc_q[...].astype(BF16)
            dwkv_ref[...] = acc_kv[...].astype(BF16)

    return _launch(
        body, name="bwd_qkv", grid=(nt,),
        in_specs=[_row_spec(D_MODEL), _row_spec(D_MODEL), _row_spec(2 * KV_DIM), _row_spec(D_MODEL), _row_spec(D_MODEL),
                  _row_spec(D_MODEL), _full_spec((D_MODEL, D_MODEL)), _full_spec((D_MODEL, 2 * KV_DIM)), _vec_spec(),
                  _vec_spec()],
        out_specs=[_row_spec(D_MODEL), _full_spec((D_MODEL, D_MODEL)), _full_spec((D_MODEL, 2 * KV_DIM)), _vec_spec(),
                   _vec_spec()],
        out_shape=[jax.ShapeDtypeStruct((T, D_MODEL), F32), jax.ShapeDtypeStruct((D_MODEL, D_MODEL), BF16),
                   jax.ShapeDtypeStruct((D_MODEL, 2 * KV_DIM), BF16)] + [jax.ShapeDtypeStruct((1, D_MODEL), F32)] * 2,
        scratch_shapes=[pltpu.VMEM((D_MODEL, D_MODEL), F32), pltpu.VMEM((D_MODEL, 2 * KV_DIM), F32)],
        args=(dxres, dq, dkv, x3, h1, hk, wq, wkv, g_mix, g_kv), job=job)


def _bwd_pool_mixer(dx2, dh2, x1, x, yraw, d, wp, scale, g_ffn, g_post, g_pre, job=None):
    T = x.shape[0]
    tm = ROW_TILE
    nt = T // tm

    def body(dx2_ref, dh2_ref, x1_ref, x_ref, yraw_ref, d_ref, wp_ref, sc_ref, gffn_ref, gpost_ref, gpre_ref,
             dx_ref, dwp_ref, dsc_ref, dgf_ref, dgp_ref, dgm_ref, carry, acc):
        i = pl.program_id(0)
        first = i == 0
        tile = nt - 1 - i

        @pl.when(first)
        def _():
            carry[...] = jnp.zeros_like(carry)

        dxn, dgf = _rms_bwd(x1_ref[...], gffn_ref[...], dh2_ref[...])
        dx1 = dx2_ref[...] + dxn
        yraw = yraw_ref[...]
        sc = sc_ref[...]
        dy, dgp = _rms_bwd(yraw * sc, gpost_ref[...], dx1)
        dsc = jnp.sum(dy * yraw, axis=0, keepdims=True)
        dyb = (dy * sc).astype(BF16)
        dv = d_ref[...]
        dds = []
        for g in range(N_POOL_GROUPS):
            cols = slice(g * POOL_GROUP, (g + 1) * POOL_GROUP)
            dds.append(_dot_nt(dyb[:, cols], wp_ref[g]))
            _acc(acc.at[g], _dot_tn(dv[:, cols], dyb[:, cols]), first)
        dd = jnp.concatenate(dds, axis=1)
        e = dd / _pool_counts(tile * tm, tm)
        ext = jnp.concatenate([e, carry[...]], axis=0)
        carry[...] = e[:POOL_HALO, :]
        sums = _window_sums(ext, lambda k: tm + POOL_HALO - k)[:tm, :]
        dxm, dgm = _rms_bwd(x_ref[...], gpre_ref[...], sums - dd)
        dx_ref[...] = dx1 + dxm
        _acc(dsc_ref, dsc, first)
        _acc(dgf_ref, dgf, first)
        _acc(dgp_ref, dgp, first)
        _acc(dgm_ref, dgm, first)

        @pl.when(i == nt - 1)
        def _():
            dwp_ref[...] = acc[...].astype(BF16)

    rev = pl.BlockSpec((tm, D_MODEL), lambda i: (nt - 1 - i, 0))
    return _launch(
        body, name="bwd_pool_mixer", grid=(nt,),
        in_specs=[rev] * 6 + [_full_spec((N_POOL_GROUPS, POOL_GROUP, POOL_GROUP))] + [_vec_spec()] * 4,
        out_specs=[rev, _full_spec((N_POOL_GROUPS, POOL_GROUP, POOL_GROUP))] + [_vec_spec()] * 4,
        out_shape=[jax.ShapeDtypeStruct((T, D_MODEL), F32),
                   jax.ShapeDtypeStruct((N_POOL_GROUPS, POOL_GROUP, POOL_GROUP), BF16)]
                  + [jax.ShapeDtypeStruct((1, D_MODEL), F32)] * 4,
        scratch_shapes=[pltpu.VMEM((POOL_HALO, D_MODEL), F32), pltpu.VMEM((N_POOL_GROUPS, POOL_GROUP, POOL_GROUP), F32)],
        args=(dx2, dh2, x1, x, yraw, d, wp, scale, g_ffn, g_post, g_pre), job=job)


def _my_place():
    return lax.axis_index("x"), lax.axis_index("y"), lax.axis_index("c")


def _dev_index(px, py, pc):
    return 4 * px + 2 * py + pc


def _peer_by_relation(r):
    x, y, c = _my_place()
    return (x ^ ((r >> 2) & 1), y ^ ((r >> 1) & 1), c ^ (r & 1))


def _slot_pool(ref, j):
    return ref.at[:, pl.ds(pl.multiple_of(j * 32, 32), 32), :]


def _slot_scale(ref, j):
    return ref.at[:, pl.ds(pl.multiple_of(j * 128, 128), 128)]


def _slot_rows128(ref, j):
    return ref.at[pl.ds(pl.multiple_of(j * 128, 128), 128), :]


def _slot_gu(ref, j):
    return ref.at[j % FF_CHUNKS, j // FF_CHUNKS]


def _slot_wd(ref, j):
    return ref.at[pl.ds(pl.multiple_of(j * WD_ROWS, 16), WD_ROWS), :]


def _slot_cols128(ref, j):
    return ref.at[:, pl.ds(pl.multiple_of(j * 128, 128), 128)]


_GATHERED = {
    "pool": ((N_POOL_GROUPS, POOL_GROUP, POOL_GROUP), BF16, _slot_pool),
    "scale": ((1, D_MODEL), F32, _slot_scale),
    "kv": ((D_MODEL, 2 * KV_DIM), BF16, _slot_rows128),
    "q": ((D_MODEL, D_MODEL), BF16, _slot_rows128),
    "o": ((D_MODEL, D_MODEL), BF16, _slot_rows128),
    "gu": ((FF_CHUNKS, 2, D_MODEL, FF_BLOCK), BF16, _slot_gu),
    "wd": ((D_FF, D_MODEL), BF16, _slot_wd),
    "gate": ((D_MODEL, D_MODEL), BF16, _slot_rows128),
    "proj": ((PLE_DIM, D_MODEL), BF16, _slot_cols128),
}


def _no_compute():
    pass


class _AllGather:
    def __init__(self, names, shards):
        self.kinds = [_GATHERED[n.rstrip("01")] for n in names]
        self.args = [shards[n] for n in names]
        self.out_shape = [jax.ShapeDtypeStruct(shape, dtype) for shape, dtype, _ in self.kinds]
        n = len(names)
        self.scratch = [pltpu.SemaphoreType.DMA((n, 7)), pltpu.SemaphoreType.DMA((n, 7)), pltpu.SemaphoreType.DMA((n,))]

    def _copies(self, srcs, outs, sems):
        send_sems, recv_sems, local_sems = sems
        x, y, c = _my_place()
        me, sibling = (x, y, c), (x, y, 1 - c)
        chips = [(1 - x, y), (x, 1 - y), (1 - x, 1 - y)]
        n = len(srcs)

        def slot(t, dev):
            return self.kinds[t][2](outs[t], _dev_index(*dev))

        def copy(t, k, block, to, src=None):
            return pltpu.make_async_remote_copy(
                src_ref=slot(t, block) if src is None else src, dst_ref=slot(t, block),
                send_sem=send_sems.at[t, k], recv_sem=recv_sems.at[t, k], device_id=to, device_id_type=MESH)

        mine = [pltpu.make_async_copy(srcs[t], slot(t, me), local_sems.at[t]) for t in range(n)]
        first = []
        for t in range(n):
            first.append(copy(t, 0, me, sibling, src=srcs[t]))
            first += [copy(t, 1 + j, me, (*chip, c), src=srcs[t]) for j, chip in enumerate(chips)]
        return me, sibling, chips, copy, mine, first

    def start(self, srcs, outs, sems):
        _, _, _, _, mine, first = self._copies(srcs, outs, sems)
        for cp in mine + first:
            cp.start()

    def finish(self, srcs, outs, sems):
        me, sibling, chips, copy, mine, first = self._copies(srcs, outs, sems)
        c = me[2]
        n = len(srcs)
        passed = []
        for j, chip in enumerate(chips):
            for t in range(n):
                copy(t, 1 + j, (*chip, c), me).wait_recv()
                fwd = copy(t, 4 + j, (*chip, c), sibling)
                fwd.start()
                passed.append(fwd)
        for t in range(n):
            copy(t, 0, sibling, me).wait_recv()
            for j, chip in enumerate(chips):
                copy(t, 4 + j, (*chip, 1 - c), me).wait_recv()
        for cp in first + passed:
            cp.wait_send()
        for cp in mine:
            cp.wait()


def _all_gather_only(name, names, shards):
    return _launch(_no_compute, name=name, grid=(), in_specs=[], out_specs=[], out_shape=[], args=(),
                   job=_AllGather(names, shards))[1]


def _block_pool(ref, j):
    return ref.at[:, pl.ds(pl.multiple_of(j * 32, 32), 32), :]


def _block_rows128(ref, j):
    return ref.at[pl.ds(pl.multiple_of(j * 128, 128), 128), :]


def _block_gu(ref, j):
    return ref.at[j % FF_CHUNKS, j // FF_CHUNKS]


def _block_wd(ref, j):
    return ref.at[pl.ds(pl.multiple_of(j * WD_ROWS, 16), WD_ROWS), :]


def _block_cols128(ref, j):
    return ref.at[:, pl.ds(pl.multiple_of(j * 128, 128), 128)]


_SCATTERED = {
    "pool": ((N_POOL_GROUPS, 32, POOL_GROUP), _block_pool),
    "kv": ((128, 2 * KV_DIM), _block_rows128),
    "q": ((128, D_MODEL), _block_rows128),
    "o": ((128, D_MODEL), _block_rows128),
    "gu": ((FF_PART, FF_BLOCK), _block_gu),
    "wd": ((WD_ROWS, FF_PART), _block_wd),
    "gate": ((128, D_MODEL), _block_rows128),
    "proj": ((PLE_DIM, 128), _block_cols128),
}


class _SiblingSwap:
    def __init__(self, pieces):
        self.kinds = [_SCATTERED[kind] for kind, _ in pieces]
        self.args = [g for _, g in pieces]
        self.out_shape = [jax.ShapeDtypeStruct((N_CHIPS, *block), BF16) for block, _ in self.kinds]
        n = len(pieces)
        self.scratch = [pltpu.SemaphoreType.DMA((n, N_CHIPS)), pltpu.SemaphoreType.DMA((n, N_CHIPS))]

    def _copies(self, srcs, outs, sems):
        send_sems, recv_sems = sems
        x, y, c = _my_place()
        return [pltpu.make_async_remote_copy(
            src_ref=block(srcs[t], 2 * ch + 1 - c), dst_ref=outs[t].at[ch], send_sem=send_sems.at[t, ch],
            recv_sem=recv_sems.at[t, ch], device_id=(x, y, 1 - c), device_id_type=MESH)
            for t, (_, block) in enumerate(self.kinds) for ch in range(N_CHIPS)]

    def start(self, srcs, outs, sems):
        for cp in self._copies(srcs, outs, sems):
            cp.start()

    def finish(self, srcs, outs, sems):
        for cp in self._copies(srcs, outs, sems):
            cp.wait()


class _ChipScatter:
    def __init__(self, pieces):
        self.kinds = [_SCATTERED[kind] for kind, _, _ in pieces]
        self.n = n = len(pieces)
        self.args = [g for _, g, _ in pieces] + [s for _, _, s in pieces]
        self.out_shape = [jax.ShapeDtypeStruct((N_CHIPS, *block), BF16) for block, _ in self.kinds]
        self.scratch = []
        for block, _ in self.kinds:
            self.scratch += [pltpu.VMEM((N_CHIPS, *block), BF16)] * 3
        self.scratch += [pltpu.SemaphoreType.DMA((n, N_CHIPS + 1)), pltpu.SemaphoreType.DMA((n, N_CHIPS - 1)),
                         pltpu.SemaphoreType.DMA((n, N_CHIPS - 1)), pltpu.SemaphoreType.DMA((n,))]

    def _sends(self, outs, scr):
        n = self.n
        send_sems, recv_sems, local_sems = scr[3 * n + 1:]
        x, y, c = _my_place()
        chip = 2 * x + y
        copies = []
        for t in range(n):
            total = scr[3 * t + 2]
            copies.append(pltpu.make_async_copy(total.at[chip], outs[t].at[chip], local_sems.at[t]))
            for r in range(1, N_CHIPS):
                to = chip ^ r
                copies.append(pltpu.make_async_remote_copy(
                    src_ref=total.at[to], dst_ref=outs[t].at[chip], send_sem=send_sems.at[t, r - 1],
                    recv_sem=recv_sems.at[t, r - 1], device_id=(to // 2, to % 2, c), device_id_type=MESH))
        return copies

    def start(self, ins, outs, scr):
        n = self.n
        load_sems = scr[3 * n]
        c = lax.axis_index("c")
        loads = []
        for t, (_, block) in enumerate(self.kinds):
            mine, theirs = scr[3 * t], scr[3 * t + 1]
            loads += [pltpu.make_async_copy(block(ins[t], 2 * ch + c), mine.at[ch], load_sems.at[t, ch])
                      for ch in range(N_CHIPS)]
            loads.append(pltpu.make_async_copy(ins[n + t], theirs, load_sems.at[t, N_CHIPS]))
        for cp in loads:
            cp.start()
        for cp in loads:
            cp.wait()
        for t in range(n):
            mine, theirs, total = scr[3 * t:3 * t + 3]
            for ch in range(N_CHIPS):
                total[ch] = (mine[ch].astype(F32) + theirs[ch].astype(F32)).astype(BF16)
        for cp in self._sends(outs, scr):
            cp.start()

    def finish(self, ins, outs, scr):
        for cp in self._sends(outs, scr):
            cp.wait()


class _Jobs:
    def __init__(self, *jobs):
        self.jobs = jobs
        self.args = [a for j in jobs for a in j.args]
        self.out_shape = [o for j in jobs for o in j.out_shape]
        self.scratch = [s for j in jobs for s in j.scratch]

    def _split(self, refs, attr):
        at = 0
        for j in self.jobs:
            n = len(getattr(j, attr))
            yield refs[at:at + n]
            at += n

    def _each(self, ins, outs, scr):
        return zip(self.jobs, self._split(ins, "args"), self._split(outs, "out_shape"), self._split(scr, "scratch"))

    def start(self, ins, outs, scr):
        for j, i, o, s in self._each(ins, outs, scr):
            j.start(i, o, s)

    def finish(self, ins, outs, scr):
        for j, i, o, s in self._each(ins, outs, scr):
            j.finish(i, o, s)

    def split_outputs(self, outs):
        return list(self._split(outs, "out_shape"))


def _adamw_math(w, g, m, v):
    m = ADAM_B1 * m + (1.0 - ADAM_B1) * g
    v = ADAM_B2 * v + (1.0 - ADAM_B2) * (g * g)
    m_hat = m / (1.0 - ADAM_B1 ** ADAM_STEP)
    v_hat = v / (1.0 - ADAM_B2 ** ADAM_STEP)
    delta = -ADAM_LR * (m_hat / (jnp.sqrt(v_hat) + ADAM_EPS) + ADAM_WD * w)
    return delta, m, v


def _adamw(name, w, m, v, landings, n_col_blocks=1, job=None):
    _, r, c = landings[0].shape
    grid = (w.shape[0] // r, n_col_blocks)

    def body(w_ref, m_ref, v_ref, *rest):
        l_refs, (g_ref, d_ref, nm_ref, nv_ref) = rest[:len(landings)], rest[len(landings):]
        step = pl.program_id(0) * n_col_blocks + pl.program_id(1)
        for idx, l_ref in enumerate(l_refs):
            @pl.when(step == idx)
            def _(l_ref=l_ref):
                g = l_ref[0].astype(F32)
                for s in range(1, N_CHIPS):
                    g = g + l_ref[s].astype(F32)
                g_ref[...] = g
                d_ref[...], nm_ref[...], nv_ref[...] = _adamw_math(w_ref[...], g, m_ref[...], v_ref[...])

    spec = pl.BlockSpec((r, c), lambda a, b: (a, b))
    return _launch(
        body, name=f"adamw_{name}", grid=grid,
        in_specs=[spec, spec, spec] + [_full_spec((N_CHIPS, r, c))] * len(landings),
        out_specs=[spec] * 4, out_shape=[jax.ShapeDtypeStruct(w.shape, F32)] * 4,
        args=(w, m, v, *landings), vmem=VMEM_BIG, job=job)


def _small_all_reduce_adamw(part, w, m, v):
    def body(part_ref, w_ref, m_ref, v_ref, g_ref, d_ref, nm_ref, nv_ref, buf, send_sems, recv_sems):
        x, y, c = _my_place()
        me = _dev_index(x, y, c)
        buf[me] = part_ref[...]
        copies = [pltpu.make_async_remote_copy(
            src_ref=part_ref, dst_ref=buf.at[me], send_sem=send_sems.at[r - 1], recv_sem=recv_sems.at[r - 1],
            device_id=_peer_by_relation(r), device_id_type=MESH) for r in range(1, N_DEV)]
        for cp in copies:
            cp.start()
        for cp in copies:
            cp.wait()
        g = buf[0]
        for s in range(1, N_DEV):
            g = g + buf[s]
        g_ref[...] = g
        d_ref[...], nm_ref[...], nv_ref[...] = _adamw_math(w_ref[...], g, m_ref[...], v_ref[...])

    vm = pl.BlockSpec(memory_space=pltpu.VMEM)
    return pl.pallas_call(
        body, name="small_all_reduce_adamw", out_shape=[jax.ShapeDtypeStruct((SV_ROWS, D_MODEL), F32)] * 4,
        in_specs=[vm] * 4, out_specs=[vm] * 4,
        scratch_shapes=[pltpu.VMEM((N_DEV, SV_ROWS, D_MODEL), F32), pltpu.SemaphoreType.DMA((N_DEV - 1,)),
                        pltpu.SemaphoreType.DMA((N_DEV - 1,))],
    )(part, w, m, v)


def _local_step(x, p, tgt, gains, sinks, shards, weights):
    row = _Gain
    gather = lambda *names: _AllGather(names, shards)
    g_pre_mix, g_post_mix = gains["pre_mix_g"], gains["post_mix_g"]
    g_pre_ffn, g_post_ffn = gains["pre_ffn_g"], gains["post_ffn_g"]
    g_ple, g_ple_post, g_kv = gains["ple_g"], gains["ple_post_g"], _Gain(gains["kv_g"], 0)

    wp, scale, wgu0 = _all_gather_only("gather_first", ("pool", "scale", "gu0"), shards)
    (x1_0, h2_0, yraw, dpool), (wd0,) = _fwd_pool_mixer(
        x, row(g_pre_mix, 0), wp, scale, row(g_post_mix, 0), row(g_pre_ffn, 0), job=gather("wd0"))
    (gs0, us0, f0, x2_0, h3_0), (wgate0, wproj0, wgu1) = _fwd_ffn(
        0, h2_0, x1_0, wgu0, wd0, row(g_post_ffn, 0), row(g_ple, 0), job=gather("gate0", "proj0", "gu1"))
    (x3_0, z0, pe0), (wkv, wq) = _fwd_ple(0, x2_0, h3_0, p[0], wgate0, wproj0, row(g_ple_post, 0),
                                          job=gather("kv", "q"))
    (hk, h1, q, kv), (wo,) = _fwd_qkv(x3_0, g_kv, row(g_pre_mix, 1), wkv, wq, job=gather("o"))
    front = ((ATT_BLOCK, 0), (0, 0))
    kpad = jnp.pad(kv[:, :KV_DIM], front)
    vpad = jnp.pad(kv[:, KV_DIM:], front)
    (attn,), (wd1,) = _fwd_attention(q, kpad, vpad, sinks, job=gather("wd1"))
    (y1, x1_1, h2_1), _ = _fwd_attn_out(attn, x3_0, wo, row(g_post_mix, 1), row(g_pre_ffn, 1))
    (gs1, us1, f1, x2_1, h3_1), (wgate1, wproj1) = _fwd_ffn(
        1, h2_1, x1_1, wgu1, wd1, row(g_post_ffn, 1), row(g_ple, 1), job=gather("gate1", "proj1"))
    (dx3_1, z1, pe1, loss), _ = _fwd_ple(1, x2_1, h3_1, p[1], wgate1, wproj1, row(g_ple_post, 1), target=tgt)

    produced, swapped, landed = {}, {}, {}

    def kind_of(name):
        return name.rstrip("0123_")

    def carry(swap=(), spread=()):
        jobs = []
        if swap:
            jobs.append(_SiblingSwap([(kind_of(n), produced[n]) for n in swap]))
        if spread:
            jobs.append(_ChipScatter([(kind_of(n), produced[n], swapped[n]) for n in spread]))
        return _Jobs(*jobs)

    def carried(jobs, outs, swap=(), spread=()):
        parts = jobs.split_outputs(outs)
        if swap:
            swapped.update(zip(swap, parts[0]))
        if spread:
            landed.update(zip(spread, parts[-1]))

    def hosted(call, *args, swap=(), spread=()):
        jobs = carry(swap, spread)
        outs, job_outs = call(*args, job=jobs)
        carried(jobs, job_outs, swap, spread)
        return outs

    def ffn_weight_grads(layer, h2, df, dg, du, a, hosts):
        for qtr in range(FF_PARTS):
            dgu, dwd = hosted(_bwd_ffn_dw, layer, qtr, h2, df, dg, du, a, **hosts[qtr])
            produced[f"gu{layer}_{qtr}"], produced[f"wd{layer}_{qtr}"] = dgu, dwd

    ffn_q = lambda layer, qtr: (f"gu{layer}_{qtr}", f"wd{layer}_{qtr}")

    dx2_1, df1, produced["gate1"], produced["proj1"], dg_ple_post1, dg_ple1, dg_post_ffn1 = hosted(
        _bwd_ple, 1, dx3_1, x2_1, z1, pe1, h3_1, p[1], f1, wgate1, row(g_ple_post, 1), row(g_ple, 1),
        row(g_post_ffn, 1))
    dh2_1, dg1, du1, a1 = hosted(_bwd_ffn_act, 1, df1, gs1, us1, wgu1, wd1, swap=("gate1", "proj1"))
    ffn_weight_grads(1, h2_1, df1, dg1, du1, a1, [dict(spread=("gate1", "proj1")), dict(swap=ffn_q(1, 0))])
    dx1_1, dattn, produced["o"], dg_pre_ffn1, dg_post_mix1 = hosted(
        _bwd_attn_out, dx2_1, dh2_1, x1_1, y1, attn, wo, row(g_pre_ffn, 1), row(g_post_mix, 1), swap=ffn_q(1, 1))
    dq, dkpad, dvpad, dsinks = hosted(_bwd_attention, q, dattn, kpad, vpad, sinks,
                                      spread=ffn_q(1, 0) + ffn_q(1, 1))
    dkv = jnp.concatenate([dkpad[ATT_BLOCK:], dvpad[ATT_BLOCK:]], axis=1).astype(BF16)
    dx3_0, produced["q"], produced["kv"], dg_pre_mix1, dg_kv = hosted(
        _bwd_qkv, dx1_1, dq, dkv, x3_0, h1, hk, wq, wkv, row(g_pre_mix, 1), g_kv, swap=("o",))
    dx2_0, df0, produced["gate0"], produced["proj0"], dg_ple_post0, dg_ple0, dg_post_ffn0 = hosted(
        _bwd_ple, 0, dx3_0, x2_0, z0, pe0, h3_0, p[0], f0, wgate0, row(g_ple_post, 0), row(g_ple, 0),
        row(g_post_ffn, 0), swap=("q", "kv"), spread=("o",))
    dh2_0, dg0, du0, a0 = hosted(_bwd_ffn_act, 0, df0, gs0, us0, wgu0, wd0,
                                 swap=("gate0", "proj0"), spread=("q", "kv"))
    ffn_weight_grads(0, h2_0, df0, dg0, du0, a0, [dict(spread=("gate0", "proj0")), dict(swap=ffn_q(0, 0))])
    grad_x, produced["pool"], dscale, dg_pre_ffn0, dg_post_mix0, dg_pre_mix0 = hosted(
        _bwd_pool_mixer, dx2_0, dh2_0, x1_0, x, yraw, dpool, wp, scale, row(g_pre_ffn, 0), row(g_post_mix, 0),
        row(g_pre_mix, 0), swap=ffn_q(0, 1), spread=ffn_q(0, 0))

    def update(name, n_col_blocks=1, pieces=None, swap=(), spread=()):
        w, m, v = weights[name]
        rows = w.size // w.shape[-1]
        flat = [landed[n].reshape(N_CHIPS, -1, landed[n].shape[-1]) for n in (pieces or [kind_short[name]])]
        outs = hosted(_adamw, name, w.reshape(rows, -1), m.reshape(rows, -1), v.reshape(rows, -1), flat,
                      n_col_blocks, swap=swap, spread=spread)
        return [o.reshape(w.shape) for o in outs]

    kind_short = {"w_q": "q", "w_kv": "kv", "w_o": "o", "pool_w": "pool"}
    upd = {}
    upd["w_ple_gate"] = update("w_ple_gate", pieces=("gate0", "gate1"), swap=("pool",), spread=ffn_q(0, 1))
    upd["w_ple_proj"] = update("w_ple_proj", pieces=("proj0", "proj1"), spread=("pool",))
    for name in ("w_q", "w_kv", "w_o", "pool_w"):
        upd[name] = update(name)
    upd["w_gu"] = update("w_gu", pieces=[f"gu{layer}_{qtr}" for layer in range(2) for qtr in range(FF_PARTS)])
    upd["w_down"] = update("w_down", FF_PARTS,
                           pieces=[f"wd{layer}_{qtr}" for layer in range(2) for qtr in range(FF_PARTS)])

    lanes = lambda a: jnp.pad(a, ((0, 0), (0, D_MODEL - a.shape[1])))
    small = jnp.concatenate([
        dg_pre_mix0, dg_pre_mix1, dg_post_mix0, dg_post_mix1, dg_pre_ffn0, dg_pre_ffn1, dg_post_ffn0, dg_post_ffn1,
        dg_ple0, dg_ple1, dg_ple_post0, dg_ple_post1, dg_kv, dscale, lanes(dsinks[:, :N_HEADS]), lanes(loss)], axis=0)
    return grad_x, upd, small


def kernel(x, p, pre_mix_g, post_mix_g, pre_ffn_g, post_ffn_g, pool_w, pool_scale, kv_g, w_kv, w_q, sinks, w_o, w_gu, w_down, ple_g, w_ple_gate, w_ple_proj, ple_post_g, loss_target, m_pre_mix_g, m_post_mix_g, m_pre_ffn_g, m_post_ffn_g, m_pool_w, m_pool_scale, m_kv_g, m_w_kv, m_w_q, m_sinks, m_w_o, m_w_gu, m_w_down, m_ple_g, m_w_ple_gate, m_w_ple_proj, m_ple_post_g, v_pre_mix_g, v_post_mix_g, v_pre_ffn_g, v_post_ffn_g, v_pool_w, v_pool_scale, v_kv_g, v_w_kv, v_w_q, v_sinks, v_w_o, v_w_gu, v_w_down, v_ple_g, v_w_ple_gate, v_w_ple_proj, v_ple_post_g):
    me = _dev_index(*_my_place())

    shards = {"pool": pool_w[0].astype(BF16), "scale": pool_scale, "kv": w_kv.astype(BF16),
              "q": w_q[0].astype(BF16), "o": w_o[0].astype(BF16)}
    for layer in range(2):
        shards[f"gu{layer}"] = w_gu[layer].astype(BF16)
        shards[f"wd{layer}"] = w_down[layer].astype(BF16)
        shards[f"gate{layer}"] = w_ple_gate[layer].astype(BF16)
        shards[f"proj{layer}"] = w_ple_proj[layer].astype(BF16)
    stacked = lambda g: g.reshape(-1, 1, D_MODEL)
    gains = dict(pre_mix_g=stacked(pre_mix_g), post_mix_g=stacked(post_mix_g), pre_ffn_g=stacked(pre_ffn_g),
                 post_ffn_g=stacked(post_ffn_g), ple_g=stacked(ple_g), ple_post_g=stacked(ple_post_g),
                 kv_g=stacked(kv_g))
    weights = {"pool_w": (pool_w, m_pool_w, v_pool_w), "w_kv": (w_kv, m_w_kv, v_w_kv), "w_q": (w_q, m_w_q, v_w_q),
               "w_o": (w_o, m_w_o, v_w_o), "w_gu": (w_gu, m_w_gu, v_w_gu), "w_down": (w_down, m_w_down, v_w_down),
               "w_ple_gate": (w_ple_gate, m_w_ple_gate, v_w_ple_gate),
               "w_ple_proj": (w_ple_proj, m_w_ple_proj, v_w_ple_proj)}
    grad_x, upd, small = _local_step(x[0], p[:, 0], loss_target[0], gains, sinks, shards, weights)

    lane0 = me * 128

    def slab(pre_mix, post_mix, pre_ffn, post_ffn, ple, ple_post, kv, scale_shard, snk):
        scale_row = lax.dynamic_update_slice(jnp.zeros((1, D_MODEL), F32), scale_shard, (0, lane0))
        snk_row = jnp.pad(snk, ((0, 0), (0, D_MODEL - N_HEADS)))
        return jnp.concatenate([pre_mix, post_mix, pre_ffn, post_ffn, ple, ple_post, kv[None, :], scale_row, snk_row,
                                jnp.zeros((1, D_MODEL), F32)], axis=0)

    sw = slab(pre_mix_g, post_mix_g, pre_ffn_g, post_ffn_g, ple_g, ple_post_g, kv_g, pool_scale, sinks)
    sm = slab(m_pre_mix_g, m_post_mix_g, m_pre_ffn_g, m_post_ffn_g, m_ple_g, m_ple_post_g, m_kv_g, m_pool_scale, m_sinks)
    sv = slab(v_pre_mix_g, v_post_mix_g, v_pre_ffn_g, v_post_ffn_g, v_ple_g, v_ple_post_g, v_kv_g, v_pool_scale, v_sinks)
    sg, sd, snm, snv = _small_all_reduce_adamw(small, sw, sm, sv)
    loss = sg[SV_LOSS, 0]

    def unslab(s):
        return {
            "pre_mix_g": s[SV_PRE_MIX:SV_PRE_MIX + 2], "post_mix_g": s[SV_POST_MIX:SV_POST_MIX + 2],
            "pre_ffn_g": s[SV_PRE_FFN:SV_PRE_FFN + 2], "post_ffn_g": s[SV_POST_FFN:SV_POST_FFN + 2],
            "ple_g": s[SV_PLE:SV_PLE + 2], "ple_post_g": s[SV_PLE_POST:SV_PLE_POST + 2], "kv_g": s[SV_KV],
            "pool_scale": lax.dynamic_slice(s, (SV_POOL_SCALE, lane0), (1, 128)),
            "sinks": s[SV_SINKS:SV_SINKS + 1, :N_HEADS],
        }

    names = ["pre_mix_g", "post_mix_g", "pre_ffn_g", "post_ffn_g", "pool_w", "pool_scale", "kv_g", "w_kv", "w_q",
             "sinks", "w_o", "w_gu", "w_down", "ple_g", "w_ple_gate", "w_ple_proj", "ple_post_g"]
    outs = [loss, grad_x[None]]
    for kind, slab_out in enumerate((sg, sd, snm, snv)):
        small_out = unslab(slab_out)
        outs += [upd[n][kind] if n in upd else small_out[n] for n in names]
    return tuple(outs)
```

```python
import functools

import jax
import jax.numpy as jnp
from jax import lax
from jax.experimental import pallas as pl
from jax.experimental.pallas import tpu as pltpu

F32 = jnp.float32
BF16 = jnp.bfloat16

N_DEV = 8
D_MODEL = 1024
N_POOL_GROUPS = 4
POOL_GROUP = 256
POOL_HALO = 16
HEAD_DIM = 64
N_HEADS = 16
N_KV_HEADS = 4
GQA_GROUP = 4
KV_DIM = N_KV_HEADS * HEAD_DIM
ATT_BLOCK = 128
D_FF = 2816
FF_CHUNKS = 4
FF_BLOCK = D_FF // FF_CHUNKS
WD_ROWS = D_FF // N_DEV
FF_PARTS = 2
FF_PART = D_MODEL // FF_PARTS
N_CHIPS = 4
PLE_DIM = 256
EPS = 1e-6
NEG_INF = -1e30
ATT_SCALE = HEAD_DIM ** -0.5

ADAM_LR = 0.001
ADAM_B1 = 0.9
ADAM_B2 = 0.999
ADAM_EPS = 1e-08
ADAM_WD = 0.01
ADAM_STEP = 10

ROW_TILE = 512
FFN_ROW_TILE = 512
FFN_SUB_TILES = 2
VMEM_BIG = 56 * 1024 * 1024
VMEM_MID = 48 * 1024 * 1024
HBM_PIN_ELEMS = 1024

SV_ROWS = 16
SV_PRE_MIX, SV_POST_MIX, SV_PRE_FFN, SV_POST_FFN, SV_PLE, SV_PLE_POST = 0, 2, 4, 6, 8, 10
SV_KV, SV_POOL_SCALE, SV_SINKS, SV_LOSS = 12, 13, 14, 15

MESH = pl.DeviceIdType.MESH
ANY = pl.BlockSpec(memory_space=pl.ANY)


def _dot(a, b):
    return jnp.dot(a, b, preferred_element_type=F32)


def _dot_nt(a, b):
    return lax.dot_general(a, b, (((1,), (1,)), ((), ())), preferred_element_type=F32)


def _dot_tn(a, b):
    return lax.dot_general(a, b, (((0,), (0,)), ((), ())), preferred_element_type=F32)


def _rstd(x):
    return lax.rsqrt(jnp.mean(x * x, axis=-1, keepdims=True) + EPS)


def _rms(x, g):
    return x * _rstd(x) * g


def _rms_bwd(x, g, dy):
    r = _rstd(x)
    n = x * r
    dn = dy * g
    dx = r * (dn - n * jnp.mean(dn * n, axis=-1, keepdims=True))
    dg = jnp.sum(dy * n, axis=0, keepdims=True)
    return dx, dg


def _sigmoid(x):
    return 1.0 / (1.0 + jnp.exp(-x))


def _acc(ref, val, first):
    @pl.when(first)
    def _():
        ref[...] = val

    @pl.when(jnp.logical_not(first))
    def _():
        ref[...] += val


def _pool_counts(row0, rows):
    t = row0 + lax.broadcasted_iota(jnp.int32, (rows, D_MODEL), 0) + 1
    grp = lax.broadcasted_iota(jnp.int32, (rows, D_MODEL), 1) // POOL_GROUP
    win = jnp.left_shift(2, grp)
    return jnp.minimum(t, win).astype(F32)


def _window_sums(ext, shift_of):
    outs = []
    s = ext
    for gi in range(N_POOL_GROUPS):
        s = s + pltpu.roll(s, shift_of(1 << gi), axis=0)
        outs.append(s[:, :POOL_GROUP])
        s = s[:, POOL_GROUP:]
    return jnp.concatenate(outs, axis=1)


def _cparams(n_axes, vmem):
    return pltpu.CompilerParams(dimension_semantics=("arbitrary",) * n_axes, vmem_limit_bytes=vmem)


def _row_spec(cols, tm=ROW_TILE):
    return pl.BlockSpec((tm, cols), lambda i: (i, 0))


def _full_spec(shape):
    zeros = (0,) * len(shape)
    return pl.BlockSpec(shape, lambda *_: zeros)


def _vec_spec():
    return _full_spec((1, D_MODEL))


class _Gain:
    def __init__(self, stacked, layer):
        self.stacked, self.layer = stacked, layer

    def spec(self):
        layer = self.layer
        return pl.BlockSpec((None, 1, D_MODEL), lambda *_: (layer, 0, 0))


def _in_hbm(a):
    return pltpu.with_memory_space_constraint(a, pltpu.HBM) if a.size >= HBM_PIN_ELEMS else a


def _launch(body, *, name, grid, in_specs, out_specs, out_shape, args, scratch_shapes=(), vmem=VMEM_MID, job=None):
    in_specs = [a.spec() if isinstance(a, _Gain) else s for s, a in zip(in_specs, args)]
    args = [_in_hbm(a.stacked if isinstance(a, _Gain) else a) for a in args]
    n_in, n_out, n_scr = len(args), len(out_shape), len(scratch_shapes)
    j_args, j_out, j_scr = ([], [], []) if job is None else ([_in_hbm(a) for a in job.args], job.out_shape, job.scratch)

    def run(*refs):
        groups, at = [], 0
        for n in (n_in, len(j_args), n_out, len(j_out), n_scr, len(j_scr)):
            groups.append(refs[at:at + n])
            at += n
        ins, j_ins, outs, j_outs, scr, j_sems = groups
        if job is None:
            body(*ins, *outs, *scr)
        elif not grid:
            job.start(j_ins, j_outs, j_sems)
            body(*ins, *outs, *scr)
            job.finish(j_ins, j_outs, j_sems)
        else:
            ids = [pl.program_id(a) for a in range(len(grid))]
            first = functools.reduce(jnp.logical_and, [i == 0 for i in ids])
            last = functools.reduce(jnp.logical_and, [i == g - 1 for i, g in zip(ids, grid)])
            pl.when(first)(lambda: job.start(j_ins, j_outs, j_sems))
            body(*ins, *outs, *scr)
            pl.when(last)(lambda: job.finish(j_ins, j_outs, j_sems))

    res = pl.pallas_call(
        run, name=name, grid=grid,
        in_specs=list(in_specs) + [ANY] * len(j_args), out_specs=list(out_specs) + [ANY] * len(j_out),
        out_shape=list(out_shape) + list(j_out), scratch_shapes=list(scratch_shapes) + list(j_scr),
        compiler_params=_cparams(len(grid), vmem),
    )(*args, *j_args)
    return res[:n_out], res[n_out:]


def _fwd_pool_mixer(x, g_pre, wp, scale, g_post, g_ffn, job=None):
    T = x.shape[0]
    tm = ROW_TILE
    nt = T // tm

    def body(x_ref, gpre_ref, wp_ref, sc_ref, gpost_ref, gffn_ref, x1_ref, h2_ref, yraw_ref, d_ref, carry):
        i = pl.program_id(0)

        @pl.when(i == 0)
        def _():
            carry[...] = jnp.zeros_like(carry)

        xv = x_ref[...]
        h = _rms(xv, gpre_ref[...])
        ext = jnp.concatenate([carry[...], h], axis=0)
        carry[...] = h[tm - POOL_HALO:, :]
        sums = _window_sums(ext, lambda k: k)[POOL_HALO:, :]
        d = sums / _pool_counts(i * tm, tm) - h
        db = d.astype(BF16)
        d_ref[...] = db
        yraw = jnp.concatenate(
            [_dot(db[:, g * POOL_GROUP:(g + 1) * POOL_GROUP], wp_ref[g]) for g in range(N_POOL_GROUPS)], axis=1)
        yraw_ref[...] = yraw
        x1 = xv + _rms(yraw * sc_ref[...], gpost_ref[...])
        x1_ref[...] = x1
        h2_ref[...] = _rms(x1, gffn_ref[...]).astype(BF16)

    return _launch(
        body, name="fwd_pool_mixer", grid=(nt,),
        in_specs=[_row_spec(D_MODEL), _vec_spec(), _full_spec((N_POOL_GROUPS, POOL_GROUP, POOL_GROUP)), _vec_spec(),
                  _vec_spec(), _vec_spec()],
        out_specs=[_row_spec(D_MODEL)] * 4,
        out_shape=[jax.ShapeDtypeStruct((T, D_MODEL), F32), jax.ShapeDtypeStruct((T, D_MODEL), BF16),
                   jax.ShapeDtypeStruct((T, D_MODEL), F32), jax.ShapeDtypeStruct((T, D_MODEL), BF16)],
        scratch_shapes=[pltpu.VMEM((POOL_HALO, D_MODEL), F32)],
        args=(x, g_pre, wp, scale, g_post, g_ffn), job=job)


def _fwd_ffn(layer, h2, x1, wgu, wd, g_post, g_ple, job=None):
    T = h2.shape[0]
    tm = min(FFN_ROW_TILE, T)
    nt = T // tm
    sub = tm // FFN_SUB_TILES
    last = FF_CHUNKS - 1

    def body(h2_ref, x1_ref, wgu_ref, wd_ref, gpost_ref, gple_ref, gs_ref, us_ref, f_ref, x2_ref, h3_ref, acc):
        k = pl.program_id(0)
        i = pl.program_id(1)
        rows = pl.ds(pl.multiple_of(i * tm, tm), tm)
        parts = []
        for s in range(FFN_SUB_TILES):
            r = pl.ds(s * sub, sub)
            h = h2_ref[r, :]
            g = _dot_nt(h, wgu_ref[0])
            u = _dot_nt(h, wgu_ref[1])
            gs_ref[r, :] = g.astype(BF16)
            us_ref[r, :] = u.astype(BF16)
            a = (g * _sigmoid(g) * u).astype(BF16)
            parts.append(_dot(a, wd_ref[...]))
        part = jnp.concatenate(parts, axis=0)

        @pl.when(k == 0)
        def _():
            acc[rows, :] = part

        @pl.when(jnp.logical_and(k > 0, k < last))
        def _():
            acc[rows, :] += part

        @pl.when(k == last)
        def _():
            f = acc[rows, :] + part
            f_ref[...] = f
            x2 = x1_ref[...] + _rms(f, gpost_ref[...])
            x2_ref[...] = x2
            h3_ref[...] = _rms(x2, gple_ref[...]).astype(BF16)

    def late(k, i):
        return (jnp.where(k == last, i, 0), 0)

    return _launch(
        body, name=f"fwd_ffn{layer}", grid=(FF_CHUNKS, nt),
        in_specs=[pl.BlockSpec((tm, D_MODEL), lambda k, i: (i, 0)),
                  pl.BlockSpec((tm, D_MODEL), late),
                  pl.BlockSpec((None, 2, FF_BLOCK, D_MODEL), lambda k, i: (k, 0, 0, 0)),
                  pl.BlockSpec((FF_BLOCK, D_MODEL), lambda k, i: (k, 0)),
                  pl.BlockSpec((1, D_MODEL), lambda k, i: (0, 0)),
                  pl.BlockSpec((1, D_MODEL), lambda k, i: (0, 0))],
        out_specs=[pl.BlockSpec((None, tm, FF_BLOCK), lambda k, i: (k, i, 0)),
                   pl.BlockSpec((None, tm, FF_BLOCK), lambda k, i: (k, i, 0)),
                   pl.BlockSpec((tm, D_MODEL), late),
                   pl.BlockSpec((tm, D_MODEL), late),
                   pl.BlockSpec((tm, D_MODEL), late)],
        out_shape=[jax.ShapeDtypeStruct((FF_CHUNKS, T, FF_BLOCK), BF16),
                   jax.ShapeDtypeStruct((FF_CHUNKS, T, FF_BLOCK), BF16),
                   jax.ShapeDtypeStruct((T, D_MODEL), F32),
                   jax.ShapeDtypeStruct((T, D_MODEL), F32),
                   jax.ShapeDtypeStruct((T, D_MODEL), BF16)],
        scratch_shapes=[pltpu.VMEM((T, D_MODEL), F32)],
        args=(h2, x1, wgu, wd, g_post, g_ple), vmem=VMEM_BIG, job=job)


def _fwd_ple(layer, x2, h3, p, wgate, wproj, g_post, target=None, job=None):
    T = x2.shape[0]
    tm = ROW_TILE
    nt = T // tm
    with_loss = target is not None

    def body(*refs):
        if with_loss:
            x2_ref, h3_ref, p_ref, wg_ref, wp_ref, gpost_ref, tgt_ref, out_ref, z_ref, pe_ref, loss_ref = refs
        else:
            x2_ref, h3_ref, p_ref, wg_ref, wp_ref, gpost_ref, out_ref, z_ref, pe_ref = refs
        z = _dot(h3_ref[...], wg_ref[...])
        pe = _dot(p_ref[...].astype(BF16), wp_ref[...])
        z_ref[...] = z
        pe_ref[...] = pe
        x3 = x2_ref[...] + _rms(pe * _sigmoid(z), gpost_ref[...])
        if with_loss:
            err = x3 - tgt_ref[...]
            out_ref[...] = err * (1.0 / D_MODEL)
            part = 0.5 * jnp.sum(jnp.mean(err * err, axis=-1, keepdims=True), axis=0, keepdims=True)
            _acc(loss_ref, part, pl.program_id(0) == 0)
        else:
            out_ref[...] = x3

    in_specs = [_row_spec(D_MODEL), _row_spec(D_MODEL), _row_spec(PLE_DIM), _full_spec((D_MODEL, D_MODEL)),
                _full_spec((PLE_DIM, D_MODEL)), _vec_spec()]
    out_specs = [_row_spec(D_MODEL)] * 3
    out_shape = [jax.ShapeDtypeStruct((T, D_MODEL), F32)] * 3
    args = [x2, h3, p, wgate, wproj, g_post]
    if with_loss:
        in_specs.append(_row_spec(D_MODEL))
        out_specs.append(_full_spec((1, 1)))
        out_shape.append(jax.ShapeDtypeStruct((1, 1), F32))
        args.append(target)
    return _launch(body, name=f"fwd_ple{layer}", grid=(nt,), in_specs=in_specs, out_specs=out_specs,
                   out_shape=out_shape, args=args, job=job)


def _fwd_qkv(x3, g_kv, g_mix, wkv, wq, job=None):
    T = x3.shape[0]
    nt = T // ROW_TILE

    def body(x_ref, gkv_ref, gmix_ref, wkv_ref, wq_ref, hk_ref, h1_ref, q_ref, kv_ref):
        xv = x_ref[...]
        r = _rstd(xv)
        hk = (xv * r * gkv_ref[...]).astype(BF16)
        h1 = (xv * r * gmix_ref[...]).astype(BF16)
        hk_ref[...] = hk
        h1_ref[...] = h1
        kv_ref[...] = _dot(hk, wkv_ref[...]).astype(BF16)
        q_ref[...] = _dot(h1, wq_ref[...]).astype(BF16)

    return _launch(
        body, name="fwd_qkv", grid=(nt,),
        in_specs=[_row_spec(D_MODEL), _vec_spec(), _vec_spec(), _full_spec((D_MODEL, 2 * KV_DIM)),
                  _full_spec((D_MODEL, D_MODEL))],
        out_specs=[_row_spec(D_MODEL), _row_spec(D_MODEL), _row_spec(D_MODEL), _row_spec(2 * KV_DIM)],
        out_shape=[jax.ShapeDtypeStruct((T, D_MODEL), BF16)] * 3 + [jax.ShapeDtypeStruct((T, 2 * KV_DIM), BF16)],
        args=(x3, g_kv, g_mix, wkv, wq), job=job)


def _alibi_slope(h):
    return 2.0 ** (-8.0 * (h + 1) / N_HEADS)


def _att_mask(n):
    qi = lax.broadcasted_iota(jnp.int32, (ATT_BLOCK, 2 * ATT_BLOCK), 0)
    si = lax.broadcasted_iota(jnp.int32, (ATT_BLOCK, 2 * ATT_BLOCK), 1)
    rel = ATT_BLOCK + qi - si
    valid = (rel >= 0) & (rel < ATT_BLOCK) & ((si >= ATT_BLOCK) | (n > 0))
    return rel.astype(F32), valid


def _att_probs(qh, kk, relf, valid, slope, sink):
    s = _dot_nt(qh, kk) * ATT_SCALE
    s = jnp.where(valid, s - slope * relf, NEG_INF)
    m = jnp.maximum(jnp.max(s, axis=-1, keepdims=True), sink)
    e = jnp.exp(s - m)
    es = jnp.exp(sink - m)
    inv = 1.0 / (jnp.sum(e, axis=-1, keepdims=True) + es)
    return e * inv, es * inv


def _fwd_attention(q, kpad, vpad, sinks, job=None):
    T = q.shape[0]
    nb = T // ATT_BLOCK

    def body(q_ref, k_ref, v_ref, sink_ref, o_ref):
        n = pl.program_id(0)
        start = pl.multiple_of(n * ATT_BLOCK, ATT_BLOCK)
        kw = k_ref[pl.ds(start, 2 * ATT_BLOCK), :]
        vw = v_ref[pl.ds(start, 2 * ATT_BLOCK), :]
        relf, valid = _att_mask(n)
        outs = []
        for h in range(N_HEADS):
            kh = h // GQA_GROUP
            qh = q_ref[:, h * HEAD_DIM:(h + 1) * HEAD_DIM]
            kk = kw[:, kh * HEAD_DIM:(kh + 1) * HEAD_DIM]
            vv = vw[:, kh * HEAD_DIM:(kh + 1) * HEAD_DIM]
            pr, _ = _att_probs(qh, kk, relf, valid, _alibi_slope(h), sink_ref[0, h])
            outs.append(_dot(pr.astype(BF16), vv))
        o_ref[...] = jnp.concatenate(outs, axis=1).astype(BF16)

    return _launch(
        body, name="fwd_attention", grid=(nb,),
        in_specs=[_row_spec(D_MODEL, ATT_BLOCK), _full_spec((T + ATT_BLOCK, KV_DIM)), _full_spec((T + ATT_BLOCK, KV_DIM)),
                  pl.BlockSpec(memory_space=pltpu.SMEM)],
        out_specs=[_row_spec(D_MODEL, ATT_BLOCK)],
        out_shape=[jax.ShapeDtypeStruct((T, D_MODEL), BF16)],
        args=(q, kpad, vpad, sinks), job=job)


def _fwd_attn_out(attn, x, wo, g_post, g_ffn, job=None):
    T = x.shape[0]
    nt = T // ROW_TILE

    def body(a_ref, x_ref, wo_ref, gpost_ref, gffn_ref, y_ref, x1_ref, h2_ref):
        y = _dot(a_ref[...], wo_ref[...])
        y_ref[...] = y
        x1 = x_ref[...] + _rms(y, gpost_ref[...])
        x1_ref[...] = x1
        h2_ref[...] = _rms(x1, gffn_ref[...]).astype(BF16)

    return _launch(
        body, name="fwd_attn_out", grid=(nt,),
        in_specs=[_row_spec(D_MODEL), _row_spec(D_MODEL), _full_spec((D_MODEL, D_MODEL)), _vec_spec(), _vec_spec()],
        out_specs=[_row_spec(D_MODEL)] * 3,
        out_shape=[jax.ShapeDtypeStruct((T, D_MODEL), F32), jax.ShapeDtypeStruct((T, D_MODEL), F32),
                   jax.ShapeDtypeStruct((T, D_MODEL), BF16)],
        args=(attn, x, wo, g_post, g_ffn), job=job)


def _bwd_ple(layer, dx3, x2, z, pe, h3, p, f, wgate, g_ple_post, g_ple, g_post_ffn, job=None):
    T = x2.shape[0]
    tm = ROW_TILE
    nt = T // tm

    def body(dx3_ref, x2_ref, z_ref, pe_ref, h3_ref, p_ref, f_ref, wg_ref, gpp_ref, gp_ref, gpf_ref,
             dx2_ref, df_ref, dwg_ref, dwp_ref, dgpp_ref, dgp_ref, dgpf_ref, acc_g, acc_p):
        i = pl.program_id(0)
        first = i == 0
        dx3v = dx3_ref[...]
        gate = _sigmoid(z_ref[...])
        pev = pe_ref[...]
        de, dgpp = _rms_bwd(pev * gate, gpp_ref[...], dx3v)
        dpe = (de * gate).astype(BF16)
        dz = (de * pev * gate * (1.0 - gate)).astype(BF16)
        _acc(acc_p, _dot_tn(p_ref[...].astype(BF16), dpe), first)
        _acc(acc_g, _dot_tn(h3_ref[...], dz), first)
        dh3 = _dot_nt(dz, wg_ref[...])
        dxn, dgp = _rms_bwd(x2_ref[...], gp_ref[...], dh3)
        dx2 = dx3v + dxn
        dx2_ref[...] = dx2
        df, dgpf = _rms_bwd(f_ref[...], gpf_ref[...], dx2)
        df_ref[...] = df.astype(BF16)
        _acc(dgpp_ref, dgpp, first)
        _acc(dgp_ref, dgp, first)
        _acc(dgpf_ref, dgpf, first)

        @pl.when(i == nt - 1)
        def _():
            dwg_ref[...] = acc_g[...].astype(BF16)
            dwp_ref[...] = acc_p[...].astype(BF16)

    return _launch(
        body, name=f"bwd_ple{layer}", grid=(nt,),
        in_specs=[_row_spec(D_MODEL)] * 5 + [_row_spec(PLE_DIM), _row_spec(D_MODEL), _full_spec((D_MODEL, D_MODEL)),
                  _vec_spec(), _vec_spec(), _vec_spec()],
        out_specs=[_row_spec(D_MODEL), _row_spec(D_MODEL), _full_spec((D_MODEL, D_MODEL)), _full_spec((PLE_DIM, D_MODEL)),
                   _vec_spec(), _vec_spec(), _vec_spec()],
        out_shape=[jax.ShapeDtypeStruct((T, D_MODEL), F32), jax.ShapeDtypeStruct((T, D_MODEL), BF16),
                   jax.ShapeDtypeStruct((D_MODEL, D_MODEL), BF16), jax.ShapeDtypeStruct((PLE_DIM, D_MODEL), BF16)]
                  + [jax.ShapeDtypeStruct((1, D_MODEL), F32)] * 3,
        scratch_shapes=[pltpu.VMEM((D_MODEL, D_MODEL), F32), pltpu.VMEM((PLE_DIM, D_MODEL), F32)],
        args=(dx3, x2, z, pe, h3, p, f, wgate, g_ple_post, g_ple, g_post_ffn), job=job)


def _bwd_ffn_act(layer, df, gs, us, wgu, wd, job=None):
    T = df.shape[0]
    tm = min(FFN_ROW_TILE, T)
    nt = T // tm
    sub = tm // FFN_SUB_TILES
    last = FF_CHUNKS - 1

    def body(df_ref, gs_ref, us_ref, wgu_ref, wd_ref, dh_ref, dg_ref, du_ref, a_ref, acc_h):
        k = pl.program_id(0)
        i = pl.program_id(1)
        rows = pl.ds(pl.multiple_of(i * tm, tm), tm)
        dhs = []
        for s in range(FFN_SUB_TILES):
            r = pl.ds(s * sub, sub)
            g = gs_ref[r, :].astype(F32)
            u = us_ref[r, :].astype(F32)
            sg = _sigmoid(g)
            silu = g * sg
            a_ref[r, :] = (silu * u).astype(BF16)
            da = _dot_nt(df_ref[r, :], wd_ref[...])
            dg = (da * u * (sg * (1.0 + g * (1.0 - sg)))).astype(BF16)
            du = (da * silu).astype(BF16)
            dg_ref[r, :] = dg
            du_ref[r, :] = du
            dhs.append(_dot(dg, wgu_ref[0]) + _dot(du, wgu_ref[1]))
        dh = jnp.concatenate(dhs, axis=0)

        @pl.when(k == 0)
        def _():
            acc_h[rows, :] = dh

        @pl.when(jnp.logical_and(k > 0, k < last))
        def _():
            acc_h[rows, :] += dh

        @pl.when(k == last)
        def _():
            dh_ref[...] = acc_h[rows, :] + dh

    chunk_rows = pl.BlockSpec((None, tm, FF_BLOCK), lambda k, i: (k, i, 0))
    saved = jax.ShapeDtypeStruct((FF_CHUNKS, T, FF_BLOCK), BF16)
    return _launch(
        body, name=f"bwd_ffn_act{layer}", grid=(FF_CHUNKS, nt),
        in_specs=[pl.BlockSpec((tm, D_MODEL), lambda k, i: (i, 0)), chunk_rows, chunk_rows,
                  pl.BlockSpec((None, 2, FF_BLOCK, D_MODEL), lambda k, i: (k, 0, 0, 0)),
                  pl.BlockSpec((FF_BLOCK, D_MODEL), lambda k, i: (k, 0))],
        out_specs=[pl.BlockSpec((tm, D_MODEL), lambda k, i: (jnp.where(k == last, i, 0), 0)),
                   chunk_rows, chunk_rows, chunk_rows],
        out_shape=[jax.ShapeDtypeStruct((T, D_MODEL), F32), saved, saved, saved],
        scratch_shapes=[pltpu.VMEM((T, D_MODEL), F32)],
        args=(df, gs, us, wgu, wd), vmem=VMEM_BIG, job=job)


def _bwd_ffn_dw(layer, q, h2, df, dg, du, a, job=None):
    T = h2.shape[0]

    def body(h_ref, df_ref, dg_ref, du_ref, a_ref, dgu_ref, dwd_ref):
        h = h_ref[...]
        dgu_ref[0] = _dot_tn(dg_ref[...], h).astype(BF16)
        dgu_ref[1] = _dot_tn(du_ref[...], h).astype(BF16)
        dwd_ref[...] = _dot_tn(a_ref[...], df_ref[...]).astype(BF16)

    cols = pl.BlockSpec((T, FF_PART), lambda k: (0, q))
    chunk = pl.BlockSpec((None, T, FF_BLOCK), lambda k: (k, 0, 0))
    return _launch(
        body, name=f"bwd_ffn_dw{layer}_{q}", grid=(FF_CHUNKS,),
        in_specs=[cols, cols, chunk, chunk, chunk],
        out_specs=[pl.BlockSpec((None, 2, FF_BLOCK, FF_PART), lambda k: (k, 0, 0, 0)),
                   pl.BlockSpec((FF_BLOCK, FF_PART), lambda k: (k, 0))],
        out_shape=[jax.ShapeDtypeStruct((FF_CHUNKS, 2, FF_BLOCK, FF_PART), BF16),
                   jax.ShapeDtypeStruct((D_FF, FF_PART), BF16)],
        args=(h2, df, dg, du, a), vmem=VMEM_BIG, job=job)


def _bwd_attn_out(dx2, dh2, x1, y, attn, wo, g_ffn, g_post, job=None):
    T = x1.shape[0]
    nt = T // ROW_TILE

    def body(dx2_ref, dh2_ref, x1_ref, y_ref, a_ref, wo_ref, gffn_ref, gpost_ref,
             dx1_ref, da_ref, dwo_ref, dgf_ref, dgp_ref, acc):
        i = pl.program_id(0)
        first = i == 0
        dxn, dgf = _rms_bwd(x1_ref[...], gffn_ref[...], dh2_ref[...])
        dx1 = dx2_ref[...] + dxn
        dx1_ref[...] = dx1
        dy, dgp = _rms_bwd(y_ref[...], gpost_ref[...], dx1)
        dyb = dy.astype(BF16)
        da_ref[...] = _dot_nt(dyb, wo_ref[...]).astype(BF16)
        _acc(acc, _dot_tn(a_ref[...], dyb), first)
        _acc(dgf_ref, dgf, first)
        _acc(dgp_ref, dgp, first)

        @pl.when(i == nt - 1)
        def _():
            dwo_ref[...] = acc[...].astype(BF16)

    return _launch(
        body, name="bwd_attn_out", grid=(nt,),
        in_specs=[_row_spec(D_MODEL)] * 5 + [_full_spec((D_MODEL, D_MODEL)), _vec_spec(), _vec_spec()],
        out_specs=[_row_spec(D_MODEL), _row_spec(D_MODEL), _full_spec((D_MODEL, D_MODEL)), _vec_spec(), _vec_spec()],
        out_shape=[jax.ShapeDtypeStruct((T, D_MODEL), F32), jax.ShapeDtypeStruct((T, D_MODEL), BF16),
                   jax.ShapeDtypeStruct((D_MODEL, D_MODEL), BF16)] + [jax.ShapeDtypeStruct((1, D_MODEL), F32)] * 2,
        scratch_shapes=[pltpu.VMEM((D_MODEL, D_MODEL), F32)],
        args=(dx2, dh2, x1, y, attn, wo, g_ffn, g_post), job=job)


def _bwd_attention(q, dattn, kpad, vpad, sinks, job=None):
    T = q.shape[0]
    nb = T // ATT_BLOCK

    def body(q_ref, do_ref, k_ref, v_ref, sink_ref, dq_ref, dk_ref, dv_ref, ds_ref):
        n = pl.program_id(0)

        @pl.when(n == 0)
        def _():
            dk_ref[...] = jnp.zeros_like(dk_ref)
            dv_ref[...] = jnp.zeros_like(dv_ref)
            ds_ref[...] = jnp.zeros_like(ds_ref)

        start = pl.multiple_of(n * ATT_BLOCK, ATT_BLOCK)
        win = pl.ds(start, 2 * ATT_BLOCK)
        kw = k_ref[win, :]
        vw = v_ref[win, :]
        relf, valid = _att_mask(n)
        lane = lax.broadcasted_iota(jnp.int32, (1, ATT_BLOCK), 1)
        dsink = jnp.zeros((1, ATT_BLOCK), F32)
        dqs, dks, dvs = [], [], []
        for kh in range(N_KV_HEADS):
            kk = kw[:, kh * HEAD_DIM:(kh + 1) * HEAD_DIM]
            vv = vw[:, kh * HEAD_DIM:(kh + 1) * HEAD_DIM]
            dk_h = jnp.zeros((2 * ATT_BLOCK, HEAD_DIM), F32)
            dv_h = jnp.zeros((2 * ATT_BLOCK, HEAD_DIM), F32)
            for gq in range(GQA_GROUP):
                h = kh * GQA_GROUP + gq
                qh = q_ref[:, h * HEAD_DIM:(h + 1) * HEAD_DIM]
                do = do_ref[:, h * HEAD_DIM:(h + 1) * HEAD_DIM]
                pr, ps = _att_probs(qh, kk, relf, valid, _alibi_slope(h), sink_ref[0, h])
                dp = _dot_nt(do, vv)
                delta = jnp.sum(pr * dp, axis=-1, keepdims=True)
                dsb = (pr * (dp - delta) * ATT_SCALE).astype(BF16)
                dsink = dsink + jnp.where(lane == h, -jnp.sum(ps * delta, axis=0, keepdims=True), 0.0)
                dqs.append(_dot(dsb, kk))
                dk_h = dk_h + _dot_tn(dsb, qh)
                dv_h = dv_h + _dot_tn(pr.astype(BF16), do)
            dks.append(dk_h)
            dvs.append(dv_h)
        dq_ref[...] = jnp.concatenate(dqs, axis=1).astype(BF16)
        dk_ref[win, :] += jnp.concatenate(dks, axis=1)
        dv_ref[win, :] += jnp.concatenate(dvs, axis=1)
        ds_ref[...] += dsink

    return _launch(
        body, name="bwd_attention", grid=(nb,),
        in_specs=[_row_spec(D_MODEL, ATT_BLOCK), _row_spec(D_MODEL, ATT_BLOCK), _full_spec((T + ATT_BLOCK, KV_DIM)),
                  _full_spec((T + ATT_BLOCK, KV_DIM)), pl.BlockSpec(memory_space=pltpu.SMEM)],
        out_specs=[_row_spec(D_MODEL, ATT_BLOCK), _full_spec((T + ATT_BLOCK, KV_DIM)), _full_spec((T + ATT_BLOCK, KV_DIM)),
                   _full_spec((1, ATT_BLOCK))],
        out_shape=[jax.ShapeDtypeStruct((T, D_MODEL), BF16), jax.ShapeDtypeStruct((T + ATT_BLOCK, KV_DIM), F32),
                   jax.ShapeDtypeStruct((T + ATT_BLOCK, KV_DIM), F32), jax.ShapeDtypeStruct((1, ATT_BLOCK), F32)],
        args=(q, dattn, kpad, vpad, sinks), vmem=VMEM_BIG, job=job)


def _bwd_qkv(dxres, dq, dkv, x3, h1, hk, wq, wkv, g_mix, g_kv, job=None):
    T = x3.shape[0]
    nt = T // ROW_TILE

    def body(dxr_ref, dq_ref, dkv_ref, x_ref, h1_ref, hk_ref, wq_ref, wkv_ref, gmix_ref, gkv_ref,
             dx_ref, dwq_ref, dwkv_ref, dgm_ref, dgk_ref, acc_q, acc_kv):
        i = pl.program_id(0)
        first = i == 0
        dqv = dq_ref[...]
        dkvv = dkv_ref[...]
        xv = x_ref[...]
        d1, dgm = _rms_bwd(xv, gmix_ref[...], _dot_nt(dqv, wq_ref[...]))
        d2, dgk = _rms_bwd(xv, gkv_ref[...], _dot_nt(dkvv, wkv_ref[...]))
        dx_ref[...] = dxr_ref[...] + d1 + d2
        _acc(acc_q, _dot_tn(h1_ref[...], dqv), first)
        _acc(acc_kv, _dot_tn(hk_ref[...], dkvv), first)
        _acc(dgm_ref, dgm, first)
        _acc(dgk_ref, dgk, first)

        @pl.when(i == nt - 1)
        def _():
            dwq_ref[...] = acc_q[...].astype(BF16)
            dwkv_ref[...] = acc_kv[...].astype(BF16)

    return _launch(
        body, name="bwd_qkv", grid=(nt,),
        in_specs=[_row_spec(D_MODEL), _row_spec(D_MODEL), _row_spec(2 * KV_DIM), _row_spec(D_MODEL), _row_spec(D_MODEL),
                  _row_spec(D_MODEL), _full_spec((D_MODEL, D_MODEL)), _full_spec((D_MODEL, 2 * KV_DIM)), _vec_spec(),
                  _vec_spec()],
        out_specs=[_row_spec(D_MODEL), _full_spec((D_MODEL, D_MODEL)), _full_spec((D_MODEL, 2 * KV_DIM)), _vec_spec(),
                   _vec_spec()],
        out_shape=[jax.ShapeDtypeStruct((T, D_MODEL), F32), jax.ShapeDtypeStruct((D_MODEL, D_MODEL), BF16),
                   jax.ShapeDtypeStruct((D_MODEL, 2 * KV_DIM), BF16)] + [jax.ShapeDtypeStruct((1, D_MODEL), F32)] * 2,
        scratch_shapes=[pltpu.VMEM((D_MODEL, D_MODEL), F32), pltpu.VMEM((D_MODEL, 2 * KV_DIM), F32)],
        args=(dxres, dq, dkv, x3, h1, hk, wq, wkv, g_mix, g_kv), job=job)


def _bwd_pool_mixer(dx2, dh2, x1, x, yraw, d, wp, scale, g_ffn, g_post, g_pre, job=None):
    T = x.shape[0]
    tm = ROW_TILE
    nt = T // tm

    def body(dx2_ref, dh2_ref, x1_ref, x_ref, yraw_ref, d_ref, wp_ref, sc_ref, gffn_ref, gpost_ref, gpre_ref,
             dx_ref, dwp_ref, dsc_ref, dgf_ref, dgp_ref, dgm_ref, carry, acc):
        i = pl.program_id(0)
        first = i == 0
        tile = nt - 1 - i

        @pl.when(first)
        def _():
            carry[...] = jnp.zeros_like(carry)

        dxn, dgf = _rms_bwd(x1_ref[...], gffn_ref[...], dh2_ref[...])
        dx1 = dx2_ref[...] + dxn
        yraw = yraw_ref[...]
        sc = sc_ref[...]
        dy, dgp = _rms_bwd(yraw * sc, gpost_ref[...], dx1)
        dsc = jnp.sum(dy * yraw, axis=0, keepdims=True)
        dyb = (dy * sc).astype(BF16)
        dv = d_ref[...]
        dds = []
        for g in range(N_POOL_GROUPS):
            cols = slice(g * POOL_GROUP, (g + 1) * POOL_GROUP)
            dds.append(_dot_nt(dyb[:, cols], wp_ref[g]))
            _acc(acc.at[g], _dot_tn(dv[:, cols], dyb[:, cols]), first)
        dd = jnp.concatenate(dds, axis=1)
        e = dd / _pool_counts(tile * tm, tm)
        ext = jnp.concatenate([e, carry[...]], axis=0)
        carry[...] = e[:POOL_HALO, :]
        sums = _window_sums(ext, lambda k: tm + POOL_HALO - k)[:tm, :]
        dxm, dgm = _rms_bwd(x_ref[...], gpre_ref[...], sums - dd)
        dx_ref[...] = dx1 + dxm
        _acc(dsc_ref, dsc, first)
        _acc(dgf_ref, dgf, first)
        _acc(dgp_ref, dgp, first)
        _acc(dgm_ref, dgm, first)

        @pl.when(i == nt - 1)
        def _():
            dwp_ref[...] = acc[...].astype(BF16)

    rev = pl.BlockSpec((tm, D_MODEL), lambda i: (nt - 1 - i, 0))
    return _launch(
        body, name="bwd_pool_mixer", grid=(nt,),
        in_specs=[rev] * 6 + [_full_spec((N_POOL_GROUPS, POOL_GROUP, POOL_GROUP))] + [_vec_spec()] * 4,
        out_specs=[rev, _full_spec((N_POOL_GROUPS, POOL_GROUP, POOL_GROUP))] + [_vec_spec()] * 4,
        out_shape=[jax.ShapeDtypeStruct((T, D_MODEL), F32),
                   jax.ShapeDtypeStruct((N_POOL_GROUPS, POOL_GROUP, POOL_GROUP), BF16)]
                  + [jax.ShapeDtypeStruct((1, D_MODEL), F32)] * 4,
        scratch_shapes=[pltpu.VMEM((POOL_HALO, D_MODEL), F32), pltpu.VMEM((N_POOL_GROUPS, POOL_GROUP, POOL_GROUP), F32)],
        args=(dx2, dh2, x1, x, yraw, d, wp, scale, g_ffn, g_post, g_pre), job=job)


def _my_place():
    return lax.axis_index("x"), lax.axis_index("y"), lax.axis_index("c")


def _dev_index(px, py, pc):
    return 4 * px + 2 * py + pc


def _peer_by_relation(r):
    x, y, c = _my_place()
    return (x ^ ((r >> 2) & 1), y ^ ((r >> 1) & 1), c ^ (r & 1))


def _slot_pool(ref, j):
    return ref.at[:, pl.ds(pl.multiple_of(j * 32, 32), 32), :]


def _slot_scale(ref, j):
    return ref.at[:, pl.ds(pl.multiple_of(j * 128, 128), 128)]


def _slot_rows128(ref, j):
    return ref.at[pl.ds(pl.multiple_of(j * 128, 128), 128), :]


def _slot_gu(ref, j):
    return ref.at[j % FF_CHUNKS, j // FF_CHUNKS]


def _slot_wd(ref, j):
    return ref.at[pl.ds(pl.multiple_of(j * WD_ROWS, 16), WD_ROWS), :]


def _slot_cols128(ref, j):
    return ref.at[:, pl.ds(pl.multiple_of(j * 128, 128), 128)]


_GATHERED = {
    "pool": ((N_POOL_GROUPS, POOL_GROUP, POOL_GROUP), BF16, _slot_pool),
    "scale": ((1, D_MODEL), F32, _slot_scale),
    "kv": ((D_MODEL, 2 * KV_DIM), BF16, _slot_rows128),
    "q": ((D_MODEL, D_MODEL), BF16, _slot_rows128),
    "o": ((D_MODEL, D_MODEL), BF16, _slot_rows128),
    "gu": ((FF_CHUNKS, 2, FF_BLOCK, D_MODEL), BF16, _slot_gu),
    "wd": ((D_FF, D_MODEL), BF16, _slot_wd),
    "gate": ((D_MODEL, D_MODEL), BF16, _slot_rows128),
    "proj": ((PLE_DIM, D_MODEL), BF16, _slot_cols128),
}


def _no_compute():
    pass


class _AllGather:
    def __init__(self, names, shards):
        self.kinds = [_GATHERED[n.rstrip("01")] for n in names]
        self.args = [shards[n] for n in names]
        self.out_shape = [jax.ShapeDtypeStruct(shape, dtype) for shape, dtype, _ in self.kinds]
        n = len(names)
        self.scratch = [pltpu.SemaphoreType.DMA((n, 7)), pltpu.SemaphoreType.DMA((n, 7)), pltpu.SemaphoreType.DMA((n,))]

    def _copies(self, srcs, outs, sems):
        send_sems, recv_sems, local_sems = sems
        x, y, c = _my_place()
        me, sibling = (x, y, c), (x, y, 1 - c)
        chips = [(1 - x, y), (x, 1 - y), (1 - x, 1 - y)]
        n = len(srcs)

        def slot(t, dev):
            return self.kinds[t][2](outs[t], _dev_index(*dev))

        def copy(t, k, block, to, src=None):
            return pltpu.make_async_remote_copy(
                src_ref=slot(t, block) if src is None else src, dst_ref=slot(t, block),
                send_sem=send_sems.at[t, k], recv_sem=recv_sems.at[t, k], device_id=to, device_id_type=MESH)

        mine = [pltpu.make_async_copy(srcs[t], slot(t, me), local_sems.at[t]) for t in range(n)]
        first = []
        for t in range(n):
            first.append(copy(t, 0, me, sibling, src=srcs[t]))
            first += [copy(t, 1 + j, me, (*chip, c), src=srcs[t]) for j, chip in enumerate(chips)]
        return me, sibling, chips, copy, mine, first

    def start(self, srcs, outs, sems):
        _, _, _, _, mine, first = self._copies(srcs, outs, sems)
        for cp in mine + first:
            cp.start()

    def finish(self, srcs, outs, sems):
        me, sibling, chips, copy, mine, first = self._copies(srcs, outs, sems)
        c = me[2]
        n = len(srcs)
        passed = []
        for j, chip in enumerate(chips):
            for t in range(n):
                copy(t, 1 + j, (*chip, c), me).wait_recv()
                fwd = copy(t, 4 + j, (*chip, c), sibling)
                fwd.start()
                passed.append(fwd)
        for t in range(n):
            copy(t, 0, sibling, me).wait_recv()
            for j, chip in enumerate(chips):
                copy(t, 4 + j, (*chip, 1 - c), me).wait_recv()
        for cp in first + passed:
            cp.wait_send()
        for cp in mine:
            cp.wait()


def _all_gather_only(name, names, shards):
    return _launch(_no_compute, name=name, grid=(), in_specs=[], out_specs=[], out_shape=[], args=(),
                   job=_AllGather(names, shards))[1]


def _block_pool(ref, j):
    return ref.at[:, pl.ds(pl.multiple_of(j * 32, 32), 32), :]


def _block_rows128(ref, j):
    return ref.at[pl.ds(pl.multiple_of(j * 128, 128), 128), :]


def _block_gu(ref, j):
    return ref.at[j % FF_CHUNKS, j // FF_CHUNKS]


def _block_wd(ref, j):
    return ref.at[pl.ds(pl.multiple_of(j * WD_ROWS, 16), WD_ROWS), :]


def _block_cols128(ref, j):
    return ref.at[:, pl.ds(pl.multiple_of(j * 128, 128), 128)]


_SCATTERED = {
    "pool": ((N_POOL_GROUPS, 32, POOL_GROUP), _block_pool),
    "kv": ((128, 2 * KV_DIM), _block_rows128),
    "q": ((128, D_MODEL), _block_rows128),
    "o": ((128, D_MODEL), _block_rows128),
    "gu": ((FF_BLOCK, FF_PART), _block_gu),
    "wd": ((WD_ROWS, FF_PART), _block_wd),
    "gate": ((128, D_MODEL), _block_rows128),
    "proj": ((PLE_DIM, 128), _block_cols128),
}


class _SiblingSwap:
    def __init__(self, pieces):
        self.kinds = [_SCATTERED[kind] for kind, _ in pieces]
        self.args = [g for _, g in pieces]
        self.out_shape = [jax.ShapeDtypeStruct((N_CHIPS, *block), BF16) for block, _ in self.kinds]
        n = len(pieces)
        self.scratch = [pltpu.SemaphoreType.DMA((n, N_CHIPS)), pltpu.SemaphoreType.DMA((n, N_CHIPS))]

    def _copies(self, srcs, outs, sems):
        send_sems, recv_sems = sems
        x, y, c = _my_place()
        return [pltpu.make_async_remote_copy(
            src_ref=block(srcs[t], 2 * ch + 1 - c), dst_ref=outs[t].at[ch], send_sem=send_sems.at[t, ch],
            recv_sem=recv_sems.at[t, ch], device_id=(x, y, 1 - c), device_id_type=MESH)
            for t, (_, block) in enumerate(self.kinds) for ch in range(N_CHIPS)]

    def start(self, srcs, outs, sems):
        for cp in self._copies(srcs, outs, sems):
            cp.start()

    def finish(self, srcs, outs, sems):
        for cp in self._copies(srcs, outs, sems):
            cp.wait()


class _ChipScatter:
    def __init__(self, pieces):
        self.kinds = [_SCATTERED[kind] for kind, _, _ in pieces]
        self.n = n = len(pieces)
        self.args = [g for _, g, _ in pieces] + [s for _, _, s in pieces]
        self.out_shape = [jax.ShapeDtypeStruct((N_CHIPS, *block), BF16) for block, _ in self.kinds]
        self.scratch = []
        for block, _ in self.kinds:
            self.scratch += [pltpu.VMEM((N_CHIPS, *block), BF16)] * 3
        self.scratch += [pltpu.SemaphoreType.DMA((n, N_CHIPS + 1)), pltpu.SemaphoreType.DMA((n, N_CHIPS - 1)),
                         pltpu.SemaphoreType.DMA((n, N_CHIPS - 1)), pltpu.SemaphoreType.DMA((n,))]

    def _sends(self, outs, scr):
        n = self.n
        send_sems, recv_sems, local_sems = scr[3 * n + 1:]
        x, y, c = _my_place()
        chip = 2 * x + y
        copies = []
        for t in range(n):
            total = scr[3 * t + 2]
            copies.append(pltpu.make_async_copy(total.at[chip], outs[t].at[chip], local_sems.at[t]))
            for r in range(1, N_CHIPS):
                to = chip ^ r
                copies.append(pltpu.make_async_remote_copy(
                    src_ref=total.at[to], dst_ref=outs[t].at[chip], send_sem=send_sems.at[t, r - 1],
                    recv_sem=recv_sems.at[t, r - 1], device_id=(to // 2, to % 2, c), device_id_type=MESH))
        return copies

    def start(self, ins, outs, scr):
        n = self.n
        load_sems = scr[3 * n]
        c = lax.axis_index("c")
        loads = []
        for t, (_, block) in enumerate(self.kinds):
            mine, theirs = scr[3 * t], scr[3 * t + 1]
            loads += [pltpu.make_async_copy(block(ins[t], 2 * ch + c), mine.at[ch], load_sems.at[t, ch])
                      for ch in range(N_CHIPS)]
            loads.append(pltpu.make_async_copy(ins[n + t], theirs, load_sems.at[t, N_CHIPS]))
        for cp in loads:
            cp.start()
        for cp in loads:
            cp.wait()
        for t in range(n):
            mine, theirs, total = scr[3 * t:3 * t + 3]
            for ch in range(N_CHIPS):
                total[ch] = (mine[ch].astype(F32) + theirs[ch].astype(F32)).astype(BF16)
        for cp in self._sends(outs, scr):
            cp.start()

    def finish(self, ins, outs, scr):
        for cp in self._sends(outs, scr):
            cp.wait()


class _Jobs:
    def __init__(self, *jobs):
        self.jobs = jobs
        self.args = [a for j in jobs for a in j.args]
        self.out_shape = [o for j in jobs for o in j.out_shape]
        self.scratch = [s for j in jobs for s in j.scratch]

    def _split(self, refs, attr):
        at = 0
        for j in self.jobs:
            n = len(getattr(j, attr))
            yield refs[at:at + n]
            at += n

    def _each(self, ins, outs, scr):
        return zip(self.jobs, self._split(ins, "args"), self._split(outs, "out_shape"), self._split(scr, "scratch"))

    def start(self, ins, outs, scr):
        for j, i, o, s in self._each(ins, outs, scr):
            j.start(i, o, s)

    def finish(self, ins, outs, scr):
        for j, i, o, s in self._each(ins, outs, scr):
            j.finish(i, o, s)

    def split_outputs(self, outs):
        return list(self._split(outs, "out_shape"))


def _adamw_math(w, g, m, v):
    m = ADAM_B1 * m + (1.0 - ADAM_B1) * g
    v = ADAM_B2 * v + (1.0 - ADAM_B2) * (g * g)
    m_hat = m / (1.0 - ADAM_B1 ** ADAM_STEP)
    v_hat = v / (1.0 - ADAM_B2 ** ADAM_STEP)
    delta = -ADAM_LR * (m_hat / (jnp.sqrt(v_hat) + ADAM_EPS) + ADAM_WD * w)
    return delta, m, v


def _adamw(name, w, m, v, landings, n_col_blocks=1, job=None):
    _, r, c = landings[0].shape
    grid = (w.shape[0] // r, n_col_blocks)

    def body(w_ref, m_ref, v_ref, *rest):
        l_refs, (g_ref, d_ref, nm_ref, nv_ref) = rest[:len(landings)], rest[len(landings):]
        step = pl.program_id(0) * n_col_blocks + pl.program_id(1)
        for idx, l_ref in enumerate(l_refs):
            @pl.when(step == idx)
            def _(l_ref=l_ref):
                g = l_ref[0].astype(F32)
                for s in range(1, N_CHIPS):
                    g = g + l_ref[s].astype(F32)
                g_ref[...] = g
                d_ref[...], nm_ref[...], nv_ref[...] = _adamw_math(w_ref[...], g, m_ref[...], v_ref[...])

    spec = pl.BlockSpec((r, c), lambda a, b: (a, b))
    return _launch(
        body, name=f"adamw_{name}", grid=grid,
        in_specs=[spec, spec, spec] + [_full_spec((N_CHIPS, r, c))] * len(landings),
        out_specs=[spec] * 4, out_shape=[jax.ShapeDtypeStruct(w.shape, F32)] * 4,
        args=(w, m, v, *landings), vmem=VMEM_BIG, job=job)


_SMALL = (("pre_mix_g", SV_PRE_MIX, 2), ("post_mix_g", SV_POST_MIX, 2), ("pre_ffn_g", SV_PRE_FFN, 2),
          ("post_ffn_g", SV_POST_FFN, 2), ("ple_g", SV_PLE, 2), ("ple_post_g", SV_PLE_POST, 2), ("kv_g", SV_KV, 1),
          ("pool_scale", SV_POOL_SCALE, 1), ("sinks", SV_SINKS, 1))


def _small_all_reduce_adamw(part, params):
    flat = [a for name, _, _ in _SMALL for a in params[name]]
    n_in = 1 + len(flat)

    def body(*refs):
        part_ref, wmv = refs[0], refs[1:n_in]
        loss_ref, outs = refs[n_in], refs[n_in + 1:n_in + 1 + 4 * len(_SMALL)]
        buf, total, send_sems, recv_sems = refs[n_in + 1 + 4 * len(_SMALL):]
        x, y, c = _my_place()
        me = _dev_index(x, y, c)
        buf[me] = part_ref[...]
        copies = [pltpu.make_async_remote_copy(
            src_ref=part_ref, dst_ref=buf.at[me], send_sem=send_sems.at[r - 1], recv_sem=recv_sems.at[r - 1],
            device_id=_peer_by_relation(r), device_id_type=MESH) for r in range(1, N_DEV)]
        for cp in copies:
            cp.start()
        for cp in copies:
            cp.wait()
        g = buf[0]
        for s in range(1, N_DEV):
            g = g + buf[s]
        total[...] = g
        loss_ref[...] = total[SV_LOSS:SV_LOSS + 1, 0:1]
        for idx, (name, row, n_rows) in enumerate(_SMALL):
            w_ref, m_ref, v_ref = wmv[3 * idx:3 * idx + 3]
            g_ref, d_ref, nm_ref, nv_ref = outs[4 * idx:4 * idx + 4]
            if name == "pool_scale":
                g = total[row:row + 1, pl.ds(pl.multiple_of(me * 128, 128), 128)]
            else:
                g = total[row:row + n_rows, 0:w_ref.shape[1]]
            g_ref[...] = g
            d_ref[...], nm_ref[...], nv_ref[...] = _adamw_math(w_ref[...], g, m_ref[...], v_ref[...])

    vm = pl.BlockSpec(memory_space=pltpu.VMEM)
    out_shape = [jax.ShapeDtypeStruct((1, 1), F32)]
    for name, _, _ in _SMALL:
        out_shape += [jax.ShapeDtypeStruct(params[name][0].shape, F32)] * 4
    res = pl.pallas_call(
        body, name="small_all_reduce_adamw", out_shape=out_shape,
        in_specs=[vm] * n_in, out_specs=[vm] * len(out_shape),
        scratch_shapes=[pltpu.VMEM((N_DEV, SV_ROWS, D_MODEL), F32), pltpu.VMEM((SV_ROWS, D_MODEL), F32),
                        pltpu.SemaphoreType.DMA((N_DEV - 1,)), pltpu.SemaphoreType.DMA((N_DEV - 1,))],
    )(part, *flat)
    return res[0], {name: res[1 + 4 * idx:5 + 4 * idx] for idx, (name, _, _) in enumerate(_SMALL)}


def _local_step(x, p, tgt, gains, sinks, shards, weights):
    row = _Gain
    gather = lambda *names: _AllGather(names, shards)
    g_pre_mix, g_post_mix = gains["pre_mix_g"], gains["post_mix_g"]
    g_pre_ffn, g_post_ffn = gains["pre_ffn_g"], gains["post_ffn_g"]
    g_ple, g_ple_post, g_kv = gains["ple_g"], gains["ple_post_g"], _Gain(gains["kv_g"], 0)

    wp, scale, wgu0 = _all_gather_only("gather_first", ("pool", "scale", "gu0"), shards)
    (x1_0, h2_0, yraw, dpool), (wd0,) = _fwd_pool_mixer(
        x, row(g_pre_mix, 0), wp, scale, row(g_post_mix, 0), row(g_pre_ffn, 0), job=gather("wd0"))
    (gs0, us0, f0, x2_0, h3_0), (wgate0, wproj0, wgu1) = _fwd_ffn(
        0, h2_0, x1_0, wgu0, wd0, row(g_post_ffn, 0), row(g_ple, 0), job=gather("gate0", "proj0", "gu1"))
    (x3_0, z0, pe0), (wkv, wq) = _fwd_ple(0, x2_0, h3_0, p[0], wgate0, wproj0, row(g_ple_post, 0),
                                          job=gather("kv", "q"))
    (hk, h1, q, kv), (wo,) = _fwd_qkv(x3_0, g_kv, row(g_pre_mix, 1), wkv, wq, job=gather("o"))
    front = ((ATT_BLOCK, 0), (0, 0))
    kpad = jnp.pad(kv[:, :KV_DIM], front)
    vpad = jnp.pad(kv[:, KV_DIM:], front)
    (attn,), (wd1,) = _fwd_attention(q, kpad, vpad, sinks, job=gather("wd1"))
    (y1, x1_1, h2_1), _ = _fwd_attn_out(attn, x3_0, wo, row(g_post_mix, 1), row(g_pre_ffn, 1))
    (gs1, us1, f1, x2_1, h3_1), (wgate1, wproj1) = _fwd_ffn(
        1, h2_1, x1_1, wgu1, wd1, row(g_post_ffn, 1), row(g_ple, 1), job=gather("gate1", "proj1"))
    (dx3_1, z1, pe1, loss), _ = _fwd_ple(1, x2_1, h3_1, p[1], wgate1, wproj1, row(g_ple_post, 1), target=tgt)

    produced, swapped, landed = {}, {}, {}

    def kind_of(name):
        return name.rstrip("0123_")

    def carry(swap=(), spread=()):
        jobs = []
        if swap:
            jobs.append(_SiblingSwap([(kind_of(n), produced[n]) for n in swap]))
        if spread:
            jobs.append(_ChipScatter([(kind_of(n), produced[n], swapped[n]) for n in spread]))
        return _Jobs(*jobs)

    def carried(jobs, outs, swap=(), spread=()):
        parts = jobs.split_outputs(outs)
        if swap:
            swapped.update(zip(swap, parts[0]))
        if spread:
            landed.update(zip(spread, parts[-1]))

    def hosted(call, *args, swap=(), spread=()):
        jobs = carry(swap, spread)
        outs, job_outs = call(*args, job=jobs)
        carried(jobs, job_outs, swap, spread)
        return outs

    def ffn_weight_grads(layer, h2, df, dg, du, a, hosts):
        for qtr in range(FF_PARTS):
            dgu, dwd = hosted(_bwd_ffn_dw, layer, qtr, h2, df, dg, du, a, **hosts[qtr])
            produced[f"gu{layer}_{qtr}"], produced[f"wd{layer}_{qtr}"] = dgu, dwd

    ffn_q = lambda layer, qtr: (f"gu{layer}_{qtr}", f"wd{layer}_{qtr}")

    dx2_1, df1, produced["gate1"], produced["proj1"], dg_ple_post1, dg_ple1, dg_post_ffn1 = hosted(
        _bwd_ple, 1, dx3_1, x2_1, z1, pe1, h3_1, p[1], f1, wgate1, row(g_ple_post, 1), row(g_ple, 1),
        row(g_post_ffn, 1))
    dh2_1, dg1, du1, a1 = hosted(_bwd_ffn_act, 1, df1, gs1, us1, wgu1, wd1, swap=("gate1", "proj1"))
    ffn_weight_grads(1, h2_1, df1, dg1, du1, a1, [dict(spread=("gate1", "proj1")), dict(swap=ffn_q(1, 0))])
    dx1_1, dattn, produced["o"], dg_pre_ffn1, dg_post_mix1 = hosted(
        _bwd_attn_out, dx2_1, dh2_1, x1_1, y1, attn, wo, row(g_pre_ffn, 1), row(g_post_mix, 1), swap=ffn_q(1, 1))
    dq, dkpad, dvpad, dsinks = hosted(_bwd_attention, q, dattn, kpad, vpad, sinks,
                                      spread=ffn_q(1, 0) + ffn_q(1, 1))
    dkv = jnp.concatenate([dkpad[ATT_BLOCK:], dvpad[ATT_BLOCK:]], axis=1).astype(BF16)
    dx3_0, produced["q"], produced["kv"], dg_pre_mix1, dg_kv = hosted(
        _bwd_qkv, dx1_1, dq, dkv, x3_0, h1, hk, wq, wkv, row(g_pre_mix, 1), g_kv, swap=("o",))
    dx2_0, df0, produced["gate0"], produced["proj0"], dg_ple_post0, dg_ple0, dg_post_ffn0 = hosted(
        _bwd_ple, 0, dx3_0, x2_0, z0, pe0, h3_0, p[0], f0, wgate0, row(g_ple_post, 0), row(g_ple, 0),
        row(g_post_ffn, 0), swap=("q", "kv"), spread=("o",))
    dh2_0, dg0, du0, a0 = hosted(_bwd_ffn_act, 0, df0, gs0, us0, wgu0, wd0,
                                 swap=("gate0", "proj0"), spread=("q", "kv"))
    ffn_weight_grads(0, h2_0, df0, dg0, du0, a0, [dict(spread=("gate0", "proj0")), dict(swap=ffn_q(0, 0))])
    grad_x, produced["pool"], dscale, dg_pre_ffn0, dg_post_mix0, dg_pre_mix0 = hosted(
        _bwd_pool_mixer, dx2_0, dh2_0, x1_0, x, yraw, dpool, wp, scale, row(g_pre_ffn, 0), row(g_post_mix, 0),
        row(g_pre_mix, 0), swap=ffn_q(0, 1), spread=ffn_q(0, 0))

    def update(name, n_col_blocks=1, pieces=None, swap=(), spread=()):
        w, m, v = weights[name]
        rows = w.size // w.shape[-1]
        flat = [landed[n].reshape(N_CHIPS, -1, landed[n].shape[-1]) for n in (pieces or [kind_short[name]])]
        outs = hosted(_adamw, name, w.reshape(rows, -1), m.reshape(rows, -1), v.reshape(rows, -1), flat,
                      n_col_blocks, swap=swap, spread=spread)
        return [o.reshape(w.shape) for o in outs]

    kind_short = {"w_q": "q", "w_kv": "kv", "w_o": "o", "pool_w": "pool"}
    upd = {}
    upd["w_ple_gate"] = update("w_ple_gate", pieces=("gate0", "gate1"), swap=("pool",), spread=ffn_q(0, 1))
    upd["w_ple_proj"] = update("w_ple_proj", pieces=("proj0", "proj1"), spread=("pool",))
    for name in ("w_q", "w_kv", "w_o", "pool_w"):
        upd[name] = update(name)
    upd["w_gu"] = update("w_gu", FF_PARTS,
                         pieces=[f"gu{layer}_{qtr}" for layer in range(2) for qtr in range(FF_PARTS)])
    upd["w_gu"] = [jnp.swapaxes(a, 1, 2) for a in upd["w_gu"]]
    upd["w_down"] = update("w_down", FF_PARTS,
                           pieces=[f"wd{layer}_{qtr}" for layer in range(2) for qtr in range(FF_PARTS)])

    lanes = lambda a: jnp.pad(a, ((0, 0), (0, D_MODEL - a.shape[1])))
    small = jnp.concatenate([
        dg_pre_mix0, dg_pre_mix1, dg_post_mix0, dg_post_mix1, dg_pre_ffn0, dg_pre_ffn1, dg_post_ffn0, dg_post_ffn1,
        dg_ple0, dg_ple1, dg_ple_post0, dg_ple_post1, dg_kv, dscale, lanes(dsinks[:, :N_HEADS]), lanes(loss)], axis=0)
    return grad_x, upd, small


def kernel(x, p, pre_mix_g, post_mix_g, pre_ffn_g, post_ffn_g, pool_w, pool_scale, kv_g, w_kv, w_q, sinks, w_o, w_gu, w_down, ple_g, w_ple_gate, w_ple_proj, ple_post_g, loss_target, m_pre_mix_g, m_post_mix_g, m_pre_ffn_g, m_post_ffn_g, m_pool_w, m_pool_scale, m_kv_g, m_w_kv, m_w_q, m_sinks, m_w_o, m_w_gu, m_w_down, m_ple_g, m_w_ple_gate, m_w_ple_proj, m_ple_post_g, v_pre_mix_g, v_post_mix_g, v_pre_ffn_g, v_post_ffn_g, v_pool_w, v_pool_scale, v_kv_g, v_w_kv, v_w_q, v_sinks, v_w_o, v_w_gu, v_w_down, v_ple_g, v_w_ple_gate, v_w_ple_proj, v_ple_post_g):
    shards = {"pool": pool_w[0].astype(BF16), "scale": pool_scale, "kv": w_kv.astype(BF16),
              "q": w_q[0].astype(BF16), "o": w_o[0].astype(BF16)}
    for layer in range(2):
        shards[f"gu{layer}"] = w_gu[layer].T.astype(BF16)
        shards[f"wd{layer}"] = w_down[layer].astype(BF16)
        shards[f"gate{layer}"] = w_ple_gate[layer].astype(BF16)
        shards[f"proj{layer}"] = w_ple_proj[layer].astype(BF16)
    stacked = lambda g: g.reshape(-1, 1, D_MODEL)
    gains = dict(pre_mix_g=stacked(pre_mix_g), post_mix_g=stacked(post_mix_g), pre_ffn_g=stacked(pre_ffn_g),
                 post_ffn_g=stacked(post_ffn_g), ple_g=stacked(ple_g), ple_post_g=stacked(ple_post_g),
                 kv_g=stacked(kv_g))
    weights = {"pool_w": (pool_w, m_pool_w, v_pool_w), "w_kv": (w_kv, m_w_kv, v_w_kv), "w_q": (w_q, m_w_q, v_w_q),
               "w_o": (w_o, m_w_o, v_w_o), "w_down": (w_down, m_w_down, v_w_down),
               "w_gu": tuple(jnp.swapaxes(a, 1, 2) for a in (w_gu, m_w_gu, v_w_gu)),
               "w_ple_gate": (w_ple_gate, m_w_ple_gate, v_w_ple_gate),
               "w_ple_proj": (w_ple_proj, m_w_ple_proj, v_w_ple_proj)}
    grad_x, upd, small = _local_step(x[0], p[:, 0], loss_target[0], gains, sinks, shards, weights)

    small_params = {
        "pre_mix_g": (pre_mix_g, m_pre_mix_g, v_pre_mix_g), "post_mix_g": (post_mix_g, m_post_mix_g, v_post_mix_g),
        "pre_ffn_g": (pre_ffn_g, m_pre_ffn_g, v_pre_ffn_g), "post_ffn_g": (post_ffn_g, m_post_ffn_g, v_post_ffn_g),
        "ple_g": (ple_g, m_ple_g, v_ple_g), "ple_post_g": (ple_post_g, m_ple_post_g, v_ple_post_g),
        "kv_g": (kv_g[None, :], m_kv_g[None, :], v_kv_g[None, :]),
        "pool_scale": (pool_scale, m_pool_scale, v_pool_scale), "sinks": (sinks, m_sinks, v_sinks)}
    loss, small_upd = _small_all_reduce_adamw(small, small_params)
    small_upd["kv_g"] = [a[0] for a in small_upd["kv_g"]]
    upd.update(small_upd)

    names = ["pre_mix_g", "post_mix_g", "pre_ffn_g", "post_ffn_g", "pool_w", "pool_scale", "kv_g", "w_kv", "w_q",
             "sinks", "w_o", "w_gu", "w_down", "ple_g", "w_ple_gate", "w_ple_proj", "ple_post_g"]
    outs = [loss[0, 0], grad_x[None]]
    for kind in range(4):
        outs += [upd[n][kind] for n in names]
    return tuple(outs)
```

```python
import functools

import jax
import jax.numpy as jnp
from jax import lax
from jax.experimental import pallas as pl
from jax.experimental.pallas import tpu as pltpu

F32 = jnp.float32
BF16 = jnp.bfloat16

N_DEV = 8
D_MODEL = 1024
N_POOL_GROUPS = 4
POOL_GROUP = 256
POOL_HALO = 16
HEAD_DIM = 64
N_HEADS = 16
N_KV_HEADS = 4
GQA_GROUP = 4
KV_DIM = N_KV_HEADS * HEAD_DIM
ATT_BLOCK = 128
D_FF = 2816
FF_CHUNKS = 4
FF_BLOCK = D_FF // FF_CHUNKS
WD_ROWS = D_FF // N_DEV
FF_PARTS = 2
FF_PART = D_MODEL // FF_PARTS
N_CHIPS = 4
PLE_DIM = 256
EPS = 1e-6
NEG_INF = -1e30
ATT_SCALE = HEAD_DIM ** -0.5

ADAM_LR = 0.001
ADAM_B1 = 0.9
ADAM_B2 = 0.999
ADAM_EPS = 1e-08
ADAM_WD = 0.01
ADAM_STEP = 10

ROW_TILE = 512
FFN_ROW_TILE = 512
FFN_SUB_TILES = 2
VMEM_BIG = 56 * 1024 * 1024
VMEM_MID = 48 * 1024 * 1024
HBM_PIN_ELEMS = 1024

SV_ROWS = 16
SV_PRE_MIX, SV_POST_MIX, SV_PRE_FFN, SV_POST_FFN, SV_PLE, SV_PLE_POST = 0, 2, 4, 6, 8, 10
SV_KV, SV_POOL_SCALE, SV_SINKS, SV_LOSS = 12, 13, 14, 15

MESH = pl.DeviceIdType.MESH
ANY = pl.BlockSpec(memory_space=pl.ANY)


def _dot(a, b):
    return jnp.dot(a, b, preferred_element_type=F32)


def _dot_nt(a, b):
    return lax.dot_general(a, b, (((1,), (1,)), ((), ())), preferred_element_type=F32)


def _dot_tn(a, b):
    return lax.dot_general(a, b, (((0,), (0,)), ((), ())), preferred_element_type=F32)


def _rstd(x):
    return lax.rsqrt(jnp.mean(x * x, axis=-1, keepdims=True) + EPS)


def _rms(x, g):
    return x * _rstd(x) * g


def _rms_bwd(x, g, dy):
    r = _rstd(x)
    n = x * r
    dn = dy * g
    dx = r * (dn - n * jnp.mean(dn * n, axis=-1, keepdims=True))
    dg = jnp.sum(dy * n, axis=0, keepdims=True)
    return dx, dg


def _sigmoid(x):
    return 1.0 / (1.0 + jnp.exp(-x))


def _acc(ref, val, first):
    @pl.when(first)
    def _():
        ref[...] = val

    @pl.when(jnp.logical_not(first))
    def _():
        ref[...] += val


def _pool_counts(row0, rows):
    t = row0 + lax.broadcasted_iota(jnp.int32, (rows, D_MODEL), 0) + 1
    grp = lax.broadcasted_iota(jnp.int32, (rows, D_MODEL), 1) // POOL_GROUP
    win = jnp.left_shift(2, grp)
    return jnp.minimum(t, win).astype(F32)


def _window_sums(ext, shift_of):
    outs = []
    s = ext
    for gi in range(N_POOL_GROUPS):
        s = s + pltpu.roll(s, shift_of(1 << gi), axis=0)
        outs.append(s[:, :POOL_GROUP])
        s = s[:, POOL_GROUP:]
    return jnp.concatenate(outs, axis=1)


def _cparams(n_axes, vmem):
    return pltpu.CompilerParams(dimension_semantics=("arbitrary",) * n_axes, vmem_limit_bytes=vmem)


def _row_spec(cols, tm=ROW_TILE):
    return pl.BlockSpec((tm, cols), lambda i: (i, 0))


def _full_spec(shape):
    zeros = (0,) * len(shape)
    return pl.BlockSpec(shape, lambda *_: zeros)


def _vec_spec():
    return _full_spec((1, D_MODEL))


class _Gain:
    def __init__(self, stacked, layer):
        self.stacked, self.layer = stacked, layer

    def spec(self):
        layer = self.layer
        return pl.BlockSpec((None, 1, D_MODEL), lambda *_: (layer, 0, 0))


def _in_hbm(a):
    return pltpu.with_memory_space_constraint(a, pltpu.HBM) if a.size >= HBM_PIN_ELEMS else a


def _launch(body, *, name, grid, in_specs, out_specs, out_shape, args, scratch_shapes=(), vmem=VMEM_MID, job=None):
    in_specs = [a.spec() if isinstance(a, _Gain) else s for s, a in zip(in_specs, args)]
    args = [_in_hbm(a.stacked if isinstance(a, _Gain) else a) for a in args]
    n_in, n_out, n_scr = len(args), len(out_shape), len(scratch_shapes)
    j_args, j_out, j_scr = ([], [], []) if job is None else ([_in_hbm(a) for a in job.args], job.out_shape, job.scratch)

    def run(*refs):
        groups, at = [], 0
        for n in (n_in, len(j_args), n_out, len(j_out), n_scr, len(j_scr)):
            groups.append(refs[at:at + n])
            at += n
        ins, j_ins, outs, j_outs, scr, j_sems = groups
        if job is None:
            body(*ins, *outs, *scr)
        elif not grid:
            job.start(j_ins, j_outs, j_sems)
            body(*ins, *outs, *scr)
            job.finish(j_ins, j_outs, j_sems)
        else:
            ids = [pl.program_id(a) for a in range(len(grid))]
            first = functools.reduce(jnp.logical_and, [i == 0 for i in ids])
            last = functools.reduce(jnp.logical_and, [i == g - 1 for i, g in zip(ids, grid)])
            pl.when(first)(lambda: job.start(j_ins, j_outs, j_sems))
            body(*ins, *outs, *scr)
            pl.when(last)(lambda: job.finish(j_ins, j_outs, j_sems))

    res = pl.pallas_call(
        run, name=name, grid=grid,
        in_specs=list(in_specs) + [ANY] * len(j_args), out_specs=list(out_specs) + [ANY] * len(j_out),
        out_shape=list(out_shape) + list(j_out), scratch_shapes=list(scratch_shapes) + list(j_scr),
        compiler_params=_cparams(len(grid), vmem),
    )(*args, *j_args)
    return res[:n_out], res[n_out:]


def _fwd_pool_mixer(x, g_pre, wp, scale, g_post, g_ffn, job=None):
    T = x.shape[0]
    tm = ROW_TILE
    nt = T // tm

    def body(x_ref, gpre_ref, wp_ref, sc_ref, gpost_ref, gffn_ref, x1_ref, h2_ref, yraw_ref, d_ref, carry):
        i = pl.program_id(0)

        @pl.when(i == 0)
        def _():
            carry[...] = jnp.zeros_like(carry)

        xv = x_ref[...]
        h = _rms(xv, gpre_ref[...])
        ext = jnp.concatenate([carry[...], h], axis=0)
        carry[...] = h[tm - POOL_HALO:, :]
        sums = _window_sums(ext, lambda k: k)[POOL_HALO:, :]
        d = sums / _pool_counts(i * tm, tm) - h
        db = d.astype(BF16)
        d_ref[...] = db
        yraw = jnp.concatenate(
            [_dot(db[:, g * POOL_GROUP:(g + 1) * POOL_GROUP], wp_ref[g]) for g in range(N_POOL_GROUPS)], axis=1)
        yraw_ref[...] = yraw
        x1 = xv + _rms(yraw * sc_ref[...], gpost_ref[...])
        x1_ref[...] = x1
        h2_ref[...] = _rms(x1, gffn_ref[...]).astype(BF16)

    return _launch(
        body, name="fwd_pool_mixer", grid=(nt,),
        in_specs=[_row_spec(D_MODEL), _vec_spec(), _full_spec((N_POOL_GROUPS, POOL_GROUP, POOL_GROUP)), _vec_spec(),
                  _vec_spec(), _vec_spec()],
        out_specs=[_row_spec(D_MODEL)] * 4,
        out_shape=[jax.ShapeDtypeStruct((T, D_MODEL), F32), jax.ShapeDtypeStruct((T, D_MODEL), BF16),
                   jax.ShapeDtypeStruct((T, D_MODEL), F32), jax.ShapeDtypeStruct((T, D_MODEL), BF16)],
        scratch_shapes=[pltpu.VMEM((POOL_HALO, D_MODEL), F32)],
        args=(x, g_pre, wp, scale, g_post, g_ffn), job=job)


def _fwd_ffn(layer, h2, x1, wgu, wd, g_post, g_ple, job=None):
    T = h2.shape[0]
    tm = min(FFN_ROW_TILE, T)
    nt = T // tm
    sub = tm // FFN_SUB_TILES
    last = FF_CHUNKS - 1

    def body(h2_ref, x1_ref, wgu_ref, wd_ref, gpost_ref, gple_ref, gs_ref, us_ref, f_ref, x2_ref, h3_ref, acc):
        k = pl.program_id(0)
        i = pl.program_id(1)
        rows = pl.ds(pl.multiple_of(i * tm, tm), tm)
        parts = []
        for s in range(FFN_SUB_TILES):
            r = pl.ds(s * sub, sub)
            h = h2_ref[r, :]
            g = _dot_nt(h, wgu_ref[0])
            u = _dot_nt(h, wgu_ref[1])
            gs_ref[r, :] = g.astype(BF16)
            us_ref[r, :] = u.astype(BF16)
            a = (g * _sigmoid(g) * u).astype(BF16)
            parts.append(_dot(a, wd_ref[...]))
        part = jnp.concatenate(parts, axis=0)

        @pl.when(k == 0)
        def _():
            acc[rows, :] = part

        @pl.when(jnp.logical_and(k > 0, k < last))
        def _():
            acc[rows, :] += part

        @pl.when(k == last)
        def _():
            f = acc[rows, :] + part
            f_ref[...] = f
            x2 = x1_ref[...] + _rms(f, gpost_ref[...])
            x2_ref[...] = x2
            h3_ref[...] = _rms(x2, gple_ref[...]).astype(BF16)

    def late(k, i):
        return (jnp.where(k == last, i, 0), 0)

    return _launch(
        body, name=f"fwd_ffn{layer}", grid=(FF_CHUNKS, nt),
        in_specs=[pl.BlockSpec((tm, D_MODEL), lambda k, i: (i, 0)),
                  pl.BlockSpec((tm, D_MODEL), late),
                  pl.BlockSpec((None, 2, FF_BLOCK, D_MODEL), lambda k, i: (k, 0, 0, 0)),
                  pl.BlockSpec((FF_BLOCK, D_MODEL), lambda k, i: (k, 0)),
                  pl.BlockSpec((1, D_MODEL), lambda k, i: (0, 0)),
                  pl.BlockSpec((1, D_MODEL), lambda k, i: (0, 0))],
        out_specs=[pl.BlockSpec((None, tm, FF_BLOCK), lambda k, i: (k, i, 0)),
                   pl.BlockSpec((None, tm, FF_BLOCK), lambda k, i: (k, i, 0)),
                   pl.BlockSpec((tm, D_MODEL), late),
                   pl.BlockSpec((tm, D_MODEL), late),
                   pl.BlockSpec((tm, D_MODEL), late)],
        out_shape=[jax.ShapeDtypeStruct((FF_CHUNKS, T, FF_BLOCK), BF16),
                   jax.ShapeDtypeStruct((FF_CHUNKS, T, FF_BLOCK), BF16),
                   jax.ShapeDtypeStruct((T, D_MODEL), F32),
                   jax.ShapeDtypeStruct((T, D_MODEL), F32),
                   jax.ShapeDtypeStruct((T, D_MODEL), BF16)],
        scratch_shapes=[pltpu.VMEM((T, D_MODEL), F32)],
        args=(h2, x1, wgu, wd, g_post, g_ple), vmem=VMEM_BIG, job=job)


def _fwd_ple(layer, x2, h3, p, wgate, wproj, g_post, target=None, job=None):
    T = x2.shape[0]
    tm = ROW_TILE
    nt = T // tm
    with_loss = target is not None

    def body(*refs):
        if with_loss:
            x2_ref, h3_ref, p_ref, wg_ref, wp_ref, gpost_ref, tgt_ref, out_ref, z_ref, pe_ref, loss_ref = refs
        else:
            x2_ref, h3_ref, p_ref, wg_ref, wp_ref, gpost_ref, out_ref, z_ref, pe_ref = refs
        z = _dot(h3_ref[...], wg_ref[...])
        pe = _dot(p_ref[...].astype(BF16), wp_ref[...])
        z_ref[...] = z
        pe_ref[...] = pe
        x3 = x2_ref[...] + _rms(pe * _sigmoid(z), gpost_ref[...])
        if with_loss:
            err = x3 - tgt_ref[...]
            out_ref[...] = err * (1.0 / D_MODEL)
            part = 0.5 * jnp.sum(jnp.mean(err * err, axis=-1, keepdims=True), axis=0, keepdims=True)
            _acc(loss_ref, part, pl.program_id(0) == 0)
        else:
            out_ref[...] = x3

    in_specs = [_row_spec(D_MODEL), _row_spec(D_MODEL), _row_spec(PLE_DIM), _full_spec((D_MODEL, D_MODEL)),
                _full_spec((PLE_DIM, D_MODEL)), _vec_spec()]
    out_specs = [_row_spec(D_MODEL)] * 3
    out_shape = [jax.ShapeDtypeStruct((T, D_MODEL), F32)] * 3
    args = [x2, h3, p, wgate, wproj, g_post]
    if with_loss:
        in_specs.append(_row_spec(D_MODEL))
        out_specs.append(_full_spec((1, 1)))
        out_shape.append(jax.ShapeDtypeStruct((1, 1), F32))
        args.append(target)
    return _launch(body, name=f"fwd_ple{layer}", grid=(nt,), in_specs=in_specs, out_specs=out_specs,
                   out_shape=out_shape, args=args, job=job)


def _fwd_qkv(x3, g_kv, g_mix, wkv, wq, job=None):
    T = x3.shape[0]
    nt = T // ROW_TILE

    def body(x_ref, gkv_ref, gmix_ref, wkv_ref, wq_ref, hk_ref, h1_ref, q_ref, kv_ref):
        xv = x_ref[...]
        r = _rstd(xv)
        hk = (xv * r * gkv_ref[...]).astype(BF16)
        h1 = (xv * r * gmix_ref[...]).astype(BF16)
        hk_ref[...] = hk
        h1_ref[...] = h1
        kv_ref[...] = _dot(hk, wkv_ref[...]).astype(BF16)
        q_ref[...] = _dot(h1, wq_ref[...]).astype(BF16)

    return _launch(
        body, name="fwd_qkv", grid=(nt,),
        in_specs=[_row_spec(D_MODEL), _vec_spec(), _vec_spec(), _full_spec((D_MODEL, 2 * KV_DIM)),
                  _full_spec((D_MODEL, D_MODEL))],
        out_specs=[_row_spec(D_MODEL), _row_spec(D_MODEL), _row_spec(D_MODEL), _row_spec(2 * KV_DIM)],
        out_shape=[jax.ShapeDtypeStruct((T, D_MODEL), BF16)] * 3 + [jax.ShapeDtypeStruct((T, 2 * KV_DIM), BF16)],
        args=(x3, g_kv, g_mix, wkv, wq), job=job)


def _alibi_slope(h):
    return 2.0 ** (-8.0 * (h + 1) / N_HEADS)


ATT_SUB = 32
ATT_GROUP_ROWS = GQA_GROUP * ATT_BLOCK


def _att_mask(n, row0):
    qi = lax.broadcasted_iota(jnp.int32, (ATT_SUB, 2 * ATT_BLOCK), 0) + row0
    si = lax.broadcasted_iota(jnp.int32, (ATT_SUB, 2 * ATT_BLOCK), 1)
    rel = ATT_BLOCK + qi - si
    valid = (rel >= 0) & (rel < ATT_BLOCK) & ((si >= ATT_BLOCK) | (n > 0))
    return rel.astype(F32), valid


def _att_probs(raw, relf, valid, slope, sink):
    s = jnp.where(valid, raw * ATT_SCALE - slope * relf, NEG_INF)
    m = jnp.maximum(jnp.max(s, axis=-1, keepdims=True), sink)
    e = jnp.exp(s - m)
    es = jnp.exp(sink - m)
    inv = 1.0 / (jnp.sum(e, axis=-1, keepdims=True) + es)
    return e * inv, es * inv


def _stack_heads(ref, kh):
    first = kh * GQA_GROUP
    return jnp.concatenate([ref[:, (first + g) * HEAD_DIM:(first + g + 1) * HEAD_DIM] for g in range(GQA_GROUP)], axis=0)


def _unstack_heads(stacked):
    return [stacked[g * ATT_BLOCK:(g + 1) * ATT_BLOCK, :] for g in range(GQA_GROUP)]


def _fwd_attention(q, kpad, vpad, sinks, job=None):
    T = q.shape[0]
    nb = T // ATT_BLOCK

    def body(q_ref, k_ref, v_ref, sink_ref, o_ref, s_scr, p_scr):
        n = pl.program_id(0)
        start = pl.multiple_of(n * ATT_BLOCK, ATT_BLOCK)
        kw = k_ref[pl.ds(start, 2 * ATT_BLOCK), :]
        vw = v_ref[pl.ds(start, 2 * ATT_BLOCK), :]
        outs = []
        for kh in range(N_KV_HEADS):
            kk = kw[:, kh * HEAD_DIM:(kh + 1) * HEAD_DIM]
            vv = vw[:, kh * HEAD_DIM:(kh + 1) * HEAD_DIM]
            s_scr[...] = _dot_nt(_stack_heads(q_ref, kh), kk)
            for g in range(GQA_GROUP):
                h = kh * GQA_GROUP + g
                for row0 in range(0, ATT_BLOCK, ATT_SUB):
                    rows = pl.ds(g * ATT_BLOCK + row0, ATT_SUB)
                    relf, valid = _att_mask(n, row0)
                    pr, _ = _att_probs(s_scr[rows, :], relf, valid, _alibi_slope(h), sink_ref[0, h])
                    p_scr[rows, :] = pr.astype(BF16)
            outs += _unstack_heads(_dot(p_scr[...], vv))
        o_ref[...] = jnp.concatenate(outs, axis=1).astype(BF16)

    return _launch(
        body, name="fwd_attention", grid=(nb,),
        in_specs=[_row_spec(D_MODEL, ATT_BLOCK), _full_spec((T + ATT_BLOCK, KV_DIM)), _full_spec((T + ATT_BLOCK, KV_DIM)),
                  pl.BlockSpec(memory_space=pltpu.SMEM)],
        out_specs=[_row_spec(D_MODEL, ATT_BLOCK)],
        out_shape=[jax.ShapeDtypeStruct((T, D_MODEL), BF16)],
        scratch_shapes=[pltpu.VMEM((ATT_GROUP_ROWS, 2 * ATT_BLOCK), F32), pltpu.VMEM((ATT_GROUP_ROWS, 2 * ATT_BLOCK), BF16)],
        args=(q, kpad, vpad, sinks), job=job)


def _fwd_attn_out(attn, x, wo, g_post, g_ffn, job=None):
    T = x.shape[0]
    nt = T // ROW_TILE

    def body(a_ref, x_ref, wo_ref, gpost_ref, gffn_ref, y_ref, x1_ref, h2_ref):
        y = _dot(a_ref[...], wo_ref[...])
        y_ref[...] = y
        x1 = x_ref[...] + _rms(y, gpost_ref[...])
        x1_ref[...] = x1
        h2_ref[...] = _rms(x1, gffn_ref[...]).astype(BF16)

    return _launch(
        body, name="fwd_attn_out", grid=(nt,),
        in_specs=[_row_spec(D_MODEL), _row_spec(D_MODEL), _full_spec((D_MODEL, D_MODEL)), _vec_spec(), _vec_spec()],
        out_specs=[_row_spec(D_MODEL)] * 3,
        out_shape=[jax.ShapeDtypeStruct((T, D_MODEL), F32), jax.ShapeDtypeStruct((T, D_MODEL), F32),
                   jax.ShapeDtypeStruct((T, D_MODEL), BF16)],
        args=(attn, x, wo, g_post, g_ffn), job=job)


def _bwd_ple(layer, dx3, x2, z, pe, h3, p, f, wgate, g_ple_post, g_ple, g_post_ffn, job=None):
    T = x2.shape[0]
    tm = ROW_TILE
    nt = T // tm

    def body(dx3_ref, x2_ref, z_ref, pe_ref, h3_ref, p_ref, f_ref, wg_ref, gpp_ref, gp_ref, gpf_ref,
             dx2_ref, df_ref, dwg_ref, dwp_ref, dgpp_ref, dgp_ref, dgpf_ref, acc_g, acc_p):
        i = pl.program_id(0)
        first = i == 0
        dx3v = dx3_ref[...]
        gate = _sigmoid(z_ref[...])
        pev = pe_ref[...]
        de, dgpp = _rms_bwd(pev * gate, gpp_ref[...], dx3v)
        dpe = (de * gate).astype(BF16)
        dz = (de * pev * gate * (1.0 - gate)).astype(BF16)
        _acc(acc_p, _dot_tn(p_ref[...].astype(BF16), dpe), first)
        _acc(acc_g, _dot_tn(h3_ref[...], dz), first)
        dh3 = _dot_nt(dz, wg_ref[...])
        dxn, dgp = _rms_bwd(x2_ref[...], gp_ref[...], dh3)
        dx2 = dx3v + dxn
        dx2_ref[...] = dx2
        df, dgpf = _rms_bwd(f_ref[...], gpf_ref[...], dx2)
        df_ref[...] = df.astype(BF16)
        _acc(dgpp_ref, dgpp, first)
        _acc(dgp_ref, dgp, first)
        _acc(dgpf_ref, dgpf, first)

        @pl.when(i == nt - 1)
        def _():
            dwg_ref[...] = acc_g[...].astype(BF16)
            dwp_ref[...] = acc_p[...].astype(BF16)

    return _launch(
        body, name=f"bwd_ple{layer}", grid=(nt,),
        in_specs=[_row_spec(D_MODEL)] * 5 + [_row_spec(PLE_DIM), _row_spec(D_MODEL), _full_spec((D_MODEL, D_MODEL)),
                  _vec_spec(), _vec_spec(), _vec_spec()],
        out_specs=[_row_spec(D_MODEL), _row_spec(D_MODEL), _full_spec((D_MODEL, D_MODEL)), _full_spec((PLE_DIM, D_MODEL)),
                   _vec_spec(), _vec_spec(), _vec_spec()],
        out_shape=[jax.ShapeDtypeStruct((T, D_MODEL), F32), jax.ShapeDtypeStruct((T, D_MODEL), BF16),
                   jax.ShapeDtypeStruct((D_MODEL, D_MODEL), BF16), jax.ShapeDtypeStruct((PLE_DIM, D_MODEL), BF16)]
                  + [jax.ShapeDtypeStruct((1, D_MODEL), F32)] * 3,
        scratch_shapes=[pltpu.VMEM((D_MODEL, D_MODEL), F32), pltpu.VMEM((PLE_DIM, D_MODEL), F32)],
        args=(dx3, x2, z, pe, h3, p, f, wgate, g_ple_post, g_ple, g_post_ffn), vmem=VMEM_BIG, job=job)


def _bwd_ffn_act(layer, df, gs, us, wgu, wd, job=None):
    T = df.shape[0]
    tm = min(FFN_ROW_TILE, T)
    nt = T // tm
    sub = tm // FFN_SUB_TILES
    last = FF_CHUNKS - 1

    def body(df_ref, gs_ref, us_ref, wgu_ref, wd_ref, dh_ref, dg_ref, du_ref, a_ref, acc_h):
        k = pl.program_id(0)
        i = pl.program_id(1)
        rows = pl.ds(pl.multiple_of(i * tm, tm), tm)
        dhs = []
        for s in range(FFN_SUB_TILES):
            r = pl.ds(s * sub, sub)
            g = gs_ref[r, :].astype(F32)
            u = us_ref[r, :].astype(F32)
            sg = _sigmoid(g)
            silu = g * sg
            a_ref[r, :] = (silu * u).astype(BF16)
            da = _dot_nt(df_ref[r, :], wd_ref[...])
            dg = (da * u * (sg * (1.0 + g * (1.0 - sg)))).astype(BF16)
            du = (da * silu).astype(BF16)
            dg_ref[r, :] = dg
            du_ref[r, :] = du
            dhs.append(_dot(dg, wgu_ref[0]) + _dot(du, wgu_ref[1]))
        dh = jnp.concatenate(dhs, axis=0)

        @pl.when(k == 0)
        def _():
            acc_h[rows, :] = dh

        @pl.when(jnp.logical_and(k > 0, k < last))
        def _():
            acc_h[rows, :] += dh

        @pl.when(k == last)
        def _():
            dh_ref[...] = acc_h[rows, :] + dh

    chunk_rows = pl.BlockSpec((None, tm, FF_BLOCK), lambda k, i: (k, i, 0))
    saved = jax.ShapeDtypeStruct((FF_CHUNKS, T, FF_BLOCK), BF16)
    return _launch(
        body, name=f"bwd_ffn_act{layer}", grid=(FF_CHUNKS, nt),
        in_specs=[pl.BlockSpec((tm, D_MODEL), lambda k, i: (i, 0)), chunk_rows, chunk_rows,
                  pl.BlockSpec((None, 2, FF_BLOCK, D_MODEL), lambda k, i: (k, 0, 0, 0)),
                  pl.BlockSpec((FF_BLOCK, D_MODEL), lambda k, i: (k, 0))],
        out_specs=[pl.BlockSpec((tm, D_MODEL), lambda k, i: (jnp.where(k == last, i, 0), 0)),
                   chunk_rows, chunk_rows, chunk_rows],
        out_shape=[jax.ShapeDtypeStruct((T, D_MODEL), F32), saved, saved, saved],
        scratch_shapes=[pltpu.VMEM((T, D_MODEL), F32)],
        args=(df, gs, us, wgu, wd), vmem=VMEM_BIG, job=job)


def _bwd_ffn_dw(layer, q, h2, df, dg, du, a, job=None):
    T = h2.shape[0]

    def body(h_ref, df_ref, dg_ref, du_ref, a_ref, dgu_ref, dwd_ref):
        h = h_ref[...]
        dgu_ref[0] = _dot_tn(dg_ref[...], h).astype(BF16)
        dgu_ref[1] = _dot_tn(du_ref[...], h).astype(BF16)
        dwd_ref[...] = _dot_tn(a_ref[...], df_ref[...]).astype(BF16)

    cols = pl.BlockSpec((T, FF_PART), lambda k: (0, q))
    chunk = pl.BlockSpec((None, T, FF_BLOCK), lambda k: (k, 0, 0))
    return _launch(
        body, name=f"bwd_ffn_dw{layer}_{q}", grid=(FF_CHUNKS,),
        in_specs=[cols, cols, chunk, chunk, chunk],
        out_specs=[pl.BlockSpec((None, 2, FF_BLOCK, FF_PART), lambda k: (k, 0, 0, 0)),
                   pl.BlockSpec((FF_BLOCK, FF_PART), lambda k: (k, 0))],
        out_shape=[jax.ShapeDtypeStruct((FF_CHUNKS, 2, FF_BLOCK, FF_PART), BF16),
                   jax.ShapeDtypeStruct((D_FF, FF_PART), BF16)],
        args=(h2, df, dg, du, a), vmem=VMEM_BIG, job=job)


def _bwd_attn_out(dx2, dh2, x1, y, attn, wo, g_ffn, g_post, job=None):
    T = x1.shape[0]
    nt = T // ROW_TILE

    def body(dx2_ref, dh2_ref, x1_ref, y_ref, a_ref, wo_ref, gffn_ref, gpost_ref,
             dx1_ref, da_ref, dwo_ref, dgf_ref, dgp_ref, acc):
        i = pl.program_id(0)
        first = i == 0
        dxn, dgf = _rms_bwd(x1_ref[...], gffn_ref[...], dh2_ref[...])
        dx1 = dx2_ref[...] + dxn
        dx1_ref[...] = dx1
        dy, dgp = _rms_bwd(y_ref[...], gpost_ref[...], dx1)
        dyb = dy.astype(BF16)
        da_ref[...] = _dot_nt(dyb, wo_ref[...]).astype(BF16)
        _acc(acc, _dot_tn(a_ref[...], dyb), first)
        _acc(dgf_ref, dgf, first)
        _acc(dgp_ref, dgp, first)

        @pl.when(i == nt - 1)
        def _():
            dwo_ref[...] = acc[...].astype(BF16)

    return _launch(
        body, name="bwd_attn_out", grid=(nt,),
        in_specs=[_row_spec(D_MODEL)] * 5 + [_full_spec((D_MODEL, D_MODEL)), _vec_spec(), _vec_spec()],
        out_specs=[_row_spec(D_MODEL), _row_spec(D_MODEL), _full_spec((D_MODEL, D_MODEL)), _vec_spec(), _vec_spec()],
        out_shape=[jax.ShapeDtypeStruct((T, D_MODEL), F32), jax.ShapeDtypeStruct((T, D_MODEL), BF16),
                   jax.ShapeDtypeStruct((D_MODEL, D_MODEL), BF16)] + [jax.ShapeDtypeStruct((1, D_MODEL), F32)] * 2,
        scratch_shapes=[pltpu.VMEM((D_MODEL, D_MODEL), F32)],
        args=(dx2, dh2, x1, y, attn, wo, g_ffn, g_post), job=job)


def _bwd_attention(q, dattn, kpad, vpad, sinks, job=None):
    T = q.shape[0]
    nb = T // ATT_BLOCK

    def body(q_ref, do_ref, k_ref, v_ref, sink_ref, dq_ref, dk_ref, dv_ref, ds_ref, s_scr, dp_scr, p_scr, dsb_scr):
        n = pl.program_id(0)

        @pl.when(n == 0)
        def _():
            dk_ref[...] = jnp.zeros_like(dk_ref)
            dv_ref[...] = jnp.zeros_like(dv_ref)
            ds_ref[...] = jnp.zeros_like(ds_ref)

        start = pl.multiple_of(n * ATT_BLOCK, ATT_BLOCK)
        win = pl.ds(start, 2 * ATT_BLOCK)
        kw = k_ref[win, :]
        vw = v_ref[win, :]
        lane = lax.broadcasted_iota(jnp.int32, (1, ATT_BLOCK), 1)
        dsink = jnp.zeros((1, ATT_BLOCK), F32)
        dqs, dks, dvs = [], [], []
        for kh in range(N_KV_HEADS):
            kk = kw[:, kh * HEAD_DIM:(kh + 1) * HEAD_DIM]
            vv = vw[:, kh * HEAD_DIM:(kh + 1) * HEAD_DIM]
            qs = _stack_heads(q_ref, kh)
            dos = _stack_heads(do_ref, kh)
            s_scr[...] = _dot_nt(qs, kk)
            dp_scr[...] = _dot_nt(dos, vv)
            for g in range(GQA_GROUP):
                h = kh * GQA_GROUP + g
                dsink_h = jnp.zeros((1, 1), F32)
                for row0 in range(0, ATT_BLOCK, ATT_SUB):
                    rows = pl.ds(g * ATT_BLOCK + row0, ATT_SUB)
                    relf, valid = _att_mask(n, row0)
                    pr, ps = _att_probs(s_scr[rows, :], relf, valid, _alibi_slope(h), sink_ref[0, h])
                    dp = dp_scr[rows, :]
                    delta = jnp.sum(pr * dp, axis=-1, keepdims=True)
                    dsb_scr[rows, :] = (pr * (dp - delta) * ATT_SCALE).astype(BF16)
                    p_scr[rows, :] = pr.astype(BF16)
                    dsink_h = dsink_h - jnp.sum(ps * delta, axis=0, keepdims=True)
                dsink = dsink + jnp.where(lane == h, dsink_h, 0.0)
            dsb = dsb_scr[...]
            dqs += _unstack_heads(_dot(dsb, kk))
            dks.append(_dot_tn(dsb, qs))
            dvs.append(_dot_tn(p_scr[...], dos))
        dq_ref[...] = jnp.concatenate(dqs, axis=1).astype(BF16)
        dk_ref[win, :] += jnp.concatenate(dks, axis=1)
        dv_ref[win, :] += jnp.concatenate(dvs, axis=1)
        ds_ref[...] += dsink

    return _launch(
        body, name="bwd_attention", grid=(nb,),
        in_specs=[_row_spec(D_MODEL, ATT_BLOCK), _row_spec(D_MODEL, ATT_BLOCK), _full_spec((T + ATT_BLOCK, KV_DIM)),
                  _full_spec((T + ATT_BLOCK, KV_DIM)), pl.BlockSpec(memory_space=pltpu.SMEM)],
        out_specs=[_row_spec(D_MODEL, ATT_BLOCK), _full_spec((T + ATT_BLOCK, KV_DIM)), _full_spec((T + ATT_BLOCK, KV_DIM)),
                   _full_spec((1, ATT_BLOCK))],
        out_shape=[jax.ShapeDtypeStruct((T, D_MODEL), BF16), jax.ShapeDtypeStruct((T + ATT_BLOCK, KV_DIM), F32),
                   jax.ShapeDtypeStruct((T + ATT_BLOCK, KV_DIM), F32), jax.ShapeDtypeStruct((1, ATT_BLOCK), F32)],
        scratch_shapes=[pltpu.VMEM((ATT_GROUP_ROWS, 2 * ATT_BLOCK), F32)] * 2
                       + [pltpu.VMEM((ATT_GROUP_ROWS, 2 * ATT_BLOCK), BF16)] * 2,
        args=(q, dattn, kpad, vpad, sinks), vmem=VMEM_BIG, job=job)


def _bwd_qkv(dxres, dq, dkv, x3, h1, hk, wq, wkv, g_mix, g_kv, job=None):
    T = x3.shape[0]
    nt = T // ROW_TILE

    def body(dxr_ref, dq_ref, dkv_ref, x_ref, h1_ref, hk_ref, wq_ref, wkv_ref, gmix_ref, gkv_ref,
             dx_ref, dwq_ref, dwkv_ref, dgm_ref, dgk_ref, acc_q, acc_kv):
        i = pl.program_id(0)
        first = i == 0
        dqv = dq_ref[...]
        dkvv = dkv_ref[...]
        xv = x_ref[...]
        d1, dgm = _rms_bwd(xv, gmix_ref[...], _dot_nt(dqv, wq_ref[...]))
        d2, dgk = _rms_bwd(xv, gkv_ref[...], _dot_nt(dkvv, wkv_ref[...]))
        dx_ref[...] = dxr_ref[...] + d1 + d2
        _acc(acc_q, _dot_tn(h1_ref[...], dqv), first)
        _acc(acc_kv, _dot_tn(hk_ref[...], dkvv), first)
        _acc(dgm_ref, dgm, first)
        _acc(dgk_ref, dgk, first)

        @pl.when(i == nt - 1)
        def _():
            dwq_ref[...] = acc_q[...].astype(BF16)
            dwkv_ref[...] = acc_kv[...].astype(BF16)

    return _launch(
        body, name="bwd_qkv", grid=(nt,),
        in_specs=[_row_spec(D_MODEL), _row_spec(D_MODEL), _row_spec(2 * KV_DIM), _row_spec(D_MODEL), _row_spec(D_MODEL),
                  _row_spec(D_MODEL), _full_spec((D_MODEL, D_MODEL)), _full_spec((D_MODEL, 2 * KV_DIM)), _vec_spec(),
                  _vec_spec()],
        out_specs=[_row_spec(D_MODEL), _full_spec((D_MODEL, D_MODEL)), _full_spec((D_MODEL, 2 * KV_DIM)), _vec_spec(),
                   _vec_spec()],
        out_shape=[jax.ShapeDtypeStruct((T, D_MODEL), F32), jax.ShapeDtypeStruct((D_MODEL, D_MODEL), BF16),
                   jax.ShapeDtypeStruct((D_MODEL, 2 * KV_DIM), BF16)] + [jax.ShapeDtypeStruct((1, D_MODEL), F32)] * 2,
        scratch_shapes=[pltpu.VMEM((D_MODEL, D_MODEL), F32), pltpu.VMEM((D_MODEL, 2 * KV_DIM), F32)],
        args=(dxres, dq, dkv, x3, h1, hk, wq, wkv, g_mix, g_kv), job=job)


def _bwd_pool_mixer(dx2, dh2, x1, x, yraw, d, wp, scale, g_ffn, g_post, g_pre, job=None):
    T = x.shape[0]
    tm = ROW_TILE
    nt = T // tm

    def body(dx2_ref, dh2_ref, x1_ref, x_ref, yraw_ref, d_ref, wp_ref, sc_ref, gffn_ref, gpost_ref, gpre_ref,
             dx_ref, dwp_ref, dsc_ref, dgf_ref, dgp_ref, dgm_ref, carry, acc):
        i = pl.program_id(0)
        first = i == 0
        tile = nt - 1 - i

        @pl.when(first)
        def _():
            carry[...] = jnp.zeros_like(carry)

        dxn, dgf = _rms_bwd(x1_ref[...], gffn_ref[...], dh2_ref[...])
        dx1 = dx2_ref[...] + dxn
        yraw = yraw_ref[...]
        sc = sc_ref[...]
        dy, dgp = _rms_bwd(yraw * sc, gpost_ref[...], dx1)
        dsc = jnp.sum(dy * yraw, axis=0, keepdims=True)
        dyb = (dy * sc).astype(BF16)
        dv = d_ref[...]
        dds = []
        for g in range(N_POOL_GROUPS):
            cols = slice(g * POOL_GROUP, (g + 1) * POOL_GROUP)
            dds.append(_dot_nt(dyb[:, cols], wp_ref[g]))
            _acc(acc.at[g], _dot_tn(dv[:, cols], dyb[:, cols]), first)
        dd = jnp.concatenate(dds, axis=1)
        e = dd / _pool_counts(tile * tm, tm)
        ext = jnp.concatenate([e, carry[...]], axis=0)
        carry[...] = e[:POOL_HALO, :]
        sums = _window_sums(ext, lambda k: tm + POOL_HALO - k)[:tm, :]
        dxm, dgm = _rms_bwd(x_ref[...], gpre_ref[...], sums - dd)
        dx_ref[...] = dx1 + dxm
        _acc(dsc_ref, dsc, first)
        _acc(dgf_ref, dgf, first)
        _acc(dgp_ref, dgp, first)
        _acc(dgm_ref, dgm, first)

        @pl.when(i == nt - 1)
        def _():
            dwp_ref[...] = acc[...].astype(BF16)

    rev = pl.BlockSpec((tm, D_MODEL), lambda i: (nt - 1 - i, 0))
    return _launch(
        body, name="bwd_pool_mixer", grid=(nt,),
        in_specs=[rev] * 6 + [_full_spec((N_POOL_GROUPS, POOL_GROUP, POOL_GROUP))] + [_vec_spec()] * 4,
        out_specs=[rev, _full_spec((N_POOL_GROUPS, POOL_GROUP, POOL_GROUP))] + [_vec_spec()] * 4,
        out_shape=[jax.ShapeDtypeStruct((T, D_MODEL), F32),
                   jax.ShapeDtypeStruct((N_POOL_GROUPS, POOL_GROUP, POOL_GROUP), BF16)]
                  + [jax.ShapeDtypeStruct((1, D_MODEL), F32)] * 4,
        scratch_shapes=[pltpu.VMEM((POOL_HALO, D_MODEL), F32), pltpu.VMEM((N_POOL_GROUPS, POOL_GROUP, POOL_GROUP), F32)],
        args=(dx2, dh2, x1, x, yraw, d, wp, scale, g_ffn, g_post, g_pre), job=job)


def _my_place():
    return lax.axis_index("x"), lax.axis_index("y"), lax.axis_index("c")


def _dev_index(px, py, pc):
    return 4 * px + 2 * py + pc


def _peer_by_relation(r):
    x, y, c = _my_place()
    return (x ^ ((r >> 2) & 1), y ^ ((r >> 1) & 1), c ^ (r & 1))


def _slot_pool(ref, j):
    return ref.at[:, pl.ds(pl.multiple_of(j * 32, 32), 32), :]


def _slot_scale(ref, j):
    return ref.at[:, pl.ds(pl.multiple_of(j * 128, 128), 128)]


def _slot_rows128(ref, j):
    return ref.at[pl.ds(pl.multiple_of(j * 128, 128), 128), :]


def _slot_gu(ref, j):
    return ref.at[j % FF_CHUNKS, j // FF_CHUNKS]


def _slot_wd(ref, j):
    return ref.at[pl.ds(pl.multiple_of(j * WD_ROWS, 16), WD_ROWS), :]


def _slot_cols128(ref, j):
    return ref.at[:, pl.ds(pl.multiple_of(j * 128, 128), 128)]


_GATHERED = {
    "pool": ((N_POOL_GROUPS, POOL_GROUP, POOL_GROUP), BF16, _slot_pool),
    "scale": ((1, D_MODEL), F32, _slot_scale),
    "kv": ((D_MODEL, 2 * KV_DIM), BF16, _slot_rows128),
    "q": ((D_MODEL, D_MODEL), BF16, _slot_rows128),
    "o": ((D_MODEL, D_MODEL), BF16, _slot_rows128),
    "gu": ((FF_CHUNKS, 2, FF_BLOCK, D_MODEL), BF16, _slot_gu),
    "wd": ((D_FF, D_MODEL), BF16, _slot_wd),
    "gate": ((D_MODEL, D_MODEL), BF16, _slot_rows128),
    "proj": ((PLE_DIM, D_MODEL), BF16, _slot_cols128),
}


def _no_compute():
    pass


class _AllGather:
    def __init__(self, names, shards):
        self.kinds = [_GATHERED[n.rstrip("01")] for n in names]
        self.args = [shards[n] for n in names]
        self.out_shape = [jax.ShapeDtypeStruct(shape, dtype) for shape, dtype, _ in self.kinds]
        n = len(names)
        self.scratch = [pltpu.SemaphoreType.DMA((n, 7)), pltpu.SemaphoreType.DMA((n, 7)), pltpu.SemaphoreType.DMA((n,))]

    def _copies(self, srcs, outs, sems):
        send_sems, recv_sems, local_sems = sems
        x, y, c = _my_place()
        me, sibling = (x, y, c), (x, y, 1 - c)
        chips = [(1 - x, y), (x, 1 - y), (1 - x, 1 - y)]
        n = len(srcs)

        def slot(t, dev):
            return self.kinds[t][2](outs[t], _dev_index(*dev))

        def copy(t, k, block, to, src=None):
            return pltpu.make_async_remote_copy(
                src_ref=slot(t, block) if src is None else src, dst_ref=slot(t, block),
                send_sem=send_sems.at[t, k], recv_sem=recv_sems.at[t, k], device_id=to, device_id_type=MESH)

        mine = [pltpu.make_async_copy(srcs[t], slot(t, me), local_sems.at[t]) for t in range(n)]
        first = []
        for t in range(n):
            first.append(copy(t, 0, me, sibling, src=srcs[t]))
            first += [copy(t, 1 + j, me, (*chip, c), src=srcs[t]) for j, chip in enumerate(chips)]
        return me, sibling, chips, copy, mine, first

    def start(self, srcs, outs, sems):
        _, _, _, _, mine, first = self._copies(srcs, outs, sems)
        for cp in mine + first:
            cp.start()

    def finish(self, srcs, outs, sems):
        me, sibling, chips, copy, mine, first = self._copies(srcs, outs, sems)
        c = me[2]
        n = len(srcs)
        passed = []
        for j, chip in enumerate(chips):
            for t in range(n):
                copy(t, 1 + j, (*chip, c), me).wait_recv()
                fwd = copy(t, 4 + j, (*chip, c), sibling)
                fwd.start()
                passed.append(fwd)
        for t in range(n):
            copy(t, 0, sibling, me).wait_recv()
            for j, chip in enumerate(chips):
                copy(t, 4 + j, (*chip, 1 - c), me).wait_recv()
        for cp in first + passed:
            cp.wait_send()
        for cp in mine:
            cp.wait()


def _all_gather_only(name, names, shards):
    return _launch(_no_compute, name=name, grid=(), in_specs=[], out_specs=[], out_shape=[], args=(),
                   job=_AllGather(names, shards))[1]


def _block_pool(ref, j):
    return ref.at[:, pl.ds(pl.multiple_of(j * 32, 32), 32), :]


def _block_rows128(ref, j):
    return ref.at[pl.ds(pl.multiple_of(j * 128, 128), 128), :]


def _block_gu(ref, j):
    return ref.at[j % FF_CHUNKS, j // FF_CHUNKS]


def _block_wd(ref, j):
    return ref.at[pl.ds(pl.multiple_of(j * WD_ROWS, 16), WD_ROWS), :]


def _block_cols128(ref, j):
    return ref.at[:, pl.ds(pl.multiple_of(j * 128, 128), 128)]


_SCATTERED = {
    "pool": ((N_POOL_GROUPS, 32, POOL_GROUP), _block_pool),
    "kv": ((128, 2 * KV_DIM), _block_rows128),
    "q": ((128, D_MODEL), _block_rows128),
    "o": ((128, D_MODEL), _block_rows128),
    "gu": ((FF_BLOCK, FF_PART), _block_gu),
    "wd": ((WD_ROWS, FF_PART), _block_wd),
    "gate": ((128, D_MODEL), _block_rows128),
    "proj": ((PLE_DIM, 128), _block_cols128),
}


class _SiblingSwap:
    def __init__(self, pieces):
        self.kinds = [_SCATTERED[kind] for kind, _ in pieces]
        self.args = [g for _, g in pieces]
        self.out_shape = [jax.ShapeDtypeStruct((N_CHIPS, *block), BF16) for block, _ in self.kinds]
        n = len(pieces)
        self.scratch = [pltpu.SemaphoreType.DMA((n, N_CHIPS)), pltpu.SemaphoreType.DMA((n, N_CHIPS))]

    def _copies(self, srcs, outs, sems):
        send_sems, recv_sems = sems
        x, y, c = _my_place()
        return [pltpu.make_async_remote_copy(
            src_ref=block(srcs[t], 2 * ch + 1 - c), dst_ref=outs[t].at[ch], send_sem=send_sems.at[t, ch],
            recv_sem=recv_sems.at[t, ch], device_id=(x, y, 1 - c), device_id_type=MESH)
            for t, (_, block) in enumerate(self.kinds) for ch in range(N_CHIPS)]

    def start(self, srcs, outs, sems):
        for cp in self._copies(srcs, outs, sems):
            cp.start()

    def finish(self, srcs, outs, sems):
        for cp in self._copies(srcs, outs, sems):
            cp.wait()


class _ChipScatter:
    def __init__(self, pieces):
        self.kinds = [_SCATTERED[kind] for kind, _, _ in pieces]
        self.n = n = len(pieces)
        self.args = [g for _, g, _ in pieces] + [s for _, _, s in pieces]
        self.out_shape = [jax.ShapeDtypeStruct((N_CHIPS, *block), BF16) for block, _ in self.kinds]
        self.scratch = []
        for block, _ in self.kinds:
            self.scratch += [pltpu.VMEM((N_CHIPS, *block), BF16)] * 3
        self.scratch += [pltpu.SemaphoreType.DMA((n, N_CHIPS + 1)), pltpu.SemaphoreType.DMA((n, N_CHIPS - 1)),
                         pltpu.SemaphoreType.DMA((n, N_CHIPS - 1)), pltpu.SemaphoreType.DMA((n,))]

    def _sends(self, outs, scr):
        n = self.n
        send_sems, recv_sems, local_sems = scr[3 * n + 1:]
        x, y, c = _my_place()
        chip = 2 * x + y
        copies = []
        for t in range(n):
            total = scr[3 * t + 2]
            copies.append(pltpu.make_async_copy(total.at[chip], outs[t].at[chip], local_sems.at[t]))
            for r in range(1, N_CHIPS):
                to = chip ^ r
                copies.append(pltpu.make_async_remote_copy(
                    src_ref=total.at[to], dst_ref=outs[t].at[chip], send_sem=send_sems.at[t, r - 1],
                    recv_sem=recv_sems.at[t, r - 1], device_id=(to // 2, to % 2, c), device_id_type=MESH))
        return copies

    def start(self, ins, outs, scr):
        n = self.n
        load_sems = scr[3 * n]
        c = lax.axis_index("c")
        loads = []
        for t, (_, block) in enumerate(self.kinds):
            mine, theirs = scr[3 * t], scr[3 * t + 1]
            loads += [pltpu.make_async_copy(block(ins[t], 2 * ch + c), mine.at[ch], load_sems.at[t, ch])
                      for ch in range(N_CHIPS)]
            loads.append(pltpu.make_async_copy(ins[n + t], theirs, load_sems.at[t, N_CHIPS]))
        for cp in loads:
            cp.start()
        for cp in loads:
            cp.wait()
        for t in range(n):
            mine, theirs, total = scr[3 * t:3 * t + 3]
            for ch in range(N_CHIPS):
                total[ch] = (mine[ch].astype(F32) + theirs[ch].astype(F32)).astype(BF16)
        for cp in self._sends(outs, scr):
            cp.start()

    def finish(self, ins, outs, scr):
        for cp in self._sends(outs, scr):
            cp.wait()


class _Jobs:
    def __init__(self, *jobs):
        self.jobs = jobs
        self.args = [a for j in jobs for a in j.args]
        self.out_shape = [o for j in jobs for o in j.out_shape]
        self.scratch = [s for j in jobs for s in j.scratch]

    def _split(self, refs, attr):
        at = 0
        for j in self.jobs:
            n = len(getattr(j, attr))
            yield refs[at:at + n]
            at += n

    def _each(self, ins, outs, scr):
        return zip(self.jobs, self._split(ins, "args"), self._split(outs, "out_shape"), self._split(scr, "scratch"))

    def start(self, ins, outs, scr):
        for j, i, o, s in self._each(ins, outs, scr):
            j.start(i, o, s)

    def finish(self, ins, outs, scr):
        for j, i, o, s in self._each(ins, outs, scr):
            j.finish(i, o, s)

    def split_outputs(self, outs):
        return list(self._split(outs, "out_shape"))


def _adamw_math(w, g, m, v):
    m = ADAM_B1 * m + (1.0 - ADAM_B1) * g
    v = ADAM_B2 * v + (1.0 - ADAM_B2) * (g * g)
    m_hat = m / (1.0 - ADAM_B1 ** ADAM_STEP)
    v_hat = v / (1.0 - ADAM_B2 ** ADAM_STEP)
    delta = -ADAM_LR * (m_hat / (jnp.sqrt(v_hat) + ADAM_EPS) + ADAM_WD * w)
    return delta, m, v


def _adamw(name, w, m, v, landings, n_col_blocks=1, job=None):
    _, r, c = landings[0].shape
    grid = (w.shape[0] // r, n_col_blocks)

    def body(w_ref, m_ref, v_ref, *rest):
        l_refs, (g_ref, d_ref, nm_ref, nv_ref) = rest[:len(landings)], rest[len(landings):]
        step = pl.program_id(0) * n_col_blocks + pl.program_id(1)
        for idx, l_ref in enumerate(l_refs):
            @pl.when(step == idx)
            def _(l_ref=l_ref):
                g = l_ref[0].astype(F32)
                for s in range(1, N_CHIPS):
                    g = g + l_ref[s].astype(F32)
                g_ref[...] = g
                d_ref[...], nm_ref[...], nv_ref[...] = _adamw_math(w_ref[...], g, m_ref[...], v_ref[...])

    spec = pl.BlockSpec((r, c), lambda a, b: (a, b))
    return _launch(
        body, name=f"adamw_{name}", grid=grid,
        in_specs=[spec, spec, spec] + [_full_spec((N_CHIPS, r, c))] * len(landings),
        out_specs=[spec] * 4, out_shape=[jax.ShapeDtypeStruct(w.shape, F32)] * 4,
        args=(w, m, v, *landings), vmem=VMEM_BIG, job=job)


_SMALL = (("pre_mix_g", SV_PRE_MIX, 2), ("post_mix_g", SV_POST_MIX, 2), ("pre_ffn_g", SV_PRE_FFN, 2),
          ("post_ffn_g", SV_POST_FFN, 2), ("ple_g", SV_PLE, 2), ("ple_post_g", SV_PLE_POST, 2), ("kv_g", SV_KV, 1),
          ("pool_scale", SV_POOL_SCALE, 1), ("sinks", SV_SINKS, 1))


def _small_all_reduce_adamw(part, params):
    flat = [a for name, _, _ in _SMALL for a in params[name]]
    n_in = 1 + len(flat)

    def body(*refs):
        part_ref, wmv = refs[0], refs[1:n_in]
        loss_ref, outs = refs[n_in], refs[n_in + 1:n_in + 1 + 4 * len(_SMALL)]
        buf, total, send_sems, recv_sems = refs[n_in + 1 + 4 * len(_SMALL):]
        x, y, c = _my_place()
        me = _dev_index(x, y, c)
        buf[me] = part_ref[...]
        copies = [pltpu.make_async_remote_copy(
            src_ref=part_ref, dst_ref=buf.at[me], send_sem=send_sems.at[r - 1], recv_sem=recv_sems.at[r - 1],
            device_id=_peer_by_relation(r), device_id_type=MESH) for r in range(1, N_DEV)]
        for cp in copies:
            cp.start()
        for cp in copies:
            cp.wait()
        g = buf[0]
        for s in range(1, N_DEV):
            g = g + buf[s]
        total[...] = g
        loss_ref[...] = total[SV_LOSS:SV_LOSS + 1, 0:1]
        for idx, (name, row, n_rows) in enumerate(_SMALL):
            w_ref, m_ref, v_ref = wmv[3 * idx:3 * idx + 3]
            g_ref, d_ref, nm_ref, nv_ref = outs[4 * idx:4 * idx + 4]
            if name == "pool_scale":
                g = total[row:row + 1, pl.ds(pl.multiple_of(me * 128, 128), 128)]
            else:
                g = total[row:row + n_rows, 0:w_ref.shape[1]]
            g_ref[...] = g
            d_ref[...], nm_ref[...], nv_ref[...] = _adamw_math(w_ref[...], g, m_ref[...], v_ref[...])

    vm = pl.BlockSpec(memory_space=pltpu.VMEM)
    out_shape = [jax.ShapeDtypeStruct((1, 1), F32)]
    for name, _, _ in _SMALL:
        out_shape += [jax.ShapeDtypeStruct(params[name][0].shape, F32)] * 4
    res = pl.pallas_call(
        body, name="small_all_reduce_adamw", out_shape=out_shape,
        in_specs=[vm] * n_in, out_specs=[vm] * len(out_shape),
        scratch_shapes=[pltpu.VMEM((N_DEV, SV_ROWS, D_MODEL), F32), pltpu.VMEM((SV_ROWS, D_MODEL), F32),
                        pltpu.SemaphoreType.DMA((N_DEV - 1,)), pltpu.SemaphoreType.DMA((N_DEV - 1,))],
    )(part, *flat)
    return res[0], {name: res[1 + 4 * idx:5 + 4 * idx] for idx, (name, _, _) in enumerate(_SMALL)}


def _local_step(x, p, tgt, gains, sinks, shards, weights):
    row = _Gain
    gather = lambda *names: _AllGather(names, shards)
    g_pre_mix, g_post_mix = gains["pre_mix_g"], gains["post_mix_g"]
    g_pre_ffn, g_post_ffn = gains["pre_ffn_g"], gains["post_ffn_g"]
    g_ple, g_ple_post, g_kv = gains["ple_g"], gains["ple_post_g"], _Gain(gains["kv_g"], 0)

    wp, scale, wgu0 = _all_gather_only("gather_first", ("pool", "scale", "gu0"), shards)
    (x1_0, h2_0, yraw, dpool), (wd0,) = _fwd_pool_mixer(
        x, row(g_pre_mix, 0), wp, scale, row(g_post_mix, 0), row(g_pre_ffn, 0), job=gather("wd0"))
    (gs0, us0, f0, x2_0, h3_0), (wgate0, wproj0, wgu1) = _fwd_ffn(
        0, h2_0, x1_0, wgu0, wd0, row(g_post_ffn, 0), row(g_ple, 0), job=gather("gate0", "proj0", "gu1"))
    (x3_0, z0, pe0), (wkv, wq) = _fwd_ple(0, x2_0, h3_0, p[0], wgate0, wproj0, row(g_ple_post, 0),
                                          job=gather("kv", "q"))
    (hk, h1, q, kv), (wo,) = _fwd_qkv(x3_0, g_kv, row(g_pre_mix, 1), wkv, wq, job=gather("o"))
    front = ((ATT_BLOCK, 0), (0, 0))
    kpad = jnp.pad(kv[:, :KV_DIM], front)
    vpad = jnp.pad(kv[:, KV_DIM:], front)
    (attn,), (wd1,) = _fwd_attention(q, kpad, vpad, sinks, job=gather("wd1"))
    (y1, x1_1, h2_1), _ = _fwd_attn_out(attn, x3_0, wo, row(g_post_mix, 1), row(g_pre_ffn, 1))
    (gs1, us1, f1, x2_1, h3_1), (wgate1, wproj1) = _fwd_ffn(
        1, h2_1, x1_1, wgu1, wd1, row(g_post_ffn, 1), row(g_ple, 1), job=gather("gate1", "proj1"))
    (dx3_1, z1, pe1, loss), _ = _fwd_ple(1, x2_1, h3_1, p[1], wgate1, wproj1, row(g_ple_post, 1), target=tgt)

    produced, swapped, landed = {}, {}, {}

    def kind_of(name):
        return name.rstrip("0123_")

    def carry(swap=(), spread=()):
        jobs = []
        if swap:
            jobs.append(_SiblingSwap([(kind_of(n), produced[n]) for n in swap]))
        if spread:
            jobs.append(_ChipScatter([(kind_of(n), produced[n], swapped[n]) for n in spread]))
        return _Jobs(*jobs)

    def carried(jobs, outs, swap=(), spread=()):
        parts = jobs.split_outputs(outs)
        if swap:
            swapped.update(zip(swap, parts[0]))
        if spread:
            landed.update(zip(spread, parts[-1]))

    def hosted(call, *args, swap=(), spread=()):
        jobs = carry(swap, spread)
        outs, job_outs = call(*args, job=jobs)
        carried(jobs, job_outs, swap, spread)
        return outs

    def ffn_weight_grads(layer, h2, df, dg, du, a, hosts):
        for qtr in range(FF_PARTS):
            dgu, dwd = hosted(_bwd_ffn_dw, layer, qtr, h2, df, dg, du, a, **hosts[qtr])
            produced[f"gu{layer}_{qtr}"], produced[f"wd{layer}_{qtr}"] = dgu, dwd

    ffn_q = lambda layer, qtr: (f"gu{layer}_{qtr}", f"wd{layer}_{qtr}")

    dx2_1, df1, produced["gate1"], produced["proj1"], dg_ple_post1, dg_ple1, dg_post_ffn1 = hosted(
        _bwd_ple, 1, dx3_1, x2_1, z1, pe1, h3_1, p[1], f1, wgate1, row(g_ple_post, 1), row(g_ple, 1),
        row(g_post_ffn, 1))
    dh2_1, dg1, du1, a1 = hosted(_bwd_ffn_act, 1, df1, gs1, us1, wgu1, wd1, swap=("gate1", "proj1"))
    ffn_weight_grads(1, h2_1, df1, dg1, du1, a1, [dict(spread=("gate1", "proj1")), dict(swap=ffn_q(1, 0))])
    dx1_1, dattn, produced["o"], dg_pre_ffn1, dg_post_mix1 = hosted(
        _bwd_attn_out, dx2_1, dh2_1, x1_1, y1, attn, wo, row(g_pre_ffn, 1), row(g_post_mix, 1), swap=ffn_q(1, 1))
    dq, dkpad, dvpad, dsinks = hosted(_bwd_attention, q, dattn, kpad, vpad, sinks, spread=ffn_q(1, 0))
    dkv = jnp.concatenate([dkpad[ATT_BLOCK:], dvpad[ATT_BLOCK:]], axis=1).astype(BF16)
    dx3_0, produced["q"], produced["kv"], dg_pre_mix1, dg_kv = hosted(
        _bwd_qkv, dx1_1, dq, dkv, x3_0, h1, hk, wq, wkv, row(g_pre_mix, 1), g_kv, swap=("o",))
    dx2_0, df0, produced["gate0"], produced["proj0"], dg_ple_post0, dg_ple0, dg_post_ffn0 = hosted(
        _bwd_ple, 0, dx3_0, x2_0, z0, pe0, h3_0, p[0], f0, wgate0, row(g_ple_post, 0), row(g_ple, 0),
        row(g_post_ffn, 0), swap=("q", "kv"), spread=("gu1_1",))
    dh2_0, dg0, du0, a0 = hosted(_bwd_ffn_act, 0, df0, gs0, us0, wgu0, wd0,
                                 swap=("gate0", "proj0"), spread=("wd1_1", "o"))
    ffn_weight_grads(0, h2_0, df0, dg0, du0, a0, [
        dict(spread=("gate0", "proj0", "q", "kv")), dict(swap=ffn_q(0, 0))])
    grad_x, produced["pool"], dscale, dg_pre_ffn0, dg_post_mix0, dg_pre_mix0 = hosted(
        _bwd_pool_mixer, dx2_0, dh2_0, x1_0, x, yraw, dpool, wp, scale, row(g_pre_ffn, 0), row(g_post_mix, 0),
        row(g_pre_mix, 0), swap=ffn_q(0, 1), spread=ffn_q(0, 0))

    def update(name, n_col_blocks=1, pieces=None, swap=(), spread=()):
        w, m, v = weights[name]
        rows = w.size // w.shape[-1]
        flat = [landed[n].reshape(N_CHIPS, -1, landed[n].shape[-1]) for n in (pieces or [kind_short[name]])]
        outs = hosted(_adamw, name, w.reshape(rows, -1), m.reshape(rows, -1), v.reshape(rows, -1), flat,
                      n_col_blocks, swap=swap, spread=spread)
        return [o.reshape(w.shape) for o in outs]

    kind_short = {"w_q": "q", "w_kv": "kv", "w_o": "o", "pool_w": "pool"}
    upd = {}
    upd["w_ple_gate"] = update("w_ple_gate", pieces=("gate0", "gate1"), swap=("pool",), spread=ffn_q(0, 1))
    upd["w_ple_proj"] = update("w_ple_proj", pieces=("proj0", "proj1"), spread=("pool",))
    for name in ("w_q", "w_kv", "w_o", "pool_w"):
        upd[name] = update(name)
    upd["w_gu"] = update("w_gu", FF_PARTS,
                         pieces=[f"gu{layer}_{qtr}" for layer in range(2) for qtr in range(FF_PARTS)])
    upd["w_gu"] = [jnp.swapaxes(a, 1, 2) for a in upd["w_gu"]]
    upd["w_down"] = update("w_down", FF_PARTS,
                           pieces=[f"wd{layer}_{qtr}" for layer in range(2) for qtr in range(FF_PARTS)])

    lanes = lambda a: jnp.pad(a, ((0, 0), (0, D_MODEL - a.shape[1])))
    small = jnp.concatenate([
        dg_pre_mix0, dg_pre_mix1, dg_post_mix0, dg_post_mix1, dg_pre_ffn0, dg_pre_ffn1, dg_post_ffn0, dg_post_ffn1,
        dg_ple0, dg_ple1, dg_ple_post0, dg_ple_post1, dg_kv, dscale, lanes(dsinks[:, :N_HEADS]), lanes(loss)], axis=0)
    return grad_x, upd, small


def kernel(x, p, pre_mix_g, post_mix_g, pre_ffn_g, post_ffn_g, pool_w, pool_scale, kv_g, w_kv, w_q, sinks, w_o, w_gu, w_down, ple_g, w_ple_gate, w_ple_proj, ple_post_g, loss_target, m_pre_mix_g, m_post_mix_g, m_pre_ffn_g, m_post_ffn_g, m_pool_w, m_pool_scale, m_kv_g, m_w_kv, m_w_q, m_sinks, m_w_o, m_w_gu, m_w_down, m_ple_g, m_w_ple_gate, m_w_ple_proj, m_ple_post_g, v_pre_mix_g, v_post_mix_g, v_pre_ffn_g, v_post_ffn_g, v_pool_w, v_pool_scale, v_kv_g, v_w_kv, v_w_q, v_sinks, v_w_o, v_w_gu, v_w_down, v_ple_g, v_w_ple_gate, v_w_ple_proj, v_ple_post_g):
    shards = {"pool": pool_w[0].astype(BF16), "scale": pool_scale, "kv": w_kv.astype(BF16),
              "q": w_q[0].astype(BF16), "o": w_o[0].astype(BF16)}
    for layer in range(2):
        shards[f"gu{layer}"] = w_gu[layer].T.astype(BF16)
        shards[f"wd{layer}"] = w_down[layer].astype(BF16)
        shards[f"gate{layer}"] = w_ple_gate[layer].astype(BF16)
        shards[f"proj{layer}"] = w_ple_proj[layer].astype(BF16)
    stacked = lambda g: g.reshape(-1, 1, D_MODEL)
    gains = dict(pre_mix_g=stacked(pre_mix_g), post_mix_g=stacked(post_mix_g), pre_ffn_g=stacked(pre_ffn_g),
                 post_ffn_g=stacked(post_ffn_g), ple_g=stacked(ple_g), ple_post_g=stacked(ple_post_g),
                 kv_g=stacked(kv_g))
    weights = {"pool_w": (pool_w, m_pool_w, v_pool_w), "w_kv": (w_kv, m_w_kv, v_w_kv), "w_q": (w_q, m_w_q, v_w_q),
               "w_o": (w_o, m_w_o, v_w_o), "w_down": (w_down, m_w_down, v_w_down),
               "w_gu": tuple(jnp.swapaxes(a, 1, 2) for a in (w_gu, m_w_gu, v_w_gu)),
               "w_ple_gate": (w_ple_gate, m_w_ple_gate, v_w_ple_gate),
               "w_ple_proj": (w_ple_proj, m_w_ple_proj, v_w_ple_proj)}
    grad_x, upd, small = _local_step(x[0], p[:, 0], loss_target[0], gains, sinks, shards, weights)

    small_params = {
        "pre_mix_g": (pre_mix_g, m_pre_mix_g, v_pre_mix_g), "post_mix_g": (post_mix_g, m_post_mix_g, v_post_mix_g),
        "pre_ffn_g": (pre_ffn_g, m_pre_ffn_g, v_pre_ffn_g), "post_ffn_g": (post_ffn_g, m_post_ffn_g, v_post_ffn_g),
        "ple_g": (ple_g, m_ple_g, v_ple_g), "ple_post_g": (ple_post_g, m_ple_post_g, v_ple_post_g),
        "kv_g": (kv_g[None, :], m_kv_g[None, :], v_kv_g[None, :]),
        "pool_scale": (pool_scale, m_pool_scale, v_pool_scale), "sinks": (sinks, m_sinks, v_sinks)}
    loss, small_upd = _small_all_reduce_adamw(small, small_params)
    small_upd["kv_g"] = [a[0] for a in small_upd["kv_g"]]
    upd.update(small_upd)

    names = ["pre_mix_g", "post_mix_g", "pre_ffn_g", "post_ffn_g", "pool_w", "pool_scale", "kv_g", "w_kv", "w_q",
             "sinks", "w_o", "w_gu", "w_down", "ple_g", "w_ple_gate", "w_ple_proj", "ple_post_g"]
    outs = [loss[0, 0], grad_x[None]]
    for kind in range(4):
        outs += [upd[n][kind] for n in names]
    return tuple(outs)
```

```python
import functools

import jax
import jax.numpy as jnp
from jax import lax
from jax.experimental import pallas as pl
from jax.experimental.pallas import tpu as pltpu

F32 = jnp.float32
BF16 = jnp.bfloat16

N_DEV = 8
D_MODEL = 1024
N_POOL_GROUPS = 4
POOL_GROUP = 256
POOL_HALO = 16
HEAD_DIM = 64
N_HEADS = 16
N_KV_HEADS = 4
GQA_GROUP = 4
KV_DIM = N_KV_HEADS * HEAD_DIM
ATT_BLOCK = 128
D_FF = 2816
FF_CHUNKS = 4
FF_BLOCK = D_FF // FF_CHUNKS
WD_ROWS = D_FF // N_DEV
FF_PARTS = 2
FF_PART = D_MODEL // FF_PARTS
N_CHIPS = 4
PLE_DIM = 256
EPS = 1e-6
NEG_INF = -1e30
ATT_SCALE = HEAD_DIM ** -0.5

ADAM_LR = 0.001
ADAM_B1 = 0.9
ADAM_B2 = 0.999
ADAM_EPS = 1e-08
ADAM_WD = 0.01
ADAM_STEP = 10

ROW_TILE = 512
FFN_ROW_TILE = 512
FFN_SUB_TILES = 2
VMEM_BIG = 56 * 1024 * 1024
VMEM_MID = 48 * 1024 * 1024
HBM_PIN_ELEMS = 1024

SV_ROWS = 16
SV_PRE_MIX, SV_POST_MIX, SV_PRE_FFN, SV_POST_FFN, SV_PLE, SV_PLE_POST = 0, 2, 4, 6, 8, 10
SV_KV, SV_POOL_SCALE, SV_SINKS, SV_LOSS = 12, 13, 14, 15

MESH = pl.DeviceIdType.MESH
ANY = pl.BlockSpec(memory_space=pl.ANY)


def _dot(a, b):
    return jnp.dot(a, b, preferred_element_type=F32)


def _dot_nt(a, b):
    return lax.dot_general(a, b, (((1,), (1,)), ((), ())), preferred_element_type=F32)


def _dot_tn(a, b):
    return lax.dot_general(a, b, (((0,), (0,)), ((), ())), preferred_element_type=F32)


def _rstd(x):
    return lax.rsqrt(jnp.mean(x * x, axis=-1, keepdims=True) + EPS)


def _rms(x, g):
    return x * _rstd(x) * g


def _rms_bwd(x, g, dy):
    r = _rstd(x)
    n = x * r
    dn = dy * g
    dx = r * (dn - n * jnp.mean(dn * n, axis=-1, keepdims=True))
    dg = jnp.sum(dy * n, axis=0, keepdims=True)
    return dx, dg


def _sigmoid(x):
    return 1.0 / (1.0 + jnp.exp(-x))


def _acc(ref, val, first):
    @pl.when(first)
    def _():
        ref[...] = val

    @pl.when(jnp.logical_not(first))
    def _():
        ref[...] += val


def _pool_counts(row0, rows):
    t = row0 + lax.broadcasted_iota(jnp.int32, (rows, D_MODEL), 0) + 1
    grp = lax.broadcasted_iota(jnp.int32, (rows, D_MODEL), 1) // POOL_GROUP
    win = jnp.left_shift(2, grp)
    return jnp.minimum(t, win).astype(F32)


def _window_sums(ext, shift_of):
    outs = []
    s = ext
    for gi in range(N_POOL_GROUPS):
        s = s + pltpu.roll(s, shift_of(1 << gi), axis=0)
        outs.append(s[:, :POOL_GROUP])
        s = s[:, POOL_GROUP:]
    return jnp.concatenate(outs, axis=1)


def _cparams(n_axes, vmem):
    return pltpu.CompilerParams(dimension_semantics=("arbitrary",) * n_axes, vmem_limit_bytes=vmem)


def _row_spec(cols, tm=ROW_TILE):
    return pl.BlockSpec((tm, cols), lambda i: (i, 0))


def _full_spec(shape):
    zeros = (0,) * len(shape)
    return pl.BlockSpec(shape, lambda *_: zeros)


def _vec_spec():
    return _full_spec((1, D_MODEL))


class _Gain:
    def __init__(self, stacked, layer):
        self.stacked, self.layer = stacked, layer

    def spec(self):
        layer = self.layer
        return pl.BlockSpec((None, 1, D_MODEL), lambda *_: (layer, 0, 0))


def _in_hbm(a):
    return pltpu.with_memory_space_constraint(a, pltpu.HBM) if a.size >= HBM_PIN_ELEMS else a


def _out_in_hbm(s):
    return pltpu.HBM(s.shape, s.dtype) if s.size >= HBM_PIN_ELEMS else s


def _launch(body, *, name, grid, in_specs, out_specs, out_shape, args, scratch_shapes=(), vmem=VMEM_MID, job=None):
    in_specs = [a.spec() if isinstance(a, _Gain) else s for s, a in zip(in_specs, args)]
    args = [_in_hbm(a.stacked if isinstance(a, _Gain) else a) for a in args]
    n_in, n_out, n_scr = len(args), len(out_shape), len(scratch_shapes)
    j_args, j_out, j_scr = ([], [], []) if job is None else ([_in_hbm(a) for a in job.args], job.out_shape, job.scratch)

    def run(*refs):
        groups, at = [], 0
        for n in (n_in, len(j_args), n_out, len(j_out), n_scr, len(j_scr)):
            groups.append(refs[at:at + n])
            at += n
        ins, j_ins, outs, j_outs, scr, j_sems = groups
        if job is None:
            body(*ins, *outs, *scr)
        elif not grid:
            job.start(j_ins, j_outs, j_sems)
            body(*ins, *outs, *scr)
            job.finish(j_ins, j_outs, j_sems)
        else:
            ids = [pl.program_id(a) for a in range(len(grid))]
            first = functools.reduce(jnp.logical_and, [i == 0 for i in ids])
            last = functools.reduce(jnp.logical_and, [i == g - 1 for i, g in zip(ids, grid)])
            pl.when(first)(lambda: job.start(j_ins, j_outs, j_sems))
            body(*ins, *outs, *scr)
            pl.when(last)(lambda: job.finish(j_ins, j_outs, j_sems))

    res = pl.pallas_call(
        run, name=name, grid=grid,
        in_specs=list(in_specs) + [ANY] * len(j_args), out_specs=list(out_specs) + [ANY] * len(j_out),
        out_shape=[_out_in_hbm(s) for s in list(out_shape) + list(j_out)],
        scratch_shapes=list(scratch_shapes) + list(j_scr),
        compiler_params=_cparams(len(grid), vmem),
    )(*args, *j_args)
    return res[:n_out], res[n_out:]


def _fwd_pool_mixer(x, g_pre, wp, scale, g_post, g_ffn, job=None):
    T = x.shape[0]
    tm = ROW_TILE
    nt = T // tm

    def body(x_ref, gpre_ref, wp_ref, sc_ref, gpost_ref, gffn_ref, x1_ref, h2_ref, yraw_ref, d_ref, carry):
        i = pl.program_id(0)

        @pl.when(i == 0)
        def _():
            carry[...] = jnp.zeros_like(carry)

        xv = x_ref[...]
        h = _rms(xv, gpre_ref[...])
        ext = jnp.concatenate([carry[...], h], axis=0)
        carry[...] = h[tm - POOL_HALO:, :]
        sums = _window_sums(ext, lambda k: k)[POOL_HALO:, :]
        d = sums / _pool_counts(i * tm, tm) - h
        db = d.astype(BF16)
        d_ref[...] = db
        yraw = jnp.concatenate(
            [_dot(db[:, g * POOL_GROUP:(g + 1) * POOL_GROUP], wp_ref[g]) for g in range(N_POOL_GROUPS)], axis=1)
        yraw_ref[...] = yraw
        x1 = xv + _rms(yraw * sc_ref[...], gpost_ref[...])
        x1_ref[...] = x1
        h2_ref[...] = _rms(x1, gffn_ref[...]).astype(BF16)

    return _launch(
        body, name="fwd_pool_mixer", grid=(nt,),
        in_specs=[_row_spec(D_MODEL), _vec_spec(), _full_spec((N_POOL_GROUPS, POOL_GROUP, POOL_GROUP)), _vec_spec(),
                  _vec_spec(), _vec_spec()],
        out_specs=[_row_spec(D_MODEL)] * 4,
        out_shape=[jax.ShapeDtypeStruct((T, D_MODEL), F32), jax.ShapeDtypeStruct((T, D_MODEL), BF16),
                   jax.ShapeDtypeStruct((T, D_MODEL), F32), jax.ShapeDtypeStruct((T, D_MODEL), BF16)],
        scratch_shapes=[pltpu.VMEM((POOL_HALO, D_MODEL), F32)],
        args=(x, g_pre, wp, scale, g_post, g_ffn), job=job)


def _fwd_ffn(layer, h2, x1, wgu, wd, g_post, g_ple, job=None):
    T = h2.shape[0]
    tm = min(FFN_ROW_TILE, T)
    nt = T // tm
    sub = tm // FFN_SUB_TILES
    last = FF_CHUNKS - 1

    def body(h2_ref, x1_ref, wgu_ref, wd_ref, gpost_ref, gple_ref, gs_ref, us_ref, f_ref, x2_ref, h3_ref, acc):
        k = pl.program_id(0)
        i = pl.program_id(1)
        rows = pl.ds(pl.multiple_of(i * tm, tm), tm)
        parts = []
        for s in range(FFN_SUB_TILES):
            r = pl.ds(s * sub, sub)
            h = h2_ref[r, :]
            g = _dot_nt(h, wgu_ref[0])
            u = _dot_nt(h, wgu_ref[1])
            gs_ref[r, :] = g.astype(BF16)
            us_ref[r, :] = u.astype(BF16)
            a = (g * _sigmoid(g) * u).astype(BF16)
            parts.append(_dot(a, wd_ref[...]))
        part = jnp.concatenate(parts, axis=0)

        @pl.when(k == 0)
        def _():
            acc[rows, :] = part

        @pl.when(jnp.logical_and(k > 0, k < last))
        def _():
            acc[rows, :] += part

        @pl.when(k == last)
        def _():
            f = acc[rows, :] + part
            f_ref[...] = f
            x2 = x1_ref[...] + _rms(f, gpost_ref[...])
            x2_ref[...] = x2
            h3_ref[...] = _rms(x2, gple_ref[...]).astype(BF16)

    def late(k, i):
        return (jnp.where(k == last, i, 0), 0)

    return _launch(
        body, name=f"fwd_ffn{layer}", grid=(FF_CHUNKS, nt),
        in_specs=[pl.BlockSpec((tm, D_MODEL), lambda k, i: (i, 0)),
                  pl.BlockSpec((tm, D_MODEL), late),
                  pl.BlockSpec((None, 2, FF_BLOCK, D_MODEL), lambda k, i: (k, 0, 0, 0)),
                  pl.BlockSpec((FF_BLOCK, D_MODEL), lambda k, i: (k, 0)),
                  pl.BlockSpec((1, D_MODEL), lambda k, i: (0, 0)),
                  pl.BlockSpec((1, D_MODEL), lambda k, i: (0, 0))],
        out_specs=[pl.BlockSpec((None, tm, FF_BLOCK), lambda k, i: (k, i, 0)),
                   pl.BlockSpec((None, tm, FF_BLOCK), lambda k, i: (k, i, 0)),
                   pl.BlockSpec((tm, D_MODEL), late),
                   pl.BlockSpec((tm, D_MODEL), late),
                   pl.BlockSpec((tm, D_MODEL), late)],
        out_shape=[jax.ShapeDtypeStruct((FF_CHUNKS, T, FF_BLOCK), BF16),
                   jax.ShapeDtypeStruct((FF_CHUNKS, T, FF_BLOCK), BF16),
                   jax.ShapeDtypeStruct((T, D_MODEL), F32),
                   jax.ShapeDtypeStruct((T, D_MODEL), F32),
                   jax.ShapeDtypeStruct((T, D_MODEL), BF16)],
        scratch_shapes=[pltpu.VMEM((T, D_MODEL), F32)],
        args=(h2, x1, wgu, wd, g_post, g_ple), vmem=VMEM_BIG, job=job)


def _fwd_ple(layer, x2, h3, p, wgate, wproj, g_post, target=None, job=None):
    T = x2.shape[0]
    tm = ROW_TILE
    nt = T // tm
    with_loss = target is not None

    def body(*refs):
        if with_loss:
            x2_ref, h3_ref, p_ref, wg_ref, wp_ref, gpost_ref, tgt_ref, out_ref, z_ref, pe_ref, loss_ref = refs
        else:
            x2_ref, h3_ref, p_ref, wg_ref, wp_ref, gpost_ref, out_ref, z_ref, pe_ref = refs
        z = _dot(h3_ref[...], wg_ref[...])
        pe = _dot(p_ref[...].astype(BF16), wp_ref[...])
        z_ref[...] = z
        pe_ref[...] = pe
        x3 = x2_ref[...] + _rms(pe * _sigmoid(z), gpost_ref[...])
        if with_loss:
            err = x3 - tgt_ref[...]
            out_ref[...] = err * (1.0 / D_MODEL)
            part = 0.5 * jnp.sum(jnp.mean(err * err, axis=-1, keepdims=True), axis=0, keepdims=True)
            _acc(loss_ref, part, pl.program_id(0) == 0)
        else:
            out_ref[...] = x3

    in_specs = [_row_spec(D_MODEL), _row_spec(D_MODEL), _row_spec(PLE_DIM), _full_spec((D_MODEL, D_MODEL)),
                _full_spec((PLE_DIM, D_MODEL)), _vec_spec()]
    out_specs = [_row_spec(D_MODEL)] * 3
    out_shape = [jax.ShapeDtypeStruct((T, D_MODEL), F32)] * 3
    args = [x2, h3, p, wgate, wproj, g_post]
    if with_loss:
        in_specs.append(_row_spec(D_MODEL))
        out_specs.append(_full_spec((1, 1)))
        out_shape.append(jax.ShapeDtypeStruct((1, 1), F32))
        args.append(target)
    return _launch(body, name=f"fwd_ple{layer}", grid=(nt,), in_specs=in_specs, out_specs=out_specs,
                   out_shape=out_shape, args=args, job=job)


def _fwd_qkv(x3, g_kv, g_mix, wkv, wq, job=None):
    T = x3.shape[0]
    nt = T // ROW_TILE

    def body(x_ref, gkv_ref, gmix_ref, wkv_ref, wq_ref, hk_ref, h1_ref, q_ref, kv_ref):
        xv = x_ref[...]
        r = _rstd(xv)
        hk = (xv * r * gkv_ref[...]).astype(BF16)
        h1 = (xv * r * gmix_ref[...]).astype(BF16)
        hk_ref[...] = hk
        h1_ref[...] = h1
        kv_ref[...] = _dot(hk, wkv_ref[...]).astype(BF16)
        q_ref[...] = _dot(h1, wq_ref[...]).astype(BF16)

    return _launch(
        body, name="fwd_qkv", grid=(nt,),
        in_specs=[_row_spec(D_MODEL), _vec_spec(), _vec_spec(), _full_spec((D_MODEL, 2 * KV_DIM)),
                  _full_spec((D_MODEL, D_MODEL))],
        out_specs=[_row_spec(D_MODEL), _row_spec(D_MODEL), _row_spec(D_MODEL), _row_spec(2 * KV_DIM)],
        out_shape=[jax.ShapeDtypeStruct((T, D_MODEL), BF16)] * 3 + [jax.ShapeDtypeStruct((T, 2 * KV_DIM), BF16)],
        args=(x3, g_kv, g_mix, wkv, wq), job=job)


def _alibi_slope(h):
    return 2.0 ** (-8.0 * (h + 1) / N_HEADS)


ATT_SUB = 32
ATT_GROUP_ROWS = GQA_GROUP * ATT_BLOCK


def _att_mask(n, row0):
    qi = lax.broadcasted_iota(jnp.int32, (ATT_SUB, 2 * ATT_BLOCK), 0) + row0
    si = lax.broadcasted_iota(jnp.int32, (ATT_SUB, 2 * ATT_BLOCK), 1)
    rel = ATT_BLOCK + qi - si
    valid = (rel >= 0) & (rel < ATT_BLOCK) & ((si >= ATT_BLOCK) | (n > 0))
    return rel.astype(F32), valid


def _att_probs(raw, relf, valid, slope, sink):
    s = jnp.where(valid, raw * ATT_SCALE - slope * relf, NEG_INF)
    m = jnp.maximum(jnp.max(s, axis=-1, keepdims=True), sink)
    e = jnp.exp(s - m)
    es = jnp.exp(sink - m)
    inv = 1.0 / (jnp.sum(e, axis=-1, keepdims=True) + es)
    return e * inv, es * inv


def _stack_heads(ref, kh):
    first = kh * GQA_GROUP
    return jnp.concatenate([ref[:, (first + g) * HEAD_DIM:(first + g + 1) * HEAD_DIM] for g in range(GQA_GROUP)], axis=0)


def _unstack_heads(stacked):
    return [stacked[g * ATT_BLOCK:(g + 1) * ATT_BLOCK, :] for g in range(GQA_GROUP)]


def _fwd_attention(q, kpad, vpad, sinks, job=None):
    T = q.shape[0]
    nb = T // ATT_BLOCK

    def body(q_ref, k_ref, v_ref, sink_ref, o_ref, s_scr, p_scr):
        n = pl.program_id(0)
        start = pl.multiple_of(n * ATT_BLOCK, ATT_BLOCK)
        kw = k_ref[pl.ds(start, 2 * ATT_BLOCK), :]
        vw = v_ref[pl.ds(start, 2 * ATT_BLOCK), :]
        outs = []
        for kh in range(N_KV_HEADS):
            kk = kw[:, kh * HEAD_DIM:(kh + 1) * HEAD_DIM]
            vv = vw[:, kh * HEAD_DIM:(kh + 1) * HEAD_DIM]
            s_scr[...] = _dot_nt(_stack_heads(q_ref, kh), kk)
            for g in range(GQA_GROUP):
                h = kh * GQA_GROUP + g
                for row0 in range(0, ATT_BLOCK, ATT_SUB):
                    rows = pl.ds(g * ATT_BLOCK + row0, ATT_SUB)
                    relf, valid = _att_mask(n, row0)
                    pr, _ = _att_probs(s_scr[rows, :], relf, valid, _alibi_slope(h), sink_ref[0, h])
                    p_scr[rows, :] = pr.astype(BF16)
            outs += _unstack_heads(_dot(p_scr[...], vv))
        o_ref[...] = jnp.concatenate(outs, axis=1).astype(BF16)

    return _launch(
        body, name="fwd_attention", grid=(nb,),
        in_specs=[_row_spec(D_MODEL, ATT_BLOCK), _full_spec((T + ATT_BLOCK, KV_DIM)), _full_spec((T + ATT_BLOCK, KV_DIM)),
                  pl.BlockSpec(memory_space=pltpu.SMEM)],
        out_specs=[_row_spec(D_MODEL, ATT_BLOCK)],
        out_shape=[jax.ShapeDtypeStruct((T, D_MODEL), BF16)],
        scratch_shapes=[pltpu.VMEM((ATT_GROUP_ROWS, 2 * ATT_BLOCK), F32), pltpu.VMEM((ATT_GROUP_ROWS, 2 * ATT_BLOCK), BF16)],
        args=(q, kpad, vpad, sinks), job=job)


def _fwd_attn_out(attn, x, wo, g_post, g_ffn, job=None):
    T = x.shape[0]
    nt = T // ROW_TILE

    def body(a_ref, x_ref, wo_ref, gpost_ref, gffn_ref, y_ref, x1_ref, h2_ref):
        y = _dot(a_ref[...], wo_ref[...])
        y_ref[...] = y
        x1 = x_ref[...] + _rms(y, gpost_ref[...])
        x1_ref[...] = x1
        h2_ref[...] = _rms(x1, gffn_ref[...]).astype(BF16)

    return _launch(
        body, name="fwd_attn_out", grid=(nt,),
        in_specs=[_row_spec(D_MODEL), _row_spec(D_MODEL), _full_spec((D_MODEL, D_MODEL)), _vec_spec(), _vec_spec()],
        out_specs=[_row_spec(D_MODEL)] * 3,
        out_shape=[jax.ShapeDtypeStruct((T, D_MODEL), F32), jax.ShapeDtypeStruct((T, D_MODEL), F32),
                   jax.ShapeDtypeStruct((T, D_MODEL), BF16)],
        args=(attn, x, wo, g_post, g_ffn), job=job)


def _bwd_ple(layer, dx3, x2, z, pe, h3, p, f, wgate, g_ple_post, g_ple, g_post_ffn, job=None):
    T = x2.shape[0]
    tm = ROW_TILE
    nt = T // tm

    def body(dx3_ref, x2_ref, z_ref, pe_ref, h3_ref, p_ref, f_ref, wg_ref, gpp_ref, gp_ref, gpf_ref,
             dx2_ref, df_ref, dwg_ref, dwp_ref, dgpp_ref, dgp_ref, dgpf_ref, acc_g, acc_p):
        i = pl.program_id(0)
        first = i == 0
        dx3v = dx3_ref[...]
        gate = _sigmoid(z_ref[...])
        pev = pe_ref[...]
        de, dgpp = _rms_bwd(pev * gate, gpp_ref[...], dx3v)
        dpe = (de * gate).astype(BF16)
        dz = (de * pev * gate * (1.0 - gate)).astype(BF16)
        _acc(acc_p, _dot_tn(p_ref[...].astype(BF16), dpe), first)
        _acc(acc_g, _dot_tn(h3_ref[...], dz), first)
        dh3 = _dot_nt(dz, wg_ref[...])
        dxn, dgp = _rms_bwd(x2_ref[...], gp_ref[...], dh3)
        dx2 = dx3v + dxn
        dx2_ref[...] = dx2
        df, dgpf = _rms_bwd(f_ref[...], gpf_ref[...], dx2)
        df_ref[...] = df.astype(BF16)
        _acc(dgpp_ref, dgpp, first)
        _acc(dgp_ref, dgp, first)
        _acc(dgpf_ref, dgpf, first)

        @pl.when(i == nt - 1)
        def _():
            dwg_ref[...] = acc_g[...].astype(BF16)
            dwp_ref[...] = acc_p[...].astype(BF16)

    return _launch(
        body, name=f"bwd_ple{layer}", grid=(nt,),
        in_specs=[_row_spec(D_MODEL)] * 5 + [_row_spec(PLE_DIM), _row_spec(D_MODEL), _full_spec((D_MODEL, D_MODEL)),
                  _vec_spec(), _vec_spec(), _vec_spec()],
        out_specs=[_row_spec(D_MODEL), _row_spec(D_MODEL), _full_spec((D_MODEL, D_MODEL)), _full_spec((PLE_DIM, D_MODEL)),
                   _vec_spec(), _vec_spec(), _vec_spec()],
        out_shape=[jax.ShapeDtypeStruct((T, D_MODEL), F32), jax.ShapeDtypeStruct((T, D_MODEL), BF16),
                   jax.ShapeDtypeStruct((D_MODEL, D_MODEL), BF16), jax.ShapeDtypeStruct((PLE_DIM, D_MODEL), BF16)]
                  + [jax.ShapeDtypeStruct((1, D_MODEL), F32)] * 3,
        scratch_shapes=[pltpu.VMEM((D_MODEL, D_MODEL), F32), pltpu.VMEM((PLE_DIM, D_MODEL), F32)],
        args=(dx3, x2, z, pe, h3, p, f, wgate, g_ple_post, g_ple, g_post_ffn), vmem=VMEM_BIG, job=job)


def _bwd_ffn_act(layer, df, gs, us, wgu, wd, job=None):
    T = df.shape[0]
    tm = min(FFN_ROW_TILE, T)
    nt = T // tm
    sub = tm // FFN_SUB_TILES
    last = FF_CHUNKS - 1

    def body(df_ref, gs_ref, us_ref, wgu_ref, wd_ref, dh_ref, dg_ref, du_ref, a_ref, acc_h):
        k = pl.program_id(0)
        i = pl.program_id(1)
        rows = pl.ds(pl.multiple_of(i * tm, tm), tm)
        dhs = []
        for s in range(FFN_SUB_TILES):
            r = pl.ds(s * sub, sub)
            g = gs_ref[r, :].astype(F32)
            u = us_ref[r, :].astype(F32)
            sg = _sigmoid(g)
            silu = g * sg
            a_ref[r, :] = (silu * u).astype(BF16)
            da = _dot_nt(df_ref[r, :], wd_ref[...])
            dg = (da * u * (sg * (1.0 + g * (1.0 - sg)))).astype(BF16)
            du = (da * silu).astype(BF16)
            dg_ref[r, :] = dg
            du_ref[r, :] = du
            dhs.append(_dot(dg, wgu_ref[0]) + _dot(du, wgu_ref[1]))
        dh = jnp.concatenate(dhs, axis=0)

        @pl.when(k == 0)
        def _():
            acc_h[rows, :] = dh

        @pl.when(jnp.logical_and(k > 0, k < last))
        def _():
            acc_h[rows, :] += dh

        @pl.when(k == last)
        def _():
            dh_ref[...] = acc_h[rows, :] + dh

    chunk_rows = pl.BlockSpec((None, tm, FF_BLOCK), lambda k, i: (k, i, 0))
    saved = jax.ShapeDtypeStruct((FF_CHUNKS, T, FF_BLOCK), BF16)
    return _launch(
        body, name=f"bwd_ffn_act{layer}", grid=(FF_CHUNKS, nt),
        in_specs=[pl.BlockSpec((tm, D_MODEL), lambda k, i: (i, 0)), chunk_rows, chunk_rows,
                  pl.BlockSpec((None, 2, FF_BLOCK, D_MODEL), lambda k, i: (k, 0, 0, 0)),
                  pl.BlockSpec((FF_BLOCK, D_MODEL), lambda k, i: (k, 0))],
        out_specs=[pl.BlockSpec((tm, D_MODEL), lambda k, i: (jnp.where(k == last, i, 0), 0)),
                   chunk_rows, chunk_rows, chunk_rows],
        out_shape=[jax.ShapeDtypeStruct((T, D_MODEL), F32), saved, saved, saved],
        scratch_shapes=[pltpu.VMEM((T, D_MODEL), F32)],
        args=(df, gs, us, wgu, wd), vmem=VMEM_BIG, job=job)


def _bwd_ffn_dw(layer, q, h2, df, dg, du, a, job=None):
    T = h2.shape[0]

    def body(h_ref, df_ref, dg_ref, du_ref, a_ref, dgu_ref, dwd_ref):
        h = h_ref[...]
        dgu_ref[0] = _dot_tn(dg_ref[...], h).astype(BF16)
        dgu_ref[1] = _dot_tn(du_ref[...], h).astype(BF16)
        dwd_ref[...] = _dot_tn(a_ref[...], df_ref[...]).astype(BF16)

    cols = pl.BlockSpec((T, FF_PART), lambda k: (0, q))
    chunk = pl.BlockSpec((None, T, FF_BLOCK), lambda k: (k, 0, 0))
    return _launch(
        body, name=f"bwd_ffn_dw{layer}_{q}", grid=(FF_CHUNKS,),
        in_specs=[cols, cols, chunk, chunk, chunk],
        out_specs=[pl.BlockSpec((None, 2, FF_BLOCK, FF_PART), lambda k: (k, 0, 0, 0)),
                   pl.BlockSpec((FF_BLOCK, FF_PART), lambda k: (k, 0))],
        out_shape=[jax.ShapeDtypeStruct((FF_CHUNKS, 2, FF_BLOCK, FF_PART), BF16),
                   jax.ShapeDtypeStruct((D_FF, FF_PART), BF16)],
        args=(h2, df, dg, du, a), vmem=VMEM_BIG, job=job)


def _bwd_attn_out(dx2, dh2, x1, y, attn, wo, g_ffn, g_post, job=None):
    T = x1.shape[0]
    nt = T // ROW_TILE

    def body(dx2_ref, dh2_ref, x1_ref, y_ref, a_ref, wo_ref, gffn_ref, gpost_ref,
             dx1_ref, da_ref, dwo_ref, dgf_ref, dgp_ref, acc):
        i = pl.program_id(0)
        first = i == 0
        dxn, dgf = _rms_bwd(x1_ref[...], gffn_ref[...], dh2_ref[...])
        dx1 = dx2_ref[...] + dxn
        dx1_ref[...] = dx1
        dy, dgp = _rms_bwd(y_ref[...], gpost_ref[...], dx1)
        dyb = dy.astype(BF16)
        da_ref[...] = _dot_nt(dyb, wo_ref[...]).astype(BF16)
        _acc(acc, _dot_tn(a_ref[...], dyb), first)
        _acc(dgf_ref, dgf, first)
        _acc(dgp_ref, dgp, first)

        @pl.when(i == nt - 1)
        def _():
            dwo_ref[...] = acc[...].astype(BF16)

    return _launch(
        body, name="bwd_attn_out", grid=(nt,),
        in_specs=[_row_spec(D_MODEL)] * 5 + [_full_spec((D_MODEL, D_MODEL)), _vec_spec(), _vec_spec()],
        out_specs=[_row_spec(D_MODEL), _row_spec(D_MODEL), _full_spec((D_MODEL, D_MODEL)), _vec_spec(), _vec_spec()],
        out_shape=[jax.ShapeDtypeStruct((T, D_MODEL), F32), jax.ShapeDtypeStruct((T, D_MODEL), BF16),
                   jax.ShapeDtypeStruct((D_MODEL, D_MODEL), BF16)] + [jax.ShapeDtypeStruct((1, D_MODEL), F32)] * 2,
        scratch_shapes=[pltpu.VMEM((D_MODEL, D_MODEL), F32)],
        args=(dx2, dh2, x1, y, attn, wo, g_ffn, g_post), job=job)


def _bwd_attention(q, dattn, kpad, vpad, sinks, job=None):
    T = q.shape[0]
    nb = T // ATT_BLOCK

    def body(q_ref, do_ref, k_ref, v_ref, sink_ref, dq_ref, dk_ref, dv_ref, ds_ref, s_scr, dp_scr, p_scr, dsb_scr):
        n = pl.program_id(0)

        @pl.when(n == 0)
        def _():
            dk_ref[...] = jnp.zeros_like(dk_ref)
            dv_ref[...] = jnp.zeros_like(dv_ref)
            ds_ref[...] = jnp.zeros_like(ds_ref)

        start = pl.multiple_of(n * ATT_BLOCK, ATT_BLOCK)
        win = pl.ds(start, 2 * ATT_BLOCK)
        kw = k_ref[win, :]
        vw = v_ref[win, :]
        lane = lax.broadcasted_iota(jnp.int32, (1, ATT_BLOCK), 1)
        dsink = jnp.zeros((1, ATT_BLOCK), F32)
        dqs, dks, dvs = [], [], []
        for kh in range(N_KV_HEADS):
            kk = kw[:, kh * HEAD_DIM:(kh + 1) * HEAD_DIM]
            vv = vw[:, kh * HEAD_DIM:(kh + 1) * HEAD_DIM]
            qs = _stack_heads(q_ref, kh)
            dos = _stack_heads(do_ref, kh)
            s_scr[...] = _dot_nt(qs, kk)
            dp_scr[...] = _dot_nt(dos, vv)
            for g in range(GQA_GROUP):
                h = kh * GQA_GROUP + g
                dsink_h = jnp.zeros((1, 1), F32)
                for row0 in range(0, ATT_BLOCK, ATT_SUB):
                    rows = pl.ds(g * ATT_BLOCK + row0, ATT_SUB)
                    relf, valid = _att_mask(n, row0)
                    pr, ps = _att_probs(s_scr[rows, :], relf, valid, _alibi_slope(h), sink_ref[0, h])
                    dp = dp_scr[rows, :]
                    delta = jnp.sum(pr * dp, axis=-1, keepdims=True)
                    dsb_scr[rows, :] = (pr * (dp - delta) * ATT_SCALE).astype(BF16)
                    p_scr[rows, :] = pr.astype(BF16)
                    dsink_h = dsink_h - jnp.sum(ps * delta, axis=0, keepdims=True)
                dsink = dsink + jnp.where(lane == h, dsink_h, 0.0)
            dsb = dsb_scr[...]
            dqs += _unstack_heads(_dot(dsb, kk))
            dks.append(_dot_tn(dsb, qs))
            dvs.append(_dot_tn(p_scr[...], dos))
        dq_ref[...] = jnp.concatenate(dqs, axis=1).astype(BF16)
        dk_ref[win, :] += jnp.concatenate(dks, axis=1)
        dv_ref[win, :] += jnp.concatenate(dvs, axis=1)
        ds_ref[...] += dsink

    return _launch(
        body, name="bwd_attention", grid=(nb,),
        in_specs=[_row_spec(D_MODEL, ATT_BLOCK), _row_spec(D_MODEL, ATT_BLOCK), _full_spec((T + ATT_BLOCK, KV_DIM)),
                  _full_spec((T + ATT_BLOCK, KV_DIM)), pl.BlockSpec(memory_space=pltpu.SMEM)],
        out_specs=[_row_spec(D_MODEL, ATT_BLOCK), _full_spec((T + ATT_BLOCK, KV_DIM)), _full_spec((T + ATT_BLOCK, KV_DIM)),
                   _full_spec((1, ATT_BLOCK))],
        out_shape=[jax.ShapeDtypeStruct((T, D_MODEL), BF16), jax.ShapeDtypeStruct((T + ATT_BLOCK, KV_DIM), F32),
                   jax.ShapeDtypeStruct((T + ATT_BLOCK, KV_DIM), F32), jax.ShapeDtypeStruct((1, ATT_BLOCK), F32)],
        scratch_shapes=[pltpu.VMEM((ATT_GROUP_ROWS, 2 * ATT_BLOCK), F32)] * 2
                       + [pltpu.VMEM((ATT_GROUP_ROWS, 2 * ATT_BLOCK), BF16)] * 2,
        args=(q, dattn, kpad, vpad, sinks), vmem=VMEM_BIG, job=job)


def _bwd_qkv(dxres, dq, dkv, x3, h1, hk, wq, wkv, g_mix, g_kv, job=None):
    T = x3.shape[0]
    nt = T // ROW_TILE

    def body(dxr_ref, dq_ref, dkv_ref, x_ref, h1_ref, hk_ref, wq_ref, wkv_ref, gmix_ref, gkv_ref,
             dx_ref, dwq_ref, dwkv_ref, dgm_ref, dgk_ref, acc_q, acc_kv):
        i = pl.program_id(0)
        first = i == 0
        dqv = dq_ref[...]
        dkvv = dkv_ref[...]
        xv = x_ref[...]
        d1, dgm = _rms_bwd(xv, gmix_ref[...], _dot_nt(dqv, wq_ref[...]))
        d2, dgk = _rms_bwd(xv, gkv_ref[...], _dot_nt(dkvv, wkv_ref[...]))
        dx_ref[...] = dxr_ref[...] + d1 + d2
        _acc(acc_q, _dot_tn(h1_ref[...], dqv), first)
        _acc(acc_kv, _dot_tn(hk_ref[...], dkvv), first)
        _acc(dgm_ref, dgm, first)
        _acc(dgk_ref, dgk, first)

        @pl.when(i == nt - 1)
        def _():
            dwq_ref[...] = acc_q[...].astype(BF16)
            dwkv_ref[...] = acc_kv[...].astype(BF16)

    return _launch(
        body, name="bwd_qkv", grid=(nt,),
        in_specs=[_row_spec(D_MODEL), _row_spec(D_MODEL), _row_spec(2 * KV_DIM), _row_spec(D_MODEL), _row_spec(D_MODEL),
                  _row_spec(D_MODEL), _full_spec((D_MODEL, D_MODEL)), _full_spec((D_MODEL, 2 * KV_DIM)), _vec_spec(),
                  _vec_spec()],
        out_specs=[_row_spec(D_MODEL), _full_spec((D_MODEL, D_MODEL)), _full_spec((D_MODEL, 2 * KV_DIM)), _vec_spec(),
                   _vec_spec()],
        out_shape=[jax.ShapeDtypeStruct((T, D_MODEL), F32), jax.ShapeDtypeStruct((D_MODEL, D_MODEL), BF16),
                   jax.ShapeDtypeStruct((D_MODEL, 2 * KV_DIM), BF16)] + [jax.ShapeDtypeStruct((1, D_MODEL), F32)] * 2,
        scratch_shapes=[pltpu.VMEM((D_MODEL, D_MODEL), F32), pltpu.VMEM((D_MODEL, 2 * KV_DIM), F32)],
        args=(dxres, dq, dkv, x3, h1, hk, wq, wkv, g_mix, g_kv), job=job)


def _bwd_pool_mixer(dx2, dh2, x1, x, yraw, d, wp, scale, g_ffn, g_post, g_pre, job=None):
    T = x.shape[0]
    tm = ROW_TILE
    nt = T // tm

    def body(dx2_ref, dh2_ref, x1_ref, x_ref, yraw_ref, d_ref, wp_ref, sc_ref, gffn_ref, gpost_ref, gpre_ref,
             dx_ref, dwp_ref, dsc_ref, dgf_ref, dgp_ref, dgm_ref, carry, acc):
        i = pl.program_id(0)
        first = i == 0
        tile = nt - 1 - i

        @pl.when(first)
        def _():
            carry[...] = jnp.zeros_like(carry)

        dxn, dgf = _rms_bwd(x1_ref[...], gffn_ref[...], dh2_ref[...])
        dx1 = dx2_ref[...] + dxn
        yraw = yraw_ref[...]
        sc = sc_ref[...]
        dy, dgp = _rms_bwd(yraw * sc, gpost_ref[...], dx1)
        dsc = jnp.sum(dy * yraw, axis=0, keepdims=True)
        dyb = (dy * sc).astype(BF16)
        dv = d_ref[...]
        dds = []
        for g in range(N_POOL_GROUPS):
            cols = slice(g * POOL_GROUP, (g + 1) * POOL_GROUP)
            dds.append(_dot_nt(dyb[:, cols], wp_ref[g]))
            _acc(acc.at[g], _dot_tn(dv[:, cols], dyb[:, cols]), first)
        dd = jnp.concatenate(dds, axis=1)
        e = dd / _pool_counts(tile * tm, tm)
        ext = jnp.concatenate([e, carry[...]], axis=0)
        carry[...] = e[:POOL_HALO, :]
        sums = _window_sums(ext, lambda k: tm + POOL_HALO - k)[:tm, :]
        dxm, dgm = _rms_bwd(x_ref[...], gpre_ref[...], sums - dd)
        dx_ref[...] = dx1 + dxm
        _acc(dsc_ref, dsc, first)
        _acc(dgf_ref, dgf, first)
        _acc(dgp_ref, dgp, first)
        _acc(dgm_ref, dgm, first)

        @pl.when(i == nt - 1)
        def _():
            dwp_ref[...] = acc[...].astype(BF16)

    rev = pl.BlockSpec((tm, D_MODEL), lambda i: (nt - 1 - i, 0))
    return _launch(
        body, name="bwd_pool_mixer", grid=(nt,),
        in_specs=[rev] * 6 + [_full_spec((N_POOL_GROUPS, POOL_GROUP, POOL_GROUP))] + [_vec_spec()] * 4,
        out_specs=[rev, _full_spec((N_POOL_GROUPS, POOL_GROUP, POOL_GROUP))] + [_vec_spec()] * 4,
        out_shape=[jax.ShapeDtypeStruct((T, D_MODEL), F32),
                   jax.ShapeDtypeStruct((N_POOL_GROUPS, POOL_GROUP, POOL_GROUP), BF16)]
                  + [jax.ShapeDtypeStruct((1, D_MODEL), F32)] * 4,
        scratch_shapes=[pltpu.VMEM((POOL_HALO, D_MODEL), F32), pltpu.VMEM((N_POOL_GROUPS, POOL_GROUP, POOL_GROUP), F32)],
        args=(dx2, dh2, x1, x, yraw, d, wp, scale, g_ffn, g_post, g_pre), job=job)


def _my_place():
    return lax.axis_index("x"), lax.axis_index("y"), lax.axis_index("c")


def _dev_index(px, py, pc):
    return 4 * px + 2 * py + pc


def _peer_by_relation(r):
    x, y, c = _my_place()
    return (x ^ ((r >> 2) & 1), y ^ ((r >> 1) & 1), c ^ (r & 1))


def _slot_pool(ref, j):
    return ref.at[:, pl.ds(pl.multiple_of(j * 32, 32), 32), :]


def _slot_scale(ref, j):
    return ref.at[:, pl.ds(pl.multiple_of(j * 128, 128), 128)]


def _slot_rows128(ref, j):
    return ref.at[pl.ds(pl.multiple_of(j * 128, 128), 128), :]


def _slot_gu(ref, j):
    return ref.at[j % FF_CHUNKS, j // FF_CHUNKS]


def _slot_wd(ref, j):
    return ref.at[pl.ds(pl.multiple_of(j * WD_ROWS, 16), WD_ROWS), :]


def _slot_cols128(ref, j):
    return ref.at[:, pl.ds(pl.multiple_of(j * 128, 128), 128)]


_GATHERED = {
    "pool": ((N_POOL_GROUPS, POOL_GROUP, POOL_GROUP), BF16, _slot_pool),
    "scale": ((1, D_MODEL), F32, _slot_scale),
    "kv": ((D_MODEL, 2 * KV_DIM), BF16, _slot_rows128),
    "q": ((D_MODEL, D_MODEL), BF16, _slot_rows128),
    "o": ((D_MODEL, D_MODEL), BF16, _slot_rows128),
    "gu": ((FF_CHUNKS, 2, FF_BLOCK, D_MODEL), BF16, _slot_gu),
    "wd": ((D_FF, D_MODEL), BF16, _slot_wd),
    "gate": ((D_MODEL, D_MODEL), BF16, _slot_rows128),
    "proj": ((PLE_DIM, D_MODEL), BF16, _slot_cols128),
}


def _no_compute():
    pass


class _AllGather:
    def __init__(self, names, shards):
        self.kinds = [_GATHERED[n.rstrip("01")] for n in names]
        self.args = [shards[n] for n in names]
        self.out_shape = [jax.ShapeDtypeStruct(shape, dtype) for shape, dtype, _ in self.kinds]
        n = len(names)
        self.scratch = [pltpu.SemaphoreType.DMA((n, 7)), pltpu.SemaphoreType.DMA((n, 7)), pltpu.SemaphoreType.DMA((n,))]

    def _copies(self, srcs, outs, sems):
        send_sems, recv_sems, local_sems = sems
        x, y, c = _my_place()
        me, sibling = (x, y, c), (x, y, 1 - c)
        chips = [(1 - x, y), (x, 1 - y), (1 - x, 1 - y)]
        n = len(srcs)

        def slot(t, dev):
            return self.kinds[t][2](outs[t], _dev_index(*dev))

        def copy(t, k, block, to, src=None):
            return pltpu.make_async_remote_copy(
                src_ref=slot(t, block) if src is None else src, dst_ref=slot(t, block),
                send_sem=send_sems.at[t, k], recv_sem=recv_sems.at[t, k], device_id=to, device_id_type=MESH)

        mine = [pltpu.make_async_copy(srcs[t], slot(t, me), local_sems.at[t]) for t in range(n)]
        first = []
        for t in range(n):
            first.append(copy(t, 0, me, sibling, src=srcs[t]))
            first += [copy(t, 1 + j, me, (*chip, c), src=srcs[t]) for j, chip in enumerate(chips)]
        return me, sibling, chips, copy, mine, first

    def start(self, srcs, outs, sems):
        _, _, _, _, mine, first = self._copies(srcs, outs, sems)
        for cp in mine + first:
            cp.start()

    def finish(self, srcs, outs, sems):
        me, sibling, chips, copy, mine, first = self._copies(srcs, outs, sems)
        c = me[2]
        n = len(srcs)
        passed = []
        for j, chip in enumerate(chips):
            for t in range(n):
                copy(t, 1 + j, (*chip, c), me).wait_recv()
                fwd = copy(t, 4 + j, (*chip, c), sibling)
                fwd.start()
                passed.append(fwd)
        for t in range(n):
            copy(t, 0, sibling, me).wait_recv()
            for j, chip in enumerate(chips):
                copy(t, 4 + j, (*chip, 1 - c), me).wait_recv()
        for cp in first + passed:
            cp.wait_send()
        for cp in mine:
            cp.wait()


def _all_gather_only(name, names, shards):
    return _launch(_no_compute, name=name, grid=(), in_specs=[], out_specs=[], out_shape=[], args=(),
                   job=_AllGather(names, shards))[1]


def _block_pool(ref, j):
    return ref.at[:, pl.ds(pl.multiple_of(j * 32, 32), 32), :]


def _block_rows128(ref, j):
    return ref.at[pl.ds(pl.multiple_of(j * 128, 128), 128), :]


def _block_gu(ref, j):
    return ref.at[j % FF_CHUNKS, j // FF_CHUNKS]


def _block_wd(ref, j):
    return ref.at[pl.ds(pl.multiple_of(j * WD_ROWS, 16), WD_ROWS), :]


def _block_cols128(ref, j):
    return ref.at[:, pl.ds(pl.multiple_of(j * 128, 128), 128)]


_SCATTERED = {
    "pool": ((N_POOL_GROUPS, 32, POOL_GROUP), _block_pool),
    "kv": ((128, 2 * KV_DIM), _block_rows128),
    "q": ((128, D_MODEL), _block_rows128),
    "o": ((128, D_MODEL), _block_rows128),
    "gu": ((FF_BLOCK, FF_PART), _block_gu),
    "wd": ((WD_ROWS, FF_PART), _block_wd),
    "gate": ((128, D_MODEL), _block_rows128),
    "proj": ((PLE_DIM, 128), _block_cols128),
}


class _SiblingSwap:
    def __init__(self, pieces):
        self.kinds = [_SCATTERED[kind] for kind, _ in pieces]
        self.args = [g for _, g in pieces]
        self.out_shape = [jax.ShapeDtypeStruct((N_CHIPS, *block), BF16) for block, _ in self.kinds]
        n = len(pieces)
        self.scratch = [pltpu.SemaphoreType.DMA((n, N_CHIPS)), pltpu.SemaphoreType.DMA((n, N_CHIPS))]

    def _copies(self, srcs, outs, sems):
        send_sems, recv_sems = sems
        x, y, c = _my_place()
        return [pltpu.make_async_remote_copy(
            src_ref=block(srcs[t], 2 * ch + 1 - c), dst_ref=outs[t].at[ch], send_sem=send_sems.at[t, ch],
            recv_sem=recv_sems.at[t, ch], device_id=(x, y, 1 - c), device_id_type=MESH)
            for t, (_, block) in enumerate(self.kinds) for ch in range(N_CHIPS)]

    def start(self, srcs, outs, sems):
        for cp in self._copies(srcs, outs, sems):
            cp.start()

    def finish(self, srcs, outs, sems):
        for cp in self._copies(srcs, outs, sems):
            cp.wait()


class _ChipScatter:
    def __init__(self, pieces):
        self.kinds = [_SCATTERED[kind] for kind, _, _ in pieces]
        self.n = n = len(pieces)
        self.args = [g for _, g, _ in pieces] + [s for _, _, s in pieces]
        self.out_shape = [jax.ShapeDtypeStruct((N_CHIPS, *block), BF16) for block, _ in self.kinds]
        self.scratch = []
        for block, _ in self.kinds:
            self.scratch += [pltpu.VMEM((N_CHIPS, *block), BF16)] * 3
        self.scratch += [pltpu.SemaphoreType.DMA((n, N_CHIPS + 1)), pltpu.SemaphoreType.DMA((n, N_CHIPS - 1)),
                         pltpu.SemaphoreType.DMA((n, N_CHIPS - 1)), pltpu.SemaphoreType.DMA((n,))]

    def _sends(self, outs, scr):
        n = self.n
        send_sems, recv_sems, local_sems = scr[3 * n + 1:]
        x, y, c = _my_place()
        chip = 2 * x + y
        copies = []
        for t in range(n):
            total = scr[3 * t + 2]
            copies.append(pltpu.make_async_copy(total.at[chip], outs[t].at[chip], local_sems.at[t]))
            for r in range(1, N_CHIPS):
                to = chip ^ r
                copies.append(pltpu.make_async_remote_copy(
                    src_ref=total.at[to], dst_ref=outs[t].at[chip], send_sem=send_sems.at[t, r - 1],
                    recv_sem=recv_sems.at[t, r - 1], device_id=(to // 2, to % 2, c), device_id_type=MESH))
        return copies

    def start(self, ins, outs, scr):
        n = self.n
        load_sems = scr[3 * n]
        c = lax.axis_index("c")
        loads = []
        for t, (_, block) in enumerate(self.kinds):
            mine, theirs = scr[3 * t], scr[3 * t + 1]
            loads += [pltpu.make_async_copy(block(ins[t], 2 * ch + c), mine.at[ch], load_sems.at[t, ch])
                      for ch in range(N_CHIPS)]
            loads.append(pltpu.make_async_copy(ins[n + t], theirs, load_sems.at[t, N_CHIPS]))
        for cp in loads:
            cp.start()
        for cp in loads:
            cp.wait()
        for t in range(n):
            mine, theirs, total = scr[3 * t:3 * t + 3]
            for ch in range(N_CHIPS):
                total[ch] = (mine[ch].astype(F32) + theirs[ch].astype(F32)).astype(BF16)
        for cp in self._sends(outs, scr):
            cp.start()

    def finish(self, ins, outs, scr):
        for cp in self._sends(outs, scr):
            cp.wait()


class _Jobs:
    def __init__(self, *jobs):
        self.jobs = jobs
        self.args = [a for j in jobs for a in j.args]
        self.out_shape = [o for j in jobs for o in j.out_shape]
        self.scratch = [s for j in jobs for s in j.scratch]

    def _split(self, refs, attr):
        at = 0
        for j in self.jobs:
            n = len(getattr(j, attr))
            yield refs[at:at + n]
            at += n

    def _each(self, ins, outs, scr):
        return zip(self.jobs, self._split(ins, "args"), self._split(outs, "out_shape"), self._split(scr, "scratch"))

    def start(self, ins, outs, scr):
        for j, i, o, s in self._each(ins, outs, scr):
            j.start(i, o, s)

    def finish(self, ins, outs, scr):
        for j, i, o, s in self._each(ins, outs, scr):
            j.finish(i, o, s)

    def split_outputs(self, outs):
        return list(self._split(outs, "out_shape"))


def _adamw_math(w, g, m, v):
    m = ADAM_B1 * m + (1.0 - ADAM_B1) * g
    v = ADAM_B2 * v + (1.0 - ADAM_B2) * (g * g)
    m_hat = m / (1.0 - ADAM_B1 ** ADAM_STEP)
    v_hat = v / (1.0 - ADAM_B2 ** ADAM_STEP)
    delta = -ADAM_LR * (m_hat / (jnp.sqrt(v_hat) + ADAM_EPS) + ADAM_WD * w)
    return delta, m, v


def _adamw(name, w, m, v, landings, n_col_blocks=1, job=None):
    _, r, c = landings[0].shape
    grid = (w.shape[0] // r, n_col_blocks)

    def body(w_ref, m_ref, v_ref, *rest):
        l_refs, (g_ref, d_ref, nm_ref, nv_ref) = rest[:len(landings)], rest[len(landings):]
        step = pl.program_id(0) * n_col_blocks + pl.program_id(1)
        for idx, l_ref in enumerate(l_refs):
            @pl.when(step == idx)
            def _(l_ref=l_ref):
                g = l_ref[0].astype(F32)
                for s in range(1, N_CHIPS):
                    g = g + l_ref[s].astype(F32)
                g_ref[...] = g
                d_ref[...], nm_ref[...], nv_ref[...] = _adamw_math(w_ref[...], g, m_ref[...], v_ref[...])

    spec = pl.BlockSpec((r, c), lambda a, b: (a, b))
    return _launch(
        body, name=f"adamw_{name}", grid=grid,
        in_specs=[spec, spec, spec] + [_full_spec((N_CHIPS, r, c))] * len(landings),
        out_specs=[spec] * 4, out_shape=[jax.ShapeDtypeStruct(w.shape, F32)] * 4,
        args=(w, m, v, *landings), vmem=VMEM_BIG, job=job)


_SMALL = (("pre_mix_g", SV_PRE_MIX, 2), ("post_mix_g", SV_POST_MIX, 2), ("pre_ffn_g", SV_PRE_FFN, 2),
          ("post_ffn_g", SV_POST_FFN, 2), ("ple_g", SV_PLE, 2), ("ple_post_g", SV_PLE_POST, 2), ("kv_g", SV_KV, 1),
          ("pool_scale", SV_POOL_SCALE, 1), ("sinks", SV_SINKS, 1))


def _small_all_reduce_adamw(part, params):
    flat = [a for name, _, _ in _SMALL for a in params[name]]
    n_in = 1 + len(flat)

    def body(*refs):
        part_ref, wmv = refs[0], refs[1:n_in]
        loss_ref, outs = refs[n_in], refs[n_in + 1:n_in + 1 + 4 * len(_SMALL)]
        buf, total, send_sems, recv_sems = refs[n_in + 1 + 4 * len(_SMALL):]
        x, y, c = _my_place()
        me = _dev_index(x, y, c)
        buf[me] = part_ref[...]
        copies = [pltpu.make_async_remote_copy(
            src_ref=buf.at[me], dst_ref=buf.at[me], send_sem=send_sems.at[r - 1], recv_sem=recv_sems.at[r - 1],
            device_id=_peer_by_relation(r), device_id_type=MESH) for r in range(1, N_DEV)]
        for cp in copies:
            cp.start()
        for cp in copies:
            cp.wait()
        g = buf[0]
        for s in range(1, N_DEV):
            g = g + buf[s]
        total[...] = g
        loss_ref[...] = total[SV_LOSS:SV_LOSS + 1, 0:1]
        for idx, (name, row, n_rows) in enumerate(_SMALL):
            w_ref, m_ref, v_ref = wmv[3 * idx:3 * idx + 3]
            g_ref, d_ref, nm_ref, nv_ref = outs[4 * idx:4 * idx + 4]
            if name == "pool_scale":
                g = total[row:row + 1, pl.ds(pl.multiple_of(me * 128, 128), 128)]
            else:
                g = total[row:row + n_rows, 0:w_ref.shape[1]]
            g_ref[...] = g
            d_ref[...], nm_ref[...], nv_ref[...] = _adamw_math(w_ref[...], g, m_ref[...], v_ref[...])

    out_shape = [jax.ShapeDtypeStruct((1, 1), F32)]
    for name, _, _ in _SMALL:
        out_shape += [jax.ShapeDtypeStruct(params[name][0].shape, F32)] * 4
    res, _ = _launch(
        body, name="small_all_reduce_adamw", grid=(1,),
        in_specs=[_full_spec(a.shape) for a in (part, *flat)], out_specs=[_full_spec(s.shape) for s in out_shape],
        out_shape=out_shape,
        scratch_shapes=[pltpu.VMEM((N_DEV, SV_ROWS, D_MODEL), F32), pltpu.VMEM((SV_ROWS, D_MODEL), F32),
                        pltpu.SemaphoreType.DMA((N_DEV - 1,)), pltpu.SemaphoreType.DMA((N_DEV - 1,))],
        args=(part, *flat))
    return res[0], {name: res[1 + 4 * idx:5 + 4 * idx] for idx, (name, _, _) in enumerate(_SMALL)}


def _local_step(x, p, tgt, gains, sinks, shards, weights):
    row = lambda first_row, layer: _Gain(gains, first_row + layer)
    gather = lambda *names: _AllGather(names, shards)
    g_pre_mix, g_post_mix, g_pre_ffn, g_post_ffn = SV_PRE_MIX, SV_POST_MIX, SV_PRE_FFN, SV_POST_FFN
    g_ple, g_ple_post, g_kv = SV_PLE, SV_PLE_POST, _Gain(gains, SV_KV)

    wp, scale, wgu0 = _all_gather_only("gather_first", ("pool", "scale", "gu0"), shards)
    (x1_0, h2_0, yraw, dpool), (wd0,) = _fwd_pool_mixer(
        x, row(g_pre_mix, 0), wp, scale, row(g_post_mix, 0), row(g_pre_ffn, 0), job=gather("wd0"))
    (gs0, us0, f0, x2_0, h3_0), (wgate0, wproj0, wgu1) = _fwd_ffn(
        0, h2_0, x1_0, wgu0, wd0, row(g_post_ffn, 0), row(g_ple, 0), job=gather("gate0", "proj0", "gu1"))
    (x3_0, z0, pe0), (wkv, wq) = _fwd_ple(0, x2_0, h3_0, p[0], wgate0, wproj0, row(g_ple_post, 0),
                                          job=gather("kv", "q"))
    (hk, h1, q, kv), (wo,) = _fwd_qkv(x3_0, g_kv, row(g_pre_mix, 1), wkv, wq, job=gather("o"))
    front = ((ATT_BLOCK, 0), (0, 0))
    kpad = jnp.pad(kv[:, :KV_DIM], front)
    vpad = jnp.pad(kv[:, KV_DIM:], front)
    (attn,), (wd1,) = _fwd_attention(q, kpad, vpad, sinks, job=gather("wd1"))
    (y1, x1_1, h2_1), _ = _fwd_attn_out(attn, x3_0, wo, row(g_post_mix, 1), row(g_pre_ffn, 1))
    (gs1, us1, f1, x2_1, h3_1), (wgate1, wproj1) = _fwd_ffn(
        1, h2_1, x1_1, wgu1, wd1, row(g_post_ffn, 1), row(g_ple, 1), job=gather("gate1", "proj1"))
    (dx3_1, z1, pe1, loss), _ = _fwd_ple(1, x2_1, h3_1, p[1], wgate1, wproj1, row(g_ple_post, 1), target=tgt)

    produced, swapped, landed = {}, {}, {}

    def kind_of(name):
        return name.rstrip("0123_")

    def carry(swap=(), spread=()):
        jobs = []
        if swap:
            jobs.append(_SiblingSwap([(kind_of(n), produced[n]) for n in swap]))
        if spread:
            jobs.append(_ChipScatter([(kind_of(n), produced[n], swapped[n]) for n in spread]))
        return _Jobs(*jobs)

    def carried(jobs, outs, swap=(), spread=()):
        parts = jobs.split_outputs(outs)
        if swap:
            swapped.update(zip(swap, parts[0]))
        if spread:
            landed.update(zip(spread, parts[-1]))

    def hosted(call, *args, swap=(), spread=()):
        jobs = carry(swap, spread)
        outs, job_outs = call(*args, job=jobs)
        carried(jobs, job_outs, swap, spread)
        return outs

    def ffn_weight_grads(layer, h2, df, dg, du, a, hosts):
        for qtr in range(FF_PARTS):
            dgu, dwd = hosted(_bwd_ffn_dw, layer, qtr, h2, df, dg, du, a, **hosts[qtr])
            produced[f"gu{layer}_{qtr}"], produced[f"wd{layer}_{qtr}"] = dgu, dwd

    ffn_q = lambda layer, qtr: (f"gu{layer}_{qtr}", f"wd{layer}_{qtr}")

    dx2_1, df1, produced["gate1"], produced["proj1"], dg_ple_post1, dg_ple1, dg_post_ffn1 = hosted(
        _bwd_ple, 1, dx3_1, x2_1, z1, pe1, h3_1, p[1], f1, wgate1, row(g_ple_post, 1), row(g_ple, 1),
        row(g_post_ffn, 1))
    dh2_1, dg1, du1, a1 = hosted(_bwd_ffn_act, 1, df1, gs1, us1, wgu1, wd1, swap=("gate1", "proj1"))
    ffn_weight_grads(1, h2_1, df1, dg1, du1, a1, [dict(spread=("gate1", "proj1")), dict(swap=ffn_q(1, 0))])
    dx1_1, dattn, produced["o"], dg_pre_ffn1, dg_post_mix1 = hosted(
        _bwd_attn_out, dx2_1, dh2_1, x1_1, y1, attn, wo, row(g_pre_ffn, 1), row(g_post_mix, 1), swap=ffn_q(1, 1))
    dq, dkpad, dvpad, dsinks = hosted(_bwd_attention, q, dattn, kpad, vpad, sinks, spread=ffn_q(1, 0))
    dkv = jnp.concatenate([dkpad[ATT_BLOCK:], dvpad[ATT_BLOCK:]], axis=1).astype(BF16)
    dx3_0, produced["q"], produced["kv"], dg_pre_mix1, dg_kv = hosted(
        _bwd_qkv, dx1_1, dq, dkv, x3_0, h1, hk, wq, wkv, row(g_pre_mix, 1), g_kv, swap=("o",))
    dx2_0, df0, produced["gate0"], produced["proj0"], dg_ple_post0, dg_ple0, dg_post_ffn0 = hosted(
        _bwd_ple, 0, dx3_0, x2_0, z0, pe0, h3_0, p[0], f0, wgate0, row(g_ple_post, 0), row(g_ple, 0),
        row(g_post_ffn, 0), swap=("q", "kv"), spread=("gu1_1",))
    dh2_0, dg0, du0, a0 = hosted(_bwd_ffn_act, 0, df0, gs0, us0, wgu0, wd0,
                                 swap=("gate0", "proj0"), spread=("wd1_1", "o"))
    ffn_weight_grads(0, h2_0, df0, dg0, du0, a0, [
        dict(spread=("gate0", "proj0", "q", "kv")), dict(swap=ffn_q(0, 0))])
    grad_x, produced["pool"], dscale, dg_pre_ffn0, dg_post_mix0, dg_pre_mix0 = hosted(
        _bwd_pool_mixer, dx2_0, dh2_0, x1_0, x, yraw, dpool, wp, scale, row(g_pre_ffn, 0), row(g_post_mix, 0),
        row(g_pre_mix, 0), swap=ffn_q(0, 1), spread=ffn_q(0, 0))

    def update(name, n_col_blocks=1, pieces=None, swap=(), spread=()):
        w, m, v = weights[name]
        rows = w.size // w.shape[-1]
        flat = [landed[n].reshape(N_CHIPS, -1, landed[n].shape[-1]) for n in (pieces or [kind_short[name]])]
        outs = hosted(_adamw, name, w.reshape(rows, -1), m.reshape(rows, -1), v.reshape(rows, -1), flat,
                      n_col_blocks, swap=swap, spread=spread)
        return [o.reshape(w.shape) for o in outs]

    kind_short = {"w_q": "q", "w_kv": "kv", "w_o": "o", "pool_w": "pool"}
    upd = {}
    upd["w_ple_gate"] = update("w_ple_gate", pieces=("gate0", "gate1"), swap=("pool",), spread=ffn_q(0, 1))
    upd["w_ple_proj"] = update("w_ple_proj", pieces=("proj0", "proj1"), spread=("pool",))
    for name in ("w_q", "w_kv", "w_o", "pool_w"):
        upd[name] = update(name)
    upd["w_gu"] = update("w_gu", FF_PARTS,
                         pieces=[f"gu{layer}_{qtr}" for layer in range(2) for qtr in range(FF_PARTS)])
    upd["w_gu"] = [jnp.swapaxes(a, 1, 2) for a in upd["w_gu"]]
    upd["w_down"] = update("w_down", FF_PARTS,
                           pieces=[f"wd{layer}_{qtr}" for layer in range(2) for qtr in range(FF_PARTS)])

    lanes = lambda a: jnp.pad(a, ((0, 0), (0, D_MODEL - a.shape[1])))
    small = jnp.concatenate([
        dg_pre_mix0, dg_pre_mix1, dg_post_mix0, dg_post_mix1, dg_pre_ffn0, dg_pre_ffn1, dg_post_ffn0, dg_post_ffn1,
        dg_ple0, dg_ple1, dg_ple_post0, dg_ple_post1, dg_kv, dscale, lanes(dsinks[:, :N_HEADS]), lanes(loss)], axis=0)
    return grad_x, upd, small


def kernel(x, p, pre_mix_g, post_mix_g, pre_ffn_g, post_ffn_g, pool_w, pool_scale, kv_g, w_kv, w_q, sinks, w_o, w_gu, w_down, ple_g, w_ple_gate, w_ple_proj, ple_post_g, loss_target, m_pre_mix_g, m_post_mix_g, m_pre_ffn_g, m_post_ffn_g, m_pool_w, m_pool_scale, m_kv_g, m_w_kv, m_w_q, m_sinks, m_w_o, m_w_gu, m_w_down, m_ple_g, m_w_ple_gate, m_w_ple_proj, m_ple_post_g, v_pre_mix_g, v_post_mix_g, v_pre_ffn_g, v_post_ffn_g, v_pool_w, v_pool_scale, v_kv_g, v_w_kv, v_w_q, v_sinks, v_w_o, v_w_gu, v_w_down, v_ple_g, v_w_ple_gate, v_w_ple_proj, v_ple_post_g):
    shards = {"pool": pool_w[0].astype(BF16), "scale": pool_scale, "kv": w_kv.astype(BF16),
              "q": w_q[0].astype(BF16), "o": w_o[0].astype(BF16)}
    for layer in range(2):
        shards[f"gu{layer}"] = w_gu[layer].T.astype(BF16)
        shards[f"wd{layer}"] = w_down[layer].astype(BF16)
        shards[f"gate{layer}"] = w_ple_gate[layer].astype(BF16)
        shards[f"proj{layer}"] = w_ple_proj[layer].astype(BF16)
    gains = jnp.concatenate([pre_mix_g, post_mix_g, pre_ffn_g, post_ffn_g, ple_g, ple_post_g, kv_g[None, :]],
                            axis=0).reshape(-1, 1, D_MODEL)
    weights = {"pool_w": (pool_w, m_pool_w, v_pool_w), "w_kv": (w_kv, m_w_kv, v_w_kv), "w_q": (w_q, m_w_q, v_w_q),
               "w_o": (w_o, m_w_o, v_w_o), "w_down": (w_down, m_w_down, v_w_down),
               "w_gu": tuple(jnp.swapaxes(a, 1, 2) for a in (w_gu, m_w_gu, v_w_gu)),
               "w_ple_gate": (w_ple_gate, m_w_ple_gate, v_w_ple_gate),
               "w_ple_proj": (w_ple_proj, m_w_ple_proj, v_w_ple_proj)}
    grad_x, upd, small = _local_step(x[0], p[:, 0], loss_target[0], gains, sinks, shards, weights)

    small_params = {
        "pre_mix_g": (pre_mix_g, m_pre_mix_g, v_pre_mix_g), "post_mix_g": (post_mix_g, m_post_mix_g, v_post_mix_g),
        "pre_ffn_g": (pre_ffn_g, m_pre_ffn_g, v_pre_ffn_g), "post_ffn_g": (post_ffn_g, m_post_ffn_g, v_post_ffn_g),
        "ple_g": (ple_g, m_ple_g, v_ple_g), "ple_post_g": (ple_post_g, m_ple_post_g, v_ple_post_g),
        "kv_g": (kv_g[None, :], m_kv_g[None, :], v_kv_g[None, :]),
        "pool_scale": (pool_scale, m_pool_scale, v_pool_scale), "sinks": (sinks, m_sinks, v_sinks)}
    loss, small_upd = _small_all_reduce_adamw(small, small_params)
    small_upd["kv_g"] = [a[0] for a in small_upd["kv_g"]]
    upd.update(small_upd)

    names = ["pre_mix_g", "post_mix_g", "pre_ffn_g", "post_ffn_g", "pool_w", "pool_scale", "kv_g", "w_kv", "w_q",
             "sinks", "w_o", "w_gu", "w_down", "ple_g", "w_ple_gate", "w_ple_proj", "ple_post_g"]
    outs = [loss[0, 0], grad_x[None]]
    for kind in range(4):
        outs += [upd[n][kind] for n in names]
    return tuple(outs)
```

```python
import functools
import types

import jax
import jax.numpy as jnp
from jax import lax
from jax.experimental import pallas as pl
from jax.experimental.pallas import tpu as pltpu

F32 = jnp.float32
BF16 = jnp.bfloat16

N_DEV = 8
D_MODEL = 1024
N_POOL_GROUPS = 4
POOL_GROUP = 256
POOL_HALO = 16
HEAD_DIM = 64
N_HEADS = 16
N_KV_HEADS = 4
GQA_GROUP = 4
KV_DIM = N_KV_HEADS * HEAD_DIM
ATT_BLOCK = 128
D_FF = 2816
FF_CHUNKS = 4
FF_BLOCK = D_FF // FF_CHUNKS
WD_ROWS = D_FF // N_DEV
FF_PARTS = 2
FF_PART = D_MODEL // FF_PARTS
N_CHIPS = 4
PLE_DIM = 256
EPS = 1e-6
NEG_INF = -1e30
ATT_SCALE = HEAD_DIM ** -0.5

ADAM_LR = 0.001
ADAM_B1 = 0.9
ADAM_B2 = 0.999
ADAM_EPS = 1e-08
ADAM_WD = 0.01
ADAM_STEP = 10

ROW_TILE = 512
FFN_ROW_TILE = 512
FFN_SUB_TILES = 2
VMEM_BIG = 56 * 1024 * 1024
VMEM_MID = 48 * 1024 * 1024
HBM_PIN_ELEMS = 1024

SV_ROWS = 16
SV_PRE_MIX, SV_POST_MIX, SV_PRE_FFN, SV_POST_FFN, SV_PLE, SV_PLE_POST = 0, 2, 4, 6, 8, 10
SV_KV, SV_POOL_SCALE, SV_SINKS, SV_LOSS = 12, 13, 14, 15

MESH = pl.DeviceIdType.MESH
ANY = pl.BlockSpec(memory_space=pl.ANY)


def _dot(a, b):
    return jnp.dot(a, b, preferred_element_type=F32)


def _dot_nt(a, b):
    return lax.dot_general(a, b, (((1,), (1,)), ((), ())), preferred_element_type=F32)


def _dot_tn(a, b):
    return lax.dot_general(a, b, (((0,), (0,)), ((), ())), preferred_element_type=F32)


def _rstd(x):
    return lax.rsqrt(jnp.mean(x * x, axis=-1, keepdims=True) + EPS)


def _rms(x, g):
    return x * _rstd(x) * g


def _rms_bwd(x, g, dy):
    r = _rstd(x)
    n = x * r
    dn = dy * g
    dx = r * (dn - n * jnp.mean(dn * n, axis=-1, keepdims=True))
    dg = jnp.sum(dy * n, axis=0, keepdims=True)
    return dx, dg


def _sigmoid(x):
    return 1.0 / (1.0 + jnp.exp(-x))


def _acc(ref, val, first):
    @pl.when(first)
    def _():
        ref[...] = val

    @pl.when(jnp.logical_not(first))
    def _():
        ref[...] += val


def _pool_counts(row0, rows):
    t = row0 + lax.broadcasted_iota(jnp.int32, (rows, D_MODEL), 0) + 1
    grp = lax.broadcasted_iota(jnp.int32, (rows, D_MODEL), 1) // POOL_GROUP
    win = jnp.left_shift(2, grp)
    return jnp.minimum(t, win).astype(F32)


def _window_sums(ext, shift_of):
    outs = []
    s = ext
    for gi in range(N_POOL_GROUPS):
        s = s + pltpu.roll(s, shift_of(1 << gi), axis=0)
        outs.append(s[:, :POOL_GROUP])
        s = s[:, POOL_GROUP:]
    return jnp.concatenate(outs, axis=1)


def _cparams(n_axes, vmem):
    return pltpu.CompilerParams(dimension_semantics=("arbitrary",) * n_axes, vmem_limit_bytes=vmem)


def _row_spec(cols, tm=ROW_TILE):
    return pl.BlockSpec((tm, cols), lambda i: (i, 0))


def _full_spec(shape):
    zeros = (0,) * len(shape)
    return pl.BlockSpec(shape, lambda *_: zeros)


def _vec_spec():
    return _full_spec((1, D_MODEL))


class _Gain:
    def __init__(self, stacked, layer):
        self.stacked, self.layer = stacked, layer

    def spec(self):
        layer = self.layer
        return pl.BlockSpec((None, 1, D_MODEL), lambda *_: (layer, 0, 0))


def _in_hbm(a):
    return pltpu.with_memory_space_constraint(a, pltpu.HBM) if a.size >= HBM_PIN_ELEMS else a


def _out_in_hbm(s):
    return pltpu.HBM(s.shape, s.dtype) if s.size >= HBM_PIN_ELEMS else s


def _launch(body, *, name, grid, in_specs, out_specs, out_shape, args, scratch_shapes=(), vmem=VMEM_MID, job=None):
    in_specs = [a.spec() if isinstance(a, _Gain) else s for s, a in zip(in_specs, args)]
    args = [_in_hbm(a.stacked if isinstance(a, _Gain) else a) for a in args]
    n_in, n_out, n_scr = len(args), len(out_shape), len(scratch_shapes)
    j_args, j_out, j_scr = ([], [], []) if job is None else ([_in_hbm(a) for a in job.args], job.out_shape, job.scratch)

    def run(*refs):
        groups, at = [], 0
        for n in (n_in, len(j_args), n_out, len(j_out), n_scr, len(j_scr)):
            groups.append(refs[at:at + n])
            at += n
        ins, j_ins, outs, j_outs, scr, j_sems = groups
        if job is None:
            body(*ins, *outs, *scr)
        elif not grid:
            job.start(j_ins, j_outs, j_sems)
            job.mid(j_ins, j_outs, j_sems)
            body(*ins, *outs, *scr)
            job.finish(j_ins, j_outs, j_sems)
        else:
            ids = [pl.program_id(a) for a in range(len(grid))]
            first = functools.reduce(jnp.logical_and, [i == 0 for i in ids])
            half = functools.reduce(jnp.logical_and, [ids[0] == grid[0] // 2] + [i == 0 for i in ids[1:]])
            last = functools.reduce(jnp.logical_and, [i == g - 1 for i, g in zip(ids, grid)])
            pl.when(first)(lambda: job.start(j_ins, j_outs, j_sems))
            pl.when(half)(lambda: job.mid(j_ins, j_outs, j_sems))
            body(*ins, *outs, *scr)
            pl.when(last)(lambda: job.finish(j_ins, j_outs, j_sems))

    res = pl.pallas_call(
        run, name=name, grid=grid,
        in_specs=list(in_specs) + [ANY] * len(j_args), out_specs=list(out_specs) + [ANY] * len(j_out),
        out_shape=[_out_in_hbm(s) for s in list(out_shape) + list(j_out)],
        scratch_shapes=list(scratch_shapes) + list(j_scr),
        compiler_params=_cparams(len(grid), vmem),
    )(*args, *j_args)
    return res[:n_out], res[n_out:]


def _fwd_pool_mixer(x, g_pre, wp, scale, g_post, g_ffn, job=None):
    T = x.shape[0]
    tm = ROW_TILE
    nt = T // tm

    def body(x_ref, gpre_ref, wp_ref, sc_ref, gpost_ref, gffn_ref, x1_ref, h2_ref, yraw_ref, d_ref, carry):
        i = pl.program_id(0)

        @pl.when(i == 0)
        def _():
            carry[...] = jnp.zeros_like(carry)

        xv = x_ref[...]
        h = _rms(xv, gpre_ref[...])
        ext = jnp.concatenate([carry[...], h], axis=0)
        carry[...] = h[tm - POOL_HALO:, :]
        sums = _window_sums(ext, lambda k: k)[POOL_HALO:, :]
        d = sums / _pool_counts(i * tm, tm) - h
        db = d.astype(BF16)
        d_ref[...] = db
        yraw = jnp.concatenate(
            [_dot(db[:, g * POOL_GROUP:(g + 1) * POOL_GROUP], wp_ref[g]) for g in range(N_POOL_GROUPS)], axis=1)
        yraw_ref[...] = yraw
        x1 = xv + _rms(yraw * sc_ref[...], gpost_ref[...])
        x1_ref[...] = x1
        h2_ref[...] = _rms(x1, gffn_ref[...]).astype(BF16)

    return _launch(
        body, name="fwd_pool_mixer", grid=(nt,),
        in_specs=[_row_spec(D_MODEL), _vec_spec(), _full_spec((N_POOL_GROUPS, POOL_GROUP, POOL_GROUP)), _vec_spec(),
                  _vec_spec(), _vec_spec()],
        out_specs=[_row_spec(D_MODEL)] * 4,
        out_shape=[jax.ShapeDtypeStruct((T, D_MODEL), F32), jax.ShapeDtypeStruct((T, D_MODEL), BF16),
                   jax.ShapeDtypeStruct((T, D_MODEL), F32), jax.ShapeDtypeStruct((T, D_MODEL), BF16)],
        scratch_shapes=[pltpu.VMEM((POOL_HALO, D_MODEL), F32)],
        args=(x, g_pre, wp, scale, g_post, g_ffn), job=job)


def _fwd_ffn(layer, h2, x1, wgu, wd, g_post, g_ple, job=None):
    T = h2.shape[0]
    tm = min(FFN_ROW_TILE, T)
    nt = T // tm
    sub = tm // FFN_SUB_TILES
    last = FF_CHUNKS - 1

    def body(h2_ref, x1_ref, wgu_ref, wd_ref, gpost_ref, gple_ref, gs_ref, us_ref, f_ref, x2_ref, h3_ref, acc):
        k = pl.program_id(0)
        i = pl.program_id(1)
        rows = pl.ds(pl.multiple_of(i * tm, tm), tm)
        parts = []
        for s in range(FFN_SUB_TILES):
            r = pl.ds(s * sub, sub)
            h = h2_ref[r, :]
            g = _dot_nt(h, wgu_ref[0])
            u = _dot_nt(h, wgu_ref[1])
            gs_ref[r, :] = g.astype(BF16)
            us_ref[r, :] = u.astype(BF16)
            a = (g * _sigmoid(g) * u).astype(BF16)
            parts.append(_dot(a, wd_ref[...]))
        part = jnp.concatenate(parts, axis=0)

        @pl.when(k == 0)
        def _():
            acc[rows, :] = part

        @pl.when(jnp.logical_and(k > 0, k < last))
        def _():
            acc[rows, :] += part

        @pl.when(k == last)
        def _():
            f = acc[rows, :] + part
            f_ref[...] = f
            x2 = x1_ref[...] + _rms(f, gpost_ref[...])
            x2_ref[...] = x2
            h3_ref[...] = _rms(x2, gple_ref[...]).astype(BF16)

    def late(k, i):
        return (jnp.where(k == last, i, 0), 0)

    return _launch(
        body, name=f"fwd_ffn{layer}", grid=(FF_CHUNKS, nt),
        in_specs=[pl.BlockSpec((tm, D_MODEL), lambda k, i: (i, 0)),
                  pl.BlockSpec((tm, D_MODEL), late),
                  pl.BlockSpec((None, 2, FF_BLOCK, D_MODEL), lambda k, i: (k, 0, 0, 0)),
                  pl.BlockSpec((FF_BLOCK, D_MODEL), lambda k, i: (k, 0)),
                  pl.BlockSpec((1, D_MODEL), lambda k, i: (0, 0)),
                  pl.BlockSpec((1, D_MODEL), lambda k, i: (0, 0))],
        out_specs=[pl.BlockSpec((None, tm, FF_BLOCK), lambda k, i: (k, i, 0)),
                   pl.BlockSpec((None, tm, FF_BLOCK), lambda k, i: (k, i, 0)),
                   pl.BlockSpec((tm, D_MODEL), late),
                   pl.BlockSpec((tm, D_MODEL), late),
                   pl.BlockSpec((tm, D_MODEL), late)],
        out_shape=[jax.ShapeDtypeStruct((FF_CHUNKS, T, FF_BLOCK), BF16),
                   jax.ShapeDtypeStruct((FF_CHUNKS, T, FF_BLOCK), BF16),
                   jax.ShapeDtypeStruct((T, D_MODEL), F32),
                   jax.ShapeDtypeStruct((T, D_MODEL), F32),
                   jax.ShapeDtypeStruct((T, D_MODEL), BF16)],
        scratch_shapes=[pltpu.VMEM((T, D_MODEL), F32)],
        args=(h2, x1, wgu, wd, g_post, g_ple), vmem=VMEM_BIG, job=job)


def _fwd_ple(layer, x2, h3, p, wgate, wproj, g_post, target=None, job=None):
    T = x2.shape[0]
    tm = ROW_TILE
    nt = T // tm
    with_loss = target is not None

    def body(*refs):
        if with_loss:
            x2_ref, h3_ref, p_ref, wg_ref, wp_ref, gpost_ref, tgt_ref, out_ref, z_ref, pe_ref, loss_ref = refs
        else:
            x2_ref, h3_ref, p_ref, wg_ref, wp_ref, gpost_ref, out_ref, z_ref, pe_ref = refs
        z = _dot(h3_ref[...], wg_ref[...])
        pe = _dot(p_ref[...].astype(BF16), wp_ref[...])
        z_ref[...] = z
        pe_ref[...] = pe
        x3 = x2_ref[...] + _rms(pe * _sigmoid(z), gpost_ref[...])
        if with_loss:
            err = x3 - tgt_ref[...]
            out_ref[...] = err * (1.0 / D_MODEL)
            part = 0.5 * jnp.sum(jnp.mean(err * err, axis=-1, keepdims=True), axis=0, keepdims=True)
            _acc(loss_ref, part, pl.program_id(0) == 0)
        else:
            out_ref[...] = x3

    in_specs = [_row_spec(D_MODEL), _row_spec(D_MODEL), _row_spec(PLE_DIM), _full_spec((D_MODEL, D_MODEL)),
                _full_spec((PLE_DIM, D_MODEL)), _vec_spec()]
    out_specs = [_row_spec(D_MODEL)] * 3
    out_shape = [jax.ShapeDtypeStruct((T, D_MODEL), F32)] * 3
    args = [x2, h3, p, wgate, wproj, g_post]
    if with_loss:
        in_specs.append(_row_spec(D_MODEL))
        out_specs.append(_full_spec((1, 1)))
        out_shape.append(jax.ShapeDtypeStruct((1, 1), F32))
        args.append(target)
    return _launch(body, name=f"fwd_ple{layer}", grid=(nt,), in_specs=in_specs, out_specs=out_specs,
                   out_shape=out_shape, args=args, job=job)


def _fwd_qkv(x3, g_kv, g_mix, wkv, wq, job=None):
    T = x3.shape[0]
    nt = T // ROW_TILE

    def body(x_ref, gkv_ref, gmix_ref, wkv_ref, wq_ref, hk_ref, h1_ref, q_ref, kv_ref):
        xv = x_ref[...]
        r = _rstd(xv)
        hk = (xv * r * gkv_ref[...]).astype(BF16)
        h1 = (xv * r * gmix_ref[...]).astype(BF16)
        hk_ref[...] = hk
        h1_ref[...] = h1
        kv_ref[...] = _dot(hk, wkv_ref[...]).astype(BF16)
        q_ref[...] = _dot(h1, wq_ref[...]).astype(BF16)

    return _launch(
        body, name="fwd_qkv", grid=(nt,),
        in_specs=[_row_spec(D_MODEL), _vec_spec(), _vec_spec(), _full_spec((D_MODEL, 2 * KV_DIM)),
                  _full_spec((D_MODEL, D_MODEL))],
        out_specs=[_row_spec(D_MODEL), _row_spec(D_MODEL), _row_spec(D_MODEL), _row_spec(2 * KV_DIM)],
        out_shape=[jax.ShapeDtypeStruct((T, D_MODEL), BF16)] * 3 + [jax.ShapeDtypeStruct((T, 2 * KV_DIM), BF16)],
        args=(x3, g_kv, g_mix, wkv, wq), job=job)


def _alibi_slope(h):
    return 2.0 ** (-8.0 * (h + 1) / N_HEADS)


ATT_SUB = 32
ATT_GROUP_ROWS = GQA_GROUP * ATT_BLOCK


def _att_mask(n, row0):
    qi = lax.broadcasted_iota(jnp.int32, (ATT_SUB, 2 * ATT_BLOCK), 0) + row0
    si = lax.broadcasted_iota(jnp.int32, (ATT_SUB, 2 * ATT_BLOCK), 1)
    rel = ATT_BLOCK + qi - si
    valid = (rel >= 0) & (rel < ATT_BLOCK) & ((si >= ATT_BLOCK) | (n > 0))
    return rel.astype(F32), valid


def _att_probs(raw, relf, valid, slope, sink):
    s = jnp.where(valid, raw * ATT_SCALE - slope * relf, NEG_INF)
    m = jnp.maximum(jnp.max(s, axis=-1, keepdims=True), sink)
    e = jnp.exp(s - m)
    es = jnp.exp(sink - m)
    inv = 1.0 / (jnp.sum(e, axis=-1, keepdims=True) + es)
    return e * inv, es * inv


def _stack_heads(ref, kh):
    first = kh * GQA_GROUP
    return jnp.concatenate([ref[:, (first + g) * HEAD_DIM:(first + g + 1) * HEAD_DIM] for g in range(GQA_GROUP)], axis=0)


def _unstack_heads(stacked):
    return [stacked[g * ATT_BLOCK:(g + 1) * ATT_BLOCK, :] for g in range(GQA_GROUP)]


def _fwd_attention(q, kpad, vpad, sinks, job=None):
    T = q.shape[0]
    nb = T // ATT_BLOCK

    def body(q_ref, k_ref, v_ref, sink_ref, o_ref, s_scr, p_scr):
        n = pl.program_id(0)
        start = pl.multiple_of(n * ATT_BLOCK, ATT_BLOCK)
        kw = k_ref[pl.ds(start, 2 * ATT_BLOCK), :]
        vw = v_ref[pl.ds(start, 2 * ATT_BLOCK), :]
        outs = []
        for kh in range(N_KV_HEADS):
            kk = kw[:, kh * HEAD_DIM:(kh + 1) * HEAD_DIM]
            vv = vw[:, kh * HEAD_DIM:(kh + 1) * HEAD_DIM]
            s_scr[...] = _dot_nt(_stack_heads(q_ref, kh), kk)
            for g in range(GQA_GROUP):
                h = kh * GQA_GROUP + g
                for row0 in range(0, ATT_BLOCK, ATT_SUB):
                    rows = pl.ds(g * ATT_BLOCK + row0, ATT_SUB)
                    relf, valid = _att_mask(n, row0)
                    pr, _ = _att_probs(s_scr[rows, :], relf, valid, _alibi_slope(h), sink_ref[0, h])
                    p_scr[rows, :] = pr.astype(BF16)
            outs += _unstack_heads(_dot(p_scr[...], vv))
        o_ref[...] = jnp.concatenate(outs, axis=1).astype(BF16)

    return _launch(
        body, name="fwd_attention", grid=(nb,),
        in_specs=[_row_spec(D_MODEL, ATT_BLOCK), _full_spec((T + ATT_BLOCK, KV_DIM)), _full_spec((T + ATT_BLOCK, KV_DIM)),
                  pl.BlockSpec(memory_space=pltpu.SMEM)],
        out_specs=[_row_spec(D_MODEL, ATT_BLOCK)],
        out_shape=[jax.ShapeDtypeStruct((T, D_MODEL), BF16)],
        scratch_shapes=[pltpu.VMEM((ATT_GROUP_ROWS, 2 * ATT_BLOCK), F32), pltpu.VMEM((ATT_GROUP_ROWS, 2 * ATT_BLOCK), BF16)],
        args=(q, kpad, vpad, sinks), job=job)


def _fwd_attn_out(attn, x, wo, g_post, g_ffn, job=None):
    T = x.shape[0]
    nt = T // ROW_TILE

    def body(a_ref, x_ref, wo_ref, gpost_ref, gffn_ref, y_ref, x1_ref, h2_ref):
        y = _dot(a_ref[...], wo_ref[...])
        y_ref[...] = y
        x1 = x_ref[...] + _rms(y, gpost_ref[...])
        x1_ref[...] = x1
        h2_ref[...] = _rms(x1, gffn_ref[...]).astype(BF16)

    return _launch(
        body, name="fwd_attn_out", grid=(nt,),
        in_specs=[_row_spec(D_MODEL), _row_spec(D_MODEL), _full_spec((D_MODEL, D_MODEL)), _vec_spec(), _vec_spec()],
        out_specs=[_row_spec(D_MODEL)] * 3,
        out_shape=[jax.ShapeDtypeStruct((T, D_MODEL), F32), jax.ShapeDtypeStruct((T, D_MODEL), F32),
                   jax.ShapeDtypeStruct((T, D_MODEL), BF16)],
        args=(attn, x, wo, g_post, g_ffn), job=job)


def _bwd_ple(layer, dx3, x2, z, pe, h3, p, f, wgate, g_ple_post, g_ple, g_post_ffn, job=None):
    T = x2.shape[0]
    tm = ROW_TILE
    nt = T // tm

    def body(dx3_ref, x2_ref, z_ref, pe_ref, h3_ref, p_ref, f_ref, wg_ref, gpp_ref, gp_ref, gpf_ref,
             dx2_ref, df_ref, dwg_ref, dwp_ref, dgpp_ref, dgp_ref, dgpf_ref, acc_g, acc_p):
        i = pl.program_id(0)
        first = i == 0
        dx3v = dx3_ref[...]
        gate = _sigmoid(z_ref[...])
        pev = pe_ref[...]
        de, dgpp = _rms_bwd(pev * gate, gpp_ref[...], dx3v)
        dpe = (de * gate).astype(BF16)
        dz = (de * pev * gate * (1.0 - gate)).astype(BF16)
        _acc(acc_p, _dot_tn(p_ref[...].astype(BF16), dpe), first)
        _acc(acc_g, _dot_tn(h3_ref[...], dz), first)
        dh3 = _dot_nt(dz, wg_ref[...])
        dxn, dgp = _rms_bwd(x2_ref[...], gp_ref[...], dh3)
        dx2 = dx3v + dxn
        dx2_ref[...] = dx2
        df, dgpf = _rms_bwd(f_ref[...], gpf_ref[...], dx2)
        df_ref[...] = df.astype(BF16)
        _acc(dgpp_ref, dgpp, first)
        _acc(dgp_ref, dgp, first)
        _acc(dgpf_ref, dgpf, first)

        @pl.when(i == nt - 1)
        def _():
            dwg_ref[...] = acc_g[...].astype(BF16)
            dwp_ref[...] = acc_p[...].astype(BF16)

    return _launch(
        body, name=f"bwd_ple{layer}", grid=(nt,),
        in_specs=[_row_spec(D_MODEL)] * 5 + [_row_spec(PLE_DIM), _row_spec(D_MODEL), _full_spec((D_MODEL, D_MODEL)),
                  _vec_spec(), _vec_spec(), _vec_spec()],
        out_specs=[_row_spec(D_MODEL), _row_spec(D_MODEL), _full_spec((D_MODEL, D_MODEL)), _full_spec((PLE_DIM, D_MODEL)),
                   _vec_spec(), _vec_spec(), _vec_spec()],
        out_shape=[jax.ShapeDtypeStruct((T, D_MODEL), F32), jax.ShapeDtypeStruct((T, D_MODEL), BF16),
                   jax.ShapeDtypeStruct((D_MODEL, D_MODEL), BF16), jax.ShapeDtypeStruct((PLE_DIM, D_MODEL), BF16)]
                  + [jax.ShapeDtypeStruct((1, D_MODEL), F32)] * 3,
        scratch_shapes=[pltpu.VMEM((D_MODEL, D_MODEL), F32), pltpu.VMEM((PLE_DIM, D_MODEL), F32)],
        args=(dx3, x2, z, pe, h3, p, f, wgate, g_ple_post, g_ple, g_post_ffn), vmem=VMEM_BIG, job=job)


def _bwd_ffn_act(layer, df, gs, us, wgu, wd, job=None):
    T = df.shape[0]
    tm = min(FFN_ROW_TILE, T)
    nt = T // tm
    sub = tm // FFN_SUB_TILES
    last = FF_CHUNKS - 1

    def body(df_ref, gs_ref, us_ref, wgu_ref, wd_ref, dh_ref, dg_ref, du_ref, a_ref, acc_h):
        k = pl.program_id(0)
        i = pl.program_id(1)
        rows = pl.ds(pl.multiple_of(i * tm, tm), tm)
        dhs = []
        for s in range(FFN_SUB_TILES):
            r = pl.ds(s * sub, sub)
            g = gs_ref[r, :].astype(F32)
            u = us_ref[r, :].astype(F32)
            sg = _sigmoid(g)
            silu = g * sg
            a_ref[r, :] = (silu * u).astype(BF16)
            da = _dot_nt(df_ref[r, :], wd_ref[...])
            dg = (da * u * (sg * (1.0 + g * (1.0 - sg)))).astype(BF16)
            du = (da * silu).astype(BF16)
            dg_ref[r, :] = dg
            du_ref[r, :] = du
            dhs.append(_dot(dg, wgu_ref[0]) + _dot(du, wgu_ref[1]))
        dh = jnp.concatenate(dhs, axis=0)

        @pl.when(k == 0)
        def _():
            acc_h[rows, :] = dh

        @pl.when(jnp.logical_and(k > 0, k < last))
        def _():
            acc_h[rows, :] += dh

        @pl.when(k == last)
        def _():
            dh_ref[...] = acc_h[rows, :] + dh

    chunk_rows = pl.BlockSpec((None, tm, FF_BLOCK), lambda k, i: (k, i, 0))
    saved = jax.ShapeDtypeStruct((FF_CHUNKS, T, FF_BLOCK), BF16)
    return _launch(
        body, name=f"bwd_ffn_act{layer}", grid=(FF_CHUNKS, nt),
        in_specs=[pl.BlockSpec((tm, D_MODEL), lambda k, i: (i, 0)), chunk_rows, chunk_rows,
                  pl.BlockSpec((None, 2, FF_BLOCK, D_MODEL), lambda k, i: (k, 0, 0, 0)),
                  pl.BlockSpec((FF_BLOCK, D_MODEL), lambda k, i: (k, 0))],
        out_specs=[pl.BlockSpec((tm, D_MODEL), lambda k, i: (jnp.where(k == last, i, 0), 0)),
                   chunk_rows, chunk_rows, chunk_rows],
        out_shape=[jax.ShapeDtypeStruct((T, D_MODEL), F32), saved, saved, saved],
        scratch_shapes=[pltpu.VMEM((T, D_MODEL), F32)],
        args=(df, gs, us, wgu, wd), vmem=VMEM_BIG, job=job)


def _bwd_ffn_dw(layer, q, h2, df, dg, du, a, job=None):
    T = h2.shape[0]

    def body(h_ref, df_ref, dg_ref, du_ref, a_ref, dgu_ref, dwd_ref):
        h = h_ref[...]
        dgu_ref[0] = _dot_tn(dg_ref[...], h).astype(BF16)
        dgu_ref[1] = _dot_tn(du_ref[...], h).astype(BF16)
        dwd_ref[...] = _dot_tn(a_ref[...], df_ref[...]).astype(BF16)

    cols = pl.BlockSpec((T, FF_PART), lambda k: (0, q))
    chunk = pl.BlockSpec((None, T, FF_BLOCK), lambda k: (k, 0, 0))
    return _launch(
        body, name=f"bwd_ffn_dw{layer}_{q}", grid=(FF_CHUNKS,),
        in_specs=[cols, cols, chunk, chunk, chunk],
        out_specs=[pl.BlockSpec((None, 2, FF_BLOCK, FF_PART), lambda k: (k, 0, 0, 0)),
                   pl.BlockSpec((FF_BLOCK, FF_PART), lambda k: (k, 0))],
        out_shape=[jax.ShapeDtypeStruct((FF_CHUNKS, 2, FF_BLOCK, FF_PART), BF16),
                   jax.ShapeDtypeStruct((D_FF, FF_PART), BF16)],
        args=(h2, df, dg, du, a), vmem=VMEM_BIG, job=job)


def _bwd_attn_out(dx2, dh2, x1, y, attn, wo, g_ffn, g_post, job=None):
    T = x1.shape[0]
    nt = T // ROW_TILE

    def body(dx2_ref, dh2_ref, x1_ref, y_ref, a_ref, wo_ref, gffn_ref, gpost_ref,
             dx1_ref, da_ref, dwo_ref, dgf_ref, dgp_ref, acc):
        i = pl.program_id(0)
        first = i == 0
        dxn, dgf = _rms_bwd(x1_ref[...], gffn_ref[...], dh2_ref[...])
        dx1 = dx2_ref[...] + dxn
        dx1_ref[...] = dx1
        dy, dgp = _rms_bwd(y_ref[...], gpost_ref[...], dx1)
        dyb = dy.astype(BF16)
        da_ref[...] = _dot_nt(dyb, wo_ref[...]).astype(BF16)
        _acc(acc, _dot_tn(a_ref[...], dyb), first)
        _acc(dgf_ref, dgf, first)
        _acc(dgp_ref, dgp, first)

        @pl.when(i == nt - 1)
        def _():
            dwo_ref[...] = acc[...].astype(BF16)

    return _launch(
        body, name="bwd_attn_out", grid=(nt,),
        in_specs=[_row_spec(D_MODEL)] * 5 + [_full_spec((D_MODEL, D_MODEL)), _vec_spec(), _vec_spec()],
        out_specs=[_row_spec(D_MODEL), _row_spec(D_MODEL), _full_spec((D_MODEL, D_MODEL)), _vec_spec(), _vec_spec()],
        out_shape=[jax.ShapeDtypeStruct((T, D_MODEL), F32), jax.ShapeDtypeStruct((T, D_MODEL), BF16),
                   jax.ShapeDtypeStruct((D_MODEL, D_MODEL), BF16)] + [jax.ShapeDtypeStruct((1, D_MODEL), F32)] * 2,
        scratch_shapes=[pltpu.VMEM((D_MODEL, D_MODEL), F32)],
        args=(dx2, dh2, x1, y, attn, wo, g_ffn, g_post), job=job)


def _bwd_attention(q, dattn, kpad, vpad, sinks, job=None):
    T = q.shape[0]
    nb = T // ATT_BLOCK

    def body(q_ref, do_ref, k_ref, v_ref, sink_ref, dq_ref, dk_ref, dv_ref, ds_ref, s_scr, dp_scr, p_scr, dsb_scr):
        n = pl.program_id(0)

        @pl.when(n == 0)
        def _():
            dk_ref[...] = jnp.zeros_like(dk_ref)
            dv_ref[...] = jnp.zeros_like(dv_ref)
            ds_ref[...] = jnp.zeros_like(ds_ref)

        start = pl.multiple_of(n * ATT_BLOCK, ATT_BLOCK)
        win = pl.ds(start, 2 * ATT_BLOCK)
        kw = k_ref[win, :]
        vw = v_ref[win, :]
        lane = lax.broadcasted_iota(jnp.int32, (1, ATT_BLOCK), 1)
        dsink = jnp.zeros((1, ATT_BLOCK), F32)
        dqs, dks, dvs = [], [], []
        for kh in range(N_KV_HEADS):
            kk = kw[:, kh * HEAD_DIM:(kh + 1) * HEAD_DIM]
            vv = vw[:, kh * HEAD_DIM:(kh + 1) * HEAD_DIM]
            qs = _stack_heads(q_ref, kh)
            dos = _stack_heads(do_ref, kh)
            s_scr[...] = _dot_nt(qs, kk)
            dp_scr[...] = _dot_nt(dos, vv)
            for g in range(GQA_GROUP):
                h = kh * GQA_GROUP + g
                dsink_h = jnp.zeros((1, 1), F32)
                for row0 in range(0, ATT_BLOCK, ATT_SUB):
                    rows = pl.ds(g * ATT_BLOCK + row0, ATT_SUB)
                    relf, valid = _att_mask(n, row0)
                    pr, ps = _att_probs(s_scr[rows, :], relf, valid, _alibi_slope(h), sink_ref[0, h])
                    dp = dp_scr[rows, :]
                    delta = jnp.sum(pr * dp, axis=-1, keepdims=True)
                    dsb_scr[rows, :] = (pr * (dp - delta) * ATT_SCALE).astype(BF16)
                    p_scr[rows, :] = pr.astype(BF16)
                    dsink_h = dsink_h - jnp.sum(ps * delta, axis=0, keepdims=True)
                dsink = dsink + jnp.where(lane == h, dsink_h, 0.0)
            dsb = dsb_scr[...]
            dqs += _unstack_heads(_dot(dsb, kk))
            dks.append(_dot_tn(dsb, qs))
            dvs.append(_dot_tn(p_scr[...], dos))
        dq_ref[...] = jnp.concatenate(dqs, axis=1).astype(BF16)
        dk_ref[win, :] += jnp.concatenate(dks, axis=1)
        dv_ref[win, :] += jnp.concatenate(dvs, axis=1)
        ds_ref[...] += dsink

    return _launch(
        body, name="bwd_attention", grid=(nb,),
        in_specs=[_row_spec(D_MODEL, ATT_BLOCK), _row_spec(D_MODEL, ATT_BLOCK), _full_spec((T + ATT_BLOCK, KV_DIM)),
                  _full_spec((T + ATT_BLOCK, KV_DIM)), pl.BlockSpec(memory_space=pltpu.SMEM)],
        out_specs=[_row_spec(D_MODEL, ATT_BLOCK), _full_spec((T + ATT_BLOCK, KV_DIM)), _full_spec((T + ATT_BLOCK, KV_DIM)),
                   _full_spec((1, ATT_BLOCK))],
        out_shape=[jax.ShapeDtypeStruct((T, D_MODEL), BF16), jax.ShapeDtypeStruct((T + ATT_BLOCK, KV_DIM), F32),
                   jax.ShapeDtypeStruct((T + ATT_BLOCK, KV_DIM), F32), jax.ShapeDtypeStruct((1, ATT_BLOCK), F32)],
        scratch_shapes=[pltpu.VMEM((ATT_GROUP_ROWS, 2 * ATT_BLOCK), F32)] * 2
                       + [pltpu.VMEM((ATT_GROUP_ROWS, 2 * ATT_BLOCK), BF16)] * 2,
        args=(q, dattn, kpad, vpad, sinks), vmem=VMEM_BIG, job=job)


def _bwd_qkv(dxres, dq, dkv, x3, h1, hk, wq, wkv, g_mix, g_kv, job=None):
    T = x3.shape[0]
    nt = T // ROW_TILE

    def body(dxr_ref, dq_ref, dkv_ref, x_ref, h1_ref, hk_ref, wq_ref, wkv_ref, gmix_ref, gkv_ref,
             dx_ref, dwq_ref, dwkv_ref, dgm_ref, dgk_ref, acc_q, acc_kv):
        i = pl.program_id(0)
        first = i == 0
        dqv = dq_ref[...]
        dkvv = dkv_ref[...]
        xv = x_ref[...]
        d1, dgm = _rms_bwd(xv, gmix_ref[...], _dot_nt(dqv, wq_ref[...]))
        d2, dgk = _rms_bwd(xv, gkv_ref[...], _dot_nt(dkvv, wkv_ref[...]))
        dx_ref[...] = dxr_ref[...] + d1 + d2
        _acc(acc_q, _dot_tn(h1_ref[...], dqv), first)
        _acc(acc_kv, _dot_tn(hk_ref[...], dkvv), first)
        _acc(dgm_ref, dgm, first)
        _acc(dgk_ref, dgk, first)

        @pl.when(i == nt - 1)
        def _():
            dwq_ref[...] = acc_q[...].astype(BF16)
            dwkv_ref[...] = acc_kv[...].astype(BF16)

    return _launch(
        body, name="bwd_qkv", grid=(nt,),
        in_specs=[_row_spec(D_MODEL), _row_spec(D_MODEL), _row_spec(2 * KV_DIM), _row_spec(D_MODEL), _row_spec(D_MODEL),
                  _row_spec(D_MODEL), _full_spec((D_MODEL, D_MODEL)), _full_spec((D_MODEL, 2 * KV_DIM)), _vec_spec(),
                  _vec_spec()],
        out_specs=[_row_spec(D_MODEL), _full_spec((D_MODEL, D_MODEL)), _full_spec((D_MODEL, 2 * KV_DIM)), _vec_spec(),
                   _vec_spec()],
        out_shape=[jax.ShapeDtypeStruct((T, D_MODEL), F32), jax.ShapeDtypeStruct((D_MODEL, D_MODEL), BF16),
                   jax.ShapeDtypeStruct((D_MODEL, 2 * KV_DIM), BF16)] + [jax.ShapeDtypeStruct((1, D_MODEL), F32)] * 2,
        scratch_shapes=[pltpu.VMEM((D_MODEL, D_MODEL), F32), pltpu.VMEM((D_MODEL, 2 * KV_DIM), F32)],
        args=(dxres, dq, dkv, x3, h1, hk, wq, wkv, g_mix, g_kv), job=job)


def _bwd_pool_mixer(dx2, dh2, x1, x, yraw, d, wp, scale, g_ffn, g_post, g_pre, job=None):
    T = x.shape[0]
    tm = ROW_TILE
    nt = T // tm

    def body(dx2_ref, dh2_ref, x1_ref, x_ref, yraw_ref, d_ref, wp_ref, sc_ref, gffn_ref, gpost_ref, gpre_ref,
             dx_ref, dwp_ref, dsc_ref, dgf_ref, dgp_ref, dgm_ref, carry, acc):
        i = pl.program_id(0)
        first = i == 0
        tile = nt - 1 - i

        @pl.when(first)
        def _():
            carry[...] = jnp.zeros_like(carry)

        dxn, dgf = _rms_bwd(x1_ref[...], gffn_ref[...], dh2_ref[...])
        dx1 = dx2_ref[...] + dxn
        yraw = yraw_ref[...]
        sc = sc_ref[...]
        dy, dgp = _rms_bwd(yraw * sc, gpost_ref[...], dx1)
        dsc = jnp.sum(dy * yraw, axis=0, keepdims=True)
        dyb = (dy * sc).astype(BF16)
        dv = d_ref[...]
        dds = []
        for g in range(N_POOL_GROUPS):
            cols = slice(g * POOL_GROUP, (g + 1) * POOL_GROUP)
            dds.append(_dot_nt(dyb[:, cols], wp_ref[g]))
            _acc(acc.at[g], _dot_tn(dv[:, cols], dyb[:, cols]), first)
        dd = jnp.concatenate(dds, axis=1)
        e = dd / _pool_counts(tile * tm, tm)
        ext = jnp.concatenate([e, carry[...]], axis=0)
        carry[...] = e[:POOL_HALO, :]
        sums = _window_sums(ext, lambda k: tm + POOL_HALO - k)[:tm, :]
        dxm, dgm = _rms_bwd(x_ref[...], gpre_ref[...], sums - dd)
        dx_ref[...] = dx1 + dxm
        _acc(dsc_ref, dsc, first)
        _acc(dgf_ref, dgf, first)
        _acc(dgp_ref, dgp, first)
        _acc(dgm_ref, dgm, first)

        @pl.when(i == nt - 1)
        def _():
            dwp_ref[...] = acc[...].astype(BF16)

    rev = pl.BlockSpec((tm, D_MODEL), lambda i: (nt - 1 - i, 0))
    return _launch(
        body, name="bwd_pool_mixer", grid=(nt,),
        in_specs=[rev] * 6 + [_full_spec((N_POOL_GROUPS, POOL_GROUP, POOL_GROUP))] + [_vec_spec()] * 4,
        out_specs=[rev, _full_spec((N_POOL_GROUPS, POOL_GROUP, POOL_GROUP))] + [_vec_spec()] * 4,
        out_shape=[jax.ShapeDtypeStruct((T, D_MODEL), F32),
                   jax.ShapeDtypeStruct((N_POOL_GROUPS, POOL_GROUP, POOL_GROUP), BF16)]
                  + [jax.ShapeDtypeStruct((1, D_MODEL), F32)] * 4,
        scratch_shapes=[pltpu.VMEM((POOL_HALO, D_MODEL), F32), pltpu.VMEM((N_POOL_GROUPS, POOL_GROUP, POOL_GROUP), F32)],
        args=(dx2, dh2, x1, x, yraw, d, wp, scale, g_ffn, g_post, g_pre), job=job)


def _my_place():
    return lax.axis_index("x"), lax.axis_index("y"), lax.axis_index("c")


def _dev_index(px, py, pc):
    return 4 * px + 2 * py + pc


def _peer_by_relation(r):
    x, y, c = _my_place()
    return (x ^ ((r >> 2) & 1), y ^ ((r >> 1) & 1), c ^ (r & 1))


def _slot_pool(ref, j):
    return ref.at[:, pl.ds(pl.multiple_of(j * 32, 32), 32), :]


def _slot_scale(ref, j):
    return ref.at[:, pl.ds(pl.multiple_of(j * 128, 128), 128)]


def _slot_rows128(ref, j):
    return ref.at[pl.ds(pl.multiple_of(j * 128, 128), 128), :]


def _slot_gu(ref, j):
    return ref.at[j % FF_CHUNKS, j // FF_CHUNKS]


def _slot_wd(ref, j):
    return ref.at[pl.ds(pl.multiple_of(j * WD_ROWS, 16), WD_ROWS), :]


def _slot_cols128(ref, j):
    return ref.at[:, pl.ds(pl.multiple_of(j * 128, 128), 128)]


_GATHERED = {
    "pool": ((N_POOL_GROUPS, POOL_GROUP, POOL_GROUP), BF16, _slot_pool),
    "scale": ((1, D_MODEL), F32, _slot_scale),
    "kv": ((D_MODEL, 2 * KV_DIM), BF16, _slot_rows128),
    "q": ((D_MODEL, D_MODEL), BF16, _slot_rows128),
    "o": ((D_MODEL, D_MODEL), BF16, _slot_rows128),
    "gu": ((FF_CHUNKS, 2, FF_BLOCK, D_MODEL), BF16, _slot_gu),
    "wd": ((D_FF, D_MODEL), BF16, _slot_wd),
    "gate": ((D_MODEL, D_MODEL), BF16, _slot_rows128),
    "proj": ((PLE_DIM, D_MODEL), BF16, _slot_cols128),
}


def _no_compute():
    pass


class _AllGather:
    def __init__(self, names, shards):
        self.kinds = [_GATHERED[n.rstrip("01")] for n in names]
        self.args = [shards[n] for n in names]
        self.out_shape = [jax.ShapeDtypeStruct(shape, dtype) for shape, dtype, _ in self.kinds]
        n = len(names)
        self.scratch = [pltpu.SemaphoreType.DMA((n, 7)), pltpu.SemaphoreType.DMA((n, 7)), pltpu.SemaphoreType.DMA((n,))]

    def _plan(self, srcs, outs, sems):
        send_sems, recv_sems, local_sems = sems
        x, y, c = _my_place()

        def slot(t, dev):
            return self.kinds[t][2](outs[t], _dev_index(*dev))

        def copy(t, k, block, to, src=None):
            return pltpu.make_async_remote_copy(
                src_ref=slot(t, block) if src is None else src, dst_ref=slot(t, block),
                send_sem=send_sems.at[t, k], recv_sem=recv_sems.at[t, k], device_id=to, device_id_type=MESH)

        return types.SimpleNamespace(
            copy=copy, core=c, me=(x, y, c), sibling=(x, y, 1 - c),
            x_chip=(1 - x, y), y_chip=(x, 1 - y), far_chip=(1 - x, 1 - y),
            via=(x ^ (1 - c), y ^ c),
            onto=(x ^ c, y ^ (1 - c)),
            k_via=1 + c, k_onto=2 - c,
            local=[pltpu.make_async_copy(srcs[t], slot(t, (x, y, c)), local_sems.at[t]) for t in range(len(srcs))])

    def start(self, srcs, outs, sems):
        p = self._plan(srcs, outs, sems)
        for cp in p.local:
            cp.start()
        for t in range(len(srcs)):
            p.copy(t, 0, p.me, p.sibling, src=srcs[t]).start()
            p.copy(t, 1, p.me, (*p.x_chip, p.core), src=srcs[t]).start()
            p.copy(t, 2, p.me, (*p.y_chip, p.core), src=srcs[t]).start()

    def mid(self, srcs, outs, sems):
        p = self._plan(srcs, outs, sems)
        for t in range(len(srcs)):
            block = (*p.via, p.core)
            p.copy(t, p.k_via, block, p.me).wait_recv()
            p.copy(t, 3, block, (*p.onto, p.core)).start()
            p.copy(t, 3 + p.k_via, block, p.sibling).start()

    def finish(self, srcs, outs, sems):
        p = self._plan(srcs, outs, sems)
        n = len(srcs)
        for t in range(n):
            block = (*p.onto, p.core)
            p.copy(t, p.k_onto, block, p.me).wait_recv()
            p.copy(t, 3 + p.k_onto, block, p.sibling).start()
        for t in range(n):
            block = (*p.far_chip, p.core)
            p.copy(t, 3, block, p.me).wait_recv()
            p.copy(t, 6, block, p.sibling).start()
        other = 1 - p.core
        for t in range(n):
            p.copy(t, 0, (*p.me[:2], other), p.me).wait_recv()
            for k, chip in ((4, p.x_chip), (5, p.y_chip), (6, p.far_chip)):
                p.copy(t, k, (*chip, other), p.me).wait_recv()
            for k in range(7):
                p.copy(t, k, p.me, p.sibling).wait_send()
        for cp in p.local:
            cp.wait()


def _all_gather_only(name, names, shards):
    return _launch(_no_compute, name=name, grid=(), in_specs=[], out_specs=[], out_shape=[], args=(),
                   job=_AllGather(names, shards))[1]


def _block_pool(ref, j):
    return ref.at[:, pl.ds(pl.multiple_of(j * 32, 32), 32), :]


def _block_rows128(ref, j):
    return ref.at[pl.ds(pl.multiple_of(j * 128, 128), 128), :]


def _block_gu(ref, j):
    return ref.at[j % FF_CHUNKS, j // FF_CHUNKS]


def _block_wd(ref, j):
    return ref.at[pl.ds(pl.multiple_of(j * WD_ROWS, 16), WD_ROWS), :]


def _block_cols128(ref, j):
    return ref.at[:, pl.ds(pl.multiple_of(j * 128, 128), 128)]


_SCATTERED = {
    "pool": ((N_POOL_GROUPS, 32, POOL_GROUP), _block_pool),
    "kv": ((128, 2 * KV_DIM), _block_rows128),
    "q": ((128, D_MODEL), _block_rows128),
    "o": ((128, D_MODEL), _block_rows128),
    "gu": ((FF_BLOCK, FF_PART), _block_gu),
    "wd": ((WD_ROWS, FF_PART), _block_wd),
    "gate": ((128, D_MODEL), _block_rows128),
    "proj": ((PLE_DIM, 128), _block_cols128),
}


class _SiblingSwap:
    def __init__(self, pieces):
        self.kinds = [_SCATTERED[kind] for kind, _ in pieces]
        self.args = [g for _, g in pieces]
        self.out_shape = [jax.ShapeDtypeStruct((N_CHIPS, *block), BF16) for block, _ in self.kinds]
        n = len(pieces)
        self.scratch = [pltpu.SemaphoreType.DMA((n, N_CHIPS)), pltpu.SemaphoreType.DMA((n, N_CHIPS))]

    def _copies(self, srcs, outs, sems):
        send_sems, recv_sems = sems
        x, y, c = _my_place()
        return [pltpu.make_async_remote_copy(
            src_ref=block(srcs[t], 2 * ch + 1 - c), dst_ref=outs[t].at[ch], send_sem=send_sems.at[t, ch],
            recv_sem=recv_sems.at[t, ch], device_id=(x, y, 1 - c), device_id_type=MESH)
            for t, (_, block) in enumerate(self.kinds) for ch in range(N_CHIPS)]

    def start(self, srcs, outs, sems):
        for cp in self._copies(srcs, outs, sems):
            cp.start()

    def finish(self, srcs, outs, sems):
        for cp in self._copies(srcs, outs, sems):
            cp.wait()


class _ChipScatter:
    def __init__(self, pieces):
        self.kinds = [_SCATTERED[kind] for kind, _, _ in pieces]
        self.n = n = len(pieces)
        self.args = [g for _, g, _ in pieces] + [s for _, _, s in pieces]
        self.out_shape = [jax.ShapeDtypeStruct((N_CHIPS, *block), BF16) for block, _ in self.kinds]
        self.scratch = []
        for block, _ in self.kinds:
            self.scratch += [pltpu.VMEM((N_CHIPS, *block), BF16)] * 3
        self.scratch += [pltpu.SemaphoreType.DMA((n, N_CHIPS + 1)), pltpu.SemaphoreType.DMA((n, N_CHIPS - 1)),
                         pltpu.SemaphoreType.DMA((n, N_CHIPS - 1)), pltpu.SemaphoreType.DMA((n,))]

    def _sends(self, outs, scr):
        n = self.n
        send_sems, recv_sems, local_sems = scr[3 * n + 1:]
        x, y, c = _my_place()
        chip = 2 * x + y
        copies = []
        for t in range(n):
            total = scr[3 * t + 2]
            copies.append(pltpu.make_async_copy(total.at[chip], outs[t].at[chip], local_sems.at[t]))
            for r in range(1, N_CHIPS):
                to = chip ^ r
                copies.append(pltpu.make_async_remote_copy(
                    src_ref=total.at[to], dst_ref=outs[t].at[chip], send_sem=send_sems.at[t, r - 1],
                    recv_sem=recv_sems.at[t, r - 1], device_id=(to // 2, to % 2, c), device_id_type=MESH))
        return copies

    def start(self, ins, outs, scr):
        n = self.n
        load_sems = scr[3 * n]
        c = lax.axis_index("c")
        loads = []
        for t, (_, block) in enumerate(self.kinds):
            mine, theirs = scr[3 * t], scr[3 * t + 1]
            loads += [pltpu.make_async_copy(block(ins[t], 2 * ch + c), mine.at[ch], load_sems.at[t, ch])
                      for ch in range(N_CHIPS)]
            loads.append(pltpu.make_async_copy(ins[n + t], theirs, load_sems.at[t, N_CHIPS]))
        for cp in loads:
            cp.start()
        for cp in loads:
            cp.wait()
        for t in range(n):
            mine, theirs, total = scr[3 * t:3 * t + 3]
            for ch in range(N_CHIPS):
                total[ch] = (mine[ch].astype(F32) + theirs[ch].astype(F32)).astype(BF16)
        for cp in self._sends(outs, scr):
            cp.start()

    def finish(self, ins, outs, scr):
        for cp in self._sends(outs, scr):
            cp.wait()


class _Jobs:
    def __init__(self, *jobs):
        self.jobs = jobs
        self.args = [a for j in jobs for a in j.args]
        self.out_shape = [o for j in jobs for o in j.out_shape]
        self.scratch = [s for j in jobs for s in j.scratch]

    def _split(self, refs, attr):
        at = 0
        for j in self.jobs:
            n = len(getattr(j, attr))
            yield refs[at:at + n]
            at += n

    def _each(self, ins, outs, scr):
        return zip(self.jobs, self._split(ins, "args"), self._split(outs, "out_shape"), self._split(scr, "scratch"))

    def start(self, ins, outs, scr):
        for j, i, o, s in self._each(ins, outs, scr):
            j.start(i, o, s)

    def mid(self, ins, outs, scr):
        for j, i, o, s in self._each(ins, outs, scr):
            if hasattr(j, "mid"):
                j.mid(i, o, s)

    def finish(self, ins, outs, scr):
        for j, i, o, s in self._each(ins, outs, scr):
            j.finish(i, o, s)

    def split_outputs(self, outs):
        return list(self._split(outs, "out_shape"))


def _adamw_math(w, g, m, v):
    m = ADAM_B1 * m + (1.0 - ADAM_B1) * g
    v = ADAM_B2 * v + (1.0 - ADAM_B2) * (g * g)
    m_hat = m / (1.0 - ADAM_B1 ** ADAM_STEP)
    v_hat = v / (1.0 - ADAM_B2 ** ADAM_STEP)
    delta = -ADAM_LR * (m_hat / (jnp.sqrt(v_hat) + ADAM_EPS) + ADAM_WD * w)
    return delta, m, v


def _adamw(name, w, m, v, landings, n_col_blocks=1, job=None):
    _, r, c = landings[0].shape
    grid = (w.shape[0] // r, n_col_blocks)

    def body(w_ref, m_ref, v_ref, *rest):
        l_refs, (g_ref, d_ref, nm_ref, nv_ref) = rest[:len(landings)], rest[len(landings):]
        step = pl.program_id(0) * n_col_blocks + pl.program_id(1)
        for idx, l_ref in enumerate(l_refs):
            @pl.when(step == idx)
            def _(l_ref=l_ref):
                g = l_ref[0].astype(F32)
                for s in range(1, N_CHIPS):
                    g = g + l_ref[s].astype(F32)
                g_ref[...] = g
                d_ref[...], nm_ref[...], nv_ref[...] = _adamw_math(w_ref[...], g, m_ref[...], v_ref[...])

    spec = pl.BlockSpec((r, c), lambda a, b: (a, b))
    return _launch(
        body, name=f"adamw_{name}", grid=grid,
        in_specs=[spec, spec, spec] + [_full_spec((N_CHIPS, r, c))] * len(landings),
        out_specs=[spec] * 4, out_shape=[jax.ShapeDtypeStruct(w.shape, F32)] * 4,
        args=(w, m, v, *landings), vmem=VMEM_BIG, job=job)


_SMALL = (("pre_mix_g", SV_PRE_MIX, 2), ("post_mix_g", SV_POST_MIX, 2), ("pre_ffn_g", SV_PRE_FFN, 2),
          ("post_ffn_g", SV_POST_FFN, 2), ("ple_g", SV_PLE, 2), ("ple_post_g", SV_PLE_POST, 2), ("kv_g", SV_KV, 1),
          ("pool_scale", SV_POOL_SCALE, 1), ("sinks", SV_SINKS, 1))


def _small_all_reduce_adamw(part, params):
    flat = [a for name, _, _ in _SMALL for a in params[name]]
    n_in = 1 + len(flat)

    def body(*refs):
        part_ref, wmv = refs[0], refs[1:n_in]
        loss_ref, outs = refs[n_in], refs[n_in + 1:n_in + 1 + 4 * len(_SMALL)]
        buf, total, send_sems, recv_sems = refs[n_in + 1 + 4 * len(_SMALL):]
        x, y, c = _my_place()
        me = _dev_index(x, y, c)
        buf[me] = part_ref[...]
        copies = [pltpu.make_async_remote_copy(
            src_ref=buf.at[me], dst_ref=buf.at[me], send_sem=send_sems.at[r - 1], recv_sem=recv_sems.at[r - 1],
            device_id=_peer_by_relation(r), device_id_type=MESH) for r in range(1, N_DEV)]
        for cp in copies:
            cp.start()
        for cp in copies:
            cp.wait()
        g = buf[0]
        for s in range(1, N_DEV):
            g = g + buf[s]
        total[...] = g
        loss_ref[...] = total[SV_LOSS:SV_LOSS + 1, 0:1]
        for idx, (name, row, n_rows) in enumerate(_SMALL):
            w_ref, m_ref, v_ref = wmv[3 * idx:3 * idx + 3]
            g_ref, d_ref, nm_ref, nv_ref = outs[4 * idx:4 * idx + 4]
            if name == "pool_scale":
                g = total[row:row + 1, pl.ds(pl.multiple_of(me * 128, 128), 128)]
            else:
                g = total[row:row + n_rows, 0:w_ref.shape[1]]
            g_ref[...] = g
            d_ref[...], nm_ref[...], nv_ref[...] = _adamw_math(w_ref[...], g, m_ref[...], v_ref[...])

    out_shape = [jax.ShapeDtypeStruct((1, 1), F32)]
    for name, _, _ in _SMALL:
        out_shape += [jax.ShapeDtypeStruct(params[name][0].shape, F32)] * 4
    res, _ = _launch(
        body, name="small_all_reduce_adamw", grid=(1,),
        in_specs=[_full_spec(a.shape) for a in (part, *flat)], out_specs=[_full_spec(s.shape) for s in out_shape],
        out_shape=out_shape,
        scratch_shapes=[pltpu.VMEM((N_DEV, SV_ROWS, D_MODEL), F32), pltpu.VMEM((SV_ROWS, D_MODEL), F32),
                        pltpu.SemaphoreType.DMA((N_DEV - 1,)), pltpu.SemaphoreType.DMA((N_DEV - 1,))],
        args=(part, *flat))
    return res[0], {name: res[1 + 4 * idx:5 + 4 * idx] for idx, (name, _, _) in enumerate(_SMALL)}


def _local_step(x, p, tgt, gains, sinks, shards, weights):
    row = lambda first_row, layer: _Gain(gains, first_row + layer)
    gather = lambda *names: _AllGather(names, shards)
    g_pre_mix, g_post_mix, g_pre_ffn, g_post_ffn = SV_PRE_MIX, SV_POST_MIX, SV_PRE_FFN, SV_POST_FFN
    g_ple, g_ple_post, g_kv = SV_PLE, SV_PLE_POST, _Gain(gains, SV_KV)

    wp, scale, wgu0 = _all_gather_only("gather_first", ("pool", "scale", "gu0"), shards)
    (x1_0, h2_0, yraw, dpool), (wd0,) = _fwd_pool_mixer(
        x, row(g_pre_mix, 0), wp, scale, row(g_post_mix, 0), row(g_pre_ffn, 0), job=gather("wd0"))
    (gs0, us0, f0, x2_0, h3_0), (wgate0, wproj0, wgu1) = _fwd_ffn(
        0, h2_0, x1_0, wgu0, wd0, row(g_post_ffn, 0), row(g_ple, 0), job=gather("gate0", "proj0", "gu1"))
    (x3_0, z0, pe0), (wkv, wq) = _fwd_ple(0, x2_0, h3_0, p[0], wgate0, wproj0, row(g_ple_post, 0),
                                          job=gather("kv", "q"))
    (hk, h1, q, kv), (wo,) = _fwd_qkv(x3_0, g_kv, row(g_pre_mix, 1), wkv, wq, job=gather("o"))
    front = ((ATT_BLOCK, 0), (0, 0))
    kpad = jnp.pad(kv[:, :KV_DIM], front)
    vpad = jnp.pad(kv[:, KV_DIM:], front)
    (attn,), (wd1,) = _fwd_attention(q, kpad, vpad, sinks, job=gather("wd1"))
    (y1, x1_1, h2_1), _ = _fwd_attn_out(attn, x3_0, wo, row(g_post_mix, 1), row(g_pre_ffn, 1))
    (gs1, us1, f1, x2_1, h3_1), (wgate1, wproj1) = _fwd_ffn(
        1, h2_1, x1_1, wgu1, wd1, row(g_post_ffn, 1), row(g_ple, 1), job=gather("gate1", "proj1"))
    (dx3_1, z1, pe1, loss), _ = _fwd_ple(1, x2_1, h3_1, p[1], wgate1, wproj1, row(g_ple_post, 1), target=tgt)

    produced, swapped, landed = {}, {}, {}

    def kind_of(name):
        return name.rstrip("0123_")

    def carry(swap=(), spread=()):
        jobs = []
        if swap:
            jobs.append(_SiblingSwap([(kind_of(n), produced[n]) for n in swap]))
        if spread:
            jobs.append(_ChipScatter([(kind_of(n), produced[n], swapped[n]) for n in spread]))
        return _Jobs(*jobs)

    def carried(jobs, outs, swap=(), spread=()):
        parts = jobs.split_outputs(outs)
        if swap:
            swapped.update(zip(swap, parts[0]))
        if spread:
            landed.update(zip(spread, parts[-1]))

    def hosted(call, *args, swap=(), spread=()):
        jobs = carry(swap, spread)
        outs, job_outs = call(*args, job=jobs)
        carried(jobs, job_outs, swap, spread)
        return outs

    def ffn_weight_grads(layer, h2, df, dg, du, a, hosts):
        for qtr in range(FF_PARTS):
            dgu, dwd = hosted(_bwd_ffn_dw, layer, qtr, h2, df, dg, du, a, **hosts[qtr])
            produced[f"gu{layer}_{qtr}"], produced[f"wd{layer}_{qtr}"] = dgu, dwd

    ffn_q = lambda layer, qtr: (f"gu{layer}_{qtr}", f"wd{layer}_{qtr}")

    dx2_1, df1, produced["gate1"], produced["proj1"], dg_ple_post1, dg_ple1, dg_post_ffn1 = hosted(
        _bwd_ple, 1, dx3_1, x2_1, z1, pe1, h3_1, p[1], f1, wgate1, row(g_ple_post, 1), row(g_ple, 1),
        row(g_post_ffn, 1))
    dh2_1, dg1, du1, a1 = hosted(_bwd_ffn_act, 1, df1, gs1, us1, wgu1, wd1, swap=("gate1", "proj1"))
    ffn_weight_grads(1, h2_1, df1, dg1, du1, a1, [dict(spread=("gate1", "proj1")), dict(swap=ffn_q(1, 0))])
    dx1_1, dattn, produced["o"], dg_pre_ffn1, dg_post_mix1 = hosted(
        _bwd_attn_out, dx2_1, dh2_1, x1_1, y1, attn, wo, row(g_pre_ffn, 1), row(g_post_mix, 1), swap=ffn_q(1, 1))
    dq, dkpad, dvpad, dsinks = hosted(_bwd_attention, q, dattn, kpad, vpad, sinks, spread=ffn_q(1, 0))
    dkv = jnp.concatenate([dkpad[ATT_BLOCK:], dvpad[ATT_BLOCK:]], axis=1).astype(BF16)
    dx3_0, produced["q"], produced["kv"], dg_pre_mix1, dg_kv = hosted(
        _bwd_qkv, dx1_1, dq, dkv, x3_0, h1, hk, wq, wkv, row(g_pre_mix, 1), g_kv, swap=("o",))
    dx2_0, df0, produced["gate0"], produced["proj0"], dg_ple_post0, dg_ple0, dg_post_ffn0 = hosted(
        _bwd_ple, 0, dx3_0, x2_0, z0, pe0, h3_0, p[0], f0, wgate0, row(g_ple_post, 0), row(g_ple, 0),
        row(g_post_ffn, 0), swap=("q", "kv"), spread=("gu1_1",))
    dh2_0, dg0, du0, a0 = hosted(_bwd_ffn_act, 0, df0, gs0, us0, wgu0, wd0,
                                 swap=("gate0", "proj0"), spread=("wd1_1", "o"))
    ffn_weight_grads(0, h2_0, df0, dg0, du0, a0, [
        dict(spread=("gate0", "proj0", "q", "kv")), dict(swap=ffn_q(0, 0))])
    grad_x, produced["pool"], dscale, dg_pre_ffn0, dg_post_mix0, dg_pre_mix0 = hosted(
        _bwd_pool_mixer, dx2_0, dh2_0, x1_0, x, yraw, dpool, wp, scale, row(g_pre_ffn, 0), row(g_post_mix, 0),
        row(g_pre_mix, 0), swap=ffn_q(0, 1), spread=ffn_q(0, 0))

    def update(name, n_col_blocks=1, pieces=None, swap=(), spread=()):
        w, m, v = weights[name]
        rows = w.size // w.shape[-1]
        flat = [landed[n].reshape(N_CHIPS, -1, landed[n].shape[-1]) for n in (pieces or [kind_short[name]])]
        outs = hosted(_adamw, name, w.reshape(rows, -1), m.reshape(rows, -1), v.reshape(rows, -1), flat,
                      n_col_blocks, swap=swap, spread=spread)
        return [o.reshape(w.shape) for o in outs]

    kind_short = {"w_q": "q", "w_kv": "kv", "w_o": "o", "pool_w": "pool"}
    upd = {}
    upd["w_ple_gate"] = update("w_ple_gate", pieces=("gate0", "gate1"), swap=("pool",), spread=ffn_q(0, 1))
    upd["w_ple_proj"] = update("w_ple_proj", pieces=("proj0", "proj1"), spread=("pool",))
    for name in ("w_q", "w_kv", "w_o", "pool_w"):
        upd[name] = update(name)
    upd["w_gu"] = update("w_gu", FF_PARTS,
                         pieces=[f"gu{layer}_{qtr}" for layer in range(2) for qtr in range(FF_PARTS)])
    upd["w_gu"] = [jnp.swapaxes(a, 1, 2) for a in upd["w_gu"]]
    upd["w_down"] = update("w_down", FF_PARTS,
                           pieces=[f"wd{layer}_{qtr}" for layer in range(2) for qtr in range(FF_PARTS)])

    lanes = lambda a: jnp.pad(a, ((0, 0), (0, D_MODEL - a.shape[1])))
    small = jnp.concatenate([
        dg_pre_mix0, dg_pre_mix1, dg_post_mix0, dg_post_mix1, dg_pre_ffn0, dg_pre_ffn1, dg_post_ffn0, dg_post_ffn1,
        dg_ple0, dg_ple1, dg_ple_post0, dg_ple_post1, dg_kv, dscale, lanes(dsinks[:, :N_HEADS]), lanes(loss)], axis=0)
    return grad_x, upd, small


def kernel(x, p, pre_mix_g, post_mix_g, pre_ffn_g, post_ffn_g, pool_w, pool_scale, kv_g, w_kv, w_q, sinks, w_o, w_gu, w_down, ple_g, w_ple_gate, w_ple_proj, ple_post_g, loss_target, m_pre_mix_g, m_post_mix_g, m_pre_ffn_g, m_post_ffn_g, m_pool_w, m_pool_scale, m_kv_g, m_w_kv, m_w_q, m_sinks, m_w_o, m_w_gu, m_w_down, m_ple_g, m_w_ple_gate, m_w_ple_proj, m_ple_post_g, v_pre_mix_g, v_post_mix_g, v_pre_ffn_g, v_post_ffn_g, v_pool_w, v_pool_scale, v_kv_g, v_w_kv, v_w_q, v_sinks, v_w_o, v_w_gu, v_w_down, v_ple_g, v_w_ple_gate, v_w_ple_proj, v_ple_post_g):
    shards = {"pool": pool_w[0].astype(BF16), "scale": pool_scale, "kv": w_kv.astype(BF16),
              "q": w_q[0].astype(BF16), "o": w_o[0].astype(BF16)}
    for layer in range(2):
        shards[f"gu{layer}"] = w_gu[layer].T.astype(BF16)
        shards[f"wd{layer}"] = w_down[layer].astype(BF16)
        shards[f"gate{layer}"] = w_ple_gate[layer].astype(BF16)
        shards[f"proj{layer}"] = w_ple_proj[layer].astype(BF16)
    gains = jnp.concatenate([pre_mix_g, post_mix_g, pre_ffn_g, post_ffn_g, ple_g, ple_post_g, kv_g[None, :]],
                            axis=0).reshape(-1, 1, D_MODEL)
    weights = {"pool_w": (pool_w, m_pool_w, v_pool_w), "w_kv": (w_kv, m_w_kv, v_w_kv), "w_q": (w_q, m_w_q, v_w_q),
               "w_o": (w_o, m_w_o, v_w_o), "w_down": (w_down, m_w_down, v_w_down),
               "w_gu": tuple(jnp.swapaxes(a, 1, 2) for a in (w_gu, m_w_gu, v_w_gu)),
               "w_ple_gate": (w_ple_gate, m_w_ple_gate, v_w_ple_gate),
               "w_ple_proj": (w_ple_proj, m_w_ple_proj, v_w_ple_proj)}
    grad_x, upd, small = _local_step(x[0], p[:, 0], loss_target[0], gains, sinks, shards, weights)

    small_params = {
        "pre_mix_g": (pre_mix_g, m_pre_mix_g, v_pre_mix_g), "post_mix_g": (post_mix_g, m_post_mix_g, v_post_mix_g),
        "pre_ffn_g": (pre_ffn_g, m_pre_ffn_g, v_pre_ffn_g), "post_ffn_g": (post_ffn_g, m_post_ffn_g, v_post_ffn_g),
        "ple_g": (ple_g, m_ple_g, v_ple_g), "ple_post_g": (ple_post_g, m_ple_post_g, v_ple_post_g),
        "kv_g": (kv_g[None, :], m_kv_g[None, :], v_kv_g[None, :]),
        "pool_scale": (pool_scale, m_pool_scale, v_pool_scale), "sinks": (sinks, m_sinks, v_sinks)}
    loss, small_upd = _small_all_reduce_adamw(small, small_params)
    small_upd["kv_g"] = [a[0] for a in small_upd["kv_g"]]
    upd.update(small_upd)

    names = ["pre_mix_g", "post_mix_g", "pre_ffn_g", "post_ffn_g", "pool_w", "pool_scale", "kv_g", "w_kv", "w_q",
             "sinks", "w_o", "w_gu", "w_down", "ple_g", "w_ple_gate", "w_ple_proj", "ple_post_g"]
    outs = [loss[0, 0], grad_x[None]]
    for kind in range(4):
        outs += [upd[n][kind] for n in names]
    return tuple(outs)
```

```python
import functools
import types

import jax
import jax.numpy as jnp
from jax import lax
from jax.experimental import pallas as pl
from jax.experimental.pallas import tpu as pltpu

F32 = jnp.float32
BF16 = jnp.bfloat16

N_DEV = 8
D_MODEL = 1024
N_POOL_GROUPS = 4
POOL_GROUP = 256
POOL_HALO = 16
HEAD_DIM = 64
N_HEADS = 16
N_KV_HEADS = 4
GQA_GROUP = 4
KV_DIM = N_KV_HEADS * HEAD_DIM
ATT_BLOCK = 128
D_FF = 2816
FF_CHUNKS = 4
FF_BLOCK = D_FF // FF_CHUNKS
WD_ROWS = D_FF // N_DEV
FF_PARTS = 2
FF_PART = D_MODEL // FF_PARTS
N_CHIPS = 4
PLE_DIM = 256
EPS = 1e-6
NEG_INF = -1e30
ATT_SCALE = HEAD_DIM ** -0.5

ADAM_LR = 0.001
ADAM_B1 = 0.9
ADAM_B2 = 0.999
ADAM_EPS = 1e-08
ADAM_WD = 0.01
ADAM_STEP = 10

ROW_TILE = 512
FFN_ROW_TILE = 512
FFN_SUB_TILES = 2
VMEM_BIG = 60 * 1024 * 1024
VMEM_MID = 56 * 1024 * 1024
HBM_PIN_ELEMS = 1024

SV_ROWS = 16
SV_PRE_MIX, SV_POST_MIX, SV_PRE_FFN, SV_POST_FFN, SV_PLE, SV_PLE_POST = 0, 2, 4, 6, 8, 10
SV_KV, SV_POOL_SCALE, SV_SINKS, SV_LOSS = 12, 13, 14, 15

MESH = pl.DeviceIdType.MESH
ANY = pl.BlockSpec(memory_space=pl.ANY)


def _dot(a, b):
    return jnp.dot(a, b, preferred_element_type=F32)


def _dot_nt(a, b):
    return lax.dot_general(a, b, (((1,), (1,)), ((), ())), preferred_element_type=F32)


def _dot_tn(a, b):
    return lax.dot_general(a, b, (((0,), (0,)), ((), ())), preferred_element_type=F32)


def _rstd(x):
    return lax.rsqrt(jnp.mean(x * x, axis=-1, keepdims=True) + EPS)


def _rms(x, g):
    return x * _rstd(x) * g


def _rms_bwd(x, g, dy):
    r = _rstd(x)
    n = x * r
    dn = dy * g
    dx = r * (dn - n * jnp.mean(dn * n, axis=-1, keepdims=True))
    dg = jnp.sum(dy * n, axis=0, keepdims=True)
    return dx, dg


def _sigmoid(x):
    return 1.0 / (1.0 + jnp.exp(-x))


def _acc(ref, val, first):
    @pl.when(first)
    def _():
        ref[...] = val

    @pl.when(jnp.logical_not(first))
    def _():
        ref[...] += val


def _pool_counts(row0, rows):
    t = row0 + lax.broadcasted_iota(jnp.int32, (rows, D_MODEL), 0) + 1
    grp = lax.broadcasted_iota(jnp.int32, (rows, D_MODEL), 1) // POOL_GROUP
    win = jnp.left_shift(2, grp)
    return jnp.minimum(t, win).astype(F32)


def _window_sums(ext, shift_of):
    outs = []
    s = ext
    for gi in range(N_POOL_GROUPS):
        s = s + pltpu.roll(s, shift_of(1 << gi), axis=0)
        outs.append(s[:, :POOL_GROUP])
        s = s[:, POOL_GROUP:]
    return jnp.concatenate(outs, axis=1)


def _cparams(n_axes, vmem):
    return pltpu.CompilerParams(dimension_semantics=("arbitrary",) * n_axes, vmem_limit_bytes=vmem)


def _row_spec(cols, tm=ROW_TILE):
    return pl.BlockSpec((tm, cols), lambda i: (i, 0))


def _full_spec(shape):
    zeros = (0,) * len(shape)
    return pl.BlockSpec(shape, lambda *_: zeros)


def _vec_spec():
    return _full_spec((1, D_MODEL))


class _Gain:
    def __init__(self, stacked, layer):
        self.stacked, self.layer = stacked, layer

    def spec(self):
        layer = self.layer
        return pl.BlockSpec((None, 1, D_MODEL), lambda *_: (layer, 0, 0))


def _in_hbm(a):
    return pltpu.with_memory_space_constraint(a, pltpu.HBM) if a.size >= HBM_PIN_ELEMS else a


def _out_in_hbm(s):
    return pltpu.HBM(s.shape, s.dtype) if s.size >= HBM_PIN_ELEMS else s


def _launch(body, *, name, grid, in_specs, out_specs, out_shape, args, scratch_shapes=(), vmem=VMEM_MID, job=None):
    in_specs = [a.spec() if isinstance(a, _Gain) else s for s, a in zip(in_specs, args)]
    args = [_in_hbm(a.stacked if isinstance(a, _Gain) else a) for a in args]
    n_in, n_out, n_scr = len(args), len(out_shape), len(scratch_shapes)
    j_args, j_out, j_scr = ([], [], []) if job is None else ([_in_hbm(a) for a in job.args], job.out_shape, job.scratch)

    def run(*refs):
        groups, at = [], 0
        for n in (n_in, len(j_args), n_out, len(j_out), n_scr, len(j_scr)):
            groups.append(refs[at:at + n])
            at += n
        ins, j_ins, outs, j_outs, scr, j_sems = groups
        if job is None:
            body(*ins, *outs, *scr)
        elif not grid:
            job.start(j_ins, j_outs, j_sems)
            job.mid(j_ins, j_outs, j_sems)
            body(*ins, *outs, *scr)
            job.finish(j_ins, j_outs, j_sems)
        else:
            ids = [pl.program_id(a) for a in range(len(grid))]
            first = functools.reduce(jnp.logical_and, [i == 0 for i in ids])
            half = functools.reduce(jnp.logical_and, [ids[0] == grid[0] // 2] + [i == 0 for i in ids[1:]])
            last = functools.reduce(jnp.logical_and, [i == g - 1 for i, g in zip(ids, grid)])
            pl.when(first)(lambda: job.start(j_ins, j_outs, j_sems))
            pl.when(half)(lambda: job.mid(j_ins, j_outs, j_sems))
            body(*ins, *outs, *scr)
            pl.when(last)(lambda: job.finish(j_ins, j_outs, j_sems))

    res = pl.pallas_call(
        run, name=name, grid=grid,
        in_specs=list(in_specs) + [ANY] * len(j_args), out_specs=list(out_specs) + [ANY] * len(j_out),
        out_shape=[_out_in_hbm(s) for s in list(out_shape) + list(j_out)],
        scratch_shapes=list(scratch_shapes) + list(j_scr),
        compiler_params=_cparams(len(grid), vmem),
    )(*args, *j_args)
    return res[:n_out], res[n_out:]


def _fwd_pool_mixer(x, g_pre, wp, scale, g_post, g_ffn, job=None):
    T = x.shape[0]
    tm = ROW_TILE
    nt = T // tm

    def body(x_ref, gpre_ref, wp_ref, sc_ref, gpost_ref, gffn_ref, x1_ref, h2_ref, yraw_ref, d_ref, carry):
        i = pl.program_id(0)

        @pl.when(i == 0)
        def _():
            carry[...] = jnp.zeros_like(carry)

        xv = x_ref[...]
        h = _rms(xv, gpre_ref[...])
        ext = jnp.concatenate([carry[...], h], axis=0)
        carry[...] = h[tm - POOL_HALO:, :]
        sums = _window_sums(ext, lambda k: k)[POOL_HALO:, :]
        d = sums / _pool_counts(i * tm, tm) - h
        db = d.astype(BF16)
        d_ref[...] = db
        yraw = jnp.concatenate(
            [_dot(db[:, g * POOL_GROUP:(g + 1) * POOL_GROUP], wp_ref[g]) for g in range(N_POOL_GROUPS)], axis=1)
        yraw_ref[...] = yraw
        x1 = xv + _rms(yraw * sc_ref[...], gpost_ref[...])
        x1_ref[...] = x1
        h2_ref[...] = _rms(x1, gffn_ref[...]).astype(BF16)

    return _launch(
        body, name="fwd_pool_mixer", grid=(nt,),
        in_specs=[_row_spec(D_MODEL), _vec_spec(), _full_spec((N_POOL_GROUPS, POOL_GROUP, POOL_GROUP)), _vec_spec(),
                  _vec_spec(), _vec_spec()],
        out_specs=[_row_spec(D_MODEL)] * 4,
        out_shape=[jax.ShapeDtypeStruct((T, D_MODEL), F32), jax.ShapeDtypeStruct((T, D_MODEL), BF16),
                   jax.ShapeDtypeStruct((T, D_MODEL), F32), jax.ShapeDtypeStruct((T, D_MODEL), BF16)],
        scratch_shapes=[pltpu.VMEM((POOL_HALO, D_MODEL), F32)],
        args=(x, g_pre, wp, scale, g_post, g_ffn), job=job)


def _fwd_ffn(layer, h2, x1, wgu, wd, g_post, g_ple, job=None):
    T = h2.shape[0]
    tm = min(FFN_ROW_TILE, T)
    nt = T // tm
    sub = tm // FFN_SUB_TILES
    last = FF_CHUNKS - 1

    def body(h2_ref, x1_ref, wgu_ref, wd_ref, gpost_ref, gple_ref, gs_ref, us_ref, f_ref, x2_ref, h3_ref, acc):
        k = pl.program_id(0)
        i = pl.program_id(1)
        rows = pl.ds(pl.multiple_of(i * tm, tm), tm)
        parts = []
        for s in range(FFN_SUB_TILES):
            r = pl.ds(s * sub, sub)
            h = h2_ref[r, :]
            g = _dot_nt(h, wgu_ref[0])
            u = _dot_nt(h, wgu_ref[1])
            gs_ref[r, :] = g.astype(BF16)
            us_ref[r, :] = u.astype(BF16)
            a = (g * _sigmoid(g) * u).astype(BF16)
            parts.append(_dot(a, wd_ref[...]))
        part = jnp.concatenate(parts, axis=0)

        @pl.when(k == 0)
        def _():
            acc[rows, :] = part

        @pl.when(jnp.logical_and(k > 0, k < last))
        def _():
            acc[rows, :] += part

        @pl.when(k == last)
        def _():
            f = acc[rows, :] + part
            f_ref[...] = f
            x2 = x1_ref[...] + _rms(f, gpost_ref[...])
            x2_ref[...] = x2
            h3_ref[...] = _rms(x2, gple_ref[...]).astype(BF16)

    def late(k, i):
        return (jnp.where(k == last, i, 0), 0)

    return _launch(
        body, name=f"fwd_ffn{layer}", grid=(FF_CHUNKS, nt),
        in_specs=[pl.BlockSpec((tm, D_MODEL), lambda k, i: (i, 0)),
                  pl.BlockSpec((tm, D_MODEL), late),
                  pl.BlockSpec((None, 2, FF_BLOCK, D_MODEL), lambda k, i: (k, 0, 0, 0)),
                  pl.BlockSpec((FF_BLOCK, D_MODEL), lambda k, i: (k, 0)),
                  pl.BlockSpec((1, D_MODEL), lambda k, i: (0, 0)),
                  pl.BlockSpec((1, D_MODEL), lambda k, i: (0, 0))],
        out_specs=[pl.BlockSpec((None, tm, FF_BLOCK), lambda k, i: (k, i, 0)),
                   pl.BlockSpec((None, tm, FF_BLOCK), lambda k, i: (k, i, 0)),
                   pl.BlockSpec((tm, D_MODEL), late),
                   pl.BlockSpec((tm, D_MODEL), late),
                   pl.BlockSpec((tm, D_MODEL), late)],
        out_shape=[jax.ShapeDtypeStruct((FF_CHUNKS, T, FF_BLOCK), BF16),
                   jax.ShapeDtypeStruct((FF_CHUNKS, T, FF_BLOCK), BF16),
                   jax.ShapeDtypeStruct((T, D_MODEL), F32),
                   jax.ShapeDtypeStruct((T, D_MODEL), F32),
                   jax.ShapeDtypeStruct((T, D_MODEL), BF16)],
        scratch_shapes=[pltpu.VMEM((T, D_MODEL), F32)],
        args=(h2, x1, wgu, wd, g_post, g_ple), vmem=VMEM_BIG, job=job)


def _fwd_ple(layer, x2, h3, p, wgate, wproj, g_post, job=None):
    T = x2.shape[0]
    nt = T // ROW_TILE

    def body(x2_ref, h3_ref, p_ref, wg_ref, wp_ref, gpost_ref, x3_ref, z_ref, pe_ref):
        z = _dot(h3_ref[...], wg_ref[...])
        pe = _dot(p_ref[...].astype(BF16), wp_ref[...])
        z_ref[...] = z
        pe_ref[...] = pe
        x3_ref[...] = x2_ref[...] + _rms(pe * _sigmoid(z), gpost_ref[...])

    return _launch(
        body, name=f"fwd_ple{layer}", grid=(nt,),
        in_specs=[_row_spec(D_MODEL), _row_spec(D_MODEL), _row_spec(PLE_DIM), _full_spec((D_MODEL, D_MODEL)),
                  _full_spec((PLE_DIM, D_MODEL)), _vec_spec()],
        out_specs=[_row_spec(D_MODEL)] * 3, out_shape=[jax.ShapeDtypeStruct((T, D_MODEL), F32)] * 3,
        args=(x2, h3, p, wgate, wproj, g_post), job=job)


def _fwd_qkv(x3, g_kv, g_mix, wkv, wq, job=None):
    T = x3.shape[0]
    nt = T // ROW_TILE

    def body(x_ref, gkv_ref, gmix_ref, wkv_ref, wq_ref, hk_ref, h1_ref, q_ref, kv_ref):
        xv = x_ref[...]
        r = _rstd(xv)
        hk = (xv * r * gkv_ref[...]).astype(BF16)
        h1 = (xv * r * gmix_ref[...]).astype(BF16)
        hk_ref[...] = hk
        h1_ref[...] = h1
        kv_ref[...] = _dot(hk, wkv_ref[...]).astype(BF16)
        q_ref[...] = _dot(h1, wq_ref[...]).astype(BF16)

    return _launch(
        body, name="fwd_qkv", grid=(nt,),
        in_specs=[_row_spec(D_MODEL), _vec_spec(), _vec_spec(), _full_spec((D_MODEL, 2 * KV_DIM)),
                  _full_spec((D_MODEL, D_MODEL))],
        out_specs=[_row_spec(D_MODEL), _row_spec(D_MODEL), _row_spec(D_MODEL), _row_spec(2 * KV_DIM)],
        out_shape=[jax.ShapeDtypeStruct((T, D_MODEL), BF16)] * 3 + [jax.ShapeDtypeStruct((T, 2 * KV_DIM), BF16)],
        args=(x3, g_kv, g_mix, wkv, wq), job=job)


def _alibi_slope(h):
    return 2.0 ** (-8.0 * (h + 1) / N_HEADS)


ATT_SUB = 32
ATT_GROUP_ROWS = GQA_GROUP * ATT_BLOCK


def _att_mask(n, row0):
    qi = lax.broadcasted_iota(jnp.int32, (ATT_SUB, 2 * ATT_BLOCK), 0) + row0
    si = lax.broadcasted_iota(jnp.int32, (ATT_SUB, 2 * ATT_BLOCK), 1)
    rel = ATT_BLOCK + qi - si
    valid = (rel >= 0) & (rel < ATT_BLOCK) & ((si >= ATT_BLOCK) | (n > 0))
    return rel.astype(F32), valid


def _att_probs(raw, relf, valid, slope, sink):
    s = jnp.where(valid, raw * ATT_SCALE - slope * relf, NEG_INF)
    m = jnp.maximum(jnp.max(s, axis=-1, keepdims=True), sink)
    e = jnp.exp(s - m)
    es = jnp.exp(sink - m)
    inv = 1.0 / (jnp.sum(e, axis=-1, keepdims=True) + es)
    return e * inv, es * inv


def _stack_heads(ref, kh):
    first = kh * GQA_GROUP
    return jnp.concatenate([ref[:, (first + g) * HEAD_DIM:(first + g + 1) * HEAD_DIM] for g in range(GQA_GROUP)], axis=0)


def _unstack_heads(stacked):
    return [stacked[g * ATT_BLOCK:(g + 1) * ATT_BLOCK, :] for g in range(GQA_GROUP)]


def _fwd_attention(q, kpad, vpad, sinks, job=None):
    T = q.shape[0]
    nb = T // ATT_BLOCK

    def body(q_ref, k_ref, v_ref, sink_ref, o_ref, s_scr, p_scr):
        n = pl.program_id(0)
        start = pl.multiple_of(n * ATT_BLOCK, ATT_BLOCK)
        kw = k_ref[pl.ds(start, 2 * ATT_BLOCK), :]
        vw = v_ref[pl.ds(start, 2 * ATT_BLOCK), :]
        outs = []
        for kh in range(N_KV_HEADS):
            kk = kw[:, kh * HEAD_DIM:(kh + 1) * HEAD_DIM]
            vv = vw[:, kh * HEAD_DIM:(kh + 1) * HEAD_DIM]
            s_scr[...] = _dot_nt(_stack_heads(q_ref, kh), kk)
            for g in range(GQA_GROUP):
                h = kh * GQA_GROUP + g
                for row0 in range(0, ATT_BLOCK, ATT_SUB):
                    rows = pl.ds(g * ATT_BLOCK + row0, ATT_SUB)
                    relf, valid = _att_mask(n, row0)
                    pr, _ = _att_probs(s_scr[rows, :], relf, valid, _alibi_slope(h), sink_ref[0, h])
                    p_scr[rows, :] = pr.astype(BF16)
            outs += _unstack_heads(_dot(p_scr[...], vv))
        o_ref[...] = jnp.concatenate(outs, axis=1).astype(BF16)

    return _launch(
        body, name="fwd_attention", grid=(nb,),
        in_specs=[_row_spec(D_MODEL, ATT_BLOCK), _full_spec((T + ATT_BLOCK, KV_DIM)), _full_spec((T + ATT_BLOCK, KV_DIM)),
                  pl.BlockSpec(memory_space=pltpu.SMEM)],
        out_specs=[_row_spec(D_MODEL, ATT_BLOCK)],
        out_shape=[jax.ShapeDtypeStruct((T, D_MODEL), BF16)],
        scratch_shapes=[pltpu.VMEM((ATT_GROUP_ROWS, 2 * ATT_BLOCK), F32), pltpu.VMEM((ATT_GROUP_ROWS, 2 * ATT_BLOCK), BF16)],
        args=(q, kpad, vpad, sinks), job=job)


def _fwd_attn_out(attn, x, wo, g_post, g_ffn, job=None):
    T = x.shape[0]
    nt = T // ROW_TILE

    def body(a_ref, x_ref, wo_ref, gpost_ref, gffn_ref, y_ref, x1_ref, h2_ref):
        y = _dot(a_ref[...], wo_ref[...])
        y_ref[...] = y
        x1 = x_ref[...] + _rms(y, gpost_ref[...])
        x1_ref[...] = x1
        h2_ref[...] = _rms(x1, gffn_ref[...]).astype(BF16)

    return _launch(
        body, name="fwd_attn_out", grid=(nt,),
        in_specs=[_row_spec(D_MODEL), _row_spec(D_MODEL), _full_spec((D_MODEL, D_MODEL)), _vec_spec(), _vec_spec()],
        out_specs=[_row_spec(D_MODEL)] * 3,
        out_shape=[jax.ShapeDtypeStruct((T, D_MODEL), F32), jax.ShapeDtypeStruct((T, D_MODEL), F32),
                   jax.ShapeDtypeStruct((T, D_MODEL), BF16)],
        args=(attn, x, wo, g_post, g_ffn), job=job)


def _bwd_ple(layer, dx3, x2, z, pe, h3, p, f, wgate, g_ple_post, g_ple, g_post_ffn, job=None):
    T = x2.shape[0]
    tm = ROW_TILE
    nt = T // tm

    def body(dx3_ref, x2_ref, z_ref, pe_ref, h3_ref, p_ref, f_ref, wg_ref, gpp_ref, gp_ref, gpf_ref,
             dx2_ref, df_ref, dwg_ref, dwp_ref, dgpp_ref, dgp_ref, dgpf_ref, acc_g, acc_p):
        i = pl.program_id(0)
        first = i == 0
        dx3v = dx3_ref[...]
        gate = _sigmoid(z_ref[...])
        pev = pe_ref[...]
        de, dgpp = _rms_bwd(pev * gate, gpp_ref[...], dx3v)
        dpe = (de * gate).astype(BF16)
        dz = (de * pev * gate * (1.0 - gate)).astype(BF16)
        _acc(acc_p, _dot_tn(p_ref[...].astype(BF16), dpe), first)
        _acc(acc_g, _dot_tn(h3_ref[...], dz), first)
        dh3 = _dot_nt(dz, wg_ref[...])
        dxn, dgp = _rms_bwd(x2_ref[...], gp_ref[...], dh3)
        dx2 = dx3v + dxn
        dx2_ref[...] = dx2
        df, dgpf = _rms_bwd(f_ref[...], gpf_ref[...], dx2)
        df_ref[...] = df.astype(BF16)
        _acc(dgpp_ref, dgpp, first)
        _acc(dgp_ref, dgp, first)
        _acc(dgpf_ref, dgpf, first)

        @pl.when(i == nt - 1)
        def _():
            dwg_ref[...] = acc_g[...].astype(BF16)
            dwp_ref[...] = acc_p[...].astype(BF16)

    return _launch(
        body, name=f"bwd_ple{layer}", grid=(nt,),
        in_specs=[_row_spec(D_MODEL)] * 5 + [_row_spec(PLE_DIM), _row_spec(D_MODEL), _full_spec((D_MODEL, D_MODEL)),
                  _vec_spec(), _vec_spec(), _vec_spec()],
        out_specs=[_row_spec(D_MODEL), _row_spec(D_MODEL), _full_spec((D_MODEL, D_MODEL)), _full_spec((PLE_DIM, D_MODEL)),
                   _vec_spec(), _vec_spec(), _vec_spec()],
        out_shape=[jax.ShapeDtypeStruct((T, D_MODEL), F32), jax.ShapeDtypeStruct((T, D_MODEL), BF16),
                   jax.ShapeDtypeStruct((D_MODEL, D_MODEL), BF16), jax.ShapeDtypeStruct((PLE_DIM, D_MODEL), BF16)]
                  + [jax.ShapeDtypeStruct((1, D_MODEL), F32)] * 3,
        scratch_shapes=[pltpu.VMEM((D_MODEL, D_MODEL), F32), pltpu.VMEM((PLE_DIM, D_MODEL), F32)],
        args=(dx3, x2, z, pe, h3, p, f, wgate, g_ple_post, g_ple, g_post_ffn), vmem=VMEM_BIG, job=job)


def _ple_loss_bwd(layer, x2, h3, p, f, target, wgate, wproj, g_ple_post, g_ple, g_post_ffn, job=None):
    T = x2.shape[0]
    tm = ROW_TILE
    nt = T // tm

    def body(x2_ref, h3_ref, p_ref, f_ref, tgt_ref, wg_ref, wp_ref, gpp_ref, gp_ref, gpf_ref,
             dx2_ref, df_ref, dwg_ref, dwp_ref, dgpp_ref, dgp_ref, dgpf_ref, loss_ref, acc_g, acc_p):
        i = pl.program_id(0)
        first = i == 0
        h3 = h3_ref[...]
        pb = p_ref[...].astype(BF16)
        x2v = x2_ref[...]
        gate = _sigmoid(_dot(h3, wg_ref[...]))
        pev = _dot(pb, wp_ref[...])
        e = pev * gate
        err = x2v + _rms(e, gpp_ref[...]) - tgt_ref[...]
        _acc(loss_ref, 0.5 * jnp.sum(jnp.mean(err * err, axis=-1, keepdims=True), axis=0, keepdims=True), first)
        dx3v = err * (1.0 / D_MODEL)
        de, dgpp = _rms_bwd(e, gpp_ref[...], dx3v)
        dpe = (de * gate).astype(BF16)
        dz = (de * pev * gate * (1.0 - gate)).astype(BF16)
        _acc(acc_p, _dot_tn(pb, dpe), first)
        _acc(acc_g, _dot_tn(h3, dz), first)
        dxn, dgp = _rms_bwd(x2v, gp_ref[...], _dot_nt(dz, wg_ref[...]))
        dx2 = dx3v + dxn
        dx2_ref[...] = dx2
        df, dgpf = _rms_bwd(f_ref[...], gpf_ref[...], dx2)
        df_ref[...] = df.astype(BF16)
        _acc(dgpp_ref, dgpp, first)
        _acc(dgp_ref, dgp, first)
        _acc(dgpf_ref, dgpf, first)

        @pl.when(i == nt - 1)
        def _():
            dwg_ref[...] = acc_g[...].astype(BF16)
            dwp_ref[...] = acc_p[...].astype(BF16)

    return _launch(
        body, name=f"ple_loss_bwd{layer}", grid=(nt,),
        in_specs=[_row_spec(D_MODEL), _row_spec(D_MODEL), _row_spec(PLE_DIM), _row_spec(D_MODEL), _row_spec(D_MODEL),
                  _full_spec((D_MODEL, D_MODEL)), _full_spec((PLE_DIM, D_MODEL)), _vec_spec(), _vec_spec(), _vec_spec()],
        out_specs=[_row_spec(D_MODEL), _row_spec(D_MODEL), _full_spec((D_MODEL, D_MODEL)), _full_spec((PLE_DIM, D_MODEL)),
                   _vec_spec(), _vec_spec(), _vec_spec(), _full_spec((1, 1))],
        out_shape=[jax.ShapeDtypeStruct((T, D_MODEL), F32), jax.ShapeDtypeStruct((T, D_MODEL), BF16),
                   jax.ShapeDtypeStruct((D_MODEL, D_MODEL), BF16), jax.ShapeDtypeStruct((PLE_DIM, D_MODEL), BF16)]
                  + [jax.ShapeDtypeStruct((1, D_MODEL), F32)] * 3 + [jax.ShapeDtypeStruct((1, 1), F32)],
        scratch_shapes=[pltpu.VMEM((D_MODEL, D_MODEL), F32), pltpu.VMEM((PLE_DIM, D_MODEL), F32)],
        args=(x2, h3, p, f, target, wgate, wproj, g_ple_post, g_ple, g_post_ffn), vmem=VMEM_BIG, job=job)


def _bwd_ffn_act(layer, df, gs, us, wgu, wd, job=None):
    T = df.shape[0]
    tm = min(FFN_ROW_TILE, T)
    nt = T // tm
    sub = tm // FFN_SUB_TILES
    last = FF_CHUNKS - 1

    def body(df_ref, gs_ref, us_ref, wgu_ref, wd_ref, dh_ref, dg_ref, du_ref, a_ref, acc_h):
        k = pl.program_id(0)
        i = pl.program_id(1)
        rows = pl.ds(pl.multiple_of(i * tm, tm), tm)
        dhs = []
        for s in range(FFN_SUB_TILES):
            r = pl.ds(s * sub, sub)
            g = gs_ref[r, :].astype(F32)
            u = us_ref[r, :].astype(F32)
            sg = _sigmoid(g)
            silu = g * sg
            a_ref[r, :] = (silu * u).astype(BF16)
            da = _dot_nt(df_ref[r, :], wd_ref[...])
            dg = (da * u * (sg * (1.0 + g * (1.0 - sg)))).astype(BF16)
            du = (da * silu).astype(BF16)
            dg_ref[r, :] = dg
            du_ref[r, :] = du
            dhs.append(_dot(dg, wgu_ref[0]) + _dot(du, wgu_ref[1]))
        dh = jnp.concatenate(dhs, axis=0)

        @pl.when(k == 0)
        def _():
            acc_h[rows, :] = dh

        @pl.when(jnp.logical_and(k > 0, k < last))
        def _():
            acc_h[rows, :] += dh

        @pl.when(k == last)
        def _():
            dh_ref[...] = acc_h[rows, :] + dh

    chunk_rows = pl.BlockSpec((None, tm, FF_BLOCK), lambda k, i: (k, i, 0))
    saved = jax.ShapeDtypeStruct((FF_CHUNKS, T, FF_BLOCK), BF16)
    return _launch(
        body, name=f"bwd_ffn_act{layer}", grid=(FF_CHUNKS, nt),
        in_specs=[pl.BlockSpec((tm, D_MODEL), lambda k, i: (i, 0)), chunk_rows, chunk_rows,
                  pl.BlockSpec((None, 2, FF_BLOCK, D_MODEL), lambda k, i: (k, 0, 0, 0)),
                  pl.BlockSpec((FF_BLOCK, D_MODEL), lambda k, i: (k, 0))],
        out_specs=[pl.BlockSpec((tm, D_MODEL), lambda k, i: (jnp.where(k == last, i, 0), 0)),
                   chunk_rows, chunk_rows, chunk_rows],
        out_shape=[jax.ShapeDtypeStruct((T, D_MODEL), F32), saved, saved, saved],
        scratch_shapes=[pltpu.VMEM((T, D_MODEL), F32)],
        args=(df, gs, us, wgu, wd), vmem=VMEM_BIG, job=job)


def _bwd_ffn_dw(layer, q, h2, df, dg, du, a, job=None):
    T = h2.shape[0]

    def body(h_ref, df_ref, dg_ref, du_ref, a_ref, dgu_ref, dwd_ref):
        h = h_ref[...]
        dgu_ref[0] = _dot_tn(dg_ref[...], h).astype(BF16)
        dgu_ref[1] = _dot_tn(du_ref[...], h).astype(BF16)
        dwd_ref[...] = _dot_tn(a_ref[...], df_ref[...]).astype(BF16)

    cols = pl.BlockSpec((T, FF_PART), lambda k: (0, q))
    chunk = pl.BlockSpec((None, T, FF_BLOCK), lambda k: (k, 0, 0))
    return _launch(
        body, name=f"bwd_ffn_dw{layer}_{q}", grid=(FF_CHUNKS,),
        in_specs=[cols, cols, chunk, chunk, chunk],
        out_specs=[pl.BlockSpec((None, 2, FF_BLOCK, FF_PART), lambda k: (k, 0, 0, 0)),
                   pl.BlockSpec((FF_BLOCK, FF_PART), lambda k: (k, 0))],
        out_shape=[jax.ShapeDtypeStruct((FF_CHUNKS, 2, FF_BLOCK, FF_PART), BF16),
                   jax.ShapeDtypeStruct((D_FF, FF_PART), BF16)],
        args=(h2, df, dg, du, a), vmem=VMEM_BIG, job=job)


def _bwd_attn_out(dx2, dh2, x1, y, attn, wo, g_ffn, g_post, job=None):
    T = x1.shape[0]
    nt = T // ROW_TILE

    def body(dx2_ref, dh2_ref, x1_ref, y_ref, a_ref, wo_ref, gffn_ref, gpost_ref,
             dx1_ref, da_ref, dwo_ref, dgf_ref, dgp_ref, acc):
        i = pl.program_id(0)
        first = i == 0
        dxn, dgf = _rms_bwd(x1_ref[...], gffn_ref[...], dh2_ref[...])
        dx1 = dx2_ref[...] + dxn
        dx1_ref[...] = dx1
        dy, dgp = _rms_bwd(y_ref[...], gpost_ref[...], dx1)
        dyb = dy.astype(BF16)
        da_ref[...] = _dot_nt(dyb, wo_ref[...]).astype(BF16)
        _acc(acc, _dot_tn(a_ref[...], dyb), first)
        _acc(dgf_ref, dgf, first)
        _acc(dgp_ref, dgp, first)

        @pl.when(i == nt - 1)
        def _():
            dwo_ref[...] = acc[...].astype(BF16)

    return _launch(
        body, name="bwd_attn_out", grid=(nt,),
        in_specs=[_row_spec(D_MODEL)] * 5 + [_full_spec((D_MODEL, D_MODEL)), _vec_spec(), _vec_spec()],
        out_specs=[_row_spec(D_MODEL), _row_spec(D_MODEL), _full_spec((D_MODEL, D_MODEL)), _vec_spec(), _vec_spec()],
        out_shape=[jax.ShapeDtypeStruct((T, D_MODEL), F32), jax.ShapeDtypeStruct((T, D_MODEL), BF16),
                   jax.ShapeDtypeStruct((D_MODEL, D_MODEL), BF16)] + [jax.ShapeDtypeStruct((1, D_MODEL), F32)] * 2,
        scratch_shapes=[pltpu.VMEM((D_MODEL, D_MODEL), F32)],
        args=(dx2, dh2, x1, y, attn, wo, g_ffn, g_post), job=job)


def _bwd_attention(q, dattn, kpad, vpad, sinks, job=None):
    T = q.shape[0]
    nb = T // ATT_BLOCK

    def body(q_ref, do_ref, k_ref, v_ref, sink_ref, dq_ref, dk_ref, dv_ref, ds_ref, s_scr, dp_scr, p_scr, dsb_scr):
        n = pl.program_id(0)

        @pl.when(n == 0)
        def _():
            dk_ref[...] = jnp.zeros_like(dk_ref)
            dv_ref[...] = jnp.zeros_like(dv_ref)
            ds_ref[...] = jnp.zeros_like(ds_ref)

        start = pl.multiple_of(n * ATT_BLOCK, ATT_BLOCK)
        win = pl.ds(start, 2 * ATT_BLOCK)
        kw = k_ref[win, :]
        vw = v_ref[win, :]
        lane = lax.broadcasted_iota(jnp.int32, (1, ATT_BLOCK), 1)
        dsink = jnp.zeros((1, ATT_BLOCK), F32)
        dqs, dks, dvs = [], [], []
        for kh in range(N_KV_HEADS):
            kk = kw[:, kh * HEAD_DIM:(kh + 1) * HEAD_DIM]
            vv = vw[:, kh * HEAD_DIM:(kh + 1) * HEAD_DIM]
            qs = _stack_heads(q_ref, kh)
            dos = _stack_heads(do_ref, kh)
            s_scr[...] = _dot_nt(qs, kk)
            dp_scr[...] = _dot_nt(dos, vv)
            for g in range(GQA_GROUP):
                h = kh * GQA_GROUP + g
                dsink_h = jnp.zeros((1, 1), F32)
                for row0 in range(0, ATT_BLOCK, ATT_SUB):
                    rows = pl.ds(g * ATT_BLOCK + row0, ATT_SUB)
                    relf, valid = _att_mask(n, row0)
                    pr, ps = _att_probs(s_scr[rows, :], relf, valid, _alibi_slope(h), sink_ref[0, h])
                    dp = dp_scr[rows, :]
                    delta = jnp.sum(pr * dp, axis=-1, keepdims=True)
                    dsb_scr[rows, :] = (pr * (dp - delta) * ATT_SCALE).astype(BF16)
                    p_scr[rows, :] = pr.astype(BF16)
                    dsink_h = dsink_h - jnp.sum(ps * delta, axis=0, keepdims=True)
                dsink = dsink + jnp.where(lane == h, dsink_h, 0.0)
            dsb = dsb_scr[...]
            dqs += _unstack_heads(_dot(dsb, kk))
            dks.append(_dot_tn(dsb, qs))
            dvs.append(_dot_tn(p_scr[...], dos))
        dq_ref[...] = jnp.concatenate(dqs, axis=1).astype(BF16)
        dk_ref[win, :] += jnp.concatenate(dks, axis=1)
        dv_ref[win, :] += jnp.concatenate(dvs, axis=1)
        ds_ref[...] += dsink

    return _launch(
        body, name="bwd_attention", grid=(nb,),
        in_specs=[_row_spec(D_MODEL, ATT_BLOCK), _row_spec(D_MODEL, ATT_BLOCK), _full_spec((T + ATT_BLOCK, KV_DIM)),
                  _full_spec((T + ATT_BLOCK, KV_DIM)), pl.BlockSpec(memory_space=pltpu.SMEM)],
        out_specs=[_row_spec(D_MODEL, ATT_BLOCK), _full_spec((T + ATT_BLOCK, KV_DIM)), _full_spec((T + ATT_BLOCK, KV_DIM)),
                   _full_spec((1, ATT_BLOCK))],
        out_shape=[jax.ShapeDtypeStruct((T, D_MODEL), BF16), jax.ShapeDtypeStruct((T + ATT_BLOCK, KV_DIM), F32),
                   jax.ShapeDtypeStruct((T + ATT_BLOCK, KV_DIM), F32), jax.ShapeDtypeStruct((1, ATT_BLOCK), F32)],
        scratch_shapes=[pltpu.VMEM((ATT_GROUP_ROWS, 2 * ATT_BLOCK), F32)] * 2
                       + [pltpu.VMEM((ATT_GROUP_ROWS, 2 * ATT_BLOCK), BF16)] * 2,
        args=(q, dattn, kpad, vpad, sinks), vmem=VMEM_BIG, job=job)


def _bwd_qkv(dxres, dq, dkv, x3, h1, hk, wq, wkv, g_mix, g_kv, job=None):
    T = x3.shape[0]
    nt = T // ROW_TILE

    def body(dxr_ref, dq_ref, dkv_ref, x_ref, h1_ref, hk_ref, wq_ref, wkv_ref, gmix_ref, gkv_ref,
             dx_ref, dwq_ref, dwkv_ref, dgm_ref, dgk_ref, acc_q, acc_kv):
        i = pl.program_id(0)
        first = i == 0
        dqv = dq_ref[...]
        dkvv = dkv_ref[...]
        xv = x_ref[...]
        d1, dgm = _rms_bwd(xv, gmix_ref[...], _dot_nt(dqv, wq_ref[...]))
        d2, dgk = _rms_bwd(xv, gkv_ref[...], _dot_nt(dkvv, wkv_ref[...]))
        dx_ref[...] = dxr_ref[...] + d1 + d2
        _acc(acc_q, _dot_tn(h1_ref[...], dqv), first)
        _acc(acc_kv, _dot_tn(hk_ref[...], dkvv), first)
        _acc(dgm_ref, dgm, first)
        _acc(dgk_ref, dgk, first)

        @pl.when(i == nt - 1)
        def _():
            dwq_ref[...] = acc_q[...].astype(BF16)
            dwkv_ref[...] = acc_kv[...].astype(BF16)

    return _launch(
        body, name="bwd_qkv", grid=(nt,),
        in_specs=[_row_spec(D_MODEL), _row_spec(D_MODEL), _row_spec(2 * KV_DIM), _row_spec(D_MODEL), _row_spec(D_MODEL),
                  _row_spec(D_MODEL), _full_spec((D_MODEL, D_MODEL)), _full_spec((D_MODEL, 2 * KV_DIM)), _vec_spec(),
                  _vec_spec()],
        out_specs=[_row_spec(D_MODEL), _full_spec((D_MODEL, D_MODEL)), _full_spec((D_MODEL, 2 * KV_DIM)), _vec_spec(),
                   _vec_spec()],
        out_shape=[jax.ShapeDtypeStruct((T, D_MODEL), F32), jax.ShapeDtypeStruct((D_MODEL, D_MODEL), BF16),
                   jax.ShapeDtypeStruct((D_MODEL, 2 * KV_DIM), BF16)] + [jax.ShapeDtypeStruct((1, D_MODEL), F32)] * 2,
        scratch_shapes=[pltpu.VMEM((D_MODEL, D_MODEL), F32), pltpu.VMEM((D_MODEL, 2 * KV_DIM), F32)],
        args=(dxres, dq, dkv, x3, h1, hk, wq, wkv, g_mix, g_kv), job=job)


def _bwd_pool_mixer(dx2, dh2, x1, x, yraw, d, wp, scale, g_ffn, g_post, g_pre, job=None):
    T = x.shape[0]
    tm = ROW_TILE
    nt = T // tm

    def body(dx2_ref, dh2_ref, x1_ref, x_ref, yraw_ref, d_ref, wp_ref, sc_ref, gffn_ref, gpost_ref, gpre_ref,
             dx_ref, dwp_ref, dsc_ref, dgf_ref, dgp_ref, dgm_ref, carry, acc):
        i = pl.program_id(0)
        first = i == 0
        tile = nt - 1 - i

        @pl.when(first)
        def _():
            carry[...] = jnp.zeros_like(carry)

        dxn, dgf = _rms_bwd(x1_ref[...], gffn_ref[...], dh2_ref[...])
        dx1 = dx2_ref[...] + dxn
        yraw = yraw_ref[...]
        sc = sc_ref[...]
        dy, dgp = _rms_bwd(yraw * sc, gpost_ref[...], dx1)
        dsc = jnp.sum(dy * yraw, axis=0, keepdims=True)
        dyb = (dy * sc).astype(BF16)
        dv = d_ref[...]
        dds = []
        for g in range(N_POOL_GROUPS):
            cols = slice(g * POOL_GROUP, (g + 1) * POOL_GROUP)
            dds.append(_dot_nt(dyb[:, cols], wp_ref[g]))
            _acc(acc.at[g], _dot_tn(dv[:, cols], dyb[:, cols]), first)
        dd = jnp.concatenate(dds, axis=1)
        e = dd / _pool_counts(tile * tm, tm)
        ext = jnp.concatenate([e, carry[...]], axis=0)
        carry[...] = e[:POOL_HALO, :]
        sums = _window_sums(ext, lambda k: tm + POOL_HALO - k)[:tm, :]
        dxm, dgm = _rms_bwd(x_ref[...], gpre_ref[...], sums - dd)
        dx_ref[...] = dx1 + dxm
        _acc(dsc_ref, dsc, first)
        _acc(dgf_ref, dgf, first)
        _acc(dgp_ref, dgp, first)
        _acc(dgm_ref, dgm, first)

        @pl.when(i == nt - 1)
        def _():
            dwp_ref[...] = acc[...].astype(BF16)

    rev = pl.BlockSpec((tm, D_MODEL), lambda i: (nt - 1 - i, 0))
    return _launch(
        body, name="bwd_pool_mixer", grid=(nt,),
        in_specs=[rev] * 6 + [_full_spec((N_POOL_GROUPS, POOL_GROUP, POOL_GROUP))] + [_vec_spec()] * 4,
        out_specs=[rev, _full_spec((N_POOL_GROUPS, POOL_GROUP, POOL_GROUP))] + [_vec_spec()] * 4,
        out_shape=[jax.ShapeDtypeStruct((T, D_MODEL), F32),
                   jax.ShapeDtypeStruct((N_POOL_GROUPS, POOL_GROUP, POOL_GROUP), BF16)]
                  + [jax.ShapeDtypeStruct((1, D_MODEL), F32)] * 4,
        scratch_shapes=[pltpu.VMEM((POOL_HALO, D_MODEL), F32), pltpu.VMEM((N_POOL_GROUPS, POOL_GROUP, POOL_GROUP), F32)],
        args=(dx2, dh2, x1, x, yraw, d, wp, scale, g_ffn, g_post, g_pre), job=job)


def _my_place():
    return lax.axis_index("x"), lax.axis_index("y"), lax.axis_index("c")


def _dev_index(px, py, pc):
    return 4 * px + 2 * py + pc


def _peer_by_relation(r):
    x, y, c = _my_place()
    return (x ^ ((r >> 2) & 1), y ^ ((r >> 1) & 1), c ^ (r & 1))


def _slot_pool(ref, j):
    return ref.at[:, pl.ds(pl.multiple_of(j * 32, 32), 32), :]


def _slot_scale(ref, j):
    return ref.at[:, pl.ds(pl.multiple_of(j * 128, 128), 128)]


def _slot_rows128(ref, j):
    return ref.at[pl.ds(pl.multiple_of(j * 128, 128), 128), :]


def _slot_gu(ref, j):
    return ref.at[j % FF_CHUNKS, j // FF_CHUNKS]


def _slot_wd(ref, j):
    return ref.at[pl.ds(pl.multiple_of(j * WD_ROWS, 16), WD_ROWS), :]


def _slot_cols128(ref, j):
    return ref.at[:, pl.ds(pl.multiple_of(j * 128, 128), 128)]


_GATHERED = {
    "pool": ((N_POOL_GROUPS, POOL_GROUP, POOL_GROUP), BF16, _slot_pool),
    "scale": ((1, D_MODEL), F32, _slot_scale),
    "kv": ((D_MODEL, 2 * KV_DIM), BF16, _slot_rows128),
    "q": ((D_MODEL, D_MODEL), BF16, _slot_rows128),
    "o": ((D_MODEL, D_MODEL), BF16, _slot_rows128),
    "gu": ((FF_CHUNKS, 2, FF_BLOCK, D_MODEL), BF16, _slot_gu),
    "wd": ((D_FF, D_MODEL), BF16, _slot_wd),
    "gate": ((D_MODEL, D_MODEL), BF16, _slot_rows128),
    "proj": ((PLE_DIM, D_MODEL), BF16, _slot_cols128),
}


def _no_compute():
    pass


class _AllGather:
    def __init__(self, names, shards):
        self.kinds = [_GATHERED[n.rstrip("01")] for n in names]
        self.args = [shards[n] for n in names]
        self.out_shape = [jax.ShapeDtypeStruct(shape, dtype) for shape, dtype, _ in self.kinds]
        n = len(names)
        self.scratch = [pltpu.SemaphoreType.DMA((n, 7)), pltpu.SemaphoreType.DMA((n, 7)), pltpu.SemaphoreType.DMA((n,))]

    def _plan(self, srcs, outs, sems):
        send_sems, recv_sems, local_sems = sems
        x, y, c = _my_place()

        def slot(t, dev):
            return self.kinds[t][2](outs[t], _dev_index(*dev))

        def copy(t, k, block, to, src=None):
            return pltpu.make_async_remote_copy(
                src_ref=slot(t, block) if src is None else src, dst_ref=slot(t, block),
                send_sem=send_sems.at[t, k], recv_sem=recv_sems.at[t, k], device_id=to, device_id_type=MESH)

        return types.SimpleNamespace(
            copy=copy, core=c, me=(x, y, c), sibling=(x, y, 1 - c),
            x_chip=(1 - x, y), y_chip=(x, 1 - y), far_chip=(1 - x, 1 - y),
            via=(x ^ (1 - c), y ^ c),
            onto=(x ^ c, y ^ (1 - c)),
            k_via=1 + c, k_onto=2 - c,
            local=[pltpu.make_async_copy(srcs[t], slot(t, (x, y, c)), local_sems.at[t]) for t in range(len(srcs))])

    def start(self, srcs, outs, sems):
        p = self._plan(srcs, outs, sems)
        for cp in p.local:
            cp.start()
        for t in range(len(srcs)):
            p.copy(t, 0, p.me, p.sibling, src=srcs[t]).start()
            p.copy(t, 1, p.me, (*p.x_chip, p.core), src=srcs[t]).start()
            p.copy(t, 2, p.me, (*p.y_chip, p.core), src=srcs[t]).start()

    def mid(self, srcs, outs, sems):
        p = self._plan(srcs, outs, sems)
        for t in range(len(srcs)):
            block = (*p.via, p.core)
            p.copy(t, p.k_via, block, p.me).wait_recv()
            p.copy(t, 3, block, (*p.onto, p.core)).start()
            p.copy(t, 3 + p.k_via, block, p.sibling).start()

    def finish(self, srcs, outs, sems):
        p = self._plan(srcs, outs, sems)
        n = len(srcs)
        for t in range(n):
            block = (*p.onto, p.core)
            p.copy(t, p.k_onto, block, p.me).wait_recv()
            p.copy(t, 3 + p.k_onto, block, p.sibling).start()
        for t in range(n):
            block = (*p.far_chip, p.core)
            p.copy(t, 3, block, p.me).wait_recv()
            p.copy(t, 6, block, p.sibling).start()
        other = 1 - p.core
        for t in range(n):
            p.copy(t, 0, (*p.me[:2], other), p.me).wait_recv()
            for k, chip in ((4, p.x_chip), (5, p.y_chip), (6, p.far_chip)):
                p.copy(t, k, (*chip, other), p.me).wait_recv()
            for k in range(7):
                p.copy(t, k, p.me, p.sibling).wait_send()
        for cp in p.local:
            cp.wait()


def _all_gather_only(name, names, shards):
    return _launch(_no_compute, name=name, grid=(), in_specs=[], out_specs=[], out_shape=[], args=(),
                   job=_AllGather(names, shards))[1]


def _block_pool(ref, j):
    return ref.at[:, pl.ds(pl.multiple_of(j * 32, 32), 32), :]


def _block_rows128(ref, j):
    return ref.at[pl.ds(pl.multiple_of(j * 128, 128), 128), :]


def _block_gu(ref, j):
    return ref.at[j % FF_CHUNKS, j // FF_CHUNKS]


def _block_wd(ref, j):
    return ref.at[pl.ds(pl.multiple_of(j * WD_ROWS, 16), WD_ROWS), :]


def _block_cols128(ref, j):
    return ref.at[:, pl.ds(pl.multiple_of(j * 128, 128), 128)]


_SCATTERED = {
    "pool": ((N_POOL_GROUPS, 32, POOL_GROUP), _block_pool),
    "kv": ((128, 2 * KV_DIM), _block_rows128),
    "q": ((128, D_MODEL), _block_rows128),
    "o": ((128, D_MODEL), _block_rows128),
    "gu": ((FF_BLOCK, FF_PART), _block_gu),
    "wd": ((WD_ROWS, FF_PART), _block_wd),
    "gate": ((128, D_MODEL), _block_rows128),
    "proj": ((PLE_DIM, 128), _block_cols128),
}


class _SiblingSwap:
    def __init__(self, pieces):
        self.kinds = [_SCATTERED[kind] for kind, _ in pieces]
        self.args = [g for _, g in pieces]
        self.out_shape = [jax.ShapeDtypeStruct((N_CHIPS, *block), BF16) for block, _ in self.kinds]
        n = len(pieces)
        self.scratch = [pltpu.SemaphoreType.DMA((n, N_CHIPS)), pltpu.SemaphoreType.DMA((n, N_CHIPS))]

    def _copies(self, srcs, outs, sems):
        send_sems, recv_sems = sems
        x, y, c = _my_place()
        return [pltpu.make_async_remote_copy(
            src_ref=block(srcs[t], 2 * ch + 1 - c), dst_ref=outs[t].at[ch], send_sem=send_sems.at[t, ch],
            recv_sem=recv_sems.at[t, ch], device_id=(x, y, 1 - c), device_id_type=MESH)
            for t, (_, block) in enumerate(self.kinds) for ch in range(N_CHIPS)]

    def start(self, srcs, outs, sems):
        for cp in self._copies(srcs, outs, sems):
            cp.start()

    def finish(self, srcs, outs, sems):
        for cp in self._copies(srcs, outs, sems):
            cp.wait()


class _ChipScatter:
    N_BUFS = 4

    def __init__(self, pieces):
        self.kinds = [_SCATTERED[kind] for kind, _, _ in pieces]
        self.n = n = len(pieces)
        self.args = [g for _, g, _ in pieces] + [s for _, _, s in pieces]
        self.out_shape = [jax.ShapeDtypeStruct((2, *block), BF16) for block, _ in self.kinds]
        self.scratch = []
        for block, _ in self.kinds:
            self.scratch += [pltpu.VMEM((N_CHIPS, *block), BF16)] * 3 + [pltpu.VMEM((2, *block), BF16)]
        dma = pltpu.SemaphoreType.DMA
        self.scratch += [dma((n, N_CHIPS + 1)), dma((n, 2)), dma((n, 2)), dma((n,)), dma((n,)), dma((n,))]

    def _plan(self, outs, scr):
        n = self.n
        first_send, first_recv, second_send, second_recv, keep_sems = scr[self.N_BUFS * n + 1:]
        x, y, c = _my_place()
        via = (x ^ (1 - c), y ^ c)
        onto = (x ^ c, y ^ (1 - c))
        index = lambda chip: 2 * chip[0] + chip[1]
        first, second, keep = [], [], []
        for t in range(n):
            total, inbox = scr[self.N_BUFS * t + 2], scr[self.N_BUFS * t + 3]
            for k, chip in enumerate((via, (1 - x, 1 - y))):
                first.append(pltpu.make_async_remote_copy(
                    src_ref=total.at[index(chip)], dst_ref=inbox.at[k], send_sem=first_send.at[t, k],
                    recv_sem=first_recv.at[t, k], device_id=(*via, c), device_id_type=MESH))
            second.append(pltpu.make_async_remote_copy(
                src_ref=total.at[index(onto)], dst_ref=outs[t].at[1], send_sem=second_send.at[t],
                recv_sem=second_recv.at[t], device_id=(*onto, c), device_id_type=MESH))
            keep.append(pltpu.make_async_copy(total.at[index((x, y))], outs[t].at[0], keep_sems.at[t]))
        return first, second, keep, index((x, y)), index(onto)

    def start(self, ins, outs, scr):
        n = self.n
        load_sems = scr[self.N_BUFS * n]
        c = lax.axis_index("c")
        loads = []
        for t, (_, block) in enumerate(self.kinds):
            mine, theirs = scr[self.N_BUFS * t], scr[self.N_BUFS * t + 1]
            loads += [pltpu.make_async_copy(block(ins[t], 2 * ch + c), mine.at[ch], load_sems.at[t, ch])
                      for ch in range(N_CHIPS)]
            loads.append(pltpu.make_async_copy(ins[n + t], theirs, load_sems.at[t, N_CHIPS]))
        for cp in loads:
            cp.start()
        for cp in loads:
            cp.wait()
        for t in range(n):
            mine, theirs, total = scr[self.N_BUFS * t:self.N_BUFS * t + 3]
            for ch in range(N_CHIPS):
                total[ch] = (mine[ch].astype(F32) + theirs[ch].astype(F32)).astype(BF16)
        for cp in self._plan(outs, scr)[0]:
            cp.start()

    def mid(self, ins, outs, scr):
        first, second, keep, me, onto = self._plan(outs, scr)
        for cp in first:
            cp.wait_recv()
        for t in range(self.n):
            total, inbox = scr[self.N_BUFS * t + 2], scr[self.N_BUFS * t + 3]
            for k, slot in enumerate((me, onto)):
                total[slot] = (total[slot].astype(F32) + inbox[k].astype(F32)).astype(BF16)
        for cp in second + keep:
            cp.start()

    def finish(self, ins, outs, scr):
        first, second, keep, _, _ = self._plan(outs, scr)
        for cp in first:
            cp.wait_send()
        for cp in second + keep:
            cp.wait()


class _Jobs:
    def __init__(self, *jobs):
        self.jobs = jobs
        self.args = [a for j in jobs for a in j.args]
        self.out_shape = [o for j in jobs for o in j.out_shape]
        self.scratch = [s for j in jobs for s in j.scratch]

    def _split(self, refs, attr):
        at = 0
        for j in self.jobs:
            n = len(getattr(j, attr))
            yield refs[at:at + n]
            at += n

    def _each(self, ins, outs, scr):
        return zip(self.jobs, self._split(ins, "args"), self._split(outs, "out_shape"), self._split(scr, "scratch"))

    def start(self, ins, outs, scr):
        for j, i, o, s in self._each(ins, outs, scr):
            j.start(i, o, s)

    def mid(self, ins, outs, scr):
        for j, i, o, s in self._each(ins, outs, scr):
            if hasattr(j, "mid"):
                j.mid(i, o, s)

    def finish(self, ins, outs, scr):
        for j, i, o, s in self._each(ins, outs, scr):
            j.finish(i, o, s)

    def split_outputs(self, outs):
        return list(self._split(outs, "out_shape"))


def _adamw_math(w, g, m, v):
    m = ADAM_B1 * m + (1.0 - ADAM_B1) * g
    v = ADAM_B2 * v + (1.0 - ADAM_B2) * (g * g)
    m_hat = m / (1.0 - ADAM_B1 ** ADAM_STEP)
    v_hat = v / (1.0 - ADAM_B2 ** ADAM_STEP)
    delta = -ADAM_LR * (m_hat / (jnp.sqrt(v_hat) + ADAM_EPS) + ADAM_WD * w)
    return delta, m, v


def _adamw(name, w, m, v, landings, n_col_blocks=1, job=None):
    n_slots, r, c = landings[0].shape
    grid = (w.shape[0] // r, n_col_blocks)

    def body(w_ref, m_ref, v_ref, *rest):
        l_refs, (g_ref, d_ref, nm_ref, nv_ref) = rest[:len(landings)], rest[len(landings):]
        step = pl.program_id(0) * n_col_blocks + pl.program_id(1)
        for idx, l_ref in enumerate(l_refs):
            @pl.when(step == idx)
            def _(l_ref=l_ref):
                g = l_ref[0].astype(F32)
                for s in range(1, n_slots):
                    g = g + l_ref[s].astype(F32)
                g_ref[...] = g
                d_ref[...], nm_ref[...], nv_ref[...] = _adamw_math(w_ref[...], g, m_ref[...], v_ref[...])

    spec = pl.BlockSpec((r, c), lambda a, b: (a, b))
    return _launch(
        body, name=f"adamw_{name}", grid=grid,
        in_specs=[spec, spec, spec] + [_full_spec((n_slots, r, c))] * len(landings),
        out_specs=[spec] * 4, out_shape=[jax.ShapeDtypeStruct(w.shape, F32)] * 4,
        args=(w, m, v, *landings), vmem=VMEM_BIG, job=job)


_SMALL = (("pre_mix_g", SV_PRE_MIX, 2), ("post_mix_g", SV_POST_MIX, 2), ("pre_ffn_g", SV_PRE_FFN, 2),
          ("post_ffn_g", SV_POST_FFN, 2), ("ple_g", SV_PLE, 2), ("ple_post_g", SV_PLE_POST, 2), ("kv_g", SV_KV, 1),
          ("pool_scale", SV_POOL_SCALE, 1), ("sinks", SV_SINKS, 1))


def _small_all_reduce_adamw(part, params):
    flat = [a for name, _, _ in _SMALL for a in params[name]]
    n_in = 1 + len(flat)

    def body(*refs):
        part_ref, wmv = refs[0], refs[1:n_in]
        loss_ref, outs = refs[n_in], refs[n_in + 1:n_in + 1 + 4 * len(_SMALL)]
        buf, total, send_sems, recv_sems = refs[n_in + 1 + 4 * len(_SMALL):]
        x, y, c = _my_place()
        me = _dev_index(x, y, c)
        buf[me] = part_ref[...]
        copies = [pltpu.make_async_remote_copy(
            src_ref=buf.at[me], dst_ref=buf.at[me], send_sem=send_sems.at[r - 1], recv_sem=recv_sems.at[r - 1],
            device_id=_peer_by_relation(r), device_id_type=MESH) for r in range(1, N_DEV)]
        for cp in copies:
            cp.start()
        for cp in copies:
            cp.wait()
        g = buf[0]
        for s in range(1, N_DEV):
            g = g + buf[s]
        total[...] = g
        loss_ref[...] = total[SV_LOSS:SV_LOSS + 1, 0:1]
        for idx, (name, row, n_rows) in enumerate(_SMALL):
            w_ref, m_ref, v_ref = wmv[3 * idx:3 * idx + 3]
            g_ref, d_ref, nm_ref, nv_ref = outs[4 * idx:4 * idx + 4]
            if name == "pool_scale":
                g = total[row:row + 1, pl.ds(pl.multiple_of(me * 128, 128), 128)]
            else:
                g = total[row:row + n_rows, 0:w_ref.shape[1]]
            g_ref[...] = g
            d_ref[...], nm_ref[...], nv_ref[...] = _adamw_math(w_ref[...], g, m_ref[...], v_ref[...])

    out_shape = [jax.ShapeDtypeStruct((1, 1), F32)]
    for name, _, _ in _SMALL:
        out_shape += [jax.ShapeDtypeStruct(params[name][0].shape, F32)] * 4
    res, _ = _launch(
        body, name="small_all_reduce_adamw", grid=(1,),
        in_specs=[_full_spec(a.shape) for a in (part, *flat)], out_specs=[_full_spec(s.shape) for s in out_shape],
        out_shape=out_shape,
        scratch_shapes=[pltpu.VMEM((N_DEV, SV_ROWS, D_MODEL), F32), pltpu.VMEM((SV_ROWS, D_MODEL), F32),
                        pltpu.SemaphoreType.DMA((N_DEV - 1,)), pltpu.SemaphoreType.DMA((N_DEV - 1,))],
        args=(part, *flat))
    return res[0], {name: res[1 + 4 * idx:5 + 4 * idx] for idx, (name, _, _) in enumerate(_SMALL)}


def _local_step(x, p, tgt, gains, sinks, shards, weights):
    row = lambda first_row, layer: _Gain(gains, first_row + layer)
    gather = lambda *names: _AllGather(names, shards)
    g_pre_mix, g_post_mix, g_pre_ffn, g_post_ffn = SV_PRE_MIX, SV_POST_MIX, SV_PRE_FFN, SV_POST_FFN
    g_ple, g_ple_post, g_kv = SV_PLE, SV_PLE_POST, _Gain(gains, SV_KV)

    wp, scale, wgu0 = _all_gather_only("gather_first", ("pool", "scale", "gu0"), shards)
    (x1_0, h2_0, yraw, dpool), (wd0,) = _fwd_pool_mixer(
        x, row(g_pre_mix, 0), wp, scale, row(g_post_mix, 0), row(g_pre_ffn, 0), job=gather("wd0"))
    (gs0, us0, f0, x2_0, h3_0), (wgate0, wproj0, wgu1) = _fwd_ffn(
        0, h2_0, x1_0, wgu0, wd0, row(g_post_ffn, 0), row(g_ple, 0), job=gather("gate0", "proj0", "gu1"))
    (x3_0, z0, pe0), (wkv, wq) = _fwd_ple(0, x2_0, h3_0, p[0], wgate0, wproj0, row(g_ple_post, 0),
                                          job=gather("kv", "q"))
    (hk, h1, q, kv), (wo,) = _fwd_qkv(x3_0, g_kv, row(g_pre_mix, 1), wkv, wq, job=gather("o"))
    front = ((ATT_BLOCK, 0), (0, 0))
    kpad = jnp.pad(kv[:, :KV_DIM], front)
    vpad = jnp.pad(kv[:, KV_DIM:], front)
    (attn,), (wd1,) = _fwd_attention(q, kpad, vpad, sinks, job=gather("wd1"))
    (y1, x1_1, h2_1), _ = _fwd_attn_out(attn, x3_0, wo, row(g_post_mix, 1), row(g_pre_ffn, 1))
    (gs1, us1, f1, x2_1, h3_1), (wgate1, wproj1) = _fwd_ffn(
        1, h2_1, x1_1, wgu1, wd1, row(g_post_ffn, 1), row(g_ple, 1), job=gather("gate1", "proj1"))

    produced, swapped, landed = {}, {}, {}

    def kind_of(name):
        return name.rstrip("0123_")

    def carry(swap=(), spread=()):
        jobs = []
        if swap:
            jobs.append(_SiblingSwap([(kind_of(n), produced[n]) for n in swap]))
        if spread:
            jobs.append(_ChipScatter([(kind_of(n), produced[n], swapped[n]) for n in spread]))
        return _Jobs(*jobs)

    def carried(jobs, outs, swap=(), spread=()):
        parts = jobs.split_outputs(outs)
        if swap:
            swapped.update(zip(swap, parts[0]))
        if spread:
            landed.update(zip(spread, parts[-1]))

    def hosted(call, *args, swap=(), spread=()):
        jobs = carry(swap, spread)
        outs, job_outs = call(*args, job=jobs)
        carried(jobs, job_outs, swap, spread)
        return outs

    def ffn_weight_grads(layer, h2, df, dg, du, a, hosts):
        for qtr in range(FF_PARTS):
            dgu, dwd = hosted(_bwd_ffn_dw, layer, qtr, h2, df, dg, du, a, **hosts[qtr])
            produced[f"gu{layer}_{qtr}"], produced[f"wd{layer}_{qtr}"] = dgu, dwd

    ffn_q = lambda layer, qtr: (f"gu{layer}_{qtr}", f"wd{layer}_{qtr}")

    dx2_1, df1, produced["gate1"], produced["proj1"], dg_ple_post1, dg_ple1, dg_post_ffn1, loss = hosted(
        _ple_loss_bwd, 1, x2_1, h3_1, p[1], f1, tgt, wgate1, wproj1, row(g_ple_post, 1), row(g_ple, 1),
        row(g_post_ffn, 1))
    dh2_1, dg1, du1, a1 = hosted(_bwd_ffn_act, 1, df1, gs1, us1, wgu1, wd1, swap=("gate1", "proj1"))
    ffn_weight_grads(1, h2_1, df1, dg1, du1, a1, [dict(spread=("gate1", "proj1")), dict(swap=ffn_q(1, 0))])
    dx1_1, dattn, produced["o"], dg_pre_ffn1, dg_post_mix1 = hosted(
        _bwd_attn_out, dx2_1, dh2_1, x1_1, y1, attn, wo, row(g_pre_ffn, 1), row(g_post_mix, 1), swap=ffn_q(1, 1))
    dq, dkpad, dvpad, dsinks = hosted(_bwd_attention, q, dattn, kpad, vpad, sinks, spread=ffn_q(1, 0))
    dkv = jnp.concatenate([dkpad[ATT_BLOCK:], dvpad[ATT_BLOCK:]], axis=1).astype(BF16)
    dx3_0, produced["q"], produced["kv"], dg_pre_mix1, dg_kv = hosted(
        _bwd_qkv, dx1_1, dq, dkv, x3_0, h1, hk, wq, wkv, row(g_pre_mix, 1), g_kv, swap=("o",))
    dx2_0, df0, produced["gate0"], produced["proj0"], dg_ple_post0, dg_ple0, dg_post_ffn0 = hosted(
        _bwd_ple, 0, dx3_0, x2_0, z0, pe0, h3_0, p[0], f0, wgate0, row(g_ple_post, 0), row(g_ple, 0),
        row(g_post_ffn, 0), swap=("q", "kv"), spread=("gu1_1",))
    dh2_0, dg0, du0, a0 = hosted(_bwd_ffn_act, 0, df0, gs0, us0, wgu0, wd0,
                                 swap=("gate0", "proj0"), spread=("wd1_1", "o"))
    ffn_weight_grads(0, h2_0, df0, dg0, du0, a0, [
        dict(spread=("gate0", "proj0", "q", "kv")), dict(swap=ffn_q(0, 0))])
    grad_x, produced["pool"], dscale, dg_pre_ffn0, dg_post_mix0, dg_pre_mix0 = hosted(
        _bwd_pool_mixer, dx2_0, dh2_0, x1_0, x, yraw, dpool, wp, scale, row(g_pre_ffn, 0), row(g_post_mix, 0),
        row(g_pre_mix, 0), swap=ffn_q(0, 1), spread=ffn_q(0, 0))

    def update(name, n_col_blocks=1, pieces=None, swap=(), spread=()):
        w, m, v = weights[name]
        rows = w.size // w.shape[-1]
        flat = [landed[n].reshape(landed[n].shape[0], -1, landed[n].shape[-1])
                for n in (pieces or [kind_short[name]])]
        outs = hosted(_adamw, name, w.reshape(rows, -1), m.reshape(rows, -1), v.reshape(rows, -1), flat,
                      n_col_blocks, swap=swap, spread=spread)
        return [o.reshape(w.shape) for o in outs]

    kind_short = {"w_q": "q", "w_kv": "kv", "w_o": "o", "pool_w": "pool"}
    upd = {}
    upd["w_ple_gate"] = update("w_ple_gate", pieces=("gate0", "gate1"), swap=("pool",), spread=ffn_q(0, 1))
    upd["w_ple_proj"] = update("w_ple_proj", pieces=("proj0", "proj1"), spread=("pool",))
    for name in ("w_q", "w_kv", "w_o", "pool_w"):
        upd[name] = update(name)
    upd["w_gu"] = update("w_gu", FF_PARTS,
                         pieces=[f"gu{layer}_{qtr}" for layer in range(2) for qtr in range(FF_PARTS)])
    upd["w_gu"] = [jnp.swapaxes(a, 1, 2) for a in upd["w_gu"]]
    upd["w_down"] = update("w_down", FF_PARTS,
                           pieces=[f"wd{layer}_{qtr}" for layer in range(2) for qtr in range(FF_PARTS)])

    lanes = lambda a: jnp.pad(a, ((0, 0), (0, D_MODEL - a.shape[1])))
    small = jnp.concatenate([
        dg_pre_mix0, dg_pre_mix1, dg_post_mix0, dg_post_mix1, dg_pre_ffn0, dg_pre_ffn1, dg_post_ffn0, dg_post_ffn1,
        dg_ple0, dg_ple1, dg_ple_post0, dg_ple_post1, dg_kv, dscale, lanes(dsinks[:, :N_HEADS]), lanes(loss)], axis=0)
    return grad_x, upd, small


def kernel(x, p, pre_mix_g, post_mix_g, pre_ffn_g, post_ffn_g, pool_w, pool_scale, kv_g, w_kv, w_q, sinks, w_o, w_gu, w_down, ple_g, w_ple_gate, w_ple_proj, ple_post_g, loss_target, m_pre_mix_g, m_post_mix_g, m_pre_ffn_g, m_post_ffn_g, m_pool_w, m_pool_scale, m_kv_g, m_w_kv, m_w_q, m_sinks, m_w_o, m_w_gu, m_w_down, m_ple_g, m_w_ple_gate, m_w_ple_proj, m_ple_post_g, v_pre_mix_g, v_post_mix_g, v_pre_ffn_g, v_post_ffn_g, v_pool_w, v_pool_scale, v_kv_g, v_w_kv, v_w_q, v_sinks, v_w_o, v_w_gu, v_w_down, v_ple_g, v_w_ple_gate, v_w_ple_proj, v_ple_post_g):
    shards = {"pool": pool_w[0].astype(BF16), "scale": pool_scale, "kv": w_kv.astype(BF16),
              "q": w_q[0].astype(BF16), "o": w_o[0].astype(BF16)}
    for layer in range(2):
        shards[f"gu{layer}"] = w_gu[layer].T.astype(BF16)
        shards[f"wd{layer}"] = w_down[layer].astype(BF16)
        shards[f"gate{layer}"] = w_ple_gate[layer].astype(BF16)
        shards[f"proj{layer}"] = w_ple_proj[layer].astype(BF16)
    gains = jnp.concatenate([pre_mix_g, post_mix_g, pre_ffn_g, post_ffn_g, ple_g, ple_post_g, kv_g[None, :]],
                            axis=0).reshape(-1, 1, D_MODEL)
    weights = {"pool_w": (pool_w, m_pool_w, v_pool_w), "w_kv": (w_kv, m_w_kv, v_w_kv), "w_q": (w_q, m_w_q, v_w_q),
               "w_o": (w_o, m_w_o, v_w_o), "w_down": (w_down, m_w_down, v_w_down),
               "w_gu": tuple(jnp.swapaxes(a, 1, 2) for a in (w_gu, m_w_gu, v_w_gu)),
               "w_ple_gate": (w_ple_gate, m_w_ple_gate, v_w_ple_gate),
               "w_ple_proj": (w_ple_proj, m_w_ple_proj, v_w_ple_proj)}
    grad_x, upd, small = _local_step(x[0], p[:, 0], loss_target[0], gains, sinks, shards, weights)

    small_params = {
        "pre_mix_g": (pre_mix_g, m_pre_mix_g, v_pre_mix_g), "post_mix_g": (post_mix_g, m_post_mix_g, v_post_mix_g),
        "pre_ffn_g": (pre_ffn_g, m_pre_ffn_g, v_pre_ffn_g), "post_ffn_g": (post_ffn_g, m_post_ffn_g, v_post_ffn_g),
        "ple_g": (ple_g, m_ple_g, v_ple_g), "ple_post_g": (ple_post_g, m_ple_post_g, v_ple_post_g),
        "kv_g": (kv_g[None, :], m_kv_g[None, :], v_kv_g[None, :]),
        "pool_scale": (pool_scale, m_pool_scale, v_pool_scale), "sinks": (sinks, m_sinks, v_sinks)}
    loss, small_upd = _small_all_reduce_adamw(small, small_params)
    small_upd["kv_g"] = [a[0] for a in small_upd["kv_g"]]
    upd.update(small_upd)

    names = ["pre_mix_g", "post_mix_g", "pre_ffn_g", "post_ffn_g", "pool_w", "pool_scale", "kv_g", "w_kv", "w_q",
             "sinks", "w_o", "w_gu", "w_down", "ple_g", "w_ple_gate", "w_ple_proj", "ple_post_g"]
    outs = [loss[0, 0], grad_x[None]]
    for kind in range(4):
        outs += [upd[n][kind] for n in names]
    return tuple(outs)
```

```python
import functools
import types

import jax
import jax.numpy as jnp
from jax import lax
from jax.experimental import pallas as pl
from jax.experimental.pallas import tpu as pltpu

F32 = jnp.float32
BF16 = jnp.bfloat16

N_DEV = 8
D_MODEL = 1024
N_POOL_GROUPS = 4
POOL_GROUP = 256
POOL_HALO = 16
HEAD_DIM = 64
N_HEADS = 16
N_KV_HEADS = 4
GQA_GROUP = 4
KV_DIM = N_KV_HEADS * HEAD_DIM
ATT_BLOCK = 128
D_FF = 2816
FF_CHUNKS = 4
FF_BLOCK = D_FF // FF_CHUNKS
WD_ROWS = D_FF // N_DEV
FF_PARTS = 2
FF_PART = D_MODEL // FF_PARTS
N_CHIPS = 4
PLE_DIM = 256
EPS = 1e-6
NEG_INF = -1e30
ATT_SCALE = HEAD_DIM ** -0.5

ADAM_LR = 0.001
ADAM_B1 = 0.9
ADAM_B2 = 0.999
ADAM_EPS = 1e-08
ADAM_WD = 0.01
ADAM_STEP = 10

ROW_TILE = 512
FFN_ROW_TILE = 512
FFN_SUB_TILES = 2
VMEM_BIG = 60 * 1024 * 1024
VMEM_MID = 56 * 1024 * 1024
HBM_PIN_ELEMS = 1024

SV_ROWS = 16
SV_PRE_MIX, SV_POST_MIX, SV_PRE_FFN, SV_POST_FFN, SV_PLE, SV_PLE_POST = 0, 2, 4, 6, 8, 10
SV_KV, SV_POOL_SCALE, SV_SINKS, SV_LOSS = 12, 13, 14, 15

MESH = pl.DeviceIdType.MESH
ANY = pl.BlockSpec(memory_space=pl.ANY)


def _dot(a, b):
    return jnp.dot(a, b, preferred_element_type=F32)


def _dot_nt(a, b):
    return lax.dot_general(a, b, (((1,), (1,)), ((), ())), preferred_element_type=F32)


def _dot_tn(a, b):
    return lax.dot_general(a, b, (((0,), (0,)), ((), ())), preferred_element_type=F32)


def _rstd(x):
    return lax.rsqrt(jnp.mean(x * x, axis=-1, keepdims=True) + EPS)


def _rms(x, g):
    return x * _rstd(x) * g


def _rms_bwd(x, g, dy):
    r = _rstd(x)
    n = x * r
    dn = dy * g
    dx = r * (dn - n * jnp.mean(dn * n, axis=-1, keepdims=True))
    dg = jnp.sum(dy * n, axis=0, keepdims=True)
    return dx, dg


def _sigmoid(x):
    return 1.0 / (1.0 + jnp.exp(-x))


def _acc(ref, val, first):
    @pl.when(first)
    def _():
        ref[...] = val

    @pl.when(jnp.logical_not(first))
    def _():
        ref[...] += val


def _pool_counts(row0, rows):
    t = row0 + lax.broadcasted_iota(jnp.int32, (rows, D_MODEL), 0) + 1
    grp = lax.broadcasted_iota(jnp.int32, (rows, D_MODEL), 1) // POOL_GROUP
    win = jnp.left_shift(2, grp)
    return jnp.minimum(t, win).astype(F32)


def _window_sums(ext, shift_of):
    outs = []
    s = ext
    for gi in range(N_POOL_GROUPS):
        s = s + pltpu.roll(s, shift_of(1 << gi), axis=0)
        outs.append(s[:, :POOL_GROUP])
        s = s[:, POOL_GROUP:]
    return jnp.concatenate(outs, axis=1)


def _cparams(n_axes, vmem):
    return pltpu.CompilerParams(dimension_semantics=("arbitrary",) * n_axes, vmem_limit_bytes=vmem)


def _row_spec(cols, tm=ROW_TILE):
    return pl.BlockSpec((tm, cols), lambda i: (i, 0))


def _full_spec(shape):
    zeros = (0,) * len(shape)
    return pl.BlockSpec(shape, lambda *_: zeros)


def _vec_spec():
    return _full_spec((1, D_MODEL))


def _column_ranges(parts):
    ends = [0]
    for part in parts:
        ends.append(ends[-1] + part.shape[-1])
    return list(zip(ends[:-1], ends[1:]))


class _Gain:
    def __init__(self, stacked, layer):
        self.stacked, self.layer = stacked, layer

    def spec(self):
        layer = self.layer
        return pl.BlockSpec((None, 1, D_MODEL), lambda *_: (layer, 0, 0))


def _in_hbm(a):
    return pltpu.with_memory_space_constraint(a, pltpu.HBM) if a.size >= HBM_PIN_ELEMS else a


def _out_in_hbm(s):
    return pltpu.HBM(s.shape, s.dtype) if s.size >= HBM_PIN_ELEMS else s


def _launch(body, *, name, grid, in_specs, out_specs, out_shape, args, scratch_shapes=(), vmem=VMEM_MID, job=None):
    in_specs = [a.spec() if isinstance(a, _Gain) else s for s, a in zip(in_specs, args)]
    args = [_in_hbm(a.stacked if isinstance(a, _Gain) else a) for a in args]
    n_in, n_out, n_scr = len(args), len(out_shape), len(scratch_shapes)
    j_args, j_out, j_scr = ([], [], []) if job is None else ([_in_hbm(a) for a in job.args], job.out_shape, job.scratch)

    def run(*refs):
        groups, at = [], 0
        for n in (n_in, len(j_args), n_out, len(j_out), n_scr, len(j_scr)):
            groups.append(refs[at:at + n])
            at += n
        ins, j_ins, outs, j_outs, scr, j_sems = groups
        if job is None:
            body(*ins, *outs, *scr)
        elif not grid:
            job.start(j_ins, j_outs, j_sems)
            job.mid(j_ins, j_outs, j_sems)
            body(*ins, *outs, *scr)
            job.finish(j_ins, j_outs, j_sems)
        else:
            ids = [pl.program_id(a) for a in range(len(grid))]
            first = functools.reduce(jnp.logical_and, [i == 0 for i in ids])
            half = functools.reduce(jnp.logical_and, [ids[0] == grid[0] // 2] + [i == 0 for i in ids[1:]])
            last = functools.reduce(jnp.logical_and, [i == g - 1 for i, g in zip(ids, grid)])
            pl.when(first)(lambda: job.start(j_ins, j_outs, j_sems))
            pl.when(half)(lambda: job.mid(j_ins, j_outs, j_sems))
            body(*ins, *outs, *scr)
            pl.when(last)(lambda: job.finish(j_ins, j_outs, j_sems))

    res = pl.pallas_call(
        run, name=name, grid=grid,
        in_specs=list(in_specs) + [ANY] * len(j_args), out_specs=list(out_specs) + [ANY] * len(j_out),
        out_shape=[_out_in_hbm(s) for s in list(out_shape) + list(j_out)],
        scratch_shapes=list(scratch_shapes) + list(j_scr),
        compiler_params=_cparams(len(grid), vmem),
    )(*args, *j_args)
    return res[:n_out], res[n_out:]


def _fwd_pool_mixer(x, g_pre, wp, scale, g_post, g_ffn, job=None):
    T = x.shape[0]
    tm = ROW_TILE
    nt = T // tm

    def body(x_ref, gpre_ref, wp_ref, sc_ref, gpost_ref, gffn_ref, x1_ref, h2_ref, yraw_ref, d_ref, carry):
        i = pl.program_id(0)

        @pl.when(i == 0)
        def _():
            carry[...] = jnp.zeros_like(carry)

        xv = x_ref[...]
        h = _rms(xv, gpre_ref[...])
        ext = jnp.concatenate([carry[...], h], axis=0)
        carry[...] = h[tm - POOL_HALO:, :]
        sums = _window_sums(ext, lambda k: k)[POOL_HALO:, :]
        d = sums / _pool_counts(i * tm, tm) - h
        db = d.astype(BF16)
        d_ref[...] = db
        yraw = jnp.concatenate(
            [_dot(db[:, g * POOL_GROUP:(g + 1) * POOL_GROUP], wp_ref[g]) for g in range(N_POOL_GROUPS)], axis=1)
        yraw_ref[...] = yraw
        x1 = xv + _rms(yraw * sc_ref[...], gpost_ref[...])
        x1_ref[...] = x1
        h2_ref[...] = _rms(x1, gffn_ref[...]).astype(BF16)

    return _launch(
        body, name="fwd_pool_mixer", grid=(nt,),
        in_specs=[_row_spec(D_MODEL), _vec_spec(), _full_spec((N_POOL_GROUPS, POOL_GROUP, POOL_GROUP)), _vec_spec(),
                  _vec_spec(), _vec_spec()],
        out_specs=[_row_spec(D_MODEL)] * 4,
        out_shape=[jax.ShapeDtypeStruct((T, D_MODEL), F32), jax.ShapeDtypeStruct((T, D_MODEL), BF16),
                   jax.ShapeDtypeStruct((T, D_MODEL), F32), jax.ShapeDtypeStruct((T, D_MODEL), BF16)],
        scratch_shapes=[pltpu.VMEM((POOL_HALO, D_MODEL), F32)],
        args=(x, g_pre, wp, scale, g_post, g_ffn), job=job)


def _fwd_ffn(layer, h2, x1, wgu, wd, g_post, g_ple, job=None):
    T = h2.shape[0]
    tm = min(FFN_ROW_TILE, T)
    nt = T // tm
    sub = tm // FFN_SUB_TILES
    last = FF_CHUNKS - 1
    n_gu, n_wd = len(wgu), len(wd)
    gu_cols = _column_ranges(wgu)

    def body(h2_ref, x1_ref, *refs):
        wgu_refs, wd_refs = refs[:n_gu], refs[n_gu:n_gu + n_wd]
        gpost_ref, gple_ref, gs_ref, us_ref, f_ref, x2_ref, h3_ref, acc = refs[n_gu + n_wd:]
        k = pl.program_id(0)
        i = pl.program_id(1)
        rows = pl.ds(pl.multiple_of(i * tm, tm), tm)
        parts = []
        for s in range(FFN_SUB_TILES):
            r = pl.ds(s * sub, sub)
            g = sum(_dot_nt(h2_ref[r, c0:c1], w[0]) for (c0, c1), w in zip(gu_cols, wgu_refs))
            u = sum(_dot_nt(h2_ref[r, c0:c1], w[1]) for (c0, c1), w in zip(gu_cols, wgu_refs))
            gs_ref[r, :] = g.astype(BF16)
            us_ref[r, :] = u.astype(BF16)
            a = (g * _sigmoid(g) * u).astype(BF16)
            parts.append(jnp.concatenate([_dot(a, w[...]) for w in wd_refs], axis=1))
        part = jnp.concatenate(parts, axis=0)

        @pl.when(k == 0)
        def _():
            acc[rows, :] = part

        @pl.when(jnp.logical_and(k > 0, k < last))
        def _():
            acc[rows, :] += part

        @pl.when(k == last)
        def _():
            f = acc[rows, :] + part
            f_ref[...] = f
            x2 = x1_ref[...] + _rms(f, gpost_ref[...])
            x2_ref[...] = x2
            h3_ref[...] = _rms(x2, gple_ref[...]).astype(BF16)

    def late(k, i):
        return (jnp.where(k == last, i, 0), 0)

    return _launch(
        body, name=f"fwd_ffn{layer}", grid=(FF_CHUNKS, nt),
        in_specs=[pl.BlockSpec((tm, D_MODEL), lambda k, i: (i, 0)), pl.BlockSpec((tm, D_MODEL), late)]
                 + [pl.BlockSpec((None, 2, FF_BLOCK, w.shape[-1]), lambda k, i: (k, 0, 0, 0)) for w in wgu]
                 + [pl.BlockSpec((FF_BLOCK, w.shape[-1]), lambda k, i: (k, 0)) for w in wd]
                 + [pl.BlockSpec((1, D_MODEL), lambda k, i: (0, 0))] * 2,
        out_specs=[pl.BlockSpec((None, tm, FF_BLOCK), lambda k, i: (k, i, 0)),
                   pl.BlockSpec((None, tm, FF_BLOCK), lambda k, i: (k, i, 0)),
                   pl.BlockSpec((tm, D_MODEL), late),
                   pl.BlockSpec((tm, D_MODEL), late),
                   pl.BlockSpec((tm, D_MODEL), late)],
        out_shape=[jax.ShapeDtypeStruct((FF_CHUNKS, T, FF_BLOCK), BF16),
                   jax.ShapeDtypeStruct((FF_CHUNKS, T, FF_BLOCK), BF16),
                   jax.ShapeDtypeStruct((T, D_MODEL), F32),
                   jax.ShapeDtypeStruct((T, D_MODEL), F32),
                   jax.ShapeDtypeStruct((T, D_MODEL), BF16)],
        scratch_shapes=[pltpu.VMEM((T, D_MODEL), F32)],
        args=(h2, x1, *wgu, *wd, g_post, g_ple), vmem=VMEM_BIG, job=job)


def _fwd_ple(layer, x2, h3, p, wgate, wproj, g_post, job=None):
    T = x2.shape[0]
    nt = T // ROW_TILE

    def body(x2_ref, h3_ref, p_ref, wg_ref, wp_ref, gpost_ref, x3_ref, z_ref, pe_ref):
        z = _dot(h3_ref[...], wg_ref[...])
        pe = _dot(p_ref[...].astype(BF16), wp_ref[...])
        z_ref[...] = z
        pe_ref[...] = pe
        x3_ref[...] = x2_ref[...] + _rms(pe * _sigmoid(z), gpost_ref[...])

    return _launch(
        body, name=f"fwd_ple{layer}", grid=(nt,),
        in_specs=[_row_spec(D_MODEL), _row_spec(D_MODEL), _row_spec(PLE_DIM), _full_spec((D_MODEL, D_MODEL)),
                  _full_spec((PLE_DIM, D_MODEL)), _vec_spec()],
        out_specs=[_row_spec(D_MODEL)] * 3, out_shape=[jax.ShapeDtypeStruct((T, D_MODEL), F32)] * 3,
        args=(x2, h3, p, wgate, wproj, g_post), job=job)


def _fwd_qkv(x3, g_kv, g_mix, wkv, wq, job=None):
    T = x3.shape[0]
    nt = T // ROW_TILE

    def body(x_ref, gkv_ref, gmix_ref, wkv_ref, wq_ref, hk_ref, h1_ref, q_ref, kv_ref):
        xv = x_ref[...]
        r = _rstd(xv)
        hk = (xv * r * gkv_ref[...]).astype(BF16)
        h1 = (xv * r * gmix_ref[...]).astype(BF16)
        hk_ref[...] = hk
        h1_ref[...] = h1
        kv_ref[...] = _dot(hk, wkv_ref[...]).astype(BF16)
        q_ref[...] = _dot(h1, wq_ref[...]).astype(BF16)

    return _launch(
        body, name="fwd_qkv", grid=(nt,),
        in_specs=[_row_spec(D_MODEL), _vec_spec(), _vec_spec(), _full_spec((D_MODEL, 2 * KV_DIM)),
                  _full_spec((D_MODEL, D_MODEL))],
        out_specs=[_row_spec(D_MODEL), _row_spec(D_MODEL), _row_spec(D_MODEL), _row_spec(2 * KV_DIM)],
        out_shape=[jax.ShapeDtypeStruct((T, D_MODEL), BF16)] * 3 + [jax.ShapeDtypeStruct((T, 2 * KV_DIM), BF16)],
        args=(x3, g_kv, g_mix, wkv, wq), job=job)


def _alibi_slope(h):
    return 2.0 ** (-8.0 * (h + 1) / N_HEADS)


ATT_SUB = 32
ATT_GROUP_ROWS = GQA_GROUP * ATT_BLOCK


def _att_mask(n, row0):
    qi = lax.broadcasted_iota(jnp.int32, (ATT_SUB, 2 * ATT_BLOCK), 0) + row0
    si = lax.broadcasted_iota(jnp.int32, (ATT_SUB, 2 * ATT_BLOCK), 1)
    rel = ATT_BLOCK + qi - si
    valid = (rel >= 0) & (rel < ATT_BLOCK) & ((si >= ATT_BLOCK) | (n > 0))
    return rel.astype(F32), valid


def _att_probs(raw, relf, valid, slope, sink):
    s = jnp.where(valid, raw * ATT_SCALE - slope * relf, NEG_INF)
    m = jnp.maximum(jnp.max(s, axis=-1, keepdims=True), sink)
    e = jnp.exp(s - m)
    es = jnp.exp(sink - m)
    inv = 1.0 / (jnp.sum(e, axis=-1, keepdims=True) + es)
    return e * inv, es * inv


def _stack_heads(ref, kh):
    first = kh * GQA_GROUP
    return jnp.concatenate([ref[:, (first + g) * HEAD_DIM:(first + g + 1) * HEAD_DIM] for g in range(GQA_GROUP)], axis=0)


def _unstack_heads(stacked):
    return [stacked[g * ATT_BLOCK:(g + 1) * ATT_BLOCK, :] for g in range(GQA_GROUP)]


def _fwd_attention(q, kpad, vpad, sinks, job=None):
    T = q.shape[0]
    nb = T // ATT_BLOCK

    def body(q_ref, k_ref, v_ref, sink_ref, o_ref, s_scr, p_scr):
        n = pl.program_id(0)
        start = pl.multiple_of(n * ATT_BLOCK, ATT_BLOCK)
        kw = k_ref[pl.ds(start, 2 * ATT_BLOCK), :]
        vw = v_ref[pl.ds(start, 2 * ATT_BLOCK), :]
        outs = []
        for kh in range(N_KV_HEADS):
            kk = kw[:, kh * HEAD_DIM:(kh + 1) * HEAD_DIM]
            vv = vw[:, kh * HEAD_DIM:(kh + 1) * HEAD_DIM]
            s_scr[...] = _dot_nt(_stack_heads(q_ref, kh), kk)
            for g in range(GQA_GROUP):
                h = kh * GQA_GROUP + g
                for row0 in range(0, ATT_BLOCK, ATT_SUB):
                    rows = pl.ds(g * ATT_BLOCK + row0, ATT_SUB)
                    relf, valid = _att_mask(n, row0)
                    pr, _ = _att_probs(s_scr[rows, :], relf, valid, _alibi_slope(h), sink_ref[0, h])
                    p_scr[rows, :] = pr.astype(BF16)
            outs += _unstack_heads(_dot(p_scr[...], vv))
        o_ref[...] = jnp.concatenate(outs, axis=1).astype(BF16)

    return _launch(
        body, name="fwd_attention", grid=(nb,),
        in_specs=[_row_spec(D_MODEL, ATT_BLOCK), _full_spec((T + ATT_BLOCK, KV_DIM)), _full_spec((T + ATT_BLOCK, KV_DIM)),
                  pl.BlockSpec(memory_space=pltpu.SMEM)],
        out_specs=[_row_spec(D_MODEL, ATT_BLOCK)],
        out_shape=[jax.ShapeDtypeStruct((T, D_MODEL), BF16)],
        scratch_shapes=[pltpu.VMEM((ATT_GROUP_ROWS, 2 * ATT_BLOCK), F32), pltpu.VMEM((ATT_GROUP_ROWS, 2 * ATT_BLOCK), BF16)],
        args=(q, kpad, vpad, sinks), job=job)


def _fwd_attn_out(attn, x, wo, g_post, g_ffn, job=None):
    T = x.shape[0]
    nt = T // ROW_TILE

    def body(a_ref, x_ref, wo_ref, gpost_ref, gffn_ref, y_ref, x1_ref, h2_ref):
        y = _dot(a_ref[...], wo_ref[...])
        y_ref[...] = y
        x1 = x_ref[...] + _rms(y, gpost_ref[...])
        x1_ref[...] = x1
        h2_ref[...] = _rms(x1, gffn_ref[...]).astype(BF16)

    return _launch(
        body, name="fwd_attn_out", grid=(nt,),
        in_specs=[_row_spec(D_MODEL), _row_spec(D_MODEL), _full_spec((D_MODEL, D_MODEL)), _vec_spec(), _vec_spec()],
        out_specs=[_row_spec(D_MODEL)] * 3,
        out_shape=[jax.ShapeDtypeStruct((T, D_MODEL), F32), jax.ShapeDtypeStruct((T, D_MODEL), F32),
                   jax.ShapeDtypeStruct((T, D_MODEL), BF16)],
        args=(attn, x, wo, g_post, g_ffn), job=job)


def _bwd_ple(layer, dx3, x2, z, pe, h3, p, f, wgate, g_ple_post, g_ple, g_post_ffn, job=None):
    T = x2.shape[0]
    tm = ROW_TILE
    nt = T // tm

    def body(dx3_ref, x2_ref, z_ref, pe_ref, h3_ref, p_ref, f_ref, wg_ref, gpp_ref, gp_ref, gpf_ref,
             dx2_ref, df_ref, dwg_ref, dwp_ref, dgpp_ref, dgp_ref, dgpf_ref, acc_g, acc_p):
        i = pl.program_id(0)
        first = i == 0
        dx3v = dx3_ref[...]
        gate = _sigmoid(z_ref[...])
        pev = pe_ref[...]
        de, dgpp = _rms_bwd(pev * gate, gpp_ref[...], dx3v)
        dpe = (de * gate).astype(BF16)
        dz = (de * pev * gate * (1.0 - gate)).astype(BF16)
        _acc(acc_p, _dot_tn(p_ref[...].astype(BF16), dpe), first)
        _acc(acc_g, _dot_tn(h3_ref[...], dz), first)
        dh3 = _dot_nt(dz, wg_ref[...])
        dxn, dgp = _rms_bwd(x2_ref[...], gp_ref[...], dh3)
        dx2 = dx3v + dxn
        dx2_ref[...] = dx2
        df, dgpf = _rms_bwd(f_ref[...], gpf_ref[...], dx2)
        df_ref[...] = df.astype(BF16)
        _acc(dgpp_ref, dgpp, first)
        _acc(dgp_ref, dgp, first)
        _acc(dgpf_ref, dgpf, first)

        @pl.when(i == nt - 1)
        def _():
            dwg_ref[...] = acc_g[...].astype(BF16)
            dwp_ref[...] = acc_p[...].astype(BF16)

    return _launch(
        body, name=f"bwd_ple{layer}", grid=(nt,),
        in_specs=[_row_spec(D_MODEL)] * 5 + [_row_spec(PLE_DIM), _row_spec(D_MODEL), _full_spec((D_MODEL, D_MODEL)),
                  _vec_spec(), _vec_spec(), _vec_spec()],
        out_specs=[_row_spec(D_MODEL), _row_spec(D_MODEL), _full_spec((D_MODEL, D_MODEL)), _full_spec((PLE_DIM, D_MODEL)),
                   _vec_spec(), _vec_spec(), _vec_spec()],
        out_shape=[jax.ShapeDtypeStruct((T, D_MODEL), F32), jax.ShapeDtypeStruct((T, D_MODEL), BF16),
                   jax.ShapeDtypeStruct((D_MODEL, D_MODEL), BF16), jax.ShapeDtypeStruct((PLE_DIM, D_MODEL), BF16)]
                  + [jax.ShapeDtypeStruct((1, D_MODEL), F32)] * 3,
        scratch_shapes=[pltpu.VMEM((D_MODEL, D_MODEL), F32), pltpu.VMEM((PLE_DIM, D_MODEL), F32)],
        args=(dx3, x2, z, pe, h3, p, f, wgate, g_ple_post, g_ple, g_post_ffn), vmem=VMEM_BIG, job=job)


def _ple_loss_bwd(layer, x2, h3, p, f, target, wgate, wproj, g_ple_post, g_ple, g_post_ffn, job=None):
    T = x2.shape[0]
    tm = ROW_TILE
    nt = T // tm

    def body(x2_ref, h3_ref, p_ref, f_ref, tgt_ref, wg_ref, wp_ref, gpp_ref, gp_ref, gpf_ref,
             dx2_ref, df_ref, dwg_ref, dwp_ref, dgpp_ref, dgp_ref, dgpf_ref, loss_ref, acc_g, acc_p):
        i = pl.program_id(0)
        first = i == 0
        h3 = h3_ref[...]
        pb = p_ref[...].astype(BF16)
        x2v = x2_ref[...]
        gate = _sigmoid(_dot(h3, wg_ref[...]))
        pev = _dot(pb, wp_ref[...])
        e = pev * gate
        err = x2v + _rms(e, gpp_ref[...]) - tgt_ref[...]
        _acc(loss_ref, 0.5 * jnp.sum(jnp.mean(err * err, axis=-1, keepdims=True), axis=0, keepdims=True), first)
        dx3v = err * (1.0 / D_MODEL)
        de, dgpp = _rms_bwd(e, gpp_ref[...], dx3v)
        dpe = (de * gate).astype(BF16)
        dz = (de * pev * gate * (1.0 - gate)).astype(BF16)
        _acc(acc_p, _dot_tn(pb, dpe), first)
        _acc(acc_g, _dot_tn(h3, dz), first)
        dxn, dgp = _rms_bwd(x2v, gp_ref[...], _dot_nt(dz, wg_ref[...]))
        dx2 = dx3v + dxn
        dx2_ref[...] = dx2
        df, dgpf = _rms_bwd(f_ref[...], gpf_ref[...], dx2)
        df_ref[...] = df.astype(BF16)
        _acc(dgpp_ref, dgpp, first)
        _acc(dgp_ref, dgp, first)
        _acc(dgpf_ref, dgpf, first)

        @pl.when(i == nt - 1)
        def _():
            dwg_ref[...] = acc_g[...].astype(BF16)
            dwp_ref[...] = acc_p[...].astype(BF16)

    return _launch(
        body, name=f"ple_loss_bwd{layer}", grid=(nt,),
        in_specs=[_row_spec(D_MODEL), _row_spec(D_MODEL), _row_spec(PLE_DIM), _row_spec(D_MODEL), _row_spec(D_MODEL),
                  _full_spec((D_MODEL, D_MODEL)), _full_spec((PLE_DIM, D_MODEL)), _vec_spec(), _vec_spec(), _vec_spec()],
        out_specs=[_row_spec(D_MODEL), _row_spec(D_MODEL), _full_spec((D_MODEL, D_MODEL)), _full_spec((PLE_DIM, D_MODEL)),
                   _vec_spec(), _vec_spec(), _vec_spec(), _full_spec((1, 1))],
        out_shape=[jax.ShapeDtypeStruct((T, D_MODEL), F32), jax.ShapeDtypeStruct((T, D_MODEL), BF16),
                   jax.ShapeDtypeStruct((D_MODEL, D_MODEL), BF16), jax.ShapeDtypeStruct((PLE_DIM, D_MODEL), BF16)]
                  + [jax.ShapeDtypeStruct((1, D_MODEL), F32)] * 3 + [jax.ShapeDtypeStruct((1, 1), F32)],
        scratch_shapes=[pltpu.VMEM((D_MODEL, D_MODEL), F32), pltpu.VMEM((PLE_DIM, D_MODEL), F32)],
        args=(x2, h3, p, f, target, wgate, wproj, g_ple_post, g_ple, g_post_ffn), vmem=VMEM_BIG, job=job)


def _bwd_ffn_act(layer, df, gs, us, wgu, wd, job=None):
    T = df.shape[0]
    tm = min(FFN_ROW_TILE, T)
    nt = T // tm
    sub = tm // FFN_SUB_TILES
    last = FF_CHUNKS - 1
    n_gu, n_wd = len(wgu), len(wd)
    wd_cols = _column_ranges(wd)

    def body(df_ref, gs_ref, us_ref, *refs):
        wgu_refs, wd_refs = refs[:n_gu], refs[n_gu:n_gu + n_wd]
        dh_ref, dg_ref, du_ref, a_ref, acc_h = refs[n_gu + n_wd:]
        k = pl.program_id(0)
        i = pl.program_id(1)
        rows = pl.ds(pl.multiple_of(i * tm, tm), tm)
        dhs = []
        for s in range(FFN_SUB_TILES):
            r = pl.ds(s * sub, sub)
            g = gs_ref[r, :].astype(F32)
            u = us_ref[r, :].astype(F32)
            sg = _sigmoid(g)
            silu = g * sg
            a_ref[r, :] = (silu * u).astype(BF16)
            da = sum(_dot_nt(df_ref[r, c0:c1], w[...]) for (c0, c1), w in zip(wd_cols, wd_refs))
            dg = (da * u * (sg * (1.0 + g * (1.0 - sg)))).astype(BF16)
            du = (da * silu).astype(BF16)
            dg_ref[r, :] = dg
            du_ref[r, :] = du
            dhs.append(jnp.concatenate([_dot(dg, w[0]) + _dot(du, w[1]) for w in wgu_refs], axis=1))
        dh = jnp.concatenate(dhs, axis=0)

        @pl.when(k == 0)
        def _():
            acc_h[rows, :] = dh

        @pl.when(jnp.logical_and(k > 0, k < last))
        def _():
            acc_h[rows, :] += dh

        @pl.when(k == last)
        def _():
            dh_ref[...] = acc_h[rows, :] + dh

    chunk_rows = pl.BlockSpec((None, tm, FF_BLOCK), lambda k, i: (k, i, 0))
    saved = jax.ShapeDtypeStruct((FF_CHUNKS, T, FF_BLOCK), BF16)
    return _launch(
        body, name=f"bwd_ffn_act{layer}", grid=(FF_CHUNKS, nt),
        in_specs=[pl.BlockSpec((tm, D_MODEL), lambda k, i: (i, 0)), chunk_rows, chunk_rows]
                 + [pl.BlockSpec((None, 2, FF_BLOCK, w.shape[-1]), lambda k, i: (k, 0, 0, 0)) for w in wgu]
                 + [pl.BlockSpec((FF_BLOCK, w.shape[-1]), lambda k, i: (k, 0)) for w in wd],
        out_specs=[pl.BlockSpec((tm, D_MODEL), lambda k, i: (jnp.where(k == last, i, 0), 0)),
                   chunk_rows, chunk_rows, chunk_rows],
        out_shape=[jax.ShapeDtypeStruct((T, D_MODEL), F32), saved, saved, saved],
        scratch_shapes=[pltpu.VMEM((T, D_MODEL), F32)],
        args=(df, gs, us, *wgu, *wd), vmem=VMEM_BIG, job=job)


def _bwd_ffn_dw(layer, q, h2, df, dg, du, a, job=None):
    T = h2.shape[0]

    def body(h_ref, df_ref, dg_ref, du_ref, a_ref, dgu_ref, dwd_ref):
        h = h_ref[...]
        dgu_ref[0] = _dot_tn(dg_ref[...], h).astype(BF16)
        dgu_ref[1] = _dot_tn(du_ref[...], h).astype(BF16)
        dwd_ref[...] = _dot_tn(a_ref[...], df_ref[...]).astype(BF16)

    cols = pl.BlockSpec((T, FF_PART), lambda k: (0, q))
    chunk = pl.BlockSpec((None, T, FF_BLOCK), lambda k: (k, 0, 0))
    return _launch(
        body, name=f"bwd_ffn_dw{layer}_{q}", grid=(FF_CHUNKS,),
        in_specs=[cols, cols, chunk, chunk, chunk],
        out_specs=[pl.BlockSpec((None, 2, FF_BLOCK, FF_PART), lambda k: (k, 0, 0, 0)),
                   pl.BlockSpec((FF_BLOCK, FF_PART), lambda k: (k, 0))],
        out_shape=[jax.ShapeDtypeStruct((FF_CHUNKS, 2, FF_BLOCK, FF_PART), BF16),
                   jax.ShapeDtypeStruct((D_FF, FF_PART), BF16)],
        args=(h2, df, dg, du, a), vmem=VMEM_BIG, job=job)


def _bwd_attn_out(dx2, dh2, x1, y, attn, wo, g_ffn, g_post, job=None):
    T = x1.shape[0]
    nt = T // ROW_TILE

    def body(dx2_ref, dh2_ref, x1_ref, y_ref, a_ref, wo_ref, gffn_ref, gpost_ref,
             dx1_ref, da_ref, dwo_ref, dgf_ref, dgp_ref, acc):
        i = pl.program_id(0)
        first = i == 0
        dxn, dgf = _rms_bwd(x1_ref[...], gffn_ref[...], dh2_ref[...])
        dx1 = dx2_ref[...] + dxn
        dx1_ref[...] = dx1
        dy, dgp = _rms_bwd(y_ref[...], gpost_ref[...], dx1)
        dyb = dy.astype(BF16)
        da_ref[...] = _dot_nt(dyb, wo_ref[...]).astype(BF16)
        _acc(acc, _dot_tn(a_ref[...], dyb), first)
        _acc(dgf_ref, dgf, first)
        _acc(dgp_ref, dgp, first)

        @pl.when(i == nt - 1)
        def _():
            dwo_ref[...] = acc[...].astype(BF16)

    return _launch(
        body, name="bwd_attn_out", grid=(nt,),
        in_specs=[_row_spec(D_MODEL)] * 5 + [_full_spec((D_MODEL, D_MODEL)), _vec_spec(), _vec_spec()],
        out_specs=[_row_spec(D_MODEL), _row_spec(D_MODEL), _full_spec((D_MODEL, D_MODEL)), _vec_spec(), _vec_spec()],
        out_shape=[jax.ShapeDtypeStruct((T, D_MODEL), F32), jax.ShapeDtypeStruct((T, D_MODEL), BF16),
                   jax.ShapeDtypeStruct((D_MODEL, D_MODEL), BF16)] + [jax.ShapeDtypeStruct((1, D_MODEL), F32)] * 2,
        scratch_shapes=[pltpu.VMEM((D_MODEL, D_MODEL), F32)],
        args=(dx2, dh2, x1, y, attn, wo, g_ffn, g_post), job=job)


def _bwd_attention(q, dattn, kpad, vpad, sinks, job=None):
    T = q.shape[0]
    nb = T // ATT_BLOCK

    def body(q_ref, do_ref, k_ref, v_ref, sink_ref, dq_ref, dk_ref, dv_ref, ds_ref, s_scr, dp_scr, p_scr, dsb_scr):
        n = pl.program_id(0)

        @pl.when(n == 0)
        def _():
            dk_ref[...] = jnp.zeros_like(dk_ref)
            dv_ref[...] = jnp.zeros_like(dv_ref)
            ds_ref[...] = jnp.zeros_like(ds_ref)

        start = pl.multiple_of(n * ATT_BLOCK, ATT_BLOCK)
        win = pl.ds(start, 2 * ATT_BLOCK)
        kw = k_ref[win, :]
        vw = v_ref[win, :]
        lane = lax.broadcasted_iota(jnp.int32, (1, ATT_BLOCK), 1)
        dsink = jnp.zeros((1, ATT_BLOCK), F32)
        dqs, dks, dvs = [], [], []
        for kh in range(N_KV_HEADS):
            kk = kw[:, kh * HEAD_DIM:(kh + 1) * HEAD_DIM]
            vv = vw[:, kh * HEAD_DIM:(kh + 1) * HEAD_DIM]
            qs = _stack_heads(q_ref, kh)
            dos = _stack_heads(do_ref, kh)
            s_scr[...] = _dot_nt(qs, kk)
            dp_scr[...] = _dot_nt(dos, vv)
            for g in range(GQA_GROUP):
                h = kh * GQA_GROUP + g
                dsink_h = jnp.zeros((1, 1), F32)
                for row0 in range(0, ATT_BLOCK, ATT_SUB):
                    rows = pl.ds(g * ATT_BLOCK + row0, ATT_SUB)
                    relf, valid = _att_mask(n, row0)
                    pr, ps = _att_probs(s_scr[rows, :], relf, valid, _alibi_slope(h), sink_ref[0, h])
                    dp = dp_scr[rows, :]
                    delta = jnp.sum(pr * dp, axis=-1, keepdims=True)
                    dsb_scr[rows, :] = (pr * (dp - delta) * ATT_SCALE).astype(BF16)
                    p_scr[rows, :] = pr.astype(BF16)
                    dsink_h = dsink_h - jnp.sum(ps * delta, axis=0, keepdims=True)
                dsink = dsink + jnp.where(lane == h, dsink_h, 0.0)
            dsb = dsb_scr[...]
            dqs += _unstack_heads(_dot(dsb, kk))
            dks.append(_dot_tn(dsb, qs))
            dvs.append(_dot_tn(p_scr[...], dos))
        dq_ref[...] = jnp.concatenate(dqs, axis=1).astype(BF16)
        dk_ref[win, :] += jnp.concatenate(dks, axis=1)
        dv_ref[win, :] += jnp.concatenate(dvs, axis=1)
        ds_ref[...] += dsink

    return _launch(
        body, name="bwd_attention", grid=(nb,),
        in_specs=[_row_spec(D_MODEL, ATT_BLOCK), _row_spec(D_MODEL, ATT_BLOCK), _full_spec((T + ATT_BLOCK, KV_DIM)),
                  _full_spec((T + ATT_BLOCK, KV_DIM)), pl.BlockSpec(memory_space=pltpu.SMEM)],
        out_specs=[_row_spec(D_MODEL, ATT_BLOCK), _full_spec((T + ATT_BLOCK, KV_DIM)), _full_spec((T + ATT_BLOCK, KV_DIM)),
                   _full_spec((1, ATT_BLOCK))],
        out_shape=[jax.ShapeDtypeStruct((T, D_MODEL), BF16), jax.ShapeDtypeStruct((T + ATT_BLOCK, KV_DIM), F32),
                   jax.ShapeDtypeStruct((T + ATT_BLOCK, KV_DIM), F32), jax.ShapeDtypeStruct((1, ATT_BLOCK), F32)],
        scratch_shapes=[pltpu.VMEM((ATT_GROUP_ROWS, 2 * ATT_BLOCK), F32)] * 2
                       + [pltpu.VMEM((ATT_GROUP_ROWS, 2 * ATT_BLOCK), BF16)] * 2,
        args=(q, dattn, kpad, vpad, sinks), vmem=VMEM_BIG, job=job)


def _bwd_qkv(dxres, dq, dkv, x3, h1, hk, wq, wkv, g_mix, g_kv, job=None):
    T = x3.shape[0]
    nt = T // ROW_TILE

    def body(dxr_ref, dq_ref, dkv_ref, x_ref, h1_ref, hk_ref, wq_ref, wkv_ref, gmix_ref, gkv_ref,
             dx_ref, dwq_ref, dwkv_ref, dgm_ref, dgk_ref, acc_q, acc_kv):
        i = pl.program_id(0)
        first = i == 0
        dqv = dq_ref[...]
        dkvv = dkv_ref[...]
        xv = x_ref[...]
        d1, dgm = _rms_bwd(xv, gmix_ref[...], _dot_nt(dqv, wq_ref[...]))
        d2, dgk = _rms_bwd(xv, gkv_ref[...], _dot_nt(dkvv, wkv_ref[...]))
        dx_ref[...] = dxr_ref[...] + d1 + d2
        _acc(acc_q, _dot_tn(h1_ref[...], dqv), first)
        _acc(acc_kv, _dot_tn(hk_ref[...], dkvv), first)
        _acc(dgm_ref, dgm, first)
        _acc(dgk_ref, dgk, first)

        @pl.when(i == nt - 1)
        def _():
            dwq_ref[...] = acc_q[...].astype(BF16)
            dwkv_ref[...] = acc_kv[...].astype(BF16)

    return _launch(
        body, name="bwd_qkv", grid=(nt,),
        in_specs=[_row_spec(D_MODEL), _row_spec(D_MODEL), _row_spec(2 * KV_DIM), _row_spec(D_MODEL), _row_spec(D_MODEL),
                  _row_spec(D_MODEL), _full_spec((D_MODEL, D_MODEL)), _full_spec((D_MODEL, 2 * KV_DIM)), _vec_spec(),
                  _vec_spec()],
        out_specs=[_row_spec(D_MODEL), _full_spec((D_MODEL, D_MODEL)), _full_spec((D_MODEL, 2 * KV_DIM)), _vec_spec(),
                   _vec_spec()],
        out_shape=[jax.ShapeDtypeStruct((T, D_MODEL), F32), jax.ShapeDtypeStruct((D_MODEL, D_MODEL), BF16),
                   jax.ShapeDtypeStruct((D_MODEL, 2 * KV_DIM), BF16)] + [jax.ShapeDtypeStruct((1, D_MODEL), F32)] * 2,
        scratch_shapes=[pltpu.VMEM((D_MODEL, D_MODEL), F32), pltpu.VMEM((D_MODEL, 2 * KV_DIM), F32)],
        args=(dxres, dq, dkv, x3, h1, hk, wq, wkv, g_mix, g_kv), job=job)


def _bwd_pool_mixer(dx2, dh2, x1, x, yraw, d, wp, scale, g_ffn, g_post, g_pre, job=None):
    T = x.shape[0]
    tm = ROW_TILE
    nt = T // tm

    def body(dx2_ref, dh2_ref, x1_ref, x_ref, yraw_ref, d_ref, wp_ref, sc_ref, gffn_ref, gpost_ref, gpre_ref,
             dx_ref, dwp_ref, dsc_ref, dgf_ref, dgp_ref, dgm_ref, carry, acc):
        i = pl.program_id(0)
        first = i == 0
        tile = nt - 1 - i

        @pl.when(first)
        def _():
            carry[...] = jnp.zeros_like(carry)

        dxn, dgf = _rms_bwd(x1_ref[...], gffn_ref[...], dh2_ref[...])
        dx1 = dx2_ref[...] + dxn
        yraw = yraw_ref[...]
        sc = sc_ref[...]
        dy, dgp = _rms_bwd(yraw * sc, gpost_ref[...], dx1)
        dsc = jnp.sum(dy * yraw, axis=0, keepdims=True)
        dyb = (dy * sc).astype(BF16)
        dv = d_ref[...]
        dds = []
        for g in range(N_POOL_GROUPS):
            cols = slice(g * POOL_GROUP, (g + 1) * POOL_GROUP)
            dds.append(_dot_nt(dyb[:, cols], wp_ref[g]))
            _acc(acc.at[g], _dot_tn(dv[:, cols], dyb[:, cols]), first)
        dd = jnp.concatenate(dds, axis=1)
        e = dd / _pool_counts(tile * tm, tm)
        ext = jnp.concatenate([e, carry[...]], axis=0)
        carry[...] = e[:POOL_HALO, :]
        sums = _window_sums(ext, lambda k: tm + POOL_HALO - k)[:tm, :]
        dxm, dgm = _rms_bwd(x_ref[...], gpre_ref[...], sums - dd)
        dx_ref[...] = dx1 + dxm
        _acc(dsc_ref, dsc, first)
        _acc(dgf_ref, dgf, first)
        _acc(dgp_ref, dgp, first)
        _acc(dgm_ref, dgm, first)

        @pl.when(i == nt - 1)
        def _():
            dwp_ref[...] = acc[...].astype(BF16)

    rev = pl.BlockSpec((tm, D_MODEL), lambda i: (nt - 1 - i, 0))
    return _launch(
        body, name="bwd_pool_mixer", grid=(nt,),
        in_specs=[rev] * 6 + [_full_spec((N_POOL_GROUPS, POOL_GROUP, POOL_GROUP))] + [_vec_spec()] * 4,
        out_specs=[rev, _full_spec((N_POOL_GROUPS, POOL_GROUP, POOL_GROUP))] + [_vec_spec()] * 4,
        out_shape=[jax.ShapeDtypeStruct((T, D_MODEL), F32),
                   jax.ShapeDtypeStruct((N_POOL_GROUPS, POOL_GROUP, POOL_GROUP), BF16)]
                  + [jax.ShapeDtypeStruct((1, D_MODEL), F32)] * 4,
        scratch_shapes=[pltpu.VMEM((POOL_HALO, D_MODEL), F32), pltpu.VMEM((N_POOL_GROUPS, POOL_GROUP, POOL_GROUP), F32)],
        args=(dx2, dh2, x1, x, yraw, d, wp, scale, g_ffn, g_post, g_pre), job=job)


def _my_place():
    return lax.axis_index("x"), lax.axis_index("y"), lax.axis_index("c")


def _dev_index(px, py, pc):
    return 4 * px + 2 * py + pc


def _peer_by_relation(r):
    x, y, c = _my_place()
    return (x ^ ((r >> 2) & 1), y ^ ((r >> 1) & 1), c ^ (r & 1))


def _slot_pool(ref, j):
    return ref.at[:, pl.ds(pl.multiple_of(j * 32, 32), 32), :]


def _slot_scale(ref, j):
    return ref.at[:, pl.ds(pl.multiple_of(j * 128, 128), 128)]


def _slot_rows128(ref, j):
    return ref.at[pl.ds(pl.multiple_of(j * 128, 128), 128), :]


def _slot_gu(ref, j):
    return ref.at[j % FF_CHUNKS, j // FF_CHUNKS]


def _slot_wd(ref, j):
    return ref.at[pl.ds(pl.multiple_of(j * WD_ROWS, 16), WD_ROWS), :]


def _slot_cols128(ref, j):
    return ref.at[:, pl.ds(pl.multiple_of(j * 128, 128), 128)]


_GATHERED = {
    "pool": ((N_POOL_GROUPS, POOL_GROUP, POOL_GROUP), BF16, _slot_pool),
    "scale": ((1, D_MODEL), F32, _slot_scale),
    "kv": ((D_MODEL, 2 * KV_DIM), BF16, _slot_rows128),
    "q": ((D_MODEL, D_MODEL), BF16, _slot_rows128),
    "o": ((D_MODEL, D_MODEL), BF16, _slot_rows128),
    "gu": ((FF_CHUNKS, 2, FF_BLOCK, D_MODEL), BF16, _slot_gu),
    "wd": ((D_FF, D_MODEL), BF16, _slot_wd),
    "guh": ((FF_CHUNKS, 2, FF_BLOCK, D_MODEL // 2), BF16, _slot_gu),
    "wdh": ((D_FF, D_MODEL // 2), BF16, _slot_wd),
    "gate": ((D_MODEL, D_MODEL), BF16, _slot_rows128),
    "proj": ((PLE_DIM, D_MODEL), BF16, _slot_cols128),
}


def _no_compute():
    pass


class _AllGather:
    def __init__(self, names, shards):
        self.kinds = [_GATHERED[n.rstrip("01_")] for n in names]
        self.args = [shards[n] for n in names]
        self.out_shape = [jax.ShapeDtypeStruct(shape, dtype) for shape, dtype, _ in self.kinds]
        n = len(names)
        self.scratch = [pltpu.SemaphoreType.DMA((n, 7)), pltpu.SemaphoreType.DMA((n, 7)), pltpu.SemaphoreType.DMA((n,))]

    def _plan(self, srcs, outs, sems):
        send_sems, recv_sems, local_sems = sems
        x, y, c = _my_place()

        def slot(t, dev):
            return self.kinds[t][2](outs[t], _dev_index(*dev))

        def copy(t, k, block, to, src=None):
            return pltpu.make_async_remote_copy(
                src_ref=slot(t, block) if src is None else src, dst_ref=slot(t, block),
                send_sem=send_sems.at[t, k], recv_sem=recv_sems.at[t, k], device_id=to, device_id_type=MESH)

        return types.SimpleNamespace(
            copy=copy, core=c, me=(x, y, c), sibling=(x, y, 1 - c),
            x_chip=(1 - x, y), y_chip=(x, 1 - y), far_chip=(1 - x, 1 - y),
            via=(x ^ (1 - c), y ^ c),
            onto=(x ^ c, y ^ (1 - c)),
            k_via=1 + c, k_onto=2 - c,
            local=[pltpu.make_async_copy(srcs[t], slot(t, (x, y, c)), local_sems.at[t]) for t in range(len(srcs))])

    def start(self, srcs, outs, sems):
        p = self._plan(srcs, outs, sems)
        for cp in p.local:
            cp.start()
        for t in range(len(srcs)):
            p.copy(t, 0, p.me, p.sibling, src=srcs[t]).start()
            p.copy(t, 1, p.me, (*p.x_chip, p.core), src=srcs[t]).start()
            p.copy(t, 2, p.me, (*p.y_chip, p.core), src=srcs[t]).start()

    def mid(self, srcs, outs, sems):
        p = self._plan(srcs, outs, sems)
        for t in range(len(srcs)):
            block = (*p.via, p.core)
            p.copy(t, p.k_via, block, p.me).wait_recv()
            p.copy(t, 3, block, (*p.onto, p.core)).start()
            p.copy(t, 3 + p.k_via, block, p.sibling).start()

    def finish(self, srcs, outs, sems):
        p = self._plan(srcs, outs, sems)
        n = len(srcs)
        for t in range(n):
            block = (*p.onto, p.core)
            p.copy(t, p.k_onto, block, p.me).wait_recv()
            p.copy(t, 3 + p.k_onto, block, p.sibling).start()
        for t in range(n):
            block = (*p.far_chip, p.core)
            p.copy(t, 3, block, p.me).wait_recv()
            p.copy(t, 6, block, p.sibling).start()
        other = 1 - p.core
        for t in range(n):
            p.copy(t, 0, (*p.me[:2], other), p.me).wait_recv()
            for k, chip in ((4, p.x_chip), (5, p.y_chip), (6, p.far_chip)):
                p.copy(t, k, (*chip, other), p.me).wait_recv()
            for k in range(7):
                p.copy(t, k, p.me, p.sibling).wait_send()
        for cp in p.local:
            cp.wait()


def _all_gather_only(name, names, shards):
    return _launch(_no_compute, name=name, grid=(), in_specs=[], out_specs=[], out_shape=[], args=(),
                   job=_AllGather(names, shards))[1]


def _block_pool(ref, j):
    return ref.at[:, pl.ds(pl.multiple_of(j * 32, 32), 32), :]


def _block_rows128(ref, j):
    return ref.at[pl.ds(pl.multiple_of(j * 128, 128), 128), :]


def _block_gu(ref, j):
    return ref.at[j % FF_CHUNKS, j // FF_CHUNKS]


def _block_wd(ref, j):
    return ref.at[pl.ds(pl.multiple_of(j * WD_ROWS, 16), WD_ROWS), :]


def _block_cols128(ref, j):
    return ref.at[:, pl.ds(pl.multiple_of(j * 128, 128), 128)]


_SCATTERED = {
    "pool": ((N_POOL_GROUPS, 32, POOL_GROUP), _block_pool),
    "kv": ((128, 2 * KV_DIM), _block_rows128),
    "q": ((128, D_MODEL), _block_rows128),
    "o": ((128, D_MODEL), _block_rows128),
    "gu": ((FF_BLOCK, FF_PART), _block_gu),
    "wd": ((WD_ROWS, FF_PART), _block_wd),
    "gate": ((128, D_MODEL), _block_rows128),
    "proj": ((PLE_DIM, 128), _block_cols128),
}


class _SiblingSwap:
    def __init__(self, pieces):
        self.kinds = [_SCATTERED[kind] for kind, _ in pieces]
        self.args = [g for _, g in pieces]
        self.out_shape = [jax.ShapeDtypeStruct((N_CHIPS, *block), BF16) for block, _ in self.kinds]
        n = len(pieces)
        self.scratch = [pltpu.SemaphoreType.DMA((n, N_CHIPS)), pltpu.SemaphoreType.DMA((n, N_CHIPS))]

    def _copies(self, srcs, outs, sems):
        send_sems, recv_sems = sems
        x, y, c = _my_place()
        return [pltpu.make_async_remote_copy(
            src_ref=block(srcs[t], 2 * ch + 1 - c), dst_ref=outs[t].at[ch], send_sem=send_sems.at[t, ch],
            recv_sem=recv_sems.at[t, ch], device_id=(x, y, 1 - c), device_id_type=MESH)
            for t, (_, block) in enumerate(self.kinds) for ch in range(N_CHIPS)]

    def start(self, srcs, outs, sems):
        for cp in self._copies(srcs, outs, sems):
            cp.start()

    def finish(self, srcs, outs, sems):
        for cp in self._copies(srcs, outs, sems):
            cp.wait()


class _ChipScatter:
    N_BUFS = 4

    def __init__(self, pieces):
        self.kinds = [_SCATTERED[kind] for kind, _, _ in pieces]
        self.n = n = len(pieces)
        self.args = [g for _, g, _ in pieces] + [s for _, _, s in pieces]
        self.out_shape = [jax.ShapeDtypeStruct((2, *block), BF16) for block, _ in self.kinds]
        self.scratch = []
        for block, _ in self.kinds:
            self.scratch += [pltpu.VMEM((N_CHIPS, *block), BF16)] * 3 + [pltpu.VMEM((2, *block), BF16)]
        dma = pltpu.SemaphoreType.DMA
        self.scratch += [dma((n, N_CHIPS + 1)), dma((n, 2)), dma((n, 2)), dma((n,)), dma((n,)), dma((n,))]

    def _plan(self, outs, scr):
        n = self.n
        first_send, first_recv, second_send, second_recv, keep_sems = scr[self.N_BUFS * n + 1:]
        x, y, c = _my_place()
        via = (x ^ (1 - c), y ^ c)
        onto = (x ^ c, y ^ (1 - c))
        index = lambda chip: 2 * chip[0] + chip[1]
        first, second, keep = [], [], []
        for t in range(n):
            total, inbox = scr[self.N_BUFS * t + 2], scr[self.N_BUFS * t + 3]
            for k, chip in enumerate((via, (1 - x, 1 - y))):
                first.append(pltpu.make_async_remote_copy(
                    src_ref=total.at[index(chip)], dst_ref=inbox.at[k], send_sem=first_send.at[t, k],
                    recv_sem=first_recv.at[t, k], device_id=(*via, c), device_id_type=MESH))
            second.append(pltpu.make_async_remote_copy(
                src_ref=total.at[index(onto)], dst_ref=outs[t].at[1], send_sem=second_send.at[t],
                recv_sem=second_recv.at[t], device_id=(*onto, c), device_id_type=MESH))
            keep.append(pltpu.make_async_copy(total.at[index((x, y))], outs[t].at[0], keep_sems.at[t]))
        return first, second, keep, index((x, y)), index(onto)

    def start(self, ins, outs, scr):
        n = self.n
        load_sems = scr[self.N_BUFS * n]
        c = lax.axis_index("c")
        loads = []
        for t, (_, block) in enumerate(self.kinds):
            mine, theirs = scr[self.N_BUFS * t], scr[self.N_BUFS * t + 1]
            loads += [pltpu.make_async_copy(block(ins[t], 2 * ch + c), mine.at[ch], load_sems.at[t, ch])
                      for ch in range(N_CHIPS)]
            loads.append(pltpu.make_async_copy(ins[n + t], theirs, load_sems.at[t, N_CHIPS]))
        for cp in loads:
            cp.start()
        for cp in loads:
            cp.wait()
        for t in range(n):
            mine, theirs, total = scr[self.N_BUFS * t:self.N_BUFS * t + 3]
            for ch in range(N_CHIPS):
                total[ch] = (mine[ch].astype(F32) + theirs[ch].astype(F32)).astype(BF16)
        for cp in self._plan(outs, scr)[0]:
            cp.start()

    def mid(self, ins, outs, scr):
        first, second, keep, me, onto = self._plan(outs, scr)
        for cp in first:
            cp.wait_recv()
        for t in range(self.n):
            total, inbox = scr[self.N_BUFS * t + 2], scr[self.N_BUFS * t + 3]
            for k, slot in enumerate((me, onto)):
                total[slot] = (total[slot].astype(F32) + inbox[k].astype(F32)).astype(BF16)
        for cp in second + keep:
            cp.start()

    def finish(self, ins, outs, scr):
        first, second, keep, _, _ = self._plan(outs, scr)
        for cp in first:
            cp.wait_send()
        for cp in second + keep:
            cp.wait()


class _Jobs:
    def __init__(self, *jobs):
        self.jobs = jobs
        self.args = [a for j in jobs for a in j.args]
        self.out_shape = [o for j in jobs for o in j.out_shape]
        self.scratch = [s for j in jobs for s in j.scratch]

    def _split(self, refs, attr):
        at = 0
        for j in self.jobs:
            n = len(getattr(j, attr))
            yield refs[at:at + n]
            at += n

    def _each(self, ins, outs, scr):
        return zip(self.jobs, self._split(ins, "args"), self._split(outs, "out_shape"), self._split(scr, "scratch"))

    def start(self, ins, outs, scr):
        for j, i, o, s in self._each(ins, outs, scr):
            j.start(i, o, s)

    def mid(self, ins, outs, scr):
        for j, i, o, s in self._each(ins, outs, scr):
            if hasattr(j, "mid"):
                j.mid(i, o, s)

    def finish(self, ins, outs, scr):
        for j, i, o, s in self._each(ins, outs, scr):
            j.finish(i, o, s)

    def split_outputs(self, outs):
        return list(self._split(outs, "out_shape"))


def _adamw_math(w, g, m, v):
    m = ADAM_B1 * m + (1.0 - ADAM_B1) * g
    v = ADAM_B2 * v + (1.0 - ADAM_B2) * (g * g)
    m_hat = m / (1.0 - ADAM_B1 ** ADAM_STEP)
    v_hat = v / (1.0 - ADAM_B2 ** ADAM_STEP)
    delta = -ADAM_LR * (m_hat / (jnp.sqrt(v_hat) + ADAM_EPS) + ADAM_WD * w)
    return delta, m, v


def _adamw(name, w, m, v, landings, n_col_blocks=1, job=None):
    n_slots, r, c = landings[0].shape
    grid = (w.shape[0] // r, n_col_blocks)

    def body(w_ref, m_ref, v_ref, *rest):
        l_refs, (g_ref, d_ref, nm_ref, nv_ref) = rest[:len(landings)], rest[len(landings):]
        step = pl.program_id(0) * n_col_blocks + pl.program_id(1)
        for idx, l_ref in enumerate(l_refs):
            @pl.when(step == idx)
            def _(l_ref=l_ref):
                g = l_ref[0].astype(F32)
                for s in range(1, n_slots):
                    g = g + l_ref[s].astype(F32)
                g_ref[...] = g
                d_ref[...], nm_ref[...], nv_ref[...] = _adamw_math(w_ref[...], g, m_ref[...], v_ref[...])

    spec = pl.BlockSpec((r, c), lambda a, b: (a, b))
    return _launch(
        body, name=f"adamw_{name}", grid=grid,
        in_specs=[spec, spec, spec] + [_full_spec((n_slots, r, c))] * len(landings),
        out_specs=[spec] * 4, out_shape=[jax.ShapeDtypeStruct(w.shape, F32)] * 4,
        args=(w, m, v, *landings), vmem=VMEM_BIG, job=job)


_SMALL = (("pre_mix_g", SV_PRE_MIX, 2), ("post_mix_g", SV_POST_MIX, 2), ("pre_ffn_g", SV_PRE_FFN, 2),
          ("post_ffn_g", SV_POST_FFN, 2), ("ple_g", SV_PLE, 2), ("ple_post_g", SV_PLE_POST, 2), ("kv_g", SV_KV, 1),
          ("pool_scale", SV_POOL_SCALE, 1), ("sinks", SV_SINKS, 1))


def _small_all_reduce_adamw(part, params):
    flat = [a for name, _, _ in _SMALL for a in params[name]]
    n_in = 1 + len(flat)

    def body(*refs):
        part_ref, wmv = refs[0], refs[1:n_in]
        loss_ref, outs = refs[n_in], refs[n_in + 1:n_in + 1 + 4 * len(_SMALL)]
        buf, total, send_sems, recv_sems = refs[n_in + 1 + 4 * len(_SMALL):]
        x, y, c = _my_place()
        me = _dev_index(x, y, c)
        buf[me] = part_ref[...]
        copies = [pltpu.make_async_remote_copy(
            src_ref=buf.at[me], dst_ref=buf.at[me], send_sem=send_sems.at[r - 1], recv_sem=recv_sems.at[r - 1],
            device_id=_peer_by_relation(r), device_id_type=MESH) for r in range(1, N_DEV)]
        for cp in copies:
            cp.start()
        for cp in copies:
            cp.wait()
        g = buf[0]
        for s in range(1, N_DEV):
            g = g + buf[s]
        total[...] = g
        loss_ref[...] = total[SV_LOSS:SV_LOSS + 1, 0:1]
        for idx, (name, row, n_rows) in enumerate(_SMALL):
            w_ref, m_ref, v_ref = wmv[3 * idx:3 * idx + 3]
            g_ref, d_ref, nm_ref, nv_ref = outs[4 * idx:4 * idx + 4]
            if name == "pool_scale":
                g = total[row:row + 1, pl.ds(pl.multiple_of(me * 128, 128), 128)]
            else:
                g = total[row:row + n_rows, 0:w_ref.shape[1]]
            g_ref[...] = g
            d_ref[...], nm_ref[...], nv_ref[...] = _adamw_math(w_ref[...], g, m_ref[...], v_ref[...])

    out_shape = [jax.ShapeDtypeStruct((1, 1), F32)]
    for name, _, _ in _SMALL:
        out_shape += [jax.ShapeDtypeStruct(params[name][0].shape, F32)] * 4
    res, _ = _launch(
        body, name="small_all_reduce_adamw", grid=(1,),
        in_specs=[_full_spec(a.shape) for a in (part, *flat)], out_specs=[_full_spec(s.shape) for s in out_shape],
        out_shape=out_shape,
        scratch_shapes=[pltpu.VMEM((N_DEV, SV_ROWS, D_MODEL), F32), pltpu.VMEM((SV_ROWS, D_MODEL), F32),
                        pltpu.SemaphoreType.DMA((N_DEV - 1,)), pltpu.SemaphoreType.DMA((N_DEV - 1,))],
        args=(part, *flat))
    return res[0], {name: res[1 + 4 * idx:5 + 4 * idx] for idx, (name, _, _) in enumerate(_SMALL)}


def _local_step(x, p, tgt, gains, sinks, shards, weights):
    row = lambda first_row, layer: _Gain(gains, first_row + layer)
    gather = lambda *names: _AllGather(names, shards)
    g_pre_mix, g_post_mix, g_pre_ffn, g_post_ffn = SV_PRE_MIX, SV_POST_MIX, SV_PRE_FFN, SV_POST_FFN
    g_ple, g_ple_post, g_kv = SV_PLE, SV_PLE_POST, _Gain(gains, SV_KV)

    wp, scale, wgu0 = _all_gather_only("gather_first", ("pool", "scale", "gu0"), shards)
    wgu0 = [wgu0]
    (x1_0, h2_0, yraw, dpool), wd0 = _fwd_pool_mixer(
        x, row(g_pre_mix, 0), wp, scale, row(g_post_mix, 0), row(g_pre_ffn, 0), job=gather("wd0"))
    (gs0, us0, f0, x2_0, h3_0), (wgate0, wproj0, wkv, wq, wgu1_a) = _fwd_ffn(
        0, h2_0, x1_0, wgu0, wd0, row(g_post_ffn, 0), row(g_ple, 0),
        job=gather("gate0", "proj0", "kv", "q", "guh1_0"))
    (x3_0, z0, pe0), (wo,) = _fwd_ple(0, x2_0, h3_0, p[0], wgate0, wproj0, row(g_ple_post, 0), job=gather("o"))
    (hk, h1, q, kv), _ = _fwd_qkv(x3_0, g_kv, row(g_pre_mix, 1), wkv, wq)
    front = ((ATT_BLOCK, 0), (0, 0))
    kpad = jnp.pad(kv[:, :KV_DIM], front)
    vpad = jnp.pad(kv[:, KV_DIM:], front)
    (attn,), (wgu1_b, wd1_a) = _fwd_attention(q, kpad, vpad, sinks, job=gather("guh1_1", "wdh1_0"))
    (y1, x1_1, h2_1), (wd1_b,) = _fwd_attn_out(attn, x3_0, wo, row(g_post_mix, 1), row(g_pre_ffn, 1),
                                               job=gather("wdh1_1"))
    wgu1, wd1 = [wgu1_a, wgu1_b], [wd1_a, wd1_b]
    (gs1, us1, f1, x2_1, h3_1), (wgate1, wproj1) = _fwd_ffn(
        1, h2_1, x1_1, wgu1, wd1, row(g_post_ffn, 1), row(g_ple, 1), job=gather("gate1", "proj1"))

    produced, swapped, landed = {}, {}, {}

    def kind_of(name):
        return name.rstrip("0123_")

    def carry(swap=(), spread=()):
        jobs = []
        if swap:
            jobs.append(_SiblingSwap([(kind_of(n), produced[n]) for n in swap]))
        if spread:
            jobs.append(_ChipScatter([(kind_of(n), produced[n], swapped[n]) for n in spread]))
        return _Jobs(*jobs)

    def carried(jobs, outs, swap=(), spread=()):
        parts = jobs.split_outputs(outs)
        if swap:
            swapped.update(zip(swap, parts[0]))
        if spread:
            landed.update(zip(spread, parts[-1]))

    def hosted(call, *args, swap=(), spread=()):
        jobs = carry(swap, spread)
        outs, job_outs = call(*args, job=jobs)
        carried(jobs, job_outs, swap, spread)
        return outs

    def ffn_weight_grads(layer, h2, df, dg, du, a, hosts):
        for qtr in range(FF_PARTS):
            dgu, dwd = hosted(_bwd_ffn_dw, layer, qtr, h2, df, dg, du, a, **hosts[qtr])
            produced[f"gu{layer}_{qtr}"], produced[f"wd{layer}_{qtr}"] = dgu, dwd

    ffn_q = lambda layer, qtr: (f"gu{layer}_{qtr}", f"wd{layer}_{qtr}")

    dx2_1, df1, produced["gate1"], produced["proj1"], dg_ple_post1, dg_ple1, dg_post_ffn1, loss = hosted(
        _ple_loss_bwd, 1, x2_1, h3_1, p[1], f1, tgt, wgate1, wproj1, row(g_ple_post, 1), row(g_ple, 1),
        row(g_post_ffn, 1))
    dh2_1, dg1, du1, a1 = hosted(_bwd_ffn_act, 1, df1, gs1, us1, wgu1, wd1, swap=("gate1", "proj1"))
    ffn_weight_grads(1, h2_1, df1, dg1, du1, a1, [dict(spread=("gate1", "proj1")), dict(swap=ffn_q(1, 0))])
    dx1_1, dattn, produced["o"], dg_pre_ffn1, dg_post_mix1 = hosted(
        _bwd_attn_out, dx2_1, dh2_1, x1_1, y1, attn, wo, row(g_pre_ffn, 1), row(g_post_mix, 1), swap=ffn_q(1, 1))
    dq, dkpad, dvpad, dsinks = hosted(_bwd_attention, q, dattn, kpad, vpad, sinks, spread=ffn_q(1, 0))
    dkv = jnp.concatenate([dkpad[ATT_BLOCK:], dvpad[ATT_BLOCK:]], axis=1).astype(BF16)
    dx3_0, produced["q"], produced["kv"], dg_pre_mix1, dg_kv = hosted(
        _bwd_qkv, dx1_1, dq, dkv, x3_0, h1, hk, wq, wkv, row(g_pre_mix, 1), g_kv, swap=("o",))
    dx2_0, df0, produced["gate0"], produced["proj0"], dg_ple_post0, dg_ple0, dg_post_ffn0 = hosted(
        _bwd_ple, 0, dx3_0, x2_0, z0, pe0, h3_0, p[0], f0, wgate0, row(g_ple_post, 0), row(g_ple, 0),
        row(g_post_ffn, 0), swap=("q", "kv"), spread=("gu1_1",))
    dh2_0, dg0, du0, a0 = hosted(_bwd_ffn_act, 0, df0, gs0, us0, wgu0, wd0,
                                 swap=("gate0", "proj0"), spread=("wd1_1", "o"))
    ffn_weight_grads(0, h2_0, df0, dg0, du0, a0, [
        dict(spread=("gate0", "proj0", "q", "kv")), dict(swap=ffn_q(0, 0))])
    grad_x, produced["pool"], dscale, dg_pre_ffn0, dg_post_mix0, dg_pre_mix0 = hosted(
        _bwd_pool_mixer, dx2_0, dh2_0, x1_0, x, yraw, dpool, wp, scale, row(g_pre_ffn, 0), row(g_post_mix, 0),
        row(g_pre_mix, 0), swap=ffn_q(0, 1), spread=ffn_q(0, 0))

    def update(name, n_col_blocks=1, pieces=None, swap=(), spread=()):
        w, m, v = weights[name]
        rows = w.size // w.shape[-1]
        flat = [landed[n].reshape(landed[n].shape[0], -1, landed[n].shape[-1])
                for n in (pieces or [kind_short[name]])]
        outs = hosted(_adamw, name, w.reshape(rows, -1), m.reshape(rows, -1), v.reshape(rows, -1), flat,
                      n_col_blocks, swap=swap, spread=spread)
        return [o.reshape(w.shape) for o in outs]

    kind_short = {"w_q": "q", "w_kv": "kv", "w_o": "o", "pool_w": "pool"}
    upd = {}
    upd["w_ple_gate"] = update("w_ple_gate", pieces=("gate0", "gate1"), swap=("pool",), spread=ffn_q(0, 1))
    upd["w_ple_proj"] = update("w_ple_proj", pieces=("proj0", "proj1"), spread=("pool",))
    for name in ("w_q", "w_kv", "w_o", "pool_w"):
        upd[name] = update(name)
    upd["w_gu"] = update("w_gu", FF_PARTS,
                         pieces=[f"gu{layer}_{qtr}" for layer in range(2) for qtr in range(FF_PARTS)])
    upd["w_gu"] = [jnp.swapaxes(a, 1, 2) for a in upd["w_gu"]]
    upd["w_down"] = update("w_down", FF_PARTS,
                           pieces=[f"wd{layer}_{qtr}" for layer in range(2) for qtr in range(FF_PARTS)])

    lanes = lambda a: jnp.pad(a, ((0, 0), (0, D_MODEL - a.shape[1])))
    small = jnp.concatenate([
        dg_pre_mix0, dg_pre_mix1, dg_post_mix0, dg_post_mix1, dg_pre_ffn0, dg_pre_ffn1, dg_post_ffn0, dg_post_ffn1,
        dg_ple0, dg_ple1, dg_ple_post0, dg_ple_post1, dg_kv, dscale, lanes(dsinks[:, :N_HEADS]), lanes(loss)], axis=0)
    return grad_x, upd, small


def kernel(x, p, pre_mix_g, post_mix_g, pre_ffn_g, post_ffn_g, pool_w, pool_scale, kv_g, w_kv, w_q, sinks, w_o, w_gu, w_down, ple_g, w_ple_gate, w_ple_proj, ple_post_g, loss_target, m_pre_mix_g, m_post_mix_g, m_pre_ffn_g, m_post_ffn_g, m_pool_w, m_pool_scale, m_kv_g, m_w_kv, m_w_q, m_sinks, m_w_o, m_w_gu, m_w_down, m_ple_g, m_w_ple_gate, m_w_ple_proj, m_ple_post_g, v_pre_mix_g, v_post_mix_g, v_pre_ffn_g, v_post_ffn_g, v_pool_w, v_pool_scale, v_kv_g, v_w_kv, v_w_q, v_sinks, v_w_o, v_w_gu, v_w_down, v_ple_g, v_w_ple_gate, v_w_ple_proj, v_ple_post_g):
    shards = {"pool": pool_w[0].astype(BF16), "scale": pool_scale, "kv": w_kv.astype(BF16),
              "q": w_q[0].astype(BF16), "o": w_o[0].astype(BF16)}
    for layer in range(2):
        shards[f"gu{layer}"] = w_gu[layer].T.astype(BF16)
        shards[f"wd{layer}"] = w_down[layer].astype(BF16)
        for half in range(2):
            cols = slice(half * D_MODEL // 2, (half + 1) * D_MODEL // 2)
            shards[f"guh{layer}_{half}"] = shards[f"gu{layer}"][:, cols]
            shards[f"wdh{layer}_{half}"] = shards[f"wd{layer}"][:, cols]
        shards[f"gate{layer}"] = w_ple_gate[layer].astype(BF16)
        shards[f"proj{layer}"] = w_ple_proj[layer].astype(BF16)
    gains = jnp.concatenate([pre_mix_g, post_mix_g, pre_ffn_g, post_ffn_g, ple_g, ple_post_g, kv_g[None, :]],
                            axis=0).reshape(-1, 1, D_MODEL)
    weights = {"pool_w": (pool_w, m_pool_w, v_pool_w), "w_kv": (w_kv, m_w_kv, v_w_kv), "w_q": (w_q, m_w_q, v_w_q),
               "w_o": (w_o, m_w_o, v_w_o), "w_down": (w_down, m_w_down, v_w_down),
               "w_gu": tuple(jnp.swapaxes(a, 1, 2) for a in (w_gu, m_w_gu, v_w_gu)),
               "w_ple_gate": (w_ple_gate, m_w_ple_gate, v_w_ple_gate),
               "w_ple_proj": (w_ple_proj, m_w_ple_proj, v_w_ple_proj)}
    grad_x, upd, small = _local_step(x[0], p[:, 0], loss_target[0], gains, sinks, shards, weights)

    small_params = {
        "pre_mix_g": (pre_mix_g, m_pre_mix_g, v_pre_mix_g), "post_mix_g": (post_mix_g, m_post_mix_g, v_post_mix_g),
        "pre_ffn_g": (pre_ffn_g, m_pre_ffn_g, v_pre_ffn_g), "post_ffn_g": (post_ffn_g, m_post_ffn_g, v_post_ffn_g),
        "ple_g": (ple_g, m_ple_g, v_ple_g), "ple_post_g": (ple_post_g, m_ple_post_g, v_ple_post_g),
        "kv_g": (kv_g[None, :], m_kv_g[None, :], v_kv_g[None, :]),
        "pool_scale": (pool_scale, m_pool_scale, v_pool_scale), "sinks": (sinks, m_sinks, v_sinks)}
    loss, small_upd = _small_all_reduce_adamw(small, small_params)
    small_upd["kv_g"] = [a[0] for a in small_upd["kv_g"]]
    upd.update(small_upd)

    names = ["pre_mix_g", "post_mix_g", "pre_ffn_g", "post_ffn_g", "pool_w", "pool_scale", "kv_g", "w_kv", "w_q",
             "sinks", "w_o", "w_gu", "w_down", "ple_g", "w_ple_gate", "w_ple_proj", "ple_post_g"]
    outs = [loss[0, 0], grad_x[None]]
    for kind in range(4):
        outs += [upd[n][kind] for n in names]
    return tuple(outs)
```

```python
import functools
import types

import jax
import jax.numpy as jnp
from jax import lax
from jax.experimental import pallas as pl
from jax.experimental.pallas import tpu as pltpu

F32 = jnp.float32
BF16 = jnp.bfloat16

N_DEV = 8
D_MODEL = 1024
N_POOL_GROUPS = 4
POOL_GROUP = 256
POOL_HALO = 16
HEAD_DIM = 64
N_HEADS = 16
N_KV_HEADS = 4
GQA_GROUP = 4
KV_DIM = N_KV_HEADS * HEAD_DIM
ATT_BLOCK = 128
D_FF = 2816
FF_CHUNKS = 4
FF_BLOCK = D_FF // FF_CHUNKS
WD_ROWS = D_FF // N_DEV
FF_PARTS = 2
FF_PART = D_MODEL // FF_PARTS
N_CHIPS = 4
PLE_DIM = 256
EPS = 1e-6
NEG_INF = -1e30
ATT_SCALE = HEAD_DIM ** -0.5

ADAM_LR = 0.001
ADAM_B1 = 0.9
ADAM_B2 = 0.999
ADAM_EPS = 1e-08
ADAM_WD = 0.01
ADAM_STEP = 10

ROW_TILE = 512
FFN_ROW_TILE = 512
FFN_SUB_TILES = 2
VMEM_BIG = 60 * 1024 * 1024
VMEM_MID = 56 * 1024 * 1024
HBM_PIN_ELEMS = 1024

SV_ROWS = 16
SV_PRE_MIX, SV_POST_MIX, SV_PRE_FFN, SV_POST_FFN, SV_PLE, SV_PLE_POST = 0, 2, 4, 6, 8, 10
SV_KV, SV_POOL_SCALE, SV_SINKS, SV_LOSS = 12, 13, 14, 15

MESH = pl.DeviceIdType.MESH
ANY = pl.BlockSpec(memory_space=pl.ANY)


def _dot(a, b):
    return jnp.dot(a, b, preferred_element_type=F32)


def _dot_nt(a, b):
    return lax.dot_general(a, b, (((1,), (1,)), ((), ())), preferred_element_type=F32)


def _dot_tn(a, b):
    return lax.dot_general(a, b, (((0,), (0,)), ((), ())), preferred_element_type=F32)


def _rstd(x):
    return lax.rsqrt(jnp.mean(x * x, axis=-1, keepdims=True) + EPS)


def _rms(x, g):
    return x * _rstd(x) * g


def _rms_bwd(x, g, dy):
    r = _rstd(x)
    n = x * r
    dn = dy * g
    dx = r * (dn - n * jnp.mean(dn * n, axis=-1, keepdims=True))
    dg = jnp.sum(dy * n, axis=0, keepdims=True)
    return dx, dg


def _sigmoid(x):
    return 1.0 / (1.0 + jnp.exp(-x))


def _acc(ref, val, first):
    @pl.when(first)
    def _():
        ref[...] = val

    @pl.when(jnp.logical_not(first))
    def _():
        ref[...] += val


def _pool_counts(row0, rows):
    t = row0 + lax.broadcasted_iota(jnp.int32, (rows, D_MODEL), 0) + 1
    grp = lax.broadcasted_iota(jnp.int32, (rows, D_MODEL), 1) // POOL_GROUP
    win = jnp.left_shift(2, grp)
    return jnp.minimum(t, win).astype(F32)


def _window_sums(ext, shift_of):
    outs = []
    s = ext
    for gi in range(N_POOL_GROUPS):
        s = s + pltpu.roll(s, shift_of(1 << gi), axis=0)
        outs.append(s[:, :POOL_GROUP])
        s = s[:, POOL_GROUP:]
    return jnp.concatenate(outs, axis=1)


def _cparams(n_axes, vmem, collective_id=None):
    return pltpu.CompilerParams(dimension_semantics=("arbitrary",) * n_axes, vmem_limit_bytes=vmem,
                                collective_id=collective_id)


_PEER_SETS = (("sibling", "x", "y"), ("sibling",), ("x", "y"))


def _meet(peers):
    x, y, c = lax.axis_index("x"), lax.axis_index("y"), lax.axis_index("c")
    device = {"sibling": (x, y, 1 - c), "x": (1 - x, y, c), "y": (x, 1 - y, c)}
    barrier = pltpu.get_barrier_semaphore()
    for peer in peers:
        pl.semaphore_signal(barrier, inc=1, device_id=device[peer], device_id_type=pl.DeviceIdType.MESH)
    pl.semaphore_wait(barrier, len(peers))


def _row_spec(cols, tm=ROW_TILE):
    return pl.BlockSpec((tm, cols), lambda i: (i, 0))


def _full_spec(shape):
    zeros = (0,) * len(shape)
    return pl.BlockSpec(shape, lambda *_: zeros)


def _vec_spec():
    return _full_spec((1, D_MODEL))


def _column_ranges(parts):
    ends = [0]
    for part in parts:
        ends.append(ends[-1] + part.shape[-1])
    return list(zip(ends[:-1], ends[1:]))


class _Gain:
    def __init__(self, stacked, layer):
        self.stacked, self.layer = stacked, layer

    def spec(self):
        layer = self.layer
        return pl.BlockSpec((None, 1, D_MODEL), lambda *_: (layer, 0, 0))


def _in_hbm(a):
    return pltpu.with_memory_space_constraint(a, pltpu.HBM) if a.size >= HBM_PIN_ELEMS else a


def _out_in_hbm(s):
    return pltpu.HBM(s.shape, s.dtype) if s.size >= HBM_PIN_ELEMS else s


def _launch(body, *, name, grid, in_specs, out_specs, out_shape, args, scratch_shapes=(), vmem=VMEM_MID, job=None):
    in_specs = [a.spec() if isinstance(a, _Gain) else s for s, a in zip(in_specs, args)]
    args = [_in_hbm(a.stacked if isinstance(a, _Gain) else a) for a in args]
    n_in, n_out, n_scr = len(args), len(out_shape), len(scratch_shapes)
    if job is not None and not job.args:
        job = None
    j_args, j_out, j_scr = ([], [], []) if job is None else ([_in_hbm(a) for a in job.args], job.out_shape, job.scratch)

    def run(*refs):
        groups, at = [], 0
        for n in (n_in, len(j_args), n_out, len(j_out), n_scr, len(j_scr)):
            groups.append(refs[at:at + n])
            at += n
        ins, j_ins, outs, j_outs, scr, j_sems = groups

        def begin():
            _meet(job.peers)
            job.start(j_ins, j_outs, j_sems)

        if job is None:
            body(*ins, *outs, *scr)
        elif not grid:
            begin()
            job.mid(j_ins, j_outs, j_sems)
            body(*ins, *outs, *scr)
            job.finish(j_ins, j_outs, j_sems)
        else:
            ids = [pl.program_id(a) for a in range(len(grid))]
            first = functools.reduce(jnp.logical_and, [i == 0 for i in ids])
            half = functools.reduce(jnp.logical_and, [ids[0] == grid[0] // 2] + [i == 0 for i in ids[1:]])
            last = functools.reduce(jnp.logical_and, [i == g - 1 for i, g in zip(ids, grid)])
            pl.when(first)(begin)
            pl.when(half)(lambda: job.mid(j_ins, j_outs, j_sems))
            body(*ins, *outs, *scr)
            pl.when(last)(lambda: job.finish(j_ins, j_outs, j_sems))

    res = pl.pallas_call(
        run, name=name, grid=grid,
        in_specs=list(in_specs) + [ANY] * len(j_args), out_specs=list(out_specs) + [ANY] * len(j_out),
        out_shape=[_out_in_hbm(s) for s in list(out_shape) + list(j_out)],
        scratch_shapes=list(scratch_shapes) + list(j_scr),
        compiler_params=_cparams(len(grid), vmem, None if job is None else _PEER_SETS.index(job.peers)),
    )(*args, *j_args)
    return res[:n_out], res[n_out:]


def _fwd_pool_mixer(x, g_pre, wp, scale, g_post, g_ffn, job=None):
    T = x.shape[0]
    tm = ROW_TILE
    nt = T // tm

    def body(x_ref, gpre_ref, wp_ref, sc_ref, gpost_ref, gffn_ref, x1_ref, h2_ref, yraw_ref, d_ref, carry):
        i = pl.program_id(0)

        @pl.when(i == 0)
        def _():
            carry[...] = jnp.zeros_like(carry)

        xv = x_ref[...]
        h = _rms(xv, gpre_ref[...])
        ext = jnp.concatenate([carry[...], h], axis=0)
        carry[...] = h[tm - POOL_HALO:, :]
        sums = _window_sums(ext, lambda k: k)[POOL_HALO:, :]
        d = sums / _pool_counts(i * tm, tm) - h
        db = d.astype(BF16)
        d_ref[...] = db
        yraw = jnp.concatenate(
            [_dot(db[:, g * POOL_GROUP:(g + 1) * POOL_GROUP], wp_ref[g]) for g in range(N_POOL_GROUPS)], axis=1)
        yraw_ref[...] = yraw
        x1 = xv + _rms(yraw * sc_ref[...], gpost_ref[...])
        x1_ref[...] = x1
        h2_ref[...] = _rms(x1, gffn_ref[...]).astype(BF16)

    return _launch(
        body, name="fwd_pool_mixer", grid=(nt,),
        in_specs=[_row_spec(D_MODEL), _vec_spec(), _full_spec((N_POOL_GROUPS, POOL_GROUP, POOL_GROUP)), _vec_spec(),
                  _vec_spec(), _vec_spec()],
        out_specs=[_row_spec(D_MODEL)] * 4,
        out_shape=[jax.ShapeDtypeStruct((T, D_MODEL), F32), jax.ShapeDtypeStruct((T, D_MODEL), BF16),
                   jax.ShapeDtypeStruct((T, D_MODEL), F32), jax.ShapeDtypeStruct((T, D_MODEL), BF16)],
        scratch_shapes=[pltpu.VMEM((POOL_HALO, D_MODEL), F32)],
        args=(x, g_pre, wp, scale, g_post, g_ffn), job=job)


def _fwd_ffn(layer, h2, x1, wgu, wd, g_post, g_ple, job=None):
    T = h2.shape[0]
    tm = min(FFN_ROW_TILE, T)
    nt = T // tm
    sub = tm // FFN_SUB_TILES
    last = FF_CHUNKS - 1
    n_gu, n_wd = len(wgu), len(wd)
    gu_cols = _column_ranges(wgu)

    def body(h2_ref, x1_ref, *refs):
        wgu_refs, wd_refs = refs[:n_gu], refs[n_gu:n_gu + n_wd]
        gpost_ref, gple_ref, gs_ref, us_ref, f_ref, x2_ref, h3_ref, acc = refs[n_gu + n_wd:]
        k = pl.program_id(0)
        i = pl.program_id(1)
        rows = pl.ds(pl.multiple_of(i * tm, tm), tm)
        parts = []
        for s in range(FFN_SUB_TILES):
            r = pl.ds(s * sub, sub)
            g = sum(_dot_nt(h2_ref[r, c0:c1], w[0]) for (c0, c1), w in zip(gu_cols, wgu_refs))
            u = sum(_dot_nt(h2_ref[r, c0:c1], w[1]) for (c0, c1), w in zip(gu_cols, wgu_refs))
            gs_ref[r, :] = g.astype(BF16)
            us_ref[r, :] = u.astype(BF16)
            a = (g * _sigmoid(g) * u).astype(BF16)
            parts.append(jnp.concatenate([_dot(a, w[...]) for w in wd_refs], axis=1))
        part = jnp.concatenate(parts, axis=0)

        @pl.when(k == 0)
        def _():
            acc[rows, :] = part

        @pl.when(jnp.logical_and(k > 0, k < last))
        def _():
            acc[rows, :] += part

        @pl.when(k == last)
        def _():
            f = acc[rows, :] + part
            f_ref[...] = f
            x2 = x1_ref[...] + _rms(f, gpost_ref[...])
            x2_ref[...] = x2
            h3_ref[...] = _rms(x2, gple_ref[...]).astype(BF16)

    def late(k, i):
        return (jnp.where(k == last, i, 0), 0)

    return _launch(
        body, name=f"fwd_ffn{layer}", grid=(FF_CHUNKS, nt),
        in_specs=[pl.BlockSpec((tm, D_MODEL), lambda k, i: (i, 0)), pl.BlockSpec((tm, D_MODEL), late)]
                 + [pl.BlockSpec((None, 2, FF_BLOCK, w.shape[-1]), lambda k, i: (k, 0, 0, 0)) for w in wgu]
                 + [pl.BlockSpec((FF_BLOCK, w.shape[-1]), lambda k, i: (k, 0)) for w in wd]
                 + [pl.BlockSpec((1, D_MODEL), lambda k, i: (0, 0))] * 2,
        out_specs=[pl.BlockSpec((None, tm, FF_BLOCK), lambda k, i: (k, i, 0)),
                   pl.BlockSpec((None, tm, FF_BLOCK), lambda k, i: (k, i, 0)),
                   pl.BlockSpec((tm, D_MODEL), late),
                   pl.BlockSpec((tm, D_MODEL), late),
                   pl.BlockSpec((tm, D_MODEL), late)],
        out_shape=[jax.ShapeDtypeStruct((FF_CHUNKS, T, FF_BLOCK), BF16),
                   jax.ShapeDtypeStruct((FF_CHUNKS, T, FF_BLOCK), BF16),
                   jax.ShapeDtypeStruct((T, D_MODEL), F32),
                   jax.ShapeDtypeStruct((T, D_MODEL), F32),
                   jax.ShapeDtypeStruct((T, D_MODEL), BF16)],
        scratch_shapes=[pltpu.VMEM((T, D_MODEL), F32)],
        args=(h2, x1, *wgu, *wd, g_post, g_ple), vmem=VMEM_BIG, job=job)


def _fwd_ple(layer, x2, h3, p, wgate, wproj, g_post, job=None):
    T = x2.shape[0]
    nt = T // ROW_TILE

    def body(x2_ref, h3_ref, p_ref, wg_ref, wp_ref, gpost_ref, x3_ref, z_ref, pe_ref):
        z = _dot(h3_ref[...], wg_ref[...])
        pe = _dot(p_ref[...].astype(BF16), wp_ref[...])
        z_ref[...] = z
        pe_ref[...] = pe
        x3_ref[...] = x2_ref[...] + _rms(pe * _sigmoid(z), gpost_ref[...])

    return _launch(
        body, name=f"fwd_ple{layer}", grid=(nt,),
        in_specs=[_row_spec(D_MODEL), _row_spec(D_MODEL), _row_spec(PLE_DIM), _full_spec((D_MODEL, D_MODEL)),
                  _full_spec((PLE_DIM, D_MODEL)), _vec_spec()],
        out_specs=[_row_spec(D_MODEL)] * 3, out_shape=[jax.ShapeDtypeStruct((T, D_MODEL), F32)] * 3,
        args=(x2, h3, p, wgate, wproj, g_post), job=job)


def _fwd_qkv(x3, g_kv, g_mix, wkv, wq, job=None):
    T = x3.shape[0]
    nt = T // ROW_TILE

    def body(x_ref, gkv_ref, gmix_ref, wkv_ref, wq_ref, hk_ref, h1_ref, q_ref, kv_ref):
        xv = x_ref[...]
        r = _rstd(xv)
        hk = (xv * r * gkv_ref[...]).astype(BF16)
        h1 = (xv * r * gmix_ref[...]).astype(BF16)
        hk_ref[...] = hk
        h1_ref[...] = h1
        kv_ref[...] = _dot(hk, wkv_ref[...]).astype(BF16)
        q_ref[...] = _dot(h1, wq_ref[...]).astype(BF16)

    return _launch(
        body, name="fwd_qkv", grid=(nt,),
        in_specs=[_row_spec(D_MODEL), _vec_spec(), _vec_spec(), _full_spec((D_MODEL, 2 * KV_DIM)),
                  _full_spec((D_MODEL, D_MODEL))],
        out_specs=[_row_spec(D_MODEL), _row_spec(D_MODEL), _row_spec(D_MODEL), _row_spec(2 * KV_DIM)],
        out_shape=[jax.ShapeDtypeStruct((T, D_MODEL), BF16)] * 3 + [jax.ShapeDtypeStruct((T, 2 * KV_DIM), BF16)],
        args=(x3, g_kv, g_mix, wkv, wq), job=job)


def _alibi_slope(h):
    return 2.0 ** (-8.0 * (h + 1) / N_HEADS)


ATT_SUB = 32
ATT_GROUP_ROWS = GQA_GROUP * ATT_BLOCK


def _att_mask(n, row0):
    qi = lax.broadcasted_iota(jnp.int32, (ATT_SUB, 2 * ATT_BLOCK), 0) + row0
    si = lax.broadcasted_iota(jnp.int32, (ATT_SUB, 2 * ATT_BLOCK), 1)
    rel = ATT_BLOCK + qi - si
    valid = (rel >= 0) & (rel < ATT_BLOCK) & ((si >= ATT_BLOCK) | (n > 0))
    return rel.astype(F32), valid


def _att_probs(raw, relf, valid, slope, sink):
    s = jnp.where(valid, raw * ATT_SCALE - slope * relf, NEG_INF)
    m = jnp.maximum(jnp.max(s, axis=-1, keepdims=True), sink)
    e = jnp.exp(s - m)
    es = jnp.exp(sink - m)
    inv = 1.0 / (jnp.sum(e, axis=-1, keepdims=True) + es)
    return e * inv, es * inv


def _stack_heads(ref, kh):
    first = kh * GQA_GROUP
    return jnp.concatenate([ref[:, (first + g) * HEAD_DIM:(first + g + 1) * HEAD_DIM] for g in range(GQA_GROUP)], axis=0)


def _unstack_heads(stacked):
    return [stacked[g * ATT_BLOCK:(g + 1) * ATT_BLOCK, :] for g in range(GQA_GROUP)]


def _fwd_attention(q, kpad, vpad, sinks, job=None):
    T = q.shape[0]
    nb = T // ATT_BLOCK

    def body(q_ref, k_ref, v_ref, sink_ref, o_ref, s_scr, p_scr):
        n = pl.program_id(0)
        start = pl.multiple_of(n * ATT_BLOCK, ATT_BLOCK)
        kw = k_ref[pl.ds(start, 2 * ATT_BLOCK), :]
        vw = v_ref[pl.ds(start, 2 * ATT_BLOCK), :]
        outs = []
        for kh in range(N_KV_HEADS):
            kk = kw[:, kh * HEAD_DIM:(kh + 1) * HEAD_DIM]
            vv = vw[:, kh * HEAD_DIM:(kh + 1) * HEAD_DIM]
            s_scr[...] = _dot_nt(_stack_heads(q_ref, kh), kk)
            for g in range(GQA_GROUP):
                h = kh * GQA_GROUP + g
                for row0 in range(0, ATT_BLOCK, ATT_SUB):
                    rows = pl.ds(g * ATT_BLOCK + row0, ATT_SUB)
                    relf, valid = _att_mask(n, row0)
                    pr, _ = _att_probs(s_scr[rows, :], relf, valid, _alibi_slope(h), sink_ref[0, h])
                    p_scr[rows, :] = pr.astype(BF16)
            outs += _unstack_heads(_dot(p_scr[...], vv))
        o_ref[...] = jnp.concatenate(outs, axis=1).astype(BF16)

    return _launch(
        body, name="fwd_attention", grid=(nb,),
        in_specs=[_row_spec(D_MODEL, ATT_BLOCK), _full_spec((T + ATT_BLOCK, KV_DIM)), _full_spec((T + ATT_BLOCK, KV_DIM)),
                  pl.BlockSpec(memory_space=pltpu.SMEM)],
        out_specs=[_row_spec(D_MODEL, ATT_BLOCK)],
        out_shape=[jax.ShapeDtypeStruct((T, D_MODEL), BF16)],
        scratch_shapes=[pltpu.VMEM((ATT_GROUP_ROWS, 2 * ATT_BLOCK), F32), pltpu.VMEM((ATT_GROUP_ROWS, 2 * ATT_BLOCK), BF16)],
        args=(q, kpad, vpad, sinks), job=job)


def _fwd_attn_out(attn, x, wo, g_post, g_ffn, job=None):
    T = x.shape[0]
    nt = T // ROW_TILE

    def body(a_ref, x_ref, wo_ref, gpost_ref, gffn_ref, y_ref, x1_ref, h2_ref):
        y = _dot(a_ref[...], wo_ref[...])
        y_ref[...] = y
        x1 = x_ref[...] + _rms(y, gpost_ref[...])
        x1_ref[...] = x1
        h2_ref[...] = _rms(x1, gffn_ref[...]).astype(BF16)

    return _launch(
        body, name="fwd_attn_out", grid=(nt,),
        in_specs=[_row_spec(D_MODEL), _row_spec(D_MODEL), _full_spec((D_MODEL, D_MODEL)), _vec_spec(), _vec_spec()],
        out_specs=[_row_spec(D_MODEL)] * 3,
        out_shape=[jax.ShapeDtypeStruct((T, D_MODEL), F32), jax.ShapeDtypeStruct((T, D_MODEL), F32),
                   jax.ShapeDtypeStruct((T, D_MODEL), BF16)],
        args=(attn, x, wo, g_post, g_ffn), job=job)


def _bwd_ple(layer, dx3, x2, z, pe, h3, p, f, wgate, g_ple_post, g_ple, g_post_ffn, job=None):
    T = x2.shape[0]
    tm = ROW_TILE
    nt = T // tm

    def body(dx3_ref, x2_ref, z_ref, pe_ref, h3_ref, p_ref, f_ref, wg_ref, gpp_ref, gp_ref, gpf_ref,
             dx2_ref, df_ref, dwg_ref, dwp_ref, dgpp_ref, dgp_ref, dgpf_ref, acc_g, acc_p):
        i = pl.program_id(0)
        first = i == 0
        dx3v = dx3_ref[...]
        gate = _sigmoid(z_ref[...])
        pev = pe_ref[...]
        de, dgpp = _rms_bwd(pev * gate, gpp_ref[...], dx3v)
        dpe = (de * gate).astype(BF16)
        dz = (de * pev * gate * (1.0 - gate)).astype(BF16)
        _acc(acc_p, _dot_tn(p_ref[...].astype(BF16), dpe), first)
        _acc(acc_g, _dot_tn(h3_ref[...], dz), first)
        dh3 = _dot_nt(dz, wg_ref[...])
        dxn, dgp = _rms_bwd(x2_ref[...], gp_ref[...], dh3)
        dx2 = dx3v + dxn
        dx2_ref[...] = dx2
        df, dgpf = _rms_bwd(f_ref[...], gpf_ref[...], dx2)
        df_ref[...] = df.astype(BF16)
        _acc(dgpp_ref, dgpp, first)
        _acc(dgp_ref, dgp, first)
        _acc(dgpf_ref, dgpf, first)

        @pl.when(i == nt - 1)
        def _():
            dwg_ref[...] = acc_g[...].astype(BF16)
            dwp_ref[...] = acc_p[...].astype(BF16)

    return _launch(
        body, name=f"bwd_ple{layer}", grid=(nt,),
        in_specs=[_row_spec(D_MODEL)] * 5 + [_row_spec(PLE_DIM), _row_spec(D_MODEL), _full_spec((D_MODEL, D_MODEL)),
                  _vec_spec(), _vec_spec(), _vec_spec()],
        out_specs=[_row_spec(D_MODEL), _row_spec(D_MODEL), _full_spec((D_MODEL, D_MODEL)), _full_spec((PLE_DIM, D_MODEL)),
                   _vec_spec(), _vec_spec(), _vec_spec()],
        out_shape=[jax.ShapeDtypeStruct((T, D_MODEL), F32), jax.ShapeDtypeStruct((T, D_MODEL), BF16),
                   jax.ShapeDtypeStruct((D_MODEL, D_MODEL), BF16), jax.ShapeDtypeStruct((PLE_DIM, D_MODEL), BF16)]
                  + [jax.ShapeDtypeStruct((1, D_MODEL), F32)] * 3,
        scratch_shapes=[pltpu.VMEM((D_MODEL, D_MODEL), F32), pltpu.VMEM((PLE_DIM, D_MODEL), F32)],
        args=(dx3, x2, z, pe, h3, p, f, wgate, g_ple_post, g_ple, g_post_ffn), vmem=VMEM_BIG, job=job)


def _ple_loss_bwd(layer, x2, h3, p, f, target, wgate, wproj, g_ple_post, g_ple, g_post_ffn, job=None):
    T = x2.shape[0]
    tm = ROW_TILE
    nt = T // tm

    def body(x2_ref, h3_ref, p_ref, f_ref, tgt_ref, wg_ref, wp_ref, gpp_ref, gp_ref, gpf_ref,
             dx2_ref, df_ref, dwg_ref, dwp_ref, dgpp_ref, dgp_ref, dgpf_ref, loss_ref, acc_g, acc_p):
        i = pl.program_id(0)
        first = i == 0
        h3 = h3_ref[...]
        pb = p_ref[...].astype(BF16)
        x2v = x2_ref[...]
        gate = _sigmoid(_dot(h3, wg_ref[...]))
        pev = _dot(pb, wp_ref[...])
        e = pev * gate
        err = x2v + _rms(e, gpp_ref[...]) - tgt_ref[...]
        _acc(loss_ref, 0.5 * jnp.sum(jnp.mean(err * err, axis=-1, keepdims=True), axis=0, keepdims=True), first)
        dx3v = err * (1.0 / D_MODEL)
        de, dgpp = _rms_bwd(e, gpp_ref[...], dx3v)
        dpe = (de * gate).astype(BF16)
        dz = (de * pev * gate * (1.0 - gate)).astype(BF16)
        _acc(acc_p, _dot_tn(pb, dpe), first)
        _acc(acc_g, _dot_tn(h3, dz), first)
        dxn, dgp = _rms_bwd(x2v, gp_ref[...], _dot_nt(dz, wg_ref[...]))
        dx2 = dx3v + dxn
        dx2_ref[...] = dx2
        df, dgpf = _rms_bwd(f_ref[...], gpf_ref[...], dx2)
        df_ref[...] = df.astype(BF16)
        _acc(dgpp_ref, dgpp, first)
        _acc(dgp_ref, dgp, first)
        _acc(dgpf_ref, dgpf, first)

        @pl.when(i == nt - 1)
        def _():
            dwg_ref[...] = acc_g[...].astype(BF16)
            dwp_ref[...] = acc_p[...].astype(BF16)

    return _launch(
        body, name=f"ple_loss_bwd{layer}", grid=(nt,),
        in_specs=[_row_spec(D_MODEL), _row_spec(D_MODEL), _row_spec(PLE_DIM), _row_spec(D_MODEL), _row_spec(D_MODEL),
                  _full_spec((D_MODEL, D_MODEL)), _full_spec((PLE_DIM, D_MODEL)), _vec_spec(), _vec_spec(), _vec_spec()],
        out_specs=[_row_spec(D_MODEL), _row_spec(D_MODEL), _full_spec((D_MODEL, D_MODEL)), _full_spec((PLE_DIM, D_MODEL)),
                   _vec_spec(), _vec_spec(), _vec_spec(), _full_spec((1, 1))],
        out_shape=[jax.ShapeDtypeStruct((T, D_MODEL), F32), jax.ShapeDtypeStruct((T, D_MODEL), BF16),
                   jax.ShapeDtypeStruct((D_MODEL, D_MODEL), BF16), jax.ShapeDtypeStruct((PLE_DIM, D_MODEL), BF16)]
                  + [jax.ShapeDtypeStruct((1, D_MODEL), F32)] * 3 + [jax.ShapeDtypeStruct((1, 1), F32)],
        scratch_shapes=[pltpu.VMEM((D_MODEL, D_MODEL), F32), pltpu.VMEM((PLE_DIM, D_MODEL), F32)],
        args=(x2, h3, p, f, target, wgate, wproj, g_ple_post, g_ple, g_post_ffn), vmem=VMEM_BIG, job=job)


def _bwd_ffn_act(layer, df, gs, us, wgu, wd, job=None):
    T = df.shape[0]
    tm = min(FFN_ROW_TILE, T)
    nt = T // tm
    sub = tm // FFN_SUB_TILES
    last = FF_CHUNKS - 1
    n_gu, n_wd = len(wgu), len(wd)
    wd_cols = _column_ranges(wd)

    def body(df_ref, gs_ref, us_ref, *refs):
        wgu_refs, wd_refs = refs[:n_gu], refs[n_gu:n_gu + n_wd]
        dh_ref, dg_ref, du_ref, a_ref, acc_h = refs[n_gu + n_wd:]
        k = pl.program_id(0)
        i = pl.program_id(1)
        rows = pl.ds(pl.multiple_of(i * tm, tm), tm)
        dhs = []
        for s in range(FFN_SUB_TILES):
            r = pl.ds(s * sub, sub)
            g = gs_ref[r, :].astype(F32)
            u = us_ref[r, :].astype(F32)
            sg = _sigmoid(g)
            silu = g * sg
            a_ref[r, :] = (silu * u).astype(BF16)
            da = sum(_dot_nt(df_ref[r, c0:c1], w[...]) for (c0, c1), w in zip(wd_cols, wd_refs))
            dg = (da * u * (sg * (1.0 + g * (1.0 - sg)))).astype(BF16)
            du = (da * silu).astype(BF16)
            dg_ref[r, :] = dg
            du_ref[r, :] = du
            dhs.append(jnp.concatenate([_dot(dg, w[0]) + _dot(du, w[1]) for w in wgu_refs], axis=1))
        dh = jnp.concatenate(dhs, axis=0)

        @pl.when(k == 0)
        def _():
            acc_h[rows, :] = dh

        @pl.when(jnp.logical_and(k > 0, k < last))
        def _():
            acc_h[rows, :] += dh

        @pl.when(k == last)
        def _():
            dh_ref[...] = acc_h[rows, :] + dh

    chunk_rows = pl.BlockSpec((None, tm, FF_BLOCK), lambda k, i: (k, i, 0))
    saved = jax.ShapeDtypeStruct((FF_CHUNKS, T, FF_BLOCK), BF16)
    return _launch(
        body, name=f"bwd_ffn_act{layer}", grid=(FF_CHUNKS, nt),
        in_specs=[pl.BlockSpec((tm, D_MODEL), lambda k, i: (i, 0)), chunk_rows, chunk_rows]
                 + [pl.BlockSpec((None, 2, FF_BLOCK, w.shape[-1]), lambda k, i: (k, 0, 0, 0)) for w in wgu]
                 + [pl.BlockSpec((FF_BLOCK, w.shape[-1]), lambda k, i: (k, 0)) for w in wd],
        out_specs=[pl.BlockSpec((tm, D_MODEL), lambda k, i: (jnp.where(k == last, i, 0), 0)),
                   chunk_rows, chunk_rows, chunk_rows],
        out_shape=[jax.ShapeDtypeStruct((T, D_MODEL), F32), saved, saved, saved],
        scratch_shapes=[pltpu.VMEM((T, D_MODEL), F32)],
        args=(df, gs, us, *wgu, *wd), vmem=VMEM_BIG, job=job)


def _bwd_ffn_dw(layer, q, h2, df, dg, du, a, job=None):
    T = h2.shape[0]

    def body(h_ref, df_ref, dg_ref, du_ref, a_ref, dgu_ref, dwd_ref):
        h = h_ref[...]
        dgu_ref[0] = _dot_tn(dg_ref[...], h).astype(BF16)
        dgu_ref[1] = _dot_tn(du_ref[...], h).astype(BF16)
        dwd_ref[...] = _dot_tn(a_ref[...], df_ref[...]).astype(BF16)

    cols = pl.BlockSpec((T, FF_PART), lambda k: (0, q))
    chunk = pl.BlockSpec((None, T, FF_BLOCK), lambda k: (k, 0, 0))
    return _launch(
        body, name=f"bwd_ffn_dw{layer}_{q}", grid=(FF_CHUNKS,),
        in_specs=[cols, cols, chunk, chunk, chunk],
        out_specs=[pl.BlockSpec((None, 2, FF_BLOCK, FF_PART), lambda k: (k, 0, 0, 0)),
                   pl.BlockSpec((FF_BLOCK, FF_PART), lambda k: (k, 0))],
        out_shape=[jax.ShapeDtypeStruct((FF_CHUNKS, 2, FF_BLOCK, FF_PART), BF16),
                   jax.ShapeDtypeStruct((D_FF, FF_PART), BF16)],
        args=(h2, df, dg, du, a), vmem=VMEM_BIG, job=job)


def _bwd_attn_out(dx2, dh2, x1, y, attn, wo, g_ffn, g_post, job=None):
    T = x1.shape[0]
    nt = T // ROW_TILE

    def body(dx2_ref, dh2_ref, x1_ref, y_ref, a_ref, wo_ref, gffn_ref, gpost_ref,
             dx1_ref, da_ref, dwo_ref, dgf_ref, dgp_ref, acc):
        i = pl.program_id(0)
        first = i == 0
        dxn, dgf = _rms_bwd(x1_ref[...], gffn_ref[...], dh2_ref[...])
        dx1 = dx2_ref[...] + dxn
        dx1_ref[...] = dx1
        dy, dgp = _rms_bwd(y_ref[...], gpost_ref[...], dx1)
        dyb = dy.astype(BF16)
        da_ref[...] = _dot_nt(dyb, wo_ref[...]).astype(BF16)
        _acc(acc, _dot_tn(a_ref[...], dyb), first)
        _acc(dgf_ref, dgf, first)
        _acc(dgp_ref, dgp, first)

        @pl.when(i == nt - 1)
        def _():
            dwo_ref[...] = acc[...].astype(BF16)

    return _launch(
        body, name="bwd_attn_out", grid=(nt,),
        in_specs=[_row_spec(D_MODEL)] * 5 + [_full_spec((D_MODEL, D_MODEL)), _vec_spec(), _vec_spec()],
        out_specs=[_row_spec(D_MODEL), _row_spec(D_MODEL), _full_spec((D_MODEL, D_MODEL)), _vec_spec(), _vec_spec()],
        out_shape=[jax.ShapeDtypeStruct((T, D_MODEL), F32), jax.ShapeDtypeStruct((T, D_MODEL), BF16),
                   jax.ShapeDtypeStruct((D_MODEL, D_MODEL), BF16)] + [jax.ShapeDtypeStruct((1, D_MODEL), F32)] * 2,
        scratch_shapes=[pltpu.VMEM((D_MODEL, D_MODEL), F32)],
        args=(dx2, dh2, x1, y, attn, wo, g_ffn, g_post), job=job)


def _bwd_attention(q, dattn, kpad, vpad, sinks, job=None):
    T = q.shape[0]
    nb = T // ATT_BLOCK

    def body(q_ref, do_ref, k_ref, v_ref, sink_ref, dq_ref, dk_ref, dv_ref, ds_ref, s_scr, dp_scr, p_scr, dsb_scr):
        n = pl.program_id(0)

        @pl.when(n == 0)
        def _():
            dk_ref[...] = jnp.zeros_like(dk_ref)
            dv_ref[...] = jnp.zeros_like(dv_ref)
            ds_ref[...] = jnp.zeros_like(ds_ref)

        start = pl.multiple_of(n * ATT_BLOCK, ATT_BLOCK)
        win = pl.ds(start, 2 * ATT_BLOCK)
        kw = k_ref[win, :]
        vw = v_ref[win, :]
        lane = lax.broadcasted_iota(jnp.int32, (1, ATT_BLOCK), 1)
        dsink = jnp.zeros((1, ATT_BLOCK), F32)
        dqs, dks, dvs = [], [], []
        for kh in range(N_KV_HEADS):
            kk = kw[:, kh * HEAD_DIM:(kh + 1) * HEAD_DIM]
            vv = vw[:, kh * HEAD_DIM:(kh + 1) * HEAD_DIM]
            qs = _stack_heads(q_ref, kh)
            dos = _stack_heads(do_ref, kh)
            s_scr[...] = _dot_nt(qs, kk)
            dp_scr[...] = _dot_nt(dos, vv)
            for g in range(GQA_GROUP):
                h = kh * GQA_GROUP + g
                dsink_h = jnp.zeros((1, 1), F32)
                for row0 in range(0, ATT_BLOCK, ATT_SUB):
                    rows = pl.ds(g * ATT_BLOCK + row0, ATT_SUB)
                    relf, valid = _att_mask(n, row0)
                    pr, ps = _att_probs(s_scr[rows, :], relf, valid, _alibi_slope(h), sink_ref[0, h])
                    dp = dp_scr[rows, :]
                    delta = jnp.sum(pr * dp, axis=-1, keepdims=True)
                    dsb_scr[rows, :] = (pr * (dp - delta) * ATT_SCALE).astype(BF16)
                    p_scr[rows, :] = pr.astype(BF16)
                    dsink_h = dsink_h - jnp.sum(ps * delta, axis=0, keepdims=True)
                dsink = dsink + jnp.where(lane == h, dsink_h, 0.0)
            dsb = dsb_scr[...]
            dqs += _unstack_heads(_dot(dsb, kk))
            dks.append(_dot_tn(dsb, qs))
            dvs.append(_dot_tn(p_scr[...], dos))
        dq_ref[...] = jnp.concatenate(dqs, axis=1).astype(BF16)
        dk_ref[win, :] += jnp.concatenate(dks, axis=1)
        dv_ref[win, :] += jnp.concatenate(dvs, axis=1)
        ds_ref[...] += dsink

    return _launch(
        body, name="bwd_attention", grid=(nb,),
        in_specs=[_row_spec(D_MODEL, ATT_BLOCK), _row_spec(D_MODEL, ATT_BLOCK), _full_spec((T + ATT_BLOCK, KV_DIM)),
                  _full_spec((T + ATT_BLOCK, KV_DIM)), pl.BlockSpec(memory_space=pltpu.SMEM)],
        out_specs=[_row_spec(D_MODEL, ATT_BLOCK), _full_spec((T + ATT_BLOCK, KV_DIM)), _full_spec((T + ATT_BLOCK, KV_DIM)),
                   _full_spec((1, ATT_BLOCK))],
        out_shape=[jax.ShapeDtypeStruct((T, D_MODEL), BF16), jax.ShapeDtypeStruct((T + ATT_BLOCK, KV_DIM), F32),
                   jax.ShapeDtypeStruct((T + ATT_BLOCK, KV_DIM), F32), jax.ShapeDtypeStruct((1, ATT_BLOCK), F32)],
        scratch_shapes=[pltpu.VMEM((ATT_GROUP_ROWS, 2 * ATT_BLOCK), F32)] * 2
                       + [pltpu.VMEM((ATT_GROUP_ROWS, 2 * ATT_BLOCK), BF16)] * 2,
        args=(q, dattn, kpad, vpad, sinks), vmem=VMEM_BIG, job=job)


def _bwd_qkv(dxres, dq, dkv, x3, h1, hk, wq, wkv, g_mix, g_kv, job=None):
    T = x3.shape[0]
    nt = T // ROW_TILE

    def body(dxr_ref, dq_ref, dkv_ref, x_ref, h1_ref, hk_ref, wq_ref, wkv_ref, gmix_ref, gkv_ref,
             dx_ref, dwq_ref, dwkv_ref, dgm_ref, dgk_ref, acc_q, acc_kv):
        i = pl.program_id(0)
        first = i == 0
        dqv = dq_ref[...]
        dkvv = dkv_ref[...]
        xv = x_ref[...]
        d1, dgm = _rms_bwd(xv, gmix_ref[...], _dot_nt(dqv, wq_ref[...]))
        d2, dgk = _rms_bwd(xv, gkv_ref[...], _dot_nt(dkvv, wkv_ref[...]))
        dx_ref[...] = dxr_ref[...] + d1 + d2
        _acc(acc_q, _dot_tn(h1_ref[...], dqv), first)
        _acc(acc_kv, _dot_tn(hk_ref[...], dkvv), first)
        _acc(dgm_ref, dgm, first)
        _acc(dgk_ref, dgk, first)

        @pl.when(i == nt - 1)
        def _():
            dwq_ref[...] = acc_q[...].astype(BF16)
            dwkv_ref[...] = acc_kv[...].astype(BF16)

    return _launch(
        body, name="bwd_qkv", grid=(nt,),
        in_specs=[_row_spec(D_MODEL), _row_spec(D_MODEL), _row_spec(2 * KV_DIM), _row_spec(D_MODEL), _row_spec(D_MODEL),
                  _row_spec(D_MODEL), _full_spec((D_MODEL, D_MODEL)), _full_spec((D_MODEL, 2 * KV_DIM)), _vec_spec(),
                  _vec_spec()],
        out_specs=[_row_spec(D_MODEL), _full_spec((D_MODEL, D_MODEL)), _full_spec((D_MODEL, 2 * KV_DIM)), _vec_spec(),
                   _vec_spec()],
        out_shape=[jax.ShapeDtypeStruct((T, D_MODEL), F32), jax.ShapeDtypeStruct((D_MODEL, D_MODEL), BF16),
                   jax.ShapeDtypeStruct((D_MODEL, 2 * KV_DIM), BF16)] + [jax.ShapeDtypeStruct((1, D_MODEL), F32)] * 2,
        scratch_shapes=[pltpu.VMEM((D_MODEL, D_MODEL), F32), pltpu.VMEM((D_MODEL, 2 * KV_DIM), F32)],
        args=(dxres, dq, dkv, x3, h1, hk, wq, wkv, g_mix, g_kv), job=job)


def _bwd_pool_mixer(dx2, dh2, x1, x, yraw, d, wp, scale, g_ffn, g_post, g_pre, job=None):
    T = x.shape[0]
    tm = ROW_TILE
    nt = T // tm

    def body(dx2_ref, dh2_ref, x1_ref, x_ref, yraw_ref, d_ref, wp_ref, sc_ref, gffn_ref, gpost_ref, gpre_ref,
             dx_ref, dwp_ref, dsc_ref, dgf_ref, dgp_ref, dgm_ref, carry, acc):
        i = pl.program_id(0)
        first = i == 0
        tile = nt - 1 - i

        @pl.when(first)
        def _():
            carry[...] = jnp.zeros_like(carry)

        dxn, dgf = _rms_bwd(x1_ref[...], gffn_ref[...], dh2_ref[...])
        dx1 = dx2_ref[...] + dxn
        yraw = yraw_ref[...]
        sc = sc_ref[...]
        dy, dgp = _rms_bwd(yraw * sc, gpost_ref[...], dx1)
        dsc = jnp.sum(dy * yraw, axis=0, keepdims=True)
        dyb = (dy * sc).astype(BF16)
        dv = d_ref[...]
        dds = []
        for g in range(N_POOL_GROUPS):
            cols = slice(g * POOL_GROUP, (g + 1) * POOL_GROUP)
            dds.append(_dot_nt(dyb[:, cols], wp_ref[g]))
            _acc(acc.at[g], _dot_tn(dv[:, cols], dyb[:, cols]), first)
        dd = jnp.concatenate(dds, axis=1)
        e = dd / _pool_counts(tile * tm, tm)
        ext = jnp.concatenate([e, carry[...]], axis=0)
        carry[...] = e[:POOL_HALO, :]
        sums = _window_sums(ext, lambda k: tm + POOL_HALO - k)[:tm, :]
        dxm, dgm = _rms_bwd(x_ref[...], gpre_ref[...], sums - dd)
        dx_ref[...] = dx1 + dxm
        _acc(dsc_ref, dsc, first)
        _acc(dgf_ref, dgf, first)
        _acc(dgp_ref, dgp, first)
        _acc(dgm_ref, dgm, first)

        @pl.when(i == nt - 1)
        def _():
            dwp_ref[...] = acc[...].astype(BF16)

    rev = pl.BlockSpec((tm, D_MODEL), lambda i: (nt - 1 - i, 0))
    return _launch(
        body, name="bwd_pool_mixer", grid=(nt,),
        in_specs=[rev] * 6 + [_full_spec((N_POOL_GROUPS, POOL_GROUP, POOL_GROUP))] + [_vec_spec()] * 4,
        out_specs=[rev, _full_spec((N_POOL_GROUPS, POOL_GROUP, POOL_GROUP))] + [_vec_spec()] * 4,
        out_shape=[jax.ShapeDtypeStruct((T, D_MODEL), F32),
                   jax.ShapeDtypeStruct((N_POOL_GROUPS, POOL_GROUP, POOL_GROUP), BF16)]
                  + [jax.ShapeDtypeStruct((1, D_MODEL), F32)] * 4,
        scratch_shapes=[pltpu.VMEM((POOL_HALO, D_MODEL), F32), pltpu.VMEM((N_POOL_GROUPS, POOL_GROUP, POOL_GROUP), F32)],
        args=(dx2, dh2, x1, x, yraw, d, wp, scale, g_ffn, g_post, g_pre), job=job)


def _my_place():
    return lax.axis_index("x"), lax.axis_index("y"), lax.axis_index("c")


def _dev_index(px, py, pc):
    return 4 * px + 2 * py + pc


def _peer_by_relation(r):
    x, y, c = _my_place()
    return (x ^ ((r >> 2) & 1), y ^ ((r >> 1) & 1), c ^ (r & 1))


def _slot_pool(ref, j):
    return ref.at[:, pl.ds(pl.multiple_of(j * 32, 32), 32), :]


def _slot_scale(ref, j):
    return ref.at[:, pl.ds(pl.multiple_of(j * 128, 128), 128)]


def _slot_rows128(ref, j):
    return ref.at[pl.ds(pl.multiple_of(j * 128, 128), 128), :]


def _slot_gu(ref, j):
    return ref.at[j % FF_CHUNKS, j // FF_CHUNKS]


def _slot_wd(ref, j):
    return ref.at[pl.ds(pl.multiple_of(j * WD_ROWS, 16), WD_ROWS), :]


def _slot_cols128(ref, j):
    return ref.at[:, pl.ds(pl.multiple_of(j * 128, 128), 128)]


_GATHERED = {
    "pool": ((N_POOL_GROUPS, POOL_GROUP, POOL_GROUP), BF16, _slot_pool),
    "scale": ((1, D_MODEL), F32, _slot_scale),
    "kv": ((D_MODEL, 2 * KV_DIM), BF16, _slot_rows128),
    "q": ((D_MODEL, D_MODEL), BF16, _slot_rows128),
    "o": ((D_MODEL, D_MODEL), BF16, _slot_rows128),
    "gu": ((FF_CHUNKS, 2, FF_BLOCK, D_MODEL), BF16, _slot_gu),
    "wd": ((D_FF, D_MODEL), BF16, _slot_wd),
    "guh": ((FF_CHUNKS, 2, FF_BLOCK, D_MODEL // 2), BF16, _slot_gu),
    "wdh": ((D_FF, D_MODEL // 2), BF16, _slot_wd),
    "gate": ((D_MODEL, D_MODEL), BF16, _slot_rows128),
    "proj": ((PLE_DIM, D_MODEL), BF16, _slot_cols128),
}


def _no_compute():
    pass


class _AllGather:
    peers = ("sibling", "x", "y")

    def __init__(self, names, shards):
        self.kinds = [_GATHERED[n.rstrip("01_")] for n in names]
        self.args = [shards[n] for n in names]
        self.out_shape = [jax.ShapeDtypeStruct(shape, dtype) for shape, dtype, _ in self.kinds]
        n = len(names)
        self.scratch = [pltpu.SemaphoreType.DMA((n, 7)), pltpu.SemaphoreType.DMA((n, 7)), pltpu.SemaphoreType.DMA((n,))]

    def _plan(self, srcs, outs, sems):
        send_sems, recv_sems, local_sems = sems
        x, y, c = _my_place()

        def slot(t, dev):
            return self.kinds[t][2](outs[t], _dev_index(*dev))

        def copy(t, k, block, to, src=None):
            return pltpu.make_async_remote_copy(
                src_ref=slot(t, block) if src is None else src, dst_ref=slot(t, block),
                send_sem=send_sems.at[t, k], recv_sem=recv_sems.at[t, k], device_id=to, device_id_type=MESH)

        return types.SimpleNamespace(
            copy=copy, core=c, me=(x, y, c), sibling=(x, y, 1 - c),
            x_chip=(1 - x, y), y_chip=(x, 1 - y), far_chip=(1 - x, 1 - y),
            via=(x ^ (1 - c), y ^ c),
            onto=(x ^ c, y ^ (1 - c)),
            k_via=1 + c, k_onto=2 - c,
            local=[pltpu.make_async_copy(srcs[t], slot(t, (x, y, c)), local_sems.at[t]) for t in range(len(srcs))])

    def start(self, srcs, outs, sems):
        p = self._plan(srcs, outs, sems)
        for cp in p.local:
            cp.start()
        for t in range(len(srcs)):
            p.copy(t, 0, p.me, p.sibling, src=srcs[t]).start()
            p.copy(t, 1, p.me, (*p.x_chip, p.core), src=srcs[t]).start()
            p.copy(t, 2, p.me, (*p.y_chip, p.core), src=srcs[t]).start()

    def mid(self, srcs, outs, sems):
        p = self._plan(srcs, outs, sems)
        for t in range(len(srcs)):
            block = (*p.via, p.core)
            p.copy(t, p.k_via, block, p.me).wait_recv()
            p.copy(t, 3, block, (*p.onto, p.core)).start()
            p.copy(t, 3 + p.k_via, block, p.sibling).start()

    def finish(self, srcs, outs, sems):
        p = self._plan(srcs, outs, sems)
        n = len(srcs)
        for t in range(n):
            block = (*p.onto, p.core)
            p.copy(t, p.k_onto, block, p.me).wait_recv()
            p.copy(t, 3 + p.k_onto, block, p.sibling).start()
        for t in range(n):
            block = (*p.far_chip, p.core)
            p.copy(t, 3, block, p.me).wait_recv()
            p.copy(t, 6, block, p.sibling).start()
        other = 1 - p.core
        for t in range(n):
            p.copy(t, 0, (*p.me[:2], other), p.me).wait_recv()
            for k, chip in ((4, p.x_chip), (5, p.y_chip), (6, p.far_chip)):
                p.copy(t, k, (*chip, other), p.me).wait_recv()
            for k in range(7):
                p.copy(t, k, p.me, p.sibling).wait_send()
        for cp in p.local:
            cp.wait()


def _all_gather_only(name, names, shards):
    return _launch(_no_compute, name=name, grid=(), in_specs=[], out_specs=[], out_shape=[], args=(),
                   job=_AllGather(names, shards))[1]


def _block_pool(ref, j):
    return ref.at[:, pl.ds(pl.multiple_of(j * 32, 32), 32), :]


def _block_rows128(ref, j):
    return ref.at[pl.ds(pl.multiple_of(j * 128, 128), 128), :]


def _block_gu(ref, j):
    return ref.at[j % FF_CHUNKS, j // FF_CHUNKS]


def _block_wd(ref, j):
    return ref.at[pl.ds(pl.multiple_of(j * WD_ROWS, 16), WD_ROWS), :]


def _block_cols128(ref, j):
    return ref.at[:, pl.ds(pl.multiple_of(j * 128, 128), 128)]


_SCATTERED = {
    "pool": ((N_POOL_GROUPS, 32, POOL_GROUP), _block_pool),
    "kv": ((128, 2 * KV_DIM), _block_rows128),
    "q": ((128, D_MODEL), _block_rows128),
    "o": ((128, D_MODEL), _block_rows128),
    "gu": ((FF_BLOCK, FF_PART), _block_gu),
    "wd": ((WD_ROWS, FF_PART), _block_wd),
    "gate": ((128, D_MODEL), _block_rows128),
    "proj": ((PLE_DIM, 128), _block_cols128),
}


class _SiblingSwap:
    peers = ("sibling",)

    def __init__(self, pieces):
        self.kinds = [_SCATTERED[kind] for kind, _ in pieces]
        self.args = [g for _, g in pieces]
        self.out_shape = [jax.ShapeDtypeStruct((N_CHIPS, *block), BF16) for block, _ in self.kinds]
        n = len(pieces)
        self.scratch = [pltpu.SemaphoreType.DMA((n, N_CHIPS)), pltpu.SemaphoreType.DMA((n, N_CHIPS))]

    def _copies(self, srcs, outs, sems):
        send_sems, recv_sems = sems
        x, y, c = _my_place()
        return [pltpu.make_async_remote_copy(
            src_ref=block(srcs[t], 2 * ch + 1 - c), dst_ref=outs[t].at[ch], send_sem=send_sems.at[t, ch],
            recv_sem=recv_sems.at[t, ch], device_id=(x, y, 1 - c), device_id_type=MESH)
            for t, (_, block) in enumerate(self.kinds) for ch in range(N_CHIPS)]

    def start(self, srcs, outs, sems):
        for cp in self._copies(srcs, outs, sems):
            cp.start()

    def finish(self, srcs, outs, sems):
        for cp in self._copies(srcs, outs, sems):
            cp.wait()


class _ChipScatter:
    N_BUFS = 4
    peers = ("x", "y")

    def __init__(self, pieces):
        self.kinds = [_SCATTERED[kind] for kind, _, _ in pieces]
        self.n = n = len(pieces)
        self.args = [g for _, g, _ in pieces] + [s for _, _, s in pieces]
        self.out_shape = [jax.ShapeDtypeStruct((2, *block), BF16) for block, _ in self.kinds]
        self.scratch = []
        for block, _ in self.kinds:
            self.scratch += [pltpu.VMEM((N_CHIPS, *block), BF16)] * 3 + [pltpu.VMEM((2, *block), BF16)]
        dma = pltpu.SemaphoreType.DMA
        self.scratch += [dma((n, N_CHIPS + 1)), dma((n, 2)), dma((n, 2)), dma((n,)), dma((n,)), dma((n,))]

    def _plan(self, outs, scr):
        n = self.n
        first_send, first_recv, second_send, second_recv, keep_sems = scr[self.N_BUFS * n + 1:]
        x, y, c = _my_place()
        via = (x ^ (1 - c), y ^ c)
        onto = (x ^ c, y ^ (1 - c))
        index = lambda chip: 2 * chip[0] + chip[1]
        first, second, keep = [], [], []
        for t in range(n):
            total, inbox = scr[self.N_BUFS * t + 2], scr[self.N_BUFS * t + 3]
            for k, chip in enumerate((via, (1 - x, 1 - y))):
                first.append(pltpu.make_async_remote_copy(
                    src_ref=total.at[index(chip)], dst_ref=inbox.at[k], send_sem=first_send.at[t, k],
                    recv_sem=first_recv.at[t, k], device_id=(*via, c), device_id_type=MESH))
            second.append(pltpu.make_async_remote_copy(
                src_ref=total.at[index(onto)], dst_ref=outs[t].at[1], send_sem=second_send.at[t],
                recv_sem=second_recv.at[t], device_id=(*onto, c), device_id_type=MESH))
            keep.append(pltpu.make_async_copy(total.at[index((x, y))], outs[t].at[0], keep_sems.at[t]))
        return first, second, keep, index((x, y)), index(onto)

    def start(self, ins, outs, scr):
        n = self.n
        load_sems = scr[self.N_BUFS * n]
        c = lax.axis_index("c")
        loads = []
        for t, (_, block) in enumerate(self.kinds):
            mine, theirs = scr[self.N_BUFS * t], scr[self.N_BUFS * t + 1]
            loads += [pltpu.make_async_copy(block(ins[t], 2 * ch + c), mine.at[ch], load_sems.at[t, ch])
                      for ch in range(N_CHIPS)]
            loads.append(pltpu.make_async_copy(ins[n + t], theirs, load_sems.at[t, N_CHIPS]))
        for cp in loads:
            cp.start()
        for cp in loads:
            cp.wait()
        for t in range(n):
            mine, theirs, total = scr[self.N_BUFS * t:self.N_BUFS * t + 3]
            for ch in range(N_CHIPS):
                total[ch] = (mine[ch].astype(F32) + theirs[ch].astype(F32)).astype(BF16)
        for cp in self._plan(outs, scr)[0]:
            cp.start()

    def mid(self, ins, outs, scr):
        first, second, keep, me, onto = self._plan(outs, scr)
        for cp in first:
            cp.wait_recv()
        for t in range(self.n):
            total, inbox = scr[self.N_BUFS * t + 2], scr[self.N_BUFS * t + 3]
            for k, slot in enumerate((me, onto)):
                total[slot] = (total[slot].astype(F32) + inbox[k].astype(F32)).astype(BF16)
        for cp in second + keep:
            cp.start()

    def finish(self, ins, outs, scr):
        first, second, keep, _, _ = self._plan(outs, scr)
        for cp in first:
            cp.wait_send()
        for cp in second + keep:
            cp.wait()


class _Jobs:
    def __init__(self, *jobs):
        self.jobs = jobs
        together = {p for j in jobs for p in j.peers}
        self.peers = tuple(p for p in _PEER_SETS[0] if p in together)
        self.args = [a for j in jobs for a in j.args]
        self.out_shape = [o for j in jobs for o in j.out_shape]
        self.scratch = [s for j in jobs for s in j.scratch]

    def _split(self, refs, attr):
        at = 0
        for j in self.jobs:
            n = len(getattr(j, attr))
            yield refs[at:at + n]
            at += n

    def _each(self, ins, outs, scr):
        return zip(self.jobs, self._split(ins, "args"), self._split(outs, "out_shape"), self._split(scr, "scratch"))

    def start(self, ins, outs, scr):
        for j, i, o, s in self._each(ins, outs, scr):
            j.start(i, o, s)

    def mid(self, ins, outs, scr):
        for j, i, o, s in self._each(ins, outs, scr):
            if hasattr(j, "mid"):
                j.mid(i, o, s)

    def finish(self, ins, outs, scr):
        for j, i, o, s in self._each(ins, outs, scr):
            j.finish(i, o, s)

    def split_outputs(self, outs):
        return list(self._split(outs, "out_shape"))


def _adamw_math(w, g, m, v):
    m = ADAM_B1 * m + (1.0 - ADAM_B1) * g
    v = ADAM_B2 * v + (1.0 - ADAM_B2) * (g * g)
    m_hat = m / (1.0 - ADAM_B1 ** ADAM_STEP)
    v_hat = v / (1.0 - ADAM_B2 ** ADAM_STEP)
    delta = -ADAM_LR * (m_hat / (jnp.sqrt(v_hat) + ADAM_EPS) + ADAM_WD * w)
    return delta, m, v


def _adamw(name, w, m, v, landings, n_col_blocks=1, job=None):
    n_slots, r, c = landings[0].shape
    grid = (w.shape[0] // r, n_col_blocks)

    def body(w_ref, m_ref, v_ref, *rest):
        l_refs, (g_ref, d_ref, nm_ref, nv_ref) = rest[:len(landings)], rest[len(landings):]
        step = pl.program_id(0) * n_col_blocks + pl.program_id(1)
        for idx, l_ref in enumerate(l_refs):
            @pl.when(step == idx)
            def _(l_ref=l_ref):
                g = l_ref[0].astype(F32)
                for s in range(1, n_slots):
                    g = g + l_ref[s].astype(F32)
                g_ref[...] = g
                d_ref[...], nm_ref[...], nv_ref[...] = _adamw_math(w_ref[...], g, m_ref[...], v_ref[...])

    spec = pl.BlockSpec((r, c), lambda a, b: (a, b))
    return _launch(
        body, name=f"adamw_{name}", grid=grid,
        in_specs=[spec, spec, spec] + [_full_spec((n_slots, r, c))] * len(landings),
        out_specs=[spec] * 4, out_shape=[jax.ShapeDtypeStruct(w.shape, F32)] * 4,
        args=(w, m, v, *landings), vmem=VMEM_BIG, job=job)


_SMALL = (("pre_mix_g", SV_PRE_MIX, 2), ("post_mix_g", SV_POST_MIX, 2), ("pre_ffn_g", SV_PRE_FFN, 2),
          ("post_ffn_g", SV_POST_FFN, 2), ("ple_g", SV_PLE, 2), ("ple_post_g", SV_PLE_POST, 2), ("kv_g", SV_KV, 1),
          ("pool_scale", SV_POOL_SCALE, 1), ("sinks", SV_SINKS, 1))


def _small_all_reduce_adamw(part, params):
    flat = [a for name, _, _ in _SMALL for a in params[name]]
    n_in = 1 + len(flat)

    def body(*refs):
        part_ref, wmv = refs[0], refs[1:n_in]
        loss_ref, outs = refs[n_in], refs[n_in + 1:n_in + 1 + 4 * len(_SMALL)]
        buf, total, send_sems, recv_sems = refs[n_in + 1 + 4 * len(_SMALL):]
        x, y, c = _my_place()
        me = _dev_index(x, y, c)
        buf[me] = part_ref[...]
        copies = [pltpu.make_async_remote_copy(
            src_ref=buf.at[me], dst_ref=buf.at[me], send_sem=send_sems.at[r - 1], recv_sem=recv_sems.at[r - 1],
            device_id=_peer_by_relation(r), device_id_type=MESH) for r in range(1, N_DEV)]
        for cp in copies:
            cp.start()
        for cp in copies:
            cp.wait()
        g = buf[0]
        for s in range(1, N_DEV):
            g = g + buf[s]
        total[...] = g
        loss_ref[...] = total[SV_LOSS:SV_LOSS + 1, 0:1]
        for idx, (name, row, n_rows) in enumerate(_SMALL):
            w_ref, m_ref, v_ref = wmv[3 * idx:3 * idx + 3]
            g_ref, d_ref, nm_ref, nv_ref = outs[4 * idx:4 * idx + 4]
            if name == "pool_scale":
                g = total[row:row + 1, pl.ds(pl.multiple_of(me * 128, 128), 128)]
            else:
                g = total[row:row + n_rows, 0:w_ref.shape[1]]
            g_ref[...] = g
            d_ref[...], nm_ref[...], nv_ref[...] = _adamw_math(w_ref[...], g, m_ref[...], v_ref[...])

    out_shape = [jax.ShapeDtypeStruct((1, 1), F32)]
    for name, _, _ in _SMALL:
        out_shape += [jax.ShapeDtypeStruct(params[name][0].shape, F32)] * 4
    res, _ = _launch(
        body, name="small_all_reduce_adamw", grid=(1,),
        in_specs=[_full_spec(a.shape) for a in (part, *flat)], out_specs=[_full_spec(s.shape) for s in out_shape],
        out_shape=out_shape,
        scratch_shapes=[pltpu.VMEM((N_DEV, SV_ROWS, D_MODEL), F32), pltpu.VMEM((SV_ROWS, D_MODEL), F32),
                        pltpu.SemaphoreType.DMA((N_DEV - 1,)), pltpu.SemaphoreType.DMA((N_DEV - 1,))],
        args=(part, *flat))
    return res[0], {name: res[1 + 4 * idx:5 + 4 * idx] for idx, (name, _, _) in enumerate(_SMALL)}


def _local_step(x, p, tgt, gains, sinks, shards, weights):
    row = lambda first_row, layer: _Gain(gains, first_row + layer)
    gather = lambda *names: _AllGather(names, shards)
    g_pre_mix, g_post_mix, g_pre_ffn, g_post_ffn = SV_PRE_MIX, SV_POST_MIX, SV_PRE_FFN, SV_POST_FFN
    g_ple, g_ple_post, g_kv = SV_PLE, SV_PLE_POST, _Gain(gains, SV_KV)

    wp, scale, wgu0 = _all_gather_only("gather_first", ("pool", "scale", "gu0"), shards)
    wgu0 = [wgu0]
    (x1_0, h2_0, yraw, dpool), wd0 = _fwd_pool_mixer(
        x, row(g_pre_mix, 0), wp, scale, row(g_post_mix, 0), row(g_pre_ffn, 0), job=gather("wd0"))
    (gs0, us0, f0, x2_0, h3_0), (wgate0, wproj0, wkv, wq, wgu1_a) = _fwd_ffn(
        0, h2_0, x1_0, wgu0, wd0, row(g_post_ffn, 0), row(g_ple, 0),
        job=gather("gate0", "proj0", "kv", "q", "guh1_0"))
    (x3_0, z0, pe0), (wo,) = _fwd_ple(0, x2_0, h3_0, p[0], wgate0, wproj0, row(g_ple_post, 0), job=gather("o"))
    (hk, h1, q, kv), _ = _fwd_qkv(x3_0, g_kv, row(g_pre_mix, 1), wkv, wq)
    front = ((ATT_BLOCK, 0), (0, 0))
    kpad = jnp.pad(kv[:, :KV_DIM], front)
    vpad = jnp.pad(kv[:, KV_DIM:], front)
    (attn,), (wgu1_b, wd1_a) = _fwd_attention(q, kpad, vpad, sinks, job=gather("guh1_1", "wdh1_0"))
    (y1, x1_1, h2_1), (wd1_b,) = _fwd_attn_out(attn, x3_0, wo, row(g_post_mix, 1), row(g_pre_ffn, 1),
                                               job=gather("wdh1_1"))
    wgu1, wd1 = [wgu1_a, wgu1_b], [wd1_a, wd1_b]
    (gs1, us1, f1, x2_1, h3_1), (wgate1, wproj1) = _fwd_ffn(
        1, h2_1, x1_1, wgu1, wd1, row(g_post_ffn, 1), row(g_ple, 1), job=gather("gate1", "proj1"))

    produced, swapped, landed = {}, {}, {}

    def kind_of(name):
        return name.rstrip("0123_")

    def carry(swap=(), spread=()):
        jobs = []
        if swap:
            jobs.append(_SiblingSwap([(kind_of(n), produced[n]) for n in swap]))
        if spread:
            jobs.append(_ChipScatter([(kind_of(n), produced[n], swapped[n]) for n in spread]))
        return _Jobs(*jobs)

    def carried(jobs, outs, swap=(), spread=()):
        parts = jobs.split_outputs(outs)
        if swap:
            swapped.update(zip(swap, parts[0]))
        if spread:
            landed.update(zip(spread, parts[-1]))

    def hosted(call, *args, swap=(), spread=()):
        jobs = carry(swap, spread)
        outs, job_outs = call(*args, job=jobs)
        carried(jobs, job_outs, swap, spread)
        return outs

    def ffn_weight_grads(layer, h2, df, dg, du, a, hosts):
        for qtr in range(FF_PARTS):
            dgu, dwd = hosted(_bwd_ffn_dw, layer, qtr, h2, df, dg, du, a, **hosts[qtr])
            produced[f"gu{layer}_{qtr}"], produced[f"wd{layer}_{qtr}"] = dgu, dwd

    ffn_q = lambda layer, qtr: (f"gu{layer}_{qtr}", f"wd{layer}_{qtr}")

    dx2_1, df1, produced["gate1"], produced["proj1"], dg_ple_post1, dg_ple1, dg_post_ffn1, loss = hosted(
        _ple_loss_bwd, 1, x2_1, h3_1, p[1], f1, tgt, wgate1, wproj1, row(g_ple_post, 1), row(g_ple, 1),
        row(g_post_ffn, 1))
    dh2_1, dg1, du1, a1 = hosted(_bwd_ffn_act, 1, df1, gs1, us1, wgu1, wd1, swap=("gate1", "proj1"))
    ffn_weight_grads(1, h2_1, df1, dg1, du1, a1, [dict(spread=("gate1", "proj1")), dict(swap=ffn_q(1, 0))])
    dx1_1, dattn, produced["o"], dg_pre_ffn1, dg_post_mix1 = hosted(
        _bwd_attn_out, dx2_1, dh2_1, x1_1, y1, attn, wo, row(g_pre_ffn, 1), row(g_post_mix, 1), swap=ffn_q(1, 1))
    dq, dkpad, dvpad, dsinks = hosted(_bwd_attention, q, dattn, kpad, vpad, sinks, spread=ffn_q(1, 0))
    dkv = jnp.concatenate([dkpad[ATT_BLOCK:], dvpad[ATT_BLOCK:]], axis=1).astype(BF16)
    dx3_0, produced["q"], produced["kv"], dg_pre_mix1, dg_kv = hosted(
        _bwd_qkv, dx1_1, dq, dkv, x3_0, h1, hk, wq, wkv, row(g_pre_mix, 1), g_kv, swap=("o",))
    dx2_0, df0, produced["gate0"], produced["proj0"], dg_ple_post0, dg_ple0, dg_post_ffn0 = hosted(
        _bwd_ple, 0, dx3_0, x2_0, z0, pe0, h3_0, p[0], f0, wgate0, row(g_ple_post, 0), row(g_ple, 0),
        row(g_post_ffn, 0), swap=("q", "kv"), spread=("gu1_1",))
    dh2_0, dg0, du0, a0 = hosted(_bwd_ffn_act, 0, df0, gs0, us0, wgu0, wd0,
                                 swap=("gate0", "proj0"), spread=("wd1_1", "o"))
    ffn_weight_grads(0, h2_0, df0, dg0, du0, a0, [
        dict(spread=("gate0", "proj0", "q", "kv")), dict(swap=ffn_q(0, 0))])
    grad_x, produced["pool"], dscale, dg_pre_ffn0, dg_post_mix0, dg_pre_mix0 = hosted(
        _bwd_pool_mixer, dx2_0, dh2_0, x1_0, x, yraw, dpool, wp, scale, row(g_pre_ffn, 0), row(g_post_mix, 0),
        row(g_pre_mix, 0), swap=ffn_q(0, 1), spread=ffn_q(0, 0))

    def update(name, n_col_blocks=1, pieces=None, swap=(), spread=()):
        w, m, v = weights[name]
        rows = w.size // w.shape[-1]
        flat = [landed[n].reshape(landed[n].shape[0], -1, landed[n].shape[-1])
                for n in (pieces or [kind_short[name]])]
        outs = hosted(_adamw, name, w.reshape(rows, -1), m.reshape(rows, -1), v.reshape(rows, -1), flat,
                      n_col_blocks, swap=swap, spread=spread)
        return [o.reshape(w.shape) for o in outs]

    kind_short = {"w_q": "q", "w_kv": "kv", "w_o": "o", "pool_w": "pool"}
    upd = {}
    upd["w_ple_gate"] = update("w_ple_gate", pieces=("gate0", "gate1"), swap=("pool",), spread=ffn_q(0, 1))
    upd["w_ple_proj"] = update("w_ple_proj", pieces=("proj0", "proj1"), spread=("pool",))
    for name in ("w_q", "w_kv", "w_o", "pool_w"):
        upd[name] = update(name)
    upd["w_gu"] = update("w_gu", FF_PARTS,
                         pieces=[f"gu{layer}_{qtr}" for layer in range(2) for qtr in range(FF_PARTS)])
    upd["w_gu"] = [jnp.swapaxes(a, 1, 2) for a in upd["w_gu"]]
    upd["w_down"] = update("w_down", FF_PARTS,
                           pieces=[f"wd{layer}_{qtr}" for layer in range(2) for qtr in range(FF_PARTS)])

    lanes = lambda a: jnp.pad(a, ((0, 0), (0, D_MODEL - a.shape[1])))
    small = jnp.concatenate([
        dg_pre_mix0, dg_pre_mix1, dg_post_mix0, dg_post_mix1, dg_pre_ffn0, dg_pre_ffn1, dg_post_ffn0, dg_post_ffn1,
        dg_ple0, dg_ple1, dg_ple_post0, dg_ple_post1, dg_kv, dscale, lanes(dsinks[:, :N_HEADS]), lanes(loss)], axis=0)
    return grad_x, upd, small


def kernel(x, p, pre_mix_g, post_mix_g, pre_ffn_g, post_ffn_g, pool_w, pool_scale, kv_g, w_kv, w_q, sinks, w_o, w_gu, w_down, ple_g, w_ple_gate, w_ple_proj, ple_post_g, loss_target, m_pre_mix_g, m_post_mix_g, m_pre_ffn_g, m_post_ffn_g, m_pool_w, m_pool_scale, m_kv_g, m_w_kv, m_w_q, m_sinks, m_w_o, m_w_gu, m_w_down, m_ple_g, m_w_ple_gate, m_w_ple_proj, m_ple_post_g, v_pre_mix_g, v_post_mix_g, v_pre_ffn_g, v_post_ffn_g, v_pool_w, v_pool_scale, v_kv_g, v_w_kv, v_w_q, v_sinks, v_w_o, v_w_gu, v_w_down, v_ple_g, v_w_ple_gate, v_w_ple_proj, v_ple_post_g):
    shards = {"pool": pool_w[0].astype(BF16), "scale": pool_scale, "kv": w_kv.astype(BF16),
              "q": w_q[0].astype(BF16), "o": w_o[0].astype(BF16)}
    for layer in range(2):
        shards[f"gu{layer}"] = w_gu[layer].T.astype(BF16)
        shards[f"wd{layer}"] = w_down[layer].astype(BF16)
        for half in range(2):
            cols = slice(half * D_MODEL // 2, (half + 1) * D_MODEL // 2)
            shards[f"guh{layer}_{half}"] = shards[f"gu{layer}"][:, cols]
            shards[f"wdh{layer}_{half}"] = shards[f"wd{layer}"][:, cols]
        shards[f"gate{layer}"] = w_ple_gate[layer].astype(BF16)
        shards[f"proj{layer}"] = w_ple_proj[layer].astype(BF16)
    gains = jnp.concatenate([pre_mix_g, post_mix_g, pre_ffn_g, post_ffn_g, ple_g, ple_post_g, kv_g[None, :]],
                            axis=0).reshape(-1, 1, D_MODEL)
    weights = {"pool_w": (pool_w, m_pool_w, v_pool_w), "w_kv": (w_kv, m_w_kv, v_w_kv), "w_q": (w_q, m_w_q, v_w_q),
               "w_o": (w_o, m_w_o, v_w_o), "w_down": (w_down, m_w_down, v_w_down),
               "w_gu": tuple(jnp.swapaxes(a, 1, 2) for a in (w_gu, m_w_gu, v_w_gu)),
               "w_ple_gate": (w_ple_gate, m_w_ple_gate, v_w_ple_gate),
               "w_ple_proj": (w_ple_proj, m_w_ple_proj, v_w_ple_proj)}
    grad_x, upd, small = _local_step(x[0], p[:, 0], loss_target[0], gains, sinks, shards, weights)

    small_params = {
        "pre_mix_g": (pre_mix_g, m_pre_mix_g, v_pre_mix_g), "post_mix_g": (post_mix_g, m_post_mix_g, v_post_mix_g),
        "pre_ffn_g": (pre_ffn_g, m_pre_ffn_g, v_pre_ffn_g), "post_ffn_g": (post_ffn_g, m_post_ffn_g, v_post_ffn_g),
        "ple_g": (ple_g, m_ple_g, v_ple_g), "ple_post_g": (ple_post_g, m_ple_post_g, v_ple_post_g),
        "kv_g": (kv_g[None, :], m_kv_g[None, :], v_kv_g[None, :]),
        "pool_scale": (pool_scale, m_pool_scale, v_pool_scale), "sinks": (sinks, m_sinks, v_sinks)}
    loss, small_upd = _small_all_reduce_adamw(small, small_params)
    small_upd["kv_g"] = [a[0] for a in small_upd["kv_g"]]
    upd.update(small_upd)

    names = ["pre_mix_g", "post_mix_g", "pre_ffn_g", "post_ffn_g", "pool_w", "pool_scale", "kv_g", "w_kv", "w_q",
             "sinks", "w_o", "w_gu", "w_down", "ple_g", "w_ple_gate", "w_ple_proj", "ple_post_g"]
    outs = [loss[0, 0], grad_x[None]]
    for kind in range(4):
        outs += [upd[n][kind] for n in names]
    return tuple(outs)
```

```python
import functools
import types

import jax
import jax.numpy as jnp
from jax import lax
from jax.experimental import pallas as pl
from jax.experimental.pallas import tpu as pltpu

F32 = jnp.float32
BF16 = jnp.bfloat16

N_DEV = 8
D_MODEL = 1024
N_POOL_GROUPS = 4
POOL_GROUP = 256
POOL_HALO = 16
HEAD_DIM = 64
N_HEADS = 16
N_KV_HEADS = 4
GQA_GROUP = 4
KV_DIM = N_KV_HEADS * HEAD_DIM
ATT_BLOCK = 128
D_FF = 2816
FF_CHUNKS = 4
FF_BLOCK = D_FF // FF_CHUNKS
WD_ROWS = D_FF // N_DEV
FF_PARTS = 2
FF_PART = D_MODEL // FF_PARTS
N_CHIPS = 4
PLE_DIM = 256
EPS = 1e-6
NEG_INF = -1e30
ATT_SCALE = HEAD_DIM ** -0.5

ADAM_LR = 0.001
ADAM_B1 = 0.9
ADAM_B2 = 0.999
ADAM_EPS = 1e-08
ADAM_WD = 0.01
ADAM_STEP = 10

ROW_TILE = 512
FFN_ROW_TILE = 512
FFN_SUB_TILES = 2
VMEM_BIG = 60 * 1024 * 1024
VMEM_MID = 56 * 1024 * 1024
HBM_PIN_ELEMS = 1024

SV_ROWS = 16
SV_PRE_MIX, SV_POST_MIX, SV_PRE_FFN, SV_POST_FFN, SV_PLE, SV_PLE_POST = 0, 2, 4, 6, 8, 10
SV_KV, SV_POOL_SCALE, SV_SINKS, SV_LOSS = 12, 13, 14, 15

MESH = pl.DeviceIdType.MESH
ANY = pl.BlockSpec(memory_space=pl.ANY)


def _dot(a, b):
    return jnp.dot(a, b, preferred_element_type=F32)


def _dot_nt(a, b):
    return lax.dot_general(a, b, (((1,), (1,)), ((), ())), preferred_element_type=F32)


def _dot_tn(a, b):
    return lax.dot_general(a, b, (((0,), (0,)), ((), ())), preferred_element_type=F32)


def _rstd(x):
    return lax.rsqrt(jnp.mean(x * x, axis=-1, keepdims=True) + EPS)


def _rms(x, g):
    return x * _rstd(x) * g


def _rms_bwd(x, g, dy):
    r = _rstd(x)
    n = x * r
    dn = dy * g
    dx = r * (dn - n * jnp.mean(dn * n, axis=-1, keepdims=True))
    dg = jnp.sum(dy * n, axis=0, keepdims=True)
    return dx, dg


def _add_all(terms):
    return functools.reduce(jnp.add, terms)


def _sigmoid(x):
    return 1.0 / (1.0 + jnp.exp(-x))


def _acc(ref, val, first):
    @pl.when(first)
    def _():
        ref[...] = val

    @pl.when(jnp.logical_not(first))
    def _():
        ref[...] += val


def _pool_counts(row0, rows):
    t = row0 + lax.broadcasted_iota(jnp.int32, (rows, D_MODEL), 0) + 1
    grp = lax.broadcasted_iota(jnp.int32, (rows, D_MODEL), 1) // POOL_GROUP
    win = jnp.left_shift(2, grp)
    return jnp.minimum(t, win).astype(F32)


def _window_sums(ext, shift_of):
    outs = []
    s = ext
    for gi in range(N_POOL_GROUPS):
        s = s + pltpu.roll(s, shift_of(1 << gi), axis=0)
        outs.append(s[:, :POOL_GROUP])
        s = s[:, POOL_GROUP:]
    return jnp.concatenate(outs, axis=1)


def _cparams(n_axes, vmem, collective_id=None):
    return pltpu.CompilerParams(dimension_semantics=("arbitrary",) * n_axes, vmem_limit_bytes=vmem,
                                collective_id=collective_id)


_PEER_SETS = (("sibling", "x", "y"), ("sibling",), ("x", "y"))


def _meet(peers):
    x, y, c = lax.axis_index("x"), lax.axis_index("y"), lax.axis_index("c")
    device = {"sibling": (x, y, 1 - c), "x": (1 - x, y, c), "y": (x, 1 - y, c)}
    barrier = pltpu.get_barrier_semaphore()
    for peer in peers:
        pl.semaphore_signal(barrier, inc=1, device_id=device[peer], device_id_type=pl.DeviceIdType.MESH)
    pl.semaphore_wait(barrier, len(peers))


def _row_spec(cols, tm=ROW_TILE):
    return pl.BlockSpec((tm, cols), lambda i: (i, 0))


def _full_spec(shape):
    zeros = (0,) * len(shape)
    return pl.BlockSpec(shape, lambda *_: zeros)


def _vec_spec():
    return _full_spec((1, D_MODEL))


def _column_ranges(parts):
    ends = [0]
    for part in parts:
        ends.append(ends[-1] + part.shape[-1])
    return list(zip(ends[:-1], ends[1:]))


class _Gain:
    def __init__(self, stacked, layer):
        self.stacked, self.layer = stacked, layer

    def spec(self):
        layer = self.layer
        return pl.BlockSpec((None, 1, D_MODEL), lambda *_: (layer, 0, 0))


def _in_hbm(a):
    return pltpu.with_memory_space_constraint(a, pltpu.HBM) if a.size >= HBM_PIN_ELEMS else a


def _out_in_hbm(s):
    return pltpu.HBM(s.shape, s.dtype) if s.size >= HBM_PIN_ELEMS else s


def _launch(body, *, name, grid, in_specs, out_specs, out_shape, args, scratch_shapes=(), vmem=VMEM_MID, job=None):
    in_specs = [a.spec() if isinstance(a, _Gain) else s for s, a in zip(in_specs, args)]
    args = [_in_hbm(a.stacked if isinstance(a, _Gain) else a) for a in args]
    n_in, n_out, n_scr = len(args), len(out_shape), len(scratch_shapes)
    if job is not None and not job.args:
        job = None
    j_args, j_out, j_scr = ([], [], []) if job is None else ([_in_hbm(a) for a in job.args], job.out_shape, job.scratch)

    def run(*refs):
        groups, at = [], 0
        for n in (n_in, len(j_args), n_out, len(j_out), n_scr, len(j_scr)):
            groups.append(refs[at:at + n])
            at += n
        ins, j_ins, outs, j_outs, scr, j_sems = groups

        def begin():
            _meet(job.peers)
            job.start(j_ins, j_outs, j_sems)

        if job is None:
            body(*ins, *outs, *scr)
        elif not grid:
            begin()
            job.mid(j_ins, j_outs, j_sems)
            body(*ins, *outs, *scr)
            job.finish(j_ins, j_outs, j_sems)
        else:
            ids = [pl.program_id(a) for a in range(len(grid))]
            first = functools.reduce(jnp.logical_and, [i == 0 for i in ids])
            half = functools.reduce(jnp.logical_and, [ids[0] == grid[0] // 2] + [i == 0 for i in ids[1:]])
            last = functools.reduce(jnp.logical_and, [i == g - 1 for i, g in zip(ids, grid)])
            pl.when(first)(begin)
            pl.when(half)(lambda: job.mid(j_ins, j_outs, j_sems))
            body(*ins, *outs, *scr)
            pl.when(last)(lambda: job.finish(j_ins, j_outs, j_sems))

    res = pl.pallas_call(
        run, name=name, grid=grid,
        in_specs=list(in_specs) + [ANY] * len(j_args), out_specs=list(out_specs) + [ANY] * len(j_out),
        out_shape=[_out_in_hbm(s) for s in list(out_shape) + list(j_out)],
        scratch_shapes=list(scratch_shapes) + list(j_scr),
        compiler_params=_cparams(len(grid), vmem, None if job is None else _PEER_SETS.index(job.peers)),
    )(*args, *j_args)
    return res[:n_out], res[n_out:]


def _fwd_pool_mixer(x, g_pre, wp, scale, g_post, g_ffn, job=None):
    T = x.shape[0]
    tm = ROW_TILE
    nt = T // tm

    def body(x_ref, gpre_ref, wp_ref, sc_ref, gpost_ref, gffn_ref, x1_ref, h2_ref, yraw_ref, d_ref, carry):
        i = pl.program_id(0)

        @pl.when(i == 0)
        def _():
            carry[...] = jnp.zeros_like(carry)

        xv = x_ref[...]
        h = _rms(xv, gpre_ref[...])
        ext = jnp.concatenate([carry[...], h], axis=0)
        carry[...] = h[tm - POOL_HALO:, :]
        sums = _window_sums(ext, lambda k: k)[POOL_HALO:, :]
        d = sums / _pool_counts(i * tm, tm) - h
        db = d.astype(BF16)
        d_ref[...] = db
        yraw = jnp.concatenate(
            [_dot(db[:, g * POOL_GROUP:(g + 1) * POOL_GROUP], wp_ref[g]) for g in range(N_POOL_GROUPS)], axis=1)
        yraw_ref[...] = yraw.astype(BF16)
        x1 = xv + _rms(yraw * sc_ref[...], gpost_ref[...])
        x1_ref[...] = x1
        h2_ref[...] = _rms(x1, gffn_ref[...]).astype(BF16)

    return _launch(
        body, name="fwd_pool_mixer", grid=(nt,),
        in_specs=[_row_spec(D_MODEL), _vec_spec(), _full_spec((N_POOL_GROUPS, POOL_GROUP, POOL_GROUP)), _vec_spec(),
                  _vec_spec(), _vec_spec()],
        out_specs=[_row_spec(D_MODEL)] * 4,
        out_shape=[jax.ShapeDtypeStruct((T, D_MODEL), F32)] + [jax.ShapeDtypeStruct((T, D_MODEL), BF16)] * 3,
        scratch_shapes=[pltpu.VMEM((POOL_HALO, D_MODEL), F32)],
        args=(x, g_pre, wp, scale, g_post, g_ffn), job=job)


def _fwd_ffn(layer, h2, x1, wgu, wd, g_post, g_ple, job=None):
    T = h2.shape[0]
    tm = min(FFN_ROW_TILE, T)
    nt = T // tm
    sub = tm // FFN_SUB_TILES
    last = FF_CHUNKS - 1
    n_gu, n_wd = len(wgu), len(wd)
    gu_cols = _column_ranges(wgu)

    def body(h2_ref, x1_ref, *refs):
        wgu_refs, wd_refs = refs[:n_gu], refs[n_gu:n_gu + n_wd]
        gpost_ref, gple_ref, gs_ref, us_ref, f_ref, x2_ref, h3_ref, acc = refs[n_gu + n_wd:]
        k = pl.program_id(0)
        i = pl.program_id(1)
        rows = pl.ds(pl.multiple_of(i * tm, tm), tm)
        parts = []
        for s in range(FFN_SUB_TILES):
            r = pl.ds(s * sub, sub)
            g = _add_all([_dot_nt(h2_ref[r, c0:c1], w[0]) for (c0, c1), w in zip(gu_cols, wgu_refs)])
            u = _add_all([_dot_nt(h2_ref[r, c0:c1], w[1]) for (c0, c1), w in zip(gu_cols, wgu_refs)])
            gs_ref[r, :] = g.astype(BF16)
            us_ref[r, :] = u.astype(BF16)
            a = (g * _sigmoid(g) * u).astype(BF16)
            parts.append(jnp.concatenate([_dot(a, w[...]) for w in wd_refs], axis=1))
        part = jnp.concatenate(parts, axis=0)

        @pl.when(k == 0)
        def _():
            acc[rows, :] = part

        @pl.when(jnp.logical_and(k > 0, k < last))
        def _():
            acc[rows, :] += part

        @pl.when(k == last)
        def _():
            f = acc[rows, :] + part
            f_ref[...] = f.astype(BF16)
            x2 = x1_ref[...] + _rms(f, gpost_ref[...])
            x2_ref[...] = x2
            h3_ref[...] = _rms(x2, gple_ref[...]).astype(BF16)

    def late(k, i):
        return (jnp.where(k == last, i, 0), 0)

    return _launch(
        body, name=f"fwd_ffn{layer}", grid=(FF_CHUNKS, nt),
        in_specs=[pl.BlockSpec((tm, D_MODEL), lambda k, i: (i, 0)), pl.BlockSpec((tm, D_MODEL), late)]
                 + [pl.BlockSpec((None, 2, FF_BLOCK, w.shape[-1]), lambda k, i: (k, 0, 0, 0)) for w in wgu]
                 + [pl.BlockSpec((FF_BLOCK, w.shape[-1]), lambda k, i: (k, 0)) for w in wd]
                 + [pl.BlockSpec((1, D_MODEL), lambda k, i: (0, 0))] * 2,
        out_specs=[pl.BlockSpec((None, tm, FF_BLOCK), lambda k, i: (k, i, 0)),
                   pl.BlockSpec((None, tm, FF_BLOCK), lambda k, i: (k, i, 0)),
                   pl.BlockSpec((tm, D_MODEL), late),
                   pl.BlockSpec((tm, D_MODEL), late),
                   pl.BlockSpec((tm, D_MODEL), late)],
        out_shape=[jax.ShapeDtypeStruct((FF_CHUNKS, T, FF_BLOCK), BF16),
                   jax.ShapeDtypeStruct((FF_CHUNKS, T, FF_BLOCK), BF16),
                   jax.ShapeDtypeStruct((T, D_MODEL), BF16),
                   jax.ShapeDtypeStruct((T, D_MODEL), F32),
                   jax.ShapeDtypeStruct((T, D_MODEL), BF16)],
        scratch_shapes=[pltpu.VMEM((T, D_MODEL), F32)],
        args=(h2, x1, *wgu, *wd, g_post, g_ple), vmem=VMEM_BIG, job=job)


def _fwd_ple(layer, x2, h3, p, wgate, wproj, g_post, job=None):
    T = x2.shape[0]
    nt = T // ROW_TILE

    def body(x2_ref, h3_ref, p_ref, wg_ref, wp_ref, gpost_ref, x3_ref, z_ref, pe_ref):
        z = _dot(h3_ref[...], wg_ref[...])
        pe = _dot(p_ref[...].astype(BF16), wp_ref[...])
        z_ref[...] = z.astype(BF16)
        pe_ref[...] = pe.astype(BF16)
        x3_ref[...] = x2_ref[...] + _rms(pe * _sigmoid(z), gpost_ref[...])

    return _launch(
        body, name=f"fwd_ple{layer}", grid=(nt,),
        in_specs=[_row_spec(D_MODEL), _row_spec(D_MODEL), _row_spec(PLE_DIM), _full_spec((D_MODEL, D_MODEL)),
                  _full_spec((PLE_DIM, D_MODEL)), _vec_spec()],
        out_specs=[_row_spec(D_MODEL)] * 3,
        out_shape=[jax.ShapeDtypeStruct((T, D_MODEL), F32)] + [jax.ShapeDtypeStruct((T, D_MODEL), BF16)] * 2,
        args=(x2, h3, p, wgate, wproj, g_post), job=job)


def _fwd_qkv(x3, g_kv, g_mix, wkv, wq, job=None):
    T = x3.shape[0]
    nt = T // ROW_TILE

    def body(x_ref, gkv_ref, gmix_ref, wkv_ref, wq_ref, hk_ref, h1_ref, q_ref, kv_ref):
        xv = x_ref[...]
        r = _rstd(xv)
        hk = (xv * r * gkv_ref[...]).astype(BF16)
        h1 = (xv * r * gmix_ref[...]).astype(BF16)
        hk_ref[...] = hk
        h1_ref[...] = h1
        kv_ref[...] = _dot(hk, wkv_ref[...]).astype(BF16)
        q_ref[...] = _dot(h1, wq_ref[...]).astype(BF16)

    return _launch(
        body, name="fwd_qkv", grid=(nt,),
        in_specs=[_row_spec(D_MODEL), _vec_spec(), _vec_spec(), _full_spec((D_MODEL, 2 * KV_DIM)),
                  _full_spec((D_MODEL, D_MODEL))],
        out_specs=[_row_spec(D_MODEL), _row_spec(D_MODEL), _row_spec(D_MODEL), _row_spec(2 * KV_DIM)],
        out_shape=[jax.ShapeDtypeStruct((T, D_MODEL), BF16)] * 3 + [jax.ShapeDtypeStruct((T, 2 * KV_DIM), BF16)],
        args=(x3, g_kv, g_mix, wkv, wq), job=job)


def _alibi_slope(h):
    return 2.0 ** (-8.0 * (h + 1) / N_HEADS)


ATT_SUB = 32
ATT_GROUP_ROWS = GQA_GROUP * ATT_BLOCK


def _att_mask(n, row0):
    qi = lax.broadcasted_iota(jnp.int32, (ATT_SUB, 2 * ATT_BLOCK), 0) + row0
    si = lax.broadcasted_iota(jnp.int32, (ATT_SUB, 2 * ATT_BLOCK), 1)
    rel = ATT_BLOCK + qi - si
    valid = (rel >= 0) & (rel < ATT_BLOCK) & ((si >= ATT_BLOCK) | (n > 0))
    return rel.astype(F32), valid


def _att_probs(raw, relf, valid, slope, sink):
    s = jnp.where(valid, raw * ATT_SCALE - slope * relf, NEG_INF)
    m = jnp.maximum(jnp.max(s, axis=-1, keepdims=True), sink)
    e = jnp.exp(s - m)
    es = jnp.exp(sink - m)
    inv = 1.0 / (jnp.sum(e, axis=-1, keepdims=True) + es)
    return e * inv, es * inv


def _stack_heads(ref, kh):
    first = kh * GQA_GROUP
    return jnp.concatenate([ref[:, (first + g) * HEAD_DIM:(first + g + 1) * HEAD_DIM] for g in range(GQA_GROUP)], axis=0)


def _unstack_heads(stacked):
    return [stacked[g * ATT_BLOCK:(g + 1) * ATT_BLOCK, :] for g in range(GQA_GROUP)]


def _fwd_attention(q, kpad, vpad, sinks, job=None):
    T = q.shape[0]
    nb = T // ATT_BLOCK

    def body(q_ref, k_ref, v_ref, sink_ref, o_ref, s_scr, p_scr):
        n = pl.program_id(0)
        start = pl.multiple_of(n * ATT_BLOCK, ATT_BLOCK)
        kw = k_ref[pl.ds(start, 2 * ATT_BLOCK), :]
        vw = v_ref[pl.ds(start, 2 * ATT_BLOCK), :]
        outs = []
        for kh in range(N_KV_HEADS):
            kk = kw[:, kh * HEAD_DIM:(kh + 1) * HEAD_DIM]
            vv = vw[:, kh * HEAD_DIM:(kh + 1) * HEAD_DIM]
            s_scr[...] = _dot_nt(_stack_heads(q_ref, kh), kk)
            for g in range(GQA_GROUP):
                h = kh * GQA_GROUP + g
                for row0 in range(0, ATT_BLOCK, ATT_SUB):
                    rows = pl.ds(g * ATT_BLOCK + row0, ATT_SUB)
                    relf, valid = _att_mask(n, row0)
                    pr, _ = _att_probs(s_scr[rows, :], relf, valid, _alibi_slope(h), sink_ref[0, h])
                    p_scr[rows, :] = pr.astype(BF16)
            outs += _unstack_heads(_dot(p_scr[...], vv))
        o_ref[...] = jnp.concatenate(outs, axis=1).astype(BF16)

    return _launch(
        body, name="fwd_attention", grid=(nb,),
        in_specs=[_row_spec(D_MODEL, ATT_BLOCK), _full_spec((T + ATT_BLOCK, KV_DIM)), _full_spec((T + ATT_BLOCK, KV_DIM)),
                  pl.BlockSpec(memory_space=pltpu.SMEM)],
        out_specs=[_row_spec(D_MODEL, ATT_BLOCK)],
        out_shape=[jax.ShapeDtypeStruct((T, D_MODEL), BF16)],
        scratch_shapes=[pltpu.VMEM((ATT_GROUP_ROWS, 2 * ATT_BLOCK), F32), pltpu.VMEM((ATT_GROUP_ROWS, 2 * ATT_BLOCK), BF16)],
        args=(q, kpad, vpad, sinks), job=job)


def _fwd_attn_out(attn, x, wo, g_post, g_ffn, job=None):
    T = x.shape[0]
    nt = T // ROW_TILE

    def body(a_ref, x_ref, wo_ref, gpost_ref, gffn_ref, y_ref, x1_ref, h2_ref):
        y = _dot(a_ref[...], wo_ref[...])
        y_ref[...] = y.astype(BF16)
        x1 = x_ref[...] + _rms(y, gpost_ref[...])
        x1_ref[...] = x1
        h2_ref[...] = _rms(x1, gffn_ref[...]).astype(BF16)

    return _launch(
        body, name="fwd_attn_out", grid=(nt,),
        in_specs=[_row_spec(D_MODEL), _row_spec(D_MODEL), _full_spec((D_MODEL, D_MODEL)), _vec_spec(), _vec_spec()],
        out_specs=[_row_spec(D_MODEL)] * 3,
        out_shape=[jax.ShapeDtypeStruct((T, D_MODEL), BF16), jax.ShapeDtypeStruct((T, D_MODEL), F32),
                   jax.ShapeDtypeStruct((T, D_MODEL), BF16)],
        args=(attn, x, wo, g_post, g_ffn), job=job)


def _bwd_ple(layer, dx3, x2, z, pe, h3, p, f, wgate, g_ple_post, g_ple, g_post_ffn, job=None):
    T = x2.shape[0]
    tm = ROW_TILE
    nt = T // tm

    def body(dx3_ref, x2_ref, z_ref, pe_ref, h3_ref, p_ref, f_ref, wg_ref, gpp_ref, gp_ref, gpf_ref,
             dx2_ref, df_ref, dwg_ref, dwp_ref, dgpp_ref, dgp_ref, dgpf_ref, acc_g, acc_p):
        i = pl.program_id(0)
        first = i == 0
        dx3v = dx3_ref[...]
        gate = _sigmoid(z_ref[...].astype(F32))
        pev = pe_ref[...].astype(F32)
        de, dgpp = _rms_bwd(pev * gate, gpp_ref[...], dx3v)
        dpe = (de * gate).astype(BF16)
        dz = (de * pev * gate * (1.0 - gate)).astype(BF16)
        _acc(acc_p, _dot_tn(p_ref[...].astype(BF16), dpe), first)
        _acc(acc_g, _dot_tn(h3_ref[...], dz), first)
        dh3 = _dot_nt(dz, wg_ref[...])
        dxn, dgp = _rms_bwd(x2_ref[...], gp_ref[...], dh3)
        dx2 = dx3v + dxn
        dx2_ref[...] = dx2
        df, dgpf = _rms_bwd(f_ref[...].astype(F32), gpf_ref[...], dx2)
        df_ref[...] = df.astype(BF16)
        _acc(dgpp_ref, dgpp, first)
        _acc(dgp_ref, dgp, first)
        _acc(dgpf_ref, dgpf, first)

        @pl.when(i == nt - 1)
        def _():
            dwg_ref[...] = acc_g[...].astype(BF16)
            dwp_ref[...] = acc_p[...].astype(BF16)

    return _launch(
        body, name=f"bwd_ple{layer}", grid=(nt,),
        in_specs=[_row_spec(D_MODEL)] * 5 + [_row_spec(PLE_DIM), _row_spec(D_MODEL), _full_spec((D_MODEL, D_MODEL)),
                  _vec_spec(), _vec_spec(), _vec_spec()],
        out_specs=[_row_spec(D_MODEL), _row_spec(D_MODEL), _full_spec((D_MODEL, D_MODEL)), _full_spec((PLE_DIM, D_MODEL)),
                   _vec_spec(), _vec_spec(), _vec_spec()],
        out_shape=[jax.ShapeDtypeStruct((T, D_MODEL), F32), jax.ShapeDtypeStruct((T, D_MODEL), BF16),
                   jax.ShapeDtypeStruct((D_MODEL, D_MODEL), BF16), jax.ShapeDtypeStruct((PLE_DIM, D_MODEL), BF16)]
                  + [jax.ShapeDtypeStruct((1, D_MODEL), F32)] * 3,
        scratch_shapes=[pltpu.VMEM((D_MODEL, D_MODEL), F32), pltpu.VMEM((PLE_DIM, D_MODEL), F32)],
        args=(dx3, x2, z, pe, h3, p, f, wgate, g_ple_post, g_ple, g_post_ffn), vmem=VMEM_BIG, job=job)


def _ple_loss_bwd(layer, x2, h3, p, f, target, wgate, wproj, g_ple_post, g_ple, g_post_ffn, job=None):
    T = x2.shape[0]
    tm = ROW_TILE
    nt = T // tm

    def body(x2_ref, h3_ref, p_ref, f_ref, tgt_ref, wg_ref, wp_ref, gpp_ref, gp_ref, gpf_ref,
             dx2_ref, df_ref, dwg_ref, dwp_ref, dgpp_ref, dgp_ref, dgpf_ref, loss_ref, acc_g, acc_p):
        i = pl.program_id(0)
        first = i == 0
        h3 = h3_ref[...]
        pb = p_ref[...].astype(BF16)
        x2v = x2_ref[...]
        gate = _sigmoid(_dot(h3, wg_ref[...]))
        pev = _dot(pb, wp_ref[...])
        e = pev * gate
        err = x2v + _rms(e, gpp_ref[...]) - tgt_ref[...]
        _acc(loss_ref, 0.5 * jnp.sum(jnp.mean(err * err, axis=-1, keepdims=True), axis=0, keepdims=True), first)
        dx3v = err * (1.0 / D_MODEL)
        de, dgpp = _rms_bwd(e, gpp_ref[...], dx3v)
        dpe = (de * gate).astype(BF16)
        dz = (de * pev * gate * (1.0 - gate)).astype(BF16)
        _acc(acc_p, _dot_tn(pb, dpe), first)
        _acc(acc_g, _dot_tn(h3, dz), first)
        dxn, dgp = _rms_bwd(x2v, gp_ref[...], _dot_nt(dz, wg_ref[...]))
        dx2 = dx3v + dxn
        dx2_ref[...] = dx2
        df, dgpf = _rms_bwd(f_ref[...].astype(F32), gpf_ref[...], dx2)
        df_ref[...] = df.astype(BF16)
        _acc(dgpp_ref, dgpp, first)
        _acc(dgp_ref, dgp, first)
        _acc(dgpf_ref, dgpf, first)

        @pl.when(i == nt - 1)
        def _():
            dwg_ref[...] = acc_g[...].astype(BF16)
            dwp_ref[...] = acc_p[...].astype(BF16)

    return _launch(
        body, name=f"ple_loss_bwd{layer}", grid=(nt,),
        in_specs=[_row_spec(D_MODEL), _row_spec(D_MODEL), _row_spec(PLE_DIM), _row_spec(D_MODEL), _row_spec(D_MODEL),
                  _full_spec((D_MODEL, D_MODEL)), _full_spec((PLE_DIM, D_MODEL)), _vec_spec(), _vec_spec(), _vec_spec()],
        out_specs=[_row_spec(D_MODEL), _row_spec(D_MODEL), _full_spec((D_MODEL, D_MODEL)), _full_spec((PLE_DIM, D_MODEL)),
                   _vec_spec(), _vec_spec(), _vec_spec(), _full_spec((1, 1))],
        out_shape=[jax.ShapeDtypeStruct((T, D_MODEL), F32), jax.ShapeDtypeStruct((T, D_MODEL), BF16),
                   jax.ShapeDtypeStruct((D_MODEL, D_MODEL), BF16), jax.ShapeDtypeStruct((PLE_DIM, D_MODEL), BF16)]
                  + [jax.ShapeDtypeStruct((1, D_MODEL), F32)] * 3 + [jax.ShapeDtypeStruct((1, 1), F32)],
        scratch_shapes=[pltpu.VMEM((D_MODEL, D_MODEL), F32), pltpu.VMEM((PLE_DIM, D_MODEL), F32)],
        args=(x2, h3, p, f, target, wgate, wproj, g_ple_post, g_ple, g_post_ffn), vmem=VMEM_BIG, job=job)


def _bwd_ffn_act(layer, df, gs, us, wgu, wd, job=None):
    T = df.shape[0]
    tm = min(FFN_ROW_TILE, T)
    nt = T // tm
    sub = tm // FFN_SUB_TILES
    last = FF_CHUNKS - 1
    n_gu, n_wd = len(wgu), len(wd)
    wd_cols = _column_ranges(wd)

    def body(df_ref, gs_ref, us_ref, *refs):
        wgu_refs, wd_refs = refs[:n_gu], refs[n_gu:n_gu + n_wd]
        dh_ref, dg_ref, du_ref, a_ref, acc_h = refs[n_gu + n_wd:]
        k = pl.program_id(0)
        i = pl.program_id(1)
        rows = pl.ds(pl.multiple_of(i * tm, tm), tm)
        dhs = []
        for s in range(FFN_SUB_TILES):
            r = pl.ds(s * sub, sub)
            g = gs_ref[r, :].astype(F32)
            u = us_ref[r, :].astype(F32)
            sg = _sigmoid(g)
            silu = g * sg
            a_ref[r, :] = (silu * u).astype(BF16)
            da = _add_all([_dot_nt(df_ref[r, c0:c1], w[...]) for (c0, c1), w in zip(wd_cols, wd_refs)])
            dg = (da * u * (sg * (1.0 + g * (1.0 - sg)))).astype(BF16)
            du = (da * silu).astype(BF16)
            dg_ref[r, :] = dg
            du_ref[r, :] = du
            dhs.append(jnp.concatenate([_dot(dg, w[0]) + _dot(du, w[1]) for w in wgu_refs], axis=1))
        dh = jnp.concatenate(dhs, axis=0)

        @pl.when(k == 0)
        def _():
            acc_h[rows, :] = dh

        @pl.when(jnp.logical_and(k > 0, k < last))
        def _():
            acc_h[rows, :] += dh

        @pl.when(k == last)
        def _():
            dh_ref[...] = acc_h[rows, :] + dh

    chunk_rows = pl.BlockSpec((None, tm, FF_BLOCK), lambda k, i: (k, i, 0))
    saved = jax.ShapeDtypeStruct((FF_CHUNKS, T, FF_BLOCK), BF16)
    return _launch(
        body, name=f"bwd_ffn_act{layer}", grid=(FF_CHUNKS, nt),
        in_specs=[pl.BlockSpec((tm, D_MODEL), lambda k, i: (i, 0)), chunk_rows, chunk_rows]
                 + [pl.BlockSpec((None, 2, FF_BLOCK, w.shape[-1]), lambda k, i: (k, 0, 0, 0)) for w in wgu]
                 + [pl.BlockSpec((FF_BLOCK, w.shape[-1]), lambda k, i: (k, 0)) for w in wd],
        out_specs=[pl.BlockSpec((tm, D_MODEL), lambda k, i: (jnp.where(k == last, i, 0), 0)),
                   chunk_rows, chunk_rows, chunk_rows],
        out_shape=[jax.ShapeDtypeStruct((T, D_MODEL), F32), saved, saved, saved],
        scratch_shapes=[pltpu.VMEM((T, D_MODEL), F32)],
        args=(df, gs, us, *wgu, *wd), vmem=VMEM_BIG, job=job)


def _bwd_ffn_dw(layer, q, h2, df, dg, du, a, job=None):
    T = h2.shape[0]

    def body(h_ref, df_ref, dg_ref, du_ref, a_ref, dgu_ref, dwd_ref):
        h = h_ref[...]
        dgu_ref[0] = _dot_tn(dg_ref[...], h).astype(BF16)
        dgu_ref[1] = _dot_tn(du_ref[...], h).astype(BF16)
        dwd_ref[...] = _dot_tn(a_ref[...], df_ref[...]).astype(BF16)

    cols = pl.BlockSpec((T, FF_PART), lambda k: (0, q))
    chunk = pl.BlockSpec((None, T, FF_BLOCK), lambda k: (k, 0, 0))
    return _launch(
        body, name=f"bwd_ffn_dw{layer}_{q}", grid=(FF_CHUNKS,),
        in_specs=[cols, cols, chunk, chunk, chunk],
        out_specs=[pl.BlockSpec((None, 2, FF_BLOCK, FF_PART), lambda k: (k, 0, 0, 0)),
                   pl.BlockSpec((FF_BLOCK, FF_PART), lambda k: (k, 0))],
        out_shape=[jax.ShapeDtypeStruct((FF_CHUNKS, 2, FF_BLOCK, FF_PART), BF16),
                   jax.ShapeDtypeStruct((D_FF, FF_PART), BF16)],
        args=(h2, df, dg, du, a), vmem=VMEM_BIG, job=job)


def _bwd_attn_out(dx2, dh2, x1, y, attn, wo, g_ffn, g_post, job=None):
    T = x1.shape[0]
    nt = T // ROW_TILE

    def body(dx2_ref, dh2_ref, x1_ref, y_ref, a_ref, wo_ref, gffn_ref, gpost_ref,
             dx1_ref, da_ref, dwo_ref, dgf_ref, dgp_ref, acc):
        i = pl.program_id(0)
        first = i == 0
        dxn, dgf = _rms_bwd(x1_ref[...], gffn_ref[...], dh2_ref[...])
        dx1 = dx2_ref[...] + dxn
        dx1_ref[...] = dx1
        dy, dgp = _rms_bwd(y_ref[...].astype(F32), gpost_ref[...], dx1)
        dyb = dy.astype(BF16)
        da_ref[...] = _dot_nt(dyb, wo_ref[...]).astype(BF16)
        _acc(acc, _dot_tn(a_ref[...], dyb), first)
        _acc(dgf_ref, dgf, first)
        _acc(dgp_ref, dgp, first)

        @pl.when(i == nt - 1)
        def _():
            dwo_ref[...] = acc[...].astype(BF16)

    return _launch(
        body, name="bwd_attn_out", grid=(nt,),
        in_specs=[_row_spec(D_MODEL)] * 5 + [_full_spec((D_MODEL, D_MODEL)), _vec_spec(), _vec_spec()],
        out_specs=[_row_spec(D_MODEL), _row_spec(D_MODEL), _full_spec((D_MODEL, D_MODEL)), _vec_spec(), _vec_spec()],
        out_shape=[jax.ShapeDtypeStruct((T, D_MODEL), F32), jax.ShapeDtypeStruct((T, D_MODEL), BF16),
                   jax.ShapeDtypeStruct((D_MODEL, D_MODEL), BF16)] + [jax.ShapeDtypeStruct((1, D_MODEL), F32)] * 2,
        scratch_shapes=[pltpu.VMEM((D_MODEL, D_MODEL), F32)],
        args=(dx2, dh2, x1, y, attn, wo, g_ffn, g_post), job=job)


def _bwd_attention(q, dattn, kpad, vpad, sinks, job=None):
    T = q.shape[0]
    nb = T // ATT_BLOCK

    def body(q_ref, do_ref, k_ref, v_ref, sink_ref, dq_ref, dk_ref, dv_ref, ds_ref, s_scr, dp_scr, p_scr, dsb_scr):
        n = pl.program_id(0)

        @pl.when(n == 0)
        def _():
            dk_ref[...] = jnp.zeros_like(dk_ref)
            dv_ref[...] = jnp.zeros_like(dv_ref)
            ds_ref[...] = jnp.zeros_like(ds_ref)

        start = pl.multiple_of(n * ATT_BLOCK, ATT_BLOCK)
        win = pl.ds(start, 2 * ATT_BLOCK)
        kw = k_ref[win, :]
        vw = v_ref[win, :]
        lane = lax.broadcasted_iota(jnp.int32, (1, ATT_BLOCK), 1)
        dsink = jnp.zeros((1, ATT_BLOCK), F32)
        dqs, dks, dvs = [], [], []
        for kh in range(N_KV_HEADS):
            kk = kw[:, kh * HEAD_DIM:(kh + 1) * HEAD_DIM]
            vv = vw[:, kh * HEAD_DIM:(kh + 1) * HEAD_DIM]
            qs = _stack_heads(q_ref, kh)
            dos = _stack_heads(do_ref, kh)
            s_scr[...] = _dot_nt(qs, kk)
            dp_scr[...] = _dot_nt(dos, vv)
            for g in range(GQA_GROUP):
                h = kh * GQA_GROUP + g
                dsink_h = jnp.zeros((1, 1), F32)
                for row0 in range(0, ATT_BLOCK, ATT_SUB):
                    rows = pl.ds(g * ATT_BLOCK + row0, ATT_SUB)
                    relf, valid = _att_mask(n, row0)
                    pr, ps = _att_probs(s_scr[rows, :], relf, valid, _alibi_slope(h), sink_ref[0, h])
                    dp = dp_scr[rows, :]
                    delta = jnp.sum(pr * dp, axis=-1, keepdims=True)
                    dsb_scr[rows, :] = (pr * (dp - delta) * ATT_SCALE).astype(BF16)
                    p_scr[rows, :] = pr.astype(BF16)
                    dsink_h = dsink_h - jnp.sum(ps * delta, axis=0, keepdims=True)
                dsink = dsink + jnp.where(lane == h, dsink_h, 0.0)
            dsb = dsb_scr[...]
            dqs += _unstack_heads(_dot(dsb, kk))
            dks.append(_dot_tn(dsb, qs))
            dvs.append(_dot_tn(p_scr[...], dos))
        dq_ref[...] = jnp.concatenate(dqs, axis=1).astype(BF16)
        dk_ref[win, :] += jnp.concatenate(dks, axis=1)
        dv_ref[win, :] += jnp.concatenate(dvs, axis=1)
        ds_ref[...] += dsink

    return _launch(
        body, name="bwd_attention", grid=(nb,),
        in_specs=[_row_spec(D_MODEL, ATT_BLOCK), _row_spec(D_MODEL, ATT_BLOCK), _full_spec((T + ATT_BLOCK, KV_DIM)),
                  _full_spec((T + ATT_BLOCK, KV_DIM)), pl.BlockSpec(memory_space=pltpu.SMEM)],
        out_specs=[_row_spec(D_MODEL, ATT_BLOCK), _full_spec((T + ATT_BLOCK, KV_DIM)), _full_spec((T + ATT_BLOCK, KV_DIM)),
                   _full_spec((1, ATT_BLOCK))],
        out_shape=[jax.ShapeDtypeStruct((T, D_MODEL), BF16), jax.ShapeDtypeStruct((T + ATT_BLOCK, KV_DIM), F32),
                   jax.ShapeDtypeStruct((T + ATT_BLOCK, KV_DIM), F32), jax.ShapeDtypeStruct((1, ATT_BLOCK), F32)],
        scratch_shapes=[pltpu.VMEM((ATT_GROUP_ROWS, 2 * ATT_BLOCK), F32)] * 2
                       + [pltpu.VMEM((ATT_GROUP_ROWS, 2 * ATT_BLOCK), BF16)] * 2,
        args=(q, dattn, kpad, vpad, sinks), vmem=VMEM_BIG, job=job)


def _bwd_qkv(dxres, dq, dkv, x3, h1, hk, wq, wkv, g_mix, g_kv, job=None):
    T = x3.shape[0]
    nt = T // ROW_TILE

    def body(dxr_ref, dq_ref, dkv_ref, x_ref, h1_ref, hk_ref, wq_ref, wkv_ref, gmix_ref, gkv_ref,
             dx_ref, dwq_ref, dwkv_ref, dgm_ref, dgk_ref, acc_q, acc_kv):
        i = pl.program_id(0)
        first = i == 0
        dqv = dq_ref[...]
        dkvv = dkv_ref[...]
        xv = x_ref[...]
        d1, dgm = _rms_bwd(xv, gmix_ref[...], _dot_nt(dqv, wq_ref[...]))
        d2, dgk = _rms_bwd(xv, gkv_ref[...], _dot_nt(dkvv, wkv_ref[...]))
        dx_ref[...] = dxr_ref[...] + d1 + d2
        _acc(acc_q, _dot_tn(h1_ref[...], dqv), first)
        _acc(acc_kv, _dot_tn(hk_ref[...], dkvv), first)
        _acc(dgm_ref, dgm, first)
        _acc(dgk_ref, dgk, first)

        @pl.when(i == nt - 1)
        def _():
            dwq_ref[...] = acc_q[...].astype(BF16)
            dwkv_ref[...] = acc_kv[...].astype(BF16)

    return _launch(
        body, name="bwd_qkv", grid=(nt,),
        in_specs=[_row_spec(D_MODEL), _row_spec(D_MODEL), _row_spec(2 * KV_DIM), _row_spec(D_MODEL), _row_spec(D_MODEL),
                  _row_spec(D_MODEL), _full_spec((D_MODEL, D_MODEL)), _full_spec((D_MODEL, 2 * KV_DIM)), _vec_spec(),
                  _vec_spec()],
        out_specs=[_row_spec(D_MODEL), _full_spec((D_MODEL, D_MODEL)), _full_spec((D_MODEL, 2 * KV_DIM)), _vec_spec(),
                   _vec_spec()],
        out_shape=[jax.ShapeDtypeStruct((T, D_MODEL), F32), jax.ShapeDtypeStruct((D_MODEL, D_MODEL), BF16),
                   jax.ShapeDtypeStruct((D_MODEL, 2 * KV_DIM), BF16)] + [jax.ShapeDtypeStruct((1, D_MODEL), F32)] * 2,
        scratch_shapes=[pltpu.VMEM((D_MODEL, D_MODEL), F32), pltpu.VMEM((D_MODEL, 2 * KV_DIM), F32)],
        args=(dxres, dq, dkv, x3, h1, hk, wq, wkv, g_mix, g_kv), job=job)


def _bwd_pool_mixer(dx2, dh2, x1, x, yraw, d, wp, scale, g_ffn, g_post, g_pre, job=None):
    T = x.shape[0]
    tm = ROW_TILE
    nt = T // tm

    def body(dx2_ref, dh2_ref, x1_ref, x_ref, yraw_ref, d_ref, wp_ref, sc_ref, gffn_ref, gpost_ref, gpre_ref,
             dx_ref, dwp_ref, dsc_ref, dgf_ref, dgp_ref, dgm_ref, carry, acc):
        i = pl.program_id(0)
        first = i == 0
        tile = nt - 1 - i

        @pl.when(first)
        def _():
            carry[...] = jnp.zeros_like(carry)

        dxn, dgf = _rms_bwd(x1_ref[...], gffn_ref[...], dh2_ref[...])
        dx1 = dx2_ref[...] + dxn
        yraw = yraw_ref[...].astype(F32)
        sc = sc_ref[...]
        dy, dgp = _rms_bwd(yraw * sc, gpost_ref[...], dx1)
        dsc = jnp.sum(dy * yraw, axis=0, keepdims=True)
        dyb = (dy * sc).astype(BF16)
        dv = d_ref[...]
        dds = []
        for g in range(N_POOL_GROUPS):
            cols = slice(g * POOL_GROUP, (g + 1) * POOL_GROUP)
            dds.append(_dot_nt(dyb[:, cols], wp_ref[g]))
            _acc(acc.at[g], _dot_tn(dv[:, cols], dyb[:, cols]), first)
        dd = jnp.concatenate(dds, axis=1)
        e = dd / _pool_counts(tile * tm, tm)
        ext = jnp.concatenate([e, carry[...]], axis=0)
        carry[...] = e[:POOL_HALO, :]
        sums = _window_sums(ext, lambda k: tm + POOL_HALO - k)[:tm, :]
        dxm, dgm = _rms_bwd(x_ref[...], gpre_ref[...], sums - dd)
        dx_ref[...] = dx1 + dxm
        _acc(dsc_ref, dsc, first)
        _acc(dgf_ref, dgf, first)
        _acc(dgp_ref, dgp, first)
        _acc(dgm_ref, dgm, first)

        @pl.when(i == nt - 1)
        def _():
            dwp_ref[...] = acc[...].astype(BF16)

    rev = pl.BlockSpec((tm, D_MODEL), lambda i: (nt - 1 - i, 0))
    return _launch(
        body, name="bwd_pool_mixer", grid=(nt,),
        in_specs=[rev] * 6 + [_full_spec((N_POOL_GROUPS, POOL_GROUP, POOL_GROUP))] + [_vec_spec()] * 4,
        out_specs=[rev, _full_spec((N_POOL_GROUPS, POOL_GROUP, POOL_GROUP))] + [_vec_spec()] * 4,
        out_shape=[jax.ShapeDtypeStruct((T, D_MODEL), F32),
                   jax.ShapeDtypeStruct((N_POOL_GROUPS, POOL_GROUP, POOL_GROUP), BF16)]
                  + [jax.ShapeDtypeStruct((1, D_MODEL), F32)] * 4,
        scratch_shapes=[pltpu.VMEM((POOL_HALO, D_MODEL), F32), pltpu.VMEM((N_POOL_GROUPS, POOL_GROUP, POOL_GROUP), F32)],
        args=(dx2, dh2, x1, x, yraw, d, wp, scale, g_ffn, g_post, g_pre), job=job)


def _my_place():
    return lax.axis_index("x"), lax.axis_index("y"), lax.axis_index("c")


def _dev_index(px, py, pc):
    return 4 * px + 2 * py + pc


def _peer_by_relation(r):
    x, y, c = _my_place()
    return (x ^ ((r >> 2) & 1), y ^ ((r >> 1) & 1), c ^ (r & 1))


def _slot_pool(ref, j):
    return ref.at[:, pl.ds(pl.multiple_of(j * 32, 32), 32), :]


def _slot_scale(ref, j):
    return ref.at[:, pl.ds(pl.multiple_of(j * 128, 128), 128)]


def _slot_rows128(ref, j):
    return ref.at[pl.ds(pl.multiple_of(j * 128, 128), 128), :]


def _slot_gu(ref, j):
    return ref.at[j % FF_CHUNKS, j // FF_CHUNKS]


def _slot_wd(ref, j):
    return ref.at[pl.ds(pl.multiple_of(j * WD_ROWS, 16), WD_ROWS), :]


def _slot_cols128(ref, j):
    return ref.at[:, pl.ds(pl.multiple_of(j * 128, 128), 128)]


_GATHERED = {
    "pool": ((N_POOL_GROUPS, POOL_GROUP, POOL_GROUP), BF16, _slot_pool),
    "scale": ((1, D_MODEL), F32, _slot_scale),
    "kv": ((D_MODEL, 2 * KV_DIM), BF16, _slot_rows128),
    "q": ((D_MODEL, D_MODEL), BF16, _slot_rows128),
    "o": ((D_MODEL, D_MODEL), BF16, _slot_rows128),
    "gu": ((FF_CHUNKS, 2, FF_BLOCK, D_MODEL), BF16, _slot_gu),
    "wd": ((D_FF, D_MODEL), BF16, _slot_wd),
    "guh": ((FF_CHUNKS, 2, FF_BLOCK, D_MODEL // 2), BF16, _slot_gu),
    "wdh": ((D_FF, D_MODEL // 2), BF16, _slot_wd),
    "gate": ((D_MODEL, D_MODEL), BF16, _slot_rows128),
    "proj": ((PLE_DIM, D_MODEL), BF16, _slot_cols128),
}


def _no_compute():
    pass


class _AllGather:
    peers = ("sibling", "x", "y")

    def __init__(self, names, shards):
        self.kinds = [_GATHERED[n.rstrip("01_")] for n in names]
        self.args = [shards[n] for n in names]
        self.out_shape = [jax.ShapeDtypeStruct(shape, dtype) for shape, dtype, _ in self.kinds]
        n = len(names)
        self.scratch = [pltpu.SemaphoreType.DMA((n, 7)), pltpu.SemaphoreType.DMA((n, 7)), pltpu.SemaphoreType.DMA((n,))]

    def _plan(self, srcs, outs, sems):
        send_sems, recv_sems, local_sems = sems
        x, y, c = _my_place()

        def slot(t, dev):
            return self.kinds[t][2](outs[t], _dev_index(*dev))

        def copy(t, k, block, to, src=None):
            return pltpu.make_async_remote_copy(
                src_ref=slot(t, block) if src is None else src, dst_ref=slot(t, block),
                send_sem=send_sems.at[t, k], recv_sem=recv_sems.at[t, k], device_id=to, device_id_type=MESH)

        return types.SimpleNamespace(
            copy=copy, core=c, me=(x, y, c), sibling=(x, y, 1 - c),
            x_chip=(1 - x, y), y_chip=(x, 1 - y), far_chip=(1 - x, 1 - y),
            via=(x ^ (1 - c), y ^ c),
            onto=(x ^ c, y ^ (1 - c)),
            k_via=1 + c, k_onto=2 - c,
            local=[pltpu.make_async_copy(srcs[t], slot(t, (x, y, c)), local_sems.at[t]) for t in range(len(srcs))])

    def start(self, srcs, outs, sems):
        p = self._plan(srcs, outs, sems)
        for cp in p.local:
            cp.start()
        for t in range(len(srcs)):
            p.copy(t, 0, p.me, p.sibling, src=srcs[t]).start()
            p.copy(t, 1, p.me, (*p.x_chip, p.core), src=srcs[t]).start()
            p.copy(t, 2, p.me, (*p.y_chip, p.core), src=srcs[t]).start()

    def mid(self, srcs, outs, sems):
        p = self._plan(srcs, outs, sems)
        for t in range(len(srcs)):
            block = (*p.via, p.core)
            p.copy(t, p.k_via, block, p.me).wait_recv()
            p.copy(t, 3, block, (*p.onto, p.core)).start()
            p.copy(t, 3 + p.k_via, block, p.sibling).start()

    def finish(self, srcs, outs, sems):
        p = self._plan(srcs, outs, sems)
        n = len(srcs)
        for t in range(n):
            block = (*p.onto, p.core)
            p.copy(t, p.k_onto, block, p.me).wait_recv()
            p.copy(t, 3 + p.k_onto, block, p.sibling).start()
        for t in range(n):
            block = (*p.far_chip, p.core)
            p.copy(t, 3, block, p.me).wait_recv()
            p.copy(t, 6, block, p.sibling).start()
        other = 1 - p.core
        for t in range(n):
            p.copy(t, 0, (*p.me[:2], other), p.me).wait_recv()
            for k, chip in ((4, p.x_chip), (5, p.y_chip), (6, p.far_chip)):
                p.copy(t, k, (*chip, other), p.me).wait_recv()
            for k in range(7):
                p.copy(t, k, p.me, p.sibling).wait_send()
        for cp in p.local:
            cp.wait()


def _all_gather_only(name, names, shards):
    return _launch(_no_compute, name=name, grid=(), in_specs=[], out_specs=[], out_shape=[], args=(),
                   job=_AllGather(names, shards))[1]


def _block_pool(ref, j):
    return ref.at[:, pl.ds(pl.multiple_of(j * 32, 32), 32), :]


def _block_rows128(ref, j):
    return ref.at[pl.ds(pl.multiple_of(j * 128, 128), 128), :]


def _block_gu(ref, j):
    return ref.at[j % FF_CHUNKS, j // FF_CHUNKS]


def _block_wd(ref, j):
    return ref.at[pl.ds(pl.multiple_of(j * WD_ROWS, 16), WD_ROWS), :]


def _block_cols128(ref, j):
    return ref.at[:, pl.ds(pl.multiple_of(j * 128, 128), 128)]


_SCATTERED = {
    "pool": ((N_POOL_GROUPS, 32, POOL_GROUP), _block_pool),
    "kv": ((128, 2 * KV_DIM), _block_rows128),
    "q": ((128, D_MODEL), _block_rows128),
    "o": ((128, D_MODEL), _block_rows128),
    "gu": ((FF_BLOCK, FF_PART), _block_gu),
    "wd": ((WD_ROWS, FF_PART), _block_wd),
    "gate": ((128, D_MODEL), _block_rows128),
    "proj": ((PLE_DIM, 128), _block_cols128),
}


class _SiblingSwap:
    peers = ("sibling",)

    def __init__(self, pieces):
        self.kinds = [_SCATTERED[kind] for kind, _ in pieces]
        self.args = [g for _, g in pieces]
        self.out_shape = [jax.ShapeDtypeStruct((N_CHIPS, *block), BF16) for block, _ in self.kinds]
        n = len(pieces)
        self.scratch = [pltpu.SemaphoreType.DMA((n, N_CHIPS)), pltpu.SemaphoreType.DMA((n, N_CHIPS))]

    def _copies(self, srcs, outs, sems):
        send_sems, recv_sems = sems
        x, y, c = _my_place()
        return [pltpu.make_async_remote_copy(
            src_ref=block(srcs[t], 2 * ch + 1 - c), dst_ref=outs[t].at[ch], send_sem=send_sems.at[t, ch],
            recv_sem=recv_sems.at[t, ch], device_id=(x, y, 1 - c), device_id_type=MESH)
            for t, (_, block) in enumerate(self.kinds) for ch in range(N_CHIPS)]

    def start(self, srcs, outs, sems):
        for cp in self._copies(srcs, outs, sems):
            cp.start()

    def finish(self, srcs, outs, sems):
        for cp in self._copies(srcs, outs, sems):
            cp.wait()


class _ChipScatter:
    N_BUFS = 4
    peers = ("x", "y")

    def __init__(self, pieces):
        self.kinds = [_SCATTERED[kind] for kind, _, _ in pieces]
        self.n = n = len(pieces)
        self.args = [g for _, g, _ in pieces] + [s for _, _, s in pieces]
        self.out_shape = [jax.ShapeDtypeStruct((2, *block), BF16) for block, _ in self.kinds]
        self.scratch = []
        for block, _ in self.kinds:
            self.scratch += [pltpu.VMEM((N_CHIPS, *block), BF16)] * 3 + [pltpu.VMEM((2, *block), BF16)]
        dma = pltpu.SemaphoreType.DMA
        self.scratch += [dma((n, N_CHIPS + 1)), dma((n, 2)), dma((n, 2)), dma((n,)), dma((n,)), dma((n,))]

    def _plan(self, outs, scr):
        n = self.n
        first_send, first_recv, second_send, second_recv, keep_sems = scr[self.N_BUFS * n + 1:]
        x, y, c = _my_place()
        via = (x ^ (1 - c), y ^ c)
        onto = (x ^ c, y ^ (1 - c))
        index = lambda chip: 2 * chip[0] + chip[1]
        first, second, keep = [], [], []
        for t in range(n):
            total, inbox = scr[self.N_BUFS * t + 2], scr[self.N_BUFS * t + 3]
            for k, chip in enumerate((via, (1 - x, 1 - y))):
                first.append(pltpu.make_async_remote_copy(
                    src_ref=total.at[index(chip)], dst_ref=inbox.at[k], send_sem=first_send.at[t, k],
                    recv_sem=first_recv.at[t, k], device_id=(*via, c), device_id_type=MESH))
            second.append(pltpu.make_async_remote_copy(
                src_ref=total.at[index(onto)], dst_ref=outs[t].at[1], send_sem=second_send.at[t],
                recv_sem=second_recv.at[t], device_id=(*onto, c), device_id_type=MESH))
            keep.append(pltpu.make_async_copy(total.at[index((x, y))], outs[t].at[0], keep_sems.at[t]))
        return first, second, keep, index((x, y)), index(onto)

    def start(self, ins, outs, scr):
        n = self.n
        load_sems = scr[self.N_BUFS * n]
        c = lax.axis_index("c")
        loads = []
        for t, (_, block) in enumerate(self.kinds):
            mine, theirs = scr[self.N_BUFS * t], scr[self.N_BUFS * t + 1]
            loads += [pltpu.make_async_copy(block(ins[t], 2 * ch + c), mine.at[ch], load_sems.at[t, ch])
                      for ch in range(N_CHIPS)]
            loads.append(pltpu.make_async_copy(ins[n + t], theirs, load_sems.at[t, N_CHIPS]))
        for cp in loads:
            cp.start()
        for cp in loads:
            cp.wait()
        for t in range(n):
            mine, theirs, total = scr[self.N_BUFS * t:self.N_BUFS * t + 3]
            for ch in range(N_CHIPS):
                total[ch] = (mine[ch].astype(F32) + theirs[ch].astype(F32)).astype(BF16)
        for cp in self._plan(outs, scr)[0]:
            cp.start()

    def mid(self, ins, outs, scr):
        first, second, keep, me, onto = self._plan(outs, scr)
        for cp in first:
            cp.wait_recv()
        for t in range(self.n):
            total, inbox = scr[self.N_BUFS * t + 2], scr[self.N_BUFS * t + 3]
            for k, slot in enumerate((me, onto)):
                total[slot] = (total[slot].astype(F32) + inbox[k].astype(F32)).astype(BF16)
        for cp in second + keep:
            cp.start()

    def finish(self, ins, outs, scr):
        first, second, keep, _, _ = self._plan(outs, scr)
        for cp in first:
            cp.wait_send()
        for cp in second + keep:
            cp.wait()


class _Jobs:
    def __init__(self, *jobs):
        self.jobs = jobs
        together = {p for j in jobs for p in j.peers}
        self.peers = tuple(p for p in _PEER_SETS[0] if p in together)
        self.args = [a for j in jobs for a in j.args]
        self.out_shape = [o for j in jobs for o in j.out_shape]
        self.scratch = [s for j in jobs for s in j.scratch]

    def _split(self, refs, attr):
        at = 0
        for j in self.jobs:
            n = len(getattr(j, attr))
            yield refs[at:at + n]
            at += n

    def _each(self, ins, outs, scr):
        return zip(self.jobs, self._split(ins, "args"), self._split(outs, "out_shape"), self._split(scr, "scratch"))

    def start(self, ins, outs, scr):
        for j, i, o, s in self._each(ins, outs, scr):
            j.start(i, o, s)

    def mid(self, ins, outs, scr):
        for j, i, o, s in self._each(ins, outs, scr):
            if hasattr(j, "mid"):
                j.mid(i, o, s)

    def finish(self, ins, outs, scr):
        for j, i, o, s in self._each(ins, outs, scr):
            j.finish(i, o, s)

    def split_outputs(self, outs):
        return list(self._split(outs, "out_shape"))


def _adamw_math(w, g, m, v):
    m = ADAM_B1 * m + (1.0 - ADAM_B1) * g
    v = ADAM_B2 * v + (1.0 - ADAM_B2) * (g * g)
    m_hat = m / (1.0 - ADAM_B1 ** ADAM_STEP)
    v_hat = v / (1.0 - ADAM_B2 ** ADAM_STEP)
    delta = -ADAM_LR * (m_hat / (jnp.sqrt(v_hat) + ADAM_EPS) + ADAM_WD * w)
    return delta, m, v


def _adamw(name, w, m, v, landings, n_col_blocks=1, job=None):
    n_slots, r, c = landings[0].shape
    grid = (w.shape[0] // r, n_col_blocks)

    def body(w_ref, m_ref, v_ref, *rest):
        l_refs, (g_ref, d_ref, nm_ref, nv_ref) = rest[:len(landings)], rest[len(landings):]
        step = pl.program_id(0) * n_col_blocks + pl.program_id(1)
        for idx, l_ref in enumerate(l_refs):
            @pl.when(step == idx)
            def _(l_ref=l_ref):
                g = l_ref[0].astype(F32)
                for s in range(1, n_slots):
                    g = g + l_ref[s].astype(F32)
                g_ref[...] = g
                d_ref[...], nm_ref[...], nv_ref[...] = _adamw_math(w_ref[...], g, m_ref[...], v_ref[...])

    spec = pl.BlockSpec((r, c), lambda a, b: (a, b))
    return _launch(
        body, name=f"adamw_{name}", grid=grid,
        in_specs=[spec, spec, spec] + [_full_spec((n_slots, r, c))] * len(landings),
        out_specs=[spec] * 4, out_shape=[jax.ShapeDtypeStruct(w.shape, F32)] * 4,
        args=(w, m, v, *landings), vmem=VMEM_BIG, job=job)


_SMALL = (("pre_mix_g", SV_PRE_MIX, 2), ("post_mix_g", SV_POST_MIX, 2), ("pre_ffn_g", SV_PRE_FFN, 2),
          ("post_ffn_g", SV_POST_FFN, 2), ("ple_g", SV_PLE, 2), ("ple_post_g", SV_PLE_POST, 2), ("kv_g", SV_KV, 1),
          ("pool_scale", SV_POOL_SCALE, 1), ("sinks", SV_SINKS, 1))


def _small_all_reduce_adamw(part, params):
    flat = [a for name, _, _ in _SMALL for a in params[name]]
    n_in = 1 + len(flat)

    def body(*refs):
        part_ref, wmv = refs[0], refs[1:n_in]
        loss_ref, outs = refs[n_in], refs[n_in + 1:n_in + 1 + 4 * len(_SMALL)]
        buf, total, send_sems, recv_sems = refs[n_in + 1 + 4 * len(_SMALL):]
        x, y, c = _my_place()
        me = _dev_index(x, y, c)
        buf[me] = part_ref[...]
        copies = [pltpu.make_async_remote_copy(
            src_ref=buf.at[me], dst_ref=buf.at[me], send_sem=send_sems.at[r - 1], recv_sem=recv_sems.at[r - 1],
            device_id=_peer_by_relation(r), device_id_type=MESH) for r in range(1, N_DEV)]
        for cp in copies:
            cp.start()
        for cp in copies:
            cp.wait()
        g = buf[0]
        for s in range(1, N_DEV):
            g = g + buf[s]
        total[...] = g
        loss_ref[...] = total[SV_LOSS:SV_LOSS + 1, 0:1]
        for idx, (name, row, n_rows) in enumerate(_SMALL):
            w_ref, m_ref, v_ref = wmv[3 * idx:3 * idx + 3]
            g_ref, d_ref, nm_ref, nv_ref = outs[4 * idx:4 * idx + 4]
            if name == "pool_scale":
                g = total[row:row + 1, pl.ds(pl.multiple_of(me * 128, 128), 128)]
            else:
                g = total[row:row + n_rows, 0:w_ref.shape[1]]
            g_ref[...] = g
            d_ref[...], nm_ref[...], nv_ref[...] = _adamw_math(w_ref[...], g, m_ref[...], v_ref[...])

    out_shape = [jax.ShapeDtypeStruct((1, 1), F32)]
    for name, _, _ in _SMALL:
        out_shape += [jax.ShapeDtypeStruct(params[name][0].shape, F32)] * 4
    res, _ = _launch(
        body, name="small_all_reduce_adamw", grid=(1,),
        in_specs=[_full_spec(a.shape) for a in (part, *flat)], out_specs=[_full_spec(s.shape) for s in out_shape],
        out_shape=out_shape,
        scratch_shapes=[pltpu.VMEM((N_DEV, SV_ROWS, D_MODEL), F32), pltpu.VMEM((SV_ROWS, D_MODEL), F32),
                        pltpu.SemaphoreType.DMA((N_DEV - 1,)), pltpu.SemaphoreType.DMA((N_DEV - 1,))],
        args=(part, *flat))
    return res[0], {name: res[1 + 4 * idx:5 + 4 * idx] for idx, (name, _, _) in enumerate(_SMALL)}


def _local_step(x, p, tgt, gains, sinks, shards, weights):
    row = lambda first_row, layer: _Gain(gains, first_row + layer)
    gather = lambda *names: _AllGather(names, shards)
    g_pre_mix, g_post_mix, g_pre_ffn, g_post_ffn = SV_PRE_MIX, SV_POST_MIX, SV_PRE_FFN, SV_POST_FFN
    g_ple, g_ple_post, g_kv = SV_PLE, SV_PLE_POST, _Gain(gains, SV_KV)

    wp, scale, wgu0 = _all_gather_only("gather_first", ("pool", "scale", "gu0"), shards)
    wgu0 = [wgu0]
    (x1_0, h2_0, yraw, dpool), wd0 = _fwd_pool_mixer(
        x, row(g_pre_mix, 0), wp, scale, row(g_post_mix, 0), row(g_pre_ffn, 0), job=gather("wd0"))
    (gs0, us0, f0, x2_0, h3_0), (wgate0, wproj0, wkv, wq, wgu1_a) = _fwd_ffn(
        0, h2_0, x1_0, wgu0, wd0, row(g_post_ffn, 0), row(g_ple, 0),
        job=gather("gate0", "proj0", "kv", "q", "guh1_0"))
    (x3_0, z0, pe0), (wo,) = _fwd_ple(0, x2_0, h3_0, p[0], wgate0, wproj0, row(g_ple_post, 0), job=gather("o"))
    (hk, h1, q, kv), (wd1_a,) = _fwd_qkv(x3_0, g_kv, row(g_pre_mix, 1), wkv, wq, job=gather("wdh1_0"))
    front = ((ATT_BLOCK, 0), (0, 0))
    kpad = jnp.pad(kv[:, :KV_DIM], front)
    vpad = jnp.pad(kv[:, KV_DIM:], front)
    (attn,), (wgu1_b,) = _fwd_attention(q, kpad, vpad, sinks, job=gather("guh1_1"))
    (y1, x1_1, h2_1), (wd1_b,) = _fwd_attn_out(attn, x3_0, wo, row(g_post_mix, 1), row(g_pre_ffn, 1),
                                               job=gather("wdh1_1"))
    wgu1, wd1 = [wgu1_a, wgu1_b], [wd1_a, wd1_b]
    (gs1, us1, f1, x2_1, h3_1), (wgate1, wproj1) = _fwd_ffn(
        1, h2_1, x1_1, wgu1, wd1, row(g_post_ffn, 1), row(g_ple, 1), job=gather("gate1", "proj1"))

    produced, swapped, landed = {}, {}, {}

    def kind_of(name):
        return name.rstrip("0123_")

    def carry(swap=(), spread=()):
        jobs = []
        if swap:
            jobs.append(_SiblingSwap([(kind_of(n), produced[n]) for n in swap]))
        if spread:
            jobs.append(_ChipScatter([(kind_of(n), produced[n], swapped[n]) for n in spread]))
        return _Jobs(*jobs)

    def carried(jobs, outs, swap=(), spread=()):
        parts = jobs.split_outputs(outs)
        if swap:
            swapped.update(zip(swap, parts[0]))
        if spread:
            landed.update(zip(spread, parts[-1]))

    def hosted(call, *args, swap=(), spread=()):
        jobs = carry(swap, spread)
        outs, job_outs = call(*args, job=jobs)
        carried(jobs, job_outs, swap, spread)
        return outs

    def ffn_weight_grads(layer, h2, df, dg, du, a, hosts):
        for qtr in range(FF_PARTS):
            dgu, dwd = hosted(_bwd_ffn_dw, layer, qtr, h2, df, dg, du, a, **hosts[qtr])
            produced[f"gu{layer}_{qtr}"], produced[f"wd{layer}_{qtr}"] = dgu, dwd

    ffn_q = lambda layer, qtr: (f"gu{layer}_{qtr}", f"wd{layer}_{qtr}")

    dx2_1, df1, produced["gate1"], produced["proj1"], dg_ple_post1, dg_ple1, dg_post_ffn1, loss = hosted(
        _ple_loss_bwd, 1, x2_1, h3_1, p[1], f1, tgt, wgate1, wproj1, row(g_ple_post, 1), row(g_ple, 1),
        row(g_post_ffn, 1))
    dh2_1, dg1, du1, a1 = hosted(_bwd_ffn_act, 1, df1, gs1, us1, wgu1, wd1, swap=("gate1", "proj1"))
    ffn_weight_grads(1, h2_1, df1, dg1, du1, a1, [dict(spread=("gate1", "proj1")), dict(swap=ffn_q(1, 0))])
    dx1_1, dattn, produced["o"], dg_pre_ffn1, dg_post_mix1 = hosted(
        _bwd_attn_out, dx2_1, dh2_1, x1_1, y1, attn, wo, row(g_pre_ffn, 1), row(g_post_mix, 1), swap=ffn_q(1, 1))
    dq, dkpad, dvpad, dsinks = hosted(_bwd_attention, q, dattn, kpad, vpad, sinks, spread=ffn_q(1, 0))
    dkv = jnp.concatenate([dkpad[ATT_BLOCK:], dvpad[ATT_BLOCK:]], axis=1).astype(BF16)
    dx3_0, produced["q"], produced["kv"], dg_pre_mix1, dg_kv = hosted(
        _bwd_qkv, dx1_1, dq, dkv, x3_0, h1, hk, wq, wkv, row(g_pre_mix, 1), g_kv, swap=("o",))
    dx2_0, df0, produced["gate0"], produced["proj0"], dg_ple_post0, dg_ple0, dg_post_ffn0 = hosted(
        _bwd_ple, 0, dx3_0, x2_0, z0, pe0, h3_0, p[0], f0, wgate0, row(g_ple_post, 0), row(g_ple, 0),
        row(g_post_ffn, 0), swap=("q", "kv"), spread=("gu1_1",))
    dh2_0, dg0, du0, a0 = hosted(_bwd_ffn_act, 0, df0, gs0, us0, wgu0, wd0,
                                 swap=("gate0", "proj0"), spread=("wd1_1", "o"))
    ffn_weight_grads(0, h2_0, df0, dg0, du0, a0, [
        dict(spread=("gate0", "proj0", "q", "kv")), dict(swap=ffn_q(0, 0))])
    grad_x, produced["pool"], dscale, dg_pre_ffn0, dg_post_mix0, dg_pre_mix0 = hosted(
        _bwd_pool_mixer, dx2_0, dh2_0, x1_0, x, yraw, dpool, wp, scale, row(g_pre_ffn, 0), row(g_post_mix, 0),
        row(g_pre_mix, 0), swap=ffn_q(0, 1), spread=ffn_q(0, 0))

    def update(name, n_col_blocks=1, pieces=None, swap=(), spread=()):
        w, m, v = weights[name]
        rows = w.size // w.shape[-1]
        flat = [landed[n].reshape(landed[n].shape[0], -1, landed[n].shape[-1])
                for n in (pieces or [kind_short[name]])]
        outs = hosted(_adamw, name, w.reshape(rows, -1), m.reshape(rows, -1), v.reshape(rows, -1), flat,
                      n_col_blocks, swap=swap, spread=spread)
        return [o.reshape(w.shape) for o in outs]

    kind_short = {"w_q": "q", "w_kv": "kv", "w_o": "o", "pool_w": "pool"}
    upd = {}
    upd["w_ple_gate"] = update("w_ple_gate", pieces=("gate0", "gate1"), swap=("pool",), spread=ffn_q(0, 1))
    upd["w_ple_proj"] = update("w_ple_proj", pieces=("proj0", "proj1"), spread=("pool",))
    for name in ("w_q", "w_kv", "w_o", "pool_w"):
        upd[name] = update(name)
    upd["w_gu"] = update("w_gu", FF_PARTS,
                         pieces=[f"gu{layer}_{qtr}" for layer in range(2) for qtr in range(FF_PARTS)])
    upd["w_gu"] = [jnp.swapaxes(a, 1, 2) for a in upd["w_gu"]]
    upd["w_down"] = update("w_down", FF_PARTS,
                           pieces=[f"wd{layer}_{qtr}" for layer in range(2) for qtr in range(FF_PARTS)])

    lanes = lambda a: jnp.pad(a, ((0, 0), (0, D_MODEL - a.shape[1])))
    small = jnp.concatenate([
        dg_pre_mix0, dg_pre_mix1, dg_post_mix0, dg_post_mix1, dg_pre_ffn0, dg_pre_ffn1, dg_post_ffn0, dg_post_ffn1,
        dg_ple0, dg_ple1, dg_ple_post0, dg_ple_post1, dg_kv, dscale, lanes(dsinks[:, :N_HEADS]), lanes(loss)], axis=0)
    return grad_x, upd, small


def kernel(x, p, pre_mix_g, post_mix_g, pre_ffn_g, post_ffn_g, pool_w, pool_scale, kv_g, w_kv, w_q, sinks, w_o, w_gu, w_down, ple_g, w_ple_gate, w_ple_proj, ple_post_g, loss_target, m_pre_mix_g, m_post_mix_g, m_pre_ffn_g, m_post_ffn_g, m_pool_w, m_pool_scale, m_kv_g, m_w_kv, m_w_q, m_sinks, m_w_o, m_w_gu, m_w_down, m_ple_g, m_w_ple_gate, m_w_ple_proj, m_ple_post_g, v_pre_mix_g, v_post_mix_g, v_pre_ffn_g, v_post_ffn_g, v_pool_w, v_pool_scale, v_kv_g, v_w_kv, v_w_q, v_sinks, v_w_o, v_w_gu, v_w_down, v_ple_g, v_w_ple_gate, v_w_ple_proj, v_ple_post_g):
    shards = {"pool": pool_w[0].astype(BF16), "scale": pool_scale, "kv": w_kv.astype(BF16),
              "q": w_q[0].astype(BF16), "o": w_o[0].astype(BF16)}
    for layer in range(2):
        shards[f"gu{layer}"] = w_gu[layer].T.astype(BF16)
        shards[f"wd{layer}"] = w_down[layer].astype(BF16)
        for half in range(2):
            cols = slice(half * D_MODEL // 2, (half + 1) * D_MODEL // 2)
            shards[f"guh{layer}_{half}"] = shards[f"gu{layer}"][:, cols]
            shards[f"wdh{layer}_{half}"] = shards[f"wd{layer}"][:, cols]
        shards[f"gate{layer}"] = w_ple_gate[layer].astype(BF16)
        shards[f"proj{layer}"] = w_ple_proj[layer].astype(BF16)
    gains = jnp.concatenate([pre_mix_g, post_mix_g, pre_ffn_g, post_ffn_g, ple_g, ple_post_g, kv_g[None, :]],
                            axis=0).reshape(-1, 1, D_MODEL)
    weights = {"pool_w": (pool_w, m_pool_w, v_pool_w), "w_kv": (w_kv, m_w_kv, v_w_kv), "w_q": (w_q, m_w_q, v_w_q),
               "w_o": (w_o, m_w_o, v_w_o), "w_down": (w_down, m_w_down, v_w_down),
               "w_gu": tuple(jnp.swapaxes(a, 1, 2) for a in (w_gu, m_w_gu, v_w_gu)),
               "w_ple_gate": (w_ple_gate, m_w_ple_gate, v_w_ple_gate),
               "w_ple_proj": (w_ple_proj, m_w_ple_proj, v_w_ple_proj)}
    grad_x, upd, small = _local_step(x[0], p[:, 0], loss_target[0], gains, sinks, shards, weights)

    small_params = {
        "pre_mix_g": (pre_mix_g, m_pre_mix_g, v_pre_mix_g), "post_mix_g": (post_mix_g, m_post_mix_g, v_post_mix_g),
        "pre_ffn_g": (pre_ffn_g, m_pre_ffn_g, v_pre_ffn_g), "post_ffn_g": (post_ffn_g, m_post_ffn_g, v_post_ffn_g),
        "ple_g": (ple_g, m_ple_g, v_ple_g), "ple_post_g": (ple_post_g, m_ple_post_g, v_ple_post_g),
        "kv_g": (kv_g[None, :], m_kv_g[None, :], v_kv_g[None, :]),
        "pool_scale": (pool_scale, m_pool_scale, v_pool_scale), "sinks": (sinks, m_sinks, v_sinks)}
    loss, small_upd = _small_all_reduce_adamw(small, small_params)
    small_upd["kv_g"] = [a[0] for a in small_upd["kv_g"]]
    upd.update(small_upd)

    names = ["pre_mix_g", "post_mix_g", "pre_ffn_g", "post_ffn_g", "pool_w", "pool_scale", "kv_g", "w_kv", "w_q",
             "sinks", "w_o", "w_gu", "w_down", "ple_g", "w_ple_gate", "w_ple_proj", "ple_post_g"]
    outs = [loss[0, 0], grad_x[None]]
    for kind in range(4):
        outs += [upd[n][kind] for n in names]
    return tuple(outs)
```

```python
import functools
import types

import jax
import jax.numpy as jnp
from jax import lax
from jax.experimental import pallas as pl
from jax.experimental.pallas import tpu as pltpu

F32 = jnp.float32
BF16 = jnp.bfloat16

N_DEV = 8
D_MODEL = 1024
N_POOL_GROUPS = 4
POOL_GROUP = 256
POOL_HALO = 16
HEAD_DIM = 64
N_HEADS = 16
N_KV_HEADS = 4
GQA_GROUP = 4
KV_DIM = N_KV_HEADS * HEAD_DIM
ATT_BLOCK = 128
D_FF = 2816
FF_CHUNKS = 4
FF_BLOCK = D_FF // FF_CHUNKS
WD_ROWS = D_FF // N_DEV
FF_PARTS = 2
FF_PART = D_MODEL // FF_PARTS
N_CHIPS = 4
PLE_DIM = 256
EPS = 1e-6
NEG_INF = -1e30
ATT_SCALE = HEAD_DIM ** -0.5

ADAM_LR = 0.001
ADAM_B1 = 0.9
ADAM_B2 = 0.999
ADAM_EPS = 1e-08
ADAM_WD = 0.01
ADAM_STEP = 10

ROW_TILE = 512
FFN_ROW_TILE = 512
FFN_SUB_TILES = 1
VMEM_BIG = 60 * 1024 * 1024
VMEM_MID = 56 * 1024 * 1024
HBM_PIN_ELEMS = 1024

SV_ROWS = 16
SV_PRE_MIX, SV_POST_MIX, SV_PRE_FFN, SV_POST_FFN, SV_PLE, SV_PLE_POST = 0, 2, 4, 6, 8, 10
SV_KV, SV_POOL_SCALE, SV_SINKS, SV_LOSS = 12, 13, 14, 15

MESH = pl.DeviceIdType.MESH
ANY = pl.BlockSpec(memory_space=pl.ANY)


def _dot(a, b):
    return jnp.dot(a, b, preferred_element_type=F32)


def _dot_nt(a, b):
    return lax.dot_general(a, b, (((1,), (1,)), ((), ())), preferred_element_type=F32)


def _dot_tn(a, b):
    return lax.dot_general(a, b, (((0,), (0,)), ((), ())), preferred_element_type=F32)


def _rstd(x):
    return lax.rsqrt(jnp.mean(x * x, axis=-1, keepdims=True) + EPS)


def _rms(x, g):
    return x * _rstd(x) * g


def _rms_bwd(x, g, dy):
    r = _rstd(x)
    n = x * r
    dn = dy * g
    dx = r * (dn - n * jnp.mean(dn * n, axis=-1, keepdims=True))
    dg = jnp.sum(dy * n, axis=0, keepdims=True)
    return dx, dg


def _add_all(terms):
    return functools.reduce(jnp.add, terms)


def _sigmoid(x):
    return 1.0 / (1.0 + jnp.exp(-x))


def _acc(ref, val, first):
    @pl.when(first)
    def _():
        ref[...] = val

    @pl.when(jnp.logical_not(first))
    def _():
        ref[...] += val


def _pool_counts(row0, rows):
    t = row0 + lax.broadcasted_iota(jnp.int32, (rows, D_MODEL), 0) + 1
    grp = lax.broadcasted_iota(jnp.int32, (rows, D_MODEL), 1) // POOL_GROUP
    win = jnp.left_shift(2, grp)
    return jnp.minimum(t, win).astype(F32)


def _window_sums(ext, shift_of):
    outs = []
    s = ext
    for gi in range(N_POOL_GROUPS):
        s = s + pltpu.roll(s, shift_of(1 << gi), axis=0)
        outs.append(s[:, :POOL_GROUP])
        s = s[:, POOL_GROUP:]
    return jnp.concatenate(outs, axis=1)


def _cparams(n_axes, vmem, collective_id=None):
    return pltpu.CompilerParams(dimension_semantics=("arbitrary",) * n_axes, vmem_limit_bytes=vmem,
                                collective_id=collective_id)


_PEER_SETS = (("sibling", "x", "y"), ("sibling",), ("x", "y"))


def _meet(peers):
    x, y, c = lax.axis_index("x"), lax.axis_index("y"), lax.axis_index("c")
    device = {"sibling": (x, y, 1 - c), "x": (1 - x, y, c), "y": (x, 1 - y, c)}
    barrier = pltpu.get_barrier_semaphore()
    for peer in peers:
        pl.semaphore_signal(barrier, inc=1, device_id=device[peer], device_id_type=pl.DeviceIdType.MESH)
    pl.semaphore_wait(barrier, len(peers))


def _row_spec(cols, tm=ROW_TILE):
    return pl.BlockSpec((tm, cols), lambda i: (i, 0))


def _full_spec(shape):
    zeros = (0,) * len(shape)
    return pl.BlockSpec(shape, lambda *_: zeros)


def _vec_spec():
    return _full_spec((1, D_MODEL))


def _column_ranges(parts):
    ends = [0]
    for part in parts:
        ends.append(ends[-1] + part.shape[-1])
    return list(zip(ends[:-1], ends[1:]))


class _Gain:
    def __init__(self, stacked, layer):
        self.stacked, self.layer = stacked, layer

    def spec(self):
        layer = self.layer
        return pl.BlockSpec((None, 1, D_MODEL), lambda *_: (layer, 0, 0))


def _in_hbm(a):
    return pltpu.with_memory_space_constraint(a, pltpu.HBM) if a.size >= HBM_PIN_ELEMS else a


def _out_in_hbm(s):
    return pltpu.HBM(s.shape, s.dtype) if s.size >= HBM_PIN_ELEMS else s


def _launch(body, *, name, grid, in_specs, out_specs, out_shape, args, scratch_shapes=(), vmem=VMEM_MID, job=None):
    in_specs = [a.spec() if isinstance(a, _Gain) else s for s, a in zip(in_specs, args)]
    args = [_in_hbm(a.stacked if isinstance(a, _Gain) else a) for a in args]
    n_in, n_out, n_scr = len(args), len(out_shape), len(scratch_shapes)
    if job is not None and not job.args:
        job = None
    j_args, j_out, j_scr = ([], [], []) if job is None else ([_in_hbm(a) for a in job.args], job.out_shape, job.scratch)

    def run(*refs):
        groups, at = [], 0
        for n in (n_in, len(j_args), n_out, len(j_out), n_scr, len(j_scr)):
            groups.append(refs[at:at + n])
            at += n
        ins, j_ins, outs, j_outs, scr, j_sems = groups

        def begin():
            _meet(job.peers)
            job.start(j_ins, j_outs, j_sems)

        if job is None:
            body(*ins, *outs, *scr)
        elif not grid:
            begin()
            job.mid(j_ins, j_outs, j_sems)
            body(*ins, *outs, *scr)
            job.finish(j_ins, j_outs, j_sems)
        else:
            ids = [pl.program_id(a) for a in range(len(grid))]
            first = functools.reduce(jnp.logical_and, [i == 0 for i in ids])
            half = functools.reduce(jnp.logical_and, [ids[0] == grid[0] // 2] + [i == 0 for i in ids[1:]])
            last = functools.reduce(jnp.logical_and, [i == g - 1 for i, g in zip(ids, grid)])
            pl.when(first)(begin)
            pl.when(half)(lambda: job.mid(j_ins, j_outs, j_sems))
            body(*ins, *outs, *scr)
            pl.when(last)(lambda: job.finish(j_ins, j_outs, j_sems))

    res = pl.pallas_call(
        run, name=name, grid=grid,
        in_specs=list(in_specs) + [ANY] * len(j_args), out_specs=list(out_specs) + [ANY] * len(j_out),
        out_shape=[_out_in_hbm(s) for s in list(out_shape) + list(j_out)],
        scratch_shapes=list(scratch_shapes) + list(j_scr),
        compiler_params=_cparams(len(grid), vmem, None if job is None else _PEER_SETS.index(job.peers)),
    )(*args, *j_args)
    return res[:n_out], res[n_out:]


def _fwd_pool_mixer(x, g_pre, wp, scale, g_post, g_ffn, job=None):
    T = x.shape[0]
    tm = ROW_TILE
    nt = T // tm

    def body(x_ref, gpre_ref, wp_ref, sc_ref, gpost_ref, gffn_ref, x1_ref, h2_ref, yraw_ref, d_ref, carry):
        i = pl.program_id(0)

        @pl.when(i == 0)
        def _():
            carry[...] = jnp.zeros_like(carry)

        xv = x_ref[...]
        h = _rms(xv, gpre_ref[...])
        ext = jnp.concatenate([carry[...], h], axis=0)
        carry[...] = h[tm - POOL_HALO:, :]
        sums = _window_sums(ext, lambda k: k)[POOL_HALO:, :]
        d = sums / _pool_counts(i * tm, tm) - h
        db = d.astype(BF16)
        d_ref[...] = db
        yraw = jnp.concatenate(
            [_dot(db[:, g * POOL_GROUP:(g + 1) * POOL_GROUP], wp_ref[g]) for g in range(N_POOL_GROUPS)], axis=1)
        yraw_ref[...] = yraw.astype(BF16)
        x1 = xv + _rms(yraw * sc_ref[...], gpost_ref[...])
        x1_ref[...] = x1
        h2_ref[...] = _rms(x1, gffn_ref[...]).astype(BF16)

    return _launch(
        body, name="fwd_pool_mixer", grid=(nt,),
        in_specs=[_row_spec(D_MODEL), _vec_spec(), _full_spec((N_POOL_GROUPS, POOL_GROUP, POOL_GROUP)), _vec_spec(),
                  _vec_spec(), _vec_spec()],
        out_specs=[_row_spec(D_MODEL)] * 4,
        out_shape=[jax.ShapeDtypeStruct((T, D_MODEL), F32)] + [jax.ShapeDtypeStruct((T, D_MODEL), BF16)] * 3,
        scratch_shapes=[pltpu.VMEM((POOL_HALO, D_MODEL), F32)],
        args=(x, g_pre, wp, scale, g_post, g_ffn), job=job)


def _fwd_ffn(layer, h2, x1, wgu, wd, g_post, g_ple, job=None):
    T = h2.shape[0]
    tm = min(FFN_ROW_TILE, T)
    nt = T // tm
    sub = tm // FFN_SUB_TILES
    last = FF_CHUNKS - 1
    n_gu, n_wd = len(wgu), len(wd)
    gu_cols = _column_ranges(wgu)

    def body(h2_ref, x1_ref, *refs):
        wgu_refs, wd_refs = refs[:n_gu], refs[n_gu:n_gu + n_wd]
        gpost_ref, gple_ref, gs_ref, us_ref, f_ref, x2_ref, h3_ref, acc = refs[n_gu + n_wd:]
        k = pl.program_id(0)
        i = pl.program_id(1)
        rows = pl.ds(pl.multiple_of(i * tm, tm), tm)
        parts = []
        for s in range(FFN_SUB_TILES):
            r = pl.ds(s * sub, sub)
            g = _add_all([_dot_nt(h2_ref[r, c0:c1], w[0]) for (c0, c1), w in zip(gu_cols, wgu_refs)])
            u = _add_all([_dot_nt(h2_ref[r, c0:c1], w[1]) for (c0, c1), w in zip(gu_cols, wgu_refs)])
            gs_ref[r, :] = g.astype(BF16)
            us_ref[r, :] = u.astype(BF16)
            a = (g * _sigmoid(g) * u).astype(BF16)
            parts.append(jnp.concatenate([_dot(a, w[...]) for w in wd_refs], axis=1))
        part = jnp.concatenate(parts, axis=0)

        @pl.when(k == 0)
        def _():
            acc[rows, :] = part

        @pl.when(jnp.logical_and(k > 0, k < last))
        def _():
            acc[rows, :] += part

        @pl.when(k == last)
        def _():
            f = acc[rows, :] + part
            f_ref[...] = f.astype(BF16)
            x2 = x1_ref[...] + _rms(f, gpost_ref[...])
            x2_ref[...] = x2
            h3_ref[...] = _rms(x2, gple_ref[...]).astype(BF16)

    def late(k, i):
        return (jnp.where(k == last, i, 0), 0)

    return _launch(
        body, name=f"fwd_ffn{layer}", grid=(FF_CHUNKS, nt),
        in_specs=[pl.BlockSpec((tm, D_MODEL), lambda k, i: (i, 0)), pl.BlockSpec((tm, D_MODEL), late)]
                 + [pl.BlockSpec((None, 2, FF_BLOCK, w.shape[-1]), lambda k, i: (k, 0, 0, 0)) for w in wgu]
                 + [pl.BlockSpec((FF_BLOCK, w.shape[-1]), lambda k, i: (k, 0)) for w in wd]
                 + [pl.BlockSpec((1, D_MODEL), lambda k, i: (0, 0))] * 2,
        out_specs=[pl.BlockSpec((None, tm, FF_BLOCK), lambda k, i: (k, i, 0)),
                   pl.BlockSpec((None, tm, FF_BLOCK), lambda k, i: (k, i, 0)),
                   pl.BlockSpec((tm, D_MODEL), late),
                   pl.BlockSpec((tm, D_MODEL), late),
                   pl.BlockSpec((tm, D_MODEL), late)],
        out_shape=[jax.ShapeDtypeStruct((FF_CHUNKS, T, FF_BLOCK), BF16),
                   jax.ShapeDtypeStruct((FF_CHUNKS, T, FF_BLOCK), BF16),
                   jax.ShapeDtypeStruct((T, D_MODEL), BF16),
                   jax.ShapeDtypeStruct((T, D_MODEL), F32),
                   jax.ShapeDtypeStruct((T, D_MODEL), BF16)],
        scratch_shapes=[pltpu.VMEM((T, D_MODEL), F32)],
        args=(h2, x1, *wgu, *wd, g_post, g_ple), vmem=VMEM_BIG, job=job)


def _fwd_ple(layer, x2, h3, p, wgate, wproj, g_post, job=None):
    T = x2.shape[0]
    nt = T // ROW_TILE

    def body(x2_ref, h3_ref, p_ref, wg_ref, wp_ref, gpost_ref, x3_ref, z_ref, pe_ref):
        z = _dot(h3_ref[...], wg_ref[...])
        pe = _dot(p_ref[...].astype(BF16), wp_ref[...])
        z_ref[...] = z.astype(BF16)
        pe_ref[...] = pe.astype(BF16)
        x3_ref[...] = x2_ref[...] + _rms(pe * _sigmoid(z), gpost_ref[...])

    return _launch(
        body, name=f"fwd_ple{layer}", grid=(nt,),
        in_specs=[_row_spec(D_MODEL), _row_spec(D_MODEL), _row_spec(PLE_DIM), _full_spec((D_MODEL, D_MODEL)),
                  _full_spec((PLE_DIM, D_MODEL)), _vec_spec()],
        out_specs=[_row_spec(D_MODEL)] * 3,
        out_shape=[jax.ShapeDtypeStruct((T, D_MODEL), F32)] + [jax.ShapeDtypeStruct((T, D_MODEL), BF16)] * 2,
        args=(x2, h3, p, wgate, wproj, g_post), job=job)


def _fwd_qkv(x3, g_kv, g_mix, wkv, wq, job=None):
    T = x3.shape[0]
    nt = T // ROW_TILE

    def body(x_ref, gkv_ref, gmix_ref, wkv_ref, wq_ref, hk_ref, h1_ref, q_ref, kv_ref):
        xv = x_ref[...]
        r = _rstd(xv)
        hk = (xv * r * gkv_ref[...]).astype(BF16)
        h1 = (xv * r * gmix_ref[...]).astype(BF16)
        hk_ref[...] = hk
        h1_ref[...] = h1
        kv_ref[...] = _dot(hk, wkv_ref[...]).astype(BF16)
        q_ref[...] = _dot(h1, wq_ref[...]).astype(BF16)

    return _launch(
        body, name="fwd_qkv", grid=(nt,),
        in_specs=[_row_spec(D_MODEL), _vec_spec(), _vec_spec(), _full_spec((D_MODEL, 2 * KV_DIM)),
                  _full_spec((D_MODEL, D_MODEL))],
        out_specs=[_row_spec(D_MODEL), _row_spec(D_MODEL), _row_spec(D_MODEL), _row_spec(2 * KV_DIM)],
        out_shape=[jax.ShapeDtypeStruct((T, D_MODEL), BF16)] * 3 + [jax.ShapeDtypeStruct((T, 2 * KV_DIM), BF16)],
        args=(x3, g_kv, g_mix, wkv, wq), job=job)


def _alibi_slope(h):
    return 2.0 ** (-8.0 * (h + 1) / N_HEADS)


ATT_SUB = 32
ATT_GROUP_ROWS = GQA_GROUP * ATT_BLOCK


def _att_mask(n, row0):
    qi = lax.broadcasted_iota(jnp.int32, (ATT_SUB, 2 * ATT_BLOCK), 0) + row0
    si = lax.broadcasted_iota(jnp.int32, (ATT_SUB, 2 * ATT_BLOCK), 1)
    rel = ATT_BLOCK + qi - si
    valid = (rel >= 0) & (rel < ATT_BLOCK) & ((si >= ATT_BLOCK) | (n > 0))
    return rel.astype(F32), valid


def _att_probs(raw, relf, valid, slope, sink):
    s = jnp.where(valid, raw * ATT_SCALE - slope * relf, NEG_INF)
    m = jnp.maximum(jnp.max(s, axis=-1, keepdims=True), sink)
    e = jnp.exp(s - m)
    es = jnp.exp(sink - m)
    inv = 1.0 / (jnp.sum(e, axis=-1, keepdims=True) + es)
    return e * inv, es * inv


def _stack_heads(ref, kh):
    first = kh * GQA_GROUP
    return jnp.concatenate([ref[:, (first + g) * HEAD_DIM:(first + g + 1) * HEAD_DIM] for g in range(GQA_GROUP)], axis=0)


def _unstack_heads(stacked):
    return [stacked[g * ATT_BLOCK:(g + 1) * ATT_BLOCK, :] for g in range(GQA_GROUP)]


def _fwd_attention(q, kpad, vpad, sinks, job=None):
    T = q.shape[0]
    nb = T // ATT_BLOCK

    def body(q_ref, k_ref, v_ref, sink_ref, o_ref, s_scr, p_scr):
        n = pl.program_id(0)
        start = pl.multiple_of(n * ATT_BLOCK, ATT_BLOCK)
        kw = k_ref[pl.ds(start, 2 * ATT_BLOCK), :]
        vw = v_ref[pl.ds(start, 2 * ATT_BLOCK), :]
        outs = []
        for kh in range(N_KV_HEADS):
            kk = kw[:, kh * HEAD_DIM:(kh + 1) * HEAD_DIM]
            vv = vw[:, kh * HEAD_DIM:(kh + 1) * HEAD_DIM]
            s_scr[...] = _dot_nt(_stack_heads(q_ref, kh), kk)
            for g in range(GQA_GROUP):
                h = kh * GQA_GROUP + g
                for row0 in range(0, ATT_BLOCK, ATT_SUB):
                    rows = pl.ds(g * ATT_BLOCK + row0, ATT_SUB)
                    relf, valid = _att_mask(n, row0)
                    pr, _ = _att_probs(s_scr[rows, :], relf, valid, _alibi_slope(h), sink_ref[0, h])
                    p_scr[rows, :] = pr.astype(BF16)
            outs += _unstack_heads(_dot(p_scr[...], vv))
        o_ref[...] = jnp.concatenate(outs, axis=1).astype(BF16)

    return _launch(
        body, name="fwd_attention", grid=(nb,),
        in_specs=[_row_spec(D_MODEL, ATT_BLOCK), _full_spec((T + ATT_BLOCK, KV_DIM)), _full_spec((T + ATT_BLOCK, KV_DIM)),
                  pl.BlockSpec(memory_space=pltpu.SMEM)],
        out_specs=[_row_spec(D_MODEL, ATT_BLOCK)],
        out_shape=[jax.ShapeDtypeStruct((T, D_MODEL), BF16)],
        scratch_shapes=[pltpu.VMEM((ATT_GROUP_ROWS, 2 * ATT_BLOCK), F32), pltpu.VMEM((ATT_GROUP_ROWS, 2 * ATT_BLOCK), BF16)],
        args=(q, kpad, vpad, sinks), job=job)


def _fwd_attn_out(attn, x, wo, g_post, g_ffn, job=None):
    T = x.shape[0]
    nt = T // ROW_TILE

    def body(a_ref, x_ref, wo_ref, gpost_ref, gffn_ref, y_ref, x1_ref, h2_ref):
        y = _dot(a_ref[...], wo_ref[...])
        y_ref[...] = y.astype(BF16)
        x1 = x_ref[...] + _rms(y, gpost_ref[...])
        x1_ref[...] = x1
        h2_ref[...] = _rms(x1, gffn_ref[...]).astype(BF16)

    return _launch(
        body, name="fwd_attn_out", grid=(nt,),
        in_specs=[_row_spec(D_MODEL), _row_spec(D_MODEL), _full_spec((D_MODEL, D_MODEL)), _vec_spec(), _vec_spec()],
        out_specs=[_row_spec(D_MODEL)] * 3,
        out_shape=[jax.ShapeDtypeStruct((T, D_MODEL), BF16), jax.ShapeDtypeStruct((T, D_MODEL), F32),
                   jax.ShapeDtypeStruct((T, D_MODEL), BF16)],
        args=(attn, x, wo, g_post, g_ffn), job=job)


def _bwd_ple(layer, dx3, x2, z, pe, h3, p, f, wgate, g_ple_post, g_ple, g_post_ffn, job=None):
    T = x2.shape[0]
    tm = ROW_TILE
    nt = T // tm

    def body(dx3_ref, x2_ref, z_ref, pe_ref, h3_ref, p_ref, f_ref, wg_ref, gpp_ref, gp_ref, gpf_ref,
             dx2_ref, df_ref, dwg_ref, dwp_ref, dgpp_ref, dgp_ref, dgpf_ref, acc_g, acc_p):
        i = pl.program_id(0)
        first = i == 0
        dx3v = dx3_ref[...]
        gate = _sigmoid(z_ref[...].astype(F32))
        pev = pe_ref[...].astype(F32)
        de, dgpp = _rms_bwd(pev * gate, gpp_ref[...], dx3v)
        dpe = (de * gate).astype(BF16)
        dz = (de * pev * gate * (1.0 - gate)).astype(BF16)
        _acc(acc_p, _dot_tn(p_ref[...].astype(BF16), dpe), first)
        _acc(acc_g, _dot_tn(h3_ref[...], dz), first)
        dh3 = _dot_nt(dz, wg_ref[...])
        dxn, dgp = _rms_bwd(x2_ref[...], gp_ref[...], dh3)
        dx2 = dx3v + dxn
        dx2_ref[...] = dx2
        df, dgpf = _rms_bwd(f_ref[...].astype(F32), gpf_ref[...], dx2)
        df_ref[...] = df.astype(BF16)
        _acc(dgpp_ref, dgpp, first)
        _acc(dgp_ref, dgp, first)
        _acc(dgpf_ref, dgpf, first)

        @pl.when(i == nt - 1)
        def _():
            dwg_ref[...] = acc_g[...].astype(BF16)
            dwp_ref[...] = acc_p[...].astype(BF16)

    return _launch(
        body, name=f"bwd_ple{layer}", grid=(nt,),
        in_specs=[_row_spec(D_MODEL)] * 5 + [_row_spec(PLE_DIM), _row_spec(D_MODEL), _full_spec((D_MODEL, D_MODEL)),
                  _vec_spec(), _vec_spec(), _vec_spec()],
        out_specs=[_row_spec(D_MODEL), _row_spec(D_MODEL), _full_spec((D_MODEL, D_MODEL)), _full_spec((PLE_DIM, D_MODEL)),
                   _vec_spec(), _vec_spec(), _vec_spec()],
        out_shape=[jax.ShapeDtypeStruct((T, D_MODEL), F32), jax.ShapeDtypeStruct((T, D_MODEL), BF16),
                   jax.ShapeDtypeStruct((D_MODEL, D_MODEL), BF16), jax.ShapeDtypeStruct((PLE_DIM, D_MODEL), BF16)]
                  + [jax.ShapeDtypeStruct((1, D_MODEL), F32)] * 3,
        scratch_shapes=[pltpu.VMEM((D_MODEL, D_MODEL), F32), pltpu.VMEM((PLE_DIM, D_MODEL), F32)],
        args=(dx3, x2, z, pe, h3, p, f, wgate, g_ple_post, g_ple, g_post_ffn), vmem=VMEM_BIG, job=job)


def _ple_loss_bwd(layer, x2, h3, p, f, target, wgate, wproj, g_ple_post, g_ple, g_post_ffn, job=None):
    T = x2.shape[0]
    tm = ROW_TILE
    nt = T // tm

    def body(x2_ref, h3_ref, p_ref, f_ref, tgt_ref, wg_ref, wp_ref, gpp_ref, gp_ref, gpf_ref,
             dx2_ref, df_ref, dwg_ref, dwp_ref, dgpp_ref, dgp_ref, dgpf_ref, loss_ref, acc_g, acc_p):
        i = pl.program_id(0)
        first = i == 0
        h3 = h3_ref[...]
        pb = p_ref[...].astype(BF16)
        x2v = x2_ref[...]
        gate = _sigmoid(_dot(h3, wg_ref[...]))
        pev = _dot(pb, wp_ref[...])
        e = pev * gate
        err = x2v + _rms(e, gpp_ref[...]) - tgt_ref[...]
        _acc(loss_ref, 0.5 * jnp.sum(jnp.mean(err * err, axis=-1, keepdims=True), axis=0, keepdims=True), first)
        dx3v = err * (1.0 / D_MODEL)
        de, dgpp = _rms_bwd(e, gpp_ref[...], dx3v)
        dpe = (de * gate).astype(BF16)
        dz = (de * pev * gate * (1.0 - gate)).astype(BF16)
        _acc(acc_p, _dot_tn(pb, dpe), first)
        _acc(acc_g, _dot_tn(h3, dz), first)
        dxn, dgp = _rms_bwd(x2v, gp_ref[...], _dot_nt(dz, wg_ref[...]))
        dx2 = dx3v + dxn
        dx2_ref[...] = dx2
        df, dgpf = _rms_bwd(f_ref[...].astype(F32), gpf_ref[...], dx2)
        df_ref[...] = df.astype(BF16)
        _acc(dgpp_ref, dgpp, first)
        _acc(dgp_ref, dgp, first)
        _acc(dgpf_ref, dgpf, first)

        @pl.when(i == nt - 1)
        def _():
            dwg_ref[...] = acc_g[...].astype(BF16)
            dwp_ref[...] = acc_p[...].astype(BF16)

    return _launch(
        body, name=f"ple_loss_bwd{layer}", grid=(nt,),
        in_specs=[_row_spec(D_MODEL), _row_spec(D_MODEL), _row_spec(PLE_DIM), _row_spec(D_MODEL), _row_spec(D_MODEL),
                  _full_spec((D_MODEL, D_MODEL)), _full_spec((PLE_DIM, D_MODEL)), _vec_spec(), _vec_spec(), _vec_spec()],
        out_specs=[_row_spec(D_MODEL), _row_spec(D_MODEL), _full_spec((D_MODEL, D_MODEL)), _full_spec((PLE_DIM, D_MODEL)),
                   _vec_spec(), _vec_spec(), _vec_spec(), _full_spec((1, 1))],
        out_shape=[jax.ShapeDtypeStruct((T, D_MODEL), F32), jax.ShapeDtypeStruct((T, D_MODEL), BF16),
                   jax.ShapeDtypeStruct((D_MODEL, D_MODEL), BF16), jax.ShapeDtypeStruct((PLE_DIM, D_MODEL), BF16)]
                  + [jax.ShapeDtypeStruct((1, D_MODEL), F32)] * 3 + [jax.ShapeDtypeStruct((1, 1), F32)],
        scratch_shapes=[pltpu.VMEM((D_MODEL, D_MODEL), F32), pltpu.VMEM((PLE_DIM, D_MODEL), F32)],
        args=(x2, h3, p, f, target, wgate, wproj, g_ple_post, g_ple, g_post_ffn), vmem=VMEM_BIG, job=job)


def _bwd_ffn_act(layer, df, gs, us, wgu, wd, job=None):
    T = df.shape[0]
    tm = min(FFN_ROW_TILE, T)
    nt = T // tm
    sub = tm // FFN_SUB_TILES
    last = FF_CHUNKS - 1
    n_gu, n_wd = len(wgu), len(wd)
    wd_cols = _column_ranges(wd)

    def body(df_ref, gs_ref, us_ref, *refs):
        wgu_refs, wd_refs = refs[:n_gu], refs[n_gu:n_gu + n_wd]
        dh_ref, dg_ref, du_ref, a_ref, acc_h = refs[n_gu + n_wd:]
        k = pl.program_id(0)
        i = pl.program_id(1)
        rows = pl.ds(pl.multiple_of(i * tm, tm), tm)
        dhs = []
        for s in range(FFN_SUB_TILES):
            r = pl.ds(s * sub, sub)
            g = gs_ref[r, :].astype(F32)
            u = us_ref[r, :].astype(F32)
            sg = _sigmoid(g)
            silu = g * sg
            a_ref[r, :] = (silu * u).astype(BF16)
            da = _add_all([_dot_nt(df_ref[r, c0:c1], w[...]) for (c0, c1), w in zip(wd_cols, wd_refs)])
            dg = (da * u * (sg * (1.0 + g * (1.0 - sg)))).astype(BF16)
            du = (da * silu).astype(BF16)
            dg_ref[r, :] = dg
            du_ref[r, :] = du
            dhs.append(jnp.concatenate([_dot(dg, w[0]) + _dot(du, w[1]) for w in wgu_refs], axis=1))
        dh = jnp.concatenate(dhs, axis=0)

        @pl.when(k == 0)
        def _():
            acc_h[rows, :] = dh

        @pl.when(jnp.logical_and(k > 0, k < last))
        def _():
            acc_h[rows, :] += dh

        @pl.when(k == last)
        def _():
            dh_ref[...] = acc_h[rows, :] + dh

    chunk_rows = pl.BlockSpec((None, tm, FF_BLOCK), lambda k, i: (k, i, 0))
    saved = jax.ShapeDtypeStruct((FF_CHUNKS, T, FF_BLOCK), BF16)
    return _launch(
        body, name=f"bwd_ffn_act{layer}", grid=(FF_CHUNKS, nt),
        in_specs=[pl.BlockSpec((tm, D_MODEL), lambda k, i: (i, 0)), chunk_rows, chunk_rows]
                 + [pl.BlockSpec((None, 2, FF_BLOCK, w.shape[-1]), lambda k, i: (k, 0, 0, 0)) for w in wgu]
                 + [pl.BlockSpec((FF_BLOCK, w.shape[-1]), lambda k, i: (k, 0)) for w in wd],
        out_specs=[pl.BlockSpec((tm, D_MODEL), lambda k, i: (jnp.where(k == last, i, 0), 0)),
                   chunk_rows, chunk_rows, chunk_rows],
        out_shape=[jax.ShapeDtypeStruct((T, D_MODEL), F32), saved, saved, saved],
        scratch_shapes=[pltpu.VMEM((T, D_MODEL), F32)],
        args=(df, gs, us, *wgu, *wd), vmem=VMEM_BIG, job=job)


def _bwd_ffn_dw(layer, q, h2, df, dg, du, a, job=None):
    T = h2.shape[0]

    def body(h_ref, df_ref, dg_ref, du_ref, a_ref, dgu_ref, dwd_ref):
        h = h_ref[...]
        dgu_ref[0] = _dot_tn(dg_ref[...], h).astype(BF16)
        dgu_ref[1] = _dot_tn(du_ref[...], h).astype(BF16)
        dwd_ref[...] = _dot_tn(a_ref[...], df_ref[...]).astype(BF16)

    cols = pl.BlockSpec((T, FF_PART), lambda k: (0, q))
    chunk = pl.BlockSpec((None, T, FF_BLOCK), lambda k: (k, 0, 0))
    return _launch(
        body, name=f"bwd_ffn_dw{layer}_{q}", grid=(FF_CHUNKS,),
        in_specs=[cols, cols, chunk, chunk, chunk],
        out_specs=[pl.BlockSpec((None, 2, FF_BLOCK, FF_PART), lambda k: (k, 0, 0, 0)),
                   pl.BlockSpec((FF_BLOCK, FF_PART), lambda k: (k, 0))],
        out_shape=[jax.ShapeDtypeStruct((FF_CHUNKS, 2, FF_BLOCK, FF_PART), BF16),
                   jax.ShapeDtypeStruct((D_FF, FF_PART), BF16)],
        args=(h2, df, dg, du, a), vmem=VMEM_BIG, job=job)


def _bwd_attn_out(dx2, dh2, x1, y, attn, wo, g_ffn, g_post, job=None):
    T = x1.shape[0]
    nt = T // ROW_TILE

    def body(dx2_ref, dh2_ref, x1_ref, y_ref, a_ref, wo_ref, gffn_ref, gpost_ref,
             dx1_ref, da_ref, dwo_ref, dgf_ref, dgp_ref, acc):
        i = pl.program_id(0)
        first = i == 0
        dxn, dgf = _rms_bwd(x1_ref[...], gffn_ref[...], dh2_ref[...])
        dx1 = dx2_ref[...] + dxn
        dx1_ref[...] = dx1
        dy, dgp = _rms_bwd(y_ref[...].astype(F32), gpost_ref[...], dx1)
        dyb = dy.astype(BF16)
        da_ref[...] = _dot_nt(dyb, wo_ref[...]).astype(BF16)
        _acc(acc, _dot_tn(a_ref[...], dyb), first)
        _acc(dgf_ref, dgf, first)
        _acc(dgp_ref, dgp, first)

        @pl.when(i == nt - 1)
        def _():
            dwo_ref[...] = acc[...].astype(BF16)

    return _launch(
        body, name="bwd_attn_out", grid=(nt,),
        in_specs=[_row_spec(D_MODEL)] * 5 + [_full_spec((D_MODEL, D_MODEL)), _vec_spec(), _vec_spec()],
        out_specs=[_row_spec(D_MODEL), _row_spec(D_MODEL), _full_spec((D_MODEL, D_MODEL)), _vec_spec(), _vec_spec()],
        out_shape=[jax.ShapeDtypeStruct((T, D_MODEL), F32), jax.ShapeDtypeStruct((T, D_MODEL), BF16),
                   jax.ShapeDtypeStruct((D_MODEL, D_MODEL), BF16)] + [jax.ShapeDtypeStruct((1, D_MODEL), F32)] * 2,
        scratch_shapes=[pltpu.VMEM((D_MODEL, D_MODEL), F32)],
        args=(dx2, dh2, x1, y, attn, wo, g_ffn, g_post), job=job)


def _bwd_attention(q, dattn, kpad, vpad, sinks, job=None):
    T = q.shape[0]
    nb = T // ATT_BLOCK

    def body(q_ref, do_ref, k_ref, v_ref, sink_ref, dq_ref, dk_ref, dv_ref, ds_ref, s_scr, dp_scr, p_scr, dsb_scr):
        n = pl.program_id(0)

        @pl.when(n == 0)
        def _():
            dk_ref[...] = jnp.zeros_like(dk_ref)
            dv_ref[...] = jnp.zeros_like(dv_ref)
            ds_ref[...] = jnp.zeros_like(ds_ref)

        start = pl.multiple_of(n * ATT_BLOCK, ATT_BLOCK)
        win = pl.ds(start, 2 * ATT_BLOCK)
        kw = k_ref[win, :]
        vw = v_ref[win, :]
        lane = lax.broadcasted_iota(jnp.int32, (1, ATT_BLOCK), 1)
        dsink = jnp.zeros((1, ATT_BLOCK), F32)
        dqs, dks, dvs = [], [], []
        for kh in range(N_KV_HEADS):
            kk = kw[:, kh * HEAD_DIM:(kh + 1) * HEAD_DIM]
            vv = vw[:, kh * HEAD_DIM:(kh + 1) * HEAD_DIM]
            qs = _stack_heads(q_ref, kh)
            dos = _stack_heads(do_ref, kh)
            s_scr[...] = _dot_nt(qs, kk)
            dp_scr[...] = _dot_nt(dos, vv)
            for g in range(GQA_GROUP):
                h = kh * GQA_GROUP + g
                dsink_h = jnp.zeros((1, 1), F32)
                for row0 in range(0, ATT_BLOCK, ATT_SUB):
                    rows = pl.ds(g * ATT_BLOCK + row0, ATT_SUB)
                    relf, valid = _att_mask(n, row0)
                    pr, ps = _att_probs(s_scr[rows, :], relf, valid, _alibi_slope(h), sink_ref[0, h])
                    dp = dp_scr[rows, :]
                    delta = jnp.sum(pr * dp, axis=-1, keepdims=True)
                    dsb_scr[rows, :] = (pr * (dp - delta) * ATT_SCALE).astype(BF16)
                    p_scr[rows, :] = pr.astype(BF16)
                    dsink_h = dsink_h - jnp.sum(ps * delta, axis=0, keepdims=True)
                dsink = dsink + jnp.where(lane == h, dsink_h, 0.0)
            dsb = dsb_scr[...]
            dqs += _unstack_heads(_dot(dsb, kk))
            dks.append(_dot_tn(dsb, qs))
            dvs.append(_dot_tn(p_scr[...], dos))
        dq_ref[...] = jnp.concatenate(dqs, axis=1).astype(BF16)
        dk_ref[win, :] += jnp.concatenate(dks, axis=1)
        dv_ref[win, :] += jnp.concatenate(dvs, axis=1)
        ds_ref[...] += dsink

    return _launch(
        body, name="bwd_attention", grid=(nb,),
        in_specs=[_row_spec(D_MODEL, ATT_BLOCK), _row_spec(D_MODEL, ATT_BLOCK), _full_spec((T + ATT_BLOCK, KV_DIM)),
                  _full_spec((T + ATT_BLOCK, KV_DIM)), pl.BlockSpec(memory_space=pltpu.SMEM)],
        out_specs=[_row_spec(D_MODEL, ATT_BLOCK), _full_spec((T + ATT_BLOCK, KV_DIM)), _full_spec((T + ATT_BLOCK, KV_DIM)),
                   _full_spec((1, ATT_BLOCK))],
        out_shape=[jax.ShapeDtypeStruct((T, D_MODEL), BF16), jax.ShapeDtypeStruct((T + ATT_BLOCK, KV_DIM), F32),
                   jax.ShapeDtypeStruct((T + ATT_BLOCK, KV_DIM), F32), jax.ShapeDtypeStruct((1, ATT_BLOCK), F32)],
        scratch_shapes=[pltpu.VMEM((ATT_GROUP_ROWS, 2 * ATT_BLOCK), F32)] * 2
                       + [pltpu.VMEM((ATT_GROUP_ROWS, 2 * ATT_BLOCK), BF16)] * 2,
        args=(q, dattn, kpad, vpad, sinks), vmem=VMEM_BIG, job=job)


def _bwd_qkv(dxres, dq, dkv, x3, h1, hk, wq, wkv, g_mix, g_kv, job=None):
    T = x3.shape[0]
    nt = T // ROW_TILE

    def body(dxr_ref, dq_ref, dkv_ref, x_ref, h1_ref, hk_ref, wq_ref, wkv_ref, gmix_ref, gkv_ref,
             dx_ref, dwq_ref, dwkv_ref, dgm_ref, dgk_ref, acc_q, acc_kv):
        i = pl.program_id(0)
        first = i == 0
        dqv = dq_ref[...]
        dkvv = dkv_ref[...]
        xv = x_ref[...]
        d1, dgm = _rms_bwd(xv, gmix_ref[...], _dot_nt(dqv, wq_ref[...]))
        d2, dgk = _rms_bwd(xv, gkv_ref[...], _dot_nt(dkvv, wkv_ref[...]))
        dx_ref[...] = dxr_ref[...] + d1 + d2
        _acc(acc_q, _dot_tn(h1_ref[...], dqv), first)
        _acc(acc_kv, _dot_tn(hk_ref[...], dkvv), first)
        _acc(dgm_ref, dgm, first)
        _acc(dgk_ref, dgk, first)

        @pl.when(i == nt - 1)
        def _():
            dwq_ref[...] = acc_q[...].astype(BF16)
            dwkv_ref[...] = acc_kv[...].astype(BF16)

    return _launch(
        body, name="bwd_qkv", grid=(nt,),
        in_specs=[_row_spec(D_MODEL), _row_spec(D_MODEL), _row_spec(2 * KV_DIM), _row_spec(D_MODEL), _row_spec(D_MODEL),
                  _row_spec(D_MODEL), _full_spec((D_MODEL, D_MODEL)), _full_spec((D_MODEL, 2 * KV_DIM)), _vec_spec(),
                  _vec_spec()],
        out_specs=[_row_spec(D_MODEL), _full_spec((D_MODEL, D_MODEL)), _full_spec((D_MODEL, 2 * KV_DIM)), _vec_spec(),
                   _vec_spec()],
        out_shape=[jax.ShapeDtypeStruct((T, D_MODEL), F32), jax.ShapeDtypeStruct((D_MODEL, D_MODEL), BF16),
                   jax.ShapeDtypeStruct((D_MODEL, 2 * KV_DIM), BF16)] + [jax.ShapeDtypeStruct((1, D_MODEL), F32)] * 2,
        scratch_shapes=[pltpu.VMEM((D_MODEL, D_MODEL), F32), pltpu.VMEM((D_MODEL, 2 * KV_DIM), F32)],
        args=(dxres, dq, dkv, x3, h1, hk, wq, wkv, g_mix, g_kv), job=job)


def _bwd_pool_mixer(dx2, dh2, x1, x, yraw, d, wp, scale, g_ffn, g_post, g_pre, job=None):
    T = x.shape[0]
    tm = ROW_TILE
    nt = T // tm

    def body(dx2_ref, dh2_ref, x1_ref, x_ref, yraw_ref, d_ref, wp_ref, sc_ref, gffn_ref, gpost_ref, gpre_ref,
             dx_ref, dwp_ref, dsc_ref, dgf_ref, dgp_ref, dgm_ref, carry, acc):
        i = pl.program_id(0)
        first = i == 0
        tile = nt - 1 - i

        @pl.when(first)
        def _():
            carry[...] = jnp.zeros_like(carry)

        dxn, dgf = _rms_bwd(x1_ref[...], gffn_ref[...], dh2_ref[...])
        dx1 = dx2_ref[...] + dxn
        yraw = yraw_ref[...].astype(F32)
        sc = sc_ref[...]
        dy, dgp = _rms_bwd(yraw * sc, gpost_ref[...], dx1)
        dsc = jnp.sum(dy * yraw, axis=0, keepdims=True)
        dyb = (dy * sc).astype(BF16)
        dv = d_ref[...]
        dds = []
        for g in range(N_POOL_GROUPS):
            cols = slice(g * POOL_GROUP, (g + 1) * POOL_GROUP)
            dds.append(_dot_nt(dyb[:, cols], wp_ref[g]))
            _acc(acc.at[g], _dot_tn(dv[:, cols], dyb[:, cols]), first)
        dd = jnp.concatenate(dds, axis=1)
        e = dd / _pool_counts(tile * tm, tm)
        ext = jnp.concatenate([e, carry[...]], axis=0)
        carry[...] = e[:POOL_HALO, :]
        sums = _window_sums(ext, lambda k: tm + POOL_HALO - k)[:tm, :]
        dxm, dgm = _rms_bwd(x_ref[...], gpre_ref[...], sums - dd)
        dx_ref[...] = dx1 + dxm
        _acc(dsc_ref, dsc, first)
        _acc(dgf_ref, dgf, first)
        _acc(dgp_ref, dgp, first)
        _acc(dgm_ref, dgm, first)

        @pl.when(i == nt - 1)
        def _():
            dwp_ref[...] = acc[...].astype(BF16)

    rev = pl.BlockSpec((tm, D_MODEL), lambda i: (nt - 1 - i, 0))
    return _launch(
        body, name="bwd_pool_mixer", grid=(nt,),
        in_specs=[rev] * 6 + [_full_spec((N_POOL_GROUPS, POOL_GROUP, POOL_GROUP))] + [_vec_spec()] * 4,
        out_specs=[rev, _full_spec((N_POOL_GROUPS, POOL_GROUP, POOL_GROUP))] + [_vec_spec()] * 4,
        out_shape=[jax.ShapeDtypeStruct((T, D_MODEL), F32),
                   jax.ShapeDtypeStruct((N_POOL_GROUPS, POOL_GROUP, POOL_GROUP), BF16)]
                  + [jax.ShapeDtypeStruct((1, D_MODEL), F32)] * 4,
        scratch_shapes=[pltpu.VMEM((POOL_HALO, D_MODEL), F32), pltpu.VMEM((N_POOL_GROUPS, POOL_GROUP, POOL_GROUP), F32)],
        args=(dx2, dh2, x1, x, yraw, d, wp, scale, g_ffn, g_post, g_pre), job=job)


def _my_place():
    return lax.axis_index("x"), lax.axis_index("y"), lax.axis_index("c")


def _dev_index(px, py, pc):
    return 4 * px + 2 * py + pc


def _peer_by_relation(r):
    x, y, c = _my_place()
    return (x ^ ((r >> 2) & 1), y ^ ((r >> 1) & 1), c ^ (r & 1))


def _slot_pool(ref, j):
    return ref.at[:, pl.ds(pl.multiple_of(j * 32, 32), 32), :]


def _slot_scale(ref, j):
    return ref.at[:, pl.ds(pl.multiple_of(j * 128, 128), 128)]


def _slot_rows128(ref, j):
    return ref.at[pl.ds(pl.multiple_of(j * 128, 128), 128), :]


def _slot_gu(ref, j):
    return ref.at[j % FF_CHUNKS, j // FF_CHUNKS]


def _slot_wd(ref, j):
    return ref.at[pl.ds(pl.multiple_of(j * WD_ROWS, 16), WD_ROWS), :]


def _slot_cols128(ref, j):
    return ref.at[:, pl.ds(pl.multiple_of(j * 128, 128), 128)]


_GATHERED = {
    "pool": ((N_POOL_GROUPS, POOL_GROUP, POOL_GROUP), BF16, _slot_pool),
    "scale": ((1, D_MODEL), F32, _slot_scale),
    "kv": ((D_MODEL, 2 * KV_DIM), BF16, _slot_rows128),
    "q": ((D_MODEL, D_MODEL), BF16, _slot_rows128),
    "o": ((D_MODEL, D_MODEL), BF16, _slot_rows128),
    "gu": ((FF_CHUNKS, 2, FF_BLOCK, D_MODEL), BF16, _slot_gu),
    "wd": ((D_FF, D_MODEL), BF16, _slot_wd),
    "guh": ((FF_CHUNKS, 2, FF_BLOCK, D_MODEL // 2), BF16, _slot_gu),
    "wdh": ((D_FF, D_MODEL // 2), BF16, _slot_wd),
    "gate": ((D_MODEL, D_MODEL), BF16, _slot_rows128),
    "proj": ((PLE_DIM, D_MODEL), BF16, _slot_cols128),
}


def _no_compute():
    pass


class _AllGather:
    peers = ("sibling", "x", "y")

    def __init__(self, names, shards):
        self.kinds = [_GATHERED[n.rstrip("01_")] for n in names]
        self.args = [shards[n] for n in names]
        self.out_shape = [jax.ShapeDtypeStruct(shape, dtype) for shape, dtype, _ in self.kinds]
        n = len(names)
        self.scratch = [pltpu.SemaphoreType.DMA((n, 7)), pltpu.SemaphoreType.DMA((n, 7)), pltpu.SemaphoreType.DMA((n,))]

    def _plan(self, srcs, outs, sems):
        send_sems, recv_sems, local_sems = sems
        x, y, c = _my_place()

        def slot(t, dev):
            return self.kinds[t][2](outs[t], _dev_index(*dev))

        def copy(t, k, block, to, src=None):
            return pltpu.make_async_remote_copy(
                src_ref=slot(t, block) if src is None else src, dst_ref=slot(t, block),
                send_sem=send_sems.at[t, k], recv_sem=recv_sems.at[t, k], device_id=to, device_id_type=MESH)

        return types.SimpleNamespace(
            copy=copy, core=c, me=(x, y, c), sibling=(x, y, 1 - c),
            x_chip=(1 - x, y), y_chip=(x, 1 - y), far_chip=(1 - x, 1 - y),
            via=(x ^ (1 - c), y ^ c),
            onto=(x ^ c, y ^ (1 - c)),
            k_via=1 + c, k_onto=2 - c,
            local=[pltpu.make_async_copy(srcs[t], slot(t, (x, y, c)), local_sems.at[t]) for t in range(len(srcs))])

    def start(self, srcs, outs, sems):
        p = self._plan(srcs, outs, sems)
        for cp in p.local:
            cp.start()
        for t in range(len(srcs)):
            p.copy(t, 0, p.me, p.sibling, src=srcs[t]).start()
            p.copy(t, 1, p.me, (*p.x_chip, p.core), src=srcs[t]).start()
            p.copy(t, 2, p.me, (*p.y_chip, p.core), src=srcs[t]).start()

    def mid(self, srcs, outs, sems):
        p = self._plan(srcs, outs, sems)
        for t in range(len(srcs)):
            block = (*p.via, p.core)
            p.copy(t, p.k_via, block, p.me).wait_recv()
            p.copy(t, 3, block, (*p.onto, p.core)).start()
            p.copy(t, 3 + p.k_via, block, p.sibling).start()

    def finish(self, srcs, outs, sems):
        p = self._plan(srcs, outs, sems)
        n = len(srcs)
        for t in range(n):
            block = (*p.onto, p.core)
            p.copy(t, p.k_onto, block, p.me).wait_recv()
            p.copy(t, 3 + p.k_onto, block, p.sibling).start()
        for t in range(n):
            block = (*p.far_chip, p.core)
            p.copy(t, 3, block, p.me).wait_recv()
            p.copy(t, 6, block, p.sibling).start()
        other = 1 - p.core
        for t in range(n):
            p.copy(t, 0, (*p.me[:2], other), p.me).wait_recv()
            for k, chip in ((4, p.x_chip), (5, p.y_chip), (6, p.far_chip)):
                p.copy(t, k, (*chip, other), p.me).wait_recv()
            for k in range(7):
                p.copy(t, k, p.me, p.sibling).wait_send()
        for cp in p.local:
            cp.wait()


def _jobs_only(name, job=None):
    return _launch(_no_compute, name=name, grid=(), in_specs=[], out_specs=[], out_shape=[], args=(), job=job)


def _all_gather_only(name, names, shards):
    return _launch(_no_compute, name=name, grid=(), in_specs=[], out_specs=[], out_shape=[], args=(),
                   job=_AllGather(names, shards))[1]


def _block_pool(ref, j):
    return ref.at[:, pl.ds(pl.multiple_of(j * 32, 32), 32), :]


def _block_rows128(ref, j):
    return ref.at[pl.ds(pl.multiple_of(j * 128, 128), 128), :]


def _block_gu(ref, j):
    return ref.at[j % FF_CHUNKS, j // FF_CHUNKS]


def _block_wd(ref, j):
    return ref.at[pl.ds(pl.multiple_of(j * WD_ROWS, 16), WD_ROWS), :]


def _block_cols128(ref, j):
    return ref.at[:, pl.ds(pl.multiple_of(j * 128, 128), 128)]


_SCATTERED = {
    "pool": ((N_POOL_GROUPS, 32, POOL_GROUP), _block_pool),
    "kv": ((128, 2 * KV_DIM), _block_rows128),
    "q": ((128, D_MODEL), _block_rows128),
    "o": ((128, D_MODEL), _block_rows128),
    "gu": ((FF_BLOCK, FF_PART), _block_gu),
    "wd": ((WD_ROWS, FF_PART), _block_wd),
    "gate": ((128, D_MODEL), _block_rows128),
    "proj": ((PLE_DIM, 128), _block_cols128),
}


class _SiblingSwap:
    peers = ("sibling",)

    def __init__(self, pieces):
        self.kinds = [_SCATTERED[kind] for kind, _ in pieces]
        self.args = [g for _, g in pieces]
        self.out_shape = [jax.ShapeDtypeStruct((N_CHIPS, *block), BF16) for block, _ in self.kinds]
        n = len(pieces)
        self.scratch = [pltpu.SemaphoreType.DMA((n, N_CHIPS)), pltpu.SemaphoreType.DMA((n, N_CHIPS))]

    def _copies(self, srcs, outs, sems):
        send_sems, recv_sems = sems
        x, y, c = _my_place()
        return [pltpu.make_async_remote_copy(
            src_ref=block(srcs[t], 2 * ch + 1 - c), dst_ref=outs[t].at[ch], send_sem=send_sems.at[t, ch],
            recv_sem=recv_sems.at[t, ch], device_id=(x, y, 1 - c), device_id_type=MESH)
            for t, (_, block) in enumerate(self.kinds) for ch in range(N_CHIPS)]

    def start(self, srcs, outs, sems):
        for cp in self._copies(srcs, outs, sems):
            cp.start()

    def finish(self, srcs, outs, sems):
        for cp in self._copies(srcs, outs, sems):
            cp.wait()


class _ChipScatter:
    N_BUFS = 4
    peers = ("x", "y")

    def __init__(self, pieces):
        self.kinds = [_SCATTERED[kind] for kind, _, _ in pieces]
        self.n = n = len(pieces)
        self.args = [g for _, g, _ in pieces] + [s for _, _, s in pieces]
        self.out_shape = [jax.ShapeDtypeStruct((2, *block), BF16) for block, _ in self.kinds]
        self.scratch = []
        for block, _ in self.kinds:
            self.scratch += [pltpu.VMEM((N_CHIPS, *block), BF16)] * 3 + [pltpu.VMEM((2, *block), BF16)]
        dma = pltpu.SemaphoreType.DMA
        self.scratch += [dma((n, N_CHIPS + 1)), dma((n, 2)), dma((n, 2)), dma((n,)), dma((n,)), dma((n,))]

    def _plan(self, outs, scr):
        n = self.n
        first_send, first_recv, second_send, second_recv, keep_sems = scr[self.N_BUFS * n + 1:]
        x, y, c = _my_place()
        via = (x ^ (1 - c), y ^ c)
        onto = (x ^ c, y ^ (1 - c))
        index = lambda chip: 2 * chip[0] + chip[1]
        first, second, keep = [], [], []
        for t in range(n):
            total, inbox = scr[self.N_BUFS * t + 2], scr[self.N_BUFS * t + 3]
            for k, chip in enumerate((via, (1 - x, 1 - y))):
                first.append(pltpu.make_async_remote_copy(
                    src_ref=total.at[index(chip)], dst_ref=inbox.at[k], send_sem=first_send.at[t, k],
                    recv_sem=first_recv.at[t, k], device_id=(*via, c), device_id_type=MESH))
            second.append(pltpu.make_async_remote_copy(
                src_ref=total.at[index(onto)], dst_ref=outs[t].at[1], send_sem=second_send.at[t],
                recv_sem=second_recv.at[t], device_id=(*onto, c), device_id_type=MESH))
            keep.append(pltpu.make_async_copy(total.at[index((x, y))], outs[t].at[0], keep_sems.at[t]))
        return first, second, keep, index((x, y)), index(onto)

    def start(self, ins, outs, scr):
        n = self.n
        load_sems = scr[self.N_BUFS * n]
        c = lax.axis_index("c")
        loads = []
        for t, (_, block) in enumerate(self.kinds):
            mine, theirs = scr[self.N_BUFS * t], scr[self.N_BUFS * t + 1]
            loads += [pltpu.make_async_copy(block(ins[t], 2 * ch + c), mine.at[ch], load_sems.at[t, ch])
                      for ch in range(N_CHIPS)]
            loads.append(pltpu.make_async_copy(ins[n + t], theirs, load_sems.at[t, N_CHIPS]))
        for cp in loads:
            cp.start()
        for cp in loads:
            cp.wait()
        for t in range(n):
            mine, theirs, total = scr[self.N_BUFS * t:self.N_BUFS * t + 3]
            for ch in range(N_CHIPS):
                total[ch] = (mine[ch].astype(F32) + theirs[ch].astype(F32)).astype(BF16)
        for cp in self._plan(outs, scr)[0]:
            cp.start()

    def mid(self, ins, outs, scr):
        first, second, keep, me, onto = self._plan(outs, scr)
        for cp in first:
            cp.wait_recv()
        for t in range(self.n):
            total, inbox = scr[self.N_BUFS * t + 2], scr[self.N_BUFS * t + 3]
            for k, slot in enumerate((me, onto)):
                total[slot] = (total[slot].astype(F32) + inbox[k].astype(F32)).astype(BF16)
        for cp in second + keep:
            cp.start()

    def finish(self, ins, outs, scr):
        first, second, keep, _, _ = self._plan(outs, scr)
        for cp in first:
            cp.wait_send()
        for cp in second + keep:
            cp.wait()


class _Jobs:
    def __init__(self, *jobs):
        self.jobs = jobs
        together = {p for j in jobs for p in j.peers}
        self.peers = tuple(p for p in _PEER_SETS[0] if p in together)
        self.args = [a for j in jobs for a in j.args]
        self.out_shape = [o for j in jobs for o in j.out_shape]
        self.scratch = [s for j in jobs for s in j.scratch]

    def _split(self, refs, attr):
        at = 0
        for j in self.jobs:
            n = len(getattr(j, attr))
            yield refs[at:at + n]
            at += n

    def _each(self, ins, outs, scr):
        return zip(self.jobs, self._split(ins, "args"), self._split(outs, "out_shape"), self._split(scr, "scratch"))

    def start(self, ins, outs, scr):
        for j, i, o, s in self._each(ins, outs, scr):
            j.start(i, o, s)

    def mid(self, ins, outs, scr):
        for j, i, o, s in self._each(ins, outs, scr):
            if hasattr(j, "mid"):
                j.mid(i, o, s)

    def finish(self, ins, outs, scr):
        for j, i, o, s in self._each(ins, outs, scr):
            j.finish(i, o, s)

    def split_outputs(self, outs):
        return list(self._split(outs, "out_shape"))


def _adamw_math(w, g, m, v):
    m = ADAM_B1 * m + (1.0 - ADAM_B1) * g
    v = ADAM_B2 * v + (1.0 - ADAM_B2) * (g * g)
    m_hat = m / (1.0 - ADAM_B1 ** ADAM_STEP)
    v_hat = v / (1.0 - ADAM_B2 ** ADAM_STEP)
    delta = -ADAM_LR * (m_hat / (jnp.sqrt(v_hat) + ADAM_EPS) + ADAM_WD * w)
    return delta, m, v


def _adamw(name, w, m, v, landings, n_col_blocks=1, job=None):
    n_slots, r, c = landings[0].shape
    grid = (w.shape[0] // r, n_col_blocks)

    def body(w_ref, m_ref, v_ref, *rest):
        l_refs, (g_ref, d_ref, nm_ref, nv_ref) = rest[:len(landings)], rest[len(landings):]
        step = pl.program_id(0) * n_col_blocks + pl.program_id(1)
        for idx, l_ref in enumerate(l_refs):
            @pl.when(step == idx)
            def _(l_ref=l_ref):
                g = l_ref[0].astype(F32)
                for s in range(1, n_slots):
                    g = g + l_ref[s].astype(F32)
                g_ref[...] = g
                d_ref[...], nm_ref[...], nv_ref[...] = _adamw_math(w_ref[...], g, m_ref[...], v_ref[...])

    spec = pl.BlockSpec((r, c), lambda a, b: (a, b))
    return _launch(
        body, name=f"adamw_{name}", grid=grid,
        in_specs=[spec, spec, spec] + [_full_spec((n_slots, r, c))] * len(landings),
        out_specs=[spec] * 4, out_shape=[jax.ShapeDtypeStruct(w.shape, F32)] * 4,
        args=(w, m, v, *landings), vmem=VMEM_BIG, job=job)


_SMALL = (("pre_mix_g", SV_PRE_MIX, 2), ("post_mix_g", SV_POST_MIX, 2), ("pre_ffn_g", SV_PRE_FFN, 2),
          ("post_ffn_g", SV_POST_FFN, 2), ("ple_g", SV_PLE, 2), ("ple_post_g", SV_PLE_POST, 2), ("kv_g", SV_KV, 1),
          ("pool_scale", SV_POOL_SCALE, 1), ("sinks", SV_SINKS, 1))


def _small_all_reduce(part):
    def body(part_ref, total_ref, buf, send_sems, recv_sems):
        x, y, c = _my_place()
        me = _dev_index(x, y, c)
        buf[me] = part_ref[...]
        copies = [pltpu.make_async_remote_copy(
            src_ref=buf.at[me], dst_ref=buf.at[me], send_sem=send_sems.at[r - 1], recv_sem=recv_sems.at[r - 1],
            device_id=_peer_by_relation(r), device_id_type=MESH) for r in range(1, N_DEV)]
        for cp in copies:
            cp.start()
        for cp in copies:
            cp.wait()
        g = buf[0]
        for s in range(1, N_DEV):
            g = g + buf[s]
        total_ref[...] = g

    slab = jax.ShapeDtypeStruct((SV_ROWS, D_MODEL), F32)
    (total,), _ = _launch(
        body, name="small_all_reduce", grid=(1,), in_specs=[_full_spec(slab.shape)], out_specs=[_full_spec(slab.shape)],
        out_shape=[slab],
        scratch_shapes=[pltpu.VMEM((N_DEV, SV_ROWS, D_MODEL), F32), pltpu.SemaphoreType.DMA((N_DEV - 1,)),
                        pltpu.SemaphoreType.DMA((N_DEV - 1,))],
        args=(part,))
    return total


def _small_adamw(total, params):
    flat = [a for name, _, _ in _SMALL for a in params[name]]
    n_in = 1 + len(flat)

    def body(*refs):
        total, wmv = refs[0], refs[1:n_in]
        loss_ref, outs = refs[n_in], refs[n_in + 1:]
        me = _dev_index(*_my_place())
        loss_ref[...] = total[SV_LOSS:SV_LOSS + 1, 0:1]
        for idx, (name, row, n_rows) in enumerate(_SMALL):
            w_ref, m_ref, v_ref = wmv[3 * idx:3 * idx + 3]
            g_ref, d_ref, nm_ref, nv_ref = outs[4 * idx:4 * idx + 4]
            if name == "pool_scale":
                g = total[row:row + 1, pl.ds(pl.multiple_of(me * 128, 128), 128)]
            else:
                g = total[row:row + n_rows, 0:w_ref.shape[1]]
            g_ref[...] = g
            d_ref[...], nm_ref[...], nv_ref[...] = _adamw_math(w_ref[...], g, m_ref[...], v_ref[...])

    out_shape = [jax.ShapeDtypeStruct((1, 1), F32)]
    for name, _, _ in _SMALL:
        out_shape += [jax.ShapeDtypeStruct(params[name][0].shape, F32)] * 4
    res, _ = _launch(
        body, name="small_adamw", grid=(1,),
        in_specs=[_full_spec(a.shape) for a in (total, *flat)], out_specs=[_full_spec(s.shape) for s in out_shape],
        out_shape=out_shape, args=(total, *flat))
    return res[0], {name: res[1 + 4 * idx:5 + 4 * idx] for idx, (name, _, _) in enumerate(_SMALL)}


def _local_step(x, p, tgt, gains, sinks, shards, weights):
    row = lambda first_row, layer: _Gain(gains, first_row + layer)
    gather = lambda *names: _AllGather(names, shards)
    g_pre_mix, g_post_mix, g_pre_ffn, g_post_ffn = SV_PRE_MIX, SV_POST_MIX, SV_PRE_FFN, SV_POST_FFN
    g_ple, g_ple_post, g_kv = SV_PLE, SV_PLE_POST, _Gain(gains, SV_KV)

    wp, scale, wgu0 = _all_gather_only("gather_first", ("pool", "scale", "gu0"), shards)
    wgu0 = [wgu0]
    (x1_0, h2_0, yraw, dpool), wd0 = _fwd_pool_mixer(
        x, row(g_pre_mix, 0), wp, scale, row(g_post_mix, 0), row(g_pre_ffn, 0), job=gather("wd0"))
    (gs0, us0, f0, x2_0, h3_0), (wgate0, wproj0, wkv, wq, wgu1_a) = _fwd_ffn(
        0, h2_0, x1_0, wgu0, wd0, row(g_post_ffn, 0), row(g_ple, 0),
        job=gather("gate0", "proj0", "kv", "q", "guh1_0"))
    (x3_0, z0, pe0), (wo,) = _fwd_ple(0, x2_0, h3_0, p[0], wgate0, wproj0, row(g_ple_post, 0), job=gather("o"))
    (hk, h1, q, kv), (wd1_a,) = _fwd_qkv(x3_0, g_kv, row(g_pre_mix, 1), wkv, wq, job=gather("wdh1_0"))
    front = ((ATT_BLOCK, 0), (0, 0))
    kpad = jnp.pad(kv[:, :KV_DIM], front)
    vpad = jnp.pad(kv[:, KV_DIM:], front)
    (attn,), (wgu1_b,) = _fwd_attention(q, kpad, vpad, sinks, job=gather("guh1_1"))
    (y1, x1_1, h2_1), (wd1_b,) = _fwd_attn_out(attn, x3_0, wo, row(g_post_mix, 1), row(g_pre_ffn, 1),
                                               job=gather("wdh1_1"))
    wgu1, wd1 = [wgu1_a, wgu1_b], [wd1_a, wd1_b]
    (gs1, us1, f1, x2_1, h3_1), (wgate1, wproj1) = _fwd_ffn(
        1, h2_1, x1_1, wgu1, wd1, row(g_post_ffn, 1), row(g_ple, 1), job=gather("gate1", "proj1"))

    produced, swapped, landed = {}, {}, {}

    def kind_of(name):
        return name.rstrip("0123_")

    def carry(swap=(), spread=()):
        jobs = []
        if swap:
            jobs.append(_SiblingSwap([(kind_of(n), produced[n]) for n in swap]))
        if spread:
            jobs.append(_ChipScatter([(kind_of(n), produced[n], swapped[n]) for n in spread]))
        return _Jobs(*jobs)

    def carried(jobs, outs, swap=(), spread=()):
        parts = jobs.split_outputs(outs)
        if swap:
            swapped.update(zip(swap, parts[0]))
        if spread:
            landed.update(zip(spread, parts[-1]))

    def hosted(call, *args, swap=(), spread=()):
        jobs = carry(swap, spread)
        outs, job_outs = call(*args, job=jobs)
        carried(jobs, job_outs, swap, spread)
        return outs

    def ffn_weight_grads(layer, h2, df, dg, du, a, hosts):
        for qtr in range(FF_PARTS):
            dgu, dwd = hosted(_bwd_ffn_dw, layer, qtr, h2, df, dg, du, a, **hosts[qtr])
            produced[f"gu{layer}_{qtr}"], produced[f"wd{layer}_{qtr}"] = dgu, dwd

    ffn_q = lambda layer, qtr: (f"gu{layer}_{qtr}", f"wd{layer}_{qtr}")

    dx2_1, df1, produced["gate1"], produced["proj1"], dg_ple_post1, dg_ple1, dg_post_ffn1, loss = hosted(
        _ple_loss_bwd, 1, x2_1, h3_1, p[1], f1, tgt, wgate1, wproj1, row(g_ple_post, 1), row(g_ple, 1),
        row(g_post_ffn, 1))
    dh2_1, dg1, du1, a1 = hosted(_bwd_ffn_act, 1, df1, gs1, us1, wgu1, wd1, swap=("gate1", "proj1"))
    ffn_weight_grads(1, h2_1, df1, dg1, du1, a1, [dict(spread=("gate1", "proj1")), dict(swap=ffn_q(1, 0))])
    dx1_1, dattn, produced["o"], dg_pre_ffn1, dg_post_mix1 = hosted(
        _bwd_attn_out, dx2_1, dh2_1, x1_1, y1, attn, wo, row(g_pre_ffn, 1), row(g_post_mix, 1), swap=ffn_q(1, 1))
    dq, dkpad, dvpad, dsinks = hosted(_bwd_attention, q, dattn, kpad, vpad, sinks, spread=ffn_q(1, 0))
    dkv = jnp.concatenate([dkpad[ATT_BLOCK:], dvpad[ATT_BLOCK:]], axis=1).astype(BF16)
    dx3_0, produced["q"], produced["kv"], dg_pre_mix1, dg_kv = hosted(
        _bwd_qkv, dx1_1, dq, dkv, x3_0, h1, hk, wq, wkv, row(g_pre_mix, 1), g_kv, swap=("o",))
    dx2_0, df0, produced["gate0"], produced["proj0"], dg_ple_post0, dg_ple0, dg_post_ffn0 = hosted(
        _bwd_ple, 0, dx3_0, x2_0, z0, pe0, h3_0, p[0], f0, wgate0, row(g_ple_post, 0), row(g_ple, 0),
        row(g_post_ffn, 0), swap=("q", "kv"), spread=("gu1_1",))
    dh2_0, dg0, du0, a0 = hosted(_bwd_ffn_act, 0, df0, gs0, us0, wgu0, wd0,
                                 swap=("gate0", "proj0"), spread=("wd1_1", "o"))
    ffn_weight_grads(0, h2_0, df0, dg0, du0, a0, [
        dict(spread=("gate0", "proj0", "q", "kv")), dict(swap=ffn_q(0, 0))])
    grad_x, produced["pool"], dscale, dg_pre_ffn0, dg_post_mix0, dg_pre_mix0 = hosted(
        _bwd_pool_mixer, dx2_0, dh2_0, x1_0, x, yraw, dpool, wp, scale, row(g_pre_ffn, 0), row(g_post_mix, 0),
        row(g_pre_mix, 0), swap=ffn_q(0, 1), spread=ffn_q(0, 0))

    def update(name, n_col_blocks=1, pieces=None, swap=(), spread=()):
        w, m, v = weights[name]
        rows = w.size // w.shape[-1]
        flat = [landed[n].reshape(landed[n].shape[0], -1, landed[n].shape[-1])
                for n in (pieces or [kind_short[name]])]
        outs = hosted(_adamw, name, w.reshape(rows, -1), m.reshape(rows, -1), v.reshape(rows, -1), flat,
                      n_col_blocks, swap=swap, spread=spread)
        return [o.reshape(w.shape) for o in outs]

    kind_short = {"w_q": "q", "w_kv": "kv", "w_o": "o", "pool_w": "pool"}
    upd = {}
    hosted(_jobs_only, "scatter_tail0", swap=("pool",), spread=ffn_q(0, 1))
    hosted(_jobs_only, "scatter_tail1", spread=("pool",))
    upd["w_ple_gate"] = update("w_ple_gate", pieces=("gate0", "gate1"))
    upd["w_ple_proj"] = update("w_ple_proj", pieces=("proj0", "proj1"))
    for name in ("w_q", "w_kv", "w_o", "pool_w"):
        upd[name] = update(name)
    upd["w_gu"] = update("w_gu", FF_PARTS,
                         pieces=[f"gu{layer}_{qtr}" for layer in range(2) for qtr in range(FF_PARTS)])
    upd["w_gu"] = [jnp.swapaxes(a, 1, 2) for a in upd["w_gu"]]
    upd["w_down"] = update("w_down", FF_PARTS,
                           pieces=[f"wd{layer}_{qtr}" for layer in range(2) for qtr in range(FF_PARTS)])

    lanes = lambda a: jnp.pad(a, ((0, 0), (0, D_MODEL - a.shape[1])))
    small = jnp.concatenate([
        dg_pre_mix0, dg_pre_mix1, dg_post_mix0, dg_post_mix1, dg_pre_ffn0, dg_pre_ffn1, dg_post_ffn0, dg_post_ffn1,
        dg_ple0, dg_ple1, dg_ple_post0, dg_ple_post1, dg_kv, dscale, lanes(dsinks[:, :N_HEADS]), lanes(loss)], axis=0)
    return grad_x, upd, small


def kernel(x, p, pre_mix_g, post_mix_g, pre_ffn_g, post_ffn_g, pool_w, pool_scale, kv_g, w_kv, w_q, sinks, w_o, w_gu, w_down, ple_g, w_ple_gate, w_ple_proj, ple_post_g, loss_target, m_pre_mix_g, m_post_mix_g, m_pre_ffn_g, m_post_ffn_g, m_pool_w, m_pool_scale, m_kv_g, m_w_kv, m_w_q, m_sinks, m_w_o, m_w_gu, m_w_down, m_ple_g, m_w_ple_gate, m_w_ple_proj, m_ple_post_g, v_pre_mix_g, v_post_mix_g, v_pre_ffn_g, v_post_ffn_g, v_pool_w, v_pool_scale, v_kv_g, v_w_kv, v_w_q, v_sinks, v_w_o, v_w_gu, v_w_down, v_ple_g, v_w_ple_gate, v_w_ple_proj, v_ple_post_g):
    shards = {"pool": pool_w[0].astype(BF16), "scale": pool_scale, "kv": w_kv.astype(BF16),
              "q": w_q[0].astype(BF16), "o": w_o[0].astype(BF16)}
    for layer in range(2):
        shards[f"gu{layer}"] = w_gu[layer].T.astype(BF16)
        shards[f"wd{layer}"] = w_down[layer].astype(BF16)
        for half in range(2):
            cols = slice(half * D_MODEL // 2, (half + 1) * D_MODEL // 2)
            shards[f"guh{layer}_{half}"] = shards[f"gu{layer}"][:, cols]
            shards[f"wdh{layer}_{half}"] = shards[f"wd{layer}"][:, cols]
        shards[f"gate{layer}"] = w_ple_gate[layer].astype(BF16)
        shards[f"proj{layer}"] = w_ple_proj[layer].astype(BF16)
    gains = jnp.concatenate([pre_mix_g, post_mix_g, pre_ffn_g, post_ffn_g, ple_g, ple_post_g, kv_g[None, :]],
                            axis=0).reshape(-1, 1, D_MODEL)
    weights = {"pool_w": (pool_w, m_pool_w, v_pool_w), "w_kv": (w_kv, m_w_kv, v_w_kv), "w_q": (w_q, m_w_q, v_w_q),
               "w_o": (w_o, m_w_o, v_w_o), "w_down": (w_down, m_w_down, v_w_down),
               "w_gu": tuple(jnp.swapaxes(a, 1, 2) for a in (w_gu, m_w_gu, v_w_gu)),
               "w_ple_gate": (w_ple_gate, m_w_ple_gate, v_w_ple_gate),
               "w_ple_proj": (w_ple_proj, m_w_ple_proj, v_w_ple_proj)}
    grad_x, upd, small = _local_step(x[0], p[:, 0], loss_target[0], gains, sinks, shards, weights)

    small_params = {
        "pre_mix_g": (pre_mix_g, m_pre_mix_g, v_pre_mix_g), "post_mix_g": (post_mix_g, m_post_mix_g, v_post_mix_g),
        "pre_ffn_g": (pre_ffn_g, m_pre_ffn_g, v_pre_ffn_g), "post_ffn_g": (post_ffn_g, m_post_ffn_g, v_post_ffn_g),
        "ple_g": (ple_g, m_ple_g, v_ple_g), "ple_post_g": (ple_post_g, m_ple_post_g, v_ple_post_g),
        "kv_g": (kv_g[None, :], m_kv_g[None, :], v_kv_g[None, :]),
        "pool_scale": (pool_scale, m_pool_scale, v_pool_scale), "sinks": (sinks, m_sinks, v_sinks)}
    loss, small_upd = _small_adamw(_small_all_reduce(small), small_params)
    small_upd["kv_g"] = [a[0] for a in small_upd["kv_g"]]
    upd.update(small_upd)

    names = ["pre_mix_g", "post_mix_g", "pre_ffn_g", "post_ffn_g", "pool_w", "pool_scale", "kv_g", "w_kv", "w_q",
             "sinks", "w_o", "w_gu", "w_down", "ple_g", "w_ple_gate", "w_ple_proj", "ple_post_g"]
    outs = [loss[0, 0], grad_x[None]]
    for kind in range(4):
        outs += [upd[n][kind] for n in names]
    return tuple(outs)
```

```python
import functools
import types

import jax
import jax.numpy as jnp
from jax import lax
from jax.experimental import pallas as pl
from jax.experimental.pallas import tpu as pltpu

F32 = jnp.float32
BF16 = jnp.bfloat16

N_DEV = 8
D_MODEL = 1024
N_POOL_GROUPS = 4
POOL_GROUP = 256
POOL_HALO = 16
HEAD_DIM = 64
N_HEADS = 16
N_KV_HEADS = 4
GQA_GROUP = 4
KV_DIM = N_KV_HEADS * HEAD_DIM
ATT_BLOCK = 128
D_FF = 2816
FF_CHUNKS = 4
FF_BLOCK = D_FF // FF_CHUNKS
WD_ROWS = D_FF // N_DEV
FF_PARTS = 2
FF_PART = D_MODEL // FF_PARTS
N_CHIPS = 4
PLE_DIM = 256
EPS = 1e-6
NEG_INF = -1e30
ATT_SCALE = HEAD_DIM ** -0.5

ADAM_LR = 0.001
ADAM_B1 = 0.9
ADAM_B2 = 0.999
ADAM_EPS = 1e-08
ADAM_WD = 0.01
ADAM_STEP = 10

ROW_TILE = 512
FFN_ROW_TILE = 512
FFN_SUB_TILES = 1
VMEM_BIG = 60 * 1024 * 1024
VMEM_MID = 56 * 1024 * 1024
HBM_PIN_ELEMS = 1024

SV_ROWS = 16
SV_PRE_MIX, SV_POST_MIX, SV_PRE_FFN, SV_POST_FFN, SV_PLE, SV_PLE_POST = 0, 2, 4, 6, 8, 10
SV_KV, SV_POOL_SCALE, SV_SINKS, SV_LOSS = 12, 13, 14, 15

MESH = pl.DeviceIdType.MESH
ANY = pl.BlockSpec(memory_space=pl.ANY)


def _dot(a, b):
    return jnp.dot(a, b, preferred_element_type=F32)


def _dot_nt(a, b):
    return lax.dot_general(a, b, (((1,), (1,)), ((), ())), preferred_element_type=F32)


def _dot_tn(a, b):
    return lax.dot_general(a, b, (((0,), (0,)), ((), ())), preferred_element_type=F32)


def _rstd(x):
    return lax.rsqrt(jnp.mean(x * x, axis=-1, keepdims=True) + EPS)


def _rms(x, g):
    return x * _rstd(x) * g


def _rms_bwd(x, g, dy):
    r = _rstd(x)
    n = x * r
    dn = dy * g
    dx = r * (dn - n * jnp.mean(dn * n, axis=-1, keepdims=True))
    dg = jnp.sum(dy * n, axis=0, keepdims=True)
    return dx, dg


def _add_all(terms):
    return functools.reduce(jnp.add, terms)


def _sigmoid(x):
    return 1.0 / (1.0 + jnp.exp(-x))


def _acc(ref, val, first):
    @pl.when(first)
    def _():
        ref[...] = val

    @pl.when(jnp.logical_not(first))
    def _():
        ref[...] += val


def _pool_counts(row0, rows):
    t = row0 + lax.broadcasted_iota(jnp.int32, (rows, D_MODEL), 0) + 1
    grp = lax.broadcasted_iota(jnp.int32, (rows, D_MODEL), 1) // POOL_GROUP
    win = jnp.left_shift(2, grp)
    return jnp.minimum(t, win).astype(F32)


def _window_sums(ext, shift_of):
    outs = []
    s = ext
    for gi in range(N_POOL_GROUPS):
        s = s + pltpu.roll(s, shift_of(1 << gi), axis=0)
        outs.append(s[:, :POOL_GROUP])
        s = s[:, POOL_GROUP:]
    return jnp.concatenate(outs, axis=1)


def _cparams(n_axes, vmem, collective_id=None):
    return pltpu.CompilerParams(dimension_semantics=("arbitrary",) * n_axes, vmem_limit_bytes=vmem,
                                collective_id=collective_id)


_PEER_SETS = (("sibling", "x", "y"), ("sibling",), ("x", "y"))


def _meet(peers):
    x, y, c = lax.axis_index("x"), lax.axis_index("y"), lax.axis_index("c")
    device = {"sibling": (x, y, 1 - c), "x": (1 - x, y, c), "y": (x, 1 - y, c)}
    barrier = pltpu.get_barrier_semaphore()
    for peer in peers:
        pl.semaphore_signal(barrier, inc=1, device_id=device[peer], device_id_type=pl.DeviceIdType.MESH)
    pl.semaphore_wait(barrier, len(peers))


def _row_spec(cols, tm=ROW_TILE):
    return pl.BlockSpec((tm, cols), lambda i: (i, 0))


def _full_spec(shape):
    zeros = (0,) * len(shape)
    return pl.BlockSpec(shape, lambda *_: zeros)


def _vec_spec():
    return _full_spec((1, D_MODEL))


def _column_ranges(parts):
    ends = [0]
    for part in parts:
        ends.append(ends[-1] + part.shape[-1])
    return list(zip(ends[:-1], ends[1:]))


class _Gain:
    def __init__(self, stacked, layer):
        self.stacked, self.layer = stacked, layer

    def spec(self):
        layer = self.layer
        return pl.BlockSpec((None, 1, D_MODEL), lambda *_: (layer, 0, 0))


def _in_hbm(a):
    return pltpu.with_memory_space_constraint(a, pltpu.HBM) if a.size >= HBM_PIN_ELEMS else a


def _out_in_hbm(s):
    return pltpu.HBM(s.shape, s.dtype) if s.size >= HBM_PIN_ELEMS else s


def _launch(body, *, name, grid, in_specs, out_specs, out_shape, args, scratch_shapes=(), vmem=VMEM_MID, job=None):
    in_specs = [a.spec() if isinstance(a, _Gain) else s for s, a in zip(in_specs, args)]
    args = [_in_hbm(a.stacked if isinstance(a, _Gain) else a) for a in args]
    n_in, n_out, n_scr = len(args), len(out_shape), len(scratch_shapes)
    if job is not None and not job.args:
        job = None
    j_args, j_out, j_scr = ([], [], []) if job is None else ([_in_hbm(a) for a in job.args], job.out_shape, job.scratch)

    def run(*refs):
        groups, at = [], 0
        for n in (n_in, len(j_args), n_out, len(j_out), n_scr, len(j_scr)):
            groups.append(refs[at:at + n])
            at += n
        ins, j_ins, outs, j_outs, scr, j_sems = groups

        def begin():
            _meet(job.peers)
            job.start(j_ins, j_outs, j_sems)

        if job is None:
            body(*ins, *outs, *scr)
        elif not grid:
            begin()
            job.mid(j_ins, j_outs, j_sems)
            body(*ins, *outs, *scr)
            job.finish(j_ins, j_outs, j_sems)
        else:
            ids = [pl.program_id(a) for a in range(len(grid))]
            first = functools.reduce(jnp.logical_and, [i == 0 for i in ids])
            half = functools.reduce(jnp.logical_and, [ids[0] == grid[0] // 2] + [i == 0 for i in ids[1:]])
            last = functools.reduce(jnp.logical_and, [i == g - 1 for i, g in zip(ids, grid)])
            pl.when(first)(begin)
            pl.when(half)(lambda: job.mid(j_ins, j_outs, j_sems))
            body(*ins, *outs, *scr)
            pl.when(last)(lambda: job.finish(j_ins, j_outs, j_sems))

    res = pl.pallas_call(
        run, name=name, grid=grid,
        in_specs=list(in_specs) + [ANY] * len(j_args), out_specs=list(out_specs) + [ANY] * len(j_out),
        out_shape=[_out_in_hbm(s) for s in list(out_shape) + list(j_out)],
        scratch_shapes=list(scratch_shapes) + list(j_scr),
        compiler_params=_cparams(len(grid), vmem, None if job is None else _PEER_SETS.index(job.peers)),
    )(*args, *j_args)
    return res[:n_out], res[n_out:]


def _fwd_pool_mixer(x, g_pre, wp, scale, g_post, g_ffn, job=None):
    T = x.shape[0]
    tm = ROW_TILE
    nt = T // tm

    def body(x_ref, gpre_ref, wp_ref, sc_ref, gpost_ref, gffn_ref, x1_ref, h2_ref, yraw_ref, d_ref, carry):
        i = pl.program_id(0)

        @pl.when(i == 0)
        def _():
            carry[...] = jnp.zeros_like(carry)

        xv = x_ref[...]
        h = _rms(xv, gpre_ref[...])
        ext = jnp.concatenate([carry[...], h], axis=0)
        carry[...] = h[tm - POOL_HALO:, :]
        sums = _window_sums(ext, lambda k: k)[POOL_HALO:, :]
        d = sums / _pool_counts(i * tm, tm) - h
        db = d.astype(BF16)
        d_ref[...] = db
        yraw = jnp.concatenate(
            [_dot(db[:, g * POOL_GROUP:(g + 1) * POOL_GROUP], wp_ref[g]) for g in range(N_POOL_GROUPS)], axis=1)
        yraw_ref[...] = yraw.astype(BF16)
        x1 = xv + _rms(yraw * sc_ref[...], gpost_ref[...])
        x1_ref[...] = x1
        h2_ref[...] = _rms(x1, gffn_ref[...]).astype(BF16)

    return _launch(
        body, name="fwd_pool_mixer", grid=(nt,),
        in_specs=[_row_spec(D_MODEL), _vec_spec(), _full_spec((N_POOL_GROUPS, POOL_GROUP, POOL_GROUP)), _vec_spec(),
                  _vec_spec(), _vec_spec()],
        out_specs=[_row_spec(D_MODEL)] * 4,
        out_shape=[jax.ShapeDtypeStruct((T, D_MODEL), F32)] + [jax.ShapeDtypeStruct((T, D_MODEL), BF16)] * 3,
        scratch_shapes=[pltpu.VMEM((POOL_HALO, D_MODEL), F32)],
        args=(x, g_pre, wp, scale, g_post, g_ffn), job=job)


def _fwd_ffn(layer, h2, x1, wgu, wd, g_post, g_ple, job=None):
    T = h2.shape[0]
    tm = min(FFN_ROW_TILE, T)
    nt = T // tm
    sub = tm // FFN_SUB_TILES
    last = FF_CHUNKS - 1
    n_gu, n_wd = len(wgu), len(wd)
    gu_cols = _column_ranges(wgu)

    def body(h2_ref, x1_ref, *refs):
        wgu_refs, wd_refs = refs[:n_gu], refs[n_gu:n_gu + n_wd]
        gpost_ref, gple_ref, gs_ref, us_ref, f_ref, x2_ref, h3_ref, acc = refs[n_gu + n_wd:]
        k = pl.program_id(0)
        i = pl.program_id(1)
        rows = pl.ds(pl.multiple_of(i * tm, tm), tm)
        parts = []
        for s in range(FFN_SUB_TILES):
            r = pl.ds(s * sub, sub)
            g = _add_all([_dot_nt(h2_ref[r, c0:c1], w[0]) for (c0, c1), w in zip(gu_cols, wgu_refs)])
            u = _add_all([_dot_nt(h2_ref[r, c0:c1], w[1]) for (c0, c1), w in zip(gu_cols, wgu_refs)])
            gs_ref[r, :] = g.astype(BF16)
            us_ref[r, :] = u.astype(BF16)
            a = (g * _sigmoid(g) * u).astype(BF16)
            parts.append(jnp.concatenate([_dot(a, w[...]) for w in wd_refs], axis=1))
        part = jnp.concatenate(parts, axis=0)

        @pl.when(k == 0)
        def _():
            acc[rows, :] = part

        @pl.when(jnp.logical_and(k > 0, k < last))
        def _():
            acc[rows, :] += part

        @pl.when(k == last)
        def _():
            f = acc[rows, :] + part
            f_ref[...] = f.astype(BF16)
            x2 = x1_ref[...] + _rms(f, gpost_ref[...])
            x2_ref[...] = x2
            h3_ref[...] = _rms(x2, gple_ref[...]).astype(BF16)

    def late(k, i):
        return (jnp.where(k == last, i, 0), 0)

    return _launch(
        body, name=f"fwd_ffn{layer}", grid=(FF_CHUNKS, nt),
        in_specs=[pl.BlockSpec((tm, D_MODEL), lambda k, i: (i, 0)), pl.BlockSpec((tm, D_MODEL), late)]
                 + [pl.BlockSpec((None, 2, FF_BLOCK, w.shape[-1]), lambda k, i: (k, 0, 0, 0)) for w in wgu]
                 + [pl.BlockSpec((FF_BLOCK, w.shape[-1]), lambda k, i: (k, 0)) for w in wd]
                 + [pl.BlockSpec((1, D_MODEL), lambda k, i: (0, 0))] * 2,
        out_specs=[pl.BlockSpec((None, tm, FF_BLOCK), lambda k, i: (k, i, 0)),
                   pl.BlockSpec((None, tm, FF_BLOCK), lambda k, i: (k, i, 0)),
                   pl.BlockSpec((tm, D_MODEL), late),
                   pl.BlockSpec((tm, D_MODEL), late),
                   pl.BlockSpec((tm, D_MODEL), late)],
        out_shape=[jax.ShapeDtypeStruct((FF_CHUNKS, T, FF_BLOCK), BF16),
                   jax.ShapeDtypeStruct((FF_CHUNKS, T, FF_BLOCK), BF16),
                   jax.ShapeDtypeStruct((T, D_MODEL), BF16),
                   jax.ShapeDtypeStruct((T, D_MODEL), F32),
                   jax.ShapeDtypeStruct((T, D_MODEL), BF16)],
        scratch_shapes=[pltpu.VMEM((T, D_MODEL), F32)],
        args=(h2, x1, *wgu, *wd, g_post, g_ple), vmem=VMEM_BIG, job=job)


def _fwd_ple(layer, x2, h3, p, wgate, wproj, g_post, job=None):
    T = x2.shape[0]
    nt = T // ROW_TILE

    def body(x2_ref, h3_ref, p_ref, wg_ref, wp_ref, gpost_ref, x3_ref, z_ref, pe_ref):
        z = _dot(h3_ref[...], wg_ref[...])
        pe = _dot(p_ref[...].astype(BF16), wp_ref[...])
        z_ref[...] = z.astype(BF16)
        pe_ref[...] = pe.astype(BF16)
        x3_ref[...] = x2_ref[...] + _rms(pe * _sigmoid(z), gpost_ref[...])

    return _launch(
        body, name=f"fwd_ple{layer}", grid=(nt,),
        in_specs=[_row_spec(D_MODEL), _row_spec(D_MODEL), _row_spec(PLE_DIM), _full_spec((D_MODEL, D_MODEL)),
                  _full_spec((PLE_DIM, D_MODEL)), _vec_spec()],
        out_specs=[_row_spec(D_MODEL)] * 3,
        out_shape=[jax.ShapeDtypeStruct((T, D_MODEL), F32)] + [jax.ShapeDtypeStruct((T, D_MODEL), BF16)] * 2,
        args=(x2, h3, p, wgate, wproj, g_post), job=job)


def _fwd_qkv(x3, g_kv, g_mix, wkv, wq, job=None):
    T = x3.shape[0]
    nt = T // ROW_TILE

    def body(x_ref, gkv_ref, gmix_ref, wkv_ref, wq_ref, hk_ref, h1_ref, q_ref, kv_ref):
        xv = x_ref[...]
        r = _rstd(xv)
        hk = (xv * r * gkv_ref[...]).astype(BF16)
        h1 = (xv * r * gmix_ref[...]).astype(BF16)
        hk_ref[...] = hk
        h1_ref[...] = h1
        kv_ref[...] = _dot(hk, wkv_ref[...]).astype(BF16)
        q_ref[...] = _dot(h1, wq_ref[...]).astype(BF16)

    return _launch(
        body, name="fwd_qkv", grid=(nt,),
        in_specs=[_row_spec(D_MODEL), _vec_spec(), _vec_spec(), _full_spec((D_MODEL, 2 * KV_DIM)),
                  _full_spec((D_MODEL, D_MODEL))],
        out_specs=[_row_spec(D_MODEL), _row_spec(D_MODEL), _row_spec(D_MODEL), _row_spec(2 * KV_DIM)],
        out_shape=[jax.ShapeDtypeStruct((T, D_MODEL), BF16)] * 3 + [jax.ShapeDtypeStruct((T, 2 * KV_DIM), BF16)],
        args=(x3, g_kv, g_mix, wkv, wq), job=job)


def _alibi_slope(h):
    return 2.0 ** (-8.0 * (h + 1) / N_HEADS)


ATT_SUB = 32
ATT_GROUP_ROWS = GQA_GROUP * ATT_BLOCK


def _att_mask(n, row0):
    qi = lax.broadcasted_iota(jnp.int32, (ATT_SUB, 2 * ATT_BLOCK), 0) + row0
    si = lax.broadcasted_iota(jnp.int32, (ATT_SUB, 2 * ATT_BLOCK), 1)
    rel = ATT_BLOCK + qi - si
    valid = (rel >= 0) & (rel < ATT_BLOCK) & ((si >= ATT_BLOCK) | (n > 0))
    return rel.astype(F32), valid


def _att_probs(raw, relf, valid, slope, sink):
    s = jnp.where(valid, raw * ATT_SCALE - slope * relf, NEG_INF)
    m = jnp.maximum(jnp.max(s, axis=-1, keepdims=True), sink)
    e = jnp.exp(s - m)
    es = jnp.exp(sink - m)
    inv = 1.0 / (jnp.sum(e, axis=-1, keepdims=True) + es)
    return e * inv, es * inv


def _stack_heads(ref, kh):
    first = kh * GQA_GROUP
    return jnp.concatenate([ref[:, (first + g) * HEAD_DIM:(first + g + 1) * HEAD_DIM] for g in range(GQA_GROUP)], axis=0)


def _unstack_heads(stacked):
    return [stacked[g * ATT_BLOCK:(g + 1) * ATT_BLOCK, :] for g in range(GQA_GROUP)]


def _fwd_attention(q, kpad, vpad, sinks, job=None):
    T = q.shape[0]
    nb = T // ATT_BLOCK

    def body(q_ref, k_ref, v_ref, sink_ref, o_ref, s_scr, p_scr):
        n = pl.program_id(0)
        start = pl.multiple_of(n * ATT_BLOCK, ATT_BLOCK)
        kw = k_ref[pl.ds(start, 2 * ATT_BLOCK), :]
        vw = v_ref[pl.ds(start, 2 * ATT_BLOCK), :]
        outs = []
        for kh in range(N_KV_HEADS):
            kk = kw[:, kh * HEAD_DIM:(kh + 1) * HEAD_DIM]
            vv = vw[:, kh * HEAD_DIM:(kh + 1) * HEAD_DIM]
            s_scr[...] = _dot_nt(_stack_heads(q_ref, kh), kk)
            for g in range(GQA_GROUP):
                h = kh * GQA_GROUP + g
                for row0 in range(0, ATT_BLOCK, ATT_SUB):
                    rows = pl.ds(g * ATT_BLOCK + row0, ATT_SUB)
                    relf, valid = _att_mask(n, row0)
                    pr, _ = _att_probs(s_scr[rows, :], relf, valid, _alibi_slope(h), sink_ref[0, h])
                    p_scr[rows, :] = pr.astype(BF16)
            outs += _unstack_heads(_dot(p_scr[...], vv))
        o_ref[...] = jnp.concatenate(outs, axis=1).astype(BF16)

    return _launch(
        body, name="fwd_attention", grid=(nb,),
        in_specs=[_row_spec(D_MODEL, ATT_BLOCK), _full_spec((T + ATT_BLOCK, KV_DIM)), _full_spec((T + ATT_BLOCK, KV_DIM)),
                  pl.BlockSpec(memory_space=pltpu.SMEM)],
        out_specs=[_row_spec(D_MODEL, ATT_BLOCK)],
        out_shape=[jax.ShapeDtypeStruct((T, D_MODEL), BF16)],
        scratch_shapes=[pltpu.VMEM((ATT_GROUP_ROWS, 2 * ATT_BLOCK), F32), pltpu.VMEM((ATT_GROUP_ROWS, 2 * ATT_BLOCK), BF16)],
        args=(q, kpad, vpad, sinks), job=job)


def _fwd_attn_out(attn, x, wo, g_post, g_ffn, job=None):
    T = x.shape[0]
    nt = T // ROW_TILE

    def body(a_ref, x_ref, wo_ref, gpost_ref, gffn_ref, y_ref, x1_ref, h2_ref):
        y = _dot(a_ref[...], wo_ref[...])
        y_ref[...] = y.astype(BF16)
        x1 = x_ref[...] + _rms(y, gpost_ref[...])
        x1_ref[...] = x1
        h2_ref[...] = _rms(x1, gffn_ref[...]).astype(BF16)

    return _launch(
        body, name="fwd_attn_out", grid=(nt,),
        in_specs=[_row_spec(D_MODEL), _row_spec(D_MODEL), _full_spec((D_MODEL, D_MODEL)), _vec_spec(), _vec_spec()],
        out_specs=[_row_spec(D_MODEL)] * 3,
        out_shape=[jax.ShapeDtypeStruct((T, D_MODEL), BF16), jax.ShapeDtypeStruct((T, D_MODEL), F32),
                   jax.ShapeDtypeStruct((T, D_MODEL), BF16)],
        args=(attn, x, wo, g_post, g_ffn), job=job)


def _bwd_ple(layer, dx3, x2, z, pe, h3, p, f, wgate, g_ple_post, g_ple, g_post_ffn, job=None):
    T = x2.shape[0]
    tm = ROW_TILE
    nt = T // tm

    def body(dx3_ref, x2_ref, z_ref, pe_ref, h3_ref, p_ref, f_ref, wg_ref, gpp_ref, gp_ref, gpf_ref,
             dx2_ref, df_ref, dwg_ref, dwp_ref, dgpp_ref, dgp_ref, dgpf_ref, acc_g, acc_p):
        i = pl.program_id(0)
        first = i == 0
        dx3v = dx3_ref[...]
        gate = _sigmoid(z_ref[...].astype(F32))
        pev = pe_ref[...].astype(F32)
        de, dgpp = _rms_bwd(pev * gate, gpp_ref[...], dx3v)
        dpe = (de * gate).astype(BF16)
        dz = (de * pev * gate * (1.0 - gate)).astype(BF16)
        _acc(acc_p, _dot_tn(p_ref[...].astype(BF16), dpe), first)
        _acc(acc_g, _dot_tn(h3_ref[...], dz), first)
        dh3 = _dot_nt(dz, wg_ref[...])
        dxn, dgp = _rms_bwd(x2_ref[...], gp_ref[...], dh3)
        dx2 = dx3v + dxn
        dx2_ref[...] = dx2
        df, dgpf = _rms_bwd(f_ref[...].astype(F32), gpf_ref[...], dx2)
        df_ref[...] = df.astype(BF16)
        _acc(dgpp_ref, dgpp, first)
        _acc(dgp_ref, dgp, first)
        _acc(dgpf_ref, dgpf, first)

        @pl.when(i == nt - 1)
        def _():
            dwg_ref[...] = acc_g[...].astype(BF16)
            dwp_ref[...] = acc_p[...].astype(BF16)

    return _launch(
        body, name=f"bwd_ple{layer}", grid=(nt,),
        in_specs=[_row_spec(D_MODEL)] * 5 + [_row_spec(PLE_DIM), _row_spec(D_MODEL), _full_spec((D_MODEL, D_MODEL)),
                  _vec_spec(), _vec_spec(), _vec_spec()],
        out_specs=[_row_spec(D_MODEL), _row_spec(D_MODEL), _full_spec((D_MODEL, D_MODEL)), _full_spec((PLE_DIM, D_MODEL)),
                   _vec_spec(), _vec_spec(), _vec_spec()],
        out_shape=[jax.ShapeDtypeStruct((T, D_MODEL), F32), jax.ShapeDtypeStruct((T, D_MODEL), BF16),
                   jax.ShapeDtypeStruct((D_MODEL, D_MODEL), BF16), jax.ShapeDtypeStruct((PLE_DIM, D_MODEL), BF16)]
                  + [jax.ShapeDtypeStruct((1, D_MODEL), F32)] * 3,
        scratch_shapes=[pltpu.VMEM((D_MODEL, D_MODEL), F32), pltpu.VMEM((PLE_DIM, D_MODEL), F32)],
        args=(dx3, x2, z, pe, h3, p, f, wgate, g_ple_post, g_ple, g_post_ffn), vmem=VMEM_BIG, job=job)


def _ple_loss_bwd(layer, x2, h3, p, f, target, wgate, wproj, g_ple_post, g_ple, g_post_ffn, job=None):
    T = x2.shape[0]
    tm = ROW_TILE
    nt = T // tm

    def body(x2_ref, h3_ref, p_ref, f_ref, tgt_ref, wg_ref, wp_ref, gpp_ref, gp_ref, gpf_ref,
             dx2_ref, df_ref, dwg_ref, dwp_ref, dgpp_ref, dgp_ref, dgpf_ref, loss_ref, acc_g, acc_p):
        i = pl.program_id(0)
        first = i == 0
        h3 = h3_ref[...]
        pb = p_ref[...].astype(BF16)
        x2v = x2_ref[...]
        gate = _sigmoid(_dot(h3, wg_ref[...]))
        pev = _dot(pb, wp_ref[...])
        e = pev * gate
        err = x2v + _rms(e, gpp_ref[...]) - tgt_ref[...]
        _acc(loss_ref, 0.5 * jnp.sum(jnp.mean(err * err, axis=-1, keepdims=True), axis=0, keepdims=True), first)
        dx3v = err * (1.0 / D_MODEL)
        de, dgpp = _rms_bwd(e, gpp_ref[...], dx3v)
        dpe = (de * gate).astype(BF16)
        dz = (de * pev * gate * (1.0 - gate)).astype(BF16)
        _acc(acc_p, _dot_tn(pb, dpe), first)
        _acc(acc_g, _dot_tn(h3, dz), first)
        dxn, dgp = _rms_bwd(x2v, gp_ref[...], _dot_nt(dz, wg_ref[...]))
        dx2 = dx3v + dxn
        dx2_ref[...] = dx2
        df, dgpf = _rms_bwd(f_ref[...].astype(F32), gpf_ref[...], dx2)
        df_ref[...] = df.astype(BF16)
        _acc(dgpp_ref, dgpp, first)
        _acc(dgp_ref, dgp, first)
        _acc(dgpf_ref, dgpf, first)

        @pl.when(i == nt - 1)
        def _():
            dwg_ref[...] = acc_g[...].astype(BF16)
            dwp_ref[...] = acc_p[...].astype(BF16)

    return _launch(
        body, name=f"ple_loss_bwd{layer}", grid=(nt,),
        in_specs=[_row_spec(D_MODEL), _row_spec(D_MODEL), _row_spec(PLE_DIM), _row_spec(D_MODEL), _row_spec(D_MODEL),
                  _full_spec((D_MODEL, D_MODEL)), _full_spec((PLE_DIM, D_MODEL)), _vec_spec(), _vec_spec(), _vec_spec()],
        out_specs=[_row_spec(D_MODEL), _row_spec(D_MODEL), _full_spec((D_MODEL, D_MODEL)), _full_spec((PLE_DIM, D_MODEL)),
                   _vec_spec(), _vec_spec(), _vec_spec(), _full_spec((1, 1))],
        out_shape=[jax.ShapeDtypeStruct((T, D_MODEL), F32), jax.ShapeDtypeStruct((T, D_MODEL), BF16),
                   jax.ShapeDtypeStruct((D_MODEL, D_MODEL), BF16), jax.ShapeDtypeStruct((PLE_DIM, D_MODEL), BF16)]
                  + [jax.ShapeDtypeStruct((1, D_MODEL), F32)] * 3 + [jax.ShapeDtypeStruct((1, 1), F32)],
        scratch_shapes=[pltpu.VMEM((D_MODEL, D_MODEL), F32), pltpu.VMEM((PLE_DIM, D_MODEL), F32)],
        args=(x2, h3, p, f, target, wgate, wproj, g_ple_post, g_ple, g_post_ffn), vmem=VMEM_BIG, job=job)


def _bwd_ffn_act(layer, df, gs, us, wgu, wd, job=None):
    T = df.shape[0]
    tm = min(FFN_ROW_TILE, T)
    nt = T // tm
    sub = tm // FFN_SUB_TILES
    last = FF_CHUNKS - 1
    n_gu, n_wd = len(wgu), len(wd)
    wd_cols = _column_ranges(wd)

    def body(df_ref, gs_ref, us_ref, *refs):
        wgu_refs, wd_refs = refs[:n_gu], refs[n_gu:n_gu + n_wd]
        dh_ref, dg_ref, du_ref, a_ref, acc_h = refs[n_gu + n_wd:]
        k = pl.program_id(0)
        i = pl.program_id(1)
        rows = pl.ds(pl.multiple_of(i * tm, tm), tm)
        dhs = []
        for s in range(FFN_SUB_TILES):
            r = pl.ds(s * sub, sub)
            g = gs_ref[r, :].astype(F32)
            u = us_ref[r, :].astype(F32)
            sg = _sigmoid(g)
            silu = g * sg
            a_ref[r, :] = (silu * u).astype(BF16)
            da = _add_all([_dot_nt(df_ref[r, c0:c1], w[...]) for (c0, c1), w in zip(wd_cols, wd_refs)])
            dg = (da * u * (sg * (1.0 + g * (1.0 - sg)))).astype(BF16)
            du = (da * silu).astype(BF16)
            dg_ref[r, :] = dg
            du_ref[r, :] = du
            dhs.append(jnp.concatenate([_dot(dg, w[0]) + _dot(du, w[1]) for w in wgu_refs], axis=1))
        dh = jnp.concatenate(dhs, axis=0)

        @pl.when(k == 0)
        def _():
            acc_h[rows, :] = dh

        @pl.when(jnp.logical_and(k > 0, k < last))
        def _():
            acc_h[rows, :] += dh

        @pl.when(k == last)
        def _():
            dh_ref[...] = acc_h[rows, :] + dh

    chunk_rows = pl.BlockSpec((None, tm, FF_BLOCK), lambda k, i: (k, i, 0))
    saved = jax.ShapeDtypeStruct((FF_CHUNKS, T, FF_BLOCK), BF16)
    return _launch(
        body, name=f"bwd_ffn_act{layer}", grid=(FF_CHUNKS, nt),
        in_specs=[pl.BlockSpec((tm, D_MODEL), lambda k, i: (i, 0)), chunk_rows, chunk_rows]
                 + [pl.BlockSpec((None, 2, FF_BLOCK, w.shape[-1]), lambda k, i: (k, 0, 0, 0)) for w in wgu]
                 + [pl.BlockSpec((FF_BLOCK, w.shape[-1]), lambda k, i: (k, 0)) for w in wd],
        out_specs=[pl.BlockSpec((tm, D_MODEL), lambda k, i: (jnp.where(k == last, i, 0), 0)),
                   chunk_rows, chunk_rows, chunk_rows],
        out_shape=[jax.ShapeDtypeStruct((T, D_MODEL), F32), saved, saved, saved],
        scratch_shapes=[pltpu.VMEM((T, D_MODEL), F32)],
        args=(df, gs, us, *wgu, *wd), vmem=VMEM_BIG, job=job)


def _bwd_ffn_dw(layer, q, parts, h2, df, dg, du, a, job=None):
    T = h2.shape[0]
    width = D_MODEL // parts

    def body(h_ref, df_ref, dg_ref, du_ref, a_ref, dgu_ref, dwd_ref):
        h = h_ref[...]
        dgu_ref[0] = _dot_tn(dg_ref[...], h).astype(BF16)
        dgu_ref[1] = _dot_tn(du_ref[...], h).astype(BF16)
        dwd_ref[...] = _dot_tn(a_ref[...], df_ref[...]).astype(BF16)

    cols = pl.BlockSpec((T, width), lambda k: (0, q))
    chunk = pl.BlockSpec((None, T, FF_BLOCK), lambda k: (k, 0, 0))
    return _launch(
        body, name=f"bwd_ffn_dw{layer}_{q}", grid=(FF_CHUNKS,),
        in_specs=[cols, cols, chunk, chunk, chunk],
        out_specs=[pl.BlockSpec((None, 2, FF_BLOCK, width), lambda k: (k, 0, 0, 0)),
                   pl.BlockSpec((FF_BLOCK, width), lambda k: (k, 0))],
        out_shape=[jax.ShapeDtypeStruct((FF_CHUNKS, 2, FF_BLOCK, width), BF16),
                   jax.ShapeDtypeStruct((D_FF, width), BF16)],
        args=(h2, df, dg, du, a), vmem=VMEM_BIG, job=job)


def _bwd_attn_out(dx2, dh2, x1, y, attn, wo, g_ffn, g_post, job=None):
    T = x1.shape[0]
    nt = T // ROW_TILE

    def body(dx2_ref, dh2_ref, x1_ref, y_ref, a_ref, wo_ref, gffn_ref, gpost_ref,
             dx1_ref, da_ref, dwo_ref, dgf_ref, dgp_ref, acc):
        i = pl.program_id(0)
        first = i == 0
        dxn, dgf = _rms_bwd(x1_ref[...], gffn_ref[...], dh2_ref[...])
        dx1 = dx2_ref[...] + dxn
        dx1_ref[...] = dx1
        dy, dgp = _rms_bwd(y_ref[...].astype(F32), gpost_ref[...], dx1)
        dyb = dy.astype(BF16)
        da_ref[...] = _dot_nt(dyb, wo_ref[...]).astype(BF16)
        _acc(acc, _dot_tn(a_ref[...], dyb), first)
        _acc(dgf_ref, dgf, first)
        _acc(dgp_ref, dgp, first)

        @pl.when(i == nt - 1)
        def _():
            dwo_ref[...] = acc[...].astype(BF16)

    return _launch(
        body, name="bwd_attn_out", grid=(nt,),
        in_specs=[_row_spec(D_MODEL)] * 5 + [_full_spec((D_MODEL, D_MODEL)), _vec_spec(), _vec_spec()],
        out_specs=[_row_spec(D_MODEL), _row_spec(D_MODEL), _full_spec((D_MODEL, D_MODEL)), _vec_spec(), _vec_spec()],
        out_shape=[jax.ShapeDtypeStruct((T, D_MODEL), F32), jax.ShapeDtypeStruct((T, D_MODEL), BF16),
                   jax.ShapeDtypeStruct((D_MODEL, D_MODEL), BF16)] + [jax.ShapeDtypeStruct((1, D_MODEL), F32)] * 2,
        scratch_shapes=[pltpu.VMEM((D_MODEL, D_MODEL), F32)],
        args=(dx2, dh2, x1, y, attn, wo, g_ffn, g_post), job=job)


def _bwd_attention(q, dattn, kpad, vpad, sinks, job=None):
    T = q.shape[0]
    nb = T // ATT_BLOCK

    def body(q_ref, do_ref, k_ref, v_ref, sink_ref, dq_ref, dk_ref, dv_ref, ds_ref, s_scr, dp_scr, p_scr, dsb_scr):
        n = pl.program_id(0)

        @pl.when(n == 0)
        def _():
            dk_ref[...] = jnp.zeros_like(dk_ref)
            dv_ref[...] = jnp.zeros_like(dv_ref)
            ds_ref[...] = jnp.zeros_like(ds_ref)

        start = pl.multiple_of(n * ATT_BLOCK, ATT_BLOCK)
        win = pl.ds(start, 2 * ATT_BLOCK)
        kw = k_ref[win, :]
        vw = v_ref[win, :]
        lane = lax.broadcasted_iota(jnp.int32, (1, ATT_BLOCK), 1)
        dsink = jnp.zeros((1, ATT_BLOCK), F32)
        dqs, dks, dvs = [], [], []
        for kh in range(N_KV_HEADS):
            kk = kw[:, kh * HEAD_DIM:(kh + 1) * HEAD_DIM]
            vv = vw[:, kh * HEAD_DIM:(kh + 1) * HEAD_DIM]
            qs = _stack_heads(q_ref, kh)
            dos = _stack_heads(do_ref, kh)
            s_scr[...] = _dot_nt(qs, kk)
            dp_scr[...] = _dot_nt(dos, vv)
            for g in range(GQA_GROUP):
                h = kh * GQA_GROUP + g
                dsink_h = jnp.zeros((1, 1), F32)
                for row0 in range(0, ATT_BLOCK, ATT_SUB):
                    rows = pl.ds(g * ATT_BLOCK + row0, ATT_SUB)
                    relf, valid = _att_mask(n, row0)
                    pr, ps = _att_probs(s_scr[rows, :], relf, valid, _alibi_slope(h), sink_ref[0, h])
                    dp = dp_scr[rows, :]
                    delta = jnp.sum(pr * dp, axis=-1, keepdims=True)
                    dsb_scr[rows, :] = (pr * (dp - delta) * ATT_SCALE).astype(BF16)
                    p_scr[rows, :] = pr.astype(BF16)
                    dsink_h = dsink_h - jnp.sum(ps * delta, axis=0, keepdims=True)
                dsink = dsink + jnp.where(lane == h, dsink_h, 0.0)
            dsb = dsb_scr[...]
            dqs += _unstack_heads(_dot(dsb, kk))
            dks.append(_dot_tn(dsb, qs))
            dvs.append(_dot_tn(p_scr[...], dos))
        dq_ref[...] = jnp.concatenate(dqs, axis=1).astype(BF16)
        dk_ref[win, :] += jnp.concatenate(dks, axis=1)
        dv_ref[win, :] += jnp.concatenate(dvs, axis=1)
        ds_ref[...] += dsink

    return _launch(
        body, name="bwd_attention", grid=(nb,),
        in_specs=[_row_spec(D_MODEL, ATT_BLOCK), _row_spec(D_MODEL, ATT_BLOCK), _full_spec((T + ATT_BLOCK, KV_DIM)),
                  _full_spec((T + ATT_BLOCK, KV_DIM)), pl.BlockSpec(memory_space=pltpu.SMEM)],
        out_specs=[_row_spec(D_MODEL, ATT_BLOCK), _full_spec((T + ATT_BLOCK, KV_DIM)), _full_spec((T + ATT_BLOCK, KV_DIM)),
                   _full_spec((1, ATT_BLOCK))],
        out_shape=[jax.ShapeDtypeStruct((T, D_MODEL), BF16), jax.ShapeDtypeStruct((T + ATT_BLOCK, KV_DIM), F32),
                   jax.ShapeDtypeStruct((T + ATT_BLOCK, KV_DIM), F32), jax.ShapeDtypeStruct((1, ATT_BLOCK), F32)],
        scratch_shapes=[pltpu.VMEM((ATT_GROUP_ROWS, 2 * ATT_BLOCK), F32)] * 2
                       + [pltpu.VMEM((ATT_GROUP_ROWS, 2 * ATT_BLOCK), BF16)] * 2,
        args=(q, dattn, kpad, vpad, sinks), vmem=VMEM_BIG, job=job)


def _bwd_qkv(dxres, dq, dkv, x3, h1, hk, wq, wkv, g_mix, g_kv, job=None):
    T = x3.shape[0]
    nt = T // ROW_TILE

    def body(dxr_ref, dq_ref, dkv_ref, x_ref, h1_ref, hk_ref, wq_ref, wkv_ref, gmix_ref, gkv_ref,
             dx_ref, dwq_ref, dwkv_ref, dgm_ref, dgk_ref, acc_q, acc_kv):
        i = pl.program_id(0)
        first = i == 0
        dqv = dq_ref[...]
        dkvv = dkv_ref[...]
        xv = x_ref[...]
        d1, dgm = _rms_bwd(xv, gmix_ref[...], _dot_nt(dqv, wq_ref[...]))
        d2, dgk = _rms_bwd(xv, gkv_ref[...], _dot_nt(dkvv, wkv_ref[...]))
        dx_ref[...] = dxr_ref[...] + d1 + d2
        _acc(acc_q, _dot_tn(h1_ref[...], dqv), first)
        _acc(acc_kv, _dot_tn(hk_ref[...], dkvv), first)
        _acc(dgm_ref, dgm, first)
        _acc(dgk_ref, dgk, first)

        @pl.when(i == nt - 1)
        def _():
            dwq_ref[...] = acc_q[...].astype(BF16)
            dwkv_ref[...] = acc_kv[...].astype(BF16)

    return _launch(
        body, name="bwd_qkv", grid=(nt,),
        in_specs=[_row_spec(D_MODEL), _row_spec(D_MODEL), _row_spec(2 * KV_DIM), _row_spec(D_MODEL), _row_spec(D_MODEL),
                  _row_spec(D_MODEL), _full_spec((D_MODEL, D_MODEL)), _full_spec((D_MODEL, 2 * KV_DIM)), _vec_spec(),
                  _vec_spec()],
        out_specs=[_row_spec(D_MODEL), _full_spec((D_MODEL, D_MODEL)), _full_spec((D_MODEL, 2 * KV_DIM)), _vec_spec(),
                   _vec_spec()],
        out_shape=[jax.ShapeDtypeStruct((T, D_MODEL), F32), jax.ShapeDtypeStruct((D_MODEL, D_MODEL), BF16),
                   jax.ShapeDtypeStruct((D_MODEL, 2 * KV_DIM), BF16)] + [jax.ShapeDtypeStruct((1, D_MODEL), F32)] * 2,
        scratch_shapes=[pltpu.VMEM((D_MODEL, D_MODEL), F32), pltpu.VMEM((D_MODEL, 2 * KV_DIM), F32)],
        args=(dxres, dq, dkv, x3, h1, hk, wq, wkv, g_mix, g_kv), job=job)


def _bwd_pool_mixer(dx2, dh2, x1, x, yraw, d, wp, scale, g_ffn, g_post, g_pre, job=None):
    T = x.shape[0]
    tm = ROW_TILE
    nt = T // tm

    def body(dx2_ref, dh2_ref, x1_ref, x_ref, yraw_ref, d_ref, wp_ref, sc_ref, gffn_ref, gpost_ref, gpre_ref,
             dx_ref, dwp_ref, dsc_ref, dgf_ref, dgp_ref, dgm_ref, carry, acc):
        i = pl.program_id(0)
        first = i == 0
        tile = nt - 1 - i

        @pl.when(first)
        def _():
            carry[...] = jnp.zeros_like(carry)

        dxn, dgf = _rms_bwd(x1_ref[...], gffn_ref[...], dh2_ref[...])
        dx1 = dx2_ref[...] + dxn
        yraw = yraw_ref[...].astype(F32)
        sc = sc_ref[...]
        dy, dgp = _rms_bwd(yraw * sc, gpost_ref[...], dx1)
        dsc = jnp.sum(dy * yraw, axis=0, keepdims=True)
        dyb = (dy * sc).astype(BF16)
        dv = d_ref[...]
        dds = []
        for g in range(N_POOL_GROUPS):
            cols = slice(g * POOL_GROUP, (g + 1) * POOL_GROUP)
            dds.append(_dot_nt(dyb[:, cols], wp_ref[g]))
            _acc(acc.at[g], _dot_tn(dv[:, cols], dyb[:, cols]), first)
        dd = jnp.concatenate(dds, axis=1)
        e = dd / _pool_counts(tile * tm, tm)
        ext = jnp.concatenate([e, carry[...]], axis=0)
        carry[...] = e[:POOL_HALO, :]
        sums = _window_sums(ext, lambda k: tm + POOL_HALO - k)[:tm, :]
        dxm, dgm = _rms_bwd(x_ref[...], gpre_ref[...], sums - dd)
        dx_ref[...] = dx1 + dxm
        _acc(dsc_ref, dsc, first)
        _acc(dgf_ref, dgf, first)
        _acc(dgp_ref, dgp, first)
        _acc(dgm_ref, dgm, first)

        @pl.when(i == nt - 1)
        def _():
            dwp_ref[...] = acc[...].astype(BF16)

    rev = pl.BlockSpec((tm, D_MODEL), lambda i: (nt - 1 - i, 0))
    return _launch(
        body, name="bwd_pool_mixer", grid=(nt,),
        in_specs=[rev] * 6 + [_full_spec((N_POOL_GROUPS, POOL_GROUP, POOL_GROUP))] + [_vec_spec()] * 4,
        out_specs=[rev, _full_spec((N_POOL_GROUPS, POOL_GROUP, POOL_GROUP))] + [_vec_spec()] * 4,
        out_shape=[jax.ShapeDtypeStruct((T, D_MODEL), F32),
                   jax.ShapeDtypeStruct((N_POOL_GROUPS, POOL_GROUP, POOL_GROUP), BF16)]
                  + [jax.ShapeDtypeStruct((1, D_MODEL), F32)] * 4,
        scratch_shapes=[pltpu.VMEM((POOL_HALO, D_MODEL), F32), pltpu.VMEM((N_POOL_GROUPS, POOL_GROUP, POOL_GROUP), F32)],
        args=(dx2, dh2, x1, x, yraw, d, wp, scale, g_ffn, g_post, g_pre), job=job)


def _my_place():
    return lax.axis_index("x"), lax.axis_index("y"), lax.axis_index("c")


def _dev_index(px, py, pc):
    return 4 * px + 2 * py + pc


def _peer_by_relation(r):
    x, y, c = _my_place()
    return (x ^ ((r >> 2) & 1), y ^ ((r >> 1) & 1), c ^ (r & 1))


def _slot_pool(ref, j):
    return ref.at[:, pl.ds(pl.multiple_of(j * 32, 32), 32), :]


def _slot_scale(ref, j):
    return ref.at[:, pl.ds(pl.multiple_of(j * 128, 128), 128)]


def _slot_rows128(ref, j):
    return ref.at[pl.ds(pl.multiple_of(j * 128, 128), 128), :]


def _slot_gu(ref, j):
    return ref.at[j % FF_CHUNKS, j // FF_CHUNKS]


def _slot_wd(ref, j):
    return ref.at[pl.ds(pl.multiple_of(j * WD_ROWS, 16), WD_ROWS), :]


def _slot_cols128(ref, j):
    return ref.at[:, pl.ds(pl.multiple_of(j * 128, 128), 128)]


_GATHERED = {
    "pool": ((N_POOL_GROUPS, POOL_GROUP, POOL_GROUP), BF16, _slot_pool),
    "scale": ((1, D_MODEL), F32, _slot_scale),
    "kv": ((D_MODEL, 2 * KV_DIM), BF16, _slot_rows128),
    "q": ((D_MODEL, D_MODEL), BF16, _slot_rows128),
    "o": ((D_MODEL, D_MODEL), BF16, _slot_rows128),
    "gu": ((FF_CHUNKS, 2, FF_BLOCK, D_MODEL), BF16, _slot_gu),
    "wd": ((D_FF, D_MODEL), BF16, _slot_wd),
    "guh": ((FF_CHUNKS, 2, FF_BLOCK, D_MODEL // 2), BF16, _slot_gu),
    "wdh": ((D_FF, D_MODEL // 2), BF16, _slot_wd),
    "gate": ((D_MODEL, D_MODEL), BF16, _slot_rows128),
    "proj": ((PLE_DIM, D_MODEL), BF16, _slot_cols128),
}


def _no_compute():
    pass


class _AllGather:
    peers = ("sibling", "x", "y")

    def __init__(self, names, shards):
        self.kinds = [_GATHERED[n.rstrip("01_")] for n in names]
        self.args = [shards[n] for n in names]
        self.out_shape = [jax.ShapeDtypeStruct(shape, dtype) for shape, dtype, _ in self.kinds]
        n = len(names)
        self.scratch = [pltpu.SemaphoreType.DMA((n, 7)), pltpu.SemaphoreType.DMA((n, 7)), pltpu.SemaphoreType.DMA((n,))]

    def _plan(self, srcs, outs, sems):
        send_sems, recv_sems, local_sems = sems
        x, y, c = _my_place()

        def slot(t, dev):
            return self.kinds[t][2](outs[t], _dev_index(*dev))

        def copy(t, k, block, to, src=None):
            return pltpu.make_async_remote_copy(
                src_ref=slot(t, block) if src is None else src, dst_ref=slot(t, block),
                send_sem=send_sems.at[t, k], recv_sem=recv_sems.at[t, k], device_id=to, device_id_type=MESH)

        return types.SimpleNamespace(
            copy=copy, core=c, me=(x, y, c), sibling=(x, y, 1 - c),
            x_chip=(1 - x, y), y_chip=(x, 1 - y), far_chip=(1 - x, 1 - y),
            via=(x ^ (1 - c), y ^ c),
            onto=(x ^ c, y ^ (1 - c)),
            k_via=1 + c, k_onto=2 - c,
            local=[pltpu.make_async_copy(srcs[t], slot(t, (x, y, c)), local_sems.at[t]) for t in range(len(srcs))])

    def start(self, srcs, outs, sems):
        p = self._plan(srcs, outs, sems)
        for cp in p.local:
            cp.start()
        for t in range(len(srcs)):
            p.copy(t, 0, p.me, p.sibling, src=srcs[t]).start()
            p.copy(t, 1, p.me, (*p.x_chip, p.core), src=srcs[t]).start()
            p.copy(t, 2, p.me, (*p.y_chip, p.core), src=srcs[t]).start()

    def mid(self, srcs, outs, sems):
        p = self._plan(srcs, outs, sems)
        for t in range(len(srcs)):
            block = (*p.via, p.core)
            p.copy(t, p.k_via, block, p.me).wait_recv()
            p.copy(t, 3, block, (*p.onto, p.core)).start()
            p.copy(t, 3 + p.k_via, block, p.sibling).start()

    def finish(self, srcs, outs, sems):
        p = self._plan(srcs, outs, sems)
        n = len(srcs)
        for t in range(n):
            block = (*p.onto, p.core)
            p.copy(t, p.k_onto, block, p.me).wait_recv()
            p.copy(t, 3 + p.k_onto, block, p.sibling).start()
        for t in range(n):
            block = (*p.far_chip, p.core)
            p.copy(t, 3, block, p.me).wait_recv()
            p.copy(t, 6, block, p.sibling).start()
        other = 1 - p.core
        for t in range(n):
            p.copy(t, 0, (*p.me[:2], other), p.me).wait_recv()
            for k, chip in ((4, p.x_chip), (5, p.y_chip), (6, p.far_chip)):
                p.copy(t, k, (*chip, other), p.me).wait_recv()
            for k in range(7):
                p.copy(t, k, p.me, p.sibling).wait_send()
        for cp in p.local:
            cp.wait()


def _jobs_only(name, job=None):
    return _launch(_no_compute, name=name, grid=(), in_specs=[], out_specs=[], out_shape=[], args=(), job=job)


def _all_gather_only(name, names, shards):
    return _launch(_no_compute, name=name, grid=(), in_specs=[], out_specs=[], out_shape=[], args=(),
                   job=_AllGather(names, shards))[1]


def _block_pool(ref, j):
    return ref.at[:, pl.ds(pl.multiple_of(j * 32, 32), 32), :]


def _block_rows128(ref, j):
    return ref.at[pl.ds(pl.multiple_of(j * 128, 128), 128), :]


def _block_gu(ref, j):
    return ref.at[j % FF_CHUNKS, j // FF_CHUNKS]


def _block_wd(ref, j):
    return ref.at[pl.ds(pl.multiple_of(j * WD_ROWS, 16), WD_ROWS), :]


def _block_cols128(ref, j):
    return ref.at[:, pl.ds(pl.multiple_of(j * 128, 128), 128)]


_SCATTERED = {
    "pool": ((N_POOL_GROUPS, 32, POOL_GROUP), _block_pool),
    "kv": ((128, 2 * KV_DIM), _block_rows128),
    "q": ((128, D_MODEL), _block_rows128),
    "o": ((128, D_MODEL), _block_rows128),
    "gu": ((FF_BLOCK, FF_PART), _block_gu),
    "wd": ((WD_ROWS, FF_PART), _block_wd),
    "guF": ((FF_BLOCK, D_MODEL), _block_gu),
    "wdF": ((WD_ROWS, D_MODEL), _block_wd),
    "gate": ((128, D_MODEL), _block_rows128),
    "proj": ((PLE_DIM, 128), _block_cols128),
}


class _SiblingSwap:
    peers = ("sibling",)

    def __init__(self, pieces):
        self.kinds = [_SCATTERED[kind] for kind, _ in pieces]
        self.args = [g for _, g in pieces]
        self.out_shape = [jax.ShapeDtypeStruct((N_CHIPS, *block), BF16) for block, _ in self.kinds]
        n = len(pieces)
        self.scratch = [pltpu.SemaphoreType.DMA((n, N_CHIPS)), pltpu.SemaphoreType.DMA((n, N_CHIPS))]

    def _copies(self, srcs, outs, sems):
        send_sems, recv_sems = sems
        x, y, c = _my_place()
        return [pltpu.make_async_remote_copy(
            src_ref=block(srcs[t], 2 * ch + 1 - c), dst_ref=outs[t].at[ch], send_sem=send_sems.at[t, ch],
            recv_sem=recv_sems.at[t, ch], device_id=(x, y, 1 - c), device_id_type=MESH)
            for t, (_, block) in enumerate(self.kinds) for ch in range(N_CHIPS)]

    def start(self, srcs, outs, sems):
        for cp in self._copies(srcs, outs, sems):
            cp.start()

    def finish(self, srcs, outs, sems):
        for cp in self._copies(srcs, outs, sems):
            cp.wait()


class _ChipScatter:
    N_BUFS = 4
    peers = ("x", "y")

    def __init__(self, pieces):
        self.kinds = [_SCATTERED[kind] for kind, _, _ in pieces]
        self.n = n = len(pieces)
        self.args = [g for _, g, _ in pieces] + [s for _, _, s in pieces]
        self.out_shape = [jax.ShapeDtypeStruct((2, *block), BF16) for block, _ in self.kinds]
        self.scratch = []
        for block, _ in self.kinds:
            self.scratch += [pltpu.VMEM((N_CHIPS, *block), BF16)] * 3 + [pltpu.VMEM((2, *block), BF16)]
        dma = pltpu.SemaphoreType.DMA
        self.scratch += [dma((n, N_CHIPS + 1)), dma((n, 2)), dma((n, 2)), dma((n,)), dma((n,)), dma((n,))]

    def _plan(self, outs, scr):
        n = self.n
        first_send, first_recv, second_send, second_recv, keep_sems = scr[self.N_BUFS * n + 1:]
        x, y, c = _my_place()
        via = (x ^ (1 - c), y ^ c)
        onto = (x ^ c, y ^ (1 - c))
        index = lambda chip: 2 * chip[0] + chip[1]
        first, second, keep = [], [], []
        for t in range(n):
            total, inbox = scr[self.N_BUFS * t + 2], scr[self.N_BUFS * t + 3]
            for k, chip in enumerate((via, (1 - x, 1 - y))):
                first.append(pltpu.make_async_remote_copy(
                    src_ref=total.at[index(chip)], dst_ref=inbox.at[k], send_sem=first_send.at[t, k],
                    recv_sem=first_recv.at[t, k], device_id=(*via, c), device_id_type=MESH))
            second.append(pltpu.make_async_remote_copy(
                src_ref=total.at[index(onto)], dst_ref=outs[t].at[1], send_sem=second_send.at[t],
                recv_sem=second_recv.at[t], device_id=(*onto, c), device_id_type=MESH))
            keep.append(pltpu.make_async_copy(total.at[index((x, y))], outs[t].at[0], keep_sems.at[t]))
        return first, second, keep, index((x, y)), index(onto)

    def start(self, ins, outs, scr):
        n = self.n
        load_sems = scr[self.N_BUFS * n]
        c = lax.axis_index("c")
        loads = []
        for t, (_, block) in enumerate(self.kinds):
            mine, theirs = scr[self.N_BUFS * t], scr[self.N_BUFS * t + 1]
            loads += [pltpu.make_async_copy(block(ins[t], 2 * ch + c), mine.at[ch], load_sems.at[t, ch])
                      for ch in range(N_CHIPS)]
            loads.append(pltpu.make_async_copy(ins[n + t], theirs, load_sems.at[t, N_CHIPS]))
        for cp in loads:
            cp.start()
        for cp in loads:
            cp.wait()
        for t in range(n):
            mine, theirs, total = scr[self.N_BUFS * t:self.N_BUFS * t + 3]
            for ch in range(N_CHIPS):
                total[ch] = (mine[ch].astype(F32) + theirs[ch].astype(F32)).astype(BF16)
        for cp in self._plan(outs, scr)[0]:
            cp.start()

    def mid(self, ins, outs, scr):
        first, second, keep, me, onto = self._plan(outs, scr)
        for cp in first:
            cp.wait_recv()
        for t in range(self.n):
            total, inbox = scr[self.N_BUFS * t + 2], scr[self.N_BUFS * t + 3]
            for k, slot in enumerate((me, onto)):
                total[slot] = (total[slot].astype(F32) + inbox[k].astype(F32)).astype(BF16)
        for cp in second + keep:
            cp.start()

    def finish(self, ins, outs, scr):
        first, second, keep, _, _ = self._plan(outs, scr)
        for cp in first:
            cp.wait_send()
        for cp in second + keep:
            cp.wait()


class _Jobs:
    def __init__(self, *jobs):
        self.jobs = jobs
        together = {p for j in jobs for p in j.peers}
        self.peers = tuple(p for p in _PEER_SETS[0] if p in together)
        self.args = [a for j in jobs for a in j.args]
        self.out_shape = [o for j in jobs for o in j.out_shape]
        self.scratch = [s for j in jobs for s in j.scratch]

    def _split(self, refs, attr):
        at = 0
        for j in self.jobs:
            n = len(getattr(j, attr))
            yield refs[at:at + n]
            at += n

    def _each(self, ins, outs, scr):
        return zip(self.jobs, self._split(ins, "args"), self._split(outs, "out_shape"), self._split(scr, "scratch"))

    def start(self, ins, outs, scr):
        for j, i, o, s in self._each(ins, outs, scr):
            j.start(i, o, s)

    def mid(self, ins, outs, scr):
        for j, i, o, s in self._each(ins, outs, scr):
            if hasattr(j, "mid"):
                j.mid(i, o, s)

    def finish(self, ins, outs, scr):
        for j, i, o, s in self._each(ins, outs, scr):
            j.finish(i, o, s)

    def split_outputs(self, outs):
        return list(self._split(outs, "out_shape"))


def _adamw_math(w, g, m, v):
    m = ADAM_B1 * m + (1.0 - ADAM_B1) * g
    v = ADAM_B2 * v + (1.0 - ADAM_B2) * (g * g)
    m_hat = m / (1.0 - ADAM_B1 ** ADAM_STEP)
    v_hat = v / (1.0 - ADAM_B2 ** ADAM_STEP)
    delta = -ADAM_LR * (m_hat / (jnp.sqrt(v_hat) + ADAM_EPS) + ADAM_WD * w)
    return delta, m, v


def _adamw(name, w, m, v, landings, n_col_blocks=1, job=None):
    n_slots, r, c = landings[0].shape
    grid = (w.shape[0] // r, n_col_blocks)

    def body(w_ref, m_ref, v_ref, *rest):
        l_refs, (g_ref, d_ref, nm_ref, nv_ref) = rest[:len(landings)], rest[len(landings):]
        step = pl.program_id(0) * n_col_blocks + pl.program_id(1)
        for idx, l_ref in enumerate(l_refs):
            @pl.when(step == idx)
            def _(l_ref=l_ref):
                g = l_ref[0].astype(F32)
                for s in range(1, n_slots):
                    g = g + l_ref[s].astype(F32)
                g_ref[...] = g
                d_ref[...], nm_ref[...], nv_ref[...] = _adamw_math(w_ref[...], g, m_ref[...], v_ref[...])

    spec = pl.BlockSpec((r, c), lambda a, b: (a, b))
    return _launch(
        body, name=f"adamw_{name}", grid=grid,
        in_specs=[spec, spec, spec] + [_full_spec((n_slots, r, c))] * len(landings),
        out_specs=[spec] * 4, out_shape=[jax.ShapeDtypeStruct(w.shape, F32)] * 4,
        args=(w, m, v, *landings), vmem=VMEM_BIG, job=job)


_SMALL = (("pre_mix_g", SV_PRE_MIX, 2), ("post_mix_g", SV_POST_MIX, 2), ("pre_ffn_g", SV_PRE_FFN, 2),
          ("post_ffn_g", SV_POST_FFN, 2), ("ple_g", SV_PLE, 2), ("ple_post_g", SV_PLE_POST, 2), ("kv_g", SV_KV, 1),
          ("pool_scale", SV_POOL_SCALE, 1), ("sinks", SV_SINKS, 1))


def _small_all_reduce(part):
    def body(part_ref, total_ref, buf, send_sems, recv_sems):
        x, y, c = _my_place()
        me = _dev_index(x, y, c)
        buf[me] = part_ref[...]
        copies = [pltpu.make_async_remote_copy(
            src_ref=buf.at[me], dst_ref=buf.at[me], send_sem=send_sems.at[r - 1], recv_sem=recv_sems.at[r - 1],
            device_id=_peer_by_relation(r), device_id_type=MESH) for r in range(1, N_DEV)]
        for cp in copies:
            cp.start()
        for cp in copies:
            cp.wait()
        g = buf[0]
        for s in range(1, N_DEV):
            g = g + buf[s]
        total_ref[...] = g

    slab = jax.ShapeDtypeStruct((SV_ROWS, D_MODEL), F32)
    (total,), _ = _launch(
        body, name="small_all_reduce", grid=(1,), in_specs=[_full_spec(slab.shape)], out_specs=[_full_spec(slab.shape)],
        out_shape=[slab],
        scratch_shapes=[pltpu.VMEM((N_DEV, SV_ROWS, D_MODEL), F32), pltpu.SemaphoreType.DMA((N_DEV - 1,)),
                        pltpu.SemaphoreType.DMA((N_DEV - 1,))],
        args=(part,))
    return total


def _small_adamw(total, params):
    flat = [a for name, _, _ in _SMALL for a in params[name]]
    n_in = 1 + len(flat)

    def body(*refs):
        total, wmv = refs[0], refs[1:n_in]
        loss_ref, outs = refs[n_in], refs[n_in + 1:]
        me = _dev_index(*_my_place())
        loss_ref[...] = total[SV_LOSS:SV_LOSS + 1, 0:1]
        for idx, (name, row, n_rows) in enumerate(_SMALL):
            w_ref, m_ref, v_ref = wmv[3 * idx:3 * idx + 3]
            g_ref, d_ref, nm_ref, nv_ref = outs[4 * idx:4 * idx + 4]
            if name == "pool_scale":
                g = total[row:row + 1, pl.ds(pl.multiple_of(me * 128, 128), 128)]
            else:
                g = total[row:row + n_rows, 0:w_ref.shape[1]]
            g_ref[...] = g
            d_ref[...], nm_ref[...], nv_ref[...] = _adamw_math(w_ref[...], g, m_ref[...], v_ref[...])

    out_shape = [jax.ShapeDtypeStruct((1, 1), F32)]
    for name, _, _ in _SMALL:
        out_shape += [jax.ShapeDtypeStruct(params[name][0].shape, F32)] * 4
    res, _ = _launch(
        body, name="small_adamw", grid=(1,),
        in_specs=[_full_spec(a.shape) for a in (total, *flat)], out_specs=[_full_spec(s.shape) for s in out_shape],
        out_shape=out_shape, args=(total, *flat))
    return res[0], {name: res[1 + 4 * idx:5 + 4 * idx] for idx, (name, _, _) in enumerate(_SMALL)}


def _local_step(x, p, tgt, gains, sinks, shards, weights):
    row = lambda first_row, layer: _Gain(gains, first_row + layer)
    gather = lambda *names: _AllGather(names, shards)
    g_pre_mix, g_post_mix, g_pre_ffn, g_post_ffn = SV_PRE_MIX, SV_POST_MIX, SV_PRE_FFN, SV_POST_FFN
    g_ple, g_ple_post, g_kv = SV_PLE, SV_PLE_POST, _Gain(gains, SV_KV)

    wp, scale, wgu0 = _all_gather_only("gather_first", ("pool", "scale", "gu0"), shards)
    wgu0 = [wgu0]
    (x1_0, h2_0, yraw, dpool), wd0 = _fwd_pool_mixer(
        x, row(g_pre_mix, 0), wp, scale, row(g_post_mix, 0), row(g_pre_ffn, 0), job=gather("wd0"))
    (gs0, us0, f0, x2_0, h3_0), (wgate0, wproj0, wkv, wq, wgu1_a) = _fwd_ffn(
        0, h2_0, x1_0, wgu0, wd0, row(g_post_ffn, 0), row(g_ple, 0),
        job=gather("gate0", "proj0", "kv", "q", "guh1_0"))
    (x3_0, z0, pe0), (wo,) = _fwd_ple(0, x2_0, h3_0, p[0], wgate0, wproj0, row(g_ple_post, 0), job=gather("o"))
    (hk, h1, q, kv), (wd1_a,) = _fwd_qkv(x3_0, g_kv, row(g_pre_mix, 1), wkv, wq, job=gather("wdh1_0"))
    front = ((ATT_BLOCK, 0), (0, 0))
    kpad = jnp.pad(kv[:, :KV_DIM], front)
    vpad = jnp.pad(kv[:, KV_DIM:], front)
    (attn,), (wgu1_b,) = _fwd_attention(q, kpad, vpad, sinks, job=gather("guh1_1"))
    (y1, x1_1, h2_1), (wd1_b,) = _fwd_attn_out(attn, x3_0, wo, row(g_post_mix, 1), row(g_pre_ffn, 1),
                                               job=gather("wdh1_1"))
    wgu1, wd1 = [wgu1_a, wgu1_b], [wd1_a, wd1_b]
    (gs1, us1, f1, x2_1, h3_1), (wgate1, wproj1) = _fwd_ffn(
        1, h2_1, x1_1, wgu1, wd1, row(g_post_ffn, 1), row(g_ple, 1), job=gather("gate1", "proj1"))

    produced, swapped, landed = {}, {}, {}

    def kind_of(name):
        return name.rstrip("0123_")

    def carry(swap=(), spread=()):
        jobs = []
        if swap:
            jobs.append(_SiblingSwap([(kind_of(n), produced[n]) for n in swap]))
        if spread:
            jobs.append(_ChipScatter([(kind_of(n), produced[n], swapped[n]) for n in spread]))
        return _Jobs(*jobs)

    def carried(jobs, outs, swap=(), spread=()):
        parts = jobs.split_outputs(outs)
        if swap:
            swapped.update(zip(swap, parts[0]))
        if spread:
            landed.update(zip(spread, parts[-1]))

    def hosted(call, *args, swap=(), spread=()):
        jobs = carry(swap, spread)
        outs, job_outs = call(*args, job=jobs)
        carried(jobs, job_outs, swap, spread)
        return outs

    ffn_q = lambda layer, qtr: (f"gu{layer}_{qtr}", f"wd{layer}_{qtr}")

    dx2_1, df1, produced["gate1"], produced["proj1"], dg_ple_post1, dg_ple1, dg_post_ffn1, loss = hosted(
        _ple_loss_bwd, 1, x2_1, h3_1, p[1], f1, tgt, wgate1, wproj1, row(g_ple_post, 1), row(g_ple, 1),
        row(g_post_ffn, 1))
    dh2_1, dg1, du1, a1 = hosted(_bwd_ffn_act, 1, df1, gs1, us1, wgu1, wd1, swap=("gate1", "proj1"))
    produced["guF1"], produced["wdF1"] = hosted(_bwd_ffn_dw, 1, 0, 1, h2_1, df1, dg1, du1, a1,
                                                spread=("gate1", "proj1"))
    dx1_1, dattn, produced["o"], dg_pre_ffn1, dg_post_mix1 = hosted(
        _bwd_attn_out, dx2_1, dh2_1, x1_1, y1, attn, wo, row(g_pre_ffn, 1), row(g_post_mix, 1),
        swap=("guF1", "wdF1"))
    dq, dkpad, dvpad, dsinks = hosted(_bwd_attention, q, dattn, kpad, vpad, sinks, spread=("guF1",))
    dkv = jnp.concatenate([dkpad[ATT_BLOCK:], dvpad[ATT_BLOCK:]], axis=1).astype(BF16)
    dx3_0, produced["q"], produced["kv"], dg_pre_mix1, dg_kv = hosted(
        _bwd_qkv, dx1_1, dq, dkv, x3_0, h1, hk, wq, wkv, row(g_pre_mix, 1), g_kv, swap=("o",), spread=("wdF1",))
    for name in ("gu", "wd"):
        whole = landed.pop(f"{name}F1")
        for half in range(FF_PARTS):
            landed[f"{name}1_{half}"] = whole[..., half * FF_PART:(half + 1) * FF_PART]
    dx2_0, df0, produced["gate0"], produced["proj0"], dg_ple_post0, dg_ple0, dg_post_ffn0 = hosted(
        _bwd_ple, 0, dx3_0, x2_0, z0, pe0, h3_0, p[0], f0, wgate0, row(g_ple_post, 0), row(g_ple, 0),
        row(g_post_ffn, 0), swap=("q", "kv"))
    dh2_0, dg0, du0, a0 = hosted(_bwd_ffn_act, 0, df0, gs0, us0, wgu0, wd0,
                                 swap=("gate0", "proj0"), spread=("o", "q", "kv"))
    part_hosts = [dict(spread=("gate0", "proj0")), dict(swap=ffn_q(0, 0))]
    for part in range(FF_PARTS):
        produced[f"gu0_{part}"], produced[f"wd0_{part}"] = hosted(
            _bwd_ffn_dw, 0, part, FF_PARTS, h2_0, df0, dg0, du0, a0, **part_hosts[part])
    grad_x, produced["pool"], dscale, dg_pre_ffn0, dg_post_mix0, dg_pre_mix0 = hosted(
        _bwd_pool_mixer, dx2_0, dh2_0, x1_0, x, yraw, dpool, wp, scale, row(g_pre_ffn, 0), row(g_post_mix, 0),
        row(g_pre_mix, 0), swap=ffn_q(0, 1), spread=ffn_q(0, 0))

    def update(name, n_col_blocks=1, pieces=None, swap=(), spread=()):
        w, m, v = weights[name]
        rows = w.size // w.shape[-1]
        flat = [landed[n].reshape(landed[n].shape[0], -1, landed[n].shape[-1])
                for n in (pieces or [kind_short[name]])]
        outs = hosted(_adamw, name, w.reshape(rows, -1), m.reshape(rows, -1), v.reshape(rows, -1), flat,
                      n_col_blocks, swap=swap, spread=spread)
        return [o.reshape(w.shape) for o in outs]

    kind_short = {"w_q": "q", "w_kv": "kv", "w_o": "o", "pool_w": "pool"}
    upd = {}
    hosted(_jobs_only, "scatter_tail0", swap=("pool",), spread=ffn_q(0, 1))
    hosted(_jobs_only, "scatter_tail1", spread=("pool",))
    upd["w_ple_gate"] = update("w_ple_gate", pieces=("gate0", "gate1"))
    upd["w_ple_proj"] = update("w_ple_proj", pieces=("proj0", "proj1"))
    for name in ("w_q", "w_kv", "w_o", "pool_w"):
        upd[name] = update(name)
    upd["w_gu"] = update("w_gu", FF_PARTS,
                         pieces=[f"gu{layer}_{qtr}" for layer in range(2) for qtr in range(FF_PARTS)])
    upd["w_gu"] = [jnp.swapaxes(a, 1, 2) for a in upd["w_gu"]]
    upd["w_down"] = update("w_down", FF_PARTS,
                           pieces=[f"wd{layer}_{qtr}" for layer in range(2) for qtr in range(FF_PARTS)])

    lanes = lambda a: jnp.pad(a, ((0, 0), (0, D_MODEL - a.shape[1])))
    small = jnp.concatenate([
        dg_pre_mix0, dg_pre_mix1, dg_post_mix0, dg_post_mix1, dg_pre_ffn0, dg_pre_ffn1, dg_post_ffn0, dg_post_ffn1,
        dg_ple0, dg_ple1, dg_ple_post0, dg_ple_post1, dg_kv, dscale, lanes(dsinks[:, :N_HEADS]), lanes(loss)], axis=0)
    return grad_x, upd, small


def kernel(x, p, pre_mix_g, post_mix_g, pre_ffn_g, post_ffn_g, pool_w, pool_scale, kv_g, w_kv, w_q, sinks, w_o, w_gu, w_down, ple_g, w_ple_gate, w_ple_proj, ple_post_g, loss_target, m_pre_mix_g, m_post_mix_g, m_pre_ffn_g, m_post_ffn_g, m_pool_w, m_pool_scale, m_kv_g, m_w_kv, m_w_q, m_sinks, m_w_o, m_w_gu, m_w_down, m_ple_g, m_w_ple_gate, m_w_ple_proj, m_ple_post_g, v_pre_mix_g, v_post_mix_g, v_pre_ffn_g, v_post_ffn_g, v_pool_w, v_pool_scale, v_kv_g, v_w_kv, v_w_q, v_sinks, v_w_o, v_w_gu, v_w_down, v_ple_g, v_w_ple_gate, v_w_ple_proj, v_ple_post_g):
    shards = {"pool": pool_w[0].astype(BF16), "scale": pool_scale, "kv": w_kv.astype(BF16),
              "q": w_q[0].astype(BF16), "o": w_o[0].astype(BF16)}
    for layer in range(2):
        shards[f"gu{layer}"] = w_gu[layer].T.astype(BF16)
        shards[f"wd{layer}"] = w_down[layer].astype(BF16)
        for half in range(2):
            cols = slice(half * D_MODEL // 2, (half + 1) * D_MODEL // 2)
            shards[f"guh{layer}_{half}"] = shards[f"gu{layer}"][:, cols]
            shards[f"wdh{layer}_{half}"] = shards[f"wd{layer}"][:, cols]
        shards[f"gate{layer}"] = w_ple_gate[layer].astype(BF16)
        shards[f"proj{layer}"] = w_ple_proj[layer].astype(BF16)
    gains = jnp.concatenate([pre_mix_g, post_mix_g, pre_ffn_g, post_ffn_g, ple_g, ple_post_g, kv_g[None, :]],
                            axis=0).reshape(-1, 1, D_MODEL)
    weights = {"pool_w": (pool_w, m_pool_w, v_pool_w), "w_kv": (w_kv, m_w_kv, v_w_kv), "w_q": (w_q, m_w_q, v_w_q),
               "w_o": (w_o, m_w_o, v_w_o), "w_down": (w_down, m_w_down, v_w_down),
               "w_gu": tuple(jnp.swapaxes(a, 1, 2) for a in (w_gu, m_w_gu, v_w_gu)),
               "w_ple_gate": (w_ple_gate, m_w_ple_gate, v_w_ple_gate),
               "w_ple_proj": (w_ple_proj, m_w_ple_proj, v_w_ple_proj)}
    grad_x, upd, small = _local_step(x[0], p[:, 0], loss_target[0], gains, sinks, shards, weights)

    small_params = {
        "pre_mix_g": (pre_mix_g, m_pre_mix_g, v_pre_mix_g), "post_mix_g": (post_mix_g, m_post_mix_g, v_post_mix_g),
        "pre_ffn_g": (pre_ffn_g, m_pre_ffn_g, v_pre_ffn_g), "post_ffn_g": (post_ffn_g, m_post_ffn_g, v_post_ffn_g),
        "ple_g": (ple_g, m_ple_g, v_ple_g), "ple_post_g": (ple_post_g, m_ple_post_g, v_ple_post_g),
        "kv_g": (kv_g[None, :], m_kv_g[None, :], v_kv_g[None, :]),
        "pool_scale": (pool_scale, m_pool_scale, v_pool_scale), "sinks": (sinks, m_sinks, v_sinks)}
    loss, small_upd = _small_adamw(_small_all_reduce(small), small_params)
    small_upd["kv_g"] = [a[0] for a in small_upd["kv_g"]]
    upd.update(small_upd)

    names = ["pre_mix_g", "post_mix_g", "pre_ffn_g", "post_ffn_g", "pool_w", "pool_scale", "kv_g", "w_kv", "w_q",
             "sinks", "w_o", "w_gu", "w_down", "ple_g", "w_ple_gate", "w_ple_proj", "ple_post_g"]
    outs = [loss[0, 0], grad_x[None]]
    for kind in range(4):
        outs += [upd[n][kind] for n in names]
    return tuple(outs)
```

```python
import functools
import types

import jax
import jax.numpy as jnp
from jax import lax
from jax.experimental import pallas as pl
from jax.experimental.pallas import tpu as pltpu

F32 = jnp.float32
BF16 = jnp.bfloat16

N_DEV = 8
D_MODEL = 1024
N_POOL_GROUPS = 4
POOL_GROUP = 256
POOL_HALO = 16
HEAD_DIM = 64
N_HEADS = 16
N_KV_HEADS = 4
GQA_GROUP = 4
KV_DIM = N_KV_HEADS * HEAD_DIM
ATT_BLOCK = 128
D_FF = 2816
FF_CHUNKS = 4
FF_BLOCK = D_FF // FF_CHUNKS
WD_ROWS = D_FF // N_DEV
FF_PARTS = 2
FF_PART = D_MODEL // FF_PARTS
N_CHIPS = 4
PLE_DIM = 256
EPS = 1e-6
NEG_INF = -1e30
ATT_SCALE = HEAD_DIM ** -0.5

ADAM_LR = 0.001
ADAM_B1 = 0.9
ADAM_B2 = 0.999
ADAM_EPS = 1e-08
ADAM_WD = 0.01
ADAM_STEP = 10

ROW_TILE = 512
FFN_ROW_TILE = 512
FFN_SUB_TILES = 1
VMEM_BIG = 60 * 1024 * 1024
VMEM_MID = 56 * 1024 * 1024
HBM_PIN_ELEMS = 1024

SV_ROWS = 16
SV_PRE_MIX, SV_POST_MIX, SV_PRE_FFN, SV_POST_FFN, SV_PLE, SV_PLE_POST = 0, 2, 4, 6, 8, 10
SV_KV, SV_POOL_SCALE, SV_SINKS, SV_LOSS = 12, 13, 14, 15

MESH = pl.DeviceIdType.MESH
ANY = pl.BlockSpec(memory_space=pl.ANY)


def _dot(a, b):
    return jnp.dot(a, b, preferred_element_type=F32)


def _dot_nt(a, b):
    return lax.dot_general(a, b, (((1,), (1,)), ((), ())), preferred_element_type=F32)


def _dot_tn(a, b):
    return lax.dot_general(a, b, (((0,), (0,)), ((), ())), preferred_element_type=F32)


def _rstd(x):
    return lax.rsqrt(jnp.mean(x * x, axis=-1, keepdims=True) + EPS)


def _rms(x, g):
    return x * _rstd(x) * g


def _rms_bwd(x, g, dy):
    r = _rstd(x)
    n = x * r
    dn = dy * g
    dx = r * (dn - n * jnp.mean(dn * n, axis=-1, keepdims=True))
    dg = jnp.sum(dy * n, axis=0, keepdims=True)
    return dx, dg


def _add_all(terms):
    return functools.reduce(jnp.add, terms)


def _sigmoid(x):
    return 1.0 / (1.0 + jnp.exp(-x))


def _acc(ref, val, first):
    @pl.when(first)
    def _():
        ref[...] = val

    @pl.when(jnp.logical_not(first))
    def _():
        ref[...] += val


def _pool_counts(row0, rows):
    t = row0 + lax.broadcasted_iota(jnp.int32, (rows, D_MODEL), 0) + 1
    grp = lax.broadcasted_iota(jnp.int32, (rows, D_MODEL), 1) // POOL_GROUP
    win = jnp.left_shift(2, grp)
    return jnp.minimum(t, win).astype(F32)


def _window_sums(ext, shift_of):
    outs = []
    s = ext
    for gi in range(N_POOL_GROUPS):
        s = s + pltpu.roll(s, shift_of(1 << gi), axis=0)
        outs.append(s[:, :POOL_GROUP])
        s = s[:, POOL_GROUP:]
    return jnp.concatenate(outs, axis=1)


def _cparams(n_axes, vmem, collective_id=None):
    return pltpu.CompilerParams(dimension_semantics=("arbitrary",) * n_axes, vmem_limit_bytes=vmem,
                                collective_id=collective_id)


_PEER_SETS = (("sibling", "x", "y"), ("sibling",), ("x", "y"))


def _meet(peers):
    x, y, c = lax.axis_index("x"), lax.axis_index("y"), lax.axis_index("c")
    device = {"sibling": (x, y, 1 - c), "x": (1 - x, y, c), "y": (x, 1 - y, c)}
    barrier = pltpu.get_barrier_semaphore()
    for peer in peers:
        pl.semaphore_signal(barrier, inc=1, device_id=device[peer], device_id_type=pl.DeviceIdType.MESH)
    pl.semaphore_wait(barrier, len(peers))


def _row_spec(cols, tm=ROW_TILE):
    return pl.BlockSpec((tm, cols), lambda i: (i, 0))


def _full_spec(shape):
    zeros = (0,) * len(shape)
    return pl.BlockSpec(shape, lambda *_: zeros)


def _vec_spec():
    return _full_spec((1, D_MODEL))


def _column_ranges(parts):
    ends = [0]
    for part in parts:
        ends.append(ends[-1] + part.shape[-1])
    return list(zip(ends[:-1], ends[1:]))


class _Gain:
    def __init__(self, stacked, layer):
        self.stacked, self.layer = stacked, layer

    def spec(self):
        layer = self.layer
        return pl.BlockSpec((None, 1, D_MODEL), lambda *_: (layer, 0, 0))


def _in_hbm(a):
    return pltpu.with_memory_space_constraint(a, pltpu.HBM) if a.size >= HBM_PIN_ELEMS else a


def _out_in_hbm(s):
    return pltpu.HBM(s.shape, s.dtype) if s.size >= HBM_PIN_ELEMS else s


def _launch(body, *, name, grid, in_specs, out_specs, out_shape, args, scratch_shapes=(), vmem=VMEM_MID, job=None):
    in_specs = [a.spec() if isinstance(a, _Gain) else s for s, a in zip(in_specs, args)]
    args = [_in_hbm(a.stacked if isinstance(a, _Gain) else a) for a in args]
    n_in, n_out, n_scr = len(args), len(out_shape), len(scratch_shapes)
    if job is not None and not job.args:
        job = None
    j_args, j_out, j_scr = ([], [], []) if job is None else ([_in_hbm(a) for a in job.args], job.out_shape, job.scratch)

    def run(*refs):
        groups, at = [], 0
        for n in (n_in, len(j_args), n_out, len(j_out), n_scr, len(j_scr)):
            groups.append(refs[at:at + n])
            at += n
        ins, j_ins, outs, j_outs, scr, j_sems = groups

        def begin():
            _meet(job.peers)
            job.start(j_ins, j_outs, j_sems)

        if job is None:
            body(*ins, *outs, *scr)
        elif not grid:
            begin()
            job.mid(j_ins, j_outs, j_sems)
            body(*ins, *outs, *scr)
            job.finish(j_ins, j_outs, j_sems)
        else:
            ids = [pl.program_id(a) for a in range(len(grid))]
            first = functools.reduce(jnp.logical_and, [i == 0 for i in ids])
            half = functools.reduce(jnp.logical_and, [ids[0] == grid[0] // 2] + [i == 0 for i in ids[1:]])
            last = functools.reduce(jnp.logical_and, [i == g - 1 for i, g in zip(ids, grid)])
            pl.when(first)(begin)
            pl.when(half)(lambda: job.mid(j_ins, j_outs, j_sems))
            body(*ins, *outs, *scr)
            pl.when(last)(lambda: job.finish(j_ins, j_outs, j_sems))

    res = pl.pallas_call(
        run, name=name, grid=grid,
        in_specs=list(in_specs) + [ANY] * len(j_args), out_specs=list(out_specs) + [ANY] * len(j_out),
        out_shape=[_out_in_hbm(s) for s in list(out_shape) + list(j_out)],
        scratch_shapes=list(scratch_shapes) + list(j_scr),
        compiler_params=_cparams(len(grid), vmem, None if job is None else _PEER_SETS.index(job.peers)),
    )(*args, *j_args)
    return res[:n_out], res[n_out:]


def _fwd_pool_mixer(x, g_pre, wp, scale, g_post, g_ffn, job=None):
    T = x.shape[0]
    tm = ROW_TILE
    nt = T // tm

    def body(x_ref, gpre_ref, wp_ref, sc_ref, gpost_ref, gffn_ref, x1_ref, h2_ref, yraw_ref, d_ref, carry):
        i = pl.program_id(0)

        @pl.when(i == 0)
        def _():
            carry[...] = jnp.zeros_like(carry)

        xv = x_ref[...]
        h = _rms(xv, gpre_ref[...])
        ext = jnp.concatenate([carry[...], h], axis=0)
        carry[...] = h[tm - POOL_HALO:, :]
        sums = _window_sums(ext, lambda k: k)[POOL_HALO:, :]
        d = sums / _pool_counts(i * tm, tm) - h
        db = d.astype(BF16)
        d_ref[...] = db
        yraw = jnp.concatenate(
            [_dot(db[:, g * POOL_GROUP:(g + 1) * POOL_GROUP], wp_ref[g]) for g in range(N_POOL_GROUPS)], axis=1)
        yraw_ref[...] = yraw.astype(BF16)
        x1 = xv + _rms(yraw * sc_ref[...], gpost_ref[...])
        x1_ref[...] = x1
        h2_ref[...] = _rms(x1, gffn_ref[...]).astype(BF16)

    return _launch(
        body, name="fwd_pool_mixer", grid=(nt,),
        in_specs=[_row_spec(D_MODEL), _vec_spec(), _full_spec((N_POOL_GROUPS, POOL_GROUP, POOL_GROUP)), _vec_spec(),
                  _vec_spec(), _vec_spec()],
        out_specs=[_row_spec(D_MODEL)] * 4,
        out_shape=[jax.ShapeDtypeStruct((T, D_MODEL), F32)] + [jax.ShapeDtypeStruct((T, D_MODEL), BF16)] * 3,
        scratch_shapes=[pltpu.VMEM((POOL_HALO, D_MODEL), F32)],
        args=(x, g_pre, wp, scale, g_post, g_ffn), job=job)


def _fwd_ffn(layer, h2, x1, wgu, wd, g_post, g_ple, job=None):
    T = h2.shape[0]
    tm = min(FFN_ROW_TILE, T)
    nt = T // tm
    sub = tm // FFN_SUB_TILES
    last = FF_CHUNKS - 1
    n_gu, n_wd = len(wgu), len(wd)
    gu_cols = _column_ranges(wgu)

    def body(h2_ref, x1_ref, *refs):
        wgu_refs, wd_refs = refs[:n_gu], refs[n_gu:n_gu + n_wd]
        gpost_ref, gple_ref, gs_ref, us_ref, f_ref, x2_ref, h3_ref, acc = refs[n_gu + n_wd:]
        k = pl.program_id(0)
        i = pl.program_id(1)
        rows = pl.ds(pl.multiple_of(i * tm, tm), tm)
        parts = []
        for s in range(FFN_SUB_TILES):
            r = pl.ds(s * sub, sub)
            g = _add_all([_dot_nt(h2_ref[r, c0:c1], w[0]) for (c0, c1), w in zip(gu_cols, wgu_refs)])
            u = _add_all([_dot_nt(h2_ref[r, c0:c1], w[1]) for (c0, c1), w in zip(gu_cols, wgu_refs)])
            gs_ref[r, :] = g.astype(BF16)
            us_ref[r, :] = u.astype(BF16)
            a = (g * _sigmoid(g) * u).astype(BF16)
            parts.append(jnp.concatenate([_dot(a, w[...]) for w in wd_refs], axis=1))
        part = jnp.concatenate(parts, axis=0)

        @pl.when(k == 0)
        def _():
            acc[rows, :] = part

        @pl.when(jnp.logical_and(k > 0, k < last))
        def _():
            acc[rows, :] += part

        @pl.when(k == last)
        def _():
            f = acc[rows, :] + part
            f_ref[...] = f.astype(BF16)
            x2 = x1_ref[...] + _rms(f, gpost_ref[...])
            x2_ref[...] = x2
            h3_ref[...] = _rms(x2, gple_ref[...]).astype(BF16)

    def late(k, i):
        return (jnp.where(k == last, i, 0), 0)

    return _launch(
        body, name=f"fwd_ffn{layer}", grid=(FF_CHUNKS, nt),
        in_specs=[pl.BlockSpec((tm, D_MODEL), lambda k, i: (i, 0)), pl.BlockSpec((tm, D_MODEL), late)]
                 + [pl.BlockSpec((None, 2, FF_BLOCK, w.shape[-1]), lambda k, i: (k, 0, 0, 0)) for w in wgu]
                 + [pl.BlockSpec((FF_BLOCK, w.shape[-1]), lambda k, i: (k, 0)) for w in wd]
                 + [pl.BlockSpec((1, D_MODEL), lambda k, i: (0, 0))] * 2,
        out_specs=[pl.BlockSpec((None, tm, FF_BLOCK), lambda k, i: (k, i, 0)),
                   pl.BlockSpec((None, tm, FF_BLOCK), lambda k, i: (k, i, 0)),
                   pl.BlockSpec((tm, D_MODEL), late),
                   pl.BlockSpec((tm, D_MODEL), late),
                   pl.BlockSpec((tm, D_MODEL), late)],
        out_shape=[jax.ShapeDtypeStruct((FF_CHUNKS, T, FF_BLOCK), BF16),
                   jax.ShapeDtypeStruct((FF_CHUNKS, T, FF_BLOCK), BF16),
                   jax.ShapeDtypeStruct((T, D_MODEL), BF16),
                   jax.ShapeDtypeStruct((T, D_MODEL), F32),
                   jax.ShapeDtypeStruct((T, D_MODEL), BF16)],
        scratch_shapes=[pltpu.VMEM((T, D_MODEL), F32)],
        args=(h2, x1, *wgu, *wd, g_post, g_ple), vmem=VMEM_BIG, job=job)


def _fwd_ple(layer, x2, h3, p, wgate, wproj, g_post, job=None):
    T = x2.shape[0]
    nt = T // ROW_TILE

    def body(x2_ref, h3_ref, p_ref, wg_ref, wp_ref, gpost_ref, x3_ref, z_ref, pe_ref):
        z = _dot(h3_ref[...], wg_ref[...])
        pe = _dot(p_ref[...].astype(BF16), wp_ref[...])
        z_ref[...] = z.astype(BF16)
        pe_ref[...] = pe.astype(BF16)
        x3_ref[...] = x2_ref[...] + _rms(pe * _sigmoid(z), gpost_ref[...])

    return _launch(
        body, name=f"fwd_ple{layer}", grid=(nt,),
        in_specs=[_row_spec(D_MODEL), _row_spec(D_MODEL), _row_spec(PLE_DIM), _full_spec((D_MODEL, D_MODEL)),
                  _full_spec((PLE_DIM, D_MODEL)), _vec_spec()],
        out_specs=[_row_spec(D_MODEL)] * 3,
        out_shape=[jax.ShapeDtypeStruct((T, D_MODEL), F32)] + [jax.ShapeDtypeStruct((T, D_MODEL), BF16)] * 2,
        args=(x2, h3, p, wgate, wproj, g_post), job=job)


def _fwd_qkv(x3, g_kv, g_mix, wkv, wq, job=None):
    T = x3.shape[0]
    nt = T // ROW_TILE

    def body(x_ref, gkv_ref, gmix_ref, wkv_ref, wq_ref, hk_ref, h1_ref, q_ref, kv_ref):
        xv = x_ref[...]
        r = _rstd(xv)
        hk = (xv * r * gkv_ref[...]).astype(BF16)
        h1 = (xv * r * gmix_ref[...]).astype(BF16)
        hk_ref[...] = hk
        h1_ref[...] = h1
        kv_ref[...] = _dot(hk, wkv_ref[...]).astype(BF16)
        q_ref[...] = _dot(h1, wq_ref[...]).astype(BF16)

    return _launch(
        body, name="fwd_qkv", grid=(nt,),
        in_specs=[_row_spec(D_MODEL), _vec_spec(), _vec_spec(), _full_spec((D_MODEL, 2 * KV_DIM)),
                  _full_spec((D_MODEL, D_MODEL))],
        out_specs=[_row_spec(D_MODEL), _row_spec(D_MODEL), _row_spec(D_MODEL), _row_spec(2 * KV_DIM)],
        out_shape=[jax.ShapeDtypeStruct((T, D_MODEL), BF16)] * 3 + [jax.ShapeDtypeStruct((T, 2 * KV_DIM), BF16)],
        args=(x3, g_kv, g_mix, wkv, wq), job=job)


def _alibi_slope(h):
    return 2.0 ** (-8.0 * (h + 1) / N_HEADS)


ATT_SUB = 32
ATT_GROUP_ROWS = GQA_GROUP * ATT_BLOCK


def _att_mask(n, rel_ref, off_ref):
    qi = lax.broadcasted_iota(jnp.int32, (ATT_BLOCK, 2 * ATT_BLOCK), 0)
    si = lax.broadcasted_iota(jnp.int32, (ATT_BLOCK, 2 * ATT_BLOCK), 1)
    rel = ATT_BLOCK + qi - si
    valid = (rel >= 0) & (rel < ATT_BLOCK) & ((si >= ATT_BLOCK) | (n > 0))
    rel_ref[...] = rel.astype(F32)
    off_ref[...] = jnp.where(valid, 0.0, NEG_INF)


def _att_probs(raw, relf, off, slope, sink):
    s = raw * ATT_SCALE - slope * relf + off
    m = jnp.maximum(jnp.max(s, axis=-1, keepdims=True), sink)
    e = jnp.exp(s - m)
    es = jnp.exp(sink - m)
    inv = 1.0 / (jnp.sum(e, axis=-1, keepdims=True) + es)
    return e * inv, es * inv


def _stack_heads(ref, kh):
    first = kh * GQA_GROUP
    return jnp.concatenate([ref[:, (first + g) * HEAD_DIM:(first + g + 1) * HEAD_DIM] for g in range(GQA_GROUP)], axis=0)


def _unstack_heads(stacked):
    return [stacked[g * ATT_BLOCK:(g + 1) * ATT_BLOCK, :] for g in range(GQA_GROUP)]


def _fwd_attention(q, kpad, vpad, sinks, job=None):
    T = q.shape[0]
    nb = T // ATT_BLOCK

    def body(q_ref, k_ref, v_ref, sink_ref, o_ref, s_scr, p_scr, rel_scr, off_scr):
        n = pl.program_id(0)
        start = pl.multiple_of(n * ATT_BLOCK, ATT_BLOCK)
        kw = k_ref[pl.ds(start, 2 * ATT_BLOCK), :]
        vw = v_ref[pl.ds(start, 2 * ATT_BLOCK), :]
        _att_mask(n, rel_scr, off_scr)
        outs = []
        for kh in range(N_KV_HEADS):
            kk = kw[:, kh * HEAD_DIM:(kh + 1) * HEAD_DIM]
            vv = vw[:, kh * HEAD_DIM:(kh + 1) * HEAD_DIM]
            s_scr[...] = _dot_nt(_stack_heads(q_ref, kh), kk)
            for g in range(GQA_GROUP):
                h = kh * GQA_GROUP + g
                for row0 in range(0, ATT_BLOCK, ATT_SUB):
                    rows, sub = pl.ds(g * ATT_BLOCK + row0, ATT_SUB), pl.ds(row0, ATT_SUB)
                    pr, _ = _att_probs(s_scr[rows, :], rel_scr[sub, :], off_scr[sub, :], _alibi_slope(h),
                                       sink_ref[0, h])
                    p_scr[rows, :] = pr.astype(BF16)
            outs += _unstack_heads(_dot(p_scr[...], vv))
        o_ref[...] = jnp.concatenate(outs, axis=1).astype(BF16)

    return _launch(
        body, name="fwd_attention", grid=(nb,),
        in_specs=[_row_spec(D_MODEL, ATT_BLOCK), _full_spec((T + ATT_BLOCK, KV_DIM)), _full_spec((T + ATT_BLOCK, KV_DIM)),
                  pl.BlockSpec(memory_space=pltpu.SMEM)],
        out_specs=[_row_spec(D_MODEL, ATT_BLOCK)],
        out_shape=[jax.ShapeDtypeStruct((T, D_MODEL), BF16)],
        scratch_shapes=[pltpu.VMEM((ATT_GROUP_ROWS, 2 * ATT_BLOCK), F32), pltpu.VMEM((ATT_GROUP_ROWS, 2 * ATT_BLOCK), BF16)]
                       + [pltpu.VMEM((ATT_BLOCK, 2 * ATT_BLOCK), F32)] * 2,
        args=(q, kpad, vpad, sinks), job=job)


def _fwd_attn_out(attn, x, wo, g_post, g_ffn, job=None):
    T = x.shape[0]
    nt = T // ROW_TILE

    def body(a_ref, x_ref, wo_ref, gpost_ref, gffn_ref, y_ref, x1_ref, h2_ref):
        y = _dot(a_ref[...], wo_ref[...])
        y_ref[...] = y.astype(BF16)
        x1 = x_ref[...] + _rms(y, gpost_ref[...])
        x1_ref[...] = x1
        h2_ref[...] = _rms(x1, gffn_ref[...]).astype(BF16)

    return _launch(
        body, name="fwd_attn_out", grid=(nt,),
        in_specs=[_row_spec(D_MODEL), _row_spec(D_MODEL), _full_spec((D_MODEL, D_MODEL)), _vec_spec(), _vec_spec()],
        out_specs=[_row_spec(D_MODEL)] * 3,
        out_shape=[jax.ShapeDtypeStruct((T, D_MODEL), BF16), jax.ShapeDtypeStruct((T, D_MODEL), F32),
                   jax.ShapeDtypeStruct((T, D_MODEL), BF16)],
        args=(attn, x, wo, g_post, g_ffn), job=job)


def _bwd_ple(layer, dx3, x2, z, pe, h3, p, f, wgate, g_ple_post, g_ple, g_post_ffn, job=None):
    T = x2.shape[0]
    tm = ROW_TILE
    nt = T // tm

    def body(dx3_ref, x2_ref, z_ref, pe_ref, h3_ref, p_ref, f_ref, wg_ref, gpp_ref, gp_ref, gpf_ref,
             dx2_ref, df_ref, dwg_ref, dwp_ref, dgpp_ref, dgp_ref, dgpf_ref, acc_g, acc_p):
        i = pl.program_id(0)
        first = i == 0
        dx3v = dx3_ref[...]
        gate = _sigmoid(z_ref[...].astype(F32))
        pev = pe_ref[...].astype(F32)
        de, dgpp = _rms_bwd(pev * gate, gpp_ref[...], dx3v)
        dpe = (de * gate).astype(BF16)
        dz = (de * pev * gate * (1.0 - gate)).astype(BF16)
        _acc(acc_p, _dot_tn(p_ref[...].astype(BF16), dpe), first)
        _acc(acc_g, _dot_tn(h3_ref[...], dz), first)
        dh3 = _dot_nt(dz, wg_ref[...])
        dxn, dgp = _rms_bwd(x2_ref[...], gp_ref[...], dh3)
        dx2 = dx3v + dxn
        dx2_ref[...] = dx2
        df, dgpf = _rms_bwd(f_ref[...].astype(F32), gpf_ref[...], dx2)
        df_ref[...] = df.astype(BF16)
        _acc(dgpp_ref, dgpp, first)
        _acc(dgp_ref, dgp, first)
        _acc(dgpf_ref, dgpf, first)

        @pl.when(i == nt - 1)
        def _():
            dwg_ref[...] = acc_g[...].astype(BF16)
            dwp_ref[...] = acc_p[...].astype(BF16)

    return _launch(
        body, name=f"bwd_ple{layer}", grid=(nt,),
        in_specs=[_row_spec(D_MODEL)] * 5 + [_row_spec(PLE_DIM), _row_spec(D_MODEL), _full_spec((D_MODEL, D_MODEL)),
                  _vec_spec(), _vec_spec(), _vec_spec()],
        out_specs=[_row_spec(D_MODEL), _row_spec(D_MODEL), _full_spec((D_MODEL, D_MODEL)), _full_spec((PLE_DIM, D_MODEL)),
                   _vec_spec(), _vec_spec(), _vec_spec()],
        out_shape=[jax.ShapeDtypeStruct((T, D_MODEL), F32), jax.ShapeDtypeStruct((T, D_MODEL), BF16),
                   jax.ShapeDtypeStruct((D_MODEL, D_MODEL), BF16), jax.ShapeDtypeStruct((PLE_DIM, D_MODEL), BF16)]
                  + [jax.ShapeDtypeStruct((1, D_MODEL), F32)] * 3,
        scratch_shapes=[pltpu.VMEM((D_MODEL, D_MODEL), F32), pltpu.VMEM((PLE_DIM, D_MODEL), F32)],
        args=(dx3, x2, z, pe, h3, p, f, wgate, g_ple_post, g_ple, g_post_ffn), vmem=VMEM_BIG, job=job)


def _ple_loss_bwd(layer, x2, h3, p, f, target, wgate, wproj, g_ple_post, g_ple, g_post_ffn, job=None):
    T = x2.shape[0]
    tm = ROW_TILE
    nt = T // tm

    def body(x2_ref, h3_ref, p_ref, f_ref, tgt_ref, wg_ref, wp_ref, gpp_ref, gp_ref, gpf_ref,
             dx2_ref, df_ref, dwg_ref, dwp_ref, dgpp_ref, dgp_ref, dgpf_ref, loss_ref, acc_g, acc_p):
        i = pl.program_id(0)
        first = i == 0
        h3 = h3_ref[...]
        pb = p_ref[...].astype(BF16)
        x2v = x2_ref[...]
        gate = _sigmoid(_dot(h3, wg_ref[...]))
        pev = _dot(pb, wp_ref[...])
        e = pev * gate
        err = x2v + _rms(e, gpp_ref[...]) - tgt_ref[...]
        _acc(loss_ref, 0.5 * jnp.sum(jnp.mean(err * err, axis=-1, keepdims=True), axis=0, keepdims=True), first)
        dx3v = err * (1.0 / D_MODEL)
        de, dgpp = _rms_bwd(e, gpp_ref[...], dx3v)
        dpe = (de * gate).astype(BF16)
        dz = (de * pev * gate * (1.0 - gate)).astype(BF16)
        _acc(acc_p, _dot_tn(pb, dpe), first)
        _acc(acc_g, _dot_tn(h3, dz), first)
        dxn, dgp = _rms_bwd(x2v, gp_ref[...], _dot_nt(dz, wg_ref[...]))
        dx2 = dx3v + dxn
        dx2_ref[...] = dx2
        df, dgpf = _rms_bwd(f_ref[...].astype(F32), gpf_ref[...], dx2)
        df_ref[...] = df.astype(BF16)
        _acc(dgpp_ref, dgpp, first)
        _acc(dgp_ref, dgp, first)
        _acc(dgpf_ref, dgpf, first)

        @pl.when(i == nt - 1)
        def _():
            dwg_ref[...] = acc_g[...].astype(BF16)
            dwp_ref[...] = acc_p[...].astype(BF16)

    return _launch(
        body, name=f"ple_loss_bwd{layer}", grid=(nt,),
        in_specs=[_row_spec(D_MODEL), _row_spec(D_MODEL), _row_spec(PLE_DIM), _row_spec(D_MODEL), _row_spec(D_MODEL),
                  _full_spec((D_MODEL, D_MODEL)), _full_spec((PLE_DIM, D_MODEL)), _vec_spec(), _vec_spec(), _vec_spec()],
        out_specs=[_row_spec(D_MODEL), _row_spec(D_MODEL), _full_spec((D_MODEL, D_MODEL)), _full_spec((PLE_DIM, D_MODEL)),
                   _vec_spec(), _vec_spec(), _vec_spec(), _full_spec((1, 1))],
        out_shape=[jax.ShapeDtypeStruct((T, D_MODEL), F32), jax.ShapeDtypeStruct((T, D_MODEL), BF16),
                   jax.ShapeDtypeStruct((D_MODEL, D_MODEL), BF16), jax.ShapeDtypeStruct((PLE_DIM, D_MODEL), BF16)]
                  + [jax.ShapeDtypeStruct((1, D_MODEL), F32)] * 3 + [jax.ShapeDtypeStruct((1, 1), F32)],
        scratch_shapes=[pltpu.VMEM((D_MODEL, D_MODEL), F32), pltpu.VMEM((PLE_DIM, D_MODEL), F32)],
        args=(x2, h3, p, f, target, wgate, wproj, g_ple_post, g_ple, g_post_ffn), vmem=VMEM_BIG, job=job)


def _bwd_ffn_act(layer, df, gs, us, wgu, wd, job=None):
    T = df.shape[0]
    tm = min(FFN_ROW_TILE, T)
    nt = T // tm
    sub = tm // FFN_SUB_TILES
    last = FF_CHUNKS - 1
    n_gu, n_wd = len(wgu), len(wd)
    wd_cols = _column_ranges(wd)

    def body(df_ref, gs_ref, us_ref, *refs):
        wgu_refs, wd_refs = refs[:n_gu], refs[n_gu:n_gu + n_wd]
        dh_ref, dg_ref, du_ref, a_ref, acc_h = refs[n_gu + n_wd:]
        k = pl.program_id(0)
        i = pl.program_id(1)
        rows = pl.ds(pl.multiple_of(i * tm, tm), tm)
        dhs = []
        for s in range(FFN_SUB_TILES):
            r = pl.ds(s * sub, sub)
            g = gs_ref[r, :].astype(F32)
            u = us_ref[r, :].astype(F32)
            sg = _sigmoid(g)
            silu = g * sg
            a_ref[r, :] = (silu * u).astype(BF16)
            da = _add_all([_dot_nt(df_ref[r, c0:c1], w[...]) for (c0, c1), w in zip(wd_cols, wd_refs)])
            dg = (da * u * (sg * (1.0 + g * (1.0 - sg)))).astype(BF16)
            du = (da * silu).astype(BF16)
            dg_ref[r, :] = dg
            du_ref[r, :] = du
            dhs.append(jnp.concatenate([_dot(dg, w[0]) + _dot(du, w[1]) for w in wgu_refs], axis=1))
        dh = jnp.concatenate(dhs, axis=0)

        @pl.when(k == 0)
        def _():
            acc_h[rows, :] = dh

        @pl.when(jnp.logical_and(k > 0, k < last))
        def _():
            acc_h[rows, :] += dh

        @pl.when(k == last)
        def _():
            dh_ref[...] = acc_h[rows, :] + dh

    chunk_rows = pl.BlockSpec((None, tm, FF_BLOCK), lambda k, i: (k, i, 0))
    saved = jax.ShapeDtypeStruct((FF_CHUNKS, T, FF_BLOCK), BF16)
    return _launch(
        body, name=f"bwd_ffn_act{layer}", grid=(FF_CHUNKS, nt),
        in_specs=[pl.BlockSpec((tm, D_MODEL), lambda k, i: (i, 0)), chunk_rows, chunk_rows]
                 + [pl.BlockSpec((None, 2, FF_BLOCK, w.shape[-1]), lambda k, i: (k, 0, 0, 0)) for w in wgu]
                 + [pl.BlockSpec((FF_BLOCK, w.shape[-1]), lambda k, i: (k, 0)) for w in wd],
        out_specs=[pl.BlockSpec((tm, D_MODEL), lambda k, i: (jnp.where(k == last, i, 0), 0)),
                   chunk_rows, chunk_rows, chunk_rows],
        out_shape=[jax.ShapeDtypeStruct((T, D_MODEL), F32), saved, saved, saved],
        scratch_shapes=[pltpu.VMEM((T, D_MODEL), F32)],
        args=(df, gs, us, *wgu, *wd), vmem=VMEM_BIG, job=job)


def _bwd_ffn_dw(layer, q, parts, h2, df, dg, du, a, job=None):
    T = h2.shape[0]
    width = D_MODEL // parts

    def body(h_ref, df_ref, dg_ref, du_ref, a_ref, dgu_ref, dwd_ref):
        h = h_ref[...]
        dgu_ref[0] = _dot_tn(dg_ref[...], h).astype(BF16)
        dgu_ref[1] = _dot_tn(du_ref[...], h).astype(BF16)
        dwd_ref[...] = _dot_tn(a_ref[...], df_ref[...]).astype(BF16)

    cols = pl.BlockSpec((T, width), lambda k: (0, q))
    chunk = pl.BlockSpec((None, T, FF_BLOCK), lambda k: (k, 0, 0))
    return _launch(
        body, name=f"bwd_ffn_dw{layer}_{q}", grid=(FF_CHUNKS,),
        in_specs=[cols, cols, chunk, chunk, chunk],
        out_specs=[pl.BlockSpec((None, 2, FF_BLOCK, width), lambda k: (k, 0, 0, 0)),
                   pl.BlockSpec((FF_BLOCK, width), lambda k: (k, 0))],
        out_shape=[jax.ShapeDtypeStruct((FF_CHUNKS, 2, FF_BLOCK, width), BF16),
                   jax.ShapeDtypeStruct((D_FF, width), BF16)],
        args=(h2, df, dg, du, a), vmem=VMEM_BIG, job=job)


def _bwd_attn_out(dx2, dh2, x1, y, attn, wo, g_ffn, g_post, job=None):
    T = x1.shape[0]
    nt = T // ROW_TILE

    def body(dx2_ref, dh2_ref, x1_ref, y_ref, a_ref, wo_ref, gffn_ref, gpost_ref,
             dx1_ref, da_ref, dwo_ref, dgf_ref, dgp_ref, acc):
        i = pl.program_id(0)
        first = i == 0
        dxn, dgf = _rms_bwd(x1_ref[...], gffn_ref[...], dh2_ref[...])
        dx1 = dx2_ref[...] + dxn
        dx1_ref[...] = dx1
        dy, dgp = _rms_bwd(y_ref[...].astype(F32), gpost_ref[...], dx1)
        dyb = dy.astype(BF16)
        da_ref[...] = _dot_nt(dyb, wo_ref[...]).astype(BF16)
        _acc(acc, _dot_tn(a_ref[...], dyb), first)
        _acc(dgf_ref, dgf, first)
        _acc(dgp_ref, dgp, first)

        @pl.when(i == nt - 1)
        def _():
            dwo_ref[...] = acc[...].astype(BF16)

    return _launch(
        body, name="bwd_attn_out", grid=(nt,),
        in_specs=[_row_spec(D_MODEL)] * 5 + [_full_spec((D_MODEL, D_MODEL)), _vec_spec(), _vec_spec()],
        out_specs=[_row_spec(D_MODEL), _row_spec(D_MODEL), _full_spec((D_MODEL, D_MODEL)), _vec_spec(), _vec_spec()],
        out_shape=[jax.ShapeDtypeStruct((T, D_MODEL), F32), jax.ShapeDtypeStruct((T, D_MODEL), BF16),
                   jax.ShapeDtypeStruct((D_MODEL, D_MODEL), BF16)] + [jax.ShapeDtypeStruct((1, D_MODEL), F32)] * 2,
        scratch_shapes=[pltpu.VMEM((D_MODEL, D_MODEL), F32)],
        args=(dx2, dh2, x1, y, attn, wo, g_ffn, g_post), job=job)


def _bwd_attention(q, dattn, kpad, vpad, sinks, job=None):
    T = q.shape[0]
    nb = T // ATT_BLOCK

    def body(q_ref, do_ref, k_ref, v_ref, sink_ref, dq_ref, dk_ref, dv_ref, ds_ref, s_scr, dp_scr, p_scr, dsb_scr,
             rel_scr, off_scr):
        n = pl.program_id(0)
        _att_mask(n, rel_scr, off_scr)

        @pl.when(n == 0)
        def _():
            dk_ref[...] = jnp.zeros_like(dk_ref)
            dv_ref[...] = jnp.zeros_like(dv_ref)
            ds_ref[...] = jnp.zeros_like(ds_ref)

        start = pl.multiple_of(n * ATT_BLOCK, ATT_BLOCK)
        win = pl.ds(start, 2 * ATT_BLOCK)
        kw = k_ref[win, :]
        vw = v_ref[win, :]
        lane = lax.broadcasted_iota(jnp.int32, (1, ATT_BLOCK), 1)
        dsink = jnp.zeros((1, ATT_BLOCK), F32)
        dqs, dks, dvs = [], [], []
        for kh in range(N_KV_HEADS):
            kk = kw[:, kh * HEAD_DIM:(kh + 1) * HEAD_DIM]
            vv = vw[:, kh * HEAD_DIM:(kh + 1) * HEAD_DIM]
            qs = _stack_heads(q_ref, kh)
            dos = _stack_heads(do_ref, kh)
            s_scr[...] = _dot_nt(qs, kk)
            dp_scr[...] = _dot_nt(dos, vv)
            for g in range(GQA_GROUP):
                h = kh * GQA_GROUP + g
                dsink_h = jnp.zeros((1, 1), F32)
                for row0 in range(0, ATT_BLOCK, ATT_SUB):
                    rows, sub = pl.ds(g * ATT_BLOCK + row0, ATT_SUB), pl.ds(row0, ATT_SUB)
                    pr, ps = _att_probs(s_scr[rows, :], rel_scr[sub, :], off_scr[sub, :], _alibi_slope(h),
                                        sink_ref[0, h])
                    dp = dp_scr[rows, :]
                    delta = jnp.sum(pr * dp, axis=-1, keepdims=True)
                    dsb_scr[rows, :] = (pr * (dp - delta) * ATT_SCALE).astype(BF16)
                    p_scr[rows, :] = pr.astype(BF16)
                    dsink_h = dsink_h - jnp.sum(ps * delta, axis=0, keepdims=True)
                dsink = dsink + jnp.where(lane == h, dsink_h, 0.0)
            dsb = dsb_scr[...]
            dqs += _unstack_heads(_dot(dsb, kk))
            dks.append(_dot_tn(dsb, qs))
            dvs.append(_dot_tn(p_scr[...], dos))
        dq_ref[...] = jnp.concatenate(dqs, axis=1).astype(BF16)
        dk_ref[win, :] += jnp.concatenate(dks, axis=1)
        dv_ref[win, :] += jnp.concatenate(dvs, axis=1)
        ds_ref[...] += dsink

    return _launch(
        body, name="bwd_attention", grid=(nb,),
        in_specs=[_row_spec(D_MODEL, ATT_BLOCK), _row_spec(D_MODEL, ATT_BLOCK), _full_spec((T + ATT_BLOCK, KV_DIM)),
                  _full_spec((T + ATT_BLOCK, KV_DIM)), pl.BlockSpec(memory_space=pltpu.SMEM)],
        out_specs=[_row_spec(D_MODEL, ATT_BLOCK), _full_spec((T + ATT_BLOCK, KV_DIM)), _full_spec((T + ATT_BLOCK, KV_DIM)),
                   _full_spec((1, ATT_BLOCK))],
        out_shape=[jax.ShapeDtypeStruct((T, D_MODEL), BF16), jax.ShapeDtypeStruct((T + ATT_BLOCK, KV_DIM), F32),
                   jax.ShapeDtypeStruct((T + ATT_BLOCK, KV_DIM), F32), jax.ShapeDtypeStruct((1, ATT_BLOCK), F32)],
        scratch_shapes=[pltpu.VMEM((ATT_GROUP_ROWS, 2 * ATT_BLOCK), F32)] * 2
                       + [pltpu.VMEM((ATT_GROUP_ROWS, 2 * ATT_BLOCK), BF16)] * 2
                       + [pltpu.VMEM((ATT_BLOCK, 2 * ATT_BLOCK), F32)] * 2,
        args=(q, dattn, kpad, vpad, sinks), vmem=VMEM_BIG, job=job)


def _bwd_qkv(dxres, dq, dkv, x3, h1, hk, wq, wkv, g_mix, g_kv, job=None):
    T = x3.shape[0]
    nt = T // ROW_TILE

    def body(dxr_ref, dq_ref, dkv_ref, x_ref, h1_ref, hk_ref, wq_ref, wkv_ref, gmix_ref, gkv_ref,
             dx_ref, dwq_ref, dwkv_ref, dgm_ref, dgk_ref, acc_q, acc_kv):
        i = pl.program_id(0)
        first = i == 0
        dqv = dq_ref[...]
        dkvv = dkv_ref[...]
        xv = x_ref[...]
        d1, dgm = _rms_bwd(xv, gmix_ref[...], _dot_nt(dqv, wq_ref[...]))
        d2, dgk = _rms_bwd(xv, gkv_ref[...], _dot_nt(dkvv, wkv_ref[...]))
        dx_ref[...] = dxr_ref[...] + d1 + d2
        _acc(acc_q, _dot_tn(h1_ref[...], dqv), first)
        _acc(acc_kv, _dot_tn(hk_ref[...], dkvv), first)
        _acc(dgm_ref, dgm, first)
        _acc(dgk_ref, dgk, first)

        @pl.when(i == nt - 1)
        def _():
            dwq_ref[...] = acc_q[...].astype(BF16)
            dwkv_ref[...] = acc_kv[...].astype(BF16)

    return _launch(
        body, name="bwd_qkv", grid=(nt,),
        in_specs=[_row_spec(D_MODEL), _row_spec(D_MODEL), _row_spec(2 * KV_DIM), _row_spec(D_MODEL), _row_spec(D_MODEL),
                  _row_spec(D_MODEL), _full_spec((D_MODEL, D_MODEL)), _full_spec((D_MODEL, 2 * KV_DIM)), _vec_spec(),
                  _vec_spec()],
        out_specs=[_row_spec(D_MODEL), _full_spec((D_MODEL, D_MODEL)), _full_spec((D_MODEL, 2 * KV_DIM)), _vec_spec(),
                   _vec_spec()],
        out_shape=[jax.ShapeDtypeStruct((T, D_MODEL), F32), jax.ShapeDtypeStruct((D_MODEL, D_MODEL), BF16),
                   jax.ShapeDtypeStruct((D_MODEL, 2 * KV_DIM), BF16)] + [jax.ShapeDtypeStruct((1, D_MODEL), F32)] * 2,
        scratch_shapes=[pltpu.VMEM((D_MODEL, D_MODEL), F32), pltpu.VMEM((D_MODEL, 2 * KV_DIM), F32)],
        args=(dxres, dq, dkv, x3, h1, hk, wq, wkv, g_mix, g_kv), job=job)


def _bwd_pool_mixer(dx2, dh2, x1, x, yraw, d, wp, scale, g_ffn, g_post, g_pre, job=None):
    T = x.shape[0]
    tm = ROW_TILE
    nt = T // tm

    def body(dx2_ref, dh2_ref, x1_ref, x_ref, yraw_ref, d_ref, wp_ref, sc_ref, gffn_ref, gpost_ref, gpre_ref,
             dx_ref, dwp_ref, dsc_ref, dgf_ref, dgp_ref, dgm_ref, carry, acc):
        i = pl.program_id(0)
        first = i == 0
        tile = nt - 1 - i

        @pl.when(first)
        def _():
            carry[...] = jnp.zeros_like(carry)

        dxn, dgf = _rms_bwd(x1_ref[...], gffn_ref[...], dh2_ref[...])
        dx1 = dx2_ref[...] + dxn
        yraw = yraw_ref[...].astype(F32)
        sc = sc_ref[...]
        dy, dgp = _rms_bwd(yraw * sc, gpost_ref[...], dx1)
        dsc = jnp.sum(dy * yraw, axis=0, keepdims=True)
        dyb = (dy * sc).astype(BF16)
        dv = d_ref[...]
        dds = []
        for g in range(N_POOL_GROUPS):
            cols = slice(g * POOL_GROUP, (g + 1) * POOL_GROUP)
            dds.append(_dot_nt(dyb[:, cols], wp_ref[g]))
            _acc(acc.at[g], _dot_tn(dv[:, cols], dyb[:, cols]), first)
        dd = jnp.concatenate(dds, axis=1)
        e = dd / _pool_counts(tile * tm, tm)
        ext = jnp.concatenate([e, carry[...]], axis=0)
        carry[...] = e[:POOL_HALO, :]
        sums = _window_sums(ext, lambda k: tm + POOL_HALO - k)[:tm, :]
        dxm, dgm = _rms_bwd(x_ref[...], gpre_ref[...], sums - dd)
        dx_ref[...] = dx1 + dxm
        _acc(dsc_ref, dsc, first)
        _acc(dgf_ref, dgf, first)
        _acc(dgp_ref, dgp, first)
        _acc(dgm_ref, dgm, first)

        @pl.when(i == nt - 1)
        def _():
            dwp_ref[...] = acc[...].astype(BF16)

    rev = pl.BlockSpec((tm, D_MODEL), lambda i: (nt - 1 - i, 0))
    return _launch(
        body, name="bwd_pool_mixer", grid=(nt,),
        in_specs=[rev] * 6 + [_full_spec((N_POOL_GROUPS, POOL_GROUP, POOL_GROUP))] + [_vec_spec()] * 4,
        out_specs=[rev, _full_spec((N_POOL_GROUPS, POOL_GROUP, POOL_GROUP))] + [_vec_spec()] * 4,
        out_shape=[jax.ShapeDtypeStruct((T, D_MODEL), F32),
                   jax.ShapeDtypeStruct((N_POOL_GROUPS, POOL_GROUP, POOL_GROUP), BF16)]
                  + [jax.ShapeDtypeStruct((1, D_MODEL), F32)] * 4,
        scratch_shapes=[pltpu.VMEM((POOL_HALO, D_MODEL), F32), pltpu.VMEM((N_POOL_GROUPS, POOL_GROUP, POOL_GROUP), F32)],
        args=(dx2, dh2, x1, x, yraw, d, wp, scale, g_ffn, g_post, g_pre), job=job)


def _my_place():
    return lax.axis_index("x"), lax.axis_index("y"), lax.axis_index("c")


def _dev_index(px, py, pc):
    return 4 * px + 2 * py + pc


def _peer_by_relation(r):
    x, y, c = _my_place()
    return (x ^ ((r >> 2) & 1), y ^ ((r >> 1) & 1), c ^ (r & 1))


def _slot_pool(ref, j):
    return ref.at[:, pl.ds(pl.multiple_of(j * 32, 32), 32), :]


def _slot_scale(ref, j):
    return ref.at[:, pl.ds(pl.multiple_of(j * 128, 128), 128)]


def _slot_rows128(ref, j):
    return ref.at[pl.ds(pl.multiple_of(j * 128, 128), 128), :]


def _slot_gu(ref, j):
    return ref.at[j % FF_CHUNKS, j // FF_CHUNKS]


def _slot_wd(ref, j):
    return ref.at[pl.ds(pl.multiple_of(j * WD_ROWS, 16), WD_ROWS), :]


def _slot_cols128(ref, j):
    return ref.at[:, pl.ds(pl.multiple_of(j * 128, 128), 128)]


_GATHERED = {
    "pool": ((N_POOL_GROUPS, POOL_GROUP, POOL_GROUP), BF16, _slot_pool),
    "scale": ((1, D_MODEL), F32, _slot_scale),
    "kv": ((D_MODEL, 2 * KV_DIM), BF16, _slot_rows128),
    "q": ((D_MODEL, D_MODEL), BF16, _slot_rows128),
    "o": ((D_MODEL, D_MODEL), BF16, _slot_rows128),
    "gu": ((FF_CHUNKS, 2, FF_BLOCK, D_MODEL), BF16, _slot_gu),
    "wd": ((D_FF, D_MODEL), BF16, _slot_wd),
    "guh": ((FF_CHUNKS, 2, FF_BLOCK, D_MODEL // 2), BF16, _slot_gu),
    "wdh": ((D_FF, D_MODEL // 2), BF16, _slot_wd),
    "gate": ((D_MODEL, D_MODEL), BF16, _slot_rows128),
    "proj": ((PLE_DIM, D_MODEL), BF16, _slot_cols128),
}


def _no_compute():
    pass


class _AllGather:
    peers = ("sibling", "x", "y")

    def __init__(self, names, shards):
        self.kinds = [_GATHERED[n.rstrip("01_")] for n in names]
        entries = [shards[n] if isinstance(shards[n], tuple) else (shards[n], None) for n in names]
        self.args = [array for array, _ in entries]
        self.columns = [columns for _, columns in entries]
        self.out_shape = [jax.ShapeDtypeStruct(shape, dtype) for shape, dtype, _ in self.kinds]
        n = len(names)
        self.scratch = [pltpu.SemaphoreType.DMA((n, 7)), pltpu.SemaphoreType.DMA((n, 7)), pltpu.SemaphoreType.DMA((n,))]

    def _plan(self, srcs, outs, sems):
        send_sems, recv_sems, local_sems = sems
        x, y, c = _my_place()

        def slot(t, dev):
            return self.kinds[t][2](outs[t], _dev_index(*dev))

        def copy(t, k, block, to, src=None):
            return pltpu.make_async_remote_copy(
                src_ref=slot(t, block) if src is None else src, dst_ref=slot(t, block),
                send_sem=send_sems.at[t, k], recv_sem=recv_sems.at[t, k], device_id=to, device_id_type=MESH)

        return types.SimpleNamespace(
            copy=copy, core=c, me=(x, y, c), sibling=(x, y, 1 - c),
            x_chip=(1 - x, y), y_chip=(x, 1 - y), far_chip=(1 - x, 1 - y),
            via=(x ^ (1 - c), y ^ c),
            onto=(x ^ c, y ^ (1 - c)),
            k_via=1 + c, k_onto=2 - c,
            local=[pltpu.make_async_copy(self._shard(srcs, t), slot(t, (x, y, c)), local_sems.at[t])
                   for t in range(len(srcs))])

    def _shard(self, srcs, t):
        if self.columns[t] is None:
            return srcs[t]
        first, end = self.columns[t]
        return srcs[t].at[:, first:end]

    def start(self, srcs, outs, sems):
        p = self._plan(srcs, outs, sems)
        for cp in p.local:
            cp.start()
        for t in range(len(srcs)):
            shard = self._shard(srcs, t)
            p.copy(t, 0, p.me, p.sibling, src=shard).start()
            p.copy(t, 1, p.me, (*p.x_chip, p.core), src=shard).start()
            p.copy(t, 2, p.me, (*p.y_chip, p.core), src=shard).start()

    def mid(self, srcs, outs, sems):
        p = self._plan(srcs, outs, sems)
        for t in range(len(srcs)):
            block = (*p.via, p.core)
            p.copy(t, p.k_via, block, p.me).wait_recv()
            p.copy(t, 3, block, (*p.onto, p.core)).start()
            p.copy(t, 3 + p.k_via, block, p.sibling).start()

    def finish(self, srcs, outs, sems):
        p = self._plan(srcs, outs, sems)
        n = len(srcs)
        for t in range(n):
            block = (*p.onto, p.core)
            p.copy(t, p.k_onto, block, p.me).wait_recv()
            p.copy(t, 3 + p.k_onto, block, p.sibling).start()
        for t in range(n):
            block = (*p.far_chip, p.core)
            p.copy(t, 3, block, p.me).wait_recv()
            p.copy(t, 6, block, p.sibling).start()
        other = 1 - p.core
        for t in range(n):
            p.copy(t, 0, (*p.me[:2], other), p.me).wait_recv()
            for k, chip in ((4, p.x_chip), (5, p.y_chip), (6, p.far_chip)):
                p.copy(t, k, (*chip, other), p.me).wait_recv()
            for k in range(7):
                p.copy(t, k, p.me, p.sibling).wait_send()
        for cp in p.local:
            cp.wait()


def _jobs_only(name, job=None):
    return _launch(_no_compute, name=name, grid=(), in_specs=[], out_specs=[], out_shape=[], args=(), job=job)


def _all_gather_only(name, names, shards):
    return _launch(_no_compute, name=name, grid=(), in_specs=[], out_specs=[], out_shape=[], args=(),
                   job=_AllGather(names, shards))[1]


def _block_pool(ref, j):
    return ref.at[:, pl.ds(pl.multiple_of(j * 32, 32), 32), :]


def _block_rows128(ref, j):
    return ref.at[pl.ds(pl.multiple_of(j * 128, 128), 128), :]


def _block_gu(ref, j):
    return ref.at[j % FF_CHUNKS, j // FF_CHUNKS]


def _block_wd(ref, j):
    return ref.at[pl.ds(pl.multiple_of(j * WD_ROWS, 16), WD_ROWS), :]


def _block_cols128(ref, j):
    return ref.at[:, pl.ds(pl.multiple_of(j * 128, 128), 128)]


_SCATTERED = {
    "pool": ((N_POOL_GROUPS, 32, POOL_GROUP), _block_pool),
    "kv": ((128, 2 * KV_DIM), _block_rows128),
    "q": ((128, D_MODEL), _block_rows128),
    "o": ((128, D_MODEL), _block_rows128),
    "gu": ((FF_BLOCK, FF_PART), _block_gu),
    "wd": ((WD_ROWS, FF_PART), _block_wd),
    "guF": ((FF_BLOCK, D_MODEL), _block_gu),
    "wdF": ((WD_ROWS, D_MODEL), _block_wd),
    "gate": ((128, D_MODEL), _block_rows128),
    "proj": ((PLE_DIM, 128), _block_cols128),
}


class _SiblingSwap:
    peers = ("sibling",)

    def __init__(self, pieces):
        self.kinds = [_SCATTERED[kind] for kind, _ in pieces]
        self.args = [g for _, g in pieces]
        self.out_shape = [jax.ShapeDtypeStruct((N_CHIPS, *block), BF16) for block, _ in self.kinds]
        n = len(pieces)
        self.scratch = [pltpu.SemaphoreType.DMA((n, N_CHIPS)), pltpu.SemaphoreType.DMA((n, N_CHIPS))]

    def _copies(self, srcs, outs, sems):
        send_sems, recv_sems = sems
        x, y, c = _my_place()
        return [pltpu.make_async_remote_copy(
            src_ref=block(srcs[t], 2 * ch + 1 - c), dst_ref=outs[t].at[ch], send_sem=send_sems.at[t, ch],
            recv_sem=recv_sems.at[t, ch], device_id=(x, y, 1 - c), device_id_type=MESH)
            for t, (_, block) in enumerate(self.kinds) for ch in range(N_CHIPS)]

    def start(self, srcs, outs, sems):
        for cp in self._copies(srcs, outs, sems):
            cp.start()

    def finish(self, srcs, outs, sems):
        for cp in self._copies(srcs, outs, sems):
            cp.wait()


class _ChipScatter:
    N_BUFS = 4
    peers = ("x", "y")

    def __init__(self, pieces):
        self.kinds = [_SCATTERED[kind] for kind, _, _ in pieces]
        self.n = n = len(pieces)
        self.args = [g for _, g, _ in pieces] + [s for _, _, s in pieces]
        self.out_shape = [jax.ShapeDtypeStruct((2, *block), BF16) for block, _ in self.kinds]
        self.scratch = []
        for block, _ in self.kinds:
            self.scratch += [pltpu.VMEM((N_CHIPS, *block), BF16)] * 3 + [pltpu.VMEM((2, *block), BF16)]
        dma = pltpu.SemaphoreType.DMA
        self.scratch += [dma((n, N_CHIPS + 1)), dma((n, 2)), dma((n, 2)), dma((n,)), dma((n,)), dma((n,))]

    def _plan(self, outs, scr):
        n = self.n
        first_send, first_recv, second_send, second_recv, keep_sems = scr[self.N_BUFS * n + 1:]
        x, y, c = _my_place()
        via = (x ^ (1 - c), y ^ c)
        onto = (x ^ c, y ^ (1 - c))
        index = lambda chip: 2 * chip[0] + chip[1]
        first, second, keep = [], [], []
        for t in range(n):
            total, inbox = scr[self.N_BUFS * t + 2], scr[self.N_BUFS * t + 3]
            for k, chip in enumerate((via, (1 - x, 1 - y))):
                first.append(pltpu.make_async_remote_copy(
                    src_ref=total.at[index(chip)], dst_ref=inbox.at[k], send_sem=first_send.at[t, k],
                    recv_sem=first_recv.at[t, k], device_id=(*via, c), device_id_type=MESH))
            second.append(pltpu.make_async_remote_copy(
                src_ref=total.at[index(onto)], dst_ref=outs[t].at[1], send_sem=second_send.at[t],
                recv_sem=second_recv.at[t], device_id=(*onto, c), device_id_type=MESH))
            keep.append(pltpu.make_async_copy(total.at[index((x, y))], outs[t].at[0], keep_sems.at[t]))
        return first, second, keep, index((x, y)), index(onto)

    def start(self, ins, outs, scr):
        n = self.n
        load_sems = scr[self.N_BUFS * n]
        c = lax.axis_index("c")
        loads = []
        for t, (_, block) in enumerate(self.kinds):
            mine, theirs = scr[self.N_BUFS * t], scr[self.N_BUFS * t + 1]
            loads += [pltpu.make_async_copy(block(ins[t], 2 * ch + c), mine.at[ch], load_sems.at[t, ch])
                      for ch in range(N_CHIPS)]
            loads.append(pltpu.make_async_copy(ins[n + t], theirs, load_sems.at[t, N_CHIPS]))
        for cp in loads:
            cp.start()
        for cp in loads:
            cp.wait()
        for t in range(n):
            mine, theirs, total = scr[self.N_BUFS * t:self.N_BUFS * t + 3]
            for ch in range(N_CHIPS):
                total[ch] = (mine[ch].astype(F32) + theirs[ch].astype(F32)).astype(BF16)
        for cp in self._plan(outs, scr)[0]:
            cp.start()

    def mid(self, ins, outs, scr):
        first, second, keep, me, onto = self._plan(outs, scr)
        for cp in first:
            cp.wait_recv()
        for t in range(self.n):
            total, inbox = scr[self.N_BUFS * t + 2], scr[self.N_BUFS * t + 3]
            for k, slot in enumerate((me, onto)):
                total[slot] = (total[slot].astype(F32) + inbox[k].astype(F32)).astype(BF16)
        for cp in second + keep:
            cp.start()

    def finish(self, ins, outs, scr):
        first, second, keep, _, _ = self._plan(outs, scr)
        for cp in first:
            cp.wait_send()
        for cp in second + keep:
            cp.wait()


class _Jobs:
    def __init__(self, *jobs):
        self.jobs = jobs
        together = {p for j in jobs for p in j.peers}
        self.peers = tuple(p for p in _PEER_SETS[0] if p in together)
        self.args = [a for j in jobs for a in j.args]
        self.out_shape = [o for j in jobs for o in j.out_shape]
        self.scratch = [s for j in jobs for s in j.scratch]

    def _split(self, refs, attr):
        at = 0
        for j in self.jobs:
            n = len(getattr(j, attr))
            yield refs[at:at + n]
            at += n

    def _each(self, ins, outs, scr):
        return zip(self.jobs, self._split(ins, "args"), self._split(outs, "out_shape"), self._split(scr, "scratch"))

    def start(self, ins, outs, scr):
        for j, i, o, s in self._each(ins, outs, scr):
            j.start(i, o, s)

    def mid(self, ins, outs, scr):
        for j, i, o, s in self._each(ins, outs, scr):
            if hasattr(j, "mid"):
                j.mid(i, o, s)

    def finish(self, ins, outs, scr):
        for j, i, o, s in self._each(ins, outs, scr):
            j.finish(i, o, s)

    def split_outputs(self, outs):
        return list(self._split(outs, "out_shape"))


def _adamw_math(w, g, m, v):
    m = ADAM_B1 * m + (1.0 - ADAM_B1) * g
    v = ADAM_B2 * v + (1.0 - ADAM_B2) * (g * g)
    m_hat = m / (1.0 - ADAM_B1 ** ADAM_STEP)
    v_hat = v / (1.0 - ADAM_B2 ** ADAM_STEP)
    delta = -ADAM_LR * (m_hat / (jnp.sqrt(v_hat) + ADAM_EPS) + ADAM_WD * w)
    return delta, m, v


def _adamw(name, w, m, v, landings, n_col_blocks=1, job=None):
    n_slots, r, c = landings[0].shape
    grid = (w.shape[0] // r, n_col_blocks)

    def body(w_ref, m_ref, v_ref, *rest):
        l_refs, (g_ref, d_ref, nm_ref, nv_ref) = rest[:len(landings)], rest[len(landings):]
        step = pl.program_id(0) * n_col_blocks + pl.program_id(1)
        for idx, l_ref in enumerate(l_refs):
            @pl.when(step == idx)
            def _(l_ref=l_ref):
                g = l_ref[0].astype(F32)
                for s in range(1, n_slots):
                    g = g + l_ref[s].astype(F32)
                g_ref[...] = g
                d_ref[...], nm_ref[...], nv_ref[...] = _adamw_math(w_ref[...], g, m_ref[...], v_ref[...])

    spec = pl.BlockSpec((r, c), lambda a, b: (a, b))
    return _launch(
        body, name=f"adamw_{name}", grid=grid,
        in_specs=[spec, spec, spec] + [_full_spec((n_slots, r, c))] * len(landings),
        out_specs=[spec] * 4, out_shape=[jax.ShapeDtypeStruct(w.shape, F32)] * 4,
        args=(w, m, v, *landings), vmem=VMEM_BIG, job=job)


_SMALL = (("pre_mix_g", SV_PRE_MIX, 2), ("post_mix_g", SV_POST_MIX, 2), ("pre_ffn_g", SV_PRE_FFN, 2),
          ("post_ffn_g", SV_POST_FFN, 2), ("ple_g", SV_PLE, 2), ("ple_post_g", SV_PLE_POST, 2), ("kv_g", SV_KV, 1),
          ("pool_scale", SV_POOL_SCALE, 1), ("sinks", SV_SINKS, 1))


def _small_all_reduce(part):
    def body(part_ref, total_ref, buf, send_sems, recv_sems):
        x, y, c = _my_place()
        me = _dev_index(x, y, c)
        buf[me] = part_ref[...]
        copies = [pltpu.make_async_remote_copy(
            src_ref=buf.at[me], dst_ref=buf.at[me], send_sem=send_sems.at[r - 1], recv_sem=recv_sems.at[r - 1],
            device_id=_peer_by_relation(r), device_id_type=MESH) for r in range(1, N_DEV)]
        for cp in copies:
            cp.start()
        for cp in copies:
            cp.wait()
        g = buf[0]
        for s in range(1, N_DEV):
            g = g + buf[s]
        total_ref[...] = g

    slab = jax.ShapeDtypeStruct((SV_ROWS, D_MODEL), F32)
    (total,), _ = _launch(
        body, name="small_all_reduce", grid=(1,), in_specs=[_full_spec(slab.shape)], out_specs=[_full_spec(slab.shape)],
        out_shape=[slab],
        scratch_shapes=[pltpu.VMEM((N_DEV, SV_ROWS, D_MODEL), F32), pltpu.SemaphoreType.DMA((N_DEV - 1,)),
                        pltpu.SemaphoreType.DMA((N_DEV - 1,))],
        args=(part,))
    return total


def _small_adamw(total, params):
    flat = [a for name, _, _ in _SMALL for a in params[name]]
    n_in = 1 + len(flat)

    def body(*refs):
        total, wmv = refs[0], refs[1:n_in]
        loss_ref, outs = refs[n_in], refs[n_in + 1:]
        me = _dev_index(*_my_place())
        loss_ref[...] = total[SV_LOSS:SV_LOSS + 1, 0:1]
        for idx, (name, row, n_rows) in enumerate(_SMALL):
            w_ref, m_ref, v_ref = wmv[3 * idx:3 * idx + 3]
            g_ref, d_ref, nm_ref, nv_ref = outs[4 * idx:4 * idx + 4]
            if name == "pool_scale":
                g = total[row:row + 1, pl.ds(pl.multiple_of(me * 128, 128), 128)]
            else:
                g = total[row:row + n_rows, 0:w_ref.shape[1]]
            g_ref[...] = g
            d_ref[...], nm_ref[...], nv_ref[...] = _adamw_math(w_ref[...], g, m_ref[...], v_ref[...])

    out_shape = [jax.ShapeDtypeStruct((1, 1), F32)]
    for name, _, _ in _SMALL:
        out_shape += [jax.ShapeDtypeStruct(params[name][0].shape, F32)] * 4
    res, _ = _launch(
        body, name="small_adamw", grid=(1,),
        in_specs=[_full_spec(a.shape) for a in (total, *flat)], out_specs=[_full_spec(s.shape) for s in out_shape],
        out_shape=out_shape, args=(total, *flat))
    return res[0], {name: res[1 + 4 * idx:5 + 4 * idx] for idx, (name, _, _) in enumerate(_SMALL)}


def _local_step(x, p, tgt, gains, sinks, shards, weights):
    row = lambda first_row, layer: _Gain(gains, first_row + layer)
    gather = lambda *names: _AllGather(names, shards)
    g_pre_mix, g_post_mix, g_pre_ffn, g_post_ffn = SV_PRE_MIX, SV_POST_MIX, SV_PRE_FFN, SV_POST_FFN
    g_ple, g_ple_post, g_kv = SV_PLE, SV_PLE_POST, _Gain(gains, SV_KV)

    wp, scale, wgu0 = _all_gather_only("gather_first", ("pool", "scale", "gu0"), shards)
    wgu0 = [wgu0]
    (x1_0, h2_0, yraw, dpool), wd0 = _fwd_pool_mixer(
        x, row(g_pre_mix, 0), wp, scale, row(g_post_mix, 0), row(g_pre_ffn, 0), job=gather("wd0"))
    (gs0, us0, f0, x2_0, h3_0), (wgate0, wproj0, wkv, wq, wgu1_a) = _fwd_ffn(
        0, h2_0, x1_0, wgu0, wd0, row(g_post_ffn, 0), row(g_ple, 0),
        job=gather("gate0", "proj0", "kv", "q", "guh1_0"))
    (x3_0, z0, pe0), (wo,) = _fwd_ple(0, x2_0, h3_0, p[0], wgate0, wproj0, row(g_ple_post, 0), job=gather("o"))
    (hk, h1, q, kv), (wd1_a,) = _fwd_qkv(x3_0, g_kv, row(g_pre_mix, 1), wkv, wq, job=gather("wdh1_0"))
    front = ((ATT_BLOCK, 0), (0, 0))
    kpad = jnp.pad(kv[:, :KV_DIM], front)
    vpad = jnp.pad(kv[:, KV_DIM:], front)
    (attn,), (wgu1_b,) = _fwd_attention(q, kpad, vpad, sinks, job=gather("guh1_1"))
    (y1, x1_1, h2_1), (wd1_b,) = _fwd_attn_out(attn, x3_0, wo, row(g_post_mix, 1), row(g_pre_ffn, 1),
                                               job=gather("wdh1_1"))
    wgu1, wd1 = [wgu1_a, wgu1_b], [wd1_a, wd1_b]
    (gs1, us1, f1, x2_1, h3_1), (wgate1, wproj1) = _fwd_ffn(
        1, h2_1, x1_1, wgu1, wd1, row(g_post_ffn, 1), row(g_ple, 1), job=gather("gate1", "proj1"))

    produced, swapped, landed = {}, {}, {}

    def kind_of(name):
        return name.rstrip("0123_")

    def carry(swap=(), spread=()):
        jobs = []
        if swap:
            jobs.append(_SiblingSwap([(kind_of(n), produced[n]) for n in swap]))
        if spread:
            jobs.append(_ChipScatter([(kind_of(n), produced[n], swapped[n]) for n in spread]))
        return _Jobs(*jobs)

    def carried(jobs, outs, swap=(), spread=()):
        parts = jobs.split_outputs(outs)
        if swap:
            swapped.update(zip(swap, parts[0]))
        if spread:
            landed.update(zip(spread, parts[-1]))

    def hosted(call, *args, swap=(), spread=()):
        jobs = carry(swap, spread)
        outs, job_outs = call(*args, job=jobs)
        carried(jobs, job_outs, swap, spread)
        return outs

    ffn_q = lambda layer, qtr: (f"gu{layer}_{qtr}", f"wd{layer}_{qtr}")

    dx2_1, df1, produced["gate1"], produced["proj1"], dg_ple_post1, dg_ple1, dg_post_ffn1, loss = hosted(
        _ple_loss_bwd, 1, x2_1, h3_1, p[1], f1, tgt, wgate1, wproj1, row(g_ple_post, 1), row(g_ple, 1),
        row(g_post_ffn, 1))
    dh2_1, dg1, du1, a1 = hosted(_bwd_ffn_act, 1, df1, gs1, us1, wgu1, wd1, swap=("gate1", "proj1"))
    produced["guF1"], produced["wdF1"] = hosted(_bwd_ffn_dw, 1, 0, 1, h2_1, df1, dg1, du1, a1,
                                                spread=("gate1", "proj1"))
    dx1_1, dattn, produced["o"], dg_pre_ffn1, dg_post_mix1 = hosted(
        _bwd_attn_out, dx2_1, dh2_1, x1_1, y1, attn, wo, row(g_pre_ffn, 1), row(g_post_mix, 1),
        swap=("guF1", "wdF1"))
    dq, dkpad, dvpad, dsinks = hosted(_bwd_attention, q, dattn, kpad, vpad, sinks, spread=("guF1",))
    dkv = jnp.concatenate([dkpad[ATT_BLOCK:], dvpad[ATT_BLOCK:]], axis=1).astype(BF16)
    dx3_0, produced["q"], produced["kv"], dg_pre_mix1, dg_kv = hosted(
        _bwd_qkv, dx1_1, dq, dkv, x3_0, h1, hk, wq, wkv, row(g_pre_mix, 1), g_kv, swap=("o",), spread=("wdF1",))
    for name in ("gu", "wd"):
        whole = landed.pop(f"{name}F1")
        for half in range(FF_PARTS):
            landed[f"{name}1_{half}"] = whole[..., half * FF_PART:(half + 1) * FF_PART]
    dx2_0, df0, produced["gate0"], produced["proj0"], dg_ple_post0, dg_ple0, dg_post_ffn0 = hosted(
        _bwd_ple, 0, dx3_0, x2_0, z0, pe0, h3_0, p[0], f0, wgate0, row(g_ple_post, 0), row(g_ple, 0),
        row(g_post_ffn, 0), swap=("q", "kv"), spread=("o",))
    dh2_0, dg0, du0, a0 = hosted(_bwd_ffn_act, 0, df0, gs0, us0, wgu0, wd0,
                                 swap=("gate0", "proj0"), spread=("q", "kv"))
    part_hosts = [dict(spread=("gate0", "proj0")), dict(swap=ffn_q(0, 0))]
    for part in range(FF_PARTS):
        produced[f"gu0_{part}"], produced[f"wd0_{part}"] = hosted(
            _bwd_ffn_dw, 0, part, FF_PARTS, h2_0, df0, dg0, du0, a0, **part_hosts[part])
    grad_x, produced["pool"], dscale, dg_pre_ffn0, dg_post_mix0, dg_pre_mix0 = hosted(
        _bwd_pool_mixer, dx2_0, dh2_0, x1_0, x, yraw, dpool, wp, scale, row(g_pre_ffn, 0), row(g_post_mix, 0),
        row(g_pre_mix, 0), swap=ffn_q(0, 1), spread=ffn_q(0, 0))

    def update(name, n_col_blocks=1, pieces=None, swap=(), spread=()):
        w, m, v = weights[name]
        rows = w.size // w.shape[-1]
        flat = [landed[n].reshape(landed[n].shape[0], -1, landed[n].shape[-1])
                for n in (pieces or [kind_short[name]])]
        outs = hosted(_adamw, name, w.reshape(rows, -1), m.reshape(rows, -1), v.reshape(rows, -1), flat,
                      n_col_blocks, swap=swap, spread=spread)
        return [o.reshape(w.shape) for o in outs]

    kind_short = {"w_q": "q", "w_kv": "kv", "w_o": "o", "pool_w": "pool"}
    upd = {}
    hosted(_jobs_only, "scatter_tail0", swap=("pool",), spread=ffn_q(0, 1))
    hosted(_jobs_only, "scatter_tail1", spread=("pool",))
    upd["w_ple_gate"] = update("w_ple_gate", pieces=("gate0", "gate1"))
    upd["w_ple_proj"] = update("w_ple_proj", pieces=("proj0", "proj1"))
    for name in ("w_q", "w_kv", "w_o", "pool_w"):
        upd[name] = update(name)
    upd["w_gu"] = update("w_gu", FF_PARTS,
                         pieces=[f"gu{layer}_{qtr}" for layer in range(2) for qtr in range(FF_PARTS)])
    upd["w_gu"] = [jnp.swapaxes(a, 1, 2) for a in upd["w_gu"]]
    upd["w_down"] = update("w_down", FF_PARTS,
                           pieces=[f"wd{layer}_{qtr}" for layer in range(2) for qtr in range(FF_PARTS)])

    lanes = lambda a: jnp.pad(a, ((0, 0), (0, D_MODEL - a.shape[1])))
    small = jnp.concatenate([
        dg_pre_mix0, dg_pre_mix1, dg_post_mix0, dg_post_mix1, dg_pre_ffn0, dg_pre_ffn1, dg_post_ffn0, dg_post_ffn1,
        dg_ple0, dg_ple1, dg_ple_post0, dg_ple_post1, dg_kv, dscale, lanes(dsinks[:, :N_HEADS]), lanes(loss)], axis=0)
    return grad_x, upd, small


def kernel(x, p, pre_mix_g, post_mix_g, pre_ffn_g, post_ffn_g, pool_w, pool_scale, kv_g, w_kv, w_q, sinks, w_o, w_gu, w_down, ple_g, w_ple_gate, w_ple_proj, ple_post_g, loss_target, m_pre_mix_g, m_post_mix_g, m_pre_ffn_g, m_post_ffn_g, m_pool_w, m_pool_scale, m_kv_g, m_w_kv, m_w_q, m_sinks, m_w_o, m_w_gu, m_w_down, m_ple_g, m_w_ple_gate, m_w_ple_proj, m_ple_post_g, v_pre_mix_g, v_post_mix_g, v_pre_ffn_g, v_post_ffn_g, v_pool_w, v_pool_scale, v_kv_g, v_w_kv, v_w_q, v_sinks, v_w_o, v_w_gu, v_w_down, v_ple_g, v_w_ple_gate, v_w_ple_proj, v_ple_post_g):
    shards = {"pool": pool_w[0].astype(BF16), "scale": pool_scale, "kv": w_kv.astype(BF16),
              "q": w_q[0].astype(BF16), "o": w_o[0].astype(BF16)}
    for layer in range(2):
        shards[f"gu{layer}"] = w_gu[layer].T.astype(BF16)
        shards[f"wd{layer}"] = w_down[layer].astype(BF16)
        for half in range(2):
            cols = (half * D_MODEL // 2, (half + 1) * D_MODEL // 2)
            shards[f"guh{layer}_{half}"] = (shards[f"gu{layer}"], cols)
            shards[f"wdh{layer}_{half}"] = (shards[f"wd{layer}"], cols)
        shards[f"gate{layer}"] = w_ple_gate[layer].astype(BF16)
        shards[f"proj{layer}"] = w_ple_proj[layer].astype(BF16)
    gains = jnp.concatenate([pre_mix_g, post_mix_g, pre_ffn_g, post_ffn_g, ple_g, ple_post_g, kv_g[None, :]],
                            axis=0).reshape(-1, 1, D_MODEL)
    weights = {"pool_w": (pool_w, m_pool_w, v_pool_w), "w_kv": (w_kv, m_w_kv, v_w_kv), "w_q": (w_q, m_w_q, v_w_q),
               "w_o": (w_o, m_w_o, v_w_o), "w_down": (w_down, m_w_down, v_w_down),
               "w_gu": tuple(jnp.swapaxes(a, 1, 2) for a in (w_gu, m_w_gu, v_w_gu)),
               "w_ple_gate": (w_ple_gate, m_w_ple_gate, v_w_ple_gate),
               "w_ple_proj": (w_ple_proj, m_w_ple_proj, v_w_ple_proj)}
    grad_x, upd, small = _local_step(x[0], p[:, 0], loss_target[0], gains, sinks, shards, weights)

    small_params = {
        "pre_mix_g": (pre_mix_g, m_pre_mix_g, v_pre_mix_g), "post_mix_g": (post_mix_g, m_post_mix_g, v_post_mix_g),
        "pre_ffn_g": (pre_ffn_g, m_pre_ffn_g, v_pre_ffn_g), "post_ffn_g": (post_ffn_g, m_post_ffn_g, v_post_ffn_g),
        "ple_g": (ple_g, m_ple_g, v_ple_g), "ple_post_g": (ple_post_g, m_ple_post_g, v_ple_post_g),
        "kv_g": (kv_g[None, :], m_kv_g[None, :], v_kv_g[None, :]),
        "pool_scale": (pool_scale, m_pool_scale, v_pool_scale), "sinks": (sinks, m_sinks, v_sinks)}
    loss, small_upd = _small_adamw(_small_all_reduce(small), small_params)
    small_upd["kv_g"] = [a[0] for a in small_upd["kv_g"]]
    upd.update(small_upd)

    names = ["pre_mix_g", "post_mix_g", "pre_ffn_g", "post_ffn_g", "pool_w", "pool_scale", "kv_g", "w_kv", "w_q",
             "sinks", "w_o", "w_gu", "w_down", "ple_g", "w_ple_gate", "w_ple_proj", "ple_post_g"]
    outs = [loss[0, 0], grad_x[None]]
    for kind in range(4):
        outs += [upd[n][kind] for n in names]
    return tuple(outs)
```

```python
import functools
import types

import jax
import jax.numpy as jnp
from jax import lax
from jax.experimental import pallas as pl
from jax.experimental.pallas import tpu as pltpu

F32 = jnp.float32
BF16 = jnp.bfloat16

N_DEV = 8
D_MODEL = 1024
N_POOL_GROUPS = 4
POOL_GROUP = 256
POOL_HALO = 16
HEAD_DIM = 64
N_HEADS = 16
N_KV_HEADS = 4
GQA_GROUP = 4
KV_DIM = N_KV_HEADS * HEAD_DIM
ATT_BLOCK = 128
D_FF = 2816
FF_CHUNKS = 4
FF_BLOCK = D_FF // FF_CHUNKS
WD_ROWS = D_FF // N_DEV
FF_PARTS = 2
FF_PART = D_MODEL // FF_PARTS
N_CHIPS = 4
PLE_DIM = 256
EPS = 1e-6
NEG_INF = -1e30
ATT_SCALE = HEAD_DIM ** -0.5

ADAM_LR = 0.001
ADAM_B1 = 0.9
ADAM_B2 = 0.999
ADAM_EPS = 1e-08
ADAM_WD = 0.01
ADAM_STEP = 10

ROW_TILE = 512
FFN_ROW_TILE = 512
FFN_WEIGHT_COLS = 512
FFN_SUB_TILES = 1
VMEM_BIG = 60 * 1024 * 1024
VMEM_MID = 56 * 1024 * 1024
HBM_PIN_ELEMS = 1024

SV_ROWS = 16
SV_PRE_MIX, SV_POST_MIX, SV_PRE_FFN, SV_POST_FFN, SV_PLE, SV_PLE_POST = 0, 2, 4, 6, 8, 10
SV_KV, SV_POOL_SCALE, SV_SINKS, SV_LOSS = 12, 13, 14, 15

MESH = pl.DeviceIdType.MESH
ANY = pl.BlockSpec(memory_space=pl.ANY)


def _dot(a, b):
    return jnp.dot(a, b, preferred_element_type=F32)


def _dot_nt(a, b):
    return lax.dot_general(a, b, (((1,), (1,)), ((), ())), preferred_element_type=F32)


def _dot_tn(a, b):
    return lax.dot_general(a, b, (((0,), (0,)), ((), ())), preferred_element_type=F32)


def _rstd(x):
    return lax.rsqrt(jnp.mean(x * x, axis=-1, keepdims=True) + EPS)


def _rms(x, g):
    return x * _rstd(x) * g


def _rms_bwd(x, g, dy):
    r = _rstd(x)
    n = x * r
    dn = dy * g
    dx = r * (dn - n * jnp.mean(dn * n, axis=-1, keepdims=True))
    dg = jnp.sum(dy * n, axis=0, keepdims=True)
    return dx, dg


def _add_all(terms):
    return functools.reduce(jnp.add, terms)


def _sigmoid(x):
    return 1.0 / (1.0 + jnp.exp(-x))


def _acc(ref, val, first):
    @pl.when(first)
    def _():
        ref[...] = val

    @pl.when(jnp.logical_not(first))
    def _():
        ref[...] += val


def _pool_counts(row0, rows):
    t = row0 + lax.broadcasted_iota(jnp.int32, (rows, D_MODEL), 0) + 1
    grp = lax.broadcasted_iota(jnp.int32, (rows, D_MODEL), 1) // POOL_GROUP
    win = jnp.left_shift(2, grp)
    return jnp.minimum(t, win).astype(F32)


def _window_sums(ext, shift_of):
    outs = []
    s = ext
    for gi in range(N_POOL_GROUPS):
        s = s + pltpu.roll(s, shift_of(1 << gi), axis=0)
        outs.append(s[:, :POOL_GROUP])
        s = s[:, POOL_GROUP:]
    return jnp.concatenate(outs, axis=1)


def _cparams(n_axes, vmem, collective_id=None):
    return pltpu.CompilerParams(dimension_semantics=("arbitrary",) * n_axes, vmem_limit_bytes=vmem,
                                collective_id=collective_id)


_PEER_SETS = (("sibling", "x", "y"), ("sibling",), ("x", "y"))


def _meet(peers):
    x, y, c = lax.axis_index("x"), lax.axis_index("y"), lax.axis_index("c")
    device = {"sibling": (x, y, 1 - c), "x": (1 - x, y, c), "y": (x, 1 - y, c)}
    barrier = pltpu.get_barrier_semaphore()
    for peer in peers:
        pl.semaphore_signal(barrier, inc=1, device_id=device[peer], device_id_type=pl.DeviceIdType.MESH)
    pl.semaphore_wait(barrier, len(peers))


def _row_spec(cols, tm=ROW_TILE):
    return pl.BlockSpec((tm, cols), lambda i: (i, 0))


def _full_spec(shape):
    zeros = (0,) * len(shape)
    return pl.BlockSpec(shape, lambda *_: zeros)


def _vec_spec():
    return _full_spec((1, D_MODEL))


def _column_views(parts):
    return [(a, b) for a in parts for b in range(a.shape[-1] // FFN_WEIGHT_COLS)]


def _column_ranges(views):
    return [(n * FFN_WEIGHT_COLS, (n + 1) * FFN_WEIGHT_COLS) for n in range(len(views))]


class _Gain:
    def __init__(self, stacked, layer):
        self.stacked, self.layer = stacked, layer

    def spec(self):
        layer = self.layer
        return pl.BlockSpec((None, 1, D_MODEL), lambda *_: (layer, 0, 0))


def _in_hbm(a):
    return pltpu.with_memory_space_constraint(a, pltpu.HBM) if a.size >= HBM_PIN_ELEMS else a


def _out_in_hbm(s):
    return pltpu.HBM(s.shape, s.dtype) if s.size >= HBM_PIN_ELEMS else s


def _launch(body, *, name, grid, in_specs, out_specs, out_shape, args, scratch_shapes=(), vmem=VMEM_MID, job=None):
    in_specs = [a.spec() if isinstance(a, _Gain) else s for s, a in zip(in_specs, args)]
    args = [_in_hbm(a.stacked if isinstance(a, _Gain) else a) for a in args]
    n_in, n_out, n_scr = len(args), len(out_shape), len(scratch_shapes)
    if job is not None and not job.args:
        job = None
    j_args, j_out, j_scr = ([], [], []) if job is None else ([_in_hbm(a) for a in job.args], job.out_shape, job.scratch)

    def run(*refs):
        groups, at = [], 0
        for n in (n_in, len(j_args), n_out, len(j_out), n_scr, len(j_scr)):
            groups.append(refs[at:at + n])
            at += n
        ins, j_ins, outs, j_outs, scr, j_sems = groups

        def begin():
            _meet(job.peers)
            job.start(j_ins, j_outs, j_sems)

        if job is None:
            body(*ins, *outs, *scr)
        elif not grid:
            begin()
            job.mid(j_ins, j_outs, j_sems)
            body(*ins, *outs, *scr)
            job.finish(j_ins, j_outs, j_sems)
        else:
            ids = [pl.program_id(a) for a in range(len(grid))]
            first = functools.reduce(jnp.logical_and, [i == 0 for i in ids])
            half = functools.reduce(jnp.logical_and, [ids[0] == grid[0] // 2] + [i == 0 for i in ids[1:]])
            last = functools.reduce(jnp.logical_and, [i == g - 1 for i, g in zip(ids, grid)])
            pl.when(first)(begin)
            pl.when(half)(lambda: job.mid(j_ins, j_outs, j_sems))
            body(*ins, *outs, *scr)
            pl.when(last)(lambda: job.finish(j_ins, j_outs, j_sems))

    res = pl.pallas_call(
        run, name=name, grid=grid,
        in_specs=list(in_specs) + [ANY] * len(j_args), out_specs=list(out_specs) + [ANY] * len(j_out),
        out_shape=[_out_in_hbm(s) for s in list(out_shape) + list(j_out)],
        scratch_shapes=list(scratch_shapes) + list(j_scr),
        compiler_params=_cparams(len(grid), vmem, None if job is None else _PEER_SETS.index(job.peers)),
    )(*args, *j_args)
    return res[:n_out], res[n_out:]


def _fwd_pool_mixer(x, g_pre, wp, scale, g_post, g_ffn, job=None):
    T = x.shape[0]
    tm = ROW_TILE
    nt = T // tm

    def body(x_ref, gpre_ref, wp_ref, sc_ref, gpost_ref, gffn_ref, x1_ref, h2_ref, yraw_ref, d_ref, carry):
        i = pl.program_id(0)

        @pl.when(i == 0)
        def _():
            carry[...] = jnp.zeros_like(carry)

        xv = x_ref[...]
        h = _rms(xv, gpre_ref[...])
        ext = jnp.concatenate([carry[...], h], axis=0)
        carry[...] = h[tm - POOL_HALO:, :]
        sums = _window_sums(ext, lambda k: k)[POOL_HALO:, :]
        d = sums / _pool_counts(i * tm, tm) - h
        db = d.astype(BF16)
        d_ref[...] = db
        yraw = jnp.concatenate(
            [_dot(db[:, g * POOL_GROUP:(g + 1) * POOL_GROUP], wp_ref[g]) for g in range(N_POOL_GROUPS)], axis=1)
        yraw_ref[...] = yraw.astype(BF16)
        x1 = xv + _rms(yraw * sc_ref[...], gpost_ref[...])
        x1_ref[...] = x1
        h2_ref[...] = _rms(x1, gffn_ref[...]).astype(BF16)

    return _launch(
        body, name="fwd_pool_mixer", grid=(nt,),
        in_specs=[_row_spec(D_MODEL), _vec_spec(), _full_spec((N_POOL_GROUPS, POOL_GROUP, POOL_GROUP)), _vec_spec(),
                  _vec_spec(), _vec_spec()],
        out_specs=[_row_spec(D_MODEL)] * 4,
        out_shape=[jax.ShapeDtypeStruct((T, D_MODEL), F32)] + [jax.ShapeDtypeStruct((T, D_MODEL), BF16)] * 3,
        scratch_shapes=[pltpu.VMEM((POOL_HALO, D_MODEL), F32)],
        args=(x, g_pre, wp, scale, g_post, g_ffn), job=job)


def _fwd_ffn(layer, h2, x1, wgu, wd, g_post, g_ple, job=None):
    T = h2.shape[0]
    tm = min(FFN_ROW_TILE, T)
    nt = T // tm
    sub = tm // FFN_SUB_TILES
    last = FF_CHUNKS - 1
    wgu, wd = _column_views(wgu), _column_views(wd)
    n_gu, n_wd = len(wgu), len(wd)
    gu_cols = _column_ranges(wgu)

    def body(h2_ref, x1_ref, *refs):
        wgu_refs, wd_refs = refs[:n_gu], refs[n_gu:n_gu + n_wd]
        gpost_ref, gple_ref, gs_ref, us_ref, f_ref, x2_ref, h3_ref, acc = refs[n_gu + n_wd:]
        k = pl.program_id(0)
        i = pl.program_id(1)
        rows = pl.ds(pl.multiple_of(i * tm, tm), tm)
        parts = []
        for s in range(FFN_SUB_TILES):
            r = pl.ds(s * sub, sub)
            g = _add_all([_dot_nt(h2_ref[r, c0:c1], w[0]) for (c0, c1), w in zip(gu_cols, wgu_refs)])
            u = _add_all([_dot_nt(h2_ref[r, c0:c1], w[1]) for (c0, c1), w in zip(gu_cols, wgu_refs)])
            gs_ref[r, :] = g.astype(BF16)
            us_ref[r, :] = u.astype(BF16)
            a = (g * _sigmoid(g) * u).astype(BF16)
            parts.append(jnp.concatenate([_dot(a, w[...]) for w in wd_refs], axis=1))
        part = jnp.concatenate(parts, axis=0)

        @pl.when(k == 0)
        def _():
            acc[rows, :] = part

        @pl.when(jnp.logical_and(k > 0, k < last))
        def _():
            acc[rows, :] += part

        @pl.when(k == last)
        def _():
            f = acc[rows, :] + part
            f_ref[...] = f.astype(BF16)
            x2 = x1_ref[...] + _rms(f, gpost_ref[...])
            x2_ref[...] = x2
            h3_ref[...] = _rms(x2, gple_ref[...]).astype(BF16)

    def late(k, i):
        return (jnp.where(k == last, i, 0), 0)

    return _launch(
        body, name=f"fwd_ffn{layer}", grid=(FF_CHUNKS, nt),
        in_specs=[pl.BlockSpec((tm, D_MODEL), lambda k, i: (i, 0)), pl.BlockSpec((tm, D_MODEL), late)]
                 + [pl.BlockSpec((None, 2, FF_BLOCK, FFN_WEIGHT_COLS), lambda k, i, b=b: (k, 0, 0, b)) for _, b in wgu]
                 + [pl.BlockSpec((FF_BLOCK, FFN_WEIGHT_COLS), lambda k, i, b=b: (k, b)) for _, b in wd]
                 + [pl.BlockSpec((1, D_MODEL), lambda k, i: (0, 0))] * 2,
        out_specs=[pl.BlockSpec((None, tm, FF_BLOCK), lambda k, i: (k, i, 0)),
                   pl.BlockSpec((None, tm, FF_BLOCK), lambda k, i: (k, i, 0)),
                   pl.BlockSpec((tm, D_MODEL), late),
                   pl.BlockSpec((tm, D_MODEL), late),
                   pl.BlockSpec((tm, D_MODEL), late)],
        out_shape=[jax.ShapeDtypeStruct((FF_CHUNKS, T, FF_BLOCK), BF16),
                   jax.ShapeDtypeStruct((FF_CHUNKS, T, FF_BLOCK), BF16),
                   jax.ShapeDtypeStruct((T, D_MODEL), BF16),
                   jax.ShapeDtypeStruct((T, D_MODEL), F32),
                   jax.ShapeDtypeStruct((T, D_MODEL), BF16)],
        scratch_shapes=[pltpu.VMEM((T, D_MODEL), F32)],
        args=(h2, x1, *[w for w, _ in wgu], *[w for w, _ in wd], g_post, g_ple), vmem=VMEM_BIG, job=job)


def _fwd_ple(layer, x2, h3, p, wgate, wproj, g_post, job=None):
    T = x2.shape[0]
    nt = T // ROW_TILE

    def body(x2_ref, h3_ref, p_ref, wg_ref, wp_ref, gpost_ref, x3_ref, z_ref, pe_ref):
        z = _dot(h3_ref[...], wg_ref[...])
        pe = _dot(p_ref[...].astype(BF16), wp_ref[...])
        z_ref[...] = z.astype(BF16)
        pe_ref[...] = pe.astype(BF16)
        x3_ref[...] = x2_ref[...] + _rms(pe * _sigmoid(z), gpost_ref[...])

    return _launch(
        body, name=f"fwd_ple{layer}", grid=(nt,),
        in_specs=[_row_spec(D_MODEL), _row_spec(D_MODEL), _row_spec(PLE_DIM), _full_spec((D_MODEL, D_MODEL)),
                  _full_spec((PLE_DIM, D_MODEL)), _vec_spec()],
        out_specs=[_row_spec(D_MODEL)] * 3,
        out_shape=[jax.ShapeDtypeStruct((T, D_MODEL), F32)] + [jax.ShapeDtypeStruct((T, D_MODEL), BF16)] * 2,
        args=(x2, h3, p, wgate, wproj, g_post), job=job)


def _fwd_qkv(x3, g_kv, g_mix, wkv, wq, job=None):
    T = x3.shape[0]
    nt = T // ROW_TILE

    def body(x_ref, gkv_ref, gmix_ref, wkv_ref, wq_ref, hk_ref, h1_ref, q_ref, kv_ref):
        xv = x_ref[...]
        r = _rstd(xv)
        hk = (xv * r * gkv_ref[...]).astype(BF16)
        h1 = (xv * r * gmix_ref[...]).astype(BF16)
        hk_ref[...] = hk
        h1_ref[...] = h1
        kv_ref[...] = _dot(hk, wkv_ref[...]).astype(BF16)
        q_ref[...] = _dot(h1, wq_ref[...]).astype(BF16)

    return _launch(
        body, name="fwd_qkv", grid=(nt,),
        in_specs=[_row_spec(D_MODEL), _vec_spec(), _vec_spec(), _full_spec((D_MODEL, 2 * KV_DIM)),
                  _full_spec((D_MODEL, D_MODEL))],
        out_specs=[_row_spec(D_MODEL), _row_spec(D_MODEL), _row_spec(D_MODEL), _row_spec(2 * KV_DIM)],
        out_shape=[jax.ShapeDtypeStruct((T, D_MODEL), BF16)] * 3 + [jax.ShapeDtypeStruct((T, 2 * KV_DIM), BF16)],
        args=(x3, g_kv, g_mix, wkv, wq), job=job)


def _alibi_slope(h):
    return 2.0 ** (-8.0 * (h + 1) / N_HEADS)


ATT_SUB = 32
ATT_GROUP_ROWS = GQA_GROUP * ATT_BLOCK


def _att_mask(n, rel_ref, off_ref):
    qi = lax.broadcasted_iota(jnp.int32, (ATT_BLOCK, 2 * ATT_BLOCK), 0)
    si = lax.broadcasted_iota(jnp.int32, (ATT_BLOCK, 2 * ATT_BLOCK), 1)
    rel = ATT_BLOCK + qi - si
    valid = (rel >= 0) & (rel < ATT_BLOCK) & ((si >= ATT_BLOCK) | (n > 0))
    rel_ref[...] = rel.astype(F32)
    off_ref[...] = jnp.where(valid, 0.0, NEG_INF)


def _att_probs(raw, relf, off, slope, sink):
    s = raw * ATT_SCALE - slope * relf + off
    m = jnp.maximum(jnp.max(s, axis=-1, keepdims=True), sink)
    e = jnp.exp(s - m)
    es = jnp.exp(sink - m)
    inv = 1.0 / (jnp.sum(e, axis=-1, keepdims=True) + es)
    return e * inv, es * inv


def _stack_heads(ref, kh):
    first = kh * GQA_GROUP
    return jnp.concatenate([ref[:, (first + g) * HEAD_DIM:(first + g + 1) * HEAD_DIM] for g in range(GQA_GROUP)], axis=0)


def _unstack_heads(stacked):
    return [stacked[g * ATT_BLOCK:(g + 1) * ATT_BLOCK, :] for g in range(GQA_GROUP)]


def _fwd_attention(q, kpad, vpad, sinks, job=None):
    T = q.shape[0]
    nb = T // ATT_BLOCK

    def body(q_ref, k_ref, v_ref, sink_ref, o_ref, s_scr, p_scr, rel_scr, off_scr):
        n = pl.program_id(0)
        start = pl.multiple_of(n * ATT_BLOCK, ATT_BLOCK)
        kw = k_ref[pl.ds(start, 2 * ATT_BLOCK), :]
        vw = v_ref[pl.ds(start, 2 * ATT_BLOCK), :]
        _att_mask(n, rel_scr, off_scr)
        outs = []
        for kh in range(N_KV_HEADS):
            kk = kw[:, kh * HEAD_DIM:(kh + 1) * HEAD_DIM]
            vv = vw[:, kh * HEAD_DIM:(kh + 1) * HEAD_DIM]
            s_scr[...] = _dot_nt(_stack_heads(q_ref, kh), kk)
            for g in range(GQA_GROUP):
                h = kh * GQA_GROUP + g
                for row0 in range(0, ATT_BLOCK, ATT_SUB):
                    rows, sub = pl.ds(g * ATT_BLOCK + row0, ATT_SUB), pl.ds(row0, ATT_SUB)
                    pr, _ = _att_probs(s_scr[rows, :], rel_scr[sub, :], off_scr[sub, :], _alibi_slope(h),
                                       sink_ref[0, h])
                    p_scr[rows, :] = pr.astype(BF16)
            outs += _unstack_heads(_dot(p_scr[...], vv))
        o_ref[...] = jnp.concatenate(outs, axis=1).astype(BF16)

    return _launch(
        body, name="fwd_attention", grid=(nb,),
        in_specs=[_row_spec(D_MODEL, ATT_BLOCK), _full_spec((T + ATT_BLOCK, KV_DIM)), _full_spec((T + ATT_BLOCK, KV_DIM)),
                  pl.BlockSpec(memory_space=pltpu.SMEM)],
        out_specs=[_row_spec(D_MODEL, ATT_BLOCK)],
        out_shape=[jax.ShapeDtypeStruct((T, D_MODEL), BF16)],
        scratch_shapes=[pltpu.VMEM((ATT_GROUP_ROWS, 2 * ATT_BLOCK), F32), pltpu.VMEM((ATT_GROUP_ROWS, 2 * ATT_BLOCK), BF16)]
                       + [pltpu.VMEM((ATT_BLOCK, 2 * ATT_BLOCK), F32)] * 2,
        args=(q, kpad, vpad, sinks), job=job)


def _fwd_attn_out(attn, x, wo, g_post, g_ffn, job=None):
    T = x.shape[0]
    nt = T // ROW_TILE

    def body(a_ref, x_ref, wo_ref, gpost_ref, gffn_ref, y_ref, x1_ref, h2_ref):
        y = _dot(a_ref[...], wo_ref[...])
        y_ref[...] = y.astype(BF16)
        x1 = x_ref[...] + _rms(y, gpost_ref[...])
        x1_ref[...] = x1
        h2_ref[...] = _rms(x1, gffn_ref[...]).astype(BF16)

    return _launch(
        body, name="fwd_attn_out", grid=(nt,),
        in_specs=[_row_spec(D_MODEL), _row_spec(D_MODEL), _full_spec((D_MODEL, D_MODEL)), _vec_spec(), _vec_spec()],
        out_specs=[_row_spec(D_MODEL)] * 3,
        out_shape=[jax.ShapeDtypeStruct((T, D_MODEL), BF16), jax.ShapeDtypeStruct((T, D_MODEL), F32),
                   jax.ShapeDtypeStruct((T, D_MODEL), BF16)],
        args=(attn, x, wo, g_post, g_ffn), job=job)


def _bwd_ple(layer, dx3, x2, z, pe, h3, p, f, wgate, g_ple_post, g_ple, g_post_ffn, job=None):
    T = x2.shape[0]
    tm = ROW_TILE
    nt = T // tm

    def body(dx3_ref, x2_ref, z_ref, pe_ref, h3_ref, p_ref, f_ref, wg_ref, gpp_ref, gp_ref, gpf_ref,
             dx2_ref, df_ref, dwg_ref, dwp_ref, dgpp_ref, dgp_ref, dgpf_ref, acc_g, acc_p):
        i = pl.program_id(0)
        first = i == 0
        dx3v = dx3_ref[...]
        gate = _sigmoid(z_ref[...].astype(F32))
        pev = pe_ref[...].astype(F32)
        de, dgpp = _rms_bwd(pev * gate, gpp_ref[...], dx3v)
        dpe = (de * gate).astype(BF16)
        dz = (de * pev * gate * (1.0 - gate)).astype(BF16)
        _acc(acc_p, _dot_tn(p_ref[...].astype(BF16), dpe), first)
        _acc(acc_g, _dot_tn(h3_ref[...], dz), first)
        dh3 = _dot_nt(dz, wg_ref[...])
        dxn, dgp = _rms_bwd(x2_ref[...], gp_ref[...], dh3)
        dx2 = dx3v + dxn
        dx2_ref[...] = dx2
        df, dgpf = _rms_bwd(f_ref[...].astype(F32), gpf_ref[...], dx2)
        df_ref[...] = df.astype(BF16)
        _acc(dgpp_ref, dgpp, first)
        _acc(dgp_ref, dgp, first)
        _acc(dgpf_ref, dgpf, first)

        @pl.when(i == nt - 1)
        def _():
            dwg_ref[...] = acc_g[...].astype(BF16)
            dwp_ref[...] = acc_p[...].astype(BF16)

    return _launch(
        body, name=f"bwd_ple{layer}", grid=(nt,),
        in_specs=[_row_spec(D_MODEL)] * 5 + [_row_spec(PLE_DIM), _row_spec(D_MODEL), _full_spec((D_MODEL, D_MODEL)),
                  _vec_spec(), _vec_spec(), _vec_spec()],
        out_specs=[_row_spec(D_MODEL), _row_spec(D_MODEL), _full_spec((D_MODEL, D_MODEL)), _full_spec((PLE_DIM, D_MODEL)),
                   _vec_spec(), _vec_spec(), _vec_spec()],
        out_shape=[jax.ShapeDtypeStruct((T, D_MODEL), F32), jax.ShapeDtypeStruct((T, D_MODEL), BF16),
                   jax.ShapeDtypeStruct((D_MODEL, D_MODEL), BF16), jax.ShapeDtypeStruct((PLE_DIM, D_MODEL), BF16)]
                  + [jax.ShapeDtypeStruct((1, D_MODEL), F32)] * 3,
        scratch_shapes=[pltpu.VMEM((D_MODEL, D_MODEL), F32), pltpu.VMEM((PLE_DIM, D_MODEL), F32)],
        args=(dx3, x2, z, pe, h3, p, f, wgate, g_ple_post, g_ple, g_post_ffn), vmem=VMEM_BIG, job=job)


def _ple_loss_bwd(layer, x2, h3, p, f, target, wgate, wproj, g_ple_post, g_ple, g_post_ffn, job=None):
    T = x2.shape[0]
    tm = ROW_TILE
    nt = T // tm

    def body(x2_ref, h3_ref, p_ref, f_ref, tgt_ref, wg_ref, wp_ref, gpp_ref, gp_ref, gpf_ref,
             dx2_ref, df_ref, dwg_ref, dwp_ref, dgpp_ref, dgp_ref, dgpf_ref, loss_ref, acc_g, acc_p):
        i = pl.program_id(0)
        first = i == 0
        h3 = h3_ref[...]
        pb = p_ref[...].astype(BF16)
        x2v = x2_ref[...]
        gate = _sigmoid(_dot(h3, wg_ref[...]))
        pev = _dot(pb, wp_ref[...])
        e = pev * gate
        err = x2v + _rms(e, gpp_ref[...]) - tgt_ref[...]
        _acc(loss_ref, 0.5 * jnp.sum(jnp.mean(err * err, axis=-1, keepdims=True), axis=0, keepdims=True), first)
        dx3v = err * (1.0 / D_MODEL)
        de, dgpp = _rms_bwd(e, gpp_ref[...], dx3v)
        dpe = (de * gate).astype(BF16)
        dz = (de * pev * gate * (1.0 - gate)).astype(BF16)
        _acc(acc_p, _dot_tn(pb, dpe), first)
        _acc(acc_g, _dot_tn(h3, dz), first)
        dxn, dgp = _rms_bwd(x2v, gp_ref[...], _dot_nt(dz, wg_ref[...]))
        dx2 = dx3v + dxn
        dx2_ref[...] = dx2
        df, dgpf = _rms_bwd(f_ref[...].astype(F32), gpf_ref[...], dx2)
        df_ref[...] = df.astype(BF16)
        _acc(dgpp_ref, dgpp, first)
        _acc(dgp_ref, dgp, first)
        _acc(dgpf_ref, dgpf, first)

        @pl.when(i == nt - 1)
        def _():
            dwg_ref[...] = acc_g[...].astype(BF16)
            dwp_ref[...] = acc_p[...].astype(BF16)

    return _launch(
        body, name=f"ple_loss_bwd{layer}", grid=(nt,),
        in_specs=[_row_spec(D_MODEL), _row_spec(D_MODEL), _row_spec(PLE_DIM), _row_spec(D_MODEL), _row_spec(D_MODEL),
                  _full_spec((D_MODEL, D_MODEL)), _full_spec((PLE_DIM, D_MODEL)), _vec_spec(), _vec_spec(), _vec_spec()],
        out_specs=[_row_spec(D_MODEL), _row_spec(D_MODEL), _full_spec((D_MODEL, D_MODEL)), _full_spec((PLE_DIM, D_MODEL)),
                   _vec_spec(), _vec_spec(), _vec_spec(), _full_spec((1, 1))],
        out_shape=[jax.ShapeDtypeStruct((T, D_MODEL), F32), jax.ShapeDtypeStruct((T, D_MODEL), BF16),
                   jax.ShapeDtypeStruct((D_MODEL, D_MODEL), BF16), jax.ShapeDtypeStruct((PLE_DIM, D_MODEL), BF16)]
                  + [jax.ShapeDtypeStruct((1, D_MODEL), F32)] * 3 + [jax.ShapeDtypeStruct((1, 1), F32)],
        scratch_shapes=[pltpu.VMEM((D_MODEL, D_MODEL), F32), pltpu.VMEM((PLE_DIM, D_MODEL), F32)],
        args=(x2, h3, p, f, target, wgate, wproj, g_ple_post, g_ple, g_post_ffn), vmem=VMEM_BIG, job=job)


def _bwd_ffn_act(layer, df, gs, us, wgu, wd, job=None):
    T = df.shape[0]
    tm = min(FFN_ROW_TILE, T)
    nt = T // tm
    sub = tm // FFN_SUB_TILES
    last = FF_CHUNKS - 1
    wgu, wd = _column_views(wgu), _column_views(wd)
    n_gu, n_wd = len(wgu), len(wd)
    wd_cols = _column_ranges(wd)

    def body(df_ref, gs_ref, us_ref, *refs):
        wgu_refs, wd_refs = refs[:n_gu], refs[n_gu:n_gu + n_wd]
        dh_ref, dg_ref, du_ref, a_ref, acc_h = refs[n_gu + n_wd:]
        k = pl.program_id(0)
        i = pl.program_id(1)
        rows = pl.ds(pl.multiple_of(i * tm, tm), tm)
        dhs = []
        for s in range(FFN_SUB_TILES):
            r = pl.ds(s * sub, sub)
            g = gs_ref[r, :].astype(F32)
            u = us_ref[r, :].astype(F32)
            sg = _sigmoid(g)
            silu = g * sg
            a_ref[r, :] = (silu * u).astype(BF16)
            da = _add_all([_dot_nt(df_ref[r, c0:c1], w[...]) for (c0, c1), w in zip(wd_cols, wd_refs)])
            dg = (da * u * (sg * (1.0 + g * (1.0 - sg)))).astype(BF16)
            du = (da * silu).astype(BF16)
            dg_ref[r, :] = dg
            du_ref[r, :] = du
            dhs.append(jnp.concatenate([_dot(dg, w[0]) + _dot(du, w[1]) for w in wgu_refs], axis=1))
        dh = jnp.concatenate(dhs, axis=0)

        @pl.when(k == 0)
        def _():
            acc_h[rows, :] = dh

        @pl.when(jnp.logical_and(k > 0, k < last))
        def _():
            acc_h[rows, :] += dh

        @pl.when(k == last)
        def _():
            dh_ref[...] = acc_h[rows, :] + dh

    chunk_rows = pl.BlockSpec((None, tm, FF_BLOCK), lambda k, i: (k, i, 0))
    saved = jax.ShapeDtypeStruct((FF_CHUNKS, T, FF_BLOCK), BF16)
    return _launch(
        body, name=f"bwd_ffn_act{layer}", grid=(FF_CHUNKS, nt),
        in_specs=[pl.BlockSpec((tm, D_MODEL), lambda k, i: (i, 0)), chunk_rows, chunk_rows]
                 + [pl.BlockSpec((None, 2, FF_BLOCK, FFN_WEIGHT_COLS), lambda k, i, b=b: (k, 0, 0, b)) for _, b in wgu]
                 + [pl.BlockSpec((FF_BLOCK, FFN_WEIGHT_COLS), lambda k, i, b=b: (k, b)) for _, b in wd],
        out_specs=[pl.BlockSpec((tm, D_MODEL), lambda k, i: (jnp.where(k == last, i, 0), 0)),
                   chunk_rows, chunk_rows, chunk_rows],
        out_shape=[jax.ShapeDtypeStruct((T, D_MODEL), F32), saved, saved, saved],
        scratch_shapes=[pltpu.VMEM((T, D_MODEL), F32)],
        args=(df, gs, us, *[w for w, _ in wgu], *[w for w, _ in wd]), vmem=VMEM_BIG, job=job)


def _bwd_ffn_dw(layer, q, parts, h2, df, dg, du, a, job=None):
    T = h2.shape[0]
    width = D_MODEL // parts

    def body(h_ref, df_ref, dg_ref, du_ref, a_ref, dgu_ref, dwd_ref):
        h = h_ref[...]
        dgu_ref[0] = _dot_tn(dg_ref[...], h).astype(BF16)
        dgu_ref[1] = _dot_tn(du_ref[...], h).astype(BF16)
        dwd_ref[...] = _dot_tn(a_ref[...], df_ref[...]).astype(BF16)

    cols = pl.BlockSpec((T, width), lambda k: (0, q))
    chunk = pl.BlockSpec((None, T, FF_BLOCK), lambda k: (k, 0, 0))
    return _launch(
        body, name=f"bwd_ffn_dw{layer}_{q}", grid=(FF_CHUNKS,),
        in_specs=[cols, cols, chunk, chunk, chunk],
        out_specs=[pl.BlockSpec((None, 2, FF_BLOCK, width), lambda k: (k, 0, 0, 0)),
                   pl.BlockSpec((FF_BLOCK, width), lambda k: (k, 0))],
        out_shape=[jax.ShapeDtypeStruct((FF_CHUNKS, 2, FF_BLOCK, width), BF16),
                   jax.ShapeDtypeStruct((D_FF, width), BF16)],
        args=(h2, df, dg, du, a), vmem=VMEM_BIG, job=job)


def _bwd_attn_out(dx2, dh2, x1, y, attn, wo, g_ffn, g_post, job=None):
    T = x1.shape[0]
    nt = T // ROW_TILE

    def body(dx2_ref, dh2_ref, x1_ref, y_ref, a_ref, wo_ref, gffn_ref, gpost_ref,
             dx1_ref, da_ref, dwo_ref, dgf_ref, dgp_ref, acc):
        i = pl.program_id(0)
        first = i == 0
        dxn, dgf = _rms_bwd(x1_ref[...], gffn_ref[...], dh2_ref[...])
        dx1 = dx2_ref[...] + dxn
        dx1_ref[...] = dx1
        dy, dgp = _rms_bwd(y_ref[...].astype(F32), gpost_ref[...], dx1)
        dyb = dy.astype(BF16)
        da_ref[...] = _dot_nt(dyb, wo_ref[...]).astype(BF16)
        _acc(acc, _dot_tn(a_ref[...], dyb), first)
        _acc(dgf_ref, dgf, first)
        _acc(dgp_ref, dgp, first)

        @pl.when(i == nt - 1)
        def _():
            dwo_ref[...] = acc[...].astype(BF16)

    return _launch(
        body, name="bwd_attn_out", grid=(nt,),
        in_specs=[_row_spec(D_MODEL)] * 5 + [_full_spec((D_MODEL, D_MODEL)), _vec_spec(), _vec_spec()],
        out_specs=[_row_spec(D_MODEL), _row_spec(D_MODEL), _full_spec((D_MODEL, D_MODEL)), _vec_spec(), _vec_spec()],
        out_shape=[jax.ShapeDtypeStruct((T, D_MODEL), F32), jax.ShapeDtypeStruct((T, D_MODEL), BF16),
                   jax.ShapeDtypeStruct((D_MODEL, D_MODEL), BF16)] + [jax.ShapeDtypeStruct((1, D_MODEL), F32)] * 2,
        scratch_shapes=[pltpu.VMEM((D_MODEL, D_MODEL), F32)],
        args=(dx2, dh2, x1, y, attn, wo, g_ffn, g_post), job=job)


def _bwd_attention(q, dattn, kpad, vpad, sinks, job=None):
    T = q.shape[0]
    nb = T // ATT_BLOCK

    def body(q_ref, do_ref, k_ref, v_ref, sink_ref, dq_ref, dk_ref, dv_ref, ds_ref, s_scr, dp_scr, p_scr, dsb_scr,
             rel_scr, off_scr):
        n = pl.program_id(0)
        _att_mask(n, rel_scr, off_scr)

        @pl.when(n == 0)
        def _():
            dk_ref[...] = jnp.zeros_like(dk_ref)
            dv_ref[...] = jnp.zeros_like(dv_ref)
            ds_ref[...] = jnp.zeros_like(ds_ref)

        start = pl.multiple_of(n * ATT_BLOCK, ATT_BLOCK)
        win = pl.ds(start, 2 * ATT_BLOCK)
        kw = k_ref[win, :]
        vw = v_ref[win, :]
        lane = lax.broadcasted_iota(jnp.int32, (1, ATT_BLOCK), 1)
        dsink = jnp.zeros((1, ATT_BLOCK), F32)
        dqs, dks, dvs = [], [], []
        for kh in range(N_KV_HEADS):
            kk = kw[:, kh * HEAD_DIM:(kh + 1) * HEAD_DIM]
            vv = vw[:, kh * HEAD_DIM:(kh + 1) * HEAD_DIM]
            qs = _stack_heads(q_ref, kh)
            dos = _stack_heads(do_ref, kh)
            s_scr[...] = _dot_nt(qs, kk)
            dp_scr[...] = _dot_nt(dos, vv)
            for g in range(GQA_GROUP):
                h = kh * GQA_GROUP + g
                dsink_h = jnp.zeros((1, 1), F32)
                for row0 in range(0, ATT_BLOCK, ATT_SUB):
                    rows, sub = pl.ds(g * ATT_BLOCK + row0, ATT_SUB), pl.ds(row0, ATT_SUB)
                    pr, ps = _att_probs(s_scr[rows, :], rel_scr[sub, :], off_scr[sub, :], _alibi_slope(h),
                                        sink_ref[0, h])
                    dp = dp_scr[rows, :]
                    delta = jnp.sum(pr * dp, axis=-1, keepdims=True)
                    dsb_scr[rows, :] = (pr * (dp - delta) * ATT_SCALE).astype(BF16)
                    p_scr[rows, :] = pr.astype(BF16)
                    dsink_h = dsink_h - jnp.sum(ps * delta, axis=0, keepdims=True)
                dsink = dsink + jnp.where(lane == h, dsink_h, 0.0)
            dsb = dsb_scr[...]
            dqs += _unstack_heads(_dot(dsb, kk))
            dks.append(_dot_tn(dsb, qs))
            dvs.append(_dot_tn(p_scr[...], dos))
        dq_ref[...] = jnp.concatenate(dqs, axis=1).astype(BF16)
        dk_ref[win, :] += jnp.concatenate(dks, axis=1)
        dv_ref[win, :] += jnp.concatenate(dvs, axis=1)
        ds_ref[...] += dsink

    return _launch(
        body, name="bwd_attention", grid=(nb,),
        in_specs=[_row_spec(D_MODEL, ATT_BLOCK), _row_spec(D_MODEL, ATT_BLOCK), _full_spec((T + ATT_BLOCK, KV_DIM)),
                  _full_spec((T + ATT_BLOCK, KV_DIM)), pl.BlockSpec(memory_space=pltpu.SMEM)],
        out_specs=[_row_spec(D_MODEL, ATT_BLOCK), _full_spec((T + ATT_BLOCK, KV_DIM)), _full_spec((T + ATT_BLOCK, KV_DIM)),
                   _full_spec((1, ATT_BLOCK))],
        out_shape=[jax.ShapeDtypeStruct((T, D_MODEL), BF16), jax.ShapeDtypeStruct((T + ATT_BLOCK, KV_DIM), F32),
                   jax.ShapeDtypeStruct((T + ATT_BLOCK, KV_DIM), F32), jax.ShapeDtypeStruct((1, ATT_BLOCK), F32)],
        scratch_shapes=[pltpu.VMEM((ATT_GROUP_ROWS, 2 * ATT_BLOCK), F32)] * 2
                       + [pltpu.VMEM((ATT_GROUP_ROWS, 2 * ATT_BLOCK), BF16)] * 2
                       + [pltpu.VMEM((ATT_BLOCK, 2 * ATT_BLOCK), F32)] * 2,
        args=(q, dattn, kpad, vpad, sinks), vmem=VMEM_BIG, job=job)


def _bwd_qkv(dxres, dq, dkv, x3, h1, hk, wq, wkv, g_mix, g_kv, job=None):
    T = x3.shape[0]
    nt = T // ROW_TILE

    def body(dxr_ref, dq_ref, dkv_ref, x_ref, h1_ref, hk_ref, wq_ref, wkv_ref, gmix_ref, gkv_ref,
             dx_ref, dwq_ref, dwkv_ref, dgm_ref, dgk_ref, acc_q, acc_kv):
        i = pl.program_id(0)
        first = i == 0
        dqv = dq_ref[...]
        dkvv = dkv_ref[...]
        xv = x_ref[...]
        d1, dgm = _rms_bwd(xv, gmix_ref[...], _dot_nt(dqv, wq_ref[...]))
        d2, dgk = _rms_bwd(xv, gkv_ref[...], _dot_nt(dkvv, wkv_ref[...]))
        dx_ref[...] = dxr_ref[...] + d1 + d2
        _acc(acc_q, _dot_tn(h1_ref[...], dqv), first)
        _acc(acc_kv, _dot_tn(hk_ref[...], dkvv), first)
        _acc(dgm_ref, dgm, first)
        _acc(dgk_ref, dgk, first)

        @pl.when(i == nt - 1)
        def _():
            dwq_ref[...] = acc_q[...].astype(BF16)
            dwkv_ref[...] = acc_kv[...].astype(BF16)

    return _launch(
        body, name="bwd_qkv", grid=(nt,),
        in_specs=[_row_spec(D_MODEL), _row_spec(D_MODEL), _row_spec(2 * KV_DIM), _row_spec(D_MODEL), _row_spec(D_MODEL),
                  _row_spec(D_MODEL), _full_spec((D_MODEL, D_MODEL)), _full_spec((D_MODEL, 2 * KV_DIM)), _vec_spec(),
                  _vec_spec()],
        out_specs=[_row_spec(D_MODEL), _full_spec((D_MODEL, D_MODEL)), _full_spec((D_MODEL, 2 * KV_DIM)), _vec_spec(),
                   _vec_spec()],
        out_shape=[jax.ShapeDtypeStruct((T, D_MODEL), F32), jax.ShapeDtypeStruct((D_MODEL, D_MODEL), BF16),
                   jax.ShapeDtypeStruct((D_MODEL, 2 * KV_DIM), BF16)] + [jax.ShapeDtypeStruct((1, D_MODEL), F32)] * 2,
        scratch_shapes=[pltpu.VMEM((D_MODEL, D_MODEL), F32), pltpu.VMEM((D_MODEL, 2 * KV_DIM), F32)],
        args=(dxres, dq, dkv, x3, h1, hk, wq, wkv, g_mix, g_kv), job=job)


def _bwd_pool_mixer(dx2, dh2, x1, x, yraw, d, wp, scale, g_ffn, g_post, g_pre, job=None):
    T = x.shape[0]
    tm = ROW_TILE
    nt = T // tm

    def body(dx2_ref, dh2_ref, x1_ref, x_ref, yraw_ref, d_ref, wp_ref, sc_ref, gffn_ref, gpost_ref, gpre_ref,
             dx_ref, dwp_ref, dsc_ref, dgf_ref, dgp_ref, dgm_ref, carry, acc):
        i = pl.program_id(0)
        first = i == 0
        tile = nt - 1 - i

        @pl.when(first)
        def _():
            carry[...] = jnp.zeros_like(carry)

        dxn, dgf = _rms_bwd(x1_ref[...], gffn_ref[...], dh2_ref[...])
        dx1 = dx2_ref[...] + dxn
        yraw = yraw_ref[...].astype(F32)
        sc = sc_ref[...]
        dy, dgp = _rms_bwd(yraw * sc, gpost_ref[...], dx1)
        dsc = jnp.sum(dy * yraw, axis=0, keepdims=True)
        dyb = (dy * sc).astype(BF16)
        dv = d_ref[...]
        dds = []
        for g in range(N_POOL_GROUPS):
            cols = slice(g * POOL_GROUP, (g + 1) * POOL_GROUP)
            dds.append(_dot_nt(dyb[:, cols], wp_ref[g]))
            _acc(acc.at[g], _dot_tn(dv[:, cols], dyb[:, cols]), first)
        dd = jnp.concatenate(dds, axis=1)
        e = dd / _pool_counts(tile * tm, tm)
        ext = jnp.concatenate([e, carry[...]], axis=0)
        carry[...] = e[:POOL_HALO, :]
        sums = _window_sums(ext, lambda k: tm + POOL_HALO - k)[:tm, :]
        dxm, dgm = _rms_bwd(x_ref[...], gpre_ref[...], sums - dd)
        dx_ref[...] = dx1 + dxm
        _acc(dsc_ref, dsc, first)
        _acc(dgf_ref, dgf, first)
        _acc(dgp_ref, dgp, first)
        _acc(dgm_ref, dgm, first)

        @pl.when(i == nt - 1)
        def _():
            dwp_ref[...] = acc[...].astype(BF16)

    rev = pl.BlockSpec((tm, D_MODEL), lambda i: (nt - 1 - i, 0))
    return _launch(
        body, name="bwd_pool_mixer", grid=(nt,),
        in_specs=[rev] * 6 + [_full_spec((N_POOL_GROUPS, POOL_GROUP, POOL_GROUP))] + [_vec_spec()] * 4,
        out_specs=[rev, _full_spec((N_POOL_GROUPS, POOL_GROUP, POOL_GROUP))] + [_vec_spec()] * 4,
        out_shape=[jax.ShapeDtypeStruct((T, D_MODEL), F32),
                   jax.ShapeDtypeStruct((N_POOL_GROUPS, POOL_GROUP, POOL_GROUP), BF16)]
                  + [jax.ShapeDtypeStruct((1, D_MODEL), F32)] * 4,
        scratch_shapes=[pltpu.VMEM((POOL_HALO, D_MODEL), F32), pltpu.VMEM((N_POOL_GROUPS, POOL_GROUP, POOL_GROUP), F32)],
        args=(dx2, dh2, x1, x, yraw, d, wp, scale, g_ffn, g_post, g_pre), job=job)


def _my_place():
    return lax.axis_index("x"), lax.axis_index("y"), lax.axis_index("c")


def _dev_index(px, py, pc):
    return 4 * px + 2 * py + pc


def _peer_by_relation(r):
    x, y, c = _my_place()
    return (x ^ ((r >> 2) & 1), y ^ ((r >> 1) & 1), c ^ (r & 1))


def _slot_pool(ref, j):
    return ref.at[:, pl.ds(pl.multiple_of(j * 32, 32), 32), :]


def _slot_scale(ref, j):
    return ref.at[:, pl.ds(pl.multiple_of(j * 128, 128), 128)]


def _slot_rows128(ref, j):
    return ref.at[pl.ds(pl.multiple_of(j * 128, 128), 128), :]


def _slot_gu(ref, j):
    return ref.at[j % FF_CHUNKS, j // FF_CHUNKS]


def _slot_wd(ref, j):
    return ref.at[pl.ds(pl.multiple_of(j * WD_ROWS, 16), WD_ROWS), :]


def _slot_cols128(ref, j):
    return ref.at[:, pl.ds(pl.multiple_of(j * 128, 128), 128)]


_GATHERED = {
    "pool": ((N_POOL_GROUPS, POOL_GROUP, POOL_GROUP), BF16, _slot_pool),
    "scale": ((1, D_MODEL), F32, _slot_scale),
    "kv": ((D_MODEL, 2 * KV_DIM), BF16, _slot_rows128),
    "q": ((D_MODEL, D_MODEL), BF16, _slot_rows128),
    "o": ((D_MODEL, D_MODEL), BF16, _slot_rows128),
    "gu": ((FF_CHUNKS, 2, FF_BLOCK, D_MODEL), BF16, _slot_gu),
    "wd": ((D_FF, D_MODEL), BF16, _slot_wd),
    "guh": ((FF_CHUNKS, 2, FF_BLOCK, D_MODEL // 2), BF16, _slot_gu),
    "wdh": ((D_FF, D_MODEL // 2), BF16, _slot_wd),
    "gate": ((D_MODEL, D_MODEL), BF16, _slot_rows128),
    "proj": ((PLE_DIM, D_MODEL), BF16, _slot_cols128),
}


def _no_compute():
    pass


class _AllGather:
    peers = ("sibling", "x", "y")

    def __init__(self, names, shards):
        self.kinds = [_GATHERED[n.rstrip("01_")] for n in names]
        entries = [shards[n] if isinstance(shards[n], tuple) else (shards[n], None) for n in names]
        self.args = [array for array, _ in entries]
        self.columns = [columns for _, columns in entries]
        self.out_shape = [jax.ShapeDtypeStruct(shape, dtype) for shape, dtype, _ in self.kinds]
        n = len(names)
        self.scratch = [pltpu.SemaphoreType.DMA((n, 7)), pltpu.SemaphoreType.DMA((n, 7)), pltpu.SemaphoreType.DMA((n,))]

    def _plan(self, srcs, outs, sems):
        send_sems, recv_sems, local_sems = sems
        x, y, c = _my_place()

        def slot(t, dev):
            return self.kinds[t][2](outs[t], _dev_index(*dev))

        def copy(t, k, block, to, src=None):
            return pltpu.make_async_remote_copy(
                src_ref=slot(t, block) if src is None else src, dst_ref=slot(t, block),
                send_sem=send_sems.at[t, k], recv_sem=recv_sems.at[t, k], device_id=to, device_id_type=MESH)

        return types.SimpleNamespace(
            copy=copy, core=c, me=(x, y, c), sibling=(x, y, 1 - c),
            x_chip=(1 - x, y), y_chip=(x, 1 - y), far_chip=(1 - x, 1 - y),
            via=(x ^ (1 - c), y ^ c),
            onto=(x ^ c, y ^ (1 - c)),
            k_via=1 + c, k_onto=2 - c,
            local=[pltpu.make_async_copy(self._shard(srcs, t), slot(t, (x, y, c)), local_sems.at[t])
                   for t in range(len(srcs))])

    def _shard(self, srcs, t):
        if self.columns[t] is None:
            return srcs[t]
        first, end = self.columns[t]
        return srcs[t].at[:, first:end]

    def start(self, srcs, outs, sems):
        p = self._plan(srcs, outs, sems)
        for cp in p.local:
            cp.start()
        for t in range(len(srcs)):
            shard = self._shard(srcs, t)
            p.copy(t, 0, p.me, p.sibling, src=shard).start()
            p.copy(t, 1, p.me, (*p.x_chip, p.core), src=shard).start()
            p.copy(t, 2, p.me, (*p.y_chip, p.core), src=shard).start()

    def mid(self, srcs, outs, sems):
        p = self._plan(srcs, outs, sems)
        for t in range(len(srcs)):
            block = (*p.via, p.core)
            p.copy(t, p.k_via, block, p.me).wait_recv()
            p.copy(t, 3, block, (*p.onto, p.core)).start()
            p.copy(t, 3 + p.k_via, block, p.sibling).start()

    def finish(self, srcs, outs, sems):
        p = self._plan(srcs, outs, sems)
        n = len(srcs)
        for t in range(n):
            block = (*p.onto, p.core)
            p.copy(t, p.k_onto, block, p.me).wait_recv()
            p.copy(t, 3 + p.k_onto, block, p.sibling).start()
        for t in range(n):
            block = (*p.far_chip, p.core)
            p.copy(t, 3, block, p.me).wait_recv()
            p.copy(t, 6, block, p.sibling).start()
        other = 1 - p.core
        for t in range(n):
            p.copy(t, 0, (*p.me[:2], other), p.me).wait_recv()
            for k, chip in ((4, p.x_chip), (5, p.y_chip), (6, p.far_chip)):
                p.copy(t, k, (*chip, other), p.me).wait_recv()
            for k in range(7):
                p.copy(t, k, p.me, p.sibling).wait_send()
        for cp in p.local:
            cp.wait()


def _jobs_only(name, job=None):
    return _launch(_no_compute, name=name, grid=(), in_specs=[], out_specs=[], out_shape=[], args=(), job=job)


def _all_gather_only(name, names, shards):
    return _launch(_no_compute, name=name, grid=(), in_specs=[], out_specs=[], out_shape=[], args=(),
                   job=_AllGather(names, shards))[1]


def _block_pool(ref, j):
    return ref.at[:, pl.ds(pl.multiple_of(j * 32, 32), 32), :]


def _block_rows128(ref, j):
    return ref.at[pl.ds(pl.multiple_of(j * 128, 128), 128), :]


def _block_gu(ref, j):
    return ref.at[j % FF_CHUNKS, j // FF_CHUNKS]


def _block_wd(ref, j):
    return ref.at[pl.ds(pl.multiple_of(j * WD_ROWS, 16), WD_ROWS), :]


def _block_cols128(ref, j):
    return ref.at[:, pl.ds(pl.multiple_of(j * 128, 128), 128)]


_SCATTERED = {
    "pool": ((N_POOL_GROUPS, 32, POOL_GROUP), _block_pool),
    "kv": ((128, 2 * KV_DIM), _block_rows128),
    "q": ((128, D_MODEL), _block_rows128),
    "o": ((128, D_MODEL), _block_rows128),
    "gu": ((FF_BLOCK, FF_PART), _block_gu),
    "wd": ((WD_ROWS, FF_PART), _block_wd),
    "guF": ((FF_BLOCK, D_MODEL), _block_gu),
    "wdF": ((WD_ROWS, D_MODEL), _block_wd),
    "gate": ((128, D_MODEL), _block_rows128),
    "proj": ((PLE_DIM, 128), _block_cols128),
}


class _SiblingSwap:
    peers = ("sibling",)

    def __init__(self, pieces):
        self.kinds = [_SCATTERED[kind] for kind, _ in pieces]
        self.args = [g for _, g in pieces]
        self.out_shape = [jax.ShapeDtypeStruct((N_CHIPS, *block), BF16) for block, _ in self.kinds]
        n = len(pieces)
        self.scratch = [pltpu.SemaphoreType.DMA((n, N_CHIPS)), pltpu.SemaphoreType.DMA((n, N_CHIPS))]

    def _copies(self, srcs, outs, sems):
        send_sems, recv_sems = sems
        x, y, c = _my_place()
        return [pltpu.make_async_remote_copy(
            src_ref=block(srcs[t], 2 * ch + 1 - c), dst_ref=outs[t].at[ch], send_sem=send_sems.at[t, ch],
            recv_sem=recv_sems.at[t, ch], device_id=(x, y, 1 - c), device_id_type=MESH)
            for t, (_, block) in enumerate(self.kinds) for ch in range(N_CHIPS)]

    def start(self, srcs, outs, sems):
        for cp in self._copies(srcs, outs, sems):
            cp.start()

    def finish(self, srcs, outs, sems):
        for cp in self._copies(srcs, outs, sems):
            cp.wait()


class _ChipScatter:
    N_BUFS = 4
    peers = ("x", "y")

    def __init__(self, pieces):
        self.kinds = [_SCATTERED[kind] for kind, _, _ in pieces]
        self.n = n = len(pieces)
        self.args = [g for _, g, _ in pieces] + [s for _, _, s in pieces]
        self.out_shape = [jax.ShapeDtypeStruct((2, *block), BF16) for block, _ in self.kinds]
        self.scratch = []
        for block, _ in self.kinds:
            self.scratch += [pltpu.VMEM((N_CHIPS, *block), BF16)] * 3 + [pltpu.VMEM((2, *block), BF16)]
        dma = pltpu.SemaphoreType.DMA
        self.scratch += [dma((n, N_CHIPS + 1)), dma((n, 2)), dma((n, 2)), dma((n,)), dma((n,)), dma((n,))]

    def _plan(self, outs, scr):
        n = self.n
        first_send, first_recv, second_send, second_recv, keep_sems = scr[self.N_BUFS * n + 1:]
        x, y, c = _my_place()
        via = (x ^ (1 - c), y ^ c)
        onto = (x ^ c, y ^ (1 - c))
        index = lambda chip: 2 * chip[0] + chip[1]
        first, second, keep = [], [], []
        for t in range(n):
            total, inbox = scr[self.N_BUFS * t + 2], scr[self.N_BUFS * t + 3]
            for k, chip in enumerate((via, (1 - x, 1 - y))):
                first.append(pltpu.make_async_remote_copy(
                    src_ref=total.at[index(chip)], dst_ref=inbox.at[k], send_sem=first_send.at[t, k],
                    recv_sem=first_recv.at[t, k], device_id=(*via, c), device_id_type=MESH))
            second.append(pltpu.make_async_remote_copy(
                src_ref=total.at[index(onto)], dst_ref=outs[t].at[1], send_sem=second_send.at[t],
                recv_sem=second_recv.at[t], device_id=(*onto, c), device_id_type=MESH))
            keep.append(pltpu.make_async_copy(total.at[index((x, y))], outs[t].at[0], keep_sems.at[t]))
        return first, second, keep, index((x, y)), index(onto)

    def start(self, ins, outs, scr):
        n = self.n
        load_sems = scr[self.N_BUFS * n]
        c = lax.axis_index("c")
        loads = []
        for t, (_, block) in enumerate(self.kinds):
            mine, theirs = scr[self.N_BUFS * t], scr[self.N_BUFS * t + 1]
            loads += [pltpu.make_async_copy(block(ins[t], 2 * ch + c), mine.at[ch], load_sems.at[t, ch])
                      for ch in range(N_CHIPS)]
            loads.append(pltpu.make_async_copy(ins[n + t], theirs, load_sems.at[t, N_CHIPS]))
        for cp in loads:
            cp.start()
        for cp in loads:
            cp.wait()
        for t in range(n):
            mine, theirs, total = scr[self.N_BUFS * t:self.N_BUFS * t + 3]
            for ch in range(N_CHIPS):
                total[ch] = (mine[ch].astype(F32) + theirs[ch].astype(F32)).astype(BF16)
        for cp in self._plan(outs, scr)[0]:
            cp.start()

    def mid(self, ins, outs, scr):
        first, second, keep, me, onto = self._plan(outs, scr)
        for cp in first:
            cp.wait_recv()
        for t in range(self.n):
            total, inbox = scr[self.N_BUFS * t + 2], scr[self.N_BUFS * t + 3]
            for k, slot in enumerate((me, onto)):
                total[slot] = (total[slot].astype(F32) + inbox[k].astype(F32)).astype(BF16)
        for cp in second + keep:
            cp.start()

    def finish(self, ins, outs, scr):
        first, second, keep, _, _ = self._plan(outs, scr)
        for cp in first:
            cp.wait_send()
        for cp in second + keep:
            cp.wait()


class _Jobs:
    def __init__(self, *jobs):
        self.jobs = jobs
        together = {p for j in jobs for p in j.peers}
        self.peers = tuple(p for p in _PEER_SETS[0] if p in together)
        self.args = [a for j in jobs for a in j.args]
        self.out_shape = [o for j in jobs for o in j.out_shape]
        self.scratch = [s for j in jobs for s in j.scratch]

    def _split(self, refs, attr):
        at = 0
        for j in self.jobs:
            n = len(getattr(j, attr))
            yield refs[at:at + n]
            at += n

    def _each(self, ins, outs, scr):
        return zip(self.jobs, self._split(ins, "args"), self._split(outs, "out_shape"), self._split(scr, "scratch"))

    def start(self, ins, outs, scr):
        for j, i, o, s in self._each(ins, outs, scr):
            j.start(i, o, s)

    def mid(self, ins, outs, scr):
        for j, i, o, s in self._each(ins, outs, scr):
            if hasattr(j, "mid"):
                j.mid(i, o, s)

    def finish(self, ins, outs, scr):
        for j, i, o, s in self._each(ins, outs, scr):
            j.finish(i, o, s)

    def split_outputs(self, outs):
        return list(self._split(outs, "out_shape"))


def _adamw_math(w, g, m, v):
    m = ADAM_B1 * m + (1.0 - ADAM_B1) * g
    v = ADAM_B2 * v + (1.0 - ADAM_B2) * (g * g)
    m_hat = m / (1.0 - ADAM_B1 ** ADAM_STEP)
    v_hat = v / (1.0 - ADAM_B2 ** ADAM_STEP)
    delta = -ADAM_LR * (m_hat / (jnp.sqrt(v_hat) + ADAM_EPS) + ADAM_WD * w)
    return delta, m, v


def _adamw(name, w, m, v, landings, n_col_blocks=1, job=None):
    n_slots, r, c = landings[0].shape
    grid = (w.shape[0] // r, n_col_blocks)

    def body(w_ref, m_ref, v_ref, *rest):
        l_refs, (g_ref, d_ref, nm_ref, nv_ref) = rest[:len(landings)], rest[len(landings):]
        step = pl.program_id(0) * n_col_blocks + pl.program_id(1)
        for idx, l_ref in enumerate(l_refs):
            @pl.when(step == idx)
            def _(l_ref=l_ref):
                g = l_ref[0].astype(F32)
                for s in range(1, n_slots):
                    g = g + l_ref[s].astype(F32)
                g_ref[...] = g
                d_ref[...], nm_ref[...], nv_ref[...] = _adamw_math(w_ref[...], g, m_ref[...], v_ref[...])

    spec = pl.BlockSpec((r, c), lambda a, b: (a, b))
    return _launch(
        body, name=f"adamw_{name}", grid=grid,
        in_specs=[spec, spec, spec] + [_full_spec((n_slots, r, c))] * len(landings),
        out_specs=[spec] * 4, out_shape=[jax.ShapeDtypeStruct(w.shape, F32)] * 4,
        args=(w, m, v, *landings), vmem=VMEM_BIG, job=job)


_SMALL = (("pre_mix_g", SV_PRE_MIX, 2), ("post_mix_g", SV_POST_MIX, 2), ("pre_ffn_g", SV_PRE_FFN, 2),
          ("post_ffn_g", SV_POST_FFN, 2), ("ple_g", SV_PLE, 2), ("ple_post_g", SV_PLE_POST, 2), ("kv_g", SV_KV, 1),
          ("pool_scale", SV_POOL_SCALE, 1), ("sinks", SV_SINKS, 1))


def _small_all_reduce(part):
    def body(part_ref, total_ref, buf, send_sems, recv_sems):
        x, y, c = _my_place()
        me = _dev_index(x, y, c)
        buf[me] = part_ref[...]
        copies = [pltpu.make_async_remote_copy(
            src_ref=buf.at[me], dst_ref=buf.at[me], send_sem=send_sems.at[r - 1], recv_sem=recv_sems.at[r - 1],
            device_id=_peer_by_relation(r), device_id_type=MESH) for r in range(1, N_DEV)]
        for cp in copies:
            cp.start()
        for cp in copies:
            cp.wait()
        g = buf[0]
        for s in range(1, N_DEV):
            g = g + buf[s]
        total_ref[...] = g

    slab = jax.ShapeDtypeStruct((SV_ROWS, D_MODEL), F32)
    (total,), _ = _launch(
        body, name="small_all_reduce", grid=(1,), in_specs=[_full_spec(slab.shape)], out_specs=[_full_spec(slab.shape)],
        out_shape=[slab],
        scratch_shapes=[pltpu.VMEM((N_DEV, SV_ROWS, D_MODEL), F32), pltpu.SemaphoreType.DMA((N_DEV - 1,)),
                        pltpu.SemaphoreType.DMA((N_DEV - 1,))],
        args=(part,))
    return total


def _small_adamw(total, params):
    flat = [a for name, _, _ in _SMALL for a in params[name]]
    n_in = 1 + len(flat)

    def body(*refs):
        total, wmv = refs[0], refs[1:n_in]
        loss_ref, outs = refs[n_in], refs[n_in + 1:]
        me = _dev_index(*_my_place())
        loss_ref[...] = total[SV_LOSS:SV_LOSS + 1, 0:1]
        for idx, (name, row, n_rows) in enumerate(_SMALL):
            w_ref, m_ref, v_ref = wmv[3 * idx:3 * idx + 3]
            g_ref, d_ref, nm_ref, nv_ref = outs[4 * idx:4 * idx + 4]
            if name == "pool_scale":
                g = total[row:row + 1, pl.ds(pl.multiple_of(me * 128, 128), 128)]
            else:
                g = total[row:row + n_rows, 0:w_ref.shape[1]]
            g_ref[...] = g
            d_ref[...], nm_ref[...], nv_ref[...] = _adamw_math(w_ref[...], g, m_ref[...], v_ref[...])

    out_shape = [jax.ShapeDtypeStruct((1, 1), F32)]
    for name, _, _ in _SMALL:
        out_shape += [jax.ShapeDtypeStruct(params[name][0].shape, F32)] * 4
    res, _ = _launch(
        body, name="small_adamw", grid=(1,),
        in_specs=[_full_spec(a.shape) for a in (total, *flat)], out_specs=[_full_spec(s.shape) for s in out_shape],
        out_shape=out_shape, args=(total, *flat))
    return res[0], {name: res[1 + 4 * idx:5 + 4 * idx] for idx, (name, _, _) in enumerate(_SMALL)}


def _local_step(x, p, tgt, gains, sinks, shards, weights):
    row = lambda first_row, layer: _Gain(gains, first_row + layer)
    gather = lambda *names: _AllGather(names, shards)
    g_pre_mix, g_post_mix, g_pre_ffn, g_post_ffn = SV_PRE_MIX, SV_POST_MIX, SV_PRE_FFN, SV_POST_FFN
    g_ple, g_ple_post, g_kv = SV_PLE, SV_PLE_POST, _Gain(gains, SV_KV)

    wp, scale, wgu0 = _all_gather_only("gather_first", ("pool", "scale", "gu0"), shards)
    wgu0 = [wgu0]
    (x1_0, h2_0, yraw, dpool), wd0 = _fwd_pool_mixer(
        x, row(g_pre_mix, 0), wp, scale, row(g_post_mix, 0), row(g_pre_ffn, 0), job=gather("wd0"))
    (gs0, us0, f0, x2_0, h3_0), (wgate0, wproj0, wkv, wq, wgu1_a) = _fwd_ffn(
        0, h2_0, x1_0, wgu0, wd0, row(g_post_ffn, 0), row(g_ple, 0),
        job=gather("gate0", "proj0", "kv", "q", "guh1_0"))
    (x3_0, z0, pe0), (wo,) = _fwd_ple(0, x2_0, h3_0, p[0], wgate0, wproj0, row(g_ple_post, 0), job=gather("o"))
    (hk, h1, q, kv), (wd1_a,) = _fwd_qkv(x3_0, g_kv, row(g_pre_mix, 1), wkv, wq, job=gather("wdh1_0"))
    front = ((ATT_BLOCK, 0), (0, 0))
    kpad = jnp.pad(kv[:, :KV_DIM], front)
    vpad = jnp.pad(kv[:, KV_DIM:], front)
    (attn,), (wgu1_b,) = _fwd_attention(q, kpad, vpad, sinks, job=gather("guh1_1"))
    (y1, x1_1, h2_1), (wd1_b,) = _fwd_attn_out(attn, x3_0, wo, row(g_post_mix, 1), row(g_pre_ffn, 1),
                                               job=gather("wdh1_1"))
    wgu1, wd1 = [wgu1_a, wgu1_b], [wd1_a, wd1_b]
    (gs1, us1, f1, x2_1, h3_1), (wgate1, wproj1) = _fwd_ffn(
        1, h2_1, x1_1, wgu1, wd1, row(g_post_ffn, 1), row(g_ple, 1), job=gather("gate1", "proj1"))

    produced, swapped, landed = {}, {}, {}

    def kind_of(name):
        return name.rstrip("0123_")

    def carry(swap=(), spread=()):
        jobs = []
        if swap:
            jobs.append(_SiblingSwap([(kind_of(n), produced[n]) for n in swap]))
        if spread:
            jobs.append(_ChipScatter([(kind_of(n), produced[n], swapped[n]) for n in spread]))
        return _Jobs(*jobs)

    def carried(jobs, outs, swap=(), spread=()):
        parts = jobs.split_outputs(outs)
        if swap:
            swapped.update(zip(swap, parts[0]))
        if spread:
            landed.update(zip(spread, parts[-1]))

    def hosted(call, *args, swap=(), spread=()):
        jobs = carry(swap, spread)
        outs, job_outs = call(*args, job=jobs)
        carried(jobs, job_outs, swap, spread)
        return outs

    ffn_q = lambda layer, qtr: (f"gu{layer}_{qtr}", f"wd{layer}_{qtr}")

    dx2_1, df1, produced["gate1"], produced["proj1"], dg_ple_post1, dg_ple1, dg_post_ffn1, loss = hosted(
        _ple_loss_bwd, 1, x2_1, h3_1, p[1], f1, tgt, wgate1, wproj1, row(g_ple_post, 1), row(g_ple, 1),
        row(g_post_ffn, 1))
    dh2_1, dg1, du1, a1 = hosted(_bwd_ffn_act, 1, df1, gs1, us1, wgu1, wd1, swap=("gate1", "proj1"))
    produced["guF1"], produced["wdF1"] = hosted(_bwd_ffn_dw, 1, 0, 1, h2_1, df1, dg1, du1, a1,
                                                spread=("gate1", "proj1"))
    dx1_1, dattn, produced["o"], dg_pre_ffn1, dg_post_mix1 = hosted(
        _bwd_attn_out, dx2_1, dh2_1, x1_1, y1, attn, wo, row(g_pre_ffn, 1), row(g_post_mix, 1),
        swap=("guF1", "wdF1"))
    dq, dkpad, dvpad, dsinks = hosted(_bwd_attention, q, dattn, kpad, vpad, sinks, spread=("guF1",))
    dkv = jnp.concatenate([dkpad[ATT_BLOCK:], dvpad[ATT_BLOCK:]], axis=1).astype(BF16)
    dx3_0, produced["q"], produced["kv"], dg_pre_mix1, dg_kv = hosted(
        _bwd_qkv, dx1_1, dq, dkv, x3_0, h1, hk, wq, wkv, row(g_pre_mix, 1), g_kv, swap=("o",), spread=("wdF1",))
    for name in ("gu", "wd"):
        whole = landed.pop(f"{name}F1")
        for half in range(FF_PARTS):
            landed[f"{name}1_{half}"] = whole[..., half * FF_PART:(half + 1) * FF_PART]
    dx2_0, df0, produced["gate0"], produced["proj0"], dg_ple_post0, dg_ple0, dg_post_ffn0 = hosted(
        _bwd_ple, 0, dx3_0, x2_0, z0, pe0, h3_0, p[0], f0, wgate0, row(g_ple_post, 0), row(g_ple, 0),
        row(g_post_ffn, 0), swap=("q", "kv"), spread=("o",))
    dh2_0, dg0, du0, a0 = hosted(_bwd_ffn_act, 0, df0, gs0, us0, wgu0, wd0,
                                 swap=("gate0", "proj0"), spread=("q", "kv"))
    part_hosts = [dict(spread=("gate0", "proj0")), dict(swap=ffn_q(0, 0))]
    for part in range(FF_PARTS):
        produced[f"gu0_{part}"], produced[f"wd0_{part}"] = hosted(
            _bwd_ffn_dw, 0, part, FF_PARTS, h2_0, df0, dg0, du0, a0, **part_hosts[part])
    grad_x, produced["pool"], dscale, dg_pre_ffn0, dg_post_mix0, dg_pre_mix0 = hosted(
        _bwd_pool_mixer, dx2_0, dh2_0, x1_0, x, yraw, dpool, wp, scale, row(g_pre_ffn, 0), row(g_post_mix, 0),
        row(g_pre_mix, 0), swap=ffn_q(0, 1), spread=ffn_q(0, 0))

    def update(name, n_col_blocks=1, pieces=None, swap=(), spread=()):
        w, m, v = weights[name]
        rows = w.size // w.shape[-1]
        flat = [landed[n].reshape(landed[n].shape[0], -1, landed[n].shape[-1])
                for n in (pieces or [kind_short[name]])]
        outs = hosted(_adamw, name, w.reshape(rows, -1), m.reshape(rows, -1), v.reshape(rows, -1), flat,
                      n_col_blocks, swap=swap, spread=spread)
        return [o.reshape(w.shape) for o in outs]

    kind_short = {"w_q": "q", "w_kv": "kv", "w_o": "o", "pool_w": "pool"}
    upd = {}
    hosted(_jobs_only, "scatter_tail0", swap=("pool",), spread=ffn_q(0, 1))
    hosted(_jobs_only, "scatter_tail1", spread=("pool",))
    upd["w_ple_gate"] = update("w_ple_gate", pieces=("gate0", "gate1"))
    upd["w_ple_proj"] = update("w_ple_proj", pieces=("proj0", "proj1"))
    for name in ("w_q", "w_kv", "w_o", "pool_w"):
        upd[name] = update(name)
    upd["w_gu"] = update("w_gu", FF_PARTS,
                         pieces=[f"gu{layer}_{qtr}" for layer in range(2) for qtr in range(FF_PARTS)])
    upd["w_gu"] = [jnp.swapaxes(a, 1, 2) for a in upd["w_gu"]]
    upd["w_down"] = update("w_down", FF_PARTS,
                           pieces=[f"wd{layer}_{qtr}" for layer in range(2) for qtr in range(FF_PARTS)])

    lanes = lambda a: jnp.pad(a, ((0, 0), (0, D_MODEL - a.shape[1])))
    small = jnp.concatenate([
        dg_pre_mix0, dg_pre_mix1, dg_post_mix0, dg_post_mix1, dg_pre_ffn0, dg_pre_ffn1, dg_post_ffn0, dg_post_ffn1,
        dg_ple0, dg_ple1, dg_ple_post0, dg_ple_post1, dg_kv, dscale, lanes(dsinks[:, :N_HEADS]), lanes(loss)], axis=0)
    return grad_x, upd, small


def kernel(x, p, pre_mix_g, post_mix_g, pre_ffn_g, post_ffn_g, pool_w, pool_scale, kv_g, w_kv, w_q, sinks, w_o, w_gu, w_down, ple_g, w_ple_gate, w_ple_proj, ple_post_g, loss_target, m_pre_mix_g, m_post_mix_g, m_pre_ffn_g, m_post_ffn_g, m_pool_w, m_pool_scale, m_kv_g, m_w_kv, m_w_q, m_sinks, m_w_o, m_w_gu, m_w_down, m_ple_g, m_w_ple_gate, m_w_ple_proj, m_ple_post_g, v_pre_mix_g, v_post_mix_g, v_pre_ffn_g, v_post_ffn_g, v_pool_w, v_pool_scale, v_kv_g, v_w_kv, v_w_q, v_sinks, v_w_o, v_w_gu, v_w_down, v_ple_g, v_w_ple_gate, v_w_ple_proj, v_ple_post_g):
    shards = {"pool": pool_w[0].astype(BF16), "scale": pool_scale, "kv": w_kv.astype(BF16),
              "q": w_q[0].astype(BF16), "o": w_o[0].astype(BF16)}
    for layer in range(2):
        shards[f"gu{layer}"] = w_gu[layer].T.astype(BF16)
        shards[f"wd{layer}"] = w_down[layer].astype(BF16)
        for half in range(2):
            cols = (half * D_MODEL // 2, (half + 1) * D_MODEL // 2)
            shards[f"guh{layer}_{half}"] = (shards[f"gu{layer}"], cols)
            shards[f"wdh{layer}_{half}"] = (shards[f"wd{layer}"], cols)
        shards[f"gate{layer}"] = w_ple_gate[layer].astype(BF16)
        shards[f"proj{layer}"] = w_ple_proj[layer].astype(BF16)
    gains = jnp.concatenate([pre_mix_g, post_mix_g, pre_ffn_g, post_ffn_g, ple_g, ple_post_g, kv_g[None, :]],
                            axis=0).reshape(-1, 1, D_MODEL)
    weights = {"pool_w": (pool_w, m_pool_w, v_pool_w), "w_kv": (w_kv, m_w_kv, v_w_kv), "w_q": (w_q, m_w_q, v_w_q),
               "w_o": (w_o, m_w_o, v_w_o), "w_down": (w_down, m_w_down, v_w_down),
               "w_gu": tuple(jnp.swapaxes(a, 1, 2) for a in (w_gu, m_w_gu, v_w_gu)),
               "w_ple_gate": (w_ple_gate, m_w_ple_gate, v_w_ple_gate),
               "w_ple_proj": (w_ple_proj, m_w_ple_proj, v_w_ple_proj)}
    grad_x, upd, small = _local_step(x[0], p[:, 0], loss_target[0], gains, sinks, shards, weights)

    small_params = {
        "pre_mix_g": (pre_mix_g, m_pre_mix_g, v_pre_mix_g), "post_mix_g": (post_mix_g, m_post_mix_g, v_post_mix_g),
        "pre_ffn_g": (pre_ffn_g, m_pre_ffn_g, v_pre_ffn_g), "post_ffn_g": (post_ffn_g, m_post_ffn_g, v_post_ffn_g),
        "ple_g": (ple_g, m_ple_g, v_ple_g), "ple_post_g": (ple_post_g, m_ple_post_g, v_ple_post_g),
        "kv_g": (kv_g[None, :], m_kv_g[None, :], v_kv_g[None, :]),
        "pool_scale": (pool_scale, m_pool_scale, v_pool_scale), "sinks": (sinks, m_sinks, v_sinks)}
    loss, small_upd = _small_adamw(_small_all_reduce(small), small_params)
    small_upd["kv_g"] = [a[0] for a in small_upd["kv_g"]]
    upd.update(small_upd)

    names = ["pre_mix_g", "post_mix_g", "pre_ffn_g", "post_ffn_g", "pool_w", "pool_scale", "kv_g", "w_kv", "w_q",
             "sinks", "w_o", "w_gu", "w_down", "ple_g", "w_ple_gate", "w_ple_proj", "ple_post_g"]
    outs = [loss[0, 0], grad_x[None]]
    for kind in range(4):
        outs += [upd[n][kind] for n in names]
    return tuple(outs)
```

```python
import functools
import types

import jax
import jax.numpy as jnp
from jax import lax
from jax.experimental import pallas as pl
from jax.experimental.pallas import tpu as pltpu

F32 = jnp.float32
BF16 = jnp.bfloat16

N_DEV = 8
D_MODEL = 1024
N_POOL_GROUPS = 4
POOL_GROUP = 256
POOL_HALO = 16
HEAD_DIM = 64
N_HEADS = 16
N_KV_HEADS = 4
GQA_GROUP = 4
KV_DIM = N_KV_HEADS * HEAD_DIM
ATT_BLOCK = 128
D_FF = 2816
FF_CHUNKS = 4
FF_BLOCK = D_FF // FF_CHUNKS
WD_ROWS = D_FF // N_DEV
FF_PARTS = 2
FF_PART = D_MODEL // FF_PARTS
N_CHIPS = 4
PLE_DIM = 256
EPS = 1e-6
NEG_INF = -1e30
ATT_SCALE = HEAD_DIM ** -0.5

ADAM_LR = 0.001
ADAM_B1 = 0.9
ADAM_B2 = 0.999
ADAM_EPS = 1e-08
ADAM_WD = 0.01
ADAM_STEP = 10

ROW_TILE = 512
FFN_ROW_TILE = 512
FFN_WEIGHT_COLS = 512
FFN_SUB_TILES = 1
VMEM_BIG = 60 * 1024 * 1024
VMEM_MID = 56 * 1024 * 1024
HBM_PIN_ELEMS = 1024

SV_ROWS = 16
SV_PRE_MIX, SV_POST_MIX, SV_PRE_FFN, SV_POST_FFN, SV_PLE, SV_PLE_POST = 0, 2, 4, 6, 8, 10
SV_KV, SV_POOL_SCALE, SV_SINKS, SV_LOSS = 12, 13, 14, 15

MESH = pl.DeviceIdType.MESH
ANY = pl.BlockSpec(memory_space=pl.ANY)


def _dot(a, b):
    return jnp.dot(a, b, preferred_element_type=F32)


def _dot_nt(a, b):
    return lax.dot_general(a, b, (((1,), (1,)), ((), ())), preferred_element_type=F32)


def _dot_tn(a, b):
    return lax.dot_general(a, b, (((0,), (0,)), ((), ())), preferred_element_type=F32)


def _rstd(x):
    return lax.rsqrt(jnp.mean(x * x, axis=-1, keepdims=True) + EPS)


def _rms(x, g):
    return x * _rstd(x) * g


def _rms_bwd(x, g, dy):
    r = _rstd(x)
    n = x * r
    dn = dy * g
    dx = r * (dn - n * jnp.mean(dn * n, axis=-1, keepdims=True))
    dg = jnp.sum(dy * n, axis=0, keepdims=True)
    return dx, dg


def _add_all(terms):
    return functools.reduce(jnp.add, terms)


def _sigmoid(x):
    return 1.0 / (1.0 + jnp.exp(-x))


def _acc(ref, val, first):
    @pl.when(first)
    def _():
        ref[...] = val

    @pl.when(jnp.logical_not(first))
    def _():
        ref[...] += val


def _pool_counts(row0, rows):
    t = row0 + lax.broadcasted_iota(jnp.int32, (rows, D_MODEL), 0) + 1
    grp = lax.broadcasted_iota(jnp.int32, (rows, D_MODEL), 1) // POOL_GROUP
    win = jnp.left_shift(2, grp)
    return jnp.minimum(t, win).astype(F32)


def _window_sums(ext, shift_of):
    outs = []
    s = ext
    for gi in range(N_POOL_GROUPS):
        s = s + pltpu.roll(s, shift_of(1 << gi), axis=0)
        outs.append(s[:, :POOL_GROUP])
        s = s[:, POOL_GROUP:]
    return jnp.concatenate(outs, axis=1)


def _cparams(n_axes, vmem, collective_id=None):
    return pltpu.CompilerParams(dimension_semantics=("arbitrary",) * n_axes, vmem_limit_bytes=vmem,
                                collective_id=collective_id)


_PEER_SETS = (("sibling", "x", "y"), ("sibling",), ("x", "y"))


def _meet(peers):
    x, y, c = lax.axis_index("x"), lax.axis_index("y"), lax.axis_index("c")
    device = {"sibling": (x, y, 1 - c), "x": (1 - x, y, c), "y": (x, 1 - y, c)}
    barrier = pltpu.get_barrier_semaphore()
    for peer in peers:
        pl.semaphore_signal(barrier, inc=1, device_id=device[peer], device_id_type=pl.DeviceIdType.MESH)
    pl.semaphore_wait(barrier, len(peers))


def _row_spec(cols, tm=ROW_TILE):
    return pl.BlockSpec((tm, cols), lambda i: (i, 0))


def _full_spec(shape):
    zeros = (0,) * len(shape)
    return pl.BlockSpec(shape, lambda *_: zeros)


def _vec_spec():
    return _full_spec((1, D_MODEL))


def _column_views(parts):
    return [(a, b) for a in parts for b in range(a.shape[-1] // FFN_WEIGHT_COLS)]


def _column_ranges(views):
    return [(n * FFN_WEIGHT_COLS, (n + 1) * FFN_WEIGHT_COLS) for n in range(len(views))]


class _Gain:
    def __init__(self, stacked, layer):
        self.stacked, self.layer = stacked, layer

    def spec(self):
        layer = self.layer
        return pl.BlockSpec((None, 1, D_MODEL), lambda *_: (layer, 0, 0))


def _in_hbm(a):
    return pltpu.with_memory_space_constraint(a, pltpu.HBM) if a.size >= HBM_PIN_ELEMS else a


def _out_in_hbm(s):
    return pltpu.HBM(s.shape, s.dtype) if s.size >= HBM_PIN_ELEMS else s


def _launch(body, *, name, grid, in_specs, out_specs, out_shape, args, scratch_shapes=(), vmem=VMEM_MID, job=None):
    in_specs = [a.spec() if isinstance(a, _Gain) else s for s, a in zip(in_specs, args)]
    args = [_in_hbm(a.stacked if isinstance(a, _Gain) else a) for a in args]
    n_in, n_out, n_scr = len(args), len(out_shape), len(scratch_shapes)
    if job is not None and not job.args:
        job = None
    j_args, j_out, j_scr = ([], [], []) if job is None else ([_in_hbm(a) for a in job.args], job.out_shape, job.scratch)

    def run(*refs):
        groups, at = [], 0
        for n in (n_in, len(j_args), n_out, len(j_out), n_scr, len(j_scr)):
            groups.append(refs[at:at + n])
            at += n
        ins, j_ins, outs, j_outs, scr, j_sems = groups

        def begin():
            _meet(job.peers)
            job.start(j_ins, j_outs, j_sems)

        if job is None:
            body(*ins, *outs, *scr)
        elif not grid:
            begin()
            job.mid(j_ins, j_outs, j_sems)
            body(*ins, *outs, *scr)
            job.finish(j_ins, j_outs, j_sems)
        else:
            ids = [pl.program_id(a) for a in range(len(grid))]
            first = functools.reduce(jnp.logical_and, [i == 0 for i in ids])
            half = functools.reduce(jnp.logical_and, [ids[0] == grid[0] // 2] + [i == 0 for i in ids[1:]])
            last = functools.reduce(jnp.logical_and, [i == g - 1 for i, g in zip(ids, grid)])
            pl.when(first)(begin)
            pl.when(half)(lambda: job.mid(j_ins, j_outs, j_sems))
            body(*ins, *outs, *scr)
            pl.when(last)(lambda: job.finish(j_ins, j_outs, j_sems))

    res = pl.pallas_call(
        run, name=name, grid=grid,
        in_specs=list(in_specs) + [ANY] * len(j_args), out_specs=list(out_specs) + [ANY] * len(j_out),
        out_shape=[_out_in_hbm(s) for s in list(out_shape) + list(j_out)],
        scratch_shapes=list(scratch_shapes) + list(j_scr),
        compiler_params=_cparams(len(grid), vmem, None if job is None else _PEER_SETS.index(job.peers)),
    )(*args, *j_args)
    return res[:n_out], res[n_out:]


def _fwd_pool(x, g_pre, job=None):
    T = x.shape[0]
    tm = ROW_TILE
    nt = T // tm

    def body(x_ref, gpre_ref, d_ref, carry):
        i = pl.program_id(0)

        @pl.when(i == 0)
        def _():
            carry[...] = jnp.zeros_like(carry)

        h = _rms(x_ref[...], gpre_ref[...])
        ext = jnp.concatenate([carry[...], h], axis=0)
        carry[...] = h[tm - POOL_HALO:, :]
        sums = _window_sums(ext, lambda k: k)[POOL_HALO:, :]
        d_ref[...] = (sums / _pool_counts(i * tm, tm) - h).astype(BF16)

    return _launch(
        body, name="fwd_pool", grid=(nt,), in_specs=[_row_spec(D_MODEL), _vec_spec()], out_specs=[_row_spec(D_MODEL)],
        out_shape=[jax.ShapeDtypeStruct((T, D_MODEL), BF16)], scratch_shapes=[pltpu.VMEM((POOL_HALO, D_MODEL), F32)],
        args=(x, g_pre), job=job)


def _fwd_pool_mixer(x, d, wp, scale, g_post, g_ffn, job=None):
    T = x.shape[0]
    nt = T // ROW_TILE

    def body(x_ref, d_ref, wp_ref, sc_ref, gpost_ref, gffn_ref, x1_ref, h2_ref, yraw_ref):
        db = d_ref[...]
        yraw = jnp.concatenate(
            [_dot(db[:, g * POOL_GROUP:(g + 1) * POOL_GROUP], wp_ref[g]) for g in range(N_POOL_GROUPS)], axis=1)
        yraw_ref[...] = yraw.astype(BF16)
        x1 = x_ref[...] + _rms(yraw * sc_ref[...], gpost_ref[...])
        x1_ref[...] = x1
        h2_ref[...] = _rms(x1, gffn_ref[...]).astype(BF16)

    return _launch(
        body, name="fwd_pool_mixer", grid=(nt,),
        in_specs=[_row_spec(D_MODEL), _row_spec(D_MODEL), _full_spec((N_POOL_GROUPS, POOL_GROUP, POOL_GROUP)),
                  _vec_spec(), _vec_spec(), _vec_spec()],
        out_specs=[_row_spec(D_MODEL)] * 3,
        out_shape=[jax.ShapeDtypeStruct((T, D_MODEL), F32)] + [jax.ShapeDtypeStruct((T, D_MODEL), BF16)] * 2,
        args=(x, d, wp, scale, g_post, g_ffn), job=job)


def _fwd_ffn(layer, h2, x1, wgu, wd, g_post, g_ple, job=None):
    T = h2.shape[0]
    tm = min(FFN_ROW_TILE, T)
    nt = T // tm
    sub = tm // FFN_SUB_TILES
    last = FF_CHUNKS - 1
    wgu, wd = _column_views(wgu), _column_views(wd)
    n_gu, n_wd = len(wgu), len(wd)
    gu_cols = _column_ranges(wgu)

    def body(h2_ref, x1_ref, *refs):
        wgu_refs, wd_refs = refs[:n_gu], refs[n_gu:n_gu + n_wd]
        gpost_ref, gple_ref, gs_ref, us_ref, f_ref, x2_ref, h3_ref, acc = refs[n_gu + n_wd:]
        k = pl.program_id(0)
        i = pl.program_id(1)
        rows = pl.ds(pl.multiple_of(i * tm, tm), tm)
        parts = []
        for s in range(FFN_SUB_TILES):
            r = pl.ds(s * sub, sub)
            g = _add_all([_dot_nt(h2_ref[r, c0:c1], w[0]) for (c0, c1), w in zip(gu_cols, wgu_refs)])
            u = _add_all([_dot_nt(h2_ref[r, c0:c1], w[1]) for (c0, c1), w in zip(gu_cols, wgu_refs)])
            gs_ref[r, :] = g.astype(BF16)
            us_ref[r, :] = u.astype(BF16)
            a = (g * _sigmoid(g) * u).astype(BF16)
            parts.append(jnp.concatenate([_dot(a, w[...]) for w in wd_refs], axis=1))
        part = jnp.concatenate(parts, axis=0)

        @pl.when(k == 0)
        def _():
            acc[rows, :] = part

        @pl.when(jnp.logical_and(k > 0, k < last))
        def _():
            acc[rows, :] += part

        @pl.when(k == last)
        def _():
            f = acc[rows, :] + part
            f_ref[...] = f.astype(BF16)
            x2 = x1_ref[...] + _rms(f, gpost_ref[...])
            x2_ref[...] = x2
            h3_ref[...] = _rms(x2, gple_ref[...]).astype(BF16)

    def late(k, i):
        return (jnp.where(k == last, i, 0), 0)

    return _launch(
        body, name=f"fwd_ffn{layer}", grid=(FF_CHUNKS, nt),
        in_specs=[pl.BlockSpec((tm, D_MODEL), lambda k, i: (i, 0)), pl.BlockSpec((tm, D_MODEL), late)]
                 + [pl.BlockSpec((None, 2, FF_BLOCK, FFN_WEIGHT_COLS), lambda k, i, b=b: (k, 0, 0, b)) for _, b in wgu]
                 + [pl.BlockSpec((FF_BLOCK, FFN_WEIGHT_COLS), lambda k, i, b=b: (k, b)) for _, b in wd]
                 + [pl.BlockSpec((1, D_MODEL), lambda k, i: (0, 0))] * 2,
        out_specs=[pl.BlockSpec((None, tm, FF_BLOCK), lambda k, i: (k, i, 0)),
                   pl.BlockSpec((None, tm, FF_BLOCK), lambda k, i: (k, i, 0)),
                   pl.BlockSpec((tm, D_MODEL), late),
                   pl.BlockSpec((tm, D_MODEL), late),
                   pl.BlockSpec((tm, D_MODEL), late)],
        out_shape=[jax.ShapeDtypeStruct((FF_CHUNKS, T, FF_BLOCK), BF16),
                   jax.ShapeDtypeStruct((FF_CHUNKS, T, FF_BLOCK), BF16),
                   jax.ShapeDtypeStruct((T, D_MODEL), BF16),
                   jax.ShapeDtypeStruct((T, D_MODEL), F32),
                   jax.ShapeDtypeStruct((T, D_MODEL), BF16)],
        scratch_shapes=[pltpu.VMEM((T, D_MODEL), F32)],
        args=(h2, x1, *[w for w, _ in wgu], *[w for w, _ in wd], g_post, g_ple), vmem=VMEM_BIG, job=job)


def _fwd_ple_qkv(x2, h3, p, wgate, wproj, g_post, g_kv, g_mix, wkv, wq, job=None):
    T = x2.shape[0]
    nt = T // ROW_TILE

    def body(x2_ref, h3_ref, p_ref, wg_ref, wp_ref, gpost_ref, gkv_ref, gmix_ref, wkv_ref, wq_ref,
             x3_ref, z_ref, pe_ref, hk_ref, h1_ref, q_ref, kv_ref):
        z = _dot(h3_ref[...], wg_ref[...])
        pe = _dot(p_ref[...].astype(BF16), wp_ref[...])
        z_ref[...] = z.astype(BF16)
        pe_ref[...] = pe.astype(BF16)
        x3 = x2_ref[...] + _rms(pe * _sigmoid(z), gpost_ref[...])
        x3_ref[...] = x3
        r = _rstd(x3)
        hk = (x3 * r * gkv_ref[...]).astype(BF16)
        h1 = (x3 * r * gmix_ref[...]).astype(BF16)
        hk_ref[...] = hk
        h1_ref[...] = h1
        kv_ref[...] = _dot(hk, wkv_ref[...]).astype(BF16)
        q_ref[...] = _dot(h1, wq_ref[...]).astype(BF16)

    wide = jax.ShapeDtypeStruct((T, D_MODEL), BF16)
    return _launch(
        body, name="fwd_ple_qkv", grid=(nt,),
        in_specs=[_row_spec(D_MODEL), _row_spec(D_MODEL), _row_spec(PLE_DIM), _full_spec((D_MODEL, D_MODEL)),
                  _full_spec((PLE_DIM, D_MODEL)), _vec_spec(), _vec_spec(), _vec_spec(),
                  _full_spec((D_MODEL, 2 * KV_DIM)), _full_spec((D_MODEL, D_MODEL))],
        out_specs=[_row_spec(D_MODEL)] * 6 + [_row_spec(2 * KV_DIM)],
        out_shape=[jax.ShapeDtypeStruct((T, D_MODEL), F32)] + [wide] * 5 + [jax.ShapeDtypeStruct((T, 2 * KV_DIM), BF16)],
        args=(x2, h3, p, wgate, wproj, g_post, g_kv, g_mix, wkv, wq), job=job)


def _alibi_slope(h):
    return 2.0 ** (-8.0 * (h + 1) / N_HEADS)


ATT_SUB = 32
ATT_GROUP_ROWS = GQA_GROUP * ATT_BLOCK


def _att_mask(n, rel_ref, off_ref):
    qi = lax.broadcasted_iota(jnp.int32, (ATT_BLOCK, 2 * ATT_BLOCK), 0)
    si = lax.broadcasted_iota(jnp.int32, (ATT_BLOCK, 2 * ATT_BLOCK), 1)
    rel = ATT_BLOCK + qi - si
    valid = (rel >= 0) & (rel < ATT_BLOCK) & ((si >= ATT_BLOCK) | (n > 0))
    rel_ref[...] = rel.astype(F32)
    off_ref[...] = jnp.where(valid, 0.0, NEG_INF)


def _att_probs(raw, relf, off, slope, sink):
    s = raw * ATT_SCALE - slope * relf + off
    m = jnp.maximum(jnp.max(s, axis=-1, keepdims=True), sink)
    e = jnp.exp(s - m)
    es = jnp.exp(sink - m)
    inv = 1.0 / (jnp.sum(e, axis=-1, keepdims=True) + es)
    return e * inv, es * inv


def _stack_heads(ref, kh):
    first = kh * GQA_GROUP
    return jnp.concatenate([ref[:, (first + g) * HEAD_DIM:(first + g + 1) * HEAD_DIM] for g in range(GQA_GROUP)], axis=0)


def _unstack_heads(stacked):
    return [stacked[g * ATT_BLOCK:(g + 1) * ATT_BLOCK, :] for g in range(GQA_GROUP)]


def _fwd_attention(q, kpad, vpad, sinks, job=None):
    T = q.shape[0]
    nb = T // ATT_BLOCK

    def body(q_ref, k_ref, v_ref, sink_ref, o_ref, s_scr, p_scr, rel_scr, off_scr):
        n = pl.program_id(0)
        start = pl.multiple_of(n * ATT_BLOCK, ATT_BLOCK)
        kw = k_ref[pl.ds(start, 2 * ATT_BLOCK), :]
        vw = v_ref[pl.ds(start, 2 * ATT_BLOCK), :]
        _att_mask(n, rel_scr, off_scr)
        outs = []
        for kh in range(N_KV_HEADS):
            kk = kw[:, kh * HEAD_DIM:(kh + 1) * HEAD_DIM]
            vv = vw[:, kh * HEAD_DIM:(kh + 1) * HEAD_DIM]
            s_scr[...] = _dot_nt(_stack_heads(q_ref, kh), kk)
            for g in range(GQA_GROUP):
                h = kh * GQA_GROUP + g
                for row0 in range(0, ATT_BLOCK, ATT_SUB):
                    rows, sub = pl.ds(g * ATT_BLOCK + row0, ATT_SUB), pl.ds(row0, ATT_SUB)
                    pr, _ = _att_probs(s_scr[rows, :], rel_scr[sub, :], off_scr[sub, :], _alibi_slope(h),
                                       sink_ref[0, h])
                    p_scr[rows, :] = pr.astype(BF16)
            outs += _unstack_heads(_dot(p_scr[...], vv))
        o_ref[...] = jnp.concatenate(outs, axis=1).astype(BF16)

    return _launch(
        body, name="fwd_attention", grid=(nb,),
        in_specs=[_row_spec(D_MODEL, ATT_BLOCK), _full_spec((T + ATT_BLOCK, KV_DIM)), _full_spec((T + ATT_BLOCK, KV_DIM)),
                  pl.BlockSpec(memory_space=pltpu.SMEM)],
        out_specs=[_row_spec(D_MODEL, ATT_BLOCK)],
        out_shape=[jax.ShapeDtypeStruct((T, D_MODEL), BF16)],
        scratch_shapes=[pltpu.VMEM((ATT_GROUP_ROWS, 2 * ATT_BLOCK), F32), pltpu.VMEM((ATT_GROUP_ROWS, 2 * ATT_BLOCK), BF16)]
                       + [pltpu.VMEM((ATT_BLOCK, 2 * ATT_BLOCK), F32)] * 2,
        args=(q, kpad, vpad, sinks), job=job)


def _fwd_attn_out(attn, x, wo, g_post, g_ffn, job=None):
    T = x.shape[0]
    nt = T // ROW_TILE

    def body(a_ref, x_ref, wo_ref, gpost_ref, gffn_ref, y_ref, x1_ref, h2_ref):
        y = _dot(a_ref[...], wo_ref[...])
        y_ref[...] = y.astype(BF16)
        x1 = x_ref[...] + _rms(y, gpost_ref[...])
        x1_ref[...] = x1
        h2_ref[...] = _rms(x1, gffn_ref[...]).astype(BF16)

    return _launch(
        body, name="fwd_attn_out", grid=(nt,),
        in_specs=[_row_spec(D_MODEL), _row_spec(D_MODEL), _full_spec((D_MODEL, D_MODEL)), _vec_spec(), _vec_spec()],
        out_specs=[_row_spec(D_MODEL)] * 3,
        out_shape=[jax.ShapeDtypeStruct((T, D_MODEL), BF16), jax.ShapeDtypeStruct((T, D_MODEL), F32),
                   jax.ShapeDtypeStruct((T, D_MODEL), BF16)],
        args=(attn, x, wo, g_post, g_ffn), job=job)


def _bwd_ple(layer, dx3, x2, z, pe, h3, p, f, wgate, g_ple_post, g_ple, g_post_ffn, job=None):
    T = x2.shape[0]
    tm = ROW_TILE
    nt = T // tm

    def body(dx3_ref, x2_ref, z_ref, pe_ref, h3_ref, p_ref, f_ref, wg_ref, gpp_ref, gp_ref, gpf_ref,
             dx2_ref, df_ref, dwg_ref, dwp_ref, dgpp_ref, dgp_ref, dgpf_ref, acc_g, acc_p):
        i = pl.program_id(0)
        first = i == 0
        dx3v = dx3_ref[...]
        gate = _sigmoid(z_ref[...].astype(F32))
        pev = pe_ref[...].astype(F32)
        de, dgpp = _rms_bwd(pev * gate, gpp_ref[...], dx3v)
        dpe = (de * gate).astype(BF16)
        dz = (de * pev * gate * (1.0 - gate)).astype(BF16)
        _acc(acc_p, _dot_tn(p_ref[...].astype(BF16), dpe), first)
        _acc(acc_g, _dot_tn(h3_ref[...], dz), first)
        dh3 = _dot_nt(dz, wg_ref[...])
        dxn, dgp = _rms_bwd(x2_ref[...], gp_ref[...], dh3)
        dx2 = dx3v + dxn
        dx2_ref[...] = dx2
        df, dgpf = _rms_bwd(f_ref[...].astype(F32), gpf_ref[...], dx2)
        df_ref[...] = df.astype(BF16)
        _acc(dgpp_ref, dgpp, first)
        _acc(dgp_ref, dgp, first)
        _acc(dgpf_ref, dgpf, first)

        @pl.when(i == nt - 1)
        def _():
            dwg_ref[...] = acc_g[...].astype(BF16)
            dwp_ref[...] = acc_p[...].astype(BF16)

    return _launch(
        body, name=f"bwd_ple{layer}", grid=(nt,),
        in_specs=[_row_spec(D_MODEL)] * 5 + [_row_spec(PLE_DIM), _row_spec(D_MODEL), _full_spec((D_MODEL, D_MODEL)),
                  _vec_spec(), _vec_spec(), _vec_spec()],
        out_specs=[_row_spec(D_MODEL), _row_spec(D_MODEL), _full_spec((D_MODEL, D_MODEL)), _full_spec((PLE_DIM, D_MODEL)),
                   _vec_spec(), _vec_spec(), _vec_spec()],
        out_shape=[jax.ShapeDtypeStruct((T, D_MODEL), F32), jax.ShapeDtypeStruct((T, D_MODEL), BF16),
                   jax.ShapeDtypeStruct((D_MODEL, D_MODEL), BF16), jax.ShapeDtypeStruct((PLE_DIM, D_MODEL), BF16)]
                  + [jax.ShapeDtypeStruct((1, D_MODEL), F32)] * 3,
        scratch_shapes=[pltpu.VMEM((D_MODEL, D_MODEL), F32), pltpu.VMEM((PLE_DIM, D_MODEL), F32)],
        args=(dx3, x2, z, pe, h3, p, f, wgate, g_ple_post, g_ple, g_post_ffn), vmem=VMEM_BIG, job=job)


def _ple_loss_bwd(layer, x2, h3, p, f, target, wgate, wproj, g_ple_post, g_ple, g_post_ffn, job=None):
    T = x2.shape[0]
    tm = ROW_TILE
    nt = T // tm

    def body(x2_ref, h3_ref, p_ref, f_ref, tgt_ref, wg_ref, wp_ref, gpp_ref, gp_ref, gpf_ref,
             dx2_ref, df_ref, dwg_ref, dwp_ref, dgpp_ref, dgp_ref, dgpf_ref, loss_ref, acc_g, acc_p):
        i = pl.program_id(0)
        first = i == 0
        h3 = h3_ref[...]
        pb = p_ref[...].astype(BF16)
        x2v = x2_ref[...]
        gate = _sigmoid(_dot(h3, wg_ref[...]))
        pev = _dot(pb, wp_ref[...])
        e = pev * gate
        err = x2v + _rms(e, gpp_ref[...]) - tgt_ref[...]
        _acc(loss_ref, 0.5 * jnp.sum(jnp.mean(err * err, axis=-1, keepdims=True), axis=0, keepdims=True), first)
        dx3v = err * (1.0 / D_MODEL)
        de, dgpp = _rms_bwd(e, gpp_ref[...], dx3v)
        dpe = (de * gate).astype(BF16)
        dz = (de * pev * gate * (1.0 - gate)).astype(BF16)
        _acc(acc_p, _dot_tn(pb, dpe), first)
        _acc(acc_g, _dot_tn(h3, dz), first)
        dxn, dgp = _rms_bwd(x2v, gp_ref[...], _dot_nt(dz, wg_ref[...]))
        dx2 = dx3v + dxn
        dx2_ref[...] = dx2
        df, dgpf = _rms_bwd(f_ref[...].astype(F32), gpf_ref[...], dx2)
        df_ref[...] = df.astype(BF16)
        _acc(dgpp_ref, dgpp, first)
        _acc(dgp_ref, dgp, first)
        _acc(dgpf_ref, dgpf, first)

        @pl.when(i == nt - 1)
        def _():
            dwg_ref[...] = acc_g[...].astype(BF16)
            dwp_ref[...] = acc_p[...].astype(BF16)

    return _launch(
        body, name=f"ple_loss_bwd{layer}", grid=(nt,),
        in_specs=[_row_spec(D_MODEL), _row_spec(D_MODEL), _row_spec(PLE_DIM), _row_spec(D_MODEL), _row_spec(D_MODEL),
                  _full_spec((D_MODEL, D_MODEL)), _full_spec((PLE_DIM, D_MODEL)), _vec_spec(), _vec_spec(), _vec_spec()],
        out_specs=[_row_spec(D_MODEL), _row_spec(D_MODEL), _full_spec((D_MODEL, D_MODEL)), _full_spec((PLE_DIM, D_MODEL)),
                   _vec_spec(), _vec_spec(), _vec_spec(), _full_spec((1, 1))],
        out_shape=[jax.ShapeDtypeStruct((T, D_MODEL), F32), jax.ShapeDtypeStruct((T, D_MODEL), BF16),
                   jax.ShapeDtypeStruct((D_MODEL, D_MODEL), BF16), jax.ShapeDtypeStruct((PLE_DIM, D_MODEL), BF16)]
                  + [jax.ShapeDtypeStruct((1, D_MODEL), F32)] * 3 + [jax.ShapeDtypeStruct((1, 1), F32)],
        scratch_shapes=[pltpu.VMEM((D_MODEL, D_MODEL), F32), pltpu.VMEM((PLE_DIM, D_MODEL), F32)],
        args=(x2, h3, p, f, target, wgate, wproj, g_ple_post, g_ple, g_post_ffn), vmem=VMEM_BIG, job=job)


def _bwd_ffn_act(layer, df, gs, us, wgu, wd, job=None):
    T = df.shape[0]
    tm = min(FFN_ROW_TILE, T)
    nt = T // tm
    sub = tm // FFN_SUB_TILES
    last = FF_CHUNKS - 1
    wgu, wd = _column_views(wgu), _column_views(wd)
    n_gu, n_wd = len(wgu), len(wd)
    wd_cols = _column_ranges(wd)

    def body(df_ref, gs_ref, us_ref, *refs):
        wgu_refs, wd_refs = refs[:n_gu], refs[n_gu:n_gu + n_wd]
        dh_ref, dg_ref, du_ref, a_ref, acc_h = refs[n_gu + n_wd:]
        k = pl.program_id(0)
        i = pl.program_id(1)
        rows = pl.ds(pl.multiple_of(i * tm, tm), tm)
        dhs = []
        for s in range(FFN_SUB_TILES):
            r = pl.ds(s * sub, sub)
            g = gs_ref[r, :].astype(F32)
            u = us_ref[r, :].astype(F32)
            sg = _sigmoid(g)
            silu = g * sg
            a_ref[r, :] = (silu * u).astype(BF16)
            da = _add_all([_dot_nt(df_ref[r, c0:c1], w[...]) for (c0, c1), w in zip(wd_cols, wd_refs)])
            dg = (da * u * (sg * (1.0 + g * (1.0 - sg)))).astype(BF16)
            du = (da * silu).astype(BF16)
            dg_ref[r, :] = dg
            du_ref[r, :] = du
            dhs.append(jnp.concatenate([_dot(dg, w[0]) + _dot(du, w[1]) for w in wgu_refs], axis=1))
        dh = jnp.concatenate(dhs, axis=0)

        @pl.when(k == 0)
        def _():
            acc_h[rows, :] = dh

        @pl.when(jnp.logical_and(k > 0, k < last))
        def _():
            acc_h[rows, :] += dh

        @pl.when(k == last)
        def _():
            dh_ref[...] = acc_h[rows, :] + dh

    chunk_rows = pl.BlockSpec((None, tm, FF_BLOCK), lambda k, i: (k, i, 0))
    saved = jax.ShapeDtypeStruct((FF_CHUNKS, T, FF_BLOCK), BF16)
    return _launch(
        body, name=f"bwd_ffn_act{layer}", grid=(FF_CHUNKS, nt),
        in_specs=[pl.BlockSpec((tm, D_MODEL), lambda k, i: (i, 0)), chunk_rows, chunk_rows]
                 + [pl.BlockSpec((None, 2, FF_BLOCK, FFN_WEIGHT_COLS), lambda k, i, b=b: (k, 0, 0, b)) for _, b in wgu]
                 + [pl.BlockSpec((FF_BLOCK, FFN_WEIGHT_COLS), lambda k, i, b=b: (k, b)) for _, b in wd],
        out_specs=[pl.BlockSpec((tm, D_MODEL), lambda k, i: (jnp.where(k == last, i, 0), 0)),
                   chunk_rows, chunk_rows, chunk_rows],
        out_shape=[jax.ShapeDtypeStruct((T, D_MODEL), F32), saved, saved, saved],
        scratch_shapes=[pltpu.VMEM((T, D_MODEL), F32)],
        args=(df, gs, us, *[w for w, _ in wgu], *[w for w, _ in wd]), vmem=VMEM_BIG, job=job)


def _bwd_ffn_dw(layer, q, parts, h2, df, dg, du, a, job=None):
    T = h2.shape[0]
    width = D_MODEL // parts

    def body(h_ref, df_ref, dg_ref, du_ref, a_ref, dgu_ref, dwd_ref):
        h = h_ref[...]
        dgu_ref[0] = _dot_tn(dg_ref[...], h).astype(BF16)
        dgu_ref[1] = _dot_tn(du_ref[...], h).astype(BF16)
        dwd_ref[...] = _dot_tn(a_ref[...], df_ref[...]).astype(BF16)

    cols = pl.BlockSpec((T, width), lambda k: (0, q))
    chunk = pl.BlockSpec((None, T, FF_BLOCK), lambda k: (k, 0, 0))
    return _launch(
        body, name=f"bwd_ffn_dw{layer}_{q}", grid=(FF_CHUNKS,),
        in_specs=[cols, cols, chunk, chunk, chunk],
        out_specs=[pl.BlockSpec((None, 2, FF_BLOCK, width), lambda k: (k, 0, 0, 0)),
                   pl.BlockSpec((FF_BLOCK, width), lambda k: (k, 0))],
        out_shape=[jax.ShapeDtypeStruct((FF_CHUNKS, 2, FF_BLOCK, width), BF16),
                   jax.ShapeDtypeStruct((D_FF, width), BF16)],
        args=(h2, df, dg, du, a), vmem=VMEM_BIG, job=job)


def _bwd_attn_out(dx2, dh2, x1, y, attn, wo, g_ffn, g_post, job=None):
    T = x1.shape[0]
    nt = T // ROW_TILE

    def body(dx2_ref, dh2_ref, x1_ref, y_ref, a_ref, wo_ref, gffn_ref, gpost_ref,
             dx1_ref, da_ref, dwo_ref, dgf_ref, dgp_ref, acc):
        i = pl.program_id(0)
        first = i == 0
        dxn, dgf = _rms_bwd(x1_ref[...], gffn_ref[...], dh2_ref[...])
        dx1 = dx2_ref[...] + dxn
        dx1_ref[...] = dx1
        dy, dgp = _rms_bwd(y_ref[...].astype(F32), gpost_ref[...], dx1)
        dyb = dy.astype(BF16)
        da_ref[...] = _dot_nt(dyb, wo_ref[...]).astype(BF16)
        _acc(acc, _dot_tn(a_ref[...], dyb), first)
        _acc(dgf_ref, dgf, first)
        _acc(dgp_ref, dgp, first)

        @pl.when(i == nt - 1)
        def _():
            dwo_ref[...] = acc[...].astype(BF16)

    return _launch(
        body, name="bwd_attn_out", grid=(nt,),
        in_specs=[_row_spec(D_MODEL)] * 5 + [_full_spec((D_MODEL, D_MODEL)), _vec_spec(), _vec_spec()],
        out_specs=[_row_spec(D_MODEL), _row_spec(D_MODEL), _full_spec((D_MODEL, D_MODEL)), _vec_spec(), _vec_spec()],
        out_shape=[jax.ShapeDtypeStruct((T, D_MODEL), F32), jax.ShapeDtypeStruct((T, D_MODEL), BF16),
                   jax.ShapeDtypeStruct((D_MODEL, D_MODEL), BF16)] + [jax.ShapeDtypeStruct((1, D_MODEL), F32)] * 2,
        scratch_shapes=[pltpu.VMEM((D_MODEL, D_MODEL), F32)],
        args=(dx2, dh2, x1, y, attn, wo, g_ffn, g_post), job=job)


def _bwd_attention(q, dattn, kpad, vpad, sinks, job=None):
    T = q.shape[0]
    nb = T // ATT_BLOCK

    def body(q_ref, do_ref, k_ref, v_ref, sink_ref, dq_ref, dk_ref, dv_ref, ds_ref, s_scr, dp_scr, p_scr, dsb_scr,
             rel_scr, off_scr):
        n = pl.program_id(0)
        _att_mask(n, rel_scr, off_scr)

        @pl.when(n == 0)
        def _():
            dk_ref[...] = jnp.zeros_like(dk_ref)
            dv_ref[...] = jnp.zeros_like(dv_ref)
            ds_ref[...] = jnp.zeros_like(ds_ref)

        start = pl.multiple_of(n * ATT_BLOCK, ATT_BLOCK)
        win = pl.ds(start, 2 * ATT_BLOCK)
        kw = k_ref[win, :]
        vw = v_ref[win, :]
        lane = lax.broadcasted_iota(jnp.int32, (1, ATT_BLOCK), 1)
        dsink = jnp.zeros((1, ATT_BLOCK), F32)
        dqs, dks, dvs = [], [], []
        for kh in range(N_KV_HEADS):
            kk = kw[:, kh * HEAD_DIM:(kh + 1) * HEAD_DIM]
            vv = vw[:, kh * HEAD_DIM:(kh + 1) * HEAD_DIM]
            qs = _stack_heads(q_ref, kh)
            dos = _stack_heads(do_ref, kh)
            s_scr[...] = _dot_nt(qs, kk)
            dp_scr[...] = _dot_nt(dos, vv)
            for g in range(GQA_GROUP):
                h = kh * GQA_GROUP + g
                dsink_h = jnp.zeros((1, 1), F32)
                for row0 in range(0, ATT_BLOCK, ATT_SUB):
                    rows, sub = pl.ds(g * ATT_BLOCK + row0, ATT_SUB), pl.ds(row0, ATT_SUB)
                    pr, ps = _att_probs(s_scr[rows, :], rel_scr[sub, :], off_scr[sub, :], _alibi_slope(h),
                                        sink_ref[0, h])
                    dp = dp_scr[rows, :]
                    delta = jnp.sum(pr * dp, axis=-1, keepdims=True)
                    dsb_scr[rows, :] = (pr * (dp - delta) * ATT_SCALE).astype(BF16)
                    p_scr[rows, :] = pr.astype(BF16)
                    dsink_h = dsink_h - jnp.sum(ps * delta, axis=0, keepdims=True)
                dsink = dsink + jnp.where(lane == h, dsink_h, 0.0)
            dsb = dsb_scr[...]
            dqs += _unstack_heads(_dot(dsb, kk))
            dks.append(_dot_tn(dsb, qs))
            dvs.append(_dot_tn(p_scr[...], dos))
        dq_ref[...] = jnp.concatenate(dqs, axis=1).astype(BF16)
        dk_ref[win, :] += jnp.concatenate(dks, axis=1)
        dv_ref[win, :] += jnp.concatenate(dvs, axis=1)
        ds_ref[...] += dsink

    return _launch(
        body, name="bwd_attention", grid=(nb,),
        in_specs=[_row_spec(D_MODEL, ATT_BLOCK), _row_spec(D_MODEL, ATT_BLOCK), _full_spec((T + ATT_BLOCK, KV_DIM)),
                  _full_spec((T + ATT_BLOCK, KV_DIM)), pl.BlockSpec(memory_space=pltpu.SMEM)],
        out_specs=[_row_spec(D_MODEL, ATT_BLOCK), _full_spec((T + ATT_BLOCK, KV_DIM)), _full_spec((T + ATT_BLOCK, KV_DIM)),
                   _full_spec((1, ATT_BLOCK))],
        out_shape=[jax.ShapeDtypeStruct((T, D_MODEL), BF16), jax.ShapeDtypeStruct((T + ATT_BLOCK, KV_DIM), F32),
                   jax.ShapeDtypeStruct((T + ATT_BLOCK, KV_DIM), F32), jax.ShapeDtypeStruct((1, ATT_BLOCK), F32)],
        scratch_shapes=[pltpu.VMEM((ATT_GROUP_ROWS, 2 * ATT_BLOCK), F32)] * 2
                       + [pltpu.VMEM((ATT_GROUP_ROWS, 2 * ATT_BLOCK), BF16)] * 2
                       + [pltpu.VMEM((ATT_BLOCK, 2 * ATT_BLOCK), F32)] * 2,
        args=(q, dattn, kpad, vpad, sinks), vmem=VMEM_BIG, job=job)


def _bwd_qkv(dxres, dq, dkv, x3, h1, hk, wq, wkv, g_mix, g_kv, job=None):
    T = x3.shape[0]
    nt = T // ROW_TILE

    def body(dxr_ref, dq_ref, dkv_ref, x_ref, h1_ref, hk_ref, wq_ref, wkv_ref, gmix_ref, gkv_ref,
             dx_ref, dwq_ref, dwkv_ref, dgm_ref, dgk_ref, acc_q, acc_kv):
        i = pl.program_id(0)
        first = i == 0
        dqv = dq_ref[...]
        dkvv = dkv_ref[...]
        xv = x_ref[...]
        d1, dgm = _rms_bwd(xv, gmix_ref[...], _dot_nt(dqv, wq_ref[...]))
        d2, dgk = _rms_bwd(xv, gkv_ref[...], _dot_nt(dkvv, wkv_ref[...]))
        dx_ref[...] = dxr_ref[...] + d1 + d2
        _acc(acc_q, _dot_tn(h1_ref[...], dqv), first)
        _acc(acc_kv, _dot_tn(hk_ref[...], dkvv), first)
        _acc(dgm_ref, dgm, first)
        _acc(dgk_ref, dgk, first)

        @pl.when(i == nt - 1)
        def _():
            dwq_ref[...] = acc_q[...].astype(BF16)
            dwkv_ref[...] = acc_kv[...].astype(BF16)

    return _launch(
        body, name="bwd_qkv", grid=(nt,),
        in_specs=[_row_spec(D_MODEL), _row_spec(D_MODEL), _row_spec(2 * KV_DIM), _row_spec(D_MODEL), _row_spec(D_MODEL),
                  _row_spec(D_MODEL), _full_spec((D_MODEL, D_MODEL)), _full_spec((D_MODEL, 2 * KV_DIM)), _vec_spec(),
                  _vec_spec()],
        out_specs=[_row_spec(D_MODEL), _full_spec((D_MODEL, D_MODEL)), _full_spec((D_MODEL, 2 * KV_DIM)), _vec_spec(),
                   _vec_spec()],
        out_shape=[jax.ShapeDtypeStruct((T, D_MODEL), F32), jax.ShapeDtypeStruct((D_MODEL, D_MODEL), BF16),
                   jax.ShapeDtypeStruct((D_MODEL, 2 * KV_DIM), BF16)] + [jax.ShapeDtypeStruct((1, D_MODEL), F32)] * 2,
        scratch_shapes=[pltpu.VMEM((D_MODEL, D_MODEL), F32), pltpu.VMEM((D_MODEL, 2 * KV_DIM), F32)],
        args=(dxres, dq, dkv, x3, h1, hk, wq, wkv, g_mix, g_kv), job=job)


def _bwd_pool_mixer(dx2, dh2, x1, x, yraw, d, wp, scale, g_ffn, g_post, g_pre, job=None):
    T = x.shape[0]
    tm = ROW_TILE
    nt = T // tm

    def body(dx2_ref, dh2_ref, x1_ref, x_ref, yraw_ref, d_ref, wp_ref, sc_ref, gffn_ref, gpost_ref, gpre_ref,
             dx_ref, dwp_ref, dsc_ref, dgf_ref, dgp_ref, dgm_ref, carry, acc):
        i = pl.program_id(0)
        first = i == 0
        tile = nt - 1 - i

        @pl.when(first)
        def _():
            carry[...] = jnp.zeros_like(carry)

        dxn, dgf = _rms_bwd(x1_ref[...], gffn_ref[...], dh2_ref[...])
        dx1 = dx2_ref[...] + dxn
        yraw = yraw_ref[...].astype(F32)
        sc = sc_ref[...]
        dy, dgp = _rms_bwd(yraw * sc, gpost_ref[...], dx1)
        dsc = jnp.sum(dy * yraw, axis=0, keepdims=True)
        dyb = (dy * sc).astype(BF16)
        dv = d_ref[...]
        dds = []
        for g in range(N_POOL_GROUPS):
            cols = slice(g * POOL_GROUP, (g + 1) * POOL_GROUP)
            dds.append(_dot_nt(dyb[:, cols], wp_ref[g]))
            _acc(acc.at[g], _dot_tn(dv[:, cols], dyb[:, cols]), first)
        dd = jnp.concatenate(dds, axis=1)
        e = dd / _pool_counts(tile * tm, tm)
        ext = jnp.concatenate([e, carry[...]], axis=0)
        carry[...] = e[:POOL_HALO, :]
        sums = _window_sums(ext, lambda k: tm + POOL_HALO - k)[:tm, :]
        dxm, dgm = _rms_bwd(x_ref[...], gpre_ref[...], sums - dd)
        dx_ref[...] = dx1 + dxm
        _acc(dsc_ref, dsc, first)
        _acc(dgf_ref, dgf, first)
        _acc(dgp_ref, dgp, first)
        _acc(dgm_ref, dgm, first)

        @pl.when(i == nt - 1)
        def _():
            dwp_ref[...] = acc[...].astype(BF16)

    rev = pl.BlockSpec((tm, D_MODEL), lambda i: (nt - 1 - i, 0))
    return _launch(
        body, name="bwd_pool_mixer", grid=(nt,),
        in_specs=[rev] * 6 + [_full_spec((N_POOL_GROUPS, POOL_GROUP, POOL_GROUP))] + [_vec_spec()] * 4,
        out_specs=[rev, _full_spec((N_POOL_GROUPS, POOL_GROUP, POOL_GROUP))] + [_vec_spec()] * 4,
        out_shape=[jax.ShapeDtypeStruct((T, D_MODEL), F32),
                   jax.ShapeDtypeStruct((N_POOL_GROUPS, POOL_GROUP, POOL_GROUP), BF16)]
                  + [jax.ShapeDtypeStruct((1, D_MODEL), F32)] * 4,
        scratch_shapes=[pltpu.VMEM((POOL_HALO, D_MODEL), F32), pltpu.VMEM((N_POOL_GROUPS, POOL_GROUP, POOL_GROUP), F32)],
        args=(dx2, dh2, x1, x, yraw, d, wp, scale, g_ffn, g_post, g_pre), job=job)


def _my_place():
    return lax.axis_index("x"), lax.axis_index("y"), lax.axis_index("c")


def _dev_index(px, py, pc):
    return 4 * px + 2 * py + pc


def _peer_by_relation(r):
    x, y, c = _my_place()
    return (x ^ ((r >> 2) & 1), y ^ ((r >> 1) & 1), c ^ (r & 1))


def _slot_pool(ref, j):
    return ref.at[:, pl.ds(pl.multiple_of(j * 32, 32), 32), :]


def _slot_scale(ref, j):
    return ref.at[:, pl.ds(pl.multiple_of(j * 128, 128), 128)]


def _slot_rows128(ref, j):
    return ref.at[pl.ds(pl.multiple_of(j * 128, 128), 128), :]


def _slot_gu(ref, j):
    return ref.at[j % FF_CHUNKS, j // FF_CHUNKS]


def _slot_wd(ref, j):
    return ref.at[pl.ds(pl.multiple_of(j * WD_ROWS, 16), WD_ROWS), :]


def _slot_cols128(ref, j):
    return ref.at[:, pl.ds(pl.multiple_of(j * 128, 128), 128)]


_GATHERED = {
    "pool": ((N_POOL_GROUPS, POOL_GROUP, POOL_GROUP), BF16, _slot_pool),
    "scale": ((1, D_MODEL), F32, _slot_scale),
    "kv": ((D_MODEL, 2 * KV_DIM), BF16, _slot_rows128),
    "q": ((D_MODEL, D_MODEL), BF16, _slot_rows128),
    "o": ((D_MODEL, D_MODEL), BF16, _slot_rows128),
    "gu": ((FF_CHUNKS, 2, FF_BLOCK, D_MODEL), BF16, _slot_gu),
    "wd": ((D_FF, D_MODEL), BF16, _slot_wd),
    "guh": ((FF_CHUNKS, 2, FF_BLOCK, D_MODEL // 2), BF16, _slot_gu),
    "wdh": ((D_FF, D_MODEL // 2), BF16, _slot_wd),
    "gate": ((D_MODEL, D_MODEL), BF16, _slot_rows128),
    "proj": ((PLE_DIM, D_MODEL), BF16, _slot_cols128),
}


def _no_compute():
    pass


class _AllGather:
    peers = ("sibling", "x", "y")

    def __init__(self, names, shards):
        self.kinds = [_GATHERED[n.rstrip("01_")] for n in names]
        entries = [shards[n] if isinstance(shards[n], tuple) else (shards[n], None) for n in names]
        self.args = [array for array, _ in entries]
        self.columns = [columns for _, columns in entries]
        self.out_shape = [jax.ShapeDtypeStruct(shape, dtype) for shape, dtype, _ in self.kinds]
        n = len(names)
        self.scratch = [pltpu.SemaphoreType.DMA((n, 7)), pltpu.SemaphoreType.DMA((n, 7)), pltpu.SemaphoreType.DMA((n,))]

    def _plan(self, srcs, outs, sems):
        send_sems, recv_sems, local_sems = sems
        x, y, c = _my_place()

        def slot(t, dev):
            return self.kinds[t][2](outs[t], _dev_index(*dev))

        def copy(t, k, block, to, src=None):
            return pltpu.make_async_remote_copy(
                src_ref=slot(t, block) if src is None else src, dst_ref=slot(t, block),
                send_sem=send_sems.at[t, k], recv_sem=recv_sems.at[t, k], device_id=to, device_id_type=MESH)

        return types.SimpleNamespace(
            copy=copy, core=c, me=(x, y, c), sibling=(x, y, 1 - c),
            x_chip=(1 - x, y), y_chip=(x, 1 - y), far_chip=(1 - x, 1 - y),
            via=(x ^ (1 - c), y ^ c),
            onto=(x ^ c, y ^ (1 - c)),
            k_via=1 + c, k_onto=2 - c,
            local=[pltpu.make_async_copy(self._shard(srcs, t), slot(t, (x, y, c)), local_sems.at[t])
                   for t in range(len(srcs))])

    def _shard(self, srcs, t):
        if self.columns[t] is None:
            return srcs[t]
        first, end = self.columns[t]
        return srcs[t].at[:, first:end]

    def start(self, srcs, outs, sems):
        p = self._plan(srcs, outs, sems)
        for cp in p.local:
            cp.start()
        for t in range(len(srcs)):
            shard = self._shard(srcs, t)
            p.copy(t, 0, p.me, p.sibling, src=shard).start()
            p.copy(t, 1, p.me, (*p.x_chip, p.core), src=shard).start()
            p.copy(t, 2, p.me, (*p.y_chip, p.core), src=shard).start()

    def mid(self, srcs, outs, sems):
        p = self._plan(srcs, outs, sems)
        for t in range(len(srcs)):
            block = (*p.via, p.core)
            p.copy(t, p.k_via, block, p.me).wait_recv()
            p.copy(t, 3, block, (*p.onto, p.core)).start()
            p.copy(t, 3 + p.k_via, block, p.sibling).start()

    def finish(self, srcs, outs, sems):
        p = self._plan(srcs, outs, sems)
        n = len(srcs)
        for t in range(n):
            block = (*p.onto, p.core)
            p.copy(t, p.k_onto, block, p.me).wait_recv()
            p.copy(t, 3 + p.k_onto, block, p.sibling).start()
        for t in range(n):
            block = (*p.far_chip, p.core)
            p.copy(t, 3, block, p.me).wait_recv()
            p.copy(t, 6, block, p.sibling).start()
        other = 1 - p.core
        for t in range(n):
            p.copy(t, 0, (*p.me[:2], other), p.me).wait_recv()
            for k, chip in ((4, p.x_chip), (5, p.y_chip), (6, p.far_chip)):
                p.copy(t, k, (*chip, other), p.me).wait_recv()
            for k in range(7):
                p.copy(t, k, p.me, p.sibling).wait_send()
        for cp in p.local:
            cp.wait()


def _jobs_only(name, job=None):
    return _launch(_no_compute, name=name, grid=(), in_specs=[], out_specs=[], out_shape=[], args=(), job=job)


def _block_pool(ref, j):
    return ref.at[:, pl.ds(pl.multiple_of(j * 32, 32), 32), :]


def _block_rows128(ref, j):
    return ref.at[pl.ds(pl.multiple_of(j * 128, 128), 128), :]


def _block_gu(ref, j):
    return ref.at[j % FF_CHUNKS, j // FF_CHUNKS]


def _block_wd(ref, j):
    return ref.at[pl.ds(pl.multiple_of(j * WD_ROWS, 16), WD_ROWS), :]


def _block_cols128(ref, j):
    return ref.at[:, pl.ds(pl.multiple_of(j * 128, 128), 128)]


_SCATTERED = {
    "pool": ((N_POOL_GROUPS, 32, POOL_GROUP), _block_pool),
    "kv": ((128, 2 * KV_DIM), _block_rows128),
    "q": ((128, D_MODEL), _block_rows128),
    "o": ((128, D_MODEL), _block_rows128),
    "gu": ((FF_BLOCK, FF_PART), _block_gu),
    "wd": ((WD_ROWS, FF_PART), _block_wd),
    "guF": ((FF_BLOCK, D_MODEL), _block_gu),
    "wdF": ((WD_ROWS, D_MODEL), _block_wd),
    "gate": ((128, D_MODEL), _block_rows128),
    "proj": ((PLE_DIM, 128), _block_cols128),
}


class _SiblingSwap:
    peers = ("sibling",)

    def __init__(self, pieces):
        self.kinds = [_SCATTERED[kind] for kind, _ in pieces]
        self.args = [g for _, g in pieces]
        self.out_shape = [jax.ShapeDtypeStruct((N_CHIPS, *block), BF16) for block, _ in self.kinds]
        n = len(pieces)
        self.scratch = [pltpu.SemaphoreType.DMA((n, N_CHIPS)), pltpu.SemaphoreType.DMA((n, N_CHIPS))]

    def _copies(self, srcs, outs, sems):
        send_sems, recv_sems = sems
        x, y, c = _my_place()
        return [pltpu.make_async_remote_copy(
            src_ref=block(srcs[t], 2 * ch + 1 - c), dst_ref=outs[t].at[ch], send_sem=send_sems.at[t, ch],
            recv_sem=recv_sems.at[t, ch], device_id=(x, y, 1 - c), device_id_type=MESH)
            for t, (_, block) in enumerate(self.kinds) for ch in range(N_CHIPS)]

    def start(self, srcs, outs, sems):
        for cp in self._copies(srcs, outs, sems):
            cp.start()

    def finish(self, srcs, outs, sems):
        for cp in self._copies(srcs, outs, sems):
            cp.wait()


class _ChipScatter:
    N_BUFS = 4
    peers = ("x", "y")

    def __init__(self, pieces):
        self.kinds = [_SCATTERED[kind] for kind, _, _ in pieces]
        self.n = n = len(pieces)
        self.args = [g for _, g, _ in pieces] + [s for _, _, s in pieces]
        self.out_shape = [jax.ShapeDtypeStruct((2, *block), BF16) for block, _ in self.kinds]
        self.scratch = []
        for block, _ in self.kinds:
            self.scratch += [pltpu.VMEM((N_CHIPS, *block), BF16)] * 3 + [pltpu.VMEM((2, *block), BF16)]
        dma = pltpu.SemaphoreType.DMA
        self.scratch += [dma((n, N_CHIPS + 1)), dma((n, 2)), dma((n, 2)), dma((n,)), dma((n,)), dma((n,))]

    def _plan(self, outs, scr):
        n = self.n
        first_send, first_recv, second_send, second_recv, keep_sems = scr[self.N_BUFS * n + 1:]
        x, y, c = _my_place()
        via = (x ^ (1 - c), y ^ c)
        onto = (x ^ c, y ^ (1 - c))
        index = lambda chip: 2 * chip[0] + chip[1]
        first, second, keep = [], [], []
        for t in range(n):
            total, inbox = scr[self.N_BUFS * t + 2], scr[self.N_BUFS * t + 3]
            for k, chip in enumerate((via, (1 - x, 1 - y))):
                first.append(pltpu.make_async_remote_copy(
                    src_ref=total.at[index(chip)], dst_ref=inbox.at[k], send_sem=first_send.at[t, k],
                    recv_sem=first_recv.at[t, k], device_id=(*via, c), device_id_type=MESH))
            second.append(pltpu.make_async_remote_copy(
                src_ref=total.at[index(onto)], dst_ref=outs[t].at[1], send_sem=second_send.at[t],
                recv_sem=second_recv.at[t], device_id=(*onto, c), device_id_type=MESH))
            keep.append(pltpu.make_async_copy(total.at[index((x, y))], outs[t].at[0], keep_sems.at[t]))
        return first, second, keep, index((x, y)), index(onto)

    def start(self, ins, outs, scr):
        n = self.n
        load_sems = scr[self.N_BUFS * n]
        c = lax.axis_index("c")
        loads = []
        for t, (_, block) in enumerate(self.kinds):
            mine, theirs = scr[self.N_BUFS * t], scr[self.N_BUFS * t + 1]
            loads += [pltpu.make_async_copy(block(ins[t], 2 * ch + c), mine.at[ch], load_sems.at[t, ch])
                      for ch in range(N_CHIPS)]
            loads.append(pltpu.make_async_copy(ins[n + t], theirs, load_sems.at[t, N_CHIPS]))
        for cp in loads:
            cp.start()
        for cp in loads:
            cp.wait()
        for t in range(n):
            mine, theirs, total = scr[self.N_BUFS * t:self.N_BUFS * t + 3]
            for ch in range(N_CHIPS):
                total[ch] = (mine[ch].astype(F32) + theirs[ch].astype(F32)).astype(BF16)
        for cp in self._plan(outs, scr)[0]:
            cp.start()

    def mid(self, ins, outs, scr):
        first, second, keep, me, onto = self._plan(outs, scr)
        for cp in first:
            cp.wait_recv()
        for t in range(self.n):
            total, inbox = scr[self.N_BUFS * t + 2], scr[self.N_BUFS * t + 3]
            for k, slot in enumerate((me, onto)):
                total[slot] = (total[slot].astype(F32) + inbox[k].astype(F32)).astype(BF16)
        for cp in second + keep:
            cp.start()

    def finish(self, ins, outs, scr):
        first, second, keep, _, _ = self._plan(outs, scr)
        for cp in first:
            cp.wait_send()
        for cp in second + keep:
            cp.wait()


class _Jobs:
    def __init__(self, *jobs):
        self.jobs = jobs
        together = {p for j in jobs for p in j.peers}
        self.peers = tuple(p for p in _PEER_SETS[0] if p in together)
        self.args = [a for j in jobs for a in j.args]
        self.out_shape = [o for j in jobs for o in j.out_shape]
        self.scratch = [s for j in jobs for s in j.scratch]

    def _split(self, refs, attr):
        at = 0
        for j in self.jobs:
            n = len(getattr(j, attr))
            yield refs[at:at + n]
            at += n

    def _each(self, ins, outs, scr):
        return zip(self.jobs, self._split(ins, "args"), self._split(outs, "out_shape"), self._split(scr, "scratch"))

    def start(self, ins, outs, scr):
        for j, i, o, s in self._each(ins, outs, scr):
            j.start(i, o, s)

    def mid(self, ins, outs, scr):
        for j, i, o, s in self._each(ins, outs, scr):
            if hasattr(j, "mid"):
                j.mid(i, o, s)

    def finish(self, ins, outs, scr):
        for j, i, o, s in self._each(ins, outs, scr):
            j.finish(i, o, s)

    def split_outputs(self, outs):
        return list(self._split(outs, "out_shape"))


def _adamw_math(w, g, m, v):
    m = ADAM_B1 * m + (1.0 - ADAM_B1) * g
    v = ADAM_B2 * v + (1.0 - ADAM_B2) * (g * g)
    m_hat = m / (1.0 - ADAM_B1 ** ADAM_STEP)
    v_hat = v / (1.0 - ADAM_B2 ** ADAM_STEP)
    delta = -ADAM_LR * (m_hat / (jnp.sqrt(v_hat) + ADAM_EPS) + ADAM_WD * w)
    return delta, m, v


def _adamw(name, w, m, v, landings, n_col_blocks=1, job=None):
    n_slots, r, c = landings[0].shape
    grid = (w.shape[0] // r, n_col_blocks)

    def body(w_ref, m_ref, v_ref, *rest):
        l_refs, (g_ref, d_ref, nm_ref, nv_ref) = rest[:len(landings)], rest[len(landings):]
        step = pl.program_id(0) * n_col_blocks + pl.program_id(1)
        for idx, l_ref in enumerate(l_refs):
            @pl.when(step == idx)
            def _(l_ref=l_ref):
                g = l_ref[0].astype(F32)
                for s in range(1, n_slots):
                    g = g + l_ref[s].astype(F32)
                g_ref[...] = g
                d_ref[...], nm_ref[...], nv_ref[...] = _adamw_math(w_ref[...], g, m_ref[...], v_ref[...])

    spec = pl.BlockSpec((r, c), lambda a, b: (a, b))
    return _launch(
        body, name=f"adamw_{name}", grid=grid,
        in_specs=[spec, spec, spec] + [_full_spec((n_slots, r, c))] * len(landings),
        out_specs=[spec] * 4, out_shape=[jax.ShapeDtypeStruct(w.shape, F32)] * 4,
        args=(w, m, v, *landings), vmem=VMEM_BIG, job=job)


_SMALL = (("pre_mix_g", SV_PRE_MIX, 2), ("post_mix_g", SV_POST_MIX, 2), ("pre_ffn_g", SV_PRE_FFN, 2),
          ("post_ffn_g", SV_POST_FFN, 2), ("ple_g", SV_PLE, 2), ("ple_post_g", SV_PLE_POST, 2), ("kv_g", SV_KV, 1),
          ("pool_scale", SV_POOL_SCALE, 1), ("sinks", SV_SINKS, 1))


def _small_all_reduce(part):
    def body(part_ref, total_ref, buf, send_sems, recv_sems):
        x, y, c = _my_place()
        me = _dev_index(x, y, c)
        buf[me] = part_ref[...]
        copies = [pltpu.make_async_remote_copy(
            src_ref=buf.at[me], dst_ref=buf.at[me], send_sem=send_sems.at[r - 1], recv_sem=recv_sems.at[r - 1],
            device_id=_peer_by_relation(r), device_id_type=MESH) for r in range(1, N_DEV)]
        for cp in copies:
            cp.start()
        for cp in copies:
            cp.wait()
        g = buf[0]
        for s in range(1, N_DEV):
            g = g + buf[s]
        total_ref[...] = g

    slab = jax.ShapeDtypeStruct((SV_ROWS, D_MODEL), F32)
    (total,), _ = _launch(
        body, name="small_all_reduce", grid=(1,), in_specs=[_full_spec(slab.shape)], out_specs=[_full_spec(slab.shape)],
        out_shape=[slab],
        scratch_shapes=[pltpu.VMEM((N_DEV, SV_ROWS, D_MODEL), F32), pltpu.SemaphoreType.DMA((N_DEV - 1,)),
                        pltpu.SemaphoreType.DMA((N_DEV - 1,))],
        args=(part,))
    return total


def _small_adamw(total, params):
    flat = [a for name, _, _ in _SMALL for a in params[name]]
    n_in = 1 + len(flat)

    def body(*refs):
        total, wmv = refs[0], refs[1:n_in]
        loss_ref, outs = refs[n_in], refs[n_in + 1:]
        me = _dev_index(*_my_place())
        loss_ref[...] = total[SV_LOSS:SV_LOSS + 1, 0:1]
        for idx, (name, row, n_rows) in enumerate(_SMALL):
            w_ref, m_ref, v_ref = wmv[3 * idx:3 * idx + 3]
            g_ref, d_ref, nm_ref, nv_ref = outs[4 * idx:4 * idx + 4]
            if name == "pool_scale":
                g = total[row:row + 1, pl.ds(pl.multiple_of(me * 128, 128), 128)]
            else:
                g = total[row:row + n_rows, 0:w_ref.shape[1]]
            g_ref[...] = g
            d_ref[...], nm_ref[...], nv_ref[...] = _adamw_math(w_ref[...], g, m_ref[...], v_ref[...])

    out_shape = [jax.ShapeDtypeStruct((1, 1), F32)]
    for name, _, _ in _SMALL:
        out_shape += [jax.ShapeDtypeStruct(params[name][0].shape, F32)] * 4
    res, _ = _launch(
        body, name="small_adamw", grid=(1,),
        in_specs=[_full_spec(a.shape) for a in (total, *flat)], out_specs=[_full_spec(s.shape) for s in out_shape],
        out_shape=out_shape, args=(total, *flat))
    return res[0], {name: res[1 + 4 * idx:5 + 4 * idx] for idx, (name, _, _) in enumerate(_SMALL)}


def _local_step(x, p, tgt, gains, sinks, shards, weights):
    row = lambda first_row, layer: _Gain(gains, first_row + layer)
    gather = lambda *names: _AllGather(names, shards)
    g_pre_mix, g_post_mix, g_pre_ffn, g_post_ffn = SV_PRE_MIX, SV_POST_MIX, SV_PRE_FFN, SV_POST_FFN
    g_ple, g_ple_post, g_kv = SV_PLE, SV_PLE_POST, _Gain(gains, SV_KV)

    (dpool,), (wp, scale, wgu0) = _fwd_pool(x, row(g_pre_mix, 0), job=gather("pool", "scale", "gu0"))
    wgu0 = [wgu0]
    (x1_0, h2_0, yraw), wd0 = _fwd_pool_mixer(
        x, dpool, wp, scale, row(g_post_mix, 0), row(g_pre_ffn, 0), job=gather("wd0"))
    (gs0, us0, f0, x2_0, h3_0), (wgate0, wproj0, wkv, wq, wgu1_a) = _fwd_ffn(
        0, h2_0, x1_0, wgu0, wd0, row(g_post_ffn, 0), row(g_ple, 0),
        job=gather("gate0", "proj0", "kv", "q", "guh1_0"))
    (x3_0, z0, pe0, hk, h1, q, kv), (wo, wd1_a) = _fwd_ple_qkv(
        x2_0, h3_0, p[0], wgate0, wproj0, row(g_ple_post, 0), g_kv, row(g_pre_mix, 1), wkv, wq,
        job=gather("o", "wdh1_0"))
    front = ((ATT_BLOCK, 0), (0, 0))
    kpad = jnp.pad(kv[:, :KV_DIM], front)
    vpad = jnp.pad(kv[:, KV_DIM:], front)
    (attn,), (wgu1_b,) = _fwd_attention(q, kpad, vpad, sinks, job=gather("guh1_1"))
    (y1, x1_1, h2_1), (wd1_b,) = _fwd_attn_out(attn, x3_0, wo, row(g_post_mix, 1), row(g_pre_ffn, 1),
                                               job=gather("wdh1_1"))
    wgu1, wd1 = [wgu1_a, wgu1_b], [wd1_a, wd1_b]
    (gs1, us1, f1, x2_1, h3_1), (wgate1, wproj1) = _fwd_ffn(
        1, h2_1, x1_1, wgu1, wd1, row(g_post_ffn, 1), row(g_ple, 1), job=gather("gate1", "proj1"))

    produced, swapped, landed = {}, {}, {}

    def kind_of(name):
        return name.rstrip("0123_")

    def carry(swap=(), spread=()):
        jobs = []
        if swap:
            jobs.append(_SiblingSwap([(kind_of(n), produced[n]) for n in swap]))
        if spread:
            jobs.append(_ChipScatter([(kind_of(n), produced[n], swapped[n]) for n in spread]))
        return _Jobs(*jobs)

    def carried(jobs, outs, swap=(), spread=()):
        parts = jobs.split_outputs(outs)
        if swap:
            swapped.update(zip(swap, parts[0]))
        if spread:
            landed.update(zip(spread, parts[-1]))

    def hosted(call, *args, swap=(), spread=()):
        jobs = carry(swap, spread)
        outs, job_outs = call(*args, job=jobs)
        carried(jobs, job_outs, swap, spread)
        return outs

    ffn_q = lambda layer, qtr: (f"gu{layer}_{qtr}", f"wd{layer}_{qtr}")

    dx2_1, df1, produced["gate1"], produced["proj1"], dg_ple_post1, dg_ple1, dg_post_ffn1, loss = hosted(
        _ple_loss_bwd, 1, x2_1, h3_1, p[1], f1, tgt, wgate1, wproj1, row(g_ple_post, 1), row(g_ple, 1),
        row(g_post_ffn, 1))
    dh2_1, dg1, du1, a1 = hosted(_bwd_ffn_act, 1, df1, gs1, us1, wgu1, wd1, swap=("gate1", "proj1"))
    produced["guF1"], produced["wdF1"] = hosted(_bwd_ffn_dw, 1, 0, 1, h2_1, df1, dg1, du1, a1,
                                                spread=("gate1", "proj1"))
    dx1_1, dattn, produced["o"], dg_pre_ffn1, dg_post_mix1 = hosted(
        _bwd_attn_out, dx2_1, dh2_1, x1_1, y1, attn, wo, row(g_pre_ffn, 1), row(g_post_mix, 1),
        swap=("guF1", "wdF1"))
    dq, dkpad, dvpad, dsinks = hosted(_bwd_attention, q, dattn, kpad, vpad, sinks, spread=("guF1",))
    dkv = jnp.concatenate([dkpad[ATT_BLOCK:], dvpad[ATT_BLOCK:]], axis=1).astype(BF16)
    dx3_0, produced["q"], produced["kv"], dg_pre_mix1, dg_kv = hosted(
        _bwd_qkv, dx1_1, dq, dkv, x3_0, h1, hk, wq, wkv, row(g_pre_mix, 1), g_kv, swap=("o",), spread=("wdF1",))
    for name in ("gu", "wd"):
        whole = landed.pop(f"{name}F1")
        for half in range(FF_PARTS):
            landed[f"{name}1_{half}"] = whole[..., half * FF_PART:(half + 1) * FF_PART]
    dx2_0, df0, produced["gate0"], produced["proj0"], dg_ple_post0, dg_ple0, dg_post_ffn0 = hosted(
        _bwd_ple, 0, dx3_0, x2_0, z0, pe0, h3_0, p[0], f0, wgate0, row(g_ple_post, 0), row(g_ple, 0),
        row(g_post_ffn, 0), swap=("q", "kv"), spread=("o",))
    dh2_0, dg0, du0, a0 = hosted(_bwd_ffn_act, 0, df0, gs0, us0, wgu0, wd0,
                                 swap=("gate0", "proj0"), spread=("q", "kv"))
    part_hosts = [dict(spread=("gate0", "proj0")), dict(swap=ffn_q(0, 0))]
    for part in range(FF_PARTS):
        produced[f"gu0_{part}"], produced[f"wd0_{part}"] = hosted(
            _bwd_ffn_dw, 0, part, FF_PARTS, h2_0, df0, dg0, du0, a0, **part_hosts[part])
    grad_x, produced["pool"], dscale, dg_pre_ffn0, dg_post_mix0, dg_pre_mix0 = hosted(
        _bwd_pool_mixer, dx2_0, dh2_0, x1_0, x, yraw, dpool, wp, scale, row(g_pre_ffn, 0), row(g_post_mix, 0),
        row(g_pre_mix, 0), swap=ffn_q(0, 1), spread=ffn_q(0, 0))

    def update(name, n_col_blocks=1, pieces=None, swap=(), spread=()):
        w, m, v = weights[name]
        rows = w.size // w.shape[-1]
        flat = [landed[n].reshape(landed[n].shape[0], -1, landed[n].shape[-1])
                for n in (pieces or [kind_short[name]])]
        outs = hosted(_adamw, name, w.reshape(rows, -1), m.reshape(rows, -1), v.reshape(rows, -1), flat,
                      n_col_blocks, swap=swap, spread=spread)
        return [o.reshape(w.shape) for o in outs]

    kind_short = {"w_q": "q", "w_kv": "kv", "w_o": "o", "pool_w": "pool"}
    upd = {}
    hosted(_jobs_only, "scatter_tail0", swap=("pool",), spread=ffn_q(0, 1))
    hosted(_jobs_only, "scatter_tail1", spread=("pool",))
    upd["w_ple_gate"] = update("w_ple_gate", pieces=("gate0", "gate1"))
    upd["w_ple_proj"] = update("w_ple_proj", pieces=("proj0", "proj1"))
    for name in ("w_q", "w_kv", "w_o", "pool_w"):
        upd[name] = update(name)
    upd["w_gu"] = update("w_gu", FF_PARTS,
                         pieces=[f"gu{layer}_{qtr}" for layer in range(2) for qtr in range(FF_PARTS)])
    upd["w_gu"] = [jnp.swapaxes(a, 1, 2) for a in upd["w_gu"]]
    upd["w_down"] = update("w_down", FF_PARTS,
                           pieces=[f"wd{layer}_{qtr}" for layer in range(2) for qtr in range(FF_PARTS)])

    lanes = lambda a: jnp.pad(a, ((0, 0), (0, D_MODEL - a.shape[1])))
    small = jnp.concatenate([
        dg_pre_mix0, dg_pre_mix1, dg_post_mix0, dg_post_mix1, dg_pre_ffn0, dg_pre_ffn1, dg_post_ffn0, dg_post_ffn1,
        dg_ple0, dg_ple1, dg_ple_post0, dg_ple_post1, dg_kv, dscale, lanes(dsinks[:, :N_HEADS]), lanes(loss)], axis=0)
    return grad_x, upd, small


def kernel(x, p, pre_mix_g, post_mix_g, pre_ffn_g, post_ffn_g, pool_w, pool_scale, kv_g, w_kv, w_q, sinks, w_o, w_gu, w_down, ple_g, w_ple_gate, w_ple_proj, ple_post_g, loss_target, m_pre_mix_g, m_post_mix_g, m_pre_ffn_g, m_post_ffn_g, m_pool_w, m_pool_scale, m_kv_g, m_w_kv, m_w_q, m_sinks, m_w_o, m_w_gu, m_w_down, m_ple_g, m_w_ple_gate, m_w_ple_proj, m_ple_post_g, v_pre_mix_g, v_post_mix_g, v_pre_ffn_g, v_post_ffn_g, v_pool_w, v_pool_scale, v_kv_g, v_w_kv, v_w_q, v_sinks, v_w_o, v_w_gu, v_w_down, v_ple_g, v_w_ple_gate, v_w_ple_proj, v_ple_post_g):
    shards = {"pool": pool_w[0].astype(BF16), "scale": pool_scale, "kv": w_kv.astype(BF16),
              "q": w_q[0].astype(BF16), "o": w_o[0].astype(BF16)}
    for layer in range(2):
        shards[f"gu{layer}"] = w_gu[layer].T.astype(BF16)
        shards[f"wd{layer}"] = w_down[layer].astype(BF16)
        for half in range(2):
            cols = (half * D_MODEL // 2, (half + 1) * D_MODEL // 2)
            shards[f"guh{layer}_{half}"] = (shards[f"gu{layer}"], cols)
            shards[f"wdh{layer}_{half}"] = (shards[f"wd{layer}"], cols)
        shards[f"gate{layer}"] = w_ple_gate[layer].astype(BF16)
        shards[f"proj{layer}"] = w_ple_proj[layer].astype(BF16)
    gains = jnp.concatenate([pre_mix_g, post_mix_g, pre_ffn_g, post_ffn_g, ple_g, ple_post_g, kv_g[None, :]],
                            axis=0).reshape(-1, 1, D_MODEL)
    weights = {"pool_w": (pool_w, m_pool_w, v_pool_w), "w_kv": (w_kv, m_w_kv, v_w_kv), "w_q": (w_q, m_w_q, v_w_q),
               "w_o": (w_o, m_w_o, v_w_o), "w_down": (w_down, m_w_down, v_w_down),
               "w_gu": tuple(jnp.swapaxes(a, 1, 2) for a in (w_gu, m_w_gu, v_w_gu)),
               "w_ple_gate": (w_ple_gate, m_w_ple_gate, v_w_ple_gate),
               "w_ple_proj": (w_ple_proj, m_w_ple_proj, v_w_ple_proj)}
    grad_x, upd, small = _local_step(x[0], p[:, 0], loss_target[0], gains, sinks, shards, weights)

    small_params = {
        "pre_mix_g": (pre_mix_g, m_pre_mix_g, v_pre_mix_g), "post_mix_g": (post_mix_g, m_post_mix_g, v_post_mix_g),
        "pre_ffn_g": (pre_ffn_g, m_pre_ffn_g, v_pre_ffn_g), "post_ffn_g": (post_ffn_g, m_post_ffn_g, v_post_ffn_g),
        "ple_g": (ple_g, m_ple_g, v_ple_g), "ple_post_g": (ple_post_g, m_ple_post_g, v_ple_post_g),
        "kv_g": (kv_g[None, :], m_kv_g[None, :], v_kv_g[None, :]),
        "pool_scale": (pool_scale, m_pool_scale, v_pool_scale), "sinks": (sinks, m_sinks, v_sinks)}
    loss, small_upd = _small_adamw(_small_all_reduce(small), small_params)
    small_upd["kv_g"] = [a[0] for a in small_upd["kv_g"]]
    upd.update(small_upd)

    names = ["pre_mix_g", "post_mix_g", "pre_ffn_g", "post_ffn_g", "pool_w", "pool_scale", "kv_g", "w_kv", "w_q",
             "sinks", "w_o", "w_gu", "w_down", "ple_g", "w_ple_gate", "w_ple_proj", "ple_post_g"]
    outs = [loss[0, 0], grad_x[None]]
    for kind in range(4):
        outs += [upd[n][kind] for n in names]
    return tuple(outs)
```

```python
import functools
import types

import jax
import jax.numpy as jnp
from jax import lax
from jax.experimental import pallas as pl
from jax.experimental.pallas import tpu as pltpu

F32 = jnp.float32
BF16 = jnp.bfloat16

N_DEV = 8
D_MODEL = 1024
N_POOL_GROUPS = 4
POOL_GROUP = 256
POOL_HALO = 16
HEAD_DIM = 64
N_HEADS = 16
N_KV_HEADS = 4
GQA_GROUP = 4
KV_DIM = N_KV_HEADS * HEAD_DIM
ATT_BLOCK = 128
D_FF = 2816
FF_CHUNKS = 4
FF_BLOCK = D_FF // FF_CHUNKS
WD_ROWS = D_FF // N_DEV
FF_PARTS = 2
FF_PART = D_MODEL // FF_PARTS
N_CHIPS = 4
PLE_DIM = 256
EPS = 1e-6
NEG_INF = -1e30
ATT_SCALE = HEAD_DIM ** -0.5

ADAM_LR = 0.001
ADAM_B1 = 0.9
ADAM_B2 = 0.999
ADAM_EPS = 1e-08
ADAM_WD = 0.01
ADAM_STEP = 10

ROW_TILE = 512
FFN_ROW_TILE = 512
FFN_WEIGHT_COLS = 512
FFN_SUB_TILES = 1
VMEM_BIG = 60 * 1024 * 1024
VMEM_MID = 56 * 1024 * 1024
HBM_PIN_ELEMS = 1024

SV_ROWS = 16
SV_PRE_MIX, SV_POST_MIX, SV_PRE_FFN, SV_POST_FFN, SV_PLE, SV_PLE_POST = 0, 2, 4, 6, 8, 10
SV_KV, SV_POOL_SCALE, SV_SINKS, SV_LOSS = 12, 13, 14, 15

MESH = pl.DeviceIdType.MESH
ANY = pl.BlockSpec(memory_space=pl.ANY)


def _dot(a, b):
    return jnp.dot(a, b, preferred_element_type=F32)


def _dot_nt(a, b):
    return lax.dot_general(a, b, (((1,), (1,)), ((), ())), preferred_element_type=F32)


def _dot_tn(a, b):
    return lax.dot_general(a, b, (((0,), (0,)), ((), ())), preferred_element_type=F32)


def _rstd(x):
    return lax.rsqrt(jnp.mean(x * x, axis=-1, keepdims=True) + EPS)


def _rms(x, g):
    return x * _rstd(x) * g


def _rms_bwd(x, g, dy):
    r = _rstd(x)
    n = x * r
    dn = dy * g
    dx = r * (dn - n * jnp.mean(dn * n, axis=-1, keepdims=True))
    dg = jnp.sum(dy * n, axis=0, keepdims=True)
    return dx, dg


def _add_all(terms):
    return functools.reduce(jnp.add, terms)


def _sigmoid(x):
    return 1.0 / (1.0 + jnp.exp(-x))


def _acc(ref, val, first):
    @pl.when(first)
    def _():
        ref[...] = val

    @pl.when(jnp.logical_not(first))
    def _():
        ref[...] += val


def _pool_counts(row0, rows):
    t = row0 + lax.broadcasted_iota(jnp.int32, (rows, D_MODEL), 0) + 1
    grp = lax.broadcasted_iota(jnp.int32, (rows, D_MODEL), 1) // POOL_GROUP
    win = jnp.left_shift(2, grp)
    return jnp.minimum(t, win).astype(F32)


def _window_sums(ext, shift_of):
    outs = []
    s = ext
    for gi in range(N_POOL_GROUPS):
        s = s + pltpu.roll(s, shift_of(1 << gi), axis=0)
        outs.append(s[:, :POOL_GROUP])
        s = s[:, POOL_GROUP:]
    return jnp.concatenate(outs, axis=1)


def _cparams(n_axes, vmem, collective_id=None):
    return pltpu.CompilerParams(dimension_semantics=("arbitrary",) * n_axes, vmem_limit_bytes=vmem,
                                collective_id=collective_id)


_PEER_SETS = (("sibling", "x", "y"), ("sibling",), ("x", "y"))


def _meet(peers):
    x, y, c = lax.axis_index("x"), lax.axis_index("y"), lax.axis_index("c")
    device = {"sibling": (x, y, 1 - c), "x": (1 - x, y, c), "y": (x, 1 - y, c)}
    barrier = pltpu.get_barrier_semaphore()
    for peer in peers:
        pl.semaphore_signal(barrier, inc=1, device_id=device[peer], device_id_type=pl.DeviceIdType.MESH)
    pl.semaphore_wait(barrier, len(peers))


def _row_spec(cols, tm=ROW_TILE):
    return pl.BlockSpec((tm, cols), lambda i: (i, 0))


def _full_spec(shape):
    zeros = (0,) * len(shape)
    return pl.BlockSpec(shape, lambda *_: zeros)


def _vec_spec():
    return _full_spec((1, D_MODEL))


def _column_views(parts):
    return [(a, b) for a in parts for b in range(a.shape[-1] // FFN_WEIGHT_COLS)]


def _column_ranges(views):
    return [(n * FFN_WEIGHT_COLS, (n + 1) * FFN_WEIGHT_COLS) for n in range(len(views))]


class _Gain:
    def __init__(self, stacked, layer):
        self.stacked, self.layer = stacked, layer

    def spec(self):
        layer = self.layer
        return pl.BlockSpec((None, 1, D_MODEL), lambda *_: (layer, 0, 0))


def _in_hbm(a):
    return pltpu.with_memory_space_constraint(a, pltpu.HBM) if a.size >= HBM_PIN_ELEMS else a


def _out_in_hbm(s):
    return pltpu.HBM(s.shape, s.dtype) if s.size >= HBM_PIN_ELEMS else s


def _launch(body, *, name, grid, in_specs, out_specs, out_shape, args, scratch_shapes=(), vmem=VMEM_MID, job=None):
    in_specs = [a.spec() if isinstance(a, _Gain) else s for s, a in zip(in_specs, args)]
    args = [_in_hbm(a.stacked if isinstance(a, _Gain) else a) for a in args]
    n_in, n_out, n_scr = len(args), len(out_shape), len(scratch_shapes)
    if job is not None and not job.args:
        job = None
    j_args, j_out, j_scr = ([], [], []) if job is None else ([_in_hbm(a) for a in job.args], job.out_shape, job.scratch)

    def run(*refs):
        groups, at = [], 0
        for n in (n_in, len(j_args), n_out, len(j_out), n_scr, len(j_scr)):
            groups.append(refs[at:at + n])
            at += n
        ins, j_ins, outs, j_outs, scr, j_sems = groups

        def begin():
            _meet(job.peers)
            job.start(j_ins, j_outs, j_sems)

        if job is None:
            body(*ins, *outs, *scr)
        elif not grid:
            begin()
            job.mid(j_ins, j_outs, j_sems)
            body(*ins, *outs, *scr)
            job.finish(j_ins, j_outs, j_sems)
        else:
            ids = [pl.program_id(a) for a in range(len(grid))]
            first = functools.reduce(jnp.logical_and, [i == 0 for i in ids])
            half = functools.reduce(jnp.logical_and, [ids[0] == grid[0] // 2] + [i == 0 for i in ids[1:]])
            last = functools.reduce(jnp.logical_and, [i == g - 1 for i, g in zip(ids, grid)])
            pl.when(first)(begin)
            pl.when(half)(lambda: job.mid(j_ins, j_outs, j_sems))
            body(*ins, *outs, *scr)
            pl.when(last)(lambda: job.finish(j_ins, j_outs, j_sems))

    res = pl.pallas_call(
        run, name=name, grid=grid,
        in_specs=list(in_specs) + [ANY] * len(j_args), out_specs=list(out_specs) + [ANY] * len(j_out),
        out_shape=[_out_in_hbm(s) for s in list(out_shape) + list(j_out)],
        scratch_shapes=list(scratch_shapes) + list(j_scr),
        compiler_params=_cparams(len(grid), vmem, None if job is None else _PEER_SETS.index(job.peers)),
    )(*args, *j_args)
    return res[:n_out], res[n_out:]


def _fwd_pool(x, g_pre, job=None):
    T = x.shape[0]
    tm = ROW_TILE
    nt = T // tm

    def body(x_ref, gpre_ref, d_ref, carry):
        i = pl.program_id(0)

        @pl.when(i == 0)
        def _():
            carry[...] = jnp.zeros_like(carry)

        h = _rms(x_ref[...], gpre_ref[...])
        ext = jnp.concatenate([carry[...], h], axis=0)
        carry[...] = h[tm - POOL_HALO:, :]
        sums = _window_sums(ext, lambda k: k)[POOL_HALO:, :]
        d_ref[...] = (sums / _pool_counts(i * tm, tm) - h).astype(BF16)

    return _launch(
        body, name="fwd_pool", grid=(nt,), in_specs=[_row_spec(D_MODEL), _vec_spec()], out_specs=[_row_spec(D_MODEL)],
        out_shape=[jax.ShapeDtypeStruct((T, D_MODEL), BF16)], scratch_shapes=[pltpu.VMEM((POOL_HALO, D_MODEL), F32)],
        args=(x, g_pre), job=job)


def _fwd_pool_mixer(x, d, wp, scale, g_post, g_ffn, job=None):
    T = x.shape[0]
    nt = T // ROW_TILE

    def body(x_ref, d_ref, wp_ref, sc_ref, gpost_ref, gffn_ref, x1_ref, h2_ref, yraw_ref):
        db = d_ref[...]
        yraw = jnp.concatenate(
            [_dot(db[:, g * POOL_GROUP:(g + 1) * POOL_GROUP], wp_ref[g]) for g in range(N_POOL_GROUPS)], axis=1)
        yraw_ref[...] = yraw.astype(BF16)
        x1 = x_ref[...] + _rms(yraw * sc_ref[...], gpost_ref[...])
        x1_ref[...] = x1
        h2_ref[...] = _rms(x1, gffn_ref[...]).astype(BF16)

    return _launch(
        body, name="fwd_pool_mixer", grid=(nt,),
        in_specs=[_row_spec(D_MODEL), _row_spec(D_MODEL), _full_spec((N_POOL_GROUPS, POOL_GROUP, POOL_GROUP)),
                  _vec_spec(), _vec_spec(), _vec_spec()],
        out_specs=[_row_spec(D_MODEL)] * 3,
        out_shape=[jax.ShapeDtypeStruct((T, D_MODEL), F32)] + [jax.ShapeDtypeStruct((T, D_MODEL), BF16)] * 2,
        args=(x, d, wp, scale, g_post, g_ffn), job=job)


def _fwd_ffn(layer, h2, x1, wgu, wd, g_post, g_ple, job=None):
    T = h2.shape[0]
    tm = min(FFN_ROW_TILE, T)
    nt = T // tm
    sub = tm // FFN_SUB_TILES
    last = FF_CHUNKS - 1
    wgu, wd = _column_views(wgu), _column_views(wd)
    n_gu, n_wd = len(wgu), len(wd)
    gu_cols = _column_ranges(wgu)

    def body(h2_ref, x1_ref, *refs):
        wgu_refs, wd_refs = refs[:n_gu], refs[n_gu:n_gu + n_wd]
        gpost_ref, gple_ref, gs_ref, us_ref, f_ref, x2_ref, h3_ref, acc = refs[n_gu + n_wd:]
        k = pl.program_id(0)
        i = pl.program_id(1)
        rows = pl.ds(pl.multiple_of(i * tm, tm), tm)
        parts = []
        for s in range(FFN_SUB_TILES):
            r = pl.ds(s * sub, sub)
            g = _add_all([_dot_nt(h2_ref[r, c0:c1], w[0]) for (c0, c1), w in zip(gu_cols, wgu_refs)])
            u = _add_all([_dot_nt(h2_ref[r, c0:c1], w[1]) for (c0, c1), w in zip(gu_cols, wgu_refs)])
            gs_ref[r, :] = g.astype(BF16)
            us_ref[r, :] = u.astype(BF16)
            a = (g * _sigmoid(g) * u).astype(BF16)
            parts.append(jnp.concatenate([_dot(a, w[...]) for w in wd_refs], axis=1))
        part = jnp.concatenate(parts, axis=0)

        @pl.when(k == 0)
        def _():
            acc[rows, :] = part

        @pl.when(jnp.logical_and(k > 0, k < last))
        def _():
            acc[rows, :] += part

        @pl.when(k == last)
        def _():
            f = acc[rows, :] + part
            f_ref[...] = f.astype(BF16)
            x2 = x1_ref[...] + _rms(f, gpost_ref[...])
            x2_ref[...] = x2
            h3_ref[...] = _rms(x2, gple_ref[...]).astype(BF16)

    def late(k, i):
        return (jnp.where(k == last, i, 0), 0)

    return _launch(
        body, name=f"fwd_ffn{layer}", grid=(FF_CHUNKS, nt),
        in_specs=[pl.BlockSpec((tm, D_MODEL), lambda k, i: (i, 0)), pl.BlockSpec((tm, D_MODEL), late)]
                 + [pl.BlockSpec((None, 2, FF_BLOCK, FFN_WEIGHT_COLS), lambda k, i, b=b: (k, 0, 0, b)) for _, b in wgu]
                 + [pl.BlockSpec((FF_BLOCK, FFN_WEIGHT_COLS), lambda k, i, b=b: (k, b)) for _, b in wd]
                 + [pl.BlockSpec((1, D_MODEL), lambda k, i: (0, 0))] * 2,
        out_specs=[pl.BlockSpec((None, tm, FF_BLOCK), lambda k, i: (k, i, 0)),
                   pl.BlockSpec((None, tm, FF_BLOCK), lambda k, i: (k, i, 0)),
                   pl.BlockSpec((tm, D_MODEL), late),
                   pl.BlockSpec((tm, D_MODEL), late),
                   pl.BlockSpec((tm, D_MODEL), late)],
        out_shape=[jax.ShapeDtypeStruct((FF_CHUNKS, T, FF_BLOCK), BF16),
                   jax.ShapeDtypeStruct((FF_CHUNKS, T, FF_BLOCK), BF16),
                   jax.ShapeDtypeStruct((T, D_MODEL), BF16),
                   jax.ShapeDtypeStruct((T, D_MODEL), F32),
                   jax.ShapeDtypeStruct((T, D_MODEL), BF16)],
        scratch_shapes=[pltpu.VMEM((T, D_MODEL), F32)],
        args=(h2, x1, *[w for w, _ in wgu], *[w for w, _ in wd], g_post, g_ple), vmem=VMEM_BIG, job=job)


def _fwd_ple_qkv(x2, h3, p, wgate, wproj, g_post, g_kv, g_mix, wkv, wq, job=None):
    T = x2.shape[0]
    nt = T // ROW_TILE

    def body(x2_ref, h3_ref, p_ref, wg_ref, wp_ref, gpost_ref, gkv_ref, gmix_ref, wkv_ref, wq_ref,
             x3_ref, z_ref, pe_ref, hk_ref, h1_ref, q_ref, kv_ref):
        z = _dot(h3_ref[...], wg_ref[...])
        pe = _dot(p_ref[...].astype(BF16), wp_ref[...])
        z_ref[...] = z.astype(BF16)
        pe_ref[...] = pe.astype(BF16)
        x3 = x2_ref[...] + _rms(pe * _sigmoid(z), gpost_ref[...])
        x3_ref[...] = x3
        r = _rstd(x3)
        hk = (x3 * r * gkv_ref[...]).astype(BF16)
        h1 = (x3 * r * gmix_ref[...]).astype(BF16)
        hk_ref[...] = hk
        h1_ref[...] = h1
        kv_ref[...] = _dot(hk, wkv_ref[...]).astype(BF16)
        q_ref[...] = _dot(h1, wq_ref[...]).astype(BF16)

    wide = jax.ShapeDtypeStruct((T, D_MODEL), BF16)
    return _launch(
        body, name="fwd_ple_qkv", grid=(nt,),
        in_specs=[_row_spec(D_MODEL), _row_spec(D_MODEL), _row_spec(PLE_DIM), _full_spec((D_MODEL, D_MODEL)),
                  _full_spec((PLE_DIM, D_MODEL)), _vec_spec(), _vec_spec(), _vec_spec(),
                  _full_spec((D_MODEL, 2 * KV_DIM)), _full_spec((D_MODEL, D_MODEL))],
        out_specs=[_row_spec(D_MODEL)] * 6 + [_row_spec(2 * KV_DIM)],
        out_shape=[jax.ShapeDtypeStruct((T, D_MODEL), F32)] + [wide] * 5 + [jax.ShapeDtypeStruct((T, 2 * KV_DIM), BF16)],
        args=(x2, h3, p, wgate, wproj, g_post, g_kv, g_mix, wkv, wq), job=job)


def _alibi_slope(h):
    return 2.0 ** (-8.0 * (h + 1) / N_HEADS)


ATT_SUB = 32
ATT_GROUP_ROWS = GQA_GROUP * ATT_BLOCK


def _att_mask(n, rel_ref, off_ref):
    qi = lax.broadcasted_iota(jnp.int32, (ATT_BLOCK, 2 * ATT_BLOCK), 0)
    si = lax.broadcasted_iota(jnp.int32, (ATT_BLOCK, 2 * ATT_BLOCK), 1)
    rel = ATT_BLOCK + qi - si
    valid = (rel >= 0) & (rel < ATT_BLOCK) & ((si >= ATT_BLOCK) | (n > 0))
    rel_ref[...] = rel.astype(F32)
    off_ref[...] = jnp.where(valid, 0.0, NEG_INF)


def _att_probs(raw, relf, off, slope, sink):
    s = raw * ATT_SCALE - slope * relf + off
    m = jnp.maximum(jnp.max(s, axis=-1, keepdims=True), sink)
    e = jnp.exp(s - m)
    es = jnp.exp(sink - m)
    inv = 1.0 / (jnp.sum(e, axis=-1, keepdims=True) + es)
    return e * inv, es * inv


def _stack_heads(ref, kh):
    first = kh * GQA_GROUP
    return jnp.concatenate([ref[:, (first + g) * HEAD_DIM:(first + g + 1) * HEAD_DIM] for g in range(GQA_GROUP)], axis=0)


def _unstack_heads(stacked):
    return [stacked[g * ATT_BLOCK:(g + 1) * ATT_BLOCK, :] for g in range(GQA_GROUP)]


def _fwd_attention(q, kpad, vpad, sinks, job=None):
    T = q.shape[0]
    nb = T // ATT_BLOCK

    def body(q_ref, k_ref, v_ref, sink_ref, o_ref, s_scr, p_scr, rel_scr, off_scr):
        n = pl.program_id(0)
        start = pl.multiple_of(n * ATT_BLOCK, ATT_BLOCK)
        kw = k_ref[pl.ds(start, 2 * ATT_BLOCK), :]
        vw = v_ref[pl.ds(start, 2 * ATT_BLOCK), :]
        _att_mask(n, rel_scr, off_scr)
        outs = []
        for kh in range(N_KV_HEADS):
            kk = kw[:, kh * HEAD_DIM:(kh + 1) * HEAD_DIM]
            vv = vw[:, kh * HEAD_DIM:(kh + 1) * HEAD_DIM]
            s_scr[...] = _dot_nt(_stack_heads(q_ref, kh), kk)
            for g in range(GQA_GROUP):
                h = kh * GQA_GROUP + g
                for row0 in range(0, ATT_BLOCK, ATT_SUB):
                    rows, sub = pl.ds(g * ATT_BLOCK + row0, ATT_SUB), pl.ds(row0, ATT_SUB)
                    pr, _ = _att_probs(s_scr[rows, :], rel_scr[sub, :], off_scr[sub, :], _alibi_slope(h),
                                       sink_ref[0, h])
                    p_scr[rows, :] = pr.astype(BF16)
            outs += _unstack_heads(_dot(p_scr[...], vv))
        o_ref[...] = jnp.concatenate(outs, axis=1).astype(BF16)

    return _launch(
        body, name="fwd_attention", grid=(nb,),
        in_specs=[_row_spec(D_MODEL, ATT_BLOCK), _full_spec((T + ATT_BLOCK, KV_DIM)), _full_spec((T + ATT_BLOCK, KV_DIM)),
                  pl.BlockSpec(memory_space=pltpu.SMEM)],
        out_specs=[_row_spec(D_MODEL, ATT_BLOCK)],
        out_shape=[jax.ShapeDtypeStruct((T, D_MODEL), BF16)],
        scratch_shapes=[pltpu.VMEM((ATT_GROUP_ROWS, 2 * ATT_BLOCK), F32), pltpu.VMEM((ATT_GROUP_ROWS, 2 * ATT_BLOCK), BF16)]
                       + [pltpu.VMEM((ATT_BLOCK, 2 * ATT_BLOCK), F32)] * 2,
        args=(q, kpad, vpad, sinks), job=job)


def _fwd_attn_out(attn, x, wo, g_post, g_ffn, job=None):
    T = x.shape[0]
    nt = T // ROW_TILE

    def body(a_ref, x_ref, wo_ref, gpost_ref, gffn_ref, y_ref, x1_ref, h2_ref):
        y = _dot(a_ref[...], wo_ref[...])
        y_ref[...] = y.astype(BF16)
        x1 = x_ref[...] + _rms(y, gpost_ref[...])
        x1_ref[...] = x1
        h2_ref[...] = _rms(x1, gffn_ref[...]).astype(BF16)

    return _launch(
        body, name="fwd_attn_out", grid=(nt,),
        in_specs=[_row_spec(D_MODEL), _row_spec(D_MODEL), _full_spec((D_MODEL, D_MODEL)), _vec_spec(), _vec_spec()],
        out_specs=[_row_spec(D_MODEL)] * 3,
        out_shape=[jax.ShapeDtypeStruct((T, D_MODEL), BF16), jax.ShapeDtypeStruct((T, D_MODEL), F32),
                   jax.ShapeDtypeStruct((T, D_MODEL), BF16)],
        args=(attn, x, wo, g_post, g_ffn), job=job)


def _bwd_ple(layer, dx3, x2, z, pe, h3, p, f, wgate, g_ple_post, g_ple, g_post_ffn, job=None):
    T = x2.shape[0]
    tm = ROW_TILE
    nt = T // tm

    def body(dx3_ref, x2_ref, z_ref, pe_ref, h3_ref, p_ref, f_ref, wg_ref, gpp_ref, gp_ref, gpf_ref,
             dx2_ref, df_ref, dwg_ref, dwp_ref, dgpp_ref, dgp_ref, dgpf_ref, acc_g, acc_p):
        i = pl.program_id(0)
        first = i == 0
        dx3v = dx3_ref[...]
        gate = _sigmoid(z_ref[...].astype(F32))
        pev = pe_ref[...].astype(F32)
        de, dgpp = _rms_bwd(pev * gate, gpp_ref[...], dx3v)
        dpe = (de * gate).astype(BF16)
        dz = (de * pev * gate * (1.0 - gate)).astype(BF16)
        _acc(acc_p, _dot_tn(p_ref[...].astype(BF16), dpe), first)
        _acc(acc_g, _dot_tn(h3_ref[...], dz), first)
        dh3 = _dot_nt(dz, wg_ref[...])
        dxn, dgp = _rms_bwd(x2_ref[...], gp_ref[...], dh3)
        dx2 = dx3v + dxn
        dx2_ref[...] = dx2
        df, dgpf = _rms_bwd(f_ref[...].astype(F32), gpf_ref[...], dx2)
        df_ref[...] = df.astype(BF16)
        _acc(dgpp_ref, dgpp, first)
        _acc(dgp_ref, dgp, first)
        _acc(dgpf_ref, dgpf, first)

        @pl.when(i == nt - 1)
        def _():
            dwg_ref[...] = acc_g[...].astype(BF16)
            dwp_ref[...] = acc_p[...].astype(BF16)

    return _launch(
        body, name=f"bwd_ple{layer}", grid=(nt,),
        in_specs=[_row_spec(D_MODEL)] * 5 + [_row_spec(PLE_DIM), _row_spec(D_MODEL), _full_spec((D_MODEL, D_MODEL)),
                  _vec_spec(), _vec_spec(), _vec_spec()],
        out_specs=[_row_spec(D_MODEL), _row_spec(D_MODEL), _full_spec((D_MODEL, D_MODEL)), _full_spec((PLE_DIM, D_MODEL)),
                   _vec_spec(), _vec_spec(), _vec_spec()],
        out_shape=[jax.ShapeDtypeStruct((T, D_MODEL), F32), jax.ShapeDtypeStruct((T, D_MODEL), BF16),
                   jax.ShapeDtypeStruct((D_MODEL, D_MODEL), BF16), jax.ShapeDtypeStruct((PLE_DIM, D_MODEL), BF16)]
                  + [jax.ShapeDtypeStruct((1, D_MODEL), F32)] * 3,
        scratch_shapes=[pltpu.VMEM((D_MODEL, D_MODEL), F32), pltpu.VMEM((PLE_DIM, D_MODEL), F32)],
        args=(dx3, x2, z, pe, h3, p, f, wgate, g_ple_post, g_ple, g_post_ffn), vmem=VMEM_BIG, job=job)


def _ple_loss_bwd(layer, x2, h3, p, f, target, wgate, wproj, g_ple_post, g_ple, g_post_ffn, job=None):
    T = x2.shape[0]
    tm = ROW_TILE
    nt = T // tm

    def body(x2_ref, h3_ref, p_ref, f_ref, tgt_ref, wg_ref, wp_ref, gpp_ref, gp_ref, gpf_ref,
             dx2_ref, df_ref, dwg_ref, dwp_ref, dgpp_ref, dgp_ref, dgpf_ref, loss_ref, acc_g, acc_p):
        i = pl.program_id(0)
        first = i == 0
        h3 = h3_ref[...]
        pb = p_ref[...].astype(BF16)
        x2v = x2_ref[...]
        gate = _sigmoid(_dot(h3, wg_ref[...]))
        pev = _dot(pb, wp_ref[...])
        e = pev * gate
        err = x2v + _rms(e, gpp_ref[...]) - tgt_ref[...]
        _acc(loss_ref, 0.5 * jnp.sum(jnp.mean(err * err, axis=-1, keepdims=True), axis=0, keepdims=True), first)
        dx3v = err * (1.0 / D_MODEL)
        de, dgpp = _rms_bwd(e, gpp_ref[...], dx3v)
        dpe = (de * gate).astype(BF16)
        dz = (de * pev * gate * (1.0 - gate)).astype(BF16)
        _acc(acc_p, _dot_tn(pb, dpe), first)
        _acc(acc_g, _dot_tn(h3, dz), first)
        dxn, dgp = _rms_bwd(x2v, gp_ref[...], _dot_nt(dz, wg_ref[...]))
        dx2 = dx3v + dxn
        dx2_ref[...] = dx2
        df, dgpf = _rms_bwd(f_ref[...].astype(F32), gpf_ref[...], dx2)
        df_ref[...] = df.astype(BF16)
        _acc(dgpp_ref, dgpp, first)
        _acc(dgp_ref, dgp, first)
        _acc(dgpf_ref, dgpf, first)

        @pl.when(i == nt - 1)
        def _():
            dwg_ref[...] = acc_g[...].astype(BF16)
            dwp_ref[...] = acc_p[...].astype(BF16)

    return _launch(
        body, name=f"ple_loss_bwd{layer}", grid=(nt,),
        in_specs=[_row_spec(D_MODEL), _row_spec(D_MODEL), _row_spec(PLE_DIM), _row_spec(D_MODEL), _row_spec(D_MODEL),
                  _full_spec((D_MODEL, D_MODEL)), _full_spec((PLE_DIM, D_MODEL)), _vec_spec(), _vec_spec(), _vec_spec()],
        out_specs=[_row_spec(D_MODEL), _row_spec(D_MODEL), _full_spec((D_MODEL, D_MODEL)), _full_spec((PLE_DIM, D_MODEL)),
                   _vec_spec(), _vec_spec(), _vec_spec(), _full_spec((1, 1))],
        out_shape=[jax.ShapeDtypeStruct((T, D_MODEL), F32), jax.ShapeDtypeStruct((T, D_MODEL), BF16),
                   jax.ShapeDtypeStruct((D_MODEL, D_MODEL), BF16), jax.ShapeDtypeStruct((PLE_DIM, D_MODEL), BF16)]
                  + [jax.ShapeDtypeStruct((1, D_MODEL), F32)] * 3 + [jax.ShapeDtypeStruct((1, 1), F32)],
        scratch_shapes=[pltpu.VMEM((D_MODEL, D_MODEL), F32), pltpu.VMEM((PLE_DIM, D_MODEL), F32)],
        args=(x2, h3, p, f, target, wgate, wproj, g_ple_post, g_ple, g_post_ffn), vmem=VMEM_BIG, job=job)


def _bwd_ffn_act(layer, df, gs, us, wgu, wd, job=None):
    T = df.shape[0]
    tm = min(FFN_ROW_TILE, T)
    nt = T // tm
    sub = tm // FFN_SUB_TILES
    last = FF_CHUNKS - 1
    wgu, wd = _column_views(wgu), _column_views(wd)
    n_gu, n_wd = len(wgu), len(wd)
    wd_cols = _column_ranges(wd)

    def body(df_ref, gs_ref, us_ref, *refs):
        wgu_refs, wd_refs = refs[:n_gu], refs[n_gu:n_gu + n_wd]
        dh_ref, dg_ref, du_ref, a_ref, acc_h = refs[n_gu + n_wd:]
        k = pl.program_id(0)
        i = pl.program_id(1)
        rows = pl.ds(pl.multiple_of(i * tm, tm), tm)
        dhs = []
        for s in range(FFN_SUB_TILES):
            r = pl.ds(s * sub, sub)
            g = gs_ref[r, :].astype(F32)
            u = us_ref[r, :].astype(F32)
            sg = _sigmoid(g)
            silu = g * sg
            a_ref[r, :] = (silu * u).astype(BF16)
            da = _add_all([_dot_nt(df_ref[r, c0:c1], w[...]) for (c0, c1), w in zip(wd_cols, wd_refs)])
            dg = (da * u * (sg * (1.0 + g * (1.0 - sg)))).astype(BF16)
            du = (da * silu).astype(BF16)
            dg_ref[r, :] = dg
            du_ref[r, :] = du
            dhs.append(jnp.concatenate([_dot(dg, w[0]) + _dot(du, w[1]) for w in wgu_refs], axis=1))
        dh = jnp.concatenate(dhs, axis=0)

        @pl.when(k == 0)
        def _():
            acc_h[rows, :] = dh

        @pl.when(jnp.logical_and(k > 0, k < last))
        def _():
            acc_h[rows, :] += dh

        @pl.when(k == last)
        def _():
            dh_ref[...] = acc_h[rows, :] + dh

    chunk_rows = pl.BlockSpec((None, tm, FF_BLOCK), lambda k, i: (k, i, 0))
    saved = jax.ShapeDtypeStruct((FF_CHUNKS, T, FF_BLOCK), BF16)
    return _launch(
        body, name=f"bwd_ffn_act{layer}", grid=(FF_CHUNKS, nt),
        in_specs=[pl.BlockSpec((tm, D_MODEL), lambda k, i: (i, 0)), chunk_rows, chunk_rows]
                 + [pl.BlockSpec((None, 2, FF_BLOCK, FFN_WEIGHT_COLS), lambda k, i, b=b: (k, 0, 0, b)) for _, b in wgu]
                 + [pl.BlockSpec((FF_BLOCK, FFN_WEIGHT_COLS), lambda k, i, b=b: (k, b)) for _, b in wd],
        out_specs=[pl.BlockSpec((tm, D_MODEL), lambda k, i: (jnp.where(k == last, i, 0), 0)),
                   chunk_rows, chunk_rows, chunk_rows],
        out_shape=[jax.ShapeDtypeStruct((T, D_MODEL), F32), saved, saved, saved],
        scratch_shapes=[pltpu.VMEM((T, D_MODEL), F32)],
        args=(df, gs, us, *[w for w, _ in wgu], *[w for w, _ in wd]), vmem=VMEM_BIG, job=job)


def _bwd_ffn_dw(layer, q, parts, h2, df, dg, du, a, job=None):
    T = h2.shape[0]
    width = D_MODEL // parts

    def body(h_ref, df_ref, dg_ref, du_ref, a_ref, dgu_ref, dwd_ref):
        h = h_ref[...]
        dgu_ref[0] = _dot_tn(dg_ref[...], h).astype(BF16)
        dgu_ref[1] = _dot_tn(du_ref[...], h).astype(BF16)
        dwd_ref[...] = _dot_tn(a_ref[...], df_ref[...]).astype(BF16)

    cols = pl.BlockSpec((T, width), lambda k: (0, q))
    chunk = pl.BlockSpec((None, T, FF_BLOCK), lambda k: (k, 0, 0))
    return _launch(
        body, name=f"bwd_ffn_dw{layer}_{q}", grid=(FF_CHUNKS,),
        in_specs=[cols, cols, chunk, chunk, chunk],
        out_specs=[pl.BlockSpec((None, 2, FF_BLOCK, width), lambda k: (k, 0, 0, 0)),
                   pl.BlockSpec((FF_BLOCK, width), lambda k: (k, 0))],
        out_shape=[jax.ShapeDtypeStruct((FF_CHUNKS, 2, FF_BLOCK, width), BF16),
                   jax.ShapeDtypeStruct((D_FF, width), BF16)],
        args=(h2, df, dg, du, a), vmem=VMEM_BIG, job=job)


def _bwd_attn_out(dx2, dh2, x1, y, attn, wo, g_ffn, g_post, job=None):
    T = x1.shape[0]
    nt = T // ROW_TILE

    def body(dx2_ref, dh2_ref, x1_ref, y_ref, a_ref, wo_ref, gffn_ref, gpost_ref,
             dx1_ref, da_ref, dwo_ref, dgf_ref, dgp_ref, acc):
        i = pl.program_id(0)
        first = i == 0
        dxn, dgf = _rms_bwd(x1_ref[...], gffn_ref[...], dh2_ref[...])
        dx1 = dx2_ref[...] + dxn
        dx1_ref[...] = dx1
        dy, dgp = _rms_bwd(y_ref[...].astype(F32), gpost_ref[...], dx1)
        dyb = dy.astype(BF16)
        da_ref[...] = _dot_nt(dyb, wo_ref[...]).astype(BF16)
        _acc(acc, _dot_tn(a_ref[...], dyb), first)
        _acc(dgf_ref, dgf, first)
        _acc(dgp_ref, dgp, first)

        @pl.when(i == nt - 1)
        def _():
            dwo_ref[...] = acc[...].astype(BF16)

    return _launch(
        body, name="bwd_attn_out", grid=(nt,),
        in_specs=[_row_spec(D_MODEL)] * 5 + [_full_spec((D_MODEL, D_MODEL)), _vec_spec(), _vec_spec()],
        out_specs=[_row_spec(D_MODEL), _row_spec(D_MODEL), _full_spec((D_MODEL, D_MODEL)), _vec_spec(), _vec_spec()],
        out_shape=[jax.ShapeDtypeStruct((T, D_MODEL), F32), jax.ShapeDtypeStruct((T, D_MODEL), BF16),
                   jax.ShapeDtypeStruct((D_MODEL, D_MODEL), BF16)] + [jax.ShapeDtypeStruct((1, D_MODEL), F32)] * 2,
        scratch_shapes=[pltpu.VMEM((D_MODEL, D_MODEL), F32)],
        args=(dx2, dh2, x1, y, attn, wo, g_ffn, g_post), job=job)


def _bwd_attention(q, dattn, kpad, vpad, sinks, job=None):
    T = q.shape[0]
    nb = T // ATT_BLOCK

    def body(q_ref, do_ref, k_ref, v_ref, sink_ref, dq_ref, dk_ref, dv_ref, ds_ref, s_scr, dp_scr, p_scr, dsb_scr,
             rel_scr, off_scr):
        n = pl.program_id(0)
        _att_mask(n, rel_scr, off_scr)

        @pl.when(n == 0)
        def _():
            dk_ref[...] = jnp.zeros_like(dk_ref)
            dv_ref[...] = jnp.zeros_like(dv_ref)
            ds_ref[...] = jnp.zeros_like(ds_ref)

        start = pl.multiple_of(n * ATT_BLOCK, ATT_BLOCK)
        win = pl.ds(start, 2 * ATT_BLOCK)
        kw = k_ref[win, :]
        vw = v_ref[win, :]
        lane = lax.broadcasted_iota(jnp.int32, (1, ATT_BLOCK), 1)
        dsink = jnp.zeros((1, ATT_BLOCK), F32)
        dqs, dks, dvs = [], [], []
        for kh in range(N_KV_HEADS):
            kk = kw[:, kh * HEAD_DIM:(kh + 1) * HEAD_DIM]
            vv = vw[:, kh * HEAD_DIM:(kh + 1) * HEAD_DIM]
            qs = _stack_heads(q_ref, kh)
            dos = _stack_heads(do_ref, kh)
            s_scr[...] = _dot_nt(qs, kk)
            dp_scr[...] = _dot_nt(dos, vv)
            for g in range(GQA_GROUP):
                h = kh * GQA_GROUP + g
                dsink_h = jnp.zeros((1, 1), F32)
                for row0 in range(0, ATT_BLOCK, ATT_SUB):
                    rows, sub = pl.ds(g * ATT_BLOCK + row0, ATT_SUB), pl.ds(row0, ATT_SUB)
                    pr, ps = _att_probs(s_scr[rows, :], rel_scr[sub, :], off_scr[sub, :], _alibi_slope(h),
                                        sink_ref[0, h])
                    dp = dp_scr[rows, :]
                    delta = jnp.sum(pr * dp, axis=-1, keepdims=True)
                    dsb_scr[rows, :] = (pr * (dp - delta) * ATT_SCALE).astype(BF16)
                    p_scr[rows, :] = pr.astype(BF16)
                    dsink_h = dsink_h - jnp.sum(ps * delta, axis=0, keepdims=True)
                dsink = dsink + jnp.where(lane == h, dsink_h, 0.0)
            dsb = dsb_scr[...]
            dqs += _unstack_heads(_dot(dsb, kk))
            dks.append(_dot_tn(dsb, qs))
            dvs.append(_dot_tn(p_scr[...], dos))
        dq_ref[...] = jnp.concatenate(dqs, axis=1).astype(BF16)
        dk_ref[win, :] += jnp.concatenate(dks, axis=1)
        dv_ref[win, :] += jnp.concatenate(dvs, axis=1)
        ds_ref[...] += dsink

    return _launch(
        body, name="bwd_attention", grid=(nb,),
        in_specs=[_row_spec(D_MODEL, ATT_BLOCK), _row_spec(D_MODEL, ATT_BLOCK), _full_spec((T + ATT_BLOCK, KV_DIM)),
                  _full_spec((T + ATT_BLOCK, KV_DIM)), pl.BlockSpec(memory_space=pltpu.SMEM)],
        out_specs=[_row_spec(D_MODEL, ATT_BLOCK), _full_spec((T + ATT_BLOCK, KV_DIM)), _full_spec((T + ATT_BLOCK, KV_DIM)),
                   _full_spec((1, ATT_BLOCK))],
        out_shape=[jax.ShapeDtypeStruct((T, D_MODEL), BF16), jax.ShapeDtypeStruct((T + ATT_BLOCK, KV_DIM), F32),
                   jax.ShapeDtypeStruct((T + ATT_BLOCK, KV_DIM), F32), jax.ShapeDtypeStruct((1, ATT_BLOCK), F32)],
        scratch_shapes=[pltpu.VMEM((ATT_GROUP_ROWS, 2 * ATT_BLOCK), F32)] * 2
                       + [pltpu.VMEM((ATT_GROUP_ROWS, 2 * ATT_BLOCK), BF16)] * 2
                       + [pltpu.VMEM((ATT_BLOCK, 2 * ATT_BLOCK), F32)] * 2,
        args=(q, dattn, kpad, vpad, sinks), vmem=VMEM_BIG, job=job)


def _bwd_qkv(dxres, dq, dkv, x3, h1, hk, wq, wkv, g_mix, g_kv, job=None):
    T = x3.shape[0]
    nt = T // ROW_TILE

    def body(dxr_ref, dq_ref, dkv_ref, x_ref, h1_ref, hk_ref, wq_ref, wkv_ref, gmix_ref, gkv_ref,
             dx_ref, dwq_ref, dwkv_ref, dgm_ref, dgk_ref, acc_q, acc_kv):
        i = pl.program_id(0)
        first = i == 0
        dqv = dq_ref[...]
        dkvv = dkv_ref[...]
        xv = x_ref[...]
        d1, dgm = _rms_bwd(xv, gmix_ref[...], _dot_nt(dqv, wq_ref[...]))
        d2, dgk = _rms_bwd(xv, gkv_ref[...], _dot_nt(dkvv, wkv_ref[...]))
        dx_ref[...] = dxr_ref[...] + d1 + d2
        _acc(acc_q, _dot_tn(h1_ref[...], dqv), first)
        _acc(acc_kv, _dot_tn(hk_ref[...], dkvv), first)
        _acc(dgm_ref, dgm, first)
        _acc(dgk_ref, dgk, first)

        @pl.when(i == nt - 1)
        def _():
            dwq_ref[...] = acc_q[...].astype(BF16)
            dwkv_ref[...] = acc_kv[...].astype(BF16)

    return _launch(
        body, name="bwd_qkv", grid=(nt,),
        in_specs=[_row_spec(D_MODEL), _row_spec(D_MODEL), _row_spec(2 * KV_DIM), _row_spec(D_MODEL), _row_spec(D_MODEL),
                  _row_spec(D_MODEL), _full_spec((D_MODEL, D_MODEL)), _full_spec((D_MODEL, 2 * KV_DIM)), _vec_spec(),
                  _vec_spec()],
        out_specs=[_row_spec(D_MODEL), _full_spec((D_MODEL, D_MODEL)), _full_spec((D_MODEL, 2 * KV_DIM)), _vec_spec(),
                   _vec_spec()],
        out_shape=[jax.ShapeDtypeStruct((T, D_MODEL), F32), jax.ShapeDtypeStruct((D_MODEL, D_MODEL), BF16),
                   jax.ShapeDtypeStruct((D_MODEL, 2 * KV_DIM), BF16)] + [jax.ShapeDtypeStruct((1, D_MODEL), F32)] * 2,
        scratch_shapes=[pltpu.VMEM((D_MODEL, D_MODEL), F32), pltpu.VMEM((D_MODEL, 2 * KV_DIM), F32)],
        args=(dxres, dq, dkv, x3, h1, hk, wq, wkv, g_mix, g_kv), job=job)


def _bwd_pool_mixer(dx2, dh2, x1, x, yraw, d, wp, scale, g_ffn, g_post, g_pre, job=None):
    T = x.shape[0]
    tm = ROW_TILE
    nt = T // tm

    def body(dx2_ref, dh2_ref, x1_ref, x_ref, yraw_ref, d_ref, wp_ref, sc_ref, gffn_ref, gpost_ref, gpre_ref,
             dx_ref, dwp_ref, dsc_ref, dgf_ref, dgp_ref, dgm_ref, carry, acc):
        i = pl.program_id(0)
        first = i == 0
        tile = nt - 1 - i

        @pl.when(first)
        def _():
            carry[...] = jnp.zeros_like(carry)

        dxn, dgf = _rms_bwd(x1_ref[...], gffn_ref[...], dh2_ref[...])
        dx1 = dx2_ref[...] + dxn
        yraw = yraw_ref[...].astype(F32)
        sc = sc_ref[...]
        dy, dgp = _rms_bwd(yraw * sc, gpost_ref[...], dx1)
        dsc = jnp.sum(dy * yraw, axis=0, keepdims=True)
        dyb = (dy * sc).astype(BF16)
        dv = d_ref[...]
        dds = []
        for g in range(N_POOL_GROUPS):
            cols = slice(g * POOL_GROUP, (g + 1) * POOL_GROUP)
            dds.append(_dot_nt(dyb[:, cols], wp_ref[g]))
            _acc(acc.at[g], _dot_tn(dv[:, cols], dyb[:, cols]), first)
        dd = jnp.concatenate(dds, axis=1)
        e = dd / _pool_counts(tile * tm, tm)
        ext = jnp.concatenate([e, carry[...]], axis=0)
        carry[...] = e[:POOL_HALO, :]
        sums = _window_sums(ext, lambda k: tm + POOL_HALO - k)[:tm, :]
        dxm, dgm = _rms_bwd(x_ref[...], gpre_ref[...], sums - dd)
        dx_ref[...] = dx1 + dxm
        _acc(dsc_ref, dsc, first)
        _acc(dgf_ref, dgf, first)
        _acc(dgp_ref, dgp, first)
        _acc(dgm_ref, dgm, first)

        @pl.when(i == nt - 1)
        def _():
            dwp_ref[...] = acc[...].astype(BF16)

    rev = pl.BlockSpec((tm, D_MODEL), lambda i: (nt - 1 - i, 0))
    return _launch(
        body, name="bwd_pool_mixer", grid=(nt,),
        in_specs=[rev] * 6 + [_full_spec((N_POOL_GROUPS, POOL_GROUP, POOL_GROUP))] + [_vec_spec()] * 4,
        out_specs=[rev, _full_spec((N_POOL_GROUPS, POOL_GROUP, POOL_GROUP))] + [_vec_spec()] * 4,
        out_shape=[jax.ShapeDtypeStruct((T, D_MODEL), F32),
                   jax.ShapeDtypeStruct((N_POOL_GROUPS, POOL_GROUP, POOL_GROUP), BF16)]
                  + [jax.ShapeDtypeStruct((1, D_MODEL), F32)] * 4,
        scratch_shapes=[pltpu.VMEM((POOL_HALO, D_MODEL), F32), pltpu.VMEM((N_POOL_GROUPS, POOL_GROUP, POOL_GROUP), F32)],
        args=(dx2, dh2, x1, x, yraw, d, wp, scale, g_ffn, g_post, g_pre), job=job)


def _my_place():
    return lax.axis_index("x"), lax.axis_index("y"), lax.axis_index("c")


def _dev_index(px, py, pc):
    return 4 * px + 2 * py + pc


def _peer_by_relation(r):
    x, y, c = _my_place()
    return (x ^ ((r >> 2) & 1), y ^ ((r >> 1) & 1), c ^ (r & 1))


def _slot_pool(ref, j):
    return ref.at[:, pl.ds(pl.multiple_of(j * 32, 32), 32), :]


def _slot_scale(ref, j):
    return ref.at[:, pl.ds(pl.multiple_of(j * 128, 128), 128)]


def _slot_rows128(ref, j):
    return ref.at[pl.ds(pl.multiple_of(j * 128, 128), 128), :]


def _slot_gu(ref, j):
    return ref.at[j % FF_CHUNKS, j // FF_CHUNKS]


def _slot_wd(ref, j):
    return ref.at[pl.ds(pl.multiple_of(j * WD_ROWS, 16), WD_ROWS), :]


def _slot_cols128(ref, j):
    return ref.at[:, pl.ds(pl.multiple_of(j * 128, 128), 128)]


_GATHERED = {
    "pool": ((N_POOL_GROUPS, POOL_GROUP, POOL_GROUP), BF16, _slot_pool),
    "scale": ((1, D_MODEL), F32, _slot_scale),
    "kv": ((D_MODEL, 2 * KV_DIM), BF16, _slot_rows128),
    "q": ((D_MODEL, D_MODEL), BF16, _slot_rows128),
    "o": ((D_MODEL, D_MODEL), BF16, _slot_rows128),
    "gu": ((FF_CHUNKS, 2, FF_BLOCK, D_MODEL), BF16, _slot_gu),
    "wd": ((D_FF, D_MODEL), BF16, _slot_wd),
    "guh": ((FF_CHUNKS, 2, FF_BLOCK, D_MODEL // 2), BF16, _slot_gu),
    "wdh": ((D_FF, D_MODEL // 2), BF16, _slot_wd),
    "gate": ((D_MODEL, D_MODEL), BF16, _slot_rows128),
    "proj": ((PLE_DIM, D_MODEL), BF16, _slot_cols128),
}


def _no_compute():
    pass


class _AllGather:
    peers = ("sibling", "x", "y")

    def __init__(self, names, shards):
        self.kinds = [_GATHERED[n.rstrip("01_")] for n in names]
        entries = [shards[n] if isinstance(shards[n], tuple) else (shards[n], None) for n in names]
        self.args = [array for array, _ in entries]
        self.columns = [columns for _, columns in entries]
        self.out_shape = [jax.ShapeDtypeStruct(shape, dtype) for shape, dtype, _ in self.kinds]
        n = len(names)
        self.scratch = [pltpu.SemaphoreType.DMA((n, 7)), pltpu.SemaphoreType.DMA((n, 7)), pltpu.SemaphoreType.DMA((n,))]

    def _plan(self, srcs, outs, sems):
        send_sems, recv_sems, local_sems = sems
        x, y, c = _my_place()

        def slot(t, dev):
            return self.kinds[t][2](outs[t], _dev_index(*dev))

        def copy(t, k, block, to, src=None):
            return pltpu.make_async_remote_copy(
                src_ref=slot(t, block) if src is None else src, dst_ref=slot(t, block),
                send_sem=send_sems.at[t, k], recv_sem=recv_sems.at[t, k], device_id=to, device_id_type=MESH)

        return types.SimpleNamespace(
            copy=copy, core=c, me=(x, y, c), sibling=(x, y, 1 - c),
            x_chip=(1 - x, y), y_chip=(x, 1 - y), far_chip=(1 - x, 1 - y),
            via=(x ^ (1 - c), y ^ c),
            onto=(x ^ c, y ^ (1 - c)),
            k_via=1 + c, k_onto=2 - c,
            local=[pltpu.make_async_copy(self._shard(srcs, t), slot(t, (x, y, c)), local_sems.at[t])
                   for t in range(len(srcs))])

    def _shard(self, srcs, t):
        if self.columns[t] is None:
            return srcs[t]
        first, end = self.columns[t]
        return srcs[t].at[:, first:end]

    def start(self, srcs, outs, sems):
        p = self._plan(srcs, outs, sems)
        for cp in p.local:
            cp.start()
        for t in range(len(srcs)):
            shard = self._shard(srcs, t)
            p.copy(t, 0, p.me, p.sibling, src=shard).start()
            p.copy(t, 1, p.me, (*p.x_chip, p.core), src=shard).start()
            p.copy(t, 2, p.me, (*p.y_chip, p.core), src=shard).start()

    def mid(self, srcs, outs, sems):
        p = self._plan(srcs, outs, sems)
        for t in range(len(srcs)):
            block = (*p.via, p.core)
            p.copy(t, p.k_via, block, p.me).wait_recv()
            p.copy(t, 3, block, (*p.onto, p.core)).start()
            p.copy(t, 3 + p.k_via, block, p.sibling).start()

    def finish(self, srcs, outs, sems):
        p = self._plan(srcs, outs, sems)
        n = len(srcs)
        for t in range(n):
            block = (*p.onto, p.core)
            p.copy(t, p.k_onto, block, p.me).wait_recv()
            p.copy(t, 3 + p.k_onto, block, p.sibling).start()
        for t in range(n):
            block = (*p.far_chip, p.core)
            p.copy(t, 3, block, p.me).wait_recv()
            p.copy(t, 6, block, p.sibling).start()
        other = 1 - p.core
        for t in range(n):
            p.copy(t, 0, (*p.me[:2], other), p.me).wait_recv()
            for k, chip in ((4, p.x_chip), (5, p.y_chip), (6, p.far_chip)):
                p.copy(t, k, (*chip, other), p.me).wait_recv()
            for k in range(7):
                p.copy(t, k, p.me, p.sibling).wait_send()
        for cp in p.local:
            cp.wait()


def _jobs_only(name, job=None):
    return _launch(_no_compute, name=name, grid=(), in_specs=[], out_specs=[], out_shape=[], args=(), job=job)


def _block_pool(ref, j):
    return ref.at[:, pl.ds(pl.multiple_of(j * 32, 32), 32), :]


def _block_rows128(ref, j):
    return ref.at[pl.ds(pl.multiple_of(j * 128, 128), 128), :]


def _block_gu(ref, j):
    return ref.at[j % FF_CHUNKS, j // FF_CHUNKS]


def _block_wd(ref, j):
    return ref.at[pl.ds(pl.multiple_of(j * WD_ROWS, 16), WD_ROWS), :]


def _block_cols128(ref, j):
    return ref.at[:, pl.ds(pl.multiple_of(j * 128, 128), 128)]


_SCATTERED = {
    "pool": ((N_POOL_GROUPS, 32, POOL_GROUP), _block_pool),
    "kv": ((128, 2 * KV_DIM), _block_rows128),
    "q": ((128, D_MODEL), _block_rows128),
    "o": ((128, D_MODEL), _block_rows128),
    "gu": ((FF_BLOCK, FF_PART), _block_gu),
    "wd": ((WD_ROWS, FF_PART), _block_wd),
    "guA": ((FF_BLOCK, FF_PART), lambda ref, j: _block_gu(ref, j).at[:, :FF_PART]),
    "guB": ((FF_BLOCK, FF_PART), lambda ref, j: _block_gu(ref, j).at[:, FF_PART:]),
    "wdA": ((WD_ROWS, FF_PART), lambda ref, j: _block_wd(ref, j).at[:, :FF_PART]),
    "wdB": ((WD_ROWS, FF_PART), lambda ref, j: _block_wd(ref, j).at[:, FF_PART:]),
    "gate": ((128, D_MODEL), _block_rows128),
    "proj": ((PLE_DIM, 128), _block_cols128),
}


class _SiblingSwap:
    peers = ("sibling",)

    def __init__(self, pieces):
        self.kinds = [_SCATTERED[kind] for kind, _ in pieces]
        self.args = [g for _, g in pieces]
        self.out_shape = [jax.ShapeDtypeStruct((N_CHIPS, *block), BF16) for block, _ in self.kinds]
        n = len(pieces)
        self.scratch = [pltpu.SemaphoreType.DMA((n, N_CHIPS)), pltpu.SemaphoreType.DMA((n, N_CHIPS))]

    def _copies(self, srcs, outs, sems):
        send_sems, recv_sems = sems
        x, y, c = _my_place()
        return [pltpu.make_async_remote_copy(
            src_ref=block(srcs[t], 2 * ch + 1 - c), dst_ref=outs[t].at[ch], send_sem=send_sems.at[t, ch],
            recv_sem=recv_sems.at[t, ch], device_id=(x, y, 1 - c), device_id_type=MESH)
            for t, (_, block) in enumerate(self.kinds) for ch in range(N_CHIPS)]

    def start(self, srcs, outs, sems):
        for cp in self._copies(srcs, outs, sems):
            cp.start()

    def finish(self, srcs, outs, sems):
        for cp in self._copies(srcs, outs, sems):
            cp.wait()


class _ChipScatter:
    N_BUFS = 4
    peers = ("x", "y")

    def __init__(self, pieces):
        self.kinds = [_SCATTERED[kind] for kind, _, _ in pieces]
        self.n = n = len(pieces)
        self.args = [g for _, g, _ in pieces] + [s for _, _, s in pieces]
        self.out_shape = [jax.ShapeDtypeStruct((2, *block), BF16) for block, _ in self.kinds]
        self.scratch = []
        for block, _ in self.kinds:
            self.scratch += [pltpu.VMEM((N_CHIPS, *block), BF16)] * 3 + [pltpu.VMEM((2, *block), BF16)]
        dma = pltpu.SemaphoreType.DMA
        self.scratch += [dma((n, N_CHIPS + 1)), dma((n, 2)), dma((n, 2)), dma((n,)), dma((n,)), dma((n,))]

    def _plan(self, outs, scr):
        n = self.n
        first_send, first_recv, second_send, second_recv, keep_sems = scr[self.N_BUFS * n + 1:]
        x, y, c = _my_place()
        via = (x ^ (1 - c), y ^ c)
        onto = (x ^ c, y ^ (1 - c))
        index = lambda chip: 2 * chip[0] + chip[1]
        first, second, keep = [], [], []
        for t in range(n):
            total, inbox = scr[self.N_BUFS * t + 2], scr[self.N_BUFS * t + 3]
            for k, chip in enumerate((via, (1 - x, 1 - y))):
                first.append(pltpu.make_async_remote_copy(
                    src_ref=total.at[index(chip)], dst_ref=inbox.at[k], send_sem=first_send.at[t, k],
                    recv_sem=first_recv.at[t, k], device_id=(*via, c), device_id_type=MESH))
            second.append(pltpu.make_async_remote_copy(
                src_ref=total.at[index(onto)], dst_ref=outs[t].at[1], send_sem=second_send.at[t],
                recv_sem=second_recv.at[t], device_id=(*onto, c), device_id_type=MESH))
            keep.append(pltpu.make_async_copy(total.at[index((x, y))], outs[t].at[0], keep_sems.at[t]))
        return first, second, keep, index((x, y)), index(onto)

    def start(self, ins, outs, scr):
        n = self.n
        load_sems = scr[self.N_BUFS * n]
        c = lax.axis_index("c")
        loads = []
        for t, (_, block) in enumerate(self.kinds):
            mine, theirs = scr[self.N_BUFS * t], scr[self.N_BUFS * t + 1]
            loads += [pltpu.make_async_copy(block(ins[t], 2 * ch + c), mine.at[ch], load_sems.at[t, ch])
                      for ch in range(N_CHIPS)]
            loads.append(pltpu.make_async_copy(ins[n + t], theirs, load_sems.at[t, N_CHIPS]))
        for cp in loads:
            cp.start()
        for cp in loads:
            cp.wait()
        for t in range(n):
            mine, theirs, total = scr[self.N_BUFS * t:self.N_BUFS * t + 3]
            for ch in range(N_CHIPS):
                total[ch] = (mine[ch].astype(F32) + theirs[ch].astype(F32)).astype(BF16)
        for cp in self._plan(outs, scr)[0]:
            cp.start()

    def mid(self, ins, outs, scr):
        first, second, keep, me, onto = self._plan(outs, scr)
        for cp in first:
            cp.wait_recv()
        for t in range(self.n):
            total, inbox = scr[self.N_BUFS * t + 2], scr[self.N_BUFS * t + 3]
            for k, slot in enumerate((me, onto)):
                total[slot] = (total[slot].astype(F32) + inbox[k].astype(F32)).astype(BF16)
        for cp in second + keep:
            cp.start()

    def finish(self, ins, outs, scr):
        first, second, keep, _, _ = self._plan(outs, scr)
        for cp in first:
            cp.wait_send()
        for cp in second + keep:
            cp.wait()


class _Jobs:
    def __init__(self, *jobs):
        self.jobs = jobs
        together = {p for j in jobs for p in j.peers}
        self.peers = tuple(p for p in _PEER_SETS[0] if p in together)
        self.args = [a for j in jobs for a in j.args]
        self.out_shape = [o for j in jobs for o in j.out_shape]
        self.scratch = [s for j in jobs for s in j.scratch]

    def _split(self, refs, attr):
        at = 0
        for j in self.jobs:
            n = len(getattr(j, attr))
            yield refs[at:at + n]
            at += n

    def _each(self, ins, outs, scr):
        return zip(self.jobs, self._split(ins, "args"), self._split(outs, "out_shape"), self._split(scr, "scratch"))

    def start(self, ins, outs, scr):
        for j, i, o, s in self._each(ins, outs, scr):
            j.start(i, o, s)

    def mid(self, ins, outs, scr):
        for j, i, o, s in self._each(ins, outs, scr):
            if hasattr(j, "mid"):
                j.mid(i, o, s)

    def finish(self, ins, outs, scr):
        for j, i, o, s in self._each(ins, outs, scr):
            j.finish(i, o, s)

    def split_outputs(self, outs):
        return list(self._split(outs, "out_shape"))


def _adamw_math(w, g, m, v):
    m = ADAM_B1 * m + (1.0 - ADAM_B1) * g
    v = ADAM_B2 * v + (1.0 - ADAM_B2) * (g * g)
    m_hat = m / (1.0 - ADAM_B1 ** ADAM_STEP)
    v_hat = v / (1.0 - ADAM_B2 ** ADAM_STEP)
    delta = -ADAM_LR * (m_hat / (jnp.sqrt(v_hat) + ADAM_EPS) + ADAM_WD * w)
    return delta, m, v


def _adamw(name, w, m, v, landings, n_col_blocks=1, job=None):
    n_slots, r, c = landings[0].shape
    grid = (w.shape[0] // r, n_col_blocks)

    def body(w_ref, m_ref, v_ref, *rest):
        l_refs, (g_ref, d_ref, nm_ref, nv_ref) = rest[:len(landings)], rest[len(landings):]
        step = pl.program_id(0) * n_col_blocks + pl.program_id(1)
        for idx, l_ref in enumerate(l_refs):
            @pl.when(step == idx)
            def _(l_ref=l_ref):
                g = l_ref[0].astype(F32)
                for s in range(1, n_slots):
                    g = g + l_ref[s].astype(F32)
                g_ref[...] = g
                d_ref[...], nm_ref[...], nv_ref[...] = _adamw_math(w_ref[...], g, m_ref[...], v_ref[...])

    spec = pl.BlockSpec((r, c), lambda a, b: (a, b))
    return _launch(
        body, name=f"adamw_{name}", grid=grid,
        in_specs=[spec, spec, spec] + [_full_spec((n_slots, r, c))] * len(landings),
        out_specs=[spec] * 4, out_shape=[jax.ShapeDtypeStruct(w.shape, F32)] * 4,
        args=(w, m, v, *landings), vmem=VMEM_BIG, job=job)


_SMALL = (("pre_mix_g", SV_PRE_MIX, 2), ("post_mix_g", SV_POST_MIX, 2), ("pre_ffn_g", SV_PRE_FFN, 2),
          ("post_ffn_g", SV_POST_FFN, 2), ("ple_g", SV_PLE, 2), ("ple_post_g", SV_PLE_POST, 2), ("kv_g", SV_KV, 1),
          ("pool_scale", SV_POOL_SCALE, 1), ("sinks", SV_SINKS, 1))


def _small_all_reduce(part):
    def body(part_ref, total_ref, buf, send_sems, recv_sems):
        x, y, c = _my_place()
        me = _dev_index(x, y, c)
        buf[me] = part_ref[...]
        copies = [pltpu.make_async_remote_copy(
            src_ref=buf.at[me], dst_ref=buf.at[me], send_sem=send_sems.at[r - 1], recv_sem=recv_sems.at[r - 1],
            device_id=_peer_by_relation(r), device_id_type=MESH) for r in range(1, N_DEV)]
        for cp in copies:
            cp.start()
        for cp in copies:
            cp.wait()
        g = buf[0]
        for s in range(1, N_DEV):
            g = g + buf[s]
        total_ref[...] = g

    slab = jax.ShapeDtypeStruct((SV_ROWS, D_MODEL), F32)
    (total,), _ = _launch(
        body, name="small_all_reduce", grid=(1,), in_specs=[_full_spec(slab.shape)], out_specs=[_full_spec(slab.shape)],
        out_shape=[slab],
        scratch_shapes=[pltpu.VMEM((N_DEV, SV_ROWS, D_MODEL), F32), pltpu.SemaphoreType.DMA((N_DEV - 1,)),
                        pltpu.SemaphoreType.DMA((N_DEV - 1,))],
        args=(part,))
    return total


def _small_adamw(total, params):
    flat = [a for name, _, _ in _SMALL for a in params[name]]
    n_in = 1 + len(flat)

    def body(*refs):
        total, wmv = refs[0], refs[1:n_in]
        loss_ref, outs = refs[n_in], refs[n_in + 1:]
        me = _dev_index(*_my_place())
        loss_ref[...] = total[SV_LOSS:SV_LOSS + 1, 0:1]
        for idx, (name, row, n_rows) in enumerate(_SMALL):
            w_ref, m_ref, v_ref = wmv[3 * idx:3 * idx + 3]
            g_ref, d_ref, nm_ref, nv_ref = outs[4 * idx:4 * idx + 4]
            if name == "pool_scale":
                g = total[row:row + 1, pl.ds(pl.multiple_of(me * 128, 128), 128)]
            else:
                g = total[row:row + n_rows, 0:w_ref.shape[1]]
            g_ref[...] = g
            d_ref[...], nm_ref[...], nv_ref[...] = _adamw_math(w_ref[...], g, m_ref[...], v_ref[...])

    out_shape = [jax.ShapeDtypeStruct((1, 1), F32)]
    for name, _, _ in _SMALL:
        out_shape += [jax.ShapeDtypeStruct(params[name][0].shape, F32)] * 4
    res, _ = _launch(
        body, name="small_adamw", grid=(1,),
        in_specs=[_full_spec(a.shape) for a in (total, *flat)], out_specs=[_full_spec(s.shape) for s in out_shape],
        out_shape=out_shape, args=(total, *flat))
    return res[0], {name: res[1 + 4 * idx:5 + 4 * idx] for idx, (name, _, _) in enumerate(_SMALL)}


def _local_step(x, p, tgt, gains, sinks, shards, weights):
    row = lambda first_row, layer: _Gain(gains, first_row + layer)
    gather = lambda *names: _AllGather(names, shards)
    g_pre_mix, g_post_mix, g_pre_ffn, g_post_ffn = SV_PRE_MIX, SV_POST_MIX, SV_PRE_FFN, SV_POST_FFN
    g_ple, g_ple_post, g_kv = SV_PLE, SV_PLE_POST, _Gain(gains, SV_KV)

    (dpool,), (wp, scale, wgu0, wd0) = _fwd_pool(x, row(g_pre_mix, 0), job=gather("pool", "scale", "gu0", "wd0"))
    wgu0, wd0 = [wgu0], [wd0]
    (x1_0, h2_0, yraw), _ = _fwd_pool_mixer(x, dpool, wp, scale, row(g_post_mix, 0), row(g_pre_ffn, 0))
    (gs0, us0, f0, x2_0, h3_0), (wgate0, wproj0, wkv, wq, wgu1_a) = _fwd_ffn(
        0, h2_0, x1_0, wgu0, wd0, row(g_post_ffn, 0), row(g_ple, 0),
        job=gather("gate0", "proj0", "kv", "q", "guh1_0"))
    (x3_0, z0, pe0, hk, h1, q, kv), (wo, wd1_a) = _fwd_ple_qkv(
        x2_0, h3_0, p[0], wgate0, wproj0, row(g_ple_post, 0), g_kv, row(g_pre_mix, 1), wkv, wq,
        job=gather("o", "wdh1_0"))
    front = ((ATT_BLOCK, 0), (0, 0))
    kpad = jnp.pad(kv[:, :KV_DIM], front)
    vpad = jnp.pad(kv[:, KV_DIM:], front)
    (attn,), (wgu1_b,) = _fwd_attention(q, kpad, vpad, sinks, job=gather("guh1_1"))
    (y1, x1_1, h2_1), (wd1_b,) = _fwd_attn_out(attn, x3_0, wo, row(g_post_mix, 1), row(g_pre_ffn, 1),
                                               job=gather("wdh1_1"))
    wgu1, wd1 = [wgu1_a, wgu1_b], [wd1_a, wd1_b]
    (gs1, us1, f1, x2_1, h3_1), (wgate1, wproj1) = _fwd_ffn(
        1, h2_1, x1_1, wgu1, wd1, row(g_post_ffn, 1), row(g_ple, 1), job=gather("gate1", "proj1"))

    produced, swapped, landed = {}, {}, {}

    def kind_of(name):
        return name.rstrip("0123_")

    def carry(swap=(), spread=()):
        jobs = []
        if swap:
            jobs.append(_SiblingSwap([(kind_of(n), produced[n]) for n in swap]))
        if spread:
            jobs.append(_ChipScatter([(kind_of(n), produced[n], swapped[n]) for n in spread]))
        return _Jobs(*jobs)

    def carried(jobs, outs, swap=(), spread=()):
        parts = jobs.split_outputs(outs)
        if swap:
            swapped.update(zip(swap, parts[0]))
        if spread:
            landed.update(zip(spread, parts[-1]))

    def hosted(call, *args, swap=(), spread=()):
        jobs = carry(swap, spread)
        outs, job_outs = call(*args, job=jobs)
        carried(jobs, job_outs, swap, spread)
        return outs

    ffn_q = lambda layer, qtr: (f"gu{layer}_{qtr}", f"wd{layer}_{qtr}")

    dx2_1, df1, produced["gate1"], produced["proj1"], dg_ple_post1, dg_ple1, dg_post_ffn1, loss = hosted(
        _ple_loss_bwd, 1, x2_1, h3_1, p[1], f1, tgt, wgate1, wproj1, row(g_ple_post, 1), row(g_ple, 1),
        row(g_post_ffn, 1))
    dh2_1, dg1, du1, a1 = hosted(_bwd_ffn_act, 1, df1, gs1, us1, wgu1, wd1, swap=("gate1", "proj1"))
    dgu1, dwd1 = hosted(_bwd_ffn_dw, 1, 0, 1, h2_1, df1, dg1, du1, a1, spread=("gate1", "proj1"))
    produced.update(guA1=dgu1, guB1=dgu1, wdA1=dwd1, wdB1=dwd1)
    dx1_1, dattn, produced["o"], dg_pre_ffn1, dg_post_mix1 = hosted(
        _bwd_attn_out, dx2_1, dh2_1, x1_1, y1, attn, wo, row(g_pre_ffn, 1), row(g_post_mix, 1),
        swap=("guA1", "wdA1", "guB1", "wdB1"))
    dq, dkpad, dvpad, dsinks = hosted(_bwd_attention, q, dattn, kpad, vpad, sinks, spread=("guA1", "wdA1"))
    dkv = jnp.concatenate([dkpad[ATT_BLOCK:], dvpad[ATT_BLOCK:]], axis=1).astype(BF16)
    dx3_0, produced["q"], produced["kv"], dg_pre_mix1, dg_kv = hosted(
        _bwd_qkv, dx1_1, dq, dkv, x3_0, h1, hk, wq, wkv, row(g_pre_mix, 1), g_kv, swap=("o",), spread=("wdB1",))
    dx2_0, df0, produced["gate0"], produced["proj0"], dg_ple_post0, dg_ple0, dg_post_ffn0 = hosted(
        _bwd_ple, 0, dx3_0, x2_0, z0, pe0, h3_0, p[0], f0, wgate0, row(g_ple_post, 0), row(g_ple, 0),
        row(g_post_ffn, 0), swap=("q", "kv"), spread=("guB1",))
    for half, letter in enumerate("AB"):
        landed[f"gu1_{half}"], landed[f"wd1_{half}"] = landed[f"gu{letter}1"], landed[f"wd{letter}1"]
    dh2_0, dg0, du0, a0 = hosted(_bwd_ffn_act, 0, df0, gs0, us0, wgu0, wd0,
                                 swap=("gate0", "proj0"), spread=("o", "q", "kv"))
    part_hosts = [dict(spread=("gate0", "proj0")), dict(swap=ffn_q(0, 0))]
    for part in range(FF_PARTS):
        produced[f"gu0_{part}"], produced[f"wd0_{part}"] = hosted(
            _bwd_ffn_dw, 0, part, FF_PARTS, h2_0, df0, dg0, du0, a0, **part_hosts[part])
    grad_x, produced["pool"], dscale, dg_pre_ffn0, dg_post_mix0, dg_pre_mix0 = hosted(
        _bwd_pool_mixer, dx2_0, dh2_0, x1_0, x, yraw, dpool, wp, scale, row(g_pre_ffn, 0), row(g_post_mix, 0),
        row(g_pre_mix, 0), swap=ffn_q(0, 1), spread=ffn_q(0, 0))

    def update(name, n_col_blocks=1, pieces=None, swap=(), spread=()):
        w, m, v = weights[name]
        rows = w.size // w.shape[-1]
        flat = [landed[n].reshape(landed[n].shape[0], -1, landed[n].shape[-1])
                for n in (pieces or [kind_short[name]])]
        outs = hosted(_adamw, name, w.reshape(rows, -1), m.reshape(rows, -1), v.reshape(rows, -1), flat,
                      n_col_blocks, swap=swap, spread=spread)
        return [o.reshape(w.shape) for o in outs]

    kind_short = {"w_q": "q", "w_kv": "kv", "w_o": "o", "pool_w": "pool"}
    upd = {}
    hosted(_jobs_only, "scatter_tail0", swap=("pool",), spread=ffn_q(0, 1))
    hosted(_jobs_only, "scatter_tail1", spread=("pool",))
    upd["w_ple_gate"] = update("w_ple_gate", pieces=("gate0", "gate1"))
    upd["w_ple_proj"] = update("w_ple_proj", pieces=("proj0", "proj1"))
    for name in ("w_q", "w_kv", "w_o", "pool_w"):
        upd[name] = update(name)
    upd["w_gu"] = update("w_gu", FF_PARTS,
                         pieces=[f"gu{layer}_{qtr}" for layer in range(2) for qtr in range(FF_PARTS)])
    upd["w_gu"] = [jnp.swapaxes(a, 1, 2) for a in upd["w_gu"]]
    upd["w_down"] = update("w_down", FF_PARTS,
                           pieces=[f"wd{layer}_{qtr}" for layer in range(2) for qtr in range(FF_PARTS)])

    lanes = lambda a: jnp.pad(a, ((0, 0), (0, D_MODEL - a.shape[1])))
    small = jnp.concatenate([
        dg_pre_mix0, dg_pre_mix1, dg_post_mix0, dg_post_mix1, dg_pre_ffn0, dg_pre_ffn1, dg_post_ffn0, dg_post_ffn1,
        dg_ple0, dg_ple1, dg_ple_post0, dg_ple_post1, dg_kv, dscale, lanes(dsinks[:, :N_HEADS]), lanes(loss)], axis=0)
    return grad_x, upd, small


def kernel(x, p, pre_mix_g, post_mix_g, pre_ffn_g, post_ffn_g, pool_w, pool_scale, kv_g, w_kv, w_q, sinks, w_o, w_gu, w_down, ple_g, w_ple_gate, w_ple_proj, ple_post_g, loss_target, m_pre_mix_g, m_post_mix_g, m_pre_ffn_g, m_post_ffn_g, m_pool_w, m_pool_scale, m_kv_g, m_w_kv, m_w_q, m_sinks, m_w_o, m_w_gu, m_w_down, m_ple_g, m_w_ple_gate, m_w_ple_proj, m_ple_post_g, v_pre_mix_g, v_post_mix_g, v_pre_ffn_g, v_post_ffn_g, v_pool_w, v_pool_scale, v_kv_g, v_w_kv, v_w_q, v_sinks, v_w_o, v_w_gu, v_w_down, v_ple_g, v_w_ple_gate, v_w_ple_proj, v_ple_post_g):
    shards = {"pool": pool_w[0].astype(BF16), "scale": pool_scale, "kv": w_kv.astype(BF16),
              "q": w_q[0].astype(BF16), "o": w_o[0].astype(BF16)}
    for layer in range(2):
        shards[f"gu{layer}"] = w_gu[layer].T.astype(BF16)
        shards[f"wd{layer}"] = w_down[layer].astype(BF16)
        for half in range(2):
            cols = (half * D_MODEL // 2, (half + 1) * D_MODEL // 2)
            shards[f"guh{layer}_{half}"] = (shards[f"gu{layer}"], cols)
            shards[f"wdh{layer}_{half}"] = (shards[f"wd{layer}"], cols)
        shards[f"gate{layer}"] = w_ple_gate[layer].astype(BF16)
        shards[f"proj{layer}"] = w_ple_proj[layer].astype(BF16)
    gains = jnp.concatenate([pre_mix_g, post_mix_g, pre_ffn_g, post_ffn_g, ple_g, ple_post_g, kv_g[None, :]],
                            axis=0).reshape(-1, 1, D_MODEL)
    weights = {"pool_w": (pool_w, m_pool_w, v_pool_w), "w_kv": (w_kv, m_w_kv, v_w_kv), "w_q": (w_q, m_w_q, v_w_q),
               "w_o": (w_o, m_w_o, v_w_o), "w_down": (w_down, m_w_down, v_w_down),
               "w_gu": tuple(jnp.swapaxes(a, 1, 2) for a in (w_gu, m_w_gu, v_w_gu)),
               "w_ple_gate": (w_ple_gate, m_w_ple_gate, v_w_ple_gate),
               "w_ple_proj": (w_ple_proj, m_w_ple_proj, v_w_ple_proj)}
    grad_x, upd, small = _local_step(x[0], p[:, 0], loss_target[0], gains, sinks, shards, weights)

    small_params = {
        "pre_mix_g": (pre_mix_g, m_pre_mix_g, v_pre_mix_g), "post_mix_g": (post_mix_g, m_post_mix_g, v_post_mix_g),
        "pre_ffn_g": (pre_ffn_g, m_pre_ffn_g, v_pre_ffn_g), "post_ffn_g": (post_ffn_g, m_post_ffn_g, v_post_ffn_g),
        "ple_g": (ple_g, m_ple_g, v_ple_g), "ple_post_g": (ple_post_g, m_ple_post_g, v_ple_post_g),
        "kv_g": (kv_g[None, :], m_kv_g[None, :], v_kv_g[None, :]),
        "pool_scale": (pool_scale, m_pool_scale, v_pool_scale), "sinks": (sinks, m_sinks, v_sinks)}
    loss, small_upd = _small_adamw(_small_all_reduce(small), small_params)
    small_upd["kv_g"] = [a[0] for a in small_upd["kv_g"]]
    upd.update(small_upd)

    names = ["pre_mix_g", "post_mix_g", "pre_ffn_g", "post_ffn_g", "pool_w", "pool_scale", "kv_g", "w_kv", "w_q",
             "sinks", "w_o", "w_gu", "w_down", "ple_g", "w_ple_gate", "w_ple_proj", "ple_post_g"]
    outs = [loss[0, 0], grad_x[None]]
    for kind in range(4):
        outs += [upd[n][kind] for n in names]
    return tuple(outs)
```

```python
import functools
import types

import jax
import jax.numpy as jnp
from jax import lax
from jax.experimental import pallas as pl
from jax.experimental.pallas import tpu as pltpu

F32 = jnp.float32
BF16 = jnp.bfloat16

N_DEV = 8
D_MODEL = 1024
N_POOL_GROUPS = 4
POOL_GROUP = 256
POOL_HALO = 16
HEAD_DIM = 64
N_HEADS = 16
N_KV_HEADS = 4
GQA_GROUP = 4
KV_DIM = N_KV_HEADS * HEAD_DIM
ATT_BLOCK = 128
D_FF = 2816
FF_CHUNKS = 4
FF_BLOCK = D_FF // FF_CHUNKS
WD_ROWS = D_FF // N_DEV
FF_PARTS = 2
FF_PART = D_MODEL // FF_PARTS
N_CHIPS = 4
PLE_DIM = 256
EPS = 1e-6
NEG_INF = -1e30
ATT_SCALE = HEAD_DIM ** -0.5

ADAM_LR = 0.001
ADAM_B1 = 0.9
ADAM_B2 = 0.999
ADAM_EPS = 1e-08
ADAM_WD = 0.01
ADAM_STEP = 10

ROW_TILE = 512
FFN_ROW_TILE = 512
FFN_WEIGHT_COLS = 512
FFN_SUB_TILES = 1
VMEM_BIG = 60 * 1024 * 1024
VMEM_MID = 56 * 1024 * 1024
HBM_PIN_ELEMS = 1024

SV_ROWS = 16
SV_PRE_MIX, SV_POST_MIX, SV_PRE_FFN, SV_POST_FFN, SV_PLE, SV_PLE_POST = 0, 2, 4, 6, 8, 10
SV_KV, SV_POOL_SCALE, SV_SINKS, SV_LOSS = 12, 13, 14, 15

MESH = pl.DeviceIdType.MESH
ANY = pl.BlockSpec(memory_space=pl.ANY)


def _dot(a, b):
    return jnp.dot(a, b, preferred_element_type=F32)


def _dot_nt(a, b):
    return lax.dot_general(a, b, (((1,), (1,)), ((), ())), preferred_element_type=F32)


def _dot_tn(a, b):
    return lax.dot_general(a, b, (((0,), (0,)), ((), ())), preferred_element_type=F32)


def _rstd(x):
    return lax.rsqrt(jnp.mean(x * x, axis=-1, keepdims=True) + EPS)


def _rms(x, g):
    return x * _rstd(x) * g


def _rms_bwd(x, g, dy):
    r = _rstd(x)
    n = x * r
    dn = dy * g
    dx = r * (dn - n * jnp.mean(dn * n, axis=-1, keepdims=True))
    dg = jnp.sum(dy * n, axis=0, keepdims=True)
    return dx, dg


def _add_all(terms):
    return functools.reduce(jnp.add, terms)


def _sigmoid(x):
    return 1.0 / (1.0 + jnp.exp(-x))


def _acc(ref, val, first):
    @pl.when(first)
    def _():
        ref[...] = val

    @pl.when(jnp.logical_not(first))
    def _():
        ref[...] += val


def _pool_counts(row0, rows):
    t = row0 + lax.broadcasted_iota(jnp.int32, (rows, D_MODEL), 0) + 1
    grp = lax.broadcasted_iota(jnp.int32, (rows, D_MODEL), 1) // POOL_GROUP
    win = jnp.left_shift(2, grp)
    return jnp.minimum(t, win).astype(F32)


def _window_sums(ext, shift_of):
    outs = []
    s = ext
    for gi in range(N_POOL_GROUPS):
        s = s + pltpu.roll(s, shift_of(1 << gi), axis=0)
        outs.append(s[:, :POOL_GROUP])
        s = s[:, POOL_GROUP:]
    return jnp.concatenate(outs, axis=1)


def _cparams(n_axes, vmem, collective_id=None):
    return pltpu.CompilerParams(dimension_semantics=("arbitrary",) * n_axes, vmem_limit_bytes=vmem,
                                collective_id=collective_id)


_EVERYONE = ("sibling", "x", "y", "far", "x sibling", "y sibling", "far sibling")
_PEER_SETS = (("sibling", "x", "y"), ("sibling",), ("x", "y"), _EVERYONE)


def _meet(peers):
    x, y, c = lax.axis_index("x"), lax.axis_index("y"), lax.axis_index("c")
    device = {"sibling": (x, y, 1 - c), "x": (1 - x, y, c), "y": (x, 1 - y, c), "far": (1 - x, 1 - y, c),
              "x sibling": (1 - x, y, 1 - c), "y sibling": (x, 1 - y, 1 - c), "far sibling": (1 - x, 1 - y, 1 - c)}
    barrier = pltpu.get_barrier_semaphore()
    for peer in peers:
        pl.semaphore_signal(barrier, inc=1, device_id=device[peer], device_id_type=pl.DeviceIdType.MESH)
    pl.semaphore_wait(barrier, len(peers))


def _row_spec(cols, tm=ROW_TILE):
    return pl.BlockSpec((tm, cols), lambda i: (i, 0))


def _full_spec(shape):
    zeros = (0,) * len(shape)
    return pl.BlockSpec(shape, lambda *_: zeros)


def _vec_spec():
    return _full_spec((1, D_MODEL))


def _column_views(parts):
    return [(a, b) for a in parts for b in range(a.shape[-1] // FFN_WEIGHT_COLS)]


def _column_ranges(views):
    return [(n * FFN_WEIGHT_COLS, (n + 1) * FFN_WEIGHT_COLS) for n in range(len(views))]


class _Gain:
    def __init__(self, stacked, layer):
        self.stacked, self.layer = stacked, layer

    def spec(self):
        layer = self.layer
        return pl.BlockSpec((None, 1, D_MODEL), lambda *_: (layer, 0, 0))


def _in_hbm(a):
    return pltpu.with_memory_space_constraint(a, pltpu.HBM) if a.size >= HBM_PIN_ELEMS else a


def _out_in_hbm(s):
    return pltpu.HBM(s.shape, s.dtype) if s.size >= HBM_PIN_ELEMS else s


def _launch(body, *, name, grid, in_specs, out_specs, out_shape, args, scratch_shapes=(), vmem=VMEM_MID, job=None):
    in_specs = [a.spec() if isinstance(a, _Gain) else s for s, a in zip(in_specs, args)]
    args = [_in_hbm(a.stacked if isinstance(a, _Gain) else a) for a in args]
    n_in, n_out, n_scr = len(args), len(out_shape), len(scratch_shapes)
    if job is not None and not job.args:
        job = None
    j_args, j_out, j_scr = ([], [], []) if job is None else ([_in_hbm(a) for a in job.args], job.out_shape, job.scratch)

    def run(*refs):
        groups, at = [], 0
        for n in (n_in, len(j_args), n_out, len(j_out), n_scr, len(j_scr)):
            groups.append(refs[at:at + n])
            at += n
        ins, j_ins, outs, j_outs, scr, j_sems = groups

        def begin():
            _meet(job.peers)
            job.start(j_ins, j_outs, j_sems)

        if job is None:
            body(*ins, *outs, *scr)
        elif not grid:
            begin()
            job.mid(j_ins, j_outs, j_sems)
            body(*ins, *outs, *scr)
            job.finish(j_ins, j_outs, j_sems)
        else:
            ids = [pl.program_id(a) for a in range(len(grid))]
            first = functools.reduce(jnp.logical_and, [i == 0 for i in ids])
            half = functools.reduce(jnp.logical_and, [ids[0] == grid[0] // 2] + [i == 0 for i in ids[1:]])
            last = functools.reduce(jnp.logical_and, [i == g - 1 for i, g in zip(ids, grid)])
            pl.when(first)(begin)
            pl.when(half)(lambda: job.mid(j_ins, j_outs, j_sems))
            body(*ins, *outs, *scr)
            pl.when(last)(lambda: job.finish(j_ins, j_outs, j_sems))

    res = pl.pallas_call(
        run, name=name, grid=grid,
        in_specs=list(in_specs) + [ANY] * len(j_args), out_specs=list(out_specs) + [ANY] * len(j_out),
        out_shape=[_out_in_hbm(s) for s in list(out_shape) + list(j_out)],
        scratch_shapes=list(scratch_shapes) + list(j_scr),
        compiler_params=_cparams(len(grid), vmem, None if job is None else _PEER_SETS.index(job.peers)),
    )(*args, *j_args)
    return res[:n_out], res[n_out:]


def _fwd_pool(x, g_pre, job=None):
    T = x.shape[0]
    tm = ROW_TILE
    nt = T // tm

    def body(x_ref, gpre_ref, d_ref, carry):
        i = pl.program_id(0)

        @pl.when(i == 0)
        def _():
            carry[...] = jnp.zeros_like(carry)

        h = _rms(x_ref[...], gpre_ref[...])
        ext = jnp.concatenate([carry[...], h], axis=0)
        carry[...] = h[tm - POOL_HALO:, :]
        sums = _window_sums(ext, lambda k: k)[POOL_HALO:, :]
        d_ref[...] = (sums / _pool_counts(i * tm, tm) - h).astype(BF16)

    return _launch(
        body, name="fwd_pool", grid=(nt,), in_specs=[_row_spec(D_MODEL), _vec_spec()], out_specs=[_row_spec(D_MODEL)],
        out_shape=[jax.ShapeDtypeStruct((T, D_MODEL), BF16)], scratch_shapes=[pltpu.VMEM((POOL_HALO, D_MODEL), F32)],
        args=(x, g_pre), job=job)


def _fwd_pool_mixer(x, d, wp, scale, g_post, g_ffn, job=None):
    T = x.shape[0]
    nt = T // ROW_TILE

    def body(x_ref, d_ref, wp_ref, sc_ref, gpost_ref, gffn_ref, x1_ref, h2_ref, yraw_ref):
        db = d_ref[...]
        yraw = jnp.concatenate(
            [_dot(db[:, g * POOL_GROUP:(g + 1) * POOL_GROUP], wp_ref[g]) for g in range(N_POOL_GROUPS)], axis=1)
        yraw_ref[...] = yraw.astype(BF16)
        x1 = x_ref[...] + _rms(yraw * sc_ref[...], gpost_ref[...])
        x1_ref[...] = x1
        h2_ref[...] = _rms(x1, gffn_ref[...]).astype(BF16)

    return _launch(
        body, name="fwd_pool_mixer", grid=(nt,),
        in_specs=[_row_spec(D_MODEL), _row_spec(D_MODEL), _full_spec((N_POOL_GROUPS, POOL_GROUP, POOL_GROUP)),
                  _vec_spec(), _vec_spec(), _vec_spec()],
        out_specs=[_row_spec(D_MODEL)] * 3,
        out_shape=[jax.ShapeDtypeStruct((T, D_MODEL), F32)] + [jax.ShapeDtypeStruct((T, D_MODEL), BF16)] * 2,
        args=(x, d, wp, scale, g_post, g_ffn), job=job)


def _fwd_ffn(layer, h2, x1, wgu, wd, g_post, g_ple, job=None):
    T = h2.shape[0]
    tm = min(FFN_ROW_TILE, T)
    nt = T // tm
    sub = tm // FFN_SUB_TILES
    last = FF_CHUNKS - 1
    wgu, wd = _column_views(wgu), _column_views(wd)
    n_gu, n_wd = len(wgu), len(wd)
    gu_cols = _column_ranges(wgu)

    def body(h2_ref, x1_ref, *refs):
        wgu_refs, wd_refs = refs[:n_gu], refs[n_gu:n_gu + n_wd]
        gpost_ref, gple_ref, gs_ref, us_ref, f_ref, x2_ref, h3_ref, acc = refs[n_gu + n_wd:]
        k = pl.program_id(0)
        i = pl.program_id(1)
        rows = pl.ds(pl.multiple_of(i * tm, tm), tm)
        parts = []
        for s in range(FFN_SUB_TILES):
            r = pl.ds(s * sub, sub)
            g = _add_all([_dot_nt(h2_ref[r, c0:c1], w[0]) for (c0, c1), w in zip(gu_cols, wgu_refs)])
            u = _add_all([_dot_nt(h2_ref[r, c0:c1], w[1]) for (c0, c1), w in zip(gu_cols, wgu_refs)])
            gs_ref[r, :] = g.astype(BF16)
            us_ref[r, :] = u.astype(BF16)
            a = (g * _sigmoid(g) * u).astype(BF16)
            parts.append(jnp.concatenate([_dot(a, w[...]) for w in wd_refs], axis=1))
        part = jnp.concatenate(parts, axis=0)

        @pl.when(k == 0)
        def _():
            acc[rows, :] = part

        @pl.when(jnp.logical_and(k > 0, k < last))
        def _():
            acc[rows, :] += part

        @pl.when(k == last)
        def _():
            f = acc[rows, :] + part
            f_ref[...] = f.astype(BF16)
            x2 = x1_ref[...] + _rms(f, gpost_ref[...])
            x2_ref[...] = x2
            h3_ref[...] = _rms(x2, gple_ref[...]).astype(BF16)

    def late(k, i):
        return (jnp.where(k == last, i, 0), 0)

    return _launch(
        body, name=f"fwd_ffn{layer}", grid=(FF_CHUNKS, nt),
        in_specs=[pl.BlockSpec((tm, D_MODEL), lambda k, i: (i, 0)), pl.BlockSpec((tm, D_MODEL), late)]
                 + [pl.BlockSpec((None, 2, FF_BLOCK, FFN_WEIGHT_COLS), lambda k, i, b=b: (k, 0, 0, b)) for _, b in wgu]
                 + [pl.BlockSpec((FF_BLOCK, FFN_WEIGHT_COLS), lambda k, i, b=b: (k, b)) for _, b in wd]
                 + [pl.BlockSpec((1, D_MODEL), lambda k, i: (0, 0))] * 2,
        out_specs=[pl.BlockSpec((None, tm, FF_BLOCK), lambda k, i: (k, i, 0)),
                   pl.BlockSpec((None, tm, FF_BLOCK), lambda k, i: (k, i, 0)),
                   pl.BlockSpec((tm, D_MODEL), late),
                   pl.BlockSpec((tm, D_MODEL), late),
                   pl.BlockSpec((tm, D_MODEL), late)],
        out_shape=[jax.ShapeDtypeStruct((FF_CHUNKS, T, FF_BLOCK), BF16),
                   jax.ShapeDtypeStruct((FF_CHUNKS, T, FF_BLOCK), BF16),
                   jax.ShapeDtypeStruct((T, D_MODEL), BF16),
                   jax.ShapeDtypeStruct((T, D_MODEL), F32),
                   jax.ShapeDtypeStruct((T, D_MODEL), BF16)],
        scratch_shapes=[pltpu.VMEM((T, D_MODEL), F32)],
        args=(h2, x1, *[w for w, _ in wgu], *[w for w, _ in wd], g_post, g_ple), vmem=VMEM_BIG, job=job)


def _fwd_ple_qkv(x2, h3, p, wgate, wproj, g_post, g_kv, g_mix, wkv, wq, job=None):
    T = x2.shape[0]
    nt = T // ROW_TILE

    def body(x2_ref, h3_ref, p_ref, wg_ref, wp_ref, gpost_ref, gkv_ref, gmix_ref, wkv_ref, wq_ref,
             x3_ref, z_ref, pe_ref, hk_ref, h1_ref, q_ref, kv_ref):
        z = _dot(h3_ref[...], wg_ref[...])
        pe = _dot(p_ref[...].astype(BF16), wp_ref[...])
        z_ref[...] = z.astype(BF16)
        pe_ref[...] = pe.astype(BF16)
        x3 = x2_ref[...] + _rms(pe * _sigmoid(z), gpost_ref[...])
        x3_ref[...] = x3
        r = _rstd(x3)
        hk = (x3 * r * gkv_ref[...]).astype(BF16)
        h1 = (x3 * r * gmix_ref[...]).astype(BF16)
        hk_ref[...] = hk
        h1_ref[...] = h1
        kv_ref[...] = _dot(hk, wkv_ref[...]).astype(BF16)
        q_ref[...] = _dot(h1, wq_ref[...]).astype(BF16)

    wide = jax.ShapeDtypeStruct((T, D_MODEL), BF16)
    return _launch(
        body, name="fwd_ple_qkv", grid=(nt,),
        in_specs=[_row_spec(D_MODEL), _row_spec(D_MODEL), _row_spec(PLE_DIM), _full_spec((D_MODEL, D_MODEL)),
                  _full_spec((PLE_DIM, D_MODEL)), _vec_spec(), _vec_spec(), _vec_spec(),
                  _full_spec((D_MODEL, 2 * KV_DIM)), _full_spec((D_MODEL, D_MODEL))],
        out_specs=[_row_spec(D_MODEL)] * 6 + [_row_spec(2 * KV_DIM)],
        out_shape=[jax.ShapeDtypeStruct((T, D_MODEL), F32)] + [wide] * 5 + [jax.ShapeDtypeStruct((T, 2 * KV_DIM), BF16)],
        args=(x2, h3, p, wgate, wproj, g_post, g_kv, g_mix, wkv, wq), job=job)


def _alibi_slope(h):
    return 2.0 ** (-8.0 * (h + 1) / N_HEADS)


ATT_SUB = 32
ATT_GROUP_ROWS = GQA_GROUP * ATT_BLOCK


def _att_mask(n, rel_ref, off_ref):
    qi = lax.broadcasted_iota(jnp.int32, (ATT_BLOCK, 2 * ATT_BLOCK), 0)
    si = lax.broadcasted_iota(jnp.int32, (ATT_BLOCK, 2 * ATT_BLOCK), 1)
    rel = ATT_BLOCK + qi - si
    valid = (rel >= 0) & (rel < ATT_BLOCK) & ((si >= ATT_BLOCK) | (n > 0))
    rel_ref[...] = rel.astype(F32)
    off_ref[...] = jnp.where(valid, 0.0, NEG_INF)


def _att_probs(raw, relf, off, slope, sink):
    s = raw * ATT_SCALE - slope * relf + off
    m = jnp.maximum(jnp.max(s, axis=-1, keepdims=True), sink)
    e = jnp.exp(s - m)
    es = jnp.exp(sink - m)
    inv = 1.0 / (jnp.sum(e, axis=-1, keepdims=True) + es)
    return e * inv, es * inv


def _stack_heads(ref, kh):
    first = kh * GQA_GROUP
    return jnp.concatenate([ref[:, (first + g) * HEAD_DIM:(first + g + 1) * HEAD_DIM] for g in range(GQA_GROUP)], axis=0)


def _unstack_heads(stacked):
    return [stacked[g * ATT_BLOCK:(g + 1) * ATT_BLOCK, :] for g in range(GQA_GROUP)]


def _fwd_attention(q, kpad, vpad, sinks, job=None):
    T = q.shape[0]
    nb = T // ATT_BLOCK

    def body(q_ref, k_ref, v_ref, sink_ref, o_ref, s_scr, p_scr, rel_scr, off_scr):
        n = pl.program_id(0)
        start = pl.multiple_of(n * ATT_BLOCK, ATT_BLOCK)
        kw = k_ref[pl.ds(start, 2 * ATT_BLOCK), :]
        vw = v_ref[pl.ds(start, 2 * ATT_BLOCK), :]
        _att_mask(n, rel_scr, off_scr)
        outs = []
        for kh in range(N_KV_HEADS):
            kk = kw[:, kh * HEAD_DIM:(kh + 1) * HEAD_DIM]
            vv = vw[:, kh * HEAD_DIM:(kh + 1) * HEAD_DIM]
            s_scr[...] = _dot_nt(_stack_heads(q_ref, kh), kk)
            for g in range(GQA_GROUP):
                h = kh * GQA_GROUP + g
                for row0 in range(0, ATT_BLOCK, ATT_SUB):
                    rows, sub = pl.ds(g * ATT_BLOCK + row0, ATT_SUB), pl.ds(row0, ATT_SUB)
                    pr, _ = _att_probs(s_scr[rows, :], rel_scr[sub, :], off_scr[sub, :], _alibi_slope(h),
                                       sink_ref[0, h])
                    p_scr[rows, :] = pr.astype(BF16)
            outs += _unstack_heads(_dot(p_scr[...], vv))
        o_ref[...] = jnp.concatenate(outs, axis=1).astype(BF16)

    return _launch(
        body, name="fwd_attention", grid=(nb,),
        in_specs=[_row_spec(D_MODEL, ATT_BLOCK), _full_spec((T + ATT_BLOCK, KV_DIM)), _full_spec((T + ATT_BLOCK, KV_DIM)),
                  pl.BlockSpec(memory_space=pltpu.SMEM)],
        out_specs=[_row_spec(D_MODEL, ATT_BLOCK)],
        out_shape=[jax.ShapeDtypeStruct((T, D_MODEL), BF16)],
        scratch_shapes=[pltpu.VMEM((ATT_GROUP_ROWS, 2 * ATT_BLOCK), F32), pltpu.VMEM((ATT_GROUP_ROWS, 2 * ATT_BLOCK), BF16)]
                       + [pltpu.VMEM((ATT_BLOCK, 2 * ATT_BLOCK), F32)] * 2,
        args=(q, kpad, vpad, sinks), job=job)


def _fwd_attn_out(attn, x, wo, g_post, g_ffn, job=None):
    T = x.shape[0]
    nt = T // ROW_TILE

    def body(a_ref, x_ref, wo_ref, gpost_ref, gffn_ref, y_ref, x1_ref, h2_ref):
        y = _dot(a_ref[...], wo_ref[...])
        y_ref[...] = y.astype(BF16)
        x1 = x_ref[...] + _rms(y, gpost_ref[...])
        x1_ref[...] = x1
        h2_ref[...] = _rms(x1, gffn_ref[...]).astype(BF16)

    return _launch(
        body, name="fwd_attn_out", grid=(nt,),
        in_specs=[_row_spec(D_MODEL), _row_spec(D_MODEL), _full_spec((D_MODEL, D_MODEL)), _vec_spec(), _vec_spec()],
        out_specs=[_row_spec(D_MODEL)] * 3,
        out_shape=[jax.ShapeDtypeStruct((T, D_MODEL), BF16), jax.ShapeDtypeStruct((T, D_MODEL), F32),
                   jax.ShapeDtypeStruct((T, D_MODEL), BF16)],
        args=(attn, x, wo, g_post, g_ffn), job=job)


def _bwd_ple(layer, dx3, x2, z, pe, h3, p, f, wgate, g_ple_post, g_ple, g_post_ffn, job=None):
    T = x2.shape[0]
    tm = ROW_TILE
    nt = T // tm

    def body(dx3_ref, x2_ref, z_ref, pe_ref, h3_ref, p_ref, f_ref, wg_ref, gpp_ref, gp_ref, gpf_ref,
             dx2_ref, df_ref, dwg_ref, dwp_ref, dgpp_ref, dgp_ref, dgpf_ref, acc_g, acc_p):
        i = pl.program_id(0)
        first = i == 0
        dx3v = dx3_ref[...]
        gate = _sigmoid(z_ref[...].astype(F32))
        pev = pe_ref[...].astype(F32)
        de, dgpp = _rms_bwd(pev * gate, gpp_ref[...], dx3v)
        dpe = (de * gate).astype(BF16)
        dz = (de * pev * gate * (1.0 - gate)).astype(BF16)
        _acc(acc_p, _dot_tn(p_ref[...].astype(BF16), dpe), first)
        _acc(acc_g, _dot_tn(h3_ref[...], dz), first)
        dh3 = _dot_nt(dz, wg_ref[...])
        dxn, dgp = _rms_bwd(x2_ref[...], gp_ref[...], dh3)
        dx2 = dx3v + dxn
        dx2_ref[...] = dx2
        df, dgpf = _rms_bwd(f_ref[...].astype(F32), gpf_ref[...], dx2)
        df_ref[...] = df.astype(BF16)
        _acc(dgpp_ref, dgpp, first)
        _acc(dgp_ref, dgp, first)
        _acc(dgpf_ref, dgpf, first)

        @pl.when(i == nt - 1)
        def _():
            dwg_ref[...] = acc_g[...].astype(BF16)
            dwp_ref[...] = acc_p[...].astype(BF16)

    return _launch(
        body, name=f"bwd_ple{layer}", grid=(nt,),
        in_specs=[_row_spec(D_MODEL)] * 5 + [_row_spec(PLE_DIM), _row_spec(D_MODEL), _full_spec((D_MODEL, D_MODEL)),
                  _vec_spec(), _vec_spec(), _vec_spec()],
        out_specs=[_row_spec(D_MODEL), _row_spec(D_MODEL), _full_spec((D_MODEL, D_MODEL)), _full_spec((PLE_DIM, D_MODEL)),
                   _vec_spec(), _vec_spec(), _vec_spec()],
        out_shape=[jax.ShapeDtypeStruct((T, D_MODEL), F32), jax.ShapeDtypeStruct((T, D_MODEL), BF16),
                   jax.ShapeDtypeStruct((D_MODEL, D_MODEL), BF16), jax.ShapeDtypeStruct((PLE_DIM, D_MODEL), BF16)]
                  + [jax.ShapeDtypeStruct((1, D_MODEL), F32)] * 3,
        scratch_shapes=[pltpu.VMEM((D_MODEL, D_MODEL), F32), pltpu.VMEM((PLE_DIM, D_MODEL), F32)],
        args=(dx3, x2, z, pe, h3, p, f, wgate, g_ple_post, g_ple, g_post_ffn), vmem=VMEM_BIG, job=job)


def _ple_loss_bwd(layer, x2, h3, p, f, target, wgate, wproj, g_ple_post, g_ple, g_post_ffn, job=None):
    T = x2.shape[0]
    tm = ROW_TILE
    nt = T // tm

    def body(x2_ref, h3_ref, p_ref, f_ref, tgt_ref, wg_ref, wp_ref, gpp_ref, gp_ref, gpf_ref,
             dx2_ref, df_ref, dwg_ref, dwp_ref, dgpp_ref, dgp_ref, dgpf_ref, loss_ref, acc_g, acc_p):
        i = pl.program_id(0)
        first = i == 0
        h3 = h3_ref[...]
        pb = p_ref[...].astype(BF16)
        x2v = x2_ref[...]
        gate = _sigmoid(_dot(h3, wg_ref[...]))
        pev = _dot(pb, wp_ref[...])
        e = pev * gate
        err = x2v + _rms(e, gpp_ref[...]) - tgt_ref[...]
        _acc(loss_ref, 0.5 * jnp.sum(jnp.mean(err * err, axis=-1, keepdims=True), axis=0, keepdims=True), first)
        dx3v = err * (1.0 / D_MODEL)
        de, dgpp = _rms_bwd(e, gpp_ref[...], dx3v)
        dpe = (de * gate).astype(BF16)
        dz = (de * pev * gate * (1.0 - gate)).astype(BF16)
        _acc(acc_p, _dot_tn(pb, dpe), first)
        _acc(acc_g, _dot_tn(h3, dz), first)
        dxn, dgp = _rms_bwd(x2v, gp_ref[...], _dot_nt(dz, wg_ref[...]))
        dx2 = dx3v + dxn
        dx2_ref[...] = dx2
        df, dgpf = _rms_bwd(f_ref[...].astype(F32), gpf_ref[...], dx2)
        df_ref[...] = df.astype(BF16)
        _acc(dgpp_ref, dgpp, first)
        _acc(dgp_ref, dgp, first)
        _acc(dgpf_ref, dgpf, first)

        @pl.when(i == nt - 1)
        def _():
            dwg_ref[...] = acc_g[...].astype(BF16)
            dwp_ref[...] = acc_p[...].astype(BF16)

    return _launch(
        body, name=f"ple_loss_bwd{layer}", grid=(nt,),
        in_specs=[_row_spec(D_MODEL), _row_spec(D_MODEL), _row_spec(PLE_DIM), _row_spec(D_MODEL), _row_spec(D_MODEL),
                  _full_spec((D_MODEL, D_MODEL)), _full_spec((PLE_DIM, D_MODEL)), _vec_spec(), _vec_spec(), _vec_spec()],
        out_specs=[_row_spec(D_MODEL), _row_spec(D_MODEL), _full_spec((D_MODEL, D_MODEL)), _full_spec((PLE_DIM, D_MODEL)),
                   _vec_spec(), _vec_spec(), _vec_spec(), _full_spec((1, 1))],
        out_shape=[jax.ShapeDtypeStruct((T, D_MODEL), F32), jax.ShapeDtypeStruct((T, D_MODEL), BF16),
                   jax.ShapeDtypeStruct((D_MODEL, D_MODEL), BF16), jax.ShapeDtypeStruct((PLE_DIM, D_MODEL), BF16)]
                  + [jax.ShapeDtypeStruct((1, D_MODEL), F32)] * 3 + [jax.ShapeDtypeStruct((1, 1), F32)],
        scratch_shapes=[pltpu.VMEM((D_MODEL, D_MODEL), F32), pltpu.VMEM((PLE_DIM, D_MODEL), F32)],
        args=(x2, h3, p, f, target, wgate, wproj, g_ple_post, g_ple, g_post_ffn), vmem=VMEM_BIG, job=job)


def _bwd_ffn_act(layer, df, gs, us, wgu, wd, job=None):
    T = df.shape[0]
    tm = min(FFN_ROW_TILE, T)
    nt = T // tm
    sub = tm // FFN_SUB_TILES
    last = FF_CHUNKS - 1
    wgu, wd = _column_views(wgu), _column_views(wd)
    n_gu, n_wd = len(wgu), len(wd)
    wd_cols = _column_ranges(wd)

    def body(df_ref, gs_ref, us_ref, *refs):
        wgu_refs, wd_refs = refs[:n_gu], refs[n_gu:n_gu + n_wd]
        dh_ref, dg_ref, du_ref, a_ref, acc_h = refs[n_gu + n_wd:]
        k = pl.program_id(0)
        i = pl.program_id(1)
        rows = pl.ds(pl.multiple_of(i * tm, tm), tm)
        dhs = []
        for s in range(FFN_SUB_TILES):
            r = pl.ds(s * sub, sub)
            g = gs_ref[r, :].astype(F32)
            u = us_ref[r, :].astype(F32)
            sg = _sigmoid(g)
            silu = g * sg
            a_ref[r, :] = (silu * u).astype(BF16)
            da = _add_all([_dot_nt(df_ref[r, c0:c1], w[...]) for (c0, c1), w in zip(wd_cols, wd_refs)])
            dg = (da * u * (sg * (1.0 + g * (1.0 - sg)))).astype(BF16)
            du = (da * silu).astype(BF16)
            dg_ref[r, :] = dg
            du_ref[r, :] = du
            dhs.append(jnp.concatenate([_dot(dg, w[0]) + _dot(du, w[1]) for w in wgu_refs], axis=1))
        dh = jnp.concatenate(dhs, axis=0)

        @pl.when(k == 0)
        def _():
            acc_h[rows, :] = dh

        @pl.when(jnp.logical_and(k > 0, k < last))
        def _():
            acc_h[rows, :] += dh

        @pl.when(k == last)
        def _():
            dh_ref[...] = acc_h[rows, :] + dh

    chunk_rows = pl.BlockSpec((None, tm, FF_BLOCK), lambda k, i: (k, i, 0))
    saved = jax.ShapeDtypeStruct((FF_CHUNKS, T, FF_BLOCK), BF16)
    return _launch(
        body, name=f"bwd_ffn_act{layer}", grid=(FF_CHUNKS, nt),
        in_specs=[pl.BlockSpec((tm, D_MODEL), lambda k, i: (i, 0)), chunk_rows, chunk_rows]
                 + [pl.BlockSpec((None, 2, FF_BLOCK, FFN_WEIGHT_COLS), lambda k, i, b=b: (k, 0, 0, b)) for _, b in wgu]
                 + [pl.BlockSpec((FF_BLOCK, FFN_WEIGHT_COLS), lambda k, i, b=b: (k, b)) for _, b in wd],
        out_specs=[pl.BlockSpec((tm, D_MODEL), lambda k, i: (jnp.where(k == last, i, 0), 0)),
                   chunk_rows, chunk_rows, chunk_rows],
        out_shape=[jax.ShapeDtypeStruct((T, D_MODEL), F32), saved, saved, saved],
        scratch_shapes=[pltpu.VMEM((T, D_MODEL), F32)],
        args=(df, gs, us, *[w for w, _ in wgu], *[w for w, _ in wd]), vmem=VMEM_BIG, job=job)


def _bwd_ffn_dw(layer, q, parts, h2, df, dg, du, a, job=None):
    T = h2.shape[0]
    width = D_MODEL // parts

    def body(h_ref, df_ref, dg_ref, du_ref, a_ref, dgu_ref, dwd_ref):
        h = h_ref[...]
        dgu_ref[0] = _dot_tn(dg_ref[...], h).astype(BF16)
        dgu_ref[1] = _dot_tn(du_ref[...], h).astype(BF16)
        dwd_ref[...] = _dot_tn(a_ref[...], df_ref[...]).astype(BF16)

    cols = pl.BlockSpec((T, width), lambda k: (0, q))
    chunk = pl.BlockSpec((None, T, FF_BLOCK), lambda k: (k, 0, 0))
    return _launch(
        body, name=f"bwd_ffn_dw{layer}_{q}", grid=(FF_CHUNKS,),
        in_specs=[cols, cols, chunk, chunk, chunk],
        out_specs=[pl.BlockSpec((None, 2, FF_BLOCK, width), lambda k: (k, 0, 0, 0)),
                   pl.BlockSpec((FF_BLOCK, width), lambda k: (k, 0))],
        out_shape=[jax.ShapeDtypeStruct((FF_CHUNKS, 2, FF_BLOCK, width), BF16),
                   jax.ShapeDtypeStruct((D_FF, width), BF16)],
        args=(h2, df, dg, du, a), vmem=VMEM_BIG, job=job)


def _bwd_attn_out(dx2, dh2, x1, y, attn, wo, g_ffn, g_post, job=None):
    T = x1.shape[0]
    nt = T // ROW_TILE

    def body(dx2_ref, dh2_ref, x1_ref, y_ref, a_ref, wo_ref, gffn_ref, gpost_ref,
             dx1_ref, da_ref, dwo_ref, dgf_ref, dgp_ref, acc):
        i = pl.program_id(0)
        first = i == 0
        dxn, dgf = _rms_bwd(x1_ref[...], gffn_ref[...], dh2_ref[...])
        dx1 = dx2_ref[...] + dxn
        dx1_ref[...] = dx1
        dy, dgp = _rms_bwd(y_ref[...].astype(F32), gpost_ref[...], dx1)
        dyb = dy.astype(BF16)
        da_ref[...] = _dot_nt(dyb, wo_ref[...]).astype(BF16)
        _acc(acc, _dot_tn(a_ref[...], dyb), first)
        _acc(dgf_ref, dgf, first)
        _acc(dgp_ref, dgp, first)

        @pl.when(i == nt - 1)
        def _():
            dwo_ref[...] = acc[...].astype(BF16)

    return _launch(
        body, name="bwd_attn_out", grid=(nt,),
        in_specs=[_row_spec(D_MODEL)] * 5 + [_full_spec((D_MODEL, D_MODEL)), _vec_spec(), _vec_spec()],
        out_specs=[_row_spec(D_MODEL), _row_spec(D_MODEL), _full_spec((D_MODEL, D_MODEL)), _vec_spec(), _vec_spec()],
        out_shape=[jax.ShapeDtypeStruct((T, D_MODEL), F32), jax.ShapeDtypeStruct((T, D_MODEL), BF16),
                   jax.ShapeDtypeStruct((D_MODEL, D_MODEL), BF16)] + [jax.ShapeDtypeStruct((1, D_MODEL), F32)] * 2,
        scratch_shapes=[pltpu.VMEM((D_MODEL, D_MODEL), F32)],
        args=(dx2, dh2, x1, y, attn, wo, g_ffn, g_post), job=job)


def _bwd_attention(q, dattn, kpad, vpad, sinks, job=None):
    T = q.shape[0]
    nb = T // ATT_BLOCK

    def body(q_ref, do_ref, k_ref, v_ref, sink_ref, dq_ref, dk_ref, dv_ref, ds_ref, s_scr, dp_scr, p_scr, dsb_scr,
             rel_scr, off_scr):
        n = pl.program_id(0)
        _att_mask(n, rel_scr, off_scr)

        @pl.when(n == 0)
        def _():
            dk_ref[...] = jnp.zeros_like(dk_ref)
            dv_ref[...] = jnp.zeros_like(dv_ref)
            ds_ref[...] = jnp.zeros_like(ds_ref)

        start = pl.multiple_of(n * ATT_BLOCK, ATT_BLOCK)
        win = pl.ds(start, 2 * ATT_BLOCK)
        kw = k_ref[win, :]
        vw = v_ref[win, :]
        lane = lax.broadcasted_iota(jnp.int32, (1, ATT_BLOCK), 1)
        dsink = jnp.zeros((1, ATT_BLOCK), F32)
        dqs, dks, dvs = [], [], []
        for kh in range(N_KV_HEADS):
            kk = kw[:, kh * HEAD_DIM:(kh + 1) * HEAD_DIM]
            vv = vw[:, kh * HEAD_DIM:(kh + 1) * HEAD_DIM]
            qs = _stack_heads(q_ref, kh)
            dos = _stack_heads(do_ref, kh)
            s_scr[...] = _dot_nt(qs, kk)
            dp_scr[...] = _dot_nt(dos, vv)
            for g in range(GQA_GROUP):
                h = kh * GQA_GROUP + g
                dsink_h = jnp.zeros((1, 1), F32)
                for row0 in range(0, ATT_BLOCK, ATT_SUB):
                    rows, sub = pl.ds(g * ATT_BLOCK + row0, ATT_SUB), pl.ds(row0, ATT_SUB)
                    pr, ps = _att_probs(s_scr[rows, :], rel_scr[sub, :], off_scr[sub, :], _alibi_slope(h),
                                        sink_ref[0, h])
                    dp = dp_scr[rows, :]
                    delta = jnp.sum(pr * dp, axis=-1, keepdims=True)
                    dsb_scr[rows, :] = (pr * (dp - delta) * ATT_SCALE).astype(BF16)
                    p_scr[rows, :] = pr.astype(BF16)
                    dsink_h = dsink_h - jnp.sum(ps * delta, axis=0, keepdims=True)
                dsink = dsink + jnp.where(lane == h, dsink_h, 0.0)
            dsb = dsb_scr[...]
            dqs += _unstack_heads(_dot(dsb, kk))
            dks.append(_dot_tn(dsb, qs))
            dvs.append(_dot_tn(p_scr[...], dos))
        dq_ref[...] = jnp.concatenate(dqs, axis=1).astype(BF16)
        dk_ref[win, :] += jnp.concatenate(dks, axis=1)
        dv_ref[win, :] += jnp.concatenate(dvs, axis=1)
        ds_ref[...] += dsink

    return _launch(
        body, name="bwd_attention", grid=(nb,),
        in_specs=[_row_spec(D_MODEL, ATT_BLOCK), _row_spec(D_MODEL, ATT_BLOCK), _full_spec((T + ATT_BLOCK, KV_DIM)),
                  _full_spec((T + ATT_BLOCK, KV_DIM)), pl.BlockSpec(memory_space=pltpu.SMEM)],
        out_specs=[_row_spec(D_MODEL, ATT_BLOCK), _full_spec((T + ATT_BLOCK, KV_DIM)), _full_spec((T + ATT_BLOCK, KV_DIM)),
                   _full_spec((1, ATT_BLOCK))],
        out_shape=[jax.ShapeDtypeStruct((T, D_MODEL), BF16), jax.ShapeDtypeStruct((T + ATT_BLOCK, KV_DIM), F32),
                   jax.ShapeDtypeStruct((T + ATT_BLOCK, KV_DIM), F32), jax.ShapeDtypeStruct((1, ATT_BLOCK), F32)],
        scratch_shapes=[pltpu.VMEM((ATT_GROUP_ROWS, 2 * ATT_BLOCK), F32)] * 2
                       + [pltpu.VMEM((ATT_GROUP_ROWS, 2 * ATT_BLOCK), BF16)] * 2
                       + [pltpu.VMEM((ATT_BLOCK, 2 * ATT_BLOCK), F32)] * 2,
        args=(q, dattn, kpad, vpad, sinks), vmem=VMEM_BIG, job=job)


def _bwd_qkv(dxres, dq, dkv, x3, h1, hk, wq, wkv, g_mix, g_kv, job=None):
    T = x3.shape[0]
    nt = T // ROW_TILE

    def body(dxr_ref, dq_ref, dkv_ref, x_ref, h1_ref, hk_ref, wq_ref, wkv_ref, gmix_ref, gkv_ref,
             dx_ref, dwq_ref, dwkv_ref, dgm_ref, dgk_ref, acc_q, acc_kv):
        i = pl.program_id(0)
        first = i == 0
        dqv = dq_ref[...]
        dkvv = dkv_ref[...]
        xv = x_ref[...]
        d1, dgm = _rms_bwd(xv, gmix_ref[...], _dot_nt(dqv, wq_ref[...]))
        d2, dgk = _rms_bwd(xv, gkv_ref[...], _dot_nt(dkvv, wkv_ref[...]))
        dx_ref[...] = dxr_ref[...] + d1 + d2
        _acc(acc_q, _dot_tn(h1_ref[...], dqv), first)
        _acc(acc_kv, _dot_tn(hk_ref[...], dkvv), first)
        _acc(dgm_ref, dgm, first)
        _acc(dgk_ref, dgk, first)

        @pl.when(i == nt - 1)
        def _():
            dwq_ref[...] = acc_q[...].astype(BF16)
            dwkv_ref[...] = acc_kv[...].astype(BF16)

    return _launch(
        body, name="bwd_qkv", grid=(nt,),
        in_specs=[_row_spec(D_MODEL), _row_spec(D_MODEL), _row_spec(2 * KV_DIM), _row_spec(D_MODEL), _row_spec(D_MODEL),
                  _row_spec(D_MODEL), _full_spec((D_MODEL, D_MODEL)), _full_spec((D_MODEL, 2 * KV_DIM)), _vec_spec(),
                  _vec_spec()],
        out_specs=[_row_spec(D_MODEL), _full_spec((D_MODEL, D_MODEL)), _full_spec((D_MODEL, 2 * KV_DIM)), _vec_spec(),
                   _vec_spec()],
        out_shape=[jax.ShapeDtypeStruct((T, D_MODEL), F32), jax.ShapeDtypeStruct((D_MODEL, D_MODEL), BF16),
                   jax.ShapeDtypeStruct((D_MODEL, 2 * KV_DIM), BF16)] + [jax.ShapeDtypeStruct((1, D_MODEL), F32)] * 2,
        scratch_shapes=[pltpu.VMEM((D_MODEL, D_MODEL), F32), pltpu.VMEM((D_MODEL, 2 * KV_DIM), F32)],
        args=(dxres, dq, dkv, x3, h1, hk, wq, wkv, g_mix, g_kv), job=job)


def _bwd_pool_mixer(dx2, dh2, x1, x, yraw, d, wp, scale, g_ffn, g_post, g_pre, job=None):
    T = x.shape[0]
    tm = ROW_TILE
    nt = T // tm

    def body(dx2_ref, dh2_ref, x1_ref, x_ref, yraw_ref, d_ref, wp_ref, sc_ref, gffn_ref, gpost_ref, gpre_ref,
             dx_ref, dwp_ref, dsc_ref, dgf_ref, dgp_ref, dgm_ref, carry, acc):
        i = pl.program_id(0)
        first = i == 0
        tile = nt - 1 - i

        @pl.when(first)
        def _():
            carry[...] = jnp.zeros_like(carry)

        dxn, dgf = _rms_bwd(x1_ref[...], gffn_ref[...], dh2_ref[...])
        dx1 = dx2_ref[...] + dxn
        yraw = yraw_ref[...].astype(F32)
        sc = sc_ref[...]
        dy, dgp = _rms_bwd(yraw * sc, gpost_ref[...], dx1)
        dsc = jnp.sum(dy * yraw, axis=0, keepdims=True)
        dyb = (dy * sc).astype(BF16)
        dv = d_ref[...]
        dds = []
        for g in range(N_POOL_GROUPS):
            cols = slice(g * POOL_GROUP, (g + 1) * POOL_GROUP)
            dds.append(_dot_nt(dyb[:, cols], wp_ref[g]))
            _acc(acc.at[g], _dot_tn(dv[:, cols], dyb[:, cols]), first)
        dd = jnp.concatenate(dds, axis=1)
        e = dd / _pool_counts(tile * tm, tm)
        ext = jnp.concatenate([e, carry[...]], axis=0)
        carry[...] = e[:POOL_HALO, :]
        sums = _window_sums(ext, lambda k: tm + POOL_HALO - k)[:tm, :]
        dxm, dgm = _rms_bwd(x_ref[...], gpre_ref[...], sums - dd)
        dx_ref[...] = dx1 + dxm
        _acc(dsc_ref, dsc, first)
        _acc(dgf_ref, dgf, first)
        _acc(dgp_ref, dgp, first)
        _acc(dgm_ref, dgm, first)

        @pl.when(i == nt - 1)
        def _():
            dwp_ref[...] = acc[...].astype(BF16)

    rev = pl.BlockSpec((tm, D_MODEL), lambda i: (nt - 1 - i, 0))
    return _launch(
        body, name="bwd_pool_mixer", grid=(nt,),
        in_specs=[rev] * 6 + [_full_spec((N_POOL_GROUPS, POOL_GROUP, POOL_GROUP))] + [_vec_spec()] * 4,
        out_specs=[rev, _full_spec((N_POOL_GROUPS, POOL_GROUP, POOL_GROUP))] + [_vec_spec()] * 4,
        out_shape=[jax.ShapeDtypeStruct((T, D_MODEL), F32),
                   jax.ShapeDtypeStruct((N_POOL_GROUPS, POOL_GROUP, POOL_GROUP), BF16)]
                  + [jax.ShapeDtypeStruct((1, D_MODEL), F32)] * 4,
        scratch_shapes=[pltpu.VMEM((POOL_HALO, D_MODEL), F32), pltpu.VMEM((N_POOL_GROUPS, POOL_GROUP, POOL_GROUP), F32)],
        args=(dx2, dh2, x1, x, yraw, d, wp, scale, g_ffn, g_post, g_pre), job=job)


def _my_place():
    return lax.axis_index("x"), lax.axis_index("y"), lax.axis_index("c")


def _dev_index(px, py, pc):
    return 4 * px + 2 * py + pc


def _peer_by_relation(r):
    x, y, c = _my_place()
    return (x ^ ((r >> 2) & 1), y ^ ((r >> 1) & 1), c ^ (r & 1))


def _slot_pool(ref, j):
    return ref.at[:, pl.ds(pl.multiple_of(j * 32, 32), 32), :]


def _slot_scale(ref, j):
    return ref.at[:, pl.ds(pl.multiple_of(j * 128, 128), 128)]


def _slot_rows128(ref, j):
    return ref.at[pl.ds(pl.multiple_of(j * 128, 128), 128), :]


def _slot_gu(ref, j):
    return ref.at[j % FF_CHUNKS, j // FF_CHUNKS]


def _slot_wd(ref, j):
    return ref.at[pl.ds(pl.multiple_of(j * WD_ROWS, 16), WD_ROWS), :]


def _slot_cols128(ref, j):
    return ref.at[:, pl.ds(pl.multiple_of(j * 128, 128), 128)]


_GATHERED = {
    "pool": ((N_POOL_GROUPS, POOL_GROUP, POOL_GROUP), BF16, _slot_pool),
    "scale": ((1, D_MODEL), F32, _slot_scale),
    "kv": ((D_MODEL, 2 * KV_DIM), BF16, _slot_rows128),
    "q": ((D_MODEL, D_MODEL), BF16, _slot_rows128),
    "o": ((D_MODEL, D_MODEL), BF16, _slot_rows128),
    "gu": ((FF_CHUNKS, 2, FF_BLOCK, D_MODEL), BF16, _slot_gu),
    "wd": ((D_FF, D_MODEL), BF16, _slot_wd),
    "guh": ((FF_CHUNKS, 2, FF_BLOCK, D_MODEL // 2), BF16, _slot_gu),
    "wdh": ((D_FF, D_MODEL // 2), BF16, _slot_wd),
    "gate": ((D_MODEL, D_MODEL), BF16, _slot_rows128),
    "proj": ((PLE_DIM, D_MODEL), BF16, _slot_cols128),
}


def _no_compute():
    pass


class _AllGather:
    peers = ("sibling", "x", "y")

    def __init__(self, names, shards):
        self.kinds = [_GATHERED[n.rstrip("01_")] for n in names]
        entries = [shards[n] if isinstance(shards[n], tuple) else (shards[n], None) for n in names]
        self.args = [array for array, _ in entries]
        self.columns = [columns for _, columns in entries]
        self.out_shape = [jax.ShapeDtypeStruct(shape, dtype) for shape, dtype, _ in self.kinds]
        n = len(names)
        self.scratch = [pltpu.SemaphoreType.DMA((n, 7)), pltpu.SemaphoreType.DMA((n, 7)), pltpu.SemaphoreType.DMA((n,))]

    def _plan(self, srcs, outs, sems):
        send_sems, recv_sems, local_sems = sems
        x, y, c = _my_place()

        def slot(t, dev):
            return self.kinds[t][2](outs[t], _dev_index(*dev))

        def copy(t, k, block, to, src=None):
            return pltpu.make_async_remote_copy(
                src_ref=slot(t, block) if src is None else src, dst_ref=slot(t, block),
                send_sem=send_sems.at[t, k], recv_sem=recv_sems.at[t, k], device_id=to, device_id_type=MESH)

        return types.SimpleNamespace(
            copy=copy, core=c, me=(x, y, c), sibling=(x, y, 1 - c),
            x_chip=(1 - x, y), y_chip=(x, 1 - y), far_chip=(1 - x, 1 - y),
            via=(x ^ (1 - c), y ^ c),
            onto=(x ^ c, y ^ (1 - c)),
            k_via=1 + c, k_onto=2 - c,
            local=[pltpu.make_async_copy(self._shard(srcs, t), slot(t, (x, y, c)), local_sems.at[t])
                   for t in range(len(srcs))])

    def _shard(self, srcs, t):
        if self.columns[t] is None:
            return srcs[t]
        first, end = self.columns[t]
        return srcs[t].at[:, first:end]

    def start(self, srcs, outs, sems):
        p = self._plan(srcs, outs, sems)
        for cp in p.local:
            cp.start()
        for t in range(len(srcs)):
            shard = self._shard(srcs, t)
            p.copy(t, 0, p.me, p.sibling, src=shard).start()
            p.copy(t, 1, p.me, (*p.x_chip, p.core), src=shard).start()
            p.copy(t, 2, p.me, (*p.y_chip, p.core), src=shard).start()

    def mid(self, srcs, outs, sems):
        p = self._plan(srcs, outs, sems)
        for t in range(len(srcs)):
            block = (*p.via, p.core)
            p.copy(t, p.k_via, block, p.me).wait_recv()
            p.copy(t, 3, block, (*p.onto, p.core)).start()
            p.copy(t, 3 + p.k_via, block, p.sibling).start()

    def finish(self, srcs, outs, sems):
        p = self._plan(srcs, outs, sems)
        n = len(srcs)
        for t in range(n):
            block = (*p.onto, p.core)
            p.copy(t, p.k_onto, block, p.me).wait_recv()
            p.copy(t, 3 + p.k_onto, block, p.sibling).start()
        for t in range(n):
            block = (*p.far_chip, p.core)
            p.copy(t, 3, block, p.me).wait_recv()
            p.copy(t, 6, block, p.sibling).start()
        other = 1 - p.core
        for t in range(n):
            p.copy(t, 0, (*p.me[:2], other), p.me).wait_recv()
            for k, chip in ((4, p.x_chip), (5, p.y_chip), (6, p.far_chip)):
                p.copy(t, k, (*chip, other), p.me).wait_recv()
            for k in range(7):
                p.copy(t, k, p.me, p.sibling).wait_send()
        for cp in p.local:
            cp.wait()


def _jobs_only(name, job=None):
    return _launch(_no_compute, name=name, grid=(), in_specs=[], out_specs=[], out_shape=[], args=(), job=job)


def _block_pool(ref, j):
    return ref.at[:, pl.ds(pl.multiple_of(j * 32, 32), 32), :]


def _block_rows128(ref, j):
    return ref.at[pl.ds(pl.multiple_of(j * 128, 128), 128), :]


def _block_gu(ref, j):
    return ref.at[j % FF_CHUNKS, j // FF_CHUNKS]


def _block_wd(ref, j):
    return ref.at[pl.ds(pl.multiple_of(j * WD_ROWS, 16), WD_ROWS), :]


def _block_cols128(ref, j):
    return ref.at[:, pl.ds(pl.multiple_of(j * 128, 128), 128)]


_SCATTERED = {
    "pool": ((N_POOL_GROUPS, 32, POOL_GROUP), _block_pool),
    "kv": ((128, 2 * KV_DIM), _block_rows128),
    "q": ((128, D_MODEL), _block_rows128),
    "o": ((128, D_MODEL), _block_rows128),
    "gu": ((FF_BLOCK, FF_PART), _block_gu),
    "wd": ((WD_ROWS, FF_PART), _block_wd),
    "guA": ((FF_BLOCK, FF_PART), lambda ref, j: _block_gu(ref, j).at[:, :FF_PART]),
    "guB": ((FF_BLOCK, FF_PART), lambda ref, j: _block_gu(ref, j).at[:, FF_PART:]),
    "wdA": ((WD_ROWS, FF_PART), lambda ref, j: _block_wd(ref, j).at[:, :FF_PART]),
    "wdB": ((WD_ROWS, FF_PART), lambda ref, j: _block_wd(ref, j).at[:, FF_PART:]),
    "gate": ((128, D_MODEL), _block_rows128),
    "proj": ((PLE_DIM, 128), _block_cols128),
}


class _SiblingSwap:
    peers = ("sibling",)

    def __init__(self, pieces):
        self.kinds = [_SCATTERED[kind] for kind, _ in pieces]
        self.args = [g for _, g in pieces]
        self.out_shape = [jax.ShapeDtypeStruct((N_CHIPS, *block), BF16) for block, _ in self.kinds]
        n = len(pieces)
        self.scratch = [pltpu.SemaphoreType.DMA((n, N_CHIPS)), pltpu.SemaphoreType.DMA((n, N_CHIPS))]

    def _copies(self, srcs, outs, sems):
        send_sems, recv_sems = sems
        x, y, c = _my_place()
        return [pltpu.make_async_remote_copy(
            src_ref=block(srcs[t], 2 * ch + 1 - c), dst_ref=outs[t].at[ch], send_sem=send_sems.at[t, ch],
            recv_sem=recv_sems.at[t, ch], device_id=(x, y, 1 - c), device_id_type=MESH)
            for t, (_, block) in enumerate(self.kinds) for ch in range(N_CHIPS)]

    def start(self, srcs, outs, sems):
        for cp in self._copies(srcs, outs, sems):
            cp.start()

    def finish(self, srcs, outs, sems):
        for cp in self._copies(srcs, outs, sems):
            cp.wait()


class _ChipScatter:
    N_BUFS = 4
    peers = ("x", "y")

    def __init__(self, pieces):
        self.kinds = [_SCATTERED[kind] for kind, _, _ in pieces]
        self.n = n = len(pieces)
        self.args = [g for _, g, _ in pieces] + [s for _, _, s in pieces]
        self.out_shape = [jax.ShapeDtypeStruct((2, *block), BF16) for block, _ in self.kinds]
        self.scratch = []
        for block, _ in self.kinds:
            self.scratch += [pltpu.VMEM((N_CHIPS, *block), BF16)] * 3 + [pltpu.VMEM((2, *block), BF16)]
        dma = pltpu.SemaphoreType.DMA
        self.scratch += [dma((n, N_CHIPS + 1)), dma((n, 2)), dma((n, 2)), dma((n,)), dma((n,)), dma((n,))]

    def _plan(self, outs, scr):
        n = self.n
        first_send, first_recv, second_send, second_recv, keep_sems = scr[self.N_BUFS * n + 1:]
        x, y, c = _my_place()
        via = (x ^ (1 - c), y ^ c)
        onto = (x ^ c, y ^ (1 - c))
        index = lambda chip: 2 * chip[0] + chip[1]
        first, second, keep = [], [], []
        for t in range(n):
            total, inbox = scr[self.N_BUFS * t + 2], scr[self.N_BUFS * t + 3]
            for k, chip in enumerate((via, (1 - x, 1 - y))):
                first.append(pltpu.make_async_remote_copy(
                    src_ref=total.at[index(chip)], dst_ref=inbox.at[k], send_sem=first_send.at[t, k],
                    recv_sem=first_recv.at[t, k], device_id=(*via, c), device_id_type=MESH))
            second.append(pltpu.make_async_remote_copy(
                src_ref=total.at[index(onto)], dst_ref=outs[t].at[1], send_sem=second_send.at[t],
                recv_sem=second_recv.at[t], device_id=(*onto, c), device_id_type=MESH))
            keep.append(pltpu.make_async_copy(total.at[index((x, y))], outs[t].at[0], keep_sems.at[t]))
        return first, second, keep, index((x, y)), index(onto)

    def start(self, ins, outs, scr):
        n = self.n
        load_sems = scr[self.N_BUFS * n]
        c = lax.axis_index("c")
        loads = []
        for t, (_, block) in enumerate(self.kinds):
            mine, theirs = scr[self.N_BUFS * t], scr[self.N_BUFS * t + 1]
            loads += [pltpu.make_async_copy(block(ins[t], 2 * ch + c), mine.at[ch], load_sems.at[t, ch])
                      for ch in range(N_CHIPS)]
            loads.append(pltpu.make_async_copy(ins[n + t], theirs, load_sems.at[t, N_CHIPS]))
        for cp in loads:
            cp.start()
        for cp in loads:
            cp.wait()
        for t in range(n):
            mine, theirs, total = scr[self.N_BUFS * t:self.N_BUFS * t + 3]
            for ch in range(N_CHIPS):
                total[ch] = (mine[ch].astype(F32) + theirs[ch].astype(F32)).astype(BF16)
        for cp in self._plan(outs, scr)[0]:
            cp.start()

    def mid(self, ins, outs, scr):
        first, second, keep, me, onto = self._plan(outs, scr)
        for cp in first:
            cp.wait_recv()
        for t in range(self.n):
            total, inbox = scr[self.N_BUFS * t + 2], scr[self.N_BUFS * t + 3]
            for k, slot in enumerate((me, onto)):
                total[slot] = (total[slot].astype(F32) + inbox[k].astype(F32)).astype(BF16)
        for cp in second + keep:
            cp.start()

    def finish(self, ins, outs, scr):
        first, second, keep, _, _ = self._plan(outs, scr)
        for cp in first:
            cp.wait_send()
        for cp in second + keep:
            cp.wait()


class _ToEveryone:
    peers = _EVERYONE

    def __init__(self, scattered=(), gathered=()):
        self.blocks = [_SCATTERED[kind][1] for kind, _ in scattered] + [None] * len(gathered)
        self.args = [g for _, g in scattered] + list(gathered)
        self.out_shape = [jax.ShapeDtypeStruct((N_DEV, *_SCATTERED[kind][0]), BF16) for kind, _ in scattered]
        self.out_shape += [jax.ShapeDtypeStruct((N_DEV, *a.shape), a.dtype) for a in gathered]
        n = len(self.args)
        self.scratch = [pltpu.SemaphoreType.DMA((n, N_DEV - 1)), pltpu.SemaphoreType.DMA((n, N_DEV - 1)),
                        pltpu.SemaphoreType.DMA((n,))]

    def _copies(self, srcs, outs, sems):
        send_sems, recv_sems, local_sems = sems
        me = _dev_index(*_my_place())
        copies = []
        for t, block in enumerate(self.blocks):
            part = (lambda j, t=t, block=block: srcs[t] if block is None else block(srcs[t], j))
            copies.append(pltpu.make_async_copy(part(me), outs[t].at[me], local_sems.at[t]))
            for r in range(1, N_DEV):
                peer = _peer_by_relation(r)
                copies.append(pltpu.make_async_remote_copy(
                    src_ref=part(_dev_index(*peer)), dst_ref=outs[t].at[me], send_sem=send_sems.at[t, r - 1],
                    recv_sem=recv_sems.at[t, r - 1], device_id=peer, device_id_type=MESH))
        return copies

    def start(self, srcs, outs, sems):
        for cp in self._copies(srcs, outs, sems):
            cp.start()

    def finish(self, srcs, outs, sems):
        for cp in self._copies(srcs, outs, sems):
            cp.wait()


class _Jobs:
    def __init__(self, *jobs):
        self.jobs = jobs
        together = {p for j in jobs for p in j.peers}
        self.peers = tuple(p for p in _EVERYONE if p in together)
        self.args = [a for j in jobs for a in j.args]
        self.out_shape = [o for j in jobs for o in j.out_shape]
        self.scratch = [s for j in jobs for s in j.scratch]

    def _split(self, refs, attr):
        at = 0
        for j in self.jobs:
            n = len(getattr(j, attr))
            yield refs[at:at + n]
            at += n

    def _each(self, ins, outs, scr):
        return zip(self.jobs, self._split(ins, "args"), self._split(outs, "out_shape"), self._split(scr, "scratch"))

    def start(self, ins, outs, scr):
        for j, i, o, s in self._each(ins, outs, scr):
            j.start(i, o, s)

    def mid(self, ins, outs, scr):
        for j, i, o, s in self._each(ins, outs, scr):
            if hasattr(j, "mid"):
                j.mid(i, o, s)

    def finish(self, ins, outs, scr):
        for j, i, o, s in self._each(ins, outs, scr):
            j.finish(i, o, s)

    def split_outputs(self, outs):
        return list(self._split(outs, "out_shape"))


def _adamw_math(w, g, m, v):
    m = ADAM_B1 * m + (1.0 - ADAM_B1) * g
    v = ADAM_B2 * v + (1.0 - ADAM_B2) * (g * g)
    m_hat = m / (1.0 - ADAM_B1 ** ADAM_STEP)
    v_hat = v / (1.0 - ADAM_B2 ** ADAM_STEP)
    delta = -ADAM_LR * (m_hat / (jnp.sqrt(v_hat) + ADAM_EPS) + ADAM_WD * w)
    return delta, m, v


def _adamw(name, w, m, v, landings, n_col_blocks=1, job=None):
    n_slots, r, c = landings[0].shape
    grid = (w.shape[0] // r, n_col_blocks)

    def body(w_ref, m_ref, v_ref, *rest):
        l_refs, (g_ref, d_ref, nm_ref, nv_ref) = rest[:len(landings)], rest[len(landings):]
        step = pl.program_id(0) * n_col_blocks + pl.program_id(1)
        for idx, l_ref in enumerate(l_refs):
            @pl.when(step == idx)
            def _(l_ref=l_ref):
                g = l_ref[0].astype(F32)
                for s in range(1, n_slots):
                    g = g + l_ref[s].astype(F32)
                g_ref[...] = g
                d_ref[...], nm_ref[...], nv_ref[...] = _adamw_math(w_ref[...], g, m_ref[...], v_ref[...])

    spec = pl.BlockSpec((r, c), lambda a, b: (a, b))
    return _launch(
        body, name=f"adamw_{name}", grid=grid,
        in_specs=[spec, spec, spec] + [_full_spec((n_slots, r, c))] * len(landings),
        out_specs=[spec] * 4, out_shape=[jax.ShapeDtypeStruct(w.shape, F32)] * 4,
        args=(w, m, v, *landings), vmem=VMEM_BIG, job=job)


_SMALL = (("pre_mix_g", SV_PRE_MIX, 2), ("post_mix_g", SV_POST_MIX, 2), ("pre_ffn_g", SV_PRE_FFN, 2),
          ("post_ffn_g", SV_POST_FFN, 2), ("ple_g", SV_PLE, 2), ("ple_post_g", SV_PLE_POST, 2), ("kv_g", SV_KV, 1),
          ("pool_scale", SV_POOL_SCALE, 1), ("sinks", SV_SINKS, 1))


def _small_adamw(slabs, params):
    flat = [a for name, _, _ in _SMALL for a in params[name]]
    n_in = 1 + len(flat)

    def body(*refs):
        slabs_ref, wmv = refs[0], refs[1:n_in]
        loss_ref, outs, total = refs[n_in], refs[n_in + 1:-1], refs[-1]
        me = _dev_index(*_my_place())
        g = slabs_ref[0]
        for s in range(1, N_DEV):
            g = g + slabs_ref[s]
        total[...] = g
        loss_ref[...] = total[SV_LOSS:SV_LOSS + 1, 0:1]
        for idx, (name, row, n_rows) in enumerate(_SMALL):
            w_ref, m_ref, v_ref = wmv[3 * idx:3 * idx + 3]
            g_ref, d_ref, nm_ref, nv_ref = outs[4 * idx:4 * idx + 4]
            if name == "pool_scale":
                g = total[row:row + 1, pl.ds(pl.multiple_of(me * 128, 128), 128)]
            else:
                g = total[row:row + n_rows, 0:w_ref.shape[1]]
            g_ref[...] = g
            d_ref[...], nm_ref[...], nv_ref[...] = _adamw_math(w_ref[...], g, m_ref[...], v_ref[...])

    out_shape = [jax.ShapeDtypeStruct((1, 1), F32)]
    for name, _, _ in _SMALL:
        out_shape += [jax.ShapeDtypeStruct(params[name][0].shape, F32)] * 4
    res, _ = _launch(
        body, name="small_adamw", grid=(1,),
        in_specs=[_full_spec(a.shape) for a in (slabs, *flat)], out_specs=[_full_spec(s.shape) for s in out_shape],
        out_shape=out_shape, scratch_shapes=[pltpu.VMEM((SV_ROWS, D_MODEL), F32)], args=(slabs, *flat))
    return res[0], {name: res[1 + 4 * idx:5 + 4 * idx] for idx, (name, _, _) in enumerate(_SMALL)}


def _local_step(x, p, tgt, gains, sinks, shards, weights):
    row = lambda first_row, layer: _Gain(gains, first_row + layer)
    gather = lambda *names: _AllGather(names, shards)
    g_pre_mix, g_post_mix, g_pre_ffn, g_post_ffn = SV_PRE_MIX, SV_POST_MIX, SV_PRE_FFN, SV_POST_FFN
    g_ple, g_ple_post, g_kv = SV_PLE, SV_PLE_POST, _Gain(gains, SV_KV)

    (dpool,), (wp, scale, wgu0, wd0) = _fwd_pool(x, row(g_pre_mix, 0), job=gather("pool", "scale", "gu0", "wd0"))
    wgu0, wd0 = [wgu0], [wd0]
    (x1_0, h2_0, yraw), _ = _fwd_pool_mixer(x, dpool, wp, scale, row(g_post_mix, 0), row(g_pre_ffn, 0))
    (gs0, us0, f0, x2_0, h3_0), (wgate0, wproj0, wkv, wq, wgu1_a) = _fwd_ffn(
        0, h2_0, x1_0, wgu0, wd0, row(g_post_ffn, 0), row(g_ple, 0),
        job=gather("gate0", "proj0", "kv", "q", "guh1_0"))
    (x3_0, z0, pe0, hk, h1, q, kv), (wo, wd1_a) = _fwd_ple_qkv(
        x2_0, h3_0, p[0], wgate0, wproj0, row(g_ple_post, 0), g_kv, row(g_pre_mix, 1), wkv, wq,
        job=gather("o", "wdh1_0"))
    front = ((ATT_BLOCK, 0), (0, 0))
    kpad = jnp.pad(kv[:, :KV_DIM], front)
    vpad = jnp.pad(kv[:, KV_DIM:], front)
    (attn,), (wgu1_b,) = _fwd_attention(q, kpad, vpad, sinks, job=gather("guh1_1"))
    (y1, x1_1, h2_1), (wd1_b,) = _fwd_attn_out(attn, x3_0, wo, row(g_post_mix, 1), row(g_pre_ffn, 1),
                                               job=gather("wdh1_1"))
    wgu1, wd1 = [wgu1_a, wgu1_b], [wd1_a, wd1_b]
    (gs1, us1, f1, x2_1, h3_1), (wgate1, wproj1) = _fwd_ffn(
        1, h2_1, x1_1, wgu1, wd1, row(g_post_ffn, 1), row(g_ple, 1), job=gather("gate1", "proj1"))

    produced, swapped, landed = {}, {}, {}

    def kind_of(name):
        return name.rstrip("0123_")

    def hosted(call, *args, swap=(), spread=(), extra=None):
        jobs = []
        if swap:
            jobs.append(_SiblingSwap([(kind_of(n), produced[n]) for n in swap]))
        if spread:
            jobs.append(_ChipScatter([(kind_of(n), produced[n], swapped[n]) for n in spread]))
        if extra is not None:
            jobs.append(extra)
        jobs = _Jobs(*jobs)
        outs, job_outs = call(*args, job=jobs)
        parts = jobs.split_outputs(job_outs)
        if swap:
            swapped.update(zip(swap, parts.pop(0)))
        if spread:
            landed.update(zip(spread, parts.pop(0)))
        return outs if extra is None else (outs, parts.pop(0))

    ffn_q = lambda layer, qtr: (f"gu{layer}_{qtr}", f"wd{layer}_{qtr}")

    dx2_1, df1, produced["gate1"], produced["proj1"], dg_ple_post1, dg_ple1, dg_post_ffn1, loss = hosted(
        _ple_loss_bwd, 1, x2_1, h3_1, p[1], f1, tgt, wgate1, wproj1, row(g_ple_post, 1), row(g_ple, 1),
        row(g_post_ffn, 1))
    dh2_1, dg1, du1, a1 = hosted(_bwd_ffn_act, 1, df1, gs1, us1, wgu1, wd1, swap=("gate1", "proj1"))
    dgu1, dwd1 = hosted(_bwd_ffn_dw, 1, 0, 1, h2_1, df1, dg1, du1, a1, spread=("gate1", "proj1"))
    produced.update(guA1=dgu1, guB1=dgu1, wdA1=dwd1, wdB1=dwd1)
    dx1_1, dattn, produced["o"], dg_pre_ffn1, dg_post_mix1 = hosted(
        _bwd_attn_out, dx2_1, dh2_1, x1_1, y1, attn, wo, row(g_pre_ffn, 1), row(g_post_mix, 1),
        swap=("guA1", "wdA1", "guB1", "wdB1"))
    dq, dkpad, dvpad, dsinks = hosted(_bwd_attention, q, dattn, kpad, vpad, sinks, spread=("guA1", "wdA1"))
    dkv = jnp.concatenate([dkpad[ATT_BLOCK:], dvpad[ATT_BLOCK:]], axis=1).astype(BF16)
    dx3_0, produced["q"], produced["kv"], dg_pre_mix1, dg_kv = hosted(
        _bwd_qkv, dx1_1, dq, dkv, x3_0, h1, hk, wq, wkv, row(g_pre_mix, 1), g_kv, swap=("o",), spread=("wdB1",))
    dx2_0, df0, produced["gate0"], produced["proj0"], dg_ple_post0, dg_ple0, dg_post_ffn0 = hosted(
        _bwd_ple, 0, dx3_0, x2_0, z0, pe0, h3_0, p[0], f0, wgate0, row(g_ple_post, 0), row(g_ple, 0),
        row(g_post_ffn, 0), swap=("q", "kv"), spread=("guB1",))
    for half, letter in enumerate("AB"):
        landed[f"gu1_{half}"], landed[f"wd1_{half}"] = landed[f"gu{letter}1"], landed[f"wd{letter}1"]
    dh2_0, dg0, du0, a0 = hosted(_bwd_ffn_act, 0, df0, gs0, us0, wgu0, wd0,
                                 swap=("gate0", "proj0"), spread=("o", "q", "kv"))
    part_hosts = [dict(spread=("gate0", "proj0")), dict(swap=ffn_q(0, 0))]
    for part in range(FF_PARTS):
        produced[f"gu0_{part}"], produced[f"wd0_{part}"] = hosted(
            _bwd_ffn_dw, 0, part, FF_PARTS, h2_0, df0, dg0, du0, a0, **part_hosts[part])
    grad_x, produced["pool"], dscale, dg_pre_ffn0, dg_post_mix0, dg_pre_mix0 = hosted(
        _bwd_pool_mixer, dx2_0, dh2_0, x1_0, x, yraw, dpool, wp, scale, row(g_pre_ffn, 0), row(g_post_mix, 0),
        row(g_pre_mix, 0), swap=ffn_q(0, 1), spread=ffn_q(0, 0))

    def update(name, n_col_blocks=1, pieces=None, swap=(), spread=()):
        w, m, v = weights[name]
        rows = w.size // w.shape[-1]
        flat = [landed[n].reshape(landed[n].shape[0], -1, landed[n].shape[-1])
                for n in (pieces or [kind_short[name]])]
        outs = hosted(_adamw, name, w.reshape(rows, -1), m.reshape(rows, -1), v.reshape(rows, -1), flat,
                      n_col_blocks, swap=swap, spread=spread)
        return [o.reshape(w.shape) for o in outs]

    kind_short = {"w_q": "q", "w_kv": "kv", "w_o": "o", "pool_w": "pool"}
    upd = {}
    lanes = lambda a: jnp.pad(a, ((0, 0), (0, D_MODEL - a.shape[1])))
    small = jnp.concatenate([
        dg_pre_mix0, dg_pre_mix1, dg_post_mix0, dg_post_mix1, dg_pre_ffn0, dg_pre_ffn1, dg_post_ffn0, dg_post_ffn1,
        dg_ple0, dg_ple1, dg_ple_post0, dg_ple_post1, dg_kv, dscale, lanes(dsinks[:, :N_HEADS]), lanes(loss)], axis=0)

    everyone = _ToEveryone(scattered=[("pool", produced["pool"])], gathered=[small])
    _, (landed["pool"], slabs) = hosted(_jobs_only, "scatter_tail", spread=ffn_q(0, 1), extra=everyone)
    upd["w_ple_gate"] = update("w_ple_gate", pieces=("gate0", "gate1"))
    upd["w_ple_proj"] = update("w_ple_proj", pieces=("proj0", "proj1"))
    for name in ("w_q", "w_kv", "w_o", "pool_w"):
        upd[name] = update(name)
    upd["w_gu"] = update("w_gu", FF_PARTS,
                         pieces=[f"gu{layer}_{qtr}" for layer in range(2) for qtr in range(FF_PARTS)])
    upd["w_gu"] = [jnp.swapaxes(a, 1, 2) for a in upd["w_gu"]]
    upd["w_down"] = update("w_down", FF_PARTS,
                           pieces=[f"wd{layer}_{qtr}" for layer in range(2) for qtr in range(FF_PARTS)])
    return grad_x, upd, slabs


def kernel(x, p, pre_mix_g, post_mix_g, pre_ffn_g, post_ffn_g, pool_w, pool_scale, kv_g, w_kv, w_q, sinks, w_o, w_gu, w_down, ple_g, w_ple_gate, w_ple_proj, ple_post_g, loss_target, m_pre_mix_g, m_post_mix_g, m_pre_ffn_g, m_post_ffn_g, m_pool_w, m_pool_scale, m_kv_g, m_w_kv, m_w_q, m_sinks, m_w_o, m_w_gu, m_w_down, m_ple_g, m_w_ple_gate, m_w_ple_proj, m_ple_post_g, v_pre_mix_g, v_post_mix_g, v_pre_ffn_g, v_post_ffn_g, v_pool_w, v_pool_scale, v_kv_g, v_w_kv, v_w_q, v_sinks, v_w_o, v_w_gu, v_w_down, v_ple_g, v_w_ple_gate, v_w_ple_proj, v_ple_post_g):
    shards = {"pool": pool_w[0].astype(BF16), "scale": pool_scale, "kv": w_kv.astype(BF16),
              "q": w_q[0].astype(BF16), "o": w_o[0].astype(BF16)}
    for layer in range(2):
        shards[f"gu{layer}"] = w_gu[layer].T.astype(BF16)
        shards[f"wd{layer}"] = w_down[layer].astype(BF16)
        for half in range(2):
            cols = (half * D_MODEL // 2, (half + 1) * D_MODEL // 2)
            shards[f"guh{layer}_{half}"] = (shards[f"gu{layer}"], cols)
            shards[f"wdh{layer}_{half}"] = (shards[f"wd{layer}"], cols)
        shards[f"gate{layer}"] = w_ple_gate[layer].astype(BF16)
        shards[f"proj{layer}"] = w_ple_proj[layer].astype(BF16)
    gains = jnp.concatenate([pre_mix_g, post_mix_g, pre_ffn_g, post_ffn_g, ple_g, ple_post_g, kv_g[None, :]],
                            axis=0).reshape(-1, 1, D_MODEL)
    weights = {"pool_w": (pool_w, m_pool_w, v_pool_w), "w_kv": (w_kv, m_w_kv, v_w_kv), "w_q": (w_q, m_w_q, v_w_q),
               "w_o": (w_o, m_w_o, v_w_o), "w_down": (w_down, m_w_down, v_w_down),
               "w_gu": tuple(jnp.swapaxes(a, 1, 2) for a in (w_gu, m_w_gu, v_w_gu)),
               "w_ple_gate": (w_ple_gate, m_w_ple_gate, v_w_ple_gate),
               "w_ple_proj": (w_ple_proj, m_w_ple_proj, v_w_ple_proj)}
    grad_x, upd, slabs = _local_step(x[0], p[:, 0], loss_target[0], gains, sinks, shards, weights)

    small_params = {
        "pre_mix_g": (pre_mix_g, m_pre_mix_g, v_pre_mix_g), "post_mix_g": (post_mix_g, m_post_mix_g, v_post_mix_g),
        "pre_ffn_g": (pre_ffn_g, m_pre_ffn_g, v_pre_ffn_g), "post_ffn_g": (post_ffn_g, m_post_ffn_g, v_post_ffn_g),
        "ple_g": (ple_g, m_ple_g, v_ple_g), "ple_post_g": (ple_post_g, m_ple_post_g, v_ple_post_g),
        "kv_g": (kv_g[None, :], m_kv_g[None, :], v_kv_g[None, :]),
        "pool_scale": (pool_scale, m_pool_scale, v_pool_scale), "sinks": (sinks, m_sinks, v_sinks)}
    loss, small_upd = _small_adamw(slabs, small_params)
    small_upd["kv_g"] = [a[0] for a in small_upd["kv_g"]]
    upd.update(small_upd)

    names = ["pre_mix_g", "post_mix_g", "pre_ffn_g", "post_ffn_g", "pool_w", "pool_scale", "kv_g", "w_kv", "w_q",
             "sinks", "w_o", "w_gu", "w_down", "ple_g", "w_ple_gate", "w_ple_proj", "ple_post_g"]
    outs = [loss[0, 0], grad_x[None]]
    for kind in range(4):
        outs += [upd[n][kind] for n in names]
    return tuple(outs)
```

```python
import functools
import types

import jax
import jax.numpy as jnp
from jax import lax
from jax.experimental import pallas as pl
from jax.experimental.pallas import tpu as pltpu

F32 = jnp.float32
BF16 = jnp.bfloat16

N_DEV = 8
D_MODEL = 1024
N_POOL_GROUPS = 4
POOL_GROUP = 256
POOL_HALO = 16
HEAD_DIM = 64
N_HEADS = 16
N_KV_HEADS = 4
GQA_GROUP = 4
KV_DIM = N_KV_HEADS * HEAD_DIM
ATT_BLOCK = 128
D_FF = 2816
FF_CHUNKS = 4
FF_BLOCK = D_FF // FF_CHUNKS
WD_ROWS = D_FF // N_DEV
FF_PARTS = 2
FF_PART = D_MODEL // FF_PARTS
N_CHIPS = 4
PLE_DIM = 256
EPS = 1e-6
NEG_INF = -1e30
ATT_SCALE = HEAD_DIM ** -0.5

ADAM_LR = 0.001
ADAM_B1 = 0.9
ADAM_B2 = 0.999
ADAM_EPS = 1e-08
ADAM_WD = 0.01
ADAM_STEP = 10

ROW_TILE = 512
FFN_ROW_TILE = 512
FFN_WEIGHT_COLS = 512
FFN_SUB_TILES = 1
VMEM_BIG = 60 * 1024 * 1024
VMEM_MID = 56 * 1024 * 1024
HBM_PIN_ELEMS = 1024

SV_ROWS = 16
SV_PRE_MIX, SV_POST_MIX, SV_PRE_FFN, SV_POST_FFN, SV_PLE, SV_PLE_POST = 0, 2, 4, 6, 8, 10
SV_KV, SV_POOL_SCALE, SV_SINKS, SV_LOSS = 12, 13, 14, 15

MESH = pl.DeviceIdType.MESH
ANY = pl.BlockSpec(memory_space=pl.ANY)


def _dot(a, b):
    return jnp.dot(a, b, preferred_element_type=F32)


def _dot_nt(a, b):
    return lax.dot_general(a, b, (((1,), (1,)), ((), ())), preferred_element_type=F32)


def _dot_tn(a, b):
    return lax.dot_general(a, b, (((0,), (0,)), ((), ())), preferred_element_type=F32)


def _rstd(x):
    return lax.rsqrt(jnp.mean(x * x, axis=-1, keepdims=True) + EPS)


def _rms(x, g):
    return x * _rstd(x) * g


def _rms_bwd(x, g, dy):
    r = _rstd(x)
    n = x * r
    dn = dy * g
    dx = r * (dn - n * jnp.mean(dn * n, axis=-1, keepdims=True))
    dg = jnp.sum(dy * n, axis=0, keepdims=True)
    return dx, dg


def _add_all(terms):
    return functools.reduce(jnp.add, terms)


def _sigmoid(x):
    return 1.0 / (1.0 + jnp.exp(-x))


def _acc(ref, val, first):
    @pl.when(first)
    def _():
        ref[...] = val

    @pl.when(jnp.logical_not(first))
    def _():
        ref[...] += val


def _pool_counts(row0, rows):
    t = row0 + lax.broadcasted_iota(jnp.int32, (rows, D_MODEL), 0) + 1
    grp = lax.broadcasted_iota(jnp.int32, (rows, D_MODEL), 1) // POOL_GROUP
    win = jnp.left_shift(2, grp)
    return jnp.minimum(t, win).astype(F32)


def _window_sums(ext, shift_of):
    outs = []
    s = ext
    for gi in range(N_POOL_GROUPS):
        s = s + pltpu.roll(s, shift_of(1 << gi), axis=0)
        outs.append(s[:, :POOL_GROUP])
        s = s[:, POOL_GROUP:]
    return jnp.concatenate(outs, axis=1)


def _cparams(n_axes, vmem, collective_id=None):
    return pltpu.CompilerParams(dimension_semantics=("arbitrary",) * n_axes, vmem_limit_bytes=vmem,
                                collective_id=collective_id)


_EVERYONE = ("sibling", "x", "y", "far", "x sibling", "y sibling", "far sibling")
_PEER_SETS = (("sibling", "x", "y"), ("sibling",), ("x", "y"), _EVERYONE)


def _meet(peers):
    x, y, c = lax.axis_index("x"), lax.axis_index("y"), lax.axis_index("c")
    device = {"sibling": (x, y, 1 - c), "x": (1 - x, y, c), "y": (x, 1 - y, c), "far": (1 - x, 1 - y, c),
              "x sibling": (1 - x, y, 1 - c), "y sibling": (x, 1 - y, 1 - c), "far sibling": (1 - x, 1 - y, 1 - c)}
    barrier = pltpu.get_barrier_semaphore()
    for peer in peers:
        pl.semaphore_signal(barrier, inc=1, device_id=device[peer], device_id_type=pl.DeviceIdType.MESH)
    pl.semaphore_wait(barrier, len(peers))


def _row_spec(cols, tm=ROW_TILE):
    return pl.BlockSpec((tm, cols), lambda i: (i, 0))


def _full_spec(shape):
    zeros = (0,) * len(shape)
    return pl.BlockSpec(shape, lambda *_: zeros)


def _vec_spec():
    return _full_spec((1, D_MODEL))


def _column_views(parts):
    return [(a, b) for a in parts for b in range(a.shape[-1] // FFN_WEIGHT_COLS)]


def _column_ranges(views):
    return [(n * FFN_WEIGHT_COLS, (n + 1) * FFN_WEIGHT_COLS) for n in range(len(views))]


class _Gain:
    def __init__(self, stacked, layer):
        self.stacked, self.layer = stacked, layer

    def spec(self):
        layer = self.layer
        return pl.BlockSpec((None, 1, D_MODEL), lambda *_: (layer, 0, 0))


def _in_hbm(a):
    return pltpu.with_memory_space_constraint(a, pltpu.HBM) if a.size >= HBM_PIN_ELEMS else a


def _out_in_hbm(s):
    return pltpu.HBM(s.shape, s.dtype) if s.size >= HBM_PIN_ELEMS else s


def _launch(body, *, name, grid, in_specs, out_specs, out_shape, args, scratch_shapes=(), vmem=VMEM_MID, job=None):
    in_specs = [a.spec() if isinstance(a, _Gain) else s for s, a in zip(in_specs, args)]
    args = [_in_hbm(a.stacked if isinstance(a, _Gain) else a) for a in args]
    n_in, n_out, n_scr = len(args), len(out_shape), len(scratch_shapes)
    if job is not None and not job.args:
        job = None
    j_args, j_out, j_scr = ([], [], []) if job is None else ([_in_hbm(a) for a in job.args], job.out_shape, job.scratch)

    def run(*refs):
        groups, at = [], 0
        for n in (n_in, len(j_args), n_out, len(j_out), n_scr, len(j_scr)):
            groups.append(refs[at:at + n])
            at += n
        ins, j_ins, outs, j_outs, scr, j_sems = groups

        def begin():
            _meet(job.peers)
            job.start(j_ins, j_outs, j_sems)

        if job is None:
            body(*ins, *outs, *scr)
        elif not grid:
            begin()
            job.mid(j_ins, j_outs, j_sems)
            body(*ins, *outs, *scr)
            job.finish(j_ins, j_outs, j_sems)
        else:
            ids = [pl.program_id(a) for a in range(len(grid))]
            first = functools.reduce(jnp.logical_and, [i == 0 for i in ids])
            half = functools.reduce(jnp.logical_and, [ids[0] == grid[0] // 2] + [i == 0 for i in ids[1:]])
            last = functools.reduce(jnp.logical_and, [i == g - 1 for i, g in zip(ids, grid)])
            pl.when(first)(begin)
            pl.when(half)(lambda: job.mid(j_ins, j_outs, j_sems))
            body(*ins, *outs, *scr)
            pl.when(last)(lambda: job.finish(j_ins, j_outs, j_sems))

    res = pl.pallas_call(
        run, name=name, grid=grid,
        in_specs=list(in_specs) + [ANY] * len(j_args), out_specs=list(out_specs) + [ANY] * len(j_out),
        out_shape=[_out_in_hbm(s) for s in list(out_shape) + list(j_out)],
        scratch_shapes=list(scratch_shapes) + list(j_scr),
        compiler_params=_cparams(len(grid), vmem, None if job is None else _PEER_SETS.index(job.peers)),
    )(*args, *j_args)
    return res[:n_out], res[n_out:]


def _fwd_pool(x, g_pre, job=None):
    T = x.shape[0]
    tm = ROW_TILE
    nt = T // tm

    def body(x_ref, gpre_ref, d_ref, carry):
        i = pl.program_id(0)

        @pl.when(i == 0)
        def _():
            carry[...] = jnp.zeros_like(carry)

        h = _rms(x_ref[...], gpre_ref[...])
        ext = jnp.concatenate([carry[...], h], axis=0)
        carry[...] = h[tm - POOL_HALO:, :]
        sums = _window_sums(ext, lambda k: k)[POOL_HALO:, :]
        d_ref[...] = (sums / _pool_counts(i * tm, tm) - h).astype(BF16)

    return _launch(
        body, name="fwd_pool", grid=(nt,), in_specs=[_row_spec(D_MODEL), _vec_spec()], out_specs=[_row_spec(D_MODEL)],
        out_shape=[jax.ShapeDtypeStruct((T, D_MODEL), BF16)], scratch_shapes=[pltpu.VMEM((POOL_HALO, D_MODEL), F32)],
        args=(x, g_pre), job=job)


def _fwd_pool_mixer(x, d, wp, scale, g_post, g_ffn, job=None):
    T = x.shape[0]
    nt = T // ROW_TILE

    def body(x_ref, d_ref, wp_ref, sc_ref, gpost_ref, gffn_ref, x1_ref, h2_ref, yraw_ref):
        db = d_ref[...]
        yraw = jnp.concatenate(
            [_dot(db[:, g * POOL_GROUP:(g + 1) * POOL_GROUP], wp_ref[g]) for g in range(N_POOL_GROUPS)], axis=1)
        yraw_ref[...] = yraw.astype(BF16)
        x1 = x_ref[...] + _rms(yraw * sc_ref[...], gpost_ref[...])
        x1_ref[...] = x1
        h2_ref[...] = _rms(x1, gffn_ref[...]).astype(BF16)

    return _launch(
        body, name="fwd_pool_mixer", grid=(nt,),
        in_specs=[_row_spec(D_MODEL), _row_spec(D_MODEL), _full_spec((N_POOL_GROUPS, POOL_GROUP, POOL_GROUP)),
                  _vec_spec(), _vec_spec(), _vec_spec()],
        out_specs=[_row_spec(D_MODEL)] * 3,
        out_shape=[jax.ShapeDtypeStruct((T, D_MODEL), F32)] + [jax.ShapeDtypeStruct((T, D_MODEL), BF16)] * 2,
        args=(x, d, wp, scale, g_post, g_ffn), job=job)


def _fwd_ffn(layer, h2, x1, wgu, wd, g_post, g_ple, job=None):
    T = h2.shape[0]
    tm = min(FFN_ROW_TILE, T)
    nt = T // tm
    sub = tm // FFN_SUB_TILES
    last = FF_CHUNKS - 1
    wgu, wd = _column_views(wgu), _column_views(wd)
    n_gu, n_wd = len(wgu), len(wd)
    gu_cols = _column_ranges(wgu)

    def body(h2_ref, x1_ref, *refs):
        wgu_refs, wd_refs = refs[:n_gu], refs[n_gu:n_gu + n_wd]
        gpost_ref, gple_ref, gs_ref, us_ref, f_ref, x2_ref, h3_ref, acc = refs[n_gu + n_wd:]
        k = pl.program_id(0)
        i = pl.program_id(1)
        rows = pl.ds(pl.multiple_of(i * tm, tm), tm)
        parts = []
        for s in range(FFN_SUB_TILES):
            r = pl.ds(s * sub, sub)
            g = _add_all([_dot_nt(h2_ref[r, c0:c1], w[0]) for (c0, c1), w in zip(gu_cols, wgu_refs)])
            u = _add_all([_dot_nt(h2_ref[r, c0:c1], w[1]) for (c0, c1), w in zip(gu_cols, wgu_refs)])
            gs_ref[r, :] = g.astype(BF16)
            us_ref[r, :] = u.astype(BF16)
            a = (g * _sigmoid(g) * u).astype(BF16)
            parts.append(jnp.concatenate([_dot(a, w[...]) for w in wd_refs], axis=1))
        part = jnp.concatenate(parts, axis=0)

        @pl.when(k == 0)
        def _():
            acc[rows, :] = part

        @pl.when(jnp.logical_and(k > 0, k < last))
        def _():
            acc[rows, :] += part

        @pl.when(k == last)
        def _():
            f = acc[rows, :] + part
            f_ref[...] = f.astype(BF16)
            x2 = x1_ref[...] + _rms(f, gpost_ref[...])
            x2_ref[...] = x2
            h3_ref[...] = _rms(x2, gple_ref[...]).astype(BF16)

    def late(k, i):
        return (jnp.where(k == last, i, 0), 0)

    return _launch(
        body, name=f"fwd_ffn{layer}", grid=(FF_CHUNKS, nt),
        in_specs=[pl.BlockSpec((tm, D_MODEL), lambda k, i: (i, 0)), pl.BlockSpec((tm, D_MODEL), late)]
                 + [pl.BlockSpec((None, 2, FF_BLOCK, FFN_WEIGHT_COLS), lambda k, i, b=b: (k, 0, 0, b)) for _, b in wgu]
                 + [pl.BlockSpec((FF_BLOCK, FFN_WEIGHT_COLS), lambda k, i, b=b: (k, b)) for _, b in wd]
                 + [pl.BlockSpec((1, D_MODEL), lambda k, i: (0, 0))] * 2,
        out_specs=[pl.BlockSpec((None, tm, FF_BLOCK), lambda k, i: (k, i, 0)),
                   pl.BlockSpec((None, tm, FF_BLOCK), lambda k, i: (k, i, 0)),
                   pl.BlockSpec((tm, D_MODEL), late),
                   pl.BlockSpec((tm, D_MODEL), late),
                   pl.BlockSpec((tm, D_MODEL), late)],
        out_shape=[jax.ShapeDtypeStruct((FF_CHUNKS, T, FF_BLOCK), BF16),
                   jax.ShapeDtypeStruct((FF_CHUNKS, T, FF_BLOCK), BF16),
                   jax.ShapeDtypeStruct((T, D_MODEL), BF16),
                   jax.ShapeDtypeStruct((T, D_MODEL), F32),
                   jax.ShapeDtypeStruct((T, D_MODEL), BF16)],
        scratch_shapes=[pltpu.VMEM((T, D_MODEL), F32)],
        args=(h2, x1, *[w for w, _ in wgu], *[w for w, _ in wd], g_post, g_ple), vmem=VMEM_BIG, job=job)


def _fwd_ple_qkv(x2, h3, p, wgate, wproj, g_post, g_kv, g_mix, wkv, wq, job=None):
    T = x2.shape[0]
    nt = T // ROW_TILE

    def body(x2_ref, h3_ref, p_ref, wg_ref, wp_ref, gpost_ref, gkv_ref, gmix_ref, wkv_ref, wq_ref,
             x3_ref, z_ref, pe_ref, hk_ref, h1_ref, q_ref, kv_ref):
        z = _dot(h3_ref[...], wg_ref[...])
        pe = _dot(p_ref[...].astype(BF16), wp_ref[...])
        z_ref[...] = z.astype(BF16)
        pe_ref[...] = pe.astype(BF16)
        x3 = x2_ref[...] + _rms(pe * _sigmoid(z), gpost_ref[...])
        x3_ref[...] = x3
        r = _rstd(x3)
        hk = (x3 * r * gkv_ref[...]).astype(BF16)
        h1 = (x3 * r * gmix_ref[...]).astype(BF16)
        hk_ref[...] = hk
        h1_ref[...] = h1
        kv_ref[...] = _dot(hk, wkv_ref[...]).astype(BF16)
        q_ref[...] = _dot(h1, wq_ref[...]).astype(BF16)

    wide = jax.ShapeDtypeStruct((T, D_MODEL), BF16)
    return _launch(
        body, name="fwd_ple_qkv", grid=(nt,),
        in_specs=[_row_spec(D_MODEL), _row_spec(D_MODEL), _row_spec(PLE_DIM), _full_spec((D_MODEL, D_MODEL)),
                  _full_spec((PLE_DIM, D_MODEL)), _vec_spec(), _vec_spec(), _vec_spec(),
                  _full_spec((D_MODEL, 2 * KV_DIM)), _full_spec((D_MODEL, D_MODEL))],
        out_specs=[_row_spec(D_MODEL)] * 6 + [_row_spec(2 * KV_DIM)],
        out_shape=[jax.ShapeDtypeStruct((T, D_MODEL), F32)] + [wide] * 5 + [jax.ShapeDtypeStruct((T, 2 * KV_DIM), BF16)],
        args=(x2, h3, p, wgate, wproj, g_post, g_kv, g_mix, wkv, wq), job=job)


def _alibi_slope(h):
    return 2.0 ** (-8.0 * (h + 1) / N_HEADS)


ATT_SUB = 32
ATT_GROUP_ROWS = GQA_GROUP * ATT_BLOCK


def _att_mask(n, rel_ref, off_ref):
    qi = lax.broadcasted_iota(jnp.int32, (ATT_BLOCK, 2 * ATT_BLOCK), 0)
    si = lax.broadcasted_iota(jnp.int32, (ATT_BLOCK, 2 * ATT_BLOCK), 1)
    rel = ATT_BLOCK + qi - si
    valid = (rel >= 0) & (rel < ATT_BLOCK) & ((si >= ATT_BLOCK) | (n > 0))
    rel_ref[...] = rel.astype(F32)
    off_ref[...] = jnp.where(valid, 0.0, NEG_INF)


def _att_probs(raw, relf, off, slope, sink):
    s = raw * ATT_SCALE - slope * relf + off
    m = jnp.maximum(jnp.max(s, axis=-1, keepdims=True), sink)
    e = jnp.exp(s - m)
    es = jnp.exp(sink - m)
    inv = 1.0 / (jnp.sum(e, axis=-1, keepdims=True) + es)
    return e * inv, es * inv


def _stack_heads(ref, kh):
    first = kh * GQA_GROUP
    return jnp.concatenate([ref[:, (first + g) * HEAD_DIM:(first + g + 1) * HEAD_DIM] for g in range(GQA_GROUP)], axis=0)


def _unstack_heads(stacked):
    return [stacked[g * ATT_BLOCK:(g + 1) * ATT_BLOCK, :] for g in range(GQA_GROUP)]


def _fwd_attention(q, kpad, vpad, sinks, job=None):
    T = q.shape[0]
    nb = T // ATT_BLOCK

    def body(q_ref, k_ref, v_ref, sink_ref, o_ref, s_scr, p_scr, rel_scr, off_scr):
        n = pl.program_id(0)
        start = pl.multiple_of(n * ATT_BLOCK, ATT_BLOCK)
        kw = k_ref[pl.ds(start, 2 * ATT_BLOCK), :]
        vw = v_ref[pl.ds(start, 2 * ATT_BLOCK), :]
        _att_mask(n, rel_scr, off_scr)
        outs = []
        for kh in range(N_KV_HEADS):
            kk = kw[:, kh * HEAD_DIM:(kh + 1) * HEAD_DIM]
            vv = vw[:, kh * HEAD_DIM:(kh + 1) * HEAD_DIM]
            s_scr[...] = _dot_nt(_stack_heads(q_ref, kh), kk)
            for g in range(GQA_GROUP):
                h = kh * GQA_GROUP + g
                for row0 in range(0, ATT_BLOCK, ATT_SUB):
                    rows, sub = pl.ds(g * ATT_BLOCK + row0, ATT_SUB), pl.ds(row0, ATT_SUB)
                    pr, _ = _att_probs(s_scr[rows, :], rel_scr[sub, :], off_scr[sub, :], _alibi_slope(h),
                                       sink_ref[0, h])
                    p_scr[rows, :] = pr.astype(BF16)
            outs += _unstack_heads(_dot(p_scr[...], vv))
        o_ref[...] = jnp.concatenate(outs, axis=1).astype(BF16)

    return _launch(
        body, name="fwd_attention", grid=(nb,),
        in_specs=[_row_spec(D_MODEL, ATT_BLOCK), _full_spec((T + ATT_BLOCK, KV_DIM)), _full_spec((T + ATT_BLOCK, KV_DIM)),
                  pl.BlockSpec(memory_space=pltpu.SMEM)],
        out_specs=[_row_spec(D_MODEL, ATT_BLOCK)],
        out_shape=[jax.ShapeDtypeStruct((T, D_MODEL), BF16)],
        scratch_shapes=[pltpu.VMEM((ATT_GROUP_ROWS, 2 * ATT_BLOCK), F32), pltpu.VMEM((ATT_GROUP_ROWS, 2 * ATT_BLOCK), BF16)]
                       + [pltpu.VMEM((ATT_BLOCK, 2 * ATT_BLOCK), F32)] * 2,
        args=(q, kpad, vpad, sinks), job=job)


def _fwd_attn_out(attn, x, wo, g_post, g_ffn, job=None):
    T = x.shape[0]
    nt = T // ROW_TILE

    def body(a_ref, x_ref, wo_ref, gpost_ref, gffn_ref, y_ref, x1_ref, h2_ref):
        y = _dot(a_ref[...], wo_ref[...])
        y_ref[...] = y.astype(BF16)
        x1 = x_ref[...] + _rms(y, gpost_ref[...])
        x1_ref[...] = x1
        h2_ref[...] = _rms(x1, gffn_ref[...]).astype(BF16)

    return _launch(
        body, name="fwd_attn_out", grid=(nt,),
        in_specs=[_row_spec(D_MODEL), _row_spec(D_MODEL), _full_spec((D_MODEL, D_MODEL)), _vec_spec(), _vec_spec()],
        out_specs=[_row_spec(D_MODEL)] * 3,
        out_shape=[jax.ShapeDtypeStruct((T, D_MODEL), BF16), jax.ShapeDtypeStruct((T, D_MODEL), F32),
                   jax.ShapeDtypeStruct((T, D_MODEL), BF16)],
        args=(attn, x, wo, g_post, g_ffn), job=job)


def _bwd_ple(layer, dx3, x2, z, pe, h3, p, f, wgate, g_ple_post, g_ple, g_post_ffn, job=None):
    T = x2.shape[0]
    tm = ROW_TILE
    nt = T // tm

    def body(dx3_ref, x2_ref, z_ref, pe_ref, h3_ref, p_ref, f_ref, wg_ref, gpp_ref, gp_ref, gpf_ref,
             dx2_ref, df_ref, dwg_ref, dwp_ref, dgpp_ref, dgp_ref, dgpf_ref, acc_g, acc_p):
        i = pl.program_id(0)
        first = i == 0
        dx3v = dx3_ref[...]
        gate = _sigmoid(z_ref[...].astype(F32))
        pev = pe_ref[...].astype(F32)
        de, dgpp = _rms_bwd(pev * gate, gpp_ref[...], dx3v)
        dpe = (de * gate).astype(BF16)
        dz = (de * pev * gate * (1.0 - gate)).astype(BF16)
        _acc(acc_p, _dot_tn(p_ref[...].astype(BF16), dpe), first)
        _acc(acc_g, _dot_tn(h3_ref[...], dz), first)
        dh3 = _dot_nt(dz, wg_ref[...])
        dxn, dgp = _rms_bwd(x2_ref[...], gp_ref[...], dh3)
        dx2 = dx3v + dxn
        dx2_ref[...] = dx2
        df, dgpf = _rms_bwd(f_ref[...].astype(F32), gpf_ref[...], dx2)
        df_ref[...] = df.astype(BF16)
        _acc(dgpp_ref, dgpp, first)
        _acc(dgp_ref, dgp, first)
        _acc(dgpf_ref, dgpf, first)

        @pl.when(i == nt - 1)
        def _():
            dwg_ref[...] = acc_g[...].astype(BF16)
            dwp_ref[...] = acc_p[...].astype(BF16)

    return _launch(
        body, name=f"bwd_ple{layer}", grid=(nt,),
        in_specs=[_row_spec(D_MODEL)] * 5 + [_row_spec(PLE_DIM), _row_spec(D_MODEL), _full_spec((D_MODEL, D_MODEL)),
                  _vec_spec(), _vec_spec(), _vec_spec()],
        out_specs=[_row_spec(D_MODEL), _row_spec(D_MODEL), _full_spec((D_MODEL, D_MODEL)), _full_spec((PLE_DIM, D_MODEL)),
                   _vec_spec(), _vec_spec(), _vec_spec()],
        out_shape=[jax.ShapeDtypeStruct((T, D_MODEL), F32), jax.ShapeDtypeStruct((T, D_MODEL), BF16),
                   jax.ShapeDtypeStruct((D_MODEL, D_MODEL), BF16), jax.ShapeDtypeStruct((PLE_DIM, D_MODEL), BF16)]
                  + [jax.ShapeDtypeStruct((1, D_MODEL), F32)] * 3,
        scratch_shapes=[pltpu.VMEM((D_MODEL, D_MODEL), F32), pltpu.VMEM((PLE_DIM, D_MODEL), F32)],
        args=(dx3, x2, z, pe, h3, p, f, wgate, g_ple_post, g_ple, g_post_ffn), vmem=VMEM_BIG, job=job)


def _ple_loss_bwd(layer, x2, h3, p, f, target, wgate, wproj, g_ple_post, g_ple, g_post_ffn, job=None):
    T = x2.shape[0]
    tm = ROW_TILE
    nt = T // tm

    def body(x2_ref, h3_ref, p_ref, f_ref, tgt_ref, wg_ref, wp_ref, gpp_ref, gp_ref, gpf_ref,
             dx2_ref, df_ref, dwg_ref, dwp_ref, dgpp_ref, dgp_ref, dgpf_ref, loss_ref, acc_g, acc_p):
        i = pl.program_id(0)
        first = i == 0
        h3 = h3_ref[...]
        pb = p_ref[...].astype(BF16)
        x2v = x2_ref[...]
        gate = _sigmoid(_dot(h3, wg_ref[...]))
        pev = _dot(pb, wp_ref[...])
        e = pev * gate
        err = x2v + _rms(e, gpp_ref[...]) - tgt_ref[...]
        _acc(loss_ref, 0.5 * jnp.sum(jnp.mean(err * err, axis=-1, keepdims=True), axis=0, keepdims=True), first)
        dx3v = err * (1.0 / D_MODEL)
        de, dgpp = _rms_bwd(e, gpp_ref[...], dx3v)
        dpe = (de * gate).astype(BF16)
        dz = (de * pev * gate * (1.0 - gate)).astype(BF16)
        _acc(acc_p, _dot_tn(pb, dpe), first)
        _acc(acc_g, _dot_tn(h3, dz), first)
        dxn, dgp = _rms_bwd(x2v, gp_ref[...], _dot_nt(dz, wg_ref[...]))
        dx2 = dx3v + dxn
        dx2_ref[...] = dx2
        df, dgpf = _rms_bwd(f_ref[...].astype(F32), gpf_ref[...], dx2)
        df_ref[...] = df.astype(BF16)
        _acc(dgpp_ref, dgpp, first)
        _acc(dgp_ref, dgp, first)
        _acc(dgpf_ref, dgpf, first)

        @pl.when(i == nt - 1)
        def _():
            dwg_ref[...] = acc_g[...].astype(BF16)
            dwp_ref[...] = acc_p[...].astype(BF16)

    return _launch(
        body, name=f"ple_loss_bwd{layer}", grid=(nt,),
        in_specs=[_row_spec(D_MODEL), _row_spec(D_MODEL), _row_spec(PLE_DIM), _row_spec(D_MODEL), _row_spec(D_MODEL),
                  _full_spec((D_MODEL, D_MODEL)), _full_spec((PLE_DIM, D_MODEL)), _vec_spec(), _vec_spec(), _vec_spec()],
        out_specs=[_row_spec(D_MODEL), _row_spec(D_MODEL), _full_spec((D_MODEL, D_MODEL)), _full_spec((PLE_DIM, D_MODEL)),
                   _vec_spec(), _vec_spec(), _vec_spec(), _full_spec((1, 1))],
        out_shape=[jax.ShapeDtypeStruct((T, D_MODEL), F32), jax.ShapeDtypeStruct((T, D_MODEL), BF16),
                   jax.ShapeDtypeStruct((D_MODEL, D_MODEL), BF16), jax.ShapeDtypeStruct((PLE_DIM, D_MODEL), BF16)]
                  + [jax.ShapeDtypeStruct((1, D_MODEL), F32)] * 3 + [jax.ShapeDtypeStruct((1, 1), F32)],
        scratch_shapes=[pltpu.VMEM((D_MODEL, D_MODEL), F32), pltpu.VMEM((PLE_DIM, D_MODEL), F32)],
        args=(x2, h3, p, f, target, wgate, wproj, g_ple_post, g_ple, g_post_ffn), vmem=VMEM_BIG, job=job)


def _bwd_ffn_act(layer, df, gs, us, wgu, wd, job=None):
    T = df.shape[0]
    tm = min(FFN_ROW_TILE, T)
    nt = T // tm
    sub = tm // FFN_SUB_TILES
    last = FF_CHUNKS - 1
    wgu, wd = _column_views(wgu), _column_views(wd)
    n_gu, n_wd = len(wgu), len(wd)
    wd_cols = _column_ranges(wd)

    def body(df_ref, gs_ref, us_ref, *refs):
        wgu_refs, wd_refs = refs[:n_gu], refs[n_gu:n_gu + n_wd]
        dh_ref, dg_ref, du_ref, a_ref, acc_h = refs[n_gu + n_wd:]
        k = pl.program_id(0)
        i = pl.program_id(1)
        rows = pl.ds(pl.multiple_of(i * tm, tm), tm)
        dhs = []
        for s in range(FFN_SUB_TILES):
            r = pl.ds(s * sub, sub)
            g = gs_ref[r, :].astype(F32)
            u = us_ref[r, :].astype(F32)
            sg = _sigmoid(g)
            silu = g * sg
            a_ref[r, :] = (silu * u).astype(BF16)
            da = _add_all([_dot_nt(df_ref[r, c0:c1], w[...]) for (c0, c1), w in zip(wd_cols, wd_refs)])
            dg = (da * u * (sg * (1.0 + g * (1.0 - sg)))).astype(BF16)
            du = (da * silu).astype(BF16)
            dg_ref[r, :] = dg
            du_ref[r, :] = du
            dhs.append(jnp.concatenate([_dot(dg, w[0]) + _dot(du, w[1]) for w in wgu_refs], axis=1))
        dh = jnp.concatenate(dhs, axis=0)

        @pl.when(k == 0)
        def _():
            acc_h[rows, :] = dh

        @pl.when(jnp.logical_and(k > 0, k < last))
        def _():
            acc_h[rows, :] += dh

        @pl.when(k == last)
        def _():
            dh_ref[...] = acc_h[rows, :] + dh

    chunk_rows = pl.BlockSpec((None, tm, FF_BLOCK), lambda k, i: (k, i, 0))
    saved = jax.ShapeDtypeStruct((FF_CHUNKS, T, FF_BLOCK), BF16)
    return _launch(
        body, name=f"bwd_ffn_act{layer}", grid=(FF_CHUNKS, nt),
        in_specs=[pl.BlockSpec((tm, D_MODEL), lambda k, i: (i, 0)), chunk_rows, chunk_rows]
                 + [pl.BlockSpec((None, 2, FF_BLOCK, FFN_WEIGHT_COLS), lambda k, i, b=b: (k, 0, 0, b)) for _, b in wgu]
                 + [pl.BlockSpec((FF_BLOCK, FFN_WEIGHT_COLS), lambda k, i, b=b: (k, b)) for _, b in wd],
        out_specs=[pl.BlockSpec((tm, D_MODEL), lambda k, i: (jnp.where(k == last, i, 0), 0)),
                   chunk_rows, chunk_rows, chunk_rows],
        out_shape=[jax.ShapeDtypeStruct((T, D_MODEL), F32), saved, saved, saved],
        scratch_shapes=[pltpu.VMEM((T, D_MODEL), F32)],
        args=(df, gs, us, *[w for w, _ in wgu], *[w for w, _ in wd]), vmem=VMEM_BIG, job=job)


def _bwd_ffn_dw(layer, q, parts, h2, df, dg, du, a, job=None):
    T = h2.shape[0]
    width = D_MODEL // parts

    def body(h_ref, df_ref, dg_ref, du_ref, a_ref, dgu_ref, dwd_ref):
        h = h_ref[...]
        dgu_ref[0] = _dot_tn(dg_ref[...], h).astype(BF16)
        dgu_ref[1] = _dot_tn(du_ref[...], h).astype(BF16)
        dwd_ref[...] = _dot_tn(a_ref[...], df_ref[...]).astype(BF16)

    cols = pl.BlockSpec((T, width), lambda k: (0, q))
    chunk = pl.BlockSpec((None, T, FF_BLOCK), lambda k: (k, 0, 0))
    return _launch(
        body, name=f"bwd_ffn_dw{layer}_{q}", grid=(FF_CHUNKS,),
        in_specs=[cols, cols, chunk, chunk, chunk],
        out_specs=[pl.BlockSpec((None, 2, FF_BLOCK, width), lambda k: (k, 0, 0, 0)),
                   pl.BlockSpec((FF_BLOCK, width), lambda k: (k, 0))],
        out_shape=[jax.ShapeDtypeStruct((FF_CHUNKS, 2, FF_BLOCK, width), BF16),
                   jax.ShapeDtypeStruct((D_FF, width), BF16)],
        args=(h2, df, dg, du, a), vmem=VMEM_BIG, job=job)


def _bwd_attn_out(dx2, dh2, x1, y, attn, wo, g_ffn, g_post, job=None):
    T = x1.shape[0]
    nt = T // ROW_TILE

    def body(dx2_ref, dh2_ref, x1_ref, y_ref, a_ref, wo_ref, gffn_ref, gpost_ref,
             dx1_ref, da_ref, dwo_ref, dgf_ref, dgp_ref, acc):
        i = pl.program_id(0)
        first = i == 0
        dxn, dgf = _rms_bwd(x1_ref[...], gffn_ref[...], dh2_ref[...])
        dx1 = dx2_ref[...] + dxn
        dx1_ref[...] = dx1
        dy, dgp = _rms_bwd(y_ref[...].astype(F32), gpost_ref[...], dx1)
        dyb = dy.astype(BF16)
        da_ref[...] = _dot_nt(dyb, wo_ref[...]).astype(BF16)
        _acc(acc, _dot_tn(a_ref[...], dyb), first)
        _acc(dgf_ref, dgf, first)
        _acc(dgp_ref, dgp, first)

        @pl.when(i == nt - 1)
        def _():
            dwo_ref[...] = acc[...].astype(BF16)

    return _launch(
        body, name="bwd_attn_out", grid=(nt,),
        in_specs=[_row_spec(D_MODEL)] * 5 + [_full_spec((D_MODEL, D_MODEL)), _vec_spec(), _vec_spec()],
        out_specs=[_row_spec(D_MODEL), _row_spec(D_MODEL), _full_spec((D_MODEL, D_MODEL)), _vec_spec(), _vec_spec()],
        out_shape=[jax.ShapeDtypeStruct((T, D_MODEL), F32), jax.ShapeDtypeStruct((T, D_MODEL), BF16),
                   jax.ShapeDtypeStruct((D_MODEL, D_MODEL), BF16)] + [jax.ShapeDtypeStruct((1, D_MODEL), F32)] * 2,
        scratch_shapes=[pltpu.VMEM((D_MODEL, D_MODEL), F32)],
        args=(dx2, dh2, x1, y, attn, wo, g_ffn, g_post), job=job)


def _bwd_attention(q, dattn, kpad, vpad, sinks, job=None):
    T = q.shape[0]
    nb = T // ATT_BLOCK

    def body(q_ref, do_ref, k_ref, v_ref, sink_ref, dq_ref, dk_ref, dv_ref, ds_ref, s_scr, dp_scr, p_scr, dsb_scr,
             rel_scr, off_scr):
        n = pl.program_id(0)
        _att_mask(n, rel_scr, off_scr)

        @pl.when(n == 0)
        def _():
            dk_ref[...] = jnp.zeros_like(dk_ref)
            dv_ref[...] = jnp.zeros_like(dv_ref)
            ds_ref[...] = jnp.zeros_like(ds_ref)

        start = pl.multiple_of(n * ATT_BLOCK, ATT_BLOCK)
        win = pl.ds(start, 2 * ATT_BLOCK)
        kw = k_ref[win, :]
        vw = v_ref[win, :]
        lane = lax.broadcasted_iota(jnp.int32, (1, ATT_BLOCK), 1)
        dsink = jnp.zeros((1, ATT_BLOCK), F32)
        dqs, dks, dvs = [], [], []
        for kh in range(N_KV_HEADS):
            kk = kw[:, kh * HEAD_DIM:(kh + 1) * HEAD_DIM]
            vv = vw[:, kh * HEAD_DIM:(kh + 1) * HEAD_DIM]
            qs = _stack_heads(q_ref, kh)
            dos = _stack_heads(do_ref, kh)
            s_scr[...] = _dot_nt(qs, kk)
            dp_scr[...] = _dot_nt(dos, vv)
            for g in range(GQA_GROUP):
                h = kh * GQA_GROUP + g
                dsink_h = jnp.zeros((1, 1), F32)
                for row0 in range(0, ATT_BLOCK, ATT_SUB):
                    rows, sub = pl.ds(g * ATT_BLOCK + row0, ATT_SUB), pl.ds(row0, ATT_SUB)
                    pr, ps = _att_probs(s_scr[rows, :], rel_scr[sub, :], off_scr[sub, :], _alibi_slope(h),
                                        sink_ref[0, h])
                    dp = dp_scr[rows, :]
                    delta = jnp.sum(pr * dp, axis=-1, keepdims=True)
                    dsb_scr[rows, :] = (pr * (dp - delta) * ATT_SCALE).astype(BF16)
                    p_scr[rows, :] = pr.astype(BF16)
                    dsink_h = dsink_h - jnp.sum(ps * delta, axis=0, keepdims=True)
                dsink = dsink + jnp.where(lane == h, dsink_h, 0.0)
            dsb = dsb_scr[...]
            dqs += _unstack_heads(_dot(dsb, kk))
            dks.append(_dot_tn(dsb, qs))
            dvs.append(_dot_tn(p_scr[...], dos))
        dq_ref[...] = jnp.concatenate(dqs, axis=1).astype(BF16)
        dk_ref[win, :] += jnp.concatenate(dks, axis=1)
        dv_ref[win, :] += jnp.concatenate(dvs, axis=1)
        ds_ref[...] += dsink

    return _launch(
        body, name="bwd_attention", grid=(nb,),
        in_specs=[_row_spec(D_MODEL, ATT_BLOCK), _row_spec(D_MODEL, ATT_BLOCK), _full_spec((T + ATT_BLOCK, KV_DIM)),
                  _full_spec((T + ATT_BLOCK, KV_DIM)), pl.BlockSpec(memory_space=pltpu.SMEM)],
        out_specs=[_row_spec(D_MODEL, ATT_BLOCK), _full_spec((T + ATT_BLOCK, KV_DIM)), _full_spec((T + ATT_BLOCK, KV_DIM)),
                   _full_spec((1, ATT_BLOCK))],
        out_shape=[jax.ShapeDtypeStruct((T, D_MODEL), BF16), jax.ShapeDtypeStruct((T + ATT_BLOCK, KV_DIM), F32),
                   jax.ShapeDtypeStruct((T + ATT_BLOCK, KV_DIM), F32), jax.ShapeDtypeStruct((1, ATT_BLOCK), F32)],
        scratch_shapes=[pltpu.VMEM((ATT_GROUP_ROWS, 2 * ATT_BLOCK), F32)] * 2
                       + [pltpu.VMEM((ATT_GROUP_ROWS, 2 * ATT_BLOCK), BF16)] * 2
                       + [pltpu.VMEM((ATT_BLOCK, 2 * ATT_BLOCK), F32)] * 2,
        args=(q, dattn, kpad, vpad, sinks), vmem=VMEM_BIG, job=job)


def _bwd_qkv(dxres, dq, dkv, x3, h1, hk, wq, wkv, g_mix, g_kv, job=None):
    T = x3.shape[0]
    nt = T // ROW_TILE

    def body(dxr_ref, dq_ref, dkv_ref, x_ref, h1_ref, hk_ref, wq_ref, wkv_ref, gmix_ref, gkv_ref,
             dx_ref, dwq_ref, dwkv_ref, dgm_ref, dgk_ref, acc_q, acc_kv):
        i = pl.program_id(0)
        first = i == 0
        dqv = dq_ref[...]
        dkvv = dkv_ref[...]
        xv = x_ref[...]
        d1, dgm = _rms_bwd(xv, gmix_ref[...], _dot_nt(dqv, wq_ref[...]))
        d2, dgk = _rms_bwd(xv, gkv_ref[...], _dot_nt(dkvv, wkv_ref[...]))
        dx_ref[...] = dxr_ref[...] + d1 + d2
        _acc(acc_q, _dot_tn(h1_ref[...], dqv), first)
        _acc(acc_kv, _dot_tn(hk_ref[...], dkvv), first)
        _acc(dgm_ref, dgm, first)
        _acc(dgk_ref, dgk, first)

        @pl.when(i == nt - 1)
        def _():
            dwq_ref[...] = acc_q[...].astype(BF16)
            dwkv_ref[...] = acc_kv[...].astype(BF16)

    return _launch(
        body, name="bwd_qkv", grid=(nt,),
        in_specs=[_row_spec(D_MODEL), _row_spec(D_MODEL), _row_spec(2 * KV_DIM), _row_spec(D_MODEL), _row_spec(D_MODEL),
                  _row_spec(D_MODEL), _full_spec((D_MODEL, D_MODEL)), _full_spec((D_MODEL, 2 * KV_DIM)), _vec_spec(),
                  _vec_spec()],
        out_specs=[_row_spec(D_MODEL), _full_spec((D_MODEL, D_MODEL)), _full_spec((D_MODEL, 2 * KV_DIM)), _vec_spec(),
                   _vec_spec()],
        out_shape=[jax.ShapeDtypeStruct((T, D_MODEL), F32), jax.ShapeDtypeStruct((D_MODEL, D_MODEL), BF16),
                   jax.ShapeDtypeStruct((D_MODEL, 2 * KV_DIM), BF16)] + [jax.ShapeDtypeStruct((1, D_MODEL), F32)] * 2,
        scratch_shapes=[pltpu.VMEM((D_MODEL, D_MODEL), F32), pltpu.VMEM((D_MODEL, 2 * KV_DIM), F32)],
        args=(dxres, dq, dkv, x3, h1, hk, wq, wkv, g_mix, g_kv), job=job)


def _bwd_pool_mixer(dx2, dh2, x1, x, yraw, d, wp, scale, g_ffn, g_post, g_pre, job=None):
    T = x.shape[0]
    tm = ROW_TILE
    nt = T // tm

    def body(dx2_ref, dh2_ref, x1_ref, x_ref, yraw_ref, d_ref, wp_ref, sc_ref, gffn_ref, gpost_ref, gpre_ref,
             dx_ref, dwp_ref, dsc_ref, dgf_ref, dgp_ref, dgm_ref, carry, acc):
        i = pl.program_id(0)
        first = i == 0
        tile = nt - 1 - i

        @pl.when(first)
        def _():
            carry[...] = jnp.zeros_like(carry)

        dxn, dgf = _rms_bwd(x1_ref[...], gffn_ref[...], dh2_ref[...])
        dx1 = dx2_ref[...] + dxn
        yraw = yraw_ref[...].astype(F32)
        sc = sc_ref[...]
        dy, dgp = _rms_bwd(yraw * sc, gpost_ref[...], dx1)
        dsc = jnp.sum(dy * yraw, axis=0, keepdims=True)
        dyb = (dy * sc).astype(BF16)
        dv = d_ref[...]
        dds = []
        for g in range(N_POOL_GROUPS):
            cols = slice(g * POOL_GROUP, (g + 1) * POOL_GROUP)
            dds.append(_dot_nt(dyb[:, cols], wp_ref[g]))
            _acc(acc.at[g], _dot_tn(dv[:, cols], dyb[:, cols]), first)
        dd = jnp.concatenate(dds, axis=1)
        e = dd / _pool_counts(tile * tm, tm)
        ext = jnp.concatenate([e, carry[...]], axis=0)
        carry[...] = e[:POOL_HALO, :]
        sums = _window_sums(ext, lambda k: tm + POOL_HALO - k)[:tm, :]
        dxm, dgm = _rms_bwd(x_ref[...], gpre_ref[...], sums - dd)
        dx_ref[...] = dx1 + dxm
        _acc(dsc_ref, dsc, first)
        _acc(dgf_ref, dgf, first)
        _acc(dgp_ref, dgp, first)
        _acc(dgm_ref, dgm, first)

        @pl.when(i == nt - 1)
        def _():
            dwp_ref[...] = acc[...].astype(BF16)

    rev = pl.BlockSpec((tm, D_MODEL), lambda i: (nt - 1 - i, 0))
    return _launch(
        body, name="bwd_pool_mixer", grid=(nt,),
        in_specs=[rev] * 6 + [_full_spec((N_POOL_GROUPS, POOL_GROUP, POOL_GROUP))] + [_vec_spec()] * 4,
        out_specs=[rev, _full_spec((N_POOL_GROUPS, POOL_GROUP, POOL_GROUP))] + [_vec_spec()] * 4,
        out_shape=[jax.ShapeDtypeStruct((T, D_MODEL), F32),
                   jax.ShapeDtypeStruct((N_POOL_GROUPS, POOL_GROUP, POOL_GROUP), BF16)]
                  + [jax.ShapeDtypeStruct((1, D_MODEL), F32)] * 4,
        scratch_shapes=[pltpu.VMEM((POOL_HALO, D_MODEL), F32), pltpu.VMEM((N_POOL_GROUPS, POOL_GROUP, POOL_GROUP), F32)],
        args=(dx2, dh2, x1, x, yraw, d, wp, scale, g_ffn, g_post, g_pre), job=job)


def _my_place():
    return lax.axis_index("x"), lax.axis_index("y"), lax.axis_index("c")


def _dev_index(px, py, pc):
    return 4 * px + 2 * py + pc


def _peer_by_relation(r):
    x, y, c = _my_place()
    return (x ^ ((r >> 2) & 1), y ^ ((r >> 1) & 1), c ^ (r & 1))


def _slot_pool(ref, j):
    return ref.at[:, pl.ds(pl.multiple_of(j * 32, 32), 32), :]


def _slot_scale(ref, j):
    return ref.at[:, pl.ds(pl.multiple_of(j * 128, 128), 128)]


def _slot_rows128(ref, j):
    return ref.at[pl.ds(pl.multiple_of(j * 128, 128), 128), :]


def _slot_gu(ref, j):
    return ref.at[j % FF_CHUNKS, j // FF_CHUNKS]


def _slot_wd(ref, j):
    return ref.at[pl.ds(pl.multiple_of(j * WD_ROWS, 16), WD_ROWS), :]


def _slot_cols128(ref, j):
    return ref.at[:, pl.ds(pl.multiple_of(j * 128, 128), 128)]


_GATHERED = {
    "pool": ((N_POOL_GROUPS, POOL_GROUP, POOL_GROUP), BF16, _slot_pool),
    "scale": ((1, D_MODEL), F32, _slot_scale),
    "kv": ((D_MODEL, 2 * KV_DIM), BF16, _slot_rows128),
    "q": ((D_MODEL, D_MODEL), BF16, _slot_rows128),
    "o": ((D_MODEL, D_MODEL), BF16, _slot_rows128),
    "gu": ((FF_CHUNKS, 2, FF_BLOCK, D_MODEL), BF16, _slot_gu),
    "wd": ((D_FF, D_MODEL), BF16, _slot_wd),
    "guh": ((FF_CHUNKS, 2, FF_BLOCK, D_MODEL // 2), BF16, _slot_gu),
    "wdh": ((D_FF, D_MODEL // 2), BF16, _slot_wd),
    "gate": ((D_MODEL, D_MODEL), BF16, _slot_rows128),
    "proj": ((PLE_DIM, D_MODEL), BF16, _slot_cols128),
}


def _no_compute():
    pass


class _AllGather:
    peers = ("sibling", "x", "y")

    def __init__(self, names, shards):
        self.kinds = [_GATHERED[n.rstrip("01_")] for n in names]
        entries = [shards[n] if isinstance(shards[n], tuple) else (shards[n], None) for n in names]
        self.args = [array for array, _ in entries]
        self.columns = [columns for _, columns in entries]
        self.out_shape = [jax.ShapeDtypeStruct(shape, dtype) for shape, dtype, _ in self.kinds]
        n = len(names)
        self.scratch = [pltpu.SemaphoreType.DMA((n, 7)), pltpu.SemaphoreType.DMA((n, 7)), pltpu.SemaphoreType.DMA((n,))]

    def _plan(self, srcs, outs, sems):
        send_sems, recv_sems, local_sems = sems
        x, y, c = _my_place()

        def slot(t, dev):
            return self.kinds[t][2](outs[t], _dev_index(*dev))

        def copy(t, k, block, to, src=None):
            return pltpu.make_async_remote_copy(
                src_ref=slot(t, block) if src is None else src, dst_ref=slot(t, block),
                send_sem=send_sems.at[t, k], recv_sem=recv_sems.at[t, k], device_id=to, device_id_type=MESH)

        return types.SimpleNamespace(
            copy=copy, core=c, me=(x, y, c), sibling=(x, y, 1 - c),
            x_chip=(1 - x, y), y_chip=(x, 1 - y), far_chip=(1 - x, 1 - y),
            via=(x ^ (1 - c), y ^ c),
            onto=(x ^ c, y ^ (1 - c)),
            k_via=1 + c, k_onto=2 - c,
            local=[pltpu.make_async_copy(self._shard(srcs, t), slot(t, (x, y, c)), local_sems.at[t])
                   for t in range(len(srcs))])

    def _shard(self, srcs, t):
        if self.columns[t] is None:
            return srcs[t]
        first, end = self.columns[t]
        return srcs[t].at[:, first:end]

    def start(self, srcs, outs, sems):
        p = self._plan(srcs, outs, sems)
        for cp in p.local:
            cp.start()
        for t in range(len(srcs)):
            shard = self._shard(srcs, t)
            p.copy(t, 0, p.me, p.sibling, src=shard).start()
            p.copy(t, 1, p.me, (*p.x_chip, p.core), src=shard).start()
            p.copy(t, 2, p.me, (*p.y_chip, p.core), src=shard).start()

    def mid(self, srcs, outs, sems):
        p = self._plan(srcs, outs, sems)
        for t in range(len(srcs)):
            block = (*p.via, p.core)
            p.copy(t, p.k_via, block, p.me).wait_recv()
            p.copy(t, 3, block, (*p.onto, p.core)).start()
            p.copy(t, 3 + p.k_via, block, p.sibling).start()

    def finish(self, srcs, outs, sems):
        p = self._plan(srcs, outs, sems)
        n = len(srcs)
        for t in range(n):
            block = (*p.onto, p.core)
            p.copy(t, p.k_onto, block, p.me).wait_recv()
            p.copy(t, 3 + p.k_onto, block, p.sibling).start()
        for t in range(n):
            block = (*p.far_chip, p.core)
            p.copy(t, 3, block, p.me).wait_recv()
            p.copy(t, 6, block, p.sibling).start()
        other = 1 - p.core
        for t in range(n):
            p.copy(t, 0, (*p.me[:2], other), p.me).wait_recv()
            for k, chip in ((4, p.x_chip), (5, p.y_chip), (6, p.far_chip)):
                p.copy(t, k, (*chip, other), p.me).wait_recv()
            for k in range(7):
                p.copy(t, k, p.me, p.sibling).wait_send()
        for cp in p.local:
            cp.wait()


def _jobs_only(name, job=None):
    return _launch(_no_compute, name=name, grid=(), in_specs=[], out_specs=[], out_shape=[], args=(), job=job)


def _block_pool(ref, j):
    return ref.at[:, pl.ds(pl.multiple_of(j * 32, 32), 32), :]


def _block_rows128(ref, j):
    return ref.at[pl.ds(pl.multiple_of(j * 128, 128), 128), :]


def _block_gu(ref, j):
    return ref.at[j % FF_CHUNKS, j // FF_CHUNKS]


def _block_wd(ref, j):
    return ref.at[pl.ds(pl.multiple_of(j * WD_ROWS, 16), WD_ROWS), :]


def _block_cols128(ref, j):
    return ref.at[:, pl.ds(pl.multiple_of(j * 128, 128), 128)]


_SCATTERED = {
    "pool": ((N_POOL_GROUPS, 32, POOL_GROUP), _block_pool),
    "kv": ((128, 2 * KV_DIM), _block_rows128),
    "q": ((128, D_MODEL), _block_rows128),
    "o": ((128, D_MODEL), _block_rows128),
    "gu": ((FF_BLOCK, FF_PART), _block_gu),
    "wd": ((WD_ROWS, FF_PART), _block_wd),
    "guA": ((FF_BLOCK, FF_PART), lambda ref, j: _block_gu(ref, j).at[:, :FF_PART]),
    "guB": ((FF_BLOCK, FF_PART), lambda ref, j: _block_gu(ref, j).at[:, FF_PART:]),
    "wdA": ((WD_ROWS, FF_PART), lambda ref, j: _block_wd(ref, j).at[:, :FF_PART]),
    "wdB": ((WD_ROWS, FF_PART), lambda ref, j: _block_wd(ref, j).at[:, FF_PART:]),
    "gate": ((128, D_MODEL), _block_rows128),
    "proj": ((PLE_DIM, 128), _block_cols128),
}


class _SiblingSwap:
    peers = ("sibling",)

    def __init__(self, pieces):
        self.kinds = [_SCATTERED[kind] for kind, _ in pieces]
        self.args = [g for _, g in pieces]
        self.out_shape = [jax.ShapeDtypeStruct((N_CHIPS, *block), BF16) for block, _ in self.kinds]
        n = len(pieces)
        self.scratch = [pltpu.SemaphoreType.DMA((n, N_CHIPS)), pltpu.SemaphoreType.DMA((n, N_CHIPS))]

    def _copies(self, srcs, outs, sems):
        send_sems, recv_sems = sems
        x, y, c = _my_place()
        return [pltpu.make_async_remote_copy(
            src_ref=block(srcs[t], 2 * ch + 1 - c), dst_ref=outs[t].at[ch], send_sem=send_sems.at[t, ch],
            recv_sem=recv_sems.at[t, ch], device_id=(x, y, 1 - c), device_id_type=MESH)
            for t, (_, block) in enumerate(self.kinds) for ch in range(N_CHIPS)]

    def start(self, srcs, outs, sems):
        for cp in self._copies(srcs, outs, sems):
            cp.start()

    def finish(self, srcs, outs, sems):
        for cp in self._copies(srcs, outs, sems):
            cp.wait()


class _ChipScatter:
    N_BUFS = 4
    peers = ("x", "y")

    def __init__(self, pieces):
        self.kinds = [_SCATTERED[kind] for kind, _, _ in pieces]
        self.n = n = len(pieces)
        self.args = [g for _, g, _ in pieces] + [s for _, _, s in pieces]
        self.out_shape = [jax.ShapeDtypeStruct((2, *block), BF16) for block, _ in self.kinds]
        self.scratch = []
        for block, _ in self.kinds:
            self.scratch += [pltpu.VMEM((N_CHIPS, *block), BF16)] * 3 + [pltpu.VMEM((2, *block), BF16)]
        dma = pltpu.SemaphoreType.DMA
        self.scratch += [dma((n, N_CHIPS + 1)), dma((n, 2)), dma((n, 2)), dma((n,)), dma((n,)), dma((n,))]

    def _plan(self, outs, scr):
        n = self.n
        first_send, first_recv, second_send, second_recv, keep_sems = scr[self.N_BUFS * n + 1:]
        x, y, c = _my_place()
        via = (x ^ (1 - c), y ^ c)
        onto = (x ^ c, y ^ (1 - c))
        index = lambda chip: 2 * chip[0] + chip[1]
        first, second, keep = [], [], []
        for t in range(n):
            total, inbox = scr[self.N_BUFS * t + 2], scr[self.N_BUFS * t + 3]
            for k, chip in enumerate((via, (1 - x, 1 - y))):
                first.append(pltpu.make_async_remote_copy(
                    src_ref=total.at[index(chip)], dst_ref=inbox.at[k], send_sem=first_send.at[t, k],
                    recv_sem=first_recv.at[t, k], device_id=(*via, c), device_id_type=MESH))
            second.append(pltpu.make_async_remote_copy(
                src_ref=total.at[index(onto)], dst_ref=outs[t].at[1], send_sem=second_send.at[t],
                recv_sem=second_recv.at[t], device_id=(*onto, c), device_id_type=MESH))
            keep.append(pltpu.make_async_copy(total.at[index((x, y))], outs[t].at[0], keep_sems.at[t]))
        return first, second, keep, index((x, y)), index(onto)

    def start(self, ins, outs, scr):
        n = self.n
        load_sems = scr[self.N_BUFS * n]
        c = lax.axis_index("c")
        loads = []
        for t, (_, block) in enumerate(self.kinds):
            mine, theirs = scr[self.N_BUFS * t], scr[self.N_BUFS * t + 1]
            loads += [pltpu.make_async_copy(block(ins[t], 2 * ch + c), mine.at[ch], load_sems.at[t, ch])
                      for ch in range(N_CHIPS)]
            loads.append(pltpu.make_async_copy(ins[n + t], theirs, load_sems.at[t, N_CHIPS]))
        for cp in loads:
            cp.start()
        for cp in loads:
            cp.wait()
        for t in range(n):
            mine, theirs, total = scr[self.N_BUFS * t:self.N_BUFS * t + 3]
            for ch in range(N_CHIPS):
                total[ch] = (mine[ch].astype(F32) + theirs[ch].astype(F32)).astype(BF16)
        for cp in self._plan(outs, scr)[0]:
            cp.start()

    def mid(self, ins, outs, scr):
        first, second, keep, me, onto = self._plan(outs, scr)
        for cp in first:
            cp.wait_recv()
        for t in range(self.n):
            total, inbox = scr[self.N_BUFS * t + 2], scr[self.N_BUFS * t + 3]
            for k, slot in enumerate((me, onto)):
                total[slot] = (total[slot].astype(F32) + inbox[k].astype(F32)).astype(BF16)
        for cp in second + keep:
            cp.start()

    def finish(self, ins, outs, scr):
        first, second, keep, _, _ = self._plan(outs, scr)
        for cp in first:
            cp.wait_send()
        for cp in second + keep:
            cp.wait()


class _ToEveryone:
    peers = _EVERYONE

    def __init__(self, scattered=(), gathered=()):
        self.blocks = [_SCATTERED[kind][1] for kind, _ in scattered] + [None] * len(gathered)
        self.args = [g for _, g in scattered] + list(gathered)
        self.out_shape = [jax.ShapeDtypeStruct((N_DEV, *_SCATTERED[kind][0]), BF16) for kind, _ in scattered]
        self.out_shape += [jax.ShapeDtypeStruct((N_DEV, *a.shape), a.dtype) for a in gathered]
        n = len(self.args)
        self.scratch = [pltpu.SemaphoreType.DMA((n, N_DEV - 1)), pltpu.SemaphoreType.DMA((n, N_DEV - 1)),
                        pltpu.SemaphoreType.DMA((n,))]

    def _copies(self, srcs, outs, sems):
        send_sems, recv_sems, local_sems = sems
        me = _dev_index(*_my_place())
        copies = []
        for t, block in enumerate(self.blocks):
            part = (lambda j, t=t, block=block: srcs[t] if block is None else block(srcs[t], j))
            copies.append(pltpu.make_async_copy(part(me), outs[t].at[me], local_sems.at[t]))
            for r in range(1, N_DEV):
                peer = _peer_by_relation(r)
                copies.append(pltpu.make_async_remote_copy(
                    src_ref=part(_dev_index(*peer)), dst_ref=outs[t].at[me], send_sem=send_sems.at[t, r - 1],
                    recv_sem=recv_sems.at[t, r - 1], device_id=peer, device_id_type=MESH))
        return copies

    def start(self, srcs, outs, sems):
        for cp in self._copies(srcs, outs, sems):
            cp.start()

    def finish(self, srcs, outs, sems):
        for cp in self._copies(srcs, outs, sems):
            cp.wait()


class _Jobs:
    def __init__(self, *jobs):
        self.jobs = jobs
        together = {p for j in jobs for p in j.peers}
        self.peers = tuple(p for p in _EVERYONE if p in together)
        self.args = [a for j in jobs for a in j.args]
        self.out_shape = [o for j in jobs for o in j.out_shape]
        self.scratch = [s for j in jobs for s in j.scratch]

    def _split(self, refs, attr):
        at = 0
        for j in self.jobs:
            n = len(getattr(j, attr))
            yield refs[at:at + n]
            at += n

    def _each(self, ins, outs, scr):
        return zip(self.jobs, self._split(ins, "args"), self._split(outs, "out_shape"), self._split(scr, "scratch"))

    def start(self, ins, outs, scr):
        for j, i, o, s in self._each(ins, outs, scr):
            j.start(i, o, s)

    def mid(self, ins, outs, scr):
        for j, i, o, s in self._each(ins, outs, scr):
            if hasattr(j, "mid"):
                j.mid(i, o, s)

    def finish(self, ins, outs, scr):
        for j, i, o, s in self._each(ins, outs, scr):
            j.finish(i, o, s)

    def split_outputs(self, outs):
        return list(self._split(outs, "out_shape"))


def _adamw_math(w, g, m, v):
    m = ADAM_B1 * m + (1.0 - ADAM_B1) * g
    v = ADAM_B2 * v + (1.0 - ADAM_B2) * (g * g)
    m_hat = m / (1.0 - ADAM_B1 ** ADAM_STEP)
    v_hat = v / (1.0 - ADAM_B2 ** ADAM_STEP)
    delta = -ADAM_LR * (m_hat / (jnp.sqrt(v_hat) + ADAM_EPS) + ADAM_WD * w)
    return delta, m, v


def _adamw(name, w, m, v, landings, n_col_blocks=1, job=None):
    n_slots, r, c = landings[0].shape
    grid = (w.shape[0] // r, n_col_blocks)

    def body(w_ref, m_ref, v_ref, *rest):
        l_refs, (g_ref, d_ref, nm_ref, nv_ref) = rest[:len(landings)], rest[len(landings):]
        step = pl.program_id(0) * n_col_blocks + pl.program_id(1)
        for idx, l_ref in enumerate(l_refs):
            @pl.when(step == idx)
            def _(l_ref=l_ref):
                g = l_ref[0].astype(F32)
                for s in range(1, n_slots):
                    g = g + l_ref[s].astype(F32)
                g_ref[...] = g
                d_ref[...], nm_ref[...], nv_ref[...] = _adamw_math(w_ref[...], g, m_ref[...], v_ref[...])

    spec = pl.BlockSpec((r, c), lambda a, b: (a, b))
    return _launch(
        body, name=f"adamw_{name}", grid=grid,
        in_specs=[spec, spec, spec] + [_full_spec((n_slots, r, c))] * len(landings),
        out_specs=[spec] * 4, out_shape=[jax.ShapeDtypeStruct(w.shape, F32)] * 4,
        args=(w, m, v, *landings), vmem=VMEM_BIG, job=job)


_SMALL = (("pre_mix_g", SV_PRE_MIX, 2), ("post_mix_g", SV_POST_MIX, 2), ("pre_ffn_g", SV_PRE_FFN, 2),
          ("post_ffn_g", SV_POST_FFN, 2), ("ple_g", SV_PLE, 2), ("ple_post_g", SV_PLE_POST, 2), ("kv_g", SV_KV, 1),
          ("pool_scale", SV_POOL_SCALE, 1), ("sinks", SV_SINKS, 1))


def _small_adamw(slabs, params):
    flat = [a for name, _, _ in _SMALL for a in params[name]]
    n_in = 1 + len(flat)

    def body(*refs):
        slabs_ref, wmv = refs[0], refs[1:n_in]
        loss_ref, outs, total = refs[n_in], refs[n_in + 1:-1], refs[-1]
        me = _dev_index(*_my_place())
        g = slabs_ref[0]
        for s in range(1, N_DEV):
            g = g + slabs_ref[s]
        total[...] = g
        loss_ref[...] = total[SV_LOSS:SV_LOSS + 1, 0:1]
        for idx, (name, row, n_rows) in enumerate(_SMALL):
            w_ref, m_ref, v_ref = wmv[3 * idx:3 * idx + 3]
            g_ref, d_ref, nm_ref, nv_ref = outs[4 * idx:4 * idx + 4]
            if name == "pool_scale":
                g = total[row:row + 1, pl.ds(pl.multiple_of(me * 128, 128), 128)]
            else:
                g = total[row:row + n_rows, 0:w_ref.shape[1]]
            g_ref[...] = g
            d_ref[...], nm_ref[...], nv_ref[...] = _adamw_math(w_ref[...], g, m_ref[...], v_ref[...])

    out_shape = [jax.ShapeDtypeStruct((1, 1), F32)]
    for name, _, _ in _SMALL:
        out_shape += [jax.ShapeDtypeStruct(params[name][0].shape, F32)] * 4
    res, _ = _launch(
        body, name="small_adamw", grid=(1,),
        in_specs=[_full_spec(a.shape) for a in (slabs, *flat)], out_specs=[_full_spec(s.shape) for s in out_shape],
        out_shape=out_shape, scratch_shapes=[pltpu.VMEM((SV_ROWS, D_MODEL), F32)], args=(slabs, *flat))
    return res[0], {name: res[1 + 4 * idx:5 + 4 * idx] for idx, (name, _, _) in enumerate(_SMALL)}


def _local_step(x, p, tgt, gains, sinks, shards, weights):
    row = lambda first_row, layer: _Gain(gains, first_row + layer)
    gather = lambda *names: _AllGather(names, shards)
    g_pre_mix, g_post_mix, g_pre_ffn, g_post_ffn = SV_PRE_MIX, SV_POST_MIX, SV_PRE_FFN, SV_POST_FFN
    g_ple, g_ple_post, g_kv = SV_PLE, SV_PLE_POST, _Gain(gains, SV_KV)

    (dpool,), (wp, scale, wgu0, wd0) = _fwd_pool(x, row(g_pre_mix, 0), job=gather("pool", "scale", "gu0", "wd0"))
    wgu0, wd0 = [wgu0], [wd0]
    (x1_0, h2_0, yraw), _ = _fwd_pool_mixer(x, dpool, wp, scale, row(g_post_mix, 0), row(g_pre_ffn, 0))
    (gs0, us0, f0, x2_0, h3_0), (wgate0, wproj0, wkv, wq, wo, wd1_a) = _fwd_ffn(
        0, h2_0, x1_0, wgu0, wd0, row(g_post_ffn, 0), row(g_ple, 0),
        job=gather("gate0", "proj0", "kv", "q", "o", "wdh1_0"))
    (x3_0, z0, pe0, hk, h1, q, kv), (wgu1_a,) = _fwd_ple_qkv(
        x2_0, h3_0, p[0], wgate0, wproj0, row(g_ple_post, 0), g_kv, row(g_pre_mix, 1), wkv, wq,
        job=gather("guh1_0"))
    front = ((ATT_BLOCK, 0), (0, 0))
    kpad = jnp.pad(kv[:, :KV_DIM], front)
    vpad = jnp.pad(kv[:, KV_DIM:], front)
    (attn,), (wgu1_b,) = _fwd_attention(q, kpad, vpad, sinks, job=gather("guh1_1"))
    (y1, x1_1, h2_1), (wd1_b,) = _fwd_attn_out(attn, x3_0, wo, row(g_post_mix, 1), row(g_pre_ffn, 1),
                                               job=gather("wdh1_1"))
    wgu1, wd1 = [wgu1_a, wgu1_b], [wd1_a, wd1_b]
    (gs1, us1, f1, x2_1, h3_1), (wgate1, wproj1) = _fwd_ffn(
        1, h2_1, x1_1, wgu1, wd1, row(g_post_ffn, 1), row(g_ple, 1), job=gather("gate1", "proj1"))

    produced, swapped, landed = {}, {}, {}

    def kind_of(name):
        return name.rstrip("0123_")

    def hosted(call, *args, swap=(), spread=(), extra=None):
        jobs = []
        if swap:
            jobs.append(_SiblingSwap([(kind_of(n), produced[n]) for n in swap]))
        if spread:
            jobs.append(_ChipScatter([(kind_of(n), produced[n], swapped[n]) for n in spread]))
        if extra is not None:
            jobs.append(extra)
        jobs = _Jobs(*jobs)
        outs, job_outs = call(*args, job=jobs)
        parts = jobs.split_outputs(job_outs)
        if swap:
            swapped.update(zip(swap, parts.pop(0)))
        if spread:
            landed.update(zip(spread, parts.pop(0)))
        return outs if extra is None else (outs, parts.pop(0))

    ffn_q = lambda layer, qtr: (f"gu{layer}_{qtr}", f"wd{layer}_{qtr}")

    dx2_1, df1, produced["gate1"], produced["proj1"], dg_ple_post1, dg_ple1, dg_post_ffn1, loss = hosted(
        _ple_loss_bwd, 1, x2_1, h3_1, p[1], f1, tgt, wgate1, wproj1, row(g_ple_post, 1), row(g_ple, 1),
        row(g_post_ffn, 1))
    dh2_1, dg1, du1, a1 = hosted(_bwd_ffn_act, 1, df1, gs1, us1, wgu1, wd1, swap=("gate1", "proj1"))
    dgu1, dwd1 = hosted(_bwd_ffn_dw, 1, 0, 1, h2_1, df1, dg1, du1, a1, spread=("gate1", "proj1"))
    produced.update(guA1=dgu1, guB1=dgu1, wdA1=dwd1, wdB1=dwd1)
    dx1_1, dattn, produced["o"], dg_pre_ffn1, dg_post_mix1 = hosted(
        _bwd_attn_out, dx2_1, dh2_1, x1_1, y1, attn, wo, row(g_pre_ffn, 1), row(g_post_mix, 1),
        swap=("guA1", "wdA1", "guB1", "wdB1"))
    dq, dkpad, dvpad, dsinks = hosted(_bwd_attention, q, dattn, kpad, vpad, sinks, spread=("guA1", "wdA1"))
    dkv = jnp.concatenate([dkpad[ATT_BLOCK:], dvpad[ATT_BLOCK:]], axis=1).astype(BF16)
    dx3_0, produced["q"], produced["kv"], dg_pre_mix1, dg_kv = hosted(
        _bwd_qkv, dx1_1, dq, dkv, x3_0, h1, hk, wq, wkv, row(g_pre_mix, 1), g_kv, swap=("o",), spread=("wdB1",))
    dx2_0, df0, produced["gate0"], produced["proj0"], dg_ple_post0, dg_ple0, dg_post_ffn0 = hosted(
        _bwd_ple, 0, dx3_0, x2_0, z0, pe0, h3_0, p[0], f0, wgate0, row(g_ple_post, 0), row(g_ple, 0),
        row(g_post_ffn, 0), swap=("q", "kv"), spread=("guB1",))
    for half, letter in enumerate("AB"):
        landed[f"gu1_{half}"], landed[f"wd1_{half}"] = landed[f"gu{letter}1"], landed[f"wd{letter}1"]
    dh2_0, dg0, du0, a0 = hosted(_bwd_ffn_act, 0, df0, gs0, us0, wgu0, wd0,
                                 swap=("gate0", "proj0"), spread=("o", "q", "kv"))
    part_hosts = [dict(spread=("gate0", "proj0")), dict(swap=ffn_q(0, 0))]
    for part in range(FF_PARTS):
        produced[f"gu0_{part}"], produced[f"wd0_{part}"] = hosted(
            _bwd_ffn_dw, 0, part, FF_PARTS, h2_0, df0, dg0, du0, a0, **part_hosts[part])
    grad_x, produced["pool"], dscale, dg_pre_ffn0, dg_post_mix0, dg_pre_mix0 = hosted(
        _bwd_pool_mixer, dx2_0, dh2_0, x1_0, x, yraw, dpool, wp, scale, row(g_pre_ffn, 0), row(g_post_mix, 0),
        row(g_pre_mix, 0), swap=ffn_q(0, 1), spread=ffn_q(0, 0))

    def update(name, n_col_blocks=1, pieces=None, swap=(), spread=()):
        w, m, v = weights[name]
        rows = w.size // w.shape[-1]
        flat = [landed[n].reshape(landed[n].shape[0], -1, landed[n].shape[-1])
                for n in (pieces or [kind_short[name]])]
        outs = hosted(_adamw, name, w.reshape(rows, -1), m.reshape(rows, -1), v.reshape(rows, -1), flat,
                      n_col_blocks, swap=swap, spread=spread)
        return [o.reshape(w.shape) for o in outs]

    kind_short = {"w_q": "q", "w_kv": "kv", "w_o": "o", "pool_w": "pool"}
    upd = {}
    lanes = lambda a: jnp.pad(a, ((0, 0), (0, D_MODEL - a.shape[1])))
    small = jnp.concatenate([
        dg_pre_mix0, dg_pre_mix1, dg_post_mix0, dg_post_mix1, dg_pre_ffn0, dg_pre_ffn1, dg_post_ffn0, dg_post_ffn1,
        dg_ple0, dg_ple1, dg_ple_post0, dg_ple_post1, dg_kv, dscale, lanes(dsinks[:, :N_HEADS]), lanes(loss)], axis=0)

    everyone = _ToEveryone(scattered=[("pool", produced["pool"])], gathered=[small])
    _, (landed["pool"], slabs) = hosted(_jobs_only, "scatter_tail", spread=ffn_q(0, 1), extra=everyone)
    upd["w_ple_gate"] = update("w_ple_gate", pieces=("gate0", "gate1"))
    upd["w_ple_proj"] = update("w_ple_proj", pieces=("proj0", "proj1"))
    for name in ("w_q", "w_kv", "w_o", "pool_w"):
        upd[name] = update(name)
    upd["w_gu"] = update("w_gu", FF_PARTS,
                         pieces=[f"gu{layer}_{qtr}" for layer in range(2) for qtr in range(FF_PARTS)])
    upd["w_gu"] = [jnp.swapaxes(a, 1, 2) for a in upd["w_gu"]]
    upd["w_down"] = update("w_down", FF_PARTS,
                           pieces=[f"wd{layer}_{qtr}" for layer in range(2) for qtr in range(FF_PARTS)])
    return grad_x, upd, slabs


def kernel(x, p, pre_mix_g, post_mix_g, pre_ffn_g, post_ffn_g, pool_w, pool_scale, kv_g, w_kv, w_q, sinks, w_o, w_gu, w_down, ple_g, w_ple_gate, w_ple_proj, ple_post_g, loss_target, m_pre_mix_g, m_post_mix_g, m_pre_ffn_g, m_post_ffn_g, m_pool_w, m_pool_scale, m_kv_g, m_w_kv, m_w_q, m_sinks, m_w_o, m_w_gu, m_w_down, m_ple_g, m_w_ple_gate, m_w_ple_proj, m_ple_post_g, v_pre_mix_g, v_post_mix_g, v_pre_ffn_g, v_post_ffn_g, v_pool_w, v_pool_scale, v_kv_g, v_w_kv, v_w_q, v_sinks, v_w_o, v_w_gu, v_w_down, v_ple_g, v_w_ple_gate, v_w_ple_proj, v_ple_post_g):
    shards = {"pool": pool_w[0].astype(BF16), "scale": pool_scale, "kv": w_kv.astype(BF16),
              "q": w_q[0].astype(BF16), "o": w_o[0].astype(BF16)}
    for layer in range(2):
        shards[f"gu{layer}"] = w_gu[layer].T.astype(BF16)
        shards[f"wd{layer}"] = w_down[layer].astype(BF16)
        for half in range(2):
            cols = (half * D_MODEL // 2, (half + 1) * D_MODEL // 2)
            shards[f"guh{layer}_{half}"] = (shards[f"gu{layer}"], cols)
            shards[f"wdh{layer}_{half}"] = (shards[f"wd{layer}"], cols)
        shards[f"gate{layer}"] = w_ple_gate[layer].astype(BF16)
        shards[f"proj{layer}"] = w_ple_proj[layer].astype(BF16)
    gains = jnp.concatenate([pre_mix_g, post_mix_g, pre_ffn_g, post_ffn_g, ple_g, ple_post_g, kv_g[None, :]],
                            axis=0).reshape(-1, 1, D_MODEL)
    weights = {"pool_w": (pool_w, m_pool_w, v_pool_w), "w_kv": (w_kv, m_w_kv, v_w_kv), "w_q": (w_q, m_w_q, v_w_q),
               "w_o": (w_o, m_w_o, v_w_o), "w_down": (w_down, m_w_down, v_w_down),
               "w_gu": tuple(jnp.swapaxes(a, 1, 2) for a in (w_gu, m_w_gu, v_w_gu)),
               "w_ple_gate": (w_ple_gate, m_w_ple_gate, v_w_ple_gate),
               "w_ple_proj": (w_ple_proj, m_w_ple_proj, v_w_ple_proj)}
    grad_x, upd, slabs = _local_step(x[0], p[:, 0], loss_target[0], gains, sinks, shards, weights)

    small_params = {
        "pre_mix_g": (pre_mix_g, m_pre_mix_g, v_pre_mix_g), "post_mix_g": (post_mix_g, m_post_mix_g, v_post_mix_g),
        "pre_ffn_g": (pre_ffn_g, m_pre_ffn_g, v_pre_ffn_g), "post_ffn_g": (post_ffn_g, m_post_ffn_g, v_post_ffn_g),
        "ple_g": (ple_g, m_ple_g, v_ple_g), "ple_post_g": (ple_post_g, m_ple_post_g, v_ple_post_g),
        "kv_g": (kv_g[None, :], m_kv_g[None, :], v_kv_g[None, :]),
        "pool_scale": (pool_scale, m_pool_scale, v_pool_scale), "sinks": (sinks, m_sinks, v_sinks)}
    loss, small_upd = _small_adamw(slabs, small_params)
    small_upd["kv_g"] = [a[0] for a in small_upd["kv_g"]]
    upd.update(small_upd)

    names = ["pre_mix_g", "post_mix_g", "pre_ffn_g", "post_ffn_g", "pool_w", "pool_scale", "kv_g", "w_kv", "w_q",
             "sinks", "w_o", "w_gu", "w_down", "ple_g", "w_ple_gate", "w_ple_proj", "ple_post_g"]
    outs = [loss[0, 0], grad_x[None]]
    for kind in range(4):
        outs += [upd[n][kind] for n in names]
    return tuple(outs)
```

```python
import functools
import types

import jax
import jax.numpy as jnp
from jax import lax
from jax.experimental import pallas as pl
from jax.experimental.pallas import tpu as pltpu

F32 = jnp.float32
BF16 = jnp.bfloat16

N_DEV = 8
D_MODEL = 1024
N_POOL_GROUPS = 4
POOL_GROUP = 256
POOL_HALO = 16
HEAD_DIM = 64
N_HEADS = 16
N_KV_HEADS = 4
GQA_GROUP = 4
KV_DIM = N_KV_HEADS * HEAD_DIM
ATT_BLOCK = 128
D_FF = 2816
FF_CHUNKS = 4
FF_BLOCK = D_FF // FF_CHUNKS
WD_ROWS = D_FF // N_DEV
FF_PARTS = 2
FF_PART = D_MODEL // FF_PARTS
N_CHIPS = 4
PLE_DIM = 256
EPS = 1e-6
NEG_INF = -1e30
ATT_SCALE = HEAD_DIM ** -0.5

ADAM_LR = 0.001
ADAM_B1 = 0.9
ADAM_B2 = 0.999
ADAM_EPS = 1e-08
ADAM_WD = 0.01
ADAM_STEP = 10

ROW_TILE = 512
FFN_ROW_TILE = 512
FFN_WEIGHT_COLS = 512
FFN_SUB_TILES = 1
VMEM_BIG = 60 * 1024 * 1024
VMEM_MID = 56 * 1024 * 1024
HBM_PIN_ELEMS = 1024

SV_ROWS = 16
SV_PRE_MIX, SV_POST_MIX, SV_PRE_FFN, SV_POST_FFN, SV_PLE, SV_PLE_POST = 0, 2, 4, 6, 8, 10
SV_KV, SV_POOL_SCALE, SV_SINKS, SV_LOSS = 12, 13, 14, 15

MESH = pl.DeviceIdType.MESH
ANY = pl.BlockSpec(memory_space=pl.ANY)


def _dot(a, b):
    return jnp.dot(a, b, preferred_element_type=F32)


def _dot_nt(a, b):
    return lax.dot_general(a, b, (((1,), (1,)), ((), ())), preferred_element_type=F32)


def _dot_tn(a, b):
    return lax.dot_general(a, b, (((0,), (0,)), ((), ())), preferred_element_type=F32)


def _rstd(x):
    return lax.rsqrt(jnp.mean(x * x, axis=-1, keepdims=True) + EPS)


def _rms(x, g):
    return x * _rstd(x) * g


def _rms_bwd(x, g, dy):
    r = _rstd(x)
    n = x * r
    dn = dy * g
    dx = r * (dn - n * jnp.mean(dn * n, axis=-1, keepdims=True))
    dg = jnp.sum(dy * n, axis=0, keepdims=True)
    return dx, dg


def _add_all(terms):
    return functools.reduce(jnp.add, terms)


def _sigmoid(x):
    return 1.0 / (1.0 + jnp.exp(-x))


def _acc(ref, val, first):
    @pl.when(first)
    def _():
        ref[...] = val

    @pl.when(jnp.logical_not(first))
    def _():
        ref[...] += val


def _pool_counts(row0, rows):
    t = row0 + lax.broadcasted_iota(jnp.int32, (rows, D_MODEL), 0) + 1
    grp = lax.broadcasted_iota(jnp.int32, (rows, D_MODEL), 1) // POOL_GROUP
    win = jnp.left_shift(2, grp)
    return jnp.minimum(t, win).astype(F32)


def _window_sums(ext, shift_of):
    outs = []
    s = ext
    for gi in range(N_POOL_GROUPS):
        s = s + pltpu.roll(s, shift_of(1 << gi), axis=0)
        outs.append(s[:, :POOL_GROUP])
        s = s[:, POOL_GROUP:]
    return jnp.concatenate(outs, axis=1)


def _cparams(n_axes, vmem, collective_id=None):
    return pltpu.CompilerParams(dimension_semantics=("arbitrary",) * n_axes, vmem_limit_bytes=vmem,
                                collective_id=collective_id)


_EVERYONE = ("sibling", "x", "y", "far", "x sibling", "y sibling", "far sibling")
_PEER_SETS = (("sibling", "x", "y"), ("sibling",), ("x", "y"), _EVERYONE)


def _meet(peers):
    x, y, c = lax.axis_index("x"), lax.axis_index("y"), lax.axis_index("c")
    device = {"sibling": (x, y, 1 - c), "x": (1 - x, y, c), "y": (x, 1 - y, c), "far": (1 - x, 1 - y, c),
              "x sibling": (1 - x, y, 1 - c), "y sibling": (x, 1 - y, 1 - c), "far sibling": (1 - x, 1 - y, 1 - c)}
    barrier = pltpu.get_barrier_semaphore()
    for peer in peers:
        pl.semaphore_signal(barrier, inc=1, device_id=device[peer], device_id_type=pl.DeviceIdType.MESH)
    pl.semaphore_wait(barrier, len(peers))


def _row_spec(cols, tm=ROW_TILE):
    return pl.BlockSpec((tm, cols), lambda i: (i, 0))


def _full_spec(shape):
    zeros = (0,) * len(shape)
    return pl.BlockSpec(shape, lambda *_: zeros)


def _vec_spec():
    return _full_spec((1, D_MODEL))


def _column_views(parts):
    return [(a, b) for a in parts for b in range(a.shape[-1] // FFN_WEIGHT_COLS)]


def _column_ranges(views):
    return [(n * FFN_WEIGHT_COLS, (n + 1) * FFN_WEIGHT_COLS) for n in range(len(views))]


class _Gain:
    def __init__(self, stacked, layer):
        self.stacked, self.layer = stacked, layer

    def spec(self):
        layer = self.layer
        return pl.BlockSpec((None, 1, D_MODEL), lambda *_: (layer, 0, 0))


def _in_hbm(a):
    return pltpu.with_memory_space_constraint(a, pltpu.HBM) if a.size >= HBM_PIN_ELEMS else a


def _out_in_hbm(s):
    return pltpu.HBM(s.shape, s.dtype) if s.size >= HBM_PIN_ELEMS else s


def _launch(body, *, name, grid, in_specs, out_specs, out_shape, args, scratch_shapes=(), vmem=VMEM_MID, job=None):
    in_specs = [a.spec() if isinstance(a, _Gain) else s for s, a in zip(in_specs, args)]
    args = [_in_hbm(a.stacked if isinstance(a, _Gain) else a) for a in args]
    n_in, n_out, n_scr = len(args), len(out_shape), len(scratch_shapes)
    if job is not None and not job.args:
        job = None
    j_args, j_out, j_scr = ([], [], []) if job is None else ([_in_hbm(a) for a in job.args], job.out_shape, job.scratch)

    def run(*refs):
        groups, at = [], 0
        for n in (n_in, len(j_args), n_out, len(j_out), n_scr, len(j_scr)):
            groups.append(refs[at:at + n])
            at += n
        ins, j_ins, outs, j_outs, scr, j_sems = groups

        def begin():
            _meet(job.peers)
            job.start(j_ins, j_outs, j_sems)

        if job is None:
            body(*ins, *outs, *scr)
        elif not grid:
            begin()
            job.mid(j_ins, j_outs, j_sems)
            job.late(j_ins, j_outs, j_sems)
            body(*ins, *outs, *scr)
            job.finish(j_ins, j_outs, j_sems)
        else:
            ids = [pl.program_id(a) for a in range(len(grid))]
            at_start = lambda step: functools.reduce(jnp.logical_and, [ids[0] == step] + [i == 0 for i in ids[1:]])
            last = functools.reduce(jnp.logical_and, [i == g - 1 for i, g in zip(ids, grid)])
            pl.when(at_start(0))(begin)
            pl.when(at_start(grid[0] // 2))(lambda: job.mid(j_ins, j_outs, j_sems))
            pl.when(at_start(3 * grid[0] // 4))(lambda: job.late(j_ins, j_outs, j_sems))
            body(*ins, *outs, *scr)
            pl.when(last)(lambda: job.finish(j_ins, j_outs, j_sems))

    res = pl.pallas_call(
        run, name=name, grid=grid,
        in_specs=list(in_specs) + [ANY] * len(j_args), out_specs=list(out_specs) + [ANY] * len(j_out),
        out_shape=[_out_in_hbm(s) for s in list(out_shape) + list(j_out)],
        scratch_shapes=list(scratch_shapes) + list(j_scr),
        compiler_params=_cparams(len(grid), vmem, None if job is None else _PEER_SETS.index(job.peers)),
    )(*args, *j_args)
    return res[:n_out], res[n_out:]


def _fwd_pool(x, g_pre, job=None):
    T = x.shape[0]
    tm = ROW_TILE
    nt = T // tm

    def body(x_ref, gpre_ref, d_ref, carry):
        i = pl.program_id(0)

        @pl.when(i == 0)
        def _():
            carry[...] = jnp.zeros_like(carry)

        h = _rms(x_ref[...], gpre_ref[...])
        ext = jnp.concatenate([carry[...], h], axis=0)
        carry[...] = h[tm - POOL_HALO:, :]
        sums = _window_sums(ext, lambda k: k)[POOL_HALO:, :]
        d_ref[...] = (sums / _pool_counts(i * tm, tm) - h).astype(BF16)

    return _launch(
        body, name="fwd_pool", grid=(nt,), in_specs=[_row_spec(D_MODEL), _vec_spec()], out_specs=[_row_spec(D_MODEL)],
        out_shape=[jax.ShapeDtypeStruct((T, D_MODEL), BF16)], scratch_shapes=[pltpu.VMEM((POOL_HALO, D_MODEL), F32)],
        args=(x, g_pre), job=job)


def _fwd_pool_mixer(x, d, wp, scale, g_post, g_ffn, job=None):
    T = x.shape[0]
    nt = T // ROW_TILE

    def body(x_ref, d_ref, wp_ref, sc_ref, gpost_ref, gffn_ref, x1_ref, h2_ref, yraw_ref):
        db = d_ref[...]
        yraw = jnp.concatenate(
            [_dot(db[:, g * POOL_GROUP:(g + 1) * POOL_GROUP], wp_ref[g]) for g in range(N_POOL_GROUPS)], axis=1)
        yraw_ref[...] = yraw.astype(BF16)
        x1 = x_ref[...] + _rms(yraw * sc_ref[...], gpost_ref[...])
        x1_ref[...] = x1
        h2_ref[...] = _rms(x1, gffn_ref[...]).astype(BF16)

    return _launch(
        body, name="fwd_pool_mixer", grid=(nt,),
        in_specs=[_row_spec(D_MODEL), _row_spec(D_MODEL), _full_spec((N_POOL_GROUPS, POOL_GROUP, POOL_GROUP)),
                  _vec_spec(), _vec_spec(), _vec_spec()],
        out_specs=[_row_spec(D_MODEL)] * 3,
        out_shape=[jax.ShapeDtypeStruct((T, D_MODEL), F32)] + [jax.ShapeDtypeStruct((T, D_MODEL), BF16)] * 2,
        args=(x, d, wp, scale, g_post, g_ffn), job=job)


def _fwd_ffn(layer, h2, x1, wgu, wd, g_post, g_ple, job=None):
    T = h2.shape[0]
    tm = min(FFN_ROW_TILE, T)
    nt = T // tm
    sub = tm // FFN_SUB_TILES
    last = FF_CHUNKS - 1
    wgu, wd = _column_views(wgu), _column_views(wd)
    n_gu, n_wd = len(wgu), len(wd)
    gu_cols = _column_ranges(wgu)

    def body(h2_ref, x1_ref, *refs):
        wgu_refs, wd_refs = refs[:n_gu], refs[n_gu:n_gu + n_wd]
        gpost_ref, gple_ref, gs_ref, us_ref, f_ref, x2_ref, h3_ref, acc = refs[n_gu + n_wd:]
        k = pl.program_id(0)
        i = pl.program_id(1)
        rows = pl.ds(pl.multiple_of(i * tm, tm), tm)
        parts = []
        for s in range(FFN_SUB_TILES):
            r = pl.ds(s * sub, sub)
            g = _add_all([_dot_nt(h2_ref[r, c0:c1], w[0]) for (c0, c1), w in zip(gu_cols, wgu_refs)])
            u = _add_all([_dot_nt(h2_ref[r, c0:c1], w[1]) for (c0, c1), w in zip(gu_cols, wgu_refs)])
            gs_ref[r, :] = g.astype(BF16)
            us_ref[r, :] = u.astype(BF16)
            a = (g * _sigmoid(g) * u).astype(BF16)
            parts.append(jnp.concatenate([_dot(a, w[...]) for w in wd_refs], axis=1))
        part = jnp.concatenate(parts, axis=0)

        @pl.when(k == 0)
        def _():
            acc[rows, :] = part

        @pl.when(jnp.logical_and(k > 0, k < last))
        def _():
            acc[rows, :] += part

        @pl.when(k == last)
        def _():
            f = acc[rows, :] + part
            f_ref[...] = f.astype(BF16)
            x2 = x1_ref[...] + _rms(f, gpost_ref[...])
            x2_ref[...] = x2
            h3_ref[...] = _rms(x2, gple_ref[...]).astype(BF16)

    def late(k, i):
        return (jnp.where(k == last, i, 0), 0)

    return _launch(
        body, name=f"fwd_ffn{layer}", grid=(FF_CHUNKS, nt),
        in_specs=[pl.BlockSpec((tm, D_MODEL), lambda k, i: (i, 0)), pl.BlockSpec((tm, D_MODEL), late)]
                 + [pl.BlockSpec((None, 2, FF_BLOCK, FFN_WEIGHT_COLS), lambda k, i, b=b: (k, 0, 0, b)) for _, b in wgu]
                 + [pl.BlockSpec((FF_BLOCK, FFN_WEIGHT_COLS), lambda k, i, b=b: (k, b)) for _, b in wd]
                 + [pl.BlockSpec((1, D_MODEL), lambda k, i: (0, 0))] * 2,
        out_specs=[pl.BlockSpec((None, tm, FF_BLOCK), lambda k, i: (k, i, 0)),
                   pl.BlockSpec((None, tm, FF_BLOCK), lambda k, i: (k, i, 0)),
                   pl.BlockSpec((tm, D_MODEL), late),
                   pl.BlockSpec((tm, D_MODEL), late),
                   pl.BlockSpec((tm, D_MODEL), late)],
        out_shape=[jax.ShapeDtypeStruct((FF_CHUNKS, T, FF_BLOCK), BF16),
                   jax.ShapeDtypeStruct((FF_CHUNKS, T, FF_BLOCK), BF16),
                   jax.ShapeDtypeStruct((T, D_MODEL), BF16),
                   jax.ShapeDtypeStruct((T, D_MODEL), F32),
                   jax.ShapeDtypeStruct((T, D_MODEL), BF16)],
        scratch_shapes=[pltpu.VMEM((T, D_MODEL), F32)],
        args=(h2, x1, *[w for w, _ in wgu], *[w for w, _ in wd], g_post, g_ple), vmem=VMEM_BIG, job=job)


def _fwd_ple_qkv(x2, h3, p, wgate, wproj, g_post, g_kv, g_mix, wkv, wq, job=None):
    T = x2.shape[0]
    nt = T // ROW_TILE

    def body(x2_ref, h3_ref, p_ref, wg_ref, wp_ref, gpost_ref, gkv_ref, gmix_ref, wkv_ref, wq_ref,
             x3_ref, z_ref, pe_ref, hk_ref, h1_ref, q_ref, kv_ref):
        z = _dot(h3_ref[...], wg_ref[...])
        pe = _dot(p_ref[...].astype(BF16), wp_ref[...])
        z_ref[...] = z.astype(BF16)
        pe_ref[...] = pe.astype(BF16)
        x3 = x2_ref[...] + _rms(pe * _sigmoid(z), gpost_ref[...])
        x3_ref[...] = x3
        r = _rstd(x3)
        hk = (x3 * r * gkv_ref[...]).astype(BF16)
        h1 = (x3 * r * gmix_ref[...]).astype(BF16)
        hk_ref[...] = hk
        h1_ref[...] = h1
        kv_ref[...] = _dot(hk, wkv_ref[...]).astype(BF16)
        q_ref[...] = _dot(h1, wq_ref[...]).astype(BF16)

    wide = jax.ShapeDtypeStruct((T, D_MODEL), BF16)
    return _launch(
        body, name="fwd_ple_qkv", grid=(nt,),
        in_specs=[_row_spec(D_MODEL), _row_spec(D_MODEL), _row_spec(PLE_DIM), _full_spec((D_MODEL, D_MODEL)),
                  _full_spec((PLE_DIM, D_MODEL)), _vec_spec(), _vec_spec(), _vec_spec(),
                  _full_spec((D_MODEL, 2 * KV_DIM)), _full_spec((D_MODEL, D_MODEL))],
        out_specs=[_row_spec(D_MODEL)] * 6 + [_row_spec(2 * KV_DIM)],
        out_shape=[jax.ShapeDtypeStruct((T, D_MODEL), F32)] + [wide] * 5 + [jax.ShapeDtypeStruct((T, 2 * KV_DIM), BF16)],
        args=(x2, h3, p, wgate, wproj, g_post, g_kv, g_mix, wkv, wq), job=job)


def _alibi_slope(h):
    return 2.0 ** (-8.0 * (h + 1) / N_HEADS)


ATT_SUB = 32
ATT_GROUP_ROWS = GQA_GROUP * ATT_BLOCK


def _att_mask(n, rel_ref, off_ref):
    qi = lax.broadcasted_iota(jnp.int32, (ATT_BLOCK, 2 * ATT_BLOCK), 0)
    si = lax.broadcasted_iota(jnp.int32, (ATT_BLOCK, 2 * ATT_BLOCK), 1)
    rel = ATT_BLOCK + qi - si
    valid = (rel >= 0) & (rel < ATT_BLOCK) & ((si >= ATT_BLOCK) | (n > 0))
    rel_ref[...] = rel.astype(F32)
    off_ref[...] = jnp.where(valid, 0.0, NEG_INF)


def _att_probs(raw, relf, off, slope, sink):
    s = raw * ATT_SCALE - slope * relf + off
    m = jnp.maximum(jnp.max(s, axis=-1, keepdims=True), sink)
    e = jnp.exp(s - m)
    es = jnp.exp(sink - m)
    inv = 1.0 / (jnp.sum(e, axis=-1, keepdims=True) + es)
    return e * inv, es * inv


def _stack_heads(ref, kh):
    first = kh * GQA_GROUP
    return jnp.concatenate([ref[:, (first + g) * HEAD_DIM:(first + g + 1) * HEAD_DIM] for g in range(GQA_GROUP)], axis=0)


def _unstack_heads(stacked):
    return [stacked[g * ATT_BLOCK:(g + 1) * ATT_BLOCK, :] for g in range(GQA_GROUP)]


def _fwd_attention(q, kpad, vpad, sinks, job=None):
    T = q.shape[0]
    nb = T // ATT_BLOCK

    def body(q_ref, k_ref, v_ref, sink_ref, o_ref, s_scr, p_scr, rel_scr, off_scr):
        n = pl.program_id(0)
        start = pl.multiple_of(n * ATT_BLOCK, ATT_BLOCK)
        kw = k_ref[pl.ds(start, 2 * ATT_BLOCK), :]
        vw = v_ref[pl.ds(start, 2 * ATT_BLOCK), :]
        _att_mask(n, rel_scr, off_scr)
        outs = []
        for kh in range(N_KV_HEADS):
            kk = kw[:, kh * HEAD_DIM:(kh + 1) * HEAD_DIM]
            vv = vw[:, kh * HEAD_DIM:(kh + 1) * HEAD_DIM]
            s_scr[...] = _dot_nt(_stack_heads(q_ref, kh), kk)
            for g in range(GQA_GROUP):
                h = kh * GQA_GROUP + g
                for row0 in range(0, ATT_BLOCK, ATT_SUB):
                    rows, sub = pl.ds(g * ATT_BLOCK + row0, ATT_SUB), pl.ds(row0, ATT_SUB)
                    pr, _ = _att_probs(s_scr[rows, :], rel_scr[sub, :], off_scr[sub, :], _alibi_slope(h),
                                       sink_ref[0, h])
                    p_scr[rows, :] = pr.astype(BF16)
            outs += _unstack_heads(_dot(p_scr[...], vv))
        o_ref[...] = jnp.concatenate(outs, axis=1).astype(BF16)

    return _launch(
        body, name="fwd_attention", grid=(nb,),
        in_specs=[_row_spec(D_MODEL, ATT_BLOCK), _full_spec((T + ATT_BLOCK, KV_DIM)), _full_spec((T + ATT_BLOCK, KV_DIM)),
                  pl.BlockSpec(memory_space=pltpu.SMEM)],
        out_specs=[_row_spec(D_MODEL, ATT_BLOCK)],
        out_shape=[jax.ShapeDtypeStruct((T, D_MODEL), BF16)],
        scratch_shapes=[pltpu.VMEM((ATT_GROUP_ROWS, 2 * ATT_BLOCK), F32), pltpu.VMEM((ATT_GROUP_ROWS, 2 * ATT_BLOCK), BF16)]
                       + [pltpu.VMEM((ATT_BLOCK, 2 * ATT_BLOCK), F32)] * 2,
        args=(q, kpad, vpad, sinks), job=job)


def _fwd_attn_out(attn, x, wo, g_post, g_ffn, job=None):
    T = x.shape[0]
    nt = T // ROW_TILE

    def body(a_ref, x_ref, wo_ref, gpost_ref, gffn_ref, y_ref, x1_ref, h2_ref):
        y = _dot(a_ref[...], wo_ref[...])
        y_ref[...] = y.astype(BF16)
        x1 = x_ref[...] + _rms(y, gpost_ref[...])
        x1_ref[...] = x1
        h2_ref[...] = _rms(x1, gffn_ref[...]).astype(BF16)

    return _launch(
        body, name="fwd_attn_out", grid=(nt,),
        in_specs=[_row_spec(D_MODEL), _row_spec(D_MODEL), _full_spec((D_MODEL, D_MODEL)), _vec_spec(), _vec_spec()],
        out_specs=[_row_spec(D_MODEL)] * 3,
        out_shape=[jax.ShapeDtypeStruct((T, D_MODEL), BF16), jax.ShapeDtypeStruct((T, D_MODEL), F32),
                   jax.ShapeDtypeStruct((T, D_MODEL), BF16)],
        args=(attn, x, wo, g_post, g_ffn), job=job)


def _bwd_ple(layer, dx3, x2, z, pe, h3, p, f, wgate, g_ple_post, g_ple, g_post_ffn, job=None):
    T = x2.shape[0]
    tm = ROW_TILE
    nt = T // tm

    def body(dx3_ref, x2_ref, z_ref, pe_ref, h3_ref, p_ref, f_ref, wg_ref, gpp_ref, gp_ref, gpf_ref,
             dx2_ref, df_ref, dwg_ref, dwp_ref, dgpp_ref, dgp_ref, dgpf_ref, acc_g, acc_p):
        i = pl.program_id(0)
        first = i == 0
        dx3v = dx3_ref[...]
        gate = _sigmoid(z_ref[...].astype(F32))
        pev = pe_ref[...].astype(F32)
        de, dgpp = _rms_bwd(pev * gate, gpp_ref[...], dx3v)
        dpe = (de * gate).astype(BF16)
        dz = (de * pev * gate * (1.0 - gate)).astype(BF16)
        _acc(acc_p, _dot_tn(p_ref[...].astype(BF16), dpe), first)
        _acc(acc_g, _dot_tn(h3_ref[...], dz), first)
        dh3 = _dot_nt(dz, wg_ref[...])
        dxn, dgp = _rms_bwd(x2_ref[...], gp_ref[...], dh3)
        dx2 = dx3v + dxn
        dx2_ref[...] = dx2
        df, dgpf = _rms_bwd(f_ref[...].astype(F32), gpf_ref[...], dx2)
        df_ref[...] = df.astype(BF16)
        _acc(dgpp_ref, dgpp, first)
        _acc(dgp_ref, dgp, first)
        _acc(dgpf_ref, dgpf, first)

        @pl.when(i == nt - 1)
        def _():
            dwg_ref[...] = acc_g[...].astype(BF16)
            dwp_ref[...] = acc_p[...].astype(BF16)

    return _launch(
        body, name=f"bwd_ple{layer}", grid=(nt,),
        in_specs=[_row_spec(D_MODEL)] * 5 + [_row_spec(PLE_DIM), _row_spec(D_MODEL), _full_spec((D_MODEL, D_MODEL)),
                  _vec_spec(), _vec_spec(), _vec_spec()],
        out_specs=[_row_spec(D_MODEL), _row_spec(D_MODEL), _full_spec((D_MODEL, D_MODEL)), _full_spec((PLE_DIM, D_MODEL)),
                   _vec_spec(), _vec_spec(), _vec_spec()],
        out_shape=[jax.ShapeDtypeStruct((T, D_MODEL), F32), jax.ShapeDtypeStruct((T, D_MODEL), BF16),
                   jax.ShapeDtypeStruct((D_MODEL, D_MODEL), BF16), jax.ShapeDtypeStruct((PLE_DIM, D_MODEL), BF16)]
                  + [jax.ShapeDtypeStruct((1, D_MODEL), F32)] * 3,
        scratch_shapes=[pltpu.VMEM((D_MODEL, D_MODEL), F32), pltpu.VMEM((PLE_DIM, D_MODEL), F32)],
        args=(dx3, x2, z, pe, h3, p, f, wgate, g_ple_post, g_ple, g_post_ffn), vmem=VMEM_BIG, job=job)


def _ple_loss_bwd(layer, x2, h3, p, f, target, wgate, wproj, g_ple_post, g_ple, g_post_ffn, job=None):
    T = x2.shape[0]
    tm = ROW_TILE
    nt = T // tm

    def body(x2_ref, h3_ref, p_ref, f_ref, tgt_ref, wg_ref, wp_ref, gpp_ref, gp_ref, gpf_ref,
             dx2_ref, df_ref, dwg_ref, dwp_ref, dgpp_ref, dgp_ref, dgpf_ref, loss_ref, acc_g, acc_p):
        i = pl.program_id(0)
        first = i == 0
        h3 = h3_ref[...]
        pb = p_ref[...].astype(BF16)
        x2v = x2_ref[...]
        gate = _sigmoid(_dot(h3, wg_ref[...]))
        pev = _dot(pb, wp_ref[...])
        e = pev * gate
        err = x2v + _rms(e, gpp_ref[...]) - tgt_ref[...]
        _acc(loss_ref, 0.5 * jnp.sum(jnp.mean(err * err, axis=-1, keepdims=True), axis=0, keepdims=True), first)
        dx3v = err * (1.0 / D_MODEL)
        de, dgpp = _rms_bwd(e, gpp_ref[...], dx3v)
        dpe = (de * gate).astype(BF16)
        dz = (de * pev * gate * (1.0 - gate)).astype(BF16)
        _acc(acc_p, _dot_tn(pb, dpe), first)
        _acc(acc_g, _dot_tn(h3, dz), first)
        dxn, dgp = _rms_bwd(x2v, gp_ref[...], _dot_nt(dz, wg_ref[...]))
        dx2 = dx3v + dxn
        dx2_ref[...] = dx2
        df, dgpf = _rms_bwd(f_ref[...].astype(F32), gpf_ref[...], dx2)
        df_ref[...] = df.astype(BF16)
        _acc(dgpp_ref, dgpp, first)
        _acc(dgp_ref, dgp, first)
        _acc(dgpf_ref, dgpf, first)

        @pl.when(i == nt - 1)
        def _():
            dwg_ref[...] = acc_g[...].astype(BF16)
            dwp_ref[...] = acc_p[...].astype(BF16)

    return _launch(
        body, name=f"ple_loss_bwd{layer}", grid=(nt,),
        in_specs=[_row_spec(D_MODEL), _row_spec(D_MODEL), _row_spec(PLE_DIM), _row_spec(D_MODEL), _row_spec(D_MODEL),
                  _full_spec((D_MODEL, D_MODEL)), _full_spec((PLE_DIM, D_MODEL)), _vec_spec(), _vec_spec(), _vec_spec()],
        out_specs=[_row_spec(D_MODEL), _row_spec(D_MODEL), _full_spec((D_MODEL, D_MODEL)), _full_spec((PLE_DIM, D_MODEL)),
                   _vec_spec(), _vec_spec(), _vec_spec(), _full_spec((1, 1))],
        out_shape=[jax.ShapeDtypeStruct((T, D_MODEL), F32), jax.ShapeDtypeStruct((T, D_MODEL), BF16),
                   jax.ShapeDtypeStruct((D_MODEL, D_MODEL), BF16), jax.ShapeDtypeStruct((PLE_DIM, D_MODEL), BF16)]
                  + [jax.ShapeDtypeStruct((1, D_MODEL), F32)] * 3 + [jax.ShapeDtypeStruct((1, 1), F32)],
        scratch_shapes=[pltpu.VMEM((D_MODEL, D_MODEL), F32), pltpu.VMEM((PLE_DIM, D_MODEL), F32)],
        args=(x2, h3, p, f, target, wgate, wproj, g_ple_post, g_ple, g_post_ffn), vmem=VMEM_BIG, job=job)


def _bwd_ffn_act(layer, df, gs, us, wgu, wd, job=None):
    T = df.shape[0]
    tm = min(FFN_ROW_TILE, T)
    nt = T // tm
    sub = tm // FFN_SUB_TILES
    last = FF_CHUNKS - 1
    wgu, wd = _column_views(wgu), _column_views(wd)
    n_gu, n_wd = len(wgu), len(wd)
    wd_cols = _column_ranges(wd)

    def body(df_ref, gs_ref, us_ref, *refs):
        wgu_refs, wd_refs = refs[:n_gu], refs[n_gu:n_gu + n_wd]
        dh_ref, dg_ref, du_ref, a_ref, acc_h = refs[n_gu + n_wd:]
        k = pl.program_id(0)
        i = pl.program_id(1)
        rows = pl.ds(pl.multiple_of(i * tm, tm), tm)
        dhs = []
        for s in range(FFN_SUB_TILES):
            r = pl.ds(s * sub, sub)
            g = gs_ref[r, :].astype(F32)
            u = us_ref[r, :].astype(F32)
            sg = _sigmoid(g)
            silu = g * sg
            a_ref[r, :] = (silu * u).astype(BF16)
            da = _add_all([_dot_nt(df_ref[r, c0:c1], w[...]) for (c0, c1), w in zip(wd_cols, wd_refs)])
            dg = (da * u * (sg * (1.0 + g * (1.0 - sg)))).astype(BF16)
            du = (da * silu).astype(BF16)
            dg_ref[r, :] = dg
            du_ref[r, :] = du
            dhs.append(jnp.concatenate([_dot(dg, w[0]) + _dot(du, w[1]) for w in wgu_refs], axis=1))
        dh = jnp.concatenate(dhs, axis=0)

        @pl.when(k == 0)
        def _():
            acc_h[rows, :] = dh

        @pl.when(jnp.logical_and(k > 0, k < last))
        def _():
            acc_h[rows, :] += dh

        @pl.when(k == last)
        def _():
            dh_ref[...] = acc_h[rows, :] + dh

    chunk_rows = pl.BlockSpec((None, tm, FF_BLOCK), lambda k, i: (k, i, 0))
    saved = jax.ShapeDtypeStruct((FF_CHUNKS, T, FF_BLOCK), BF16)
    return _launch(
        body, name=f"bwd_ffn_act{layer}", grid=(FF_CHUNKS, nt),
        in_specs=[pl.BlockSpec((tm, D_MODEL), lambda k, i: (i, 0)), chunk_rows, chunk_rows]
                 + [pl.BlockSpec((None, 2, FF_BLOCK, FFN_WEIGHT_COLS), lambda k, i, b=b: (k, 0, 0, b)) for _, b in wgu]
                 + [pl.BlockSpec((FF_BLOCK, FFN_WEIGHT_COLS), lambda k, i, b=b: (k, b)) for _, b in wd],
        out_specs=[pl.BlockSpec((tm, D_MODEL), lambda k, i: (jnp.where(k == last, i, 0), 0)),
                   chunk_rows, chunk_rows, chunk_rows],
        out_shape=[jax.ShapeDtypeStruct((T, D_MODEL), F32), saved, saved, saved],
        scratch_shapes=[pltpu.VMEM((T, D_MODEL), F32)],
        args=(df, gs, us, *[w for w, _ in wgu], *[w for w, _ in wd]), vmem=VMEM_BIG, job=job)


def _bwd_ffn_dw(layer, q, parts, h2, df, dg, du, a, job=None):
    T = h2.shape[0]
    width = D_MODEL // parts

    def body(h_ref, df_ref, dg_ref, du_ref, a_ref, dgu_ref, dwd_ref):
        h = h_ref[...]
        dgu_ref[0] = _dot_tn(dg_ref[...], h).astype(BF16)
        dgu_ref[1] = _dot_tn(du_ref[...], h).astype(BF16)
        dwd_ref[...] = _dot_tn(a_ref[...], df_ref[...]).astype(BF16)

    cols = pl.BlockSpec((T, width), lambda k: (0, q))
    chunk = pl.BlockSpec((None, T, FF_BLOCK), lambda k: (k, 0, 0))
    return _launch(
        body, name=f"bwd_ffn_dw{layer}_{q}", grid=(FF_CHUNKS,),
        in_specs=[cols, cols, chunk, chunk, chunk],
        out_specs=[pl.BlockSpec((None, 2, FF_BLOCK, width), lambda k: (k, 0, 0, 0)),
                   pl.BlockSpec((FF_BLOCK, width), lambda k: (k, 0))],
        out_shape=[jax.ShapeDtypeStruct((FF_CHUNKS, 2, FF_BLOCK, width), BF16),
                   jax.ShapeDtypeStruct((D_FF, width), BF16)],
        args=(h2, df, dg, du, a), vmem=VMEM_BIG, job=job)


def _bwd_attn_out(dx2, dh2, x1, y, attn, wo, g_ffn, g_post, job=None):
    T = x1.shape[0]
    nt = T // ROW_TILE

    def body(dx2_ref, dh2_ref, x1_ref, y_ref, a_ref, wo_ref, gffn_ref, gpost_ref,
             dx1_ref, da_ref, dwo_ref, dgf_ref, dgp_ref, acc):
        i = pl.program_id(0)
        first = i == 0
        dxn, dgf = _rms_bwd(x1_ref[...], gffn_ref[...], dh2_ref[...])
        dx1 = dx2_ref[...] + dxn
        dx1_ref[...] = dx1
        dy, dgp = _rms_bwd(y_ref[...].astype(F32), gpost_ref[...], dx1)
        dyb = dy.astype(BF16)
        da_ref[...] = _dot_nt(dyb, wo_ref[...]).astype(BF16)
        _acc(acc, _dot_tn(a_ref[...], dyb), first)
        _acc(dgf_ref, dgf, first)
        _acc(dgp_ref, dgp, first)

        @pl.when(i == nt - 1)
        def _():
            dwo_ref[...] = acc[...].astype(BF16)

    return _launch(
        body, name="bwd_attn_out", grid=(nt,),
        in_specs=[_row_spec(D_MODEL)] * 5 + [_full_spec((D_MODEL, D_MODEL)), _vec_spec(), _vec_spec()],
        out_specs=[_row_spec(D_MODEL), _row_spec(D_MODEL), _full_spec((D_MODEL, D_MODEL)), _vec_spec(), _vec_spec()],
        out_shape=[jax.ShapeDtypeStruct((T, D_MODEL), F32), jax.ShapeDtypeStruct((T, D_MODEL), BF16),
                   jax.ShapeDtypeStruct((D_MODEL, D_MODEL), BF16)] + [jax.ShapeDtypeStruct((1, D_MODEL), F32)] * 2,
        scratch_shapes=[pltpu.VMEM((D_MODEL, D_MODEL), F32)],
        args=(dx2, dh2, x1, y, attn, wo, g_ffn, g_post), job=job)


def _bwd_attention(q, dattn, kpad, vpad, sinks, job=None):
    T = q.shape[0]
    nb = T // ATT_BLOCK

    def body(q_ref, do_ref, k_ref, v_ref, sink_ref, dq_ref, dk_ref, dv_ref, ds_ref, s_scr, dp_scr, p_scr, dsb_scr,
             rel_scr, off_scr):
        n = pl.program_id(0)
        _att_mask(n, rel_scr, off_scr)

        @pl.when(n == 0)
        def _():
            dk_ref[...] = jnp.zeros_like(dk_ref)
            dv_ref[...] = jnp.zeros_like(dv_ref)
            ds_ref[...] = jnp.zeros_like(ds_ref)

        start = pl.multiple_of(n * ATT_BLOCK, ATT_BLOCK)
        win = pl.ds(start, 2 * ATT_BLOCK)
        kw = k_ref[win, :]
        vw = v_ref[win, :]
        lane = lax.broadcasted_iota(jnp.int32, (1, ATT_BLOCK), 1)
        dsink = jnp.zeros((1, ATT_BLOCK), F32)
        dqs, dks, dvs = [], [], []
        for kh in range(N_KV_HEADS):
            kk = kw[:, kh * HEAD_DIM:(kh + 1) * HEAD_DIM]
            vv = vw[:, kh * HEAD_DIM:(kh + 1) * HEAD_DIM]
            qs = _stack_heads(q_ref, kh)
            dos = _stack_heads(do_ref, kh)
            s_scr[...] = _dot_nt(qs, kk)
            dp_scr[...] = _dot_nt(dos, vv)
            for g in range(GQA_GROUP):
                h = kh * GQA_GROUP + g
                dsink_h = jnp.zeros((1, 1), F32)
                for row0 in range(0, ATT_BLOCK, ATT_SUB):
                    rows, sub = pl.ds(g * ATT_BLOCK + row0, ATT_SUB), pl.ds(row0, ATT_SUB)
                    pr, ps = _att_probs(s_scr[rows, :], rel_scr[sub, :], off_scr[sub, :], _alibi_slope(h),
                                        sink_ref[0, h])
                    dp = dp_scr[rows, :]
                    delta = jnp.sum(pr * dp, axis=-1, keepdims=True)
                    dsb_scr[rows, :] = (pr * (dp - delta) * ATT_SCALE).astype(BF16)
                    p_scr[rows, :] = pr.astype(BF16)
                    dsink_h = dsink_h - jnp.sum(ps * delta, axis=0, keepdims=True)
                dsink = dsink + jnp.where(lane == h, dsink_h, 0.0)
            dsb = dsb_scr[...]
            dqs += _unstack_heads(_dot(dsb, kk))
            dks.append(_dot_tn(dsb, qs))
            dvs.append(_dot_tn(p_scr[...], dos))
        dq_ref[...] = jnp.concatenate(dqs, axis=1).astype(BF16)
        dk_ref[win, :] += jnp.concatenate(dks, axis=1)
        dv_ref[win, :] += jnp.concatenate(dvs, axis=1)
        ds_ref[...] += dsink

    return _launch(
        body, name="bwd_attention", grid=(nb,),
        in_specs=[_row_spec(D_MODEL, ATT_BLOCK), _row_spec(D_MODEL, ATT_BLOCK), _full_spec((T + ATT_BLOCK, KV_DIM)),
                  _full_spec((T + ATT_BLOCK, KV_DIM)), pl.BlockSpec(memory_space=pltpu.SMEM)],
        out_specs=[_row_spec(D_MODEL, ATT_BLOCK), _full_spec((T + ATT_BLOCK, KV_DIM)), _full_spec((T + ATT_BLOCK, KV_DIM)),
                   _full_spec((1, ATT_BLOCK))],
        out_shape=[jax.ShapeDtypeStruct((T, D_MODEL), BF16), jax.ShapeDtypeStruct((T + ATT_BLOCK, KV_DIM), F32),
                   jax.ShapeDtypeStruct((T + ATT_BLOCK, KV_DIM), F32), jax.ShapeDtypeStruct((1, ATT_BLOCK), F32)],
        scratch_shapes=[pltpu.VMEM((ATT_GROUP_ROWS, 2 * ATT_BLOCK), F32)] * 2
                       + [pltpu.VMEM((ATT_GROUP_ROWS, 2 * ATT_BLOCK), BF16)] * 2
                       + [pltpu.VMEM((ATT_BLOCK, 2 * ATT_BLOCK), F32)] * 2,
        args=(q, dattn, kpad, vpad, sinks), vmem=VMEM_BIG, job=job)


def _bwd_qkv(dxres, dq, dkv, x3, h1, hk, wq, wkv, g_mix, g_kv, job=None):
    T = x3.shape[0]
    nt = T // ROW_TILE

    def body(dxr_ref, dq_ref, dkv_ref, x_ref, h1_ref, hk_ref, wq_ref, wkv_ref, gmix_ref, gkv_ref,
             dx_ref, dwq_ref, dwkv_ref, dgm_ref, dgk_ref, acc_q, acc_kv):
        i = pl.program_id(0)
        first = i == 0
        dqv = dq_ref[...]
        dkvv = dkv_ref[...]
        xv = x_ref[...]
        d1, dgm = _rms_bwd(xv, gmix_ref[...], _dot_nt(dqv, wq_ref[...]))
        d2, dgk = _rms_bwd(xv, gkv_ref[...], _dot_nt(dkvv, wkv_ref[...]))
        dx_ref[...] = dxr_ref[...] + d1 + d2
        _acc(acc_q, _dot_tn(h1_ref[...], dqv), first)
        _acc(acc_kv, _dot_tn(hk_ref[...], dkvv), first)
        _acc(dgm_ref, dgm, first)
        _acc(dgk_ref, dgk, first)

        @pl.when(i == nt - 1)
        def _():
            dwq_ref[...] = acc_q[...].astype(BF16)
            dwkv_ref[...] = acc_kv[...].astype(BF16)

    return _launch(
        body, name="bwd_qkv", grid=(nt,),
        in_specs=[_row_spec(D_MODEL), _row_spec(D_MODEL), _row_spec(2 * KV_DIM), _row_spec(D_MODEL), _row_spec(D_MODEL),
                  _row_spec(D_MODEL), _full_spec((D_MODEL, D_MODEL)), _full_spec((D_MODEL, 2 * KV_DIM)), _vec_spec(),
                  _vec_spec()],
        out_specs=[_row_spec(D_MODEL), _full_spec((D_MODEL, D_MODEL)), _full_spec((D_MODEL, 2 * KV_DIM)), _vec_spec(),
                   _vec_spec()],
        out_shape=[jax.ShapeDtypeStruct((T, D_MODEL), F32), jax.ShapeDtypeStruct((D_MODEL, D_MODEL), BF16),
                   jax.ShapeDtypeStruct((D_MODEL, 2 * KV_DIM), BF16)] + [jax.ShapeDtypeStruct((1, D_MODEL), F32)] * 2,
        scratch_shapes=[pltpu.VMEM((D_MODEL, D_MODEL), F32), pltpu.VMEM((D_MODEL, 2 * KV_DIM), F32)],
        args=(dxres, dq, dkv, x3, h1, hk, wq, wkv, g_mix, g_kv), job=job)


def _bwd_pool_mixer(dx2, dh2, x1, x, yraw, d, wp, scale, g_ffn, g_post, g_pre, job=None):
    T = x.shape[0]
    tm = ROW_TILE
    nt = T // tm

    def body(dx2_ref, dh2_ref, x1_ref, x_ref, yraw_ref, d_ref, wp_ref, sc_ref, gffn_ref, gpost_ref, gpre_ref,
             dx_ref, dwp_ref, dsc_ref, dgf_ref, dgp_ref, dgm_ref, carry, acc):
        i = pl.program_id(0)
        first = i == 0
        tile = nt - 1 - i

        @pl.when(first)
        def _():
            carry[...] = jnp.zeros_like(carry)

        dxn, dgf = _rms_bwd(x1_ref[...], gffn_ref[...], dh2_ref[...])
        dx1 = dx2_ref[...] + dxn
        yraw = yraw_ref[...].astype(F32)
        sc = sc_ref[...]
        dy, dgp = _rms_bwd(yraw * sc, gpost_ref[...], dx1)
        dsc = jnp.sum(dy * yraw, axis=0, keepdims=True)
        dyb = (dy * sc).astype(BF16)
        dv = d_ref[...]
        dds = []
        for g in range(N_POOL_GROUPS):
            cols = slice(g * POOL_GROUP, (g + 1) * POOL_GROUP)
            dds.append(_dot_nt(dyb[:, cols], wp_ref[g]))
            _acc(acc.at[g], _dot_tn(dv[:, cols], dyb[:, cols]), first)
        dd = jnp.concatenate(dds, axis=1)
        e = dd / _pool_counts(tile * tm, tm)
        ext = jnp.concatenate([e, carry[...]], axis=0)
        carry[...] = e[:POOL_HALO, :]
        sums = _window_sums(ext, lambda k: tm + POOL_HALO - k)[:tm, :]
        dxm, dgm = _rms_bwd(x_ref[...], gpre_ref[...], sums - dd)
        dx_ref[...] = dx1 + dxm
        _acc(dsc_ref, dsc, first)
        _acc(dgf_ref, dgf, first)
        _acc(dgp_ref, dgp, first)
        _acc(dgm_ref, dgm, first)

        @pl.when(i == nt - 1)
        def _():
            dwp_ref[...] = acc[...].astype(BF16)

    rev = pl.BlockSpec((tm, D_MODEL), lambda i: (nt - 1 - i, 0))
    return _launch(
        body, name="bwd_pool_mixer", grid=(nt,),
        in_specs=[rev] * 6 + [_full_spec((N_POOL_GROUPS, POOL_GROUP, POOL_GROUP))] + [_vec_spec()] * 4,
        out_specs=[rev, _full_spec((N_POOL_GROUPS, POOL_GROUP, POOL_GROUP))] + [_vec_spec()] * 4,
        out_shape=[jax.ShapeDtypeStruct((T, D_MODEL), F32),
                   jax.ShapeDtypeStruct((N_POOL_GROUPS, POOL_GROUP, POOL_GROUP), BF16)]
                  + [jax.ShapeDtypeStruct((1, D_MODEL), F32)] * 4,
        scratch_shapes=[pltpu.VMEM((POOL_HALO, D_MODEL), F32), pltpu.VMEM((N_POOL_GROUPS, POOL_GROUP, POOL_GROUP), F32)],
        args=(dx2, dh2, x1, x, yraw, d, wp, scale, g_ffn, g_post, g_pre), job=job)


def _my_place():
    return lax.axis_index("x"), lax.axis_index("y"), lax.axis_index("c")


def _dev_index(px, py, pc):
    return 4 * px + 2 * py + pc


def _peer_by_relation(r):
    x, y, c = _my_place()
    return (x ^ ((r >> 2) & 1), y ^ ((r >> 1) & 1), c ^ (r & 1))


def _slot_pool(ref, j):
    return ref.at[:, pl.ds(pl.multiple_of(j * 32, 32), 32), :]


def _slot_scale(ref, j):
    return ref.at[:, pl.ds(pl.multiple_of(j * 128, 128), 128)]


def _slot_rows128(ref, j):
    return ref.at[pl.ds(pl.multiple_of(j * 128, 128), 128), :]


def _slot_gu(ref, j):
    return ref.at[j % FF_CHUNKS, j // FF_CHUNKS]


def _slot_wd(ref, j):
    return ref.at[pl.ds(pl.multiple_of(j * WD_ROWS, 16), WD_ROWS), :]


def _slot_cols128(ref, j):
    return ref.at[:, pl.ds(pl.multiple_of(j * 128, 128), 128)]


_GATHERED = {
    "pool": ((N_POOL_GROUPS, POOL_GROUP, POOL_GROUP), BF16, _slot_pool),
    "scale": ((1, D_MODEL), F32, _slot_scale),
    "kv": ((D_MODEL, 2 * KV_DIM), BF16, _slot_rows128),
    "q": ((D_MODEL, D_MODEL), BF16, _slot_rows128),
    "o": ((D_MODEL, D_MODEL), BF16, _slot_rows128),
    "gu": ((FF_CHUNKS, 2, FF_BLOCK, D_MODEL), BF16, _slot_gu),
    "wd": ((D_FF, D_MODEL), BF16, _slot_wd),
    "guh": ((FF_CHUNKS, 2, FF_BLOCK, D_MODEL // 2), BF16, _slot_gu),
    "wdh": ((D_FF, D_MODEL // 2), BF16, _slot_wd),
    "gate": ((D_MODEL, D_MODEL), BF16, _slot_rows128),
    "proj": ((PLE_DIM, D_MODEL), BF16, _slot_cols128),
}


def _no_compute():
    pass


class _AllGather:
    peers = ("sibling", "x", "y")

    def __init__(self, names, shards):
        self.kinds = [_GATHERED[n.rstrip("01_")] for n in names]
        entries = [shards[n] if isinstance(shards[n], tuple) else (shards[n], None) for n in names]
        self.args = [array for array, _ in entries]
        self.columns = [columns for _, columns in entries]
        self.out_shape = [jax.ShapeDtypeStruct(shape, dtype) for shape, dtype, _ in self.kinds]
        n = len(names)
        self.scratch = [pltpu.SemaphoreType.DMA((n, 7)), pltpu.SemaphoreType.DMA((n, 7)), pltpu.SemaphoreType.DMA((n,))]

    def _plan(self, srcs, outs, sems):
        send_sems, recv_sems, local_sems = sems
        x, y, c = _my_place()

        def slot(t, dev):
            return self.kinds[t][2](outs[t], _dev_index(*dev))

        def copy(t, k, block, to, src=None):
            return pltpu.make_async_remote_copy(
                src_ref=slot(t, block) if src is None else src, dst_ref=slot(t, block),
                send_sem=send_sems.at[t, k], recv_sem=recv_sems.at[t, k], device_id=to, device_id_type=MESH)

        return types.SimpleNamespace(
            copy=copy, core=c, me=(x, y, c), sibling=(x, y, 1 - c),
            x_chip=(1 - x, y), y_chip=(x, 1 - y), far_chip=(1 - x, 1 - y),
            via=(x ^ (1 - c), y ^ c),
            onto=(x ^ c, y ^ (1 - c)),
            k_via=1 + c, k_onto=2 - c,
            local=[pltpu.make_async_copy(self._shard(srcs, t), slot(t, (x, y, c)), local_sems.at[t])
                   for t in range(len(srcs))])

    def _shard(self, srcs, t):
        if self.columns[t] is None:
            return srcs[t]
        first, end = self.columns[t]
        return srcs[t].at[:, first:end]

    def start(self, srcs, outs, sems):
        p = self._plan(srcs, outs, sems)
        for cp in p.local:
            cp.start()
        for t in range(len(srcs)):
            shard = self._shard(srcs, t)
            p.copy(t, 0, p.me, p.sibling, src=shard).start()
            p.copy(t, 1, p.me, (*p.x_chip, p.core), src=shard).start()
            p.copy(t, 2, p.me, (*p.y_chip, p.core), src=shard).start()

    def mid(self, srcs, outs, sems):
        p = self._plan(srcs, outs, sems)
        for t in range(len(srcs)):
            block = (*p.via, p.core)
            p.copy(t, p.k_via, block, p.me).wait_recv()
            p.copy(t, 3, block, (*p.onto, p.core)).start()
            p.copy(t, 3 + p.k_via, block, p.sibling).start()

    def late(self, srcs, outs, sems):
        p = self._plan(srcs, outs, sems)
        n = len(srcs)
        for t in range(n):
            block = (*p.onto, p.core)
            p.copy(t, p.k_onto, block, p.me).wait_recv()
            p.copy(t, 3 + p.k_onto, block, p.sibling).start()
        for t in range(n):
            block = (*p.far_chip, p.core)
            p.copy(t, 3, block, p.me).wait_recv()
            p.copy(t, 6, block, p.sibling).start()

    def finish(self, srcs, outs, sems):
        p = self._plan(srcs, outs, sems)
        n = len(srcs)
        other = 1 - p.core
        for t in range(n):
            p.copy(t, 0, (*p.me[:2], other), p.me).wait_recv()
            for k, chip in ((4, p.x_chip), (5, p.y_chip), (6, p.far_chip)):
                p.copy(t, k, (*chip, other), p.me).wait_recv()
            for k in range(7):
                p.copy(t, k, p.me, p.sibling).wait_send()
        for cp in p.local:
            cp.wait()


def _jobs_only(name, job=None):
    return _launch(_no_compute, name=name, grid=(), in_specs=[], out_specs=[], out_shape=[], args=(), job=job)


def _block_pool(ref, j):
    return ref.at[:, pl.ds(pl.multiple_of(j * 32, 32), 32), :]


def _block_rows128(ref, j):
    return ref.at[pl.ds(pl.multiple_of(j * 128, 128), 128), :]


def _block_gu(ref, j):
    return ref.at[j % FF_CHUNKS, j // FF_CHUNKS]


def _block_wd(ref, j):
    return ref.at[pl.ds(pl.multiple_of(j * WD_ROWS, 16), WD_ROWS), :]


def _block_cols128(ref, j):
    return ref.at[:, pl.ds(pl.multiple_of(j * 128, 128), 128)]


_SCATTERED = {
    "pool": ((N_POOL_GROUPS, 32, POOL_GROUP), _block_pool),
    "kv": ((128, 2 * KV_DIM), _block_rows128),
    "q": ((128, D_MODEL), _block_rows128),
    "o": ((128, D_MODEL), _block_rows128),
    "gu": ((FF_BLOCK, FF_PART), _block_gu),
    "wd": ((WD_ROWS, FF_PART), _block_wd),
    "guA": ((FF_BLOCK, FF_PART), lambda ref, j: _block_gu(ref, j).at[:, :FF_PART]),
    "guB": ((FF_BLOCK, FF_PART), lambda ref, j: _block_gu(ref, j).at[:, FF_PART:]),
    "wdA": ((WD_ROWS, FF_PART), lambda ref, j: _block_wd(ref, j).at[:, :FF_PART]),
    "wdB": ((WD_ROWS, FF_PART), lambda ref, j: _block_wd(ref, j).at[:, FF_PART:]),
    "gate": ((128, D_MODEL), _block_rows128),
    "proj": ((PLE_DIM, 128), _block_cols128),
}


class _SiblingSwap:
    peers = ("sibling",)

    def __init__(self, pieces):
        self.kinds = [_SCATTERED[kind] for kind, _ in pieces]
        self.args = [g for _, g in pieces]
        self.out_shape = [jax.ShapeDtypeStruct((N_CHIPS, *block), BF16) for block, _ in self.kinds]
        n = len(pieces)
        self.scratch = [pltpu.SemaphoreType.DMA((n, N_CHIPS)), pltpu.SemaphoreType.DMA((n, N_CHIPS))]

    def _copies(self, srcs, outs, sems):
        send_sems, recv_sems = sems
        x, y, c = _my_place()
        return [pltpu.make_async_remote_copy(
            src_ref=block(srcs[t], 2 * ch + 1 - c), dst_ref=outs[t].at[ch], send_sem=send_sems.at[t, ch],
            recv_sem=recv_sems.at[t, ch], device_id=(x, y, 1 - c), device_id_type=MESH)
            for t, (_, block) in enumerate(self.kinds) for ch in range(N_CHIPS)]

    def start(self, srcs, outs, sems):
        for cp in self._copies(srcs, outs, sems):
            cp.start()

    def finish(self, srcs, outs, sems):
        for cp in self._copies(srcs, outs, sems):
            cp.wait()


class _ChipScatter:
    N_BUFS = 4
    peers = ("x", "y")

    def __init__(self, pieces):
        self.kinds = [_SCATTERED[kind] for kind, _, _ in pieces]
        self.n = n = len(pieces)
        self.args = [g for _, g, _ in pieces] + [s for _, _, s in pieces]
        self.out_shape = [jax.ShapeDtypeStruct((2, *block), BF16) for block, _ in self.kinds]
        self.scratch = []
        for block, _ in self.kinds:
            self.scratch += [pltpu.VMEM((N_CHIPS, *block), BF16)] * 3 + [pltpu.VMEM((2, *block), BF16)]
        dma = pltpu.SemaphoreType.DMA
        self.scratch += [dma((n, N_CHIPS + 1)), dma((n, 2)), dma((n, 2)), dma((n,)), dma((n,)), dma((n,))]

    def _plan(self, outs, scr):
        n = self.n
        first_send, first_recv, second_send, second_recv, keep_sems = scr[self.N_BUFS * n + 1:]
        x, y, c = _my_place()
        via = (x ^ (1 - c), y ^ c)
        onto = (x ^ c, y ^ (1 - c))
        index = lambda chip: 2 * chip[0] + chip[1]
        first, second, keep = [], [], []
        for t in range(n):
            total, inbox = scr[self.N_BUFS * t + 2], scr[self.N_BUFS * t + 3]
            for k, chip in enumerate((via, (1 - x, 1 - y))):
                first.append(pltpu.make_async_remote_copy(
                    src_ref=total.at[index(chip)], dst_ref=inbox.at[k], send_sem=first_send.at[t, k],
                    recv_sem=first_recv.at[t, k], device_id=(*via, c), device_id_type=MESH))
            second.append(pltpu.make_async_remote_copy(
                src_ref=total.at[index(onto)], dst_ref=outs[t].at[1], send_sem=second_send.at[t],
                recv_sem=second_recv.at[t], device_id=(*onto, c), device_id_type=MESH))
            keep.append(pltpu.make_async_copy(total.at[index((x, y))], outs[t].at[0], keep_sems.at[t]))
        return first, second, keep, index((x, y)), index(onto)

    def start(self, ins, outs, scr):
        n = self.n
        load_sems = scr[self.N_BUFS * n]
        c = lax.axis_index("c")
        loads = []
        for t, (_, block) in enumerate(self.kinds):
            mine, theirs = scr[self.N_BUFS * t], scr[self.N_BUFS * t + 1]
            loads += [pltpu.make_async_copy(block(ins[t], 2 * ch + c), mine.at[ch], load_sems.at[t, ch])
                      for ch in range(N_CHIPS)]
            loads.append(pltpu.make_async_copy(ins[n + t], theirs, load_sems.at[t, N_CHIPS]))
        for cp in loads:
            cp.start()
        for cp in loads:
            cp.wait()
        for t in range(n):
            mine, theirs, total = scr[self.N_BUFS * t:self.N_BUFS * t + 3]
            for ch in range(N_CHIPS):
                total[ch] = (mine[ch].astype(F32) + theirs[ch].astype(F32)).astype(BF16)
        for cp in self._plan(outs, scr)[0]:
            cp.start()

    def mid(self, ins, outs, scr):
        first, second, keep, me, onto = self._plan(outs, scr)
        for cp in first:
            cp.wait_recv()
        for t in range(self.n):
            total, inbox = scr[self.N_BUFS * t + 2], scr[self.N_BUFS * t + 3]
            for k, slot in enumerate((me, onto)):
                total[slot] = (total[slot].astype(F32) + inbox[k].astype(F32)).astype(BF16)
        for cp in second + keep:
            cp.start()

    def finish(self, ins, outs, scr):
        first, second, keep, _, _ = self._plan(outs, scr)
        for cp in first:
            cp.wait_send()
        for cp in second + keep:
            cp.wait()


class _ToEveryone:
    peers = _EVERYONE

    def __init__(self, scattered=(), gathered=()):
        self.blocks = [_SCATTERED[kind][1] for kind, _ in scattered] + [None] * len(gathered)
        self.args = [g for _, g in scattered] + list(gathered)
        self.out_shape = [jax.ShapeDtypeStruct((N_DEV, *_SCATTERED[kind][0]), BF16) for kind, _ in scattered]
        self.out_shape += [jax.ShapeDtypeStruct((N_DEV, *a.shape), a.dtype) for a in gathered]
        n = len(self.args)
        self.scratch = [pltpu.SemaphoreType.DMA((n, N_DEV - 1)), pltpu.SemaphoreType.DMA((n, N_DEV - 1)),
                        pltpu.SemaphoreType.DMA((n,))]

    def _copies(self, srcs, outs, sems):
        send_sems, recv_sems, local_sems = sems
        me = _dev_index(*_my_place())
        copies = []
        for t, block in enumerate(self.blocks):
            part = (lambda j, t=t, block=block: srcs[t] if block is None else block(srcs[t], j))
            copies.append(pltpu.make_async_copy(part(me), outs[t].at[me], local_sems.at[t]))
            for r in range(1, N_DEV):
                peer = _peer_by_relation(r)
                copies.append(pltpu.make_async_remote_copy(
                    src_ref=part(_dev_index(*peer)), dst_ref=outs[t].at[me], send_sem=send_sems.at[t, r - 1],
                    recv_sem=recv_sems.at[t, r - 1], device_id=peer, device_id_type=MESH))
        return copies

    def start(self, srcs, outs, sems):
        for cp in self._copies(srcs, outs, sems):
            cp.start()

    def finish(self, srcs, outs, sems):
        for cp in self._copies(srcs, outs, sems):
            cp.wait()


class _Jobs:
    def __init__(self, *jobs):
        self.jobs = jobs
        together = {p for j in jobs for p in j.peers}
        self.peers = tuple(p for p in _EVERYONE if p in together)
        self.args = [a for j in jobs for a in j.args]
        self.out_shape = [o for j in jobs for o in j.out_shape]
        self.scratch = [s for j in jobs for s in j.scratch]

    def _split(self, refs, attr):
        at = 0
        for j in self.jobs:
            n = len(getattr(j, attr))
            yield refs[at:at + n]
            at += n

    def _each(self, ins, outs, scr):
        return zip(self.jobs, self._split(ins, "args"), self._split(outs, "out_shape"), self._split(scr, "scratch"))

    def start(self, ins, outs, scr):
        for j, i, o, s in self._each(ins, outs, scr):
            j.start(i, o, s)

    def mid(self, ins, outs, scr):
        for j, i, o, s in self._each(ins, outs, scr):
            if hasattr(j, "mid"):
                j.mid(i, o, s)

    def late(self, ins, outs, scr):
        for j, i, o, s in self._each(ins, outs, scr):
            if hasattr(j, "late"):
                j.late(i, o, s)

    def finish(self, ins, outs, scr):
        for j, i, o, s in self._each(ins, outs, scr):
            j.finish(i, o, s)

    def split_outputs(self, outs):
        return list(self._split(outs, "out_shape"))


def _adamw_math(w, g, m, v):
    m = ADAM_B1 * m + (1.0 - ADAM_B1) * g
    v = ADAM_B2 * v + (1.0 - ADAM_B2) * (g * g)
    m_hat = m / (1.0 - ADAM_B1 ** ADAM_STEP)
    v_hat = v / (1.0 - ADAM_B2 ** ADAM_STEP)
    delta = -ADAM_LR * (m_hat / (jnp.sqrt(v_hat) + ADAM_EPS) + ADAM_WD * w)
    return delta, m, v


def _adamw(name, w, m, v, landings, n_col_blocks=1, job=None):
    n_slots, r, c = landings[0].shape
    grid = (w.shape[0] // r, n_col_blocks)

    def body(w_ref, m_ref, v_ref, *rest):
        l_refs, (g_ref, d_ref, nm_ref, nv_ref) = rest[:len(landings)], rest[len(landings):]
        step = pl.program_id(0) * n_col_blocks + pl.program_id(1)
        for idx, l_ref in enumerate(l_refs):
            @pl.when(step == idx)
            def _(l_ref=l_ref):
                g = l_ref[0].astype(F32)
                for s in range(1, n_slots):
                    g = g + l_ref[s].astype(F32)
                g_ref[...] = g
                d_ref[...], nm_ref[...], nv_ref[...] = _adamw_math(w_ref[...], g, m_ref[...], v_ref[...])

    spec = pl.BlockSpec((r, c), lambda a, b: (a, b))
    return _launch(
        body, name=f"adamw_{name}", grid=grid,
        in_specs=[spec, spec, spec] + [_full_spec((n_slots, r, c))] * len(landings),
        out_specs=[spec] * 4, out_shape=[jax.ShapeDtypeStruct(w.shape, F32)] * 4,
        args=(w, m, v, *landings), vmem=VMEM_BIG, job=job)


_SMALL = (("pre_mix_g", SV_PRE_MIX, 2), ("post_mix_g", SV_POST_MIX, 2), ("pre_ffn_g", SV_PRE_FFN, 2),
          ("post_ffn_g", SV_POST_FFN, 2), ("ple_g", SV_PLE, 2), ("ple_post_g", SV_PLE_POST, 2), ("kv_g", SV_KV, 1),
          ("pool_scale", SV_POOL_SCALE, 1), ("sinks", SV_SINKS, 1))


def _small_adamw(slabs, params):
    flat = [a for name, _, _ in _SMALL for a in params[name]]
    n_in = 1 + len(flat)

    def body(*refs):
        slabs_ref, wmv = refs[0], refs[1:n_in]
        loss_ref, outs, total = refs[n_in], refs[n_in + 1:-1], refs[-1]
        me = _dev_index(*_my_place())
        g = slabs_ref[0]
        for s in range(1, N_DEV):
            g = g + slabs_ref[s]
        total[...] = g
        loss_ref[...] = total[SV_LOSS:SV_LOSS + 1, 0:1]
        for idx, (name, row, n_rows) in enumerate(_SMALL):
            w_ref, m_ref, v_ref = wmv[3 * idx:3 * idx + 3]
            g_ref, d_ref, nm_ref, nv_ref = outs[4 * idx:4 * idx + 4]
            if name == "pool_scale":
                g = total[row:row + 1, pl.ds(pl.multiple_of(me * 128, 128), 128)]
            else:
                g = total[row:row + n_rows, 0:w_ref.shape[1]]
            g_ref[...] = g
            d_ref[...], nm_ref[...], nv_ref[...] = _adamw_math(w_ref[...], g, m_ref[...], v_ref[...])

    out_shape = [jax.ShapeDtypeStruct((1, 1), F32)]
    for name, _, _ in _SMALL:
        out_shape += [jax.ShapeDtypeStruct(params[name][0].shape, F32)] * 4
    res, _ = _launch(
        body, name="small_adamw", grid=(1,),
        in_specs=[_full_spec(a.shape) for a in (slabs, *flat)], out_specs=[_full_spec(s.shape) for s in out_shape],
        out_shape=out_shape, scratch_shapes=[pltpu.VMEM((SV_ROWS, D_MODEL), F32)], args=(slabs, *flat))
    return res[0], {name: res[1 + 4 * idx:5 + 4 * idx] for idx, (name, _, _) in enumerate(_SMALL)}


def _local_step(x, p, tgt, gains, sinks, shards, weights):
    row = lambda first_row, layer: _Gain(gains, first_row + layer)
    gather = lambda *names: _AllGather(names, shards)
    g_pre_mix, g_post_mix, g_pre_ffn, g_post_ffn = SV_PRE_MIX, SV_POST_MIX, SV_PRE_FFN, SV_POST_FFN
    g_ple, g_ple_post, g_kv = SV_PLE, SV_PLE_POST, _Gain(gains, SV_KV)

    (dpool,), (wp, scale, wgu0, wd0) = _fwd_pool(x, row(g_pre_mix, 0), job=gather("pool", "scale", "gu0", "wd0"))
    wgu0, wd0 = [wgu0], [wd0]
    (x1_0, h2_0, yraw), _ = _fwd_pool_mixer(x, dpool, wp, scale, row(g_post_mix, 0), row(g_pre_ffn, 0))
    (gs0, us0, f0, x2_0, h3_0), (wgate0, wproj0, wkv, wq, wo, wd1_a) = _fwd_ffn(
        0, h2_0, x1_0, wgu0, wd0, row(g_post_ffn, 0), row(g_ple, 0),
        job=gather("gate0", "proj0", "kv", "q", "o", "wdh1_0"))
    (x3_0, z0, pe0, hk, h1, q, kv), (wgu1_a,) = _fwd_ple_qkv(
        x2_0, h3_0, p[0], wgate0, wproj0, row(g_ple_post, 0), g_kv, row(g_pre_mix, 1), wkv, wq,
        job=gather("guh1_0"))
    front = ((ATT_BLOCK, 0), (0, 0))
    kpad = jnp.pad(kv[:, :KV_DIM], front)
    vpad = jnp.pad(kv[:, KV_DIM:], front)
    (attn,), (wgu1_b,) = _fwd_attention(q, kpad, vpad, sinks, job=gather("guh1_1"))
    (y1, x1_1, h2_1), (wd1_b,) = _fwd_attn_out(attn, x3_0, wo, row(g_post_mix, 1), row(g_pre_ffn, 1),
                                               job=gather("wdh1_1"))
    wgu1, wd1 = [wgu1_a, wgu1_b], [wd1_a, wd1_b]
    (gs1, us1, f1, x2_1, h3_1), (wgate1, wproj1) = _fwd_ffn(
        1, h2_1, x1_1, wgu1, wd1, row(g_post_ffn, 1), row(g_ple, 1), job=gather("gate1", "proj1"))

    produced, swapped, landed = {}, {}, {}

    def kind_of(name):
        return name.rstrip("0123_")

    def hosted(call, *args, swap=(), spread=(), extra=None):
        jobs = []
        if swap:
            jobs.append(_SiblingSwap([(kind_of(n), produced[n]) for n in swap]))
        if spread:
            jobs.append(_ChipScatter([(kind_of(n), produced[n], swapped[n]) for n in spread]))
        if extra is not None:
            jobs.append(extra)
        jobs = _Jobs(*jobs)
        outs, job_outs = call(*args, job=jobs)
        parts = jobs.split_outputs(job_outs)
        if swap:
            swapped.update(zip(swap, parts.pop(0)))
        if spread:
            landed.update(zip(spread, parts.pop(0)))
        return outs if extra is None else (outs, parts.pop(0))

    ffn_q = lambda layer, qtr: (f"gu{layer}_{qtr}", f"wd{layer}_{qtr}")

    dx2_1, df1, produced["gate1"], produced["proj1"], dg_ple_post1, dg_ple1, dg_post_ffn1, loss = hosted(
        _ple_loss_bwd, 1, x2_1, h3_1, p[1], f1, tgt, wgate1, wproj1, row(g_ple_post, 1), row(g_ple, 1),
        row(g_post_ffn, 1))
    dh2_1, dg1, du1, a1 = hosted(_bwd_ffn_act, 1, df1, gs1, us1, wgu1, wd1, swap=("gate1", "proj1"))
    dgu1, dwd1 = hosted(_bwd_ffn_dw, 1, 0, 1, h2_1, df1, dg1, du1, a1, spread=("gate1", "proj1"))
    produced.update(guA1=dgu1, guB1=dgu1, wdA1=dwd1, wdB1=dwd1)
    dx1_1, dattn, produced["o"], dg_pre_ffn1, dg_post_mix1 = hosted(
        _bwd_attn_out, dx2_1, dh2_1, x1_1, y1, attn, wo, row(g_pre_ffn, 1), row(g_post_mix, 1),
        swap=("guA1", "wdA1", "guB1", "wdB1"))
    dq, dkpad, dvpad, dsinks = hosted(_bwd_attention, q, dattn, kpad, vpad, sinks, spread=("guA1", "wdA1"))
    dkv = jnp.concatenate([dkpad[ATT_BLOCK:], dvpad[ATT_BLOCK:]], axis=1).astype(BF16)
    dx3_0, produced["q"], produced["kv"], dg_pre_mix1, dg_kv = hosted(
        _bwd_qkv, dx1_1, dq, dkv, x3_0, h1, hk, wq, wkv, row(g_pre_mix, 1), g_kv, swap=("o",), spread=("wdB1",))
    dx2_0, df0, produced["gate0"], produced["proj0"], dg_ple_post0, dg_ple0, dg_post_ffn0 = hosted(
        _bwd_ple, 0, dx3_0, x2_0, z0, pe0, h3_0, p[0], f0, wgate0, row(g_ple_post, 0), row(g_ple, 0),
        row(g_post_ffn, 0), swap=("q", "kv"), spread=("guB1",))
    for half, letter in enumerate("AB"):
        landed[f"gu1_{half}"], landed[f"wd1_{half}"] = landed[f"gu{letter}1"], landed[f"wd{letter}1"]
    dh2_0, dg0, du0, a0 = hosted(_bwd_ffn_act, 0, df0, gs0, us0, wgu0, wd0,
                                 swap=("gate0", "proj0"), spread=("o", "q", "kv"))
    part_hosts = [dict(spread=("gate0", "proj0")), dict(swap=ffn_q(0, 0))]
    for part in range(FF_PARTS):
        produced[f"gu0_{part}"], produced[f"wd0_{part}"] = hosted(
            _bwd_ffn_dw, 0, part, FF_PARTS, h2_0, df0, dg0, du0, a0, **part_hosts[part])
    grad_x, produced["pool"], dscale, dg_pre_ffn0, dg_post_mix0, dg_pre_mix0 = hosted(
        _bwd_pool_mixer, dx2_0, dh2_0, x1_0, x, yraw, dpool, wp, scale, row(g_pre_ffn, 0), row(g_post_mix, 0),
        row(g_pre_mix, 0), swap=ffn_q(0, 1), spread=ffn_q(0, 0))

    def update(name, n_col_blocks=1, pieces=None, swap=(), spread=()):
        w, m, v = weights[name]
        rows = w.size // w.shape[-1]
        flat = [landed[n].reshape(landed[n].shape[0], -1, landed[n].shape[-1])
                for n in (pieces or [kind_short[name]])]
        outs = hosted(_adamw, name, w.reshape(rows, -1), m.reshape(rows, -1), v.reshape(rows, -1), flat,
                      n_col_blocks, swap=swap, spread=spread)
        return [o.reshape(w.shape) for o in outs]

    kind_short = {"w_q": "q", "w_kv": "kv", "w_o": "o", "pool_w": "pool"}
    upd = {}
    lanes = lambda a: jnp.pad(a, ((0, 0), (0, D_MODEL - a.shape[1])))
    small = jnp.concatenate([
        dg_pre_mix0, dg_pre_mix1, dg_post_mix0, dg_post_mix1, dg_pre_ffn0, dg_pre_ffn1, dg_post_ffn0, dg_post_ffn1,
        dg_ple0, dg_ple1, dg_ple_post0, dg_ple_post1, dg_kv, dscale, lanes(dsinks[:, :N_HEADS]), lanes(loss)], axis=0)

    everyone = _ToEveryone(scattered=[("pool", produced["pool"])], gathered=[small])
    _, (landed["pool"], slabs) = hosted(_jobs_only, "scatter_tail", spread=ffn_q(0, 1), extra=everyone)
    upd["w_ple_gate"] = update("w_ple_gate", pieces=("gate0", "gate1"))
    upd["w_ple_proj"] = update("w_ple_proj", pieces=("proj0", "proj1"))
    for name in ("w_q", "w_kv", "w_o", "pool_w"):
        upd[name] = update(name)
    upd["w_gu"] = update("w_gu", FF_PARTS,
                         pieces=[f"gu{layer}_{qtr}" for layer in range(2) for qtr in range(FF_PARTS)])
    upd["w_gu"] = [jnp.swapaxes(a, 1, 2) for a in upd["w_gu"]]
    upd["w_down"] = update("w_down", FF_PARTS,
                           pieces=[f"wd{layer}_{qtr}" for layer in range(2) for qtr in range(FF_PARTS)])
    return grad_x, upd, slabs


def kernel(x, p, pre_mix_g, post_mix_g, pre_ffn_g, post_ffn_g, pool_w, pool_scale, kv_g, w_kv, w_q, sinks, w_o, w_gu, w_down, ple_g, w_ple_gate, w_ple_proj, ple_post_g, loss_target, m_pre_mix_g, m_post_mix_g, m_pre_ffn_g, m_post_ffn_g, m_pool_w, m_pool_scale, m_kv_g, m_w_kv, m_w_q, m_sinks, m_w_o, m_w_gu, m_w_down, m_ple_g, m_w_ple_gate, m_w_ple_proj, m_ple_post_g, v_pre_mix_g, v_post_mix_g, v_pre_ffn_g, v_post_ffn_g, v_pool_w, v_pool_scale, v_kv_g, v_w_kv, v_w_q, v_sinks, v_w_o, v_w_gu, v_w_down, v_ple_g, v_w_ple_gate, v_w_ple_proj, v_ple_post_g):
    shards = {"pool": pool_w[0].astype(BF16), "scale": pool_scale, "kv": w_kv.astype(BF16),
              "q": w_q[0].astype(BF16), "o": w_o[0].astype(BF16)}
    for layer in range(2):
        shards[f"gu{layer}"] = w_gu[layer].T.astype(BF16)
        shards[f"wd{layer}"] = w_down[layer].astype(BF16)
        for half in range(2):
            cols = (half * D_MODEL // 2, (half + 1) * D_MODEL // 2)
            shards[f"guh{layer}_{half}"] = (shards[f"gu{layer}"], cols)
            shards[f"wdh{layer}_{half}"] = (shards[f"wd{layer}"], cols)
        shards[f"gate{layer}"] = w_ple_gate[layer].astype(BF16)
        shards[f"proj{layer}"] = w_ple_proj[layer].astype(BF16)
    gains = jnp.concatenate([pre_mix_g, post_mix_g, pre_ffn_g, post_ffn_g, ple_g, ple_post_g, kv_g[None, :]],
                            axis=0).reshape(-1, 1, D_MODEL)
    weights = {"pool_w": (pool_w, m_pool_w, v_pool_w), "w_kv": (w_kv, m_w_kv, v_w_kv), "w_q": (w_q, m_w_q, v_w_q),
               "w_o": (w_o, m_w_o, v_w_o), "w_down": (w_down, m_w_down, v_w_down),
               "w_gu": tuple(jnp.swapaxes(a, 1, 2) for a in (w_gu, m_w_gu, v_w_gu)),
               "w_ple_gate": (w_ple_gate, m_w_ple_gate, v_w_ple_gate),
               "w_ple_proj": (w_ple_proj, m_w_ple_proj, v_w_ple_proj)}
    grad_x, upd, slabs = _local_step(x[0], p[:, 0], loss_target[0], gains, sinks, shards, weights)

    small_params = {
        "pre_mix_g": (pre_mix_g, m_pre_mix_g, v_pre_mix_g), "post_mix_g": (post_mix_g, m_post_mix_g, v_post_mix_g),
        "pre_ffn_g": (pre_ffn_g, m_pre_ffn_g, v_pre_ffn_g), "post_ffn_g": (post_ffn_g, m_post_ffn_g, v_post_ffn_g),
        "ple_g": (ple_g, m_ple_g, v_ple_g), "ple_post_g": (ple_post_g, m_ple_post_g, v_ple_post_g),
        "kv_g": (kv_g[None, :], m_kv_g[None, :], v_kv_g[None, :]),
        "pool_scale": (pool_scale, m_pool_scale, v_pool_scale), "sinks": (sinks, m_sinks, v_sinks)}
    loss, small_upd = _small_adamw(slabs, small_params)
    small_upd["kv_g"] = [a[0] for a in small_upd["kv_g"]]
    upd.update(small_upd)

    names = ["pre_mix_g", "post_mix_g", "pre_ffn_g", "post_ffn_g", "pool_w", "pool_scale", "kv_g", "w_kv", "w_q",
             "sinks", "w_o", "w_gu", "w_down", "ple_g", "w_ple_gate", "w_ple_proj", "ple_post_g"]
    outs = [loss[0, 0], grad_x[None]]
    for kind in range(4):
        outs += [upd[n][kind] for n in names]
    return tuple(outs)
```

```python
import functools
import types

import jax
import jax.numpy as jnp
from jax import lax
from jax.experimental import pallas as pl
from jax.experimental.pallas import tpu as pltpu

F32 = jnp.float32
BF16 = jnp.bfloat16

N_DEV = 8
D_MODEL = 1024
N_POOL_GROUPS = 4
POOL_GROUP = 256
POOL_HALO = 16
HEAD_DIM = 64
N_HEADS = 16
N_KV_HEADS = 4
GQA_GROUP = 4
KV_DIM = N_KV_HEADS * HEAD_DIM
ATT_BLOCK = 128
D_FF = 2816
FF_CHUNKS = 4
FF_BLOCK = D_FF // FF_CHUNKS
WD_ROWS = D_FF // N_DEV
FF_PARTS = 2
FF_PART = D_MODEL // FF_PARTS
N_CHIPS = 4
PLE_DIM = 256
EPS = 1e-6
NEG_INF = -1e30
ATT_SCALE = HEAD_DIM ** -0.5

ADAM_LR = 0.001
ADAM_B1 = 0.9
ADAM_B2 = 0.999
ADAM_EPS = 1e-08
ADAM_WD = 0.01
ADAM_STEP = 10

ROW_TILE = 512
FFN_ROW_TILE = 512
FFN_WEIGHT_COLS = 512
FFN_SUB_TILES = 1
VMEM_BIG = 60 * 1024 * 1024
VMEM_MID = 56 * 1024 * 1024
HBM_PIN_ELEMS = 1024

SV_ROWS = 16
SV_PRE_MIX, SV_POST_MIX, SV_PRE_FFN, SV_POST_FFN, SV_PLE, SV_PLE_POST = 0, 2, 4, 6, 8, 10
SV_KV, SV_POOL_SCALE, SV_SINKS, SV_LOSS = 12, 13, 14, 15

MESH = pl.DeviceIdType.MESH
ANY = pl.BlockSpec(memory_space=pl.ANY)


def _dot(a, b):
    return jnp.dot(a, b, preferred_element_type=F32)


def _dot_nt(a, b):
    return lax.dot_general(a, b, (((1,), (1,)), ((), ())), preferred_element_type=F32)


def _dot_tn(a, b):
    return lax.dot_general(a, b, (((0,), (0,)), ((), ())), preferred_element_type=F32)


def _rstd(x):
    return lax.rsqrt(jnp.mean(x * x, axis=-1, keepdims=True) + EPS)


def _rms(x, g):
    return x * _rstd(x) * g


def _rms_bwd(x, g, dy):
    r = _rstd(x)
    n = x * r
    dn = dy * g
    dx = r * (dn - n * jnp.mean(dn * n, axis=-1, keepdims=True))
    dg = jnp.sum(dy * n, axis=0, keepdims=True)
    return dx, dg


def _add_all(terms):
    return functools.reduce(jnp.add, terms)


def _sigmoid(x):
    return 1.0 / (1.0 + jnp.exp(-x))


def _acc(ref, val, first):
    @pl.when(first)
    def _():
        ref[...] = val

    @pl.when(jnp.logical_not(first))
    def _():
        ref[...] += val


def _pool_counts(row0, rows):
    t = row0 + lax.broadcasted_iota(jnp.int32, (rows, D_MODEL), 0) + 1
    grp = lax.broadcasted_iota(jnp.int32, (rows, D_MODEL), 1) // POOL_GROUP
    win = jnp.left_shift(2, grp)
    return jnp.minimum(t, win).astype(F32)


def _window_sums(ext, shift_of):
    outs = []
    s = ext
    for gi in range(N_POOL_GROUPS):
        s = s + pltpu.roll(s, shift_of(1 << gi), axis=0)
        outs.append(s[:, :POOL_GROUP])
        s = s[:, POOL_GROUP:]
    return jnp.concatenate(outs, axis=1)


def _cparams(n_axes, vmem, collective_id=None):
    return pltpu.CompilerParams(dimension_semantics=("arbitrary",) * n_axes, vmem_limit_bytes=vmem,
                                collective_id=collective_id)


_EVERYONE = ("sibling", "x", "y", "far", "x sibling", "y sibling", "far sibling")
_PEER_SETS = (("sibling", "x", "y"), ("sibling",), ("x", "y"), _EVERYONE)


def _meet(peers):
    x, y, c = lax.axis_index("x"), lax.axis_index("y"), lax.axis_index("c")
    device = {"sibling": (x, y, 1 - c), "x": (1 - x, y, c), "y": (x, 1 - y, c), "far": (1 - x, 1 - y, c),
              "x sibling": (1 - x, y, 1 - c), "y sibling": (x, 1 - y, 1 - c), "far sibling": (1 - x, 1 - y, 1 - c)}
    barrier = pltpu.get_barrier_semaphore()
    for peer in peers:
        pl.semaphore_signal(barrier, inc=1, device_id=device[peer], device_id_type=pl.DeviceIdType.MESH)
    pl.semaphore_wait(barrier, len(peers))


def _row_spec(cols, tm=ROW_TILE):
    return pl.BlockSpec((tm, cols), lambda i: (i, 0))


def _full_spec(shape):
    zeros = (0,) * len(shape)
    return pl.BlockSpec(shape, lambda *_: zeros)


def _vec_spec():
    return _full_spec((1, D_MODEL))


def _column_views(parts):
    return [(a, b) for a in parts for b in range(a.shape[-1] // FFN_WEIGHT_COLS)]


def _column_ranges(views):
    return [(n * FFN_WEIGHT_COLS, (n + 1) * FFN_WEIGHT_COLS) for n in range(len(views))]


class _Gain:
    def __init__(self, stacked, layer):
        self.stacked, self.layer = stacked, layer

    def spec(self):
        layer = self.layer
        return pl.BlockSpec((None, 1, D_MODEL), lambda *_: (layer, 0, 0))


class _LayerRows:
    def __init__(self, stacked, layer):
        self.stacked, self.layer = stacked, layer

    def spec(self):
        layer = self.layer
        return pl.BlockSpec((None, ROW_TILE, self.stacked.shape[-1]), lambda i: (layer, i, 0))


def _in_hbm(a):
    return pltpu.with_memory_space_constraint(a, pltpu.HBM) if a.size >= HBM_PIN_ELEMS else a


def _out_in_hbm(s):
    return pltpu.HBM(s.shape, s.dtype) if s.size >= HBM_PIN_ELEMS else s


def _launch(body, *, name, grid, in_specs, out_specs, out_shape, args, scratch_shapes=(), vmem=VMEM_MID, job=None):
    picked = (_Gain, _LayerRows)
    in_specs = [a.spec() if isinstance(a, picked) else s for s, a in zip(in_specs, args)]
    args = [_in_hbm(a.stacked if isinstance(a, picked) else a) for a in args]
    n_in, n_out, n_scr = len(args), len(out_shape), len(scratch_shapes)
    if job is not None and not job.args:
        job = None
    j_args, j_out, j_scr = ([], [], []) if job is None else ([_in_hbm(a) for a in job.args], job.out_shape, job.scratch)

    def run(*refs):
        groups, at = [], 0
        for n in (n_in, len(j_args), n_out, len(j_out), n_scr, len(j_scr)):
            groups.append(refs[at:at + n])
            at += n
        ins, j_ins, outs, j_outs, scr, j_sems = groups

        def begin():
            _meet(job.peers)
            job.start(j_ins, j_outs, j_sems)

        if job is None:
            body(*ins, *outs, *scr)
        elif not grid:
            begin()
            job.mid(j_ins, j_outs, j_sems)
            job.late(j_ins, j_outs, j_sems)
            body(*ins, *outs, *scr)
            job.finish(j_ins, j_outs, j_sems)
        else:
            ids = [pl.program_id(a) for a in range(len(grid))]
            at_start = lambda step: functools.reduce(jnp.logical_and, [ids[0] == step] + [i == 0 for i in ids[1:]])
            last = functools.reduce(jnp.logical_and, [i == g - 1 for i, g in zip(ids, grid)])
            pl.when(at_start(0))(begin)
            pl.when(at_start(grid[0] // 2))(lambda: job.mid(j_ins, j_outs, j_sems))
            pl.when(at_start(3 * grid[0] // 4))(lambda: job.late(j_ins, j_outs, j_sems))
            body(*ins, *outs, *scr)
            pl.when(last)(lambda: job.finish(j_ins, j_outs, j_sems))

    res = pl.pallas_call(
        run, name=name, grid=grid,
        in_specs=list(in_specs) + [ANY] * len(j_args), out_specs=list(out_specs) + [ANY] * len(j_out),
        out_shape=[_out_in_hbm(s) for s in list(out_shape) + list(j_out)],
        scratch_shapes=list(scratch_shapes) + list(j_scr),
        compiler_params=_cparams(len(grid), vmem, None if job is None else _PEER_SETS.index(job.peers)),
    )(*args, *j_args)
    return res[:n_out], res[n_out:]


def _fwd_pool(x, g_pre, job=None):
    T = x.shape[0]
    tm = ROW_TILE
    nt = T // tm

    def body(x_ref, gpre_ref, d_ref, carry):
        i = pl.program_id(0)

        @pl.when(i == 0)
        def _():
            carry[...] = jnp.zeros_like(carry)

        h = _rms(x_ref[...], gpre_ref[...])
        ext = jnp.concatenate([carry[...], h], axis=0)
        carry[...] = h[tm - POOL_HALO:, :]
        sums = _window_sums(ext, lambda k: k)[POOL_HALO:, :]
        d_ref[...] = (sums / _pool_counts(i * tm, tm) - h).astype(BF16)

    return _launch(
        body, name="fwd_pool", grid=(nt,), in_specs=[_row_spec(D_MODEL), _vec_spec()], out_specs=[_row_spec(D_MODEL)],
        out_shape=[jax.ShapeDtypeStruct((T, D_MODEL), BF16)], scratch_shapes=[pltpu.VMEM((POOL_HALO, D_MODEL), F32)],
        args=(x, g_pre), job=job)


def _fwd_pool_mixer(x, d, wp, scale, g_post, g_ffn, job=None):
    T = x.shape[0]
    nt = T // ROW_TILE

    def body(x_ref, d_ref, wp_ref, sc_ref, gpost_ref, gffn_ref, x1_ref, h2_ref, yraw_ref):
        db = d_ref[...]
        yraw = jnp.concatenate(
            [_dot(db[:, g * POOL_GROUP:(g + 1) * POOL_GROUP], wp_ref[g]) for g in range(N_POOL_GROUPS)], axis=1)
        yraw_ref[...] = yraw.astype(BF16)
        x1 = x_ref[...] + _rms(yraw * sc_ref[...], gpost_ref[...])
        x1_ref[...] = x1
        h2_ref[...] = _rms(x1, gffn_ref[...]).astype(BF16)

    return _launch(
        body, name="fwd_pool_mixer", grid=(nt,),
        in_specs=[_row_spec(D_MODEL), _row_spec(D_MODEL), _full_spec((N_POOL_GROUPS, POOL_GROUP, POOL_GROUP)),
                  _vec_spec(), _vec_spec(), _vec_spec()],
        out_specs=[_row_spec(D_MODEL)] * 3,
        out_shape=[jax.ShapeDtypeStruct((T, D_MODEL), F32)] + [jax.ShapeDtypeStruct((T, D_MODEL), BF16)] * 2,
        args=(x, d, wp, scale, g_post, g_ffn), job=job)


def _fwd_ffn(layer, h2, x1, wgu, wd, g_post, g_ple, job=None):
    T = h2.shape[0]
    tm = min(FFN_ROW_TILE, T)
    nt = T // tm
    sub = tm // FFN_SUB_TILES
    last = FF_CHUNKS - 1
    wgu, wd = _column_views(wgu), _column_views(wd)
    n_gu, n_wd = len(wgu), len(wd)
    gu_cols = _column_ranges(wgu)

    def body(h2_ref, x1_ref, *refs):
        wgu_refs, wd_refs = refs[:n_gu], refs[n_gu:n_gu + n_wd]
        gpost_ref, gple_ref, gs_ref, us_ref, f_ref, x2_ref, h3_ref, acc = refs[n_gu + n_wd:]
        k = pl.program_id(0)
        i = pl.program_id(1)
        rows = pl.ds(pl.multiple_of(i * tm, tm), tm)
        parts = []
        for s in range(FFN_SUB_TILES):
            r = pl.ds(s * sub, sub)
            g = _add_all([_dot_nt(h2_ref[r, c0:c1], w[0]) for (c0, c1), w in zip(gu_cols, wgu_refs)])
            u = _add_all([_dot_nt(h2_ref[r, c0:c1], w[1]) for (c0, c1), w in zip(gu_cols, wgu_refs)])
            gs_ref[r, :] = g.astype(BF16)
            us_ref[r, :] = u.astype(BF16)
            a = (g * _sigmoid(g) * u).astype(BF16)
            parts.append(jnp.concatenate([_dot(a, w[...]) for w in wd_refs], axis=1))
        part = jnp.concatenate(parts, axis=0)

        @pl.when(k == 0)
        def _():
            acc[rows, :] = part

        @pl.when(jnp.logical_and(k > 0, k < last))
        def _():
            acc[rows, :] += part

        @pl.when(k == last)
        def _():
            f = acc[rows, :] + part
            f_ref[...] = f.astype(BF16)
            x2 = x1_ref[...] + _rms(f, gpost_ref[...])
            x2_ref[...] = x2
            h3_ref[...] = _rms(x2, gple_ref[...]).astype(BF16)

    def late(k, i):
        return (jnp.where(k == last, i, 0), 0)

    return _launch(
        body, name=f"fwd_ffn{layer}", grid=(FF_CHUNKS, nt),
        in_specs=[pl.BlockSpec((tm, D_MODEL), lambda k, i: (i, 0)), pl.BlockSpec((tm, D_MODEL), late)]
                 + [pl.BlockSpec((None, 2, FF_BLOCK, FFN_WEIGHT_COLS), lambda k, i, b=b: (k, 0, 0, b)) for _, b in wgu]
                 + [pl.BlockSpec((FF_BLOCK, FFN_WEIGHT_COLS), lambda k, i, b=b: (k, b)) for _, b in wd]
                 + [pl.BlockSpec((1, D_MODEL), lambda k, i: (0, 0))] * 2,
        out_specs=[pl.BlockSpec((None, tm, FF_BLOCK), lambda k, i: (k, i, 0)),
                   pl.BlockSpec((None, tm, FF_BLOCK), lambda k, i: (k, i, 0)),
                   pl.BlockSpec((tm, D_MODEL), late),
                   pl.BlockSpec((tm, D_MODEL), late),
                   pl.BlockSpec((tm, D_MODEL), late)],
        out_shape=[jax.ShapeDtypeStruct((FF_CHUNKS, T, FF_BLOCK), BF16),
                   jax.ShapeDtypeStruct((FF_CHUNKS, T, FF_BLOCK), BF16),
                   jax.ShapeDtypeStruct((T, D_MODEL), BF16),
                   jax.ShapeDtypeStruct((T, D_MODEL), F32),
                   jax.ShapeDtypeStruct((T, D_MODEL), BF16)],
        scratch_shapes=[pltpu.VMEM((T, D_MODEL), F32)],
        args=(h2, x1, *[w for w, _ in wgu], *[w for w, _ in wd], g_post, g_ple), vmem=VMEM_BIG, job=job)


def _fwd_ple_qkv(x2, h3, p, wgate, wproj, g_post, g_kv, g_mix, wkv, wq, job=None):
    T = x2.shape[0]
    nt = T // ROW_TILE

    def body(x2_ref, h3_ref, p_ref, wg_ref, wp_ref, gpost_ref, gkv_ref, gmix_ref, wkv_ref, wq_ref,
             x3_ref, z_ref, pe_ref, hk_ref, h1_ref, q_ref, kv_ref):
        z = _dot(h3_ref[...], wg_ref[...])
        pe = _dot(p_ref[...].astype(BF16), wp_ref[...])
        z_ref[...] = z.astype(BF16)
        pe_ref[...] = pe.astype(BF16)
        x3 = x2_ref[...] + _rms(pe * _sigmoid(z), gpost_ref[...])
        x3_ref[...] = x3
        r = _rstd(x3)
        hk = (x3 * r * gkv_ref[...]).astype(BF16)
        h1 = (x3 * r * gmix_ref[...]).astype(BF16)
        hk_ref[...] = hk
        h1_ref[...] = h1
        kv_ref[...] = _dot(hk, wkv_ref[...]).astype(BF16)
        q_ref[...] = _dot(h1, wq_ref[...]).astype(BF16)

    wide = jax.ShapeDtypeStruct((T, D_MODEL), BF16)
    return _launch(
        body, name="fwd_ple_qkv", grid=(nt,),
        in_specs=[_row_spec(D_MODEL), _row_spec(D_MODEL), _row_spec(PLE_DIM), _full_spec((D_MODEL, D_MODEL)),
                  _full_spec((PLE_DIM, D_MODEL)), _vec_spec(), _vec_spec(), _vec_spec(),
                  _full_spec((D_MODEL, 2 * KV_DIM)), _full_spec((D_MODEL, D_MODEL))],
        out_specs=[_row_spec(D_MODEL)] * 6 + [_row_spec(2 * KV_DIM)],
        out_shape=[jax.ShapeDtypeStruct((T, D_MODEL), F32)] + [wide] * 5 + [jax.ShapeDtypeStruct((T, 2 * KV_DIM), BF16)],
        args=(x2, h3, p, wgate, wproj, g_post, g_kv, g_mix, wkv, wq), job=job)


def _alibi_slope(h):
    return 2.0 ** (-8.0 * (h + 1) / N_HEADS)


ATT_SUB = 32
ATT_GROUP_ROWS = GQA_GROUP * ATT_BLOCK


def _att_mask(n, rel_ref, off_ref):
    qi = lax.broadcasted_iota(jnp.int32, (ATT_BLOCK, 2 * ATT_BLOCK), 0)
    si = lax.broadcasted_iota(jnp.int32, (ATT_BLOCK, 2 * ATT_BLOCK), 1)
    rel = ATT_BLOCK + qi - si
    valid = (rel >= 0) & (rel < ATT_BLOCK) & ((si >= ATT_BLOCK) | (n > 0))
    rel_ref[...] = rel.astype(F32)
    off_ref[...] = jnp.where(valid, 0.0, NEG_INF)


def _att_probs(raw, relf, off, slope, sink):
    s = raw * ATT_SCALE - slope * relf + off
    m = jnp.maximum(jnp.max(s, axis=-1, keepdims=True), sink)
    e = jnp.exp(s - m)
    es = jnp.exp(sink - m)
    inv = 1.0 / (jnp.sum(e, axis=-1, keepdims=True) + es)
    return e * inv, es * inv


def _stack_heads(ref, kh):
    first = kh * GQA_GROUP
    return jnp.concatenate([ref[:, (first + g) * HEAD_DIM:(first + g + 1) * HEAD_DIM] for g in range(GQA_GROUP)], axis=0)


def _unstack_heads(stacked):
    return [stacked[g * ATT_BLOCK:(g + 1) * ATT_BLOCK, :] for g in range(GQA_GROUP)]


def _fwd_attention(q, kpad, vpad, sinks, job=None):
    T = q.shape[0]
    nb = T // ATT_BLOCK

    def body(q_ref, k_ref, v_ref, sink_ref, o_ref, s_scr, p_scr, rel_scr, off_scr):
        n = pl.program_id(0)
        start = pl.multiple_of(n * ATT_BLOCK, ATT_BLOCK)
        kw = k_ref[pl.ds(start, 2 * ATT_BLOCK), :]
        vw = v_ref[pl.ds(start, 2 * ATT_BLOCK), :]
        _att_mask(n, rel_scr, off_scr)
        outs = []
        for kh in range(N_KV_HEADS):
            kk = kw[:, kh * HEAD_DIM:(kh + 1) * HEAD_DIM]
            vv = vw[:, kh * HEAD_DIM:(kh + 1) * HEAD_DIM]
            s_scr[...] = _dot_nt(_stack_heads(q_ref, kh), kk)
            for g in range(GQA_GROUP):
                h = kh * GQA_GROUP + g
                for row0 in range(0, ATT_BLOCK, ATT_SUB):
                    rows, sub = pl.ds(g * ATT_BLOCK + row0, ATT_SUB), pl.ds(row0, ATT_SUB)
                    pr, _ = _att_probs(s_scr[rows, :], rel_scr[sub, :], off_scr[sub, :], _alibi_slope(h),
                                       sink_ref[0, h])
                    p_scr[rows, :] = pr.astype(BF16)
            outs += _unstack_heads(_dot(p_scr[...], vv))
        o_ref[...] = jnp.concatenate(outs, axis=1).astype(BF16)

    return _launch(
        body, name="fwd_attention", grid=(nb,),
        in_specs=[_row_spec(D_MODEL, ATT_BLOCK), _full_spec((T + ATT_BLOCK, KV_DIM)), _full_spec((T + ATT_BLOCK, KV_DIM)),
                  pl.BlockSpec(memory_space=pltpu.SMEM)],
        out_specs=[_row_spec(D_MODEL, ATT_BLOCK)],
        out_shape=[jax.ShapeDtypeStruct((T, D_MODEL), BF16)],
        scratch_shapes=[pltpu.VMEM((ATT_GROUP_ROWS, 2 * ATT_BLOCK), F32), pltpu.VMEM((ATT_GROUP_ROWS, 2 * ATT_BLOCK), BF16)]
                       + [pltpu.VMEM((ATT_BLOCK, 2 * ATT_BLOCK), F32)] * 2,
        args=(q, kpad, vpad, sinks), job=job)


def _fwd_attn_out(attn, x, wo, g_post, g_ffn, job=None):
    T = x.shape[0]
    nt = T // ROW_TILE

    def body(a_ref, x_ref, wo_ref, gpost_ref, gffn_ref, y_ref, x1_ref, h2_ref):
        y = _dot(a_ref[...], wo_ref[...])
        y_ref[...] = y.astype(BF16)
        x1 = x_ref[...] + _rms(y, gpost_ref[...])
        x1_ref[...] = x1
        h2_ref[...] = _rms(x1, gffn_ref[...]).astype(BF16)

    return _launch(
        body, name="fwd_attn_out", grid=(nt,),
        in_specs=[_row_spec(D_MODEL), _row_spec(D_MODEL), _full_spec((D_MODEL, D_MODEL)), _vec_spec(), _vec_spec()],
        out_specs=[_row_spec(D_MODEL)] * 3,
        out_shape=[jax.ShapeDtypeStruct((T, D_MODEL), BF16), jax.ShapeDtypeStruct((T, D_MODEL), F32),
                   jax.ShapeDtypeStruct((T, D_MODEL), BF16)],
        args=(attn, x, wo, g_post, g_ffn), job=job)


def _bwd_ple(layer, dx3, x2, z, pe, h3, p, f, wgate, g_ple_post, g_ple, g_post_ffn, job=None):
    T = x2.shape[0]
    tm = ROW_TILE
    nt = T // tm

    def body(dx3_ref, x2_ref, z_ref, pe_ref, h3_ref, p_ref, f_ref, wg_ref, gpp_ref, gp_ref, gpf_ref,
             dx2_ref, df_ref, dwg_ref, dwp_ref, dgpp_ref, dgp_ref, dgpf_ref, acc_g, acc_p):
        i = pl.program_id(0)
        first = i == 0
        dx3v = dx3_ref[...]
        gate = _sigmoid(z_ref[...].astype(F32))
        pev = pe_ref[...].astype(F32)
        de, dgpp = _rms_bwd(pev * gate, gpp_ref[...], dx3v)
        dpe = (de * gate).astype(BF16)
        dz = (de * pev * gate * (1.0 - gate)).astype(BF16)
        _acc(acc_p, _dot_tn(p_ref[...].astype(BF16), dpe), first)
        _acc(acc_g, _dot_tn(h3_ref[...], dz), first)
        dh3 = _dot_nt(dz, wg_ref[...])
        dxn, dgp = _rms_bwd(x2_ref[...], gp_ref[...], dh3)
        dx2 = dx3v + dxn
        dx2_ref[...] = dx2
        df, dgpf = _rms_bwd(f_ref[...].astype(F32), gpf_ref[...], dx2)
        df_ref[...] = df.astype(BF16)
        _acc(dgpp_ref, dgpp, first)
        _acc(dgp_ref, dgp, first)
        _acc(dgpf_ref, dgpf, first)

        @pl.when(i == nt - 1)
        def _():
            dwg_ref[...] = acc_g[...].astype(BF16)
            dwp_ref[...] = acc_p[...].astype(BF16)

    return _launch(
        body, name=f"bwd_ple{layer}", grid=(nt,),
        in_specs=[_row_spec(D_MODEL)] * 5 + [_row_spec(PLE_DIM), _row_spec(D_MODEL), _full_spec((D_MODEL, D_MODEL)),
                  _vec_spec(), _vec_spec(), _vec_spec()],
        out_specs=[_row_spec(D_MODEL), _row_spec(D_MODEL), _full_spec((D_MODEL, D_MODEL)), _full_spec((PLE_DIM, D_MODEL)),
                   _vec_spec(), _vec_spec(), _vec_spec()],
        out_shape=[jax.ShapeDtypeStruct((T, D_MODEL), F32), jax.ShapeDtypeStruct((T, D_MODEL), BF16),
                   jax.ShapeDtypeStruct((D_MODEL, D_MODEL), BF16), jax.ShapeDtypeStruct((PLE_DIM, D_MODEL), BF16)]
                  + [jax.ShapeDtypeStruct((1, D_MODEL), F32)] * 3,
        scratch_shapes=[pltpu.VMEM((D_MODEL, D_MODEL), F32), pltpu.VMEM((PLE_DIM, D_MODEL), F32)],
        args=(dx3, x2, z, pe, h3, p, f, wgate, g_ple_post, g_ple, g_post_ffn), vmem=VMEM_BIG, job=job)


def _ple_loss_bwd(layer, x2, h3, p, f, target, wgate, wproj, g_ple_post, g_ple, g_post_ffn, job=None):
    T = x2.shape[0]
    tm = ROW_TILE
    nt = T // tm

    def body(x2_ref, h3_ref, p_ref, f_ref, tgt_ref, wg_ref, wp_ref, gpp_ref, gp_ref, gpf_ref,
             dx2_ref, df_ref, dwg_ref, dwp_ref, dgpp_ref, dgp_ref, dgpf_ref, loss_ref, acc_g, acc_p):
        i = pl.program_id(0)
        first = i == 0
        h3 = h3_ref[...]
        pb = p_ref[...].astype(BF16)
        x2v = x2_ref[...]
        gate = _sigmoid(_dot(h3, wg_ref[...]))
        pev = _dot(pb, wp_ref[...])
        e = pev * gate
        err = x2v + _rms(e, gpp_ref[...]) - tgt_ref[...]
        _acc(loss_ref, 0.5 * jnp.sum(jnp.mean(err * err, axis=-1, keepdims=True), axis=0, keepdims=True), first)
        dx3v = err * (1.0 / D_MODEL)
        de, dgpp = _rms_bwd(e, gpp_ref[...], dx3v)
        dpe = (de * gate).astype(BF16)
        dz = (de * pev * gate * (1.0 - gate)).astype(BF16)
        _acc(acc_p, _dot_tn(pb, dpe), first)
        _acc(acc_g, _dot_tn(h3, dz), first)
        dxn, dgp = _rms_bwd(x2v, gp_ref[...], _dot_nt(dz, wg_ref[...]))
        dx2 = dx3v + dxn
        dx2_ref[...] = dx2
        df, dgpf = _rms_bwd(f_ref[...].astype(F32), gpf_ref[...], dx2)
        df_ref[...] = df.astype(BF16)
        _acc(dgpp_ref, dgpp, first)
        _acc(dgp_ref, dgp, first)
        _acc(dgpf_ref, dgpf, first)

        @pl.when(i == nt - 1)
        def _():
            dwg_ref[...] = acc_g[...].astype(BF16)
            dwp_ref[...] = acc_p[...].astype(BF16)

    return _launch(
        body, name=f"ple_loss_bwd{layer}", grid=(nt,),
        in_specs=[_row_spec(D_MODEL), _row_spec(D_MODEL), _row_spec(PLE_DIM), _row_spec(D_MODEL), _row_spec(D_MODEL),
                  _full_spec((D_MODEL, D_MODEL)), _full_spec((PLE_DIM, D_MODEL)), _vec_spec(), _vec_spec(), _vec_spec()],
        out_specs=[_row_spec(D_MODEL), _row_spec(D_MODEL), _full_spec((D_MODEL, D_MODEL)), _full_spec((PLE_DIM, D_MODEL)),
                   _vec_spec(), _vec_spec(), _vec_spec(), _full_spec((1, 1))],
        out_shape=[jax.ShapeDtypeStruct((T, D_MODEL), F32), jax.ShapeDtypeStruct((T, D_MODEL), BF16),
                   jax.ShapeDtypeStruct((D_MODEL, D_MODEL), BF16), jax.ShapeDtypeStruct((PLE_DIM, D_MODEL), BF16)]
                  + [jax.ShapeDtypeStruct((1, D_MODEL), F32)] * 3 + [jax.ShapeDtypeStruct((1, 1), F32)],
        scratch_shapes=[pltpu.VMEM((D_MODEL, D_MODEL), F32), pltpu.VMEM((PLE_DIM, D_MODEL), F32)],
        args=(x2, h3, p, f, target, wgate, wproj, g_ple_post, g_ple, g_post_ffn), vmem=VMEM_BIG, job=job)


def _bwd_ffn_act(layer, df, gs, us, wgu, wd, job=None):
    T = df.shape[0]
    tm = min(FFN_ROW_TILE, T)
    nt = T // tm
    sub = tm // FFN_SUB_TILES
    last = FF_CHUNKS - 1
    wgu, wd = _column_views(wgu), _column_views(wd)
    n_gu, n_wd = len(wgu), len(wd)
    wd_cols = _column_ranges(wd)

    def body(df_ref, gs_ref, us_ref, *refs):
        wgu_refs, wd_refs = refs[:n_gu], refs[n_gu:n_gu + n_wd]
        dh_ref, dg_ref, du_ref, a_ref, acc_h = refs[n_gu + n_wd:]
        k = pl.program_id(0)
        i = pl.program_id(1)
        rows = pl.ds(pl.multiple_of(i * tm, tm), tm)
        dhs = []
        for s in range(FFN_SUB_TILES):
            r = pl.ds(s * sub, sub)
            g = gs_ref[r, :].astype(F32)
            u = us_ref[r, :].astype(F32)
            sg = _sigmoid(g)
            silu = g * sg
            a_ref[r, :] = (silu * u).astype(BF16)
            da = _add_all([_dot_nt(df_ref[r, c0:c1], w[...]) for (c0, c1), w in zip(wd_cols, wd_refs)])
            dg = (da * u * (sg * (1.0 + g * (1.0 - sg)))).astype(BF16)
            du = (da * silu).astype(BF16)
            dg_ref[r, :] = dg
            du_ref[r, :] = du
            dhs.append(jnp.concatenate([_dot(dg, w[0]) + _dot(du, w[1]) for w in wgu_refs], axis=1))
        dh = jnp.concatenate(dhs, axis=0)

        @pl.when(k == 0)
        def _():
            acc_h[rows, :] = dh

        @pl.when(jnp.logical_and(k > 0, k < last))
        def _():
            acc_h[rows, :] += dh

        @pl.when(k == last)
        def _():
            dh_ref[...] = acc_h[rows, :] + dh

    chunk_rows = pl.BlockSpec((None, tm, FF_BLOCK), lambda k, i: (k, i, 0))
    saved = jax.ShapeDtypeStruct((FF_CHUNKS, T, FF_BLOCK), BF16)
    return _launch(
        body, name=f"bwd_ffn_act{layer}", grid=(FF_CHUNKS, nt),
        in_specs=[pl.BlockSpec((tm, D_MODEL), lambda k, i: (i, 0)), chunk_rows, chunk_rows]
                 + [pl.BlockSpec((None, 2, FF_BLOCK, FFN_WEIGHT_COLS), lambda k, i, b=b: (k, 0, 0, b)) for _, b in wgu]
                 + [pl.BlockSpec((FF_BLOCK, FFN_WEIGHT_COLS), lambda k, i, b=b: (k, b)) for _, b in wd],
        out_specs=[pl.BlockSpec((tm, D_MODEL), lambda k, i: (jnp.where(k == last, i, 0), 0)),
                   chunk_rows, chunk_rows, chunk_rows],
        out_shape=[jax.ShapeDtypeStruct((T, D_MODEL), F32), saved, saved, saved],
        scratch_shapes=[pltpu.VMEM((T, D_MODEL), F32)],
        args=(df, gs, us, *[w for w, _ in wgu], *[w for w, _ in wd]), vmem=VMEM_BIG, job=job)


def _bwd_ffn_dw(layer, q, parts, h2, df, dg, du, a, job=None):
    T = h2.shape[0]
    width = D_MODEL // parts

    def body(h_ref, df_ref, dg_ref, du_ref, a_ref, dgu_ref, dwd_ref):
        h = h_ref[...]
        dgu_ref[0] = _dot_tn(dg_ref[...], h).astype(BF16)
        dgu_ref[1] = _dot_tn(du_ref[...], h).astype(BF16)
        dwd_ref[...] = _dot_tn(a_ref[...], df_ref[...]).astype(BF16)

    cols = pl.BlockSpec((T, width), lambda k: (0, q))
    chunk = pl.BlockSpec((None, T, FF_BLOCK), lambda k: (k, 0, 0))
    return _launch(
        body, name=f"bwd_ffn_dw{layer}_{q}", grid=(FF_CHUNKS,),
        in_specs=[cols, cols, chunk, chunk, chunk],
        out_specs=[pl.BlockSpec((None, 2, FF_BLOCK, width), lambda k: (k, 0, 0, 0)),
                   pl.BlockSpec((FF_BLOCK, width), lambda k: (k, 0))],
        out_shape=[jax.ShapeDtypeStruct((FF_CHUNKS, 2, FF_BLOCK, width), BF16),
                   jax.ShapeDtypeStruct((D_FF, width), BF16)],
        args=(h2, df, dg, du, a), vmem=VMEM_BIG, job=job)


def _bwd_attn_out(dx2, dh2, x1, y, attn, wo, g_ffn, g_post, job=None):
    T = x1.shape[0]
    nt = T // ROW_TILE

    def body(dx2_ref, dh2_ref, x1_ref, y_ref, a_ref, wo_ref, gffn_ref, gpost_ref,
             dx1_ref, da_ref, dwo_ref, dgf_ref, dgp_ref, acc):
        i = pl.program_id(0)
        first = i == 0
        dxn, dgf = _rms_bwd(x1_ref[...], gffn_ref[...], dh2_ref[...])
        dx1 = dx2_ref[...] + dxn
        dx1_ref[...] = dx1
        dy, dgp = _rms_bwd(y_ref[...].astype(F32), gpost_ref[...], dx1)
        dyb = dy.astype(BF16)
        da_ref[...] = _dot_nt(dyb, wo_ref[...]).astype(BF16)
        _acc(acc, _dot_tn(a_ref[...], dyb), first)
        _acc(dgf_ref, dgf, first)
        _acc(dgp_ref, dgp, first)

        @pl.when(i == nt - 1)
        def _():
            dwo_ref[...] = acc[...].astype(BF16)

    return _launch(
        body, name="bwd_attn_out", grid=(nt,),
        in_specs=[_row_spec(D_MODEL)] * 5 + [_full_spec((D_MODEL, D_MODEL)), _vec_spec(), _vec_spec()],
        out_specs=[_row_spec(D_MODEL), _row_spec(D_MODEL), _full_spec((D_MODEL, D_MODEL)), _vec_spec(), _vec_spec()],
        out_shape=[jax.ShapeDtypeStruct((T, D_MODEL), F32), jax.ShapeDtypeStruct((T, D_MODEL), BF16),
                   jax.ShapeDtypeStruct((D_MODEL, D_MODEL), BF16)] + [jax.ShapeDtypeStruct((1, D_MODEL), F32)] * 2,
        scratch_shapes=[pltpu.VMEM((D_MODEL, D_MODEL), F32)],
        args=(dx2, dh2, x1, y, attn, wo, g_ffn, g_post), job=job)


def _bwd_attention(q, dattn, kpad, vpad, sinks, job=None):
    T = q.shape[0]
    nb = T // ATT_BLOCK

    def body(q_ref, do_ref, k_ref, v_ref, sink_ref, dq_ref, dk_ref, dv_ref, ds_ref, s_scr, dp_scr, p_scr, dsb_scr,
             rel_scr, off_scr):
        n = pl.program_id(0)
        _att_mask(n, rel_scr, off_scr)

        @pl.when(n == 0)
        def _():
            dk_ref[...] = jnp.zeros_like(dk_ref)
            dv_ref[...] = jnp.zeros_like(dv_ref)
            ds_ref[...] = jnp.zeros_like(ds_ref)

        start = pl.multiple_of(n * ATT_BLOCK, ATT_BLOCK)
        win = pl.ds(start, 2 * ATT_BLOCK)
        kw = k_ref[win, :]
        vw = v_ref[win, :]
        lane = lax.broadcasted_iota(jnp.int32, (1, ATT_BLOCK), 1)
        dsink = jnp.zeros((1, ATT_BLOCK), F32)
        dqs, dks, dvs = [], [], []
        for kh in range(N_KV_HEADS):
            kk = kw[:, kh * HEAD_DIM:(kh + 1) * HEAD_DIM]
            vv = vw[:, kh * HEAD_DIM:(kh + 1) * HEAD_DIM]
            qs = _stack_heads(q_ref, kh)
            dos = _stack_heads(do_ref, kh)
            s_scr[...] = _dot_nt(qs, kk)
            dp_scr[...] = _dot_nt(dos, vv)
            for g in range(GQA_GROUP):
                h = kh * GQA_GROUP + g
                dsink_h = jnp.zeros((1, 1), F32)
                for row0 in range(0, ATT_BLOCK, ATT_SUB):
                    rows, sub = pl.ds(g * ATT_BLOCK + row0, ATT_SUB), pl.ds(row0, ATT_SUB)
                    pr, ps = _att_probs(s_scr[rows, :], rel_scr[sub, :], off_scr[sub, :], _alibi_slope(h),
                                        sink_ref[0, h])
                    dp = dp_scr[rows, :]
                    delta = jnp.sum(pr * dp, axis=-1, keepdims=True)
                    dsb_scr[rows, :] = (pr * (dp - delta) * ATT_SCALE).astype(BF16)
                    p_scr[rows, :] = pr.astype(BF16)
                    dsink_h = dsink_h - jnp.sum(ps * delta, axis=0, keepdims=True)
                dsink = dsink + jnp.where(lane == h, dsink_h, 0.0)
            dsb = dsb_scr[...]
            dqs += _unstack_heads(_dot(dsb, kk))
            dks.append(_dot_tn(dsb, qs))
            dvs.append(_dot_tn(p_scr[...], dos))
        dq_ref[...] = jnp.concatenate(dqs, axis=1).astype(BF16)
        dk_ref[win, :] += jnp.concatenate(dks, axis=1)
        dv_ref[win, :] += jnp.concatenate(dvs, axis=1)
        ds_ref[...] += dsink

    return _launch(
        body, name="bwd_attention", grid=(nb,),
        in_specs=[_row_spec(D_MODEL, ATT_BLOCK), _row_spec(D_MODEL, ATT_BLOCK), _full_spec((T + ATT_BLOCK, KV_DIM)),
                  _full_spec((T + ATT_BLOCK, KV_DIM)), pl.BlockSpec(memory_space=pltpu.SMEM)],
        out_specs=[_row_spec(D_MODEL, ATT_BLOCK), _full_spec((T + ATT_BLOCK, KV_DIM)), _full_spec((T + ATT_BLOCK, KV_DIM)),
                   _full_spec((1, ATT_BLOCK))],
        out_shape=[jax.ShapeDtypeStruct((T, D_MODEL), BF16), jax.ShapeDtypeStruct((T + ATT_BLOCK, KV_DIM), F32),
                   jax.ShapeDtypeStruct((T + ATT_BLOCK, KV_DIM), F32), jax.ShapeDtypeStruct((1, ATT_BLOCK), F32)],
        scratch_shapes=[pltpu.VMEM((ATT_GROUP_ROWS, 2 * ATT_BLOCK), F32)] * 2
                       + [pltpu.VMEM((ATT_GROUP_ROWS, 2 * ATT_BLOCK), BF16)] * 2
                       + [pltpu.VMEM((ATT_BLOCK, 2 * ATT_BLOCK), F32)] * 2,
        args=(q, dattn, kpad, vpad, sinks), vmem=VMEM_BIG, job=job)


def _bwd_qkv(dxres, dq, dkv, x3, h1, hk, wq, wkv, g_mix, g_kv, job=None):
    T = x3.shape[0]
    nt = T // ROW_TILE

    def body(dxr_ref, dq_ref, dkv_ref, x_ref, h1_ref, hk_ref, wq_ref, wkv_ref, gmix_ref, gkv_ref,
             dx_ref, dwq_ref, dwkv_ref, dgm_ref, dgk_ref, acc_q, acc_kv):
        i = pl.program_id(0)
        first = i == 0
        dqv = dq_ref[...]
        dkvv = dkv_ref[...]
        xv = x_ref[...]
        d1, dgm = _rms_bwd(xv, gmix_ref[...], _dot_nt(dqv, wq_ref[...]))
        d2, dgk = _rms_bwd(xv, gkv_ref[...], _dot_nt(dkvv, wkv_ref[...]))
        dx_ref[...] = dxr_ref[...] + d1 + d2
        _acc(acc_q, _dot_tn(h1_ref[...], dqv), first)
        _acc(acc_kv, _dot_tn(hk_ref[...], dkvv), first)
        _acc(dgm_ref, dgm, first)
        _acc(dgk_ref, dgk, first)

        @pl.when(i == nt - 1)
        def _():
            dwq_ref[...] = acc_q[...].astype(BF16)
            dwkv_ref[...] = acc_kv[...].astype(BF16)

    return _launch(
        body, name="bwd_qkv", grid=(nt,),
        in_specs=[_row_spec(D_MODEL), _row_spec(D_MODEL), _row_spec(2 * KV_DIM), _row_spec(D_MODEL), _row_spec(D_MODEL),
                  _row_spec(D_MODEL), _full_spec((D_MODEL, D_MODEL)), _full_spec((D_MODEL, 2 * KV_DIM)), _vec_spec(),
                  _vec_spec()],
        out_specs=[_row_spec(D_MODEL), _full_spec((D_MODEL, D_MODEL)), _full_spec((D_MODEL, 2 * KV_DIM)), _vec_spec(),
                   _vec_spec()],
        out_shape=[jax.ShapeDtypeStruct((T, D_MODEL), F32), jax.ShapeDtypeStruct((D_MODEL, D_MODEL), BF16),
                   jax.ShapeDtypeStruct((D_MODEL, 2 * KV_DIM), BF16)] + [jax.ShapeDtypeStruct((1, D_MODEL), F32)] * 2,
        scratch_shapes=[pltpu.VMEM((D_MODEL, D_MODEL), F32), pltpu.VMEM((D_MODEL, 2 * KV_DIM), F32)],
        args=(dxres, dq, dkv, x3, h1, hk, wq, wkv, g_mix, g_kv), job=job)


def _bwd_pool_mixer(dx2, dh2, x1, x, yraw, d, wp, scale, g_ffn, g_post, g_pre, job=None):
    T = x.shape[0]
    tm = ROW_TILE
    nt = T // tm

    def body(dx2_ref, dh2_ref, x1_ref, x_ref, yraw_ref, d_ref, wp_ref, sc_ref, gffn_ref, gpost_ref, gpre_ref,
             dx_ref, dwp_ref, dsc_ref, dgf_ref, dgp_ref, dgm_ref, carry, acc):
        i = pl.program_id(0)
        first = i == 0
        tile = nt - 1 - i

        @pl.when(first)
        def _():
            carry[...] = jnp.zeros_like(carry)

        dxn, dgf = _rms_bwd(x1_ref[...], gffn_ref[...], dh2_ref[...])
        dx1 = dx2_ref[...] + dxn
        yraw = yraw_ref[...].astype(F32)
        sc = sc_ref[...]
        dy, dgp = _rms_bwd(yraw * sc, gpost_ref[...], dx1)
        dsc = jnp.sum(dy * yraw, axis=0, keepdims=True)
        dyb = (dy * sc).astype(BF16)
        dv = d_ref[...]
        dds = []
        for g in range(N_POOL_GROUPS):
            cols = slice(g * POOL_GROUP, (g + 1) * POOL_GROUP)
            dds.append(_dot_nt(dyb[:, cols], wp_ref[g]))
            _acc(acc.at[g], _dot_tn(dv[:, cols], dyb[:, cols]), first)
        dd = jnp.concatenate(dds, axis=1)
        e = dd / _pool_counts(tile * tm, tm)
        ext = jnp.concatenate([e, carry[...]], axis=0)
        carry[...] = e[:POOL_HALO, :]
        sums = _window_sums(ext, lambda k: tm + POOL_HALO - k)[:tm, :]
        dxm, dgm = _rms_bwd(x_ref[...], gpre_ref[...], sums - dd)
        dx_ref[...] = dx1 + dxm
        _acc(dsc_ref, dsc, first)
        _acc(dgf_ref, dgf, first)
        _acc(dgp_ref, dgp, first)
        _acc(dgm_ref, dgm, first)

        @pl.when(i == nt - 1)
        def _():
            dwp_ref[...] = acc[...].astype(BF16)

    rev = pl.BlockSpec((tm, D_MODEL), lambda i: (nt - 1 - i, 0))
    return _launch(
        body, name="bwd_pool_mixer", grid=(nt,),
        in_specs=[rev] * 6 + [_full_spec((N_POOL_GROUPS, POOL_GROUP, POOL_GROUP))] + [_vec_spec()] * 4,
        out_specs=[rev, _full_spec((N_POOL_GROUPS, POOL_GROUP, POOL_GROUP))] + [_vec_spec()] * 4,
        out_shape=[jax.ShapeDtypeStruct((T, D_MODEL), F32),
                   jax.ShapeDtypeStruct((N_POOL_GROUPS, POOL_GROUP, POOL_GROUP), BF16)]
                  + [jax.ShapeDtypeStruct((1, D_MODEL), F32)] * 4,
        scratch_shapes=[pltpu.VMEM((POOL_HALO, D_MODEL), F32), pltpu.VMEM((N_POOL_GROUPS, POOL_GROUP, POOL_GROUP), F32)],
        args=(dx2, dh2, x1, x, yraw, d, wp, scale, g_ffn, g_post, g_pre), job=job)


def _my_place():
    return lax.axis_index("x"), lax.axis_index("y"), lax.axis_index("c")


def _dev_index(px, py, pc):
    return 4 * px + 2 * py + pc


def _peer_by_relation(r):
    x, y, c = _my_place()
    return (x ^ ((r >> 2) & 1), y ^ ((r >> 1) & 1), c ^ (r & 1))


def _slot_pool(ref, j):
    return ref.at[:, pl.ds(pl.multiple_of(j * 32, 32), 32), :]


def _slot_scale(ref, j):
    return ref.at[:, pl.ds(pl.multiple_of(j * 128, 128), 128)]


def _slot_rows128(ref, j):
    return ref.at[pl.ds(pl.multiple_of(j * 128, 128), 128), :]


def _slot_gu(ref, j):
    return ref.at[j % FF_CHUNKS, j // FF_CHUNKS]


def _slot_wd(ref, j):
    return ref.at[pl.ds(pl.multiple_of(j * WD_ROWS, 16), WD_ROWS), :]


def _slot_cols128(ref, j):
    return ref.at[:, pl.ds(pl.multiple_of(j * 128, 128), 128)]


_GATHERED = {
    "pool": ((N_POOL_GROUPS, POOL_GROUP, POOL_GROUP), BF16, _slot_pool),
    "scale": ((1, D_MODEL), F32, _slot_scale),
    "kv": ((D_MODEL, 2 * KV_DIM), BF16, _slot_rows128),
    "q": ((D_MODEL, D_MODEL), BF16, _slot_rows128),
    "o": ((D_MODEL, D_MODEL), BF16, _slot_rows128),
    "gu": ((FF_CHUNKS, 2, FF_BLOCK, D_MODEL), BF16, _slot_gu),
    "wd": ((D_FF, D_MODEL), BF16, _slot_wd),
    "guh": ((FF_CHUNKS, 2, FF_BLOCK, D_MODEL // 2), BF16, _slot_gu),
    "wdh": ((D_FF, D_MODEL // 2), BF16, _slot_wd),
    "gate": ((D_MODEL, D_MODEL), BF16, _slot_rows128),
    "proj": ((PLE_DIM, D_MODEL), BF16, _slot_cols128),
}


def _no_compute():
    pass


class _AllGather:
    peers = ("sibling", "x", "y")

    def __init__(self, names, shards):
        self.kinds = [_GATHERED[n.rstrip("01_")] for n in names]
        entries = [shards[n] if isinstance(shards[n], tuple) else (shards[n], None) for n in names]
        self.args = [array for array, _ in entries]
        self.columns = [columns for _, columns in entries]
        self.out_shape = [jax.ShapeDtypeStruct(shape, dtype) for shape, dtype, _ in self.kinds]
        n = len(names)
        self.scratch = [pltpu.SemaphoreType.DMA((n, 7)), pltpu.SemaphoreType.DMA((n, 7)), pltpu.SemaphoreType.DMA((n,))]

    def _plan(self, srcs, outs, sems):
        send_sems, recv_sems, local_sems = sems
        x, y, c = _my_place()

        def slot(t, dev):
            return self.kinds[t][2](outs[t], _dev_index(*dev))

        def copy(t, k, block, to, src=None):
            return pltpu.make_async_remote_copy(
                src_ref=slot(t, block) if src is None else src, dst_ref=slot(t, block),
                send_sem=send_sems.at[t, k], recv_sem=recv_sems.at[t, k], device_id=to, device_id_type=MESH)

        return types.SimpleNamespace(
            copy=copy, core=c, me=(x, y, c), sibling=(x, y, 1 - c),
            x_chip=(1 - x, y), y_chip=(x, 1 - y), far_chip=(1 - x, 1 - y),
            via=(x ^ (1 - c), y ^ c),
            onto=(x ^ c, y ^ (1 - c)),
            k_via=1 + c, k_onto=2 - c,
            local=[pltpu.make_async_copy(self._shard(srcs, t), slot(t, (x, y, c)), local_sems.at[t])
                   for t in range(len(srcs))])

    def _shard(self, srcs, t):
        if self.columns[t] is None:
            return srcs[t]
        first, end = self.columns[t]
        return srcs[t].at[:, first:end]

    def start(self, srcs, outs, sems):
        p = self._plan(srcs, outs, sems)
        for cp in p.local:
            cp.start()
        for t in range(len(srcs)):
            shard = self._shard(srcs, t)
            p.copy(t, 0, p.me, p.sibling, src=shard).start()
            p.copy(t, 1, p.me, (*p.x_chip, p.core), src=shard).start()
            p.copy(t, 2, p.me, (*p.y_chip, p.core), src=shard).start()

    def mid(self, srcs, outs, sems):
        p = self._plan(srcs, outs, sems)
        for t in range(len(srcs)):
            block = (*p.via, p.core)
            p.copy(t, p.k_via, block, p.me).wait_recv()
            p.copy(t, 3, block, (*p.onto, p.core)).start()
            p.copy(t, 3 + p.k_via, block, p.sibling).start()

    def late(self, srcs, outs, sems):
        p = self._plan(srcs, outs, sems)
        n = len(srcs)
        for t in range(n):
            block = (*p.onto, p.core)
            p.copy(t, p.k_onto, block, p.me).wait_recv()
            p.copy(t, 3 + p.k_onto, block, p.sibling).start()
        for t in range(n):
            block = (*p.far_chip, p.core)
            p.copy(t, 3, block, p.me).wait_recv()
            p.copy(t, 6, block, p.sibling).start()

    def finish(self, srcs, outs, sems):
        p = self._plan(srcs, outs, sems)
        n = len(srcs)
        other = 1 - p.core
        for t in range(n):
            p.copy(t, 0, (*p.me[:2], other), p.me).wait_recv()
            for k, chip in ((4, p.x_chip), (5, p.y_chip), (6, p.far_chip)):
                p.copy(t, k, (*chip, other), p.me).wait_recv()
            for k in range(7):
                p.copy(t, k, p.me, p.sibling).wait_send()
        for cp in p.local:
            cp.wait()


def _jobs_only(name, job=None):
    return _launch(_no_compute, name=name, grid=(), in_specs=[], out_specs=[], out_shape=[], args=(), job=job)


def _block_pool(ref, j):
    return ref.at[:, pl.ds(pl.multiple_of(j * 32, 32), 32), :]


def _block_rows128(ref, j):
    return ref.at[pl.ds(pl.multiple_of(j * 128, 128), 128), :]


def _block_gu(ref, j):
    return ref.at[j % FF_CHUNKS, j // FF_CHUNKS]


def _block_wd(ref, j):
    return ref.at[pl.ds(pl.multiple_of(j * WD_ROWS, 16), WD_ROWS), :]


def _block_cols128(ref, j):
    return ref.at[:, pl.ds(pl.multiple_of(j * 128, 128), 128)]


_SCATTERED = {
    "pool": ((N_POOL_GROUPS, 32, POOL_GROUP), _block_pool),
    "kv": ((128, 2 * KV_DIM), _block_rows128),
    "q": ((128, D_MODEL), _block_rows128),
    "o": ((128, D_MODEL), _block_rows128),
    "gu": ((FF_BLOCK, FF_PART), _block_gu),
    "wd": ((WD_ROWS, FF_PART), _block_wd),
    "guA": ((FF_BLOCK, FF_PART), lambda ref, j: _block_gu(ref, j).at[:, :FF_PART]),
    "guB": ((FF_BLOCK, FF_PART), lambda ref, j: _block_gu(ref, j).at[:, FF_PART:]),
    "wdA": ((WD_ROWS, FF_PART), lambda ref, j: _block_wd(ref, j).at[:, :FF_PART]),
    "wdB": ((WD_ROWS, FF_PART), lambda ref, j: _block_wd(ref, j).at[:, FF_PART:]),
    "gate": ((128, D_MODEL), _block_rows128),
    "proj": ((PLE_DIM, 128), _block_cols128),
}


class _SiblingSwap:
    peers = ("sibling",)

    def __init__(self, pieces):
        self.kinds = [_SCATTERED[kind] for kind, _ in pieces]
        self.args = [g for _, g in pieces]
        self.out_shape = [jax.ShapeDtypeStruct((N_CHIPS, *block), BF16) for block, _ in self.kinds]
        n = len(pieces)
        self.scratch = [pltpu.SemaphoreType.DMA((n, N_CHIPS)), pltpu.SemaphoreType.DMA((n, N_CHIPS))]

    def _copies(self, srcs, outs, sems):
        send_sems, recv_sems = sems
        x, y, c = _my_place()
        return [pltpu.make_async_remote_copy(
            src_ref=block(srcs[t], 2 * ch + 1 - c), dst_ref=outs[t].at[ch], send_sem=send_sems.at[t, ch],
            recv_sem=recv_sems.at[t, ch], device_id=(x, y, 1 - c), device_id_type=MESH)
            for t, (_, block) in enumerate(self.kinds) for ch in range(N_CHIPS)]

    def start(self, srcs, outs, sems):
        for cp in self._copies(srcs, outs, sems):
            cp.start()

    def finish(self, srcs, outs, sems):
        for cp in self._copies(srcs, outs, sems):
            cp.wait()


class _ChipScatter:
    N_BUFS = 4
    peers = ("x", "y")

    def __init__(self, pieces):
        self.kinds = [_SCATTERED[kind] for kind, _, _ in pieces]
        self.n = n = len(pieces)
        self.args = [g for _, g, _ in pieces] + [s for _, _, s in pieces]
        self.out_shape = [jax.ShapeDtypeStruct((2, *block), BF16) for block, _ in self.kinds]
        self.scratch = []
        for block, _ in self.kinds:
            self.scratch += [pltpu.VMEM((N_CHIPS, *block), BF16)] * 3 + [pltpu.VMEM((2, *block), BF16)]
        dma = pltpu.SemaphoreType.DMA
        self.scratch += [dma((n, N_CHIPS + 1)), dma((n, 2)), dma((n, 2)), dma((n,)), dma((n,)), dma((n,))]

    def _plan(self, outs, scr):
        n = self.n
        first_send, first_recv, second_send, second_recv, keep_sems = scr[self.N_BUFS * n + 1:]
        x, y, c = _my_place()
        via = (x ^ (1 - c), y ^ c)
        onto = (x ^ c, y ^ (1 - c))
        index = lambda chip: 2 * chip[0] + chip[1]
        first, second, keep = [], [], []
        for t in range(n):
            total, inbox = scr[self.N_BUFS * t + 2], scr[self.N_BUFS * t + 3]
            for k, chip in enumerate((via, (1 - x, 1 - y))):
                first.append(pltpu.make_async_remote_copy(
                    src_ref=total.at[index(chip)], dst_ref=inbox.at[k], send_sem=first_send.at[t, k],
                    recv_sem=first_recv.at[t, k], device_id=(*via, c), device_id_type=MESH))
            second.append(pltpu.make_async_remote_copy(
                src_ref=total.at[index(onto)], dst_ref=outs[t].at[1], send_sem=second_send.at[t],
                recv_sem=second_recv.at[t], device_id=(*onto, c), device_id_type=MESH))
            keep.append(pltpu.make_async_copy(total.at[index((x, y))], outs[t].at[0], keep_sems.at[t]))
        return first, second, keep, index((x, y)), index(onto)

    def start(self, ins, outs, scr):
        n = self.n
        load_sems = scr[self.N_BUFS * n]
        c = lax.axis_index("c")
        loads = []
        for t, (_, block) in enumerate(self.kinds):
            mine, theirs = scr[self.N_BUFS * t], scr[self.N_BUFS * t + 1]
            loads += [pltpu.make_async_copy(block(ins[t], 2 * ch + c), mine.at[ch], load_sems.at[t, ch])
                      for ch in range(N_CHIPS)]
            loads.append(pltpu.make_async_copy(ins[n + t], theirs, load_sems.at[t, N_CHIPS]))
        for cp in loads:
            cp.start()
        for cp in loads:
            cp.wait()
        for t in range(n):
            mine, theirs, total = scr[self.N_BUFS * t:self.N_BUFS * t + 3]
            for ch in range(N_CHIPS):
                total[ch] = (mine[ch].astype(F32) + theirs[ch].astype(F32)).astype(BF16)
        for cp in self._plan(outs, scr)[0]:
            cp.start()

    def mid(self, ins, outs, scr):
        first, second, keep, me, onto = self._plan(outs, scr)
        for cp in first:
            cp.wait_recv()
        for t in range(self.n):
            total, inbox = scr[self.N_BUFS * t + 2], scr[self.N_BUFS * t + 3]
            for k, slot in enumerate((me, onto)):
                total[slot] = (total[slot].astype(F32) + inbox[k].astype(F32)).astype(BF16)
        for cp in second + keep:
            cp.start()

    def finish(self, ins, outs, scr):
        first, second, keep, _, _ = self._plan(outs, scr)
        for cp in first:
            cp.wait_send()
        for cp in second + keep:
            cp.wait()


class _ToEveryone:
    peers = _EVERYONE

    def __init__(self, scattered=(), gathered=()):
        self.blocks = [_SCATTERED[kind][1] for kind, _ in scattered] + [None] * len(gathered)
        self.args = [g for _, g in scattered] + list(gathered)
        self.out_shape = [jax.ShapeDtypeStruct((N_DEV, *_SCATTERED[kind][0]), BF16) for kind, _ in scattered]
        self.out_shape += [jax.ShapeDtypeStruct((N_DEV, *a.shape), a.dtype) for a in gathered]
        n = len(self.args)
        self.scratch = [pltpu.SemaphoreType.DMA((n, N_DEV - 1)), pltpu.SemaphoreType.DMA((n, N_DEV - 1)),
                        pltpu.SemaphoreType.DMA((n,))]

    def _copies(self, srcs, outs, sems):
        send_sems, recv_sems, local_sems = sems
        me = _dev_index(*_my_place())
        copies = []
        for t, block in enumerate(self.blocks):
            part = (lambda j, t=t, block=block: srcs[t] if block is None else block(srcs[t], j))
            copies.append(pltpu.make_async_copy(part(me), outs[t].at[me], local_sems.at[t]))
            for r in range(1, N_DEV):
                peer = _peer_by_relation(r)
                copies.append(pltpu.make_async_remote_copy(
                    src_ref=part(_dev_index(*peer)), dst_ref=outs[t].at[me], send_sem=send_sems.at[t, r - 1],
                    recv_sem=recv_sems.at[t, r - 1], device_id=peer, device_id_type=MESH))
        return copies

    def start(self, srcs, outs, sems):
        for cp in self._copies(srcs, outs, sems):
            cp.start()

    def finish(self, srcs, outs, sems):
        for cp in self._copies(srcs, outs, sems):
            cp.wait()


class _Jobs:
    def __init__(self, *jobs):
        self.jobs = jobs
        together = {p for j in jobs for p in j.peers}
        self.peers = tuple(p for p in _EVERYONE if p in together)
        self.args = [a for j in jobs for a in j.args]
        self.out_shape = [o for j in jobs for o in j.out_shape]
        self.scratch = [s for j in jobs for s in j.scratch]

    def _split(self, refs, attr):
        at = 0
        for j in self.jobs:
            n = len(getattr(j, attr))
            yield refs[at:at + n]
            at += n

    def _each(self, ins, outs, scr):
        return zip(self.jobs, self._split(ins, "args"), self._split(outs, "out_shape"), self._split(scr, "scratch"))

    def start(self, ins, outs, scr):
        for j, i, o, s in self._each(ins, outs, scr):
            j.start(i, o, s)

    def mid(self, ins, outs, scr):
        for j, i, o, s in self._each(ins, outs, scr):
            if hasattr(j, "mid"):
                j.mid(i, o, s)

    def late(self, ins, outs, scr):
        for j, i, o, s in self._each(ins, outs, scr):
            if hasattr(j, "late"):
                j.late(i, o, s)

    def finish(self, ins, outs, scr):
        for j, i, o, s in self._each(ins, outs, scr):
            j.finish(i, o, s)

    def split_outputs(self, outs):
        return list(self._split(outs, "out_shape"))


def _adamw_math(w, g, m, v):
    m = ADAM_B1 * m + (1.0 - ADAM_B1) * g
    v = ADAM_B2 * v + (1.0 - ADAM_B2) * (g * g)
    m_hat = m / (1.0 - ADAM_B1 ** ADAM_STEP)
    v_hat = v / (1.0 - ADAM_B2 ** ADAM_STEP)
    delta = -ADAM_LR * (m_hat / (jnp.sqrt(v_hat) + ADAM_EPS) + ADAM_WD * w)
    return delta, m, v


def _adamw(name, w, m, v, landings, n_col_blocks=1, job=None):
    n_slots, r, c = landings[0].shape
    grid = (w.shape[0] // r, n_col_blocks)

    def body(w_ref, m_ref, v_ref, *rest):
        l_refs, (g_ref, d_ref, nm_ref, nv_ref) = rest[:len(landings)], rest[len(landings):]
        step = pl.program_id(0) * n_col_blocks + pl.program_id(1)
        for idx, l_ref in enumerate(l_refs):
            @pl.when(step == idx)
            def _(l_ref=l_ref):
                g = l_ref[0].astype(F32)
                for s in range(1, n_slots):
                    g = g + l_ref[s].astype(F32)
                g_ref[...] = g
                d_ref[...], nm_ref[...], nv_ref[...] = _adamw_math(w_ref[...], g, m_ref[...], v_ref[...])

    spec = pl.BlockSpec((r, c), lambda a, b: (a, b))
    return _launch(
        body, name=f"adamw_{name}", grid=grid,
        in_specs=[spec, spec, spec] + [_full_spec((n_slots, r, c))] * len(landings),
        out_specs=[spec] * 4, out_shape=[jax.ShapeDtypeStruct(w.shape, F32)] * 4,
        args=(w, m, v, *landings), vmem=VMEM_BIG, job=job)


_SMALL = (("pre_mix_g", SV_PRE_MIX, 2), ("post_mix_g", SV_POST_MIX, 2), ("pre_ffn_g", SV_PRE_FFN, 2),
          ("post_ffn_g", SV_POST_FFN, 2), ("ple_g", SV_PLE, 2), ("ple_post_g", SV_PLE_POST, 2), ("kv_g", SV_KV, 1),
          ("pool_scale", SV_POOL_SCALE, 1), ("sinks", SV_SINKS, 1))


def _small_adamw(slabs, params):
    flat = [a for name, _, _ in _SMALL for a in params[name]]
    n_in = 1 + len(flat)

    def body(*refs):
        slabs_ref, wmv = refs[0], refs[1:n_in]
        loss_ref, outs, total = refs[n_in], refs[n_in + 1:-1], refs[-1]
        me = _dev_index(*_my_place())
        g = slabs_ref[0]
        for s in range(1, N_DEV):
            g = g + slabs_ref[s]
        total[...] = g
        loss_ref[...] = total[SV_LOSS:SV_LOSS + 1, 0:1]
        for idx, (name, row, n_rows) in enumerate(_SMALL):
            w_ref, m_ref, v_ref = wmv[3 * idx:3 * idx + 3]
            g_ref, d_ref, nm_ref, nv_ref = outs[4 * idx:4 * idx + 4]
            if name == "pool_scale":
                g = total[row:row + 1, pl.ds(pl.multiple_of(me * 128, 128), 128)]
            else:
                g = total[row:row + n_rows, 0:w_ref.shape[1]]
            g_ref[...] = g
            d_ref[...], nm_ref[...], nv_ref[...] = _adamw_math(w_ref[...], g, m_ref[...], v_ref[...])

    out_shape = [jax.ShapeDtypeStruct((1, 1), F32)]
    for name, _, _ in _SMALL:
        out_shape += [jax.ShapeDtypeStruct(params[name][0].shape, F32)] * 4
    res, _ = _launch(
        body, name="small_adamw", grid=(1,),
        in_specs=[_full_spec(a.shape) for a in (slabs, *flat)], out_specs=[_full_spec(s.shape) for s in out_shape],
        out_shape=out_shape, scratch_shapes=[pltpu.VMEM((SV_ROWS, D_MODEL), F32)], args=(slabs, *flat))
    return res[0], {name: res[1 + 4 * idx:5 + 4 * idx] for idx, (name, _, _) in enumerate(_SMALL)}


def _local_step(x, p, tgt, gains, sinks, shards, weights):
    row = lambda first_row, layer: _Gain(gains, first_row + layer)
    gather = lambda *names: _AllGather(names, shards)
    g_pre_mix, g_post_mix, g_pre_ffn, g_post_ffn = SV_PRE_MIX, SV_POST_MIX, SV_PRE_FFN, SV_POST_FFN
    g_ple, g_ple_post, g_kv = SV_PLE, SV_PLE_POST, _Gain(gains, SV_KV)

    (dpool,), (wp, scale, wgu0, wd0) = _fwd_pool(x, row(g_pre_mix, 0), job=gather("pool", "scale", "gu0", "wd0"))
    wgu0, wd0 = [wgu0], [wd0]
    (x1_0, h2_0, yraw), _ = _fwd_pool_mixer(x, dpool, wp, scale, row(g_post_mix, 0), row(g_pre_ffn, 0))
    (gs0, us0, f0, x2_0, h3_0), (wgate0, wproj0, wkv, wq, wo, wd1_a) = _fwd_ffn(
        0, h2_0, x1_0, wgu0, wd0, row(g_post_ffn, 0), row(g_ple, 0),
        job=gather("gate0", "proj0", "kv", "q", "o", "wdh1_0"))
    (x3_0, z0, pe0, hk, h1, q, kv), (wgu1_a,) = _fwd_ple_qkv(
        x2_0, h3_0, p[0], wgate0, wproj0, row(g_ple_post, 0), g_kv, row(g_pre_mix, 1), wkv, wq,
        job=gather("guh1_0"))
    front = ((ATT_BLOCK, 0), (0, 0))
    kpad = jnp.pad(kv[:, :KV_DIM], front)
    vpad = jnp.pad(kv[:, KV_DIM:], front)
    (attn,), (wgu1_b,) = _fwd_attention(q, kpad, vpad, sinks, job=gather("guh1_1"))
    (y1, x1_1, h2_1), (wd1_b,) = _fwd_attn_out(attn, x3_0, wo, row(g_post_mix, 1), row(g_pre_ffn, 1),
                                               job=gather("wdh1_1"))
    wgu1, wd1 = [wgu1_a, wgu1_b], [wd1_a, wd1_b]
    (gs1, us1, f1, x2_1, h3_1), (wgate1, wproj1) = _fwd_ffn(
        1, h2_1, x1_1, wgu1, wd1, row(g_post_ffn, 1), row(g_ple, 1), job=gather("gate1", "proj1"))

    produced, swapped, landed = {}, {}, {}

    def kind_of(name):
        return name.rstrip("0123_")

    def hosted(call, *args, swap=(), spread=(), extra=None):
        jobs = []
        if swap:
            jobs.append(_SiblingSwap([(kind_of(n), produced[n]) for n in swap]))
        if spread:
            jobs.append(_ChipScatter([(kind_of(n), produced[n], swapped[n]) for n in spread]))
        if extra is not None:
            jobs.append(extra)
        jobs = _Jobs(*jobs)
        outs, job_outs = call(*args, job=jobs)
        parts = jobs.split_outputs(job_outs)
        if swap:
            swapped.update(zip(swap, parts.pop(0)))
        if spread:
            landed.update(zip(spread, parts.pop(0)))
        return outs if extra is None else (outs, parts.pop(0))

    ffn_q = lambda layer, qtr: (f"gu{layer}_{qtr}", f"wd{layer}_{qtr}")

    dx2_1, df1, produced["gate1"], produced["proj1"], dg_ple_post1, dg_ple1, dg_post_ffn1, loss = hosted(
        _ple_loss_bwd, 1, x2_1, h3_1, p[1], f1, tgt, wgate1, wproj1, row(g_ple_post, 1), row(g_ple, 1),
        row(g_post_ffn, 1))
    dh2_1, dg1, du1, a1 = hosted(_bwd_ffn_act, 1, df1, gs1, us1, wgu1, wd1, swap=("gate1", "proj1"))
    dgu1, dwd1 = hosted(_bwd_ffn_dw, 1, 0, 1, h2_1, df1, dg1, du1, a1, spread=("gate1", "proj1"))
    produced.update(guA1=dgu1, guB1=dgu1, wdA1=dwd1, wdB1=dwd1)
    dx1_1, dattn, produced["o"], dg_pre_ffn1, dg_post_mix1 = hosted(
        _bwd_attn_out, dx2_1, dh2_1, x1_1, y1, attn, wo, row(g_pre_ffn, 1), row(g_post_mix, 1),
        swap=("guA1", "wdA1", "guB1", "wdB1"))
    dq, dkpad, dvpad, dsinks = hosted(_bwd_attention, q, dattn, kpad, vpad, sinks, spread=("guA1", "wdA1"))
    dkv = jnp.concatenate([dkpad[ATT_BLOCK:], dvpad[ATT_BLOCK:]], axis=1).astype(BF16)
    dx3_0, produced["q"], produced["kv"], dg_pre_mix1, dg_kv = hosted(
        _bwd_qkv, dx1_1, dq, dkv, x3_0, h1, hk, wq, wkv, row(g_pre_mix, 1), g_kv, swap=("o",), spread=("wdB1",))
    dx2_0, df0, produced["gate0"], produced["proj0"], dg_ple_post0, dg_ple0, dg_post_ffn0 = hosted(
        _bwd_ple, 0, dx3_0, x2_0, z0, pe0, h3_0, p[0], f0, wgate0, row(g_ple_post, 0), row(g_ple, 0),
        row(g_post_ffn, 0), swap=("q", "kv"), spread=("guB1",))
    for half, letter in enumerate("AB"):
        landed[f"gu1_{half}"], landed[f"wd1_{half}"] = landed[f"gu{letter}1"], landed[f"wd{letter}1"]
    dh2_0, dg0, du0, a0 = hosted(_bwd_ffn_act, 0, df0, gs0, us0, wgu0, wd0,
                                 swap=("gate0", "proj0"), spread=("o", "q", "kv"))
    part_hosts = [dict(spread=("gate0", "proj0")), dict(swap=ffn_q(0, 0))]
    for part in range(FF_PARTS):
        produced[f"gu0_{part}"], produced[f"wd0_{part}"] = hosted(
            _bwd_ffn_dw, 0, part, FF_PARTS, h2_0, df0, dg0, du0, a0, **part_hosts[part])
    grad_x, produced["pool"], dscale, dg_pre_ffn0, dg_post_mix0, dg_pre_mix0 = hosted(
        _bwd_pool_mixer, dx2_0, dh2_0, x1_0, x, yraw, dpool, wp, scale, row(g_pre_ffn, 0), row(g_post_mix, 0),
        row(g_pre_mix, 0), swap=ffn_q(0, 1), spread=ffn_q(0, 0))

    def update(name, n_col_blocks=1, pieces=None, swap=(), spread=()):
        w, m, v = weights[name]
        rows = w.size // w.shape[-1]
        flat = [landed[n].reshape(landed[n].shape[0], -1, landed[n].shape[-1])
                for n in (pieces or [kind_short[name]])]
        outs = hosted(_adamw, name, w.reshape(rows, -1), m.reshape(rows, -1), v.reshape(rows, -1), flat,
                      n_col_blocks, swap=swap, spread=spread)
        return [o.reshape(w.shape) for o in outs]

    kind_short = {"w_q": "q", "w_kv": "kv", "w_o": "o", "pool_w": "pool"}
    upd = {}
    lanes = lambda a: jnp.pad(a, ((0, 0), (0, D_MODEL - a.shape[1])))
    small = jnp.concatenate([
        dg_pre_mix0, dg_pre_mix1, dg_post_mix0, dg_post_mix1, dg_pre_ffn0, dg_pre_ffn1, dg_post_ffn0, dg_post_ffn1,
        dg_ple0, dg_ple1, dg_ple_post0, dg_ple_post1, dg_kv, dscale, lanes(dsinks[:, :N_HEADS]), lanes(loss)], axis=0)

    everyone = _ToEveryone(scattered=[("pool", produced["pool"])], gathered=[small])
    _, (landed["pool"], slabs) = hosted(_jobs_only, "scatter_tail", spread=ffn_q(0, 1), extra=everyone)
    upd["w_ple_gate"] = update("w_ple_gate", pieces=("gate0", "gate1"))
    upd["w_ple_proj"] = update("w_ple_proj", pieces=("proj0", "proj1"))
    for name in ("w_q", "w_kv", "w_o", "pool_w"):
        upd[name] = update(name)
    upd["w_gu"] = update("w_gu", FF_PARTS,
                         pieces=[f"gu{layer}_{qtr}" for layer in range(2) for qtr in range(FF_PARTS)])
    upd["w_gu"] = [jnp.swapaxes(a, 1, 2) for a in upd["w_gu"]]
    upd["w_down"] = update("w_down", FF_PARTS,
                           pieces=[f"wd{layer}_{qtr}" for layer in range(2) for qtr in range(FF_PARTS)])
    return grad_x, upd, slabs


def kernel(x, p, pre_mix_g, post_mix_g, pre_ffn_g, post_ffn_g, pool_w, pool_scale, kv_g, w_kv, w_q, sinks, w_o, w_gu, w_down, ple_g, w_ple_gate, w_ple_proj, ple_post_g, loss_target, m_pre_mix_g, m_post_mix_g, m_pre_ffn_g, m_post_ffn_g, m_pool_w, m_pool_scale, m_kv_g, m_w_kv, m_w_q, m_sinks, m_w_o, m_w_gu, m_w_down, m_ple_g, m_w_ple_gate, m_w_ple_proj, m_ple_post_g, v_pre_mix_g, v_post_mix_g, v_pre_ffn_g, v_post_ffn_g, v_pool_w, v_pool_scale, v_kv_g, v_w_kv, v_w_q, v_sinks, v_w_o, v_w_gu, v_w_down, v_ple_g, v_w_ple_gate, v_w_ple_proj, v_ple_post_g):
    shards = {"pool": pool_w[0].astype(BF16), "scale": pool_scale, "kv": w_kv.astype(BF16),
              "q": w_q[0].astype(BF16), "o": w_o[0].astype(BF16)}
    for layer in range(2):
        shards[f"gu{layer}"] = w_gu[layer].T.astype(BF16)
        shards[f"wd{layer}"] = w_down[layer].astype(BF16)
        for half in range(2):
            cols = (half * D_MODEL // 2, (half + 1) * D_MODEL // 2)
            shards[f"guh{layer}_{half}"] = (shards[f"gu{layer}"], cols)
            shards[f"wdh{layer}_{half}"] = (shards[f"wd{layer}"], cols)
        shards[f"gate{layer}"] = w_ple_gate[layer].astype(BF16)
        shards[f"proj{layer}"] = w_ple_proj[layer].astype(BF16)
    gains = jnp.concatenate([pre_mix_g, post_mix_g, pre_ffn_g, post_ffn_g, ple_g, ple_post_g, kv_g[None, :]],
                            axis=0).reshape(-1, 1, D_MODEL)
    weights = {"pool_w": (pool_w, m_pool_w, v_pool_w), "w_kv": (w_kv, m_w_kv, v_w_kv), "w_q": (w_q, m_w_q, v_w_q),
               "w_o": (w_o, m_w_o, v_w_o), "w_down": (w_down, m_w_down, v_w_down),
               "w_gu": tuple(jnp.swapaxes(a, 1, 2) for a in (w_gu, m_w_gu, v_w_gu)),
               "w_ple_gate": (w_ple_gate, m_w_ple_gate, v_w_ple_gate),
               "w_ple_proj": (w_ple_proj, m_w_ple_proj, v_w_ple_proj)}
    per_layer = p.reshape(p.shape[0], *p.shape[2:])
    p_rows = [_LayerRows(per_layer, layer) for layer in range(2)]
    grad_x, upd, slabs = _local_step(x[0], p_rows, loss_target[0], gains, sinks, shards, weights)

    small_params = {
        "pre_mix_g": (pre_mix_g, m_pre_mix_g, v_pre_mix_g), "post_mix_g": (post_mix_g, m_post_mix_g, v_post_mix_g),
        "pre_ffn_g": (pre_ffn_g, m_pre_ffn_g, v_pre_ffn_g), "post_ffn_g": (post_ffn_g, m_post_ffn_g, v_post_ffn_g),
        "ple_g": (ple_g, m_ple_g, v_ple_g), "ple_post_g": (ple_post_g, m_ple_post_g, v_ple_post_g),
        "kv_g": (kv_g[None, :], m_kv_g[None, :], v_kv_g[None, :]),
        "pool_scale": (pool_scale, m_pool_scale, v_pool_scale), "sinks": (sinks, m_sinks, v_sinks)}
    loss, small_upd = _small_adamw(slabs, small_params)
    small_upd["kv_g"] = [a[0] for a in small_upd["kv_g"]]
    upd.update(small_upd)

    names = ["pre_mix_g", "post_mix_g", "pre_ffn_g", "post_ffn_g", "pool_w", "pool_scale", "kv_g", "w_kv", "w_q",
             "sinks", "w_o", "w_gu", "w_down", "ple_g", "w_ple_gate", "w_ple_proj", "ple_post_g"]
    outs = [loss[0, 0], grad_x[None]]
    for kind in range(4):
        outs += [upd[n][kind] for n in names]
    return tuple(outs)
```

```python
import functools
import types

import jax
import jax.numpy as jnp
from jax import lax
from jax.experimental import pallas as pl
from jax.experimental.pallas import tpu as pltpu

F32 = jnp.float32
BF16 = jnp.bfloat16

N_DEV = 8
D_MODEL = 1024
N_POOL_GROUPS = 4
POOL_GROUP = 256
POOL_HALO = 16
HEAD_DIM = 64
N_HEADS = 16
N_KV_HEADS = 4
GQA_GROUP = 4
KV_DIM = N_KV_HEADS * HEAD_DIM
ATT_BLOCK = 128
D_FF = 2816
FF_CHUNKS = 4
FF_BLOCK = D_FF // FF_CHUNKS
WD_ROWS = D_FF // N_DEV
FF_PARTS = 2
FF_PART = D_MODEL // FF_PARTS
N_CHIPS = 4
PLE_DIM = 256
EPS = 1e-6
NEG_INF = -1e30
ATT_SCALE = HEAD_DIM ** -0.5

ADAM_LR = 0.001
ADAM_B1 = 0.9
ADAM_B2 = 0.999
ADAM_EPS = 1e-08
ADAM_WD = 0.01
ADAM_STEP = 10

ROW_TILE = 512
FFN_ROW_TILE = 512
FFN_WEIGHT_COLS = 512
FFN_SUB_TILES = 1
VMEM_BIG = 60 * 1024 * 1024
VMEM_MID = 56 * 1024 * 1024
HBM_PIN_ELEMS = 1024

SV_ROWS = 16
SV_PRE_MIX, SV_POST_MIX, SV_PRE_FFN, SV_POST_FFN, SV_PLE, SV_PLE_POST = 0, 2, 4, 6, 8, 10
SV_KV, SV_POOL_SCALE, SV_SINKS, SV_LOSS = 12, 13, 14, 15

MESH = pl.DeviceIdType.MESH
ANY = pl.BlockSpec(memory_space=pl.ANY)


def _dot(a, b):
    return jnp.dot(a, b, preferred_element_type=F32)


def _dot_nt(a, b):
    return lax.dot_general(a, b, (((1,), (1,)), ((), ())), preferred_element_type=F32)


def _dot_tn(a, b):
    return lax.dot_general(a, b, (((0,), (0,)), ((), ())), preferred_element_type=F32)


def _rstd(x):
    return lax.rsqrt(jnp.mean(x * x, axis=-1, keepdims=True) + EPS)


def _rms(x, g):
    return x * _rstd(x) * g


def _rms_bwd(x, g, dy):
    r = _rstd(x)
    n = x * r
    dn = dy * g
    dx = r * (dn - n * jnp.mean(dn * n, axis=-1, keepdims=True))
    dg = jnp.sum(dy * n, axis=0, keepdims=True)
    return dx, dg


def _add_all(terms):
    return functools.reduce(jnp.add, terms)


def _sigmoid(x):
    return 1.0 / (1.0 + jnp.exp(-x))


def _acc(ref, val, first):
    @pl.when(first)
    def _():
        ref[...] = val

    @pl.when(jnp.logical_not(first))
    def _():
        ref[...] += val


def _pool_counts(row0, rows):
    t = row0 + lax.broadcasted_iota(jnp.int32, (rows, D_MODEL), 0) + 1
    grp = lax.broadcasted_iota(jnp.int32, (rows, D_MODEL), 1) // POOL_GROUP
    win = jnp.left_shift(2, grp)
    return jnp.minimum(t, win).astype(F32)


def _window_sums(ext, shift_of):
    outs = []
    s = ext
    for gi in range(N_POOL_GROUPS):
        s = s + pltpu.roll(s, shift_of(1 << gi), axis=0)
        outs.append(s[:, :POOL_GROUP])
        s = s[:, POOL_GROUP:]
    return jnp.concatenate(outs, axis=1)


def _cparams(n_axes, vmem, collective_id=None):
    return pltpu.CompilerParams(dimension_semantics=("arbitrary",) * n_axes, vmem_limit_bytes=vmem,
                                collective_id=collective_id)


_EVERYONE = ("sibling", "x", "y", "far", "x sibling", "y sibling", "far sibling")
_PEER_SETS = (("sibling", "x", "y"), ("sibling",), ("x", "y"), _EVERYONE)


def _meet(peers):
    x, y, c = lax.axis_index("x"), lax.axis_index("y"), lax.axis_index("c")
    device = {"sibling": (x, y, 1 - c), "x": (1 - x, y, c), "y": (x, 1 - y, c), "far": (1 - x, 1 - y, c),
              "x sibling": (1 - x, y, 1 - c), "y sibling": (x, 1 - y, 1 - c), "far sibling": (1 - x, 1 - y, 1 - c)}
    barrier = pltpu.get_barrier_semaphore()
    for peer in peers:
        pl.semaphore_signal(barrier, inc=1, device_id=device[peer], device_id_type=pl.DeviceIdType.MESH)
    pl.semaphore_wait(barrier, len(peers))


def _row_spec(cols, tm=ROW_TILE):
    return pl.BlockSpec((tm, cols), lambda i: (i, 0))


def _full_spec(shape):
    zeros = (0,) * len(shape)
    return pl.BlockSpec(shape, lambda *_: zeros)


def _vec_spec():
    return _full_spec((1, D_MODEL))


def _column_views(parts):
    return [(a, b) for a in parts for b in range(a.shape[-1] // FFN_WEIGHT_COLS)]


def _column_ranges(views):
    return [(n * FFN_WEIGHT_COLS, (n + 1) * FFN_WEIGHT_COLS) for n in range(len(views))]


class _Gain:
    def __init__(self, stacked, layer):
        self.stacked, self.layer = stacked, layer

    def spec(self):
        layer = self.layer
        return pl.BlockSpec((None, 1, D_MODEL), lambda *_: (layer, 0, 0))


class _LayerRows:
    def __init__(self, stacked, layer):
        self.stacked, self.layer = stacked, layer

    def spec(self):
        layer = self.layer
        return pl.BlockSpec((None, ROW_TILE, self.stacked.shape[-1]), lambda i: (layer, i, 0))


def _in_hbm(a):
    return pltpu.with_memory_space_constraint(a, pltpu.HBM) if a.size >= HBM_PIN_ELEMS else a


def _out_in_hbm(s):
    return pltpu.HBM(s.shape, s.dtype) if s.size >= HBM_PIN_ELEMS else s


def _launch(body, *, name, grid, in_specs, out_specs, out_shape, args, scratch_shapes=(), vmem=VMEM_MID, job=None):
    picked = (_Gain, _LayerRows)
    in_specs = [a.spec() if isinstance(a, picked) else s for s, a in zip(in_specs, args)]
    args = [_in_hbm(a.stacked if isinstance(a, picked) else a) for a in args]
    n_in, n_out, n_scr = len(args), len(out_shape), len(scratch_shapes)
    if job is not None and not job.args:
        job = None
    j_args, j_out, j_scr = ([], [], []) if job is None else ([_in_hbm(a) for a in job.args], job.out_shape, job.scratch)

    def run(*refs):
        groups, at = [], 0
        for n in (n_in, len(j_args), n_out, len(j_out), n_scr, len(j_scr)):
            groups.append(refs[at:at + n])
            at += n
        ins, j_ins, outs, j_outs, scr, j_sems = groups

        def begin():
            _meet(job.peers)
            job.start(j_ins, j_outs, j_sems)

        if job is None:
            body(*ins, *outs, *scr)
        elif not grid:
            begin()
            job.mid(j_ins, j_outs, j_sems)
            job.late(j_ins, j_outs, j_sems)
            body(*ins, *outs, *scr)
            job.finish(j_ins, j_outs, j_sems)
        else:
            ids = [pl.program_id(a) for a in range(len(grid))]
            at_start = lambda step: functools.reduce(jnp.logical_and, [ids[0] == step] + [i == 0 for i in ids[1:]])
            last = functools.reduce(jnp.logical_and, [i == g - 1 for i, g in zip(ids, grid)])
            pl.when(at_start(0))(begin)
            pl.when(at_start(grid[0] // 2))(lambda: job.mid(j_ins, j_outs, j_sems))
            pl.when(at_start(3 * grid[0] // 4))(lambda: job.late(j_ins, j_outs, j_sems))
            body(*ins, *outs, *scr)
            pl.when(last)(lambda: job.finish(j_ins, j_outs, j_sems))

    res = pl.pallas_call(
        run, name=name, grid=grid,
        in_specs=list(in_specs) + [ANY] * len(j_args), out_specs=list(out_specs) + [ANY] * len(j_out),
        out_shape=[_out_in_hbm(s) for s in list(out_shape) + list(j_out)],
        scratch_shapes=list(scratch_shapes) + list(j_scr),
        compiler_params=_cparams(len(grid), vmem, None if job is None else _PEER_SETS.index(job.peers)),
    )(*args, *j_args)
    return res[:n_out], res[n_out:]


def _fwd_pool(x, g_pre, job=None):
    T = x.shape[0]
    tm = ROW_TILE
    nt = T // tm

    def body(x_ref, gpre_ref, d_ref, carry):
        i = pl.program_id(0)

        @pl.when(i == 0)
        def _():
            carry[...] = jnp.zeros_like(carry)

        h = _rms(x_ref[...], gpre_ref[...])
        ext = jnp.concatenate([carry[...], h], axis=0)
        carry[...] = h[tm - POOL_HALO:, :]
        sums = _window_sums(ext, lambda k: k)[POOL_HALO:, :]
        d_ref[...] = (sums / _pool_counts(i * tm, tm) - h).astype(BF16)

    return _launch(
        body, name="fwd_pool", grid=(nt,), in_specs=[_row_spec(D_MODEL), _vec_spec()], out_specs=[_row_spec(D_MODEL)],
        out_shape=[jax.ShapeDtypeStruct((T, D_MODEL), BF16)], scratch_shapes=[pltpu.VMEM((POOL_HALO, D_MODEL), F32)],
        args=(x, g_pre), job=job)


def _fwd_pool_mixer(x, d, wp, scale, g_post, g_ffn, job=None):
    T = x.shape[0]
    nt = T // ROW_TILE

    def body(x_ref, d_ref, wp_ref, sc_ref, gpost_ref, gffn_ref, x1_ref, h2_ref, yraw_ref):
        db = d_ref[...]
        yraw = jnp.concatenate(
            [_dot(db[:, g * POOL_GROUP:(g + 1) * POOL_GROUP], wp_ref[g]) for g in range(N_POOL_GROUPS)], axis=1)
        yraw_ref[...] = yraw.astype(BF16)
        x1 = x_ref[...] + _rms(yraw * sc_ref[...], gpost_ref[...])
        x1_ref[...] = x1
        h2_ref[...] = _rms(x1, gffn_ref[...]).astype(BF16)

    return _launch(
        body, name="fwd_pool_mixer", grid=(nt,),
        in_specs=[_row_spec(D_MODEL), _row_spec(D_MODEL), _full_spec((N_POOL_GROUPS, POOL_GROUP, POOL_GROUP)),
                  _vec_spec(), _vec_spec(), _vec_spec()],
        out_specs=[_row_spec(D_MODEL)] * 3,
        out_shape=[jax.ShapeDtypeStruct((T, D_MODEL), F32)] + [jax.ShapeDtypeStruct((T, D_MODEL), BF16)] * 2,
        args=(x, d, wp, scale, g_post, g_ffn), job=job)


def _fwd_ffn(layer, h2, x1, wgu, wd, g_post, g_ple, job=None):
    T = h2.shape[0]
    tm = min(FFN_ROW_TILE, T)
    nt = T // tm
    sub = tm // FFN_SUB_TILES
    last = FF_CHUNKS - 1
    wgu, wd = _column_views(wgu), _column_views(wd)
    n_gu, n_wd = len(wgu), len(wd)
    gu_cols = _column_ranges(wgu)

    def body(h2_ref, x1_ref, *refs):
        wgu_refs, wd_refs = refs[:n_gu], refs[n_gu:n_gu + n_wd]
        gpost_ref, gple_ref, gs_ref, us_ref, f_ref, x2_ref, h3_ref, acc = refs[n_gu + n_wd:]
        k = pl.program_id(0)
        i = pl.program_id(1)
        rows = pl.ds(pl.multiple_of(i * tm, tm), tm)
        parts = []
        for s in range(FFN_SUB_TILES):
            r = pl.ds(s * sub, sub)
            g = _add_all([_dot_nt(h2_ref[r, c0:c1], w[0]) for (c0, c1), w in zip(gu_cols, wgu_refs)])
            u = _add_all([_dot_nt(h2_ref[r, c0:c1], w[1]) for (c0, c1), w in zip(gu_cols, wgu_refs)])
            gs_ref[r, :] = g.astype(BF16)
            us_ref[r, :] = u.astype(BF16)
            a = (g * _sigmoid(g) * u).astype(BF16)
            parts.append(jnp.concatenate([_dot(a, w[...]) for w in wd_refs], axis=1))
        part = jnp.concatenate(parts, axis=0)

        @pl.when(k == 0)
        def _():
            acc[rows, :] = part

        @pl.when(jnp.logical_and(k > 0, k < last))
        def _():
            acc[rows, :] += part

        @pl.when(k == last)
        def _():
            f = acc[rows, :] + part
            f_ref[...] = f.astype(BF16)
            x2 = x1_ref[...] + _rms(f, gpost_ref[...])
            x2_ref[...] = x2
            h3_ref[...] = _rms(x2, gple_ref[...]).astype(BF16)

    def late(k, i):
        return (jnp.where(k == last, i, 0), 0)

    return _launch(
        body, name=f"fwd_ffn{layer}", grid=(FF_CHUNKS, nt),
        in_specs=[pl.BlockSpec((tm, D_MODEL), lambda k, i: (i, 0)), pl.BlockSpec((tm, D_MODEL), late)]
                 + [pl.BlockSpec((None, 2, FF_BLOCK, FFN_WEIGHT_COLS), lambda k, i, b=b: (k, 0, 0, b)) for _, b in wgu]
                 + [pl.BlockSpec((FF_BLOCK, FFN_WEIGHT_COLS), lambda k, i, b=b: (k, b)) for _, b in wd]
                 + [pl.BlockSpec((1, D_MODEL), lambda k, i: (0, 0))] * 2,
        out_specs=[pl.BlockSpec((None, tm, FF_BLOCK), lambda k, i: (k, i, 0)),
                   pl.BlockSpec((None, tm, FF_BLOCK), lambda k, i: (k, i, 0)),
                   pl.BlockSpec((tm, D_MODEL), late),
                   pl.BlockSpec((tm, D_MODEL), late),
                   pl.BlockSpec((tm, D_MODEL), late)],
        out_shape=[jax.ShapeDtypeStruct((FF_CHUNKS, T, FF_BLOCK), BF16),
                   jax.ShapeDtypeStruct((FF_CHUNKS, T, FF_BLOCK), BF16),
                   jax.ShapeDtypeStruct((T, D_MODEL), BF16),
                   jax.ShapeDtypeStruct((T, D_MODEL), F32),
                   jax.ShapeDtypeStruct((T, D_MODEL), BF16)],
        scratch_shapes=[pltpu.VMEM((T, D_MODEL), F32)],
        args=(h2, x1, *[w for w, _ in wgu], *[w for w, _ in wd], g_post, g_ple), vmem=VMEM_BIG, job=job)


def _fwd_ple_qkv(x2, h3, p, wgate, wproj, g_post, g_kv, g_mix, wkv, wq, job=None):
    T = x2.shape[0]
    nt = T // ROW_TILE

    def body(x2_ref, h3_ref, p_ref, wg_ref, wp_ref, gpost_ref, gkv_ref, gmix_ref, wkv_ref, wq_ref,
             x3_ref, z_ref, pe_ref, hk_ref, h1_ref, q_ref, kv_ref):
        z = _dot(h3_ref[...], wg_ref[...])
        pe = _dot(p_ref[...].astype(BF16), wp_ref[...])
        z_ref[...] = z.astype(BF16)
        pe_ref[...] = pe.astype(BF16)
        x3 = x2_ref[...] + _rms(pe * _sigmoid(z), gpost_ref[...])
        x3_ref[...] = x3
        r = _rstd(x3)
        hk = (x3 * r * gkv_ref[...]).astype(BF16)
        h1 = (x3 * r * gmix_ref[...]).astype(BF16)
        hk_ref[...] = hk
        h1_ref[...] = h1
        kv_ref[...] = _dot(hk, wkv_ref[...]).astype(BF16)
        q_ref[...] = _dot(h1, wq_ref[...]).astype(BF16)

    wide = jax.ShapeDtypeStruct((T, D_MODEL), BF16)
    return _launch(
        body, name="fwd_ple_qkv", grid=(nt,),
        in_specs=[_row_spec(D_MODEL), _row_spec(D_MODEL), _row_spec(PLE_DIM), _full_spec((D_MODEL, D_MODEL)),
                  _full_spec((PLE_DIM, D_MODEL)), _vec_spec(), _vec_spec(), _vec_spec(),
                  _full_spec((D_MODEL, 2 * KV_DIM)), _full_spec((D_MODEL, D_MODEL))],
        out_specs=[_row_spec(D_MODEL)] * 6 + [_row_spec(2 * KV_DIM)],
        out_shape=[jax.ShapeDtypeStruct((T, D_MODEL), F32)] + [wide] * 5 + [jax.ShapeDtypeStruct((T, 2 * KV_DIM), BF16)],
        args=(x2, h3, p, wgate, wproj, g_post, g_kv, g_mix, wkv, wq), job=job)


def _alibi_slope(h):
    return 2.0 ** (-8.0 * (h + 1) / N_HEADS)


ATT_SUB = 32
ATT_GROUP_ROWS = GQA_GROUP * ATT_BLOCK


def _att_mask(n, rel_ref, off_ref):
    qi = lax.broadcasted_iota(jnp.int32, (ATT_BLOCK, 2 * ATT_BLOCK), 0)
    si = lax.broadcasted_iota(jnp.int32, (ATT_BLOCK, 2 * ATT_BLOCK), 1)
    rel = ATT_BLOCK + qi - si
    valid = (rel >= 0) & (rel < ATT_BLOCK) & ((si >= ATT_BLOCK) | (n > 0))
    rel_ref[...] = rel.astype(F32)
    off_ref[...] = jnp.where(valid, 0.0, NEG_INF)


def _att_probs(raw, relf, off, slope, sink):
    s = raw * ATT_SCALE - slope * relf + off
    m = jnp.maximum(jnp.max(s, axis=-1, keepdims=True), sink)
    e = jnp.exp(s - m)
    es = jnp.exp(sink - m)
    inv = 1.0 / (jnp.sum(e, axis=-1, keepdims=True) + es)
    return e * inv, es * inv


def _stack_heads(ref, kh):
    first = kh * GQA_GROUP
    return jnp.concatenate([ref[:, (first + g) * HEAD_DIM:(first + g + 1) * HEAD_DIM] for g in range(GQA_GROUP)], axis=0)


def _unstack_heads(stacked):
    return [stacked[g * ATT_BLOCK:(g + 1) * ATT_BLOCK, :] for g in range(GQA_GROUP)]


def _fwd_attention(q, kpad, vpad, sinks, job=None):
    T = q.shape[0]
    nb = T // ATT_BLOCK

    def body(q_ref, k_ref, v_ref, sink_ref, o_ref, s_scr, p_scr, rel_scr, off_scr):
        n = pl.program_id(0)
        start = pl.multiple_of(n * ATT_BLOCK, ATT_BLOCK)
        kw = k_ref[pl.ds(start, 2 * ATT_BLOCK), :]
        vw = v_ref[pl.ds(start, 2 * ATT_BLOCK), :]
        _att_mask(n, rel_scr, off_scr)
        outs = []
        for kh in range(N_KV_HEADS):
            kk = kw[:, kh * HEAD_DIM:(kh + 1) * HEAD_DIM]
            vv = vw[:, kh * HEAD_DIM:(kh + 1) * HEAD_DIM]
            s_scr[...] = _dot_nt(_stack_heads(q_ref, kh), kk)
            for g in range(GQA_GROUP):
                h = kh * GQA_GROUP + g
                for row0 in range(0, ATT_BLOCK, ATT_SUB):
                    rows, sub = pl.ds(g * ATT_BLOCK + row0, ATT_SUB), pl.ds(row0, ATT_SUB)
                    pr, _ = _att_probs(s_scr[rows, :], rel_scr[sub, :], off_scr[sub, :], _alibi_slope(h),
                                       sink_ref[0, h])
                    p_scr[rows, :] = pr.astype(BF16)
            outs += _unstack_heads(_dot(p_scr[...], vv))
        o_ref[...] = jnp.concatenate(outs, axis=1).astype(BF16)

    return _launch(
        body, name="fwd_attention", grid=(nb,),
        in_specs=[_row_spec(D_MODEL, ATT_BLOCK), _full_spec((T + ATT_BLOCK, KV_DIM)), _full_spec((T + ATT_BLOCK, KV_DIM)),
                  pl.BlockSpec(memory_space=pltpu.SMEM)],
        out_specs=[_row_spec(D_MODEL, ATT_BLOCK)],
        out_shape=[jax.ShapeDtypeStruct((T, D_MODEL), BF16)],
        scratch_shapes=[pltpu.VMEM((ATT_GROUP_ROWS, 2 * ATT_BLOCK), F32), pltpu.VMEM((ATT_GROUP_ROWS, 2 * ATT_BLOCK), BF16)]
                       + [pltpu.VMEM((ATT_BLOCK, 2 * ATT_BLOCK), F32)] * 2,
        args=(q, kpad, vpad, sinks), job=job)


def _fwd_attn_out(attn, x, wo, g_post, g_ffn, job=None):
    T = x.shape[0]
    nt = T // ROW_TILE

    def body(a_ref, x_ref, wo_ref, gpost_ref, gffn_ref, y_ref, x1_ref, h2_ref):
        y = _dot(a_ref[...], wo_ref[...])
        y_ref[...] = y.astype(BF16)
        x1 = x_ref[...] + _rms(y, gpost_ref[...])
        x1_ref[...] = x1
        h2_ref[...] = _rms(x1, gffn_ref[...]).astype(BF16)

    return _launch(
        body, name="fwd_attn_out", grid=(nt,),
        in_specs=[_row_spec(D_MODEL), _row_spec(D_MODEL), _full_spec((D_MODEL, D_MODEL)), _vec_spec(), _vec_spec()],
        out_specs=[_row_spec(D_MODEL)] * 3,
        out_shape=[jax.ShapeDtypeStruct((T, D_MODEL), BF16), jax.ShapeDtypeStruct((T, D_MODEL), F32),
                   jax.ShapeDtypeStruct((T, D_MODEL), BF16)],
        args=(attn, x, wo, g_post, g_ffn), job=job)


def _bwd_ple(layer, dx3, x2, z, pe, h3, p, f, wgate, g_ple_post, g_ple, g_post_ffn, job=None):
    T = x2.shape[0]
    tm = ROW_TILE
    nt = T // tm

    def body(dx3_ref, x2_ref, z_ref, pe_ref, h3_ref, p_ref, f_ref, wg_ref, gpp_ref, gp_ref, gpf_ref,
             dx2_ref, df_ref, dwg_ref, dwp_ref, dgpp_ref, dgp_ref, dgpf_ref, acc_g, acc_p):
        i = pl.program_id(0)
        first = i == 0
        dx3v = dx3_ref[...]
        gate = _sigmoid(z_ref[...].astype(F32))
        pev = pe_ref[...].astype(F32)
        de, dgpp = _rms_bwd(pev * gate, gpp_ref[...], dx3v)
        dpe = (de * gate).astype(BF16)
        dz = (de * pev * gate * (1.0 - gate)).astype(BF16)
        _acc(acc_p, _dot_tn(p_ref[...].astype(BF16), dpe), first)
        _acc(acc_g, _dot_tn(h3_ref[...], dz), first)
        dh3 = _dot_nt(dz, wg_ref[...])
        dxn, dgp = _rms_bwd(x2_ref[...], gp_ref[...], dh3)
        dx2 = dx3v + dxn
        dx2_ref[...] = dx2
        df, dgpf = _rms_bwd(f_ref[...].astype(F32), gpf_ref[...], dx2)
        df_ref[...] = df.astype(BF16)
        _acc(dgpp_ref, dgpp, first)
        _acc(dgp_ref, dgp, first)
        _acc(dgpf_ref, dgpf, first)

        @pl.when(i == nt - 1)
        def _():
            dwg_ref[...] = acc_g[...].astype(BF16)
            dwp_ref[...] = acc_p[...].astype(BF16)

    return _launch(
        body, name=f"bwd_ple{layer}", grid=(nt,),
        in_specs=[_row_spec(D_MODEL)] * 5 + [_row_spec(PLE_DIM), _row_spec(D_MODEL), _full_spec((D_MODEL, D_MODEL)),
                  _vec_spec(), _vec_spec(), _vec_spec()],
        out_specs=[_row_spec(D_MODEL), _row_spec(D_MODEL), _full_spec((D_MODEL, D_MODEL)), _full_spec((PLE_DIM, D_MODEL)),
                   _vec_spec(), _vec_spec(), _vec_spec()],
        out_shape=[jax.ShapeDtypeStruct((T, D_MODEL), F32), jax.ShapeDtypeStruct((T, D_MODEL), BF16),
                   jax.ShapeDtypeStruct((D_MODEL, D_MODEL), BF16), jax.ShapeDtypeStruct((PLE_DIM, D_MODEL), BF16)]
                  + [jax.ShapeDtypeStruct((1, D_MODEL), F32)] * 3,
        scratch_shapes=[pltpu.VMEM((D_MODEL, D_MODEL), F32), pltpu.VMEM((PLE_DIM, D_MODEL), F32)],
        args=(dx3, x2, z, pe, h3, p, f, wgate, g_ple_post, g_ple, g_post_ffn), vmem=VMEM_BIG, job=job)


def _ple_loss_bwd(layer, x2, h3, p, f, target, wgate, wproj, g_ple_post, g_ple, g_post_ffn, job=None):
    T = x2.shape[0]
    tm = ROW_TILE
    nt = T // tm

    def body(x2_ref, h3_ref, p_ref, f_ref, tgt_ref, wg_ref, wp_ref, gpp_ref, gp_ref, gpf_ref,
             dx2_ref, df_ref, dwg_ref, dwp_ref, dgpp_ref, dgp_ref, dgpf_ref, loss_ref, acc_g, acc_p):
        i = pl.program_id(0)
        first = i == 0
        h3 = h3_ref[...]
        pb = p_ref[...].astype(BF16)
        x2v = x2_ref[...]
        gate = _sigmoid(_dot(h3, wg_ref[...]))
        pev = _dot(pb, wp_ref[...])
        e = pev * gate
        err = x2v + _rms(e, gpp_ref[...]) - tgt_ref[...]
        _acc(loss_ref, 0.5 * jnp.sum(jnp.mean(err * err, axis=-1, keepdims=True), axis=0, keepdims=True), first)
        dx3v = err * (1.0 / D_MODEL)
        de, dgpp = _rms_bwd(e, gpp_ref[...], dx3v)
        dpe = (de * gate).astype(BF16)
        dz = (de * pev * gate * (1.0 - gate)).astype(BF16)
        _acc(acc_p, _dot_tn(pb, dpe), first)
        _acc(acc_g, _dot_tn(h3, dz), first)
        dxn, dgp = _rms_bwd(x2v, gp_ref[...], _dot_nt(dz, wg_ref[...]))
        dx2 = dx3v + dxn
        dx2_ref[...] = dx2
        df, dgpf = _rms_bwd(f_ref[...].astype(F32), gpf_ref[...], dx2)
        df_ref[...] = df.astype(BF16)
        _acc(dgpp_ref, dgpp, first)
        _acc(dgp_ref, dgp, first)
        _acc(dgpf_ref, dgpf, first)

        @pl.when(i == nt - 1)
        def _():
            dwg_ref[...] = acc_g[...].astype(BF16)
            dwp_ref[...] = acc_p[...].astype(BF16)

    return _launch(
        body, name=f"ple_loss_bwd{layer}", grid=(nt,),
        in_specs=[_row_spec(D_MODEL), _row_spec(D_MODEL), _row_spec(PLE_DIM), _row_spec(D_MODEL), _row_spec(D_MODEL),
                  _full_spec((D_MODEL, D_MODEL)), _full_spec((PLE_DIM, D_MODEL)), _vec_spec(), _vec_spec(), _vec_spec()],
        out_specs=[_row_spec(D_MODEL), _row_spec(D_MODEL), _full_spec((D_MODEL, D_MODEL)), _full_spec((PLE_DIM, D_MODEL)),
                   _vec_spec(), _vec_spec(), _vec_spec(), _full_spec((1, 1))],
        out_shape=[jax.ShapeDtypeStruct((T, D_MODEL), F32), jax.ShapeDtypeStruct((T, D_MODEL), BF16),
                   jax.ShapeDtypeStruct((D_MODEL, D_MODEL), BF16), jax.ShapeDtypeStruct((PLE_DIM, D_MODEL), BF16)]
                  + [jax.ShapeDtypeStruct((1, D_MODEL), F32)] * 3 + [jax.ShapeDtypeStruct((1, 1), F32)],
        scratch_shapes=[pltpu.VMEM((D_MODEL, D_MODEL), F32), pltpu.VMEM((PLE_DIM, D_MODEL), F32)],
        args=(x2, h3, p, f, target, wgate, wproj, g_ple_post, g_ple, g_post_ffn), vmem=VMEM_BIG, job=job)


def _bwd_ffn_act(layer, df, gs, us, wgu, wd, job=None):
    T = df.shape[0]
    tm = min(FFN_ROW_TILE, T)
    nt = T // tm
    sub = tm // FFN_SUB_TILES
    last = FF_CHUNKS - 1
    wgu, wd = _column_views(wgu), _column_views(wd)
    n_gu, n_wd = len(wgu), len(wd)
    wd_cols = _column_ranges(wd)

    def body(df_ref, gs_ref, us_ref, *refs):
        wgu_refs, wd_refs = refs[:n_gu], refs[n_gu:n_gu + n_wd]
        dh_ref, dg_ref, du_ref, a_ref, acc_h = refs[n_gu + n_wd:]
        k = pl.program_id(0)
        i = pl.program_id(1)
        rows = pl.ds(pl.multiple_of(i * tm, tm), tm)
        dhs = []
        for s in range(FFN_SUB_TILES):
            r = pl.ds(s * sub, sub)
            g = gs_ref[r, :].astype(F32)
            u = us_ref[r, :].astype(F32)
            sg = _sigmoid(g)
            silu = g * sg
            a_ref[r, :] = (silu * u).astype(BF16)
            da = _add_all([_dot_nt(df_ref[r, c0:c1], w[...]) for (c0, c1), w in zip(wd_cols, wd_refs)])
            dg = (da * u * (sg * (1.0 + g * (1.0 - sg)))).astype(BF16)
            du = (da * silu).astype(BF16)
            dg_ref[r, :] = dg
            du_ref[r, :] = du
            dhs.append(jnp.concatenate([_dot(dg, w[0]) + _dot(du, w[1]) for w in wgu_refs], axis=1))
        dh = jnp.concatenate(dhs, axis=0)

        @pl.when(k == 0)
        def _():
            acc_h[rows, :] = dh

        @pl.when(jnp.logical_and(k > 0, k < last))
        def _():
            acc_h[rows, :] += dh

        @pl.when(k == last)
        def _():
            dh_ref[...] = acc_h[rows, :] + dh

    chunk_rows = pl.BlockSpec((None, tm, FF_BLOCK), lambda k, i: (k, i, 0))
    saved = jax.ShapeDtypeStruct((FF_CHUNKS, T, FF_BLOCK), BF16)
    return _launch(
        body, name=f"bwd_ffn_act{layer}", grid=(FF_CHUNKS, nt),
        in_specs=[pl.BlockSpec((tm, D_MODEL), lambda k, i: (i, 0)), chunk_rows, chunk_rows]
                 + [pl.BlockSpec((None, 2, FF_BLOCK, FFN_WEIGHT_COLS), lambda k, i, b=b: (k, 0, 0, b)) for _, b in wgu]
                 + [pl.BlockSpec((FF_BLOCK, FFN_WEIGHT_COLS), lambda k, i, b=b: (k, b)) for _, b in wd],
        out_specs=[pl.BlockSpec((tm, D_MODEL), lambda k, i: (jnp.where(k == last, i, 0), 0)),
                   chunk_rows, chunk_rows, chunk_rows],
        out_shape=[jax.ShapeDtypeStruct((T, D_MODEL), F32), saved, saved, saved],
        scratch_shapes=[pltpu.VMEM((T, D_MODEL), F32)],
        args=(df, gs, us, *[w for w, _ in wgu], *[w for w, _ in wd]), vmem=VMEM_BIG, job=job)


def _bwd_ffn_dw(layer, q, parts, h2, df, dg, du, a, job=None):
    T = h2.shape[0]
    width = D_MODEL // parts

    def body(h_ref, df_ref, dg_ref, du_ref, a_ref, dgu_ref, dwd_ref):
        h = h_ref[...]
        dgu_ref[0] = _dot_tn(dg_ref[...], h).astype(BF16)
        dgu_ref[1] = _dot_tn(du_ref[...], h).astype(BF16)
        dwd_ref[...] = _dot_tn(a_ref[...], df_ref[...]).astype(BF16)

    cols = pl.BlockSpec((T, width), lambda k: (0, q))
    chunk = pl.BlockSpec((None, T, FF_BLOCK), lambda k: (k, 0, 0))
    return _launch(
        body, name=f"bwd_ffn_dw{layer}_{q}", grid=(FF_CHUNKS,),
        in_specs=[cols, cols, chunk, chunk, chunk],
        out_specs=[pl.BlockSpec((None, 2, FF_BLOCK, width), lambda k: (k, 0, 0, 0)),
                   pl.BlockSpec((FF_BLOCK, width), lambda k: (k, 0))],
        out_shape=[jax.ShapeDtypeStruct((FF_CHUNKS, 2, FF_BLOCK, width), BF16),
                   jax.ShapeDtypeStruct((D_FF, width), BF16)],
        args=(h2, df, dg, du, a), vmem=VMEM_BIG, job=job)


def _bwd_attn_out(dx2, dh2, x1, y, attn, wo, g_ffn, g_post, job=None):
    T = x1.shape[0]
    nt = T // ROW_TILE

    def body(dx2_ref, dh2_ref, x1_ref, y_ref, a_ref, wo_ref, gffn_ref, gpost_ref,
             dx1_ref, da_ref, dwo_ref, dgf_ref, dgp_ref, acc):
        i = pl.program_id(0)
        first = i == 0
        dxn, dgf = _rms_bwd(x1_ref[...], gffn_ref[...], dh2_ref[...])
        dx1 = dx2_ref[...] + dxn
        dx1_ref[...] = dx1
        dy, dgp = _rms_bwd(y_ref[...].astype(F32), gpost_ref[...], dx1)
        dyb = dy.astype(BF16)
        da_ref[...] = _dot_nt(dyb, wo_ref[...]).astype(BF16)
        _acc(acc, _dot_tn(a_ref[...], dyb), first)
        _acc(dgf_ref, dgf, first)
        _acc(dgp_ref, dgp, first)

        @pl.when(i == nt - 1)
        def _():
            dwo_ref[...] = acc[...].astype(BF16)

    return _launch(
        body, name="bwd_attn_out", grid=(nt,),
        in_specs=[_row_spec(D_MODEL)] * 5 + [_full_spec((D_MODEL, D_MODEL)), _vec_spec(), _vec_spec()],
        out_specs=[_row_spec(D_MODEL), _row_spec(D_MODEL), _full_spec((D_MODEL, D_MODEL)), _vec_spec(), _vec_spec()],
        out_shape=[jax.ShapeDtypeStruct((T, D_MODEL), F32), jax.ShapeDtypeStruct((T, D_MODEL), BF16),
                   jax.ShapeDtypeStruct((D_MODEL, D_MODEL), BF16)] + [jax.ShapeDtypeStruct((1, D_MODEL), F32)] * 2,
        scratch_shapes=[pltpu.VMEM((D_MODEL, D_MODEL), F32)],
        args=(dx2, dh2, x1, y, attn, wo, g_ffn, g_post), job=job)


def _bwd_attention(q, dattn, kpad, vpad, sinks, job=None):
    T = q.shape[0]
    nb = T // ATT_BLOCK

    def body(q_ref, do_ref, k_ref, v_ref, sink_ref, dq_ref, dk_ref, dv_ref, ds_ref, s_scr, dp_scr, p_scr, dsb_scr,
             rel_scr, off_scr):
        n = pl.program_id(0)
        _att_mask(n, rel_scr, off_scr)

        @pl.when(n == 0)
        def _():
            dk_ref[...] = jnp.zeros_like(dk_ref)
            dv_ref[...] = jnp.zeros_like(dv_ref)
            ds_ref[...] = jnp.zeros_like(ds_ref)

        start = pl.multiple_of(n * ATT_BLOCK, ATT_BLOCK)
        win = pl.ds(start, 2 * ATT_BLOCK)
        kw = k_ref[win, :]
        vw = v_ref[win, :]
        lane = lax.broadcasted_iota(jnp.int32, (1, ATT_BLOCK), 1)
        dsink = jnp.zeros((1, ATT_BLOCK), F32)
        dqs, dks, dvs = [], [], []
        for kh in range(N_KV_HEADS):
            kk = kw[:, kh * HEAD_DIM:(kh + 1) * HEAD_DIM]
            vv = vw[:, kh * HEAD_DIM:(kh + 1) * HEAD_DIM]
            qs = _stack_heads(q_ref, kh)
            dos = _stack_heads(do_ref, kh)
            s_scr[...] = _dot_nt(qs, kk)
            dp_scr[...] = _dot_nt(dos, vv)
            for g in range(GQA_GROUP):
                h = kh * GQA_GROUP + g
                dsink_h = jnp.zeros((1, 1), F32)
                for row0 in range(0, ATT_BLOCK, ATT_SUB):
                    rows, sub = pl.ds(g * ATT_BLOCK + row0, ATT_SUB), pl.ds(row0, ATT_SUB)
                    pr, ps = _att_probs(s_scr[rows, :], rel_scr[sub, :], off_scr[sub, :], _alibi_slope(h),
                                        sink_ref[0, h])
                    dp = dp_scr[rows, :]
                    delta = jnp.sum(pr * dp, axis=-1, keepdims=True)
                    dsb_scr[rows, :] = (pr * (dp - delta) * ATT_SCALE).astype(BF16)
                    p_scr[rows, :] = pr.astype(BF16)
                    dsink_h = dsink_h - jnp.sum(ps * delta, axis=0, keepdims=True)
                dsink = dsink + jnp.where(lane == h, dsink_h, 0.0)
            dsb = dsb_scr[...]
            dqs += _unstack_heads(_dot(dsb, kk))
            dks.append(_dot_tn(dsb, qs))
            dvs.append(_dot_tn(p_scr[...], dos))
        dq_ref[...] = jnp.concatenate(dqs, axis=1).astype(BF16)
        dk_ref[win, :] += jnp.concatenate(dks, axis=1)
        dv_ref[win, :] += jnp.concatenate(dvs, axis=1)
        ds_ref[...] += dsink

    return _launch(
        body, name="bwd_attention", grid=(nb,),
        in_specs=[_row_spec(D_MODEL, ATT_BLOCK), _row_spec(D_MODEL, ATT_BLOCK), _full_spec((T + ATT_BLOCK, KV_DIM)),
                  _full_spec((T + ATT_BLOCK, KV_DIM)), pl.BlockSpec(memory_space=pltpu.SMEM)],
        out_specs=[_row_spec(D_MODEL, ATT_BLOCK), _full_spec((T + ATT_BLOCK, KV_DIM)), _full_spec((T + ATT_BLOCK, KV_DIM)),
                   _full_spec((1, ATT_BLOCK))],
        out_shape=[jax.ShapeDtypeStruct((T, D_MODEL), BF16), jax.ShapeDtypeStruct((T + ATT_BLOCK, KV_DIM), F32),
                   jax.ShapeDtypeStruct((T + ATT_BLOCK, KV_DIM), F32), jax.ShapeDtypeStruct((1, ATT_BLOCK), F32)],
        scratch_shapes=[pltpu.VMEM((ATT_GROUP_ROWS, 2 * ATT_BLOCK), F32)] * 2
                       + [pltpu.VMEM((ATT_GROUP_ROWS, 2 * ATT_BLOCK), BF16)] * 2
                       + [pltpu.VMEM((ATT_BLOCK, 2 * ATT_BLOCK), F32)] * 2,
        args=(q, dattn, kpad, vpad, sinks), vmem=VMEM_BIG, job=job)


def _bwd_qkv(dxres, dq, dkv, x3, h1, hk, wq, wkv, g_mix, g_kv, job=None):
    T = x3.shape[0]
    nt = T // ROW_TILE

    def body(dxr_ref, dq_ref, dkv_ref, x_ref, h1_ref, hk_ref, wq_ref, wkv_ref, gmix_ref, gkv_ref,
             dx_ref, dwq_ref, dwkv_ref, dgm_ref, dgk_ref, acc_q, acc_kv):
        i = pl.program_id(0)
        first = i == 0
        dqv = dq_ref[...]
        dkvv = dkv_ref[...]
        xv = x_ref[...]
        d1, dgm = _rms_bwd(xv, gmix_ref[...], _dot_nt(dqv, wq_ref[...]))
        d2, dgk = _rms_bwd(xv, gkv_ref[...], _dot_nt(dkvv, wkv_ref[...]))
        dx_ref[...] = dxr_ref[...] + d1 + d2
        _acc(acc_q, _dot_tn(h1_ref[...], dqv), first)
        _acc(acc_kv, _dot_tn(hk_ref[...], dkvv), first)
        _acc(dgm_ref, dgm, first)
        _acc(dgk_ref, dgk, first)

        @pl.when(i == nt - 1)
        def _():
            dwq_ref[...] = acc_q[...].astype(BF16)
            dwkv_ref[...] = acc_kv[...].astype(BF16)

    return _launch(
        body, name="bwd_qkv", grid=(nt,),
        in_specs=[_row_spec(D_MODEL), _row_spec(D_MODEL), _row_spec(2 * KV_DIM), _row_spec(D_MODEL), _row_spec(D_MODEL),
                  _row_spec(D_MODEL), _full_spec((D_MODEL, D_MODEL)), _full_spec((D_MODEL, 2 * KV_DIM)), _vec_spec(),
                  _vec_spec()],
        out_specs=[_row_spec(D_MODEL), _full_spec((D_MODEL, D_MODEL)), _full_spec((D_MODEL, 2 * KV_DIM)), _vec_spec(),
                   _vec_spec()],
        out_shape=[jax.ShapeDtypeStruct((T, D_MODEL), F32), jax.ShapeDtypeStruct((D_MODEL, D_MODEL), BF16),
                   jax.ShapeDtypeStruct((D_MODEL, 2 * KV_DIM), BF16)] + [jax.ShapeDtypeStruct((1, D_MODEL), F32)] * 2,
        scratch_shapes=[pltpu.VMEM((D_MODEL, D_MODEL), F32), pltpu.VMEM((D_MODEL, 2 * KV_DIM), F32)],
        args=(dxres, dq, dkv, x3, h1, hk, wq, wkv, g_mix, g_kv), job=job)


def _bwd_pool_mixer(dx2, dh2, x1, x, yraw, d, wp, scale, g_ffn, g_post, g_pre, job=None):
    T = x.shape[0]
    tm = ROW_TILE
    nt = T // tm

    def body(dx2_ref, dh2_ref, x1_ref, x_ref, yraw_ref, d_ref, wp_ref, sc_ref, gffn_ref, gpost_ref, gpre_ref,
             dx_ref, dwp_ref, dsc_ref, dgf_ref, dgp_ref, dgm_ref, carry, acc):
        i = pl.program_id(0)
        first = i == 0
        tile = nt - 1 - i

        @pl.when(first)
        def _():
            carry[...] = jnp.zeros_like(carry)

        dxn, dgf = _rms_bwd(x1_ref[...], gffn_ref[...], dh2_ref[...])
        dx1 = dx2_ref[...] + dxn
        yraw = yraw_ref[...].astype(F32)
        sc = sc_ref[...]
        dy, dgp = _rms_bwd(yraw * sc, gpost_ref[...], dx1)
        dsc = jnp.sum(dy * yraw, axis=0, keepdims=True)
        dyb = (dy * sc).astype(BF16)
        dv = d_ref[...]
        dds = []
        for g in range(N_POOL_GROUPS):
            cols = slice(g * POOL_GROUP, (g + 1) * POOL_GROUP)
            dds.append(_dot_nt(dyb[:, cols], wp_ref[g]))
            _acc(acc.at[g], _dot_tn(dv[:, cols], dyb[:, cols]), first)
        dd = jnp.concatenate(dds, axis=1)
        e = dd / _pool_counts(tile * tm, tm)
        ext = jnp.concatenate([e, carry[...]], axis=0)
        carry[...] = e[:POOL_HALO, :]
        sums = _window_sums(ext, lambda k: tm + POOL_HALO - k)[:tm, :]
        dxm, dgm = _rms_bwd(x_ref[...], gpre_ref[...], sums - dd)
        dx_ref[...] = dx1 + dxm
        _acc(dsc_ref, dsc, first)
        _acc(dgf_ref, dgf, first)
        _acc(dgp_ref, dgp, first)
        _acc(dgm_ref, dgm, first)

        @pl.when(i == nt - 1)
        def _():
            dwp_ref[...] = acc[...].astype(BF16)

    rev = pl.BlockSpec((tm, D_MODEL), lambda i: (nt - 1 - i, 0))
    return _launch(
        body, name="bwd_pool_mixer", grid=(nt,),
        in_specs=[rev] * 6 + [_full_spec((N_POOL_GROUPS, POOL_GROUP, POOL_GROUP))] + [_vec_spec()] * 4,
        out_specs=[rev, _full_spec((N_POOL_GROUPS, POOL_GROUP, POOL_GROUP))] + [_vec_spec()] * 4,
        out_shape=[jax.ShapeDtypeStruct((T, D_MODEL), F32),
                   jax.ShapeDtypeStruct((N_POOL_GROUPS, POOL_GROUP, POOL_GROUP), BF16)]
                  + [jax.ShapeDtypeStruct((1, D_MODEL), F32)] * 4,
        scratch_shapes=[pltpu.VMEM((POOL_HALO, D_MODEL), F32), pltpu.VMEM((N_POOL_GROUPS, POOL_GROUP, POOL_GROUP), F32)],
        args=(dx2, dh2, x1, x, yraw, d, wp, scale, g_ffn, g_post, g_pre), job=job)


def _my_place():
    return lax.axis_index("x"), lax.axis_index("y"), lax.axis_index("c")


def _dev_index(px, py, pc):
    return 4 * px + 2 * py + pc


def _peer_by_relation(r):
    x, y, c = _my_place()
    return (x ^ ((r >> 2) & 1), y ^ ((r >> 1) & 1), c ^ (r & 1))


def _slot_pool(ref, j):
    return ref.at[:, pl.ds(pl.multiple_of(j * 32, 32), 32), :]


def _slot_scale(ref, j):
    return ref.at[:, pl.ds(pl.multiple_of(j * 128, 128), 128)]


def _slot_rows128(ref, j):
    return ref.at[pl.ds(pl.multiple_of(j * 128, 128), 128), :]


def _slot_gu(ref, j):
    return ref.at[j % FF_CHUNKS, j // FF_CHUNKS]


def _slot_wd(ref, j):
    return ref.at[pl.ds(pl.multiple_of(j * WD_ROWS, 16), WD_ROWS), :]


def _slot_cols128(ref, j):
    return ref.at[:, pl.ds(pl.multiple_of(j * 128, 128), 128)]


_GATHERED = {
    "pool": ((N_POOL_GROUPS, POOL_GROUP, POOL_GROUP), BF16, _slot_pool),
    "scale": ((1, D_MODEL), F32, _slot_scale),
    "kv": ((D_MODEL, 2 * KV_DIM), BF16, _slot_rows128),
    "q": ((D_MODEL, D_MODEL), BF16, _slot_rows128),
    "o": ((D_MODEL, D_MODEL), BF16, _slot_rows128),
    "gu": ((FF_CHUNKS, 2, FF_BLOCK, D_MODEL), BF16, _slot_gu),
    "wd": ((D_FF, D_MODEL), BF16, _slot_wd),
    "guh": ((FF_CHUNKS, 2, FF_BLOCK, D_MODEL // 2), BF16, _slot_gu),
    "wdh": ((D_FF, D_MODEL // 2), BF16, _slot_wd),
    "gate": ((D_MODEL, D_MODEL), BF16, _slot_rows128),
    "proj": ((PLE_DIM, D_MODEL), BF16, _slot_cols128),
}


def _no_compute():
    pass


class _AllGather:
    peers = ("sibling", "x", "y")

    def __init__(self, names, shards):
        self.kinds = [_GATHERED[n.rstrip("01_")] for n in names]
        entries = [shards[n] if isinstance(shards[n], tuple) else (shards[n], None) for n in names]
        self.args = [array for array, _ in entries]
        self.columns = [columns for _, columns in entries]
        self.out_shape = [jax.ShapeDtypeStruct(shape, dtype) for shape, dtype, _ in self.kinds]
        n = len(names)
        self.scratch = [pltpu.SemaphoreType.DMA((n, 7)), pltpu.SemaphoreType.DMA((n, 7)), pltpu.SemaphoreType.DMA((n,))]

    def _plan(self, srcs, outs, sems):
        send_sems, recv_sems, local_sems = sems
        x, y, c = _my_place()

        def slot(t, dev):
            return self.kinds[t][2](outs[t], _dev_index(*dev))

        def copy(t, k, block, to, src=None):
            return pltpu.make_async_remote_copy(
                src_ref=slot(t, block) if src is None else src, dst_ref=slot(t, block),
                send_sem=send_sems.at[t, k], recv_sem=recv_sems.at[t, k], device_id=to, device_id_type=MESH)

        return types.SimpleNamespace(
            copy=copy, core=c, me=(x, y, c), sibling=(x, y, 1 - c),
            x_chip=(1 - x, y), y_chip=(x, 1 - y), far_chip=(1 - x, 1 - y),
            via=(x ^ (1 - c), y ^ c),
            onto=(x ^ c, y ^ (1 - c)),
            k_via=1 + c, k_onto=2 - c,
            local=[pltpu.make_async_copy(self._shard(srcs, t), slot(t, (x, y, c)), local_sems.at[t])
                   for t in range(len(srcs))])

    def _shard(self, srcs, t):
        if self.columns[t] is None:
            return srcs[t]
        first, end = self.columns[t]
        return srcs[t].at[:, first:end]

    def start(self, srcs, outs, sems):
        p = self._plan(srcs, outs, sems)
        for cp in p.local:
            cp.start()
        for t in range(len(srcs)):
            shard = self._shard(srcs, t)
            p.copy(t, 0, p.me, p.sibling, src=shard).start()
            p.copy(t, 1, p.me, (*p.x_chip, p.core), src=shard).start()
            p.copy(t, 2, p.me, (*p.y_chip, p.core), src=shard).start()

    def mid(self, srcs, outs, sems):
        p = self._plan(srcs, outs, sems)
        for t in range(len(srcs)):
            block = (*p.via, p.core)
            p.copy(t, p.k_via, block, p.me).wait_recv()
            p.copy(t, 3, block, (*p.onto, p.core)).start()
            p.copy(t, 3 + p.k_via, block, p.sibling).start()

    def late(self, srcs, outs, sems):
        p = self._plan(srcs, outs, sems)
        n = len(srcs)
        for t in range(n):
            block = (*p.onto, p.core)
            p.copy(t, p.k_onto, block, p.me).wait_recv()
            p.copy(t, 3 + p.k_onto, block, p.sibling).start()
        for t in range(n):
            block = (*p.far_chip, p.core)
            p.copy(t, 3, block, p.me).wait_recv()
            p.copy(t, 6, block, p.sibling).start()

    def finish(self, srcs, outs, sems):
        p = self._plan(srcs, outs, sems)
        n = len(srcs)
        other = 1 - p.core
        for t in range(n):
            p.copy(t, 0, (*p.me[:2], other), p.me).wait_recv()
            for k, chip in ((4, p.x_chip), (5, p.y_chip), (6, p.far_chip)):
                p.copy(t, k, (*chip, other), p.me).wait_recv()
            for k in range(7):
                p.copy(t, k, p.me, p.sibling).wait_send()
        for cp in p.local:
            cp.wait()


def _jobs_only(name, job=None):
    return _launch(_no_compute, name=name, grid=(), in_specs=[], out_specs=[], out_shape=[], args=(), job=job)


def _block_pool(ref, j):
    return ref.at[:, pl.ds(pl.multiple_of(j * 32, 32), 32), :]


def _block_rows128(ref, j):
    return ref.at[pl.ds(pl.multiple_of(j * 128, 128), 128), :]


def _block_gu(ref, j):
    return ref.at[j % FF_CHUNKS, j // FF_CHUNKS]


def _block_wd(ref, j):
    return ref.at[pl.ds(pl.multiple_of(j * WD_ROWS, 16), WD_ROWS), :]


def _block_cols128(ref, j):
    return ref.at[:, pl.ds(pl.multiple_of(j * 128, 128), 128)]


_SCATTERED = {
    "pool": ((N_POOL_GROUPS, 32, POOL_GROUP), _block_pool),
    "kv": ((128, 2 * KV_DIM), _block_rows128),
    "q": ((128, D_MODEL), _block_rows128),
    "o": ((128, D_MODEL), _block_rows128),
    "gu": ((FF_BLOCK, FF_PART), _block_gu),
    "wd": ((WD_ROWS, FF_PART), _block_wd),
    "guA": ((FF_BLOCK, FF_PART), lambda ref, j: _block_gu(ref, j).at[:, :FF_PART]),
    "guB": ((FF_BLOCK, FF_PART), lambda ref, j: _block_gu(ref, j).at[:, FF_PART:]),
    "wdA": ((WD_ROWS, FF_PART), lambda ref, j: _block_wd(ref, j).at[:, :FF_PART]),
    "wdB": ((WD_ROWS, FF_PART), lambda ref, j: _block_wd(ref, j).at[:, FF_PART:]),
    "gate": ((128, D_MODEL), _block_rows128),
    "proj": ((PLE_DIM, 128), _block_cols128),
}


class _SiblingSwap:
    peers = ("sibling",)

    def __init__(self, pieces):
        self.kinds = [_SCATTERED[kind] for kind, _ in pieces]
        self.args = [g for _, g in pieces]
        self.out_shape = [jax.ShapeDtypeStruct((N_CHIPS, *block), BF16) for block, _ in self.kinds]
        n = len(pieces)
        self.scratch = [pltpu.SemaphoreType.DMA((n, N_CHIPS)), pltpu.SemaphoreType.DMA((n, N_CHIPS))]

    def _copies(self, srcs, outs, sems):
        send_sems, recv_sems = sems
        x, y, c = _my_place()
        return [pltpu.make_async_remote_copy(
            src_ref=block(srcs[t], 2 * ch + 1 - c), dst_ref=outs[t].at[ch], send_sem=send_sems.at[t, ch],
            recv_sem=recv_sems.at[t, ch], device_id=(x, y, 1 - c), device_id_type=MESH)
            for t, (_, block) in enumerate(self.kinds) for ch in range(N_CHIPS)]

    def start(self, srcs, outs, sems):
        for cp in self._copies(srcs, outs, sems):
            cp.start()

    def finish(self, srcs, outs, sems):
        for cp in self._copies(srcs, outs, sems):
            cp.wait()


class _ChipScatter:
    N_BUFS = 4
    peers = ("x", "y")

    def __init__(self, pieces):
        self.kinds = [_SCATTERED[kind] for kind, _, _ in pieces]
        self.n = n = len(pieces)
        self.args = [g for _, g, _ in pieces] + [s for _, _, s in pieces]
        self.out_shape = [jax.ShapeDtypeStruct((2, *block), BF16) for block, _ in self.kinds]
        self.scratch = []
        for block, _ in self.kinds:
            self.scratch += [pltpu.VMEM((N_CHIPS, *block), BF16)] * 3 + [pltpu.VMEM((2, *block), BF16)]
        dma = pltpu.SemaphoreType.DMA
        self.scratch += [dma((n, N_CHIPS + 1)), dma((n, 2)), dma((n, 2)), dma((n,)), dma((n,)), dma((n,))]

    def _plan(self, outs, scr):
        n = self.n
        first_send, first_recv, second_send, second_recv, keep_sems = scr[self.N_BUFS * n + 1:]
        x, y, c = _my_place()
        via = (x ^ (1 - c), y ^ c)
        onto = (x ^ c, y ^ (1 - c))
        index = lambda chip: 2 * chip[0] + chip[1]
        first, second, keep = [], [], []
        for t in range(n):
            total, inbox = scr[self.N_BUFS * t + 2], scr[self.N_BUFS * t + 3]
            for k, chip in enumerate((via, (1 - x, 1 - y))):
                first.append(pltpu.make_async_remote_copy(
                    src_ref=total.at[index(chip)], dst_ref=inbox.at[k], send_sem=first_send.at[t, k],
                    recv_sem=first_recv.at[t, k], device_id=(*via, c), device_id_type=MESH))
            second.append(pltpu.make_async_remote_copy(
                src_ref=total.at[index(onto)], dst_ref=outs[t].at[1], send_sem=second_send.at[t],
                recv_sem=second_recv.at[t], device_id=(*onto, c), device_id_type=MESH))
            keep.append(pltpu.make_async_copy(total.at[index((x, y))], outs[t].at[0], keep_sems.at[t]))
        return first, second, keep, index((x, y)), index(onto)

    def start(self, ins, outs, scr):
        n = self.n
        load_sems = scr[self.N_BUFS * n]
        c = lax.axis_index("c")
        loads = []
        for t, (_, block) in enumerate(self.kinds):
            mine, theirs = scr[self.N_BUFS * t], scr[self.N_BUFS * t + 1]
            loads += [pltpu.make_async_copy(block(ins[t], 2 * ch + c), mine.at[ch], load_sems.at[t, ch])
                      for ch in range(N_CHIPS)]
            loads.append(pltpu.make_async_copy(ins[n + t], theirs, load_sems.at[t, N_CHIPS]))
        for cp in loads:
            cp.start()
        for cp in loads:
            cp.wait()
        for t in range(n):
            mine, theirs, total = scr[self.N_BUFS * t:self.N_BUFS * t + 3]
            for ch in range(N_CHIPS):
                total[ch] = (mine[ch].astype(F32) + theirs[ch].astype(F32)).astype(BF16)
        for cp in self._plan(outs, scr)[0]:
            cp.start()

    def mid(self, ins, outs, scr):
        first, second, keep, me, onto = self._plan(outs, scr)
        for cp in first:
            cp.wait_recv()
        for t in range(self.n):
            total, inbox = scr[self.N_BUFS * t + 2], scr[self.N_BUFS * t + 3]
            for k, slot in enumerate((me, onto)):
                total[slot] = (total[slot].astype(F32) + inbox[k].astype(F32)).astype(BF16)
        for cp in second + keep:
            cp.start()

    def finish(self, ins, outs, scr):
        first, second, keep, _, _ = self._plan(outs, scr)
        for cp in first:
            cp.wait_send()
        for cp in second + keep:
            cp.wait()


class _ToEveryone:
    peers = _EVERYONE

    def __init__(self, scattered=(), gathered=()):
        self.blocks = [_SCATTERED[kind][1] for kind, _ in scattered] + [None] * len(gathered)
        self.args = [g for _, g in scattered] + list(gathered)
        self.out_shape = [jax.ShapeDtypeStruct((N_DEV, *_SCATTERED[kind][0]), BF16) for kind, _ in scattered]
        self.out_shape += [jax.ShapeDtypeStruct((N_DEV, *a.shape), a.dtype) for a in gathered]
        n = len(self.args)
        self.scratch = [pltpu.SemaphoreType.DMA((n, N_DEV - 1)), pltpu.SemaphoreType.DMA((n, N_DEV - 1)),
                        pltpu.SemaphoreType.DMA((n,))]

    def _copies(self, srcs, outs, sems):
        send_sems, recv_sems, local_sems = sems
        me = _dev_index(*_my_place())
        copies = []
        for t, block in enumerate(self.blocks):
            part = (lambda j, t=t, block=block: srcs[t] if block is None else block(srcs[t], j))
            copies.append(pltpu.make_async_copy(part(me), outs[t].at[me], local_sems.at[t]))
            for r in range(1, N_DEV):
                peer = _peer_by_relation(r)
                copies.append(pltpu.make_async_remote_copy(
                    src_ref=part(_dev_index(*peer)), dst_ref=outs[t].at[me], send_sem=send_sems.at[t, r - 1],
                    recv_sem=recv_sems.at[t, r - 1], device_id=peer, device_id_type=MESH))
        return copies

    def start(self, srcs, outs, sems):
        for cp in self._copies(srcs, outs, sems):
            cp.start()

    def finish(self, srcs, outs, sems):
        for cp in self._copies(srcs, outs, sems):
            cp.wait()


class _Jobs:
    def __init__(self, *jobs):
        self.jobs = jobs
        together = {p for j in jobs for p in j.peers}
        self.peers = tuple(p for p in _EVERYONE if p in together)
        self.args = [a for j in jobs for a in j.args]
        self.out_shape = [o for j in jobs for o in j.out_shape]
        self.scratch = [s for j in jobs for s in j.scratch]

    def _split(self, refs, attr):
        at = 0
        for j in self.jobs:
            n = len(getattr(j, attr))
            yield refs[at:at + n]
            at += n

    def _each(self, ins, outs, scr):
        return zip(self.jobs, self._split(ins, "args"), self._split(outs, "out_shape"), self._split(scr, "scratch"))

    def start(self, ins, outs, scr):
        for j, i, o, s in self._each(ins, outs, scr):
            j.start(i, o, s)

    def mid(self, ins, outs, scr):
        for j, i, o, s in self._each(ins, outs, scr):
            if hasattr(j, "mid"):
                j.mid(i, o, s)

    def late(self, ins, outs, scr):
        for j, i, o, s in self._each(ins, outs, scr):
            if hasattr(j, "late"):
                j.late(i, o, s)

    def finish(self, ins, outs, scr):
        for j, i, o, s in self._each(ins, outs, scr):
            j.finish(i, o, s)

    def split_outputs(self, outs):
        return list(self._split(outs, "out_shape"))


def _adamw_math(w, g, m, v):
    m = ADAM_B1 * m + (1.0 - ADAM_B1) * g
    v = ADAM_B2 * v + (1.0 - ADAM_B2) * (g * g)
    m_hat = m / (1.0 - ADAM_B1 ** ADAM_STEP)
    v_hat = v / (1.0 - ADAM_B2 ** ADAM_STEP)
    delta = -ADAM_LR * (m_hat / (jnp.sqrt(v_hat) + ADAM_EPS) + ADAM_WD * w)
    return delta, m, v


def _adamw(name, w, m, v, landings, n_col_blocks=1, job=None):
    n_slots, r, c = landings[0].shape
    grid = (w.shape[0] // r, n_col_blocks)

    def body(w_ref, m_ref, v_ref, *rest):
        l_refs, (g_ref, d_ref, nm_ref, nv_ref) = rest[:len(landings)], rest[len(landings):]
        step = pl.program_id(0) * n_col_blocks + pl.program_id(1)
        for idx, l_ref in enumerate(l_refs):
            @pl.when(step == idx)
            def _(l_ref=l_ref):
                g = l_ref[0].astype(F32)
                for s in range(1, n_slots):
                    g = g + l_ref[s].astype(F32)
                g_ref[...] = g
                d_ref[...], nm_ref[...], nv_ref[...] = _adamw_math(w_ref[...], g, m_ref[...], v_ref[...])

    spec = pl.BlockSpec((r, c), lambda a, b: (a, b))
    return _launch(
        body, name=f"adamw_{name}", grid=grid,
        in_specs=[spec, spec, spec] + [_full_spec((n_slots, r, c))] * len(landings),
        out_specs=[spec] * 4, out_shape=[jax.ShapeDtypeStruct(w.shape, F32)] * 4,
        args=(w, m, v, *landings), vmem=VMEM_BIG, job=job)


_SMALL = (("pre_mix_g", SV_PRE_MIX, 2), ("post_mix_g", SV_POST_MIX, 2), ("pre_ffn_g", SV_PRE_FFN, 2),
          ("post_ffn_g", SV_POST_FFN, 2), ("ple_g", SV_PLE, 2), ("ple_post_g", SV_PLE_POST, 2), ("kv_g", SV_KV, 1),
          ("pool_scale", SV_POOL_SCALE, 1), ("sinks", SV_SINKS, 1))


def _adamw_several(items):
    counts = [len(landings) for _, _, _, landings in items]
    args = [a for w, m, v, landings in items for a in (w, m, v, *landings)]
    out_shape = [jax.ShapeDtypeStruct(w.shape, F32) for w, _, _, _ in items for _ in range(4)]

    def body(*refs):
        ins, outs = refs[:len(args)], refs[len(args):]
        at = 0
        for idx, n_landings in enumerate(counts):
            w_ref, m_ref, v_ref = ins[at:at + 3]
            l_refs = ins[at + 3:at + 3 + n_landings]
            at += 3 + n_landings
            g_ref, d_ref, nm_ref, nv_ref = outs[4 * idx:4 * idx + 4]
            for part, l_ref in enumerate(l_refs):
                rows = slice(part * l_ref.shape[1], (part + 1) * l_ref.shape[1])
                g = l_ref[0].astype(F32)
                for s in range(1, l_ref.shape[0]):
                    g = g + l_ref[s].astype(F32)
                g_ref[rows, :] = g
                d_ref[rows, :], nm_ref[rows, :], nv_ref[rows, :] = _adamw_math(
                    w_ref[rows, :], g, m_ref[rows, :], v_ref[rows, :])

    res, _ = _launch(
        body, name="adamw_several", grid=(1,), in_specs=[_full_spec(a.shape) for a in args],
        out_specs=[_full_spec(s.shape) for s in out_shape], out_shape=out_shape, args=args)
    return [res[4 * idx:4 * idx + 4] for idx in range(len(items))]


def _small_adamw(slabs, params):
    flat = [a for name, _, _ in _SMALL for a in params[name]]
    n_in = 1 + len(flat)

    def body(*refs):
        slabs_ref, wmv = refs[0], refs[1:n_in]
        loss_ref, outs, total = refs[n_in], refs[n_in + 1:-1], refs[-1]
        me = _dev_index(*_my_place())
        g = slabs_ref[0]
        for s in range(1, N_DEV):
            g = g + slabs_ref[s]
        total[...] = g
        loss_ref[...] = total[SV_LOSS:SV_LOSS + 1, 0:1]
        for idx, (name, row, n_rows) in enumerate(_SMALL):
            w_ref, m_ref, v_ref = wmv[3 * idx:3 * idx + 3]
            g_ref, d_ref, nm_ref, nv_ref = outs[4 * idx:4 * idx + 4]
            if name == "pool_scale":
                g = total[row:row + 1, pl.ds(pl.multiple_of(me * 128, 128), 128)]
            else:
                g = total[row:row + n_rows, 0:w_ref.shape[1]]
            g_ref[...] = g
            d_ref[...], nm_ref[...], nv_ref[...] = _adamw_math(w_ref[...], g, m_ref[...], v_ref[...])

    out_shape = [jax.ShapeDtypeStruct((1, 1), F32)]
    for name, _, _ in _SMALL:
        out_shape += [jax.ShapeDtypeStruct(params[name][0].shape, F32)] * 4
    res, _ = _launch(
        body, name="small_adamw", grid=(1,),
        in_specs=[_full_spec(a.shape) for a in (slabs, *flat)], out_specs=[_full_spec(s.shape) for s in out_shape],
        out_shape=out_shape, scratch_shapes=[pltpu.VMEM((SV_ROWS, D_MODEL), F32)], args=(slabs, *flat))
    return res[0], {name: res[1 + 4 * idx:5 + 4 * idx] for idx, (name, _, _) in enumerate(_SMALL)}


def _local_step(x, p, tgt, gains, sinks, shards, weights):
    row = lambda first_row, layer: _Gain(gains, first_row + layer)
    gather = lambda *names: _AllGather(names, shards)
    g_pre_mix, g_post_mix, g_pre_ffn, g_post_ffn = SV_PRE_MIX, SV_POST_MIX, SV_PRE_FFN, SV_POST_FFN
    g_ple, g_ple_post, g_kv = SV_PLE, SV_PLE_POST, _Gain(gains, SV_KV)

    (dpool,), (wp, scale, wgu0, wd0) = _fwd_pool(x, row(g_pre_mix, 0), job=gather("pool", "scale", "gu0", "wd0"))
    wgu0, wd0 = [wgu0], [wd0]
    (x1_0, h2_0, yraw), _ = _fwd_pool_mixer(x, dpool, wp, scale, row(g_post_mix, 0), row(g_pre_ffn, 0))
    (gs0, us0, f0, x2_0, h3_0), (wgate0, wproj0, wkv, wq, wo, wd1_a) = _fwd_ffn(
        0, h2_0, x1_0, wgu0, wd0, row(g_post_ffn, 0), row(g_ple, 0),
        job=gather("gate0", "proj0", "kv", "q", "o", "wdh1_0"))
    (x3_0, z0, pe0, hk, h1, q, kv), (wgu1_a,) = _fwd_ple_qkv(
        x2_0, h3_0, p[0], wgate0, wproj0, row(g_ple_post, 0), g_kv, row(g_pre_mix, 1), wkv, wq,
        job=gather("guh1_0"))
    front = ((ATT_BLOCK, 0), (0, 0))
    kpad = jnp.pad(kv[:, :KV_DIM], front)
    vpad = jnp.pad(kv[:, KV_DIM:], front)
    (attn,), (wgu1_b,) = _fwd_attention(q, kpad, vpad, sinks, job=gather("guh1_1"))
    (y1, x1_1, h2_1), (wd1_b,) = _fwd_attn_out(attn, x3_0, wo, row(g_post_mix, 1), row(g_pre_ffn, 1),
                                               job=gather("wdh1_1"))
    wgu1, wd1 = [wgu1_a, wgu1_b], [wd1_a, wd1_b]
    (gs1, us1, f1, x2_1, h3_1), (wgate1, wproj1) = _fwd_ffn(
        1, h2_1, x1_1, wgu1, wd1, row(g_post_ffn, 1), row(g_ple, 1), job=gather("gate1", "proj1"))

    produced, swapped, landed = {}, {}, {}

    def kind_of(name):
        return name.rstrip("0123_")

    def hosted(call, *args, swap=(), spread=(), extra=None):
        jobs = []
        if swap:
            jobs.append(_SiblingSwap([(kind_of(n), produced[n]) for n in swap]))
        if spread:
            jobs.append(_ChipScatter([(kind_of(n), produced[n], swapped[n]) for n in spread]))
        if extra is not None:
            jobs.append(extra)
        jobs = _Jobs(*jobs)
        outs, job_outs = call(*args, job=jobs)
        parts = jobs.split_outputs(job_outs)
        if swap:
            swapped.update(zip(swap, parts.pop(0)))
        if spread:
            landed.update(zip(spread, parts.pop(0)))
        return outs if extra is None else (outs, parts.pop(0))

    ffn_q = lambda layer, qtr: (f"gu{layer}_{qtr}", f"wd{layer}_{qtr}")

    dx2_1, df1, produced["gate1"], produced["proj1"], dg_ple_post1, dg_ple1, dg_post_ffn1, loss = hosted(
        _ple_loss_bwd, 1, x2_1, h3_1, p[1], f1, tgt, wgate1, wproj1, row(g_ple_post, 1), row(g_ple, 1),
        row(g_post_ffn, 1))
    dh2_1, dg1, du1, a1 = hosted(_bwd_ffn_act, 1, df1, gs1, us1, wgu1, wd1, swap=("gate1", "proj1"))
    dgu1, dwd1 = hosted(_bwd_ffn_dw, 1, 0, 1, h2_1, df1, dg1, du1, a1, spread=("gate1", "proj1"))
    produced.update(guA1=dgu1, guB1=dgu1, wdA1=dwd1, wdB1=dwd1)
    dx1_1, dattn, produced["o"], dg_pre_ffn1, dg_post_mix1 = hosted(
        _bwd_attn_out, dx2_1, dh2_1, x1_1, y1, attn, wo, row(g_pre_ffn, 1), row(g_post_mix, 1),
        swap=("guA1", "wdA1", "guB1", "wdB1"))
    dq, dkpad, dvpad, dsinks = hosted(_bwd_attention, q, dattn, kpad, vpad, sinks, spread=("guA1", "wdA1"))
    dkv = jnp.concatenate([dkpad[ATT_BLOCK:], dvpad[ATT_BLOCK:]], axis=1).astype(BF16)
    dx3_0, produced["q"], produced["kv"], dg_pre_mix1, dg_kv = hosted(
        _bwd_qkv, dx1_1, dq, dkv, x3_0, h1, hk, wq, wkv, row(g_pre_mix, 1), g_kv, swap=("o",), spread=("wdB1",))
    dx2_0, df0, produced["gate0"], produced["proj0"], dg_ple_post0, dg_ple0, dg_post_ffn0 = hosted(
        _bwd_ple, 0, dx3_0, x2_0, z0, pe0, h3_0, p[0], f0, wgate0, row(g_ple_post, 0), row(g_ple, 0),
        row(g_post_ffn, 0), swap=("q", "kv"), spread=("guB1",))
    for half, letter in enumerate("AB"):
        landed[f"gu1_{half}"], landed[f"wd1_{half}"] = landed[f"gu{letter}1"], landed[f"wd{letter}1"]
    dh2_0, dg0, du0, a0 = hosted(_bwd_ffn_act, 0, df0, gs0, us0, wgu0, wd0,
                                 swap=("gate0", "proj0"), spread=("o", "q", "kv"))
    part_hosts = [dict(spread=("gate0", "proj0")), dict(swap=ffn_q(0, 0))]
    for part in range(FF_PARTS):
        produced[f"gu0_{part}"], produced[f"wd0_{part}"] = hosted(
            _bwd_ffn_dw, 0, part, FF_PARTS, h2_0, df0, dg0, du0, a0, **part_hosts[part])
    grad_x, produced["pool"], dscale, dg_pre_ffn0, dg_post_mix0, dg_pre_mix0 = hosted(
        _bwd_pool_mixer, dx2_0, dh2_0, x1_0, x, yraw, dpool, wp, scale, row(g_pre_ffn, 0), row(g_post_mix, 0),
        row(g_pre_mix, 0), swap=ffn_q(0, 1), spread=ffn_q(0, 0))

    def update(name, n_col_blocks, pieces):
        w, m, v = weights[name]
        rows = w.size // w.shape[-1]
        flat = [landed[n].reshape(landed[n].shape[0], -1, landed[n].shape[-1]) for n in pieces]
        outs, _ = _adamw(name, w.reshape(rows, -1), m.reshape(rows, -1), v.reshape(rows, -1), flat, n_col_blocks)
        return [o.reshape(w.shape) for o in outs]

    upd = {}
    lanes = lambda a: jnp.pad(a, ((0, 0), (0, D_MODEL - a.shape[1])))
    small = jnp.concatenate([
        dg_pre_mix0, dg_pre_mix1, dg_post_mix0, dg_post_mix1, dg_pre_ffn0, dg_pre_ffn1, dg_post_ffn0, dg_post_ffn1,
        dg_ple0, dg_ple1, dg_ple_post0, dg_ple_post1, dg_kv, dscale, lanes(dsinks[:, :N_HEADS]), lanes(loss)], axis=0)

    everyone = _ToEveryone(scattered=[("pool", produced["pool"])], gathered=[small])
    _, (landed["pool"], slabs) = hosted(_jobs_only, "scatter_tail", spread=ffn_q(0, 1), extra=everyone)
    several = {"w_ple_gate": ("gate0", "gate1"), "w_ple_proj": ("proj0", "proj1"), "w_q": ("q",), "w_kv": ("kv",),
               "w_o": ("o",), "pool_w": ("pool",)}
    flat2d = lambda a: a.reshape(-1, a.shape[-1])
    results = _adamw_several([
        (*map(flat2d, weights[name]),
         [landed[n].reshape(landed[n].shape[0], -1, landed[n].shape[-1]) for n in pieces])
        for name, pieces in several.items()])
    for name, outs in zip(several, results):
        upd[name] = [o.reshape(weights[name][0].shape) for o in outs]
    upd["w_gu"] = update("w_gu", FF_PARTS,
                         pieces=[f"gu{layer}_{qtr}" for layer in range(2) for qtr in range(FF_PARTS)])
    upd["w_gu"] = [jnp.swapaxes(a, 1, 2) for a in upd["w_gu"]]
    upd["w_down"] = update("w_down", FF_PARTS,
                           pieces=[f"wd{layer}_{qtr}" for layer in range(2) for qtr in range(FF_PARTS)])
    return grad_x, upd, slabs


def kernel(x, p, pre_mix_g, post_mix_g, pre_ffn_g, post_ffn_g, pool_w, pool_scale, kv_g, w_kv, w_q, sinks, w_o, w_gu, w_down, ple_g, w_ple_gate, w_ple_proj, ple_post_g, loss_target, m_pre_mix_g, m_post_mix_g, m_pre_ffn_g, m_post_ffn_g, m_pool_w, m_pool_scale, m_kv_g, m_w_kv, m_w_q, m_sinks, m_w_o, m_w_gu, m_w_down, m_ple_g, m_w_ple_gate, m_w_ple_proj, m_ple_post_g, v_pre_mix_g, v_post_mix_g, v_pre_ffn_g, v_post_ffn_g, v_pool_w, v_pool_scale, v_kv_g, v_w_kv, v_w_q, v_sinks, v_w_o, v_w_gu, v_w_down, v_ple_g, v_w_ple_gate, v_w_ple_proj, v_ple_post_g):
    shards = {"pool": pool_w[0].astype(BF16), "scale": pool_scale, "kv": w_kv.astype(BF16),
              "q": w_q[0].astype(BF16), "o": w_o[0].astype(BF16)}
    for layer in range(2):
        shards[f"gu{layer}"] = w_gu[layer].T.astype(BF16)
        shards[f"wd{layer}"] = w_down[layer].astype(BF16)
        for half in range(2):
            cols = (half * D_MODEL // 2, (half + 1) * D_MODEL // 2)
            shards[f"guh{layer}_{half}"] = (shards[f"gu{layer}"], cols)
            shards[f"wdh{layer}_{half}"] = (shards[f"wd{layer}"], cols)
        shards[f"gate{layer}"] = w_ple_gate[layer].astype(BF16)
        shards[f"proj{layer}"] = w_ple_proj[layer].astype(BF16)
    gains = jnp.concatenate([pre_mix_g, post_mix_g, pre_ffn_g, post_ffn_g, ple_g, ple_post_g, kv_g[None, :]],
                            axis=0).reshape(-1, 1, D_MODEL)
    weights = {"pool_w": (pool_w, m_pool_w, v_pool_w), "w_kv": (w_kv, m_w_kv, v_w_kv), "w_q": (w_q, m_w_q, v_w_q),
               "w_o": (w_o, m_w_o, v_w_o), "w_down": (w_down, m_w_down, v_w_down),
               "w_gu": tuple(jnp.swapaxes(a, 1, 2) for a in (w_gu, m_w_gu, v_w_gu)),
               "w_ple_gate": (w_ple_gate, m_w_ple_gate, v_w_ple_gate),
               "w_ple_proj": (w_ple_proj, m_w_ple_proj, v_w_ple_proj)}
    per_layer = p.reshape(p.shape[0], *p.shape[2:])
    p_rows = [_LayerRows(per_layer, layer) for layer in range(2)]
    grad_x, upd, slabs = _local_step(x[0], p_rows, loss_target[0], gains, sinks, shards, weights)

    small_params = {
        "pre_mix_g": (pre_mix_g, m_pre_mix_g, v_pre_mix_g), "post_mix_g": (post_mix_g, m_post_mix_g, v_post_mix_g),
        "pre_ffn_g": (pre_ffn_g, m_pre_ffn_g, v_pre_ffn_g), "post_ffn_g": (post_ffn_g, m_post_ffn_g, v_post_ffn_g),
        "ple_g": (ple_g, m_ple_g, v_ple_g), "ple_post_g": (ple_post_g, m_ple_post_g, v_ple_post_g),
        "kv_g": (kv_g[None, :], m_kv_g[None, :], v_kv_g[None, :]),
        "pool_scale": (pool_scale, m_pool_scale, v_pool_scale), "sinks": (sinks, m_sinks, v_sinks)}
    loss, small_upd = _small_adamw(slabs, small_params)
    small_upd["kv_g"] = [a[0] for a in small_upd["kv_g"]]
    upd.update(small_upd)

    names = ["pre_mix_g", "post_mix_g", "pre_ffn_g", "post_ffn_g", "pool_w", "pool_scale", "kv_g", "w_kv", "w_q",
             "sinks", "w_o", "w_gu", "w_down", "ple_g", "w_ple_gate", "w_ple_proj", "ple_post_g"]
    outs = [loss[0, 0], grad_x[None]]
    for kind in range(4):
        outs += [upd[n][kind] for n in names]
    return tuple(outs)
```

```python
import functools
import types

import jax
import jax.numpy as jnp
from jax import lax
from jax.experimental import pallas as pl
from jax.experimental.pallas import tpu as pltpu

F32 = jnp.float32
BF16 = jnp.bfloat16

N_DEV = 8
D_MODEL = 1024
N_POOL_GROUPS = 4
POOL_GROUP = 256
POOL_HALO = 16
HEAD_DIM = 64
N_HEADS = 16
N_KV_HEADS = 4
GQA_GROUP = 4
KV_DIM = N_KV_HEADS * HEAD_DIM
ATT_BLOCK = 128
D_FF = 2816
FF_CHUNKS = 4
FF_BLOCK = D_FF // FF_CHUNKS
WD_ROWS = D_FF // N_DEV
FF_PARTS = 2
FF_PART = D_MODEL // FF_PARTS
N_CHIPS = 4
PLE_DIM = 256
EPS = 1e-6
NEG_INF = -1e30
ATT_SCALE = HEAD_DIM ** -0.5

ADAM_LR = 0.001
ADAM_B1 = 0.9
ADAM_B2 = 0.999
ADAM_EPS = 1e-08
ADAM_WD = 0.01
ADAM_STEP = 10

ROW_TILE = 512
FFN_ROW_TILE = 512
FFN_WEIGHT_COLS = 512
FFN_SUB_TILES = 1
VMEM_BIG = 60 * 1024 * 1024
VMEM_MID = 56 * 1024 * 1024
HBM_PIN_ELEMS = 1024

SV_ROWS = 16
SV_PRE_MIX, SV_POST_MIX, SV_PRE_FFN, SV_POST_FFN, SV_PLE, SV_PLE_POST = 0, 2, 4, 6, 8, 10
SV_KV, SV_POOL_SCALE, SV_SINKS, SV_LOSS = 12, 13, 14, 15

MESH = pl.DeviceIdType.MESH
ANY = pl.BlockSpec(memory_space=pl.ANY)


def _dot(a, b):
    return jnp.dot(a, b, preferred_element_type=F32)


def _dot_nt(a, b):
    return lax.dot_general(a, b, (((1,), (1,)), ((), ())), preferred_element_type=F32)


def _dot_tn(a, b):
    return lax.dot_general(a, b, (((0,), (0,)), ((), ())), preferred_element_type=F32)


def _rstd(x):
    return lax.rsqrt(jnp.mean(x * x, axis=-1, keepdims=True) + EPS)


def _rms(x, g):
    return x * _rstd(x) * g


def _rms_bwd(x, g, dy):
    r = _rstd(x)
    n = x * r
    dn = dy * g
    dx = r * (dn - n * jnp.mean(dn * n, axis=-1, keepdims=True))
    dg = jnp.sum(dy * n, axis=0, keepdims=True)
    return dx, dg


def _add_all(terms):
    return functools.reduce(jnp.add, terms)


def _sigmoid(x):
    return 1.0 / (1.0 + jnp.exp(-x))


def _acc(ref, val, first):
    @pl.when(first)
    def _():
        ref[...] = val

    @pl.when(jnp.logical_not(first))
    def _():
        ref[...] += val


def _pool_counts(row0, rows):
    t = row0 + lax.broadcasted_iota(jnp.int32, (rows, D_MODEL), 0) + 1
    grp = lax.broadcasted_iota(jnp.int32, (rows, D_MODEL), 1) // POOL_GROUP
    win = jnp.left_shift(2, grp)
    return jnp.minimum(t, win).astype(F32)


def _window_sums(ext, shift_of):
    outs = []
    s = ext
    for gi in range(N_POOL_GROUPS):
        s = s + pltpu.roll(s, shift_of(1 << gi), axis=0)
        outs.append(s[:, :POOL_GROUP])
        s = s[:, POOL_GROUP:]
    return jnp.concatenate(outs, axis=1)


def _cparams(n_axes, vmem, collective_id=None):
    return pltpu.CompilerParams(dimension_semantics=("arbitrary",) * n_axes, vmem_limit_bytes=vmem,
                                collective_id=collective_id)


_EVERYONE = ("sibling", "x", "y", "far", "x sibling", "y sibling", "far sibling")
_PEER_SETS = (("sibling", "x", "y"), ("sibling",), ("x", "y"), _EVERYONE)


def _meet(peers):
    x, y, c = lax.axis_index("x"), lax.axis_index("y"), lax.axis_index("c")
    device = {"sibling": (x, y, 1 - c), "x": (1 - x, y, c), "y": (x, 1 - y, c), "far": (1 - x, 1 - y, c),
              "x sibling": (1 - x, y, 1 - c), "y sibling": (x, 1 - y, 1 - c), "far sibling": (1 - x, 1 - y, 1 - c)}
    barrier = pltpu.get_barrier_semaphore()
    for peer in peers:
        pl.semaphore_signal(barrier, inc=1, device_id=device[peer], device_id_type=pl.DeviceIdType.MESH)
    pl.semaphore_wait(barrier, len(peers))


def _row_spec(cols, tm=ROW_TILE):
    return pl.BlockSpec((tm, cols), lambda i: (i, 0))


def _full_spec(shape):
    zeros = (0,) * len(shape)
    return pl.BlockSpec(shape, lambda *_: zeros)


def _vec_spec():
    return _full_spec((1, D_MODEL))


def _column_views(parts):
    return [(a, b) for a in parts for b in range(a.shape[-1] // FFN_WEIGHT_COLS)]


def _column_ranges(views):
    return [(n * FFN_WEIGHT_COLS, (n + 1) * FFN_WEIGHT_COLS) for n in range(len(views))]


class _Gain:
    def __init__(self, stacked, layer):
        self.stacked, self.layer = stacked, layer

    def spec(self):
        layer = self.layer
        return pl.BlockSpec((None, 1, D_MODEL), lambda *_: (layer, 0, 0))


class _LayerRows:
    def __init__(self, stacked, layer):
        self.stacked, self.layer = stacked, layer

    def spec(self):
        layer = self.layer
        return pl.BlockSpec((None, ROW_TILE, self.stacked.shape[-1]), lambda i: (layer, i, 0))


def _in_hbm(a):
    return pltpu.with_memory_space_constraint(a, pltpu.HBM) if a.size >= HBM_PIN_ELEMS else a


def _out_in_hbm(s):
    return pltpu.HBM(s.shape, s.dtype) if s.size >= HBM_PIN_ELEMS else s


def _launch(body, *, name, grid, in_specs, out_specs, out_shape, args, scratch_shapes=(), vmem=VMEM_MID, job=None):
    picked = (_Gain, _LayerRows)
    in_specs = [a.spec() if isinstance(a, picked) else s for s, a in zip(in_specs, args)]
    args = [_in_hbm(a.stacked if isinstance(a, picked) else a) for a in args]
    n_in, n_out, n_scr = len(args), len(out_shape), len(scratch_shapes)
    if job is not None and not job.args:
        job = None
    j_args, j_out, j_scr = ([], [], []) if job is None else ([_in_hbm(a) for a in job.args], job.out_shape, job.scratch)

    def run(*refs):
        groups, at = [], 0
        for n in (n_in, len(j_args), n_out, len(j_out), n_scr, len(j_scr)):
            groups.append(refs[at:at + n])
            at += n
        ins, j_ins, outs, j_outs, scr, j_sems = groups

        def begin():
            _meet(job.peers)
            job.start(j_ins, j_outs, j_sems)

        if job is None:
            body(*ins, *outs, *scr)
        elif not grid:
            begin()
            job.mid(j_ins, j_outs, j_sems)
            job.late(j_ins, j_outs, j_sems)
            body(*ins, *outs, *scr)
            job.finish(j_ins, j_outs, j_sems)
        else:
            ids = [pl.program_id(a) for a in range(len(grid))]
            at_start = lambda step: functools.reduce(jnp.logical_and, [ids[0] == step] + [i == 0 for i in ids[1:]])
            last = functools.reduce(jnp.logical_and, [i == g - 1 for i, g in zip(ids, grid)])
            pl.when(at_start(0))(begin)
            pl.when(at_start(grid[0] // 2))(lambda: job.mid(j_ins, j_outs, j_sems))
            pl.when(at_start(3 * grid[0] // 4))(lambda: job.late(j_ins, j_outs, j_sems))
            body(*ins, *outs, *scr)
            pl.when(last)(lambda: job.finish(j_ins, j_outs, j_sems))

    res = pl.pallas_call(
        run, name=name, grid=grid,
        in_specs=list(in_specs) + [ANY] * len(j_args), out_specs=list(out_specs) + [ANY] * len(j_out),
        out_shape=[_out_in_hbm(s) for s in list(out_shape) + list(j_out)],
        scratch_shapes=list(scratch_shapes) + list(j_scr),
        compiler_params=_cparams(len(grid), vmem, None if job is None else _PEER_SETS.index(job.peers)),
    )(*args, *j_args)
    return res[:n_out], res[n_out:]


def _fwd_pool(x, g_pre, job=None):
    T = x.shape[0]
    tm = ROW_TILE
    nt = T // tm

    def body(x_ref, gpre_ref, d_ref, carry):
        i = pl.program_id(0)

        @pl.when(i == 0)
        def _():
            carry[...] = jnp.zeros_like(carry)

        h = _rms(x_ref[...], gpre_ref[...])
        ext = jnp.concatenate([carry[...], h], axis=0)
        carry[...] = h[tm - POOL_HALO:, :]
        sums = _window_sums(ext, lambda k: k)[POOL_HALO:, :]
        d_ref[...] = (sums / _pool_counts(i * tm, tm) - h).astype(BF16)

    return _launch(
        body, name="fwd_pool", grid=(nt,), in_specs=[_row_spec(D_MODEL), _vec_spec()], out_specs=[_row_spec(D_MODEL)],
        out_shape=[jax.ShapeDtypeStruct((T, D_MODEL), BF16)], scratch_shapes=[pltpu.VMEM((POOL_HALO, D_MODEL), F32)],
        args=(x, g_pre), job=job)


def _fwd_pool_mixer(x, d, wp, scale, g_post, g_ffn, job=None):
    T = x.shape[0]
    nt = T // ROW_TILE

    def body(x_ref, d_ref, wp_ref, sc_ref, gpost_ref, gffn_ref, x1_ref, h2_ref, yraw_ref):
        db = d_ref[...]
        yraw = jnp.concatenate(
            [_dot(db[:, g * POOL_GROUP:(g + 1) * POOL_GROUP], wp_ref[g]) for g in range(N_POOL_GROUPS)], axis=1)
        yraw_ref[...] = yraw.astype(BF16)
        x1 = x_ref[...] + _rms(yraw * sc_ref[...], gpost_ref[...])
        x1_ref[...] = x1
        h2_ref[...] = _rms(x1, gffn_ref[...]).astype(BF16)

    return _launch(
        body, name="fwd_pool_mixer", grid=(nt,),
        in_specs=[_row_spec(D_MODEL), _row_spec(D_MODEL), _full_spec((N_POOL_GROUPS, POOL_GROUP, POOL_GROUP)),
                  _vec_spec(), _vec_spec(), _vec_spec()],
        out_specs=[_row_spec(D_MODEL)] * 3,
        out_shape=[jax.ShapeDtypeStruct((T, D_MODEL), F32)] + [jax.ShapeDtypeStruct((T, D_MODEL), BF16)] * 2,
        args=(x, d, wp, scale, g_post, g_ffn), job=job)


def _fwd_ffn(layer, h2, x1, wgu, wd, g_post, g_ple, job=None):
    T = h2.shape[0]
    tm = min(FFN_ROW_TILE, T)
    nt = T // tm
    sub = tm // FFN_SUB_TILES
    last = FF_CHUNKS - 1
    wgu, wd = _column_views(wgu), _column_views(wd)
    n_gu, n_wd = len(wgu), len(wd)
    gu_cols = _column_ranges(wgu)

    def body(h2_ref, x1_ref, *refs):
        wgu_refs, wd_refs = refs[:n_gu], refs[n_gu:n_gu + n_wd]
        gpost_ref, gple_ref, gs_ref, us_ref, f_ref, x2_ref, h3_ref, acc = refs[n_gu + n_wd:]
        k = pl.program_id(0)
        i = pl.program_id(1)
        rows = pl.ds(pl.multiple_of(i * tm, tm), tm)
        parts = []
        for s in range(FFN_SUB_TILES):
            r = pl.ds(s * sub, sub)
            g = _add_all([_dot_nt(h2_ref[r, c0:c1], w[0]) for (c0, c1), w in zip(gu_cols, wgu_refs)])
            u = _add_all([_dot_nt(h2_ref[r, c0:c1], w[1]) for (c0, c1), w in zip(gu_cols, wgu_refs)])
            gs_ref[r, :] = g.astype(BF16)
            us_ref[r, :] = u.astype(BF16)
            a = (g * _sigmoid(g) * u).astype(BF16)
            parts.append(jnp.concatenate([_dot(a, w[...]) for w in wd_refs], axis=1))
        part = jnp.concatenate(parts, axis=0)

        @pl.when(k == 0)
        def _():
            acc[rows, :] = part

        @pl.when(jnp.logical_and(k > 0, k < last))
        def _():
            acc[rows, :] += part

        @pl.when(k == last)
        def _():
            f = acc[rows, :] + part
            f_ref[...] = f.astype(BF16)
            x2 = x1_ref[...] + _rms(f, gpost_ref[...])
            x2_ref[...] = x2
            h3_ref[...] = _rms(x2, gple_ref[...]).astype(BF16)

    def late(k, i):
        return (jnp.where(k == last, i, 0), 0)

    return _launch(
        body, name=f"fwd_ffn{layer}", grid=(FF_CHUNKS, nt),
        in_specs=[pl.BlockSpec((tm, D_MODEL), lambda k, i: (i, 0)), pl.BlockSpec((tm, D_MODEL), late)]
                 + [pl.BlockSpec((None, 2, FF_BLOCK, FFN_WEIGHT_COLS), lambda k, i, b=b: (k, 0, 0, b)) for _, b in wgu]
                 + [pl.BlockSpec((FF_BLOCK, FFN_WEIGHT_COLS), lambda k, i, b=b: (k, b)) for _, b in wd]
                 + [pl.BlockSpec((1, D_MODEL), lambda k, i: (0, 0))] * 2,
        out_specs=[pl.BlockSpec((None, tm, FF_BLOCK), lambda k, i: (k, i, 0)),
                   pl.BlockSpec((None, tm, FF_BLOCK), lambda k, i: (k, i, 0)),
                   pl.BlockSpec((tm, D_MODEL), late),
                   pl.BlockSpec((tm, D_MODEL), late),
                   pl.BlockSpec((tm, D_MODEL), late)],
        out_shape=[jax.ShapeDtypeStruct((FF_CHUNKS, T, FF_BLOCK), BF16),
                   jax.ShapeDtypeStruct((FF_CHUNKS, T, FF_BLOCK), BF16),
                   jax.ShapeDtypeStruct((T, D_MODEL), BF16),
                   jax.ShapeDtypeStruct((T, D_MODEL), F32),
                   jax.ShapeDtypeStruct((T, D_MODEL), BF16)],
        scratch_shapes=[pltpu.VMEM((T, D_MODEL), F32)],
        args=(h2, x1, *[w for w, _ in wgu], *[w for w, _ in wd], g_post, g_ple), vmem=VMEM_BIG, job=job)


def _fwd_ple_qkv(x2, h3, p, wgate, wproj, g_post, g_kv, g_mix, wkv, wq, job=None):
    T = x2.shape[0]
    nt = T // ROW_TILE

    def body(x2_ref, h3_ref, p_ref, wg_ref, wp_ref, gpost_ref, gkv_ref, gmix_ref, wkv_ref, wq_ref,
             x3_ref, z_ref, pe_ref, hk_ref, h1_ref, q_ref, kv_ref):
        z = _dot(h3_ref[...], wg_ref[...])
        pe = _dot(p_ref[...].astype(BF16), wp_ref[...])
        z_ref[...] = z.astype(BF16)
        pe_ref[...] = pe.astype(BF16)
        x3 = x2_ref[...] + _rms(pe * _sigmoid(z), gpost_ref[...])
        x3_ref[...] = x3
        r = _rstd(x3)
        hk = (x3 * r * gkv_ref[...]).astype(BF16)
        h1 = (x3 * r * gmix_ref[...]).astype(BF16)
        hk_ref[...] = hk
        h1_ref[...] = h1
        kv_ref[...] = _dot(hk, wkv_ref[...]).astype(BF16)
        q_ref[...] = _dot(h1, wq_ref[...]).astype(BF16)

    wide = jax.ShapeDtypeStruct((T, D_MODEL), BF16)
    return _launch(
        body, name="fwd_ple_qkv", grid=(nt,),
        in_specs=[_row_spec(D_MODEL), _row_spec(D_MODEL), _row_spec(PLE_DIM), _full_spec((D_MODEL, D_MODEL)),
                  _full_spec((PLE_DIM, D_MODEL)), _vec_spec(), _vec_spec(), _vec_spec(),
                  _full_spec((D_MODEL, 2 * KV_DIM)), _full_spec((D_MODEL, D_MODEL))],
        out_specs=[_row_spec(D_MODEL)] * 6 + [_row_spec(2 * KV_DIM)],
        out_shape=[jax.ShapeDtypeStruct((T, D_MODEL), F32)] + [wide] * 5 + [jax.ShapeDtypeStruct((T, 2 * KV_DIM), BF16)],
        args=(x2, h3, p, wgate, wproj, g_post, g_kv, g_mix, wkv, wq), job=job)


def _alibi_slope(h):
    return 2.0 ** (-8.0 * (h + 1) / N_HEADS)


ATT_SUB = 32
ATT_GROUP_ROWS = GQA_GROUP * ATT_BLOCK


def _att_mask(n, rel_ref, off_ref):
    qi = lax.broadcasted_iota(jnp.int32, (ATT_BLOCK, 2 * ATT_BLOCK), 0)
    si = lax.broadcasted_iota(jnp.int32, (ATT_BLOCK, 2 * ATT_BLOCK), 1)
    rel = ATT_BLOCK + qi - si
    valid = (rel >= 0) & (rel < ATT_BLOCK) & ((si >= ATT_BLOCK) | (n > 0))
    rel_ref[...] = rel.astype(F32)
    off_ref[...] = jnp.where(valid, 0.0, NEG_INF)


def _att_probs(raw, relf, off, slope, sink):
    s = raw * ATT_SCALE - slope * relf + off
    m = jnp.maximum(jnp.max(s, axis=-1, keepdims=True), sink)
    e = jnp.exp(s - m)
    es = jnp.exp(sink - m)
    inv = 1.0 / (jnp.sum(e, axis=-1, keepdims=True) + es)
    return e * inv, es * inv


def _stack_heads(ref, kh):
    first = kh * GQA_GROUP
    return jnp.concatenate([ref[:, (first + g) * HEAD_DIM:(first + g + 1) * HEAD_DIM] for g in range(GQA_GROUP)], axis=0)


def _unstack_heads(stacked):
    return [stacked[g * ATT_BLOCK:(g + 1) * ATT_BLOCK, :] for g in range(GQA_GROUP)]


def _fwd_attention(q, kpad, vpad, sinks, job=None):
    T = q.shape[0]
    nb = T // ATT_BLOCK

    def body(q_ref, k_ref, v_ref, sink_ref, o_ref, s_scr, p_scr, rel_scr, off_scr):
        n = pl.program_id(0)
        start = pl.multiple_of(n * ATT_BLOCK, ATT_BLOCK)
        kw = k_ref[pl.ds(start, 2 * ATT_BLOCK), :]
        vw = v_ref[pl.ds(start, 2 * ATT_BLOCK), :]
        _att_mask(n, rel_scr, off_scr)
        outs = []
        for kh in range(N_KV_HEADS):
            kk = kw[:, kh * HEAD_DIM:(kh + 1) * HEAD_DIM]
            vv = vw[:, kh * HEAD_DIM:(kh + 1) * HEAD_DIM]
            s_scr[...] = _dot_nt(_stack_heads(q_ref, kh), kk)
            for g in range(GQA_GROUP):
                h = kh * GQA_GROUP + g
                for row0 in range(0, ATT_BLOCK, ATT_SUB):
                    rows, sub = pl.ds(g * ATT_BLOCK + row0, ATT_SUB), pl.ds(row0, ATT_SUB)
                    pr, _ = _att_probs(s_scr[rows, :], rel_scr[sub, :], off_scr[sub, :], _alibi_slope(h),
                                       sink_ref[0, h])
                    p_scr[rows, :] = pr.astype(BF16)
            outs += _unstack_heads(_dot(p_scr[...], vv))
        o_ref[...] = jnp.concatenate(outs, axis=1).astype(BF16)

    return _launch(
        body, name="fwd_attention", grid=(nb,),
        in_specs=[_row_spec(D_MODEL, ATT_BLOCK), _full_spec((T + ATT_BLOCK, KV_DIM)), _full_spec((T + ATT_BLOCK, KV_DIM)),
                  pl.BlockSpec(memory_space=pltpu.SMEM)],
        out_specs=[_row_spec(D_MODEL, ATT_BLOCK)],
        out_shape=[jax.ShapeDtypeStruct((T, D_MODEL), BF16)],
        scratch_shapes=[pltpu.VMEM((ATT_GROUP_ROWS, 2 * ATT_BLOCK), F32), pltpu.VMEM((ATT_GROUP_ROWS, 2 * ATT_BLOCK), BF16)]
                       + [pltpu.VMEM((ATT_BLOCK, 2 * ATT_BLOCK), F32)] * 2,
        args=(q, kpad, vpad, sinks), job=job)


def _fwd_attn_out(attn, x, wo, g_post, g_ffn, job=None):
    T = x.shape[0]
    nt = T // ROW_TILE

    def body(a_ref, x_ref, wo_ref, gpost_ref, gffn_ref, y_ref, x1_ref, h2_ref):
        y = _dot(a_ref[...], wo_ref[...])
        y_ref[...] = y.astype(BF16)
        x1 = x_ref[...] + _rms(y, gpost_ref[...])
        x1_ref[...] = x1
        h2_ref[...] = _rms(x1, gffn_ref[...]).astype(BF16)

    return _launch(
        body, name="fwd_attn_out", grid=(nt,),
        in_specs=[_row_spec(D_MODEL), _row_spec(D_MODEL), _full_spec((D_MODEL, D_MODEL)), _vec_spec(), _vec_spec()],
        out_specs=[_row_spec(D_MODEL)] * 3,
        out_shape=[jax.ShapeDtypeStruct((T, D_MODEL), BF16), jax.ShapeDtypeStruct((T, D_MODEL), F32),
                   jax.ShapeDtypeStruct((T, D_MODEL), BF16)],
        args=(attn, x, wo, g_post, g_ffn), job=job)


def _bwd_ple(layer, dx3, x2, z, pe, h3, p, f, wgate, g_ple_post, g_ple, g_post_ffn, job=None):
    T = x2.shape[0]
    tm = ROW_TILE
    nt = T // tm

    def body(dx3_ref, x2_ref, z_ref, pe_ref, h3_ref, p_ref, f_ref, wg_ref, gpp_ref, gp_ref, gpf_ref,
             dx2_ref, df_ref, dwg_ref, dwp_ref, dgpp_ref, dgp_ref, dgpf_ref, acc_g, acc_p):
        i = pl.program_id(0)
        first = i == 0
        dx3v = dx3_ref[...]
        gate = _sigmoid(z_ref[...].astype(F32))
        pev = pe_ref[...].astype(F32)
        de, dgpp = _rms_bwd(pev * gate, gpp_ref[...], dx3v)
        dpe = (de * gate).astype(BF16)
        dz = (de * pev * gate * (1.0 - gate)).astype(BF16)
        _acc(acc_p, _dot_tn(p_ref[...].astype(BF16), dpe), first)
        _acc(acc_g, _dot_tn(h3_ref[...], dz), first)
        dh3 = _dot_nt(dz, wg_ref[...])
        dxn, dgp = _rms_bwd(x2_ref[...], gp_ref[...], dh3)
        dx2 = dx3v + dxn
        dx2_ref[...] = dx2
        df, dgpf = _rms_bwd(f_ref[...].astype(F32), gpf_ref[...], dx2)
        df_ref[...] = df.astype(BF16)
        _acc(dgpp_ref, dgpp, first)
        _acc(dgp_ref, dgp, first)
        _acc(dgpf_ref, dgpf, first)

        @pl.when(i == nt - 1)
        def _():
            dwg_ref[...] = acc_g[...].astype(BF16)
            dwp_ref[...] = acc_p[...].astype(BF16)

    return _launch(
        body, name=f"bwd_ple{layer}", grid=(nt,),
        in_specs=[_row_spec(D_MODEL)] * 5 + [_row_spec(PLE_DIM), _row_spec(D_MODEL), _full_spec((D_MODEL, D_MODEL)),
                  _vec_spec(), _vec_spec(), _vec_spec()],
        out_specs=[_row_spec(D_MODEL), _row_spec(D_MODEL), _full_spec((D_MODEL, D_MODEL)), _full_spec((PLE_DIM, D_MODEL)),
                   _vec_spec(), _vec_spec(), _vec_spec()],
        out_shape=[jax.ShapeDtypeStruct((T, D_MODEL), F32), jax.ShapeDtypeStruct((T, D_MODEL), BF16),
                   jax.ShapeDtypeStruct((D_MODEL, D_MODEL), BF16), jax.ShapeDtypeStruct((PLE_DIM, D_MODEL), BF16)]
                  + [jax.ShapeDtypeStruct((1, D_MODEL), F32)] * 3,
        scratch_shapes=[pltpu.VMEM((D_MODEL, D_MODEL), F32), pltpu.VMEM((PLE_DIM, D_MODEL), F32)],
        args=(dx3, x2, z, pe, h3, p, f, wgate, g_ple_post, g_ple, g_post_ffn), vmem=VMEM_BIG, job=job)


def _ple_loss_bwd(layer, x2, h3, p, f, target, wgate, wproj, g_ple_post, g_ple, g_post_ffn, job=None):
    T = x2.shape[0]
    tm = ROW_TILE
    nt = T // tm

    def body(x2_ref, h3_ref, p_ref, f_ref, tgt_ref, wg_ref, wp_ref, gpp_ref, gp_ref, gpf_ref,
             dx2_ref, df_ref, dwg_ref, dwp_ref, dgpp_ref, dgp_ref, dgpf_ref, loss_ref, acc_g, acc_p):
        i = pl.program_id(0)
        first = i == 0
        h3 = h3_ref[...]
        pb = p_ref[...].astype(BF16)
        x2v = x2_ref[...]
        gate = _sigmoid(_dot(h3, wg_ref[...]))
        pev = _dot(pb, wp_ref[...])
        e = pev * gate
        err = x2v + _rms(e, gpp_ref[...]) - tgt_ref[...]
        _acc(loss_ref, 0.5 * jnp.sum(jnp.mean(err * err, axis=-1, keepdims=True), axis=0, keepdims=True), first)
        dx3v = err * (1.0 / D_MODEL)
        de, dgpp = _rms_bwd(e, gpp_ref[...], dx3v)
        dpe = (de * gate).astype(BF16)
        dz = (de * pev * gate * (1.0 - gate)).astype(BF16)
        _acc(acc_p, _dot_tn(pb, dpe), first)
        _acc(acc_g, _dot_tn(h3, dz), first)
        dxn, dgp = _rms_bwd(x2v, gp_ref[...], _dot_nt(dz, wg_ref[...]))
        dx2 = dx3v + dxn
        dx2_ref[...] = dx2
        df, dgpf = _rms_bwd(f_ref[...].astype(F32), gpf_ref[...], dx2)
        df_ref[...] = df.astype(BF16)
        _acc(dgpp_ref, dgpp, first)
        _acc(dgp_ref, dgp, first)
        _acc(dgpf_ref, dgpf, first)

        @pl.when(i == nt - 1)
        def _():
            dwg_ref[...] = acc_g[...].astype(BF16)
            dwp_ref[...] = acc_p[...].astype(BF16)

    return _launch(
        body, name=f"ple_loss_bwd{layer}", grid=(nt,),
        in_specs=[_row_spec(D_MODEL), _row_spec(D_MODEL), _row_spec(PLE_DIM), _row_spec(D_MODEL), _row_spec(D_MODEL),
                  _full_spec((D_MODEL, D_MODEL)), _full_spec((PLE_DIM, D_MODEL)), _vec_spec(), _vec_spec(), _vec_spec()],
        out_specs=[_row_spec(D_MODEL), _row_spec(D_MODEL), _full_spec((D_MODEL, D_MODEL)), _full_spec((PLE_DIM, D_MODEL)),
                   _vec_spec(), _vec_spec(), _vec_spec(), _full_spec((1, 1))],
        out_shape=[jax.ShapeDtypeStruct((T, D_MODEL), F32), jax.ShapeDtypeStruct((T, D_MODEL), BF16),
                   jax.ShapeDtypeStruct((D_MODEL, D_MODEL), BF16), jax.ShapeDtypeStruct((PLE_DIM, D_MODEL), BF16)]
                  + [jax.ShapeDtypeStruct((1, D_MODEL), F32)] * 3 + [jax.ShapeDtypeStruct((1, 1), F32)],
        scratch_shapes=[pltpu.VMEM((D_MODEL, D_MODEL), F32), pltpu.VMEM((PLE_DIM, D_MODEL), F32)],
        args=(x2, h3, p, f, target, wgate, wproj, g_ple_post, g_ple, g_post_ffn), vmem=VMEM_BIG, job=job)


def _bwd_ffn_act(layer, df, gs, us, wgu, wd, job=None):
    T = df.shape[0]
    tm = min(FFN_ROW_TILE, T)
    nt = T // tm
    sub = tm // FFN_SUB_TILES
    last = FF_CHUNKS - 1
    wgu, wd = _column_views(wgu), _column_views(wd)
    n_gu, n_wd = len(wgu), len(wd)
    wd_cols = _column_ranges(wd)

    def body(df_ref, gs_ref, us_ref, *refs):
        wgu_refs, wd_refs = refs[:n_gu], refs[n_gu:n_gu + n_wd]
        dh_ref, dg_ref, du_ref, a_ref, acc_h = refs[n_gu + n_wd:]
        k = pl.program_id(0)
        i = pl.program_id(1)
        rows = pl.ds(pl.multiple_of(i * tm, tm), tm)
        dhs = []
        for s in range(FFN_SUB_TILES):
            r = pl.ds(s * sub, sub)
            g = gs_ref[r, :].astype(F32)
            u = us_ref[r, :].astype(F32)
            sg = _sigmoid(g)
            silu = g * sg
            a_ref[r, :] = (silu * u).astype(BF16)
            da = _add_all([_dot_nt(df_ref[r, c0:c1], w[...]) for (c0, c1), w in zip(wd_cols, wd_refs)])
            dg = (da * u * (sg * (1.0 + g * (1.0 - sg)))).astype(BF16)
            du = (da * silu).astype(BF16)
            dg_ref[r, :] = dg
            du_ref[r, :] = du
            dhs.append(jnp.concatenate([_dot(dg, w[0]) + _dot(du, w[1]) for w in wgu_refs], axis=1))
        dh = jnp.concatenate(dhs, axis=0)

        @pl.when(k == 0)
        def _():
            acc_h[rows, :] = dh

        @pl.when(jnp.logical_and(k > 0, k < last))
        def _():
            acc_h[rows, :] += dh

        @pl.when(k == last)
        def _():
            dh_ref[...] = acc_h[rows, :] + dh

    chunk_rows = pl.BlockSpec((None, tm, FF_BLOCK), lambda k, i: (k, i, 0))
    saved = jax.ShapeDtypeStruct((FF_CHUNKS, T, FF_BLOCK), BF16)
    return _launch(
        body, name=f"bwd_ffn_act{layer}", grid=(FF_CHUNKS, nt),
        in_specs=[pl.BlockSpec((tm, D_MODEL), lambda k, i: (i, 0)), chunk_rows, chunk_rows]
                 + [pl.BlockSpec((None, 2, FF_BLOCK, FFN_WEIGHT_COLS), lambda k, i, b=b: (k, 0, 0, b)) for _, b in wgu]
                 + [pl.BlockSpec((FF_BLOCK, FFN_WEIGHT_COLS), lambda k, i, b=b: (k, b)) for _, b in wd],
        out_specs=[pl.BlockSpec((tm, D_MODEL), lambda k, i: (jnp.where(k == last, i, 0), 0)),
                   chunk_rows, chunk_rows, chunk_rows],
        out_shape=[jax.ShapeDtypeStruct((T, D_MODEL), F32), saved, saved, saved],
        scratch_shapes=[pltpu.VMEM((T, D_MODEL), F32)],
        args=(df, gs, us, *[w for w, _ in wgu], *[w for w, _ in wd]), vmem=VMEM_BIG, job=job)


def _bwd_ffn_dw(layer, q, parts, h2, df, dg, du, a, job=None):
    T = h2.shape[0]
    width = D_MODEL // parts

    def body(h_ref, df_ref, dg_ref, du_ref, a_ref, dgu_ref, dwd_ref):
        h = h_ref[...]
        dgu_ref[0] = _dot_tn(dg_ref[...], h).astype(BF16)
        dgu_ref[1] = _dot_tn(du_ref[...], h).astype(BF16)
        dwd_ref[...] = _dot_tn(a_ref[...], df_ref[...]).astype(BF16)

    cols = pl.BlockSpec((T, width), lambda k: (0, q))
    chunk = pl.BlockSpec((None, T, FF_BLOCK), lambda k: (k, 0, 0))
    return _launch(
        body, name=f"bwd_ffn_dw{layer}_{q}", grid=(FF_CHUNKS,),
        in_specs=[cols, cols, chunk, chunk, chunk],
        out_specs=[pl.BlockSpec((None, 2, FF_BLOCK, width), lambda k: (k, 0, 0, 0)),
                   pl.BlockSpec((FF_BLOCK, width), lambda k: (k, 0))],
        out_shape=[jax.ShapeDtypeStruct((FF_CHUNKS, 2, FF_BLOCK, width), BF16),
                   jax.ShapeDtypeStruct((D_FF, width), BF16)],
        args=(h2, df, dg, du, a), vmem=VMEM_BIG, job=job)


def _bwd_attn_out(dx2, dh2, x1, y, attn, wo, g_ffn, g_post, job=None):
    T = x1.shape[0]
    nt = T // ROW_TILE

    def body(dx2_ref, dh2_ref, x1_ref, y_ref, a_ref, wo_ref, gffn_ref, gpost_ref,
             dx1_ref, da_ref, dwo_ref, dgf_ref, dgp_ref, acc):
        i = pl.program_id(0)
        first = i == 0
        dxn, dgf = _rms_bwd(x1_ref[...], gffn_ref[...], dh2_ref[...])
        dx1 = dx2_ref[...] + dxn
        dx1_ref[...] = dx1
        dy, dgp = _rms_bwd(y_ref[...].astype(F32), gpost_ref[...], dx1)
        dyb = dy.astype(BF16)
        da_ref[...] = _dot_nt(dyb, wo_ref[...]).astype(BF16)
        _acc(acc, _dot_tn(a_ref[...], dyb), first)
        _acc(dgf_ref, dgf, first)
        _acc(dgp_ref, dgp, first)

        @pl.when(i == nt - 1)
        def _():
            dwo_ref[...] = acc[...].astype(BF16)

    return _launch(
        body, name="bwd_attn_out", grid=(nt,),
        in_specs=[_row_spec(D_MODEL)] * 5 + [_full_spec((D_MODEL, D_MODEL)), _vec_spec(), _vec_spec()],
        out_specs=[_row_spec(D_MODEL), _row_spec(D_MODEL), _full_spec((D_MODEL, D_MODEL)), _vec_spec(), _vec_spec()],
        out_shape=[jax.ShapeDtypeStruct((T, D_MODEL), F32), jax.ShapeDtypeStruct((T, D_MODEL), BF16),
                   jax.ShapeDtypeStruct((D_MODEL, D_MODEL), BF16)] + [jax.ShapeDtypeStruct((1, D_MODEL), F32)] * 2,
        scratch_shapes=[pltpu.VMEM((D_MODEL, D_MODEL), F32)],
        args=(dx2, dh2, x1, y, attn, wo, g_ffn, g_post), job=job)


def _bwd_attention(q, dattn, kpad, vpad, sinks, job=None):
    T = q.shape[0]
    nb = T // ATT_BLOCK

    def body(q_ref, do_ref, k_ref, v_ref, sink_ref, dq_ref, dk_ref, dv_ref, ds_ref, s_scr, dp_scr, p_scr, dsb_scr,
             rel_scr, off_scr):
        n = pl.program_id(0)
        _att_mask(n, rel_scr, off_scr)

        @pl.when(n == 0)
        def _():
            dk_ref[...] = jnp.zeros_like(dk_ref)
            dv_ref[...] = jnp.zeros_like(dv_ref)
            ds_ref[...] = jnp.zeros_like(ds_ref)

        start = pl.multiple_of(n * ATT_BLOCK, ATT_BLOCK)
        win = pl.ds(start, 2 * ATT_BLOCK)
        kw = k_ref[win, :]
        vw = v_ref[win, :]
        lane = lax.broadcasted_iota(jnp.int32, (1, ATT_BLOCK), 1)
        dsink = jnp.zeros((1, ATT_BLOCK), F32)
        dqs, dks, dvs = [], [], []
        for kh in range(N_KV_HEADS):
            kk = kw[:, kh * HEAD_DIM:(kh + 1) * HEAD_DIM]
            vv = vw[:, kh * HEAD_DIM:(kh + 1) * HEAD_DIM]
            qs = _stack_heads(q_ref, kh)
            dos = _stack_heads(do_ref, kh)
            s_scr[...] = _dot_nt(qs, kk)
            dp_scr[...] = _dot_nt(dos, vv)
            for g in range(GQA_GROUP):
                h = kh * GQA_GROUP + g
                dsink_h = jnp.zeros((1, 1), F32)
                for row0 in range(0, ATT_BLOCK, ATT_SUB):
                    rows, sub = pl.ds(g * ATT_BLOCK + row0, ATT_SUB), pl.ds(row0, ATT_SUB)
                    pr, ps = _att_probs(s_scr[rows, :], rel_scr[sub, :], off_scr[sub, :], _alibi_slope(h),
                                        sink_ref[0, h])
                    dp = dp_scr[rows, :]
                    delta = jnp.sum(pr * dp, axis=-1, keepdims=True)
                    dsb_scr[rows, :] = (pr * (dp - delta) * ATT_SCALE).astype(BF16)
                    p_scr[rows, :] = pr.astype(BF16)
                    dsink_h = dsink_h - jnp.sum(ps * delta, axis=0, keepdims=True)
                dsink = dsink + jnp.where(lane == h, dsink_h, 0.0)
            dsb = dsb_scr[...]
            dqs += _unstack_heads(_dot(dsb, kk))
            dks.append(_dot_tn(dsb, qs))
            dvs.append(_dot_tn(p_scr[...], dos))
        dq_ref[...] = jnp.concatenate(dqs, axis=1).astype(BF16)
        dk_ref[win, :] += jnp.concatenate(dks, axis=1)
        dv_ref[win, :] += jnp.concatenate(dvs, axis=1)
        ds_ref[...] += dsink

    return _launch(
        body, name="bwd_attention", grid=(nb,),
        in_specs=[_row_spec(D_MODEL, ATT_BLOCK), _row_spec(D_MODEL, ATT_BLOCK), _full_spec((T + ATT_BLOCK, KV_DIM)),
                  _full_spec((T + ATT_BLOCK, KV_DIM)), pl.BlockSpec(memory_space=pltpu.SMEM)],
        out_specs=[_row_spec(D_MODEL, ATT_BLOCK), _full_spec((T + ATT_BLOCK, KV_DIM)), _full_spec((T + ATT_BLOCK, KV_DIM)),
                   _full_spec((1, ATT_BLOCK))],
        out_shape=[jax.ShapeDtypeStruct((T, D_MODEL), BF16), jax.ShapeDtypeStruct((T + ATT_BLOCK, KV_DIM), F32),
                   jax.ShapeDtypeStruct((T + ATT_BLOCK, KV_DIM), F32), jax.ShapeDtypeStruct((1, ATT_BLOCK), F32)],
        scratch_shapes=[pltpu.VMEM((ATT_GROUP_ROWS, 2 * ATT_BLOCK), F32)] * 2
                       + [pltpu.VMEM((ATT_GROUP_ROWS, 2 * ATT_BLOCK), BF16)] * 2
                       + [pltpu.VMEM((ATT_BLOCK, 2 * ATT_BLOCK), F32)] * 2,
        args=(q, dattn, kpad, vpad, sinks), vmem=VMEM_BIG, job=job)


def _bwd_qkv(dxres, dq, dkv, x3, h1, hk, wq, wkv, g_mix, g_kv, job=None):
    T = x3.shape[0]
    nt = T // ROW_TILE

    def body(dxr_ref, dq_ref, dkv_ref, x_ref, h1_ref, hk_ref, wq_ref, wkv_ref, gmix_ref, gkv_ref,
             dx_ref, dwq_ref, dwkv_ref, dgm_ref, dgk_ref, acc_q, acc_kv):
        i = pl.program_id(0)
        first = i == 0
        dqv = dq_ref[...]
        dkvv = dkv_ref[...]
        xv = x_ref[...]
        d1, dgm = _rms_bwd(xv, gmix_ref[...], _dot_nt(dqv, wq_ref[...]))
        d2, dgk = _rms_bwd(xv, gkv_ref[...], _dot_nt(dkvv, wkv_ref[...]))
        dx_ref[...] = dxr_ref[...] + d1 + d2
        _acc(acc_q, _dot_tn(h1_ref[...], dqv), first)
        _acc(acc_kv, _dot_tn(hk_ref[...], dkvv), first)
        _acc(dgm_ref, dgm, first)
        _acc(dgk_ref, dgk, first)

        @pl.when(i == nt - 1)
        def _():
            dwq_ref[...] = acc_q[...].astype(BF16)
            dwkv_ref[...] = acc_kv[...].astype(BF16)

    return _launch(
        body, name="bwd_qkv", grid=(nt,),
        in_specs=[_row_spec(D_MODEL), _row_spec(D_MODEL), _row_spec(2 * KV_DIM), _row_spec(D_MODEL), _row_spec(D_MODEL),
                  _row_spec(D_MODEL), _full_spec((D_MODEL, D_MODEL)), _full_spec((D_MODEL, 2 * KV_DIM)), _vec_spec(),
                  _vec_spec()],
        out_specs=[_row_spec(D_MODEL), _full_spec((D_MODEL, D_MODEL)), _full_spec((D_MODEL, 2 * KV_DIM)), _vec_spec(),
                   _vec_spec()],
        out_shape=[jax.ShapeDtypeStruct((T, D_MODEL), F32), jax.ShapeDtypeStruct((D_MODEL, D_MODEL), BF16),
                   jax.ShapeDtypeStruct((D_MODEL, 2 * KV_DIM), BF16)] + [jax.ShapeDtypeStruct((1, D_MODEL), F32)] * 2,
        scratch_shapes=[pltpu.VMEM((D_MODEL, D_MODEL), F32), pltpu.VMEM((D_MODEL, 2 * KV_DIM), F32)],
        args=(dxres, dq, dkv, x3, h1, hk, wq, wkv, g_mix, g_kv), job=job)


def _bwd_pool_mixer(dx2, dh2, x1, x, yraw, d, wp, scale, g_ffn, g_post, g_pre, job=None):
    T = x.shape[0]
    tm = ROW_TILE
    nt = T // tm

    def body(dx2_ref, dh2_ref, x1_ref, x_ref, yraw_ref, d_ref, wp_ref, sc_ref, gffn_ref, gpost_ref, gpre_ref,
             dx_ref, dwp_ref, dsc_ref, dgf_ref, dgp_ref, dgm_ref, carry, acc):
        i = pl.program_id(0)
        first = i == 0
        tile = nt - 1 - i

        @pl.when(first)
        def _():
            carry[...] = jnp.zeros_like(carry)

        dxn, dgf = _rms_bwd(x1_ref[...], gffn_ref[...], dh2_ref[...])
        dx1 = dx2_ref[...] + dxn
        yraw = yraw_ref[...].astype(F32)
        sc = sc_ref[...]
        dy, dgp = _rms_bwd(yraw * sc, gpost_ref[...], dx1)
        dsc = jnp.sum(dy * yraw, axis=0, keepdims=True)
        dyb = (dy * sc).astype(BF16)
        dv = d_ref[...]
        dds = []
        for g in range(N_POOL_GROUPS):
            cols = slice(g * POOL_GROUP, (g + 1) * POOL_GROUP)
            dds.append(_dot_nt(dyb[:, cols], wp_ref[g]))
            _acc(acc.at[g], _dot_tn(dv[:, cols], dyb[:, cols]), first)
        dd = jnp.concatenate(dds, axis=1)
        e = dd / _pool_counts(tile * tm, tm)
        ext = jnp.concatenate([e, carry[...]], axis=0)
        carry[...] = e[:POOL_HALO, :]
        sums = _window_sums(ext, lambda k: tm + POOL_HALO - k)[:tm, :]
        dxm, dgm = _rms_bwd(x_ref[...], gpre_ref[...], sums - dd)
        dx_ref[...] = dx1 + dxm
        _acc(dsc_ref, dsc, first)
        _acc(dgf_ref, dgf, first)
        _acc(dgp_ref, dgp, first)
        _acc(dgm_ref, dgm, first)

        @pl.when(i == nt - 1)
        def _():
            dwp_ref[...] = acc[...].astype(BF16)

    rev = pl.BlockSpec((tm, D_MODEL), lambda i: (nt - 1 - i, 0))
    return _launch(
        body, name="bwd_pool_mixer", grid=(nt,),
        in_specs=[rev] * 6 + [_full_spec((N_POOL_GROUPS, POOL_GROUP, POOL_GROUP))] + [_vec_spec()] * 4,
        out_specs=[rev, _full_spec((N_POOL_GROUPS, POOL_GROUP, POOL_GROUP))] + [_vec_spec()] * 4,
        out_shape=[jax.ShapeDtypeStruct((T, D_MODEL), F32),
                   jax.ShapeDtypeStruct((N_POOL_GROUPS, POOL_GROUP, POOL_GROUP), BF16)]
                  + [jax.ShapeDtypeStruct((1, D_MODEL), F32)] * 4,
        scratch_shapes=[pltpu.VMEM((POOL_HALO, D_MODEL), F32), pltpu.VMEM((N_POOL_GROUPS, POOL_GROUP, POOL_GROUP), F32)],
        args=(dx2, dh2, x1, x, yraw, d, wp, scale, g_ffn, g_post, g_pre), job=job)


def _my_place():
    return lax.axis_index("x"), lax.axis_index("y"), lax.axis_index("c")


def _dev_index(px, py, pc):
    return 4 * px + 2 * py + pc


def _peer_by_relation(r):
    x, y, c = _my_place()
    return (x ^ ((r >> 2) & 1), y ^ ((r >> 1) & 1), c ^ (r & 1))


def _slot_pool(ref, j):
    return ref.at[:, pl.ds(pl.multiple_of(j * 32, 32), 32), :]


def _slot_scale(ref, j):
    return ref.at[:, pl.ds(pl.multiple_of(j * 128, 128), 128)]


def _slot_rows128(ref, j):
    return ref.at[pl.ds(pl.multiple_of(j * 128, 128), 128), :]


def _slot_gu(ref, j):
    return ref.at[j % FF_CHUNKS, j // FF_CHUNKS]


def _slot_wd(ref, j):
    return ref.at[pl.ds(pl.multiple_of(j * WD_ROWS, 16), WD_ROWS), :]


def _slot_cols128(ref, j):
    return ref.at[:, pl.ds(pl.multiple_of(j * 128, 128), 128)]


_GATHERED = {
    "pool": ((N_POOL_GROUPS, POOL_GROUP, POOL_GROUP), BF16, _slot_pool),
    "scale": ((1, D_MODEL), F32, _slot_scale),
    "kv": ((D_MODEL, 2 * KV_DIM), BF16, _slot_rows128),
    "q": ((D_MODEL, D_MODEL), BF16, _slot_rows128),
    "o": ((D_MODEL, D_MODEL), BF16, _slot_rows128),
    "gu": ((FF_CHUNKS, 2, FF_BLOCK, D_MODEL), BF16, _slot_gu),
    "wd": ((D_FF, D_MODEL), BF16, _slot_wd),
    "guh": ((FF_CHUNKS, 2, FF_BLOCK, D_MODEL // 2), BF16, _slot_gu),
    "wdh": ((D_FF, D_MODEL // 2), BF16, _slot_wd),
    "gate": ((D_MODEL, D_MODEL), BF16, _slot_rows128),
    "proj": ((PLE_DIM, D_MODEL), BF16, _slot_cols128),
}


def _no_compute():
    pass


class _AllGather:
    peers = ("sibling", "x", "y")

    def __init__(self, names, shards):
        self.kinds = [_GATHERED[n.rstrip("01_")] for n in names]
        entries = [shards[n] if isinstance(shards[n], tuple) else (shards[n], None) for n in names]
        self.args = [array for array, _ in entries]
        self.columns = [columns for _, columns in entries]
        self.out_shape = [jax.ShapeDtypeStruct(shape, dtype) for shape, dtype, _ in self.kinds]
        n = len(names)
        self.scratch = [pltpu.SemaphoreType.DMA((n, 7)), pltpu.SemaphoreType.DMA((n, 7)), pltpu.SemaphoreType.DMA((n,))]

    def _plan(self, srcs, outs, sems):
        send_sems, recv_sems, local_sems = sems
        x, y, c = _my_place()

        def slot(t, dev):
            return self.kinds[t][2](outs[t], _dev_index(*dev))

        def copy(t, k, block, to, src=None):
            return pltpu.make_async_remote_copy(
                src_ref=slot(t, block) if src is None else src, dst_ref=slot(t, block),
                send_sem=send_sems.at[t, k], recv_sem=recv_sems.at[t, k], device_id=to, device_id_type=MESH)

        return types.SimpleNamespace(
            copy=copy, core=c, me=(x, y, c), sibling=(x, y, 1 - c),
            x_chip=(1 - x, y), y_chip=(x, 1 - y), far_chip=(1 - x, 1 - y),
            via=(x ^ (1 - c), y ^ c),
            onto=(x ^ c, y ^ (1 - c)),
            k_via=1 + c, k_onto=2 - c,
            local=[pltpu.make_async_copy(self._shard(srcs, t), slot(t, (x, y, c)), local_sems.at[t])
                   for t in range(len(srcs))])

    def _shard(self, srcs, t):
        if self.columns[t] is None:
            return srcs[t]
        first, end = self.columns[t]
        return srcs[t].at[:, first:end]

    def start(self, srcs, outs, sems):
        p = self._plan(srcs, outs, sems)
        for cp in p.local:
            cp.start()
        for t in range(len(srcs)):
            shard = self._shard(srcs, t)
            p.copy(t, 0, p.me, p.sibling, src=shard).start()
            p.copy(t, 1, p.me, (*p.x_chip, p.core), src=shard).start()
            p.copy(t, 2, p.me, (*p.y_chip, p.core), src=shard).start()

    def mid(self, srcs, outs, sems):
        p = self._plan(srcs, outs, sems)
        for t in range(len(srcs)):
            block = (*p.via, p.core)
            p.copy(t, p.k_via, block, p.me).wait_recv()
            p.copy(t, 3, block, (*p.onto, p.core)).start()
            p.copy(t, 3 + p.k_via, block, p.sibling).start()

    def late(self, srcs, outs, sems):
        p = self._plan(srcs, outs, sems)
        n = len(srcs)
        for t in range(n):
            block = (*p.onto, p.core)
            p.copy(t, p.k_onto, block, p.me).wait_recv()
            p.copy(t, 3 + p.k_onto, block, p.sibling).start()
        for t in range(n):
            block = (*p.far_chip, p.core)
            p.copy(t, 3, block, p.me).wait_recv()
            p.copy(t, 6, block, p.sibling).start()

    def finish(self, srcs, outs, sems):
        p = self._plan(srcs, outs, sems)
        n = len(srcs)
        other = 1 - p.core
        for t in range(n):
            p.copy(t, 0, (*p.me[:2], other), p.me).wait_recv()
            for k, chip in ((4, p.x_chip), (5, p.y_chip), (6, p.far_chip)):
                p.copy(t, k, (*chip, other), p.me).wait_recv()
            for k in range(7):
                p.copy(t, k, p.me, p.sibling).wait_send()
        for cp in p.local:
            cp.wait()


def _jobs_only(name, job=None):
    return _launch(_no_compute, name=name, grid=(), in_specs=[], out_specs=[], out_shape=[], args=(), job=job)


def _block_pool(ref, j):
    return ref.at[:, pl.ds(pl.multiple_of(j * 32, 32), 32), :]


def _block_rows128(ref, j):
    return ref.at[pl.ds(pl.multiple_of(j * 128, 128), 128), :]


def _block_gu(ref, j):
    return ref.at[j % FF_CHUNKS, j // FF_CHUNKS]


def _block_wd(ref, j):
    return ref.at[pl.ds(pl.multiple_of(j * WD_ROWS, 16), WD_ROWS), :]


def _block_cols128(ref, j):
    return ref.at[:, pl.ds(pl.multiple_of(j * 128, 128), 128)]


_SCATTERED = {
    "pool": ((N_POOL_GROUPS, 32, POOL_GROUP), _block_pool),
    "kv": ((128, 2 * KV_DIM), _block_rows128),
    "q": ((128, D_MODEL), _block_rows128),
    "o": ((128, D_MODEL), _block_rows128),
    "gu": ((FF_BLOCK, FF_PART), _block_gu),
    "wd": ((WD_ROWS, FF_PART), _block_wd),
    "guA": ((FF_BLOCK, FF_PART), lambda ref, j: _block_gu(ref, j).at[:, :FF_PART]),
    "guB": ((FF_BLOCK, FF_PART), lambda ref, j: _block_gu(ref, j).at[:, FF_PART:]),
    "wdA": ((WD_ROWS, FF_PART), lambda ref, j: _block_wd(ref, j).at[:, :FF_PART]),
    "wdB": ((WD_ROWS, FF_PART), lambda ref, j: _block_wd(ref, j).at[:, FF_PART:]),
    "gate": ((128, D_MODEL), _block_rows128),
    "proj": ((PLE_DIM, 128), _block_cols128),
}


class _SiblingSwap:
    peers = ("sibling",)

    def __init__(self, pieces):
        self.kinds = [_SCATTERED[kind] for kind, _ in pieces]
        self.args = [g for _, g in pieces]
        self.out_shape = [jax.ShapeDtypeStruct((N_CHIPS, *block), BF16) for block, _ in self.kinds]
        n = len(pieces)
        self.scratch = [pltpu.SemaphoreType.DMA((n, N_CHIPS)), pltpu.SemaphoreType.DMA((n, N_CHIPS))]

    def _copies(self, srcs, outs, sems):
        send_sems, recv_sems = sems
        x, y, c = _my_place()
        return [pltpu.make_async_remote_copy(
            src_ref=block(srcs[t], 2 * ch + 1 - c), dst_ref=outs[t].at[ch], send_sem=send_sems.at[t, ch],
            recv_sem=recv_sems.at[t, ch], device_id=(x, y, 1 - c), device_id_type=MESH)
            for t, (_, block) in enumerate(self.kinds) for ch in range(N_CHIPS)]

    def start(self, srcs, outs, sems):
        for cp in self._copies(srcs, outs, sems):
            cp.start()

    def finish(self, srcs, outs, sems):
        for cp in self._copies(srcs, outs, sems):
            cp.wait()


class _ChipScatter:
    N_BUFS = 4
    peers = ("x", "y")

    def __init__(self, pieces):
        self.kinds = [_SCATTERED[kind] for kind, _, _ in pieces]
        self.n = n = len(pieces)
        self.args = [g for _, g, _ in pieces] + [s for _, _, s in pieces]
        self.out_shape = [jax.ShapeDtypeStruct((2, *block), BF16) for block, _ in self.kinds]
        self.scratch = []
        for block, _ in self.kinds:
            self.scratch += [pltpu.VMEM((N_CHIPS, *block), BF16)] * 3 + [pltpu.VMEM((2, *block), BF16)]
        dma = pltpu.SemaphoreType.DMA
        self.scratch += [dma((n, N_CHIPS + 1)), dma((n, 2)), dma((n, 2)), dma((n,)), dma((n,)), dma((n,))]

    def _plan(self, outs, scr):
        n = self.n
        first_send, first_recv, second_send, second_recv, keep_sems = scr[self.N_BUFS * n + 1:]
        x, y, c = _my_place()
        via = (x ^ (1 - c), y ^ c)
        onto = (x ^ c, y ^ (1 - c))
        index = lambda chip: 2 * chip[0] + chip[1]
        first, second, keep = [], [], []
        for t in range(n):
            total, inbox = scr[self.N_BUFS * t + 2], scr[self.N_BUFS * t + 3]
            for k, chip in enumerate((via, (1 - x, 1 - y))):
                first.append(pltpu.make_async_remote_copy(
                    src_ref=total.at[index(chip)], dst_ref=inbox.at[k], send_sem=first_send.at[t, k],
                    recv_sem=first_recv.at[t, k], device_id=(*via, c), device_id_type=MESH))
            second.append(pltpu.make_async_remote_copy(
                src_ref=total.at[index(onto)], dst_ref=outs[t].at[1], send_sem=second_send.at[t],
                recv_sem=second_recv.at[t], device_id=(*onto, c), device_id_type=MESH))
            keep.append(pltpu.make_async_copy(total.at[index((x, y))], outs[t].at[0], keep_sems.at[t]))
        return first, second, keep, index((x, y)), index(onto)

    def start(self, ins, outs, scr):
        n = self.n
        load_sems = scr[self.N_BUFS * n]
        c = lax.axis_index("c")
        loads = []
        for t, (_, block) in enumerate(self.kinds):
            mine, theirs = scr[self.N_BUFS * t], scr[self.N_BUFS * t + 1]
            loads += [pltpu.make_async_copy(block(ins[t], 2 * ch + c), mine.at[ch], load_sems.at[t, ch])
                      for ch in range(N_CHIPS)]
            loads.append(pltpu.make_async_copy(ins[n + t], theirs, load_sems.at[t, N_CHIPS]))
        for cp in loads:
            cp.start()
        for cp in loads:
            cp.wait()
        for t in range(n):
            mine, theirs, total = scr[self.N_BUFS * t:self.N_BUFS * t + 3]
            for ch in range(N_CHIPS):
                total[ch] = (mine[ch].astype(F32) + theirs[ch].astype(F32)).astype(BF16)
        for cp in self._plan(outs, scr)[0]:
            cp.start()

    def mid(self, ins, outs, scr):
        first, second, keep, me, onto = self._plan(outs, scr)
        for cp in first:
            cp.wait_recv()
        for t in range(self.n):
            total, inbox = scr[self.N_BUFS * t + 2], scr[self.N_BUFS * t + 3]
            for k, slot in enumerate((me, onto)):
                total[slot] = (total[slot].astype(F32) + inbox[k].astype(F32)).astype(BF16)
        for cp in second + keep:
            cp.start()

    def finish(self, ins, outs, scr):
        first, second, keep, _, _ = self._plan(outs, scr)
        for cp in first:
            cp.wait_send()
        for cp in second + keep:
            cp.wait()


class _SwapAndScatter:
    peers = ("sibling", "x", "y")

    def __init__(self, pieces):
        self.n = len(pieces)
        self.swap = _SiblingSwap(pieces)
        self.scatter = _ChipScatter([(kind, g, None) for kind, g in pieces])
        self.args = self.swap.args
        self.out_shape = self.swap.out_shape + self.scatter.out_shape
        self.scratch = self.swap.scratch + self.scatter.scratch
        self.n_swap_scratch = len(self.swap.scratch)

    def _of_swap(self, ins, outs, scr):
        return ins, outs[:self.n], scr[:self.n_swap_scratch]

    def _of_scatter(self, ins, outs, scr):
        return list(ins) + list(outs[:self.n]), outs[self.n:], scr[self.n_swap_scratch:]

    def start(self, ins, outs, scr):
        self.swap.start(*self._of_swap(ins, outs, scr))

    def mid(self, ins, outs, scr):
        self.swap.finish(*self._of_swap(ins, outs, scr))
        self.scatter.start(*self._of_scatter(ins, outs, scr))

    def late(self, ins, outs, scr):
        self.scatter.mid(*self._of_scatter(ins, outs, scr))

    def finish(self, ins, outs, scr):
        self.scatter.finish(*self._of_scatter(ins, outs, scr))


class _ToEveryone:
    peers = _EVERYONE

    def __init__(self, scattered=(), gathered=()):
        self.blocks = [_SCATTERED[kind][1] for kind, _ in scattered] + [None] * len(gathered)
        self.args = [g for _, g in scattered] + list(gathered)
        self.out_shape = [jax.ShapeDtypeStruct((N_DEV, *_SCATTERED[kind][0]), BF16) for kind, _ in scattered]
        self.out_shape += [jax.ShapeDtypeStruct((N_DEV, *a.shape), a.dtype) for a in gathered]
        n = len(self.args)
        self.scratch = [pltpu.SemaphoreType.DMA((n, N_DEV - 1)), pltpu.SemaphoreType.DMA((n, N_DEV - 1)),
                        pltpu.SemaphoreType.DMA((n,))]

    def _copies(self, srcs, outs, sems):
        send_sems, recv_sems, local_sems = sems
        me = _dev_index(*_my_place())
        copies = []
        for t, block in enumerate(self.blocks):
            part = (lambda j, t=t, block=block: srcs[t] if block is None else block(srcs[t], j))
            copies.append(pltpu.make_async_copy(part(me), outs[t].at[me], local_sems.at[t]))
            for r in range(1, N_DEV):
                peer = _peer_by_relation(r)
                copies.append(pltpu.make_async_remote_copy(
                    src_ref=part(_dev_index(*peer)), dst_ref=outs[t].at[me], send_sem=send_sems.at[t, r - 1],
                    recv_sem=recv_sems.at[t, r - 1], device_id=peer, device_id_type=MESH))
        return copies

    def start(self, srcs, outs, sems):
        for cp in self._copies(srcs, outs, sems):
            cp.start()

    def finish(self, srcs, outs, sems):
        for cp in self._copies(srcs, outs, sems):
            cp.wait()


class _Jobs:
    def __init__(self, *jobs):
        self.jobs = jobs
        together = {p for j in jobs for p in j.peers}
        self.peers = tuple(p for p in _EVERYONE if p in together)
        self.args = [a for j in jobs for a in j.args]
        self.out_shape = [o for j in jobs for o in j.out_shape]
        self.scratch = [s for j in jobs for s in j.scratch]

    def _split(self, refs, attr):
        at = 0
        for j in self.jobs:
            n = len(getattr(j, attr))
            yield refs[at:at + n]
            at += n

    def _each(self, ins, outs, scr):
        return zip(self.jobs, self._split(ins, "args"), self._split(outs, "out_shape"), self._split(scr, "scratch"))

    def start(self, ins, outs, scr):
        for j, i, o, s in self._each(ins, outs, scr):
            j.start(i, o, s)

    def mid(self, ins, outs, scr):
        for j, i, o, s in self._each(ins, outs, scr):
            if hasattr(j, "mid"):
                j.mid(i, o, s)

    def late(self, ins, outs, scr):
        for j, i, o, s in self._each(ins, outs, scr):
            if hasattr(j, "late"):
                j.late(i, o, s)

    def finish(self, ins, outs, scr):
        for j, i, o, s in self._each(ins, outs, scr):
            j.finish(i, o, s)

    def split_outputs(self, outs):
        return list(self._split(outs, "out_shape"))


def _adamw_math(w, g, m, v):
    m = ADAM_B1 * m + (1.0 - ADAM_B1) * g
    v = ADAM_B2 * v + (1.0 - ADAM_B2) * (g * g)
    m_hat = m / (1.0 - ADAM_B1 ** ADAM_STEP)
    v_hat = v / (1.0 - ADAM_B2 ** ADAM_STEP)
    delta = -ADAM_LR * (m_hat / (jnp.sqrt(v_hat) + ADAM_EPS) + ADAM_WD * w)
    return delta, m, v


def _adamw(name, w, m, v, landings, n_col_blocks=1, job=None):
    n_slots, r, c = landings[0].shape
    grid = (w.shape[0] // r, n_col_blocks)

    def body(w_ref, m_ref, v_ref, *rest):
        l_refs, (g_ref, d_ref, nm_ref, nv_ref) = rest[:len(landings)], rest[len(landings):]
        step = pl.program_id(0) * n_col_blocks + pl.program_id(1)
        for idx, l_ref in enumerate(l_refs):
            @pl.when(step == idx)
            def _(l_ref=l_ref):
                g = l_ref[0].astype(F32)
                for s in range(1, n_slots):
                    g = g + l_ref[s].astype(F32)
                g_ref[...] = g
                d_ref[...], nm_ref[...], nv_ref[...] = _adamw_math(w_ref[...], g, m_ref[...], v_ref[...])

    spec = pl.BlockSpec((r, c), lambda a, b: (a, b))
    return _launch(
        body, name=f"adamw_{name}", grid=grid,
        in_specs=[spec, spec, spec] + [_full_spec((n_slots, r, c))] * len(landings),
        out_specs=[spec] * 4, out_shape=[jax.ShapeDtypeStruct(w.shape, F32)] * 4,
        args=(w, m, v, *landings), vmem=VMEM_BIG, job=job)


_SMALL = (("pre_mix_g", SV_PRE_MIX, 2), ("post_mix_g", SV_POST_MIX, 2), ("pre_ffn_g", SV_PRE_FFN, 2),
          ("post_ffn_g", SV_POST_FFN, 2), ("ple_g", SV_PLE, 2), ("ple_post_g", SV_PLE_POST, 2), ("kv_g", SV_KV, 1),
          ("pool_scale", SV_POOL_SCALE, 1), ("sinks", SV_SINKS, 1))


def _adamw_several(items):
    counts = [len(landings) for _, _, _, landings in items]
    args = [a for w, m, v, landings in items for a in (w, m, v, *landings)]
    out_shape = [jax.ShapeDtypeStruct(w.shape, F32) for w, _, _, _ in items for _ in range(4)]

    def body(*refs):
        ins, outs = refs[:len(args)], refs[len(args):]
        at = 0
        for idx, n_landings in enumerate(counts):
            w_ref, m_ref, v_ref = ins[at:at + 3]
            l_refs = ins[at + 3:at + 3 + n_landings]
            at += 3 + n_landings
            g_ref, d_ref, nm_ref, nv_ref = outs[4 * idx:4 * idx + 4]
            for part, l_ref in enumerate(l_refs):
                rows = slice(part * l_ref.shape[1], (part + 1) * l_ref.shape[1])
                g = l_ref[0].astype(F32)
                for s in range(1, l_ref.shape[0]):
                    g = g + l_ref[s].astype(F32)
                g_ref[rows, :] = g
                d_ref[rows, :], nm_ref[rows, :], nv_ref[rows, :] = _adamw_math(
                    w_ref[rows, :], g, m_ref[rows, :], v_ref[rows, :])

    res, _ = _launch(
        body, name="adamw_several", grid=(1,), in_specs=[_full_spec(a.shape) for a in args],
        out_specs=[_full_spec(s.shape) for s in out_shape], out_shape=out_shape, args=args)
    return [res[4 * idx:4 * idx + 4] for idx in range(len(items))]


def _small_adamw(slabs, params):
    flat = [a for name, _, _ in _SMALL for a in params[name]]
    n_in = 1 + len(flat)

    def body(*refs):
        slabs_ref, wmv = refs[0], refs[1:n_in]
        loss_ref, outs, total = refs[n_in], refs[n_in + 1:-1], refs[-1]
        me = _dev_index(*_my_place())
        g = slabs_ref[0]
        for s in range(1, N_DEV):
            g = g + slabs_ref[s]
        total[...] = g
        loss_ref[...] = total[SV_LOSS:SV_LOSS + 1, 0:1]
        for idx, (name, row, n_rows) in enumerate(_SMALL):
            w_ref, m_ref, v_ref = wmv[3 * idx:3 * idx + 3]
            g_ref, d_ref, nm_ref, nv_ref = outs[4 * idx:4 * idx + 4]
            if name == "pool_scale":
                g = total[row:row + 1, pl.ds(pl.multiple_of(me * 128, 128), 128)]
            else:
                g = total[row:row + n_rows, 0:w_ref.shape[1]]
            g_ref[...] = g
            d_ref[...], nm_ref[...], nv_ref[...] = _adamw_math(w_ref[...], g, m_ref[...], v_ref[...])

    out_shape = [jax.ShapeDtypeStruct((1, 1), F32)]
    for name, _, _ in _SMALL:
        out_shape += [jax.ShapeDtypeStruct(params[name][0].shape, F32)] * 4
    res, _ = _launch(
        body, name="small_adamw", grid=(1,),
        in_specs=[_full_spec(a.shape) for a in (slabs, *flat)], out_specs=[_full_spec(s.shape) for s in out_shape],
        out_shape=out_shape, scratch_shapes=[pltpu.VMEM((SV_ROWS, D_MODEL), F32)], args=(slabs, *flat))
    return res[0], {name: res[1 + 4 * idx:5 + 4 * idx] for idx, (name, _, _) in enumerate(_SMALL)}


def _local_step(x, p, tgt, gains, sinks, shards, weights):
    row = lambda first_row, layer: _Gain(gains, first_row + layer)
    gather = lambda *names: _AllGather(names, shards)
    g_pre_mix, g_post_mix, g_pre_ffn, g_post_ffn = SV_PRE_MIX, SV_POST_MIX, SV_PRE_FFN, SV_POST_FFN
    g_ple, g_ple_post, g_kv = SV_PLE, SV_PLE_POST, _Gain(gains, SV_KV)

    (dpool,), (wp, scale, wgu0, wd0) = _fwd_pool(x, row(g_pre_mix, 0), job=gather("pool", "scale", "gu0", "wd0"))
    wgu0, wd0 = [wgu0], [wd0]
    (x1_0, h2_0, yraw), _ = _fwd_pool_mixer(x, dpool, wp, scale, row(g_post_mix, 0), row(g_pre_ffn, 0))
    (gs0, us0, f0, x2_0, h3_0), (wgate0, wproj0, wkv, wq, wo, wd1_a) = _fwd_ffn(
        0, h2_0, x1_0, wgu0, wd0, row(g_post_ffn, 0), row(g_ple, 0),
        job=gather("gate0", "proj0", "kv", "q", "o", "wdh1_0"))
    (x3_0, z0, pe0, hk, h1, q, kv), (wgu1_a,) = _fwd_ple_qkv(
        x2_0, h3_0, p[0], wgate0, wproj0, row(g_ple_post, 0), g_kv, row(g_pre_mix, 1), wkv, wq,
        job=gather("guh1_0"))
    front = ((ATT_BLOCK, 0), (0, 0))
    kpad = jnp.pad(kv[:, :KV_DIM], front)
    vpad = jnp.pad(kv[:, KV_DIM:], front)
    (attn,), (wgu1_b,) = _fwd_attention(q, kpad, vpad, sinks, job=gather("guh1_1"))
    (y1, x1_1, h2_1), (wd1_b,) = _fwd_attn_out(attn, x3_0, wo, row(g_post_mix, 1), row(g_pre_ffn, 1),
                                               job=gather("wdh1_1"))
    wgu1, wd1 = [wgu1_a, wgu1_b], [wd1_a, wd1_b]
    (gs1, us1, f1, x2_1, h3_1), (wgate1, wproj1) = _fwd_ffn(
        1, h2_1, x1_1, wgu1, wd1, row(g_post_ffn, 1), row(g_ple, 1), job=gather("gate1", "proj1"))

    produced, swapped, landed = {}, {}, {}

    def kind_of(name):
        return name.rstrip("0123_")

    def hosted(call, *args, swap=(), spread=(), both=(), extra=None):
        jobs = []
        if swap:
            jobs.append(_SiblingSwap([(kind_of(n), produced[n]) for n in swap]))
        if spread:
            jobs.append(_ChipScatter([(kind_of(n), produced[n], swapped[n]) for n in spread]))
        if both:
            jobs.append(_SwapAndScatter([(kind_of(n), produced[n]) for n in both]))
        if extra is not None:
            jobs.append(extra)
        jobs = _Jobs(*jobs)
        outs, job_outs = call(*args, job=jobs)
        parts = jobs.split_outputs(job_outs)
        if swap:
            swapped.update(zip(swap, parts.pop(0)))
        if spread:
            landed.update(zip(spread, parts.pop(0)))
        if both:
            landed.update(zip(both, parts.pop(0)[len(both):]))
        return outs if extra is None else (outs, parts.pop(0))

    ffn_q = lambda layer, qtr: (f"gu{layer}_{qtr}", f"wd{layer}_{qtr}")

    dx2_1, df1, produced["gate1"], produced["proj1"], dg_ple_post1, dg_ple1, dg_post_ffn1, loss = hosted(
        _ple_loss_bwd, 1, x2_1, h3_1, p[1], f1, tgt, wgate1, wproj1, row(g_ple_post, 1), row(g_ple, 1),
        row(g_post_ffn, 1))
    dh2_1, dg1, du1, a1 = hosted(_bwd_ffn_act, 1, df1, gs1, us1, wgu1, wd1, swap=("gate1", "proj1"))
    dgu1, dwd1 = hosted(_bwd_ffn_dw, 1, 0, 1, h2_1, df1, dg1, du1, a1, spread=("gate1", "proj1"))
    produced.update(guA1=dgu1, guB1=dgu1, wdA1=dwd1, wdB1=dwd1)
    dx1_1, dattn, produced["o"], dg_pre_ffn1, dg_post_mix1 = hosted(
        _bwd_attn_out, dx2_1, dh2_1, x1_1, y1, attn, wo, row(g_pre_ffn, 1), row(g_post_mix, 1),
        swap=("guA1", "wdA1", "guB1", "wdB1"))
    dq, dkpad, dvpad, dsinks = hosted(_bwd_attention, q, dattn, kpad, vpad, sinks, spread=("guA1", "wdA1"))
    dkv = jnp.concatenate([dkpad[ATT_BLOCK:], dvpad[ATT_BLOCK:]], axis=1).astype(BF16)
    dx3_0, produced["q"], produced["kv"], dg_pre_mix1, dg_kv = hosted(
        _bwd_qkv, dx1_1, dq, dkv, x3_0, h1, hk, wq, wkv, row(g_pre_mix, 1), g_kv, swap=("o",), spread=("wdB1",))
    dx2_0, df0, produced["gate0"], produced["proj0"], dg_ple_post0, dg_ple0, dg_post_ffn0 = hosted(
        _bwd_ple, 0, dx3_0, x2_0, z0, pe0, h3_0, p[0], f0, wgate0, row(g_ple_post, 0), row(g_ple, 0),
        row(g_post_ffn, 0), swap=("q", "kv"), spread=("guB1",))
    for half, letter in enumerate("AB"):
        landed[f"gu1_{half}"], landed[f"wd1_{half}"] = landed[f"gu{letter}1"], landed[f"wd{letter}1"]
    dh2_0, dg0, du0, a0 = hosted(_bwd_ffn_act, 0, df0, gs0, us0, wgu0, wd0,
                                 swap=("gate0", "proj0"), spread=("o", "q", "kv"))
    part_hosts = [dict(spread=("gate0", "proj0")), dict(both=ffn_q(0, 0))]
    for part in range(FF_PARTS):
        produced[f"gu0_{part}"], produced[f"wd0_{part}"] = hosted(
            _bwd_ffn_dw, 0, part, FF_PARTS, h2_0, df0, dg0, du0, a0, **part_hosts[part])
    grad_x, produced["pool"], dscale, dg_pre_ffn0, dg_post_mix0, dg_pre_mix0 = hosted(
        _bwd_pool_mixer, dx2_0, dh2_0, x1_0, x, yraw, dpool, wp, scale, row(g_pre_ffn, 0), row(g_post_mix, 0),
        row(g_pre_mix, 0), both=ffn_q(0, 1))

    def update(name, n_col_blocks, pieces):
        w, m, v = weights[name]
        rows = w.size // w.shape[-1]
        flat = [landed[n].reshape(landed[n].shape[0], -1, landed[n].shape[-1]) for n in pieces]
        outs, _ = _adamw(name, w.reshape(rows, -1), m.reshape(rows, -1), v.reshape(rows, -1), flat, n_col_blocks)
        return [o.reshape(w.shape) for o in outs]

    upd = {}
    lanes = lambda a: jnp.pad(a, ((0, 0), (0, D_MODEL - a.shape[1])))
    small = jnp.concatenate([
        dg_pre_mix0, dg_pre_mix1, dg_post_mix0, dg_post_mix1, dg_pre_ffn0, dg_pre_ffn1, dg_post_ffn0, dg_post_ffn1,
        dg_ple0, dg_ple1, dg_ple_post0, dg_ple_post1, dg_kv, dscale, lanes(dsinks[:, :N_HEADS]), lanes(loss)], axis=0)

    everyone = _ToEveryone(scattered=[("pool", produced["pool"])], gathered=[small])
    _, (landed["pool"], slabs) = hosted(_jobs_only, "scatter_tail", extra=everyone)
    several = {"w_ple_gate": ("gate0", "gate1"), "w_ple_proj": ("proj0", "proj1"), "w_q": ("q",), "w_kv": ("kv",),
               "w_o": ("o",), "pool_w": ("pool",)}
    flat2d = lambda a: a.reshape(-1, a.shape[-1])
    results = _adamw_several([
        (*map(flat2d, weights[name]),
         [landed[n].reshape(landed[n].shape[0], -1, landed[n].shape[-1]) for n in pieces])
        for name, pieces in several.items()])
    for name, outs in zip(several, results):
        upd[name] = [o.reshape(weights[name][0].shape) for o in outs]
    upd["w_gu"] = update("w_gu", FF_PARTS,
                         pieces=[f"gu{layer}_{qtr}" for layer in range(2) for qtr in range(FF_PARTS)])
    upd["w_gu"] = [jnp.swapaxes(a, 1, 2) for a in upd["w_gu"]]
    upd["w_down"] = update("w_down", FF_PARTS,
                           pieces=[f"wd{layer}_{qtr}" for layer in range(2) for qtr in range(FF_PARTS)])
    return grad_x, upd, slabs


def kernel(x, p, pre_mix_g, post_mix_g, pre_ffn_g, post_ffn_g, pool_w, pool_scale, kv_g, w_kv, w_q, sinks, w_o, w_gu, w_down, ple_g, w_ple_gate, w_ple_proj, ple_post_g, loss_target, m_pre_mix_g, m_post_mix_g, m_pre_ffn_g, m_post_ffn_g, m_pool_w, m_pool_scale, m_kv_g, m_w_kv, m_w_q, m_sinks, m_w_o, m_w_gu, m_w_down, m_ple_g, m_w_ple_gate, m_w_ple_proj, m_ple_post_g, v_pre_mix_g, v_post_mix_g, v_pre_ffn_g, v_post_ffn_g, v_pool_w, v_pool_scale, v_kv_g, v_w_kv, v_w_q, v_sinks, v_w_o, v_w_gu, v_w_down, v_ple_g, v_w_ple_gate, v_w_ple_proj, v_ple_post_g):
    shards = {"pool": pool_w[0].astype(BF16), "scale": pool_scale, "kv": w_kv.astype(BF16),
              "q": w_q[0].astype(BF16), "o": w_o[0].astype(BF16)}
    for layer in range(2):
        shards[f"gu{layer}"] = w_gu[layer].T.astype(BF16)
        shards[f"wd{layer}"] = w_down[layer].astype(BF16)
        for half in range(2):
            cols = (half * D_MODEL // 2, (half + 1) * D_MODEL // 2)
            shards[f"guh{layer}_{half}"] = (shards[f"gu{layer}"], cols)
            shards[f"wdh{layer}_{half}"] = (shards[f"wd{layer}"], cols)
        shards[f"gate{layer}"] = w_ple_gate[layer].astype(BF16)
        shards[f"proj{layer}"] = w_ple_proj[layer].astype(BF16)
    gains = jnp.concatenate([pre_mix_g, post_mix_g, pre_ffn_g, post_ffn_g, ple_g, ple_post_g, kv_g[None, :]],
                            axis=0).reshape(-1, 1, D_MODEL)
    weights = {"pool_w": (pool_w, m_pool_w, v_pool_w), "w_kv": (w_kv, m_w_kv, v_w_kv), "w_q": (w_q, m_w_q, v_w_q),
               "w_o": (w_o, m_w_o, v_w_o), "w_down": (w_down, m_w_down, v_w_down),
               "w_gu": tuple(jnp.swapaxes(a, 1, 2) for a in (w_gu, m_w_gu, v_w_gu)),
               "w_ple_gate": (w_ple_gate, m_w_ple_gate, v_w_ple_gate),
               "w_ple_proj": (w_ple_proj, m_w_ple_proj, v_w_ple_proj)}
    per_layer = p.reshape(p.shape[0], *p.shape[2:])
    p_rows = [_LayerRows(per_layer, layer) for layer in range(2)]
    grad_x, upd, slabs = _local_step(x[0], p_rows, loss_target[0], gains, sinks, shards, weights)

    small_params = {
        "pre_mix_g": (pre_mix_g, m_pre_mix_g, v_pre_mix_g), "post_mix_g": (post_mix_g, m_post_mix_g, v_post_mix_g),
        "pre_ffn_g": (pre_ffn_g, m_pre_ffn_g, v_pre_ffn_g), "post_ffn_g": (post_ffn_g, m_post_ffn_g, v_post_ffn_g),
        "ple_g": (ple_g, m_ple_g, v_ple_g), "ple_post_g": (ple_post_g, m_ple_post_g, v_ple_post_g),
        "kv_g": (kv_g[None, :], m_kv_g[None, :], v_kv_g[None, :]),
        "pool_scale": (pool_scale, m_pool_scale, v_pool_scale), "sinks": (sinks, m_sinks, v_sinks)}
    loss, small_upd = _small_adamw(slabs, small_params)
    small_upd["kv_g"] = [a[0] for a in small_upd["kv_g"]]
    upd.update(small_upd)

    names = ["pre_mix_g", "post_mix_g", "pre_ffn_g", "post_ffn_g", "pool_w", "pool_scale", "kv_g", "w_kv", "w_q",
             "sinks", "w_o", "w_gu", "w_down", "ple_g", "w_ple_gate", "w_ple_proj", "ple_post_g"]
    outs = [loss[0, 0], grad_x[None]]
    for kind in range(4):
        outs += [upd[n][kind] for n in names]
    return tuple(outs)
```

```python
import functools
import types

import jax
import jax.numpy as jnp
from jax import lax
from jax.experimental import pallas as pl
from jax.experimental.pallas import tpu as pltpu

F32 = jnp.float32
BF16 = jnp.bfloat16

N_DEV = 8
D_MODEL = 1024
N_POOL_GROUPS = 4
POOL_GROUP = 256
POOL_HALO = 16
HEAD_DIM = 64
N_HEADS = 16
N_KV_HEADS = 4
GQA_GROUP = 4
KV_DIM = N_KV_HEADS * HEAD_DIM
ATT_BLOCK = 128
D_FF = 2816
FF_CHUNKS = 4
FF_BLOCK = D_FF // FF_CHUNKS
WD_ROWS = D_FF // N_DEV
FF_PARTS = 2
FF_PART = D_MODEL // FF_PARTS
N_CHIPS = 4
PLE_DIM = 256
EPS = 1e-6
NEG_INF = -1e30
ATT_SCALE = HEAD_DIM ** -0.5

ADAM_LR = 0.001
ADAM_B1 = 0.9
ADAM_B2 = 0.999
ADAM_EPS = 1e-08
ADAM_WD = 0.01
ADAM_STEP = 10

ROW_TILE = 512
FFN_ROW_TILE = 512
FFN_WEIGHT_COLS = 512
FFN_SUB_TILES = 1
VMEM_BIG = 60 * 1024 * 1024
VMEM_MID = 56 * 1024 * 1024
HBM_PIN_ELEMS = 1024

SV_ROWS = 16
SV_PRE_MIX, SV_POST_MIX, SV_PRE_FFN, SV_POST_FFN, SV_PLE, SV_PLE_POST = 0, 2, 4, 6, 8, 10
SV_KV, SV_POOL_SCALE, SV_SINKS, SV_LOSS = 12, 13, 14, 15

MESH = pl.DeviceIdType.MESH
ANY = pl.BlockSpec(memory_space=pl.ANY)


def _dot(a, b):
    return jnp.dot(a, b, preferred_element_type=F32)


def _dot_nt(a, b):
    return lax.dot_general(a, b, (((1,), (1,)), ((), ())), preferred_element_type=F32)


def _dot_tn(a, b):
    return lax.dot_general(a, b, (((0,), (0,)), ((), ())), preferred_element_type=F32)


def _rstd(x):
    return lax.rsqrt(jnp.mean(x * x, axis=-1, keepdims=True) + EPS)


def _rms(x, g):
    return x * _rstd(x) * g


def _rms_bwd(x, g, dy):
    r = _rstd(x)
    n = x * r
    dn = dy * g
    dx = r * (dn - n * jnp.mean(dn * n, axis=-1, keepdims=True))
    dg = jnp.sum(dy * n, axis=0, keepdims=True)
    return dx, dg


def _add_all(terms):
    return functools.reduce(jnp.add, terms)


def _sigmoid(x):
    return 1.0 / (1.0 + jnp.exp(-x))


def _acc(ref, val, first):
    @pl.when(first)
    def _():
        ref[...] = val

    @pl.when(jnp.logical_not(first))
    def _():
        ref[...] += val


def _pool_counts(row0, rows):
    t = row0 + lax.broadcasted_iota(jnp.int32, (rows, D_MODEL), 0) + 1
    grp = lax.broadcasted_iota(jnp.int32, (rows, D_MODEL), 1) // POOL_GROUP
    win = jnp.left_shift(2, grp)
    return jnp.minimum(t, win).astype(F32)


def _window_sums(ext, shift_of):
    outs = []
    s = ext
    for gi in range(N_POOL_GROUPS):
        s = s + pltpu.roll(s, shift_of(1 << gi), axis=0)
        outs.append(s[:, :POOL_GROUP])
        s = s[:, POOL_GROUP:]
    return jnp.concatenate(outs, axis=1)


def _cparams(n_axes, vmem, collective_id=None):
    return pltpu.CompilerParams(dimension_semantics=("arbitrary",) * n_axes, vmem_limit_bytes=vmem,
                                collective_id=collective_id)


_EVERYONE = ("sibling", "x", "y", "far", "x sibling", "y sibling", "far sibling")
_PEER_SETS = (("sibling", "x", "y"), ("sibling",), ("x", "y"), _EVERYONE)


def _meet(peers):
    x, y, c = lax.axis_index("x"), lax.axis_index("y"), lax.axis_index("c")
    device = {"sibling": (x, y, 1 - c), "x": (1 - x, y, c), "y": (x, 1 - y, c), "far": (1 - x, 1 - y, c),
              "x sibling": (1 - x, y, 1 - c), "y sibling": (x, 1 - y, 1 - c), "far sibling": (1 - x, 1 - y, 1 - c)}
    barrier = pltpu.get_barrier_semaphore()
    for peer in peers:
        pl.semaphore_signal(barrier, inc=1, device_id=device[peer], device_id_type=pl.DeviceIdType.MESH)
    pl.semaphore_wait(barrier, len(peers))


def _row_spec(cols, tm=ROW_TILE):
    return pl.BlockSpec((tm, cols), lambda i: (i, 0))


def _full_spec(shape):
    zeros = (0,) * len(shape)
    return pl.BlockSpec(shape, lambda *_: zeros)


def _vec_spec():
    return _full_spec((1, D_MODEL))


def _column_views(parts):
    return [(a, b) for a in parts for b in range(a.shape[-1] // FFN_WEIGHT_COLS)]


def _column_ranges(views):
    return [(n * FFN_WEIGHT_COLS, (n + 1) * FFN_WEIGHT_COLS) for n in range(len(views))]


class _Gain:
    def __init__(self, stacked, layer):
        self.stacked, self.layer = stacked, layer

    def spec(self):
        layer = self.layer
        return pl.BlockSpec((None, 1, D_MODEL), lambda *_: (layer, 0, 0))


class _LayerRows:
    def __init__(self, stacked, layer):
        self.stacked, self.layer = stacked, layer

    def spec(self):
        layer = self.layer
        return pl.BlockSpec((None, ROW_TILE, self.stacked.shape[-1]), lambda i: (layer, i, 0))


def _in_hbm(a):
    return pltpu.with_memory_space_constraint(a, pltpu.HBM) if a.size >= HBM_PIN_ELEMS else a


def _out_in_hbm(s):
    return pltpu.HBM(s.shape, s.dtype) if s.size >= HBM_PIN_ELEMS else s


def _launch(body, *, name, grid, in_specs, out_specs, out_shape, args, scratch_shapes=(), vmem=VMEM_MID, job=None):
    picked = (_Gain, _LayerRows)
    in_specs = [a.spec() if isinstance(a, picked) else s for s, a in zip(in_specs, args)]
    args = [_in_hbm(a.stacked if isinstance(a, picked) else a) for a in args]
    n_in, n_out, n_scr = len(args), len(out_shape), len(scratch_shapes)
    if job is not None and not job.args:
        job = None
    j_args, j_out, j_scr = ([], [], []) if job is None else ([_in_hbm(a) for a in job.args], job.out_shape, job.scratch)

    def run(*refs):
        groups, at = [], 0
        for n in (n_in, len(j_args), n_out, len(j_out), n_scr, len(j_scr)):
            groups.append(refs[at:at + n])
            at += n
        ins, j_ins, outs, j_outs, scr, j_sems = groups

        def begin():
            _meet(job.peers)
            job.start(j_ins, j_outs, j_sems)

        if job is None:
            body(*ins, *outs, *scr)
        elif not grid:
            begin()
            job.mid(j_ins, j_outs, j_sems)
            job.late(j_ins, j_outs, j_sems)
            body(*ins, *outs, *scr)
            job.finish(j_ins, j_outs, j_sems)
        else:
            ids = [pl.program_id(a) for a in range(len(grid))]
            at_start = lambda step: functools.reduce(jnp.logical_and, [ids[0] == step] + [i == 0 for i in ids[1:]])
            last = functools.reduce(jnp.logical_and, [i == g - 1 for i, g in zip(ids, grid)])
            pl.when(at_start(0))(begin)
            pl.when(at_start(grid[0] // 2))(lambda: job.mid(j_ins, j_outs, j_sems))
            pl.when(at_start(3 * grid[0] // 4))(lambda: job.late(j_ins, j_outs, j_sems))
            body(*ins, *outs, *scr)
            pl.when(last)(lambda: job.finish(j_ins, j_outs, j_sems))

    res = pl.pallas_call(
        run, name=name, grid=grid,
        in_specs=list(in_specs) + [ANY] * len(j_args), out_specs=list(out_specs) + [ANY] * len(j_out),
        out_shape=[_out_in_hbm(s) for s in list(out_shape) + list(j_out)],
        scratch_shapes=list(scratch_shapes) + list(j_scr),
        compiler_params=_cparams(len(grid), vmem, None if job is None else _PEER_SETS.index(job.peers)),
    )(*args, *j_args)
    return res[:n_out], res[n_out:]


def _fwd_pool(x, g_pre, job=None):
    T = x.shape[0]
    tm = ROW_TILE
    nt = T // tm

    def body(x_ref, gpre_ref, d_ref, carry):
        i = pl.program_id(0)

        @pl.when(i == 0)
        def _():
            carry[...] = jnp.zeros_like(carry)

        h = _rms(x_ref[...], gpre_ref[...])
        ext = jnp.concatenate([carry[...], h], axis=0)
        carry[...] = h[tm - POOL_HALO:, :]
        sums = _window_sums(ext, lambda k: k)[POOL_HALO:, :]
        d_ref[...] = (sums / _pool_counts(i * tm, tm) - h).astype(BF16)

    return _launch(
        body, name="fwd_pool", grid=(nt,), in_specs=[_row_spec(D_MODEL), _vec_spec()], out_specs=[_row_spec(D_MODEL)],
        out_shape=[jax.ShapeDtypeStruct((T, D_MODEL), BF16)], scratch_shapes=[pltpu.VMEM((POOL_HALO, D_MODEL), F32)],
        args=(x, g_pre), job=job)


def _fwd_pool_mixer(x, d, wp, scale, g_post, g_ffn, job=None):
    T = x.shape[0]
    nt = T // ROW_TILE

    def body(x_ref, d_ref, wp_ref, sc_ref, gpost_ref, gffn_ref, x1_ref, h2_ref, yraw_ref):
        db = d_ref[...]
        yraw = jnp.concatenate(
            [_dot(db[:, g * POOL_GROUP:(g + 1) * POOL_GROUP], wp_ref[g]) for g in range(N_POOL_GROUPS)], axis=1)
        yraw_ref[...] = yraw.astype(BF16)
        x1 = x_ref[...] + _rms(yraw * sc_ref[...], gpost_ref[...])
        x1_ref[...] = x1
        h2_ref[...] = _rms(x1, gffn_ref[...]).astype(BF16)

    return _launch(
        body, name="fwd_pool_mixer", grid=(nt,),
        in_specs=[_row_spec(D_MODEL), _row_spec(D_MODEL), _full_spec((N_POOL_GROUPS, POOL_GROUP, POOL_GROUP)),
                  _vec_spec(), _vec_spec(), _vec_spec()],
        out_specs=[_row_spec(D_MODEL)] * 3,
        out_shape=[jax.ShapeDtypeStruct((T, D_MODEL), F32)] + [jax.ShapeDtypeStruct((T, D_MODEL), BF16)] * 2,
        args=(x, d, wp, scale, g_post, g_ffn), job=job)


def _fwd_ffn(layer, h2, x1, wgu, wd, g_post, g_ple, job=None):
    T = h2.shape[0]
    tm = min(FFN_ROW_TILE, T)
    nt = T // tm
    sub = tm // FFN_SUB_TILES
    last = FF_CHUNKS - 1
    wgu, wd = _column_views(wgu), _column_views(wd)
    n_gu, n_wd = len(wgu), len(wd)
    gu_cols = _column_ranges(wgu)

    def body(h2_ref, x1_ref, *refs):
        wgu_refs, wd_refs = refs[:n_gu], refs[n_gu:n_gu + n_wd]
        gpost_ref, gple_ref, gs_ref, us_ref, f_ref, x2_ref, h3_ref, acc = refs[n_gu + n_wd:]
        k = pl.program_id(0)
        i = pl.program_id(1)
        rows = pl.ds(pl.multiple_of(i * tm, tm), tm)
        parts = []
        for s in range(FFN_SUB_TILES):
            r = pl.ds(s * sub, sub)
            g = _add_all([_dot_nt(h2_ref[r, c0:c1], w[0]) for (c0, c1), w in zip(gu_cols, wgu_refs)])
            u = _add_all([_dot_nt(h2_ref[r, c0:c1], w[1]) for (c0, c1), w in zip(gu_cols, wgu_refs)])
            gs_ref[r, :] = g.astype(BF16)
            us_ref[r, :] = u.astype(BF16)
            a = (g * _sigmoid(g) * u).astype(BF16)
            parts.append(jnp.concatenate([_dot(a, w[...]) for w in wd_refs], axis=1))
        part = jnp.concatenate(parts, axis=0)

        @pl.when(k == 0)
        def _():
            acc[rows, :] = part

        @pl.when(jnp.logical_and(k > 0, k < last))
        def _():
            acc[rows, :] += part

        @pl.when(k == last)
        def _():
            f = acc[rows, :] + part
            f_ref[...] = f.astype(BF16)
            x2 = x1_ref[...] + _rms(f, gpost_ref[...])
            x2_ref[...] = x2
            h3_ref[...] = _rms(x2, gple_ref[...]).astype(BF16)

    def late(k, i):
        return (jnp.where(k == last, i, 0), 0)

    return _launch(
        body, name=f"fwd_ffn{layer}", grid=(FF_CHUNKS, nt),
        in_specs=[pl.BlockSpec((tm, D_MODEL), lambda k, i: (i, 0)), pl.BlockSpec((tm, D_MODEL), late)]
                 + [pl.BlockSpec((None, 2, FF_BLOCK, FFN_WEIGHT_COLS), lambda k, i, b=b: (k, 0, 0, b)) for _, b in wgu]
                 + [pl.BlockSpec((FF_BLOCK, FFN_WEIGHT_COLS), lambda k, i, b=b: (k, b)) for _, b in wd]
                 + [pl.BlockSpec((1, D_MODEL), lambda k, i: (0, 0))] * 2,
        out_specs=[pl.BlockSpec((None, tm, FF_BLOCK), lambda k, i: (k, i, 0)),
                   pl.BlockSpec((None, tm, FF_BLOCK), lambda k, i: (k, i, 0)),
                   pl.BlockSpec((tm, D_MODEL), late),
                   pl.BlockSpec((tm, D_MODEL), late),
                   pl.BlockSpec((tm, D_MODEL), late)],
        out_shape=[jax.ShapeDtypeStruct((FF_CHUNKS, T, FF_BLOCK), BF16),
                   jax.ShapeDtypeStruct((FF_CHUNKS, T, FF_BLOCK), BF16),
                   jax.ShapeDtypeStruct((T, D_MODEL), BF16),
                   jax.ShapeDtypeStruct((T, D_MODEL), F32),
                   jax.ShapeDtypeStruct((T, D_MODEL), BF16)],
        scratch_shapes=[pltpu.VMEM((T, D_MODEL), F32)],
        args=(h2, x1, *[w for w, _ in wgu], *[w for w, _ in wd], g_post, g_ple), vmem=VMEM_BIG, job=job)


def _fwd_ple_qkv(x2, h3, p, wgate, wproj, g_post, g_kv, g_mix, wkv, wq, job=None):
    T = x2.shape[0]
    nt = T // ROW_TILE

    def body(x2_ref, h3_ref, p_ref, wg_ref, wp_ref, gpost_ref, gkv_ref, gmix_ref, wkv_ref, wq_ref,
             x3_ref, z_ref, pe_ref, hk_ref, h1_ref, q_ref, kpad_ref, vpad_ref):
        z = _dot(h3_ref[...], wg_ref[...])
        pe = _dot(p_ref[...].astype(BF16), wp_ref[...])
        z_ref[...] = z.astype(BF16)
        pe_ref[...] = pe.astype(BF16)
        x3 = x2_ref[...] + _rms(pe * _sigmoid(z), gpost_ref[...])
        x3_ref[...] = x3
        r = _rstd(x3)
        hk = (x3 * r * gkv_ref[...]).astype(BF16)
        h1 = (x3 * r * gmix_ref[...]).astype(BF16)
        hk_ref[...] = hk
        h1_ref[...] = h1
        kv = _dot(hk, wkv_ref[...]).astype(BF16)
        q_ref[...] = _dot(h1, wq_ref[...]).astype(BF16)
        i = pl.program_id(0)

        @pl.when(i == 0)
        def _():
            kpad_ref[:ATT_BLOCK, :] = jnp.zeros((ATT_BLOCK, KV_DIM), BF16)
            vpad_ref[:ATT_BLOCK, :] = jnp.zeros((ATT_BLOCK, KV_DIM), BF16)

        rows = pl.ds(pl.multiple_of(ATT_BLOCK + i * ROW_TILE, ATT_BLOCK), ROW_TILE)
        kpad_ref[rows, :] = kv[:, :KV_DIM]
        vpad_ref[rows, :] = kv[:, KV_DIM:]

    wide = jax.ShapeDtypeStruct((T, D_MODEL), BF16)
    padded = (ATT_BLOCK + T, KV_DIM)
    return _launch(
        body, name="fwd_ple_qkv", grid=(nt,),
        in_specs=[_row_spec(D_MODEL), _row_spec(D_MODEL), _row_spec(PLE_DIM), _full_spec((D_MODEL, D_MODEL)),
                  _full_spec((PLE_DIM, D_MODEL)), _vec_spec(), _vec_spec(), _vec_spec(),
                  _full_spec((D_MODEL, 2 * KV_DIM)), _full_spec((D_MODEL, D_MODEL))],
        out_specs=[_row_spec(D_MODEL)] * 6 + [_full_spec(padded)] * 2,
        out_shape=[jax.ShapeDtypeStruct((T, D_MODEL), F32)] + [wide] * 5 + [jax.ShapeDtypeStruct(padded, BF16)] * 2,
        args=(x2, h3, p, wgate, wproj, g_post, g_kv, g_mix, wkv, wq), job=job)


def _alibi_slope(h):
    return 2.0 ** (-8.0 * (h + 1) / N_HEADS)


ATT_SUB = 32
ATT_GROUP_ROWS = GQA_GROUP * ATT_BLOCK


def _att_mask(n, rel_ref, off_ref):
    qi = lax.broadcasted_iota(jnp.int32, (ATT_BLOCK, 2 * ATT_BLOCK), 0)
    si = lax.broadcasted_iota(jnp.int32, (ATT_BLOCK, 2 * ATT_BLOCK), 1)
    rel = ATT_BLOCK + qi - si
    valid = (rel >= 0) & (rel < ATT_BLOCK) & ((si >= ATT_BLOCK) | (n > 0))
    rel_ref[...] = rel.astype(F32)
    off_ref[...] = jnp.where(valid, 0.0, NEG_INF)


def _att_probs(raw, relf, off, slope, sink):
    s = raw * ATT_SCALE - slope * relf + off
    m = jnp.maximum(jnp.max(s, axis=-1, keepdims=True), sink)
    e = jnp.exp(s - m)
    es = jnp.exp(sink - m)
    inv = 1.0 / (jnp.sum(e, axis=-1, keepdims=True) + es)
    return e * inv, es * inv


def _stack_heads(ref, kh):
    first = kh * GQA_GROUP
    return jnp.concatenate([ref[:, (first + g) * HEAD_DIM:(first + g + 1) * HEAD_DIM] for g in range(GQA_GROUP)], axis=0)


def _unstack_heads(stacked):
    return [stacked[g * ATT_BLOCK:(g + 1) * ATT_BLOCK, :] for g in range(GQA_GROUP)]


def _fwd_attention(q, kpad, vpad, sinks, job=None):
    T = q.shape[0]
    nb = T // ATT_BLOCK

    def body(q_ref, k_ref, v_ref, sink_ref, o_ref, s_scr, p_scr, rel_scr, off_scr):
        n = pl.program_id(0)
        start = pl.multiple_of(n * ATT_BLOCK, ATT_BLOCK)
        kw = k_ref[pl.ds(start, 2 * ATT_BLOCK), :]
        vw = v_ref[pl.ds(start, 2 * ATT_BLOCK), :]
        _att_mask(n, rel_scr, off_scr)
        outs = []
        for kh in range(N_KV_HEADS):
            kk = kw[:, kh * HEAD_DIM:(kh + 1) * HEAD_DIM]
            vv = vw[:, kh * HEAD_DIM:(kh + 1) * HEAD_DIM]
            s_scr[...] = _dot_nt(_stack_heads(q_ref, kh), kk)
            for g in range(GQA_GROUP):
                h = kh * GQA_GROUP + g
                for row0 in range(0, ATT_BLOCK, ATT_SUB):
                    rows, sub = pl.ds(g * ATT_BLOCK + row0, ATT_SUB), pl.ds(row0, ATT_SUB)
                    pr, _ = _att_probs(s_scr[rows, :], rel_scr[sub, :], off_scr[sub, :], _alibi_slope(h),
                                       sink_ref[0, h])
                    p_scr[rows, :] = pr.astype(BF16)
            outs += _unstack_heads(_dot(p_scr[...], vv))
        o_ref[...] = jnp.concatenate(outs, axis=1).astype(BF16)

    return _launch(
        body, name="fwd_attention", grid=(nb,),
        in_specs=[_row_spec(D_MODEL, ATT_BLOCK), _full_spec((T + ATT_BLOCK, KV_DIM)), _full_spec((T + ATT_BLOCK, KV_DIM)),
                  pl.BlockSpec(memory_space=pltpu.SMEM)],
        out_specs=[_row_spec(D_MODEL, ATT_BLOCK)],
        out_shape=[jax.ShapeDtypeStruct((T, D_MODEL), BF16)],
        scratch_shapes=[pltpu.VMEM((ATT_GROUP_ROWS, 2 * ATT_BLOCK), F32), pltpu.VMEM((ATT_GROUP_ROWS, 2 * ATT_BLOCK), BF16)]
                       + [pltpu.VMEM((ATT_BLOCK, 2 * ATT_BLOCK), F32)] * 2,
        args=(q, kpad, vpad, sinks), job=job)


def _fwd_attn_out(attn, x, wo, g_post, g_ffn, job=None):
    T = x.shape[0]
    nt = T // ROW_TILE

    def body(a_ref, x_ref, wo_ref, gpost_ref, gffn_ref, y_ref, x1_ref, h2_ref):
        y = _dot(a_ref[...], wo_ref[...])
        y_ref[...] = y.astype(BF16)
        x1 = x_ref[...] + _rms(y, gpost_ref[...])
        x1_ref[...] = x1
        h2_ref[...] = _rms(x1, gffn_ref[...]).astype(BF16)

    return _launch(
        body, name="fwd_attn_out", grid=(nt,),
        in_specs=[_row_spec(D_MODEL), _row_spec(D_MODEL), _full_spec((D_MODEL, D_MODEL)), _vec_spec(), _vec_spec()],
        out_specs=[_row_spec(D_MODEL)] * 3,
        out_shape=[jax.ShapeDtypeStruct((T, D_MODEL), BF16), jax.ShapeDtypeStruct((T, D_MODEL), F32),
                   jax.ShapeDtypeStruct((T, D_MODEL), BF16)],
        args=(attn, x, wo, g_post, g_ffn), job=job)


def _bwd_ple(layer, dx3, x2, z, pe, h3, p, f, wgate, g_ple_post, g_ple, g_post_ffn, job=None):
    T = x2.shape[0]
    tm = ROW_TILE
    nt = T // tm

    def body(dx3_ref, x2_ref, z_ref, pe_ref, h3_ref, p_ref, f_ref, wg_ref, gpp_ref, gp_ref, gpf_ref,
             dx2_ref, df_ref, dwg_ref, dwp_ref, dgpp_ref, dgp_ref, dgpf_ref, acc_g, acc_p):
        i = pl.program_id(0)
        first = i == 0
        dx3v = dx3_ref[...]
        gate = _sigmoid(z_ref[...].astype(F32))
        pev = pe_ref[...].astype(F32)
        de, dgpp = _rms_bwd(pev * gate, gpp_ref[...], dx3v)
        dpe = (de * gate).astype(BF16)
        dz = (de * pev * gate * (1.0 - gate)).astype(BF16)
        _acc(acc_p, _dot_tn(p_ref[...].astype(BF16), dpe), first)
        _acc(acc_g, _dot_tn(h3_ref[...], dz), first)
        dh3 = _dot_nt(dz, wg_ref[...])
        dxn, dgp = _rms_bwd(x2_ref[...], gp_ref[...], dh3)
        dx2 = dx3v + dxn
        dx2_ref[...] = dx2
        df, dgpf = _rms_bwd(f_ref[...].astype(F32), gpf_ref[...], dx2)
        df_ref[...] = df.astype(BF16)
        _acc(dgpp_ref, dgpp, first)
        _acc(dgp_ref, dgp, first)
        _acc(dgpf_ref, dgpf, first)

        @pl.when(i == nt - 1)
        def _():
            dwg_ref[...] = acc_g[...].astype(BF16)
            dwp_ref[...] = acc_p[...].astype(BF16)

    return _launch(
        body, name=f"bwd_ple{layer}", grid=(nt,),
        in_specs=[_row_spec(D_MODEL)] * 5 + [_row_spec(PLE_DIM), _row_spec(D_MODEL), _full_spec((D_MODEL, D_MODEL)),
                  _vec_spec(), _vec_spec(), _vec_spec()],
        out_specs=[_row_spec(D_MODEL), _row_spec(D_MODEL), _full_spec((D_MODEL, D_MODEL)), _full_spec((PLE_DIM, D_MODEL)),
                   _vec_spec(), _vec_spec(), _vec_spec()],
        out_shape=[jax.ShapeDtypeStruct((T, D_MODEL), F32), jax.ShapeDtypeStruct((T, D_MODEL), BF16),
                   jax.ShapeDtypeStruct((D_MODEL, D_MODEL), BF16), jax.ShapeDtypeStruct((PLE_DIM, D_MODEL), BF16)]
                  + [jax.ShapeDtypeStruct((1, D_MODEL), F32)] * 3,
        scratch_shapes=[pltpu.VMEM((D_MODEL, D_MODEL), F32), pltpu.VMEM((PLE_DIM, D_MODEL), F32)],
        args=(dx3, x2, z, pe, h3, p, f, wgate, g_ple_post, g_ple, g_post_ffn), vmem=VMEM_BIG, job=job)


def _ple_loss_bwd(layer, x2, h3, p, f, target, wgate, wproj, g_ple_post, g_ple, g_post_ffn, job=None):
    T = x2.shape[0]
    tm = ROW_TILE
    nt = T // tm

    def body(x2_ref, h3_ref, p_ref, f_ref, tgt_ref, wg_ref, wp_ref, gpp_ref, gp_ref, gpf_ref,
             dx2_ref, df_ref, dwg_ref, dwp_ref, dgpp_ref, dgp_ref, dgpf_ref, loss_ref, acc_g, acc_p):
        i = pl.program_id(0)
        first = i == 0
        h3 = h3_ref[...]
        pb = p_ref[...].astype(BF16)
        x2v = x2_ref[...]
        gate = _sigmoid(_dot(h3, wg_ref[...]))
        pev = _dot(pb, wp_ref[...])
        e = pev * gate
        err = x2v + _rms(e, gpp_ref[...]) - tgt_ref[...]
        _acc(loss_ref, 0.5 * jnp.sum(jnp.mean(err * err, axis=-1, keepdims=True), axis=0, keepdims=True), first)
        dx3v = err * (1.0 / D_MODEL)
        de, dgpp = _rms_bwd(e, gpp_ref[...], dx3v)
        dpe = (de * gate).astype(BF16)
        dz = (de * pev * gate * (1.0 - gate)).astype(BF16)
        _acc(acc_p, _dot_tn(pb, dpe), first)
        _acc(acc_g, _dot_tn(h3, dz), first)
        dxn, dgp = _rms_bwd(x2v, gp_ref[...], _dot_nt(dz, wg_ref[...]))
        dx2 = dx3v + dxn
        dx2_ref[...] = dx2
        df, dgpf = _rms_bwd(f_ref[...].astype(F32), gpf_ref[...], dx2)
        df_ref[...] = df.astype(BF16)
        _acc(dgpp_ref, dgpp, first)
        _acc(dgp_ref, dgp, first)
        _acc(dgpf_ref, dgpf, first)

        @pl.when(i == nt - 1)
        def _():
            dwg_ref[...] = acc_g[...].astype(BF16)
            dwp_ref[...] = acc_p[...].astype(BF16)

    return _launch(
        body, name=f"ple_loss_bwd{layer}", grid=(nt,),
        in_specs=[_row_spec(D_MODEL), _row_spec(D_MODEL), _row_spec(PLE_DIM), _row_spec(D_MODEL), _row_spec(D_MODEL),
                  _full_spec((D_MODEL, D_MODEL)), _full_spec((PLE_DIM, D_MODEL)), _vec_spec(), _vec_spec(), _vec_spec()],
        out_specs=[_row_spec(D_MODEL), _row_spec(D_MODEL), _full_spec((D_MODEL, D_MODEL)), _full_spec((PLE_DIM, D_MODEL)),
                   _vec_spec(), _vec_spec(), _vec_spec(), _full_spec((1, 1))],
        out_shape=[jax.ShapeDtypeStruct((T, D_MODEL), F32), jax.ShapeDtypeStruct((T, D_MODEL), BF16),
                   jax.ShapeDtypeStruct((D_MODEL, D_MODEL), BF16), jax.ShapeDtypeStruct((PLE_DIM, D_MODEL), BF16)]
                  + [jax.ShapeDtypeStruct((1, D_MODEL), F32)] * 3 + [jax.ShapeDtypeStruct((1, 1), F32)],
        scratch_shapes=[pltpu.VMEM((D_MODEL, D_MODEL), F32), pltpu.VMEM((PLE_DIM, D_MODEL), F32)],
        args=(x2, h3, p, f, target, wgate, wproj, g_ple_post, g_ple, g_post_ffn), vmem=VMEM_BIG, job=job)


def _bwd_ffn_act(layer, df, gs, us, wgu, wd, job=None):
    T = df.shape[0]
    tm = min(FFN_ROW_TILE, T)
    nt = T // tm
    sub = tm // FFN_SUB_TILES
    last = FF_CHUNKS - 1
    wgu, wd = _column_views(wgu), _column_views(wd)
    n_gu, n_wd = len(wgu), len(wd)
    wd_cols = _column_ranges(wd)

    def body(df_ref, gs_ref, us_ref, *refs):
        wgu_refs, wd_refs = refs[:n_gu], refs[n_gu:n_gu + n_wd]
        dh_ref, dg_ref, du_ref, a_ref, acc_h = refs[n_gu + n_wd:]
        k = pl.program_id(0)
        i = pl.program_id(1)
        rows = pl.ds(pl.multiple_of(i * tm, tm), tm)
        dhs = []
        for s in range(FFN_SUB_TILES):
            r = pl.ds(s * sub, sub)
            g = gs_ref[r, :].astype(F32)
            u = us_ref[r, :].astype(F32)
            sg = _sigmoid(g)
            silu = g * sg
            a_ref[r, :] = (silu * u).astype(BF16)
            da = _add_all([_dot_nt(df_ref[r, c0:c1], w[...]) for (c0, c1), w in zip(wd_cols, wd_refs)])
            dg = (da * u * (sg * (1.0 + g * (1.0 - sg)))).astype(BF16)
            du = (da * silu).astype(BF16)
            dg_ref[r, :] = dg
            du_ref[r, :] = du
            dhs.append(jnp.concatenate([_dot(dg, w[0]) + _dot(du, w[1]) for w in wgu_refs], axis=1))
        dh = jnp.concatenate(dhs, axis=0)

        @pl.when(k == 0)
        def _():
            acc_h[rows, :] = dh

        @pl.when(jnp.logical_and(k > 0, k < last))
        def _():
            acc_h[rows, :] += dh

        @pl.when(k == last)
        def _():
            dh_ref[...] = acc_h[rows, :] + dh

    chunk_rows = pl.BlockSpec((None, tm, FF_BLOCK), lambda k, i: (k, i, 0))
    saved = jax.ShapeDtypeStruct((FF_CHUNKS, T, FF_BLOCK), BF16)
    return _launch(
        body, name=f"bwd_ffn_act{layer}", grid=(FF_CHUNKS, nt),
        in_specs=[pl.BlockSpec((tm, D_MODEL), lambda k, i: (i, 0)), chunk_rows, chunk_rows]
                 + [pl.BlockSpec((None, 2, FF_BLOCK, FFN_WEIGHT_COLS), lambda k, i, b=b: (k, 0, 0, b)) for _, b in wgu]
                 + [pl.BlockSpec((FF_BLOCK, FFN_WEIGHT_COLS), lambda k, i, b=b: (k, b)) for _, b in wd],
        out_specs=[pl.BlockSpec((tm, D_MODEL), lambda k, i: (jnp.where(k == last, i, 0), 0)),
                   chunk_rows, chunk_rows, chunk_rows],
        out_shape=[jax.ShapeDtypeStruct((T, D_MODEL), F32), saved, saved, saved],
        scratch_shapes=[pltpu.VMEM((T, D_MODEL), F32)],
        args=(df, gs, us, *[w for w, _ in wgu], *[w for w, _ in wd]), vmem=VMEM_BIG, job=job)


def _bwd_ffn_dw(layer, q, parts, h2, df, dg, du, a, job=None):
    T = h2.shape[0]
    width = D_MODEL // parts

    def body(h_ref, df_ref, dg_ref, du_ref, a_ref, dgu_ref, dwd_ref):
        h = h_ref[...]
        dgu_ref[0] = _dot_tn(dg_ref[...], h).astype(BF16)
        dgu_ref[1] = _dot_tn(du_ref[...], h).astype(BF16)
        dwd_ref[...] = _dot_tn(a_ref[...], df_ref[...]).astype(BF16)

    cols = pl.BlockSpec((T, width), lambda k: (0, q))
    chunk = pl.BlockSpec((None, T, FF_BLOCK), lambda k: (k, 0, 0))
    return _launch(
        body, name=f"bwd_ffn_dw{layer}_{q}", grid=(FF_CHUNKS,),
        in_specs=[cols, cols, chunk, chunk, chunk],
        out_specs=[pl.BlockSpec((None, 2, FF_BLOCK, width), lambda k: (k, 0, 0, 0)),
                   pl.BlockSpec((FF_BLOCK, width), lambda k: (k, 0))],
        out_shape=[jax.ShapeDtypeStruct((FF_CHUNKS, 2, FF_BLOCK, width), BF16),
                   jax.ShapeDtypeStruct((D_FF, width), BF16)],
        args=(h2, df, dg, du, a), vmem=VMEM_BIG, job=job)


def _bwd_attn_out(dx2, dh2, x1, y, attn, wo, g_ffn, g_post, job=None):
    T = x1.shape[0]
    nt = T // ROW_TILE

    def body(dx2_ref, dh2_ref, x1_ref, y_ref, a_ref, wo_ref, gffn_ref, gpost_ref,
             dx1_ref, da_ref, dwo_ref, dgf_ref, dgp_ref, acc):
        i = pl.program_id(0)
        first = i == 0
        dxn, dgf = _rms_bwd(x1_ref[...], gffn_ref[...], dh2_ref[...])
        dx1 = dx2_ref[...] + dxn
        dx1_ref[...] = dx1
        dy, dgp = _rms_bwd(y_ref[...].astype(F32), gpost_ref[...], dx1)
        dyb = dy.astype(BF16)
        da_ref[...] = _dot_nt(dyb, wo_ref[...]).astype(BF16)
        _acc(acc, _dot_tn(a_ref[...], dyb), first)
        _acc(dgf_ref, dgf, first)
        _acc(dgp_ref, dgp, first)

        @pl.when(i == nt - 1)
        def _():
            dwo_ref[...] = acc[...].astype(BF16)

    return _launch(
        body, name="bwd_attn_out", grid=(nt,),
        in_specs=[_row_spec(D_MODEL)] * 5 + [_full_spec((D_MODEL, D_MODEL)), _vec_spec(), _vec_spec()],
        out_specs=[_row_spec(D_MODEL), _row_spec(D_MODEL), _full_spec((D_MODEL, D_MODEL)), _vec_spec(), _vec_spec()],
        out_shape=[jax.ShapeDtypeStruct((T, D_MODEL), F32), jax.ShapeDtypeStruct((T, D_MODEL), BF16),
                   jax.ShapeDtypeStruct((D_MODEL, D_MODEL), BF16)] + [jax.ShapeDtypeStruct((1, D_MODEL), F32)] * 2,
        scratch_shapes=[pltpu.VMEM((D_MODEL, D_MODEL), F32)],
        args=(dx2, dh2, x1, y, attn, wo, g_ffn, g_post), job=job)


def _bwd_attention(q, dattn, kpad, vpad, sinks, job=None):
    T = q.shape[0]
    nb = T // ATT_BLOCK

    def body(q_ref, do_ref, k_ref, v_ref, sink_ref, dq_ref, dkv_ref, ds_ref, dk_ref, dv_ref, s_scr, dp_scr, p_scr,
             dsb_scr, rel_scr, off_scr):
        n = pl.program_id(0)
        _att_mask(n, rel_scr, off_scr)

        @pl.when(n == 0)
        def _():
            dk_ref[...] = jnp.zeros_like(dk_ref)
            dv_ref[...] = jnp.zeros_like(dv_ref)
            ds_ref[...] = jnp.zeros_like(ds_ref)

        start = pl.multiple_of(n * ATT_BLOCK, ATT_BLOCK)
        win = pl.ds(start, 2 * ATT_BLOCK)
        kw = k_ref[win, :]
        vw = v_ref[win, :]
        lane = lax.broadcasted_iota(jnp.int32, (1, ATT_BLOCK), 1)
        dsink = jnp.zeros((1, ATT_BLOCK), F32)
        dqs, dks, dvs = [], [], []
        for kh in range(N_KV_HEADS):
            kk = kw[:, kh * HEAD_DIM:(kh + 1) * HEAD_DIM]
            vv = vw[:, kh * HEAD_DIM:(kh + 1) * HEAD_DIM]
            qs = _stack_heads(q_ref, kh)
            dos = _stack_heads(do_ref, kh)
            s_scr[...] = _dot_nt(qs, kk)
            dp_scr[...] = _dot_nt(dos, vv)
            for g in range(GQA_GROUP):
                h = kh * GQA_GROUP + g
                dsink_h = jnp.zeros((1, 1), F32)
                for row0 in range(0, ATT_BLOCK, ATT_SUB):
                    rows, sub = pl.ds(g * ATT_BLOCK + row0, ATT_SUB), pl.ds(row0, ATT_SUB)
                    pr, ps = _att_probs(s_scr[rows, :], rel_scr[sub, :], off_scr[sub, :], _alibi_slope(h),
                                        sink_ref[0, h])
                    dp = dp_scr[rows, :]
                    delta = jnp.sum(pr * dp, axis=-1, keepdims=True)
                    dsb_scr[rows, :] = (pr * (dp - delta) * ATT_SCALE).astype(BF16)
                    p_scr[rows, :] = pr.astype(BF16)
                    dsink_h = dsink_h - jnp.sum(ps * delta, axis=0, keepdims=True)
                dsink = dsink + jnp.where(lane == h, dsink_h, 0.0)
            dsb = dsb_scr[...]
            dqs += _unstack_heads(_dot(dsb, kk))
            dks.append(_dot_tn(dsb, qs))
            dvs.append(_dot_tn(p_scr[...], dos))
        dq_ref[...] = jnp.concatenate(dqs, axis=1).astype(BF16)
        dk_ref[win, :] += jnp.concatenate(dks, axis=1)
        dv_ref[win, :] += jnp.concatenate(dvs, axis=1)
        ds_ref[...] += dsink

        @pl.when(n == nb - 1)
        def _():
            dkv_ref[:, :KV_DIM] = dk_ref[ATT_BLOCK:, :].astype(BF16)
            dkv_ref[:, KV_DIM:] = dv_ref[ATT_BLOCK:, :].astype(BF16)

    return _launch(
        body, name="bwd_attention", grid=(nb,),
        in_specs=[_row_spec(D_MODEL, ATT_BLOCK), _row_spec(D_MODEL, ATT_BLOCK), _full_spec((T + ATT_BLOCK, KV_DIM)),
                  _full_spec((T + ATT_BLOCK, KV_DIM)), pl.BlockSpec(memory_space=pltpu.SMEM)],
        out_specs=[_row_spec(D_MODEL, ATT_BLOCK), _full_spec((T, 2 * KV_DIM)), _full_spec((1, ATT_BLOCK))],
        out_shape=[jax.ShapeDtypeStruct((T, D_MODEL), BF16), jax.ShapeDtypeStruct((T, 2 * KV_DIM), BF16),
                   jax.ShapeDtypeStruct((1, ATT_BLOCK), F32)],
        scratch_shapes=[pltpu.VMEM((T + ATT_BLOCK, KV_DIM), F32)] * 2
                       + [pltpu.VMEM((ATT_GROUP_ROWS, 2 * ATT_BLOCK), F32)] * 2
                       + [pltpu.VMEM((ATT_GROUP_ROWS, 2 * ATT_BLOCK), BF16)] * 2
                       + [pltpu.VMEM((ATT_BLOCK, 2 * ATT_BLOCK), F32)] * 2,
        args=(q, dattn, kpad, vpad, sinks), vmem=VMEM_BIG, job=job)


def _bwd_qkv(dxres, dq, dkv, x3, h1, hk, wq, wkv, g_mix, g_kv, job=None):
    T = x3.shape[0]
    nt = T // ROW_TILE

    def body(dxr_ref, dq_ref, dkv_ref, x_ref, h1_ref, hk_ref, wq_ref, wkv_ref, gmix_ref, gkv_ref,
             dx_ref, dwq_ref, dwkv_ref, dgm_ref, dgk_ref, acc_q, acc_kv):
        i = pl.program_id(0)
        first = i == 0
        dqv = dq_ref[...]
        dkvv = dkv_ref[...]
        xv = x_ref[...]
        d1, dgm = _rms_bwd(xv, gmix_ref[...], _dot_nt(dqv, wq_ref[...]))
        d2, dgk = _rms_bwd(xv, gkv_ref[...], _dot_nt(dkvv, wkv_ref[...]))
        dx_ref[...] = dxr_ref[...] + d1 + d2
        _acc(acc_q, _dot_tn(h1_ref[...], dqv), first)
        _acc(acc_kv, _dot_tn(hk_ref[...], dkvv), first)
        _acc(dgm_ref, dgm, first)
        _acc(dgk_ref, dgk, first)

        @pl.when(i == nt - 1)
        def _():
            dwq_ref[...] = acc_q[...].astype(BF16)
            dwkv_ref[...] = acc_kv[...].astype(BF16)

    return _launch(
        body, name="bwd_qkv", grid=(nt,),
        in_specs=[_row_spec(D_MODEL), _row_spec(D_MODEL), _row_spec(2 * KV_DIM), _row_spec(D_MODEL), _row_spec(D_MODEL),
                  _row_spec(D_MODEL), _full_spec((D_MODEL, D_MODEL)), _full_spec((D_MODEL, 2 * KV_DIM)), _vec_spec(),
                  _vec_spec()],
        out_specs=[_row_spec(D_MODEL), _full_spec((D_MODEL, D_MODEL)), _full_spec((D_MODEL, 2 * KV_DIM)), _vec_spec(),
                   _vec_spec()],
        out_shape=[jax.ShapeDtypeStruct((T, D_MODEL), F32), jax.ShapeDtypeStruct((D_MODEL, D_MODEL), BF16),
                   jax.ShapeDtypeStruct((D_MODEL, 2 * KV_DIM), BF16)] + [jax.ShapeDtypeStruct((1, D_MODEL), F32)] * 2,
        scratch_shapes=[pltpu.VMEM((D_MODEL, D_MODEL), F32), pltpu.VMEM((D_MODEL, 2 * KV_DIM), F32)],
        args=(dxres, dq, dkv, x3, h1, hk, wq, wkv, g_mix, g_kv), job=job)


def _bwd_pool_mixer(dx2, dh2, x1, x, yraw, d, wp, scale, g_ffn, g_post, g_pre, job=None):
    T = x.shape[0]
    tm = ROW_TILE
    nt = T // tm

    def body(dx2_ref, dh2_ref, x1_ref, x_ref, yraw_ref, d_ref, wp_ref, sc_ref, gffn_ref, gpost_ref, gpre_ref,
             dx_ref, dwp_ref, dsc_ref, dgf_ref, dgp_ref, dgm_ref, carry, acc):
        i = pl.program_id(0)
        first = i == 0
        tile = nt - 1 - i

        @pl.when(first)
        def _():
            carry[...] = jnp.zeros_like(carry)

        dxn, dgf = _rms_bwd(x1_ref[...], gffn_ref[...], dh2_ref[...])
        dx1 = dx2_ref[...] + dxn
        yraw = yraw_ref[...].astype(F32)
        sc = sc_ref[...]
        dy, dgp = _rms_bwd(yraw * sc, gpost_ref[...], dx1)
        dsc = jnp.sum(dy * yraw, axis=0, keepdims=True)
        dyb = (dy * sc).astype(BF16)
        dv = d_ref[...]
        dds = []
        for g in range(N_POOL_GROUPS):
            cols = slice(g * POOL_GROUP, (g + 1) * POOL_GROUP)
            dds.append(_dot_nt(dyb[:, cols], wp_ref[g]))
            _acc(acc.at[g], _dot_tn(dv[:, cols], dyb[:, cols]), first)
        dd = jnp.concatenate(dds, axis=1)
        e = dd / _pool_counts(tile * tm, tm)
        ext = jnp.concatenate([e, carry[...]], axis=0)
        carry[...] = e[:POOL_HALO, :]
        sums = _window_sums(ext, lambda k: tm + POOL_HALO - k)[:tm, :]
        dxm, dgm = _rms_bwd(x_ref[...], gpre_ref[...], sums - dd)
        dx_ref[...] = dx1 + dxm
        _acc(dsc_ref, dsc, first)
        _acc(dgf_ref, dgf, first)
        _acc(dgp_ref, dgp, first)
        _acc(dgm_ref, dgm, first)

        @pl.when(i == nt - 1)
        def _():
            dwp_ref[...] = acc[...].astype(BF16)

    rev = pl.BlockSpec((tm, D_MODEL), lambda i: (nt - 1 - i, 0))
    return _launch(
        body, name="bwd_pool_mixer", grid=(nt,),
        in_specs=[rev] * 6 + [_full_spec((N_POOL_GROUPS, POOL_GROUP, POOL_GROUP))] + [_vec_spec()] * 4,
        out_specs=[rev, _full_spec((N_POOL_GROUPS, POOL_GROUP, POOL_GROUP))] + [_vec_spec()] * 4,
        out_shape=[jax.ShapeDtypeStruct((T, D_MODEL), F32),
                   jax.ShapeDtypeStruct((N_POOL_GROUPS, POOL_GROUP, POOL_GROUP), BF16)]
                  + [jax.ShapeDtypeStruct((1, D_MODEL), F32)] * 4,
        scratch_shapes=[pltpu.VMEM((POOL_HALO, D_MODEL), F32), pltpu.VMEM((N_POOL_GROUPS, POOL_GROUP, POOL_GROUP), F32)],
        args=(dx2, dh2, x1, x, yraw, d, wp, scale, g_ffn, g_post, g_pre), job=job)


def _my_place():
    return lax.axis_index("x"), lax.axis_index("y"), lax.axis_index("c")


def _dev_index(px, py, pc):
    return 4 * px + 2 * py + pc


def _peer_by_relation(r):
    x, y, c = _my_place()
    return (x ^ ((r >> 2) & 1), y ^ ((r >> 1) & 1), c ^ (r & 1))


def _slot_pool(ref, j):
    return ref.at[:, pl.ds(pl.multiple_of(j * 32, 32), 32), :]


def _slot_scale(ref, j):
    return ref.at[:, pl.ds(pl.multiple_of(j * 128, 128), 128)]


def _slot_rows128(ref, j):
    return ref.at[pl.ds(pl.multiple_of(j * 128, 128), 128), :]


def _slot_gu(ref, j):
    return ref.at[j % FF_CHUNKS, j // FF_CHUNKS]


def _slot_wd(ref, j):
    return ref.at[pl.ds(pl.multiple_of(j * WD_ROWS, 16), WD_ROWS), :]


def _slot_cols128(ref, j):
    return ref.at[:, pl.ds(pl.multiple_of(j * 128, 128), 128)]


_GATHERED = {
    "pool": ((N_POOL_GROUPS, POOL_GROUP, POOL_GROUP), BF16, _slot_pool),
    "scale": ((1, D_MODEL), F32, _slot_scale),
    "kv": ((D_MODEL, 2 * KV_DIM), BF16, _slot_rows128),
    "q": ((D_MODEL, D_MODEL), BF16, _slot_rows128),
    "o": ((D_MODEL, D_MODEL), BF16, _slot_rows128),
    "gu": ((FF_CHUNKS, 2, FF_BLOCK, D_MODEL), BF16, _slot_gu),
    "wd": ((D_FF, D_MODEL), BF16, _slot_wd),
    "guh": ((FF_CHUNKS, 2, FF_BLOCK, D_MODEL // 2), BF16, _slot_gu),
    "wdh": ((D_FF, D_MODEL // 2), BF16, _slot_wd),
    "gate": ((D_MODEL, D_MODEL), BF16, _slot_rows128),
    "proj": ((PLE_DIM, D_MODEL), BF16, _slot_cols128),
}


def _no_compute():
    pass


class _AllGather:
    peers = ("sibling", "x", "y")

    def __init__(self, names, shards):
        self.kinds = [_GATHERED[n.rstrip("01_")] for n in names]
        entries = [shards[n] if isinstance(shards[n], tuple) else (shards[n], None, None) for n in names]
        self.args = [array for array, _, _ in entries]
        self.layers = [layer for _, layer, _ in entries]
        self.columns = [columns for _, _, columns in entries]
        self.out_shape = [jax.ShapeDtypeStruct(shape, dtype) for shape, dtype, _ in self.kinds]
        n = len(names)
        self.scratch = [pltpu.SemaphoreType.DMA((n, 7)), pltpu.SemaphoreType.DMA((n, 7)), pltpu.SemaphoreType.DMA((n,))]

    def _plan(self, srcs, outs, sems):
        send_sems, recv_sems, local_sems = sems
        x, y, c = _my_place()

        def slot(t, dev):
            return self.kinds[t][2](outs[t], _dev_index(*dev))

        def copy(t, k, block, to, src=None):
            return pltpu.make_async_remote_copy(
                src_ref=slot(t, block) if src is None else src, dst_ref=slot(t, block),
                send_sem=send_sems.at[t, k], recv_sem=recv_sems.at[t, k], device_id=to, device_id_type=MESH)

        return types.SimpleNamespace(
            copy=copy, core=c, me=(x, y, c), sibling=(x, y, 1 - c),
            x_chip=(1 - x, y), y_chip=(x, 1 - y), far_chip=(1 - x, 1 - y),
            via=(x ^ (1 - c), y ^ c),
            onto=(x ^ c, y ^ (1 - c)),
            k_via=1 + c, k_onto=2 - c,
            local=[pltpu.make_async_copy(self._shard(srcs, t), slot(t, (x, y, c)), local_sems.at[t])
                   for t in range(len(srcs))])

    def _shard(self, srcs, t):
        shard = srcs[t] if self.layers[t] is None else srcs[t].at[self.layers[t]]
        if self.columns[t] is None:
            return shard
        first, end = self.columns[t]
        return shard.at[:, first:end]

    def start(self, srcs, outs, sems):
        p = self._plan(srcs, outs, sems)
        for cp in p.local:
            cp.start()
        for t in range(len(srcs)):
            shard = self._shard(srcs, t)
            p.copy(t, 0, p.me, p.sibling, src=shard).start()
            p.copy(t, 1, p.me, (*p.x_chip, p.core), src=shard).start()
            p.copy(t, 2, p.me, (*p.y_chip, p.core), src=shard).start()

    def mid(self, srcs, outs, sems):
        p = self._plan(srcs, outs, sems)
        for t in range(len(srcs)):
            block = (*p.via, p.core)
            p.copy(t, p.k_via, block, p.me).wait_recv()
            p.copy(t, 3, block, (*p.onto, p.core)).start()
            p.copy(t, 3 + p.k_via, block, p.sibling).start()

    def late(self, srcs, outs, sems):
        p = self._plan(srcs, outs, sems)
        n = len(srcs)
        for t in range(n):
            block = (*p.onto, p.core)
            p.copy(t, p.k_onto, block, p.me).wait_recv()
            p.copy(t, 3 + p.k_onto, block, p.sibling).start()
        for t in range(n):
            block = (*p.far_chip, p.core)
            p.copy(t, 3, block, p.me).wait_recv()
            p.copy(t, 6, block, p.sibling).start()

    def finish(self, srcs, outs, sems):
        p = self._plan(srcs, outs, sems)
        n = len(srcs)
        other = 1 - p.core
        for t in range(n):
            p.copy(t, 0, (*p.me[:2], other), p.me).wait_recv()
            for k, chip in ((4, p.x_chip), (5, p.y_chip), (6, p.far_chip)):
                p.copy(t, k, (*chip, other), p.me).wait_recv()
            for k in range(7):
                p.copy(t, k, p.me, p.sibling).wait_send()
        for cp in p.local:
            cp.wait()


def _jobs_only(name, job=None):
    return _launch(_no_compute, name=name, grid=(), in_specs=[], out_specs=[], out_shape=[], args=(), job=job)


def _block_pool(ref, j):
    return ref.at[:, pl.ds(pl.multiple_of(j * 32, 32), 32), :]


def _block_rows128(ref, j):
    return ref.at[pl.ds(pl.multiple_of(j * 128, 128), 128), :]


def _block_gu(ref, j):
    return ref.at[j % FF_CHUNKS, j // FF_CHUNKS]


def _block_wd(ref, j):
    return ref.at[pl.ds(pl.multiple_of(j * WD_ROWS, 16), WD_ROWS), :]


def _block_cols128(ref, j):
    return ref.at[:, pl.ds(pl.multiple_of(j * 128, 128), 128)]


_SCATTERED = {
    "pool": ((N_POOL_GROUPS, 32, POOL_GROUP), _block_pool),
    "kv": ((128, 2 * KV_DIM), _block_rows128),
    "q": ((128, D_MODEL), _block_rows128),
    "o": ((128, D_MODEL), _block_rows128),
    "gu": ((FF_BLOCK, FF_PART), _block_gu),
    "wd": ((WD_ROWS, FF_PART), _block_wd),
    "guA": ((FF_BLOCK, FF_PART), lambda ref, j: _block_gu(ref, j).at[:, :FF_PART]),
    "guB": ((FF_BLOCK, FF_PART), lambda ref, j: _block_gu(ref, j).at[:, FF_PART:]),
    "wdA": ((WD_ROWS, FF_PART), lambda ref, j: _block_wd(ref, j).at[:, :FF_PART]),
    "wdB": ((WD_ROWS, FF_PART), lambda ref, j: _block_wd(ref, j).at[:, FF_PART:]),
    "gate": ((128, D_MODEL), _block_rows128),
    "proj": ((PLE_DIM, 128), _block_cols128),
}


class _SiblingSwap:
    peers = ("sibling",)

    def __init__(self, pieces):
        self.kinds = [_SCATTERED[kind] for kind, _ in pieces]
        self.args = [g for _, g in pieces]
        self.out_shape = [jax.ShapeDtypeStruct((N_CHIPS, *block), BF16) for block, _ in self.kinds]
        n = len(pieces)
        self.scratch = [pltpu.SemaphoreType.DMA((n, N_CHIPS)), pltpu.SemaphoreType.DMA((n, N_CHIPS))]

    def _copies(self, srcs, outs, sems):
        send_sems, recv_sems = sems
        x, y, c = _my_place()
        return [pltpu.make_async_remote_copy(
            src_ref=block(srcs[t], 2 * ch + 1 - c), dst_ref=outs[t].at[ch], send_sem=send_sems.at[t, ch],
            recv_sem=recv_sems.at[t, ch], device_id=(x, y, 1 - c), device_id_type=MESH)
            for t, (_, block) in enumerate(self.kinds) for ch in range(N_CHIPS)]

    def start(self, srcs, outs, sems):
        for cp in self._copies(srcs, outs, sems):
            cp.start()

    def finish(self, srcs, outs, sems):
        for cp in self._copies(srcs, outs, sems):
            cp.wait()


class _ChipScatter:
    N_BUFS = 4
    peers = ("x", "y")

    def __init__(self, pieces):
        self.kinds = [_SCATTERED[kind] for kind, _, _ in pieces]
        self.n = n = len(pieces)
        self.args = [g for _, g, _ in pieces] + [s for _, _, s in pieces]
        self.out_shape = [jax.ShapeDtypeStruct((2, *block), BF16) for block, _ in self.kinds]
        self.scratch = []
        for block, _ in self.kinds:
            self.scratch += [pltpu.VMEM((N_CHIPS, *block), BF16)] * 3 + [pltpu.VMEM((2, *block), BF16)]
        dma = pltpu.SemaphoreType.DMA
        self.scratch += [dma((n, N_CHIPS + 1)), dma((n, 2)), dma((n, 2)), dma((n,)), dma((n,)), dma((n,))]

    def _plan(self, outs, scr):
        n = self.n
        first_send, first_recv, second_send, second_recv, keep_sems = scr[self.N_BUFS * n + 1:]
        x, y, c = _my_place()
        via = (x ^ (1 - c), y ^ c)
        onto = (x ^ c, y ^ (1 - c))
        index = lambda chip: 2 * chip[0] + chip[1]
        first, second, keep = [], [], []
        for t in range(n):
            total, inbox = scr[self.N_BUFS * t + 2], scr[self.N_BUFS * t + 3]
            for k, chip in enumerate((via, (1 - x, 1 - y))):
                first.append(pltpu.make_async_remote_copy(
                    src_ref=total.at[index(chip)], dst_ref=inbox.at[k], send_sem=first_send.at[t, k],
                    recv_sem=first_recv.at[t, k], device_id=(*via, c), device_id_type=MESH))
            second.append(pltpu.make_async_remote_copy(
                src_ref=total.at[index(onto)], dst_ref=outs[t].at[1], send_sem=second_send.at[t],
                recv_sem=second_recv.at[t], device_id=(*onto, c), device_id_type=MESH))
            keep.append(pltpu.make_async_copy(total.at[index((x, y))], outs[t].at[0], keep_sems.at[t]))
        return first, second, keep, index((x, y)), index(onto)

    def start(self, ins, outs, scr):
        n = self.n
        load_sems = scr[self.N_BUFS * n]
        c = lax.axis_index("c")
        loads = []
        for t, (_, block) in enumerate(self.kinds):
            mine, theirs = scr[self.N_BUFS * t], scr[self.N_BUFS * t + 1]
            loads += [pltpu.make_async_copy(block(ins[t], 2 * ch + c), mine.at[ch], load_sems.at[t, ch])
                      for ch in range(N_CHIPS)]
            loads.append(pltpu.make_async_copy(ins[n + t], theirs, load_sems.at[t, N_CHIPS]))
        for cp in loads:
            cp.start()
        for cp in loads:
            cp.wait()
        for t in range(n):
            mine, theirs, total = scr[self.N_BUFS * t:self.N_BUFS * t + 3]
            for ch in range(N_CHIPS):
                total[ch] = (mine[ch].astype(F32) + theirs[ch].astype(F32)).astype(BF16)
        for cp in self._plan(outs, scr)[0]:
            cp.start()

    def mid(self, ins, outs, scr):
        first, second, keep, me, onto = self._plan(outs, scr)
        for cp in first:
            cp.wait_recv()
        for t in range(self.n):
            total, inbox = scr[self.N_BUFS * t + 2], scr[self.N_BUFS * t + 3]
            for k, slot in enumerate((me, onto)):
                total[slot] = (total[slot].astype(F32) + inbox[k].astype(F32)).astype(BF16)
        for cp in second + keep:
            cp.start()

    def finish(self, ins, outs, scr):
        first, second, keep, _, _ = self._plan(outs, scr)
        for cp in first:
            cp.wait_send()
        for cp in second + keep:
            cp.wait()


class _ToEveryone:
    peers = _EVERYONE

    def __init__(self, scattered=(), gathered=()):
        self.blocks = [_SCATTERED[kind][1] for kind, _ in scattered] + [None] * len(gathered)
        self.args = [g for _, g in scattered] + list(gathered)
        self.out_shape = [jax.ShapeDtypeStruct((N_DEV, *_SCATTERED[kind][0]), BF16) for kind, _ in scattered]
        self.out_shape += [jax.ShapeDtypeStruct((N_DEV, *a.shape), a.dtype) for a in gathered]
        n = len(self.args)
        self.scratch = [pltpu.SemaphoreType.DMA((n, N_DEV - 1)), pltpu.SemaphoreType.DMA((n, N_DEV - 1)),
                        pltpu.SemaphoreType.DMA((n,))]

    def _copies(self, srcs, outs, sems):
        send_sems, recv_sems, local_sems = sems
        me = _dev_index(*_my_place())
        copies = []
        for t, block in enumerate(self.blocks):
            part = (lambda j, t=t, block=block: srcs[t] if block is None else block(srcs[t], j))
            copies.append(pltpu.make_async_copy(part(me), outs[t].at[me], local_sems.at[t]))
            for r in range(1, N_DEV):
                peer = _peer_by_relation(r)
                copies.append(pltpu.make_async_remote_copy(
                    src_ref=part(_dev_index(*peer)), dst_ref=outs[t].at[me], send_sem=send_sems.at[t, r - 1],
                    recv_sem=recv_sems.at[t, r - 1], device_id=peer, device_id_type=MESH))
        return copies

    def start(self, srcs, outs, sems):
        for cp in self._copies(srcs, outs, sems):
            cp.start()

    def finish(self, srcs, outs, sems):
        for cp in self._copies(srcs, outs, sems):
            cp.wait()


class _Jobs:
    def __init__(self, *jobs):
        self.jobs = jobs
        together = {p for j in jobs for p in j.peers}
        self.peers = tuple(p for p in _EVERYONE if p in together)
        self.args = [a for j in jobs for a in j.args]
        self.out_shape = [o for j in jobs for o in j.out_shape]
        self.scratch = [s for j in jobs for s in j.scratch]

    def _split(self, refs, attr):
        at = 0
        for j in self.jobs:
            n = len(getattr(j, attr))
            yield refs[at:at + n]
            at += n

    def _each(self, ins, outs, scr):
        return zip(self.jobs, self._split(ins, "args"), self._split(outs, "out_shape"), self._split(scr, "scratch"))

    def start(self, ins, outs, scr):
        for j, i, o, s in self._each(ins, outs, scr):
            j.start(i, o, s)

    def mid(self, ins, outs, scr):
        for j, i, o, s in self._each(ins, outs, scr):
            if hasattr(j, "mid"):
                j.mid(i, o, s)

    def late(self, ins, outs, scr):
        for j, i, o, s in self._each(ins, outs, scr):
            if hasattr(j, "late"):
                j.late(i, o, s)

    def finish(self, ins, outs, scr):
        for j, i, o, s in self._each(ins, outs, scr):
            j.finish(i, o, s)

    def split_outputs(self, outs):
        return list(self._split(outs, "out_shape"))


def _adamw_math(w, g, m, v):
    m = ADAM_B1 * m + (1.0 - ADAM_B1) * g
    v = ADAM_B2 * v + (1.0 - ADAM_B2) * (g * g)
    m_hat = m / (1.0 - ADAM_B1 ** ADAM_STEP)
    v_hat = v / (1.0 - ADAM_B2 ** ADAM_STEP)
    delta = -ADAM_LR * (m_hat / (jnp.sqrt(v_hat) + ADAM_EPS) + ADAM_WD * w)
    return delta, m, v


def _adamw(name, w, m, v, landings, n_col_blocks=1, job=None):
    n_slots, r, c = landings[0].shape
    grid = (w.shape[0] // r, n_col_blocks)

    def body(w_ref, m_ref, v_ref, *rest):
        l_refs, (g_ref, d_ref, nm_ref, nv_ref) = rest[:len(landings)], rest[len(landings):]
        step = pl.program_id(0) * n_col_blocks + pl.program_id(1)
        for idx, l_ref in enumerate(l_refs):
            @pl.when(step == idx)
            def _(l_ref=l_ref):
                g = l_ref[0].astype(F32)
                for s in range(1, n_slots):
                    g = g + l_ref[s].astype(F32)
                g_ref[...] = g
                d_ref[...], nm_ref[...], nv_ref[...] = _adamw_math(w_ref[...], g, m_ref[...], v_ref[...])

    spec = pl.BlockSpec((r, c), lambda a, b: (a, b))
    return _launch(
        body, name=f"adamw_{name}", grid=grid,
        in_specs=[spec, spec, spec] + [_full_spec((n_slots, r, c))] * len(landings),
        out_specs=[spec] * 4, out_shape=[jax.ShapeDtypeStruct(w.shape, F32)] * 4,
        args=(w, m, v, *landings), vmem=VMEM_BIG, job=job)


_SMALL = (("pre_mix_g", SV_PRE_MIX, 2), ("post_mix_g", SV_POST_MIX, 2), ("pre_ffn_g", SV_PRE_FFN, 2),
          ("post_ffn_g", SV_POST_FFN, 2), ("ple_g", SV_PLE, 2), ("ple_post_g", SV_PLE_POST, 2), ("kv_g", SV_KV, 1),
          ("pool_scale", SV_POOL_SCALE, 1), ("sinks", SV_SINKS, 1))


def _adamw_several(items):
    counts = [len(landings) for _, _, _, landings in items]
    args = [a for w, m, v, landings in items for a in (w, m, v, *landings)]
    out_shape = [jax.ShapeDtypeStruct(w.shape, F32) for w, _, _, _ in items for _ in range(4)]

    def body(*refs):
        ins, outs = refs[:len(args)], refs[len(args):]
        at = 0
        for idx, n_landings in enumerate(counts):
            w_ref, m_ref, v_ref = ins[at:at + 3]
            l_refs = ins[at + 3:at + 3 + n_landings]
            at += 3 + n_landings
            g_ref, d_ref, nm_ref, nv_ref = outs[4 * idx:4 * idx + 4]
            for part, l_ref in enumerate(l_refs):
                rows = slice(part * l_ref.shape[1], (part + 1) * l_ref.shape[1])
                g = l_ref[0].astype(F32)
                for s in range(1, l_ref.shape[0]):
                    g = g + l_ref[s].astype(F32)
                g_ref[rows, :] = g
                d_ref[rows, :], nm_ref[rows, :], nv_ref[rows, :] = _adamw_math(
                    w_ref[rows, :], g, m_ref[rows, :], v_ref[rows, :])

    res, _ = _launch(
        body, name="adamw_several", grid=(1,), in_specs=[_full_spec(a.shape) for a in args],
        out_specs=[_full_spec(s.shape) for s in out_shape], out_shape=out_shape, args=args)
    return [res[4 * idx:4 * idx + 4] for idx in range(len(items))]


def _small_adamw(slabs, params):
    flat = [a for name, _, _ in _SMALL for a in params[name]]
    n_in = 1 + len(flat)

    def body(*refs):
        slabs_ref, wmv = refs[0], refs[1:n_in]
        loss_ref, outs, total = refs[n_in], refs[n_in + 1:-1], refs[-1]
        me = _dev_index(*_my_place())
        g = slabs_ref[0]
        for s in range(1, N_DEV):
            g = g + slabs_ref[s]
        total[...] = g
        loss_ref[...] = total[SV_LOSS:SV_LOSS + 1, 0:1]
        for idx, (name, row, n_rows) in enumerate(_SMALL):
            w_ref, m_ref, v_ref = wmv[3 * idx:3 * idx + 3]
            g_ref, d_ref, nm_ref, nv_ref = outs[4 * idx:4 * idx + 4]
            if name == "pool_scale":
                g = total[row:row + 1, pl.ds(pl.multiple_of(me * 128, 128), 128)]
            else:
                g = total[row:row + n_rows, 0:w_ref.shape[1]]
            g_ref[...] = g
            d_ref[...], nm_ref[...], nv_ref[...] = _adamw_math(w_ref[...], g, m_ref[...], v_ref[...])

    out_shape = [jax.ShapeDtypeStruct((1, 1), F32)]
    for name, _, _ in _SMALL:
        out_shape += [jax.ShapeDtypeStruct(params[name][0].shape, F32)] * 4
    res, _ = _launch(
        body, name="small_adamw", grid=(1,),
        in_specs=[_full_spec(a.shape) for a in (slabs, *flat)], out_specs=[_full_spec(s.shape) for s in out_shape],
        out_shape=out_shape, scratch_shapes=[pltpu.VMEM((SV_ROWS, D_MODEL), F32)], args=(slabs, *flat))
    return res[0], {name: res[1 + 4 * idx:5 + 4 * idx] for idx, (name, _, _) in enumerate(_SMALL)}


def _local_step(x, p, tgt, gains, sinks, shards, weights):
    row = lambda first_row, layer: _Gain(gains, first_row + layer)
    gather = lambda *names: _AllGather(names, shards)
    g_pre_mix, g_post_mix, g_pre_ffn, g_post_ffn = SV_PRE_MIX, SV_POST_MIX, SV_PRE_FFN, SV_POST_FFN
    g_ple, g_ple_post, g_kv = SV_PLE, SV_PLE_POST, _Gain(gains, SV_KV)

    (dpool,), (wp, scale, wgu0, wd0) = _fwd_pool(x, row(g_pre_mix, 0), job=gather("pool", "scale", "gu0", "wd0"))
    wgu0, wd0 = [wgu0], [wd0]
    (x1_0, h2_0, yraw), _ = _fwd_pool_mixer(x, dpool, wp, scale, row(g_post_mix, 0), row(g_pre_ffn, 0))
    (gs0, us0, f0, x2_0, h3_0), (wgate0, wproj0, wkv, wq, wo, wd1_a) = _fwd_ffn(
        0, h2_0, x1_0, wgu0, wd0, row(g_post_ffn, 0), row(g_ple, 0),
        job=gather("gate0", "proj0", "kv", "q", "o", "wdh1_0"))
    (x3_0, z0, pe0, hk, h1, q, kpad, vpad), (wgu1_a,) = _fwd_ple_qkv(
        x2_0, h3_0, p[0], wgate0, wproj0, row(g_ple_post, 0), g_kv, row(g_pre_mix, 1), wkv, wq,
        job=gather("guh1_0"))
    (attn,), (wgu1_b,) = _fwd_attention(q, kpad, vpad, sinks, job=gather("guh1_1"))
    (y1, x1_1, h2_1), (wd1_b,) = _fwd_attn_out(attn, x3_0, wo, row(g_post_mix, 1), row(g_pre_ffn, 1),
                                               job=gather("wdh1_1"))
    wgu1, wd1 = [wgu1_a, wgu1_b], [wd1_a, wd1_b]
    (gs1, us1, f1, x2_1, h3_1), (wgate1, wproj1) = _fwd_ffn(
        1, h2_1, x1_1, wgu1, wd1, row(g_post_ffn, 1), row(g_ple, 1), job=gather("gate1", "proj1"))

    produced, swapped, landed = {}, {}, {}

    def kind_of(name):
        return name.rstrip("0123_")

    def hosted(call, *args, swap=(), spread=(), extra=None):
        jobs = []
        if swap:
            jobs.append(_SiblingSwap([(kind_of(n), produced[n]) for n in swap]))
        if spread:
            jobs.append(_ChipScatter([(kind_of(n), produced[n], swapped[n]) for n in spread]))
        if extra is not None:
            jobs.append(extra)
        jobs = _Jobs(*jobs)
        outs, job_outs = call(*args, job=jobs)
        parts = jobs.split_outputs(job_outs)
        if swap:
            swapped.update(zip(swap, parts.pop(0)))
        if spread:
            landed.update(zip(spread, parts.pop(0)))
        return outs if extra is None else (outs, parts.pop(0))

    ffn_q = lambda layer, qtr: (f"gu{layer}_{qtr}", f"wd{layer}_{qtr}")

    dx2_1, df1, produced["gate1"], produced["proj1"], dg_ple_post1, dg_ple1, dg_post_ffn1, loss = hosted(
        _ple_loss_bwd, 1, x2_1, h3_1, p[1], f1, tgt, wgate1, wproj1, row(g_ple_post, 1), row(g_ple, 1),
        row(g_post_ffn, 1))
    dh2_1, dg1, du1, a1 = hosted(_bwd_ffn_act, 1, df1, gs1, us1, wgu1, wd1, swap=("gate1", "proj1"))
    dgu1, dwd1 = hosted(_bwd_ffn_dw, 1, 0, 1, h2_1, df1, dg1, du1, a1, spread=("gate1", "proj1"))
    produced.update(guA1=dgu1, guB1=dgu1, wdA1=dwd1, wdB1=dwd1)
    dx1_1, dattn, produced["o"], dg_pre_ffn1, dg_post_mix1 = hosted(
        _bwd_attn_out, dx2_1, dh2_1, x1_1, y1, attn, wo, row(g_pre_ffn, 1), row(g_post_mix, 1),
        swap=("guA1", "wdA1", "guB1", "wdB1"))
    dq, dkv, dsinks = hosted(_bwd_attention, q, dattn, kpad, vpad, sinks, spread=("guA1", "wdA1"))
    dx3_0, produced["q"], produced["kv"], dg_pre_mix1, dg_kv = hosted(
        _bwd_qkv, dx1_1, dq, dkv, x3_0, h1, hk, wq, wkv, row(g_pre_mix, 1), g_kv, swap=("o",), spread=("wdB1",))
    dx2_0, df0, produced["gate0"], produced["proj0"], dg_ple_post0, dg_ple0, dg_post_ffn0 = hosted(
        _bwd_ple, 0, dx3_0, x2_0, z0, pe0, h3_0, p[0], f0, wgate0, row(g_ple_post, 0), row(g_ple, 0),
        row(g_post_ffn, 0), swap=("q", "kv"), spread=("guB1",))
    for half, letter in enumerate("AB"):
        landed[f"gu1_{half}"], landed[f"wd1_{half}"] = landed[f"gu{letter}1"], landed[f"wd{letter}1"]
    dh2_0, dg0, du0, a0 = hosted(_bwd_ffn_act, 0, df0, gs0, us0, wgu0, wd0,
                                 swap=("gate0", "proj0"), spread=("o", "q", "kv"))
    part_hosts = [dict(spread=("gate0", "proj0")), dict(swap=ffn_q(0, 0))]
    for part in range(FF_PARTS):
        produced[f"gu0_{part}"], produced[f"wd0_{part}"] = hosted(
            _bwd_ffn_dw, 0, part, FF_PARTS, h2_0, df0, dg0, du0, a0, **part_hosts[part])
    grad_x, produced["pool"], dscale, dg_pre_ffn0, dg_post_mix0, dg_pre_mix0 = hosted(
        _bwd_pool_mixer, dx2_0, dh2_0, x1_0, x, yraw, dpool, wp, scale, row(g_pre_ffn, 0), row(g_post_mix, 0),
        row(g_pre_mix, 0), swap=ffn_q(0, 1), spread=ffn_q(0, 0))

    def update(name, n_col_blocks, pieces):
        w, m, v = weights[name]
        rows = w.size // w.shape[-1]
        flat = [landed[n].reshape(landed[n].shape[0], -1, landed[n].shape[-1]) for n in pieces]
        outs, _ = _adamw(name, w.reshape(rows, -1), m.reshape(rows, -1), v.reshape(rows, -1), flat, n_col_blocks)
        return [o.reshape(w.shape) for o in outs]

    upd = {}
    lanes = lambda a: jnp.pad(a, ((0, 0), (0, D_MODEL - a.shape[1])))
    small = jnp.concatenate([
        dg_pre_mix0, dg_pre_mix1, dg_post_mix0, dg_post_mix1, dg_pre_ffn0, dg_pre_ffn1, dg_post_ffn0, dg_post_ffn1,
        dg_ple0, dg_ple1, dg_ple_post0, dg_ple_post1, dg_kv, dscale, lanes(dsinks[:, :N_HEADS]), lanes(loss)], axis=0)

    everyone = _ToEveryone(scattered=[("pool", produced["pool"])], gathered=[small])
    _, (landed["pool"], slabs) = hosted(_jobs_only, "scatter_tail", spread=ffn_q(0, 1), extra=everyone)
    several = {"w_ple_gate": ("gate0", "gate1"), "w_ple_proj": ("proj0", "proj1"), "w_q": ("q",), "w_kv": ("kv",),
               "w_o": ("o",), "pool_w": ("pool",)}
    flat2d = lambda a: a.reshape(-1, a.shape[-1])
    results = _adamw_several([
        (*map(flat2d, weights[name]),
         [landed[n].reshape(landed[n].shape[0], -1, landed[n].shape[-1]) for n in pieces])
        for name, pieces in several.items()])
    for name, outs in zip(several, results):
        upd[name] = [o.reshape(weights[name][0].shape) for o in outs]
    upd["w_gu"] = update("w_gu", FF_PARTS,
                         pieces=[f"gu{layer}_{qtr}" for layer in range(2) for qtr in range(FF_PARTS)])
    upd["w_gu"] = [jnp.swapaxes(a, 1, 2) for a in upd["w_gu"]]
    upd["w_down"] = update("w_down", FF_PARTS,
                           pieces=[f"wd{layer}_{qtr}" for layer in range(2) for qtr in range(FF_PARTS)])
    return grad_x, upd, slabs


def kernel(x, p, pre_mix_g, post_mix_g, pre_ffn_g, post_ffn_g, pool_w, pool_scale, kv_g, w_kv, w_q, sinks, w_o, w_gu, w_down, ple_g, w_ple_gate, w_ple_proj, ple_post_g, loss_target, m_pre_mix_g, m_post_mix_g, m_pre_ffn_g, m_post_ffn_g, m_pool_w, m_pool_scale, m_kv_g, m_w_kv, m_w_q, m_sinks, m_w_o, m_w_gu, m_w_down, m_ple_g, m_w_ple_gate, m_w_ple_proj, m_ple_post_g, v_pre_mix_g, v_post_mix_g, v_pre_ffn_g, v_post_ffn_g, v_pool_w, v_pool_scale, v_kv_g, v_w_kv, v_w_q, v_sinks, v_w_o, v_w_gu, v_w_down, v_ple_g, v_w_ple_gate, v_w_ple_proj, v_ple_post_g):
    shards = {"pool": pool_w[0].astype(BF16), "scale": pool_scale, "kv": w_kv.astype(BF16),
              "q": w_q[0].astype(BF16), "o": w_o[0].astype(BF16)}
    gu, wd = jnp.swapaxes(w_gu, 1, 2).astype(BF16), w_down.astype(BF16)
    gate, proj = w_ple_gate.astype(BF16), w_ple_proj.astype(BF16)
    for layer in range(2):
        shards[f"gu{layer}"] = (gu, layer, None)
        shards[f"wd{layer}"] = (wd, layer, None)
        for half in range(2):
            cols = (half * D_MODEL // 2, (half + 1) * D_MODEL // 2)
            shards[f"guh{layer}_{half}"] = (gu, layer, cols)
            shards[f"wdh{layer}_{half}"] = (wd, layer, cols)
        shards[f"gate{layer}"] = (gate, layer, None)
        shards[f"proj{layer}"] = (proj, layer, None)
    gains = jnp.concatenate([pre_mix_g, post_mix_g, pre_ffn_g, post_ffn_g, ple_g, ple_post_g, kv_g[None, :]],
                            axis=0).reshape(-1, 1, D_MODEL)
    weights = {"pool_w": (pool_w, m_pool_w, v_pool_w), "w_kv": (w_kv, m_w_kv, v_w_kv), "w_q": (w_q, m_w_q, v_w_q),
               "w_o": (w_o, m_w_o, v_w_o), "w_down": (w_down, m_w_down, v_w_down),
               "w_gu": tuple(jnp.swapaxes(a, 1, 2) for a in (w_gu, m_w_gu, v_w_gu)),
               "w_ple_gate": (w_ple_gate, m_w_ple_gate, v_w_ple_gate),
               "w_ple_proj": (w_ple_proj, m_w_ple_proj, v_w_ple_proj)}
    per_layer = p.reshape(p.shape[0], *p.shape[2:])
    p_rows = [_LayerRows(per_layer, layer) for layer in range(2)]
    grad_x, upd, slabs = _local_step(x[0], p_rows, loss_target[0], gains, sinks, shards, weights)

    small_params = {
        "pre_mix_g": (pre_mix_g, m_pre_mix_g, v_pre_mix_g), "post_mix_g": (post_mix_g, m_post_mix_g, v_post_mix_g),
        "pre_ffn_g": (pre_ffn_g, m_pre_ffn_g, v_pre_ffn_g), "post_ffn_g": (post_ffn_g, m_post_ffn_g, v_post_ffn_g),
        "ple_g": (ple_g, m_ple_g, v_ple_g), "ple_post_g": (ple_post_g, m_ple_post_g, v_ple_post_g),
        "kv_g": (kv_g[None, :], m_kv_g[None, :], v_kv_g[None, :]),
        "pool_scale": (pool_scale, m_pool_scale, v_pool_scale), "sinks": (sinks, m_sinks, v_sinks)}
    loss, small_upd = _small_adamw(slabs, small_params)
    small_upd["kv_g"] = [a[0] for a in small_upd["kv_g"]]
    upd.update(small_upd)

    names = ["pre_mix_g", "post_mix_g", "pre_ffn_g", "post_ffn_g", "pool_w", "pool_scale", "kv_g", "w_kv", "w_q",
             "sinks", "w_o", "w_gu", "w_down", "ple_g", "w_ple_gate", "w_ple_proj", "ple_post_g"]
    outs = [loss[0, 0], grad_x[None]]
    for kind in range(4):
        outs += [upd[n][kind] for n in names]
    return tuple(outs)
```

```python
import functools
import types

import jax
import jax.numpy as jnp
from jax import lax
from jax.experimental import pallas as pl
from jax.experimental.pallas import tpu as pltpu

F32 = jnp.float32
BF16 = jnp.bfloat16

N_DEV = 8
D_MODEL = 1024
N_POOL_GROUPS = 4
POOL_GROUP = 256
POOL_HALO = 16
HEAD_DIM = 64
N_HEADS = 16
N_KV_HEADS = 4
GQA_GROUP = 4
KV_DIM = N_KV_HEADS * HEAD_DIM
ATT_BLOCK = 128
D_FF = 2816
FF_CHUNKS = 4
FF_BLOCK = D_FF // FF_CHUNKS
WD_ROWS = D_FF // N_DEV
FF_PARTS = 2
FF_PART = D_MODEL // FF_PARTS
N_CHIPS = 4
PLE_DIM = 256
EPS = 1e-6
NEG_INF = -1e30
ATT_SCALE = HEAD_DIM ** -0.5

ADAM_LR = 0.001
ADAM_B1 = 0.9
ADAM_B2 = 0.999
ADAM_EPS = 1e-08
ADAM_WD = 0.01
ADAM_STEP = 10

ROW_TILE = 512
FFN_ROW_TILE = 512
FFN_WEIGHT_COLS = 512
FFN_SUB_TILES = 1
VMEM_BIG = 60 * 1024 * 1024
VMEM_MID = 56 * 1024 * 1024
HBM_PIN_ELEMS = 1024

SV_ROWS = 16
SV_PRE_MIX, SV_POST_MIX, SV_PRE_FFN, SV_POST_FFN, SV_PLE, SV_PLE_POST = 0, 2, 4, 6, 8, 10
SV_KV, SV_POOL_SCALE, SV_SINKS, SV_LOSS = 12, 13, 14, 15

MESH = pl.DeviceIdType.MESH
ANY = pl.BlockSpec(memory_space=pl.ANY)


def _dot(a, b):
    return jnp.dot(a, b, preferred_element_type=F32)


def _dot_nt(a, b):
    return lax.dot_general(a, b, (((1,), (1,)), ((), ())), preferred_element_type=F32)


def _dot_tn(a, b):
    return lax.dot_general(a, b, (((0,), (0,)), ((), ())), preferred_element_type=F32)


def _rstd(x):
    return lax.rsqrt(jnp.mean(x * x, axis=-1, keepdims=True) + EPS)


def _rms(x, g):
    return x * _rstd(x) * g


def _rms_bwd(x, g, dy):
    r = _rstd(x)
    n = x * r
    dn = dy * g
    dx = r * (dn - n * jnp.mean(dn * n, axis=-1, keepdims=True))
    dg = jnp.sum(dy * n, axis=0, keepdims=True)
    return dx, dg


def _add_all(terms):
    return functools.reduce(jnp.add, terms)


def _sigmoid(x):
    return 1.0 / (1.0 + jnp.exp(-x))


def _acc(ref, val, first):
    @pl.when(first)
    def _():
        ref[...] = val

    @pl.when(jnp.logical_not(first))
    def _():
        ref[...] += val


def _pool_counts(row0, rows):
    t = row0 + lax.broadcasted_iota(jnp.int32, (rows, D_MODEL), 0) + 1
    grp = lax.broadcasted_iota(jnp.int32, (rows, D_MODEL), 1) // POOL_GROUP
    win = jnp.left_shift(2, grp)
    return jnp.minimum(t, win).astype(F32)


def _window_sums(ext, shift_of):
    outs = []
    s = ext
    for gi in range(N_POOL_GROUPS):
        s = s + pltpu.roll(s, shift_of(1 << gi), axis=0)
        outs.append(s[:, :POOL_GROUP])
        s = s[:, POOL_GROUP:]
    return jnp.concatenate(outs, axis=1)


def _cparams(n_axes, vmem, collective_id=None):
    return pltpu.CompilerParams(dimension_semantics=("arbitrary",) * n_axes, vmem_limit_bytes=vmem,
                                collective_id=collective_id)


_EVERYONE = ("sibling", "x", "y", "far", "x sibling", "y sibling", "far sibling")
_PEER_SETS = (("sibling", "x", "y"), ("sibling",), ("x", "y"), _EVERYONE)


def _meet(peers):
    x, y, c = lax.axis_index("x"), lax.axis_index("y"), lax.axis_index("c")
    device = {"sibling": (x, y, 1 - c), "x": (1 - x, y, c), "y": (x, 1 - y, c), "far": (1 - x, 1 - y, c),
              "x sibling": (1 - x, y, 1 - c), "y sibling": (x, 1 - y, 1 - c), "far sibling": (1 - x, 1 - y, 1 - c)}
    barrier = pltpu.get_barrier_semaphore()
    for peer in peers:
        pl.semaphore_signal(barrier, inc=1, device_id=device[peer], device_id_type=pl.DeviceIdType.MESH)
    pl.semaphore_wait(barrier, len(peers))


def _row_spec(cols, tm=ROW_TILE):
    return pl.BlockSpec((tm, cols), lambda i: (i, 0))


def _full_spec(shape):
    zeros = (0,) * len(shape)
    return pl.BlockSpec(shape, lambda *_: zeros)


def _vec_spec():
    return _full_spec((1, D_MODEL))


def _column_views(parts):
    return [(a, b) for a in parts for b in range(a.shape[-1] // FFN_WEIGHT_COLS)]


def _column_ranges(views):
    return [(n * FFN_WEIGHT_COLS, (n + 1) * FFN_WEIGHT_COLS) for n in range(len(views))]


class _Gain:
    def __init__(self, stacked, layer):
        self.stacked, self.layer = stacked, layer

    def spec(self):
        layer = self.layer
        return pl.BlockSpec((None, 1, D_MODEL), lambda *_: (layer, 0, 0))


class _LayerRows:
    def __init__(self, stacked, layer):
        self.stacked, self.layer = stacked, layer

    def spec(self):
        layer = self.layer
        return pl.BlockSpec((None, ROW_TILE, self.stacked.shape[-1]), lambda i: (layer, i, 0))


def _in_hbm(a):
    return pltpu.with_memory_space_constraint(a, pltpu.HBM) if a.size >= HBM_PIN_ELEMS else a


def _out_in_hbm(s):
    return pltpu.HBM(s.shape, s.dtype) if s.size >= HBM_PIN_ELEMS else s


def _launch(body, *, name, grid, in_specs, out_specs, out_shape, args, scratch_shapes=(), vmem=VMEM_MID, job=None):
    picked = (_Gain, _LayerRows)
    in_specs = [a.spec() if isinstance(a, picked) else s for s, a in zip(in_specs, args)]
    args = [_in_hbm(a.stacked if isinstance(a, picked) else a) for a in args]
    n_in, n_out, n_scr = len(args), len(out_shape), len(scratch_shapes)
    if job is not None and not job.args:
        job = None
    j_args, j_out, j_scr = ([], [], []) if job is None else ([_in_hbm(a) for a in job.args], job.out_shape, job.scratch)

    def run(*refs):
        groups, at = [], 0
        for n in (n_in, len(j_args), n_out, len(j_out), n_scr, len(j_scr)):
            groups.append(refs[at:at + n])
            at += n
        ins, j_ins, outs, j_outs, scr, j_sems = groups

        def begin():
            _meet(job.peers)
            job.start(j_ins, j_outs, j_sems)

        if job is None:
            body(*ins, *outs, *scr)
        elif not grid:
            begin()
            job.mid(j_ins, j_outs, j_sems)
            job.late(j_ins, j_outs, j_sems)
            body(*ins, *outs, *scr)
            job.finish(j_ins, j_outs, j_sems)
        else:
            ids = [pl.program_id(a) for a in range(len(grid))]
            at_start = lambda step: functools.reduce(jnp.logical_and, [ids[0] == step] + [i == 0 for i in ids[1:]])
            last = functools.reduce(jnp.logical_and, [i == g - 1 for i, g in zip(ids, grid)])
            pl.when(at_start(0))(begin)
            pl.when(at_start(grid[0] // 2))(lambda: job.mid(j_ins, j_outs, j_sems))
            pl.when(at_start(3 * grid[0] // 4))(lambda: job.late(j_ins, j_outs, j_sems))
            body(*ins, *outs, *scr)
            pl.when(last)(lambda: job.finish(j_ins, j_outs, j_sems))

    res = pl.pallas_call(
        run, name=name, grid=grid,
        in_specs=list(in_specs) + [ANY] * len(j_args), out_specs=list(out_specs) + [ANY] * len(j_out),
        out_shape=[_out_in_hbm(s) for s in list(out_shape) + list(j_out)],
        scratch_shapes=list(scratch_shapes) + list(j_scr),
        compiler_params=_cparams(len(grid), vmem, None if job is None else _PEER_SETS.index(job.peers)),
    )(*args, *j_args)
    return res[:n_out], res[n_out:]


def _fwd_pool(x, g_pre, job=None):
    T = x.shape[0]
    tm = ROW_TILE
    nt = T // tm

    def body(x_ref, gpre_ref, d_ref, carry):
        i = pl.program_id(0)

        @pl.when(i == 0)
        def _():
            carry[...] = jnp.zeros_like(carry)

        h = _rms(x_ref[...], gpre_ref[...])
        ext = jnp.concatenate([carry[...], h], axis=0)
        carry[...] = h[tm - POOL_HALO:, :]
        sums = _window_sums(ext, lambda k: k)[POOL_HALO:, :]
        d_ref[...] = (sums / _pool_counts(i * tm, tm) - h).astype(BF16)

    return _launch(
        body, name="fwd_pool", grid=(nt,), in_specs=[_row_spec(D_MODEL), _vec_spec()], out_specs=[_row_spec(D_MODEL)],
        out_shape=[jax.ShapeDtypeStruct((T, D_MODEL), BF16)], scratch_shapes=[pltpu.VMEM((POOL_HALO, D_MODEL), F32)],
        args=(x, g_pre), job=job)


def _fwd_pool_mixer(x, d, wp, scale, g_post, g_ffn, job=None):
    T = x.shape[0]
    nt = T // ROW_TILE

    def body(x_ref, d_ref, wp_ref, sc_ref, gpost_ref, gffn_ref, x1_ref, h2_ref, yraw_ref):
        db = d_ref[...]
        yraw = jnp.concatenate(
            [_dot(db[:, g * POOL_GROUP:(g + 1) * POOL_GROUP], wp_ref[g]) for g in range(N_POOL_GROUPS)], axis=1)
        yraw_ref[...] = yraw.astype(BF16)
        x1 = x_ref[...] + _rms(yraw * sc_ref[...], gpost_ref[...])
        x1_ref[...] = x1
        h2_ref[...] = _rms(x1, gffn_ref[...]).astype(BF16)

    return _launch(
        body, name="fwd_pool_mixer", grid=(nt,),
        in_specs=[_row_spec(D_MODEL), _row_spec(D_MODEL), _full_spec((N_POOL_GROUPS, POOL_GROUP, POOL_GROUP)),
                  _vec_spec(), _vec_spec(), _vec_spec()],
        out_specs=[_row_spec(D_MODEL)] * 3,
        out_shape=[jax.ShapeDtypeStruct((T, D_MODEL), F32)] + [jax.ShapeDtypeStruct((T, D_MODEL), BF16)] * 2,
        args=(x, d, wp, scale, g_post, g_ffn), job=job)


def _fwd_ffn(layer, h2, x1, wgu, wd, g_post, g_ple, job=None):
    T = h2.shape[0]
    tm = min(FFN_ROW_TILE, T)
    nt = T // tm
    sub = tm // FFN_SUB_TILES
    last = FF_CHUNKS - 1
    wgu, wd = _column_views(wgu), _column_views(wd)
    n_gu, n_wd = len(wgu), len(wd)
    gu_cols = _column_ranges(wgu)

    def body(h2_ref, x1_ref, *refs):
        wgu_refs, wd_refs = refs[:n_gu], refs[n_gu:n_gu + n_wd]
        gpost_ref, gple_ref, gs_ref, us_ref, f_ref, x2_ref, h3_ref, acc = refs[n_gu + n_wd:]
        k = pl.program_id(0)
        i = pl.program_id(1)
        rows = pl.ds(pl.multiple_of(i * tm, tm), tm)
        parts = []
        for s in range(FFN_SUB_TILES):
            r = pl.ds(s * sub, sub)
            g = _add_all([_dot_nt(h2_ref[r, c0:c1], w[0]) for (c0, c1), w in zip(gu_cols, wgu_refs)])
            u = _add_all([_dot_nt(h2_ref[r, c0:c1], w[1]) for (c0, c1), w in zip(gu_cols, wgu_refs)])
            gs_ref[r, :] = g.astype(BF16)
            us_ref[r, :] = u.astype(BF16)
            a = (g * _sigmoid(g) * u).astype(BF16)
            parts.append(jnp.concatenate([_dot(a, w[...]) for w in wd_refs], axis=1))
        part = jnp.concatenate(parts, axis=0)

        @pl.when(k == 0)
        def _():
            acc[rows, :] = part

        @pl.when(jnp.logical_and(k > 0, k < last))
        def _():
            acc[rows, :] += part

        @pl.when(k == last)
        def _():
            f = acc[rows, :] + part
            f_ref[...] = f.astype(BF16)
            x2 = x1_ref[...] + _rms(f, gpost_ref[...])
            x2_ref[...] = x2
            h3_ref[...] = _rms(x2, gple_ref[...]).astype(BF16)

    def late(k, i):
        return (jnp.where(k == last, i, 0), 0)

    return _launch(
        body, name=f"fwd_ffn{layer}", grid=(FF_CHUNKS, nt),
        in_specs=[pl.BlockSpec((tm, D_MODEL), lambda k, i: (i, 0)), pl.BlockSpec((tm, D_MODEL), late)]
                 + [pl.BlockSpec((None, 2, FF_BLOCK, FFN_WEIGHT_COLS), lambda k, i, b=b: (k, 0, 0, b)) for _, b in wgu]
                 + [pl.BlockSpec((FF_BLOCK, FFN_WEIGHT_COLS), lambda k, i, b=b: (k, b)) for _, b in wd]
                 + [pl.BlockSpec((1, D_MODEL), lambda k, i: (0, 0))] * 2,
        out_specs=[pl.BlockSpec((None, tm, FF_BLOCK), lambda k, i: (k, i, 0)),
                   pl.BlockSpec((None, tm, FF_BLOCK), lambda k, i: (k, i, 0)),
                   pl.BlockSpec((tm, D_MODEL), late),
                   pl.BlockSpec((tm, D_MODEL), late),
                   pl.BlockSpec((tm, D_MODEL), late)],
        out_shape=[jax.ShapeDtypeStruct((FF_CHUNKS, T, FF_BLOCK), BF16),
                   jax.ShapeDtypeStruct((FF_CHUNKS, T, FF_BLOCK), BF16),
                   jax.ShapeDtypeStruct((T, D_MODEL), BF16),
                   jax.ShapeDtypeStruct((T, D_MODEL), F32),
                   jax.ShapeDtypeStruct((T, D_MODEL), BF16)],
        scratch_shapes=[pltpu.VMEM((T, D_MODEL), F32)],
        args=(h2, x1, *[w for w, _ in wgu], *[w for w, _ in wd], g_post, g_ple), vmem=VMEM_BIG, job=job)


def _fwd_ple_qkv(x2, h3, p, wgate, wproj, g_post, g_kv, g_mix, wkv, wq, job=None):
    T = x2.shape[0]
    nt = T // ROW_TILE

    def body(x2_ref, h3_ref, p_ref, wg_ref, wp_ref, gpost_ref, gkv_ref, gmix_ref, wkv_ref, wq_ref,
             x3_ref, z_ref, pe_ref, hk_ref, h1_ref, q_ref, kpad_ref, vpad_ref):
        z = _dot(h3_ref[...], wg_ref[...])
        pe = _dot(p_ref[...].astype(BF16), wp_ref[...])
        z_ref[...] = z.astype(BF16)
        pe_ref[...] = pe.astype(BF16)
        x3 = x2_ref[...] + _rms(pe * _sigmoid(z), gpost_ref[...])
        x3_ref[...] = x3
        r = _rstd(x3)
        hk = (x3 * r * gkv_ref[...]).astype(BF16)
        h1 = (x3 * r * gmix_ref[...]).astype(BF16)
        hk_ref[...] = hk
        h1_ref[...] = h1
        kv = _dot(hk, wkv_ref[...]).astype(BF16)
        q_ref[...] = _dot(h1, wq_ref[...]).astype(BF16)
        i = pl.program_id(0)

        @pl.when(i == 0)
        def _():
            kpad_ref[:ATT_BLOCK, :] = jnp.zeros((ATT_BLOCK, KV_DIM), BF16)
            vpad_ref[:ATT_BLOCK, :] = jnp.zeros((ATT_BLOCK, KV_DIM), BF16)

        rows = pl.ds(pl.multiple_of(ATT_BLOCK + i * ROW_TILE, ATT_BLOCK), ROW_TILE)
        kpad_ref[rows, :] = kv[:, :KV_DIM]
        vpad_ref[rows, :] = kv[:, KV_DIM:]

    wide = jax.ShapeDtypeStruct((T, D_MODEL), BF16)
    padded = (ATT_BLOCK + T, KV_DIM)
    return _launch(
        body, name="fwd_ple_qkv", grid=(nt,),
        in_specs=[_row_spec(D_MODEL), _row_spec(D_MODEL), _row_spec(PLE_DIM), _full_spec((D_MODEL, D_MODEL)),
                  _full_spec((PLE_DIM, D_MODEL)), _vec_spec(), _vec_spec(), _vec_spec(),
                  _full_spec((D_MODEL, 2 * KV_DIM)), _full_spec((D_MODEL, D_MODEL))],
        out_specs=[_row_spec(D_MODEL)] * 6 + [_full_spec(padded)] * 2,
        out_shape=[jax.ShapeDtypeStruct((T, D_MODEL), F32)] + [wide] * 5 + [jax.ShapeDtypeStruct(padded, BF16)] * 2,
        args=(x2, h3, p, wgate, wproj, g_post, g_kv, g_mix, wkv, wq), job=job)


def _alibi_slope(h):
    return 2.0 ** (-8.0 * (h + 1) / N_HEADS)


ATT_SUB = 32
ATT_GROUP_ROWS = GQA_GROUP * ATT_BLOCK


def _att_mask(n, rel_ref, off_ref):
    qi = lax.broadcasted_iota(jnp.int32, (ATT_BLOCK, 2 * ATT_BLOCK), 0)
    si = lax.broadcasted_iota(jnp.int32, (ATT_BLOCK, 2 * ATT_BLOCK), 1)
    rel = ATT_BLOCK + qi - si
    valid = (rel >= 0) & (rel < ATT_BLOCK) & ((si >= ATT_BLOCK) | (n > 0))
    rel_ref[...] = rel.astype(F32)
    off_ref[...] = jnp.where(valid, 0.0, NEG_INF)


def _att_probs(raw, relf, off, slope, sink):
    s = raw * ATT_SCALE - slope * relf + off
    m = jnp.maximum(jnp.max(s, axis=-1, keepdims=True), sink)
    e = jnp.exp(s - m)
    es = jnp.exp(sink - m)
    inv = 1.0 / (jnp.sum(e, axis=-1, keepdims=True) + es)
    return e * inv, es * inv


def _stack_heads(ref, kh):
    first = kh * GQA_GROUP
    return jnp.concatenate([ref[:, (first + g) * HEAD_DIM:(first + g + 1) * HEAD_DIM] for g in range(GQA_GROUP)], axis=0)


def _unstack_heads(stacked):
    return [stacked[g * ATT_BLOCK:(g + 1) * ATT_BLOCK, :] for g in range(GQA_GROUP)]


def _fwd_attention(q, kpad, vpad, sinks, job=None):
    T = q.shape[0]
    nb = T // ATT_BLOCK

    def body(q_ref, k_ref, v_ref, sink_ref, o_ref, s_scr, p_scr, rel_scr, off_scr):
        n = pl.program_id(0)
        start = pl.multiple_of(n * ATT_BLOCK, ATT_BLOCK)
        kw = k_ref[pl.ds(start, 2 * ATT_BLOCK), :]
        vw = v_ref[pl.ds(start, 2 * ATT_BLOCK), :]
        _att_mask(n, rel_scr, off_scr)
        outs = []
        for kh in range(N_KV_HEADS):
            kk = kw[:, kh * HEAD_DIM:(kh + 1) * HEAD_DIM]
            vv = vw[:, kh * HEAD_DIM:(kh + 1) * HEAD_DIM]
            s_scr[...] = _dot_nt(_stack_heads(q_ref, kh), kk)
            for g in range(GQA_GROUP):
                h = kh * GQA_GROUP + g
                for row0 in range(0, ATT_BLOCK, ATT_SUB):
                    rows, sub = pl.ds(g * ATT_BLOCK + row0, ATT_SUB), pl.ds(row0, ATT_SUB)
                    pr, _ = _att_probs(s_scr[rows, :], rel_scr[sub, :], off_scr[sub, :], _alibi_slope(h),
                                       sink_ref[0, h])
                    p_scr[rows, :] = pr.astype(BF16)
            outs += _unstack_heads(_dot(p_scr[...], vv))
        o_ref[...] = jnp.concatenate(outs, axis=1).astype(BF16)

    return _launch(
        body, name="fwd_attention", grid=(nb,),
        in_specs=[_row_spec(D_MODEL, ATT_BLOCK), _full_spec((T + ATT_BLOCK, KV_DIM)), _full_spec((T + ATT_BLOCK, KV_DIM)),
                  pl.BlockSpec(memory_space=pltpu.SMEM)],
        out_specs=[_row_spec(D_MODEL, ATT_BLOCK)],
        out_shape=[jax.ShapeDtypeStruct((T, D_MODEL), BF16)],
        scratch_shapes=[pltpu.VMEM((ATT_GROUP_ROWS, 2 * ATT_BLOCK), F32), pltpu.VMEM((ATT_GROUP_ROWS, 2 * ATT_BLOCK), BF16)]
                       + [pltpu.VMEM((ATT_BLOCK, 2 * ATT_BLOCK), F32)] * 2,
        args=(q, kpad, vpad, sinks), job=job)


def _fwd_attn_out(attn, x, wo, g_post, g_ffn, job=None):
    T = x.shape[0]
    nt = T // ROW_TILE

    def body(a_ref, x_ref, wo_ref, gpost_ref, gffn_ref, y_ref, x1_ref, h2_ref):
        y = _dot(a_ref[...], wo_ref[...])
        y_ref[...] = y.astype(BF16)
        x1 = x_ref[...] + _rms(y, gpost_ref[...])
        x1_ref[...] = x1
        h2_ref[...] = _rms(x1, gffn_ref[...]).astype(BF16)

    return _launch(
        body, name="fwd_attn_out", grid=(nt,),
        in_specs=[_row_spec(D_MODEL), _row_spec(D_MODEL), _full_spec((D_MODEL, D_MODEL)), _vec_spec(), _vec_spec()],
        out_specs=[_row_spec(D_MODEL)] * 3,
        out_shape=[jax.ShapeDtypeStruct((T, D_MODEL), BF16), jax.ShapeDtypeStruct((T, D_MODEL), F32),
                   jax.ShapeDtypeStruct((T, D_MODEL), BF16)],
        args=(attn, x, wo, g_post, g_ffn), job=job)


def _bwd_ple(layer, dx3, x2, z, pe, h3, p, f, wgate, g_ple_post, g_ple, g_post_ffn, job=None):
    T = x2.shape[0]
    tm = ROW_TILE
    nt = T // tm

    def body(dx3_ref, x2_ref, z_ref, pe_ref, h3_ref, p_ref, f_ref, wg_ref, gpp_ref, gp_ref, gpf_ref,
             dx2_ref, df_ref, dwg_ref, dwp_ref, dgpp_ref, dgp_ref, dgpf_ref, acc_g, acc_p):
        i = pl.program_id(0)
        first = i == 0
        dx3v = dx3_ref[...]
        gate = _sigmoid(z_ref[...].astype(F32))
        pev = pe_ref[...].astype(F32)
        de, dgpp = _rms_bwd(pev * gate, gpp_ref[...], dx3v)
        dpe = (de * gate).astype(BF16)
        dz = (de * pev * gate * (1.0 - gate)).astype(BF16)
        _acc(acc_p, _dot_tn(p_ref[...].astype(BF16), dpe), first)
        _acc(acc_g, _dot_tn(h3_ref[...], dz), first)
        dh3 = _dot_nt(dz, wg_ref[...])
        dxn, dgp = _rms_bwd(x2_ref[...], gp_ref[...], dh3)
        dx2 = dx3v + dxn
        dx2_ref[...] = dx2
        df, dgpf = _rms_bwd(f_ref[...].astype(F32), gpf_ref[...], dx2)
        df_ref[...] = df.astype(BF16)
        _acc(dgpp_ref, dgpp, first)
        _acc(dgp_ref, dgp, first)
        _acc(dgpf_ref, dgpf, first)

        @pl.when(i == nt - 1)
        def _():
            dwg_ref[...] = acc_g[...].astype(BF16)
            dwp_ref[...] = acc_p[...].astype(BF16)

    return _launch(
        body, name=f"bwd_ple{layer}", grid=(nt,),
        in_specs=[_row_spec(D_MODEL)] * 5 + [_row_spec(PLE_DIM), _row_spec(D_MODEL), _full_spec((D_MODEL, D_MODEL)),
                  _vec_spec(), _vec_spec(), _vec_spec()],
        out_specs=[_row_spec(D_MODEL), _row_spec(D_MODEL), _full_spec((D_MODEL, D_MODEL)), _full_spec((PLE_DIM, D_MODEL)),
                   _vec_spec(), _vec_spec(), _vec_spec()],
        out_shape=[jax.ShapeDtypeStruct((T, D_MODEL), F32), jax.ShapeDtypeStruct((T, D_MODEL), BF16),
                   jax.ShapeDtypeStruct((D_MODEL, D_MODEL), BF16), jax.ShapeDtypeStruct((PLE_DIM, D_MODEL), BF16)]
                  + [jax.ShapeDtypeStruct((1, D_MODEL), F32)] * 3,
        scratch_shapes=[pltpu.VMEM((D_MODEL, D_MODEL), F32), pltpu.VMEM((PLE_DIM, D_MODEL), F32)],
        args=(dx3, x2, z, pe, h3, p, f, wgate, g_ple_post, g_ple, g_post_ffn), vmem=VMEM_BIG, job=job)


def _ple_loss_bwd(layer, x2, h3, p, f, target, wgate, wproj, g_ple_post, g_ple, g_post_ffn, job=None):
    T = x2.shape[0]
    tm = ROW_TILE
    nt = T // tm

    def body(x2_ref, h3_ref, p_ref, f_ref, tgt_ref, wg_ref, wp_ref, gpp_ref, gp_ref, gpf_ref,
             dx2_ref, df_ref, dwg_ref, dwp_ref, dgpp_ref, dgp_ref, dgpf_ref, loss_ref, acc_g, acc_p):
        i = pl.program_id(0)
        first = i == 0
        h3 = h3_ref[...]
        pb = p_ref[...].astype(BF16)
        x2v = x2_ref[...]
        gate = _sigmoid(_dot(h3, wg_ref[...]))
        pev = _dot(pb, wp_ref[...])
        e = pev * gate
        err = x2v + _rms(e, gpp_ref[...]) - tgt_ref[...]
        _acc(loss_ref, 0.5 * jnp.sum(jnp.mean(err * err, axis=-1, keepdims=True), axis=0, keepdims=True), first)
        dx3v = err * (1.0 / D_MODEL)
        de, dgpp = _rms_bwd(e, gpp_ref[...], dx3v)
        dpe = (de * gate).astype(BF16)
        dz = (de * pev * gate * (1.0 - gate)).astype(BF16)
        _acc(acc_p, _dot_tn(pb, dpe), first)
        _acc(acc_g, _dot_tn(h3, dz), first)
        dxn, dgp = _rms_bwd(x2v, gp_ref[...], _dot_nt(dz, wg_ref[...]))
        dx2 = dx3v + dxn
        dx2_ref[...] = dx2
        df, dgpf = _rms_bwd(f_ref[...].astype(F32), gpf_ref[...], dx2)
        df_ref[...] = df.astype(BF16)
        _acc(dgpp_ref, dgpp, first)
        _acc(dgp_ref, dgp, first)
        _acc(dgpf_ref, dgpf, first)

        @pl.when(i == nt - 1)
        def _():
            dwg_ref[...] = acc_g[...].astype(BF16)
            dwp_ref[...] = acc_p[...].astype(BF16)

    return _launch(
        body, name=f"ple_loss_bwd{layer}", grid=(nt,),
        in_specs=[_row_spec(D_MODEL), _row_spec(D_MODEL), _row_spec(PLE_DIM), _row_spec(D_MODEL), _row_spec(D_MODEL),
                  _full_spec((D_MODEL, D_MODEL)), _full_spec((PLE_DIM, D_MODEL)), _vec_spec(), _vec_spec(), _vec_spec()],
        out_specs=[_row_spec(D_MODEL), _row_spec(D_MODEL), _full_spec((D_MODEL, D_MODEL)), _full_spec((PLE_DIM, D_MODEL)),
                   _vec_spec(), _vec_spec(), _vec_spec(), _full_spec((1, 1))],
        out_shape=[jax.ShapeDtypeStruct((T, D_MODEL), F32), jax.ShapeDtypeStruct((T, D_MODEL), BF16),
                   jax.ShapeDtypeStruct((D_MODEL, D_MODEL), BF16), jax.ShapeDtypeStruct((PLE_DIM, D_MODEL), BF16)]
                  + [jax.ShapeDtypeStruct((1, D_MODEL), F32)] * 3 + [jax.ShapeDtypeStruct((1, 1), F32)],
        scratch_shapes=[pltpu.VMEM((D_MODEL, D_MODEL), F32), pltpu.VMEM((PLE_DIM, D_MODEL), F32)],
        args=(x2, h3, p, f, target, wgate, wproj, g_ple_post, g_ple, g_post_ffn), vmem=VMEM_BIG, job=job)


def _bwd_ffn_act(layer, df, gs, us, wgu, wd, job=None):
    T = df.shape[0]
    tm = min(FFN_ROW_TILE, T)
    nt = T // tm
    sub = tm // FFN_SUB_TILES
    last = FF_CHUNKS - 1
    wgu, wd = _column_views(wgu), _column_views(wd)
    n_gu, n_wd = len(wgu), len(wd)
    wd_cols = _column_ranges(wd)

    def body(df_ref, gs_ref, us_ref, *refs):
        wgu_refs, wd_refs = refs[:n_gu], refs[n_gu:n_gu + n_wd]
        dh_ref, dg_ref, du_ref, a_ref, acc_h = refs[n_gu + n_wd:]
        k = pl.program_id(0)
        i = pl.program_id(1)
        rows = pl.ds(pl.multiple_of(i * tm, tm), tm)
        dhs = []
        for s in range(FFN_SUB_TILES):
            r = pl.ds(s * sub, sub)
            g = gs_ref[r, :].astype(F32)
            u = us_ref[r, :].astype(F32)
            sg = _sigmoid(g)
            silu = g * sg
            a_ref[r, :] = (silu * u).astype(BF16)
            da = _add_all([_dot_nt(df_ref[r, c0:c1], w[...]) for (c0, c1), w in zip(wd_cols, wd_refs)])
            dg = (da * u * (sg * (1.0 + g * (1.0 - sg)))).astype(BF16)
            du = (da * silu).astype(BF16)
            dg_ref[r, :] = dg
            du_ref[r, :] = du
            dhs.append(jnp.concatenate([_dot(dg, w[0]) + _dot(du, w[1]) for w in wgu_refs], axis=1))
        dh = jnp.concatenate(dhs, axis=0)

        @pl.when(k == 0)
        def _():
            acc_h[rows, :] = dh

        @pl.when(jnp.logical_and(k > 0, k < last))
        def _():
            acc_h[rows, :] += dh

        @pl.when(k == last)
        def _():
            dh_ref[...] = acc_h[rows, :] + dh

    chunk_rows = pl.BlockSpec((None, tm, FF_BLOCK), lambda k, i: (k, i, 0))
    saved = jax.ShapeDtypeStruct((FF_CHUNKS, T, FF_BLOCK), BF16)
    return _launch(
        body, name=f"bwd_ffn_act{layer}", grid=(FF_CHUNKS, nt),
        in_specs=[pl.BlockSpec((tm, D_MODEL), lambda k, i: (i, 0)), chunk_rows, chunk_rows]
                 + [pl.BlockSpec((None, 2, FF_BLOCK, FFN_WEIGHT_COLS), lambda k, i, b=b: (k, 0, 0, b)) for _, b in wgu]
                 + [pl.BlockSpec((FF_BLOCK, FFN_WEIGHT_COLS), lambda k, i, b=b: (k, b)) for _, b in wd],
        out_specs=[pl.BlockSpec((tm, D_MODEL), lambda k, i: (jnp.where(k == last, i, 0), 0)),
                   chunk_rows, chunk_rows, chunk_rows],
        out_shape=[jax.ShapeDtypeStruct((T, D_MODEL), F32), saved, saved, saved],
        scratch_shapes=[pltpu.VMEM((T, D_MODEL), F32)],
        args=(df, gs, us, *[w for w, _ in wgu], *[w for w, _ in wd]), vmem=VMEM_BIG, job=job)


def _bwd_ffn_dw(layer, q, parts, h2, df, dg, du, a, job=None):
    T = h2.shape[0]
    width = D_MODEL // parts

    def body(h_ref, df_ref, dg_ref, du_ref, a_ref, dgu_ref, dwd_ref):
        h = h_ref[...]
        dgu_ref[0] = _dot_tn(dg_ref[...], h).astype(BF16)
        dgu_ref[1] = _dot_tn(du_ref[...], h).astype(BF16)
        dwd_ref[...] = _dot_tn(a_ref[...], df_ref[...]).astype(BF16)

    cols = pl.BlockSpec((T, width), lambda k: (0, q))
    chunk = pl.BlockSpec((None, T, FF_BLOCK), lambda k: (k, 0, 0))
    return _launch(
        body, name=f"bwd_ffn_dw{layer}_{q}", grid=(FF_CHUNKS,),
        in_specs=[cols, cols, chunk, chunk, chunk],
        out_specs=[pl.BlockSpec((None, 2, FF_BLOCK, width), lambda k: (k, 0, 0, 0)),
                   pl.BlockSpec((FF_BLOCK, width), lambda k: (k, 0))],
        out_shape=[jax.ShapeDtypeStruct((FF_CHUNKS, 2, FF_BLOCK, width), BF16),
                   jax.ShapeDtypeStruct((D_FF, width), BF16)],
        args=(h2, df, dg, du, a), vmem=VMEM_BIG, job=job)


def _bwd_attn_out(dx2, dh2, x1, y, attn, wo, g_ffn, g_post, job=None):
    T = x1.shape[0]
    nt = T // ROW_TILE

    def body(dx2_ref, dh2_ref, x1_ref, y_ref, a_ref, wo_ref, gffn_ref, gpost_ref,
             dx1_ref, da_ref, dwo_ref, dgf_ref, dgp_ref, acc):
        i = pl.program_id(0)
        first = i == 0
        dxn, dgf = _rms_bwd(x1_ref[...], gffn_ref[...], dh2_ref[...])
        dx1 = dx2_ref[...] + dxn
        dx1_ref[...] = dx1
        dy, dgp = _rms_bwd(y_ref[...].astype(F32), gpost_ref[...], dx1)
        dyb = dy.astype(BF16)
        da_ref[...] = _dot_nt(dyb, wo_ref[...]).astype(BF16)
        _acc(acc, _dot_tn(a_ref[...], dyb), first)
        _acc(dgf_ref, dgf, first)
        _acc(dgp_ref, dgp, first)

        @pl.when(i == nt - 1)
        def _():
            dwo_ref[...] = acc[...].astype(BF16)

    return _launch(
        body, name="bwd_attn_out", grid=(nt,),
        in_specs=[_row_spec(D_MODEL)] * 5 + [_full_spec((D_MODEL, D_MODEL)), _vec_spec(), _vec_spec()],
        out_specs=[_row_spec(D_MODEL), _row_spec(D_MODEL), _full_spec((D_MODEL, D_MODEL)), _vec_spec(), _vec_spec()],
        out_shape=[jax.ShapeDtypeStruct((T, D_MODEL), F32), jax.ShapeDtypeStruct((T, D_MODEL), BF16),
                   jax.ShapeDtypeStruct((D_MODEL, D_MODEL), BF16)] + [jax.ShapeDtypeStruct((1, D_MODEL), F32)] * 2,
        scratch_shapes=[pltpu.VMEM((D_MODEL, D_MODEL), F32)],
        args=(dx2, dh2, x1, y, attn, wo, g_ffn, g_post), job=job)


def _bwd_attention(q, dattn, kpad, vpad, sinks, job=None):
    T = q.shape[0]
    nb = T // ATT_BLOCK

    def body(q_ref, do_ref, k_ref, v_ref, sink_ref, dq_ref, dkv_ref, ds_ref, dk_ref, dv_ref, s_scr, dp_scr, p_scr,
             dsb_scr, rel_scr, off_scr):
        n = pl.program_id(0)
        _att_mask(n, rel_scr, off_scr)

        @pl.when(n == 0)
        def _():
            dk_ref[...] = jnp.zeros_like(dk_ref)
            dv_ref[...] = jnp.zeros_like(dv_ref)
            ds_ref[...] = jnp.zeros_like(ds_ref)

        start = pl.multiple_of(n * ATT_BLOCK, ATT_BLOCK)
        win = pl.ds(start, 2 * ATT_BLOCK)
        kw = k_ref[win, :]
        vw = v_ref[win, :]
        lane = lax.broadcasted_iota(jnp.int32, (1, ATT_BLOCK), 1)
        dsink = jnp.zeros((1, ATT_BLOCK), F32)
        dqs, dks, dvs = [], [], []
        for kh in range(N_KV_HEADS):
            kk = kw[:, kh * HEAD_DIM:(kh + 1) * HEAD_DIM]
            vv = vw[:, kh * HEAD_DIM:(kh + 1) * HEAD_DIM]
            qs = _stack_heads(q_ref, kh)
            dos = _stack_heads(do_ref, kh)
            s_scr[...] = _dot_nt(qs, kk)
            dp_scr[...] = _dot_nt(dos, vv)
            for g in range(GQA_GROUP):
                h = kh * GQA_GROUP + g
                dsink_h = jnp.zeros((1, 1), F32)
                for row0 in range(0, ATT_BLOCK, ATT_SUB):
                    rows, sub = pl.ds(g * ATT_BLOCK + row0, ATT_SUB), pl.ds(row0, ATT_SUB)
                    pr, ps = _att_probs(s_scr[rows, :], rel_scr[sub, :], off_scr[sub, :], _alibi_slope(h),
                                        sink_ref[0, h])
                    dp = dp_scr[rows, :]
                    delta = jnp.sum(pr * dp, axis=-1, keepdims=True)
                    dsb_scr[rows, :] = (pr * (dp - delta) * ATT_SCALE).astype(BF16)
                    p_scr[rows, :] = pr.astype(BF16)
                    dsink_h = dsink_h - jnp.sum(ps * delta, axis=0, keepdims=True)
                dsink = dsink + jnp.where(lane == h, dsink_h, 0.0)
            dsb = dsb_scr[...]
            dqs += _unstack_heads(_dot(dsb, kk))
            dks.append(_dot_tn(dsb, qs))
            dvs.append(_dot_tn(p_scr[...], dos))
        dq_ref[...] = jnp.concatenate(dqs, axis=1).astype(BF16)
        dk_ref[win, :] += jnp.concatenate(dks, axis=1)
        dv_ref[win, :] += jnp.concatenate(dvs, axis=1)
        ds_ref[...] += dsink

        @pl.when(n == nb - 1)
        def _():
            dkv_ref[:, :KV_DIM] = dk_ref[ATT_BLOCK:, :].astype(BF16)
            dkv_ref[:, KV_DIM:] = dv_ref[ATT_BLOCK:, :].astype(BF16)

    return _launch(
        body, name="bwd_attention", grid=(nb,),
        in_specs=[_row_spec(D_MODEL, ATT_BLOCK), _row_spec(D_MODEL, ATT_BLOCK), _full_spec((T + ATT_BLOCK, KV_DIM)),
                  _full_spec((T + ATT_BLOCK, KV_DIM)), pl.BlockSpec(memory_space=pltpu.SMEM)],
        out_specs=[_row_spec(D_MODEL, ATT_BLOCK), _full_spec((T, 2 * KV_DIM)), _full_spec((1, ATT_BLOCK))],
        out_shape=[jax.ShapeDtypeStruct((T, D_MODEL), BF16), jax.ShapeDtypeStruct((T, 2 * KV_DIM), BF16),
                   jax.ShapeDtypeStruct((1, ATT_BLOCK), F32)],
        scratch_shapes=[pltpu.VMEM((T + ATT_BLOCK, KV_DIM), F32)] * 2
                       + [pltpu.VMEM((ATT_GROUP_ROWS, 2 * ATT_BLOCK), F32)] * 2
                       + [pltpu.VMEM((ATT_GROUP_ROWS, 2 * ATT_BLOCK), BF16)] * 2
                       + [pltpu.VMEM((ATT_BLOCK, 2 * ATT_BLOCK), F32)] * 2,
        args=(q, dattn, kpad, vpad, sinks), vmem=VMEM_BIG, job=job)


def _bwd_qkv(dxres, dq, dkv, x3, h1, hk, wq, wkv, g_mix, g_kv, job=None):
    T = x3.shape[0]
    nt = T // ROW_TILE

    def body(dxr_ref, dq_ref, dkv_ref, x_ref, h1_ref, hk_ref, wq_ref, wkv_ref, gmix_ref, gkv_ref,
             dx_ref, dwq_ref, dwkv_ref, dgm_ref, dgk_ref, acc_q, acc_kv):
        i = pl.program_id(0)
        first = i == 0
        dqv = dq_ref[...]
        dkvv = dkv_ref[...]
        xv = x_ref[...]
        d1, dgm = _rms_bwd(xv, gmix_ref[...], _dot_nt(dqv, wq_ref[...]))
        d2, dgk = _rms_bwd(xv, gkv_ref[...], _dot_nt(dkvv, wkv_ref[...]))
        dx_ref[...] = dxr_ref[...] + d1 + d2
        _acc(acc_q, _dot_tn(h1_ref[...], dqv), first)
        _acc(acc_kv, _dot_tn(hk_ref[...], dkvv), first)
        _acc(dgm_ref, dgm, first)
        _acc(dgk_ref, dgk, first)

        @pl.when(i == nt - 1)
        def _():
            dwq_ref[...] = acc_q[...].astype(BF16)
            dwkv_ref[...] = acc_kv[...].astype(BF16)

    return _launch(
        body, name="bwd_qkv", grid=(nt,),
        in_specs=[_row_spec(D_MODEL), _row_spec(D_MODEL), _row_spec(2 * KV_DIM), _row_spec(D_MODEL), _row_spec(D_MODEL),
                  _row_spec(D_MODEL), _full_spec((D_MODEL, D_MODEL)), _full_spec((D_MODEL, 2 * KV_DIM)), _vec_spec(),
                  _vec_spec()],
        out_specs=[_row_spec(D_MODEL), _full_spec((D_MODEL, D_MODEL)), _full_spec((D_MODEL, 2 * KV_DIM)), _vec_spec(),
                   _vec_spec()],
        out_shape=[jax.ShapeDtypeStruct((T, D_MODEL), F32), jax.ShapeDtypeStruct((D_MODEL, D_MODEL), BF16),
                   jax.ShapeDtypeStruct((D_MODEL, 2 * KV_DIM), BF16)] + [jax.ShapeDtypeStruct((1, D_MODEL), F32)] * 2,
        scratch_shapes=[pltpu.VMEM((D_MODEL, D_MODEL), F32), pltpu.VMEM((D_MODEL, 2 * KV_DIM), F32)],
        args=(dxres, dq, dkv, x3, h1, hk, wq, wkv, g_mix, g_kv), job=job)


def _bwd_pool_mixer(dx2, dh2, x1, x, yraw, d, wp, scale, g_ffn, g_post, g_pre, job=None):
    T = x.shape[0]
    tm = ROW_TILE
    nt = T // tm

    def body(dx2_ref, dh2_ref, x1_ref, x_ref, yraw_ref, d_ref, wp_ref, sc_ref, gffn_ref, gpost_ref, gpre_ref,
             dx_ref, dwp_ref, dsc_ref, dgf_ref, dgp_ref, dgm_ref, carry, acc):
        i = pl.program_id(0)
        first = i == 0
        tile = nt - 1 - i

        @pl.when(first)
        def _():
            carry[...] = jnp.zeros_like(carry)

        dxn, dgf = _rms_bwd(x1_ref[...], gffn_ref[...], dh2_ref[...])
        dx1 = dx2_ref[...] + dxn
        yraw = yraw_ref[...].astype(F32)
        sc = sc_ref[...]
        dy, dgp = _rms_bwd(yraw * sc, gpost_ref[...], dx1)
        dsc = jnp.sum(dy * yraw, axis=0, keepdims=True)
        dyb = (dy * sc).astype(BF16)
        dv = d_ref[...]
        dds = []
        for g in range(N_POOL_GROUPS):
            cols = slice(g * POOL_GROUP, (g + 1) * POOL_GROUP)
            dds.append(_dot_nt(dyb[:, cols], wp_ref[g]))
            _acc(acc.at[g], _dot_tn(dv[:, cols], dyb[:, cols]), first)
        dd = jnp.concatenate(dds, axis=1)
        e = dd / _pool_counts(tile * tm, tm)
        ext = jnp.concatenate([e, carry[...]], axis=0)
        carry[...] = e[:POOL_HALO, :]
        sums = _window_sums(ext, lambda k: tm + POOL_HALO - k)[:tm, :]
        dxm, dgm = _rms_bwd(x_ref[...], gpre_ref[...], sums - dd)
        dx_ref[...] = dx1 + dxm
        _acc(dsc_ref, dsc, first)
        _acc(dgf_ref, dgf, first)
        _acc(dgp_ref, dgp, first)
        _acc(dgm_ref, dgm, first)

        @pl.when(i == nt - 1)
        def _():
            dwp_ref[...] = acc[...].astype(BF16)

    rev = pl.BlockSpec((tm, D_MODEL), lambda i: (nt - 1 - i, 0))
    return _launch(
        body, name="bwd_pool_mixer", grid=(nt,),
        in_specs=[rev] * 6 + [_full_spec((N_POOL_GROUPS, POOL_GROUP, POOL_GROUP))] + [_vec_spec()] * 4,
        out_specs=[rev, _full_spec((N_POOL_GROUPS, POOL_GROUP, POOL_GROUP))] + [_vec_spec()] * 4,
        out_shape=[jax.ShapeDtypeStruct((T, D_MODEL), F32),
                   jax.ShapeDtypeStruct((N_POOL_GROUPS, POOL_GROUP, POOL_GROUP), BF16)]
                  + [jax.ShapeDtypeStruct((1, D_MODEL), F32)] * 4,
        scratch_shapes=[pltpu.VMEM((POOL_HALO, D_MODEL), F32), pltpu.VMEM((N_POOL_GROUPS, POOL_GROUP, POOL_GROUP), F32)],
        args=(dx2, dh2, x1, x, yraw, d, wp, scale, g_ffn, g_post, g_pre), job=job)


def _my_place():
    return lax.axis_index("x"), lax.axis_index("y"), lax.axis_index("c")


def _dev_index(px, py, pc):
    return 4 * px + 2 * py + pc


def _peer_by_relation(r):
    x, y, c = _my_place()
    return (x ^ ((r >> 2) & 1), y ^ ((r >> 1) & 1), c ^ (r & 1))


def _slot_pool(ref, j):
    return ref.at[:, pl.ds(pl.multiple_of(j * 32, 32), 32), :]


def _slot_scale(ref, j):
    return ref.at[:, pl.ds(pl.multiple_of(j * 128, 128), 128)]


def _slot_rows128(ref, j):
    return ref.at[pl.ds(pl.multiple_of(j * 128, 128), 128), :]


def _slot_gu(ref, j):
    return ref.at[j % FF_CHUNKS, j // FF_CHUNKS]


def _slot_wd(ref, j):
    return ref.at[pl.ds(pl.multiple_of(j * WD_ROWS, 16), WD_ROWS), :]


def _slot_cols128(ref, j):
    return ref.at[:, pl.ds(pl.multiple_of(j * 128, 128), 128)]


_GATHERED = {
    "pool": ((N_POOL_GROUPS, POOL_GROUP, POOL_GROUP), BF16, _slot_pool),
    "scale": ((1, D_MODEL), F32, _slot_scale),
    "kv": ((D_MODEL, 2 * KV_DIM), BF16, _slot_rows128),
    "q": ((D_MODEL, D_MODEL), BF16, _slot_rows128),
    "o": ((D_MODEL, D_MODEL), BF16, _slot_rows128),
    "gu": ((FF_CHUNKS, 2, FF_BLOCK, D_MODEL), BF16, _slot_gu),
    "wd": ((D_FF, D_MODEL), BF16, _slot_wd),
    "guh": ((FF_CHUNKS, 2, FF_BLOCK, D_MODEL // 2), BF16, _slot_gu),
    "wdh": ((D_FF, D_MODEL // 2), BF16, _slot_wd),
    "gate": ((D_MODEL, D_MODEL), BF16, _slot_rows128),
    "proj": ((PLE_DIM, D_MODEL), BF16, _slot_cols128),
}


def _no_compute():
    pass


class _AllGather:
    peers = ("sibling", "x", "y")

    def __init__(self, names, shards):
        self.kinds = [_GATHERED[n.rstrip("01_")] for n in names]
        entries = [shards[n] if isinstance(shards[n], tuple) else (shards[n], None, None) for n in names]
        self.args = [array for array, _, _ in entries]
        self.layers = [layer for _, layer, _ in entries]
        self.columns = [columns for _, _, columns in entries]
        self.out_shape = [jax.ShapeDtypeStruct(shape, dtype) for shape, dtype, _ in self.kinds]
        n = len(names)
        self.scratch = [pltpu.SemaphoreType.DMA((n, 7)), pltpu.SemaphoreType.DMA((n, 7)), pltpu.SemaphoreType.DMA((n,))]

    def _plan(self, srcs, outs, sems):
        send_sems, recv_sems, local_sems = sems
        x, y, c = _my_place()

        def slot(t, dev):
            return self.kinds[t][2](outs[t], _dev_index(*dev))

        def copy(t, k, block, to, src=None):
            return pltpu.make_async_remote_copy(
                src_ref=slot(t, block) if src is None else src, dst_ref=slot(t, block),
                send_sem=send_sems.at[t, k], recv_sem=recv_sems.at[t, k], device_id=to, device_id_type=MESH)

        return types.SimpleNamespace(
            copy=copy, core=c, me=(x, y, c), sibling=(x, y, 1 - c),
            x_chip=(1 - x, y), y_chip=(x, 1 - y), far_chip=(1 - x, 1 - y),
            via=(x ^ (1 - c), y ^ c),
            onto=(x ^ c, y ^ (1 - c)),
            k_via=1 + c, k_onto=2 - c,
            local=[pltpu.make_async_copy(self._shard(srcs, t), slot(t, (x, y, c)), local_sems.at[t])
                   for t in range(len(srcs))])

    def _shard(self, srcs, t):
        shard = srcs[t] if self.layers[t] is None else srcs[t].at[self.layers[t]]
        if self.columns[t] is None:
            return shard
        first, end = self.columns[t]
        return shard.at[:, first:end]

    def start(self, srcs, outs, sems):
        p = self._plan(srcs, outs, sems)
        for cp in p.local:
            cp.start()
        for t in range(len(srcs)):
            shard = self._shard(srcs, t)
            p.copy(t, 0, p.me, p.sibling, src=shard).start()
            p.copy(t, 1, p.me, (*p.x_chip, p.core), src=shard).start()
            p.copy(t, 2, p.me, (*p.y_chip, p.core), src=shard).start()

    def mid(self, srcs, outs, sems):
        p = self._plan(srcs, outs, sems)
        for t in range(len(srcs)):
            block = (*p.via, p.core)
            p.copy(t, p.k_via, block, p.me).wait_recv()
            p.copy(t, 3, block, (*p.onto, p.core)).start()
            p.copy(t, 3 + p.k_via, block, p.sibling).start()

    def late(self, srcs, outs, sems):
        p = self._plan(srcs, outs, sems)
        n = len(srcs)
        for t in range(n):
            block = (*p.onto, p.core)
            p.copy(t, p.k_onto, block, p.me).wait_recv()
            p.copy(t, 3 + p.k_onto, block, p.sibling).start()
        for t in range(n):
            block = (*p.far_chip, p.core)
            p.copy(t, 3, block, p.me).wait_recv()
            p.copy(t, 6, block, p.sibling).start()

    def finish(self, srcs, outs, sems):
        p = self._plan(srcs, outs, sems)
        n = len(srcs)
        other = 1 - p.core
        for t in range(n):
            p.copy(t, 0, (*p.me[:2], other), p.me).wait_recv()
            for k, chip in ((4, p.x_chip), (5, p.y_chip), (6, p.far_chip)):
                p.copy(t, k, (*chip, other), p.me).wait_recv()
            for k in range(7):
                p.copy(t, k, p.me, p.sibling).wait_send()
        for cp in p.local:
            cp.wait()


def _jobs_only(name, job=None):
    return _launch(_no_compute, name=name, grid=(), in_specs=[], out_specs=[], out_shape=[], args=(), job=job)


def _block_pool(ref, j):
    return ref.at[:, pl.ds(pl.multiple_of(j * 32, 32), 32), :]


def _block_rows128(ref, j):
    return ref.at[pl.ds(pl.multiple_of(j * 128, 128), 128), :]


def _block_gu(ref, j):
    return ref.at[j % FF_CHUNKS, j // FF_CHUNKS]


def _block_wd(ref, j):
    return ref.at[pl.ds(pl.multiple_of(j * WD_ROWS, 16), WD_ROWS), :]


def _block_cols128(ref, j):
    return ref.at[:, pl.ds(pl.multiple_of(j * 128, 128), 128)]


_SCATTERED = {
    "pool": ((N_POOL_GROUPS, 32, POOL_GROUP), _block_pool),
    "kv": ((128, 2 * KV_DIM), _block_rows128),
    "q": ((128, D_MODEL), _block_rows128),
    "o": ((128, D_MODEL), _block_rows128),
    "gu": ((FF_BLOCK, FF_PART), _block_gu),
    "wd": ((WD_ROWS, FF_PART), _block_wd),
    "guA": ((FF_BLOCK, FF_PART), lambda ref, j: _block_gu(ref, j).at[:, :FF_PART]),
    "guB": ((FF_BLOCK, FF_PART), lambda ref, j: _block_gu(ref, j).at[:, FF_PART:]),
    "wdA": ((WD_ROWS, FF_PART), lambda ref, j: _block_wd(ref, j).at[:, :FF_PART]),
    "wdB": ((WD_ROWS, FF_PART), lambda ref, j: _block_wd(ref, j).at[:, FF_PART:]),
    "gate": ((128, D_MODEL), _block_rows128),
    "proj": ((PLE_DIM, 128), _block_cols128),
}


class _SiblingSwap:
    peers = ("sibling",)

    def __init__(self, pieces):
        self.kinds = [_SCATTERED[kind] for kind, _ in pieces]
        self.args = [g for _, g in pieces]
        self.out_shape = [jax.ShapeDtypeStruct((N_CHIPS, *block), BF16) for block, _ in self.kinds]
        n = len(pieces)
        self.scratch = [pltpu.SemaphoreType.DMA((n, N_CHIPS)), pltpu.SemaphoreType.DMA((n, N_CHIPS))]

    def _copies(self, srcs, outs, sems):
        send_sems, recv_sems = sems
        x, y, c = _my_place()
        return [pltpu.make_async_remote_copy(
            src_ref=block(srcs[t], 2 * ch + 1 - c), dst_ref=outs[t].at[ch], send_sem=send_sems.at[t, ch],
            recv_sem=recv_sems.at[t, ch], device_id=(x, y, 1 - c), device_id_type=MESH)
            for t, (_, block) in enumerate(self.kinds) for ch in range(N_CHIPS)]

    def start(self, srcs, outs, sems):
        for cp in self._copies(srcs, outs, sems):
            cp.start()

    def finish(self, srcs, outs, sems):
        for cp in self._copies(srcs, outs, sems):
            cp.wait()


class _ChipScatter:
    N_BUFS = 4
    peers = ("x", "y")

    def __init__(self, pieces):
        self.kinds = [_SCATTERED[kind] for kind, _, _ in pieces]
        self.n = n = len(pieces)
        self.args = [g for _, g, _ in pieces] + [s for _, _, s in pieces]
        self.out_shape = [jax.ShapeDtypeStruct((2, *block), BF16) for block, _ in self.kinds]
        self.scratch = []
        for block, _ in self.kinds:
            self.scratch += [pltpu.VMEM((N_CHIPS, *block), BF16)] * 3 + [pltpu.VMEM((2, *block), BF16)]
        dma = pltpu.SemaphoreType.DMA
        self.scratch += [dma((n, N_CHIPS + 1)), dma((n, 2)), dma((n, 2)), dma((n,)), dma((n,)), dma((n,))]

    def _plan(self, outs, scr):
        n = self.n
        first_send, first_recv, second_send, second_recv, keep_sems = scr[self.N_BUFS * n + 1:]
        x, y, c = _my_place()
        via = (x ^ (1 - c), y ^ c)
        onto = (x ^ c, y ^ (1 - c))
        index = lambda chip: 2 * chip[0] + chip[1]
        first, second, keep = [], [], []
        for t in range(n):
            total, inbox = scr[self.N_BUFS * t + 2], scr[self.N_BUFS * t + 3]
            for k, chip in enumerate((via, (1 - x, 1 - y))):
                first.append(pltpu.make_async_remote_copy(
                    src_ref=total.at[index(chip)], dst_ref=inbox.at[k], send_sem=first_send.at[t, k],
                    recv_sem=first_recv.at[t, k], device_id=(*via, c), device_id_type=MESH))
            second.append(pltpu.make_async_remote_copy(
                src_ref=total.at[index(onto)], dst_ref=outs[t].at[1], send_sem=second_send.at[t],
                recv_sem=second_recv.at[t], device_id=(*onto, c), device_id_type=MESH))
            keep.append(pltpu.make_async_copy(total.at[index((x, y))], outs[t].at[0], keep_sems.at[t]))
        return first, second, keep, index((x, y)), index(onto)

    def start(self, ins, outs, scr):
        n = self.n
        load_sems = scr[self.N_BUFS * n]
        c = lax.axis_index("c")
        loads = []
        for t, (_, block) in enumerate(self.kinds):
            mine, theirs = scr[self.N_BUFS * t], scr[self.N_BUFS * t + 1]
            loads += [pltpu.make_async_copy(block(ins[t], 2 * ch + c), mine.at[ch], load_sems.at[t, ch])
                      for ch in range(N_CHIPS)]
            loads.append(pltpu.make_async_copy(ins[n + t], theirs, load_sems.at[t, N_CHIPS]))
        for cp in loads:
            cp.start()
        for cp in loads:
            cp.wait()
        for t in range(n):
            mine, theirs, total = scr[self.N_BUFS * t:self.N_BUFS * t + 3]
            for ch in range(N_CHIPS):
                total[ch] = (mine[ch].astype(F32) + theirs[ch].astype(F32)).astype(BF16)
        for cp in self._plan(outs, scr)[0]:
            cp.start()

    def mid(self, ins, outs, scr):
        first, second, keep, me, onto = self._plan(outs, scr)
        for cp in first:
            cp.wait_recv()
        for t in range(self.n):
            total, inbox = scr[self.N_BUFS * t + 2], scr[self.N_BUFS * t + 3]
            for k, slot in enumerate((me, onto)):
                total[slot] = (total[slot].astype(F32) + inbox[k].astype(F32)).astype(BF16)
        for cp in second + keep:
            cp.start()

    def finish(self, ins, outs, scr):
        first, second, keep, _, _ = self._plan(outs, scr)
        for cp in first:
            cp.wait_send()
        for cp in second + keep:
            cp.wait()


class _ToEveryone:
    peers = _EVERYONE

    def __init__(self, scattered=(), gathered=()):
        self.blocks = [_SCATTERED[kind][1] for kind, _ in scattered] + [None] * len(gathered)
        self.args = [g for _, g in scattered] + list(gathered)
        self.out_shape = [jax.ShapeDtypeStruct((N_DEV, *_SCATTERED[kind][0]), BF16) for kind, _ in scattered]
        self.out_shape += [jax.ShapeDtypeStruct((N_DEV, *a.shape), a.dtype) for a in gathered]
        n = len(self.args)
        self.scratch = [pltpu.SemaphoreType.DMA((n, N_DEV - 1)), pltpu.SemaphoreType.DMA((n, N_DEV - 1)),
                        pltpu.SemaphoreType.DMA((n,))]

    def _copies(self, srcs, outs, sems):
        send_sems, recv_sems, local_sems = sems
        me = _dev_index(*_my_place())
        copies = []
        for t, block in enumerate(self.blocks):
            part = (lambda j, t=t, block=block: srcs[t] if block is None else block(srcs[t], j))
            copies.append(pltpu.make_async_copy(part(me), outs[t].at[me], local_sems.at[t]))
            for r in range(1, N_DEV):
                peer = _peer_by_relation(r)
                copies.append(pltpu.make_async_remote_copy(
                    src_ref=part(_dev_index(*peer)), dst_ref=outs[t].at[me], send_sem=send_sems.at[t, r - 1],
                    recv_sem=recv_sems.at[t, r - 1], device_id=peer, device_id_type=MESH))
        return copies

    def start(self, srcs, outs, sems):
        for cp in self._copies(srcs, outs, sems):
            cp.start()

    def finish(self, srcs, outs, sems):
        for cp in self._copies(srcs, outs, sems):
            cp.wait()


class _Jobs:
    def __init__(self, *jobs):
        self.jobs = jobs
        together = {p for j in jobs for p in j.peers}
        self.peers = tuple(p for p in _EVERYONE if p in together)
        self.args = [a for j in jobs for a in j.args]
        self.out_shape = [o for j in jobs for o in j.out_shape]
        self.scratch = [s for j in jobs for s in j.scratch]

    def _split(self, refs, attr):
        at = 0
        for j in self.jobs:
            n = len(getattr(j, attr))
            yield refs[at:at + n]
            at += n

    def _each(self, ins, outs, scr):
        return zip(self.jobs, self._split(ins, "args"), self._split(outs, "out_shape"), self._split(scr, "scratch"))

    def start(self, ins, outs, scr):
        for j, i, o, s in self._each(ins, outs, scr):
            j.start(i, o, s)

    def mid(self, ins, outs, scr):
        for j, i, o, s in self._each(ins, outs, scr):
            if hasattr(j, "mid"):
                j.mid(i, o, s)

    def late(self, ins, outs, scr):
        for j, i, o, s in self._each(ins, outs, scr):
            if hasattr(j, "late"):
                j.late(i, o, s)

    def finish(self, ins, outs, scr):
        for j, i, o, s in self._each(ins, outs, scr):
            j.finish(i, o, s)

    def split_outputs(self, outs):
        return list(self._split(outs, "out_shape"))


def _adamw_math(w, g, m, v):
    m = ADAM_B1 * m + (1.0 - ADAM_B1) * g
    v = ADAM_B2 * v + (1.0 - ADAM_B2) * (g * g)
    m_hat = m / (1.0 - ADAM_B1 ** ADAM_STEP)
    v_hat = v / (1.0 - ADAM_B2 ** ADAM_STEP)
    delta = -ADAM_LR * (m_hat / (jnp.sqrt(v_hat) + ADAM_EPS) + ADAM_WD * w)
    return delta, m, v


def _adamw(name, w, m, v, landings, n_col_blocks=1, job=None):
    n_slots, r, c = landings[0].shape
    grid = (w.shape[0] // r, n_col_blocks)

    def body(w_ref, m_ref, v_ref, *rest):
        l_refs, (g_ref, d_ref, nm_ref, nv_ref) = rest[:len(landings)], rest[len(landings):]
        step = pl.program_id(0) * n_col_blocks + pl.program_id(1)
        for idx, l_ref in enumerate(l_refs):
            @pl.when(step == idx)
            def _(l_ref=l_ref):
                g = l_ref[0].astype(F32)
                for s in range(1, n_slots):
                    g = g + l_ref[s].astype(F32)
                g_ref[...] = g
                d_ref[...], nm_ref[...], nv_ref[...] = _adamw_math(w_ref[...], g, m_ref[...], v_ref[...])

    spec = pl.BlockSpec((r, c), lambda a, b: (a, b))
    return _launch(
        body, name=f"adamw_{name}", grid=grid,
        in_specs=[spec, spec, spec] + [_full_spec((n_slots, r, c))] * len(landings),
        out_specs=[spec] * 4, out_shape=[jax.ShapeDtypeStruct(w.shape, F32)] * 4,
        args=(w, m, v, *landings), vmem=VMEM_BIG, job=job)


_SMALL = (("pre_mix_g", SV_PRE_MIX, 2), ("post_mix_g", SV_POST_MIX, 2), ("pre_ffn_g", SV_PRE_FFN, 2),
          ("post_ffn_g", SV_POST_FFN, 2), ("ple_g", SV_PLE, 2), ("ple_post_g", SV_PLE_POST, 2), ("kv_g", SV_KV, 1),
          ("pool_scale", SV_POOL_SCALE, 1), ("sinks", SV_SINKS, 1))


def _adamw_several(items):
    counts = [len(landings) for _, _, _, landings in items]
    args = [a for w, m, v, landings in items for a in (w, m, v, *landings)]
    out_shape = [jax.ShapeDtypeStruct(w.shape, F32) for w, _, _, _ in items for _ in range(4)]

    def body(*refs):
        ins, outs = refs[:len(args)], refs[len(args):]
        at = 0
        for idx, n_landings in enumerate(counts):
            w_ref, m_ref, v_ref = ins[at:at + 3]
            l_refs = ins[at + 3:at + 3 + n_landings]
            at += 3 + n_landings
            g_ref, d_ref, nm_ref, nv_ref = outs[4 * idx:4 * idx + 4]
            for part, l_ref in enumerate(l_refs):
                rows = slice(part * l_ref.shape[1], (part + 1) * l_ref.shape[1])
                g = l_ref[0].astype(F32)
                for s in range(1, l_ref.shape[0]):
                    g = g + l_ref[s].astype(F32)
                g_ref[rows, :] = g
                d_ref[rows, :], nm_ref[rows, :], nv_ref[rows, :] = _adamw_math(
                    w_ref[rows, :], g, m_ref[rows, :], v_ref[rows, :])

    res, _ = _launch(
        body, name="adamw_several", grid=(1,), in_specs=[_full_spec(a.shape) for a in args],
        out_specs=[_full_spec(s.shape) for s in out_shape], out_shape=out_shape, args=args)
    return [res[4 * idx:4 * idx + 4] for idx in range(len(items))]


def _small_adamw(slabs, params):
    flat = [a for name, _, _ in _SMALL for a in params[name]]
    n_in = 1 + len(flat)

    def body(*refs):
        slabs_ref, wmv = refs[0], refs[1:n_in]
        loss_ref, outs, total = refs[n_in], refs[n_in + 1:-1], refs[-1]
        me = _dev_index(*_my_place())
        g = slabs_ref[0]
        for s in range(1, N_DEV):
            g = g + slabs_ref[s]
        total[...] = g
        loss_ref[...] = total[SV_LOSS:SV_LOSS + 1, 0:1]
        for idx, (name, row, n_rows) in enumerate(_SMALL):
            w_ref, m_ref, v_ref = wmv[3 * idx:3 * idx + 3]
            g_ref, d_ref, nm_ref, nv_ref = outs[4 * idx:4 * idx + 4]
            if name == "pool_scale":
                g = total[row:row + 1, pl.ds(pl.multiple_of(me * 128, 128), 128)]
            else:
                g = total[row:row + n_rows, 0:w_ref.shape[1]]
            g_ref[...] = g
            d_ref[...], nm_ref[...], nv_ref[...] = _adamw_math(w_ref[...], g, m_ref[...], v_ref[...])

    out_shape = [jax.ShapeDtypeStruct((1, 1), F32)]
    for name, _, _ in _SMALL:
        out_shape += [jax.ShapeDtypeStruct(params[name][0].shape, F32)] * 4
    res, _ = _launch(
        body, name="small_adamw", grid=(1,),
        in_specs=[_full_spec(a.shape) for a in (slabs, *flat)], out_specs=[_full_spec(s.shape) for s in out_shape],
        out_shape=out_shape, scratch_shapes=[pltpu.VMEM((SV_ROWS, D_MODEL), F32)], args=(slabs, *flat))
    return res[0], {name: res[1 + 4 * idx:5 + 4 * idx] for idx, (name, _, _) in enumerate(_SMALL)}


def _local_step(x, p, tgt, gains, sinks, shards, weights):
    row = lambda first_row, layer: _Gain(gains, first_row + layer)
    gather = lambda *names: _AllGather(names, shards)
    g_pre_mix, g_post_mix, g_pre_ffn, g_post_ffn = SV_PRE_MIX, SV_POST_MIX, SV_PRE_FFN, SV_POST_FFN
    g_ple, g_ple_post, g_kv = SV_PLE, SV_PLE_POST, _Gain(gains, SV_KV)

    (dpool,), (wp, scale, wgu0) = _fwd_pool(x, row(g_pre_mix, 0), job=gather("pool", "scale", "gu0"))
    (x1_0, h2_0, yraw), (wd0,) = _fwd_pool_mixer(x, dpool, wp, scale, row(g_post_mix, 0), row(g_pre_ffn, 0),
                                                 job=gather("wd0"))
    wgu0, wd0 = [wgu0], [wd0]
    (gs0, us0, f0, x2_0, h3_0), (wgate0, wproj0, wkv, wq, wo, wd1_a) = _fwd_ffn(
        0, h2_0, x1_0, wgu0, wd0, row(g_post_ffn, 0), row(g_ple, 0),
        job=gather("gate0", "proj0", "kv", "q", "o", "wdh1_0"))
    (x3_0, z0, pe0, hk, h1, q, kpad, vpad), (wgu1_a,) = _fwd_ple_qkv(
        x2_0, h3_0, p[0], wgate0, wproj0, row(g_ple_post, 0), g_kv, row(g_pre_mix, 1), wkv, wq,
        job=gather("guh1_0"))
    (attn,), (wgu1_b,) = _fwd_attention(q, kpad, vpad, sinks, job=gather("guh1_1"))
    (y1, x1_1, h2_1), (wd1_b,) = _fwd_attn_out(attn, x3_0, wo, row(g_post_mix, 1), row(g_pre_ffn, 1),
                                               job=gather("wdh1_1"))
    wgu1, wd1 = [wgu1_a, wgu1_b], [wd1_a, wd1_b]
    (gs1, us1, f1, x2_1, h3_1), (wgate1, wproj1) = _fwd_ffn(
        1, h2_1, x1_1, wgu1, wd1, row(g_post_ffn, 1), row(g_ple, 1), job=gather("gate1", "proj1"))

    produced, swapped, landed = {}, {}, {}

    def kind_of(name):
        return name.rstrip("0123_")

    def hosted(call, *args, swap=(), spread=(), extra=None):
        jobs = []
        if swap:
            jobs.append(_SiblingSwap([(kind_of(n), produced[n]) for n in swap]))
        if spread:
            jobs.append(_ChipScatter([(kind_of(n), produced[n], swapped[n]) for n in spread]))
        if extra is not None:
            jobs.append(extra)
        jobs = _Jobs(*jobs)
        outs, job_outs = call(*args, job=jobs)
        parts = jobs.split_outputs(job_outs)
        if swap:
            swapped.update(zip(swap, parts.pop(0)))
        if spread:
            landed.update(zip(spread, parts.pop(0)))
        return outs if extra is None else (outs, parts.pop(0))

    ffn_q = lambda layer, qtr: (f"gu{layer}_{qtr}", f"wd{layer}_{qtr}")

    dx2_1, df1, produced["gate1"], produced["proj1"], dg_ple_post1, dg_ple1, dg_post_ffn1, loss = hosted(
        _ple_loss_bwd, 1, x2_1, h3_1, p[1], f1, tgt, wgate1, wproj1, row(g_ple_post, 1), row(g_ple, 1),
        row(g_post_ffn, 1))
    dh2_1, dg1, du1, a1 = hosted(_bwd_ffn_act, 1, df1, gs1, us1, wgu1, wd1, swap=("gate1", "proj1"))
    dgu1, dwd1 = hosted(_bwd_ffn_dw, 1, 0, 1, h2_1, df1, dg1, du1, a1, spread=("gate1", "proj1"))
    produced.update(guA1=dgu1, guB1=dgu1, wdA1=dwd1, wdB1=dwd1)
    dx1_1, dattn, produced["o"], dg_pre_ffn1, dg_post_mix1 = hosted(
        _bwd_attn_out, dx2_1, dh2_1, x1_1, y1, attn, wo, row(g_pre_ffn, 1), row(g_post_mix, 1),
        swap=("guA1", "wdA1", "guB1", "wdB1"))
    dq, dkv, dsinks = hosted(_bwd_attention, q, dattn, kpad, vpad, sinks, spread=("guA1", "wdA1"))
    dx3_0, produced["q"], produced["kv"], dg_pre_mix1, dg_kv = hosted(
        _bwd_qkv, dx1_1, dq, dkv, x3_0, h1, hk, wq, wkv, row(g_pre_mix, 1), g_kv, swap=("o",), spread=("wdB1",))
    dx2_0, df0, produced["gate0"], produced["proj0"], dg_ple_post0, dg_ple0, dg_post_ffn0 = hosted(
        _bwd_ple, 0, dx3_0, x2_0, z0, pe0, h3_0, p[0], f0, wgate0, row(g_ple_post, 0), row(g_ple, 0),
        row(g_post_ffn, 0), swap=("q", "kv"), spread=("guB1",))
    for half, letter in enumerate("AB"):
        landed[f"gu1_{half}"], landed[f"wd1_{half}"] = landed[f"gu{letter}1"], landed[f"wd{letter}1"]
    dh2_0, dg0, du0, a0 = hosted(_bwd_ffn_act, 0, df0, gs0, us0, wgu0, wd0,
                                 swap=("gate0", "proj0"), spread=("o", "q", "kv"))
    part_hosts = [dict(spread=("gate0", "proj0")), dict(swap=ffn_q(0, 0))]
    for part in range(FF_PARTS):
        produced[f"gu0_{part}"], produced[f"wd0_{part}"] = hosted(
            _bwd_ffn_dw, 0, part, FF_PARTS, h2_0, df0, dg0, du0, a0, **part_hosts[part])
    grad_x, produced["pool"], dscale, dg_pre_ffn0, dg_post_mix0, dg_pre_mix0 = hosted(
        _bwd_pool_mixer, dx2_0, dh2_0, x1_0, x, yraw, dpool, wp, scale, row(g_pre_ffn, 0), row(g_post_mix, 0),
        row(g_pre_mix, 0), swap=ffn_q(0, 1), spread=ffn_q(0, 0))

    def update(name, n_col_blocks, pieces):
        w, m, v = weights[name]
        rows = w.size // w.shape[-1]
        flat = [landed[n].reshape(landed[n].shape[0], -1, landed[n].shape[-1]) for n in pieces]
        outs, _ = _adamw(name, w.reshape(rows, -1), m.reshape(rows, -1), v.reshape(rows, -1), flat, n_col_blocks)
        return [o.reshape(w.shape) for o in outs]

    upd = {}
    lanes = lambda a: jnp.pad(a, ((0, 0), (0, D_MODEL - a.shape[1])))
    small = jnp.concatenate([
        dg_pre_mix0, dg_pre_mix1, dg_post_mix0, dg_post_mix1, dg_pre_ffn0, dg_pre_ffn1, dg_post_ffn0, dg_post_ffn1,
        dg_ple0, dg_ple1, dg_ple_post0, dg_ple_post1, dg_kv, dscale, lanes(dsinks[:, :N_HEADS]), lanes(loss)], axis=0)

    everyone = _ToEveryone(scattered=[("pool", produced["pool"])], gathered=[small])
    _, (landed["pool"], slabs) = hosted(_jobs_only, "scatter_tail", spread=ffn_q(0, 1), extra=everyone)
    several = {"w_ple_gate": ("gate0", "gate1"), "w_ple_proj": ("proj0", "proj1"), "w_q": ("q",), "w_kv": ("kv",),
               "w_o": ("o",), "pool_w": ("pool",)}
    flat2d = lambda a: a.reshape(-1, a.shape[-1])
    results = _adamw_several([
        (*map(flat2d, weights[name]),
         [landed[n].reshape(landed[n].shape[0], -1, landed[n].shape[-1]) for n in pieces])
        for name, pieces in several.items()])
    for name, outs in zip(several, results):
        upd[name] = [o.reshape(weights[name][0].shape) for o in outs]
    upd["w_gu"] = update("w_gu", FF_PARTS,
                         pieces=[f"gu{layer}_{qtr}" for layer in range(2) for qtr in range(FF_PARTS)])
    upd["w_gu"] = [jnp.swapaxes(a, 1, 2) for a in upd["w_gu"]]
    upd["w_down"] = update("w_down", FF_PARTS,
                           pieces=[f"wd{layer}_{qtr}" for layer in range(2) for qtr in range(FF_PARTS)])
    return grad_x, upd, slabs


def kernel(x, p, pre_mix_g, post_mix_g, pre_ffn_g, post_ffn_g, pool_w, pool_scale, kv_g, w_kv, w_q, sinks, w_o, w_gu, w_down, ple_g, w_ple_gate, w_ple_proj, ple_post_g, loss_target, m_pre_mix_g, m_post_mix_g, m_pre_ffn_g, m_post_ffn_g, m_pool_w, m_pool_scale, m_kv_g, m_w_kv, m_w_q, m_sinks, m_w_o, m_w_gu, m_w_down, m_ple_g, m_w_ple_gate, m_w_ple_proj, m_ple_post_g, v_pre_mix_g, v_post_mix_g, v_pre_ffn_g, v_post_ffn_g, v_pool_w, v_pool_scale, v_kv_g, v_w_kv, v_w_q, v_sinks, v_w_o, v_w_gu, v_w_down, v_ple_g, v_w_ple_gate, v_w_ple_proj, v_ple_post_g):
    shards = {"pool": pool_w[0].astype(BF16), "scale": pool_scale, "kv": w_kv.astype(BF16),
              "q": w_q[0].astype(BF16), "o": w_o[0].astype(BF16)}
    gu, wd = jnp.swapaxes(w_gu, 1, 2).astype(BF16), w_down.astype(BF16)
    gate, proj = w_ple_gate.astype(BF16), w_ple_proj.astype(BF16)
    for layer in range(2):
        shards[f"gu{layer}"] = (gu, layer, None)
        shards[f"wd{layer}"] = (wd, layer, None)
        for half in range(2):
            cols = (half * D_MODEL // 2, (half + 1) * D_MODEL // 2)
            shards[f"guh{layer}_{half}"] = (gu, layer, cols)
            shards[f"wdh{layer}_{half}"] = (wd, layer, cols)
        shards[f"gate{layer}"] = (gate, layer, None)
        shards[f"proj{layer}"] = (proj, layer, None)
    gains = jnp.concatenate([pre_mix_g, post_mix_g, pre_ffn_g, post_ffn_g, ple_g, ple_post_g, kv_g[None, :]],
                            axis=0).reshape(-1, 1, D_MODEL)
    weights = {"pool_w": (pool_w, m_pool_w, v_pool_w), "w_kv": (w_kv, m_w_kv, v_w_kv), "w_q": (w_q, m_w_q, v_w_q),
               "w_o": (w_o, m_w_o, v_w_o), "w_down": (w_down, m_w_down, v_w_down),
               "w_gu": tuple(jnp.swapaxes(a, 1, 2) for a in (w_gu, m_w_gu, v_w_gu)),
               "w_ple_gate": (w_ple_gate, m_w_ple_gate, v_w_ple_gate),
               "w_ple_proj": (w_ple_proj, m_w_ple_proj, v_w_ple_proj)}
    per_layer = p.reshape(p.shape[0], *p.shape[2:])
    p_rows = [_LayerRows(per_layer, layer) for layer in range(2)]
    grad_x, upd, slabs = _local_step(x[0], p_rows, loss_target[0], gains, sinks, shards, weights)

    small_params = {
        "pre_mix_g": (pre_mix_g, m_pre_mix_g, v_pre_mix_g), "post_mix_g": (post_mix_g, m_post_mix_g, v_post_mix_g),
        "pre_ffn_g": (pre_ffn_g, m_pre_ffn_g, v_pre_ffn_g), "post_ffn_g": (post_ffn_g, m_post_ffn_g, v_post_ffn_g),
        "ple_g": (ple_g, m_ple_g, v_ple_g), "ple_post_g": (ple_post_g, m_ple_post_g, v_ple_post_g),
        "kv_g": (kv_g[None, :], m_kv_g[None, :], v_kv_g[None, :]),
        "pool_scale": (pool_scale, m_pool_scale, v_pool_scale), "sinks": (sinks, m_sinks, v_sinks)}
    loss, small_upd = _small_adamw(slabs, small_params)
    small_upd["kv_g"] = [a[0] for a in small_upd["kv_g"]]
    upd.update(small_upd)

    names = ["pre_mix_g", "post_mix_g", "pre_ffn_g", "post_ffn_g", "pool_w", "pool_scale", "kv_g", "w_kv", "w_q",
             "sinks", "w_o", "w_gu", "w_down", "ple_g", "w_ple_gate", "w_ple_proj", "ple_post_g"]
    outs = [loss[0, 0], grad_x[None]]
    for kind in range(4):
        outs += [upd[n][kind] for n in names]
    return tuple(outs)
```

```python
import functools
import types

import jax
import jax.numpy as jnp
from jax import lax
from jax.experimental import pallas as pl
from jax.experimental.pallas import tpu as pltpu

F32 = jnp.float32
BF16 = jnp.bfloat16

N_DEV = 8
D_MODEL = 1024
N_POOL_GROUPS = 4
POOL_GROUP = 256
POOL_HALO = 16
HEAD_DIM = 64
N_HEADS = 16
N_KV_HEADS = 4
GQA_GROUP = 4
KV_DIM = N_KV_HEADS * HEAD_DIM
ATT_BLOCK = 128
D_FF = 2816
FF_CHUNKS = 4
FF_BLOCK = D_FF // FF_CHUNKS
WD_ROWS = D_FF // N_DEV
FF_PARTS = 2
FF_PART = D_MODEL // FF_PARTS
N_CHIPS = 4
PLE_DIM = 256
EPS = 1e-6
NEG_INF = -1e30
ATT_SCALE = HEAD_DIM ** -0.5

ADAM_LR = 0.001
ADAM_B1 = 0.9
ADAM_B2 = 0.999
ADAM_EPS = 1e-08
ADAM_WD = 0.01
ADAM_STEP = 10

ROW_TILE = 512
FFN_ROW_TILE = 512
FFN_WEIGHT_COLS = 512
FFN_SUB_TILES = 1
VMEM_BIG = 60 * 1024 * 1024
VMEM_MID = 56 * 1024 * 1024
HBM_PIN_ELEMS = 1024

SV_ROWS = 16
SV_PRE_MIX, SV_POST_MIX, SV_PRE_FFN, SV_POST_FFN, SV_PLE, SV_PLE_POST = 0, 2, 4, 6, 8, 10
SV_KV, SV_POOL_SCALE, SV_SINKS, SV_LOSS = 12, 13, 14, 15

MESH = pl.DeviceIdType.MESH
ANY = pl.BlockSpec(memory_space=pl.ANY)


def _dot(a, b):
    return jnp.dot(a, b, preferred_element_type=F32)


def _dot_nt(a, b):
    return lax.dot_general(a, b, (((1,), (1,)), ((), ())), preferred_element_type=F32)


def _dot_tn(a, b):
    return lax.dot_general(a, b, (((0,), (0,)), ((), ())), preferred_element_type=F32)


def _rstd(x):
    return lax.rsqrt(jnp.mean(x * x, axis=-1, keepdims=True) + EPS)


def _rms(x, g):
    return x * _rstd(x) * g


def _rms_bwd(x, g, dy):
    r = _rstd(x)
    n = x * r
    dn = dy * g
    dx = r * (dn - n * jnp.mean(dn * n, axis=-1, keepdims=True))
    dg = jnp.sum(dy * n, axis=0, keepdims=True)
    return dx, dg


def _add_all(terms):
    return functools.reduce(jnp.add, terms)


def _sigmoid(x):
    return 1.0 / (1.0 + jnp.exp(-x))


def _acc(ref, val, first):
    @pl.when(first)
    def _():
        ref[...] = val

    @pl.when(jnp.logical_not(first))
    def _():
        ref[...] += val


def _pool_counts(row0, rows):
    t = row0 + lax.broadcasted_iota(jnp.int32, (rows, D_MODEL), 0) + 1
    grp = lax.broadcasted_iota(jnp.int32, (rows, D_MODEL), 1) // POOL_GROUP
    win = jnp.left_shift(2, grp)
    return jnp.minimum(t, win).astype(F32)


def _window_sums(ext, shift_of):
    outs = []
    s = ext
    for gi in range(N_POOL_GROUPS):
        s = s + pltpu.roll(s, shift_of(1 << gi), axis=0)
        outs.append(s[:, :POOL_GROUP])
        s = s[:, POOL_GROUP:]
    return jnp.concatenate(outs, axis=1)


def _cparams(n_axes, vmem, collective_id=None):
    return pltpu.CompilerParams(dimension_semantics=("arbitrary",) * n_axes, vmem_limit_bytes=vmem,
                                collective_id=collective_id)


_EVERYONE = ("sibling", "x", "y", "far", "x sibling", "y sibling", "far sibling")
_PEER_SETS = (("sibling", "x", "y"), ("sibling",), ("x", "y"), _EVERYONE)


def _meet(peers):
    x, y, c = lax.axis_index("x"), lax.axis_index("y"), lax.axis_index("c")
    device = {"sibling": (x, y, 1 - c), "x": (1 - x, y, c), "y": (x, 1 - y, c), "far": (1 - x, 1 - y, c),
              "x sibling": (1 - x, y, 1 - c), "y sibling": (x, 1 - y, 1 - c), "far sibling": (1 - x, 1 - y, 1 - c)}
    barrier = pltpu.get_barrier_semaphore()
    for peer in peers:
        pl.semaphore_signal(barrier, inc=1, device_id=device[peer], device_id_type=pl.DeviceIdType.MESH)
    pl.semaphore_wait(barrier, len(peers))


def _row_spec(cols, tm=ROW_TILE):
    return pl.BlockSpec((tm, cols), lambda i: (i, 0))


def _full_spec(shape):
    zeros = (0,) * len(shape)
    return pl.BlockSpec(shape, lambda *_: zeros)


def _vec_spec():
    return _full_spec((1, D_MODEL))


def _column_views(parts):
    return [(a, b) for a in parts for b in range(a.shape[-1] // FFN_WEIGHT_COLS)]


def _column_ranges(views):
    return [(n * FFN_WEIGHT_COLS, (n + 1) * FFN_WEIGHT_COLS) for n in range(len(views))]


class _Gain:
    def __init__(self, stacked, layer):
        self.stacked, self.layer = stacked, layer

    def spec(self):
        layer = self.layer
        return pl.BlockSpec((None, 1, D_MODEL), lambda *_: (layer, 0, 0))


class _LayerRows:
    def __init__(self, stacked, layer):
        self.stacked, self.layer = stacked, layer

    def spec(self):
        layer = self.layer
        return pl.BlockSpec((None, ROW_TILE, self.stacked.shape[-1]), lambda i: (layer, i, 0))


def _in_hbm(a):
    return pltpu.with_memory_space_constraint(a, pltpu.HBM) if a.size >= HBM_PIN_ELEMS else a


def _out_in_hbm(s):
    return pltpu.HBM(s.shape, s.dtype) if s.size >= HBM_PIN_ELEMS else s


def _launch(body, *, name, grid, in_specs, out_specs, out_shape, args, scratch_shapes=(), vmem=VMEM_MID, job=None):
    picked = (_Gain, _LayerRows)
    in_specs = [a.spec() if isinstance(a, picked) else s for s, a in zip(in_specs, args)]
    args = [_in_hbm(a.stacked if isinstance(a, picked) else a) for a in args]
    n_in, n_out, n_scr = len(args), len(out_shape), len(scratch_shapes)
    if job is not None and not job.args:
        job = None
    j_args, j_out, j_scr = ([], [], []) if job is None else ([_in_hbm(a) for a in job.args], job.out_shape, job.scratch)

    def run(*refs):
        groups, at = [], 0
        for n in (n_in, len(j_args), n_out, len(j_out), n_scr, len(j_scr)):
            groups.append(refs[at:at + n])
            at += n
        ins, j_ins, outs, j_outs, scr, j_sems = groups

        def begin():
            _meet(job.peers)
            job.start(j_ins, j_outs, j_sems)

        if job is None:
            body(*ins, *outs, *scr)
        elif not grid:
            begin()
            job.mid(j_ins, j_outs, j_sems)
            job.late(j_ins, j_outs, j_sems)
            body(*ins, *outs, *scr)
            job.finish(j_ins, j_outs, j_sems)
        else:
            ids = [pl.program_id(a) for a in range(len(grid))]
            at_start = lambda step: functools.reduce(jnp.logical_and, [ids[0] == step] + [i == 0 for i in ids[1:]])
            last = functools.reduce(jnp.logical_and, [i == g - 1 for i, g in zip(ids, grid)])
            pl.when(at_start(0))(begin)
            pl.when(at_start(grid[0] // 2))(lambda: job.mid(j_ins, j_outs, j_sems))
            pl.when(at_start(3 * grid[0] // 4))(lambda: job.late(j_ins, j_outs, j_sems))
            body(*ins, *outs, *scr)
            pl.when(last)(lambda: job.finish(j_ins, j_outs, j_sems))

    res = pl.pallas_call(
        run, name=name, grid=grid,
        in_specs=list(in_specs) + [ANY] * len(j_args), out_specs=list(out_specs) + [ANY] * len(j_out),
        out_shape=[_out_in_hbm(s) for s in list(out_shape) + list(j_out)],
        scratch_shapes=list(scratch_shapes) + list(j_scr),
        compiler_params=_cparams(len(grid), vmem, None if job is None else _PEER_SETS.index(job.peers)),
    )(*args, *j_args)
    return res[:n_out], res[n_out:]


def _fwd_pool(x, g_pre, job=None):
    T = x.shape[0]
    tm = ROW_TILE
    nt = T // tm

    def body(x_ref, gpre_ref, d_ref, carry):
        i = pl.program_id(0)

        @pl.when(i == 0)
        def _():
            carry[...] = jnp.zeros_like(carry)

        h = _rms(x_ref[...], gpre_ref[...])
        ext = jnp.concatenate([carry[...], h], axis=0)
        carry[...] = h[tm - POOL_HALO:, :]
        sums = _window_sums(ext, lambda k: k)[POOL_HALO:, :]
        d_ref[...] = (sums / _pool_counts(i * tm, tm) - h).astype(BF16)

    return _launch(
        body, name="fwd_pool", grid=(nt,), in_specs=[_row_spec(D_MODEL), _vec_spec()], out_specs=[_row_spec(D_MODEL)],
        out_shape=[jax.ShapeDtypeStruct((T, D_MODEL), BF16)], scratch_shapes=[pltpu.VMEM((POOL_HALO, D_MODEL), F32)],
        args=(x, g_pre), job=job)


def _fwd_pool_mixer(x, d, wp, scale, g_post, g_ffn, job=None):
    T = x.shape[0]
    nt = T // ROW_TILE

    def body(x_ref, d_ref, wp_ref, sc_ref, gpost_ref, gffn_ref, x1_ref, h2_ref, yraw_ref):
        db = d_ref[...]
        yraw = jnp.concatenate(
            [_dot(db[:, g * POOL_GROUP:(g + 1) * POOL_GROUP], wp_ref[g]) for g in range(N_POOL_GROUPS)], axis=1)
        yraw_ref[...] = yraw.astype(BF16)
        x1 = x_ref[...] + _rms(yraw * sc_ref[...], gpost_ref[...])
        x1_ref[...] = x1
        h2_ref[...] = _rms(x1, gffn_ref[...]).astype(BF16)

    return _launch(
        body, name="fwd_pool_mixer", grid=(nt,),
        in_specs=[_row_spec(D_MODEL), _row_spec(D_MODEL), _full_spec((N_POOL_GROUPS, POOL_GROUP, POOL_GROUP)),
                  _vec_spec(), _vec_spec(), _vec_spec()],
        out_specs=[_row_spec(D_MODEL)] * 3,
        out_shape=[jax.ShapeDtypeStruct((T, D_MODEL), F32)] + [jax.ShapeDtypeStruct((T, D_MODEL), BF16)] * 2,
        args=(x, d, wp, scale, g_post, g_ffn), job=job)


def _fwd_ffn(layer, h2, x1, wgu, wd, g_post, g_ple, job=None):
    T = h2.shape[0]
    tm = min(FFN_ROW_TILE, T)
    nt = T // tm
    sub = tm // FFN_SUB_TILES
    last = FF_CHUNKS - 1
    wgu, wd = _column_views(wgu), _column_views(wd)
    n_gu, n_wd = len(wgu), len(wd)
    gu_cols = _column_ranges(wgu)

    def body(h2_ref, x1_ref, *refs):
        wgu_refs, wd_refs = refs[:n_gu], refs[n_gu:n_gu + n_wd]
        gpost_ref, gple_ref, gs_ref, us_ref, f_ref, x2_ref, h3_ref, acc = refs[n_gu + n_wd:]
        k = pl.program_id(0)
        i = pl.program_id(1)
        rows = pl.ds(pl.multiple_of(i * tm, tm), tm)
        parts = []
        for s in range(FFN_SUB_TILES):
            r = pl.ds(s * sub, sub)
            g = _add_all([_dot_nt(h2_ref[r, c0:c1], w[0]) for (c0, c1), w in zip(gu_cols, wgu_refs)])
            u = _add_all([_dot_nt(h2_ref[r, c0:c1], w[1]) for (c0, c1), w in zip(gu_cols, wgu_refs)])
            gs_ref[r, :] = g.astype(BF16)
            us_ref[r, :] = u.astype(BF16)
            a = (g * _sigmoid(g) * u).astype(BF16)
            parts.append(jnp.concatenate([_dot(a, w[...]) for w in wd_refs], axis=1))
        part = jnp.concatenate(parts, axis=0)

        @pl.when(k == 0)
        def _():
            acc[rows, :] = part

        @pl.when(jnp.logical_and(k > 0, k < last))
        def _():
            acc[rows, :] += part

        @pl.when(k == last)
        def _():
            f = acc[rows, :] + part
            f_ref[...] = f.astype(BF16)
            x2 = x1_ref[...] + _rms(f, gpost_ref[...])
            x2_ref[...] = x2
            h3_ref[...] = _rms(x2, gple_ref[...]).astype(BF16)

    def late(k, i):
        return (jnp.where(k == last, i, 0), 0)

    return _launch(
        body, name=f"fwd_ffn{layer}", grid=(FF_CHUNKS, nt),
        in_specs=[pl.BlockSpec((tm, D_MODEL), lambda k, i: (i, 0)), pl.BlockSpec((tm, D_MODEL), late)]
                 + [pl.BlockSpec((None, 2, FF_BLOCK, FFN_WEIGHT_COLS), lambda k, i, b=b: (k, 0, 0, b)) for _, b in wgu]
                 + [pl.BlockSpec((FF_BLOCK, FFN_WEIGHT_COLS), lambda k, i, b=b: (k, b)) for _, b in wd]
                 + [pl.BlockSpec((1, D_MODEL), lambda k, i: (0, 0))] * 2,
        out_specs=[pl.BlockSpec((None, tm, FF_BLOCK), lambda k, i: (k, i, 0)),
                   pl.BlockSpec((None, tm, FF_BLOCK), lambda k, i: (k, i, 0)),
                   pl.BlockSpec((tm, D_MODEL), late),
                   pl.BlockSpec((tm, D_MODEL), late),
                   pl.BlockSpec((tm, D_MODEL), late)],
        out_shape=[jax.ShapeDtypeStruct((FF_CHUNKS, T, FF_BLOCK), BF16),
                   jax.ShapeDtypeStruct((FF_CHUNKS, T, FF_BLOCK), BF16),
                   jax.ShapeDtypeStruct((T, D_MODEL), BF16),
                   jax.ShapeDtypeStruct((T, D_MODEL), F32),
                   jax.ShapeDtypeStruct((T, D_MODEL), BF16)],
        scratch_shapes=[pltpu.VMEM((T, D_MODEL), F32)],
        args=(h2, x1, *[w for w, _ in wgu], *[w for w, _ in wd], g_post, g_ple), vmem=VMEM_BIG, job=job)


def _fwd_ple_qkv(x2, h3, p, wgate, wproj, g_post, g_kv, g_mix, wkv, wq, job=None):
    T = x2.shape[0]
    nt = T // ROW_TILE

    def body(x2_ref, h3_ref, p_ref, wg_ref, wp_ref, gpost_ref, gkv_ref, gmix_ref, wkv_ref, wq_ref,
             x3_ref, z_ref, pe_ref, hk_ref, h1_ref, q_ref, kpad_ref, vpad_ref):
        z = _dot(h3_ref[...], wg_ref[...])
        pe = _dot(p_ref[...].astype(BF16), wp_ref[...])
        z_ref[...] = z.astype(BF16)
        pe_ref[...] = pe.astype(BF16)
        x3 = x2_ref[...] + _rms(pe * _sigmoid(z), gpost_ref[...])
        x3_ref[...] = x3
        r = _rstd(x3)
        hk = (x3 * r * gkv_ref[...]).astype(BF16)
        h1 = (x3 * r * gmix_ref[...]).astype(BF16)
        hk_ref[...] = hk
        h1_ref[...] = h1
        kv = _dot(hk, wkv_ref[...]).astype(BF16)
        q_ref[...] = _dot(h1, wq_ref[...]).astype(BF16)
        i = pl.program_id(0)

        @pl.when(i == 0)
        def _():
            kpad_ref[:ATT_BLOCK, :] = jnp.zeros((ATT_BLOCK, KV_DIM), BF16)
            vpad_ref[:ATT_BLOCK, :] = jnp.zeros((ATT_BLOCK, KV_DIM), BF16)

        rows = pl.ds(pl.multiple_of(ATT_BLOCK + i * ROW_TILE, ATT_BLOCK), ROW_TILE)
        kpad_ref[rows, :] = kv[:, :KV_DIM]
        vpad_ref[rows, :] = kv[:, KV_DIM:]

    wide = jax.ShapeDtypeStruct((T, D_MODEL), BF16)
    padded = (ATT_BLOCK + T, KV_DIM)
    return _launch(
        body, name="fwd_ple_qkv", grid=(nt,),
        in_specs=[_row_spec(D_MODEL), _row_spec(D_MODEL), _row_spec(PLE_DIM), _full_spec((D_MODEL, D_MODEL)),
                  _full_spec((PLE_DIM, D_MODEL)), _vec_spec(), _vec_spec(), _vec_spec(),
                  _full_spec((D_MODEL, 2 * KV_DIM)), _full_spec((D_MODEL, D_MODEL))],
        out_specs=[_row_spec(D_MODEL)] * 6 + [_full_spec(padded)] * 2,
        out_shape=[jax.ShapeDtypeStruct((T, D_MODEL), F32)] + [wide] * 5 + [jax.ShapeDtypeStruct(padded, BF16)] * 2,
        args=(x2, h3, p, wgate, wproj, g_post, g_kv, g_mix, wkv, wq), job=job)


def _alibi_slope(h):
    return 2.0 ** (-8.0 * (h + 1) / N_HEADS)


ATT_SUB = 32
ATT_GROUP_ROWS = GQA_GROUP * ATT_BLOCK


def _att_mask(n, rel_ref, off_ref):
    qi = lax.broadcasted_iota(jnp.int32, (ATT_BLOCK, 2 * ATT_BLOCK), 0)
    si = lax.broadcasted_iota(jnp.int32, (ATT_BLOCK, 2 * ATT_BLOCK), 1)
    rel = ATT_BLOCK + qi - si
    valid = (rel >= 0) & (rel < ATT_BLOCK) & ((si >= ATT_BLOCK) | (n > 0))
    rel_ref[...] = rel.astype(F32)
    off_ref[...] = jnp.where(valid, 0.0, NEG_INF)


def _att_probs(raw, relf, off, slope, sink):
    s = raw * ATT_SCALE - slope * relf + off
    m = jnp.maximum(jnp.max(s, axis=-1, keepdims=True), sink)
    e = jnp.exp(s - m)
    es = jnp.exp(sink - m)
    inv = 1.0 / (jnp.sum(e, axis=-1, keepdims=True) + es)
    return e * inv, es * inv


def _stack_heads(ref, kh):
    first = kh * GQA_GROUP
    return jnp.concatenate([ref[:, (first + g) * HEAD_DIM:(first + g + 1) * HEAD_DIM] for g in range(GQA_GROUP)], axis=0)


def _unstack_heads(stacked):
    return [stacked[g * ATT_BLOCK:(g + 1) * ATT_BLOCK, :] for g in range(GQA_GROUP)]


def _fwd_attention(q, kpad, vpad, sinks, job=None):
    T = q.shape[0]
    nb = T // ATT_BLOCK

    def body(q_ref, k_ref, v_ref, sink_ref, o_ref, s_scr, p_scr, rel_scr, off_scr):
        n = pl.program_id(0)
        start = pl.multiple_of(n * ATT_BLOCK, ATT_BLOCK)
        kw = k_ref[pl.ds(start, 2 * ATT_BLOCK), :]
        vw = v_ref[pl.ds(start, 2 * ATT_BLOCK), :]
        _att_mask(n, rel_scr, off_scr)
        outs = []
        for kh in range(N_KV_HEADS):
            kk = kw[:, kh * HEAD_DIM:(kh + 1) * HEAD_DIM]
            vv = vw[:, kh * HEAD_DIM:(kh + 1) * HEAD_DIM]
            s_scr[...] = _dot_nt(_stack_heads(q_ref, kh), kk)
            for g in range(GQA_GROUP):
                h = kh * GQA_GROUP + g
                for row0 in range(0, ATT_BLOCK, ATT_SUB):
                    rows, sub = pl.ds(g * ATT_BLOCK + row0, ATT_SUB), pl.ds(row0, ATT_SUB)
                    pr, _ = _att_probs(s_scr[rows, :], rel_scr[sub, :], off_scr[sub, :], _alibi_slope(h),
                                       sink_ref[0, h])
                    p_scr[rows, :] = pr.astype(BF16)
            outs += _unstack_heads(_dot(p_scr[...], vv))
        o_ref[...] = jnp.concatenate(outs, axis=1).astype(BF16)

    return _launch(
        body, name="fwd_attention", grid=(nb,),
        in_specs=[_row_spec(D_MODEL, ATT_BLOCK), _full_spec((T + ATT_BLOCK, KV_DIM)), _full_spec((T + ATT_BLOCK, KV_DIM)),
                  pl.BlockSpec(memory_space=pltpu.SMEM)],
        out_specs=[_row_spec(D_MODEL, ATT_BLOCK)],
        out_shape=[jax.ShapeDtypeStruct((T, D_MODEL), BF16)],
        scratch_shapes=[pltpu.VMEM((ATT_GROUP_ROWS, 2 * ATT_BLOCK), F32), pltpu.VMEM((ATT_GROUP_ROWS, 2 * ATT_BLOCK), BF16)]
                       + [pltpu.VMEM((ATT_BLOCK, 2 * ATT_BLOCK), F32)] * 2,
        args=(q, kpad, vpad, sinks), job=job)


def _fwd_attn_out(attn, x, wo, g_post, g_ffn, job=None):
    T = x.shape[0]
    nt = T // ROW_TILE

    def body(a_ref, x_ref, wo_ref, gpost_ref, gffn_ref, y_ref, x1_ref, h2_ref):
        y = _dot(a_ref[...], wo_ref[...])
        y_ref[...] = y.astype(BF16)
        x1 = x_ref[...] + _rms(y, gpost_ref[...])
        x1_ref[...] = x1
        h2_ref[...] = _rms(x1, gffn_ref[...]).astype(BF16)

    return _launch(
        body, name="fwd_attn_out", grid=(nt,),
        in_specs=[_row_spec(D_MODEL), _row_spec(D_MODEL), _full_spec((D_MODEL, D_MODEL)), _vec_spec(), _vec_spec()],
        out_specs=[_row_spec(D_MODEL)] * 3,
        out_shape=[jax.ShapeDtypeStruct((T, D_MODEL), BF16), jax.ShapeDtypeStruct((T, D_MODEL), F32),
                   jax.ShapeDtypeStruct((T, D_MODEL), BF16)],
        args=(attn, x, wo, g_post, g_ffn), job=job)


def _bwd_ple(layer, dx3, x2, z, pe, h3, p, f, wgate, g_ple_post, g_ple, g_post_ffn, job=None):
    T = x2.shape[0]
    tm = ROW_TILE
    nt = T // tm

    def body(dx3_ref, x2_ref, z_ref, pe_ref, h3_ref, p_ref, f_ref, wg_ref, gpp_ref, gp_ref, gpf_ref,
             dx2_ref, df_ref, dwg_ref, dwp_ref, dgpp_ref, dgp_ref, dgpf_ref, acc_g, acc_p):
        i = pl.program_id(0)
        first = i == 0
        dx3v = dx3_ref[...]
        gate = _sigmoid(z_ref[...].astype(F32))
        pev = pe_ref[...].astype(F32)
        de, dgpp = _rms_bwd(pev * gate, gpp_ref[...], dx3v)
        dpe = (de * gate).astype(BF16)
        dz = (de * pev * gate * (1.0 - gate)).astype(BF16)
        _acc(acc_p, _dot_tn(p_ref[...].astype(BF16), dpe), first)
        _acc(acc_g, _dot_tn(h3_ref[...], dz), first)
        dh3 = _dot_nt(dz, wg_ref[...])
        dxn, dgp = _rms_bwd(x2_ref[...], gp_ref[...], dh3)
        dx2 = dx3v + dxn
        dx2_ref[...] = dx2
        df, dgpf = _rms_bwd(f_ref[...].astype(F32), gpf_ref[...], dx2)
        df_ref[...] = df.astype(BF16)
        _acc(dgpp_ref, dgpp, first)
        _acc(dgp_ref, dgp, first)
        _acc(dgpf_ref, dgpf, first)

        @pl.when(i == nt - 1)
        def _():
            dwg_ref[...] = acc_g[...].astype(BF16)
            dwp_ref[...] = acc_p[...].astype(BF16)

    return _launch(
        body, name=f"bwd_ple{layer}", grid=(nt,),
        in_specs=[_row_spec(D_MODEL)] * 5 + [_row_spec(PLE_DIM), _row_spec(D_MODEL), _full_spec((D_MODEL, D_MODEL)),
                  _vec_spec(), _vec_spec(), _vec_spec()],
        out_specs=[_row_spec(D_MODEL), _row_spec(D_MODEL), _full_spec((D_MODEL, D_MODEL)), _full_spec((PLE_DIM, D_MODEL)),
                   _vec_spec(), _vec_spec(), _vec_spec()],
        out_shape=[jax.ShapeDtypeStruct((T, D_MODEL), F32), jax.ShapeDtypeStruct((T, D_MODEL), BF16),
                   jax.ShapeDtypeStruct((D_MODEL, D_MODEL), BF16), jax.ShapeDtypeStruct((PLE_DIM, D_MODEL), BF16)]
                  + [jax.ShapeDtypeStruct((1, D_MODEL), F32)] * 3,
        scratch_shapes=[pltpu.VMEM((D_MODEL, D_MODEL), F32), pltpu.VMEM((PLE_DIM, D_MODEL), F32)],
        args=(dx3, x2, z, pe, h3, p, f, wgate, g_ple_post, g_ple, g_post_ffn), vmem=VMEM_BIG, job=job)


def _ple_loss_bwd(layer, x2, h3, p, f, target, wgate, wproj, g_ple_post, g_ple, g_post_ffn, job=None):
    T = x2.shape[0]
    tm = ROW_TILE
    nt = T // tm

    def body(x2_ref, h3_ref, p_ref, f_ref, tgt_ref, wg_ref, wp_ref, gpp_ref, gp_ref, gpf_ref,
             dx2_ref, df_ref, dwg_ref, dwp_ref, dgpp_ref, dgp_ref, dgpf_ref, loss_ref, acc_g, acc_p):
        i = pl.program_id(0)
        first = i == 0
        h3 = h3_ref[...]
        pb = p_ref[...].astype(BF16)
        x2v = x2_ref[...]
        gate = _sigmoid(_dot(h3, wg_ref[...]))
        pev = _dot(pb, wp_ref[...])
        e = pev * gate
        err = x2v + _rms(e, gpp_ref[...]) - tgt_ref[...]
        _acc(loss_ref, 0.5 * jnp.sum(jnp.mean(err * err, axis=-1, keepdims=True), axis=0, keepdims=True), first)
        dx3v = err * (1.0 / D_MODEL)
        de, dgpp = _rms_bwd(e, gpp_ref[...], dx3v)
        dpe = (de * gate).astype(BF16)
        dz = (de * pev * gate * (1.0 - gate)).astype(BF16)
        _acc(acc_p, _dot_tn(pb, dpe), first)
        _acc(acc_g, _dot_tn(h3, dz), first)
        dxn, dgp = _rms_bwd(x2v, gp_ref[...], _dot_nt(dz, wg_ref[...]))
        dx2 = dx3v + dxn
        dx2_ref[...] = dx2
        df, dgpf = _rms_bwd(f_ref[...].astype(F32), gpf_ref[...], dx2)
        df_ref[...] = df.astype(BF16)
        _acc(dgpp_ref, dgpp, first)
        _acc(dgp_ref, dgp, first)
        _acc(dgpf_ref, dgpf, first)

        @pl.when(i == nt - 1)
        def _():
            dwg_ref[...] = acc_g[...].astype(BF16)
            dwp_ref[...] = acc_p[...].astype(BF16)

    return _launch(
        body, name=f"ple_loss_bwd{layer}", grid=(nt,),
        in_specs=[_row_spec(D_MODEL), _row_spec(D_MODEL), _row_spec(PLE_DIM), _row_spec(D_MODEL), _row_spec(D_MODEL),
                  _full_spec((D_MODEL, D_MODEL)), _full_spec((PLE_DIM, D_MODEL)), _vec_spec(), _vec_spec(), _vec_spec()],
        out_specs=[_row_spec(D_MODEL), _row_spec(D_MODEL), _full_spec((D_MODEL, D_MODEL)), _full_spec((PLE_DIM, D_MODEL)),
                   _vec_spec(), _vec_spec(), _vec_spec(), _full_spec((1, 1))],
        out_shape=[jax.ShapeDtypeStruct((T, D_MODEL), F32), jax.ShapeDtypeStruct((T, D_MODEL), BF16),
                   jax.ShapeDtypeStruct((D_MODEL, D_MODEL), BF16), jax.ShapeDtypeStruct((PLE_DIM, D_MODEL), BF16)]
                  + [jax.ShapeDtypeStruct((1, D_MODEL), F32)] * 3 + [jax.ShapeDtypeStruct((1, 1), F32)],
        scratch_shapes=[pltpu.VMEM((D_MODEL, D_MODEL), F32), pltpu.VMEM((PLE_DIM, D_MODEL), F32)],
        args=(x2, h3, p, f, target, wgate, wproj, g_ple_post, g_ple, g_post_ffn), vmem=VMEM_BIG, job=job)


def _bwd_ffn_act(layer, df, gs, us, wgu, wd, job=None):
    T = df.shape[0]
    tm = min(FFN_ROW_TILE, T)
    nt = T // tm
    sub = tm // FFN_SUB_TILES
    last = FF_CHUNKS - 1
    wgu, wd = _column_views(wgu), _column_views(wd)
    n_gu, n_wd = len(wgu), len(wd)
    wd_cols = _column_ranges(wd)

    def body(df_ref, gs_ref, us_ref, *refs):
        wgu_refs, wd_refs = refs[:n_gu], refs[n_gu:n_gu + n_wd]
        dh_ref, dg_ref, du_ref, a_ref, acc_h = refs[n_gu + n_wd:]
        k = pl.program_id(0)
        i = pl.program_id(1)
        rows = pl.ds(pl.multiple_of(i * tm, tm), tm)
        dhs = []
        for s in range(FFN_SUB_TILES):
            r = pl.ds(s * sub, sub)
            g = gs_ref[r, :].astype(F32)
            u = us_ref[r, :].astype(F32)
            sg = _sigmoid(g)
            silu = g * sg
            a_ref[r, :] = (silu * u).astype(BF16)
            da = _add_all([_dot_nt(df_ref[r, c0:c1], w[...]) for (c0, c1), w in zip(wd_cols, wd_refs)])
            dg = (da * u * (sg * (1.0 + g * (1.0 - sg)))).astype(BF16)
            du = (da * silu).astype(BF16)
            dg_ref[r, :] = dg
            du_ref[r, :] = du
            dhs.append(jnp.concatenate([_dot(dg, w[0]) + _dot(du, w[1]) for w in wgu_refs], axis=1))
        dh = jnp.concatenate(dhs, axis=0)

        @pl.when(k == 0)
        def _():
            acc_h[rows, :] = dh

        @pl.when(jnp.logical_and(k > 0, k < last))
        def _():
            acc_h[rows, :] += dh

        @pl.when(k == last)
        def _():
            dh_ref[...] = acc_h[rows, :] + dh

    chunk_rows = pl.BlockSpec((None, tm, FF_BLOCK), lambda k, i: (k, i, 0))
    saved = jax.ShapeDtypeStruct((FF_CHUNKS, T, FF_BLOCK), BF16)
    return _launch(
        body, name=f"bwd_ffn_act{layer}", grid=(FF_CHUNKS, nt),
        in_specs=[pl.BlockSpec((tm, D_MODEL), lambda k, i: (i, 0)), chunk_rows, chunk_rows]
                 + [pl.BlockSpec((None, 2, FF_BLOCK, FFN_WEIGHT_COLS), lambda k, i, b=b: (k, 0, 0, b)) for _, b in wgu]
                 + [pl.BlockSpec((FF_BLOCK, FFN_WEIGHT_COLS), lambda k, i, b=b: (k, b)) for _, b in wd],
        out_specs=[pl.BlockSpec((tm, D_MODEL), lambda k, i: (jnp.where(k == last, i, 0), 0)),
                   chunk_rows, chunk_rows, chunk_rows],
        out_shape=[jax.ShapeDtypeStruct((T, D_MODEL), F32), saved, saved, saved],
        scratch_shapes=[pltpu.VMEM((T, D_MODEL), F32)],
        args=(df, gs, us, *[w for w, _ in wgu], *[w for w, _ in wd]), vmem=VMEM_BIG, job=job)


def _bwd_ffn_dw(layer, q, parts, h2, df, dg, du, a, job=None):
    T = h2.shape[0]
    width = D_MODEL // parts

    def body(h_ref, df_ref, dg_ref, du_ref, a_ref, dgu_ref, dwd_ref):
        h = h_ref[...]
        dgu_ref[0] = _dot_tn(dg_ref[...], h).astype(BF16)
        dgu_ref[1] = _dot_tn(du_ref[...], h).astype(BF16)
        dwd_ref[...] = _dot_tn(a_ref[...], df_ref[...]).astype(BF16)

    cols = pl.BlockSpec((T, width), lambda k: (0, q))
    chunk = pl.BlockSpec((None, T, FF_BLOCK), lambda k: (k, 0, 0))
    return _launch(
        body, name=f"bwd_ffn_dw{layer}_{q}", grid=(FF_CHUNKS,),
        in_specs=[cols, cols, chunk, chunk, chunk],
        out_specs=[pl.BlockSpec((None, 2, FF_BLOCK, width), lambda k: (k, 0, 0, 0)),
                   pl.BlockSpec((FF_BLOCK, width), lambda k: (k, 0))],
        out_shape=[jax.ShapeDtypeStruct((FF_CHUNKS, 2, FF_BLOCK, width), BF16),
                   jax.ShapeDtypeStruct((D_FF, width), BF16)],
        args=(h2, df, dg, du, a), vmem=VMEM_BIG, job=job)


def _bwd_attn_out(dx2, dh2, x1, y, attn, wo, g_ffn, g_post, job=None):
    T = x1.shape[0]
    nt = T // ROW_TILE

    def body(dx2_ref, dh2_ref, x1_ref, y_ref, a_ref, wo_ref, gffn_ref, gpost_ref,
             dx1_ref, da_ref, dwo_ref, dgf_ref, dgp_ref, acc):
        i = pl.program_id(0)
        first = i == 0
        dxn, dgf = _rms_bwd(x1_ref[...], gffn_ref[...], dh2_ref[...])
        dx1 = dx2_ref[...] + dxn
        dx1_ref[...] = dx1
        dy, dgp = _rms_bwd(y_ref[...].astype(F32), gpost_ref[...], dx1)
        dyb = dy.astype(BF16)
        da_ref[...] = _dot_nt(dyb, wo_ref[...]).astype(BF16)
        _acc(acc, _dot_tn(a_ref[...], dyb), first)
        _acc(dgf_ref, dgf, first)
        _acc(dgp_ref, dgp, first)

        @pl.when(i == nt - 1)
        def _():
            dwo_ref[...] = acc[...].astype(BF16)

    return _launch(
        body, name="bwd_attn_out", grid=(nt,),
        in_specs=[_row_spec(D_MODEL)] * 5 + [_full_spec((D_MODEL, D_MODEL)), _vec_spec(), _vec_spec()],
        out_specs=[_row_spec(D_MODEL), _row_spec(D_MODEL), _full_spec((D_MODEL, D_MODEL)), _vec_spec(), _vec_spec()],
        out_shape=[jax.ShapeDtypeStruct((T, D_MODEL), F32), jax.ShapeDtypeStruct((T, D_MODEL), BF16),
                   jax.ShapeDtypeStruct((D_MODEL, D_MODEL), BF16)] + [jax.ShapeDtypeStruct((1, D_MODEL), F32)] * 2,
        scratch_shapes=[pltpu.VMEM((D_MODEL, D_MODEL), F32)],
        args=(dx2, dh2, x1, y, attn, wo, g_ffn, g_post), job=job)


def _bwd_attention(q, dattn, kpad, vpad, sinks, job=None):
    T = q.shape[0]
    nb = T // ATT_BLOCK

    def body(q_ref, do_ref, k_ref, v_ref, sink_ref, dq_ref, dkv_ref, ds_ref, dk_ref, dv_ref, s_scr, dp_scr, p_scr,
             dsb_scr, rel_scr, off_scr):
        n = pl.program_id(0)
        _att_mask(n, rel_scr, off_scr)

        @pl.when(n == 0)
        def _():
            dk_ref[...] = jnp.zeros_like(dk_ref)
            dv_ref[...] = jnp.zeros_like(dv_ref)
            ds_ref[...] = jnp.zeros_like(ds_ref)

        start = pl.multiple_of(n * ATT_BLOCK, ATT_BLOCK)
        win = pl.ds(start, 2 * ATT_BLOCK)
        kw = k_ref[win, :]
        vw = v_ref[win, :]
        lane = lax.broadcasted_iota(jnp.int32, (1, ATT_BLOCK), 1)
        dsink = jnp.zeros((1, ATT_BLOCK), F32)
        dqs, dks, dvs = [], [], []
        for kh in range(N_KV_HEADS):
            kk = kw[:, kh * HEAD_DIM:(kh + 1) * HEAD_DIM]
            vv = vw[:, kh * HEAD_DIM:(kh + 1) * HEAD_DIM]
            qs = _stack_heads(q_ref, kh)
            dos = _stack_heads(do_ref, kh)
            s_scr[...] = _dot_nt(qs, kk)
            dp_scr[...] = _dot_nt(dos, vv)
            for g in range(GQA_GROUP):
                h = kh * GQA_GROUP + g
                dsink_h = jnp.zeros((1, 1), F32)
                for row0 in range(0, ATT_BLOCK, ATT_SUB):
                    rows, sub = pl.ds(g * ATT_BLOCK + row0, ATT_SUB), pl.ds(row0, ATT_SUB)
                    pr, ps = _att_probs(s_scr[rows, :], rel_scr[sub, :], off_scr[sub, :], _alibi_slope(h),
                                        sink_ref[0, h])
                    dp = dp_scr[rows, :]
                    delta = jnp.sum(pr * dp, axis=-1, keepdims=True)
                    dsb_scr[rows, :] = (pr * (dp - delta) * ATT_SCALE).astype(BF16)
                    p_scr[rows, :] = pr.astype(BF16)
                    dsink_h = dsink_h - jnp.sum(ps * delta, axis=0, keepdims=True)
                dsink = dsink + jnp.where(lane == h, dsink_h, 0.0)
            dsb = dsb_scr[...]
            dqs += _unstack_heads(_dot(dsb, kk))
            dks.append(_dot_tn(dsb, qs))
            dvs.append(_dot_tn(p_scr[...], dos))
        dq_ref[...] = jnp.concatenate(dqs, axis=1).astype(BF16)
        dk_ref[win, :] += jnp.concatenate(dks, axis=1)
        dv_ref[win, :] += jnp.concatenate(dvs, axis=1)
        ds_ref[...] += dsink

        @pl.when(n == nb - 1)
        def _():
            dkv_ref[:, :KV_DIM] = dk_ref[ATT_BLOCK:, :].astype(BF16)
            dkv_ref[:, KV_DIM:] = dv_ref[ATT_BLOCK:, :].astype(BF16)

    return _launch(
        body, name="bwd_attention", grid=(nb,),
        in_specs=[_row_spec(D_MODEL, ATT_BLOCK), _row_spec(D_MODEL, ATT_BLOCK), _full_spec((T + ATT_BLOCK, KV_DIM)),
                  _full_spec((T + ATT_BLOCK, KV_DIM)), pl.BlockSpec(memory_space=pltpu.SMEM)],
        out_specs=[_row_spec(D_MODEL, ATT_BLOCK), _full_spec((T, 2 * KV_DIM)), _full_spec((1, ATT_BLOCK))],
        out_shape=[jax.ShapeDtypeStruct((T, D_MODEL), BF16), jax.ShapeDtypeStruct((T, 2 * KV_DIM), BF16),
                   jax.ShapeDtypeStruct((1, ATT_BLOCK), F32)],
        scratch_shapes=[pltpu.VMEM((T + ATT_BLOCK, KV_DIM), F32)] * 2
                       + [pltpu.VMEM((ATT_GROUP_ROWS, 2 * ATT_BLOCK), F32)] * 2
                       + [pltpu.VMEM((ATT_GROUP_ROWS, 2 * ATT_BLOCK), BF16)] * 2
                       + [pltpu.VMEM((ATT_BLOCK, 2 * ATT_BLOCK), F32)] * 2,
        args=(q, dattn, kpad, vpad, sinks), vmem=VMEM_BIG, job=job)


def _bwd_qkv(dxres, dq, dkv, x3, h1, hk, wq, wkv, g_mix, g_kv, job=None):
    T = x3.shape[0]
    nt = T // ROW_TILE

    def body(dxr_ref, dq_ref, dkv_ref, x_ref, h1_ref, hk_ref, wq_ref, wkv_ref, gmix_ref, gkv_ref,
             dx_ref, dwq_ref, dwkv_ref, dgm_ref, dgk_ref, acc_q, acc_kv):
        i = pl.program_id(0)
        first = i == 0
        dqv = dq_ref[...]
        dkvv = dkv_ref[...]
        xv = x_ref[...]
        d1, dgm = _rms_bwd(xv, gmix_ref[...], _dot_nt(dqv, wq_ref[...]))
        d2, dgk = _rms_bwd(xv, gkv_ref[...], _dot_nt(dkvv, wkv_ref[...]))
        dx_ref[...] = dxr_ref[...] + d1 + d2
        _acc(acc_q, _dot_tn(h1_ref[...], dqv), first)
        _acc(acc_kv, _dot_tn(hk_ref[...], dkvv), first)
        _acc(dgm_ref, dgm, first)
        _acc(dgk_ref, dgk, first)

        @pl.when(i == nt - 1)
        def _():
            dwq_ref[...] = acc_q[...].astype(BF16)
            dwkv_ref[...] = acc_kv[...].astype(BF16)

    return _launch(
        body, name="bwd_qkv", grid=(nt,),
        in_specs=[_row_spec(D_MODEL), _row_spec(D_MODEL), _row_spec(2 * KV_DIM), _row_spec(D_MODEL), _row_spec(D_MODEL),
                  _row_spec(D_MODEL), _full_spec((D_MODEL, D_MODEL)), _full_spec((D_MODEL, 2 * KV_DIM)), _vec_spec(),
                  _vec_spec()],
        out_specs=[_row_spec(D_MODEL), _full_spec((D_MODEL, D_MODEL)), _full_spec((D_MODEL, 2 * KV_DIM)), _vec_spec(),
                   _vec_spec()],
        out_shape=[jax.ShapeDtypeStruct((T, D_MODEL), F32), jax.ShapeDtypeStruct((D_MODEL, D_MODEL), BF16),
                   jax.ShapeDtypeStruct((D_MODEL, 2 * KV_DIM), BF16)] + [jax.ShapeDtypeStruct((1, D_MODEL), F32)] * 2,
        scratch_shapes=[pltpu.VMEM((D_MODEL, D_MODEL), F32), pltpu.VMEM((D_MODEL, 2 * KV_DIM), F32)],
        args=(dxres, dq, dkv, x3, h1, hk, wq, wkv, g_mix, g_kv), job=job)


def _bwd_pool_mixer(dx2, dh2, x1, x, yraw, d, wp, scale, g_ffn, g_post, g_pre, job=None):
    T = x.shape[0]
    tm = ROW_TILE
    nt = T // tm

    def body(dx2_ref, dh2_ref, x1_ref, x_ref, yraw_ref, d_ref, wp_ref, sc_ref, gffn_ref, gpost_ref, gpre_ref,
             dx_ref, dwp_ref, dsc_ref, dgf_ref, dgp_ref, dgm_ref, carry, acc):
        i = pl.program_id(0)
        first = i == 0
        tile = nt - 1 - i

        @pl.when(first)
        def _():
            carry[...] = jnp.zeros_like(carry)

        dxn, dgf = _rms_bwd(x1_ref[...], gffn_ref[...], dh2_ref[...])
        dx1 = dx2_ref[...] + dxn
        yraw = yraw_ref[...].astype(F32)
        sc = sc_ref[...]
        dy, dgp = _rms_bwd(yraw * sc, gpost_ref[...], dx1)
        dsc = jnp.sum(dy * yraw, axis=0, keepdims=True)
        dyb = (dy * sc).astype(BF16)
        dv = d_ref[...]
        dds = []
        for g in range(N_POOL_GROUPS):
            cols = slice(g * POOL_GROUP, (g + 1) * POOL_GROUP)
            dds.append(_dot_nt(dyb[:, cols], wp_ref[g]))
            _acc(acc.at[g], _dot_tn(dv[:, cols], dyb[:, cols]), first)
        dd = jnp.concatenate(dds, axis=1)
        e = dd / _pool_counts(tile * tm, tm)
        ext = jnp.concatenate([e, carry[...]], axis=0)
        carry[...] = e[:POOL_HALO, :]
        sums = _window_sums(ext, lambda k: tm + POOL_HALO - k)[:tm, :]
        dxm, dgm = _rms_bwd(x_ref[...], gpre_ref[...], sums - dd)
        dx_ref[...] = dx1 + dxm
        _acc(dsc_ref, dsc, first)
        _acc(dgf_ref, dgf, first)
        _acc(dgp_ref, dgp, first)
        _acc(dgm_ref, dgm, first)

        @pl.when(i == nt - 1)
        def _():
            dwp_ref[...] = acc[...].astype(BF16)

    rev = pl.BlockSpec((tm, D_MODEL), lambda i: (nt - 1 - i, 0))
    return _launch(
        body, name="bwd_pool_mixer", grid=(nt,),
        in_specs=[rev] * 6 + [_full_spec((N_POOL_GROUPS, POOL_GROUP, POOL_GROUP))] + [_vec_spec()] * 4,
        out_specs=[rev, _full_spec((N_POOL_GROUPS, POOL_GROUP, POOL_GROUP))] + [_vec_spec()] * 4,
        out_shape=[jax.ShapeDtypeStruct((T, D_MODEL), F32),
                   jax.ShapeDtypeStruct((N_POOL_GROUPS, POOL_GROUP, POOL_GROUP), BF16)]
                  + [jax.ShapeDtypeStruct((1, D_MODEL), F32)] * 4,
        scratch_shapes=[pltpu.VMEM((POOL_HALO, D_MODEL), F32), pltpu.VMEM((N_POOL_GROUPS, POOL_GROUP, POOL_GROUP), F32)],
        args=(dx2, dh2, x1, x, yraw, d, wp, scale, g_ffn, g_post, g_pre), job=job)


def _my_place():
    return lax.axis_index("x"), lax.axis_index("y"), lax.axis_index("c")


def _dev_index(px, py, pc):
    return 4 * px + 2 * py + pc


def _peer_by_relation(r):
    x, y, c = _my_place()
    return (x ^ ((r >> 2) & 1), y ^ ((r >> 1) & 1), c ^ (r & 1))


def _slot_pool(ref, j):
    return ref.at[:, pl.ds(pl.multiple_of(j * 32, 32), 32), :]


def _slot_scale(ref, j):
    return ref.at[:, pl.ds(pl.multiple_of(j * 128, 128), 128)]


def _slot_rows128(ref, j):
    return ref.at[pl.ds(pl.multiple_of(j * 128, 128), 128), :]


def _slot_gu(ref, j):
    return ref.at[j % FF_CHUNKS, j // FF_CHUNKS]


def _slot_wd(ref, j):
    return ref.at[pl.ds(pl.multiple_of(j * WD_ROWS, 16), WD_ROWS), :]


def _slot_cols128(ref, j):
    return ref.at[:, pl.ds(pl.multiple_of(j * 128, 128), 128)]


_GATHERED = {
    "pool": ((N_POOL_GROUPS, POOL_GROUP, POOL_GROUP), BF16, _slot_pool),
    "scale": ((1, D_MODEL), F32, _slot_scale),
    "kv": ((D_MODEL, 2 * KV_DIM), BF16, _slot_rows128),
    "q": ((D_MODEL, D_MODEL), BF16, _slot_rows128),
    "o": ((D_MODEL, D_MODEL), BF16, _slot_rows128),
    "gu": ((FF_CHUNKS, 2, FF_BLOCK, D_MODEL), BF16, _slot_gu),
    "wd": ((D_FF, D_MODEL), BF16, _slot_wd),
    "guh": ((FF_CHUNKS, 2, FF_BLOCK, D_MODEL // 2), BF16, _slot_gu),
    "wdh": ((D_FF, D_MODEL // 2), BF16, _slot_wd),
    "gate": ((D_MODEL, D_MODEL), BF16, _slot_rows128),
    "proj": ((PLE_DIM, D_MODEL), BF16, _slot_cols128),
}


def _no_compute():
    pass


class _AllGather:
    peers = ("sibling", "x", "y")

    def __init__(self, names, shards):
        self.kinds = [_GATHERED[n.rstrip("01_")] for n in names]
        entries = [shards[n] if isinstance(shards[n], tuple) else (shards[n], None, None) for n in names]
        self.args = [array for array, _, _ in entries]
        self.layers = [layer for _, layer, _ in entries]
        self.columns = [columns for _, _, columns in entries]
        self.out_shape = [jax.ShapeDtypeStruct(shape, dtype) for shape, dtype, _ in self.kinds]
        n = len(names)
        self.scratch = [pltpu.SemaphoreType.DMA((n, 7)), pltpu.SemaphoreType.DMA((n, 7)), pltpu.SemaphoreType.DMA((n,))]

    def _plan(self, srcs, outs, sems):
        send_sems, recv_sems, local_sems = sems
        x, y, c = _my_place()

        def slot(t, dev):
            return self.kinds[t][2](outs[t], _dev_index(*dev))

        def copy(t, k, block, to, src=None):
            return pltpu.make_async_remote_copy(
                src_ref=slot(t, block) if src is None else src, dst_ref=slot(t, block),
                send_sem=send_sems.at[t, k], recv_sem=recv_sems.at[t, k], device_id=to, device_id_type=MESH)

        return types.SimpleNamespace(
            copy=copy, core=c, me=(x, y, c), sibling=(x, y, 1 - c),
            x_chip=(1 - x, y), y_chip=(x, 1 - y), far_chip=(1 - x, 1 - y),
            via=(x ^ (1 - c), y ^ c),
            onto=(x ^ c, y ^ (1 - c)),
            k_via=1 + c, k_onto=2 - c,
            local=[pltpu.make_async_copy(self._shard(srcs, t), slot(t, (x, y, c)), local_sems.at[t])
                   for t in range(len(srcs))])

    def _shard(self, srcs, t):
        shard = srcs[t] if self.layers[t] is None else srcs[t].at[self.layers[t]]
        if self.columns[t] is None:
            return shard
        first, end = self.columns[t]
        return shard.at[:, first:end]

    def start(self, srcs, outs, sems):
        p = self._plan(srcs, outs, sems)
        for cp in p.local:
            cp.start()
        for t in range(len(srcs)):
            shard = self._shard(srcs, t)
            p.copy(t, 0, p.me, p.sibling, src=shard).start()
            p.copy(t, 1, p.me, (*p.x_chip, p.core), src=shard).start()
            p.copy(t, 2, p.me, (*p.y_chip, p.core), src=shard).start()

    def mid(self, srcs, outs, sems):
        p = self._plan(srcs, outs, sems)
        for t in range(len(srcs)):
            block = (*p.via, p.core)
            p.copy(t, p.k_via, block, p.me).wait_recv()
            p.copy(t, 3, block, (*p.onto, p.core)).start()
            p.copy(t, 3 + p.k_via, block, p.sibling).start()

    def late(self, srcs, outs, sems):
        p = self._plan(srcs, outs, sems)
        n = len(srcs)
        for t in range(n):
            block = (*p.onto, p.core)
            p.copy(t, p.k_onto, block, p.me).wait_recv()
            p.copy(t, 3 + p.k_onto, block, p.sibling).start()
        for t in range(n):
            block = (*p.far_chip, p.core)
            p.copy(t, 3, block, p.me).wait_recv()
            p.copy(t, 6, block, p.sibling).start()

    def finish(self, srcs, outs, sems):
        p = self._plan(srcs, outs, sems)
        n = len(srcs)
        other = 1 - p.core
        for t in range(n):
            p.copy(t, 0, (*p.me[:2], other), p.me).wait_recv()
            for k, chip in ((4, p.x_chip), (5, p.y_chip), (6, p.far_chip)):
                p.copy(t, k, (*chip, other), p.me).wait_recv()
            for k in range(7):
                p.copy(t, k, p.me, p.sibling).wait_send()
        for cp in p.local:
            cp.wait()


def _jobs_only(name, job=None):
    return _launch(_no_compute, name=name, grid=(), in_specs=[], out_specs=[], out_shape=[], args=(), job=job)


def _block_pool(ref, j):
    return ref.at[:, pl.ds(pl.multiple_of(j * 32, 32), 32), :]


def _block_rows128(ref, j):
    return ref.at[pl.ds(pl.multiple_of(j * 128, 128), 128), :]


def _block_gu(ref, j):
    return ref.at[j % FF_CHUNKS, j // FF_CHUNKS]


def _block_wd(ref, j):
    return ref.at[pl.ds(pl.multiple_of(j * WD_ROWS, 16), WD_ROWS), :]


def _block_cols128(ref, j):
    return ref.at[:, pl.ds(pl.multiple_of(j * 128, 128), 128)]


_SCATTERED = {
    "pool": ((N_POOL_GROUPS, 32, POOL_GROUP), _block_pool),
    "kv": ((128, 2 * KV_DIM), _block_rows128),
    "q": ((128, D_MODEL), _block_rows128),
    "o": ((128, D_MODEL), _block_rows128),
    "gu": ((FF_BLOCK, FF_PART), _block_gu),
    "wd": ((WD_ROWS, FF_PART), _block_wd),
    "guA": ((FF_BLOCK, FF_PART), lambda ref, j: _block_gu(ref, j).at[:, :FF_PART]),
    "guB": ((FF_BLOCK, FF_PART), lambda ref, j: _block_gu(ref, j).at[:, FF_PART:]),
    "wdA": ((WD_ROWS, FF_PART), lambda ref, j: _block_wd(ref, j).at[:, :FF_PART]),
    "wdB": ((WD_ROWS, FF_PART), lambda ref, j: _block_wd(ref, j).at[:, FF_PART:]),
    "gate": ((128, D_MODEL), _block_rows128),
    "proj": ((PLE_DIM, 128), _block_cols128),
}


class _SiblingSwap:
    peers = ("sibling",)

    def __init__(self, pieces):
        self.kinds = [_SCATTERED[kind] for kind, _ in pieces]
        self.args = [g for _, g in pieces]
        self.out_shape = [jax.ShapeDtypeStruct((N_CHIPS, *block), BF16) for block, _ in self.kinds]
        n = len(pieces)
        self.scratch = [pltpu.SemaphoreType.DMA((n, N_CHIPS)), pltpu.SemaphoreType.DMA((n, N_CHIPS))]

    def _copies(self, srcs, outs, sems):
        send_sems, recv_sems = sems
        x, y, c = _my_place()
        return [pltpu.make_async_remote_copy(
            src_ref=block(srcs[t], 2 * ch + 1 - c), dst_ref=outs[t].at[ch], send_sem=send_sems.at[t, ch],
            recv_sem=recv_sems.at[t, ch], device_id=(x, y, 1 - c), device_id_type=MESH)
            for t, (_, block) in enumerate(self.kinds) for ch in range(N_CHIPS)]

    def start(self, srcs, outs, sems):
        for cp in self._copies(srcs, outs, sems):
            cp.start()

    def finish(self, srcs, outs, sems):
        for cp in self._copies(srcs, outs, sems):
            cp.wait()


class _ChipScatter:
    N_BUFS = 4
    peers = ("x", "y")

    def __init__(self, pieces):
        self.kinds = [_SCATTERED[kind] for kind, _, _ in pieces]
        self.n = n = len(pieces)
        self.args = [g for _, g, _ in pieces] + [s for _, _, s in pieces]
        self.out_shape = [jax.ShapeDtypeStruct((2, *block), BF16) for block, _ in self.kinds]
        self.scratch = []
        for block, _ in self.kinds:
            self.scratch += [pltpu.VMEM((N_CHIPS, *block), BF16)] * 3 + [pltpu.VMEM((2, *block), BF16)]
        dma = pltpu.SemaphoreType.DMA
        self.scratch += [dma((n, N_CHIPS + 1)), dma((n, 2)), dma((n, 2)), dma((n,)), dma((n,)), dma((n,))]

    def _plan(self, outs, scr):
        n = self.n
        first_send, first_recv, second_send, second_recv, keep_sems = scr[self.N_BUFS * n + 1:]
        x, y, c = _my_place()
        via = (x ^ (1 - c), y ^ c)
        onto = (x ^ c, y ^ (1 - c))
        far = (1 - x, 1 - y)
        index = lambda chip: 2 * chip[0] + chip[1]
        first = [[], []]
        second, keep = [], []
        for t in range(n):
            total, inbox = scr[self.N_BUFS * t + 2], scr[self.N_BUFS * t + 3]
            for k, chip in enumerate((via, far)):
                first[k].append(pltpu.make_async_remote_copy(
                    src_ref=total.at[index(chip)], dst_ref=inbox.at[k], send_sem=first_send.at[t, k],
                    recv_sem=first_recv.at[t, k], device_id=(*via, c), device_id_type=MESH))
            second.append(pltpu.make_async_remote_copy(
                src_ref=total.at[index(onto)], dst_ref=outs[t].at[1], send_sem=second_send.at[t],
                recv_sem=second_recv.at[t], device_id=(*onto, c), device_id_type=MESH))
            keep.append(pltpu.make_async_copy(total.at[index((x, y))], outs[t].at[0], keep_sems.at[t]))
        return types.SimpleNamespace(first=first, second=second, keep=keep, me=index((x, y)), via=index(via),
                                     onto=index(onto), far=index(far))

    def _add_inbox(self, scr, slot, k):
        for t in range(self.n):
            total, inbox = scr[self.N_BUFS * t + 2], scr[self.N_BUFS * t + 3]
            total[slot] = (total[slot].astype(F32) + inbox[k].astype(F32)).astype(BF16)

    def start(self, ins, outs, scr):
        n = self.n
        load_sems = scr[self.N_BUFS * n]
        c = lax.axis_index("c")
        loads = []
        for t, (_, block) in enumerate(self.kinds):
            mine, theirs = scr[self.N_BUFS * t], scr[self.N_BUFS * t + 1]
            loads += [pltpu.make_async_copy(block(ins[t], 2 * ch + c), mine.at[ch], load_sems.at[t, ch])
                      for ch in range(N_CHIPS)]
            loads.append(pltpu.make_async_copy(ins[n + t], theirs, load_sems.at[t, N_CHIPS]))
        for cp in loads:
            cp.start()
        for cp in loads:
            cp.wait()
        p = self._plan(outs, scr)

        def add_cores(slot):
            for t in range(n):
                mine, theirs, total = scr[self.N_BUFS * t:self.N_BUFS * t + 3]
                total[slot] = (mine[slot].astype(F32) + theirs[slot].astype(F32)).astype(BF16)

        for slot, copies in ((p.far, p.first[1]), (p.via, p.first[0])):
            add_cores(slot)
            for cp in copies:
                cp.start()
        add_cores(p.onto)
        add_cores(p.me)

    def mid(self, ins, outs, scr):
        p = self._plan(outs, scr)
        for cp in p.first[1]:
            cp.wait_recv()
        self._add_inbox(scr, p.onto, 1)
        for cp in p.second:
            cp.start()
        for cp in p.first[0]:
            cp.wait_recv()
        self._add_inbox(scr, p.me, 0)
        for cp in p.keep:
            cp.start()

    def finish(self, ins, outs, scr):
        p = self._plan(outs, scr)
        for cp in p.first[0] + p.first[1]:
            cp.wait_send()
        for cp in p.second + p.keep:
            cp.wait()


class _ToEveryone:
    peers = _EVERYONE

    def __init__(self, scattered=(), gathered=()):
        self.blocks = [_SCATTERED[kind][1] for kind, _ in scattered] + [None] * len(gathered)
        self.args = [g for _, g in scattered] + list(gathered)
        self.out_shape = [jax.ShapeDtypeStruct((N_DEV, *_SCATTERED[kind][0]), BF16) for kind, _ in scattered]
        self.out_shape += [jax.ShapeDtypeStruct((N_DEV, *a.shape), a.dtype) for a in gathered]
        n = len(self.args)
        self.scratch = [pltpu.SemaphoreType.DMA((n, N_DEV - 1)), pltpu.SemaphoreType.DMA((n, N_DEV - 1)),
                        pltpu.SemaphoreType.DMA((n,))]

    def _copies(self, srcs, outs, sems):
        send_sems, recv_sems, local_sems = sems
        me = _dev_index(*_my_place())
        copies = []
        for t, block in enumerate(self.blocks):
            part = (lambda j, t=t, block=block: srcs[t] if block is None else block(srcs[t], j))
            copies.append(pltpu.make_async_copy(part(me), outs[t].at[me], local_sems.at[t]))
            for r in range(1, N_DEV):
                peer = _peer_by_relation(r)
                copies.append(pltpu.make_async_remote_copy(
                    src_ref=part(_dev_index(*peer)), dst_ref=outs[t].at[me], send_sem=send_sems.at[t, r - 1],
                    recv_sem=recv_sems.at[t, r - 1], device_id=peer, device_id_type=MESH))
        return copies

    def start(self, srcs, outs, sems):
        for cp in self._copies(srcs, outs, sems):
            cp.start()

    def finish(self, srcs, outs, sems):
        for cp in self._copies(srcs, outs, sems):
            cp.wait()


class _Jobs:
    def __init__(self, *jobs):
        self.jobs = jobs
        together = {p for j in jobs for p in j.peers}
        self.peers = tuple(p for p in _EVERYONE if p in together)
        self.args = [a for j in jobs for a in j.args]
        self.out_shape = [o for j in jobs for o in j.out_shape]
        self.scratch = [s for j in jobs for s in j.scratch]

    def _split(self, refs, attr):
        at = 0
        for j in self.jobs:
            n = len(getattr(j, attr))
            yield refs[at:at + n]
            at += n

    def _each(self, ins, outs, scr):
        return zip(self.jobs, self._split(ins, "args"), self._split(outs, "out_shape"), self._split(scr, "scratch"))

    def start(self, ins, outs, scr):
        for j, i, o, s in self._each(ins, outs, scr):
            j.start(i, o, s)

    def mid(self, ins, outs, scr):
        for j, i, o, s in self._each(ins, outs, scr):
            if hasattr(j, "mid"):
                j.mid(i, o, s)

    def late(self, ins, outs, scr):
        for j, i, o, s in self._each(ins, outs, scr):
            if hasattr(j, "late"):
                j.late(i, o, s)

    def finish(self, ins, outs, scr):
        for j, i, o, s in self._each(ins, outs, scr):
            j.finish(i, o, s)

    def split_outputs(self, outs):
        return list(self._split(outs, "out_shape"))


def _adamw_math(w, g, m, v):
    m = ADAM_B1 * m + (1.0 - ADAM_B1) * g
    v = ADAM_B2 * v + (1.0 - ADAM_B2) * (g * g)
    m_hat = m / (1.0 - ADAM_B1 ** ADAM_STEP)
    v_hat = v / (1.0 - ADAM_B2 ** ADAM_STEP)
    delta = -ADAM_LR * (m_hat / (jnp.sqrt(v_hat) + ADAM_EPS) + ADAM_WD * w)
    return delta, m, v


def _adamw(name, w, m, v, landings, n_col_blocks=1, job=None):
    n_slots, r, c = landings[0].shape
    grid = (w.shape[0] // r, n_col_blocks)

    def body(w_ref, m_ref, v_ref, *rest):
        l_refs, (g_ref, d_ref, nm_ref, nv_ref) = rest[:len(landings)], rest[len(landings):]
        step = pl.program_id(0) * n_col_blocks + pl.program_id(1)
        for idx, l_ref in enumerate(l_refs):
            @pl.when(step == idx)
            def _(l_ref=l_ref):
                g = l_ref[0].astype(F32)
                for s in range(1, n_slots):
                    g = g + l_ref[s].astype(F32)
                g_ref[...] = g
                d_ref[...], nm_ref[...], nv_ref[...] = _adamw_math(w_ref[...], g, m_ref[...], v_ref[...])

    spec = pl.BlockSpec((r, c), lambda a, b: (a, b))
    return _launch(
        body, name=f"adamw_{name}", grid=grid,
        in_specs=[spec, spec, spec] + [_full_spec((n_slots, r, c))] * len(landings),
        out_specs=[spec] * 4, out_shape=[jax.ShapeDtypeStruct(w.shape, F32)] * 4,
        args=(w, m, v, *landings), vmem=VMEM_BIG, job=job)


_SMALL = (("pre_mix_g", SV_PRE_MIX, 2), ("post_mix_g", SV_POST_MIX, 2), ("pre_ffn_g", SV_PRE_FFN, 2),
          ("post_ffn_g", SV_POST_FFN, 2), ("ple_g", SV_PLE, 2), ("ple_post_g", SV_PLE_POST, 2), ("kv_g", SV_KV, 1),
          ("pool_scale", SV_POOL_SCALE, 1), ("sinks", SV_SINKS, 1))


def _adamw_several(items):
    counts = [len(landings) for _, _, _, landings in items]
    args = [a for w, m, v, landings in items for a in (w, m, v, *landings)]
    out_shape = [jax.ShapeDtypeStruct(w.shape, F32) for w, _, _, _ in items for _ in range(4)]

    def body(*refs):
        ins, outs = refs[:len(args)], refs[len(args):]
        at = 0
        for idx, n_landings in enumerate(counts):
            w_ref, m_ref, v_ref = ins[at:at + 3]
            l_refs = ins[at + 3:at + 3 + n_landings]
            at += 3 + n_landings
            g_ref, d_ref, nm_ref, nv_ref = outs[4 * idx:4 * idx + 4]
            for part, l_ref in enumerate(l_refs):
                rows = slice(part * l_ref.shape[1], (part + 1) * l_ref.shape[1])
                g = l_ref[0].astype(F32)
                for s in range(1, l_ref.shape[0]):
                    g = g + l_ref[s].astype(F32)
                g_ref[rows, :] = g
                d_ref[rows, :], nm_ref[rows, :], nv_ref[rows, :] = _adamw_math(
                    w_ref[rows, :], g, m_ref[rows, :], v_ref[rows, :])

    res, _ = _launch(
        body, name="adamw_several", grid=(1,), in_specs=[_full_spec(a.shape) for a in args],
        out_specs=[_full_spec(s.shape) for s in out_shape], out_shape=out_shape, args=args)
    return [res[4 * idx:4 * idx + 4] for idx in range(len(items))]


def _small_adamw(slabs, params):
    flat = [a for name, _, _ in _SMALL for a in params[name]]
    n_in = 1 + len(flat)

    def body(*refs):
        slabs_ref, wmv = refs[0], refs[1:n_in]
        loss_ref, outs, total = refs[n_in], refs[n_in + 1:-1], refs[-1]
        me = _dev_index(*_my_place())
        g = slabs_ref[0]
        for s in range(1, N_DEV):
            g = g + slabs_ref[s]
        total[...] = g
        loss_ref[...] = total[SV_LOSS:SV_LOSS + 1, 0:1]
        for idx, (name, row, n_rows) in enumerate(_SMALL):
            w_ref, m_ref, v_ref = wmv[3 * idx:3 * idx + 3]
            g_ref, d_ref, nm_ref, nv_ref = outs[4 * idx:4 * idx + 4]
            if name == "pool_scale":
                g = total[row:row + 1, pl.ds(pl.multiple_of(me * 128, 128), 128)]
            else:
                g = total[row:row + n_rows, 0:w_ref.shape[1]]
            g_ref[...] = g
            d_ref[...], nm_ref[...], nv_ref[...] = _adamw_math(w_ref[...], g, m_ref[...], v_ref[...])

    out_shape = [jax.ShapeDtypeStruct((1, 1), F32)]
    for name, _, _ in _SMALL:
        out_shape += [jax.ShapeDtypeStruct(params[name][0].shape, F32)] * 4
    res, _ = _launch(
        body, name="small_adamw", grid=(1,),
        in_specs=[_full_spec(a.shape) for a in (slabs, *flat)], out_specs=[_full_spec(s.shape) for s in out_shape],
        out_shape=out_shape, scratch_shapes=[pltpu.VMEM((SV_ROWS, D_MODEL), F32)], args=(slabs, *flat))
    return res[0], {name: res[1 + 4 * idx:5 + 4 * idx] for idx, (name, _, _) in enumerate(_SMALL)}


def _local_step(x, p, tgt, gains, sinks, shards, weights):
    row = lambda first_row, layer: _Gain(gains, first_row + layer)
    gather = lambda *names: _AllGather(names, shards)
    g_pre_mix, g_post_mix, g_pre_ffn, g_post_ffn = SV_PRE_MIX, SV_POST_MIX, SV_PRE_FFN, SV_POST_FFN
    g_ple, g_ple_post, g_kv = SV_PLE, SV_PLE_POST, _Gain(gains, SV_KV)

    (dpool,), (wp, scale, wgu0, wd0) = _fwd_pool(x, row(g_pre_mix, 0), job=gather("pool", "scale", "gu0", "wd0"))
    wgu0, wd0 = [wgu0], [wd0]
    (x1_0, h2_0, yraw), _ = _fwd_pool_mixer(x, dpool, wp, scale, row(g_post_mix, 0), row(g_pre_ffn, 0))
    (gs0, us0, f0, x2_0, h3_0), (wgate0, wproj0, wkv, wq, wo, wd1_a) = _fwd_ffn(
        0, h2_0, x1_0, wgu0, wd0, row(g_post_ffn, 0), row(g_ple, 0),
        job=gather("gate0", "proj0", "kv", "q", "o", "wdh1_0"))
    (x3_0, z0, pe0, hk, h1, q, kpad, vpad), (wgu1_a,) = _fwd_ple_qkv(
        x2_0, h3_0, p[0], wgate0, wproj0, row(g_ple_post, 0), g_kv, row(g_pre_mix, 1), wkv, wq,
        job=gather("guh1_0"))
    (attn,), (wgu1_b,) = _fwd_attention(q, kpad, vpad, sinks, job=gather("guh1_1"))
    (y1, x1_1, h2_1), (wd1_b,) = _fwd_attn_out(attn, x3_0, wo, row(g_post_mix, 1), row(g_pre_ffn, 1),
                                               job=gather("wdh1_1"))
    wgu1, wd1 = [wgu1_a, wgu1_b], [wd1_a, wd1_b]
    (gs1, us1, f1, x2_1, h3_1), (wgate1, wproj1) = _fwd_ffn(
        1, h2_1, x1_1, wgu1, wd1, row(g_post_ffn, 1), row(g_ple, 1), job=gather("gate1", "proj1"))

    produced, swapped, landed = {}, {}, {}

    def kind_of(name):
        return name.rstrip("0123_")

    def hosted(call, *args, swap=(), spread=(), extra=None):
        jobs = []
        if swap:
            jobs.append(_SiblingSwap([(kind_of(n), produced[n]) for n in swap]))
        if spread:
            jobs.append(_ChipScatter([(kind_of(n), produced[n], swapped[n]) for n in spread]))
        if extra is not None:
            jobs.append(extra)
        jobs = _Jobs(*jobs)
        outs, job_outs = call(*args, job=jobs)
        parts = jobs.split_outputs(job_outs)
        if swap:
            swapped.update(zip(swap, parts.pop(0)))
        if spread:
            landed.update(zip(spread, parts.pop(0)))
        return outs if extra is None else (outs, parts.pop(0))

    ffn_q = lambda layer, qtr: (f"gu{layer}_{qtr}", f"wd{layer}_{qtr}")

    dx2_1, df1, produced["gate1"], produced["proj1"], dg_ple_post1, dg_ple1, dg_post_ffn1, loss = hosted(
        _ple_loss_bwd, 1, x2_1, h3_1, p[1], f1, tgt, wgate1, wproj1, row(g_ple_post, 1), row(g_ple, 1),
        row(g_post_ffn, 1))
    dh2_1, dg1, du1, a1 = hosted(_bwd_ffn_act, 1, df1, gs1, us1, wgu1, wd1, swap=("gate1", "proj1"))
    dgu1, dwd1 = hosted(_bwd_ffn_dw, 1, 0, 1, h2_1, df1, dg1, du1, a1, spread=("gate1", "proj1"))
    produced.update(guA1=dgu1, guB1=dgu1, wdA1=dwd1, wdB1=dwd1)
    dx1_1, dattn, produced["o"], dg_pre_ffn1, dg_post_mix1 = hosted(
        _bwd_attn_out, dx2_1, dh2_1, x1_1, y1, attn, wo, row(g_pre_ffn, 1), row(g_post_mix, 1),
        swap=("guA1", "wdA1", "guB1", "wdB1"))
    dq, dkv, dsinks = hosted(_bwd_attention, q, dattn, kpad, vpad, sinks, spread=("guA1", "wdA1"))
    dx3_0, produced["q"], produced["kv"], dg_pre_mix1, dg_kv = hosted(
        _bwd_qkv, dx1_1, dq, dkv, x3_0, h1, hk, wq, wkv, row(g_pre_mix, 1), g_kv, swap=("o",), spread=("wdB1",))
    dx2_0, df0, produced["gate0"], produced["proj0"], dg_ple_post0, dg_ple0, dg_post_ffn0 = hosted(
        _bwd_ple, 0, dx3_0, x2_0, z0, pe0, h3_0, p[0], f0, wgate0, row(g_ple_post, 0), row(g_ple, 0),
        row(g_post_ffn, 0), swap=("q", "kv"), spread=("guB1",))
    for half, letter in enumerate("AB"):
        landed[f"gu1_{half}"], landed[f"wd1_{half}"] = landed[f"gu{letter}1"], landed[f"wd{letter}1"]
    dh2_0, dg0, du0, a0 = hosted(_bwd_ffn_act, 0, df0, gs0, us0, wgu0, wd0,
                                 swap=("gate0", "proj0"), spread=("o", "q", "kv"))
    part_hosts = [dict(spread=("gate0", "proj0")), dict(swap=ffn_q(0, 0))]
    for part in range(FF_PARTS):
        produced[f"gu0_{part}"], produced[f"wd0_{part}"] = hosted(
            _bwd_ffn_dw, 0, part, FF_PARTS, h2_0, df0, dg0, du0, a0, **part_hosts[part])
    grad_x, produced["pool"], dscale, dg_pre_ffn0, dg_post_mix0, dg_pre_mix0 = hosted(
        _bwd_pool_mixer, dx2_0, dh2_0, x1_0, x, yraw, dpool, wp, scale, row(g_pre_ffn, 0), row(g_post_mix, 0),
        row(g_pre_mix, 0), swap=ffn_q(0, 1), spread=ffn_q(0, 0))

    def update(name, n_col_blocks, pieces):
        w, m, v = weights[name]
        rows = w.size // w.shape[-1]
        flat = [landed[n].reshape(landed[n].shape[0], -1, landed[n].shape[-1]) for n in pieces]
        outs, _ = _adamw(name, w.reshape(rows, -1), m.reshape(rows, -1), v.reshape(rows, -1), flat, n_col_blocks)
        return [o.reshape(w.shape) for o in outs]

    upd = {}
    lanes = lambda a: jnp.pad(a, ((0, 0), (0, D_MODEL - a.shape[1])))
    small = jnp.concatenate([
        dg_pre_mix0, dg_pre_mix1, dg_post_mix0, dg_post_mix1, dg_pre_ffn0, dg_pre_ffn1, dg_post_ffn0, dg_post_ffn1,
        dg_ple0, dg_ple1, dg_ple_post0, dg_ple_post1, dg_kv, dscale, lanes(dsinks[:, :N_HEADS]), lanes(loss)], axis=0)

    everyone = _ToEveryone(scattered=[("pool", produced["pool"])], gathered=[small])
    _, (landed["pool"], slabs) = hosted(_jobs_only, "scatter_tail", spread=ffn_q(0, 1), extra=everyone)
    several = {"w_ple_gate": ("gate0", "gate1"), "w_ple_proj": ("proj0", "proj1"), "w_q": ("q",), "w_kv": ("kv",),
               "w_o": ("o",), "pool_w": ("pool",)}
    flat2d = lambda a: a.reshape(-1, a.shape[-1])
    results = _adamw_several([
        (*map(flat2d, weights[name]),
         [landed[n].reshape(landed[n].shape[0], -1, landed[n].shape[-1]) for n in pieces])
        for name, pieces in several.items()])
    for name, outs in zip(several, results):
        upd[name] = [o.reshape(weights[name][0].shape) for o in outs]
    upd["w_gu"] = update("w_gu", FF_PARTS,
                         pieces=[f"gu{layer}_{qtr}" for layer in range(2) for qtr in range(FF_PARTS)])
    upd["w_gu"] = [jnp.swapaxes(a, 1, 2) for a in upd["w_gu"]]
    upd["w_down"] = update("w_down", FF_PARTS,
                           pieces=[f"wd{layer}_{qtr}" for layer in range(2) for qtr in range(FF_PARTS)])
    return grad_x, upd, slabs


def kernel(x, p, pre_mix_g, post_mix_g, pre_ffn_g, post_ffn_g, pool_w, pool_scale, kv_g, w_kv, w_q, sinks, w_o, w_gu, w_down, ple_g, w_ple_gate, w_ple_proj, ple_post_g, loss_target, m_pre_mix_g, m_post_mix_g, m_pre_ffn_g, m_post_ffn_g, m_pool_w, m_pool_scale, m_kv_g, m_w_kv, m_w_q, m_sinks, m_w_o, m_w_gu, m_w_down, m_ple_g, m_w_ple_gate, m_w_ple_proj, m_ple_post_g, v_pre_mix_g, v_post_mix_g, v_pre_ffn_g, v_post_ffn_g, v_pool_w, v_pool_scale, v_kv_g, v_w_kv, v_w_q, v_sinks, v_w_o, v_w_gu, v_w_down, v_ple_g, v_w_ple_gate, v_w_ple_proj, v_ple_post_g):
    shards = {"pool": pool_w[0].astype(BF16), "scale": pool_scale, "kv": w_kv.astype(BF16),
              "q": w_q[0].astype(BF16), "o": w_o[0].astype(BF16)}
    gu, wd = jnp.swapaxes(w_gu, 1, 2).astype(BF16), w_down.astype(BF16)
    gate, proj = w_ple_gate.astype(BF16), w_ple_proj.astype(BF16)
    for layer in range(2):
        shards[f"gu{layer}"] = (gu, layer, None)
        shards[f"wd{layer}"] = (wd, layer, None)
        for half in range(2):
            cols = (half * D_MODEL // 2, (half + 1) * D_MODEL // 2)
            shards[f"guh{layer}_{half}"] = (gu, layer, cols)
            shards[f"wdh{layer}_{half}"] = (wd, layer, cols)
        shards[f"gate{layer}"] = (gate, layer, None)
        shards[f"proj{layer}"] = (proj, layer, None)
    gains = jnp.concatenate([pre_mix_g, post_mix_g, pre_ffn_g, post_ffn_g, ple_g, ple_post_g, kv_g[None, :]],
                            axis=0).reshape(-1, 1, D_MODEL)
    weights = {"pool_w": (pool_w, m_pool_w, v_pool_w), "w_kv": (w_kv, m_w_kv, v_w_kv), "w_q": (w_q, m_w_q, v_w_q),
               "w_o": (w_o, m_w_o, v_w_o), "w_down": (w_down, m_w_down, v_w_down),
               "w_gu": tuple(jnp.swapaxes(a, 1, 2) for a in (w_gu, m_w_gu, v_w_gu)),
               "w_ple_gate": (w_ple_gate, m_w_ple_gate, v_w_ple_gate),
               "w_ple_proj": (w_ple_proj, m_w_ple_proj, v_w_ple_proj)}
    per_layer = p.reshape(p.shape[0], *p.shape[2:])
    p_rows = [_LayerRows(per_layer, layer) for layer in range(2)]
    grad_x, upd, slabs = _local_step(x[0], p_rows, loss_target[0], gains, sinks, shards, weights)

    small_params = {
        "pre_mix_g": (pre_mix_g, m_pre_mix_g, v_pre_mix_g), "post_mix_g": (post_mix_g, m_post_mix_g, v_post_mix_g),
        "pre_ffn_g": (pre_ffn_g, m_pre_ffn_g, v_pre_ffn_g), "post_ffn_g": (post_ffn_g, m_post_ffn_g, v_post_ffn_g),
        "ple_g": (ple_g, m_ple_g, v_ple_g), "ple_post_g": (ple_post_g, m_ple_post_g, v_ple_post_g),
        "kv_g": (kv_g[None, :], m_kv_g[None, :], v_kv_g[None, :]),
        "pool_scale": (pool_scale, m_pool_scale, v_pool_scale), "sinks": (sinks, m_sinks, v_sinks)}
    loss, small_upd = _small_adamw(slabs, small_params)
    small_upd["kv_g"] = [a[0] for a in small_upd["kv_g"]]
    upd.update(small_upd)

    names = ["pre_mix_g", "post_mix_g", "pre_ffn_g", "post_ffn_g", "pool_w", "pool_scale", "kv_g", "w_kv", "w_q",
             "sinks", "w_o", "w_gu", "w_down", "ple_g", "w_ple_gate", "w_ple_proj", "ple_post_g"]
    outs = [loss[0, 0], grad_x[None]]
    for kind in range(4):
        outs += [upd[n][kind] for n in names]
    return tuple(outs)
```

```python
import functools
import types

import jax
import jax.numpy as jnp
from jax import lax
from jax.experimental import pallas as pl
from jax.experimental.pallas import tpu as pltpu

F32 = jnp.float32
BF16 = jnp.bfloat16

N_DEV = 8
D_MODEL = 1024
N_POOL_GROUPS = 4
POOL_GROUP = 256
POOL_HALO = 16
HEAD_DIM = 64
N_HEADS = 16
N_KV_HEADS = 4
GQA_GROUP = 4
KV_DIM = N_KV_HEADS * HEAD_DIM
ATT_BLOCK = 128
D_FF = 2816
FF_CHUNKS = 4
FF_BLOCK = D_FF // FF_CHUNKS
WD_ROWS = D_FF // N_DEV
FF_PARTS = 2
FF_PART = D_MODEL // FF_PARTS
N_CHIPS = 4
PLE_DIM = 256
EPS = 1e-6
NEG_INF = -1e30
ATT_SCALE = HEAD_DIM ** -0.5

ADAM_LR = 0.001
ADAM_B1 = 0.9
ADAM_B2 = 0.999
ADAM_EPS = 1e-08
ADAM_WD = 0.01
ADAM_STEP = 10

ROW_TILE = 512
FFN_ROW_TILE = 512
FFN_WEIGHT_COLS = 512
FFN_SUB_TILES = 1
VMEM_BIG = 60 * 1024 * 1024
VMEM_MID = 56 * 1024 * 1024
HBM_PIN_ELEMS = 1024

SV_ROWS = 16
SV_PRE_MIX, SV_POST_MIX, SV_PRE_FFN, SV_POST_FFN, SV_PLE, SV_PLE_POST = 0, 2, 4, 6, 8, 10
SV_KV, SV_POOL_SCALE, SV_SINKS, SV_LOSS = 12, 13, 14, 15

MESH = pl.DeviceIdType.MESH
ANY = pl.BlockSpec(memory_space=pl.ANY)


def _dot(a, b):
    return jnp.dot(a, b, preferred_element_type=F32)


def _dot_nt(a, b):
    return lax.dot_general(a, b, (((1,), (1,)), ((), ())), preferred_element_type=F32)


def _dot_tn(a, b):
    return lax.dot_general(a, b, (((0,), (0,)), ((), ())), preferred_element_type=F32)


def _rstd(x):
    return lax.rsqrt(jnp.mean(x * x, axis=-1, keepdims=True) + EPS)


def _rms(x, g):
    return x * _rstd(x) * g


def _rms_bwd(x, g, dy):
    r = _rstd(x)
    n = x * r
    dn = dy * g
    dx = r * (dn - n * jnp.mean(dn * n, axis=-1, keepdims=True))
    dg = jnp.sum(dy * n, axis=0, keepdims=True)
    return dx, dg


def _add_all(terms):
    return functools.reduce(jnp.add, terms)


def _sigmoid(x):
    return 1.0 / (1.0 + jnp.exp(-x))


def _acc(ref, val, first):
    @pl.when(first)
    def _():
        ref[...] = val

    @pl.when(jnp.logical_not(first))
    def _():
        ref[...] += val


def _pool_counts(row0, rows):
    t = row0 + lax.broadcasted_iota(jnp.int32, (rows, D_MODEL), 0) + 1
    grp = lax.broadcasted_iota(jnp.int32, (rows, D_MODEL), 1) // POOL_GROUP
    win = jnp.left_shift(2, grp)
    return jnp.minimum(t, win).astype(F32)


def _window_sums(ext, shift_of):
    outs = []
    s = ext
    for gi in range(N_POOL_GROUPS):
        s = s + pltpu.roll(s, shift_of(1 << gi), axis=0)
        outs.append(s[:, :POOL_GROUP])
        s = s[:, POOL_GROUP:]
    return jnp.concatenate(outs, axis=1)


def _cparams(n_axes, vmem, collective_id=None):
    return pltpu.CompilerParams(dimension_semantics=("arbitrary",) * n_axes, vmem_limit_bytes=vmem,
                                collective_id=collective_id)


_EVERYONE = ("sibling", "x", "y", "far", "x sibling", "y sibling", "far sibling")
_PEER_SETS = (("sibling", "x", "y"), ("sibling",), ("x", "y"), _EVERYONE)


def _meet(peers):
    x, y, c = lax.axis_index("x"), lax.axis_index("y"), lax.axis_index("c")
    device = {"sibling": (x, y, 1 - c), "x": (1 - x, y, c), "y": (x, 1 - y, c), "far": (1 - x, 1 - y, c),
              "x sibling": (1 - x, y, 1 - c), "y sibling": (x, 1 - y, 1 - c), "far sibling": (1 - x, 1 - y, 1 - c)}
    barrier = pltpu.get_barrier_semaphore()
    for peer in peers:
        pl.semaphore_signal(barrier, inc=1, device_id=device[peer], device_id_type=pl.DeviceIdType.MESH)
    pl.semaphore_wait(barrier, len(peers))


def _row_spec(cols, tm=ROW_TILE):
    return pl.BlockSpec((tm, cols), lambda i: (i, 0))


def _full_spec(shape):
    zeros = (0,) * len(shape)
    return pl.BlockSpec(shape, lambda *_: zeros)


def _vec_spec():
    return _full_spec((1, D_MODEL))


def _column_views(parts):
    return [(a, b) for a in parts for b in range(a.shape[-1] // FFN_WEIGHT_COLS)]


def _column_ranges(views):
    return [(n * FFN_WEIGHT_COLS, (n + 1) * FFN_WEIGHT_COLS) for n in range(len(views))]


class _Gain:
    def __init__(self, stacked, layer):
        self.stacked, self.layer = stacked, layer

    def spec(self):
        layer = self.layer
        return pl.BlockSpec((None, 1, D_MODEL), lambda *_: (layer, 0, 0))


class _LayerRows:
    def __init__(self, stacked, layer):
        self.stacked, self.layer = stacked, layer

    def spec(self):
        layer = self.layer
        return pl.BlockSpec((None, ROW_TILE, self.stacked.shape[-1]), lambda i: (layer, i, 0))


def _in_hbm(a):
    return pltpu.with_memory_space_constraint(a, pltpu.HBM) if a.size >= HBM_PIN_ELEMS else a


def _out_in_hbm(s):
    return pltpu.HBM(s.shape, s.dtype) if s.size >= HBM_PIN_ELEMS else s


def _launch(body, *, name, grid, in_specs, out_specs, out_shape, args, scratch_shapes=(), vmem=VMEM_MID, job=None):
    picked = (_Gain, _LayerRows)
    in_specs = [a.spec() if isinstance(a, picked) else s for s, a in zip(in_specs, args)]
    args = [_in_hbm(a.stacked if isinstance(a, picked) else a) for a in args]
    n_in, n_out, n_scr = len(args), len(out_shape), len(scratch_shapes)
    if job is not None and not job.args:
        job = None
    j_args, j_out, j_scr = ([], [], []) if job is None else ([_in_hbm(a) for a in job.args], job.out_shape, job.scratch)

    def run(*refs):
        groups, at = [], 0
        for n in (n_in, len(j_args), n_out, len(j_out), n_scr, len(j_scr)):
            groups.append(refs[at:at + n])
            at += n
        ins, j_ins, outs, j_outs, scr, j_sems = groups

        def begin():
            _meet(job.peers)
            job.start(j_ins, j_outs, j_sems)

        if job is None:
            body(*ins, *outs, *scr)
        elif not grid:
            begin()
            job.mid(j_ins, j_outs, j_sems)
            job.late(j_ins, j_outs, j_sems)
            body(*ins, *outs, *scr)
            job.finish(j_ins, j_outs, j_sems)
        else:
            ids = [pl.program_id(a) for a in range(len(grid))]
            at_start = lambda step: functools.reduce(jnp.logical_and, [ids[0] == step] + [i == 0 for i in ids[1:]])
            last = functools.reduce(jnp.logical_and, [i == g - 1 for i, g in zip(ids, grid)])
            pl.when(at_start(0))(begin)
            pl.when(at_start(grid[0] // 2))(lambda: job.mid(j_ins, j_outs, j_sems))
            pl.when(at_start(3 * grid[0] // 4))(lambda: job.late(j_ins, j_outs, j_sems))
            body(*ins, *outs, *scr)
            pl.when(last)(lambda: job.finish(j_ins, j_outs, j_sems))

    res = pl.pallas_call(
        run, name=name, grid=grid,
        in_specs=list(in_specs) + [ANY] * len(j_args), out_specs=list(out_specs) + [ANY] * len(j_out),
        out_shape=[_out_in_hbm(s) for s in list(out_shape) + list(j_out)],
        scratch_shapes=list(scratch_shapes) + list(j_scr),
        compiler_params=_cparams(len(grid), vmem, None if job is None else _PEER_SETS.index(job.peers)),
    )(*args, *j_args)
    return res[:n_out], res[n_out:]


def _fwd_pool(x, g_pre, job=None):
    T = x.shape[0]
    tm = ROW_TILE
    nt = T // tm

    def body(x_ref, gpre_ref, d_ref, carry):
        i = pl.program_id(0)

        @pl.when(i == 0)
        def _():
            carry[...] = jnp.zeros_like(carry)

        h = _rms(x_ref[...], gpre_ref[...])
        ext = jnp.concatenate([carry[...], h], axis=0)
        carry[...] = h[tm - POOL_HALO:, :]
        sums = _window_sums(ext, lambda k: k)[POOL_HALO:, :]
        d_ref[...] = (sums / _pool_counts(i * tm, tm) - h).astype(BF16)

    return _launch(
        body, name="fwd_pool", grid=(nt,), in_specs=[_row_spec(D_MODEL), _vec_spec()], out_specs=[_row_spec(D_MODEL)],
        out_shape=[jax.ShapeDtypeStruct((T, D_MODEL), BF16)], scratch_shapes=[pltpu.VMEM((POOL_HALO, D_MODEL), F32)],
        args=(x, g_pre), job=job)


def _fwd_pool_mixer(x, d, wp, scale, g_post, g_ffn, job=None):
    T = x.shape[0]
    nt = T // ROW_TILE

    def body(x_ref, d_ref, wp_ref, sc_ref, gpost_ref, gffn_ref, x1_ref, h2_ref, yraw_ref):
        db = d_ref[...]
        yraw = jnp.concatenate(
            [_dot(db[:, g * POOL_GROUP:(g + 1) * POOL_GROUP], wp_ref[g]) for g in range(N_POOL_GROUPS)], axis=1)
        yraw_ref[...] = yraw.astype(BF16)
        x1 = x_ref[...] + _rms(yraw * sc_ref[...], gpost_ref[...])
        x1_ref[...] = x1
        h2_ref[...] = _rms(x1, gffn_ref[...]).astype(BF16)

    return _launch(
        body, name="fwd_pool_mixer", grid=(nt,),
        in_specs=[_row_spec(D_MODEL), _row_spec(D_MODEL), _full_spec((N_POOL_GROUPS, POOL_GROUP, POOL_GROUP)),
                  _vec_spec(), _vec_spec(), _vec_spec()],
        out_specs=[_row_spec(D_MODEL)] * 3,
        out_shape=[jax.ShapeDtypeStruct((T, D_MODEL), F32)] + [jax.ShapeDtypeStruct((T, D_MODEL), BF16)] * 2,
        args=(x, d, wp, scale, g_post, g_ffn), job=job)


def _fwd_ffn(layer, h2, x1, wgu, wd, g_post, g_ple, job=None):
    T = h2.shape[0]
    tm = min(FFN_ROW_TILE, T)
    nt = T // tm
    sub = tm // FFN_SUB_TILES
    last = FF_CHUNKS - 1
    wgu, wd = _column_views(wgu), _column_views(wd)
    n_gu, n_wd = len(wgu), len(wd)
    gu_cols = _column_ranges(wgu)

    def body(h2_ref, x1_ref, *refs):
        wgu_refs, wd_refs = refs[:n_gu], refs[n_gu:n_gu + n_wd]
        gpost_ref, gple_ref, gs_ref, us_ref, f_ref, x2_ref, h3_ref, acc = refs[n_gu + n_wd:]
        k = pl.program_id(0)
        i = pl.program_id(1)
        rows = pl.ds(pl.multiple_of(i * tm, tm), tm)
        parts = []
        for s in range(FFN_SUB_TILES):
            r = pl.ds(s * sub, sub)
            g = _add_all([_dot_nt(h2_ref[r, c0:c1], w[0]) for (c0, c1), w in zip(gu_cols, wgu_refs)])
            u = _add_all([_dot_nt(h2_ref[r, c0:c1], w[1]) for (c0, c1), w in zip(gu_cols, wgu_refs)])
            gs_ref[r, :] = g.astype(BF16)
            us_ref[r, :] = u.astype(BF16)
            a = (g * _sigmoid(g) * u).astype(BF16)
            parts.append(jnp.concatenate([_dot(a, w[...]) for w in wd_refs], axis=1))
        part = jnp.concatenate(parts, axis=0)

        @pl.when(k == 0)
        def _():
            acc[rows, :] = part

        @pl.when(jnp.logical_and(k > 0, k < last))
        def _():
            acc[rows, :] += part

        @pl.when(k == last)
        def _():
            f = acc[rows, :] + part
            f_ref[...] = f.astype(BF16)
            x2 = x1_ref[...] + _rms(f, gpost_ref[...])
            x2_ref[...] = x2
            h3_ref[...] = _rms(x2, gple_ref[...]).astype(BF16)

    def late(k, i):
        return (jnp.where(k == last, i, 0), 0)

    return _launch(
        body, name=f"fwd_ffn{layer}", grid=(FF_CHUNKS, nt),
        in_specs=[pl.BlockSpec((tm, D_MODEL), lambda k, i: (i, 0)), pl.BlockSpec((tm, D_MODEL), late)]
                 + [pl.BlockSpec((None, 2, FF_BLOCK, FFN_WEIGHT_COLS), lambda k, i, b=b: (k, 0, 0, b)) for _, b in wgu]
                 + [pl.BlockSpec((FF_BLOCK, FFN_WEIGHT_COLS), lambda k, i, b=b: (k, b)) for _, b in wd]
                 + [pl.BlockSpec((1, D_MODEL), lambda k, i: (0, 0))] * 2,
        out_specs=[pl.BlockSpec((None, tm, FF_BLOCK), lambda k, i: (k, i, 0)),
                   pl.BlockSpec((None, tm, FF_BLOCK), lambda k, i: (k, i, 0)),
                   pl.BlockSpec((tm, D_MODEL), late),
                   pl.BlockSpec((tm, D_MODEL), late),
                   pl.BlockSpec((tm, D_MODEL), late)],
        out_shape=[jax.ShapeDtypeStruct((FF_CHUNKS, T, FF_BLOCK), BF16),
                   jax.ShapeDtypeStruct((FF_CHUNKS, T, FF_BLOCK), BF16),
                   jax.ShapeDtypeStruct((T, D_MODEL), BF16),
                   jax.ShapeDtypeStruct((T, D_MODEL), F32),
                   jax.ShapeDtypeStruct((T, D_MODEL), BF16)],
        scratch_shapes=[pltpu.VMEM((T, D_MODEL), F32)],
        args=(h2, x1, *[w for w, _ in wgu], *[w for w, _ in wd], g_post, g_ple), vmem=VMEM_BIG, job=job)


def _fwd_ple_qkv(x2, h3, p, wgate, wproj, g_post, g_kv, g_mix, wkv, wq, job=None):
    T = x2.shape[0]
    nt = T // ROW_TILE

    def body(x2_ref, h3_ref, p_ref, wg_ref, wp_ref, gpost_ref, gkv_ref, gmix_ref, wkv_ref, wq_ref,
             x3_ref, z_ref, pe_ref, hk_ref, h1_ref, q_ref, kpad_ref, vpad_ref):
        z = _dot(h3_ref[...], wg_ref[...])
        pe = _dot(p_ref[...].astype(BF16), wp_ref[...])
        z_ref[...] = z.astype(BF16)
        pe_ref[...] = pe.astype(BF16)
        x3 = x2_ref[...] + _rms(pe * _sigmoid(z), gpost_ref[...])
        x3_ref[...] = x3
        r = _rstd(x3)
        hk = (x3 * r * gkv_ref[...]).astype(BF16)
        h1 = (x3 * r * gmix_ref[...]).astype(BF16)
        hk_ref[...] = hk
        h1_ref[...] = h1
        kv = _dot(hk, wkv_ref[...]).astype(BF16)
        q_ref[...] = _dot(h1, wq_ref[...]).astype(BF16)
        i = pl.program_id(0)

        @pl.when(i == 0)
        def _():
            kpad_ref[:ATT_BLOCK, :] = jnp.zeros((ATT_BLOCK, KV_DIM), BF16)
            vpad_ref[:ATT_BLOCK, :] = jnp.zeros((ATT_BLOCK, KV_DIM), BF16)

        rows = pl.ds(pl.multiple_of(ATT_BLOCK + i * ROW_TILE, ATT_BLOCK), ROW_TILE)
        kpad_ref[rows, :] = kv[:, :KV_DIM]
        vpad_ref[rows, :] = kv[:, KV_DIM:]

    wide = jax.ShapeDtypeStruct((T, D_MODEL), BF16)
    padded = (ATT_BLOCK + T, KV_DIM)
    return _launch(
        body, name="fwd_ple_qkv", grid=(nt,),
        in_specs=[_row_spec(D_MODEL), _row_spec(D_MODEL), _row_spec(PLE_DIM), _full_spec((D_MODEL, D_MODEL)),
                  _full_spec((PLE_DIM, D_MODEL)), _vec_spec(), _vec_spec(), _vec_spec(),
                  _full_spec((D_MODEL, 2 * KV_DIM)), _full_spec((D_MODEL, D_MODEL))],
        out_specs=[_row_spec(D_MODEL)] * 6 + [_full_spec(padded)] * 2,
        out_shape=[jax.ShapeDtypeStruct((T, D_MODEL), F32)] + [wide] * 5 + [jax.ShapeDtypeStruct(padded, BF16)] * 2,
        args=(x2, h3, p, wgate, wproj, g_post, g_kv, g_mix, wkv, wq), job=job)


def _alibi_slope(h):
    return 2.0 ** (-8.0 * (h + 1) / N_HEADS)


ATT_SUB = 32
ATT_GROUP_ROWS = GQA_GROUP * ATT_BLOCK


def _att_mask(n, rel_ref, off_ref):
    qi = lax.broadcasted_iota(jnp.int32, (ATT_BLOCK, 2 * ATT_BLOCK), 0)
    si = lax.broadcasted_iota(jnp.int32, (ATT_BLOCK, 2 * ATT_BLOCK), 1)
    rel = ATT_BLOCK + qi - si
    valid = (rel >= 0) & (rel < ATT_BLOCK) & ((si >= ATT_BLOCK) | (n > 0))
    rel_ref[...] = rel.astype(F32)
    off_ref[...] = jnp.where(valid, 0.0, NEG_INF)


def _att_probs(raw, relf, off, slope, sink):
    s = raw * ATT_SCALE - slope * relf + off
    m = jnp.maximum(jnp.max(s, axis=-1, keepdims=True), sink)
    e = jnp.exp(s - m)
    es = jnp.exp(sink - m)
    inv = 1.0 / (jnp.sum(e, axis=-1, keepdims=True) + es)
    return e * inv, es * inv


def _stack_heads(ref, kh):
    first = kh * GQA_GROUP
    return jnp.concatenate([ref[:, (first + g) * HEAD_DIM:(first + g + 1) * HEAD_DIM] for g in range(GQA_GROUP)], axis=0)


def _unstack_heads(stacked):
    return [stacked[g * ATT_BLOCK:(g + 1) * ATT_BLOCK, :] for g in range(GQA_GROUP)]


def _fwd_attention(q, kpad, vpad, sinks, job=None):
    T = q.shape[0]
    nb = T // ATT_BLOCK

    def body(q_ref, k_ref, v_ref, sink_ref, o_ref, s_scr, p_scr, rel_scr, off_scr):
        n = pl.program_id(0)
        start = pl.multiple_of(n * ATT_BLOCK, ATT_BLOCK)
        kw = k_ref[pl.ds(start, 2 * ATT_BLOCK), :]
        vw = v_ref[pl.ds(start, 2 * ATT_BLOCK), :]
        _att_mask(n, rel_scr, off_scr)
        outs = []
        for kh in range(N_KV_HEADS):
            kk = kw[:, kh * HEAD_DIM:(kh + 1) * HEAD_DIM]
            vv = vw[:, kh * HEAD_DIM:(kh + 1) * HEAD_DIM]
            s_scr[...] = _dot_nt(_stack_heads(q_ref, kh), kk)
            for g in range(GQA_GROUP):
                h = kh * GQA_GROUP + g
                for row0 in range(0, ATT_BLOCK, ATT_SUB):
                    rows, sub = pl.ds(g * ATT_BLOCK + row0, ATT_SUB), pl.ds(row0, ATT_SUB)
                    pr, _ = _att_probs(s_scr[rows, :], rel_scr[sub, :], off_scr[sub, :], _alibi_slope(h),
                                       sink_ref[0, h])
                    p_scr[rows, :] = pr.astype(BF16)
            outs += _unstack_heads(_dot(p_scr[...], vv))
        o_ref[...] = jnp.concatenate(outs, axis=1).astype(BF16)

    return _launch(
        body, name="fwd_attention", grid=(nb,),
        in_specs=[_row_spec(D_MODEL, ATT_BLOCK), _full_spec((T + ATT_BLOCK, KV_DIM)), _full_spec((T + ATT_BLOCK, KV_DIM)),
                  pl.BlockSpec(memory_space=pltpu.SMEM)],
        out_specs=[_row_spec(D_MODEL, ATT_BLOCK)],
        out_shape=[jax.ShapeDtypeStruct((T, D_MODEL), BF16)],
        scratch_shapes=[pltpu.VMEM((ATT_GROUP_ROWS, 2 * ATT_BLOCK), F32), pltpu.VMEM((ATT_GROUP_ROWS, 2 * ATT_BLOCK), BF16)]
                       + [pltpu.VMEM((ATT_BLOCK, 2 * ATT_BLOCK), F32)] * 2,
        args=(q, kpad, vpad, sinks), job=job)


def _fwd_attn_out(attn, x, wo, g_post, g_ffn, job=None):
    T = x.shape[0]
    nt = T // ROW_TILE

    def body(a_ref, x_ref, wo_ref, gpost_ref, gffn_ref, y_ref, x1_ref, h2_ref):
        y = _dot(a_ref[...], wo_ref[...])
        y_ref[...] = y.astype(BF16)
        x1 = x_ref[...] + _rms(y, gpost_ref[...])
        x1_ref[...] = x1
        h2_ref[...] = _rms(x1, gffn_ref[...]).astype(BF16)

    return _launch(
        body, name="fwd_attn_out", grid=(nt,),
        in_specs=[_row_spec(D_MODEL), _row_spec(D_MODEL), _full_spec((D_MODEL, D_MODEL)), _vec_spec(), _vec_spec()],
        out_specs=[_row_spec(D_MODEL)] * 3,
        out_shape=[jax.ShapeDtypeStruct((T, D_MODEL), BF16), jax.ShapeDtypeStruct((T, D_MODEL), F32),
                   jax.ShapeDtypeStruct((T, D_MODEL), BF16)],
        args=(attn, x, wo, g_post, g_ffn), job=job)


def _bwd_ple(layer, dx3, x2, z, pe, h3, p, f, wgate, g_ple_post, g_ple, g_post_ffn, job=None):
    T = x2.shape[0]
    tm = ROW_TILE
    nt = T // tm

    def body(dx3_ref, x2_ref, z_ref, pe_ref, h3_ref, p_ref, f_ref, wg_ref, gpp_ref, gp_ref, gpf_ref,
             dx2_ref, df_ref, dwg_ref, dwp_ref, dgpp_ref, dgp_ref, dgpf_ref, acc_g, acc_p):
        i = pl.program_id(0)
        first = i == 0
        dx3v = dx3_ref[...]
        gate = _sigmoid(z_ref[...].astype(F32))
        pev = pe_ref[...].astype(F32)
        de, dgpp = _rms_bwd(pev * gate, gpp_ref[...], dx3v)
        dpe = (de * gate).astype(BF16)
        dz = (de * pev * gate * (1.0 - gate)).astype(BF16)
        _acc(acc_p, _dot_tn(p_ref[...].astype(BF16), dpe), first)
        _acc(acc_g, _dot_tn(h3_ref[...], dz), first)
        dh3 = _dot_nt(dz, wg_ref[...])
        dxn, dgp = _rms_bwd(x2_ref[...], gp_ref[...], dh3)
        dx2 = dx3v + dxn
        dx2_ref[...] = dx2
        df, dgpf = _rms_bwd(f_ref[...].astype(F32), gpf_ref[...], dx2)
        df_ref[...] = df.astype(BF16)
        _acc(dgpp_ref, dgpp, first)
        _acc(dgp_ref, dgp, first)
        _acc(dgpf_ref, dgpf, first)

        @pl.when(i == nt - 1)
        def _():
            dwg_ref[...] = acc_g[...].astype(BF16)
            dwp_ref[...] = acc_p[...].astype(BF16)

    return _launch(
        body, name=f"bwd_ple{layer}", grid=(nt,),
        in_specs=[_row_spec(D_MODEL)] * 5 + [_row_spec(PLE_DIM), _row_spec(D_MODEL), _full_spec((D_MODEL, D_MODEL)),
                  _vec_spec(), _vec_spec(), _vec_spec()],
        out_specs=[_row_spec(D_MODEL), _row_spec(D_MODEL), _full_spec((D_MODEL, D_MODEL)), _full_spec((PLE_DIM, D_MODEL)),
                   _vec_spec(), _vec_spec(), _vec_spec()],
        out_shape=[jax.ShapeDtypeStruct((T, D_MODEL), F32), jax.ShapeDtypeStruct((T, D_MODEL), BF16),
                   jax.ShapeDtypeStruct((D_MODEL, D_MODEL), BF16), jax.ShapeDtypeStruct((PLE_DIM, D_MODEL), BF16)]
                  + [jax.ShapeDtypeStruct((1, D_MODEL), F32)] * 3,
        scratch_shapes=[pltpu.VMEM((D_MODEL, D_MODEL), F32), pltpu.VMEM((PLE_DIM, D_MODEL), F32)],
        args=(dx3, x2, z, pe, h3, p, f, wgate, g_ple_post, g_ple, g_post_ffn), vmem=VMEM_BIG, job=job)


def _ple_loss_bwd(layer, x2, h3, p, f, target, wgate, wproj, g_ple_post, g_ple, g_post_ffn, job=None):
    T = x2.shape[0]
    tm = ROW_TILE
    nt = T // tm

    def body(x2_ref, h3_ref, p_ref, f_ref, tgt_ref, wg_ref, wp_ref, gpp_ref, gp_ref, gpf_ref,
             dx2_ref, df_ref, dwg_ref, dwp_ref, dgpp_ref, dgp_ref, dgpf_ref, loss_ref, acc_g, acc_p):
        i = pl.program_id(0)
        first = i == 0
        h3 = h3_ref[...]
        pb = p_ref[...].astype(BF16)
        x2v = x2_ref[...]
        gate = _sigmoid(_dot(h3, wg_ref[...]))
        pev = _dot(pb, wp_ref[...])
        e = pev * gate
        err = x2v + _rms(e, gpp_ref[...]) - tgt_ref[...]
        _acc(loss_ref, 0.5 * jnp.sum(jnp.mean(err * err, axis=-1, keepdims=True), axis=0, keepdims=True), first)
        dx3v = err * (1.0 / D_MODEL)
        de, dgpp = _rms_bwd(e, gpp_ref[...], dx3v)
        dpe = (de * gate).astype(BF16)
        dz = (de * pev * gate * (1.0 - gate)).astype(BF16)
        _acc(acc_p, _dot_tn(pb, dpe), first)
        _acc(acc_g, _dot_tn(h3, dz), first)
        dxn, dgp = _rms_bwd(x2v, gp_ref[...], _dot_nt(dz, wg_ref[...]))
        dx2 = dx3v + dxn
        dx2_ref[...] = dx2
        df, dgpf = _rms_bwd(f_ref[...].astype(F32), gpf_ref[...], dx2)
        df_ref[...] = df.astype(BF16)
        _acc(dgpp_ref, dgpp, first)
        _acc(dgp_ref, dgp, first)
        _acc(dgpf_ref, dgpf, first)

        @pl.when(i == nt - 1)
        def _():
            dwg_ref[...] = acc_g[...].astype(BF16)
            dwp_ref[...] = acc_p[...].astype(BF16)

    return _launch(
        body, name=f"ple_loss_bwd{layer}", grid=(nt,),
        in_specs=[_row_spec(D_MODEL), _row_spec(D_MODEL), _row_spec(PLE_DIM), _row_spec(D_MODEL), _row_spec(D_MODEL),
                  _full_spec((D_MODEL, D_MODEL)), _full_spec((PLE_DIM, D_MODEL)), _vec_spec(), _vec_spec(), _vec_spec()],
        out_specs=[_row_spec(D_MODEL), _row_spec(D_MODEL), _full_spec((D_MODEL, D_MODEL)), _full_spec((PLE_DIM, D_MODEL)),
                   _vec_spec(), _vec_spec(), _vec_spec(), _full_spec((1, 1))],
        out_shape=[jax.ShapeDtypeStruct((T, D_MODEL), F32), jax.ShapeDtypeStruct((T, D_MODEL), BF16),
                   jax.ShapeDtypeStruct((D_MODEL, D_MODEL), BF16), jax.ShapeDtypeStruct((PLE_DIM, D_MODEL), BF16)]
                  + [jax.ShapeDtypeStruct((1, D_MODEL), F32)] * 3 + [jax.ShapeDtypeStruct((1, 1), F32)],
        scratch_shapes=[pltpu.VMEM((D_MODEL, D_MODEL), F32), pltpu.VMEM((PLE_DIM, D_MODEL), F32)],
        args=(x2, h3, p, f, target, wgate, wproj, g_ple_post, g_ple, g_post_ffn), vmem=VMEM_BIG, job=job)


def _bwd_ffn_act(layer, df, gs, us, wgu, wd, job=None):
    T = df.shape[0]
    tm = min(FFN_ROW_TILE, T)
    nt = T // tm
    sub = tm // FFN_SUB_TILES
    last = FF_CHUNKS - 1
    wgu, wd = _column_views(wgu), _column_views(wd)
    n_gu, n_wd = len(wgu), len(wd)
    wd_cols = _column_ranges(wd)

    def body(df_ref, gs_ref, us_ref, *refs):
        wgu_refs, wd_refs = refs[:n_gu], refs[n_gu:n_gu + n_wd]
        dh_ref, dg_ref, du_ref, a_ref, acc_h = refs[n_gu + n_wd:]
        k = pl.program_id(0)
        i = pl.program_id(1)
        rows = pl.ds(pl.multiple_of(i * tm, tm), tm)
        dhs = []
        for s in range(FFN_SUB_TILES):
            r = pl.ds(s * sub, sub)
            g = gs_ref[r, :].astype(F32)
            u = us_ref[r, :].astype(F32)
            sg = _sigmoid(g)
            silu = g * sg
            a_ref[r, :] = (silu * u).astype(BF16)
            da = _add_all([_dot_nt(df_ref[r, c0:c1], w[...]) for (c0, c1), w in zip(wd_cols, wd_refs)])
            dg = (da * u * (sg * (1.0 + g * (1.0 - sg)))).astype(BF16)
            du = (da * silu).astype(BF16)
            dg_ref[r, :] = dg
            du_ref[r, :] = du
            dhs.append(jnp.concatenate([_dot(dg, w[0]) + _dot(du, w[1]) for w in wgu_refs], axis=1))
        dh = jnp.concatenate(dhs, axis=0)

        @pl.when(k == 0)
        def _():
            acc_h[rows, :] = dh

        @pl.when(jnp.logical_and(k > 0, k < last))
        def _():
            acc_h[rows, :] += dh

        @pl.when(k == last)
        def _():
            dh_ref[...] = acc_h[rows, :] + dh

    chunk_rows = pl.BlockSpec((None, tm, FF_BLOCK), lambda k, i: (k, i, 0))
    saved = jax.ShapeDtypeStruct((FF_CHUNKS, T, FF_BLOCK), BF16)
    return _launch(
        body, name=f"bwd_ffn_act{layer}", grid=(FF_CHUNKS, nt),
        in_specs=[pl.BlockSpec((tm, D_MODEL), lambda k, i: (i, 0)), chunk_rows, chunk_rows]
                 + [pl.BlockSpec((None, 2, FF_BLOCK, FFN_WEIGHT_COLS), lambda k, i, b=b: (k, 0, 0, b)) for _, b in wgu]
                 + [pl.BlockSpec((FF_BLOCK, FFN_WEIGHT_COLS), lambda k, i, b=b: (k, b)) for _, b in wd],
        out_specs=[pl.BlockSpec((tm, D_MODEL), lambda k, i: (jnp.where(k == last, i, 0), 0)),
                   chunk_rows, chunk_rows, chunk_rows],
        out_shape=[jax.ShapeDtypeStruct((T, D_MODEL), F32), saved, saved, saved],
        scratch_shapes=[pltpu.VMEM((T, D_MODEL), F32)],
        args=(df, gs, us, *[w for w, _ in wgu], *[w for w, _ in wd]), vmem=VMEM_BIG, job=job)


def _bwd_ffn_dw(layer, q, parts, h2, df, dg, du, a, job=None):
    T = h2.shape[0]
    width = D_MODEL // parts

    def body(h_ref, df_ref, dg_ref, du_ref, a_ref, dgu_ref, dwd_ref):
        h = h_ref[...]
        dgu_ref[0] = _dot_tn(dg_ref[...], h).astype(BF16)
        dgu_ref[1] = _dot_tn(du_ref[...], h).astype(BF16)
        dwd_ref[...] = _dot_tn(a_ref[...], df_ref[...]).astype(BF16)

    cols = pl.BlockSpec((T, width), lambda k: (0, q))
    chunk = pl.BlockSpec((None, T, FF_BLOCK), lambda k: (k, 0, 0))
    return _launch(
        body, name=f"bwd_ffn_dw{layer}_{q}", grid=(FF_CHUNKS,),
        in_specs=[cols, cols, chunk, chunk, chunk],
        out_specs=[pl.BlockSpec((None, 2, FF_BLOCK, width), lambda k: (k, 0, 0, 0)),
                   pl.BlockSpec((FF_BLOCK, width), lambda k: (k, 0))],
        out_shape=[jax.ShapeDtypeStruct((FF_CHUNKS, 2, FF_BLOCK, width), BF16),
                   jax.ShapeDtypeStruct((D_FF, width), BF16)],
        args=(h2, df, dg, du, a), vmem=VMEM_BIG, job=job)


def _bwd_attn_out(dx2, dh2, x1, y, attn, wo, g_ffn, g_post, job=None):
    T = x1.shape[0]
    nt = T // ROW_TILE

    def body(dx2_ref, dh2_ref, x1_ref, y_ref, a_ref, wo_ref, gffn_ref, gpost_ref,
             dx1_ref, da_ref, dwo_ref, dgf_ref, dgp_ref, acc):
        i = pl.program_id(0)
        first = i == 0
        dxn, dgf = _rms_bwd(x1_ref[...], gffn_ref[...], dh2_ref[...])
        dx1 = dx2_ref[...] + dxn
        dx1_ref[...] = dx1
        dy, dgp = _rms_bwd(y_ref[...].astype(F32), gpost_ref[...], dx1)
        dyb = dy.astype(BF16)
        da_ref[...] = _dot_nt(dyb, wo_ref[...]).astype(BF16)
        _acc(acc, _dot_tn(a_ref[...], dyb), first)
        _acc(dgf_ref, dgf, first)
        _acc(dgp_ref, dgp, first)

        @pl.when(i == nt - 1)
        def _():
            dwo_ref[...] = acc[...].astype(BF16)

    return _launch(
        body, name="bwd_attn_out", grid=(nt,),
        in_specs=[_row_spec(D_MODEL)] * 5 + [_full_spec((D_MODEL, D_MODEL)), _vec_spec(), _vec_spec()],
        out_specs=[_row_spec(D_MODEL), _row_spec(D_MODEL), _full_spec((D_MODEL, D_MODEL)), _vec_spec(), _vec_spec()],
        out_shape=[jax.ShapeDtypeStruct((T, D_MODEL), F32), jax.ShapeDtypeStruct((T, D_MODEL), BF16),
                   jax.ShapeDtypeStruct((D_MODEL, D_MODEL), BF16)] + [jax.ShapeDtypeStruct((1, D_MODEL), F32)] * 2,
        scratch_shapes=[pltpu.VMEM((D_MODEL, D_MODEL), F32)],
        args=(dx2, dh2, x1, y, attn, wo, g_ffn, g_post), job=job)


def _bwd_attention(q, dattn, kpad, vpad, sinks, job=None):
    T = q.shape[0]
    nb = T // ATT_BLOCK

    def body(q_ref, do_ref, k_ref, v_ref, sink_ref, dq_ref, dkv_ref, ds_ref, dk_ref, dv_ref, s_scr, dp_scr, p_scr,
             dsb_scr, rel_scr, off_scr):
        n = pl.program_id(0)
        _att_mask(n, rel_scr, off_scr)

        @pl.when(n == 0)
        def _():
            dk_ref[...] = jnp.zeros_like(dk_ref)
            dv_ref[...] = jnp.zeros_like(dv_ref)
            ds_ref[...] = jnp.zeros_like(ds_ref)

        start = pl.multiple_of(n * ATT_BLOCK, ATT_BLOCK)
        win = pl.ds(start, 2 * ATT_BLOCK)
        kw = k_ref[win, :]
        vw = v_ref[win, :]
        lane = lax.broadcasted_iota(jnp.int32, (1, ATT_BLOCK), 1)
        dsink = jnp.zeros((1, ATT_BLOCK), F32)
        dqs, dks, dvs = [], [], []
        for kh in range(N_KV_HEADS):
            kk = kw[:, kh * HEAD_DIM:(kh + 1) * HEAD_DIM]
            vv = vw[:, kh * HEAD_DIM:(kh + 1) * HEAD_DIM]
            qs = _stack_heads(q_ref, kh)
            dos = _stack_heads(do_ref, kh)
            s_scr[...] = _dot_nt(qs, kk)
            dp_scr[...] = _dot_nt(dos, vv)
            for g in range(GQA_GROUP):
                h = kh * GQA_GROUP + g
                dsink_h = jnp.zeros((1, 1), F32)
                for row0 in range(0, ATT_BLOCK, ATT_SUB):
                    rows, sub = pl.ds(g * ATT_BLOCK + row0, ATT_SUB), pl.ds(row0, ATT_SUB)
                    pr, ps = _att_probs(s_scr[rows, :], rel_scr[sub, :], off_scr[sub, :], _alibi_slope(h),
                                        sink_ref[0, h])
                    dp = dp_scr[rows, :]
                    delta = jnp.sum(pr * dp, axis=-1, keepdims=True)
                    dsb_scr[rows, :] = (pr * (dp - delta) * ATT_SCALE).astype(BF16)
                    p_scr[rows, :] = pr.astype(BF16)
                    dsink_h = dsink_h - jnp.sum(ps * delta, axis=0, keepdims=True)
                dsink = dsink + jnp.where(lane == h, dsink_h, 0.0)
            dsb = dsb_scr[...]
            dqs += _unstack_heads(_dot(dsb, kk))
            dks.append(_dot_tn(dsb, qs))
            dvs.append(_dot_tn(p_scr[...], dos))
        dq_ref[...] = jnp.concatenate(dqs, axis=1).astype(BF16)
        dk_ref[win, :] += jnp.concatenate(dks, axis=1)
        dv_ref[win, :] += jnp.concatenate(dvs, axis=1)
        ds_ref[...] += dsink

        @pl.when(n == nb - 1)
        def _():
            dkv_ref[:, :KV_DIM] = dk_ref[ATT_BLOCK:, :].astype(BF16)
            dkv_ref[:, KV_DIM:] = dv_ref[ATT_BLOCK:, :].astype(BF16)

    return _launch(
        body, name="bwd_attention", grid=(nb,),
        in_specs=[_row_spec(D_MODEL, ATT_BLOCK), _row_spec(D_MODEL, ATT_BLOCK), _full_spec((T + ATT_BLOCK, KV_DIM)),
                  _full_spec((T + ATT_BLOCK, KV_DIM)), pl.BlockSpec(memory_space=pltpu.SMEM)],
        out_specs=[_row_spec(D_MODEL, ATT_BLOCK), _full_spec((T, 2 * KV_DIM)), _full_spec((1, ATT_BLOCK))],
        out_shape=[jax.ShapeDtypeStruct((T, D_MODEL), BF16), jax.ShapeDtypeStruct((T, 2 * KV_DIM), BF16),
                   jax.ShapeDtypeStruct((1, ATT_BLOCK), F32)],
        scratch_shapes=[pltpu.VMEM((T + ATT_BLOCK, KV_DIM), F32)] * 2
                       + [pltpu.VMEM((ATT_GROUP_ROWS, 2 * ATT_BLOCK), F32)] * 2
                       + [pltpu.VMEM((ATT_GROUP_ROWS, 2 * ATT_BLOCK), BF16)] * 2
                       + [pltpu.VMEM((ATT_BLOCK, 2 * ATT_BLOCK), F32)] * 2,
        args=(q, dattn, kpad, vpad, sinks), vmem=VMEM_BIG, job=job)


def _bwd_qkv(dxres, dq, dkv, x3, h1, hk, wq, wkv, g_mix, g_kv, job=None):
    T = x3.shape[0]
    nt = T // ROW_TILE

    def body(dxr_ref, dq_ref, dkv_ref, x_ref, h1_ref, hk_ref, wq_ref, wkv_ref, gmix_ref, gkv_ref,
             dx_ref, dwq_ref, dwkv_ref, dgm_ref, dgk_ref, acc_q, acc_kv):
        i = pl.program_id(0)
        first = i == 0
        dqv = dq_ref[...]
        dkvv = dkv_ref[...]
        xv = x_ref[...]
        d1, dgm = _rms_bwd(xv, gmix_ref[...], _dot_nt(dqv, wq_ref[...]))
        d2, dgk = _rms_bwd(xv, gkv_ref[...], _dot_nt(dkvv, wkv_ref[...]))
        dx_ref[...] = dxr_ref[...] + d1 + d2
        _acc(acc_q, _dot_tn(h1_ref[...], dqv), first)
        _acc(acc_kv, _dot_tn(hk_ref[...], dkvv), first)
        _acc(dgm_ref, dgm, first)
        _acc(dgk_ref, dgk, first)

        @pl.when(i == nt - 1)
        def _():
            dwq_ref[...] = acc_q[...].astype(BF16)
            dwkv_ref[...] = acc_kv[...].astype(BF16)

    return _launch(
        body, name="bwd_qkv", grid=(nt,),
        in_specs=[_row_spec(D_MODEL), _row_spec(D_MODEL), _row_spec(2 * KV_DIM), _row_spec(D_MODEL), _row_spec(D_MODEL),
                  _row_spec(D_MODEL), _full_spec((D_MODEL, D_MODEL)), _full_spec((D_MODEL, 2 * KV_DIM)), _vec_spec(),
                  _vec_spec()],
        out_specs=[_row_spec(D_MODEL), _full_spec((D_MODEL, D_MODEL)), _full_spec((D_MODEL, 2 * KV_DIM)), _vec_spec(),
                   _vec_spec()],
        out_shape=[jax.ShapeDtypeStruct((T, D_MODEL), F32), jax.ShapeDtypeStruct((D_MODEL, D_MODEL), BF16),
                   jax.ShapeDtypeStruct((D_MODEL, 2 * KV_DIM), BF16)] + [jax.ShapeDtypeStruct((1, D_MODEL), F32)] * 2,
        scratch_shapes=[pltpu.VMEM((D_MODEL, D_MODEL), F32), pltpu.VMEM((D_MODEL, 2 * KV_DIM), F32)],
        args=(dxres, dq, dkv, x3, h1, hk, wq, wkv, g_mix, g_kv), job=job)


def _bwd_pool_mixer(dx2, dh2, x1, x, yraw, d, wp, scale, g_ffn, g_post, g_pre, job=None):
    T = x.shape[0]
    tm = ROW_TILE
    nt = T // tm

    def body(dx2_ref, dh2_ref, x1_ref, x_ref, yraw_ref, d_ref, wp_ref, sc_ref, gffn_ref, gpost_ref, gpre_ref,
             dx_ref, dwp_ref, dsc_ref, dgf_ref, dgp_ref, dgm_ref, carry, acc):
        i = pl.program_id(0)
        first = i == 0
        tile = nt - 1 - i

        @pl.when(first)
        def _():
            carry[...] = jnp.zeros_like(carry)

        dxn, dgf = _rms_bwd(x1_ref[...], gffn_ref[...], dh2_ref[...])
        dx1 = dx2_ref[...] + dxn
        yraw = yraw_ref[...].astype(F32)
        sc = sc_ref[...]
        dy, dgp = _rms_bwd(yraw * sc, gpost_ref[...], dx1)
        dsc = jnp.sum(dy * yraw, axis=0, keepdims=True)
        dyb = (dy * sc).astype(BF16)
        dv = d_ref[...]
        dds = []
        for g in range(N_POOL_GROUPS):
            cols = slice(g * POOL_GROUP, (g + 1) * POOL_GROUP)
            dds.append(_dot_nt(dyb[:, cols], wp_ref[g]))
            _acc(acc.at[g], _dot_tn(dv[:, cols], dyb[:, cols]), first)
        dd = jnp.concatenate(dds, axis=1)
        e = dd / _pool_counts(tile * tm, tm)
        ext = jnp.concatenate([e, carry[...]], axis=0)
        carry[...] = e[:POOL_HALO, :]
        sums = _window_sums(ext, lambda k: tm + POOL_HALO - k)[:tm, :]
        dxm, dgm = _rms_bwd(x_ref[...], gpre_ref[...], sums - dd)
        dx_ref[...] = dx1 + dxm
        _acc(dsc_ref, dsc, first)
        _acc(dgf_ref, dgf, first)
        _acc(dgp_ref, dgp, first)
        _acc(dgm_ref, dgm, first)

        @pl.when(i == nt - 1)
        def _():
            dwp_ref[...] = acc[...].astype(BF16)

    rev = pl.BlockSpec((tm, D_MODEL), lambda i: (nt - 1 - i, 0))
    return _launch(
        body, name="bwd_pool_mixer", grid=(nt,),
        in_specs=[rev] * 6 + [_full_spec((N_POOL_GROUPS, POOL_GROUP, POOL_GROUP))] + [_vec_spec()] * 4,
        out_specs=[rev, _full_spec((N_POOL_GROUPS, POOL_GROUP, POOL_GROUP))] + [_vec_spec()] * 4,
        out_shape=[jax.ShapeDtypeStruct((T, D_MODEL), F32),
                   jax.ShapeDtypeStruct((N_POOL_GROUPS, POOL_GROUP, POOL_GROUP), BF16)]
                  + [jax.ShapeDtypeStruct((1, D_MODEL), F32)] * 4,
        scratch_shapes=[pltpu.VMEM((POOL_HALO, D_MODEL), F32), pltpu.VMEM((N_POOL_GROUPS, POOL_GROUP, POOL_GROUP), F32)],
        args=(dx2, dh2, x1, x, yraw, d, wp, scale, g_ffn, g_post, g_pre), job=job)


def _my_place():
    return lax.axis_index("x"), lax.axis_index("y"), lax.axis_index("c")


def _dev_index(px, py, pc):
    return 4 * px + 2 * py + pc


def _peer_by_relation(r):
    x, y, c = _my_place()
    return (x ^ ((r >> 2) & 1), y ^ ((r >> 1) & 1), c ^ (r & 1))


def _slot_pool(ref, j):
    return ref.at[:, pl.ds(pl.multiple_of(j * 32, 32), 32), :]


def _slot_scale(ref, j):
    return ref.at[:, pl.ds(pl.multiple_of(j * 128, 128), 128)]


def _slot_rows128(ref, j):
    return ref.at[pl.ds(pl.multiple_of(j * 128, 128), 128), :]


def _slot_gu(ref, j):
    return ref.at[j % FF_CHUNKS, j // FF_CHUNKS]


def _slot_wd(ref, j):
    return ref.at[pl.ds(pl.multiple_of(j * WD_ROWS, 16), WD_ROWS), :]


def _slot_cols128(ref, j):
    return ref.at[:, pl.ds(pl.multiple_of(j * 128, 128), 128)]


_GATHERED = {
    "pool": ((N_POOL_GROUPS, POOL_GROUP, POOL_GROUP), BF16, _slot_pool),
    "scale": ((1, D_MODEL), F32, _slot_scale),
    "kv": ((D_MODEL, 2 * KV_DIM), BF16, _slot_rows128),
    "q": ((D_MODEL, D_MODEL), BF16, _slot_rows128),
    "o": ((D_MODEL, D_MODEL), BF16, _slot_rows128),
    "gu": ((FF_CHUNKS, 2, FF_BLOCK, D_MODEL), BF16, _slot_gu),
    "wd": ((D_FF, D_MODEL), BF16, _slot_wd),
    "guh": ((FF_CHUNKS, 2, FF_BLOCK, D_MODEL // 2), BF16, _slot_gu),
    "wdh": ((D_FF, D_MODEL // 2), BF16, _slot_wd),
    "gate": ((D_MODEL, D_MODEL), BF16, _slot_rows128),
    "proj": ((PLE_DIM, D_MODEL), BF16, _slot_cols128),
}


def _no_compute():
    pass


class _AllGather:
    peers = ("sibling", "x", "y")

    def __init__(self, names, shards):
        self.kinds = [_GATHERED[n.rstrip("01_")] for n in names]
        entries = [shards[n] if isinstance(shards[n], tuple) else (shards[n], None, None) for n in names]
        self.args = [array for array, _, _ in entries]
        self.layers = [layer for _, layer, _ in entries]
        self.columns = [columns for _, _, columns in entries]
        self.out_shape = [jax.ShapeDtypeStruct(shape, dtype) for shape, dtype, _ in self.kinds]
        n = len(names)
        self.scratch = [pltpu.SemaphoreType.DMA((n, 7)), pltpu.SemaphoreType.DMA((n, 7)), pltpu.SemaphoreType.DMA((n,))]

    def _plan(self, srcs, outs, sems):
        send_sems, recv_sems, local_sems = sems
        x, y, c = _my_place()

        def slot(t, dev):
            return self.kinds[t][2](outs[t], _dev_index(*dev))

        def copy(t, k, block, to, src=None):
            return pltpu.make_async_remote_copy(
                src_ref=slot(t, block) if src is None else src, dst_ref=slot(t, block),
                send_sem=send_sems.at[t, k], recv_sem=recv_sems.at[t, k], device_id=to, device_id_type=MESH)

        return types.SimpleNamespace(
            copy=copy, core=c, me=(x, y, c), sibling=(x, y, 1 - c),
            x_chip=(1 - x, y), y_chip=(x, 1 - y), far_chip=(1 - x, 1 - y),
            via=(x ^ (1 - c), y ^ c),
            onto=(x ^ c, y ^ (1 - c)),
            k_via=1 + c, k_onto=2 - c,
            local=[pltpu.make_async_copy(self._shard(srcs, t), slot(t, (x, y, c)), local_sems.at[t])
                   for t in range(len(srcs))])

    def _shard(self, srcs, t):
        shard = srcs[t] if self.layers[t] is None else srcs[t].at[self.layers[t]]
        if self.columns[t] is None:
            return shard
        first, end = self.columns[t]
        return shard.at[:, first:end]

    def start(self, srcs, outs, sems):
        p = self._plan(srcs, outs, sems)
        for cp in p.local:
            cp.start()
        for t in range(len(srcs)):
            shard = self._shard(srcs, t)
            p.copy(t, 0, p.me, p.sibling, src=shard).start()
            p.copy(t, 1, p.me, (*p.x_chip, p.core), src=shard).start()
            p.copy(t, 2, p.me, (*p.y_chip, p.core), src=shard).start()

    def mid(self, srcs, outs, sems):
        p = self._plan(srcs, outs, sems)
        for t in range(len(srcs)):
            block = (*p.via, p.core)
            p.copy(t, p.k_via, block, p.me).wait_recv()
            p.copy(t, 3, block, (*p.onto, p.core)).start()
            p.copy(t, 3 + p.k_via, block, p.sibling).start()

    def late(self, srcs, outs, sems):
        p = self._plan(srcs, outs, sems)
        n = len(srcs)
        for t in range(n):
            block = (*p.onto, p.core)
            p.copy(t, p.k_onto, block, p.me).wait_recv()
            p.copy(t, 3 + p.k_onto, block, p.sibling).start()
        for t in range(n):
            block = (*p.far_chip, p.core)
            p.copy(t, 3, block, p.me).wait_recv()
            p.copy(t, 6, block, p.sibling).start()

    def finish(self, srcs, outs, sems):
        p = self._plan(srcs, outs, sems)
        n = len(srcs)
        other = 1 - p.core
        for t in range(n):
            p.copy(t, 0, (*p.me[:2], other), p.me).wait_recv()
            for k, chip in ((4, p.x_chip), (5, p.y_chip), (6, p.far_chip)):
                p.copy(t, k, (*chip, other), p.me).wait_recv()
            for k in range(7):
                p.copy(t, k, p.me, p.sibling).wait_send()
        for cp in p.local:
            cp.wait()


def _jobs_only(name, job=None):
    return _launch(_no_compute, name=name, grid=(), in_specs=[], out_specs=[], out_shape=[], args=(), job=job)


def _block_pool(ref, j):
    return ref.at[:, pl.ds(pl.multiple_of(j * 32, 32), 32), :]


def _block_rows128(ref, j):
    return ref.at[pl.ds(pl.multiple_of(j * 128, 128), 128), :]


def _block_gu(ref, j):
    return ref.at[j % FF_CHUNKS, j // FF_CHUNKS]


def _block_wd(ref, j):
    return ref.at[pl.ds(pl.multiple_of(j * WD_ROWS, 16), WD_ROWS), :]


def _block_cols128(ref, j):
    return ref.at[:, pl.ds(pl.multiple_of(j * 128, 128), 128)]


_SCATTERED = {
    "pool": ((N_POOL_GROUPS, 32, POOL_GROUP), _block_pool),
    "kv": ((128, 2 * KV_DIM), _block_rows128),
    "q": ((128, D_MODEL), _block_rows128),
    "o": ((128, D_MODEL), _block_rows128),
    "gu": ((FF_BLOCK, FF_PART), _block_gu),
    "wd": ((WD_ROWS, FF_PART), _block_wd),
    "guA": ((FF_BLOCK, FF_PART), lambda ref, j: _block_gu(ref, j).at[:, :FF_PART]),
    "guB": ((FF_BLOCK, FF_PART), lambda ref, j: _block_gu(ref, j).at[:, FF_PART:]),
    "wdA": ((WD_ROWS, FF_PART), lambda ref, j: _block_wd(ref, j).at[:, :FF_PART]),
    "wdB": ((WD_ROWS, FF_PART), lambda ref, j: _block_wd(ref, j).at[:, FF_PART:]),
    "gate": ((128, D_MODEL), _block_rows128),
    "proj": ((PLE_DIM, 128), _block_cols128),
}


class _SiblingSwap:
    peers = ("sibling",)

    def __init__(self, pieces):
        self.kinds = [_SCATTERED[kind] for kind, _ in pieces]
        self.args = [g for _, g in pieces]
        self.out_shape = [jax.ShapeDtypeStruct((N_CHIPS, *block), BF16) for block, _ in self.kinds]
        n = len(pieces)
        self.scratch = [pltpu.SemaphoreType.DMA((n, N_CHIPS)), pltpu.SemaphoreType.DMA((n, N_CHIPS))]

    def _copies(self, srcs, outs, sems):
        send_sems, recv_sems = sems
        x, y, c = _my_place()
        return [pltpu.make_async_remote_copy(
            src_ref=block(srcs[t], 2 * ch + 1 - c), dst_ref=outs[t].at[ch], send_sem=send_sems.at[t, ch],
            recv_sem=recv_sems.at[t, ch], device_id=(x, y, 1 - c), device_id_type=MESH)
            for t, (_, block) in enumerate(self.kinds) for ch in range(N_CHIPS)]

    def start(self, srcs, outs, sems):
        for cp in self._copies(srcs, outs, sems):
            cp.start()

    def finish(self, srcs, outs, sems):
        for cp in self._copies(srcs, outs, sems):
            cp.wait()


class _ChipScatter:
    N_BUFS = 4
    peers = ("x", "y")

    def __init__(self, pieces):
        self.kinds = [_SCATTERED[kind] for kind, _, _ in pieces]
        self.n = n = len(pieces)
        self.args = [g for _, g, _ in pieces] + [s for _, _, s in pieces]
        self.out_shape = [jax.ShapeDtypeStruct((2, *block), BF16) for block, _ in self.kinds]
        self.scratch = []
        for block, _ in self.kinds:
            self.scratch += [pltpu.VMEM((N_CHIPS, *block), BF16)] * 3 + [pltpu.VMEM((2, *block), BF16)]
        dma = pltpu.SemaphoreType.DMA
        self.scratch += [dma((n, N_CHIPS + 1)), dma((n, 2)), dma((n, 2)), dma((n,)), dma((n,)), dma((n,))]

    def _plan(self, outs, scr):
        n = self.n
        first_send, first_recv, second_send, second_recv, keep_sems = scr[self.N_BUFS * n + 1:]
        x, y, c = _my_place()
        via = (x ^ (1 - c), y ^ c)
        onto = (x ^ c, y ^ (1 - c))
        index = lambda chip: 2 * chip[0] + chip[1]
        first, second, keep = [], [], []
        for t in range(n):
            total, inbox = scr[self.N_BUFS * t + 2], scr[self.N_BUFS * t + 3]
            for k, chip in enumerate((via, (1 - x, 1 - y))):
                first.append(pltpu.make_async_remote_copy(
                    src_ref=total.at[index(chip)], dst_ref=inbox.at[k], send_sem=first_send.at[t, k],
                    recv_sem=first_recv.at[t, k], device_id=(*via, c), device_id_type=MESH))
            second.append(pltpu.make_async_remote_copy(
                src_ref=total.at[index(onto)], dst_ref=outs[t].at[1], send_sem=second_send.at[t],
                recv_sem=second_recv.at[t], device_id=(*onto, c), device_id_type=MESH))
            keep.append(pltpu.make_async_copy(total.at[index((x, y))], outs[t].at[0], keep_sems.at[t]))
        return first, second, keep, index((x, y)), index(onto)

    def start(self, ins, outs, scr):
        n = self.n
        load_sems = scr[self.N_BUFS * n]
        c = lax.axis_index("c")
        loads = []
        for t, (_, block) in enumerate(self.kinds):
            mine, theirs = scr[self.N_BUFS * t], scr[self.N_BUFS * t + 1]
            loads += [pltpu.make_async_copy(block(ins[t], 2 * ch + c), mine.at[ch], load_sems.at[t, ch])
                      for ch in range(N_CHIPS)]
            loads.append(pltpu.make_async_copy(ins[n + t], theirs, load_sems.at[t, N_CHIPS]))
        for cp in loads:
            cp.start()
        for cp in loads:
            cp.wait()
        for t in range(n):
            mine, theirs, total = scr[self.N_BUFS * t:self.N_BUFS * t + 3]
            for ch in range(N_CHIPS):
                total[ch] = (mine[ch].astype(F32) + theirs[ch].astype(F32)).astype(BF16)
        for cp in self._plan(outs, scr)[0]:
            cp.start()

    def mid(self, ins, outs, scr):
        first, second, keep, me, onto = self._plan(outs, scr)
        for cp in first:
            cp.wait_recv()
        for t in range(self.n):
            total, inbox = scr[self.N_BUFS * t + 2], scr[self.N_BUFS * t + 3]
            for k, slot in enumerate((me, onto)):
                total[slot] = (total[slot].astype(F32) + inbox[k].astype(F32)).astype(BF16)
        for cp in second + keep:
            cp.start()

    def finish(self, ins, outs, scr):
        first, second, keep, _, _ = self._plan(outs, scr)
        for cp in first:
            cp.wait_send()
        for cp in second + keep:
            cp.wait()


class _ToEveryone:
    peers = _EVERYONE

    def __init__(self, scattered=(), gathered=()):
        self.blocks = [_SCATTERED[kind][1] for kind, _ in scattered] + [None] * len(gathered)
        self.args = [g for _, g in scattered] + list(gathered)
        self.out_shape = [jax.ShapeDtypeStruct((N_DEV, *_SCATTERED[kind][0]), BF16) for kind, _ in scattered]
        self.out_shape += [jax.ShapeDtypeStruct((N_DEV, *a.shape), a.dtype) for a in gathered]
        n = len(self.args)
        self.scratch = [pltpu.SemaphoreType.DMA((n, N_DEV - 1)), pltpu.SemaphoreType.DMA((n, N_DEV - 1)),
                        pltpu.SemaphoreType.DMA((n,))]

    def _copies(self, srcs, outs, sems):
        send_sems, recv_sems, local_sems = sems
        me = _dev_index(*_my_place())
        copies = []
        for t, block in enumerate(self.blocks):
            part = (lambda j, t=t, block=block: srcs[t] if block is None else block(srcs[t], j))
            copies.append(pltpu.make_async_copy(part(me), outs[t].at[me], local_sems.at[t]))
            for r in range(1, N_DEV):
                peer = _peer_by_relation(r)
                copies.append(pltpu.make_async_remote_copy(
                    src_ref=part(_dev_index(*peer)), dst_ref=outs[t].at[me], send_sem=send_sems.at[t, r - 1],
                    recv_sem=recv_sems.at[t, r - 1], device_id=peer, device_id_type=MESH))
        return copies

    def start(self, srcs, outs, sems):
        for cp in self._copies(srcs, outs, sems):
            cp.start()

    def finish(self, srcs, outs, sems):
        for cp in self._copies(srcs, outs, sems):
            cp.wait()


class _Jobs:
    def __init__(self, *jobs):
        self.jobs = jobs
        together = {p for j in jobs for p in j.peers}
        self.peers = tuple(p for p in _EVERYONE if p in together)
        self.args = [a for j in jobs for a in j.args]
        self.out_shape = [o for j in jobs for o in j.out_shape]
        self.scratch = [s for j in jobs for s in j.scratch]

    def _split(self, refs, attr):
        at = 0
        for j in self.jobs:
            n = len(getattr(j, attr))
            yield refs[at:at + n]
            at += n

    def _each(self, ins, outs, scr):
        return zip(self.jobs, self._split(ins, "args"), self._split(outs, "out_shape"), self._split(scr, "scratch"))

    def start(self, ins, outs, scr):
        for j, i, o, s in self._each(ins, outs, scr):
            j.start(i, o, s)

    def mid(self, ins, outs, scr):
        for j, i, o, s in self._each(ins, outs, scr):
            if hasattr(j, "mid"):
                j.mid(i, o, s)

    def late(self, ins, outs, scr):
        for j, i, o, s in self._each(ins, outs, scr):
            if hasattr(j, "late"):
                j.late(i, o, s)

    def finish(self, ins, outs, scr):
        for j, i, o, s in self._each(ins, outs, scr):
            j.finish(i, o, s)

    def split_outputs(self, outs):
        return list(self._split(outs, "out_shape"))


def _adamw_math(w, g, m, v):
    m = ADAM_B1 * m + (1.0 - ADAM_B1) * g
    v = ADAM_B2 * v + (1.0 - ADAM_B2) * (g * g)
    m_hat = m / (1.0 - ADAM_B1 ** ADAM_STEP)
    v_hat = v / (1.0 - ADAM_B2 ** ADAM_STEP)
    delta = -ADAM_LR * (m_hat / (jnp.sqrt(v_hat) + ADAM_EPS) + ADAM_WD * w)
    return delta, m, v


def _adamw(name, w, m, v, landings, n_col_blocks=1, job=None):
    n_slots, r, c = landings[0].shape
    grid = (w.shape[0] // r, n_col_blocks)

    def body(w_ref, m_ref, v_ref, *rest):
        l_refs, (g_ref, d_ref, nm_ref, nv_ref) = rest[:len(landings)], rest[len(landings):]
        step = pl.program_id(0) * n_col_blocks + pl.program_id(1)
        for idx, l_ref in enumerate(l_refs):
            @pl.when(step == idx)
            def _(l_ref=l_ref):
                g = l_ref[0].astype(F32)
                for s in range(1, n_slots):
                    g = g + l_ref[s].astype(F32)
                g_ref[...] = g
                d_ref[...], nm_ref[...], nv_ref[...] = _adamw_math(w_ref[...], g, m_ref[...], v_ref[...])

    spec = pl.BlockSpec((r, c), lambda a, b: (a, b))
    return _launch(
        body, name=f"adamw_{name}", grid=grid,
        in_specs=[spec, spec, spec] + [_full_spec((n_slots, r, c))] * len(landings),
        out_specs=[spec] * 4, out_shape=[jax.ShapeDtypeStruct(w.shape, F32)] * 4,
        args=(w, m, v, *landings), vmem=VMEM_BIG, job=job)


_SMALL = (("pre_mix_g", SV_PRE_MIX, 2), ("post_mix_g", SV_POST_MIX, 2), ("pre_ffn_g", SV_PRE_FFN, 2),
          ("post_ffn_g", SV_POST_FFN, 2), ("ple_g", SV_PLE, 2), ("ple_post_g", SV_PLE_POST, 2), ("kv_g", SV_KV, 1),
          ("pool_scale", SV_POOL_SCALE, 1), ("sinks", SV_SINKS, 1))


def _adamw_several(items):
    counts = [len(landings) for _, _, _, landings in items]
    args = [a for w, m, v, landings in items for a in (w, m, v, *landings)]
    out_shape = [jax.ShapeDtypeStruct(w.shape, F32) for w, _, _, _ in items for _ in range(4)]

    def body(*refs):
        ins, outs = refs[:len(args)], refs[len(args):]
        at = 0
        for idx, n_landings in enumerate(counts):
            w_ref, m_ref, v_ref = ins[at:at + 3]
            l_refs = ins[at + 3:at + 3 + n_landings]
            at += 3 + n_landings
            g_ref, d_ref, nm_ref, nv_ref = outs[4 * idx:4 * idx + 4]
            for part, l_ref in enumerate(l_refs):
                rows = slice(part * l_ref.shape[1], (part + 1) * l_ref.shape[1])
                g = l_ref[0].astype(F32)
                for s in range(1, l_ref.shape[0]):
                    g = g + l_ref[s].astype(F32)
                g_ref[rows, :] = g
                d_ref[rows, :], nm_ref[rows, :], nv_ref[rows, :] = _adamw_math(
                    w_ref[rows, :], g, m_ref[rows, :], v_ref[rows, :])

    res, _ = _launch(
        body, name="adamw_several", grid=(1,), in_specs=[_full_spec(a.shape) for a in args],
        out_specs=[_full_spec(s.shape) for s in out_shape], out_shape=out_shape, args=args)
    return [res[4 * idx:4 * idx + 4] for idx in range(len(items))]


def _small_adamw(slabs, params):
    flat = [a for name, _, _ in _SMALL for a in params[name]]
    n_in = 1 + len(flat)

    def body(*refs):
        slabs_ref, wmv = refs[0], refs[1:n_in]
        loss_ref, outs, total = refs[n_in], refs[n_in + 1:-1], refs[-1]
        me = _dev_index(*_my_place())
        g = slabs_ref[0]
        for s in range(1, N_DEV):
            g = g + slabs_ref[s]
        total[...] = g
        loss_ref[...] = total[SV_LOSS:SV_LOSS + 1, 0:1]
        for idx, (name, row, n_rows) in enumerate(_SMALL):
            w_ref, m_ref, v_ref = wmv[3 * idx:3 * idx + 3]
            g_ref, d_ref, nm_ref, nv_ref = outs[4 * idx:4 * idx + 4]
            if name == "pool_scale":
                g = total[row:row + 1, pl.ds(pl.multiple_of(me * 128, 128), 128)]
            else:
                g = total[row:row + n_rows, 0:w_ref.shape[1]]
            g_ref[...] = g
            d_ref[...], nm_ref[...], nv_ref[...] = _adamw_math(w_ref[...], g, m_ref[...], v_ref[...])

    out_shape = [jax.ShapeDtypeStruct((1, 1), F32)]
    for name, _, _ in _SMALL:
        out_shape += [jax.ShapeDtypeStruct(params[name][0].shape, F32)] * 4
    res, _ = _launch(
        body, name="small_adamw", grid=(1,),
        in_specs=[_full_spec(a.shape) for a in (slabs, *flat)], out_specs=[_full_spec(s.shape) for s in out_shape],
        out_shape=out_shape, scratch_shapes=[pltpu.VMEM((SV_ROWS, D_MODEL), F32)], args=(slabs, *flat))
    return res[0], {name: res[1 + 4 * idx:5 + 4 * idx] for idx, (name, _, _) in enumerate(_SMALL)}


def _local_step(x, p, tgt, gains, sinks, shards, weights):
    row = lambda first_row, layer: _Gain(gains, first_row + layer)
    gather = lambda *names: _AllGather(names, shards)
    g_pre_mix, g_post_mix, g_pre_ffn, g_post_ffn = SV_PRE_MIX, SV_POST_MIX, SV_PRE_FFN, SV_POST_FFN
    g_ple, g_ple_post, g_kv = SV_PLE, SV_PLE_POST, _Gain(gains, SV_KV)

    (dpool,), (wp, scale, wgu0, wd0) = _fwd_pool(x, row(g_pre_mix, 0), job=gather("pool", "scale", "gu0", "wd0"))
    wgu0, wd0 = [wgu0], [wd0]
    (x1_0, h2_0, yraw), _ = _fwd_pool_mixer(x, dpool, wp, scale, row(g_post_mix, 0), row(g_pre_ffn, 0))
    (gs0, us0, f0, x2_0, h3_0), (wgate0, wproj0, wkv, wq, wo, wd1_a) = _fwd_ffn(
        0, h2_0, x1_0, wgu0, wd0, row(g_post_ffn, 0), row(g_ple, 0),
        job=gather("gate0", "proj0", "kv", "q", "o", "wdh1_0"))
    (x3_0, z0, pe0, hk, h1, q, kpad, vpad), (wgu1_a,) = _fwd_ple_qkv(
        x2_0, h3_0, p[0], wgate0, wproj0, row(g_ple_post, 0), g_kv, row(g_pre_mix, 1), wkv, wq,
        job=gather("guh1_0"))
    (attn,), (wgu1_b,) = _fwd_attention(q, kpad, vpad, sinks, job=gather("guh1_1"))
    (y1, x1_1, h2_1), (wd1_b,) = _fwd_attn_out(attn, x3_0, wo, row(g_post_mix, 1), row(g_pre_ffn, 1),
                                               job=gather("wdh1_1"))
    wgu1, wd1 = [wgu1_a, wgu1_b], [wd1_a, wd1_b]
    (gs1, us1, f1, x2_1, h3_1), (wgate1, wproj1) = _fwd_ffn(
        1, h2_1, x1_1, wgu1, wd1, row(g_post_ffn, 1), row(g_ple, 1), job=gather("gate1", "proj1"))

    produced, swapped, landed = {}, {}, {}

    def kind_of(name):
        return name.rstrip("0123_")

    def hosted(call, *args, swap=(), spread=(), extra=None):
        jobs = []
        if swap:
            jobs.append(_SiblingSwap([(kind_of(n), produced[n]) for n in swap]))
        if spread:
            jobs.append(_ChipScatter([(kind_of(n), produced[n], swapped[n]) for n in spread]))
        if extra is not None:
            jobs.append(extra)
        jobs = _Jobs(*jobs)
        outs, job_outs = call(*args, job=jobs)
        parts = jobs.split_outputs(job_outs)
        if swap:
            swapped.update(zip(swap, parts.pop(0)))
        if spread:
            landed.update(zip(spread, parts.pop(0)))
        return outs if extra is None else (outs, parts.pop(0))

    ffn_q = lambda layer, qtr: (f"gu{layer}_{qtr}", f"wd{layer}_{qtr}")

    dx2_1, df1, produced["gate1"], produced["proj1"], dg_ple_post1, dg_ple1, dg_post_ffn1, loss = hosted(
        _ple_loss_bwd, 1, x2_1, h3_1, p[1], f1, tgt, wgate1, wproj1, row(g_ple_post, 1), row(g_ple, 1),
        row(g_post_ffn, 1))
    dh2_1, dg1, du1, a1 = hosted(_bwd_ffn_act, 1, df1, gs1, us1, wgu1, wd1, swap=("gate1", "proj1"))
    dgu1, dwd1 = hosted(_bwd_ffn_dw, 1, 0, 1, h2_1, df1, dg1, du1, a1, spread=("gate1", "proj1"))
    produced.update(guA1=dgu1, guB1=dgu1, wdA1=dwd1, wdB1=dwd1)
    dx1_1, dattn, produced["o"], dg_pre_ffn1, dg_post_mix1 = hosted(
        _bwd_attn_out, dx2_1, dh2_1, x1_1, y1, attn, wo, row(g_pre_ffn, 1), row(g_post_mix, 1),
        swap=("guA1", "wdA1", "guB1", "wdB1"))
    dq, dkv, dsinks = hosted(_bwd_attention, q, dattn, kpad, vpad, sinks, spread=("guA1", "wdA1"))
    dx3_0, produced["q"], produced["kv"], dg_pre_mix1, dg_kv = hosted(
        _bwd_qkv, dx1_1, dq, dkv, x3_0, h1, hk, wq, wkv, row(g_pre_mix, 1), g_kv, swap=("o",), spread=("wdB1",))
    dx2_0, df0, produced["gate0"], produced["proj0"], dg_ple_post0, dg_ple0, dg_post_ffn0 = hosted(
        _bwd_ple, 0, dx3_0, x2_0, z0, pe0, h3_0, p[0], f0, wgate0, row(g_ple_post, 0), row(g_ple, 0),
        row(g_post_ffn, 0), swap=("q", "kv"), spread=("guB1",))
    for half, letter in enumerate("AB"):
        landed[f"gu1_{half}"], landed[f"wd1_{half}"] = landed[f"gu{letter}1"], landed[f"wd{letter}1"]
    dh2_0, dg0, du0, a0 = hosted(_bwd_ffn_act, 0, df0, gs0, us0, wgu0, wd0,
                                 swap=("gate0", "proj0"))
    part_hosts = [dict(spread=("o", "q", "kv", "gate0", "proj0")), dict(swap=ffn_q(0, 0))]
    for part in range(FF_PARTS):
        produced[f"gu0_{part}"], produced[f"wd0_{part}"] = hosted(
            _bwd_ffn_dw, 0, part, FF_PARTS, h2_0, df0, dg0, du0, a0, **part_hosts[part])
    grad_x, produced["pool"], dscale, dg_pre_ffn0, dg_post_mix0, dg_pre_mix0 = hosted(
        _bwd_pool_mixer, dx2_0, dh2_0, x1_0, x, yraw, dpool, wp, scale, row(g_pre_ffn, 0), row(g_post_mix, 0),
        row(g_pre_mix, 0), swap=ffn_q(0, 1), spread=ffn_q(0, 0))

    def update(name, n_col_blocks, pieces):
        w, m, v = weights[name]
        rows = w.size // w.shape[-1]
        flat = [landed[n].reshape(landed[n].shape[0], -1, landed[n].shape[-1]) for n in pieces]
        outs, _ = _adamw(name, w.reshape(rows, -1), m.reshape(rows, -1), v.reshape(rows, -1), flat, n_col_blocks)
        return [o.reshape(w.shape) for o in outs]

    upd = {}
    lanes = lambda a: jnp.pad(a, ((0, 0), (0, D_MODEL - a.shape[1])))
    small = jnp.concatenate([
        dg_pre_mix0, dg_pre_mix1, dg_post_mix0, dg_post_mix1, dg_pre_ffn0, dg_pre_ffn1, dg_post_ffn0, dg_post_ffn1,
        dg_ple0, dg_ple1, dg_ple_post0, dg_ple_post1, dg_kv, dscale, lanes(dsinks[:, :N_HEADS]), lanes(loss)], axis=0)

    everyone = _ToEveryone(scattered=[("pool", produced["pool"])], gathered=[small])
    _, (landed["pool"], slabs) = hosted(_jobs_only, "scatter_tail", spread=ffn_q(0, 1), extra=everyone)
    several = {"w_ple_gate": ("gate0", "gate1"), "w_ple_proj": ("proj0", "proj1"), "w_q": ("q",), "w_kv": ("kv",),
               "w_o": ("o",), "pool_w": ("pool",)}
    flat2d = lambda a: a.reshape(-1, a.shape[-1])
    results = _adamw_several([
        (*map(flat2d, weights[name]),
         [landed[n].reshape(landed[n].shape[0], -1, landed[n].shape[-1]) for n in pieces])
        for name, pieces in several.items()])
    for name, outs in zip(several, results):
        upd[name] = [o.reshape(weights[name][0].shape) for o in outs]
    upd["w_gu"] = update("w_gu", FF_PARTS,
                         pieces=[f"gu{layer}_{qtr}" for layer in range(2) for qtr in range(FF_PARTS)])
    upd["w_gu"] = [jnp.swapaxes(a, 1, 2) for a in upd["w_gu"]]
    upd["w_down"] = update("w_down", FF_PARTS,
                           pieces=[f"wd{layer}_{qtr}" for layer in range(2) for qtr in range(FF_PARTS)])
    return grad_x, upd, slabs


def kernel(x, p, pre_mix_g, post_mix_g, pre_ffn_g, post_ffn_g, pool_w, pool_scale, kv_g, w_kv, w_q, sinks, w_o, w_gu, w_down, ple_g, w_ple_gate, w_ple_proj, ple_post_g, loss_target, m_pre_mix_g, m_post_mix_g, m_pre_ffn_g, m_post_ffn_g, m_pool_w, m_pool_scale, m_kv_g, m_w_kv, m_w_q, m_sinks, m_w_o, m_w_gu, m_w_down, m_ple_g, m_w_ple_gate, m_w_ple_proj, m_ple_post_g, v_pre_mix_g, v_post_mix_g, v_pre_ffn_g, v_post_ffn_g, v_pool_w, v_pool_scale, v_kv_g, v_w_kv, v_w_q, v_sinks, v_w_o, v_w_gu, v_w_down, v_ple_g, v_w_ple_gate, v_w_ple_proj, v_ple_post_g):
    shards = {"pool": pool_w[0].astype(BF16), "scale": pool_scale, "kv": w_kv.astype(BF16),
              "q": w_q[0].astype(BF16), "o": w_o[0].astype(BF16)}
    gu, wd = jnp.swapaxes(w_gu, 1, 2).astype(BF16), w_down.astype(BF16)
    gate, proj = w_ple_gate.astype(BF16), w_ple_proj.astype(BF16)
    for layer in range(2):
        shards[f"gu{layer}"] = (gu, layer, None)
        shards[f"wd{layer}"] = (wd, layer, None)
        for half in range(2):
            cols = (half * D_MODEL // 2, (half + 1) * D_MODEL // 2)
            shards[f"guh{layer}_{half}"] = (gu, layer, cols)
            shards[f"wdh{layer}_{half}"] = (wd, layer, cols)
        shards[f"gate{layer}"] = (gate, layer, None)
        shards[f"proj{layer}"] = (proj, layer, None)
    gains = jnp.concatenate([pre_mix_g, post_mix_g, pre_ffn_g, post_ffn_g, ple_g, ple_post_g, kv_g[None, :]],
                            axis=0).reshape(-1, 1, D_MODEL)
    weights = {"pool_w": (pool_w, m_pool_w, v_pool_w), "w_kv": (w_kv, m_w_kv, v_w_kv), "w_q": (w_q, m_w_q, v_w_q),
               "w_o": (w_o, m_w_o, v_w_o), "w_down": (w_down, m_w_down, v_w_down),
               "w_gu": tuple(jnp.swapaxes(a, 1, 2) for a in (w_gu, m_w_gu, v_w_gu)),
               "w_ple_gate": (w_ple_gate, m_w_ple_gate, v_w_ple_gate),
               "w_ple_proj": (w_ple_proj, m_w_ple_proj, v_w_ple_proj)}
    per_layer = p.reshape(p.shape[0], *p.shape[2:])
    p_rows = [_LayerRows(per_layer, layer) for layer in range(2)]
    grad_x, upd, slabs = _local_step(x[0], p_rows, loss_target[0], gains, sinks, shards, weights)

    small_params = {
        "pre_mix_g": (pre_mix_g, m_pre_mix_g, v_pre_mix_g), "post_mix_g": (post_mix_g, m_post_mix_g, v_post_mix_g),
        "pre_ffn_g": (pre_ffn_g, m_pre_ffn_g, v_pre_ffn_g), "post_ffn_g": (post_ffn_g, m_post_ffn_g, v_post_ffn_g),
        "ple_g": (ple_g, m_ple_g, v_ple_g), "ple_post_g": (ple_post_g, m_ple_post_g, v_ple_post_g),
        "kv_g": (kv_g[None, :], m_kv_g[None, :], v_kv_g[None, :]),
        "pool_scale": (pool_scale, m_pool_scale, v_pool_scale), "sinks": (sinks, m_sinks, v_sinks)}
    loss, small_upd = _small_adamw(slabs, small_params)
    small_upd["kv_g"] = [a[0] for a in small_upd["kv_g"]]
    upd.update(small_upd)

    names = ["pre_mix_g", "post_mix_g", "pre_ffn_g", "post_ffn_g", "pool_w", "pool_scale", "kv_g", "w_kv", "w_q",
             "sinks", "w_o", "w_gu", "w_down", "ple_g", "w_ple_gate", "w_ple_proj", "ple_post_g"]
    outs = [loss[0, 0], grad_x[None]]
    for kind in range(4):
        outs += [upd[n][kind] for n in names]
    return tuple(outs)
```

```python
import functools
import types

import jax
import jax.numpy as jnp
from jax import lax
from jax.experimental import pallas as pl
from jax.experimental.pallas import tpu as pltpu

F32 = jnp.float32
BF16 = jnp.bfloat16

N_DEV = 8
D_MODEL = 1024
N_POOL_GROUPS = 4
POOL_GROUP = 256
POOL_HALO = 16
HEAD_DIM = 64
N_HEADS = 16
N_KV_HEADS = 4
GQA_GROUP = 4
KV_DIM = N_KV_HEADS * HEAD_DIM
ATT_BLOCK = 128
D_FF = 2816
FF_CHUNKS = 4
FF_BLOCK = D_FF // FF_CHUNKS
WD_ROWS = D_FF // N_DEV
FF_PARTS = 2
FF_PART = D_MODEL // FF_PARTS
N_CHIPS = 4
PLE_DIM = 256
EPS = 1e-6
NEG_INF = -1e30
ATT_SCALE = HEAD_DIM ** -0.5

ADAM_LR = 0.001
ADAM_B1 = 0.9
ADAM_B2 = 0.999
ADAM_EPS = 1e-08
ADAM_WD = 0.01
ADAM_STEP = 10

ROW_TILE = 512
FFN_ROW_TILE = 512
FFN_WEIGHT_COLS = 512
FFN_SUB_TILES = 1
VMEM_BIG = 60 * 1024 * 1024
VMEM_MID = 56 * 1024 * 1024
HBM_PIN_ELEMS = 1024

SV_ROWS = 16
SV_PRE_MIX, SV_POST_MIX, SV_PRE_FFN, SV_POST_FFN, SV_PLE, SV_PLE_POST = 0, 2, 4, 6, 8, 10
SV_KV, SV_POOL_SCALE, SV_SINKS, SV_LOSS = 12, 13, 14, 15

MESH = pl.DeviceIdType.MESH
ANY = pl.BlockSpec(memory_space=pl.ANY)


def _dot(a, b):
    return jnp.dot(a, b, preferred_element_type=F32)


def _dot_nt(a, b):
    return lax.dot_general(a, b, (((1,), (1,)), ((), ())), preferred_element_type=F32)


def _dot_tn(a, b):
    return lax.dot_general(a, b, (((0,), (0,)), ((), ())), preferred_element_type=F32)


def _rstd(x):
    return lax.rsqrt(jnp.mean(x * x, axis=-1, keepdims=True) + EPS)


def _rms(x, g):
    return x * _rstd(x) * g


def _rms_bwd(x, g, dy):
    r = _rstd(x)
    n = x * r
    dn = dy * g
    dx = r * (dn - n * jnp.mean(dn * n, axis=-1, keepdims=True))
    dg = jnp.sum(dy * n, axis=0, keepdims=True)
    return dx, dg


def _add_all(terms):
    return functools.reduce(jnp.add, terms)


def _sigmoid(x):
    return 1.0 / (1.0 + jnp.exp(-x))


def _acc(ref, val, first):
    @pl.when(first)
    def _():
        ref[...] = val

    @pl.when(jnp.logical_not(first))
    def _():
        ref[...] += val


def _pool_counts(row0, rows):
    t = row0 + lax.broadcasted_iota(jnp.int32, (rows, D_MODEL), 0) + 1
    grp = lax.broadcasted_iota(jnp.int32, (rows, D_MODEL), 1) // POOL_GROUP
    win = jnp.left_shift(2, grp)
    return jnp.minimum(t, win).astype(F32)


def _window_sums(ext, shift_of):
    outs = []
    s = ext
    for gi in range(N_POOL_GROUPS):
        s = s + pltpu.roll(s, shift_of(1 << gi), axis=0)
        outs.append(s[:, :POOL_GROUP])
        s = s[:, POOL_GROUP:]
    return jnp.concatenate(outs, axis=1)


def _cparams(n_axes, vmem, collective_id=None):
    return pltpu.CompilerParams(dimension_semantics=("arbitrary",) * n_axes, vmem_limit_bytes=vmem,
                                collective_id=collective_id)


_EVERYONE = ("sibling", "x", "y", "far", "x sibling", "y sibling", "far sibling")
_PEER_SETS = (("sibling", "x", "y"), ("sibling",), ("x", "y"), _EVERYONE)


def _meet(peers):
    x, y, c = lax.axis_index("x"), lax.axis_index("y"), lax.axis_index("c")
    device = {"sibling": (x, y, 1 - c), "x": (1 - x, y, c), "y": (x, 1 - y, c), "far": (1 - x, 1 - y, c),
              "x sibling": (1 - x, y, 1 - c), "y sibling": (x, 1 - y, 1 - c), "far sibling": (1 - x, 1 - y, 1 - c)}
    barrier = pltpu.get_barrier_semaphore()
    for peer in peers:
        pl.semaphore_signal(barrier, inc=1, device_id=device[peer], device_id_type=pl.DeviceIdType.MESH)
    pl.semaphore_wait(barrier, len(peers))


def _row_spec(cols, tm=ROW_TILE):
    return pl.BlockSpec((tm, cols), lambda i: (i, 0))


def _full_spec(shape):
    zeros = (0,) * len(shape)
    return pl.BlockSpec(shape, lambda *_: zeros)


def _vec_spec():
    return _full_spec((1, D_MODEL))


def _column_views(parts):
    return [(a, b) for a in parts for b in range(a.shape[-1] // FFN_WEIGHT_COLS)]


def _column_ranges(views):
    return [(n * FFN_WEIGHT_COLS, (n + 1) * FFN_WEIGHT_COLS) for n in range(len(views))]


class _Gain:
    def __init__(self, stacked, layer):
        self.stacked, self.layer = stacked, layer

    def spec(self):
        layer = self.layer
        return pl.BlockSpec((None, 1, D_MODEL), lambda *_: (layer, 0, 0))


class _LayerRows:
    def __init__(self, stacked, layer):
        self.stacked, self.layer = stacked, layer

    def spec(self):
        layer = self.layer
        return pl.BlockSpec((None, ROW_TILE, self.stacked.shape[-1]), lambda i: (layer, i, 0))


def _in_hbm(a):
    return pltpu.with_memory_space_constraint(a, pltpu.HBM) if a.size >= HBM_PIN_ELEMS else a


def _out_in_hbm(s):
    return pltpu.HBM(s.shape, s.dtype) if s.size >= HBM_PIN_ELEMS else s


def _launch(body, *, name, grid, in_specs, out_specs, out_shape, args, scratch_shapes=(), vmem=VMEM_MID, job=None):
    picked = (_Gain, _LayerRows)
    in_specs = [a.spec() if isinstance(a, picked) else s for s, a in zip(in_specs, args)]
    args = [_in_hbm(a.stacked if isinstance(a, picked) else a) for a in args]
    n_in, n_out, n_scr = len(args), len(out_shape), len(scratch_shapes)
    if job is not None and not job.args:
        job = None
    j_args, j_out, j_scr = ([], [], []) if job is None else ([_in_hbm(a) for a in job.args], job.out_shape, job.scratch)

    def run(*refs):
        groups, at = [], 0
        for n in (n_in, len(j_args), n_out, len(j_out), n_scr, len(j_scr)):
            groups.append(refs[at:at + n])
            at += n
        ins, j_ins, outs, j_outs, scr, j_sems = groups

        def begin():
            _meet(job.peers)
            job.start(j_ins, j_outs, j_sems)

        if job is None:
            body(*ins, *outs, *scr)
        elif not grid:
            begin()
            job.mid(j_ins, j_outs, j_sems)
            job.late(j_ins, j_outs, j_sems)
            body(*ins, *outs, *scr)
            job.finish(j_ins, j_outs, j_sems)
        else:
            ids = [pl.program_id(a) for a in range(len(grid))]
            at_start = lambda step: functools.reduce(jnp.logical_and, [ids[0] == step] + [i == 0 for i in ids[1:]])
            last = functools.reduce(jnp.logical_and, [i == g - 1 for i, g in zip(ids, grid)])
            pl.when(at_start(0))(begin)
            pl.when(at_start(grid[0] // 2))(lambda: job.mid(j_ins, j_outs, j_sems))
            pl.when(at_start(3 * grid[0] // 4))(lambda: job.late(j_ins, j_outs, j_sems))
            body(*ins, *outs, *scr)
            pl.when(last)(lambda: job.finish(j_ins, j_outs, j_sems))

    res = pl.pallas_call(
        run, name=name, grid=grid,
        in_specs=list(in_specs) + [ANY] * len(j_args), out_specs=list(out_specs) + [ANY] * len(j_out),
        out_shape=[_out_in_hbm(s) for s in list(out_shape) + list(j_out)],
        scratch_shapes=list(scratch_shapes) + list(j_scr),
        compiler_params=_cparams(len(grid), vmem, None if job is None else _PEER_SETS.index(job.peers)),
    )(*args, *j_args)
    return res[:n_out], res[n_out:]


def _fwd_pool(x, g_pre, job=None):
    T = x.shape[0]
    tm = ROW_TILE
    nt = T // tm

    def body(x_ref, gpre_ref, d_ref, carry):
        i = pl.program_id(0)

        @pl.when(i == 0)
        def _():
            carry[...] = jnp.zeros_like(carry)

        h = _rms(x_ref[...], gpre_ref[...])
        ext = jnp.concatenate([carry[...], h], axis=0)
        carry[...] = h[tm - POOL_HALO:, :]
        sums = _window_sums(ext, lambda k: k)[POOL_HALO:, :]
        d_ref[...] = (sums / _pool_counts(i * tm, tm) - h).astype(BF16)

    return _launch(
        body, name="fwd_pool", grid=(nt,), in_specs=[_row_spec(D_MODEL), _vec_spec()], out_specs=[_row_spec(D_MODEL)],
        out_shape=[jax.ShapeDtypeStruct((T, D_MODEL), BF16)], scratch_shapes=[pltpu.VMEM((POOL_HALO, D_MODEL), F32)],
        args=(x, g_pre), job=job)


def _fwd_pool_mixer(x, d, wp, scale, g_post, g_ffn, job=None):
    T = x.shape[0]
    nt = T // ROW_TILE

    def body(x_ref, d_ref, wp_ref, sc_ref, gpost_ref, gffn_ref, x1_ref, h2_ref, yraw_ref):
        db = d_ref[...]
        yraw = jnp.concatenate(
            [_dot(db[:, g * POOL_GROUP:(g + 1) * POOL_GROUP], wp_ref[g]) for g in range(N_POOL_GROUPS)], axis=1)
        yraw_ref[...] = yraw.astype(BF16)
        x1 = x_ref[...] + _rms(yraw * sc_ref[...], gpost_ref[...])
        x1_ref[...] = x1
        h2_ref[...] = _rms(x1, gffn_ref[...]).astype(BF16)

    return _launch(
        body, name="fwd_pool_mixer", grid=(nt,),
        in_specs=[_row_spec(D_MODEL), _row_spec(D_MODEL), _full_spec((N_POOL_GROUPS, POOL_GROUP, POOL_GROUP)),
                  _vec_spec(), _vec_spec(), _vec_spec()],
        out_specs=[_row_spec(D_MODEL)] * 3,
        out_shape=[jax.ShapeDtypeStruct((T, D_MODEL), F32)] + [jax.ShapeDtypeStruct((T, D_MODEL), BF16)] * 2,
        args=(x, d, wp, scale, g_post, g_ffn), job=job)


def _fwd_ffn(layer, h2, x1, wgu, wd, g_post, g_ple, job=None):
    T = h2.shape[0]
    tm = min(FFN_ROW_TILE, T)
    nt = T // tm
    sub = tm // FFN_SUB_TILES
    last = FF_CHUNKS - 1
    wgu, wd = _column_views(wgu), _column_views(wd)
    n_gu, n_wd = len(wgu), len(wd)
    gu_cols = _column_ranges(wgu)

    def body(h2_ref, x1_ref, *refs):
        wgu_refs, wd_refs = refs[:n_gu], refs[n_gu:n_gu + n_wd]
        gpost_ref, gple_ref, gs_ref, us_ref, f_ref, x2_ref, h3_ref, acc = refs[n_gu + n_wd:]
        k = pl.program_id(0)
        i = pl.program_id(1)
        rows = pl.ds(pl.multiple_of(i * tm, tm), tm)
        parts = []
        for s in range(FFN_SUB_TILES):
            r = pl.ds(s * sub, sub)
            g = _add_all([_dot_nt(h2_ref[r, c0:c1], w[0]) for (c0, c1), w in zip(gu_cols, wgu_refs)])
            u = _add_all([_dot_nt(h2_ref[r, c0:c1], w[1]) for (c0, c1), w in zip(gu_cols, wgu_refs)])
            gs_ref[r, :] = g.astype(BF16)
            us_ref[r, :] = u.astype(BF16)
            a = (g * _sigmoid(g) * u).astype(BF16)
            parts.append(jnp.concatenate([_dot(a, w[...]) for w in wd_refs], axis=1))
        part = jnp.concatenate(parts, axis=0)

        @pl.when(k == 0)
        def _():
            acc[rows, :] = part

        @pl.when(jnp.logical_and(k > 0, k < last))
        def _():
            acc[rows, :] += part

        @pl.when(k == last)
        def _():
            f = acc[rows, :] + part
            f_ref[...] = f.astype(BF16)
            x2 = x1_ref[...] + _rms(f, gpost_ref[...])
            x2_ref[...] = x2
            h3_ref[...] = _rms(x2, gple_ref[...]).astype(BF16)

    def late(k, i):
        return (jnp.where(k == last, i, 0), 0)

    return _launch(
        body, name=f"fwd_ffn{layer}", grid=(FF_CHUNKS, nt),
        in_specs=[pl.BlockSpec((tm, D_MODEL), lambda k, i: (i, 0)), pl.BlockSpec((tm, D_MODEL), late)]
                 + [pl.BlockSpec((None, 2, FF_BLOCK, FFN_WEIGHT_COLS), lambda k, i, b=b: (k, 0, 0, b)) for _, b in wgu]
                 + [pl.BlockSpec((FF_BLOCK, FFN_WEIGHT_COLS), lambda k, i, b=b: (k, b)) for _, b in wd]
                 + [pl.BlockSpec((1, D_MODEL), lambda k, i: (0, 0))] * 2,
        out_specs=[pl.BlockSpec((None, tm, FF_BLOCK), lambda k, i: (k, i, 0)),
                   pl.BlockSpec((None, tm, FF_BLOCK), lambda k, i: (k, i, 0)),
                   pl.BlockSpec((tm, D_MODEL), late),
                   pl.BlockSpec((tm, D_MODEL), late),
                   pl.BlockSpec((tm, D_MODEL), late)],
        out_shape=[jax.ShapeDtypeStruct((FF_CHUNKS, T, FF_BLOCK), BF16),
                   jax.ShapeDtypeStruct((FF_CHUNKS, T, FF_BLOCK), BF16),
                   jax.ShapeDtypeStruct((T, D_MODEL), BF16),
                   jax.ShapeDtypeStruct((T, D_MODEL), F32),
                   jax.ShapeDtypeStruct((T, D_MODEL), BF16)],
        scratch_shapes=[pltpu.VMEM((T, D_MODEL), F32)],
        args=(h2, x1, *[w for w, _ in wgu], *[w for w, _ in wd], g_post, g_ple), vmem=VMEM_BIG, job=job)


def _fwd_ple_qkv(x2, h3, p, wgate, wproj, g_post, g_kv, g_mix, wkv, wq, job=None):
    T = x2.shape[0]
    nt = T // ROW_TILE

    def body(x2_ref, h3_ref, p_ref, wg_ref, wp_ref, gpost_ref, gkv_ref, gmix_ref, wkv_ref, wq_ref,
             x3_ref, z_ref, pe_ref, hk_ref, h1_ref, q_ref, kpad_ref, vpad_ref):
        z = _dot(h3_ref[...], wg_ref[...])
        pe = _dot(p_ref[...].astype(BF16), wp_ref[...])
        z_ref[...] = z.astype(BF16)
        pe_ref[...] = pe.astype(BF16)
        x3 = x2_ref[...] + _rms(pe * _sigmoid(z), gpost_ref[...])
        x3_ref[...] = x3
        r = _rstd(x3)
        hk = (x3 * r * gkv_ref[...]).astype(BF16)
        h1 = (x3 * r * gmix_ref[...]).astype(BF16)
        hk_ref[...] = hk
        h1_ref[...] = h1
        kv = _dot(hk, wkv_ref[...]).astype(BF16)
        q_ref[...] = _dot(h1, wq_ref[...]).astype(BF16)
        i = pl.program_id(0)

        @pl.when(i == 0)
        def _():
            kpad_ref[:ATT_BLOCK, :] = jnp.zeros((ATT_BLOCK, KV_DIM), BF16)
            vpad_ref[:ATT_BLOCK, :] = jnp.zeros((ATT_BLOCK, KV_DIM), BF16)

        rows = pl.ds(pl.multiple_of(ATT_BLOCK + i * ROW_TILE, ATT_BLOCK), ROW_TILE)
        kpad_ref[rows, :] = kv[:, :KV_DIM]
        vpad_ref[rows, :] = kv[:, KV_DIM:]

    wide = jax.ShapeDtypeStruct((T, D_MODEL), BF16)
    padded = (ATT_BLOCK + T, KV_DIM)
    return _launch(
        body, name="fwd_ple_qkv", grid=(nt,),
        in_specs=[_row_spec(D_MODEL), _row_spec(D_MODEL), _row_spec(PLE_DIM), _full_spec((D_MODEL, D_MODEL)),
                  _full_spec((PLE_DIM, D_MODEL)), _vec_spec(), _vec_spec(), _vec_spec(),
                  _full_spec((D_MODEL, 2 * KV_DIM)), _full_spec((D_MODEL, D_MODEL))],
        out_specs=[_row_spec(D_MODEL)] * 6 + [_full_spec(padded)] * 2,
        out_shape=[jax.ShapeDtypeStruct((T, D_MODEL), F32)] + [wide] * 5 + [jax.ShapeDtypeStruct(padded, BF16)] * 2,
        args=(x2, h3, p, wgate, wproj, g_post, g_kv, g_mix, wkv, wq), job=job)


def _alibi_slope(h):
    return 2.0 ** (-8.0 * (h + 1) / N_HEADS)


ATT_SUB = 32
ATT_GROUP_ROWS = GQA_GROUP * ATT_BLOCK


def _att_mask(n, rel_ref, off_ref):
    qi = lax.broadcasted_iota(jnp.int32, (ATT_BLOCK, 2 * ATT_BLOCK), 0)
    si = lax.broadcasted_iota(jnp.int32, (ATT_BLOCK, 2 * ATT_BLOCK), 1)
    rel = ATT_BLOCK + qi - si
    valid = (rel >= 0) & (rel < ATT_BLOCK) & ((si >= ATT_BLOCK) | (n > 0))
    rel_ref[...] = rel.astype(F32)
    off_ref[...] = jnp.where(valid, 0.0, NEG_INF)


def _att_probs(raw, relf, off, slope, sink):
    s = raw * ATT_SCALE - slope * relf + off
    m = jnp.maximum(jnp.max(s, axis=-1, keepdims=True), sink)
    e = jnp.exp(s - m)
    es = jnp.exp(sink - m)
    inv = 1.0 / (jnp.sum(e, axis=-1, keepdims=True) + es)
    return e * inv, es * inv


def _stack_heads(ref, kh):
    first = kh * GQA_GROUP
    return jnp.concatenate([ref[:, (first + g) * HEAD_DIM:(first + g + 1) * HEAD_DIM] for g in range(GQA_GROUP)], axis=0)


def _unstack_heads(stacked):
    return [stacked[g * ATT_BLOCK:(g + 1) * ATT_BLOCK, :] for g in range(GQA_GROUP)]


def _fwd_attention(q, kpad, vpad, sinks, job=None):
    T = q.shape[0]
    nb = T // ATT_BLOCK

    def body(q_ref, k_ref, v_ref, sink_ref, o_ref, s_scr, p_scr, rel_scr, off_scr):
        n = pl.program_id(0)
        start = pl.multiple_of(n * ATT_BLOCK, ATT_BLOCK)
        kw = k_ref[pl.ds(start, 2 * ATT_BLOCK), :]
        vw = v_ref[pl.ds(start, 2 * ATT_BLOCK), :]
        _att_mask(n, rel_scr, off_scr)
        outs = []
        for kh in range(N_KV_HEADS):
            kk = kw[:, kh * HEAD_DIM:(kh + 1) * HEAD_DIM]
            vv = vw[:, kh * HEAD_DIM:(kh + 1) * HEAD_DIM]
            s_scr[...] = _dot_nt(_stack_heads(q_ref, kh), kk)
            for g in range(GQA_GROUP):
                h = kh * GQA_GROUP + g
                for row0 in range(0, ATT_BLOCK, ATT_SUB):
                    rows, sub = pl.ds(g * ATT_BLOCK + row0, ATT_SUB), pl.ds(row0, ATT_SUB)
                    pr, _ = _att_probs(s_scr[rows, :], rel_scr[sub, :], off_scr[sub, :], _alibi_slope(h),
                                       sink_ref[0, h])
                    p_scr[rows, :] = pr.astype(BF16)
            outs += _unstack_heads(_dot(p_scr[...], vv))
        o_ref[...] = jnp.concatenate(outs, axis=1).astype(BF16)

    return _launch(
        body, name="fwd_attention", grid=(nb,),
        in_specs=[_row_spec(D_MODEL, ATT_BLOCK), _full_spec((T + ATT_BLOCK, KV_DIM)), _full_spec((T + ATT_BLOCK, KV_DIM)),
                  pl.BlockSpec(memory_space=pltpu.SMEM)],
        out_specs=[_row_spec(D_MODEL, ATT_BLOCK)],
        out_shape=[jax.ShapeDtypeStruct((T, D_MODEL), BF16)],
        scratch_shapes=[pltpu.VMEM((ATT_GROUP_ROWS, 2 * ATT_BLOCK), F32), pltpu.VMEM((ATT_GROUP_ROWS, 2 * ATT_BLOCK), BF16)]
                       + [pltpu.VMEM((ATT_BLOCK, 2 * ATT_BLOCK), F32)] * 2,
        args=(q, kpad, vpad, sinks), job=job)


def _fwd_attn_out(attn, x, wo, g_post, g_ffn, job=None):
    T = x.shape[0]
    nt = T // ROW_TILE

    def body(a_ref, x_ref, wo_ref, gpost_ref, gffn_ref, y_ref, x1_ref, h2_ref):
        y = _dot(a_ref[...], wo_ref[...])
        y_ref[...] = y.astype(BF16)
        x1 = x_ref[...] + _rms(y, gpost_ref[...])
        x1_ref[...] = x1
        h2_ref[...] = _rms(x1, gffn_ref[...]).astype(BF16)

    return _launch(
        body, name="fwd_attn_out", grid=(nt,),
        in_specs=[_row_spec(D_MODEL), _row_spec(D_MODEL), _full_spec((D_MODEL, D_MODEL)), _vec_spec(), _vec_spec()],
        out_specs=[_row_spec(D_MODEL)] * 3,
        out_shape=[jax.ShapeDtypeStruct((T, D_MODEL), BF16), jax.ShapeDtypeStruct((T, D_MODEL), F32),
                   jax.ShapeDtypeStruct((T, D_MODEL), BF16)],
        args=(attn, x, wo, g_post, g_ffn), job=job)


def _bwd_ple(layer, dx3, x2, z, pe, h3, p, f, wgate, g_ple_post, g_ple, g_post_ffn, job=None):
    T = x2.shape[0]
    tm = ROW_TILE
    nt = T // tm

    def body(dx3_ref, x2_ref, z_ref, pe_ref, h3_ref, p_ref, f_ref, wg_ref, gpp_ref, gp_ref, gpf_ref,
             dx2_ref, df_ref, dwg_ref, dwp_ref, dgpp_ref, dgp_ref, dgpf_ref, acc_g, acc_p):
        i = pl.program_id(0)
        first = i == 0
        dx3v = dx3_ref[...]
        gate = _sigmoid(z_ref[...].astype(F32))
        pev = pe_ref[...].astype(F32)
        de, dgpp = _rms_bwd(pev * gate, gpp_ref[...], dx3v)
        dpe = (de * gate).astype(BF16)
        dz = (de * pev * gate * (1.0 - gate)).astype(BF16)
        _acc(acc_p, _dot_tn(p_ref[...].astype(BF16), dpe), first)
        _acc(acc_g, _dot_tn(h3_ref[...], dz), first)
        dh3 = _dot_nt(dz, wg_ref[...])
        dxn, dgp = _rms_bwd(x2_ref[...], gp_ref[...], dh3)
        dx2 = dx3v + dxn
        dx2_ref[...] = dx2
        df, dgpf = _rms_bwd(f_ref[...].astype(F32), gpf_ref[...], dx2)
        df_ref[...] = df.astype(BF16)
        _acc(dgpp_ref, dgpp, first)
        _acc(dgp_ref, dgp, first)
        _acc(dgpf_ref, dgpf, first)

        @pl.when(i == nt - 1)
        def _():
            dwg_ref[...] = acc_g[...].astype(BF16)
            dwp_ref[...] = acc_p[...].astype(BF16)

    return _launch(
        body, name=f"bwd_ple{layer}", grid=(nt,),
        in_specs=[_row_spec(D_MODEL)] * 5 + [_row_spec(PLE_DIM), _row_spec(D_MODEL), _full_spec((D_MODEL, D_MODEL)),
                  _vec_spec(), _vec_spec(), _vec_spec()],
        out_specs=[_row_spec(D_MODEL), _row_spec(D_MODEL), _full_spec((D_MODEL, D_MODEL)), _full_spec((PLE_DIM, D_MODEL)),
                   _vec_spec(), _vec_spec(), _vec_spec()],
        out_shape=[jax.ShapeDtypeStruct((T, D_MODEL), F32), jax.ShapeDtypeStruct((T, D_MODEL), BF16),
                   jax.ShapeDtypeStruct((D_MODEL, D_MODEL), BF16), jax.ShapeDtypeStruct((PLE_DIM, D_MODEL), BF16)]
                  + [jax.ShapeDtypeStruct((1, D_MODEL), F32)] * 3,
        scratch_shapes=[pltpu.VMEM((D_MODEL, D_MODEL), F32), pltpu.VMEM((PLE_DIM, D_MODEL), F32)],
        args=(dx3, x2, z, pe, h3, p, f, wgate, g_ple_post, g_ple, g_post_ffn), vmem=VMEM_BIG, job=job)


def _ple_loss_bwd(layer, x2, h3, p, f, target, wgate, wproj, g_ple_post, g_ple, g_post_ffn, job=None):
    T = x2.shape[0]
    tm = ROW_TILE
    nt = T // tm

    def body(x2_ref, h3_ref, p_ref, f_ref, tgt_ref, wg_ref, wp_ref, gpp_ref, gp_ref, gpf_ref,
             dx2_ref, df_ref, dwg_ref, dwp_ref, dgpp_ref, dgp_ref, dgpf_ref, loss_ref, acc_g, acc_p):
        i = pl.program_id(0)
        first = i == 0
        h3 = h3_ref[...]
        pb = p_ref[...].astype(BF16)
        x2v = x2_ref[...]
        gate = _sigmoid(_dot(h3, wg_ref[...]))
        pev = _dot(pb, wp_ref[...])
        e = pev * gate
        err = x2v + _rms(e, gpp_ref[...]) - tgt_ref[...]
        _acc(loss_ref, 0.5 * jnp.sum(jnp.mean(err * err, axis=-1, keepdims=True), axis=0, keepdims=True), first)
        dx3v = err * (1.0 / D_MODEL)
        de, dgpp = _rms_bwd(e, gpp_ref[...], dx3v)
        dpe = (de * gate).astype(BF16)
        dz = (de * pev * gate * (1.0 - gate)).astype(BF16)
        _acc(acc_p, _dot_tn(pb, dpe), first)
        _acc(acc_g, _dot_tn(h3, dz), first)
        dxn, dgp = _rms_bwd(x2v, gp_ref[...], _dot_nt(dz, wg_ref[...]))
        dx2 = dx3v + dxn
        dx2_ref[...] = dx2
        df, dgpf = _rms_bwd(f_ref[...].astype(F32), gpf_ref[...], dx2)
        df_ref[...] = df.astype(BF16)
        _acc(dgpp_ref, dgpp, first)
        _acc(dgp_ref, dgp, first)
        _acc(dgpf_ref, dgpf, first)

        @pl.when(i == nt - 1)
        def _():
            dwg_ref[...] = acc_g[...].astype(BF16)
            dwp_ref[...] = acc_p[...].astype(BF16)

    return _launch(
        body, name=f"ple_loss_bwd{layer}", grid=(nt,),
        in_specs=[_row_spec(D_MODEL), _row_spec(D_MODEL), _row_spec(PLE_DIM), _row_spec(D_MODEL), _row_spec(D_MODEL),
                  _full_spec((D_MODEL, D_MODEL)), _full_spec((PLE_DIM, D_MODEL)), _vec_spec(), _vec_spec(), _vec_spec()],
        out_specs=[_row_spec(D_MODEL), _row_spec(D_MODEL), _full_spec((D_MODEL, D_MODEL)), _full_spec((PLE_DIM, D_MODEL)),
                   _vec_spec(), _vec_spec(), _vec_spec(), _full_spec((1, 1))],
        out_shape=[jax.ShapeDtypeStruct((T, D_MODEL), F32), jax.ShapeDtypeStruct((T, D_MODEL), BF16),
                   jax.ShapeDtypeStruct((D_MODEL, D_MODEL), BF16), jax.ShapeDtypeStruct((PLE_DIM, D_MODEL), BF16)]
                  + [jax.ShapeDtypeStruct((1, D_MODEL), F32)] * 3 + [jax.ShapeDtypeStruct((1, 1), F32)],
        scratch_shapes=[pltpu.VMEM((D_MODEL, D_MODEL), F32), pltpu.VMEM((PLE_DIM, D_MODEL), F32)],
        args=(x2, h3, p, f, target, wgate, wproj, g_ple_post, g_ple, g_post_ffn), vmem=VMEM_BIG, job=job)


def _bwd_ffn_act(layer, df, gs, us, wgu, wd, job=None):
    T = df.shape[0]
    tm = min(FFN_ROW_TILE, T)
    nt = T // tm
    sub = tm // FFN_SUB_TILES
    last = FF_CHUNKS - 1
    wgu, wd = _column_views(wgu), _column_views(wd)
    n_gu, n_wd = len(wgu), len(wd)
    wd_cols = _column_ranges(wd)

    def body(df_ref, gs_ref, us_ref, *refs):
        wgu_refs, wd_refs = refs[:n_gu], refs[n_gu:n_gu + n_wd]
        dh_ref, dg_ref, du_ref, a_ref, acc_h = refs[n_gu + n_wd:]
        k = pl.program_id(0)
        i = pl.program_id(1)
        rows = pl.ds(pl.multiple_of(i * tm, tm), tm)
        dhs = []
        for s in range(FFN_SUB_TILES):
            r = pl.ds(s * sub, sub)
            g = gs_ref[r, :].astype(F32)
            u = us_ref[r, :].astype(F32)
            sg = _sigmoid(g)
            silu = g * sg
            a_ref[r, :] = (silu * u).astype(BF16)
            da = _add_all([_dot_nt(df_ref[r, c0:c1], w[...]) for (c0, c1), w in zip(wd_cols, wd_refs)])
            dg = (da * u * (sg * (1.0 + g * (1.0 - sg)))).astype(BF16)
            du = (da * silu).astype(BF16)
            dg_ref[r, :] = dg
            du_ref[r, :] = du
            dhs.append(jnp.concatenate([_dot(dg, w[0]) + _dot(du, w[1]) for w in wgu_refs], axis=1))
        dh = jnp.concatenate(dhs, axis=0)

        @pl.when(k == 0)
        def _():
            acc_h[rows, :] = dh

        @pl.when(jnp.logical_and(k > 0, k < last))
        def _():
            acc_h[rows, :] += dh

        @pl.when(k == last)
        def _():
            dh_ref[...] = acc_h[rows, :] + dh

    chunk_rows = pl.BlockSpec((None, tm, FF_BLOCK), lambda k, i: (k, i, 0))
    saved = jax.ShapeDtypeStruct((FF_CHUNKS, T, FF_BLOCK), BF16)
    return _launch(
        body, name=f"bwd_ffn_act{layer}", grid=(FF_CHUNKS, nt),
        in_specs=[pl.BlockSpec((tm, D_MODEL), lambda k, i: (i, 0)), chunk_rows, chunk_rows]
                 + [pl.BlockSpec((None, 2, FF_BLOCK, FFN_WEIGHT_COLS), lambda k, i, b=b: (k, 0, 0, b)) for _, b in wgu]
                 + [pl.BlockSpec((FF_BLOCK, FFN_WEIGHT_COLS), lambda k, i, b=b: (k, b)) for _, b in wd],
        out_specs=[pl.BlockSpec((tm, D_MODEL), lambda k, i: (jnp.where(k == last, i, 0), 0)),
                   chunk_rows, chunk_rows, chunk_rows],
        out_shape=[jax.ShapeDtypeStruct((T, D_MODEL), F32), saved, saved, saved],
        scratch_shapes=[pltpu.VMEM((T, D_MODEL), F32)],
        args=(df, gs, us, *[w for w, _ in wgu], *[w for w, _ in wd]), vmem=VMEM_BIG, job=job)


def _bwd_ffn_dw(layer, q, parts, h2, df, dg, du, a, job=None):
    T = h2.shape[0]
    width = D_MODEL // parts

    def body(h_ref, df_ref, dg_ref, du_ref, a_ref, dgu_ref, dwd_ref):
        h = h_ref[...]
        dgu_ref[0] = _dot_tn(dg_ref[...], h).astype(BF16)
        dgu_ref[1] = _dot_tn(du_ref[...], h).astype(BF16)
        dwd_ref[...] = _dot_tn(a_ref[...], df_ref[...]).astype(BF16)

    cols = pl.BlockSpec((T, width), lambda k: (0, q))
    chunk = pl.BlockSpec((None, T, FF_BLOCK), lambda k: (k, 0, 0))
    return _launch(
        body, name=f"bwd_ffn_dw{layer}_{q}", grid=(FF_CHUNKS,),
        in_specs=[cols, cols, chunk, chunk, chunk],
        out_specs=[pl.BlockSpec((None, 2, FF_BLOCK, width), lambda k: (k, 0, 0, 0)),
                   pl.BlockSpec((FF_BLOCK, width), lambda k: (k, 0))],
        out_shape=[jax.ShapeDtypeStruct((FF_CHUNKS, 2, FF_BLOCK, width), BF16),
                   jax.ShapeDtypeStruct((D_FF, width), BF16)],
        args=(h2, df, dg, du, a), vmem=VMEM_BIG, job=job)


def _bwd_attn_out(dx2, dh2, x1, y, attn, wo, g_ffn, g_post, job=None):
    T = x1.shape[0]
    nt = T // ROW_TILE

    def body(dx2_ref, dh2_ref, x1_ref, y_ref, a_ref, wo_ref, gffn_ref, gpost_ref,
             dx1_ref, da_ref, dwo_ref, dgf_ref, dgp_ref, acc):
        i = pl.program_id(0)
        first = i == 0
        dxn, dgf = _rms_bwd(x1_ref[...], gffn_ref[...], dh2_ref[...])
        dx1 = dx2_ref[...] + dxn
        dx1_ref[...] = dx1
        dy, dgp = _rms_bwd(y_ref[...].astype(F32), gpost_ref[...], dx1)
        dyb = dy.astype(BF16)
        da_ref[...] = _dot_nt(dyb, wo_ref[...]).astype(BF16)
        _acc(acc, _dot_tn(a_ref[...], dyb), first)
        _acc(dgf_ref, dgf, first)
        _acc(dgp_ref, dgp, first)

        @pl.when(i == nt - 1)
        def _():
            dwo_ref[...] = acc[...].astype(BF16)

    return _launch(
        body, name="bwd_attn_out", grid=(nt,),
        in_specs=[_row_spec(D_MODEL)] * 5 + [_full_spec((D_MODEL, D_MODEL)), _vec_spec(), _vec_spec()],
        out_specs=[_row_spec(D_MODEL), _row_spec(D_MODEL), _full_spec((D_MODEL, D_MODEL)), _vec_spec(), _vec_spec()],
        out_shape=[jax.ShapeDtypeStruct((T, D_MODEL), F32), jax.ShapeDtypeStruct((T, D_MODEL), BF16),
                   jax.ShapeDtypeStruct((D_MODEL, D_MODEL), BF16)] + [jax.ShapeDtypeStruct((1, D_MODEL), F32)] * 2,
        scratch_shapes=[pltpu.VMEM((D_MODEL, D_MODEL), F32)],
        args=(dx2, dh2, x1, y, attn, wo, g_ffn, g_post), job=job)


def _bwd_attention(q, dattn, kpad, vpad, sinks, job=None):
    T = q.shape[0]
    nb = T // ATT_BLOCK

    def body(q_ref, do_ref, k_ref, v_ref, sink_ref, dq_ref, dkv_ref, ds_ref, dk_ref, dv_ref, s_scr, dp_scr, p_scr,
             dsb_scr, rel_scr, off_scr):
        n = pl.program_id(0)
        _att_mask(n, rel_scr, off_scr)

        @pl.when(n == 0)
        def _():
            dk_ref[...] = jnp.zeros_like(dk_ref)
            dv_ref[...] = jnp.zeros_like(dv_ref)
            ds_ref[...] = jnp.zeros_like(ds_ref)

        start = pl.multiple_of(n * ATT_BLOCK, ATT_BLOCK)
        win = pl.ds(start, 2 * ATT_BLOCK)
        kw = k_ref[win, :]
        vw = v_ref[win, :]
        lane = lax.broadcasted_iota(jnp.int32, (1, ATT_BLOCK), 1)
        dsink = jnp.zeros((1, ATT_BLOCK), F32)
        dqs, dks, dvs = [], [], []
        for kh in range(N_KV_HEADS):
            kk = kw[:, kh * HEAD_DIM:(kh + 1) * HEAD_DIM]
            vv = vw[:, kh * HEAD_DIM:(kh + 1) * HEAD_DIM]
            qs = _stack_heads(q_ref, kh)
            dos = _stack_heads(do_ref, kh)
            s_scr[...] = _dot_nt(qs, kk)
            dp_scr[...] = _dot_nt(dos, vv)
            for g in range(GQA_GROUP):
                h = kh * GQA_GROUP + g
                dsink_h = jnp.zeros((1, 1), F32)
                for row0 in range(0, ATT_BLOCK, ATT_SUB):
                    rows, sub = pl.ds(g * ATT_BLOCK + row0, ATT_SUB), pl.ds(row0, ATT_SUB)
                    pr, ps = _att_probs(s_scr[rows, :], rel_scr[sub, :], off_scr[sub, :], _alibi_slope(h),
                                        sink_ref[0, h])
                    dp = dp_scr[rows, :]
                    delta = jnp.sum(pr * dp, axis=-1, keepdims=True)
                    dsb_scr[rows, :] = (pr * (dp - delta) * ATT_SCALE).astype(BF16)
                    p_scr[rows, :] = pr.astype(BF16)
                    dsink_h = dsink_h - jnp.sum(ps * delta, axis=0, keepdims=True)
                dsink = dsink + jnp.where(lane == h, dsink_h, 0.0)
            dsb = dsb_scr[...]
            dqs += _unstack_heads(_dot(dsb, kk))
            dks.append(_dot_tn(dsb, qs))
            dvs.append(_dot_tn(p_scr[...], dos))
        dq_ref[...] = jnp.concatenate(dqs, axis=1).astype(BF16)
        dk_ref[win, :] += jnp.concatenate(dks, axis=1)
        dv_ref[win, :] += jnp.concatenate(dvs, axis=1)
        ds_ref[...] += dsink

        @pl.when(n == nb - 1)
        def _():
            dkv_ref[:, :KV_DIM] = dk_ref[ATT_BLOCK:, :].astype(BF16)
            dkv_ref[:, KV_DIM:] = dv_ref[ATT_BLOCK:, :].astype(BF16)

    return _launch(
        body, name="bwd_attention", grid=(nb,),
        in_specs=[_row_spec(D_MODEL, ATT_BLOCK), _row_spec(D_MODEL, ATT_BLOCK), _full_spec((T + ATT_BLOCK, KV_DIM)),
                  _full_spec((T + ATT_BLOCK, KV_DIM)), pl.BlockSpec(memory_space=pltpu.SMEM)],
        out_specs=[_row_spec(D_MODEL, ATT_BLOCK), _full_spec((T, 2 * KV_DIM)), _full_spec((1, ATT_BLOCK))],
        out_shape=[jax.ShapeDtypeStruct((T, D_MODEL), BF16), jax.ShapeDtypeStruct((T, 2 * KV_DIM), BF16),
                   jax.ShapeDtypeStruct((1, ATT_BLOCK), F32)],
        scratch_shapes=[pltpu.VMEM((T + ATT_BLOCK, KV_DIM), F32)] * 2
                       + [pltpu.VMEM((ATT_GROUP_ROWS, 2 * ATT_BLOCK), F32)] * 2
                       + [pltpu.VMEM((ATT_GROUP_ROWS, 2 * ATT_BLOCK), BF16)] * 2
                       + [pltpu.VMEM((ATT_BLOCK, 2 * ATT_BLOCK), F32)] * 2,
        args=(q, dattn, kpad, vpad, sinks), vmem=VMEM_BIG, job=job)


def _bwd_qkv(dxres, dq, dkv, x3, h1, hk, wq, wkv, g_mix, g_kv, job=None):
    T = x3.shape[0]
    nt = T // ROW_TILE

    def body(dxr_ref, dq_ref, dkv_ref, x_ref, h1_ref, hk_ref, wq_ref, wkv_ref, gmix_ref, gkv_ref,
             dx_ref, dwq_ref, dwkv_ref, dgm_ref, dgk_ref, acc_q, acc_kv):
        i = pl.program_id(0)
        first = i == 0
        dqv = dq_ref[...]
        dkvv = dkv_ref[...]
        xv = x_ref[...]
        d1, dgm = _rms_bwd(xv, gmix_ref[...], _dot_nt(dqv, wq_ref[...]))
        d2, dgk = _rms_bwd(xv, gkv_ref[...], _dot_nt(dkvv, wkv_ref[...]))
        dx_ref[...] = dxr_ref[...] + d1 + d2
        _acc(acc_q, _dot_tn(h1_ref[...], dqv), first)
        _acc(acc_kv, _dot_tn(hk_ref[...], dkvv), first)
        _acc(dgm_ref, dgm, first)
        _acc(dgk_ref, dgk, first)

        @pl.when(i == nt - 1)
        def _():
            dwq_ref[...] = acc_q[...].astype(BF16)
            dwkv_ref[...] = acc_kv[...].astype(BF16)

    return _launch(
        body, name="bwd_qkv", grid=(nt,),
        in_specs=[_row_spec(D_MODEL), _row_spec(D_MODEL), _row_spec(2 * KV_DIM), _row_spec(D_MODEL), _row_spec(D_MODEL),
                  _row_spec(D_MODEL), _full_spec((D_MODEL, D_MODEL)), _full_spec((D_MODEL, 2 * KV_DIM)), _vec_spec(),
                  _vec_spec()],
        out_specs=[_row_spec(D_MODEL), _full_spec((D_MODEL, D_MODEL)), _full_spec((D_MODEL, 2 * KV_DIM)), _vec_spec(),
                   _vec_spec()],
        out_shape=[jax.ShapeDtypeStruct((T, D_MODEL), F32), jax.ShapeDtypeStruct((D_MODEL, D_MODEL), BF16),
                   jax.ShapeDtypeStruct((D_MODEL, 2 * KV_DIM), BF16)] + [jax.ShapeDtypeStruct((1, D_MODEL), F32)] * 2,
        scratch_shapes=[pltpu.VMEM((D_MODEL, D_MODEL), F32), pltpu.VMEM((D_MODEL, 2 * KV_DIM), F32)],
        args=(dxres, dq, dkv, x3, h1, hk, wq, wkv, g_mix, g_kv), job=job)


def _bwd_pool_mixer(dx2, dh2, x1, x, yraw, d, wp, scale, g_ffn, g_post, g_pre, job=None):
    T = x.shape[0]
    tm = ROW_TILE
    nt = T // tm

    def body(dx2_ref, dh2_ref, x1_ref, x_ref, yraw_ref, d_ref, wp_ref, sc_ref, gffn_ref, gpost_ref, gpre_ref,
             dx_ref, dwp_ref, dsc_ref, dgf_ref, dgp_ref, dgm_ref, carry, acc):
        i = pl.program_id(0)
        first = i == 0
        tile = nt - 1 - i

        @pl.when(first)
        def _():
            carry[...] = jnp.zeros_like(carry)

        dxn, dgf = _rms_bwd(x1_ref[...], gffn_ref[...], dh2_ref[...])
        dx1 = dx2_ref[...] + dxn
        yraw = yraw_ref[...].astype(F32)
        sc = sc_ref[...]
        dy, dgp = _rms_bwd(yraw * sc, gpost_ref[...], dx1)
        dsc = jnp.sum(dy * yraw, axis=0, keepdims=True)
        dyb = (dy * sc).astype(BF16)
        dv = d_ref[...]
        dds = []
        for g in range(N_POOL_GROUPS):
            cols = slice(g * POOL_GROUP, (g + 1) * POOL_GROUP)
            dds.append(_dot_nt(dyb[:, cols], wp_ref[g]))
            _acc(acc.at[g], _dot_tn(dv[:, cols], dyb[:, cols]), first)
        dd = jnp.concatenate(dds, axis=1)
        e = dd / _pool_counts(tile * tm, tm)
        ext = jnp.concatenate([e, carry[...]], axis=0)
        carry[...] = e[:POOL_HALO, :]
        sums = _window_sums(ext, lambda k: tm + POOL_HALO - k)[:tm, :]
        dxm, dgm = _rms_bwd(x_ref[...], gpre_ref[...], sums - dd)
        dx_ref[...] = dx1 + dxm
        _acc(dsc_ref, dsc, first)
        _acc(dgf_ref, dgf, first)
        _acc(dgp_ref, dgp, first)
        _acc(dgm_ref, dgm, first)

        @pl.when(i == nt - 1)
        def _():
            dwp_ref[...] = acc[...].astype(BF16)

    rev = pl.BlockSpec((tm, D_MODEL), lambda i: (nt - 1 - i, 0))
    return _launch(
        body, name="bwd_pool_mixer", grid=(nt,),
        in_specs=[rev] * 6 + [_full_spec((N_POOL_GROUPS, POOL_GROUP, POOL_GROUP))] + [_vec_spec()] * 4,
        out_specs=[rev, _full_spec((N_POOL_GROUPS, POOL_GROUP, POOL_GROUP))] + [_vec_spec()] * 4,
        out_shape=[jax.ShapeDtypeStruct((T, D_MODEL), F32),
                   jax.ShapeDtypeStruct((N_POOL_GROUPS, POOL_GROUP, POOL_GROUP), BF16)]
                  + [jax.ShapeDtypeStruct((1, D_MODEL), F32)] * 4,
        scratch_shapes=[pltpu.VMEM((POOL_HALO, D_MODEL), F32), pltpu.VMEM((N_POOL_GROUPS, POOL_GROUP, POOL_GROUP), F32)],
        args=(dx2, dh2, x1, x, yraw, d, wp, scale, g_ffn, g_post, g_pre), job=job)


def _my_place():
    return lax.axis_index("x"), lax.axis_index("y"), lax.axis_index("c")


def _dev_index(px, py, pc):
    return 4 * px + 2 * py + pc


def _peer_by_relation(r):
    x, y, c = _my_place()
    return (x ^ ((r >> 2) & 1), y ^ ((r >> 1) & 1), c ^ (r & 1))


def _slot_pool(ref, j):
    return ref.at[:, pl.ds(pl.multiple_of(j * 32, 32), 32), :]


def _slot_scale(ref, j):
    return ref.at[:, pl.ds(pl.multiple_of(j * 128, 128), 128)]


def _slot_rows128(ref, j):
    return ref.at[pl.ds(pl.multiple_of(j * 128, 128), 128), :]


def _slot_gu(ref, j):
    return ref.at[j % FF_CHUNKS, j // FF_CHUNKS]


def _slot_wd(ref, j):
    return ref.at[pl.ds(pl.multiple_of(j * WD_ROWS, 16), WD_ROWS), :]


def _slot_cols128(ref, j):
    return ref.at[:, pl.ds(pl.multiple_of(j * 128, 128), 128)]


_GATHERED = {
    "pool": ((N_POOL_GROUPS, POOL_GROUP, POOL_GROUP), BF16, _slot_pool),
    "scale": ((1, D_MODEL), F32, _slot_scale),
    "kv": ((D_MODEL, 2 * KV_DIM), BF16, _slot_rows128),
    "q": ((D_MODEL, D_MODEL), BF16, _slot_rows128),
    "o": ((D_MODEL, D_MODEL), BF16, _slot_rows128),
    "gu": ((FF_CHUNKS, 2, FF_BLOCK, D_MODEL), BF16, _slot_gu),
    "wd": ((D_FF, D_MODEL), BF16, _slot_wd),
    "guh": ((FF_CHUNKS, 2, FF_BLOCK, D_MODEL // 2), BF16, _slot_gu),
    "wdh": ((D_FF, D_MODEL // 2), BF16, _slot_wd),
    "gate": ((D_MODEL, D_MODEL), BF16, _slot_rows128),
    "proj": ((PLE_DIM, D_MODEL), BF16, _slot_cols128),
}


def _no_compute():
    pass


class _AllGather:
    peers = ("sibling", "x", "y")

    def __init__(self, names, shards):
        self.kinds = [_GATHERED[n.rstrip("01_")] for n in names]
        entries = [shards[n] if isinstance(shards[n], tuple) else (shards[n], None, None) for n in names]
        self.args = [array for array, _, _ in entries]
        self.layers = [layer for _, layer, _ in entries]
        self.columns = [columns for _, _, columns in entries]
        self.out_shape = [jax.ShapeDtypeStruct(shape, dtype) for shape, dtype, _ in self.kinds]
        n = len(names)
        self.scratch = [pltpu.SemaphoreType.DMA((n, 7)), pltpu.SemaphoreType.DMA((n, 7)), pltpu.SemaphoreType.DMA((n,))]

    def _plan(self, srcs, outs, sems):
        send_sems, recv_sems, local_sems = sems
        x, y, c = _my_place()

        def slot(t, dev):
            return self.kinds[t][2](outs[t], _dev_index(*dev))

        def copy(t, k, block, to, src=None):
            return pltpu.make_async_remote_copy(
                src_ref=slot(t, block) if src is None else src, dst_ref=slot(t, block),
                send_sem=send_sems.at[t, k], recv_sem=recv_sems.at[t, k], device_id=to, device_id_type=MESH)

        return types.SimpleNamespace(
            copy=copy, core=c, me=(x, y, c), sibling=(x, y, 1 - c),
            x_chip=(1 - x, y), y_chip=(x, 1 - y), far_chip=(1 - x, 1 - y),
            via=(x ^ (1 - c), y ^ c),
            onto=(x ^ c, y ^ (1 - c)),
            k_via=1 + c, k_onto=2 - c,
            local=[pltpu.make_async_copy(self._shard(srcs, t), slot(t, (x, y, c)), local_sems.at[t])
                   for t in range(len(srcs))])

    def _shard(self, srcs, t):
        shard = srcs[t] if self.layers[t] is None else srcs[t].at[self.layers[t]]
        if self.columns[t] is None:
            return shard
        first, end = self.columns[t]
        return shard.at[:, first:end]

    def start(self, srcs, outs, sems):
        p = self._plan(srcs, outs, sems)
        for cp in p.local:
            cp.start()
        for t in range(len(srcs)):
            shard = self._shard(srcs, t)
            p.copy(t, 0, p.me, p.sibling, src=shard).start()
            p.copy(t, 1, p.me, (*p.x_chip, p.core), src=shard).start()
            p.copy(t, 2, p.me, (*p.y_chip, p.core), src=shard).start()

    def mid(self, srcs, outs, sems):
        p = self._plan(srcs, outs, sems)
        for t in range(len(srcs)):
            block = (*p.via, p.core)
            p.copy(t, p.k_via, block, p.me).wait_recv()
            p.copy(t, 3, block, (*p.onto, p.core)).start()
            p.copy(t, 3 + p.k_via, block, p.sibling).start()

    def late(self, srcs, outs, sems):
        p = self._plan(srcs, outs, sems)
        n = len(srcs)
        for t in range(n):
            block = (*p.onto, p.core)
            p.copy(t, p.k_onto, block, p.me).wait_recv()
            p.copy(t, 3 + p.k_onto, block, p.sibling).start()
        for t in range(n):
            block = (*p.far_chip, p.core)
            p.copy(t, 3, block, p.me).wait_recv()
            p.copy(t, 6, block, p.sibling).start()

    def finish(self, srcs, outs, sems):
        p = self._plan(srcs, outs, sems)
        n = len(srcs)
        other = 1 - p.core
        for t in range(n):
            p.copy(t, 0, (*p.me[:2], other), p.me).wait_recv()
            for k, chip in ((4, p.x_chip), (5, p.y_chip), (6, p.far_chip)):
                p.copy(t, k, (*chip, other), p.me).wait_recv()
            for k in range(7):
                p.copy(t, k, p.me, p.sibling).wait_send()
        for cp in p.local:
            cp.wait()


def _jobs_only(name, job=None):
    return _launch(_no_compute, name=name, grid=(), in_specs=[], out_specs=[], out_shape=[], args=(), job=job)


def _block_pool(ref, j):
    return ref.at[:, pl.ds(pl.multiple_of(j * 32, 32), 32), :]


def _block_rows128(ref, j):
    return ref.at[pl.ds(pl.multiple_of(j * 128, 128), 128), :]


def _block_gu(ref, j):
    return ref.at[j % FF_CHUNKS, j // FF_CHUNKS]


def _block_wd(ref, j):
    return ref.at[pl.ds(pl.multiple_of(j * WD_ROWS, 16), WD_ROWS), :]


def _block_cols128(ref, j):
    return ref.at[:, pl.ds(pl.multiple_of(j * 128, 128), 128)]


_SCATTERED = {
    "pool": ((N_POOL_GROUPS, 32, POOL_GROUP), _block_pool),
    "kv": ((128, 2 * KV_DIM), _block_rows128),
    "q": ((128, D_MODEL), _block_rows128),
    "o": ((128, D_MODEL), _block_rows128),
    "gu": ((FF_BLOCK, FF_PART), _block_gu),
    "wd": ((WD_ROWS, FF_PART), _block_wd),
    "guA": ((FF_BLOCK, FF_PART), lambda ref, j: _block_gu(ref, j).at[:, :FF_PART]),
    "guB": ((FF_BLOCK, FF_PART), lambda ref, j: _block_gu(ref, j).at[:, FF_PART:]),
    "wdA": ((WD_ROWS, FF_PART), lambda ref, j: _block_wd(ref, j).at[:, :FF_PART]),
    "wdB": ((WD_ROWS, FF_PART), lambda ref, j: _block_wd(ref, j).at[:, FF_PART:]),
    "gate": ((128, D_MODEL), _block_rows128),
    "proj": ((PLE_DIM, 128), _block_cols128),
}


class _SiblingSwap:
    peers = ("sibling",)

    def __init__(self, pieces):
        self.kinds = [_SCATTERED[kind] for kind, _ in pieces]
        self.args = [g for _, g in pieces]
        self.out_shape = [jax.ShapeDtypeStruct((N_CHIPS, *block), BF16) for block, _ in self.kinds]
        n = len(pieces)
        self.scratch = [pltpu.SemaphoreType.DMA((n, N_CHIPS)), pltpu.SemaphoreType.DMA((n, N_CHIPS))]

    def _copies(self, srcs, outs, sems):
        send_sems, recv_sems = sems
        x, y, c = _my_place()
        return [pltpu.make_async_remote_copy(
            src_ref=block(srcs[t], 2 * ch + 1 - c), dst_ref=outs[t].at[ch], send_sem=send_sems.at[t, ch],
            recv_sem=recv_sems.at[t, ch], device_id=(x, y, 1 - c), device_id_type=MESH)
            for t, (_, block) in enumerate(self.kinds) for ch in range(N_CHIPS)]

    def start(self, srcs, outs, sems):
        for cp in self._copies(srcs, outs, sems):
            cp.start()

    def finish(self, srcs, outs, sems):
        for cp in self._copies(srcs, outs, sems):
            cp.wait()


class _ChipScatter:
    N_BUFS = 4
    peers = ("x", "y")

    def __init__(self, pieces):
        self.kinds = [_SCATTERED[kind] for kind, _, _ in pieces]
        self.n = n = len(pieces)
        self.args = [g for _, g, _ in pieces] + [s for _, _, s in pieces]
        self.out_shape = [jax.ShapeDtypeStruct((2, *block), BF16) for block, _ in self.kinds]
        self.scratch = []
        for block, _ in self.kinds:
            self.scratch += [pltpu.VMEM((N_CHIPS, *block), BF16)] * 3 + [pltpu.VMEM((2, *block), BF16)]
        dma = pltpu.SemaphoreType.DMA
        self.scratch += [dma((n, N_CHIPS + 1)), dma((n, 2)), dma((n, 2)), dma((n,)), dma((n,)), dma((n,))]

    def _plan(self, outs, scr):
        n = self.n
        first_send, first_recv, second_send, second_recv, keep_sems = scr[self.N_BUFS * n + 1:]
        x, y, c = _my_place()
        via = (x ^ (1 - c), y ^ c)
        onto = (x ^ c, y ^ (1 - c))
        index = lambda chip: 2 * chip[0] + chip[1]
        first, second, keep = [], [], []
        for t in range(n):
            total, inbox = scr[self.N_BUFS * t + 2], scr[self.N_BUFS * t + 3]
            for k, chip in enumerate((via, (1 - x, 1 - y))):
                first.append(pltpu.make_async_remote_copy(
                    src_ref=total.at[index(chip)], dst_ref=inbox.at[k], send_sem=first_send.at[t, k],
                    recv_sem=first_recv.at[t, k], device_id=(*via, c), device_id_type=MESH))
            second.append(pltpu.make_async_remote_copy(
                src_ref=total.at[index(onto)], dst_ref=outs[t].at[1], send_sem=second_send.at[t],
                recv_sem=second_recv.at[t], device_id=(*onto, c), device_id_type=MESH))
            keep.append(pltpu.make_async_copy(total.at[index((x, y))], outs[t].at[0], keep_sems.at[t]))
        return first, second, keep, index((x, y)), index(onto), index(via), index((1 - x, 1 - y))

    def start(self, ins, outs, scr):
        n = self.n
        load_sems = scr[self.N_BUFS * n]
        c = lax.axis_index("c")
        loads = []
        for t, (_, block) in enumerate(self.kinds):
            mine, theirs = scr[self.N_BUFS * t], scr[self.N_BUFS * t + 1]
            loads += [pltpu.make_async_copy(block(ins[t], 2 * ch + c), mine.at[ch], load_sems.at[t, ch])
                      for ch in range(N_CHIPS)]
            loads.append(pltpu.make_async_copy(ins[n + t], theirs, load_sems.at[t, N_CHIPS]))
        for cp in loads:
            cp.start()
        for cp in loads:
            cp.wait()
        first, _, _, me, onto, via, far = self._plan(outs, scr)

        def add_cores(slots):
            for t in range(n):
                mine, theirs, total = scr[self.N_BUFS * t:self.N_BUFS * t + 3]
                for slot in slots:
                    total[slot] = (mine[slot].astype(F32) + theirs[slot].astype(F32)).astype(BF16)

        add_cores((via, far))
        for cp in first:
            cp.start()
        add_cores((me, onto))

    def mid(self, ins, outs, scr):
        first, second, keep, me, onto, _, _ = self._plan(outs, scr)
        for cp in first:
            cp.wait_recv()
        for t in range(self.n):
            total, inbox = scr[self.N_BUFS * t + 2], scr[self.N_BUFS * t + 3]
            for k, slot in enumerate((me, onto)):
                total[slot] = (total[slot].astype(F32) + inbox[k].astype(F32)).astype(BF16)
        for cp in second + keep:
            cp.start()

    def finish(self, ins, outs, scr):
        first, second, keep, _, _, _, _ = self._plan(outs, scr)
        for cp in first:
            cp.wait_send()
        for cp in second + keep:
            cp.wait()


class _ToEveryone:
    peers = _EVERYONE

    def __init__(self, scattered=(), gathered=()):
        self.blocks = [_SCATTERED[kind][1] for kind, _ in scattered] + [None] * len(gathered)
        self.args = [g for _, g in scattered] + list(gathered)
        self.out_shape = [jax.ShapeDtypeStruct((N_DEV, *_SCATTERED[kind][0]), BF16) for kind, _ in scattered]
        self.out_shape += [jax.ShapeDtypeStruct((N_DEV, *a.shape), a.dtype) for a in gathered]
        n = len(self.args)
        self.scratch = [pltpu.SemaphoreType.DMA((n, N_DEV - 1)), pltpu.SemaphoreType.DMA((n, N_DEV - 1)),
                        pltpu.SemaphoreType.DMA((n,))]

    def _copies(self, srcs, outs, sems):
        send_sems, recv_sems, local_sems = sems
        me = _dev_index(*_my_place())
        copies = []
        for t, block in enumerate(self.blocks):
            part = (lambda j, t=t, block=block: srcs[t] if block is None else block(srcs[t], j))
            copies.append(pltpu.make_async_copy(part(me), outs[t].at[me], local_sems.at[t]))
            for r in range(1, N_DEV):
                peer = _peer_by_relation(r)
                copies.append(pltpu.make_async_remote_copy(
                    src_ref=part(_dev_index(*peer)), dst_ref=outs[t].at[me], send_sem=send_sems.at[t, r - 1],
                    recv_sem=recv_sems.at[t, r - 1], device_id=peer, device_id_type=MESH))
        return copies

    def start(self, srcs, outs, sems):
        for cp in self._copies(srcs, outs, sems):
            cp.start()

    def finish(self, srcs, outs, sems):
        for cp in self._copies(srcs, outs, sems):
            cp.wait()


class _Jobs:
    def __init__(self, *jobs):
        self.jobs = jobs
        together = {p for j in jobs for p in j.peers}
        self.peers = tuple(p for p in _EVERYONE if p in together)
        self.args = [a for j in jobs for a in j.args]
        self.out_shape = [o for j in jobs for o in j.out_shape]
        self.scratch = [s for j in jobs for s in j.scratch]

    def _split(self, refs, attr):
        at = 0
        for j in self.jobs:
            n = len(getattr(j, attr))
            yield refs[at:at + n]
            at += n

    def _each(self, ins, outs, scr):
        return zip(self.jobs, self._split(ins, "args"), self._split(outs, "out_shape"), self._split(scr, "scratch"))

    def start(self, ins, outs, scr):
        for j, i, o, s in self._each(ins, outs, scr):
            j.start(i, o, s)

    def mid(self, ins, outs, scr):
        for j, i, o, s in self._each(ins, outs, scr):
            if hasattr(j, "mid"):
                j.mid(i, o, s)

    def late(self, ins, outs, scr):
        for j, i, o, s in self._each(ins, outs, scr):
            if hasattr(j, "late"):
                j.late(i, o, s)

    def finish(self, ins, outs, scr):
        for j, i, o, s in self._each(ins, outs, scr):
            j.finish(i, o, s)

    def split_outputs(self, outs):
        return list(self._split(outs, "out_shape"))


def _adamw_math(w, g, m, v):
    m = ADAM_B1 * m + (1.0 - ADAM_B1) * g
    v = ADAM_B2 * v + (1.0 - ADAM_B2) * (g * g)
    m_hat = m / (1.0 - ADAM_B1 ** ADAM_STEP)
    v_hat = v / (1.0 - ADAM_B2 ** ADAM_STEP)
    delta = -ADAM_LR * (m_hat / (jnp.sqrt(v_hat) + ADAM_EPS) + ADAM_WD * w)
    return delta, m, v


def _adamw(name, w, m, v, landings, n_col_blocks=1, job=None):
    n_slots, r, c = landings[0].shape
    grid = (w.shape[0] // r, n_col_blocks)

    def body(w_ref, m_ref, v_ref, *rest):
        l_refs, (g_ref, d_ref, nm_ref, nv_ref) = rest[:len(landings)], rest[len(landings):]
        step = pl.program_id(0) * n_col_blocks + pl.program_id(1)
        for idx, l_ref in enumerate(l_refs):
            @pl.when(step == idx)
            def _(l_ref=l_ref):
                g = l_ref[0].astype(F32)
                for s in range(1, n_slots):
                    g = g + l_ref[s].astype(F32)
                g_ref[...] = g
                d_ref[...], nm_ref[...], nv_ref[...] = _adamw_math(w_ref[...], g, m_ref[...], v_ref[...])

    spec = pl.BlockSpec((r, c), lambda a, b: (a, b))
    return _launch(
        body, name=f"adamw_{name}", grid=grid,
        in_specs=[spec, spec, spec] + [_full_spec((n_slots, r, c))] * len(landings),
        out_specs=[spec] * 4, out_shape=[jax.ShapeDtypeStruct(w.shape, F32)] * 4,
        args=(w, m, v, *landings), vmem=VMEM_BIG, job=job)


_SMALL = (("pre_mix_g", SV_PRE_MIX, 2), ("post_mix_g", SV_POST_MIX, 2), ("pre_ffn_g", SV_PRE_FFN, 2),
          ("post_ffn_g", SV_POST_FFN, 2), ("ple_g", SV_PLE, 2), ("ple_post_g", SV_PLE_POST, 2), ("kv_g", SV_KV, 1),
          ("pool_scale", SV_POOL_SCALE, 1), ("sinks", SV_SINKS, 1))


def _adamw_several(items):
    counts = [len(landings) for _, _, _, landings in items]
    args = [a for w, m, v, landings in items for a in (w, m, v, *landings)]
    out_shape = [jax.ShapeDtypeStruct(w.shape, F32) for w, _, _, _ in items for _ in range(4)]

    def body(*refs):
        ins, outs = refs[:len(args)], refs[len(args):]
        at = 0
        for idx, n_landings in enumerate(counts):
            w_ref, m_ref, v_ref = ins[at:at + 3]
            l_refs = ins[at + 3:at + 3 + n_landings]
            at += 3 + n_landings
            g_ref, d_ref, nm_ref, nv_ref = outs[4 * idx:4 * idx + 4]
            for part, l_ref in enumerate(l_refs):
                rows = slice(part * l_ref.shape[1], (part + 1) * l_ref.shape[1])
                g = l_ref[0].astype(F32)
                for s in range(1, l_ref.shape[0]):
                    g = g + l_ref[s].astype(F32)
                g_ref[rows, :] = g
                d_ref[rows, :], nm_ref[rows, :], nv_ref[rows, :] = _adamw_math(
                    w_ref[rows, :], g, m_ref[rows, :], v_ref[rows, :])

    res, _ = _launch(
        body, name="adamw_several", grid=(1,), in_specs=[_full_spec(a.shape) for a in args],
        out_specs=[_full_spec(s.shape) for s in out_shape], out_shape=out_shape, args=args)
    return [res[4 * idx:4 * idx + 4] for idx in range(len(items))]


def _small_adamw(slabs, params):
    flat = [a for name, _, _ in _SMALL for a in params[name]]
    n_in = 1 + len(flat)

    def body(*refs):
        slabs_ref, wmv = refs[0], refs[1:n_in]
        loss_ref, outs, total = refs[n_in], refs[n_in + 1:-1], refs[-1]
        me = _dev_index(*_my_place())
        g = slabs_ref[0]
        for s in range(1, N_DEV):
            g = g + slabs_ref[s]
        total[...] = g
        loss_ref[...] = total[SV_LOSS:SV_LOSS + 1, 0:1]
        for idx, (name, row, n_rows) in enumerate(_SMALL):
            w_ref, m_ref, v_ref = wmv[3 * idx:3 * idx + 3]
            g_ref, d_ref, nm_ref, nv_ref = outs[4 * idx:4 * idx + 4]
            if name == "pool_scale":
                g = total[row:row + 1, pl.ds(pl.multiple_of(me * 128, 128), 128)]
            else:
                g = total[row:row + n_rows, 0:w_ref.shape[1]]
            g_ref[...] = g
            d_ref[...], nm_ref[...], nv_ref[...] = _adamw_math(w_ref[...], g, m_ref[...], v_ref[...])

    out_shape = [jax.ShapeDtypeStruct((1, 1), F32)]
    for name, _, _ in _SMALL:
        out_shape += [jax.ShapeDtypeStruct(params[name][0].shape, F32)] * 4
    res, _ = _launch(
        body, name="small_adamw", grid=(1,),
        in_specs=[_full_spec(a.shape) for a in (slabs, *flat)], out_specs=[_full_spec(s.shape) for s in out_shape],
        out_shape=out_shape, scratch_shapes=[pltpu.VMEM((SV_ROWS, D_MODEL), F32)], args=(slabs, *flat))
    return res[0], {name: res[1 + 4 * idx:5 + 4 * idx] for idx, (name, _, _) in enumerate(_SMALL)}


def _local_step(x, p, tgt, gains, sinks, shards, weights):
    row = lambda first_row, layer: _Gain(gains, first_row + layer)
    gather = lambda *names: _AllGather(names, shards)
    g_pre_mix, g_post_mix, g_pre_ffn, g_post_ffn = SV_PRE_MIX, SV_POST_MIX, SV_PRE_FFN, SV_POST_FFN
    g_ple, g_ple_post, g_kv = SV_PLE, SV_PLE_POST, _Gain(gains, SV_KV)

    (dpool,), (wp, scale, wgu0, wd0) = _fwd_pool(x, row(g_pre_mix, 0), job=gather("pool", "scale", "gu0", "wd0"))
    wgu0, wd0 = [wgu0], [wd0]
    (x1_0, h2_0, yraw), _ = _fwd_pool_mixer(x, dpool, wp, scale, row(g_post_mix, 0), row(g_pre_ffn, 0))
    (gs0, us0, f0, x2_0, h3_0), (wgate0, wproj0, wkv, wq, wo, wd1_a) = _fwd_ffn(
        0, h2_0, x1_0, wgu0, wd0, row(g_post_ffn, 0), row(g_ple, 0),
        job=gather("gate0", "proj0", "kv", "q", "o", "wdh1_0"))
    (x3_0, z0, pe0, hk, h1, q, kpad, vpad), (wgu1_a,) = _fwd_ple_qkv(
        x2_0, h3_0, p[0], wgate0, wproj0, row(g_ple_post, 0), g_kv, row(g_pre_mix, 1), wkv, wq,
        job=gather("guh1_0"))
    (attn,), (wgu1_b,) = _fwd_attention(q, kpad, vpad, sinks, job=gather("guh1_1"))
    (y1, x1_1, h2_1), (wd1_b,) = _fwd_attn_out(attn, x3_0, wo, row(g_post_mix, 1), row(g_pre_ffn, 1),
                                               job=gather("wdh1_1"))
    wgu1, wd1 = [wgu1_a, wgu1_b], [wd1_a, wd1_b]
    (gs1, us1, f1, x2_1, h3_1), (wgate1, wproj1) = _fwd_ffn(
        1, h2_1, x1_1, wgu1, wd1, row(g_post_ffn, 1), row(g_ple, 1), job=gather("gate1", "proj1"))

    produced, swapped, landed = {}, {}, {}

    def kind_of(name):
        return name.rstrip("0123_")

    def hosted(call, *args, swap=(), spread=(), extra=None):
        jobs = []
        if swap:
            jobs.append(_SiblingSwap([(kind_of(n), produced[n]) for n in swap]))
        if spread:
            jobs.append(_ChipScatter([(kind_of(n), produced[n], swapped[n]) for n in spread]))
        if extra is not None:
            jobs.append(extra)
        jobs = _Jobs(*jobs)
        outs, job_outs = call(*args, job=jobs)
        parts = jobs.split_outputs(job_outs)
        if swap:
            swapped.update(zip(swap, parts.pop(0)))
        if spread:
            landed.update(zip(spread, parts.pop(0)))
        return outs if extra is None else (outs, parts.pop(0))

    ffn_q = lambda layer, qtr: (f"gu{layer}_{qtr}", f"wd{layer}_{qtr}")

    dx2_1, df1, produced["gate1"], produced["proj1"], dg_ple_post1, dg_ple1, dg_post_ffn1, loss = hosted(
        _ple_loss_bwd, 1, x2_1, h3_1, p[1], f1, tgt, wgate1, wproj1, row(g_ple_post, 1), row(g_ple, 1),
        row(g_post_ffn, 1))
    dh2_1, dg1, du1, a1 = hosted(_bwd_ffn_act, 1, df1, gs1, us1, wgu1, wd1, swap=("gate1", "proj1"))
    dgu1, dwd1 = hosted(_bwd_ffn_dw, 1, 0, 1, h2_1, df1, dg1, du1, a1, spread=("gate1", "proj1"))
    produced.update(guA1=dgu1, guB1=dgu1, wdA1=dwd1, wdB1=dwd1)
    dx1_1, dattn, produced["o"], dg_pre_ffn1, dg_post_mix1 = hosted(
        _bwd_attn_out, dx2_1, dh2_1, x1_1, y1, attn, wo, row(g_pre_ffn, 1), row(g_post_mix, 1),
        swap=("guA1", "wdA1", "guB1", "wdB1"))
    dq, dkv, dsinks = hosted(_bwd_attention, q, dattn, kpad, vpad, sinks, spread=("guA1", "wdA1"))
    dx3_0, produced["q"], produced["kv"], dg_pre_mix1, dg_kv = hosted(
        _bwd_qkv, dx1_1, dq, dkv, x3_0, h1, hk, wq, wkv, row(g_pre_mix, 1), g_kv, swap=("o",), spread=("wdB1",))
    dx2_0, df0, produced["gate0"], produced["proj0"], dg_ple_post0, dg_ple0, dg_post_ffn0 = hosted(
        _bwd_ple, 0, dx3_0, x2_0, z0, pe0, h3_0, p[0], f0, wgate0, row(g_ple_post, 0), row(g_ple, 0),
        row(g_post_ffn, 0), swap=("q", "kv"), spread=("guB1",))
    for half, letter in enumerate("AB"):
        landed[f"gu1_{half}"], landed[f"wd1_{half}"] = landed[f"gu{letter}1"], landed[f"wd{letter}1"]
    dh2_0, dg0, du0, a0 = hosted(_bwd_ffn_act, 0, df0, gs0, us0, wgu0, wd0,
                                 swap=("gate0", "proj0"), spread=("o", "q", "kv"))
    part_hosts = [dict(spread=("gate0", "proj0")), dict(swap=ffn_q(0, 0))]
    for part in range(FF_PARTS):
        produced[f"gu0_{part}"], produced[f"wd0_{part}"] = hosted(
            _bwd_ffn_dw, 0, part, FF_PARTS, h2_0, df0, dg0, du0, a0, **part_hosts[part])
    grad_x, produced["pool"], dscale, dg_pre_ffn0, dg_post_mix0, dg_pre_mix0 = hosted(
        _bwd_pool_mixer, dx2_0, dh2_0, x1_0, x, yraw, dpool, wp, scale, row(g_pre_ffn, 0), row(g_post_mix, 0),
        row(g_pre_mix, 0), swap=ffn_q(0, 1), spread=ffn_q(0, 0))

    def update(name, n_col_blocks, pieces):
        w, m, v = weights[name]
        rows = w.size // w.shape[-1]
        flat = [landed[n].reshape(landed[n].shape[0], -1, landed[n].shape[-1]) for n in pieces]
        outs, _ = _adamw(name, w.reshape(rows, -1), m.reshape(rows, -1), v.reshape(rows, -1), flat, n_col_blocks)
        return [o.reshape(w.shape) for o in outs]

    upd = {}
    lanes = lambda a: jnp.pad(a, ((0, 0), (0, D_MODEL - a.shape[1])))
    small = jnp.concatenate([
        dg_pre_mix0, dg_pre_mix1, dg_post_mix0, dg_post_mix1, dg_pre_ffn0, dg_pre_ffn1, dg_post_ffn0, dg_post_ffn1,
        dg_ple0, dg_ple1, dg_ple_post0, dg_ple_post1, dg_kv, dscale, lanes(dsinks[:, :N_HEADS]), lanes(loss)], axis=0)

    everyone = _ToEveryone(scattered=[("pool", produced["pool"])], gathered=[small])
    _, (landed["pool"], slabs) = hosted(_jobs_only, "scatter_tail", spread=ffn_q(0, 1), extra=everyone)
    several = {"w_ple_gate": ("gate0", "gate1"), "w_ple_proj": ("proj0", "proj1"), "w_q": ("q",), "w_kv": ("kv",),
               "w_o": ("o",), "pool_w": ("pool",)}
    flat2d = lambda a: a.reshape(-1, a.shape[-1])
    results = _adamw_several([
        (*map(flat2d, weights[name]),
         [landed[n].reshape(landed[n].shape[0], -1, landed[n].shape[-1]) for n in pieces])
        for name, pieces in several.items()])
    for name, outs in zip(several, results):
        upd[name] = [o.reshape(weights[name][0].shape) for o in outs]
    upd["w_gu"] = update("w_gu", FF_PARTS,
                         pieces=[f"gu{layer}_{qtr}" for layer in range(2) for qtr in range(FF_PARTS)])
    upd["w_gu"] = [jnp.swapaxes(a, 1, 2) for a in upd["w_gu"]]
    upd["w_down"] = update("w_down", FF_PARTS,
                           pieces=[f"wd{layer}_{qtr}" for layer in range(2) for qtr in range(FF_PARTS)])
    return grad_x, upd, slabs


def kernel(x, p, pre_mix_g, post_mix_g, pre_ffn_g, post_ffn_g, pool_w, pool_scale, kv_g, w_kv, w_q, sinks, w_o, w_gu, w_down, ple_g, w_ple_gate, w_ple_proj, ple_post_g, loss_target, m_pre_mix_g, m_post_mix_g, m_pre_ffn_g, m_post_ffn_g, m_pool_w, m_pool_scale, m_kv_g, m_w_kv, m_w_q, m_sinks, m_w_o, m_w_gu, m_w_down, m_ple_g, m_w_ple_gate, m_w_ple_proj, m_ple_post_g, v_pre_mix_g, v_post_mix_g, v_pre_ffn_g, v_post_ffn_g, v_pool_w, v_pool_scale, v_kv_g, v_w_kv, v_w_q, v_sinks, v_w_o, v_w_gu, v_w_down, v_ple_g, v_w_ple_gate, v_w_ple_proj, v_ple_post_g):
    shards = {"pool": pool_w[0].astype(BF16), "scale": pool_scale, "kv": w_kv.astype(BF16),
              "q": w_q[0].astype(BF16), "o": w_o[0].astype(BF16)}
    gu, wd = jnp.swapaxes(w_gu, 1, 2).astype(BF16), w_down.astype(BF16)
    gate, proj = w_ple_gate.astype(BF16), w_ple_proj.astype(BF16)
    for layer in range(2):
        shards[f"gu{layer}"] = (gu, layer, None)
        shards[f"wd{layer}"] = (wd, layer, None)
        for half in range(2):
            cols = (half * D_MODEL // 2, (half + 1) * D_MODEL // 2)
            shards[f"guh{layer}_{half}"] = (gu, layer, cols)
            shards[f"wdh{layer}_{half}"] = (wd, layer, cols)
        shards[f"gate{layer}"] = (gate, layer, None)
        shards[f"proj{layer}"] = (proj, layer, None)
    gains = jnp.concatenate([pre_mix_g, post_mix_g, pre_ffn_g, post_ffn_g, ple_g, ple_post_g, kv_g[None, :]],
                            axis=0).reshape(-1, 1, D_MODEL)
    weights = {"pool_w": (pool_w, m_pool_w, v_pool_w), "w_kv": (w_kv, m_w_kv, v_w_kv), "w_q": (w_q, m_w_q, v_w_q),
               "w_o": (w_o, m_w_o, v_w_o), "w_down": (w_down, m_w_down, v_w_down),
               "w_gu": tuple(jnp.swapaxes(a, 1, 2) for a in (w_gu, m_w_gu, v_w_gu)),
               "w_ple_gate": (w_ple_gate, m_w_ple_gate, v_w_ple_gate),
               "w_ple_proj": (w_ple_proj, m_w_ple_proj, v_w_ple_proj)}
    per_layer = p.reshape(p.shape[0], *p.shape[2:])
    p_rows = [_LayerRows(per_layer, layer) for layer in range(2)]
    grad_x, upd, slabs = _local_step(x[0], p_rows, loss_target[0], gains, sinks, shards, weights)

    small_params = {
        "pre_mix_g": (pre_mix_g, m_pre_mix_g, v_pre_mix_g), "post_mix_g": (post_mix_g, m_post_mix_g, v_post_mix_g),
        "pre_ffn_g": (pre_ffn_g, m_pre_ffn_g, v_pre_ffn_g), "post_ffn_g": (post_ffn_g, m_post_ffn_g, v_post_ffn_g),
        "ple_g": (ple_g, m_ple_g, v_ple_g), "ple_post_g": (ple_post_g, m_ple_post_g, v_ple_post_g),
        "kv_g": (kv_g[None, :], m_kv_g[None, :], v_kv_g[None, :]),
        "pool_scale": (pool_scale, m_pool_scale, v_pool_scale), "sinks": (sinks, m_sinks, v_sinks)}
    loss, small_upd = _small_adamw(slabs, small_params)
    small_upd["kv_g"] = [a[0] for a in small_upd["kv_g"]]
    upd.update(small_upd)

    names = ["pre_mix_g", "post_mix_g", "pre_ffn_g", "post_ffn_g", "pool_w", "pool_scale", "kv_g", "w_kv", "w_q",
             "sinks", "w_o", "w_gu", "w_down", "ple_g", "w_ple_gate", "w_ple_proj", "ple_post_g"]
    outs = [loss[0, 0], grad_x[None]]
    for kind in range(4):
        outs += [upd[n][kind] for n in names]
    return tuple(outs)
```

```python
import functools
import types

import jax
import jax.numpy as jnp
from jax import lax
from jax.experimental import pallas as pl
from jax.experimental.pallas import tpu as pltpu

F32 = jnp.float32
BF16 = jnp.bfloat16

N_DEV = 8
D_MODEL = 1024
N_POOL_GROUPS = 4
POOL_GROUP = 256
POOL_HALO = 16
HEAD_DIM = 64
N_HEADS = 16
N_KV_HEADS = 4
GQA_GROUP = 4
KV_DIM = N_KV_HEADS * HEAD_DIM
ATT_BLOCK = 128
D_FF = 2816
FF_CHUNKS = 4
FF_BLOCK = D_FF // FF_CHUNKS
WD_ROWS = D_FF // N_DEV
FF_PARTS = 2
FF_PART = D_MODEL // FF_PARTS
N_CHIPS = 4
PLE_DIM = 256
EPS = 1e-6
NEG_INF = -1e30
ATT_SCALE = HEAD_DIM ** -0.5

ADAM_LR = 0.001
ADAM_B1 = 0.9
ADAM_B2 = 0.999
ADAM_EPS = 1e-08
ADAM_WD = 0.01
ADAM_STEP = 10

ROW_TILE = 512
FFN_ROW_TILE = 512
FFN_WEIGHT_COLS = 512
FFN_SUB_TILES = 1
VMEM_BIG = 60 * 1024 * 1024
VMEM_MID = 56 * 1024 * 1024
HBM_PIN_ELEMS = 1024

SV_ROWS = 16
SV_PRE_MIX, SV_POST_MIX, SV_PRE_FFN, SV_POST_FFN, SV_PLE, SV_PLE_POST = 0, 2, 4, 6, 8, 10
SV_KV, SV_POOL_SCALE, SV_SINKS, SV_LOSS = 12, 13, 14, 15

MESH = pl.DeviceIdType.MESH
ANY = pl.BlockSpec(memory_space=pl.ANY)


def _dot(a, b):
    return jnp.dot(a, b, preferred_element_type=F32)


def _dot_nt(a, b):
    return lax.dot_general(a, b, (((1,), (1,)), ((), ())), preferred_element_type=F32)


def _dot_tn(a, b):
    return lax.dot_general(a, b, (((0,), (0,)), ((), ())), preferred_element_type=F32)


def _rstd(x):
    return lax.rsqrt(jnp.mean(x * x, axis=-1, keepdims=True) + EPS)


def _rms(x, g):
    return x * _rstd(x) * g


def _rms_bwd(x, g, dy):
    r = _rstd(x)
    n = x * r
    dn = dy * g
    dx = r * (dn - n * jnp.mean(dn * n, axis=-1, keepdims=True))
    dg = jnp.sum(dy * n, axis=0, keepdims=True)
    return dx, dg


def _add_all(terms):
    return functools.reduce(jnp.add, terms)


def _sigmoid(x):
    return 1.0 / (1.0 + jnp.exp(-x))


def _acc(ref, val, first):
    @pl.when(first)
    def _():
        ref[...] = val

    @pl.when(jnp.logical_not(first))
    def _():
        ref[...] += val


def _pool_counts(row0, rows):
    t = row0 + lax.broadcasted_iota(jnp.int32, (rows, D_MODEL), 0) + 1
    grp = lax.broadcasted_iota(jnp.int32, (rows, D_MODEL), 1) // POOL_GROUP
    win = jnp.left_shift(2, grp)
    return jnp.minimum(t, win).astype(F32)


def _window_sums(ext, shift_of):
    outs = []
    s = ext
    for gi in range(N_POOL_GROUPS):
        s = s + pltpu.roll(s, shift_of(1 << gi), axis=0)
        outs.append(s[:, :POOL_GROUP])
        s = s[:, POOL_GROUP:]
    return jnp.concatenate(outs, axis=1)


def _cparams(n_axes, vmem, collective_id=None):
    return pltpu.CompilerParams(dimension_semantics=("arbitrary",) * n_axes, vmem_limit_bytes=vmem,
                                collective_id=collective_id)


_EVERYONE = ("sibling", "x", "y", "far", "x sibling", "y sibling", "far sibling")
_PEER_SETS = (("sibling", "x", "y"), ("sibling",), ("x", "y"), _EVERYONE)


def _meet(peers):
    x, y, c = lax.axis_index("x"), lax.axis_index("y"), lax.axis_index("c")
    device = {"sibling": (x, y, 1 - c), "x": (1 - x, y, c), "y": (x, 1 - y, c), "far": (1 - x, 1 - y, c),
              "x sibling": (1 - x, y, 1 - c), "y sibling": (x, 1 - y, 1 - c), "far sibling": (1 - x, 1 - y, 1 - c)}
    barrier = pltpu.get_barrier_semaphore()
    for peer in peers:
        pl.semaphore_signal(barrier, inc=1, device_id=device[peer], device_id_type=pl.DeviceIdType.MESH)
    pl.semaphore_wait(barrier, len(peers))


def _row_spec(cols, tm=ROW_TILE):
    return pl.BlockSpec((tm, cols), lambda i: (i, 0))


def _full_spec(shape):
    zeros = (0,) * len(shape)
    return pl.BlockSpec(shape, lambda *_: zeros)


def _vec_spec():
    return _full_spec((1, D_MODEL))


def _column_views(parts):
    return [(a, b) for a in parts for b in range(a.shape[-1] // FFN_WEIGHT_COLS)]


def _column_ranges(views):
    return [(n * FFN_WEIGHT_COLS, (n + 1) * FFN_WEIGHT_COLS) for n in range(len(views))]


class _Gain:
    def __init__(self, stacked, layer):
        self.stacked, self.layer = stacked, layer

    def spec(self):
        layer = self.layer
        return pl.BlockSpec((None, 1, D_MODEL), lambda *_: (layer, 0, 0))


class _LayerRows:
    def __init__(self, stacked, layer):
        self.stacked, self.layer = stacked, layer

    def spec(self):
        layer = self.layer
        return pl.BlockSpec((None, ROW_TILE, self.stacked.shape[-1]), lambda i: (layer, i, 0))


def _in_hbm(a):
    return pltpu.with_memory_space_constraint(a, pltpu.HBM) if a.size >= HBM_PIN_ELEMS else a


def _out_in_hbm(s):
    return pltpu.HBM(s.shape, s.dtype) if s.size >= HBM_PIN_ELEMS else s


def _launch(body, *, name, grid, in_specs, out_specs, out_shape, args, scratch_shapes=(), vmem=VMEM_MID, job=None):
    picked = (_Gain, _LayerRows)
    in_specs = [a.spec() if isinstance(a, picked) else s for s, a in zip(in_specs, args)]
    args = [_in_hbm(a.stacked if isinstance(a, picked) else a) for a in args]
    n_in, n_out, n_scr = len(args), len(out_shape), len(scratch_shapes)
    if job is not None and not job.args:
        job = None
    j_args, j_out, j_scr = ([], [], []) if job is None else ([_in_hbm(a) for a in job.args], job.out_shape, job.scratch)

    def run(*refs):
        groups, at = [], 0
        for n in (n_in, len(j_args), n_out, len(j_out), n_scr, len(j_scr)):
            groups.append(refs[at:at + n])
            at += n
        ins, j_ins, outs, j_outs, scr, j_sems = groups

        def begin():
            _meet(job.peers)
            job.start(j_ins, j_outs, j_sems)

        if job is None:
            body(*ins, *outs, *scr)
        elif not grid:
            begin()
            job.mid(j_ins, j_outs, j_sems)
            job.late(j_ins, j_outs, j_sems)
            body(*ins, *outs, *scr)
            job.finish(j_ins, j_outs, j_sems)
        else:
            ids = [pl.program_id(a) for a in range(len(grid))]
            at_start = lambda step: functools.reduce(jnp.logical_and, [ids[0] == step] + [i == 0 for i in ids[1:]])
            last = functools.reduce(jnp.logical_and, [i == g - 1 for i, g in zip(ids, grid)])
            pl.when(at_start(0))(begin)
            pl.when(at_start(grid[0] // 2))(lambda: job.mid(j_ins, j_outs, j_sems))
            pl.when(at_start(3 * grid[0] // 4))(lambda: job.late(j_ins, j_outs, j_sems))
            body(*ins, *outs, *scr)
            pl.when(last)(lambda: job.finish(j_ins, j_outs, j_sems))

    res = pl.pallas_call(
        run, name=name, grid=grid,
        in_specs=list(in_specs) + [ANY] * len(j_args), out_specs=list(out_specs) + [ANY] * len(j_out),
        out_shape=[_out_in_hbm(s) for s in list(out_shape) + list(j_out)],
        scratch_shapes=list(scratch_shapes) + list(j_scr),
        compiler_params=_cparams(len(grid), vmem, None if job is None else _PEER_SETS.index(job.peers)),
    )(*args, *j_args)
    return res[:n_out], res[n_out:]


def _fwd_pool(x, g_pre, job=None):
    T = x.shape[0]
    tm = ROW_TILE
    nt = T // tm

    def body(x_ref, gpre_ref, d_ref, carry):
        i = pl.program_id(0)

        @pl.when(i == 0)
        def _():
            carry[...] = jnp.zeros_like(carry)

        h = _rms(x_ref[...], gpre_ref[...])
        ext = jnp.concatenate([carry[...], h], axis=0)
        carry[...] = h[tm - POOL_HALO:, :]
        sums = _window_sums(ext, lambda k: k)[POOL_HALO:, :]
        d_ref[...] = (sums / _pool_counts(i * tm, tm) - h).astype(BF16)

    return _launch(
        body, name="fwd_pool", grid=(nt,), in_specs=[_row_spec(D_MODEL), _vec_spec()], out_specs=[_row_spec(D_MODEL)],
        out_shape=[jax.ShapeDtypeStruct((T, D_MODEL), BF16)], scratch_shapes=[pltpu.VMEM((POOL_HALO, D_MODEL), F32)],
        args=(x, g_pre), job=job)


def _fwd_pool_mixer(x, d, wp, scale, g_post, g_ffn, job=None):
    T = x.shape[0]
    nt = T // ROW_TILE

    def body(x_ref, d_ref, wp_ref, sc_ref, gpost_ref, gffn_ref, x1_ref, h2_ref, yraw_ref):
        db = d_ref[...]
        yraw = jnp.concatenate(
            [_dot(db[:, g * POOL_GROUP:(g + 1) * POOL_GROUP], wp_ref[g]) for g in range(N_POOL_GROUPS)], axis=1)
        yraw_ref[...] = yraw.astype(BF16)
        x1 = x_ref[...] + _rms(yraw * sc_ref[...], gpost_ref[...])
        x1_ref[...] = x1
        h2_ref[...] = _rms(x1, gffn_ref[...]).astype(BF16)

    return _launch(
        body, name="fwd_pool_mixer", grid=(nt,),
        in_specs=[_row_spec(D_MODEL), _row_spec(D_MODEL), _full_spec((N_POOL_GROUPS, POOL_GROUP, POOL_GROUP)),
                  _vec_spec(), _vec_spec(), _vec_spec()],
        out_specs=[_row_spec(D_MODEL)] * 3,
        out_shape=[jax.ShapeDtypeStruct((T, D_MODEL), F32)] + [jax.ShapeDtypeStruct((T, D_MODEL), BF16)] * 2,
        args=(x, d, wp, scale, g_post, g_ffn), job=job)


def _fwd_ffn(layer, h2, x1, wgu, wd, g_post, g_ple, job=None):
    T = h2.shape[0]
    tm = min(FFN_ROW_TILE, T)
    nt = T // tm
    sub = tm // FFN_SUB_TILES
    last = FF_CHUNKS - 1
    wgu, wd = _column_views(wgu), _column_views(wd)
    n_gu, n_wd = len(wgu), len(wd)
    gu_cols = _column_ranges(wgu)

    def body(h2_ref, x1_ref, *refs):
        wgu_refs, wd_refs = refs[:n_gu], refs[n_gu:n_gu + n_wd]
        gpost_ref, gple_ref, gs_ref, us_ref, f_ref, x2_ref, h3_ref, acc = refs[n_gu + n_wd:]
        k = pl.program_id(0)
        i = pl.program_id(1)
        rows = pl.ds(pl.multiple_of(i * tm, tm), tm)
        parts = []
        for s in range(FFN_SUB_TILES):
            r = pl.ds(s * sub, sub)
            g = _add_all([_dot_nt(h2_ref[r, c0:c1], w[0]) for (c0, c1), w in zip(gu_cols, wgu_refs)])
            u = _add_all([_dot_nt(h2_ref[r, c0:c1], w[1]) for (c0, c1), w in zip(gu_cols, wgu_refs)])
            gs_ref[r, :] = g.astype(BF16)
            us_ref[r, :] = u.astype(BF16)
            a = (g * _sigmoid(g) * u).astype(BF16)
            parts.append(jnp.concatenate([_dot(a, w[...]) for w in wd_refs], axis=1))
        part = jnp.concatenate(parts, axis=0)

        @pl.when(k == 0)
        def _():
            acc[rows, :] = part

        @pl.when(jnp.logical_and(k > 0, k < last))
        def _():
            acc[rows, :] += part

        @pl.when(k == last)
        def _():
            f = acc[rows, :] + part
            f_ref[...] = f.astype(BF16)
            x2 = x1_ref[...] + _rms(f, gpost_ref[...])
            x2_ref[...] = x2
            h3_ref[...] = _rms(x2, gple_ref[...]).astype(BF16)

    def late(k, i):
        return (jnp.where(k == last, i, 0), 0)

    return _launch(
        body, name=f"fwd_ffn{layer}", grid=(FF_CHUNKS, nt),
        in_specs=[pl.BlockSpec((tm, D_MODEL), lambda k, i: (i, 0)), pl.BlockSpec((tm, D_MODEL), late)]
                 + [pl.BlockSpec((None, 2, FF_BLOCK, FFN_WEIGHT_COLS), lambda k, i, b=b: (k, 0, 0, b)) for _, b in wgu]
                 + [pl.BlockSpec((FF_BLOCK, FFN_WEIGHT_COLS), lambda k, i, b=b: (k, b)) for _, b in wd]
                 + [pl.BlockSpec((1, D_MODEL), lambda k, i: (0, 0))] * 2,
        out_specs=[pl.BlockSpec((None, tm, FF_BLOCK), lambda k, i: (k, i, 0)),
                   pl.BlockSpec((None, tm, FF_BLOCK), lambda k, i: (k, i, 0)),
                   pl.BlockSpec((tm, D_MODEL), late),
                   pl.BlockSpec((tm, D_MODEL), late),
                   pl.BlockSpec((tm, D_MODEL), late)],
        out_shape=[jax.ShapeDtypeStruct((FF_CHUNKS, T, FF_BLOCK), BF16),
                   jax.ShapeDtypeStruct((FF_CHUNKS, T, FF_BLOCK), BF16),
                   jax.ShapeDtypeStruct((T, D_MODEL), BF16),
                   jax.ShapeDtypeStruct((T, D_MODEL), F32),
                   jax.ShapeDtypeStruct((T, D_MODEL), BF16)],
        scratch_shapes=[pltpu.VMEM((T, D_MODEL), F32)],
        args=(h2, x1, *[w for w, _ in wgu], *[w for w, _ in wd], g_post, g_ple), vmem=VMEM_BIG, job=job)


def _fwd_ple_qkv(x2, h3, p, wgate, wproj, g_post, g_kv, g_mix, wkv, wq, job=None):
    T = x2.shape[0]
    nt = T // ROW_TILE

    def body(x2_ref, h3_ref, p_ref, wg_ref, wp_ref, gpost_ref, gkv_ref, gmix_ref, wkv_ref, wq_ref,
             x3_ref, z_ref, pe_ref, hk_ref, h1_ref, q_ref, kpad_ref, vpad_ref):
        z = _dot(h3_ref[...], wg_ref[...])
        pe = _dot(p_ref[...].astype(BF16), wp_ref[...])
        z_ref[...] = z.astype(BF16)
        pe_ref[...] = pe.astype(BF16)
        x3 = x2_ref[...] + _rms(pe * _sigmoid(z), gpost_ref[...])
        x3_ref[...] = x3
        r = _rstd(x3)
        hk = (x3 * r * gkv_ref[...]).astype(BF16)
        h1 = (x3 * r * gmix_ref[...]).astype(BF16)
        hk_ref[...] = hk
        h1_ref[...] = h1
        kv = _dot(hk, wkv_ref[...]).astype(BF16)
        q_ref[...] = _dot(h1, wq_ref[...]).astype(BF16)
        i = pl.program_id(0)

        @pl.when(i == 0)
        def _():
            kpad_ref[:ATT_BLOCK, :] = jnp.zeros((ATT_BLOCK, KV_DIM), BF16)
            vpad_ref[:ATT_BLOCK, :] = jnp.zeros((ATT_BLOCK, KV_DIM), BF16)

        rows = pl.ds(pl.multiple_of(ATT_BLOCK + i * ROW_TILE, ATT_BLOCK), ROW_TILE)
        kpad_ref[rows, :] = kv[:, :KV_DIM]
        vpad_ref[rows, :] = kv[:, KV_DIM:]

    wide = jax.ShapeDtypeStruct((T, D_MODEL), BF16)
    padded = (ATT_BLOCK + T, KV_DIM)
    return _launch(
        body, name="fwd_ple_qkv", grid=(nt,),
        in_specs=[_row_spec(D_MODEL), _row_spec(D_MODEL), _row_spec(PLE_DIM), _full_spec((D_MODEL, D_MODEL)),
                  _full_spec((PLE_DIM, D_MODEL)), _vec_spec(), _vec_spec(), _vec_spec(),
                  _full_spec((D_MODEL, 2 * KV_DIM)), _full_spec((D_MODEL, D_MODEL))],
        out_specs=[_row_spec(D_MODEL)] * 6 + [_full_spec(padded)] * 2,
        out_shape=[jax.ShapeDtypeStruct((T, D_MODEL), F32)] + [wide] * 5 + [jax.ShapeDtypeStruct(padded, BF16)] * 2,
        args=(x2, h3, p, wgate, wproj, g_post, g_kv, g_mix, wkv, wq), job=job)


def _alibi_slope(h):
    return 2.0 ** (-8.0 * (h + 1) / N_HEADS)


ATT_SUB = 32
ATT_GROUP_ROWS = GQA_GROUP * ATT_BLOCK


def _att_mask(n, rel_ref, off_ref):
    qi = lax.broadcasted_iota(jnp.int32, (ATT_BLOCK, 2 * ATT_BLOCK), 0)
    si = lax.broadcasted_iota(jnp.int32, (ATT_BLOCK, 2 * ATT_BLOCK), 1)
    rel = ATT_BLOCK + qi - si
    valid = (rel >= 0) & (rel < ATT_BLOCK) & ((si >= ATT_BLOCK) | (n > 0))
    rel_ref[...] = rel.astype(F32)
    off_ref[...] = jnp.where(valid, 0.0, NEG_INF)


def _att_probs(raw, relf, off, slope, sink):
    s = raw * ATT_SCALE - slope * relf + off
    m = jnp.maximum(jnp.max(s, axis=-1, keepdims=True), sink)
    e = jnp.exp(s - m)
    es = jnp.exp(sink - m)
    inv = 1.0 / (jnp.sum(e, axis=-1, keepdims=True) + es)
    return e * inv, es * inv


def _stack_heads(ref, kh):
    first = kh * GQA_GROUP
    return jnp.concatenate([ref[:, (first + g) * HEAD_DIM:(first + g + 1) * HEAD_DIM] for g in range(GQA_GROUP)], axis=0)


def _unstack_heads(stacked):
    return [stacked[g * ATT_BLOCK:(g + 1) * ATT_BLOCK, :] for g in range(GQA_GROUP)]


def _fwd_attention(q, kpad, vpad, sinks, job=None):
    T = q.shape[0]
    nb = T // ATT_BLOCK

    def body(q_ref, k_ref, v_ref, sink_ref, o_ref, s_scr, p_scr, rel_scr, off_scr):
        n = pl.program_id(0)
        start = pl.multiple_of(n * ATT_BLOCK, ATT_BLOCK)
        kw = k_ref[pl.ds(start, 2 * ATT_BLOCK), :]
        vw = v_ref[pl.ds(start, 2 * ATT_BLOCK), :]
        _att_mask(n, rel_scr, off_scr)
        outs = []
        for kh in range(N_KV_HEADS):
            kk = kw[:, kh * HEAD_DIM:(kh + 1) * HEAD_DIM]
            vv = vw[:, kh * HEAD_DIM:(kh + 1) * HEAD_DIM]
            s_scr[...] = _dot_nt(_stack_heads(q_ref, kh), kk)
            for g in range(GQA_GROUP):
                h = kh * GQA_GROUP + g
                for row0 in range(0, ATT_BLOCK, ATT_SUB):
                    rows, sub = pl.ds(g * ATT_BLOCK + row0, ATT_SUB), pl.ds(row0, ATT_SUB)
                    pr, _ = _att_probs(s_scr[rows, :], rel_scr[sub, :], off_scr[sub, :], _alibi_slope(h),
                                       sink_ref[0, h])
                    p_scr[rows, :] = pr.astype(BF16)
            outs += _unstack_heads(_dot(p_scr[...], vv))
        o_ref[...] = jnp.concatenate(outs, axis=1).astype(BF16)

    return _launch(
        body, name="fwd_attention", grid=(nb,),
        in_specs=[_row_spec(D_MODEL, ATT_BLOCK), _full_spec((T + ATT_BLOCK, KV_DIM)), _full_spec((T + ATT_BLOCK, KV_DIM)),
                  pl.BlockSpec(memory_space=pltpu.SMEM)],
        out_specs=[_row_spec(D_MODEL, ATT_BLOCK)],
        out_shape=[jax.ShapeDtypeStruct((T, D_MODEL), BF16)],
        scratch_shapes=[pltpu.VMEM((ATT_GROUP_ROWS, 2 * ATT_BLOCK), F32), pltpu.VMEM((ATT_GROUP_ROWS, 2 * ATT_BLOCK), BF16)]
                       + [pltpu.VMEM((ATT_BLOCK, 2 * ATT_BLOCK), F32)] * 2,
        args=(q, kpad, vpad, sinks), job=job)


def _fwd_attn_out(attn, x, wo, g_post, g_ffn, job=None):
    T = x.shape[0]
    nt = T // ROW_TILE

    def body(a_ref, x_ref, wo_ref, gpost_ref, gffn_ref, y_ref, x1_ref, h2_ref):
        y = _dot(a_ref[...], wo_ref[...])
        y_ref[...] = y.astype(BF16)
        x1 = x_ref[...] + _rms(y, gpost_ref[...])
        x1_ref[...] = x1
        h2_ref[...] = _rms(x1, gffn_ref[...]).astype(BF16)

    return _launch(
        body, name="fwd_attn_out", grid=(nt,),
        in_specs=[_row_spec(D_MODEL), _row_spec(D_MODEL), _full_spec((D_MODEL, D_MODEL)), _vec_spec(), _vec_spec()],
        out_specs=[_row_spec(D_MODEL)] * 3,
        out_shape=[jax.ShapeDtypeStruct((T, D_MODEL), BF16), jax.ShapeDtypeStruct((T, D_MODEL), F32),
                   jax.ShapeDtypeStruct((T, D_MODEL), BF16)],
        args=(attn, x, wo, g_post, g_ffn), job=job)


def _bwd_ple(layer, dx3, x2, z, pe, h3, p, f, wgate, g_ple_post, g_ple, g_post_ffn, job=None):
    T = x2.shape[0]
    tm = ROW_TILE
    nt = T // tm

    def body(dx3_ref, x2_ref, z_ref, pe_ref, h3_ref, p_ref, f_ref, wg_ref, gpp_ref, gp_ref, gpf_ref,
             dx2_ref, df_ref, dwg_ref, dwp_ref, dgpp_ref, dgp_ref, dgpf_ref, acc_g, acc_p):
        i = pl.program_id(0)
        first = i == 0
        dx3v = dx3_ref[...]
        gate = _sigmoid(z_ref[...].astype(F32))
        pev = pe_ref[...].astype(F32)
        de, dgpp = _rms_bwd(pev * gate, gpp_ref[...], dx3v)
        dpe = (de * gate).astype(BF16)
        dz = (de * pev * gate * (1.0 - gate)).astype(BF16)
        _acc(acc_p, _dot_tn(p_ref[...].astype(BF16), dpe), first)
        _acc(acc_g, _dot_tn(h3_ref[...], dz), first)
        dh3 = _dot_nt(dz, wg_ref[...])
        dxn, dgp = _rms_bwd(x2_ref[...], gp_ref[...], dh3)
        dx2 = dx3v + dxn
        dx2_ref[...] = dx2
        df, dgpf = _rms_bwd(f_ref[...].astype(F32), gpf_ref[...], dx2)
        df_ref[...] = df.astype(BF16)
        _acc(dgpp_ref, dgpp, first)
        _acc(dgp_ref, dgp, first)
        _acc(dgpf_ref, dgpf, first)

        @pl.when(i == nt - 1)
        def _():
            dwg_ref[...] = acc_g[...].astype(BF16)
            dwp_ref[...] = acc_p[...].astype(BF16)

    return _launch(
        body, name=f"bwd_ple{layer}", grid=(nt,),
        in_specs=[_row_spec(D_MODEL)] * 5 + [_row_spec(PLE_DIM), _row_spec(D_MODEL), _full_spec((D_MODEL, D_MODEL)),
                  _vec_spec(), _vec_spec(), _vec_spec()],
        out_specs=[_row_spec(D_MODEL), _row_spec(D_MODEL), _full_spec((D_MODEL, D_MODEL)), _full_spec((PLE_DIM, D_MODEL)),
                   _vec_spec(), _vec_spec(), _vec_spec()],
        out_shape=[jax.ShapeDtypeStruct((T, D_MODEL), F32), jax.ShapeDtypeStruct((T, D_MODEL), BF16),
                   jax.ShapeDtypeStruct((D_MODEL, D_MODEL), BF16), jax.ShapeDtypeStruct((PLE_DIM, D_MODEL), BF16)]
                  + [jax.ShapeDtypeStruct((1, D_MODEL), F32)] * 3,
        scratch_shapes=[pltpu.VMEM((D_MODEL, D_MODEL), F32), pltpu.VMEM((PLE_DIM, D_MODEL), F32)],
        args=(dx3, x2, z, pe, h3, p, f, wgate, g_ple_post, g_ple, g_post_ffn), vmem=VMEM_BIG, job=job)


def _ple_loss_bwd(layer, x2, h3, p, f, target, wgate, wproj, g_ple_post, g_ple, g_post_ffn, job=None):
    T = x2.shape[0]
    tm = ROW_TILE
    nt = T // tm

    def body(x2_ref, h3_ref, p_ref, f_ref, tgt_ref, wg_ref, wp_ref, gpp_ref, gp_ref, gpf_ref,
             dx2_ref, df_ref, dwg_ref, dwp_ref, dgpp_ref, dgp_ref, dgpf_ref, loss_ref, acc_g, acc_p):
        i = pl.program_id(0)
        first = i == 0
        h3 = h3_ref[...]
        pb = p_ref[...].astype(BF16)
        x2v = x2_ref[...]
        gate = _sigmoid(_dot(h3, wg_ref[...]))
        pev = _dot(pb, wp_ref[...])
        e = pev * gate
        err = x2v + _rms(e, gpp_ref[...]) - tgt_ref[...]
        _acc(loss_ref, 0.5 * jnp.sum(jnp.mean(err * err, axis=-1, keepdims=True), axis=0, keepdims=True), first)
        dx3v = err * (1.0 / D_MODEL)
        de, dgpp = _rms_bwd(e, gpp_ref[...], dx3v)
        dpe = (de * gate).astype(BF16)
        dz = (de * pev * gate * (1.0 - gate)).astype(BF16)
        _acc(acc_p, _dot_tn(pb, dpe), first)
        _acc(acc_g, _dot_tn(h3, dz), first)
        dxn, dgp = _rms_bwd(x2v, gp_ref[...], _dot_nt(dz, wg_ref[...]))
        dx2 = dx3v + dxn
        dx2_ref[...] = dx2
        df, dgpf = _rms_bwd(f_ref[...].astype(F32), gpf_ref[...], dx2)
        df_ref[...] = df.astype(BF16)
        _acc(dgpp_ref, dgpp, first)
        _acc(dgp_ref, dgp, first)
        _acc(dgpf_ref, dgpf, first)

        @pl.when(i == nt - 1)
        def _():
            dwg_ref[...] = acc_g[...].astype(BF16)
            dwp_ref[...] = acc_p[...].astype(BF16)

    return _launch(
        body, name=f"ple_loss_bwd{layer}", grid=(nt,),
        in_specs=[_row_spec(D_MODEL), _row_spec(D_MODEL), _row_spec(PLE_DIM), _row_spec(D_MODEL), _row_spec(D_MODEL),
                  _full_spec((D_MODEL, D_MODEL)), _full_spec((PLE_DIM, D_MODEL)), _vec_spec(), _vec_spec(), _vec_spec()],
        out_specs=[_row_spec(D_MODEL), _row_spec(D_MODEL), _full_spec((D_MODEL, D_MODEL)), _full_spec((PLE_DIM, D_MODEL)),
                   _vec_spec(), _vec_spec(), _vec_spec(), _full_spec((1, 1))],
        out_shape=[jax.ShapeDtypeStruct((T, D_MODEL), F32), jax.ShapeDtypeStruct((T, D_MODEL), BF16),
                   jax.ShapeDtypeStruct((D_MODEL, D_MODEL), BF16), jax.ShapeDtypeStruct((PLE_DIM, D_MODEL), BF16)]
                  + [jax.ShapeDtypeStruct((1, D_MODEL), F32)] * 3 + [jax.ShapeDtypeStruct((1, 1), F32)],
        scratch_shapes=[pltpu.VMEM((D_MODEL, D_MODEL), F32), pltpu.VMEM((PLE_DIM, D_MODEL), F32)],
        args=(x2, h3, p, f, target, wgate, wproj, g_ple_post, g_ple, g_post_ffn), vmem=VMEM_BIG, job=job)


def _bwd_ffn_act(layer, df, gs, us, wgu, wd, job=None):
    T = df.shape[0]
    tm = min(FFN_ROW_TILE, T)
    nt = T // tm
    sub = tm // FFN_SUB_TILES
    last = FF_CHUNKS - 1
    wgu, wd = _column_views(wgu), _column_views(wd)
    n_gu, n_wd = len(wgu), len(wd)
    wd_cols = _column_ranges(wd)

    def body(df_ref, gs_ref, us_ref, *refs):
        wgu_refs, wd_refs = refs[:n_gu], refs[n_gu:n_gu + n_wd]
        dh_ref, dg_ref, du_ref, a_ref, acc_h = refs[n_gu + n_wd:]
        k = pl.program_id(0)
        i = pl.program_id(1)
        rows = pl.ds(pl.multiple_of(i * tm, tm), tm)
        dhs = []
        for s in range(FFN_SUB_TILES):
            r = pl.ds(s * sub, sub)
            g = gs_ref[r, :].astype(F32)
            u = us_ref[r, :].astype(F32)
            sg = _sigmoid(g)
            silu = g * sg
            a_ref[r, :] = (silu * u).astype(BF16)
            da = _add_all([_dot_nt(df_ref[r, c0:c1], w[...]) for (c0, c1), w in zip(wd_cols, wd_refs)])
            dg = (da * u * (sg * (1.0 + g * (1.0 - sg)))).astype(BF16)
            du = (da * silu).astype(BF16)
            dg_ref[r, :] = dg
            du_ref[r, :] = du
            dhs.append(jnp.concatenate([_dot(dg, w[0]) + _dot(du, w[1]) for w in wgu_refs], axis=1))
        dh = jnp.concatenate(dhs, axis=0)

        @pl.when(k == 0)
        def _():
            acc_h[rows, :] = dh

        @pl.when(jnp.logical_and(k > 0, k < last))
        def _():
            acc_h[rows, :] += dh

        @pl.when(k == last)
        def _():
            dh_ref[...] = acc_h[rows, :] + dh

    chunk_rows = pl.BlockSpec((None, tm, FF_BLOCK), lambda k, i: (k, i, 0))
    saved = jax.ShapeDtypeStruct((FF_CHUNKS, T, FF_BLOCK), BF16)
    return _launch(
        body, name=f"bwd_ffn_act{layer}", grid=(FF_CHUNKS, nt),
        in_specs=[pl.BlockSpec((tm, D_MODEL), lambda k, i: (i, 0)), chunk_rows, chunk_rows]
                 + [pl.BlockSpec((None, 2, FF_BLOCK, FFN_WEIGHT_COLS), lambda k, i, b=b: (k, 0, 0, b)) for _, b in wgu]
                 + [pl.BlockSpec((FF_BLOCK, FFN_WEIGHT_COLS), lambda k, i, b=b: (k, b)) for _, b in wd],
        out_specs=[pl.BlockSpec((tm, D_MODEL), lambda k, i: (jnp.where(k == last, i, 0), 0)),
                   chunk_rows, chunk_rows, chunk_rows],
        out_shape=[jax.ShapeDtypeStruct((T, D_MODEL), F32), saved, saved, saved],
        scratch_shapes=[pltpu.VMEM((T, D_MODEL), F32)],
        args=(df, gs, us, *[w for w, _ in wgu], *[w for w, _ in wd]), vmem=VMEM_BIG, job=job)


def _bwd_ffn_dw(layer, q, parts, h2, df, dg, du, a, job=None):
    T = h2.shape[0]
    width = D_MODEL // parts

    def body(h_ref, df_ref, dg_ref, du_ref, a_ref, dgu_ref, dwd_ref):
        h = h_ref[...]
        dgu_ref[0] = _dot_tn(dg_ref[...], h).astype(BF16)
        dgu_ref[1] = _dot_tn(du_ref[...], h).astype(BF16)
        dwd_ref[...] = _dot_tn(a_ref[...], df_ref[...]).astype(BF16)

    cols = pl.BlockSpec((T, width), lambda k: (0, q))
    chunk = pl.BlockSpec((None, T, FF_BLOCK), lambda k: (k, 0, 0))
    return _launch(
        body, name=f"bwd_ffn_dw{layer}_{q}", grid=(FF_CHUNKS,),
        in_specs=[cols, cols, chunk, chunk, chunk],
        out_specs=[pl.BlockSpec((None, 2, FF_BLOCK, width), lambda k: (k, 0, 0, 0)),
                   pl.BlockSpec((FF_BLOCK, width), lambda k: (k, 0))],
        out_shape=[jax.ShapeDtypeStruct((FF_CHUNKS, 2, FF_BLOCK, width), BF16),
                   jax.ShapeDtypeStruct((D_FF, width), BF16)],
        args=(h2, df, dg, du, a), vmem=VMEM_BIG, job=job)


def _bwd_attn_out(dx2, dh2, x1, y, attn, wo, g_ffn, g_post, job=None):
    T = x1.shape[0]
    nt = T // ROW_TILE

    def body(dx2_ref, dh2_ref, x1_ref, y_ref, a_ref, wo_ref, gffn_ref, gpost_ref,
             dx1_ref, da_ref, dwo_ref, dgf_ref, dgp_ref, acc):
        i = pl.program_id(0)
        first = i == 0
        dxn, dgf = _rms_bwd(x1_ref[...], gffn_ref[...], dh2_ref[...])
        dx1 = dx2_ref[...] + dxn
        dx1_ref[...] = dx1
        dy, dgp = _rms_bwd(y_ref[...].astype(F32), gpost_ref[...], dx1)
        dyb = dy.astype(BF16)
        da_ref[...] = _dot_nt(dyb, wo_ref[...]).astype(BF16)
        _acc(acc, _dot_tn(a_ref[...], dyb), first)
        _acc(dgf_ref, dgf, first)
        _acc(dgp_ref, dgp, first)

        @pl.when(i == nt - 1)
        def _():
            dwo_ref[...] = acc[...].astype(BF16)

    return _launch(
        body, name="bwd_attn_out", grid=(nt,),
        in_specs=[_row_spec(D_MODEL)] * 5 + [_full_spec((D_MODEL, D_MODEL)), _vec_spec(), _vec_spec()],
        out_specs=[_row_spec(D_MODEL), _row_spec(D_MODEL), _full_spec((D_MODEL, D_MODEL)), _vec_spec(), _vec_spec()],
        out_shape=[jax.ShapeDtypeStruct((T, D_MODEL), F32), jax.ShapeDtypeStruct((T, D_MODEL), BF16),
                   jax.ShapeDtypeStruct((D_MODEL, D_MODEL), BF16)] + [jax.ShapeDtypeStruct((1, D_MODEL), F32)] * 2,
        scratch_shapes=[pltpu.VMEM((D_MODEL, D_MODEL), F32)],
        args=(dx2, dh2, x1, y, attn, wo, g_ffn, g_post), job=job)


def _bwd_attention(q, dattn, kpad, vpad, sinks, job=None):
    T = q.shape[0]
    nb = T // ATT_BLOCK

    def body(q_ref, do_ref, k_ref, v_ref, sink_ref, dq_ref, dkv_ref, ds_ref, dk_ref, dv_ref, s_scr, dp_scr, p_scr,
             dsb_scr, rel_scr, off_scr):
        n = pl.program_id(0)
        _att_mask(n, rel_scr, off_scr)

        @pl.when(n == 0)
        def _():
            dk_ref[...] = jnp.zeros_like(dk_ref)
            dv_ref[...] = jnp.zeros_like(dv_ref)
            ds_ref[...] = jnp.zeros_like(ds_ref)

        start = pl.multiple_of(n * ATT_BLOCK, ATT_BLOCK)
        win = pl.ds(start, 2 * ATT_BLOCK)
        kw = k_ref[win, :]
        vw = v_ref[win, :]
        lane = lax.broadcasted_iota(jnp.int32, (1, ATT_BLOCK), 1)
        dsink = jnp.zeros((1, ATT_BLOCK), F32)
        dqs, dks, dvs = [], [], []
        for kh in range(N_KV_HEADS):
            kk = kw[:, kh * HEAD_DIM:(kh + 1) * HEAD_DIM]
            vv = vw[:, kh * HEAD_DIM:(kh + 1) * HEAD_DIM]
            qs = _stack_heads(q_ref, kh)
            dos = _stack_heads(do_ref, kh)
            s_scr[...] = _dot_nt(qs, kk)
            dp_scr[...] = _dot_nt(dos, vv)
            for g in range(GQA_GROUP):
                h = kh * GQA_GROUP + g
                dsink_h = jnp.zeros((1, 1), F32)
                for row0 in range(0, ATT_BLOCK, ATT_SUB):
                    rows, sub = pl.ds(g * ATT_BLOCK + row0, ATT_SUB), pl.ds(row0, ATT_SUB)
                    pr, ps = _att_probs(s_scr[rows, :], rel_scr[sub, :], off_scr[sub, :], _alibi_slope(h),
                                        sink_ref[0, h])
                    dp = dp_scr[rows, :]
                    delta = jnp.sum(pr * dp, axis=-1, keepdims=True)
                    dsb_scr[rows, :] = (pr * (dp - delta) * ATT_SCALE).astype(BF16)
                    p_scr[rows, :] = pr.astype(BF16)
                    dsink_h = dsink_h - jnp.sum(ps * delta, axis=0, keepdims=True)
                dsink = dsink + jnp.where(lane == h, dsink_h, 0.0)
            dsb = dsb_scr[...]
            dqs += _unstack_heads(_dot(dsb, kk))
            dks.append(_dot_tn(dsb, qs))
            dvs.append(_dot_tn(p_scr[...], dos))
        dq_ref[...] = jnp.concatenate(dqs, axis=1).astype(BF16)
        dk_ref[win, :] += jnp.concatenate(dks, axis=1)
        dv_ref[win, :] += jnp.concatenate(dvs, axis=1)
        ds_ref[...] += dsink

        @pl.when(n == nb - 1)
        def _():
            dkv_ref[:, :KV_DIM] = dk_ref[ATT_BLOCK:, :].astype(BF16)
            dkv_ref[:, KV_DIM:] = dv_ref[ATT_BLOCK:, :].astype(BF16)

    return _launch(
        body, name="bwd_attention", grid=(nb,),
        in_specs=[_row_spec(D_MODEL, ATT_BLOCK), _row_spec(D_MODEL, ATT_BLOCK), _full_spec((T + ATT_BLOCK, KV_DIM)),
                  _full_spec((T + ATT_BLOCK, KV_DIM)), pl.BlockSpec(memory_space=pltpu.SMEM)],
        out_specs=[_row_spec(D_MODEL, ATT_BLOCK), _full_spec((T, 2 * KV_DIM)), _full_spec((1, ATT_BLOCK))],
        out_shape=[jax.ShapeDtypeStruct((T, D_MODEL), BF16), jax.ShapeDtypeStruct((T, 2 * KV_DIM), BF16),
                   jax.ShapeDtypeStruct((1, ATT_BLOCK), F32)],
        scratch_shapes=[pltpu.VMEM((T + ATT_BLOCK, KV_DIM), F32)] * 2
                       + [pltpu.VMEM((ATT_GROUP_ROWS, 2 * ATT_BLOCK), F32)] * 2
                       + [pltpu.VMEM((ATT_GROUP_ROWS, 2 * ATT_BLOCK), BF16)] * 2
                       + [pltpu.VMEM((ATT_BLOCK, 2 * ATT_BLOCK), F32)] * 2,
        args=(q, dattn, kpad, vpad, sinks), vmem=VMEM_BIG, job=job)


def _bwd_qkv(dxres, dq, dkv, x3, h1, hk, wq, wkv, g_mix, g_kv, job=None):
    T = x3.shape[0]
    nt = T // ROW_TILE

    def body(dxr_ref, dq_ref, dkv_ref, x_ref, h1_ref, hk_ref, wq_ref, wkv_ref, gmix_ref, gkv_ref,
             dx_ref, dwq_ref, dwkv_ref, dgm_ref, dgk_ref, acc_q, acc_kv):
        i = pl.program_id(0)
        first = i == 0
        dqv = dq_ref[...]
        dkvv = dkv_ref[...]
        xv = x_ref[...]
        d1, dgm = _rms_bwd(xv, gmix_ref[...], _dot_nt(dqv, wq_ref[...]))
        d2, dgk = _rms_bwd(xv, gkv_ref[...], _dot_nt(dkvv, wkv_ref[...]))
        dx_ref[...] = dxr_ref[...] + d1 + d2
        _acc(acc_q, _dot_tn(h1_ref[...], dqv), first)
        _acc(acc_kv, _dot_tn(hk_ref[...], dkvv), first)
        _acc(dgm_ref, dgm, first)
        _acc(dgk_ref, dgk, first)

        @pl.when(i == nt - 1)
        def _():
            dwq_ref[...] = acc_q[...].astype(BF16)
            dwkv_ref[...] = acc_kv[...].astype(BF16)

    return _launch(
        body, name="bwd_qkv", grid=(nt,),
        in_specs=[_row_spec(D_MODEL), _row_spec(D_MODEL), _row_spec(2 * KV_DIM), _row_spec(D_MODEL), _row_spec(D_MODEL),
                  _row_spec(D_MODEL), _full_spec((D_MODEL, D_MODEL)), _full_spec((D_MODEL, 2 * KV_DIM)), _vec_spec(),
                  _vec_spec()],
        out_specs=[_row_spec(D_MODEL), _full_spec((D_MODEL, D_MODEL)), _full_spec((D_MODEL, 2 * KV_DIM)), _vec_spec(),
                   _vec_spec()],
        out_shape=[jax.ShapeDtypeStruct((T, D_MODEL), F32), jax.ShapeDtypeStruct((D_MODEL, D_MODEL), BF16),
                   jax.ShapeDtypeStruct((D_MODEL, 2 * KV_DIM), BF16)] + [jax.ShapeDtypeStruct((1, D_MODEL), F32)] * 2,
        scratch_shapes=[pltpu.VMEM((D_MODEL, D_MODEL), F32), pltpu.VMEM((D_MODEL, 2 * KV_DIM), F32)],
        args=(dxres, dq, dkv, x3, h1, hk, wq, wkv, g_mix, g_kv), job=job)


def _bwd_pool_mixer(dx2, dh2, x1, x, yraw, d, wp, scale, g_ffn, g_post, g_pre, job=None):
    T = x.shape[0]
    tm = ROW_TILE
    nt = T // tm

    def body(dx2_ref, dh2_ref, x1_ref, x_ref, yraw_ref, d_ref, wp_ref, sc_ref, gffn_ref, gpost_ref, gpre_ref,
             dx_ref, dwp_ref, dsc_ref, dgf_ref, dgp_ref, dgm_ref, carry, acc):
        i = pl.program_id(0)
        first = i == 0
        tile = nt - 1 - i

        @pl.when(first)
        def _():
            carry[...] = jnp.zeros_like(carry)

        dxn, dgf = _rms_bwd(x1_ref[...], gffn_ref[...], dh2_ref[...])
        dx1 = dx2_ref[...] + dxn
        yraw = yraw_ref[...].astype(F32)
        sc = sc_ref[...]
        dy, dgp = _rms_bwd(yraw * sc, gpost_ref[...], dx1)
        dsc = jnp.sum(dy * yraw, axis=0, keepdims=True)
        dyb = (dy * sc).astype(BF16)
        dv = d_ref[...]
        dds = []
        for g in range(N_POOL_GROUPS):
            cols = slice(g * POOL_GROUP, (g + 1) * POOL_GROUP)
            dds.append(_dot_nt(dyb[:, cols], wp_ref[g]))
            _acc(acc.at[g], _dot_tn(dv[:, cols], dyb[:, cols]), first)
        dd = jnp.concatenate(dds, axis=1)
        e = dd / _pool_counts(tile * tm, tm)
        ext = jnp.concatenate([e, carry[...]], axis=0)
        carry[...] = e[:POOL_HALO, :]
        sums = _window_sums(ext, lambda k: tm + POOL_HALO - k)[:tm, :]
        dxm, dgm = _rms_bwd(x_ref[...], gpre_ref[...], sums - dd)
        dx_ref[...] = dx1 + dxm
        _acc(dsc_ref, dsc, first)
        _acc(dgf_ref, dgf, first)
        _acc(dgp_ref, dgp, first)
        _acc(dgm_ref, dgm, first)

        @pl.when(i == nt - 1)
        def _():
            dwp_ref[...] = acc[...].astype(BF16)

    rev = pl.BlockSpec((tm, D_MODEL), lambda i: (nt - 1 - i, 0))
    return _launch(
        body, name="bwd_pool_mixer", grid=(nt,),
        in_specs=[rev] * 6 + [_full_spec((N_POOL_GROUPS, POOL_GROUP, POOL_GROUP))] + [_vec_spec()] * 4,
        out_specs=[rev, _full_spec((N_POOL_GROUPS, POOL_GROUP, POOL_GROUP))] + [_vec_spec()] * 4,
        out_shape=[jax.ShapeDtypeStruct((T, D_MODEL), F32),
                   jax.ShapeDtypeStruct((N_POOL_GROUPS, POOL_GROUP, POOL_GROUP), BF16)]
                  + [jax.ShapeDtypeStruct((1, D_MODEL), F32)] * 4,
        scratch_shapes=[pltpu.VMEM((POOL_HALO, D_MODEL), F32), pltpu.VMEM((N_POOL_GROUPS, POOL_GROUP, POOL_GROUP), F32)],
        args=(dx2, dh2, x1, x, yraw, d, wp, scale, g_ffn, g_post, g_pre), job=job)


def _my_place():
    return lax.axis_index("x"), lax.axis_index("y"), lax.axis_index("c")


def _dev_index(px, py, pc):
    return 4 * px + 2 * py + pc


def _peer_by_relation(r):
    x, y, c = _my_place()
    return (x ^ ((r >> 2) & 1), y ^ ((r >> 1) & 1), c ^ (r & 1))


def _slot_pool(ref, j):
    return ref.at[:, pl.ds(pl.multiple_of(j * 32, 32), 32), :]


def _slot_scale(ref, j):
    return ref.at[:, pl.ds(pl.multiple_of(j * 128, 128), 128)]


def _slot_rows128(ref, j):
    return ref.at[pl.ds(pl.multiple_of(j * 128, 128), 128), :]


def _slot_gu(ref, j):
    return ref.at[j % FF_CHUNKS, j // FF_CHUNKS]


def _slot_wd(ref, j):
    return ref.at[pl.ds(pl.multiple_of(j * WD_ROWS, 16), WD_ROWS), :]


def _slot_cols128(ref, j):
    return ref.at[:, pl.ds(pl.multiple_of(j * 128, 128), 128)]


_GATHERED = {
    "pool": ((N_POOL_GROUPS, POOL_GROUP, POOL_GROUP), BF16, _slot_pool),
    "scale": ((1, D_MODEL), F32, _slot_scale),
    "kv": ((D_MODEL, 2 * KV_DIM), BF16, _slot_rows128),
    "q": ((D_MODEL, D_MODEL), BF16, _slot_rows128),
    "o": ((D_MODEL, D_MODEL), BF16, _slot_rows128),
    "gu": ((FF_CHUNKS, 2, FF_BLOCK, D_MODEL), BF16, _slot_gu),
    "wd": ((D_FF, D_MODEL), BF16, _slot_wd),
    "guh": ((FF_CHUNKS, 2, FF_BLOCK, D_MODEL // 2), BF16, _slot_gu),
    "wdh": ((D_FF, D_MODEL // 2), BF16, _slot_wd),
    "gate": ((D_MODEL, D_MODEL), BF16, _slot_rows128),
    "proj": ((PLE_DIM, D_MODEL), BF16, _slot_cols128),
}


def _no_compute():
    pass


class _AllGather:
    peers = ("sibling", "x", "y")

    def __init__(self, names, shards):
        self.kinds = [_GATHERED[n.rstrip("01_")] for n in names]
        entries = [shards[n] if isinstance(shards[n], tuple) else (shards[n], None, None) for n in names]
        self.args = [array for array, _, _ in entries]
        self.layers = [layer for _, layer, _ in entries]
        self.columns = [columns for _, _, columns in entries]
        self.out_shape = [jax.ShapeDtypeStruct(shape, dtype) for shape, dtype, _ in self.kinds]
        n = len(names)
        self.scratch = [pltpu.SemaphoreType.DMA((n, 7)), pltpu.SemaphoreType.DMA((n, 7)), pltpu.SemaphoreType.DMA((n,))]

    def _plan(self, srcs, outs, sems):
        send_sems, recv_sems, local_sems = sems
        x, y, c = _my_place()

        def slot(t, dev):
            return self.kinds[t][2](outs[t], _dev_index(*dev))

        def copy(t, k, block, to, src=None):
            return pltpu.make_async_remote_copy(
                src_ref=slot(t, block) if src is None else src, dst_ref=slot(t, block),
                send_sem=send_sems.at[t, k], recv_sem=recv_sems.at[t, k], device_id=to, device_id_type=MESH)

        return types.SimpleNamespace(
            copy=copy, core=c, me=(x, y, c), sibling=(x, y, 1 - c),
            x_chip=(1 - x, y), y_chip=(x, 1 - y), far_chip=(1 - x, 1 - y),
            via=(x ^ (1 - c), y ^ c),
            onto=(x ^ c, y ^ (1 - c)),
            k_via=1 + c, k_onto=2 - c,
            local=[pltpu.make_async_copy(self._shard(srcs, t), slot(t, (x, y, c)), local_sems.at[t])
                   for t in range(len(srcs))])

    def _shard(self, srcs, t):
        shard = srcs[t] if self.layers[t] is None else srcs[t].at[self.layers[t]]
        if self.columns[t] is None:
            return shard
        first, end = self.columns[t]
        return shard.at[:, first:end]

    def start(self, srcs, outs, sems):
        p = self._plan(srcs, outs, sems)
        for cp in p.local:
            cp.start()
        for t in range(len(srcs)):
            shard = self._shard(srcs, t)
            p.copy(t, 0, p.me, p.sibling, src=shard).start()
            p.copy(t, 1, p.me, (*p.x_chip, p.core), src=shard).start()
            p.copy(t, 2, p.me, (*p.y_chip, p.core), src=shard).start()

    def mid(self, srcs, outs, sems):
        p = self._plan(srcs, outs, sems)
        for t in range(len(srcs)):
            block = (*p.via, p.core)
            p.copy(t, p.k_via, block, p.me).wait_recv()
            p.copy(t, 3, block, (*p.onto, p.core)).start()
            p.copy(t, 3 + p.k_via, block, p.sibling).start()

    def late(self, srcs, outs, sems):
        p = self._plan(srcs, outs, sems)
        n = len(srcs)
        for t in range(n):
            block = (*p.onto, p.core)
            p.copy(t, p.k_onto, block, p.me).wait_recv()
            p.copy(t, 3 + p.k_onto, block, p.sibling).start()
        for t in range(n):
            block = (*p.far_chip, p.core)
            p.copy(t, 3, block, p.me).wait_recv()
            p.copy(t, 6, block, p.sibling).start()

    def finish(self, srcs, outs, sems):
        p = self._plan(srcs, outs, sems)
        n = len(srcs)
        other = 1 - p.core
        for t in range(n):
            p.copy(t, 0, (*p.me[:2], other), p.me).wait_recv()
            for k, chip in ((4, p.x_chip), (5, p.y_chip), (6, p.far_chip)):
                p.copy(t, k, (*chip, other), p.me).wait_recv()
            for k in range(7):
                p.copy(t, k, p.me, p.sibling).wait_send()
        for cp in p.local:
            cp.wait()


def _jobs_only(name, job=None):
    return _launch(_no_compute, name=name, grid=(), in_specs=[], out_specs=[], out_shape=[], args=(), job=job)


def _block_pool(ref, j):
    return ref.at[:, pl.ds(pl.multiple_of(j * 32, 32), 32), :]


def _block_rows128(ref, j):
    return ref.at[pl.ds(pl.multiple_of(j * 128, 128), 128), :]


def _block_gu(ref, j):
    return ref.at[j % FF_CHUNKS, j // FF_CHUNKS]


def _block_wd(ref, j):
    return ref.at[pl.ds(pl.multiple_of(j * WD_ROWS, 16), WD_ROWS), :]


def _block_cols128(ref, j):
    return ref.at[:, pl.ds(pl.multiple_of(j * 128, 128), 128)]


_SCATTERED = {
    "pool": ((N_POOL_GROUPS, 32, POOL_GROUP), _block_pool),
    "kv": ((128, 2 * KV_DIM), _block_rows128),
    "q": ((128, D_MODEL), _block_rows128),
    "o": ((128, D_MODEL), _block_rows128),
    "gu": ((FF_BLOCK, FF_PART), _block_gu),
    "wd": ((WD_ROWS, FF_PART), _block_wd),
    "guA": ((FF_BLOCK, FF_PART), lambda ref, j: _block_gu(ref, j).at[:, :FF_PART]),
    "guB": ((FF_BLOCK, FF_PART), lambda ref, j: _block_gu(ref, j).at[:, FF_PART:]),
    "wdA": ((WD_ROWS, FF_PART), lambda ref, j: _block_wd(ref, j).at[:, :FF_PART]),
    "wdB": ((WD_ROWS, FF_PART), lambda ref, j: _block_wd(ref, j).at[:, FF_PART:]),
    "gate": ((128, D_MODEL), _block_rows128),
    "proj": ((PLE_DIM, 128), _block_cols128),
}


class _SiblingSwap:
    peers = ("sibling",)

    def __init__(self, pieces):
        self.kinds = [_SCATTERED[kind] for kind, _ in pieces]
        self.args = [g for _, g in pieces]
        self.out_shape = [jax.ShapeDtypeStruct((N_CHIPS, *block), BF16) for block, _ in self.kinds]
        n = len(pieces)
        self.scratch = [pltpu.SemaphoreType.DMA((n, N_CHIPS)), pltpu.SemaphoreType.DMA((n, N_CHIPS))]

    def _copies(self, srcs, outs, sems):
        send_sems, recv_sems = sems
        x, y, c = _my_place()
        return [pltpu.make_async_remote_copy(
            src_ref=block(srcs[t], 2 * ch + 1 - c), dst_ref=outs[t].at[ch], send_sem=send_sems.at[t, ch],
            recv_sem=recv_sems.at[t, ch], device_id=(x, y, 1 - c), device_id_type=MESH)
            for t, (_, block) in enumerate(self.kinds) for ch in range(N_CHIPS)]

    def start(self, srcs, outs, sems):
        for cp in self._copies(srcs, outs, sems):
            cp.start()

    def finish(self, srcs, outs, sems):
        for cp in self._copies(srcs, outs, sems):
            cp.wait()


class _ChipScatter:
    N_BUFS = 4
    peers = ("x", "y")

    def __init__(self, pieces):
        self.kinds = [_SCATTERED[kind] for kind, _, _ in pieces]
        self.n = n = len(pieces)
        self.args = [g for _, g, _ in pieces] + [s for _, _, s in pieces]
        self.out_shape = [jax.ShapeDtypeStruct((2, *block), BF16) for block, _ in self.kinds]
        self.scratch = []
        for block, _ in self.kinds:
            self.scratch += [pltpu.VMEM((N_CHIPS, *block), BF16)] * 3 + [pltpu.VMEM((2, *block), BF16)]
        dma = pltpu.SemaphoreType.DMA
        self.scratch += [dma((n, N_CHIPS + 1)), dma((n, 2)), dma((n, 2)), dma((n,)), dma((n,)), dma((n,))]

    def _plan(self, outs, scr):
        n = self.n
        first_send, first_recv, second_send, second_recv, keep_sems = scr[self.N_BUFS * n + 1:]
        x, y, c = _my_place()
        via = (x ^ (1 - c), y ^ c)
        onto = (x ^ c, y ^ (1 - c))
        index = lambda chip: 2 * chip[0] + chip[1]
        first, second, keep = [], [], []
        for t in range(n):
            total, inbox = scr[self.N_BUFS * t + 2], scr[self.N_BUFS * t + 3]
            for k, chip in enumerate((via, (1 - x, 1 - y))):
                first.append(pltpu.make_async_remote_copy(
                    src_ref=total.at[index(chip)], dst_ref=inbox.at[k], send_sem=first_send.at[t, k],
                    recv_sem=first_recv.at[t, k], device_id=(*via, c), device_id_type=MESH))
            second.append(pltpu.make_async_remote_copy(
                src_ref=total.at[index(onto)], dst_ref=outs[t].at[1], send_sem=second_send.at[t],
                recv_sem=second_recv.at[t], device_id=(*onto, c), device_id_type=MESH))
            keep.append(pltpu.make_async_copy(total.at[index((x, y))], outs[t].at[0], keep_sems.at[t]))
        return first, second, keep, index((x, y)), index(onto), index(via), index((1 - x, 1 - y))

    def start(self, ins, outs, scr):
        n = self.n
        load_sems = scr[self.N_BUFS * n]
        c = lax.axis_index("c")
        loads = []
        for t, (_, block) in enumerate(self.kinds):
            mine, theirs = scr[self.N_BUFS * t], scr[self.N_BUFS * t + 1]
            loads += [pltpu.make_async_copy(block(ins[t], 2 * ch + c), mine.at[ch], load_sems.at[t, ch])
                      for ch in range(N_CHIPS)]
            loads.append(pltpu.make_async_copy(ins[n + t], theirs, load_sems.at[t, N_CHIPS]))
        for cp in loads:
            cp.start()
        for cp in loads:
            cp.wait()
        first, _, _, me, onto, via, far = self._plan(outs, scr)

        def add_cores(slots):
            for t in range(n):
                mine, theirs, total = scr[self.N_BUFS * t:self.N_BUFS * t + 3]
                for slot in slots:
                    total[slot] = (mine[slot].astype(F32) + theirs[slot].astype(F32)).astype(BF16)

        add_cores((via, far))
        for cp in first:
            cp.start()
        add_cores((me, onto))

    def mid(self, ins, outs, scr):
        first, second, keep, me, onto, _, _ = self._plan(outs, scr)
        for cp in first:
            cp.wait_recv()
        for k, slot, copies in ((1, onto, second), (0, me, keep)):
            for t in range(self.n):
                total, inbox = scr[self.N_BUFS * t + 2], scr[self.N_BUFS * t + 3]
                total[slot] = (total[slot].astype(F32) + inbox[k].astype(F32)).astype(BF16)
            for cp in copies:
                cp.start()

    def finish(self, ins, outs, scr):
        first, second, keep, _, _, _, _ = self._plan(outs, scr)
        for cp in first:
            cp.wait_send()
        for cp in second + keep:
            cp.wait()


class _ToEveryone:
    peers = _EVERYONE

    def __init__(self, scattered=(), gathered=()):
        self.blocks = [_SCATTERED[kind][1] for kind, _ in scattered] + [None] * len(gathered)
        self.args = [g for _, g in scattered] + list(gathered)
        self.out_shape = [jax.ShapeDtypeStruct((N_DEV, *_SCATTERED[kind][0]), BF16) for kind, _ in scattered]
        self.out_shape += [jax.ShapeDtypeStruct((N_DEV, *a.shape), a.dtype) for a in gathered]
        n = len(self.args)
        self.scratch = [pltpu.SemaphoreType.DMA((n, N_DEV - 1)), pltpu.SemaphoreType.DMA((n, N_DEV - 1)),
                        pltpu.SemaphoreType.DMA((n,))]

    def _copies(self, srcs, outs, sems):
        send_sems, recv_sems, local_sems = sems
        me = _dev_index(*_my_place())
        copies = []
        for t, block in enumerate(self.blocks):
            part = (lambda j, t=t, block=block: srcs[t] if block is None else block(srcs[t], j))
            copies.append(pltpu.make_async_copy(part(me), outs[t].at[me], local_sems.at[t]))
            for r in range(1, N_DEV):
                peer = _peer_by_relation(r)
                copies.append(pltpu.make_async_remote_copy(
                    src_ref=part(_dev_index(*peer)), dst_ref=outs[t].at[me], send_sem=send_sems.at[t, r - 1],
                    recv_sem=recv_sems.at[t, r - 1], device_id=peer, device_id_type=MESH))
        return copies

    def start(self, srcs, outs, sems):
        for cp in self._copies(srcs, outs, sems):
            cp.start()

    def finish(self, srcs, outs, sems):
        for cp in self._copies(srcs, outs, sems):
            cp.wait()


class _Jobs:
    def __init__(self, *jobs):
        self.jobs = jobs
        together = {p for j in jobs for p in j.peers}
        self.peers = tuple(p for p in _EVERYONE if p in together)
        self.args = [a for j in jobs for a in j.args]
        self.out_shape = [o for j in jobs for o in j.out_shape]
        self.scratch = [s for j in jobs for s in j.scratch]

    def _split(self, refs, attr):
        at = 0
        for j in self.jobs:
            n = len(getattr(j, attr))
            yield refs[at:at + n]
            at += n

    def _each(self, ins, outs, scr):
        return zip(self.jobs, self._split(ins, "args"), self._split(outs, "out_shape"), self._split(scr, "scratch"))

    def start(self, ins, outs, scr):
        for j, i, o, s in self._each(ins, outs, scr):
            j.start(i, o, s)

    def mid(self, ins, outs, scr):
        for j, i, o, s in self._each(ins, outs, scr):
            if hasattr(j, "mid"):
                j.mid(i, o, s)

    def late(self, ins, outs, scr):
        for j, i, o, s in self._each(ins, outs, scr):
            if hasattr(j, "late"):
                j.late(i, o, s)

    def finish(self, ins, outs, scr):
        for j, i, o, s in self._each(ins, outs, scr):
            j.finish(i, o, s)

    def split_outputs(self, outs):
        return list(self._split(outs, "out_shape"))


def _adamw_math(w, g, m, v):
    m = ADAM_B1 * m + (1.0 - ADAM_B1) * g
    v = ADAM_B2 * v + (1.0 - ADAM_B2) * (g * g)
    m_hat = m / (1.0 - ADAM_B1 ** ADAM_STEP)
    v_hat = v / (1.0 - ADAM_B2 ** ADAM_STEP)
    delta = -ADAM_LR * (m_hat / (jnp.sqrt(v_hat) + ADAM_EPS) + ADAM_WD * w)
    return delta, m, v


def _adamw(name, w, m, v, landings, n_col_blocks=1, job=None):
    n_slots, r, c = landings[0].shape
    grid = (w.shape[0] // r, n_col_blocks)

    def body(w_ref, m_ref, v_ref, *rest):
        l_refs, (g_ref, d_ref, nm_ref, nv_ref) = rest[:len(landings)], rest[len(landings):]
        step = pl.program_id(0) * n_col_blocks + pl.program_id(1)
        for idx, l_ref in enumerate(l_refs):
            @pl.when(step == idx)
            def _(l_ref=l_ref):
                g = l_ref[0].astype(F32)
                for s in range(1, n_slots):
                    g = g + l_ref[s].astype(F32)
                g_ref[...] = g
                d_ref[...], nm_ref[...], nv_ref[...] = _adamw_math(w_ref[...], g, m_ref[...], v_ref[...])

    spec = pl.BlockSpec((r, c), lambda a, b: (a, b))
    return _launch(
        body, name=f"adamw_{name}", grid=grid,
        in_specs=[spec, spec, spec] + [_full_spec((n_slots, r, c))] * len(landings),
        out_specs=[spec] * 4, out_shape=[jax.ShapeDtypeStruct(w.shape, F32)] * 4,
        args=(w, m, v, *landings), vmem=VMEM_BIG, job=job)


_SMALL = (("pre_mix_g", SV_PRE_MIX, 2), ("post_mix_g", SV_POST_MIX, 2), ("pre_ffn_g", SV_PRE_FFN, 2),
          ("post_ffn_g", SV_POST_FFN, 2), ("ple_g", SV_PLE, 2), ("ple_post_g", SV_PLE_POST, 2), ("kv_g", SV_KV, 1),
          ("pool_scale", SV_POOL_SCALE, 1), ("sinks", SV_SINKS, 1))


def _adamw_several(items):
    counts = [len(landings) for _, _, _, landings in items]
    args = [a for w, m, v, landings in items for a in (w, m, v, *landings)]
    out_shape = [jax.ShapeDtypeStruct(w.shape, F32) for w, _, _, _ in items for _ in range(4)]

    def body(*refs):
        ins, outs = refs[:len(args)], refs[len(args):]
        at = 0
        for idx, n_landings in enumerate(counts):
            w_ref, m_ref, v_ref = ins[at:at + 3]
            l_refs = ins[at + 3:at + 3 + n_landings]
            at += 3 + n_landings
            g_ref, d_ref, nm_ref, nv_ref = outs[4 * idx:4 * idx + 4]
            for part, l_ref in enumerate(l_refs):
                rows = slice(part * l_ref.shape[1], (part + 1) * l_ref.shape[1])
                g = l_ref[0].astype(F32)
                for s in range(1, l_ref.shape[0]):
                    g = g + l_ref[s].astype(F32)
                g_ref[rows, :] = g
                d_ref[rows, :], nm_ref[rows, :], nv_ref[rows, :] = _adamw_math(
                    w_ref[rows, :], g, m_ref[rows, :], v_ref[rows, :])

    res, _ = _launch(
        body, name="adamw_several", grid=(1,), in_specs=[_full_spec(a.shape) for a in args],
        out_specs=[_full_spec(s.shape) for s in out_shape], out_shape=out_shape, args=args)
    return [res[4 * idx:4 * idx + 4] for idx in range(len(items))]


def _small_adamw(slabs, params):
    flat = [a for name, _, _ in _SMALL for a in params[name]]
    n_in = 1 + len(flat)

    def body(*refs):
        slabs_ref, wmv = refs[0], refs[1:n_in]
        loss_ref, outs, total = refs[n_in], refs[n_in + 1:-1], refs[-1]
        me = _dev_index(*_my_place())
        g = slabs_ref[0]
        for s in range(1, N_DEV):
            g = g + slabs_ref[s]
        total[...] = g
        loss_ref[...] = total[SV_LOSS:SV_LOSS + 1, 0:1]
        for idx, (name, row, n_rows) in enumerate(_SMALL):
            w_ref, m_ref, v_ref = wmv[3 * idx:3 * idx + 3]
            g_ref, d_ref, nm_ref, nv_ref = outs[4 * idx:4 * idx + 4]
            if name == "pool_scale":
                g = total[row:row + 1, pl.ds(pl.multiple_of(me * 128, 128), 128)]
            else:
                g = total[row:row + n_rows, 0:w_ref.shape[1]]
            g_ref[...] = g
            d_ref[...], nm_ref[...], nv_ref[...] = _adamw_math(w_ref[...], g, m_ref[...], v_ref[...])

    out_shape = [jax.ShapeDtypeStruct((1, 1), F32)]
    for name, _, _ in _SMALL:
        out_shape += [jax.ShapeDtypeStruct(params[name][0].shape, F32)] * 4
    res, _ = _launch(
        body, name="small_adamw", grid=(1,),
        in_specs=[_full_spec(a.shape) for a in (slabs, *flat)], out_specs=[_full_spec(s.shape) for s in out_shape],
        out_shape=out_shape, scratch_shapes=[pltpu.VMEM((SV_ROWS, D_MODEL), F32)], args=(slabs, *flat))
    return res[0], {name: res[1 + 4 * idx:5 + 4 * idx] for idx, (name, _, _) in enumerate(_SMALL)}


def _local_step(x, p, tgt, gains, sinks, shards, weights):
    row = lambda first_row, layer: _Gain(gains, first_row + layer)
    gather = lambda *names: _AllGather(names, shards)
    g_pre_mix, g_post_mix, g_pre_ffn, g_post_ffn = SV_PRE_MIX, SV_POST_MIX, SV_PRE_FFN, SV_POST_FFN
    g_ple, g_ple_post, g_kv = SV_PLE, SV_PLE_POST, _Gain(gains, SV_KV)

    (dpool,), (wp, scale, wgu0, wd0) = _fwd_pool(x, row(g_pre_mix, 0), job=gather("pool", "scale", "gu0", "wd0"))
    wgu0, wd0 = [wgu0], [wd0]
    (x1_0, h2_0, yraw), _ = _fwd_pool_mixer(x, dpool, wp, scale, row(g_post_mix, 0), row(g_pre_ffn, 0))
    (gs0, us0, f0, x2_0, h3_0), (wgate0, wproj0, wkv, wq, wo, wd1_a) = _fwd_ffn(
        0, h2_0, x1_0, wgu0, wd0, row(g_post_ffn, 0), row(g_ple, 0),
        job=gather("gate0", "proj0", "kv", "q", "o", "wdh1_0"))
    (x3_0, z0, pe0, hk, h1, q, kpad, vpad), (wgu1_a,) = _fwd_ple_qkv(
        x2_0, h3_0, p[0], wgate0, wproj0, row(g_ple_post, 0), g_kv, row(g_pre_mix, 1), wkv, wq,
        job=gather("guh1_0"))
    (attn,), (wgu1_b,) = _fwd_attention(q, kpad, vpad, sinks, job=gather("guh1_1"))
    (y1, x1_1, h2_1), (wd1_b,) = _fwd_attn_out(attn, x3_0, wo, row(g_post_mix, 1), row(g_pre_ffn, 1),
                                               job=gather("wdh1_1"))
    wgu1, wd1 = [wgu1_a, wgu1_b], [wd1_a, wd1_b]
    (gs1, us1, f1, x2_1, h3_1), (wgate1, wproj1) = _fwd_ffn(
        1, h2_1, x1_1, wgu1, wd1, row(g_post_ffn, 1), row(g_ple, 1), job=gather("gate1", "proj1"))

    produced, swapped, landed = {}, {}, {}

    def kind_of(name):
        return name.rstrip("0123_")

    def hosted(call, *args, swap=(), spread=(), extra=None):
        jobs = []
        if swap:
            jobs.append(_SiblingSwap([(kind_of(n), produced[n]) for n in swap]))
        if spread:
            jobs.append(_ChipScatter([(kind_of(n), produced[n], swapped[n]) for n in spread]))
        if extra is not None:
            jobs.append(extra)
        jobs = _Jobs(*jobs)
        outs, job_outs = call(*args, job=jobs)
        parts = jobs.split_outputs(job_outs)
        if swap:
            swapped.update(zip(swap, parts.pop(0)))
        if spread:
            landed.update(zip(spread, parts.pop(0)))
        return outs if extra is None else (outs, parts.pop(0))

    ffn_q = lambda layer, qtr: (f"gu{layer}_{qtr}", f"wd{layer}_{qtr}")

    dx2_1, df1, produced["gate1"], produced["proj1"], dg_ple_post1, dg_ple1, dg_post_ffn1, loss = hosted(
        _ple_loss_bwd, 1, x2_1, h3_1, p[1], f1, tgt, wgate1, wproj1, row(g_ple_post, 1), row(g_ple, 1),
        row(g_post_ffn, 1))
    dh2_1, dg1, du1, a1 = hosted(_bwd_ffn_act, 1, df1, gs1, us1, wgu1, wd1, swap=("gate1", "proj1"))
    dgu1, dwd1 = hosted(_bwd_ffn_dw, 1, 0, 1, h2_1, df1, dg1, du1, a1, spread=("gate1", "proj1"))
    produced.update(guA1=dgu1, guB1=dgu1, wdA1=dwd1, wdB1=dwd1)
    dx1_1, dattn, produced["o"], dg_pre_ffn1, dg_post_mix1 = hosted(
        _bwd_attn_out, dx2_1, dh2_1, x1_1, y1, attn, wo, row(g_pre_ffn, 1), row(g_post_mix, 1),
        swap=("guA1", "wdA1", "guB1", "wdB1"))
    dq, dkv, dsinks = hosted(_bwd_attention, q, dattn, kpad, vpad, sinks, spread=("guA1", "wdA1"))
    dx3_0, produced["q"], produced["kv"], dg_pre_mix1, dg_kv = hosted(
        _bwd_qkv, dx1_1, dq, dkv, x3_0, h1, hk, wq, wkv, row(g_pre_mix, 1), g_kv, swap=("o",), spread=("wdB1",))
    dx2_0, df0, produced["gate0"], produced["proj0"], dg_ple_post0, dg_ple0, dg_post_ffn0 = hosted(
        _bwd_ple, 0, dx3_0, x2_0, z0, pe0, h3_0, p[0], f0, wgate0, row(g_ple_post, 0), row(g_ple, 0),
        row(g_post_ffn, 0), swap=("q", "kv"), spread=("guB1",))
    for half, letter in enumerate("AB"):
        landed[f"gu1_{half}"], landed[f"wd1_{half}"] = landed[f"gu{letter}1"], landed[f"wd{letter}1"]
    dh2_0, dg0, du0, a0 = hosted(_bwd_ffn_act, 0, df0, gs0, us0, wgu0, wd0,
                                 swap=("gate0", "proj0"), spread=("o", "q", "kv"))
    part_hosts = [dict(spread=("gate0", "proj0")), dict(swap=ffn_q(0, 0))]
    for part in range(FF_PARTS):
        produced[f"gu0_{part}"], produced[f"wd0_{part}"] = hosted(
            _bwd_ffn_dw, 0, part, FF_PARTS, h2_0, df0, dg0, du0, a0, **part_hosts[part])
    grad_x, produced["pool"], dscale, dg_pre_ffn0, dg_post_mix0, dg_pre_mix0 = hosted(
        _bwd_pool_mixer, dx2_0, dh2_0, x1_0, x, yraw, dpool, wp, scale, row(g_pre_ffn, 0), row(g_post_mix, 0),
        row(g_pre_mix, 0), swap=ffn_q(0, 1), spread=ffn_q(0, 0))

    def update(name, n_col_blocks, pieces):
        w, m, v = weights[name]
        rows = w.size // w.shape[-1]
        flat = [landed[n].reshape(landed[n].shape[0], -1, landed[n].shape[-1]) for n in pieces]
        outs, _ = _adamw(name, w.reshape(rows, -1), m.reshape(rows, -1), v.reshape(rows, -1), flat, n_col_blocks)
        return [o.reshape(w.shape) for o in outs]

    upd = {}
    lanes = lambda a: jnp.pad(a, ((0, 0), (0, D_MODEL - a.shape[1])))
    small = jnp.concatenate([
        dg_pre_mix0, dg_pre_mix1, dg_post_mix0, dg_post_mix1, dg_pre_ffn0, dg_pre_ffn1, dg_post_ffn0, dg_post_ffn1,
        dg_ple0, dg_ple1, dg_ple_post0, dg_ple_post1, dg_kv, dscale, lanes(dsinks[:, :N_HEADS]), lanes(loss)], axis=0)

    everyone = _ToEveryone(scattered=[("pool", produced["pool"])], gathered=[small])
    _, (landed["pool"], slabs) = hosted(_jobs_only, "scatter_tail", spread=ffn_q(0, 1), extra=everyone)
    several = {"w_ple_gate": ("gate0", "gate1"), "w_ple_proj": ("proj0", "proj1"), "w_q": ("q",), "w_kv": ("kv",),
               "w_o": ("o",), "pool_w": ("pool",)}
    flat2d = lambda a: a.reshape(-1, a.shape[-1])
    results = _adamw_several([
        (*map(flat2d, weights[name]),
         [landed[n].reshape(landed[n].shape[0], -1, landed[n].shape[-1]) for n in pieces])
        for name, pieces in several.items()])
    for name, outs in zip(several, results):
        upd[name] = [o.reshape(weights[name][0].shape) for o in outs]
    upd["w_gu"] = update("w_gu", FF_PARTS,
                         pieces=[f"gu{layer}_{qtr}" for layer in range(2) for qtr in range(FF_PARTS)])
    upd["w_gu"] = [jnp.swapaxes(a, 1, 2) for a in upd["w_gu"]]
    upd["w_down"] = update("w_down", FF_PARTS,
                           pieces=[f"wd{layer}_{qtr}" for layer in range(2) for qtr in range(FF_PARTS)])
    return grad_x, upd, slabs


def kernel(x, p, pre_mix_g, post_mix_g, pre_ffn_g, post_ffn_g, pool_w, pool_scale, kv_g, w_kv, w_q, sinks, w_o, w_gu, w_down, ple_g, w_ple_gate, w_ple_proj, ple_post_g, loss_target, m_pre_mix_g, m_post_mix_g, m_pre_ffn_g, m_post_ffn_g, m_pool_w, m_pool_scale, m_kv_g, m_w_kv, m_w_q, m_sinks, m_w_o, m_w_gu, m_w_down, m_ple_g, m_w_ple_gate, m_w_ple_proj, m_ple_post_g, v_pre_mix_g, v_post_mix_g, v_pre_ffn_g, v_post_ffn_g, v_pool_w, v_pool_scale, v_kv_g, v_w_kv, v_w_q, v_sinks, v_w_o, v_w_gu, v_w_down, v_ple_g, v_w_ple_gate, v_w_ple_proj, v_ple_post_g):
    shards = {"pool": pool_w[0].astype(BF16), "scale": pool_scale, "kv": w_kv.astype(BF16),
              "q": w_q[0].astype(BF16), "o": w_o[0].astype(BF16)}
    gu, wd = jnp.swapaxes(w_gu, 1, 2).astype(BF16), w_down.astype(BF16)
    gate, proj = w_ple_gate.astype(BF16), w_ple_proj.astype(BF16)
    for layer in range(2):
        shards[f"gu{layer}"] = (gu, layer, None)
        shards[f"wd{layer}"] = (wd, layer, None)
        for half in range(2):
            cols = (half * D_MODEL // 2, (half + 1) * D_MODEL // 2)
            shards[f"guh{layer}_{half}"] = (gu, layer, cols)
            shards[f"wdh{layer}_{half}"] = (wd, layer, cols)
        shards[f"gate{layer}"] = (gate, layer, None)
        shards[f"proj{layer}"] = (proj, layer, None)
    gains = jnp.concatenate([pre_mix_g, post_mix_g, pre_ffn_g, post_ffn_g, ple_g, ple_post_g, kv_g[None, :]],
                            axis=0).reshape(-1, 1, D_MODEL)
    weights = {"pool_w": (pool_w, m_pool_w, v_pool_w), "w_kv": (w_kv, m_w_kv, v_w_kv), "w_q": (w_q, m_w_q, v_w_q),
               "w_o": (w_o, m_w_o, v_w_o), "w_down": (w_down, m_w_down, v_w_down),
               "w_gu": tuple(jnp.swapaxes(a, 1, 2) for a in (w_gu, m_w_gu, v_w_gu)),
               "w_ple_gate": (w_ple_gate, m_w_ple_gate, v_w_ple_gate),
               "w_ple_proj": (w_ple_proj, m_w_ple_proj, v_w_ple_proj)}
    per_layer = p.reshape(p.shape[0], *p.shape[2:])
    p_rows = [_LayerRows(per_layer, layer) for layer in range(2)]
    grad_x, upd, slabs = _local_step(x[0], p_rows, loss_target[0], gains, sinks, shards, weights)

    small_params = {
        "pre_mix_g": (pre_mix_g, m_pre_mix_g, v_pre_mix_g), "post_mix_g": (post_mix_g, m_post_mix_g, v_post_mix_g),
        "pre_ffn_g": (pre_ffn_g, m_pre_ffn_g, v_pre_ffn_g), "post_ffn_g": (post_ffn_g, m_post_ffn_g, v_post_ffn_g),
        "ple_g": (ple_g, m_ple_g, v_ple_g), "ple_post_g": (ple_post_g, m_ple_post_g, v_ple_post_g),
        "kv_g": (kv_g[None, :], m_kv_g[None, :], v_kv_g[None, :]),
        "pool_scale": (pool_scale, m_pool_scale, v_pool_scale), "sinks": (sinks, m_sinks, v_sinks)}
    loss, small_upd = _small_adamw(slabs, small_params)
    small_upd["kv_g"] = [a[0] for a in small_upd["kv_g"]]
    upd.update(small_upd)

    names = ["pre_mix_g", "post_mix_g", "pre_ffn_g", "post_ffn_g", "pool_w", "pool_scale", "kv_g", "w_kv", "w_q",
             "sinks", "w_o", "w_gu", "w_down", "ple_g", "w_ple_gate", "w_ple_proj", "ple_post_g"]
    outs = [loss[0, 0], grad_x[None]]
    for kind in range(4):
        outs += [upd[n][kind] for n in names]
    return tuple(outs)
```
